```python
import jax, jax.numpy as jnp
from jax import lax
import numpy as np

D_MODEL = 1024
BATCH = 8
SEQ = 4096
DEPTH = 1

GRID_W = 64
CTX_LEN = 256
D_MIX = D_MODEL
D_A = D_MIX // 2
D_B = D_MIX - D_A
CHUNK = 128
SGU_GROUPS = 4
SGU_CH = D_A // SGU_GROUPS
NA_HEAD_DIM = 64
NA_HEADS = D_B // NA_HEAD_DIM
WIN_R = 8
WIN_C = 16
D_IN = 3 * D_A + 4 * D_B
BRANCH_WIDTHS = (D_A, D_A, D_A, D_B, D_B, D_B, D_B)
SPLIT_POINTS = tuple(int(s) for s in np.cumsum(BRANCH_WIDTHS)[:-1])
KV_START = 3 * D_A + D_B
KV_END = 3 * D_A + 3 * D_B
EPS = 1e-6
NEG_INF = -1e30

kernel_name = "hybrid_sgu_natten_prefix_block"


def rms_norm(x, g):
    xf = x.astype(jnp.float32)
    y = xf * lax.rsqrt(jnp.mean(xf * xf, axis=-1, keepdims=True) + EPS)
    return (y * g.astype(jnp.float32)).astype(x.dtype)


def ada_params(cond, w_ada, b_ada):
    mod = jax.nn.silu(cond) @ w_ada + b_ada
    shift, scale, gate = jnp.split(mod, 3, axis=-1)
    return shift[..., None, :], scale[..., None, :], gate[..., None, :]


def chunk_sgu(u, v, g, sgu_g, w_s, b_s):
    B, L, _ = u.shape
    u = jax.nn.gelu(u, approximate=False)
    v = jax.nn.gelu(v, approximate=False).reshape(B, L // CHUNK, CHUNK, SGU_GROUPS, SGU_CH)
    v = rms_norm(v, sgu_g.reshape(SGU_GROUPS, SGU_CH))
    mixed = jnp.einsum('gpq,bnqgc->bnpgc', w_s, v) + b_s.T[:, :, None]
    return u * mixed.reshape(B, L, D_A) * jax.nn.silu(g)


def neighborhood_attention(q, k, v, k_ctx, v_ctx, rpb):
    B, L, H, Dh = q.shape
    rows = L // GRID_W
    wr = min(WIN_R, rows)
    scale = Dh ** -0.5
    qg = q.reshape(B, rows, GRID_W, H, Dh)
    kg = k.reshape(B, rows, GRID_W, H, Dh)
    vg = v.reshape(B, rows, GRID_W, H, Dh)
    r = jnp.arange(rows)
    r0 = jnp.clip(r - WIN_R // 2, 0, rows - wr)
    key_rows = r0[:, None] + jnp.arange(wr)[None, :]
    kw = jnp.take(kg, key_rows, axis=1)
    vw = jnp.take(vg, key_rows, axis=1)
    dr = key_rows - r[:, None] + (WIN_R - 1)
    cols = jnp.arange(GRID_W)
    c0 = jnp.clip(cols - WIN_C // 2, 0, GRID_W - WIN_C)
    in_win = (cols[None, :] >= c0[:, None]) & (cols[None, :] < c0[:, None] + WIN_C)
    dc = jnp.clip(cols[None, :] - cols[:, None] + (WIN_C - 1), 0, 2 * WIN_C - 2)
    bias = rpb.astype(jnp.float32)[:, dr[:, None, :, None], dc[None, :, None, :]]
    bias = bias.transpose(1, 0, 2, 3, 4)
    s_lat = jnp.einsum('brqhd,brjkhd->brhqjk', qg, kw).astype(jnp.float32) * scale + bias[None]
    s_lat = jnp.where(in_win[:, None, :], s_lat, NEG_INF)
    s_ctx = jnp.einsum('brqhd,bchd->brhqc', qg, k_ctx).astype(jnp.float32) * scale
    m = jnp.maximum(jnp.max(s_lat, axis=(-2, -1)), jnp.max(s_ctx, axis=-1))
    p_lat = jnp.exp(s_lat - m[..., None, None])
    p_ctx = jnp.exp(s_ctx - m[..., None])
    denom = jnp.sum(p_lat, axis=(-2, -1)) + jnp.sum(p_ctx, axis=-1)
    out = (jnp.einsum('brhqjk,brjkhd->brqhd', p_lat.astype(v.dtype), vw)
           + jnp.einsum('brhqc,bchd->brqhd', p_ctx.astype(v.dtype), v_ctx))
    out = out / denom.transpose(0, 1, 3, 2)[..., None].astype(out.dtype)
    return out.reshape(B, L, H, Dh)


def context_attention(q, k, v):
    s = jnp.einsum('bqhd,bkhd->bhqk', q, k).astype(jnp.float32) * (q.shape[-1] ** -0.5)
    p = jax.nn.softmax(s, axis=-1).astype(v.dtype)
    return jnp.einsum('bhqk,bkhd->bqhd', p, v)


def _fwd_setup_inputs(seed: int = 0) -> dict:
    key = jax.random.key(seed)
    ks = jax.random.split(key, 16)
    f32 = jnp.float32
    nrm = lambda k, shape: jax.random.normal(k, shape, dtype=f32)
    return {
        "x": nrm(ks[0], (BATCH, SEQ, D_MODEL)),
        "c": nrm(ks[1], (BATCH, D_MODEL)),
        "ctx": nrm(ks[2], (BATCH, CTX_LEN, D_MODEL)),
        "c_ctx": nrm(ks[3], (D_MODEL,)),
        "w_ada": nrm(ks[4], (DEPTH, D_MODEL, 3 * D_MODEL)) * (0.5 * D_MODEL ** -0.5),
        "b_ada": nrm(ks[5], (DEPTH, 3 * D_MODEL)) * 0.02,
        "norm_g": 1.0 + 0.02 * nrm(ks[6], (DEPTH, D_MODEL)),
        "w_in": nrm(ks[7], (DEPTH, D_MODEL, D_IN)) * D_MODEL ** -0.5,
        "sgu_norm_g": 1.0 + 0.02 * nrm(ks[8], (DEPTH, D_A)),
        "w_spatial": nrm(ks[9], (DEPTH, SGU_GROUPS, CHUNK, CHUNK)) * CHUNK ** -0.5,
        "b_spatial": nrm(ks[10], (DEPTH, SGU_GROUPS, CHUNK)) * 0.02,
        "q_norm_g": 1.0 + 0.02 * nrm(ks[11], (DEPTH, NA_HEAD_DIM)),
        "k_norm_g": 1.0 + 0.02 * nrm(ks[12], (DEPTH, NA_HEAD_DIM)),
        "rpb": nrm(ks[13], (DEPTH, NA_HEADS, 2 * WIN_R - 1, 2 * WIN_C - 1)) * 0.02,
        "w_out": nrm(ks[14], (DEPTH, D_MIX, D_MODEL)) * D_MIX ** -0.5,
    }


def _fwd_reference(x, c, ctx, c_ctx, w_ada, b_ada, norm_g, w_in, sgu_norm_g, w_spatial,
              b_spatial, q_norm_g, k_norm_g, rpb, w_out):
    B, L, _ = x.shape
    Bc, C, _ = ctx.shape
    for layer in range(DEPTH):
        last = layer == DEPTH - 1
        shift, scale, gate = ada_params(c, w_ada[layer], b_ada[layer])
        cshift, cscale, cgate = ada_params(c_ctx, w_ada[layer], b_ada[layer])
        h = rms_norm(x, norm_g[layer]) * (1.0 + scale) + shift
        hc = rms_norm(ctx, norm_g[layer]) * (1.0 + cscale) + cshift

        z = h @ w_in[layer]
        a_u, a_v, a_g, b_q, b_k, b_v, b_g = jnp.split(z, SPLIT_POINTS, axis=-1)
        q = rms_norm(b_q.reshape(B, L, NA_HEADS, NA_HEAD_DIM), q_norm_g[layer])
        k = rms_norm(b_k.reshape(B, L, NA_HEADS, NA_HEAD_DIM), k_norm_g[layer])
        v = b_v.reshape(B, L, NA_HEADS, NA_HEAD_DIM)

        if last:
            ck, cv = jnp.split(hc @ w_in[layer][:, KV_START:KV_END], 2, axis=-1)
        else:
            zc = hc @ w_in[layer]
            cu, cvv, cga, cq, ck, cv, cgb = jnp.split(zc, SPLIT_POINTS, axis=-1)
        ck = rms_norm(ck.reshape(Bc, C, NA_HEADS, NA_HEAD_DIM), k_norm_g[layer])
        cv = cv.reshape(Bc, C, NA_HEADS, NA_HEAD_DIM)

        out_a = chunk_sgu(a_u, a_v, a_g, sgu_norm_g[layer], w_spatial[layer], b_spatial[layer])
        out_b = neighborhood_attention(q, k, v, ck, cv, rpb[layer]).reshape(B, L, D_B) * jax.nn.silu(b_g)
        mix = jnp.concatenate([out_a, out_b], axis=-1) @ w_out[layer]
        new_x = x + gate * mix

        if not last:
            cq = rms_norm(cq.reshape(Bc, C, NA_HEADS, NA_HEAD_DIM), q_norm_g[layer])
            cout_a = chunk_sgu(cu, cvv, cga, sgu_norm_g[layer], w_spatial[layer], b_spatial[layer])
            cout_b = context_attention(cq, ck, cv).reshape(Bc, C, D_B) * jax.nn.silu(cgb)
            cmix = jnp.concatenate([cout_a, cout_b], axis=-1) @ w_out[layer]
            ctx = ctx + cgate * cmix
        x = new_x
    return x


import jax as _jax
import jax.numpy as _jnp

TWIN_FORMAT = 'train_step'
FWD_PARAMS = ['x', 'c', 'ctx', 'c_ctx', 'w_ada', 'b_ada', 'norm_g', 'w_in', 'sgu_norm_g', 'w_spatial', 'b_spatial', 'q_norm_g', 'k_norm_g', 'rpb', 'w_out']
TWIN_WEIGHTS = ['c_ctx', 'w_ada', 'b_ada', 'norm_g', 'w_in', 'sgu_norm_g', 'w_spatial', 'b_spatial', 'q_norm_g', 'k_norm_g', 'rpb', 'w_out']
TWIN_DIFF_INPUT = 'x'
TWIN_INPUTS = ['x', 'c', 'ctx', 'c_ctx', 'w_ada', 'b_ada', 'norm_g', 'w_in', 'sgu_norm_g', 'w_spatial', 'b_spatial', 'q_norm_g', 'k_norm_g', 'rpb', 'w_out', 'loss_target', 'm_c_ctx', 'm_w_ada', 'm_b_ada', 'm_norm_g', 'm_w_in', 'm_sgu_norm_g', 'm_w_spatial', 'm_b_spatial', 'm_q_norm_g', 'm_k_norm_g', 'm_rpb', 'm_w_out', 'v_c_ctx', 'v_w_ada', 'v_b_ada', 'v_norm_g', 'v_w_in', 'v_sgu_norm_g', 'v_w_spatial', 'v_b_spatial', 'v_q_norm_g', 'v_k_norm_g', 'v_rpb', 'v_w_out']
TWIN_OUTPUTS = ['loss', 'grad_x', 'grad_c_ctx', 'grad_w_ada', 'grad_b_ada', 'grad_norm_g', 'grad_w_in', 'grad_sgu_norm_g', 'grad_w_spatial', 'grad_b_spatial', 'grad_q_norm_g', 'grad_k_norm_g', 'grad_rpb', 'grad_w_out', 'delta_c_ctx', 'delta_w_ada', 'delta_b_ada', 'delta_norm_g', 'delta_w_in', 'delta_sgu_norm_g', 'delta_w_spatial', 'delta_b_spatial', 'delta_q_norm_g', 'delta_k_norm_g', 'delta_rpb', 'delta_w_out', 'new_m_c_ctx', 'new_m_w_ada', 'new_m_b_ada', 'new_m_norm_g', 'new_m_w_in', 'new_m_sgu_norm_g', 'new_m_w_spatial', 'new_m_b_spatial', 'new_m_q_norm_g', 'new_m_k_norm_g', 'new_m_rpb', 'new_m_w_out', 'new_v_c_ctx', 'new_v_w_ada', 'new_v_b_ada', 'new_v_norm_g', 'new_v_w_in', 'new_v_sgu_norm_g', 'new_v_w_spatial', 'new_v_b_spatial', 'new_v_q_norm_g', 'new_v_k_norm_g', 'new_v_rpb', 'new_v_w_out']
TWIN_LEAF_KINDS = {'loss': 'loss', 'grad_x': 'grad_x', 'grad_c_ctx': 'grad_w', 'grad_w_ada': 'grad_w', 'grad_b_ada': 'grad_w', 'grad_norm_g': 'grad_w', 'grad_w_in': 'grad_w', 'grad_sgu_norm_g': 'grad_w', 'grad_w_spatial': 'grad_w', 'grad_b_spatial': 'grad_w', 'grad_q_norm_g': 'grad_w', 'grad_k_norm_g': 'grad_w', 'grad_rpb': 'grad_w', 'grad_w_out': 'grad_w', 'delta_c_ctx': 'delta_w', 'delta_w_ada': 'delta_w', 'delta_b_ada': 'delta_w', 'delta_norm_g': 'delta_w', 'delta_w_in': 'delta_w', 'delta_sgu_norm_g': 'delta_w', 'delta_w_spatial': 'delta_w', 'delta_b_spatial': 'delta_w', 'delta_q_norm_g': 'delta_w', 'delta_k_norm_g': 'delta_w', 'delta_rpb': 'delta_w', 'delta_w_out': 'delta_w', 'new_m_c_ctx': 'new_m', 'new_m_w_ada': 'new_m', 'new_m_b_ada': 'new_m', 'new_m_norm_g': 'new_m', 'new_m_w_in': 'new_m', 'new_m_sgu_norm_g': 'new_m', 'new_m_w_spatial': 'new_m', 'new_m_b_spatial': 'new_m', 'new_m_q_norm_g': 'new_m', 'new_m_k_norm_g': 'new_m', 'new_m_rpb': 'new_m', 'new_m_w_out': 'new_m', 'new_v_c_ctx': 'new_v', 'new_v_w_ada': 'new_v', 'new_v_b_ada': 'new_v', 'new_v_norm_g': 'new_v', 'new_v_w_in': 'new_v', 'new_v_sgu_norm_g': 'new_v', 'new_v_w_spatial': 'new_v', 'new_v_b_spatial': 'new_v', 'new_v_q_norm_g': 'new_v', 'new_v_k_norm_g': 'new_v', 'new_v_rpb': 'new_v', 'new_v_w_out': 'new_v'}


def _forward(args):
    return _fwd_reference(*[args[k] for k in FWD_PARAMS])


def _output_shape():
    out = _jax.eval_shape(lambda: _forward(_fwd_setup_inputs(0)))
    return out.shape, out.dtype

N_MICROBATCH = 1
ADAM_LR = 0.001
ADAM_B1 = 0.9
ADAM_B2 = 0.999
ADAM_EPS = 1e-08
ADAM_WD = 0.01
ADAM_STEP = 10
PER_EXAMPLE_BATCH_AXIS = {'x': 0, 'c': 0, 'ctx': 0, 'loss_target': 0}
SHARED_INPUTS = []
_WEIGHT_DTYPES = {'c_ctx': _jnp.float32, 'w_ada': _jnp.float32, 'b_ada': _jnp.float32, 'norm_g': _jnp.float32, 'w_in': _jnp.float32, 'sgu_norm_g': _jnp.float32, 'w_spatial': _jnp.float32, 'b_spatial': _jnp.float32, 'q_norm_g': _jnp.float32, 'k_norm_g': _jnp.float32, 'rpb': _jnp.float32, 'w_out': _jnp.float32}
MOMENT_SCALE = {'c_ctx': 4.440645e-02, 'w_ada': 2.450432e-01, 'b_ada': 5.767617e-01, 'norm_g': 7.171368e-01, 'w_in': 5.155959e-02, 'sgu_norm_g': 6.388473e-01, 'w_spatial': 1.217882e-01, 'b_spatial': 2.463179e-01, 'q_norm_g': 6.868375e-02, 'k_norm_g': 6.857675e-02, 'rpb': 1.453672e-03, 'w_out': 3.032230e-02}


def _to_microbatches(a, axis):
    t = _jnp.moveaxis(a, axis, 0)
    t = t.reshape((N_MICROBATCH, t.shape[0] // N_MICROBATCH) + t.shape[1:])
    return _jnp.moveaxis(t, 1, axis + 1)


def setup_inputs(seed: int = 0) -> dict:
    inp = _fwd_setup_inputs(seed)
    key = _jax.random.fold_in(_jax.random.key(seed), 7919)
    shape, _ = _output_shape()
    out = dict(inp)
    out["loss_target"] = _jax.random.normal(_jax.random.fold_in(key, 0), shape, _jnp.float32)
    for i, name in enumerate(TWIN_WEIGHTS):
        w = inp[name].astype(_jnp.float32)
        if MOMENT_SCALE is None:
            s = _jnp.sqrt(_jnp.mean(_jnp.square(w)) + 1e-30)
        else:
            s = MOMENT_SCALE[name]
        km, kv = _jax.random.split(_jax.random.fold_in(key, i + 1))
        out[name] = w
        out["m_" + name] = s * _jax.random.normal(km, w.shape, _jnp.float32)
        out["v_" + name] = (s * s) * _jax.random.uniform(kv, w.shape, _jnp.float32, 0.5, 1.5)
    if N_MICROBATCH > 1:
        for name, axis in PER_EXAMPLE_BATCH_AXIS.items():
            out[name] = _to_microbatches(out[name], axis)
    return {'x': out['x'], 'c': out['c'], 'ctx': out['ctx'], 'c_ctx': out['c_ctx'], 'w_ada': out['w_ada'], 'b_ada': out['b_ada'], 'norm_g': out['norm_g'], 'w_in': out['w_in'], 'sgu_norm_g': out['sgu_norm_g'], 'w_spatial': out['w_spatial'], 'b_spatial': out['b_spatial'], 'q_norm_g': out['q_norm_g'], 'k_norm_g': out['k_norm_g'], 'rpb': out['rpb'], 'w_out': out['w_out'], 'loss_target': out['loss_target'], 'm_c_ctx': out['m_c_ctx'], 'm_w_ada': out['m_w_ada'], 'm_b_ada': out['m_b_ada'], 'm_norm_g': out['m_norm_g'], 'm_w_in': out['m_w_in'], 'm_sgu_norm_g': out['m_sgu_norm_g'], 'm_w_spatial': out['m_w_spatial'], 'm_b_spatial': out['m_b_spatial'], 'm_q_norm_g': out['m_q_norm_g'], 'm_k_norm_g': out['m_k_norm_g'], 'm_rpb': out['m_rpb'], 'm_w_out': out['m_w_out'], 'v_c_ctx': out['v_c_ctx'], 'v_w_ada': out['v_w_ada'], 'v_b_ada': out['v_b_ada'], 'v_norm_g': out['v_norm_g'], 'v_w_in': out['v_w_in'], 'v_sgu_norm_g': out['v_sgu_norm_g'], 'v_w_spatial': out['v_w_spatial'], 'v_b_spatial': out['v_b_spatial'], 'v_q_norm_g': out['v_q_norm_g'], 'v_k_norm_g': out['v_k_norm_g'], 'v_rpb': out['v_rpb'], 'v_w_out': out['v_w_out']}


def _loss(weights, diff, rest, loss_target):
    with _jax.named_scope("forward"):
        args = {**rest, TWIN_DIFF_INPUT: diff, **{k: w.astype(_WEIGHT_DTYPES[k]) for k, w in weights.items()}}
        y = _forward(args)
    with _jax.named_scope("loss_head"):
        err = _jnp.square(y.astype(_jnp.float32) - loss_target)
        return 0.5 * _jnp.sum(_jnp.mean(err, axis=-1)) if err.ndim else 0.5 * err


def _adamw(w, g, m, v):
    m = ADAM_B1 * m + (1.0 - ADAM_B1) * g
    v = ADAM_B2 * v + (1.0 - ADAM_B2) * _jnp.square(g)
    m_hat = m / (1.0 - ADAM_B1 ** ADAM_STEP)
    v_hat = v / (1.0 - ADAM_B2 ** ADAM_STEP)
    delta = -ADAM_LR * (m_hat / (_jnp.sqrt(v_hat) + ADAM_EPS) + ADAM_WD * w)
    return delta, m, v


def reference(x, c, ctx, c_ctx, w_ada, b_ada, norm_g, w_in, sgu_norm_g, w_spatial, b_spatial, q_norm_g, k_norm_g, rpb, w_out, loss_target, m_c_ctx, m_w_ada, m_b_ada, m_norm_g, m_w_in, m_sgu_norm_g, m_w_spatial, m_b_spatial, m_q_norm_g, m_k_norm_g, m_rpb, m_w_out, v_c_ctx, v_w_ada, v_b_ada, v_norm_g, v_w_in, v_sgu_norm_g, v_w_spatial, v_b_spatial, v_q_norm_g, v_k_norm_g, v_rpb, v_w_out):
    given = dict(x=x, c=c, ctx=ctx, c_ctx=c_ctx, w_ada=w_ada, b_ada=b_ada, norm_g=norm_g, w_in=w_in, sgu_norm_g=sgu_norm_g, w_spatial=w_spatial, b_spatial=b_spatial, q_norm_g=q_norm_g, k_norm_g=k_norm_g, rpb=rpb, w_out=w_out, loss_target=loss_target, m_c_ctx=m_c_ctx, m_w_ada=m_w_ada, m_b_ada=m_b_ada, m_norm_g=m_norm_g, m_w_in=m_w_in, m_sgu_norm_g=m_sgu_norm_g, m_w_spatial=m_w_spatial, m_b_spatial=m_b_spatial, m_q_norm_g=m_q_norm_g, m_k_norm_g=m_k_norm_g, m_rpb=m_rpb, m_w_out=m_w_out, v_c_ctx=v_c_ctx, v_w_ada=v_w_ada, v_b_ada=v_b_ada, v_norm_g=v_norm_g, v_w_in=v_w_in, v_sgu_norm_g=v_sgu_norm_g, v_w_spatial=v_w_spatial, v_b_spatial=v_b_spatial, v_q_norm_g=v_q_norm_g, v_k_norm_g=v_k_norm_g, v_rpb=v_rpb, v_w_out=v_w_out)
    weights = {n: given[n] for n in TWIN_WEIGHTS}
    shared = {n: given[n] for n in SHARED_INPUTS}
    per_example = {n: given[n] for n in ['x', 'c', 'ctx']}
    grad_fn = _jax.value_and_grad(_loss, argnums=(0, 1))

    def one_microbatch(ex, loss_target):
        ex = dict(ex)
        diff = ex.pop(TWIN_DIFF_INPUT)
        return grad_fn(weights, diff, {**shared, **ex}, loss_target)

    if N_MICROBATCH == 1:
        loss, (grad_w, grad_x) = one_microbatch(per_example, given["loss_target"])
    else:
        def body(carry, xs):
            loss_sum, grad_sum = carry
            l_k, (gw_k, gx_k) = one_microbatch(xs[0], xs[1])
            with _jax.named_scope("update"):
                return (loss_sum + l_k, _jax.tree.map(_jnp.add, grad_sum, gw_k)), gx_k

        init = (_jnp.zeros((), _jnp.float32), _jax.tree.map(_jnp.zeros_like, weights))
        (loss, grad_w), grad_x = _jax.lax.scan(body, init, (per_example, given["loss_target"]))
    with _jax.named_scope("update"):
        delta_w, new_m, new_v = {}, {}, {}
        for n in TWIN_WEIGHTS:
            delta_w[n], new_m[n], new_v[n] = _adamw(weights[n], grad_w[n], given["m_" + n], given["v_" + n])
    return (loss, grad_x, *[grad_w[n] for n in TWIN_WEIGHTS], *[delta_w[n] for n in TWIN_WEIGHTS],
            *[new_m[n] for n in TWIN_WEIGHTS], *[new_v[n] for n in TWIN_WEIGHTS])
```

```python
import functools

import numpy as np
import jax
import jax.numpy as jnp
from jax import lax
from jax.experimental import pallas as pl
from jax.experimental.pallas import tpu as pltpu

F32 = jnp.float32
BF16 = jnp.bfloat16
HI = lax.Precision.HIGHEST

N_DEV = 8
D_MODEL = 1024
D_A = 512
D_B = 512
D_IN = 3584
N_BRANCH = 7
HEAD_DIM = 64
N_HEADS = 8
GRID_W = 64
WIN_R = 8
WIN_C = 16
CHUNK = 128
SGU_GROUPS = 4
EPS = 1e-6
NEG_INF = -1e30
Q_ROWS = 4
K_ROWS = 12
TQ = Q_ROWS * GRID_W
TK = K_ROWS * GRID_W
N_DIAG = 22
ATT_SCALE = HEAD_DIM ** -0.5

ADAM_LR = 0.001
ADAM_B1 = 0.9
ADAM_B2 = 0.999
ADAM_EPS = 1e-08
ADAM_WD = 0.01
ADAM_STEP = 10

VMEM_LIMIT = 56 * 1024 * 1024
MESH = pl.DeviceIdType.MESH

R_CCTX, R_BADA, R_NG, R_SG, R_WS, R_BS, R_QG, R_KG, R_RPB, R_END = 0, 8, 32, 40, 48, 560, 568, 576, 584, 616
R_DCMOD, R_DNG_CTX, R_DKG_CTX, R_TOT = 616, 640, 648, 656


def _params(sem=None):
    return pltpu.CompilerParams(dimension_semantics=sem, vmem_limit_bytes=VMEM_LIMIT)


def _sigmoid(x):
    return 1.0 / (1.0 + jnp.exp(-x))


def _gelu_parts(x):
    cdf = 0.5 * (1.0 + lax.erf(x * 0.7071067811865476))
    pdf = jnp.exp(-0.5 * x * x) * 0.3989422804014327
    return x * cdf, cdf + x * pdf


def _nt(a, b):
    return lax.dot_general(a, b, (((1,), (1,)), ((), ())), preferred_element_type=F32)


def _tn(a, b):
    return lax.dot_general(a, b, (((0,), (0,)), ((), ())), preferred_element_type=F32)


def _dot2(v, ones_bf):
    hi = v.astype(BF16)
    lo = (v - hi.astype(F32)).astype(BF16)
    return (jnp.dot(hi, ones_bf, preferred_element_type=F32)
            + jnp.dot(lo, ones_bf, preferred_element_type=F32))


def _adam(w, g, m, v):
    m2 = ADAM_B1 * m + (1.0 - ADAM_B1) * g
    v2 = ADAM_B2 * v + (1.0 - ADAM_B2) * (g * g)
    m_hat = m2 / (1.0 - ADAM_B1 ** ADAM_STEP)
    v_hat = v2 / (1.0 - ADAM_B2 ** ADAM_STEP)
    delta = -ADAM_LR * (m_hat / (jnp.sqrt(v_hat) + ADAM_EPS) + ADAM_WD * w)
    return delta, m2, v2


def _exchange(arrs, mode, name):
    n = len(arrs)
    outs_shape = []
    for a in arrs:
        blk = a.shape if mode == "ag" else a.shape[1:]
        outs_shape.append(jax.ShapeDtypeStruct((N_DEV,) + tuple(blk), a.dtype))

    def body(*refs):
        ins, outs = refs[:n], refs[n:2 * n]
        send_sems, recv_sems, loc_sems = refs[2 * n:]
        x, y, c = lax.axis_index("x"), lax.axis_index("y"), lax.axis_index("c")
        me = 4 * x + 2 * y + c
        peers = []
        for k in range(1, N_DEV):
            px = 1 - x if (k >> 2) & 1 else x
            py = 1 - y if (k >> 1) & 1 else y
            pc = 1 - c if k & 1 else c
            peers.append(((px, py, pc), 4 * px + 2 * py + pc))
        sends, locs = [], []
        for a in range(n):
            for k, (peer, pid) in enumerate(peers):
                src = ins[a] if mode == "ag" else ins[a].at[pid]
                cp = pltpu.make_async_remote_copy(
                    src_ref=src, dst_ref=outs[a].at[me],
                    send_sem=send_sems.at[a * 7 + k], recv_sem=recv_sems.at[a * 7 + k],
                    device_id=peer, device_id_type=MESH)
                cp.start()
                sends.append(cp)
            lsrc = ins[a] if mode == "ag" else ins[a].at[me]
            lc = pltpu.make_async_copy(lsrc, outs[a].at[me], loc_sems.at[a])
            lc.start()
            locs.append(lc)
        for a in range(n):
            for k, (peer, pid) in enumerate(peers):
                src = ins[a] if mode == "ag" else ins[a].at[pid]
                pltpu.make_async_remote_copy(
                    src_ref=src, dst_ref=outs[a].at[pid],
                    send_sem=send_sems.at[a * 7 + k], recv_sem=recv_sems.at[a * 7 + k],
                    device_id=peer, device_id_type=MESH).wait_recv()
        for cp in sends:
            cp.wait_send()
        for lc in locs:
            lc.wait()

    res = pl.pallas_call(
        body, name=name, out_shape=outs_shape,
        in_specs=[pl.BlockSpec(memory_space=pl.ANY)] * n,
        out_specs=[pl.BlockSpec(memory_space=pl.ANY)] * n,
        scratch_shapes=[pltpu.SemaphoreType.DMA((7 * n,)), pltpu.SemaphoreType.DMA((7 * n,)),
                        pltpu.SemaphoreType.DMA((n,))],
    )(*arrs)
    return list(res)


def _ada_fwd(c_stack, c_ctx_row, w_sh, b_sh):
    def body(cs_ref, cc_ref, w_ref, b_ref, s_ref, p_ref):
        c_all = cs_ref[0]
        for d in range(1, N_DEV):
            c_all = c_all + cs_ref[d]
        row = lax.broadcasted_iota(jnp.int32, (8, D_MODEL), 0)
        cc = jnp.where(row == 0, jnp.broadcast_to(cc_ref[...], (8, D_MODEL)), 0.0)
        call = jnp.concatenate([c_all, cc], axis=0)
        s = call * _sigmoid(call)
        s_ref[...] = s
        p_ref[...] = jnp.dot(s, w_ref[...], preferred_element_type=F32, precision=HI) + b_ref[...]

    return pl.pallas_call(
        body, name="ada_fwd",
        out_shape=[jax.ShapeDtypeStruct((16, D_MODEL), F32), jax.ShapeDtypeStruct((16, w_sh.shape[1]), F32)],
        compiler_params=_params(),
    )(c_stack, c_ctx_row, w_sh, b_sh)


def _ada_bwd(s16, dm, dc, w, m, v):
    def body(s_ref, dm_ref, dc_ref, w_ref, m_ref, v_ref, g_ref, d_ref, m2_ref, v2_ref, pc_ref):
        dct = jnp.sum(dc_ref[...], axis=0, keepdims=True)
        row = lax.broadcasted_iota(jnp.int32, dc_ref.shape, 0)
        dcb = jnp.where(row == 0, jnp.broadcast_to(dct, dc_ref.shape), 0.0)
        dm16 = jnp.concatenate([dm_ref[...], dcb], axis=0)
        g = lax.dot_general(s_ref[...], dm16, (((0,), (0,)), ((), ())),
                            preferred_element_type=F32, precision=HI)
        w_ = w_ref[...]
        delta, m2, v2 = _adam(w_, g, m_ref[...], v_ref[...])
        g_ref[...] = g
        d_ref[...] = delta
        m2_ref[...] = m2
        v2_ref[...] = v2
        pc_ref[...] = lax.dot_general(dcb, w_, (((1,), (1,)), ((), ())),
                                      preferred_element_type=F32, precision=HI)

    sh = jax.ShapeDtypeStruct(w.shape, F32)
    return pl.pallas_call(
        body, name="ada_bwd",
        out_shape=[sh, sh, sh, sh, jax.ShapeDtypeStruct((8, D_MODEL), F32)],
        compiler_params=_params(),
    )(s16, dm, dc, w, m, v)


def _head_norm(zk, ones_ref, gain):
    ss = _dot2(zk * zk, ones_ref[...])
    return zk * lax.rsqrt(ss * (1.0 / HEAD_DIM) + EPS) * gain


def _inproj_fwd(x, ng, scale, shift, w_in, ones_blk, gq, gk, tm):
    T = x.shape[0]

    def body(x_ref, g_ref, sc_ref, sh_ref, w_ref, ones_ref, gq_ref, gk_ref, h_ref, z_ref, qs_ref, kn_ref):
        xv = x_ref[...]
        r = lax.rsqrt(jnp.mean(xv * xv, axis=-1, keepdims=True) + EPS)
        h = (xv * r * g_ref[...]) * (1.0 + sc_ref[...]) + sh_ref[...]
        hb = h.astype(BF16)
        h_ref[...] = hb
        for k in range(N_BRANCH):
            zk = jnp.dot(hb, w_ref[:, 512 * k:512 * (k + 1)], preferred_element_type=F32)
            z_ref[:, 512 * k:512 * (k + 1)] = zk.astype(BF16)
            if k == 3:
                qs_ref[...] = (_head_norm(zk, ones_ref, gq_ref[...]) * ATT_SCALE).astype(BF16)
            if k == 4:
                kn_ref[...] = _head_norm(zk, ones_ref, gk_ref[...]).astype(BF16)

    vec = pl.BlockSpec((1, D_MODEL), lambda i: (0, 0))
    v512 = pl.BlockSpec((1, 512), lambda i: (0, 0))
    return pl.pallas_call(
        body, name="inproj_fwd", grid=(T // tm,),
        in_specs=[pl.BlockSpec((tm, D_MODEL), lambda i: (i, 0)), vec, vec, vec,
                  pl.BlockSpec((D_MODEL, D_IN), lambda i: (0, 0)),
                  pl.BlockSpec((512, 512), lambda i: (0, 0)), v512, v512],
        out_specs=[pl.BlockSpec((tm, D_MODEL), lambda i: (i, 0)),
                   pl.BlockSpec((tm, D_IN), lambda i: (i, 0)),
                   pl.BlockSpec((tm, 512), lambda i: (i, 0)),
                   pl.BlockSpec((tm, 512), lambda i: (i, 0))],
        out_shape=[jax.ShapeDtypeStruct((T, D_MODEL), BF16), jax.ShapeDtypeStruct((T, D_IN), BF16),
                   jax.ShapeDtypeStruct((T, 512), BF16), jax.ShapeDtypeStruct((T, 512), BF16)],
        compiler_params=_params(("arbitrary",)),
    )(x, ng, scale, shift, w_in, ones_blk, gq, gk)


def _ctx_fwd(ctx, ng, cscale, cshift, w_in, ones_blk, gk):
    C = ctx.shape[0]

    def body(x_ref, g_ref, sc_ref, sh_ref, w_ref, ones_ref, gk_ref, h_ref, z_ref, kn_ref):
        xv = x_ref[...]
        r = lax.rsqrt(jnp.mean(xv * xv, axis=-1, keepdims=True) + EPS)
        h = (xv * r * g_ref[...]) * (1.0 + sc_ref[...]) + sh_ref[...]
        hb = h.astype(BF16)
        h_ref[...] = hb
        zk = jnp.dot(hb, w_ref[:, 0:512], preferred_element_type=F32)
        zv = jnp.dot(hb, w_ref[:, 512:1024], preferred_element_type=F32)
        z_ref[:, 0:512] = zk.astype(BF16)
        z_ref[:, 512:1024] = zv.astype(BF16)
        kn_ref[...] = _head_norm(zk, ones_ref, gk_ref[...]).astype(BF16)

    vec = pl.BlockSpec((1, D_MODEL), lambda i: (0, 0))
    return pl.pallas_call(
        body, name="ctx_fwd", grid=(1,),
        in_specs=[pl.BlockSpec((C, D_MODEL), lambda i: (0, 0)), vec, vec, vec,
                  pl.BlockSpec((D_MODEL, 1024), lambda i: (0, 2)),
                  pl.BlockSpec((512, 512), lambda i: (0, 0)), pl.BlockSpec((1, 512), lambda i: (0, 0))],
        out_specs=[pl.BlockSpec((C, D_MODEL), lambda i: (0, 0)), pl.BlockSpec((C, 1024), lambda i: (0, 0)),
                   pl.BlockSpec((C, 512), lambda i: (0, 0))],
        out_shape=[jax.ShapeDtypeStruct((C, D_MODEL), BF16), jax.ShapeDtypeStruct((C, 1024), BF16),
                   jax.ShapeDtypeStruct((C, 512), BF16)],
        compiler_params=_params(("arbitrary",)),
    )(ctx, ng, cscale, cshift, w_in, ones_blk, gk)


def _sgu_chunk_fwd(au, av, ag, sg, ws_bf, bsb):
    gu, dgu = _gelu_parts(au)
    gv, dgv = _gelu_parts(av)
    rr = lax.rsqrt(jnp.mean(gv * gv, axis=-1, keepdims=True) + EPS)
    vhat = gv * rr
    vn = vhat * sg
    mixed = jnp.dot(ws_bf, vn.astype(BF16), preferred_element_type=F32) + bsb
    sig = _sigmoid(ag)
    sl = ag * sig
    return gu * mixed * sl, (gu, dgu, dgv, rr, vhat, vn, mixed, sig, sl)


def _sgu_fwd(z, sgn, ws, bsb, tm):
    T = z.shape[0]

    def body(au_ref, av_ref, ag_ref, sg_ref, ws_ref, bsb_ref, o_ref):
        for g in range(SGU_GROUPS):
            ws_bf = ws_ref[g].astype(BF16)
            sg = sg_ref[:, 128 * g:128 * (g + 1)]
            bsb_g = bsb_ref[g]
            for j in range(tm // CHUNK):
                rs, cs = slice(CHUNK * j, CHUNK * (j + 1)), slice(128 * g, 128 * (g + 1))
                out, _ = _sgu_chunk_fwd(au_ref[rs, cs].astype(F32), av_ref[rs, cs].astype(F32),
                                        ag_ref[rs, cs].astype(F32), sg, ws_bf, bsb_g)
                o_ref[rs, cs] = out.astype(BF16)

    return pl.pallas_call(
        body, name="sgu_fwd", grid=(T // tm,),
        in_specs=[pl.BlockSpec((tm, 512), lambda i: (i, 0)), pl.BlockSpec((tm, 512), lambda i: (i, 1)),
                  pl.BlockSpec((tm, 512), lambda i: (i, 2)), pl.BlockSpec((1, 512), lambda i: (0, 0)),
                  pl.BlockSpec((SGU_GROUPS, CHUNK, CHUNK), lambda i: (0, 0, 0)),
                  pl.BlockSpec((SGU_GROUPS, CHUNK, 128), lambda i: (0, 0, 0))],
        out_specs=pl.BlockSpec((tm, 512), lambda i: (i, 0)),
        out_shape=jax.ShapeDtypeStruct((T, 512), BF16),
        compiler_params=_params(("arbitrary",)),
    )(z, z, z, sgn, ws, bsb)


def _attn_type(rb, nrb):
    return jnp.where(rb == 0, 0, jnp.where(rb == nrb - 1, 2, 1))


def _attn_specs(T, C, nrb):
    return [
        pl.BlockSpec((TQ, 128), lambda hp, rb: (rb, hp)),
        pl.BlockSpec((T, 128), lambda hp, rb: (0, hp)),
        pl.BlockSpec((T, 128), lambda hp, rb: (0, 20 + hp)),
        pl.BlockSpec((C, 128), lambda hp, rb: (0, hp)),
        pl.BlockSpec((C, 128), lambda hp, rb: (0, 4 + hp)),
        pl.BlockSpec((1, 2, TQ, TK), lambda hp, rb: (_attn_type(rb, nrb), hp, 0, 0)),
        pl.BlockSpec((TQ, 128), lambda hp, rb: (rb, 24 + hp)),
    ]


def _attn_fwd(qs, kn, z, ckn, zc, bias):
    T, C = qs.shape[0], ckn.shape[0]
    rows = T // GRID_W
    nrb = rows // Q_ROWS

    def body(q_ref, k_ref, v_ref, ck_ref, cv_ref, bias_ref, bg_ref, ob_ref, outb_ref, lse_ref):
        rb = pl.program_id(1)
        ks = pl.multiple_of(jnp.clip(Q_ROWS * rb - 4, 0, rows - K_ROWS) * GRID_W, GRID_W)
        q2 = q_ref[...]
        k2 = k_ref[pl.ds(ks, TK), :]
        v2 = v_ref[pl.ds(ks, TK), :]
        ck2, cv2 = ck_ref[...], cv_ref[...]
        lane = lax.broadcasted_iota(jnp.int32, (1, 128), 1)
        o_acc = jnp.zeros((TQ, 128), F32)
        lse_acc = jnp.zeros((TQ, 128), F32)
        for hh in range(2):
            msk = (lane >= HEAD_DIM) == bool(hh)
            qm = jnp.where(msk, q2, jnp.zeros_like(q2))
            s = _nt(qm, k2) + bias_ref[0, hh]
            sc = _nt(qm, ck2)
            m = jnp.maximum(jnp.max(s, axis=-1, keepdims=True), jnp.max(sc, axis=-1, keepdims=True))
            p = jnp.exp(s - m)
            pc = jnp.exp(sc - m)
            den = jnp.sum(p, axis=-1, keepdims=True) + jnp.sum(pc, axis=-1, keepdims=True)
            o = (jnp.dot(p.astype(BF16), v2, preferred_element_type=F32)
                 + jnp.dot(pc.astype(BF16), cv2, preferred_element_type=F32)) / den
            o_acc = jnp.where(msk, o, o_acc)
            lse_acc = jnp.where(msk, m + jnp.log(den), lse_acc)
        ob_ref[...] = o_acc.astype(BF16)
        lse_ref[...] = lse_acc
        bg = bg_ref[...].astype(F32)
        outb_ref[...] = (o_acc * (bg * _sigmoid(bg))).astype(BF16)

    tile = pl.BlockSpec((TQ, 128), lambda hp, rb: (rb, hp))
    return pl.pallas_call(
        body, name="attn_fwd", grid=(4, nrb),
        in_specs=_attn_specs(T, C, nrb),
        out_specs=[tile, tile, tile],
        out_shape=[jax.ShapeDtypeStruct((T, 512), BF16), jax.ShapeDtypeStruct((T, 512), BF16),
                   jax.ShapeDtypeStruct((T, 512), F32)],
        compiler_params=_params(("arbitrary", "arbitrary")),
    )(qs, kn, z, ckn, zc, bias, z)


def _outproj_loss_bwd(x, tgt, out_a, out_b, gate, w_out, tm):
    T = x.shape[0]

    def body(x_ref, t_ref, oa_ref, ob_ref, gate_ref, w_ref, dy_ref, dmc_ref, dw_ref, dgate_ref, loss_ref):
        @pl.when(pl.program_id(0) == 0)
        def _():
            dw_ref[...] = jnp.zeros_like(dw_ref)
            dgate_ref[...] = jnp.zeros_like(dgate_ref)
            loss_ref[...] = jnp.zeros_like(loss_ref)

        oa, ob = oa_ref[...], ob_ref[...]
        gate_v = gate_ref[...]
        mix = (jnp.dot(oa, w_ref[0:512, :], preferred_element_type=F32)
               + jnp.dot(ob, w_ref[512:1024, :], preferred_element_type=F32))
        e = x_ref[...] + gate_v * mix - t_ref[...]
        se = jnp.sum(jnp.sum(e * e, axis=0, keepdims=True), axis=1, keepdims=True)
        loss_ref[...] += jnp.broadcast_to(se * (0.5 / D_MODEL), loss_ref.shape)
        dy = e * (1.0 / D_MODEL)
        dy_ref[...] = dy
        dgate_ref[...] += jnp.sum(dy * mix, axis=0, keepdims=True)
        dmix = (dy * gate_v).astype(BF16)
        dmc_ref[...] = _nt(dmix, w_ref[...]).astype(BF16)
        dw_ref[0:512, :] += _tn(oa, dmix)
        dw_ref[512:1024, :] += _tn(ob, dmix)

    row = lambda w: pl.BlockSpec((tm, w), lambda i: (i, 0))
    return pl.pallas_call(
        body, name="outproj_loss_bwd", grid=(T // tm,),
        in_specs=[row(D_MODEL), row(D_MODEL), row(512), row(512),
                  pl.BlockSpec((1, D_MODEL), lambda i: (0, 0)),
                  pl.BlockSpec((D_MODEL, D_MODEL), lambda i: (0, 0))],
        out_specs=[row(D_MODEL), row(D_MODEL), pl.BlockSpec((D_MODEL, D_MODEL), lambda i: (0, 0)),
                   pl.BlockSpec((1, D_MODEL), lambda i: (0, 0)), pl.BlockSpec((1, 128), lambda i: (0, 0))],
        out_shape=[jax.ShapeDtypeStruct((T, D_MODEL), F32), jax.ShapeDtypeStruct((T, D_MODEL), BF16),
                   jax.ShapeDtypeStruct((D_MODEL, D_MODEL), F32), jax.ShapeDtypeStruct((1, D_MODEL), F32),
                   jax.ShapeDtypeStruct((1, 128), F32)],
        compiler_params=_params(("arbitrary",)),
    )(x, tgt, out_a, out_b, gate, w_out)


def _attn_bwd(qs, kn, z, ckn, zc, bias, ob, lse, dmc):
    T, C = qs.shape[0], ckn.shape[0]
    rows = T // GRID_W
    nrb = rows // Q_ROWS

    def body(q_ref, k_ref, v_ref, ck_ref, cv_ref, bias_ref, bg_ref, ob_ref, lse_ref, do_ref,
             dq_ref, dk_ref, dv_ref, dck_ref, dcv_ref, dbg_ref, dacc_ref):
        rb = pl.program_id(1)

        @pl.when(rb == 0)
        def _():
            dk_ref[...] = jnp.zeros_like(dk_ref)
            dv_ref[...] = jnp.zeros_like(dv_ref)
            dck_ref[...] = jnp.zeros_like(dck_ref)
            dcv_ref[...] = jnp.zeros_like(dcv_ref)
            dacc_ref[...] = jnp.zeros_like(dacc_ref)

        kb = jnp.clip(Q_ROWS * rb - 4, 0, rows - K_ROWS)
        ks = pl.multiple_of(kb * GRID_W, GRID_W)
        ebase = kb - Q_ROWS * rb + 11
        q2 = q_ref[...]
        k2 = k_ref[pl.ds(ks, TK), :]
        v2 = v_ref[pl.ds(ks, TK), :]
        ck2, cv2 = ck_ref[...], cv_ref[...]
        bg = bg_ref[...].astype(F32)
        sig = _sigmoid(bg)
        obv = ob_ref[...].astype(F32)
        dout = do_ref[...].astype(F32)
        dbg_ref[...] = (dout * obv * (sig * (1.0 + bg * (1.0 - sig)))).astype(BF16)
        d_o = dout * (bg * sig)
        d_oo = d_o * obv
        lse2 = lse_ref[...]
        lane = lax.broadcasted_iota(jnp.int32, (1, 128), 1)
        dq_acc = jnp.zeros((TQ, 128), F32)
        for hh in range(2):
            msk = (lane >= HEAD_DIM) == bool(hh)
            qm = jnp.where(msk, q2, jnp.zeros_like(q2))
            lse_h = jnp.max(jnp.where(msk, lse2, -jnp.inf), axis=-1, keepdims=True)
            p = jnp.exp(_nt(qm, k2) + bias_ref[0, hh] - lse_h)
            pc = jnp.exp(_nt(qm, ck2) - lse_h)
            dom = jnp.where(msk, d_o, 0.0).astype(BF16)
            delta = jnp.sum(jnp.where(msk, d_oo, 0.0), axis=-1, keepdims=True)
            ds = p * (_nt(dom, v2) - delta)
            dsc = pc * (_nt(dom, cv2) - delta)
            dsb, dscb = ds.astype(BF16), dsc.astype(BF16)
            dq_h = (jnp.dot(dsb, k2, preferred_element_type=F32)
                    + jnp.dot(dscb, ck2, preferred_element_type=F32))
            dq_acc = jnp.where(msk, dq_h, dq_acc)
            dk_ref[pl.ds(ks, TK), :] += _tn(dsb, qm)
            dv_ref[pl.ds(ks, TK), :] += _tn(p.astype(BF16), dom)
            dck_ref[...] += _tn(dscb, qm)
            dcv_ref[...] += _tn(pc.astype(BF16), dom)
            for i in range(Q_ROWS):
                for mm in range(K_ROWS // 2):
                    dacc_ref[hh, ebase + (2 * mm - i)] += ds[GRID_W * i:GRID_W * (i + 1), 128 * mm:128 * (mm + 1)]
        dq_ref[...] = dq_acc

    tile = pl.BlockSpec((TQ, 128), lambda hp, rb: (rb, hp))
    colT = pl.BlockSpec((T, 128), lambda hp, rb: (0, hp))
    colC = pl.BlockSpec((C, 128), lambda hp, rb: (0, hp))
    return pl.pallas_call(
        body, name="attn_bwd", grid=(4, nrb),
        in_specs=_attn_specs(T, C, nrb) + [tile, tile, pl.BlockSpec((TQ, 128), lambda hp, rb: (rb, 4 + hp))],
        out_specs=[tile, colT, colT, colC, colC, tile,
                   pl.BlockSpec((2, N_DIAG, GRID_W, 128), lambda hp, rb: (hp, 0, 0, 0))],
        out_shape=[jax.ShapeDtypeStruct((T, 512), F32), jax.ShapeDtypeStruct((T, 512), F32),
                   jax.ShapeDtypeStruct((T, 512), F32), jax.ShapeDtypeStruct((C, 512), F32),
                   jax.ShapeDtypeStruct((C, 512), F32), jax.ShapeDtypeStruct((T, 512), BF16),
                   jax.ShapeDtypeStruct((N_HEADS, N_DIAG, GRID_W, 128), F32)],
        compiler_params=_params(("arbitrary", "arbitrary")),
    )(qs, kn, z, ckn, zc, bias, z, ob, lse, dmc)


def _rpb_grad(dacc, hsel, fold):
    n_off = 2 * WIN_C - 1

    def body(a_ref, hsel_ref, fold_ref, o_ref):
        a = a_ref[0, 0]
        qc = lax.broadcasted_iota(jnp.int32, (GRID_W, 128), 0)
        kc = lax.broadcasted_iota(jnp.int32, (GRID_W, 128), 1) % GRID_W
        diff = kc - qc + (WIN_C - 1)
        parts = []
        for o in range(n_off):
            mv = jnp.where(diff == o, a, 0.0)
            acc = mv[0:8]
            for r8 in range(1, GRID_W // 8):
                acc = acc + mv[8 * r8:8 * (r8 + 1)]
            parts.append(acc)
        parts.append(jnp.zeros((8, 128), F32))
        stack = jnp.concatenate(parts, axis=0)
        halves = jnp.dot(stack, hsel_ref[...], preferred_element_type=F32, precision=HI)
        o_ref[0, 0] = jnp.dot(fold_ref[...], halves, preferred_element_type=F32, precision=HI)

    return pl.pallas_call(
        body, name="rpb_grad", grid=(N_HEADS, N_DIAG),
        in_specs=[pl.BlockSpec((1, 1, GRID_W, 128), lambda h, e: (h, e, 0, 0)),
                  pl.BlockSpec((128, 128), lambda h, e: (0, 0)), pl.BlockSpec((32, 256), lambda h, e: (0, 0))],
        out_specs=pl.BlockSpec((1, 1, 32, 128), lambda h, e: (h, e, 0, 0)),
        out_shape=jax.ShapeDtypeStruct((N_HEADS, N_DIAG, 32, 128), F32),
        compiler_params=_params(("arbitrary", "arbitrary")),
    )(dacc, hsel, fold)


def _qk_bwd(z, col_q, col_k, dqs, dk, dv, ones_blk, gq, gk, foldm, tm, name):
    T = dk.shape[0]
    with_q = dqs is not None

    def norm_bwd(raw, dn, gain, ones_ref):
        rr = lax.rsqrt(_dot2(raw * raw, ones_ref[...]) * (1.0 / HEAD_DIM) + EPS)
        hat = raw * rr
        dgain = jnp.sum(dn * hat, axis=0, keepdims=True)
        dhat = dn * gain
        mean = _dot2(dhat * hat, ones_ref[...]) * (1.0 / HEAD_DIM)
        return rr * (dhat - hat * mean), dgain

    def body(*refs):
        if with_q:
            (bq_ref, bk_ref, dq_ref, dk_ref, dv_ref, ones_ref, gq_ref, gk_ref, fold_ref,
             dbq_ref, dbk_ref, dbv_ref, dgq_ref, dgk_ref, accq, acck) = refs
        else:
            (bk_ref, dk_ref, dv_ref, ones_ref, gk_ref, fold_ref, dbk_ref, dbv_ref, dgk_ref, acck) = refs
        i = pl.program_id(0)

        @pl.when(i == 0)
        def _():
            acck[...] = jnp.zeros_like(acck)
            if with_q:
                accq[...] = jnp.zeros_like(accq)

        dbk, dgk = norm_bwd(bk_ref[...].astype(F32), dk_ref[...], gk_ref[...], ones_ref)
        dbk_ref[...] = dbk.astype(BF16)
        acck[...] += dgk
        dbv_ref[...] = dv_ref[...].astype(BF16)
        if with_q:
            dbq, dgq = norm_bwd(bq_ref[...].astype(F32), dq_ref[...] * ATT_SCALE, gq_ref[...], ones_ref)
            dbq_ref[...] = dbq.astype(BF16)
            accq[...] += dgq

        @pl.when(i == pl.num_programs(0) - 1)
        def _():
            dgk_ref[...] = jnp.dot(jnp.broadcast_to(acck[...], (8, 512)), fold_ref[...],
                                   preferred_element_type=F32, precision=HI)
            if with_q:
                dgq_ref[...] = jnp.dot(jnp.broadcast_to(accq[...], (8, 512)), fold_ref[...],
                                       preferred_element_type=F32, precision=HI)

    row = pl.BlockSpec((tm, 512), lambda i: (i, 0))
    cst = lambda a, b: pl.BlockSpec((a, b), lambda i: (0, 0))
    out_row = jax.ShapeDtypeStruct((T, 512), BF16)
    gvec = jax.ShapeDtypeStruct((8, 128), F32)
    if with_q:
        in_specs = [pl.BlockSpec((tm, 512), lambda i: (i, col_q)), pl.BlockSpec((tm, 512), lambda i: (i, col_k)),
                    row, row, row, cst(512, 512), cst(1, 512), cst(1, 512), cst(512, 128)]
        args = (z, z, dqs, dk, dv, ones_blk, gq, gk, foldm)
        out_specs = [row, row, row, cst(8, 128), cst(8, 128)]
        out_shape = [out_row, out_row, out_row, gvec, gvec]
        scratch = [pltpu.VMEM((1, 512), F32), pltpu.VMEM((1, 512), F32)]
    else:
        in_specs = [pl.BlockSpec((tm, 512), lambda i: (i, col_k)), row, row, cst(512, 512), cst(1, 512), cst(512, 128)]
        args = (z, dk, dv, ones_blk, gk, foldm)
        out_specs = [row, row, cst(8, 128)]
        out_shape = [out_row, out_row, gvec]
        scratch = [pltpu.VMEM((1, 512), F32)]
    return pl.pallas_call(
        body, name=name, grid=(T // tm,), in_specs=in_specs, out_specs=out_specs, out_shape=out_shape,
        scratch_shapes=scratch, compiler_params=_params(("arbitrary",)),
    )(*args)


def _sgu_bwd(z, dmc, sgn, ws, wst, bsb, ones128, tm):
    T = z.shape[0]

    def body(au_ref, av_ref, ag_ref, d_ref, sg_ref, ws_ref, wst_ref, bsb_ref, ones_ref,
             dau_ref, dav_ref, dag_ref, dws_ref, dbs_ref, dsg_ref):
        @pl.when(pl.program_id(0) == 0)
        def _():
            dws_ref[...] = jnp.zeros_like(dws_ref)
            dbs_ref[...] = jnp.zeros_like(dbs_ref)
            dsg_ref[...] = jnp.zeros_like(dsg_ref)

        for g in range(SGU_GROUPS):
            ws_bf = ws_ref[g].astype(BF16)
            wst_bf = wst_ref[g].astype(BF16)
            sg = sg_ref[:, 128 * g:128 * (g + 1)]
            bsb_g = bsb_ref[g]
            for j in range(tm // CHUNK):
                rs, cs = slice(CHUNK * j, CHUNK * (j + 1)), slice(128 * g, 128 * (g + 1))
                au, av, ag = (au_ref[rs, cs].astype(F32), av_ref[rs, cs].astype(F32), ag_ref[rs, cs].astype(F32))
                d = d_ref[rs, cs].astype(F32)
                _, (gu, dgu, dgv, rr, vhat, vn, mixed, sig, sl) = _sgu_chunk_fwd(au, av, ag, sg, ws_bf, bsb_g)
                dau_ref[rs, cs] = (d * mixed * sl * dgu).astype(BF16)
                dag_ref[rs, cs] = (d * gu * mixed * (sig * (1.0 + ag * (1.0 - sig)))).astype(BF16)
                dmixed = d * gu * sl
                dbs_ref[g] += _dot2(dmixed, ones_ref[...])
                dmb = dmixed.astype(BF16)
                dws_ref[g] += _nt(dmb, vn.astype(BF16))
                dvn = jnp.dot(wst_bf, dmb, preferred_element_type=F32)
                dsg_ref[:, 128 * g:128 * (g + 1)] += jnp.sum(dvn * vhat, axis=0, keepdims=True)
                dvhat = dvn * sg
                mean = jnp.mean(dvhat * vhat, axis=-1, keepdims=True)
                dav_ref[rs, cs] = (rr * (dvhat - vhat * mean) * dgv).astype(BF16)

    row = lambda col: pl.BlockSpec((tm, 512), lambda i: (i, col))
    c3 = lambda n: pl.BlockSpec((SGU_GROUPS, CHUNK, n), lambda i: (0, 0, 0))
    out_row = jax.ShapeDtypeStruct((T, 512), BF16)
    return pl.pallas_call(
        body, name="sgu_bwd", grid=(T // tm,),
        in_specs=[row(0), row(1), row(2), row(0), pl.BlockSpec((1, 512), lambda i: (0, 0)),
                  c3(CHUNK), c3(CHUNK), c3(128), pl.BlockSpec((128, 128), lambda i: (0, 0))],
        out_specs=[row(0), row(0), row(0), c3(CHUNK), c3(128), pl.BlockSpec((1, 512), lambda i: (0, 0))],
        out_shape=[out_row, out_row, out_row, jax.ShapeDtypeStruct((SGU_GROUPS, CHUNK, CHUNK), F32),
                   jax.ShapeDtypeStruct((SGU_GROUPS, CHUNK, 128), F32), jax.ShapeDtypeStruct((1, 512), F32)],
        compiler_params=_params(("arbitrary",)),
    )(z, z, z, dmc, sgn, ws, wst, bsb, ones128)


def _inproj_bwd_dx(dzs, w_cols, w_in, x, dy, ng, scale, shift, tm, name):
    T = x.shape[0]
    n = len(dzs)
    with_dx = dy is not None

    def body(*refs):
        dz_refs = refs[:n]
        if with_dx:
            w_ref, x_ref, dy_ref, g_ref, sc_ref, sh_ref, gx_ref, dsh_ref, dsc_ref, dg_ref = refs[n:]
        else:
            w_ref, x_ref, g_ref, sc_ref, sh_ref, dsh_ref, dsc_ref, dg_ref = refs[n:]

        @pl.when(pl.program_id(0) == 0)
        def _():
            dsh_ref[...] = jnp.zeros_like(dsh_ref)
            dsc_ref[...] = jnp.zeros_like(dsc_ref)
            dg_ref[...] = jnp.zeros_like(dg_ref)

        dh = _nt(dz_refs[0][...], w_ref[:, 0:512])
        for k in range(1, n):
            dh = dh + _nt(dz_refs[k][...], w_ref[:, 512 * k:512 * (k + 1)])
        xv = x_ref[...]
        r = lax.rsqrt(jnp.mean(xv * xv, axis=-1, keepdims=True) + EPS)
        xn = xv * r
        gv, op = g_ref[...], 1.0 + sc_ref[...]
        dsh_ref[...] += jnp.sum(dh, axis=0, keepdims=True)
        dsc_ref[...] += jnp.sum(dh * xn * gv, axis=0, keepdims=True)
        dg_ref[...] += jnp.sum(dh * op * xn, axis=0, keepdims=True)
        if with_dx:
            dxn = dh * (gv * op)
            gx_ref[...] = r * (dxn - xn * jnp.mean(dxn * xn, axis=-1, keepdims=True)) + dy_ref[...]

    vec = pl.BlockSpec((1, D_MODEL), lambda i: (0, 0))
    rowf = pl.BlockSpec((tm, D_MODEL), lambda i: (i, 0))
    in_specs = [pl.BlockSpec((tm, 512), lambda i: (i, 0))] * n
    in_specs += [pl.BlockSpec((D_MODEL, 512 * n), lambda i: (0, w_cols // n)), rowf]
    args = list(dzs) + [w_in, x]
    vshape = jax.ShapeDtypeStruct((1, D_MODEL), F32)
    out_specs, out_shape = [vec, vec, vec], [vshape, vshape, vshape]
    if with_dx:
        in_specs.append(rowf)
        args.append(dy)
        out_specs = [rowf] + out_specs
        out_shape = [jax.ShapeDtypeStruct((T, D_MODEL), F32)] + out_shape
    in_specs += [vec, vec, vec]
    args += [ng, scale, shift]
    return pl.pallas_call(
        body, name=name, grid=(T // tm,), in_specs=in_specs, out_specs=out_specs, out_shape=out_shape,
        compiler_params=_params(("arbitrary",)),
    )(*args)


def _inproj_bwd_dw(h, dzs, hc, dzcs, tk):
    T = h.shape[0]
    nt = T // tk
    outs = []
    for n, dz in enumerate(dzs):
        ctx_col = dzcs.get(n)

        def body(*refs, ctx_col=ctx_col):
            if ctx_col is None:
                h_ref, dz_ref, o_ref, acc = refs
            else:
                h_ref, dz_ref, hc_ref, dzc_ref, o_ref, acc = refs
            t = pl.program_id(0)

            @pl.when(t == 0)
            def _():
                if ctx_col is None:
                    acc[...] = jnp.zeros_like(acc)
                else:
                    acc[...] = _tn(hc_ref[...], dzc_ref[...])

            acc[...] += _tn(h_ref[...], dz_ref[...])

            @pl.when(t == nt - 1)
            def _():
                o_ref[...] = acc[...].astype(BF16)

        in_specs = [pl.BlockSpec((tk, D_MODEL), lambda t: (t, 0)), pl.BlockSpec((tk, 512), lambda t: (t, 0))]
        args = [h, dz]
        if ctx_col is not None:
            in_specs += [pl.BlockSpec(hc.shape, lambda t: (0, 0)), pl.BlockSpec(ctx_col.shape, lambda t: (0, 0))]
            args += [hc, ctx_col]
        outs.append(pl.pallas_call(
            body, name=f"inproj_bwd_dw{n}", grid=(nt,), in_specs=in_specs,
            out_specs=pl.BlockSpec((D_MODEL, 512), lambda t: (0, 0)),
            out_shape=jax.ShapeDtypeStruct((D_MODEL, 512), BF16),
            scratch_shapes=[pltpu.VMEM((D_MODEL, 512), F32)],
            compiler_params=_params(("arbitrary",)),
        )(*args))
    return outs


def _adamw_sharded(w, gparts, m, v, tr, name):
    R, C = w.shape

    def body(w_ref, gp_ref, m_ref, v_ref, g_ref, d_ref, m2_ref, v2_ref):
        g = gp_ref[0].astype(F32)
        for d in range(1, N_DEV):
            g = g + gp_ref[d].astype(F32)
        delta, m2, v2 = _adam(w_ref[...], g, m_ref[...], v_ref[...])
        g_ref[...] = g
        d_ref[...] = delta
        m2_ref[...] = m2
        v2_ref[...] = v2

    row = pl.BlockSpec((tr, C), lambda i: (i, 0))
    sh = jax.ShapeDtypeStruct((R, C), F32)
    return pl.pallas_call(
        body, name=name, grid=(R // tr,),
        in_specs=[row, pl.BlockSpec((N_DEV, tr, C), lambda i: (0, i, 0)), row, row],
        out_specs=[row, row, row, row], out_shape=[sh, sh, sh, sh],
        compiler_params=_params(("arbitrary",)),
    )(w, gparts, m, v)


def _adamw_small(gall, pcall, w, m, v):
    def body(g_ref, pc_ref, w_ref, m_ref, v_ref, go_ref, d_ref, m2_ref, v2_ref):
        def total(lo, hi):
            s = g_ref[0, lo:hi, :]
            for d in range(1, N_DEV):
                s = s + g_ref[d, lo:hi, :]
            return s

        def emit(lo, hi, g):
            delta, m2, v2 = _adam(w_ref[lo:hi, :], g, m_ref[lo:hi, :], v_ref[lo:hi, :])
            go_ref[lo:hi, :] = g
            d_ref[lo:hi, :] = delta
            m2_ref[lo:hi, :] = m2
            v2_ref[lo:hi, :] = v2

        pc = pc_ref[0]
        for d in range(1, N_DEV):
            pc = pc + pc_ref[d]
        cc = w_ref[R_CCTX:R_BADA, :]
        sig = _sigmoid(cc)
        emit(R_CCTX, R_BADA, pc * (sig * (1.0 + cc * (1.0 - sig))))
        emit(R_BADA, R_NG, total(R_BADA, R_NG) + total(R_DCMOD, R_DNG_CTX))
        emit(R_NG, R_SG, total(R_NG, R_SG) + total(R_DNG_CTX, R_DKG_CTX))
        emit(R_SG, R_WS, total(R_SG, R_WS))
        for lo in range(R_WS, R_BS, 128):
            emit(lo, lo + 128, total(lo, lo + 128))
        emit(R_BS, R_KG, total(R_BS, R_KG))
        emit(R_KG, R_RPB, total(R_KG, R_RPB) + total(R_DKG_CTX, R_TOT))
        emit(R_RPB, R_END, total(R_RPB, R_END))

    sh = jax.ShapeDtypeStruct((R_END, 128), F32)
    return pl.pallas_call(body, name="adamw_small", out_shape=[sh, sh, sh, sh], compiler_params=_params())(
        gall, pcall, w, m, v)


def _block_ones(n, blk):
    i = np.arange(n)
    return jnp.asarray((i[:, None] // blk == i[None, :] // blk).astype(np.float32), BF16)


def _attn_bias(rpb, rows):
    n_off = 2 * WIN_C - 1
    cols = np.arange(GRID_W)
    c0 = np.clip(cols - WIN_C // 2, 0, GRID_W - WIN_C)
    in_win = (cols[None, :] >= c0[:, None]) & (cols[None, :] < c0[:, None] + WIN_C)
    dc = np.clip(cols[None, :] - cols[:, None] + (WIN_C - 1), 0, n_off - 1)
    expand = (dc[None] == np.arange(n_off)[:, None, None]) & in_win[None]
    nrb = rows // Q_ROWS
    sel = np.zeros((3, 2 * WIN_R - 1, Q_ROWS, K_ROWS), np.float32)
    valid = np.zeros((3, Q_ROWS, K_ROWS), bool)
    for t, rb in enumerate((0, 1, nrb - 1)):
        kb = int(np.clip(Q_ROWS * rb - 4, 0, rows - K_ROWS))
        for i in range(Q_ROWS):
            r = Q_ROWS * rb + i
            r0 = int(np.clip(r - WIN_R // 2, 0, rows - WIN_R))
            for j in range(K_ROWS):
                kr = kb + j
                if r0 <= kr < r0 + WIN_R:
                    sel[t, kr - r + WIN_R - 1, i, j] = 1.0
                    valid[t, i, j] = True
    ok = valid[:, :, None, :, None] & in_win[None, None, :, None, :]
    neg = np.where(ok, 0.0, NEG_INF).astype(np.float32).reshape(3, 1, TQ, TK)
    toep = jnp.einsum("hrd,dqk->hrqk", rpb, jnp.asarray(expand, F32), precision=HI)
    val = jnp.einsum("trij,hrqk->thiqjk", jnp.asarray(sel), toep, precision=HI)
    return val.reshape(3, N_HEADS, TQ, TK) + jnp.asarray(neg)


def _pad_rows(a, rows):
    flat = a.reshape(-1)
    return jnp.pad(flat, (0, rows * 128 - flat.shape[0])).reshape(rows, 128)


def _pack_small(c_ctx, b_ada, norm_g, sgu_g, w_sp, b_sp, qg, kg, rpb):
    return jnp.concatenate([
        _pad_rows(c_ctx, 8), _pad_rows(b_ada, 24), _pad_rows(norm_g, 8), _pad_rows(sgu_g, 8),
        _pad_rows(w_sp, 512), _pad_rows(b_sp, 8), _pad_rows(qg, 8), _pad_rows(kg, 8), _pad_rows(rpb, 32)], axis=0)


def _unpack_small(p):
    def take(lo, n, shape):
        return p[lo:lo + (n + 127) // 128].reshape(-1)[:n].reshape(shape)
    return dict(
        c_ctx=take(R_CCTX, 1024, (1024,)), b_ada=take(R_BADA, 3072, (1, 3072)), norm_g=take(R_NG, 1024, (1, 1024)),
        sgu_norm_g=take(R_SG, 512, (1, 512)), w_spatial=take(R_WS, 65536, (1, 4, 128, 128)),
        b_spatial=take(R_BS, 512, (1, 4, 128)), q_norm_g=take(R_QG, 64, (1, 64)), k_norm_g=take(R_KG, 64, (1, 64)),
        rpb=take(R_RPB, 3720, (1, 8, 15, 31)))


def kernel(x, c, ctx, c_ctx, w_ada, b_ada, norm_g, w_in, sgu_norm_g, w_spatial, b_spatial, q_norm_g, k_norm_g, rpb, w_out, loss_target, m_c_ctx, m_w_ada, m_b_ada, m_norm_g, m_w_in, m_sgu_norm_g, m_w_spatial, m_b_spatial, m_q_norm_g, m_k_norm_g, m_rpb, m_w_out, v_c_ctx, v_w_ada, v_b_ada, v_norm_g, v_w_in, v_sgu_norm_g, v_w_spatial, v_b_spatial, v_q_norm_g, v_k_norm_g, v_rpb, v_w_out):
    me = 4 * lax.axis_index("x") + 2 * lax.axis_index("y") + lax.axis_index("c")
    x2, ctx2, tgt2 = x[0], ctx[0], loss_target[0]
    T, C = x2.shape[0], ctx2.shape[0]
    rows = T // GRID_W
    wada, win, wout = w_ada[0], w_in[0], w_out[0]
    ada_w = wada.shape[1]
    win_w = win.shape[1]

    row8 = lax.broadcasted_iota(jnp.int32, (8, D_MODEL), 0)
    c_blk = jnp.where(row8 == me, jnp.broadcast_to(c, (8, D_MODEL)), 0.0)
    c_stack, win_g, wout_g = _exchange([c_blk, win.astype(BF16), wout.astype(BF16)], "ag", "gather_weights")
    w_in_b = win_g.transpose(1, 0, 2).reshape(D_MODEL, D_IN)
    w_out_b = wout_g.reshape(D_MODEL, D_MODEL)
    b_sh = lax.dynamic_slice(b_ada, (0, me * ada_w), (1, ada_w))
    s16, part = _ada_fwd(c_stack, c_ctx.reshape(1, D_MODEL), wada, b_sh)
    (part_g,) = _exchange([part], "ag", "gather_mod")
    mod16 = part_g.transpose(1, 0, 2).reshape(16, 3 * D_MODEL)
    mod = lax.dynamic_slice(mod16, (me, 0), (1, 3 * D_MODEL))
    shift, scale, gate = mod[:, :D_MODEL], mod[:, D_MODEL:2 * D_MODEL], mod[:, 2 * D_MODEL:]
    cshift, cscale = mod16[8:9, :D_MODEL], mod16[8:9, D_MODEL:2 * D_MODEL]

    ones512 = _block_ones(512, HEAD_DIM)
    ones128 = jnp.ones((128, 128), BF16)
    foldm = jnp.asarray((np.arange(512)[:, None] % HEAD_DIM == np.arange(128)[None, :]).astype(np.float32))
    hsel = jnp.asarray((np.arange(128)[:, None] // GRID_W == np.arange(128)[None, :]).astype(np.float32))
    foldr = jnp.asarray((np.arange(256)[None, :] // 8 == np.arange(32)[:, None]).astype(np.float32))
    gq512 = jnp.tile(q_norm_g, (1, N_HEADS))
    gk512 = jnp.tile(k_norm_g, (1, N_HEADS))
    ws = w_spatial[0]
    wst = ws.transpose(0, 2, 1)
    bsb = jnp.broadcast_to(b_spatial[0][:, :, None], (SGU_GROUPS, CHUNK, 128))
    bias = _attn_bias(rpb[0], rows)

    h, z, qs, kn = _inproj_fwd(x2, norm_g, scale, shift, w_in_b, ones512, gq512, gk512, 512)
    hc, zc, ckn = _ctx_fwd(ctx2, norm_g, cscale, cshift, w_in_b, ones512, gk512)
    out_a = _sgu_fwd(z, sgu_norm_g, ws, bsb, 512)
    ob, out_b, lse = _attn_fwd(qs, kn, z, ckn, zc, bias)

    dy, dmc, dw_out, dgate, loss_part = _outproj_loss_bwd(x2, tgt2, out_a, out_b, gate, w_out_b, 512)
    dqs, dk, dv, dck, dcv, db_g, dacc = _attn_bwd(qs, kn, z, ckn, zc, bias, ob, lse, dmc)
    drpb_lr = _rpb_grad(dacc, hsel, foldr)
    db_q, db_k, db_v, dgq, dgk = _qk_bwd(z, 3, 4, dqs, dk, dv, ones512, gq512, gk512, foldm, 512, "qk_bwd")
    dzc_k, dzc_v, dgk_c = _qk_bwd(zc, None, 0, None, dck, dcv, ones512, None, gk512, foldm, C, "ctx_k_bwd")
    da_u, da_v, da_g, dws, dbs, dsg = _sgu_bwd(z, dmc, sgu_norm_g, ws, wst, bsb, ones128, 512)
    dzs = [da_u, da_v, da_g, db_q, db_k, db_v, db_g]
    grad_x, dshift, dscale, dng = _inproj_bwd_dx(dzs, 0, w_in_b, x2, dy, norm_g, scale, shift, 512, "inproj_bwd_dx")
    dcshift, dcscale, dng_c = _inproj_bwd_dx([dzc_k, dzc_v], 4, w_in_b, ctx2, None, norm_g, cscale, cshift, C,
                                             "ctx_bwd_dx")
    dw_in_b = jnp.concatenate(_inproj_bwd_dw(h, dzs, hc, {4: dzc_k, 5: dzc_v}, 512), axis=1)

    drpb = drpb_lr[:, 4:19, :31, 0] + drpb_lr[:, 3:18, :31, 1]
    dmod = jnp.concatenate([dshift, dscale, dgate], axis=1)
    dcmod = jnp.concatenate([dcshift, dcscale, jnp.zeros_like(dcshift)], axis=1)
    small = jnp.concatenate([
        jnp.zeros((8, 128), F32), _pad_rows(dmod, 24), _pad_rows(dng, 8), _pad_rows(dsg, 8), _pad_rows(dws, 512),
        _pad_rows(dbs[:, :, 0], 8), _pad_rows(dgq[0, :HEAD_DIM], 8), _pad_rows(dgk[0, :HEAD_DIM], 8),
        _pad_rows(drpb, 32), _pad_rows(dcmod, 24), _pad_rows(dng_c, 8), _pad_rows(dgk_c[0, :HEAD_DIM], 8)], axis=0)
    (small_g,) = _exchange([small], "ag", "gather_small")

    dm_sh = lax.dynamic_slice(small_g[:, R_BADA:R_NG].reshape(N_DEV, 3 * D_MODEL), (0, me * ada_w), (N_DEV, ada_w))
    dc_sh = lax.dynamic_slice(small_g[:, R_DCMOD:R_DNG_CTX].reshape(N_DEV, 3 * D_MODEL), (0, me * ada_w),
                              (N_DEV, ada_w))
    g_ada, d_ada, m_ada, v_ada, pc = _ada_bwd(s16, dm_sh, dc_sh, wada, m_w_ada[0], v_w_ada[0])
    (pc_g,) = _exchange([pc], "ag", "gather_cctx")
    pcall = pc_g[:, 0, :].reshape(N_DEV, 8, 128)

    dw_in_blocks = dw_in_b.reshape(D_MODEL, N_DEV, win_w).transpose(1, 0, 2)
    dw_out_blocks = dw_out.astype(BF16).reshape(N_DEV, D_MODEL // N_DEV, D_MODEL)
    gin_parts, gout_parts = _exchange([dw_in_blocks, dw_out_blocks], "a2a", "scatter_grads")
    g_in, d_in, m_in, v_in = _adamw_sharded(win, gin_parts, m_w_in[0], v_w_in[0], 256, "adamw_w_in")
    g_out, d_out, m_out, v_out = _adamw_sharded(wout, gout_parts, m_w_out[0], v_w_out[0], 128, "adamw_w_out")

    w_small = _pack_small(c_ctx, b_ada, norm_g, sgu_norm_g, w_spatial, b_spatial, q_norm_g, k_norm_g, rpb)
    m_small = _pack_small(m_c_ctx, m_b_ada, m_norm_g, m_sgu_norm_g, m_w_spatial, m_b_spatial, m_q_norm_g,
                          m_k_norm_g, m_rpb)
    v_small = _pack_small(v_c_ctx, v_b_ada, v_norm_g, v_sgu_norm_g, v_w_spatial, v_b_spatial, v_q_norm_g,
                          v_k_norm_g, v_rpb)
    gs, ds, ms, vs = (_unpack_small(a) for a in _adamw_small(small_g, pcall, w_small, m_small, v_small))

    loss = lax.psum(loss_part[0, 0], ("x", "y", "c"))
    names = ["c_ctx", "w_ada", "b_ada", "norm_g", "w_in", "sgu_norm_g", "w_spatial", "b_spatial", "q_norm_g",
             "k_norm_g", "rpb", "w_out"]
    big = {"w_ada": (g_ada, d_ada, m_ada, v_ada), "w_in": (g_in, d_in, m_in, v_in), "w_out": (g_out, d_out, m_out, v_out)}
    outs = [loss, grad_x[None]]
    for idx, packed in enumerate((gs, ds, ms, vs)):
        for nme in names:
            outs.append(big[nme][idx][None] if nme in big else packed[nme])
    return tuple(outs)
```

```python
import functools

import numpy as np
import jax
import jax.numpy as jnp
from jax import lax
from jax.experimental import pallas as pl
from jax.experimental.pallas import tpu as pltpu

F32 = jnp.float32
BF16 = jnp.bfloat16
HI = lax.Precision.HIGHEST

N_DEV = 8
D_MODEL = 1024
D_A = 512
D_B = 512
D_IN = 3584
N_BRANCH = 7
HEAD_DIM = 64
N_HEADS = 8
GRID_W = 64
WIN_R = 8
WIN_C = 16
CHUNK = 128
SGU_GROUPS = 4
EPS = 1e-6
NEG_INF = -1e30
Q_ROWS = 4
K_ROWS = 12
TQ = Q_ROWS * GRID_W
TK = K_ROWS * GRID_W
N_DIAG = 22
ATT_SCALE = HEAD_DIM ** -0.5

ADAM_LR = 0.001
ADAM_B1 = 0.9
ADAM_B2 = 0.999
ADAM_EPS = 1e-08
ADAM_WD = 0.01
ADAM_STEP = 10

VMEM_LIMIT = 56 * 1024 * 1024
MESH = pl.DeviceIdType.MESH

V_DSHIFT, V_DSCALE, V_DGATE, V_DCSHIFT, V_DCSCALE, V_ZERO, V_DNG, V_DNG_CTX, V_MISC = range(9)
M_DBS, M_RPB, M_ROWS = 512, 544, 800


def _params(sem=None):
    return pltpu.CompilerParams(dimension_semantics=sem, vmem_limit_bytes=VMEM_LIMIT)


def _sigmoid(x):
    return 1.0 / (1.0 + jnp.exp(-x))


def _gelu_parts(x):
    cdf = 0.5 * (1.0 + lax.erf(x * 0.7071067811865476))
    pdf = jnp.exp(-0.5 * x * x) * 0.3989422804014327
    return x * cdf, cdf + x * pdf


def _nt(a, b):
    return lax.dot_general(a, b, (((1,), (1,)), ((), ())), preferred_element_type=F32)


def _tn(a, b):
    return lax.dot_general(a, b, (((0,), (0,)), ((), ())), preferred_element_type=F32)


def _dot2(v, ones_bf):
    hi = v.astype(BF16)
    lo = (v - hi.astype(F32)).astype(BF16)
    return (jnp.dot(hi, ones_bf, preferred_element_type=F32)
            + jnp.dot(lo, ones_bf, preferred_element_type=F32))


def _adam(w, g, m, v):
    m2 = ADAM_B1 * m + (1.0 - ADAM_B1) * g
    v2 = ADAM_B2 * v + (1.0 - ADAM_B2) * (g * g)
    m_hat = m2 / (1.0 - ADAM_B1 ** ADAM_STEP)
    v_hat = v2 / (1.0 - ADAM_B2 ** ADAM_STEP)
    delta = -ADAM_LR * (m_hat / (jnp.sqrt(v_hat) + ADAM_EPS) + ADAM_WD * w)
    return delta, m2, v2


def _allgather(arrs, name):
    n = len(arrs)
    outs_shape = [jax.ShapeDtypeStruct((N_DEV,) + tuple(a.shape), a.dtype) for a in arrs]

    def body(*refs):
        ins, outs = refs[:n], refs[n:2 * n]
        send_sems, recv_sems, loc_sems = refs[2 * n:]
        x, y, c = lax.axis_index("x"), lax.axis_index("y"), lax.axis_index("c")
        me, sib = (x, y, c), (x, y, 1 - c)
        chips = [(1 - x, y), (x, 1 - y), (1 - x, 1 - y)]

        def slot(px, py, pc):
            return 4 * px + 2 * py + pc

        def copy(a, k, block, to, src=None):
            return pltpu.make_async_remote_copy(
                src_ref=outs[a].at[slot(*block)] if src is None else src, dst_ref=outs[a].at[slot(*block)],
                send_sem=send_sems.at[7 * a + k], recv_sem=recv_sems.at[7 * a + k],
                device_id=to, device_id_type=MESH)

        started, locs = [], []
        for a in range(n):
            lc = pltpu.make_async_copy(ins[a], outs[a].at[slot(*me)], loc_sems.at[a])
            lc.start()
            locs.append(lc)
            for j, chip in enumerate(chips):
                started.append(copy(a, 1 + j, me, (*chip, c), src=ins[a]))
            started.append(copy(a, 0, me, sib, src=ins[a]))
        for cp in started:
            cp.start()
        for j, chip in enumerate(chips):
            for a in range(n):
                copy(a, 1 + j, (*chip, c), me).wait_recv()
                fwd = copy(a, 4 + j, (*chip, c), sib)
                fwd.start()
                started.append(fwd)
        for a in range(n):
            copy(a, 0, sib, me).wait_recv()
            for j, chip in enumerate(chips):
                copy(a, 4 + j, (*chip, 1 - c), me).wait_recv()
        for cp in started:
            cp.wait_send()
        for lc in locs:
            lc.wait()

    res = pl.pallas_call(
        body, name=name, out_shape=outs_shape,
        in_specs=[pl.BlockSpec(memory_space=pl.ANY)] * n,
        out_specs=[pl.BlockSpec(memory_space=pl.ANY)] * n,
        scratch_shapes=[pltpu.SemaphoreType.DMA((7 * n,)), pltpu.SemaphoreType.DMA((7 * n,)),
                        pltpu.SemaphoreType.DMA((n,))],
    )(*arrs)
    return list(res)


def _reduce_scatter(arrs, name):
    n = len(arrs)
    n_chip = N_DEV // 2
    outs_shape = [jax.ShapeDtypeStruct((n_chip,) + tuple(a.shape[1:]), a.dtype) for a in arrs]

    def body(*refs):
        ins, outs = refs[:n], refs[n:2 * n]
        tmps, sums = refs[2 * n:3 * n], refs[3 * n:4 * n]
        s1, r1, s2, r2 = refs[4 * n:]
        x, y, c = lax.axis_index("x"), lax.axis_index("y"), lax.axis_index("c")
        my_chip = 2 * x + y
        sib = (x, y, 1 - c)
        all_chips = [(0, 0), (0, 1), (1, 0), (1, 1)]
        others = [(1 - x, y), (x, 1 - y), (1 - x, 1 - y)]

        swaps = []
        for a in range(n):
            for q, (qx, qy) in enumerate(all_chips):
                cp = pltpu.make_async_remote_copy(
                    src_ref=ins[a].at[4 * qx + 2 * qy + (1 - c)], dst_ref=tmps[a].at[q],
                    send_sem=s1.at[n_chip * a + q], recv_sem=r1.at[n_chip * a + q],
                    device_id=sib, device_id_type=MESH)
                cp.start()
                swaps.append(cp)
        for a in range(n):
            for q, (qx, qy) in enumerate(all_chips):
                swaps[n_chip * a + q].wait_recv()
                both = ins[a][4 * qx + 2 * qy + c].astype(F32) + tmps[a][q].astype(F32)
                sums[a][q] = both.astype(sums[a].dtype)
        sends = []
        for a in range(n):
            for j, (tx, ty) in enumerate(others):
                cp = pltpu.make_async_remote_copy(
                    src_ref=sums[a].at[2 * tx + ty], dst_ref=outs[a].at[my_chip],
                    send_sem=s2.at[3 * a + j], recv_sem=r2.at[3 * a + j],
                    device_id=(tx, ty, c), device_id_type=MESH)
                cp.start()
                sends.append(cp)
            outs[a][my_chip] = sums[a][my_chip]
        for a in range(n):
            for j, (tx, ty) in enumerate(others):
                pltpu.make_async_remote_copy(
                    src_ref=sums[a].at[2 * tx + ty], dst_ref=outs[a].at[2 * tx + ty],
                    send_sem=s2.at[3 * a + j], recv_sem=r2.at[3 * a + j],
                    device_id=(tx, ty, c), device_id_type=MESH).wait_recv()
        for cp in swaps + sends:
            cp.wait_send()

    vm = pl.BlockSpec(memory_space=pltpu.VMEM)
    res = pl.pallas_call(
        body, name=name, out_shape=outs_shape, in_specs=[vm] * n, out_specs=[vm] * n,
        scratch_shapes=([pltpu.VMEM(s.shape, s.dtype) for s in outs_shape] * 2
                        + [pltpu.SemaphoreType.DMA((n_chip * n,)), pltpu.SemaphoreType.DMA((n_chip * n,)),
                           pltpu.SemaphoreType.DMA((3 * n,)), pltpu.SemaphoreType.DMA((3 * n,))]),
        compiler_params=_params(),
    )(*arrs)
    return list(res)


def _ada_fwd(c_stack, c_ctx_row, w_sh, b_sh):
    def body(cs_ref, cc_ref, w_ref, b_ref, s_ref, p_ref):
        c_all = cs_ref[0]
        for d in range(1, N_DEV):
            c_all = c_all + cs_ref[d]
        row = lax.broadcasted_iota(jnp.int32, (8, D_MODEL), 0)
        cc = jnp.where(row == 0, jnp.broadcast_to(cc_ref[...], (8, D_MODEL)), 0.0)
        call = jnp.concatenate([c_all, cc], axis=0)
        s = call * _sigmoid(call)
        s_ref[...] = s
        p_ref[...] = jnp.dot(s, w_ref[...], preferred_element_type=F32, precision=HI) + b_ref[...]

    return pl.pallas_call(
        body, name="ada_fwd",
        out_shape=[jax.ShapeDtypeStruct((16, D_MODEL), F32), jax.ShapeDtypeStruct((16, w_sh.shape[1]), F32)],
        compiler_params=_params(),
    )(c_stack, c_ctx_row, w_sh, b_sh)


def _ada_bwd(s16, dm, dc, w, m, v):
    def body(s_ref, dm_ref, dc_ref, w_ref, m_ref, v_ref, g_ref, d_ref, m2_ref, v2_ref, pc_ref):
        dct = jnp.sum(dc_ref[...], axis=0, keepdims=True)
        row = lax.broadcasted_iota(jnp.int32, dc_ref.shape, 0)
        dcb = jnp.where(row == 0, jnp.broadcast_to(dct, dc_ref.shape), 0.0)
        dm16 = jnp.concatenate([dm_ref[...], dcb], axis=0)
        g = lax.dot_general(s_ref[...], dm16, (((0,), (0,)), ((), ())),
                            preferred_element_type=F32, precision=HI)
        w_ = w_ref[...]
        delta, m2, v2 = _adam(w_, g, m_ref[...], v_ref[...])
        g_ref[...] = g
        d_ref[...] = delta
        m2_ref[...] = m2
        v2_ref[...] = v2
        pc_ref[...] = lax.dot_general(dcb, w_, (((1,), (1,)), ((), ())),
                                      preferred_element_type=F32, precision=HI)

    sh = jax.ShapeDtypeStruct(w.shape, F32)
    return pl.pallas_call(
        body, name="ada_bwd",
        out_shape=[sh, sh, sh, sh, jax.ShapeDtypeStruct((8, D_MODEL), F32)],
        compiler_params=_params(),
    )(s16, dm, dc, w, m, v)


def _head_norm(zk, ones_ref, gain):
    ss = _dot2(zk * zk, ones_ref[...])
    return zk * lax.rsqrt(ss * (1.0 / HEAD_DIM) + EPS) * gain


def _inproj_fwd(x, ng, scale, shift, w_in, ones_blk, gq, gk, tm):
    T = x.shape[0]

    def body(x_ref, g_ref, sc_ref, sh_ref, w_ref, ones_ref, gq_ref, gk_ref, h_ref, z_ref, qs_ref, kn_ref):
        xv = x_ref[...]
        r = lax.rsqrt(jnp.mean(xv * xv, axis=-1, keepdims=True) + EPS)
        h = (xv * r * g_ref[...]) * (1.0 + sc_ref[...]) + sh_ref[...]
        hb = h.astype(BF16)
        h_ref[...] = hb
        for k in range(N_BRANCH):
            zk = jnp.dot(hb, w_ref[:, 512 * k:512 * (k + 1)], preferred_element_type=F32)
            z_ref[:, 512 * k:512 * (k + 1)] = zk.astype(BF16)
            if k == 3:
                qs_ref[...] = (_head_norm(zk, ones_ref, gq_ref[...]) * ATT_SCALE).astype(BF16)
            if k == 4:
                kn_ref[...] = _head_norm(zk, ones_ref, gk_ref[...]).astype(BF16)

    vec = pl.BlockSpec((1, D_MODEL), lambda i: (0, 0))
    v512 = pl.BlockSpec((1, 512), lambda i: (0, 0))
    return pl.pallas_call(
        body, name="inproj_fwd", grid=(T // tm,),
        in_specs=[pl.BlockSpec((tm, D_MODEL), lambda i: (i, 0)), vec, vec, vec,
                  pl.BlockSpec((D_MODEL, D_IN), lambda i: (0, 0)),
                  pl.BlockSpec((512, 512), lambda i: (0, 0)), v512, v512],
        out_specs=[pl.BlockSpec((tm, D_MODEL), lambda i: (i, 0)),
                   pl.BlockSpec((tm, D_IN), lambda i: (i, 0)),
                   pl.BlockSpec((tm, 512), lambda i: (i, 0)),
                   pl.BlockSpec((tm, 512), lambda i: (i, 0))],
        out_shape=[jax.ShapeDtypeStruct((T, D_MODEL), BF16), jax.ShapeDtypeStruct((T, D_IN), BF16),
                   jax.ShapeDtypeStruct((T, 512), BF16), jax.ShapeDtypeStruct((T, 512), BF16)],
        compiler_params=_params(("arbitrary",)),
    )(x, ng, scale, shift, w_in, ones_blk, gq, gk)


def _ctx_fwd(ctx, ng, cscale, cshift, w_in, ones_blk, gk):
    C = ctx.shape[0]

    def body(x_ref, g_ref, sc_ref, sh_ref, w_ref, ones_ref, gk_ref, h_ref, z_ref, kn_ref):
        xv = x_ref[...]
        r = lax.rsqrt(jnp.mean(xv * xv, axis=-1, keepdims=True) + EPS)
        h = (xv * r * g_ref[...]) * (1.0 + sc_ref[...]) + sh_ref[...]
        hb = h.astype(BF16)
        h_ref[...] = hb
        zk = jnp.dot(hb, w_ref[:, 0:512], preferred_element_type=F32)
        zv = jnp.dot(hb, w_ref[:, 512:1024], preferred_element_type=F32)
        z_ref[:, 0:512] = zk.astype(BF16)
        z_ref[:, 512:1024] = zv.astype(BF16)
        kn_ref[...] = _head_norm(zk, ones_ref, gk_ref[...]).astype(BF16)

    vec = pl.BlockSpec((1, D_MODEL), lambda i: (0, 0))
    return pl.pallas_call(
        body, name="ctx_fwd", grid=(1,),
        in_specs=[pl.BlockSpec((C, D_MODEL), lambda i: (0, 0)), vec, vec, vec,
                  pl.BlockSpec((D_MODEL, 1024), lambda i: (0, 2)),
                  pl.BlockSpec((512, 512), lambda i: (0, 0)), pl.BlockSpec((1, 512), lambda i: (0, 0))],
        out_specs=[pl.BlockSpec((C, D_MODEL), lambda i: (0, 0)), pl.BlockSpec((C, 1024), lambda i: (0, 0)),
                   pl.BlockSpec((C, 512), lambda i: (0, 0))],
        out_shape=[jax.ShapeDtypeStruct((C, D_MODEL), BF16), jax.ShapeDtypeStruct((C, 1024), BF16),
                   jax.ShapeDtypeStruct((C, 512), BF16)],
        compiler_params=_params(("arbitrary",)),
    )(ctx, ng, cscale, cshift, w_in, ones_blk, gk)


def _sgu_chunk_fwd(au, av, ag, sg, ws_bf, bsb):
    gu, dgu = _gelu_parts(au)
    gv, dgv = _gelu_parts(av)
    rr = lax.rsqrt(jnp.mean(gv * gv, axis=-1, keepdims=True) + EPS)
    vhat = gv * rr
    vn = vhat * sg
    mixed = jnp.dot(ws_bf, vn.astype(BF16), preferred_element_type=F32) + bsb
    sig = _sigmoid(ag)
    sl = ag * sig
    return gu * mixed * sl, (gu, dgu, dgv, rr, vhat, vn, mixed, sig, sl)


def _sgu_fwd(z, sgn, ws, bsb, tm):
    T = z.shape[0]

    def body(au_ref, av_ref, ag_ref, sg_ref, ws_ref, bsb_ref, o_ref):
        for g in range(SGU_GROUPS):
            ws_bf = ws_ref[g].astype(BF16)
            sg = sg_ref[:, 128 * g:128 * (g + 1)]
            bsb_g = bsb_ref[g]
            for j in range(tm // CHUNK):
                rs, cs = slice(CHUNK * j, CHUNK * (j + 1)), slice(128 * g, 128 * (g + 1))
                out, _ = _sgu_chunk_fwd(au_ref[rs, cs].astype(F32), av_ref[rs, cs].astype(F32),
                                        ag_ref[rs, cs].astype(F32), sg, ws_bf, bsb_g)
                o_ref[rs, cs] = out.astype(BF16)

    return pl.pallas_call(
        body, name="sgu_fwd", grid=(T // tm,),
        in_specs=[pl.BlockSpec((tm, 512), lambda i: (i, 0)), pl.BlockSpec((tm, 512), lambda i: (i, 1)),
                  pl.BlockSpec((tm, 512), lambda i: (i, 2)), pl.BlockSpec((1, 512), lambda i: (0, 0)),
                  pl.BlockSpec((SGU_GROUPS, CHUNK, CHUNK), lambda i: (0, 0, 0)),
                  pl.BlockSpec((SGU_GROUPS, CHUNK, 128), lambda i: (0, 0, 0))],
        out_specs=pl.BlockSpec((tm, 512), lambda i: (i, 0)),
        out_shape=jax.ShapeDtypeStruct((T, 512), BF16),
        compiler_params=_params(("arbitrary",)),
    )(z, z, z, sgn, ws, bsb)


def _attn_type(rb, nrb):
    return jnp.where(rb == 0, 0, jnp.where(rb == nrb - 1, 2, 1))


def _attn_specs(T, C, nrb):
    return [
        pl.BlockSpec((TQ, 128), lambda hp, rb: (rb, hp)),
        pl.BlockSpec((T, 128), lambda hp, rb: (0, hp)),
        pl.BlockSpec((T, 128), lambda hp, rb: (0, 20 + hp)),
        pl.BlockSpec((C, 128), lambda hp, rb: (0, hp)),
        pl.BlockSpec((C, 128), lambda hp, rb: (0, 4 + hp)),
        pl.BlockSpec((1, 2, TQ, TK), lambda hp, rb: (_attn_type(rb, nrb), hp, 0, 0)),
        pl.BlockSpec((TQ, 128), lambda hp, rb: (rb, 24 + hp)),
    ]


def _attn_fwd(qs, kn, z, ckn, zc, bias):
    T, C = qs.shape[0], ckn.shape[0]
    rows = T // GRID_W
    nrb = rows // Q_ROWS

    def body(q_ref, k_ref, v_ref, ck_ref, cv_ref, bias_ref, bg_ref, ob_ref, outb_ref, lse_ref):
        rb = pl.program_id(1)
        ks = pl.multiple_of(jnp.clip(Q_ROWS * rb - 4, 0, rows - K_ROWS) * GRID_W, GRID_W)
        q2 = q_ref[...]
        k2 = k_ref[pl.ds(ks, TK), :]
        v2 = v_ref[pl.ds(ks, TK), :]
        ck2, cv2 = ck_ref[...], cv_ref[...]
        lane = lax.broadcasted_iota(jnp.int32, (1, 128), 1)
        o_acc = jnp.zeros((TQ, 128), F32)
        lse_acc = jnp.zeros((TQ, 128), F32)
        for hh in range(2):
            msk = (lane >= HEAD_DIM) == bool(hh)
            qm = jnp.where(msk, q2, jnp.zeros_like(q2))
            s = _nt(qm, k2) + bias_ref[0, hh]
            sc = _nt(qm, ck2)
            m = jnp.maximum(jnp.max(s, axis=-1, keepdims=True), jnp.max(sc, axis=-1, keepdims=True))
            p = jnp.exp(s - m)
            pc = jnp.exp(sc - m)
            den = jnp.sum(p, axis=-1, keepdims=True) + jnp.sum(pc, axis=-1, keepdims=True)
            o = (jnp.dot(p.astype(BF16), v2, preferred_element_type=F32)
                 + jnp.dot(pc.astype(BF16), cv2, preferred_element_type=F32)) / den
            o_acc = jnp.where(msk, o, o_acc)
            lse_acc = jnp.where(msk, m + jnp.log(den), lse_acc)
        ob_ref[...] = o_acc.astype(BF16)
        lse_ref[...] = lse_acc
        bg = bg_ref[...].astype(F32)
        outb_ref[...] = (o_acc * (bg * _sigmoid(bg))).astype(BF16)

    tile = pl.BlockSpec((TQ, 128), lambda hp, rb: (rb, hp))
    return pl.pallas_call(
        body, name="attn_fwd", grid=(4, nrb),
        in_specs=_attn_specs(T, C, nrb),
        out_specs=[tile, tile, tile],
        out_shape=[jax.ShapeDtypeStruct((T, 512), BF16), jax.ShapeDtypeStruct((T, 512), BF16),
                   jax.ShapeDtypeStruct((T, 512), F32)],
        compiler_params=_params(("arbitrary", "arbitrary")),
    )(qs, kn, z, ckn, zc, bias, z)


def _outproj_loss_bwd(x, tgt, out_a, out_b, gate, w_out, tm):
    T = x.shape[0]

    def body(x_ref, t_ref, oa_ref, ob_ref, gate_ref, w_ref, dy_ref, dmc_ref, dw_ref, dgate_ref, loss_ref):
        @pl.when(pl.program_id(0) == 0)
        def _():
            dw_ref[...] = jnp.zeros_like(dw_ref)
            dgate_ref[...] = jnp.zeros_like(dgate_ref)
            loss_ref[...] = jnp.zeros_like(loss_ref)

        oa, ob = oa_ref[...], ob_ref[...]
        gate_v = gate_ref[...]
        mix = (jnp.dot(oa, w_ref[0:512, :], preferred_element_type=F32)
               + jnp.dot(ob, w_ref[512:1024, :], preferred_element_type=F32))
        e = x_ref[...] + gate_v * mix - t_ref[...]
        se = jnp.sum(jnp.sum(e * e, axis=0, keepdims=True), axis=1, keepdims=True)
        loss_ref[...] += jnp.broadcast_to(se * (0.5 / D_MODEL), loss_ref.shape)
        dy = e * (1.0 / D_MODEL)
        dy_ref[...] = dy
        dgate_ref[...] += jnp.sum(dy * mix, axis=0, keepdims=True)
        dmix = (dy * gate_v).astype(BF16)
        dmc_ref[...] = _nt(dmix, w_ref[...]).astype(BF16)
        dw_ref[0:512, :] += _tn(oa, dmix)
        dw_ref[512:1024, :] += _tn(ob, dmix)

    row = lambda w: pl.BlockSpec((tm, w), lambda i: (i, 0))
    return pl.pallas_call(
        body, name="outproj_loss_bwd", grid=(T // tm,),
        in_specs=[row(D_MODEL), row(D_MODEL), row(512), row(512),
                  pl.BlockSpec((1, D_MODEL), lambda i: (0, 0)),
                  pl.BlockSpec((D_MODEL, D_MODEL), lambda i: (0, 0))],
        out_specs=[row(D_MODEL), row(D_MODEL), pl.BlockSpec((D_MODEL, D_MODEL), lambda i: (0, 0)),
                   pl.BlockSpec((1, D_MODEL), lambda i: (0, 0)), pl.BlockSpec((1, 128), lambda i: (0, 0))],
        out_shape=[jax.ShapeDtypeStruct((T, D_MODEL), F32), jax.ShapeDtypeStruct((T, D_MODEL), BF16),
                   jax.ShapeDtypeStruct((D_MODEL, D_MODEL), F32), jax.ShapeDtypeStruct((1, D_MODEL), F32),
                   jax.ShapeDtypeStruct((1, 128), F32)],
        compiler_params=_params(("arbitrary",)),
    )(x, tgt, out_a, out_b, gate, w_out)


def _attn_bwd(qs, kn, z, ckn, zc, bias, ob, lse, dmc):
    T, C = qs.shape[0], ckn.shape[0]
    rows = T // GRID_W
    nrb = rows // Q_ROWS

    def body(q_ref, k_ref, v_ref, ck_ref, cv_ref, bias_ref, bg_ref, ob_ref, lse_ref, do_ref,
             dq_ref, dk_ref, dv_ref, dck_ref, dcv_ref, dbg_ref, dacc_ref):
        rb = pl.program_id(1)

        @pl.when(rb == 0)
        def _():
            dk_ref[...] = jnp.zeros_like(dk_ref)
            dv_ref[...] = jnp.zeros_like(dv_ref)
            dck_ref[...] = jnp.zeros_like(dck_ref)
            dcv_ref[...] = jnp.zeros_like(dcv_ref)
            dacc_ref[...] = jnp.zeros_like(dacc_ref)

        kb = jnp.clip(Q_ROWS * rb - 4, 0, rows - K_ROWS)
        ks = pl.multiple_of(kb * GRID_W, GRID_W)
        ebase = kb - Q_ROWS * rb + 11
        q2 = q_ref[...]
        k2 = k_ref[pl.ds(ks, TK), :]
        v2 = v_ref[pl.ds(ks, TK), :]
        ck2, cv2 = ck_ref[...], cv_ref[...]
        bg = bg_ref[...].astype(F32)
        sig = _sigmoid(bg)
        obv = ob_ref[...].astype(F32)
        dout = do_ref[...].astype(F32)
        dbg_ref[...] = (dout * obv * (sig * (1.0 + bg * (1.0 - sig)))).astype(BF16)
        d_o = dout * (bg * sig)
        d_oo = d_o * obv
        lse2 = lse_ref[...]
        lane = lax.broadcasted_iota(jnp.int32, (1, 128), 1)
        dq_acc = jnp.zeros((TQ, 128), F32)
        for hh in range(2):
            msk = (lane >= HEAD_DIM) == bool(hh)
            qm = jnp.where(msk, q2, jnp.zeros_like(q2))
            lse_h = jnp.max(jnp.where(msk, lse2, -jnp.inf), axis=-1, keepdims=True)
            p = jnp.exp(_nt(qm, k2) + bias_ref[0, hh] - lse_h)
            pc = jnp.exp(_nt(qm, ck2) - lse_h)
            dom = jnp.where(msk, d_o, 0.0).astype(BF16)
            delta = jnp.sum(jnp.where(msk, d_oo, 0.0), axis=-1, keepdims=True)
            ds = p * (_nt(dom, v2) - delta)
            dsc = pc * (_nt(dom, cv2) - delta)
            dsb, dscb = ds.astype(BF16), dsc.astype(BF16)
            dq_h = (jnp.dot(dsb, k2, preferred_element_type=F32)
                    + jnp.dot(dscb, ck2, preferred_element_type=F32))
            dq_acc = jnp.where(msk, dq_h, dq_acc)
            dk_ref[pl.ds(ks, TK), :] += _tn(dsb, qm)
            dv_ref[pl.ds(ks, TK), :] += _tn(p.astype(BF16), dom)
            dck_ref[...] += _tn(dscb, qm)
            dcv_ref[...] += _tn(pc.astype(BF16), dom)
            for i in range(Q_ROWS):
                for mm in range(K_ROWS // 2):
                    dacc_ref[hh, ebase + (2 * mm - i)] += ds[GRID_W * i:GRID_W * (i + 1), 128 * mm:128 * (mm + 1)]
        dq_ref[...] = dq_acc

    tile = pl.BlockSpec((TQ, 128), lambda hp, rb: (rb, hp))
    colT = pl.BlockSpec((T, 128), lambda hp, rb: (0, hp))
    colC = pl.BlockSpec((C, 128), lambda hp, rb: (0, hp))
    return pl.pallas_call(
        body, name="attn_bwd", grid=(4, nrb),
        in_specs=_attn_specs(T, C, nrb) + [tile, tile, pl.BlockSpec((TQ, 128), lambda hp, rb: (rb, 4 + hp))],
        out_specs=[tile, colT, colT, colC, colC, tile,
                   pl.BlockSpec((2, N_DIAG, GRID_W, 128), lambda hp, rb: (hp, 0, 0, 0))],
        out_shape=[jax.ShapeDtypeStruct((T, 512), F32), jax.ShapeDtypeStruct((T, 512), F32),
                   jax.ShapeDtypeStruct((T, 512), F32), jax.ShapeDtypeStruct((C, 512), F32),
                   jax.ShapeDtypeStruct((C, 512), F32), jax.ShapeDtypeStruct((T, 512), BF16),
                   jax.ShapeDtypeStruct((N_HEADS, N_DIAG, GRID_W, 128), F32)],
        compiler_params=_params(("arbitrary", "arbitrary")),
    )(qs, kn, z, ckn, zc, bias, z, ob, lse, dmc)


def _rpb_grad(dacc, hsel, fold):
    n_off = 2 * WIN_C - 1
    n_dr = 2 * WIN_R - 1

    def body(a_ref, hsel_ref, fold_ref, o_ref):
        qc = lax.broadcasted_iota(jnp.int32, (GRID_W, 128), 0)
        lane = lax.broadcasted_iota(jnp.int32, (GRID_W, 128), 1)
        diff = lane % GRID_W - qc + (WIN_C - 1)
        left = lane < GRID_W

        def by_dr(dr):
            return a_ref[0, dr + 4] + pltpu.roll(a_ref[0, dr + 3], GRID_W, 1)

        out = jnp.zeros((32, 128), F32)
        for j in range((n_dr + 1) // 2):
            hi = pltpu.roll(by_dr(2 * j + 1), GRID_W, 1) if 2 * j + 1 < n_dr else 0.0
            pair = jnp.where(left, by_dr(2 * j), hi)
            parts = []
            for o in range(n_off):
                mv = jnp.where(diff == o, pair, 0.0)
                acc = mv[0:8]
                for r8 in range(1, GRID_W // 8):
                    acc = acc + mv[8 * r8:8 * (r8 + 1)]
                parts.append(acc)
            parts.append(jnp.zeros((8, 128), F32))
            stack = jnp.concatenate(parts, axis=0)
            per_o = jnp.dot(fold_ref[...], stack, preferred_element_type=F32, precision=HI)
            out = out + jnp.dot(per_o, hsel_ref[j], preferred_element_type=F32, precision=HI)
        o_ref[0] = out

    return pl.pallas_call(
        body, name="rpb_grad", grid=(N_HEADS,),
        in_specs=[pl.BlockSpec((1, N_DIAG, GRID_W, 128), lambda h: (h, 0, 0, 0)),
                  pl.BlockSpec((8, 128, 128), lambda h: (0, 0, 0)), pl.BlockSpec((32, 256), lambda h: (0, 0))],
        out_specs=pl.BlockSpec((1, 32, 128), lambda h: (h, 0, 0)),
        out_shape=jax.ShapeDtypeStruct((N_HEADS, 32, 128), F32),
        compiler_params=_params(("arbitrary",)),
    )(dacc, hsel, fold)


def _qk_bwd(z, col_q, col_k, dqs, dk, dv, ones_blk, gq, gk, foldm, tm, name):
    T = dk.shape[0]
    with_q = dqs is not None

    def norm_bwd(raw, dn, gain, ones_ref):
        rr = lax.rsqrt(_dot2(raw * raw, ones_ref[...]) * (1.0 / HEAD_DIM) + EPS)
        hat = raw * rr
        dgain = jnp.sum(dn * hat, axis=0, keepdims=True)
        dhat = dn * gain
        mean = _dot2(dhat * hat, ones_ref[...]) * (1.0 / HEAD_DIM)
        return rr * (dhat - hat * mean), dgain

    def body(*refs):
        if with_q:
            (bq_ref, bk_ref, dq_ref, dk_ref, dv_ref, ones_ref, gq_ref, gk_ref, fold_ref,
             dbq_ref, dbk_ref, dbv_ref, dgq_ref, dgk_ref, accq, acck) = refs
        else:
            (bk_ref, dk_ref, dv_ref, ones_ref, gk_ref, fold_ref, dbk_ref, dbv_ref, dgk_ref, acck) = refs
        i = pl.program_id(0)

        @pl.when(i == 0)
        def _():
            acck[...] = jnp.zeros_like(acck)
            if with_q:
                accq[...] = jnp.zeros_like(accq)

        dbk, dgk = norm_bwd(bk_ref[...].astype(F32), dk_ref[...], gk_ref[...], ones_ref)
        dbk_ref[...] = dbk.astype(BF16)
        acck[...] += dgk
        dbv_ref[...] = dv_ref[...].astype(BF16)
        if with_q:
            dbq, dgq = norm_bwd(bq_ref[...].astype(F32), dq_ref[...] * ATT_SCALE, gq_ref[...], ones_ref)
            dbq_ref[...] = dbq.astype(BF16)
            accq[...] += dgq

        @pl.when(i == pl.num_programs(0) - 1)
        def _():
            dgk_ref[...] = jnp.dot(jnp.broadcast_to(acck[...], (8, 512)), fold_ref[...],
                                   preferred_element_type=F32, precision=HI)
            if with_q:
                dgq_ref[...] = jnp.dot(jnp.broadcast_to(accq[...], (8, 512)), fold_ref[...],
                                       preferred_element_type=F32, precision=HI)

    row = pl.BlockSpec((tm, 512), lambda i: (i, 0))
    cst = lambda a, b: pl.BlockSpec((a, b), lambda i: (0, 0))
    out_row = jax.ShapeDtypeStruct((T, 512), BF16)
    gvec = jax.ShapeDtypeStruct((8, 128), F32)
    if with_q:
        in_specs = [pl.BlockSpec((tm, 512), lambda i: (i, col_q)), pl.BlockSpec((tm, 512), lambda i: (i, col_k)),
                    row, row, row, cst(512, 512), cst(1, 512), cst(1, 512), cst(512, 128)]
        args = (z, z, dqs, dk, dv, ones_blk, gq, gk, foldm)
        out_specs = [row, row, row, cst(8, 128), cst(8, 128)]
        out_shape = [out_row, out_row, out_row, gvec, gvec]
        scratch = [pltpu.VMEM((1, 512), F32), pltpu.VMEM((1, 512), F32)]
    else:
        in_specs = [pl.BlockSpec((tm, 512), lambda i: (i, col_k)), row, row, cst(512, 512), cst(1, 512), cst(512, 128)]
        args = (z, dk, dv, ones_blk, gk, foldm)
        out_specs = [row, row, cst(8, 128)]
        out_shape = [out_row, out_row, gvec]
        scratch = [pltpu.VMEM((1, 512), F32)]
    return pl.pallas_call(
        body, name=name, grid=(T // tm,), in_specs=in_specs, out_specs=out_specs, out_shape=out_shape,
        scratch_shapes=scratch, compiler_params=_params(("arbitrary",)),
    )(*args)


def _sgu_bwd(z, dmc, sgn, ws, wst, bsb, ones8, tm):
    T = z.shape[0]

    def body(au_ref, av_ref, ag_ref, d_ref, sg_ref, ws_ref, wst_ref, bsb_ref, ones_ref,
             dau_ref, dav_ref, dag_ref, dws_ref, dbs_ref, dsg_ref):
        @pl.when(pl.program_id(0) == 0)
        def _():
            dws_ref[...] = jnp.zeros_like(dws_ref)
            dbs_ref[...] = jnp.zeros_like(dbs_ref)
            dsg_ref[...] = jnp.zeros_like(dsg_ref)

        for g in range(SGU_GROUPS):
            ws_bf = ws_ref[g].astype(BF16)
            wst_bf = wst_ref[g].astype(BF16)
            sg = sg_ref[:, 128 * g:128 * (g + 1)]
            bsb_g = bsb_ref[g]
            for j in range(tm // CHUNK):
                rs, cs = slice(CHUNK * j, CHUNK * (j + 1)), slice(128 * g, 128 * (g + 1))
                au, av, ag = (au_ref[rs, cs].astype(F32), av_ref[rs, cs].astype(F32), ag_ref[rs, cs].astype(F32))
                d = d_ref[rs, cs].astype(F32)
                _, (gu, dgu, dgv, rr, vhat, vn, mixed, sig, sl) = _sgu_chunk_fwd(au, av, ag, sg, ws_bf, bsb_g)
                dau_ref[rs, cs] = (d * mixed * sl * dgu).astype(BF16)
                dag_ref[rs, cs] = (d * gu * mixed * (sig * (1.0 + ag * (1.0 - sig)))).astype(BF16)
                dmixed = d * gu * sl
                dmb = dmixed.astype(BF16)
                dm_lo = (dmixed - dmb.astype(F32)).astype(BF16)
                dbs_ref[g] += _nt(ones_ref[...], dmb) + _nt(ones_ref[...], dm_lo)
                dws_ref[g] += _nt(dmb, vn.astype(BF16))
                dvn = jnp.dot(wst_bf, dmb, preferred_element_type=F32)
                dsg_ref[:, 128 * g:128 * (g + 1)] += jnp.sum(dvn * vhat, axis=0, keepdims=True)
                dvhat = dvn * sg
                mean = jnp.mean(dvhat * vhat, axis=-1, keepdims=True)
                dav_ref[rs, cs] = (rr * (dvhat - vhat * mean) * dgv).astype(BF16)

    row = lambda col: pl.BlockSpec((tm, 512), lambda i: (i, col))
    c3 = lambda n: pl.BlockSpec((SGU_GROUPS, CHUNK, n), lambda i: (0, 0, 0))
    out_row = jax.ShapeDtypeStruct((T, 512), BF16)
    return pl.pallas_call(
        body, name="sgu_bwd", grid=(T // tm,),
        in_specs=[row(0), row(1), row(2), row(0), pl.BlockSpec((1, 512), lambda i: (0, 0)),
                  c3(CHUNK), c3(CHUNK), c3(128), pl.BlockSpec((8, 128), lambda i: (0, 0))],
        out_specs=[row(0), row(0), row(0), c3(CHUNK), pl.BlockSpec((SGU_GROUPS, 8, CHUNK), lambda i: (0, 0, 0)),
                   pl.BlockSpec((1, 512), lambda i: (0, 0))],
        out_shape=[out_row, out_row, out_row, jax.ShapeDtypeStruct((SGU_GROUPS, CHUNK, CHUNK), F32),
                   jax.ShapeDtypeStruct((SGU_GROUPS, 8, CHUNK), F32), jax.ShapeDtypeStruct((1, 512), F32)],
        compiler_params=_params(("arbitrary",)),
    )(z, z, z, dmc, sgn, ws, wst, bsb, ones8)


def _inproj_bwd_dx(dzs, w_cols, w_in, x, dy, ng, scale, shift, tm, name):
    T = x.shape[0]
    n = len(dzs)
    with_dx = dy is not None

    def body(*refs):
        dz_refs = refs[:n]
        if with_dx:
            w_ref, x_ref, dy_ref, g_ref, sc_ref, sh_ref, gx_ref, dsh_ref, dsc_ref, dg_ref = refs[n:]
        else:
            w_ref, x_ref, g_ref, sc_ref, sh_ref, dsh_ref, dsc_ref, dg_ref = refs[n:]

        @pl.when(pl.program_id(0) == 0)
        def _():
            dsh_ref[...] = jnp.zeros_like(dsh_ref)
            dsc_ref[...] = jnp.zeros_like(dsc_ref)
            dg_ref[...] = jnp.zeros_like(dg_ref)

        dh = _nt(dz_refs[0][...], w_ref[:, 0:512])
        for k in range(1, n):
            dh = dh + _nt(dz_refs[k][...], w_ref[:, 512 * k:512 * (k + 1)])
        xv = x_ref[...]
        r = lax.rsqrt(jnp.mean(xv * xv, axis=-1, keepdims=True) + EPS)
        xn = xv * r
        gv, op = g_ref[...], 1.0 + sc_ref[...]
        dsh_ref[...] += jnp.sum(dh, axis=0, keepdims=True)
        dsc_ref[...] += jnp.sum(dh * xn * gv, axis=0, keepdims=True)
        dg_ref[...] += jnp.sum(dh * op * xn, axis=0, keepdims=True)
        if with_dx:
            dxn = dh * (gv * op)
            gx_ref[...] = r * (dxn - xn * jnp.mean(dxn * xn, axis=-1, keepdims=True)) + dy_ref[...]

    vec = pl.BlockSpec((1, D_MODEL), lambda i: (0, 0))
    rowf = pl.BlockSpec((tm, D_MODEL), lambda i: (i, 0))
    in_specs = [pl.BlockSpec((tm, 512), lambda i: (i, 0))] * n
    in_specs += [pl.BlockSpec((D_MODEL, 512 * n), lambda i: (0, w_cols // n)), rowf]
    args = list(dzs) + [w_in, x]
    vshape = jax.ShapeDtypeStruct((1, D_MODEL), F32)
    out_specs, out_shape = [vec, vec, vec], [vshape, vshape, vshape]
    if with_dx:
        in_specs.append(rowf)
        args.append(dy)
        out_specs = [rowf] + out_specs
        out_shape = [jax.ShapeDtypeStruct((T, D_MODEL), F32)] + out_shape
    in_specs += [vec, vec, vec]
    args += [ng, scale, shift]
    return pl.pallas_call(
        body, name=name, grid=(T // tm,), in_specs=in_specs, out_specs=out_specs, out_shape=out_shape,
        compiler_params=_params(("arbitrary",)),
    )(*args)


def _inproj_bwd_dw(h, dzs, hc, dzcs, tk):
    T = h.shape[0]
    nt = T // tk
    outs = []
    for n, dz in enumerate(dzs):
        ctx_col = dzcs.get(n)

        def body(*refs, ctx_col=ctx_col):
            if ctx_col is None:
                h_ref, dz_ref, o_ref, acc = refs
            else:
                h_ref, dz_ref, hc_ref, dzc_ref, o_ref, acc = refs
            t = pl.program_id(0)

            @pl.when(t == 0)
            def _():
                if ctx_col is None:
                    acc[...] = jnp.zeros_like(acc)
                else:
                    acc[...] = _tn(hc_ref[...], dzc_ref[...])

            acc[...] += _tn(h_ref[...], dz_ref[...])

            @pl.when(t == nt - 1)
            def _():
                o_ref[...] = acc[...].astype(BF16)

        in_specs = [pl.BlockSpec((tk, D_MODEL), lambda t: (t, 0)), pl.BlockSpec((tk, 512), lambda t: (t, 0))]
        args = [h, dz]
        if ctx_col is not None:
            in_specs += [pl.BlockSpec(hc.shape, lambda t: (0, 0)), pl.BlockSpec(ctx_col.shape, lambda t: (0, 0))]
            args += [hc, ctx_col]
        outs.append(pl.pallas_call(
            body, name=f"inproj_bwd_dw{n}", grid=(nt,), in_specs=in_specs,
            out_specs=pl.BlockSpec((D_MODEL, 512), lambda t: (0, 0)),
            out_shape=jax.ShapeDtypeStruct((D_MODEL, 512), BF16),
            scratch_shapes=[pltpu.VMEM((D_MODEL, 512), F32)],
            compiler_params=_params(("arbitrary",)),
        )(*args))
    return outs


def _adamw_sharded(w, gparts, m, v, tr, name):
    R, C = w.shape
    n_part = gparts.shape[0]

    def body(w_ref, gp_ref, m_ref, v_ref, g_ref, d_ref, m2_ref, v2_ref):
        g = gp_ref[0].astype(F32)
        for d in range(1, n_part):
            g = g + gp_ref[d].astype(F32)
        delta, m2, v2 = _adam(w_ref[...], g, m_ref[...], v_ref[...])
        g_ref[...] = g
        d_ref[...] = delta
        m2_ref[...] = m2
        v2_ref[...] = v2

    row = pl.BlockSpec((tr, C), lambda i: (i, 0))
    sh = jax.ShapeDtypeStruct((R, C), F32)
    return pl.pallas_call(
        body, name=name, grid=(R // tr,),
        in_specs=[row, pl.BlockSpec((n_part, tr, C), lambda i: (0, i, 0)), row, row],
        out_specs=[row, row, row, row], out_shape=[sh, sh, sh, sh],
        compiler_params=_params(("arbitrary",)),
    )(w, gparts, m, v)


def _pack_small_grads(vec_rows, dsg, dgq, dgk, dgk_c, dws, dbs, drpb):
    n_vec = len(vec_rows)

    def body(*refs):
        vecs = refs[:n_vec]
        dsg_ref, dgq_ref, dgk_ref, dgkc_ref, dws_ref, dbs_ref, drpb_ref, v_ref, m_ref = refs[n_vec:]
        row = lax.broadcasted_iota(jnp.int32, (16, D_MODEL), 0)
        misc = jnp.concatenate([dsg_ref[...], dgq_ref[0:1, :], dgk_ref[0:1, :], dgkc_ref[0:1, :],
                                jnp.zeros((1, 128), F32)], axis=1)
        v = jnp.where(row == V_MISC, jnp.broadcast_to(misc, (16, D_MODEL)), 0.0)
        for (r, _), ref in zip(vec_rows, vecs):
            v = jnp.where(row == r, jnp.broadcast_to(ref[...], (16, D_MODEL)), v)
        v_ref[...] = v
        for g in range(SGU_GROUPS):
            m_ref[128 * g:128 * (g + 1), :] = dws_ref[g]
            m_ref[M_DBS + 8 * g:M_DBS + 8 * (g + 1), :] = dbs_ref[g]
        for hd in range(N_HEADS):
            m_ref[M_RPB + 32 * hd:M_RPB + 32 * (hd + 1), :] = drpb_ref[hd]

    return pl.pallas_call(
        body, name="pack_small_grads",
        out_shape=[jax.ShapeDtypeStruct((16, D_MODEL), F32), jax.ShapeDtypeStruct((M_ROWS, 128), F32)],
        compiler_params=_params(),
    )(*[a for _, a in vec_rows], dsg, dgq, dgk, dgk_c, dws, dbs, drpb)


SMALL_NAMES = ("b_ada", "norm_g", "sgu_norm_g", "w_spatial", "b_spatial", "q_norm_g", "k_norm_g", "rpb")


def _adamw_small(vg, mg, ws, ms, vs):
    k = len(SMALL_NAMES)

    def body(*refs):
        vg_ref, mg_ref = refs[0], refs[1]
        w_refs = dict(zip(SMALL_NAMES, refs[2:2 + k]))
        m_refs = dict(zip(SMALL_NAMES, refs[2 + k:2 + 2 * k]))
        v_refs = dict(zip(SMALL_NAMES, refs[2 + 2 * k:2 + 3 * k]))
        o_refs = [dict(zip(SMALL_NAMES, refs[2 + (3 + i) * k:2 + (4 + i) * k])) for i in range(4)]

        sv = vg_ref[0]
        for d in range(1, N_DEV):
            sv = sv + vg_ref[d]

        def total(lo, hi):
            s = mg_ref[0, lo:hi, :]
            for d in range(1, N_DEV):
                s = s + mg_ref[d, lo:hi, :]
            return s

        def emit(name, idx, g):
            res = _adam(w_refs[name][idx], g, m_refs[name][idx], v_refs[name][idx])
            for o, val in zip(o_refs, (g,) + res):
                o[name][idx] = val

        everything = (slice(None), slice(None))
        row = lambda r: sv[r:r + 1, :]
        emit("b_ada", everything, jnp.concatenate(
            [row(V_DSHIFT) + row(V_DCSHIFT), row(V_DSCALE) + row(V_DCSCALE), row(V_DGATE)], axis=1))
        emit("norm_g", everything, row(V_DNG) + row(V_DNG_CTX))
        misc = row(V_MISC)
        emit("sgu_norm_g", everything, misc[:, 0:512])
        emit("q_norm_g", everything, misc[:, 512:512 + HEAD_DIM])
        emit("k_norm_g", everything, misc[:, 640:640 + HEAD_DIM] + misc[:, 768:768 + HEAD_DIM])
        for g in range(SGU_GROUPS):
            emit("w_spatial", (0, g), total(128 * g, 128 * (g + 1)))
            emit("b_spatial", (0, slice(g, g + 1), slice(None)), total(M_DBS + 8 * g, M_DBS + 8 * (g + 1))[0:1, :])
        for hd in range(N_HEADS):
            by_dc = total(M_RPB + 32 * hd, M_RPB + 32 * (hd + 1))
            emit("rpb", (0, hd), by_dc.T[0:2 * WIN_R - 1, 0:2 * WIN_C - 1])

    shapes = [jax.ShapeDtypeStruct(w.shape, F32) for w in ws]
    res = pl.pallas_call(body, name="adamw_small", out_shape=shapes * 4, compiler_params=_params())(
        vg, mg, *ws, *ms, *vs)
    return [res[i * k:(i + 1) * k] for i in range(4)]


def _adamw_cctx(pc_g, w, m, v):
    def body(pc_ref, w_ref, m_ref, v_ref, g_ref, d_ref, m2_ref, v2_ref):
        pc = pc_ref[0, 0:1, :]
        for d in range(1, N_DEV):
            pc = pc + pc_ref[d, 0:1, :]
        cc = w_ref[...]
        sig = _sigmoid(cc)
        g = pc * (sig * (1.0 + cc * (1.0 - sig)))
        delta, m2, v2 = _adam(cc, g, m_ref[...], v_ref[...])
        g_ref[...] = g
        d_ref[...] = delta
        m2_ref[...] = m2
        v2_ref[...] = v2

    sh = jax.ShapeDtypeStruct((1, D_MODEL), F32)
    return pl.pallas_call(body, name="adamw_cctx", out_shape=[sh, sh, sh, sh], compiler_params=_params())(
        pc_g, w, m, v)


def _block_ones(n, blk):
    i = np.arange(n)
    return jnp.asarray((i[:, None] // blk == i[None, :] // blk).astype(np.float32), BF16)


def _rpb_pairs(rpb):
    n_off = 2 * WIN_C - 1
    cols = np.arange(GRID_W)
    c0 = np.clip(cols - WIN_C // 2, 0, GRID_W - WIN_C)
    in_win = (cols[None, :] >= c0[:, None]) & (cols[None, :] < c0[:, None] + WIN_C)
    dc = np.clip(cols[None, :] - cols[:, None] + (WIN_C - 1), 0, n_off - 1)
    expand = (dc[None] == np.arange(n_off)[:, None, None]) & in_win[None]
    toep = jnp.einsum("hrd,dqk->hrqk", rpb, jnp.asarray(expand, F32), precision=HI)
    toep = toep + jnp.asarray(np.where(in_win, 0.0, NEG_INF).astype(np.float32))
    neg = jnp.full((N_HEADS, 1, GRID_W, GRID_W), NEG_INF, F32)
    ext = jnp.concatenate([neg, toep, neg], axis=1)
    return jnp.concatenate([ext[:, :-1], ext[:, 1:]], axis=-1)


def _row_mask(rows):
    nrb = rows // Q_ROWS
    valid = np.zeros((3, Q_ROWS, 1, K_ROWS, 1), bool)
    for t, rb in enumerate((0, 1, nrb - 1)):
        kb = int(np.clip(Q_ROWS * rb - 4, 0, rows - K_ROWS))
        for i in range(Q_ROWS):
            r0 = int(np.clip(Q_ROWS * rb + i - WIN_R // 2, 0, rows - WIN_R))
            for j in range(K_ROWS):
                valid[t, i, 0, j, 0] = r0 <= kb + j < r0 + WIN_R
    full = np.broadcast_to(valid, (3, Q_ROWS, GRID_W, K_ROWS, GRID_W)).reshape(3, TQ, TK)
    return jnp.asarray(np.where(full, 0.0, NEG_INF).astype(np.float32))


def _bias_build(pairs, row_mask):
    def body(p_ref, m_ref, o_ref):
        t = pl.program_id(0)
        base = WIN_R - Q_ROWS * t
        for i in range(Q_ROWS):
            for mm in range(K_ROWS // 2):
                p = jnp.clip(base + 2 * mm - i, 0, 2 * WIN_R - 1)
                rs, cs = slice(GRID_W * i, GRID_W * (i + 1)), slice(128 * mm, 128 * (mm + 1))
                o_ref[0, 0, rs, cs] = p_ref[0, p] + m_ref[0, rs, cs]

    return pl.pallas_call(
        body, name="bias_build", grid=(3, N_HEADS),
        in_specs=[pl.BlockSpec((1, 2 * WIN_R, GRID_W, 128), lambda t, h: (h, 0, 0, 0)),
                  pl.BlockSpec((1, TQ, TK), lambda t, h: (t, 0, 0))],
        out_specs=pl.BlockSpec((1, 1, TQ, TK), lambda t, h: (t, h, 0, 0)),
        out_shape=jax.ShapeDtypeStruct((3, N_HEADS, TQ, TK), F32),
        compiler_params=_params(("arbitrary", "arbitrary")),
    )(pairs, row_mask)


def kernel(x, c, ctx, c_ctx, w_ada, b_ada, norm_g, w_in, sgu_norm_g, w_spatial, b_spatial, q_norm_g, k_norm_g, rpb, w_out, loss_target, m_c_ctx, m_w_ada, m_b_ada, m_norm_g, m_w_in, m_sgu_norm_g, m_w_spatial, m_b_spatial, m_q_norm_g, m_k_norm_g, m_rpb, m_w_out, v_c_ctx, v_w_ada, v_b_ada, v_norm_g, v_w_in, v_sgu_norm_g, v_w_spatial, v_b_spatial, v_q_norm_g, v_k_norm_g, v_rpb, v_w_out):
    me = 4 * lax.axis_index("x") + 2 * lax.axis_index("y") + lax.axis_index("c")
    x2, ctx2, tgt2 = x[0], ctx[0], loss_target[0]
    T, C = x2.shape[0], ctx2.shape[0]
    rows = T // GRID_W
    wada, win, wout = w_ada[0], w_in[0], w_out[0]
    ada_w = wada.shape[1]
    win_w = win.shape[1]

    row8 = lax.broadcasted_iota(jnp.int32, (8, D_MODEL), 0)
    c_blk = jnp.where(row8 == me, jnp.broadcast_to(c, (8, D_MODEL)), 0.0)
    c_stack, win_g, wout_g = _allgather([c_blk, win.astype(BF16), wout.astype(BF16)], "gather_weights")
    w_in_b = win_g.transpose(1, 0, 2).reshape(D_MODEL, D_IN)
    w_out_b = wout_g.reshape(D_MODEL, D_MODEL)
    b_sh = lax.dynamic_slice(b_ada, (0, me * ada_w), (1, ada_w))
    c_ctx_row = c_ctx.reshape(1, D_MODEL)
    s16, part = _ada_fwd(c_stack, c_ctx_row, wada, b_sh)
    (part_g,) = _allgather([part], "gather_mod")
    mod16 = part_g.transpose(1, 0, 2).reshape(16, 3 * D_MODEL)
    mod = lax.dynamic_slice(mod16, (me, 0), (1, 3 * D_MODEL))
    shift, scale, gate = mod[:, :D_MODEL], mod[:, D_MODEL:2 * D_MODEL], mod[:, 2 * D_MODEL:]
    cshift, cscale = mod16[8:9, :D_MODEL], mod16[8:9, D_MODEL:2 * D_MODEL]

    ones512 = _block_ones(512, HEAD_DIM)
    ones8 = jnp.ones((8, 128), BF16)
    foldm = jnp.asarray((np.arange(512)[:, None] % HEAD_DIM == np.arange(128)[None, :]).astype(np.float32))
    lane_half = np.arange(128)[None, :, None] // GRID_W
    hsel = jnp.asarray((2 * np.arange(8)[:, None, None] + lane_half == np.arange(128)[None, None, :]).astype(np.float32))
    foldr = jnp.asarray((np.arange(256)[None, :] // 8 == np.arange(32)[:, None]).astype(np.float32))
    gq512 = jnp.tile(q_norm_g, (1, N_HEADS))
    gk512 = jnp.tile(k_norm_g, (1, N_HEADS))
    ws = w_spatial[0]
    wst = ws.transpose(0, 2, 1)
    bsb = jnp.broadcast_to(b_spatial[0][:, :, None], (SGU_GROUPS, CHUNK, 128))
    bias = _bias_build(_rpb_pairs(rpb[0]), _row_mask(rows))

    h, z, qs, kn = _inproj_fwd(x2, norm_g, scale, shift, w_in_b, ones512, gq512, gk512, 512)
    hc, zc, ckn = _ctx_fwd(ctx2, norm_g, cscale, cshift, w_in_b, ones512, gk512)
    out_a = _sgu_fwd(z, sgu_norm_g, ws, bsb, 512)
    ob, out_b, lse = _attn_fwd(qs, kn, z, ckn, zc, bias)

    dy, dmc, dw_out, dgate, loss_part = _outproj_loss_bwd(x2, tgt2, out_a, out_b, gate, w_out_b, 512)
    dqs, dk, dv, dck, dcv, db_g, dacc = _attn_bwd(qs, kn, z, ckn, zc, bias, ob, lse, dmc)
    drpb = _rpb_grad(dacc, hsel, foldr)
    db_q, db_k, db_v, dgq, dgk = _qk_bwd(z, 3, 4, dqs, dk, dv, ones512, gq512, gk512, foldm, 512, "qk_bwd")
    dzc_k, dzc_v, dgk_c = _qk_bwd(zc, None, 0, None, dck, dcv, ones512, None, gk512, foldm, C, "ctx_k_bwd")
    da_u, da_v, da_g, dws, dbs, dsg = _sgu_bwd(z, dmc, sgu_norm_g, ws, wst, bsb, ones8, 512)
    dzs = [da_u, da_v, da_g, db_q, db_k, db_v, db_g]
    grad_x, dshift, dscale, dng = _inproj_bwd_dx(dzs, 0, w_in_b, x2, dy, norm_g, scale, shift, 512, "inproj_bwd_dx")
    dcshift, dcscale, dng_c = _inproj_bwd_dx([dzc_k, dzc_v], 4, w_in_b, ctx2, None, norm_g, cscale, cshift, C,
                                             "ctx_bwd_dx")
    dw_in_b = jnp.concatenate(_inproj_bwd_dw(h, dzs, hc, {4: dzc_k, 5: dzc_v}, 512), axis=1)

    dw_in_blocks = dw_in_b.reshape(D_MODEL, N_DEV, win_w).transpose(1, 0, 2)
    dw_out_blocks = dw_out.astype(BF16).reshape(N_DEV, D_MODEL // N_DEV, D_MODEL)
    gin_parts, gout_parts = _reduce_scatter([dw_in_blocks, dw_out_blocks], "scatter_grads")
    res_in = _adamw_sharded(win, gin_parts, m_w_in[0], v_w_in[0], 256, "adamw_w_in")
    res_out = _adamw_sharded(wout, gout_parts, m_w_out[0], v_w_out[0], 128, "adamw_w_out")

    zero_row = jnp.zeros((1, D_MODEL), F32)
    vec_rows = [(V_DSHIFT, dshift), (V_DSCALE, dscale), (V_DGATE, dgate), (V_DCSHIFT, dcshift),
                (V_DCSCALE, dcscale), (V_ZERO, zero_row), (V_DNG, dng), (V_DNG_CTX, dng_c)]
    vloc, mloc = _pack_small_grads(vec_rows, dsg, dgq, dgk, dgk_c, dws, dbs, drpb)
    vg, mg = _allgather([vloc, mloc], "gather_small")
    small_w = (b_ada, norm_g, sgu_norm_g, w_spatial, b_spatial, q_norm_g, k_norm_g, rpb)
    small_m = (m_b_ada, m_norm_g, m_sgu_norm_g, m_w_spatial, m_b_spatial, m_q_norm_g, m_k_norm_g, m_rpb)
    small_v = (v_b_ada, v_norm_g, v_sgu_norm_g, v_w_spatial, v_b_spatial, v_q_norm_g, v_k_norm_g, v_rpb)
    res_small = _adamw_small(vg, mg, small_w, small_m, small_v)

    dm_all = vg[:, V_DSHIFT:V_DGATE + 1, :].reshape(N_DEV, 3 * D_MODEL)
    dc_all = vg[:, V_DCSHIFT:V_ZERO + 1, :].reshape(N_DEV, 3 * D_MODEL)
    dm_sh = lax.dynamic_slice(dm_all, (0, me * ada_w), (N_DEV, ada_w))
    dc_sh = lax.dynamic_slice(dc_all, (0, me * ada_w), (N_DEV, ada_w))
    *res_ada, pc = _ada_bwd(s16, dm_sh, dc_sh, wada, m_w_ada[0], v_w_ada[0])
    (pc_g,) = _allgather([pc], "gather_cctx")
    res_cctx = _adamw_cctx(pc_g, c_ctx_row, m_c_ctx.reshape(1, D_MODEL), v_c_ctx.reshape(1, D_MODEL))

    loss = lax.psum(loss_part[0, 0], ("x", "y", "c"))
    outs = [loss, grad_x[None]]
    for kind in range(4):
        by_name = dict(zip(SMALL_NAMES, res_small[kind]))
        by_name.update(c_ctx=res_cctx[kind].reshape(D_MODEL), w_ada=res_ada[kind][None],
                       w_in=res_in[kind][None], w_out=res_out[kind][None])
        outs += [by_name[nme] for nme in ("c_ctx", "w_ada", "b_ada", "norm_g", "w_in", "sgu_norm_g", "w_spatial",
                                          "b_spatial", "q_norm_g", "k_norm_g", "rpb", "w_out")]
    return tuple(outs)
```

```python
import functools

import numpy as np
import jax
import jax.numpy as jnp
from jax import lax
from jax.experimental import pallas as pl
from jax.experimental.pallas import tpu as pltpu

F32 = jnp.float32
BF16 = jnp.bfloat16
HI = lax.Precision.HIGHEST

N_DEV = 8
D_MODEL = 1024
D_A = 512
D_B = 512
D_IN = 3584
N_BRANCH = 7
HEAD_DIM = 64
N_HEADS = 8
GRID_W = 64
WIN_R = 8
WIN_C = 16
CHUNK = 128
SGU_GROUPS = 4
EPS = 1e-6
NEG_INF = -1e30
Q_ROWS = 4
K_ROWS = 12
TQ = Q_ROWS * GRID_W
TK = K_ROWS * GRID_W
N_DIAG = 22
ATT_SUB = 2
ATT_SCALE = HEAD_DIM ** -0.5

ADAM_LR = 0.001
ADAM_B1 = 0.9
ADAM_B2 = 0.999
ADAM_EPS = 1e-08
ADAM_WD = 0.01
ADAM_STEP = 10

VMEM_LIMIT = 56 * 1024 * 1024
MESH = pl.DeviceIdType.MESH

V_DSHIFT, V_DSCALE, V_DGATE, V_DCSHIFT, V_DCSCALE, V_ZERO, V_DNG, V_DNG_CTX, V_MISC = range(9)
M_DBS, M_RPB, M_ROWS = 512, 544, 800


def _params(sem=None):
    return pltpu.CompilerParams(dimension_semantics=sem, vmem_limit_bytes=VMEM_LIMIT)


def _sigmoid(x):
    return 1.0 / (1.0 + jnp.exp(-x))


def _gelu_parts(x):
    cdf = 0.5 * (1.0 + lax.erf(x * 0.7071067811865476))
    pdf = jnp.exp(-0.5 * x * x) * 0.3989422804014327
    return x * cdf, cdf + x * pdf


def _nt(a, b):
    return lax.dot_general(a, b, (((1,), (1,)), ((), ())), preferred_element_type=F32)


def _tn(a, b):
    return lax.dot_general(a, b, (((0,), (0,)), ((), ())), preferred_element_type=F32)


def _dot2(v, ones_bf):
    hi = v.astype(BF16)
    lo = (v - hi.astype(F32)).astype(BF16)
    return (jnp.dot(hi, ones_bf, preferred_element_type=F32)
            + jnp.dot(lo, ones_bf, preferred_element_type=F32))


def _adam(w, g, m, v):
    m2 = ADAM_B1 * m + (1.0 - ADAM_B1) * g
    v2 = ADAM_B2 * v + (1.0 - ADAM_B2) * (g * g)
    m_hat = m2 / (1.0 - ADAM_B1 ** ADAM_STEP)
    v_hat = v2 / (1.0 - ADAM_B2 ** ADAM_STEP)
    delta = -ADAM_LR * (m_hat / (jnp.sqrt(v_hat) + ADAM_EPS) + ADAM_WD * w)
    return delta, m2, v2


def _allgather(arrs, name):
    n = len(arrs)
    outs_shape = [jax.ShapeDtypeStruct((N_DEV,) + tuple(a.shape), a.dtype) for a in arrs]

    def body(*refs):
        ins, outs = refs[:n], refs[n:2 * n]
        send_sems, recv_sems, loc_sems = refs[2 * n:]
        x, y, c = lax.axis_index("x"), lax.axis_index("y"), lax.axis_index("c")
        me, sib = (x, y, c), (x, y, 1 - c)
        chips = [(1 - x, y), (x, 1 - y), (1 - x, 1 - y)]

        def slot(px, py, pc):
            return 4 * px + 2 * py + pc

        def copy(a, k, block, to, src=None):
            return pltpu.make_async_remote_copy(
                src_ref=outs[a].at[slot(*block)] if src is None else src, dst_ref=outs[a].at[slot(*block)],
                send_sem=send_sems.at[7 * a + k], recv_sem=recv_sems.at[7 * a + k],
                device_id=to, device_id_type=MESH)

        started, locs = [], []
        for a in range(n):
            lc = pltpu.make_async_copy(ins[a], outs[a].at[slot(*me)], loc_sems.at[a])
            lc.start()
            locs.append(lc)
            for j, chip in enumerate(chips):
                started.append(copy(a, 1 + j, me, (*chip, c), src=ins[a]))
            started.append(copy(a, 0, me, sib, src=ins[a]))
        for cp in started:
            cp.start()
        for j, chip in enumerate(chips):
            for a in range(n):
                copy(a, 1 + j, (*chip, c), me).wait_recv()
                fwd = copy(a, 4 + j, (*chip, c), sib)
                fwd.start()
                started.append(fwd)
        for a in range(n):
            copy(a, 0, sib, me).wait_recv()
            for j, chip in enumerate(chips):
                copy(a, 4 + j, (*chip, 1 - c), me).wait_recv()
        for cp in started:
            cp.wait_send()
        for lc in locs:
            lc.wait()

    res = pl.pallas_call(
        body, name=name, out_shape=outs_shape,
        in_specs=[pl.BlockSpec(memory_space=pl.ANY)] * n,
        out_specs=[pl.BlockSpec(memory_space=pl.ANY)] * n,
        scratch_shapes=[pltpu.SemaphoreType.DMA((7 * n,)), pltpu.SemaphoreType.DMA((7 * n,)),
                        pltpu.SemaphoreType.DMA((n,))],
    )(*arrs)
    return list(res)


def _reduce_scatter(arrs, name):
    n = len(arrs)
    n_chip = N_DEV // 2
    outs_shape = [jax.ShapeDtypeStruct((n_chip,) + tuple(a.shape[1:]), a.dtype) for a in arrs]

    def body(*refs):
        ins, outs = refs[:n], refs[n:2 * n]
        tmps, sums = refs[2 * n:3 * n], refs[3 * n:4 * n]
        s1, r1, s2, r2 = refs[4 * n:]
        x, y, c = lax.axis_index("x"), lax.axis_index("y"), lax.axis_index("c")
        my_chip = 2 * x + y
        sib = (x, y, 1 - c)
        all_chips = [(0, 0), (0, 1), (1, 0), (1, 1)]
        others = [(1 - x, y), (x, 1 - y), (1 - x, 1 - y)]

        swaps = []
        for a in range(n):
            for q, (qx, qy) in enumerate(all_chips):
                cp = pltpu.make_async_remote_copy(
                    src_ref=ins[a].at[4 * qx + 2 * qy + (1 - c)], dst_ref=tmps[a].at[q],
                    send_sem=s1.at[n_chip * a + q], recv_sem=r1.at[n_chip * a + q],
                    device_id=sib, device_id_type=MESH)
                cp.start()
                swaps.append(cp)
        for a in range(n):
            for q, (qx, qy) in enumerate(all_chips):
                swaps[n_chip * a + q].wait_recv()
                both = ins[a][4 * qx + 2 * qy + c].astype(F32) + tmps[a][q].astype(F32)
                sums[a][q] = both.astype(sums[a].dtype)
        sends = []
        for a in range(n):
            for j, (tx, ty) in enumerate(others):
                cp = pltpu.make_async_remote_copy(
                    src_ref=sums[a].at[2 * tx + ty], dst_ref=outs[a].at[my_chip],
                    send_sem=s2.at[3 * a + j], recv_sem=r2.at[3 * a + j],
                    device_id=(tx, ty, c), device_id_type=MESH)
                cp.start()
                sends.append(cp)
            outs[a][my_chip] = sums[a][my_chip]
        for a in range(n):
            for j, (tx, ty) in enumerate(others):
                pltpu.make_async_remote_copy(
                    src_ref=sums[a].at[2 * tx + ty], dst_ref=outs[a].at[2 * tx + ty],
                    send_sem=s2.at[3 * a + j], recv_sem=r2.at[3 * a + j],
                    device_id=(tx, ty, c), device_id_type=MESH).wait_recv()
        for cp in swaps + sends:
            cp.wait_send()

    vm = pl.BlockSpec(memory_space=pltpu.VMEM)
    res = pl.pallas_call(
        body, name=name, out_shape=outs_shape, in_specs=[vm] * n, out_specs=[vm] * n,
        scratch_shapes=([pltpu.VMEM(s.shape, s.dtype) for s in outs_shape] * 2
                        + [pltpu.SemaphoreType.DMA((n_chip * n,)), pltpu.SemaphoreType.DMA((n_chip * n,)),
                           pltpu.SemaphoreType.DMA((3 * n,)), pltpu.SemaphoreType.DMA((3 * n,))]),
        compiler_params=_params(),
    )(*arrs)
    return list(res)


def _ada_fwd(c_stack, c_ctx_row, w_sh, b_sh):
    def body(cs_ref, cc_ref, w_ref, b_ref, s_ref, p_ref):
        c_all = cs_ref[0]
        for d in range(1, N_DEV):
            c_all = c_all + cs_ref[d]
        row = lax.broadcasted_iota(jnp.int32, (8, D_MODEL), 0)
        cc = jnp.where(row == 0, jnp.broadcast_to(cc_ref[...], (8, D_MODEL)), 0.0)
        call = jnp.concatenate([c_all, cc], axis=0)
        s = call * _sigmoid(call)
        s_ref[...] = s
        p_ref[...] = jnp.dot(s, w_ref[...], preferred_element_type=F32, precision=HI) + b_ref[...]

    return pl.pallas_call(
        body, name="ada_fwd",
        out_shape=[jax.ShapeDtypeStruct((16, D_MODEL), F32), jax.ShapeDtypeStruct((16, w_sh.shape[1]), F32)],
        compiler_params=_params(),
    )(c_stack, c_ctx_row, w_sh, b_sh)


def _ada_bwd(s16, dm, dc, w, m, v):
    def body(s_ref, dm_ref, dc_ref, w_ref, m_ref, v_ref, g_ref, d_ref, m2_ref, v2_ref, pc_ref):
        dct = jnp.sum(dc_ref[...], axis=0, keepdims=True)
        row = lax.broadcasted_iota(jnp.int32, dc_ref.shape, 0)
        dcb = jnp.where(row == 0, jnp.broadcast_to(dct, dc_ref.shape), 0.0)
        dm16 = jnp.concatenate([dm_ref[...], dcb], axis=0)
        g = lax.dot_general(s_ref[...], dm16, (((0,), (0,)), ((), ())),
                            preferred_element_type=F32, precision=HI)
        w_ = w_ref[...]
        delta, m2, v2 = _adam(w_, g, m_ref[...], v_ref[...])
        g_ref[...] = g
        d_ref[...] = delta
        m2_ref[...] = m2
        v2_ref[...] = v2
        pc_ref[...] = lax.dot_general(dcb, w_, (((1,), (1,)), ((), ())),
                                      preferred_element_type=F32, precision=HI)

    sh = jax.ShapeDtypeStruct(w.shape, F32)
    return pl.pallas_call(
        body, name="ada_bwd",
        out_shape=[sh, sh, sh, sh, jax.ShapeDtypeStruct((8, D_MODEL), F32)],
        compiler_params=_params(),
    )(s16, dm, dc, w, m, v)


def _head_norm(zk, ones_ref, gain):
    ss = _dot2(zk * zk, ones_ref[...])
    return zk * lax.rsqrt(ss * (1.0 / HEAD_DIM) + EPS) * gain


def _inproj_fwd(x, ng, scale, shift, w_in_t, ones_blk, gq, gk, tm):
    T = x.shape[0]

    def body(x_ref, g_ref, sc_ref, sh_ref, w_ref, ones_ref, gq_ref, gk_ref, h_ref, z_ref, qs_ref, kn_ref):
        xv = x_ref[...]
        r = lax.rsqrt(jnp.mean(xv * xv, axis=-1, keepdims=True) + EPS)
        h = (xv * r * g_ref[...]) * (1.0 + sc_ref[...]) + sh_ref[...]
        hb = h.astype(BF16)
        h_ref[...] = hb
        for k in range(N_BRANCH):
            zk = _nt(hb, w_ref[512 * k:512 * (k + 1), :])
            z_ref[:, 512 * k:512 * (k + 1)] = zk.astype(BF16)
            if k == 3:
                qs_ref[...] = (_head_norm(zk, ones_ref, gq_ref[...]) * ATT_SCALE).astype(BF16)
            if k == 4:
                kn_ref[...] = _head_norm(zk, ones_ref, gk_ref[...]).astype(BF16)

    vec = pl.BlockSpec((1, D_MODEL), lambda i: (0, 0))
    v512 = pl.BlockSpec((1, 512), lambda i: (0, 0))
    return pl.pallas_call(
        body, name="inproj_fwd", grid=(T // tm,),
        in_specs=[pl.BlockSpec((tm, D_MODEL), lambda i: (i, 0)), vec, vec, vec,
                  pl.BlockSpec((D_IN, D_MODEL), lambda i: (0, 0)),
                  pl.BlockSpec((512, 512), lambda i: (0, 0)), v512, v512],
        out_specs=[pl.BlockSpec((tm, D_MODEL), lambda i: (i, 0)),
                   pl.BlockSpec((tm, D_IN), lambda i: (i, 0)),
                   pl.BlockSpec((tm, 512), lambda i: (i, 0)),
                   pl.BlockSpec((tm, 512), lambda i: (i, 0))],
        out_shape=[jax.ShapeDtypeStruct((T, D_MODEL), BF16), jax.ShapeDtypeStruct((T, D_IN), BF16),
                   jax.ShapeDtypeStruct((T, 512), BF16), jax.ShapeDtypeStruct((T, 512), BF16)],
        compiler_params=_params(("arbitrary",)),
    )(x, ng, scale, shift, w_in_t, ones_blk, gq, gk)


def _ctx_fwd(ctx, ng, cscale, cshift, w_in_t, ones_blk, gk):
    C = ctx.shape[0]

    def body(x_ref, g_ref, sc_ref, sh_ref, w_ref, ones_ref, gk_ref, h_ref, z_ref, kn_ref):
        xv = x_ref[...]
        r = lax.rsqrt(jnp.mean(xv * xv, axis=-1, keepdims=True) + EPS)
        h = (xv * r * g_ref[...]) * (1.0 + sc_ref[...]) + sh_ref[...]
        hb = h.astype(BF16)
        h_ref[...] = hb
        zk = _nt(hb, w_ref[0:512, :])
        zv = _nt(hb, w_ref[512:1024, :])
        z_ref[:, 0:512] = zk.astype(BF16)
        z_ref[:, 512:1024] = zv.astype(BF16)
        kn_ref[...] = _head_norm(zk, ones_ref, gk_ref[...]).astype(BF16)

    vec = pl.BlockSpec((1, D_MODEL), lambda i: (0, 0))
    return pl.pallas_call(
        body, name="ctx_fwd", grid=(1,),
        in_specs=[pl.BlockSpec((C, D_MODEL), lambda i: (0, 0)), vec, vec, vec,
                  pl.BlockSpec((1024, D_MODEL), lambda i: (2, 0)),
                  pl.BlockSpec((512, 512), lambda i: (0, 0)), pl.BlockSpec((1, 512), lambda i: (0, 0))],
        out_specs=[pl.BlockSpec((C, D_MODEL), lambda i: (0, 0)), pl.BlockSpec((C, 1024), lambda i: (0, 0)),
                   pl.BlockSpec((C, 512), lambda i: (0, 0))],
        out_shape=[jax.ShapeDtypeStruct((C, D_MODEL), BF16), jax.ShapeDtypeStruct((C, 1024), BF16),
                   jax.ShapeDtypeStruct((C, 512), BF16)],
        compiler_params=_params(("arbitrary",)),
    )(ctx, ng, cscale, cshift, w_in_t, ones_blk, gk)


def _sgu_chunk_fwd(au, av, ag, sg, ws_bf, bsb):
    gu, dgu = _gelu_parts(au)
    gv, dgv = _gelu_parts(av)
    rr = lax.rsqrt(jnp.mean(gv * gv, axis=-1, keepdims=True) + EPS)
    vhat = gv * rr
    vn = vhat * sg
    mixed = jnp.dot(ws_bf, vn.astype(BF16), preferred_element_type=F32) + bsb
    sig = _sigmoid(ag)
    sl = ag * sig
    return gu * mixed * sl, (gu, dgu, dgv, rr, vhat, vn, mixed, sig, sl)


def _sgu_fwd(z, sgn, ws, bsb, tm):
    T = z.shape[0]

    def body(au_ref, av_ref, ag_ref, sg_ref, ws_ref, bsb_ref, o_ref):
        for g in range(SGU_GROUPS):
            ws_bf = ws_ref[g].astype(BF16)
            sg = sg_ref[:, 128 * g:128 * (g + 1)]
            bsb_g = bsb_ref[g]
            for j in range(tm // CHUNK):
                rs, cs = slice(CHUNK * j, CHUNK * (j + 1)), slice(128 * g, 128 * (g + 1))
                out, _ = _sgu_chunk_fwd(au_ref[rs, cs].astype(F32), av_ref[rs, cs].astype(F32),
                                        ag_ref[rs, cs].astype(F32), sg, ws_bf, bsb_g)
                o_ref[rs, cs] = out.astype(BF16)

    return pl.pallas_call(
        body, name="sgu_fwd", grid=(T // tm,),
        in_specs=[pl.BlockSpec((tm, 512), lambda i: (i, 0)), pl.BlockSpec((tm, 512), lambda i: (i, 1)),
                  pl.BlockSpec((tm, 512), lambda i: (i, 2)), pl.BlockSpec((1, 512), lambda i: (0, 0)),
                  pl.BlockSpec((SGU_GROUPS, CHUNK, CHUNK), lambda i: (0, 0, 0)),
                  pl.BlockSpec((SGU_GROUPS, CHUNK, 128), lambda i: (0, 0, 0))],
        out_specs=pl.BlockSpec((tm, 512), lambda i: (i, 0)),
        out_shape=jax.ShapeDtypeStruct((T, 512), BF16),
        compiler_params=_params(("arbitrary",)),
    )(z, z, z, sgn, ws, bsb)


def _attn_type(rb, nrb):
    return jnp.where(rb == 0, 0, jnp.where(rb == nrb - 1, 2, 1))


def _attn_specs(T, C, nrb):
    return [
        pl.BlockSpec((ATT_SUB * TQ, 128), lambda hp, st: (st, hp)),
        pl.BlockSpec((T, 128), lambda hp, st: (0, hp)),
        pl.BlockSpec((T, 128), lambda hp, st: (0, 20 + hp)),
        pl.BlockSpec((C, 128), lambda hp, st: (0, hp)),
        pl.BlockSpec((C, 128), lambda hp, st: (0, 4 + hp)),
    ] + [
        pl.BlockSpec((1, 2, TQ, TK), lambda hp, st, sub=sub: (_attn_type(ATT_SUB * st + sub, nrb), hp, 0, 0))
        for sub in range(ATT_SUB)
    ] + [pl.BlockSpec((ATT_SUB * TQ, 128), lambda hp, st: (st, 24 + hp))]


def _attn_fwd(qs, kn, z, ckn, zc, bias):
    T, C = qs.shape[0], ckn.shape[0]
    rows = T // GRID_W
    nrb = rows // Q_ROWS

    def body(q_ref, k_ref, v_ref, ck_ref, cv_ref, *rest):
        bias_refs = rest[:ATT_SUB]
        bg_ref, ob_ref, outb_ref, lse_ref = rest[ATT_SUB:]
        ck2, cv2 = ck_ref[...], cv_ref[...]
        lane = lax.broadcasted_iota(jnp.int32, (1, 128), 1)
        for sub, bias_ref in enumerate(bias_refs):
            rb = ATT_SUB * pl.program_id(1) + sub
            rs = slice(TQ * sub, TQ * (sub + 1))
            ks = pl.multiple_of(jnp.clip(Q_ROWS * rb - 4, 0, rows - K_ROWS) * GRID_W, GRID_W)
            q2 = q_ref[rs, :]
            k2 = k_ref[pl.ds(ks, TK), :]
            v2 = v_ref[pl.ds(ks, TK), :]
            o_acc = jnp.zeros((TQ, 128), F32)
            lse_acc = jnp.zeros((TQ, 128), F32)
            for hh in range(2):
                msk = (lane >= HEAD_DIM) == bool(hh)
                qm = jnp.where(msk, q2, jnp.zeros_like(q2))
                s = _nt(qm, k2) + bias_ref[0, hh]
                sc = _nt(qm, ck2)
                m = jnp.maximum(jnp.max(s, axis=-1, keepdims=True), jnp.max(sc, axis=-1, keepdims=True))
                p = jnp.exp(s - m)
                pc = jnp.exp(sc - m)
                den = jnp.sum(p, axis=-1, keepdims=True) + jnp.sum(pc, axis=-1, keepdims=True)
                o = (jnp.dot(p.astype(BF16), v2, preferred_element_type=F32)
                     + jnp.dot(pc.astype(BF16), cv2, preferred_element_type=F32)) / den
                o_acc = jnp.where(msk, o, o_acc)
                lse_acc = jnp.where(msk, m + jnp.log(den), lse_acc)
            ob_ref[rs, :] = o_acc.astype(BF16)
            lse_ref[rs, :] = lse_acc
            bg = bg_ref[rs, :].astype(F32)
            outb_ref[rs, :] = (o_acc * (bg * _sigmoid(bg))).astype(BF16)

    tile = pl.BlockSpec((ATT_SUB * TQ, 128), lambda hp, st: (st, hp))
    return pl.pallas_call(
        body, name="attn_fwd", grid=(4, nrb // ATT_SUB),
        in_specs=_attn_specs(T, C, nrb),
        out_specs=[tile, tile, tile],
        out_shape=[jax.ShapeDtypeStruct((T, 512), BF16), jax.ShapeDtypeStruct((T, 512), BF16),
                   jax.ShapeDtypeStruct((T, 512), F32)],
        compiler_params=_params(("arbitrary", "arbitrary")),
    )(qs, kn, z, ckn, zc, *([bias] * ATT_SUB), z)


def _outproj_loss_bwd(x, tgt, out_a, out_b, gate, w_out, tm):
    T = x.shape[0]

    def body(x_ref, t_ref, oa_ref, ob_ref, gate_ref, w_ref, dy_ref, dmc_ref, dw_ref, dgate_ref, loss_ref):
        @pl.when(pl.program_id(0) == 0)
        def _():
            dw_ref[...] = jnp.zeros_like(dw_ref)
            dgate_ref[...] = jnp.zeros_like(dgate_ref)
            loss_ref[...] = jnp.zeros_like(loss_ref)

        oa, ob = oa_ref[...], ob_ref[...]
        gate_v = gate_ref[...]
        mix = (jnp.dot(oa, w_ref[0:512, :], preferred_element_type=F32)
               + jnp.dot(ob, w_ref[512:1024, :], preferred_element_type=F32))
        e = x_ref[...] + gate_v * mix - t_ref[...]
        se = jnp.sum(jnp.sum(e * e, axis=0, keepdims=True), axis=1, keepdims=True)
        loss_ref[...] += jnp.broadcast_to(se * (0.5 / D_MODEL), loss_ref.shape)
        dy = e * (1.0 / D_MODEL)
        dy_ref[...] = dy
        dgate_ref[...] += jnp.sum(dy * mix, axis=0, keepdims=True)
        dmix = (dy * gate_v).astype(BF16)
        dmc_ref[...] = _nt(dmix, w_ref[...]).astype(BF16)
        dw_ref[0:512, :] += _tn(oa, dmix)
        dw_ref[512:1024, :] += _tn(ob, dmix)

    row = lambda w: pl.BlockSpec((tm, w), lambda i: (i, 0))
    return pl.pallas_call(
        body, name="outproj_loss_bwd", grid=(T // tm,),
        in_specs=[row(D_MODEL), row(D_MODEL), row(512), row(512),
                  pl.BlockSpec((1, D_MODEL), lambda i: (0, 0)),
                  pl.BlockSpec((D_MODEL, D_MODEL), lambda i: (0, 0))],
        out_specs=[row(D_MODEL), row(D_MODEL), pl.BlockSpec((D_MODEL, D_MODEL), lambda i: (0, 0)),
                   pl.BlockSpec((1, D_MODEL), lambda i: (0, 0)), pl.BlockSpec((1, 128), lambda i: (0, 0))],
        out_shape=[jax.ShapeDtypeStruct((T, D_MODEL), F32), jax.ShapeDtypeStruct((T, D_MODEL), BF16),
                   jax.ShapeDtypeStruct((D_MODEL, D_MODEL), F32), jax.ShapeDtypeStruct((1, D_MODEL), F32),
                   jax.ShapeDtypeStruct((1, 128), F32)],
        compiler_params=_params(("arbitrary",)),
    )(x, tgt, out_a, out_b, gate, w_out)


def _attn_bwd(qs, kn, z, ckn, zc, bias, ob, lse, dmc):
    T, C = qs.shape[0], ckn.shape[0]
    rows = T // GRID_W
    nrb = rows // Q_ROWS

    def body(q_ref, k_ref, v_ref, ck_ref, cv_ref, *rest):
        bias_refs = rest[:ATT_SUB]
        (bg_ref, ob_ref, lse_ref, do_ref,
         dq_ref, dk_ref, dv_ref, dck_ref, dcv_ref, dbg_ref, dacc_ref) = rest[ATT_SUB:]

        @pl.when(pl.program_id(1) == 0)
        def _():
            dk_ref[...] = jnp.zeros_like(dk_ref)
            dv_ref[...] = jnp.zeros_like(dv_ref)
            dck_ref[...] = jnp.zeros_like(dck_ref)
            dcv_ref[...] = jnp.zeros_like(dcv_ref)
            dacc_ref[...] = jnp.zeros_like(dacc_ref)

        ck2, cv2 = ck_ref[...], cv_ref[...]
        lane = lax.broadcasted_iota(jnp.int32, (1, 128), 1)
        for sub, bias_ref in enumerate(bias_refs):
            rb = ATT_SUB * pl.program_id(1) + sub
            rs = slice(TQ * sub, TQ * (sub + 1))
            kb = jnp.clip(Q_ROWS * rb - 4, 0, rows - K_ROWS)
            ks = pl.multiple_of(kb * GRID_W, GRID_W)
            ebase = kb - Q_ROWS * rb + 11
            q2 = q_ref[rs, :]
            k2 = k_ref[pl.ds(ks, TK), :]
            v2 = v_ref[pl.ds(ks, TK), :]
            bg = bg_ref[rs, :].astype(F32)
            sig = _sigmoid(bg)
            obv = ob_ref[rs, :].astype(F32)
            dout = do_ref[rs, :].astype(F32)
            dbg_ref[rs, :] = (dout * obv * (sig * (1.0 + bg * (1.0 - sig)))).astype(BF16)
            d_o = dout * (bg * sig)
            d_oo = d_o * obv
            lse2 = lse_ref[rs, :]
            dq_acc = jnp.zeros((TQ, 128), F32)
            for hh in range(2):
                msk = (lane >= HEAD_DIM) == bool(hh)
                qm = jnp.where(msk, q2, jnp.zeros_like(q2))
                lse_h = jnp.max(jnp.where(msk, lse2, -jnp.inf), axis=-1, keepdims=True)
                p = jnp.exp(_nt(qm, k2) + bias_ref[0, hh] - lse_h)
                pc = jnp.exp(_nt(qm, ck2) - lse_h)
                dom = jnp.where(msk, d_o, 0.0).astype(BF16)
                delta = jnp.sum(jnp.where(msk, d_oo, 0.0), axis=-1, keepdims=True)
                ds = p * (_nt(dom, v2) - delta)
                dsc = pc * (_nt(dom, cv2) - delta)
                dsb, dscb = ds.astype(BF16), dsc.astype(BF16)
                dq_h = (jnp.dot(dsb, k2, preferred_element_type=F32)
                        + jnp.dot(dscb, ck2, preferred_element_type=F32))
                dq_acc = jnp.where(msk, dq_h, dq_acc)
                dk_ref[pl.ds(ks, TK), :] += _tn(dsb, qm)
                dv_ref[pl.ds(ks, TK), :] += _tn(p.astype(BF16), dom)
                dck_ref[...] += _tn(dscb, qm)
                dcv_ref[...] += _tn(pc.astype(BF16), dom)
                for i in range(Q_ROWS):
                    for mm in range(K_ROWS // 2):
                        dacc_ref[hh, ebase + (2 * mm - i)] += ds[GRID_W * i:GRID_W * (i + 1),
                                                                 128 * mm:128 * (mm + 1)]
            dq_ref[rs, :] = dq_acc

    tile = pl.BlockSpec((ATT_SUB * TQ, 128), lambda hp, st: (st, hp))
    colT = pl.BlockSpec((T, 128), lambda hp, st: (0, hp))
    colC = pl.BlockSpec((C, 128), lambda hp, st: (0, hp))
    return pl.pallas_call(
        body, name="attn_bwd", grid=(4, nrb // ATT_SUB),
        in_specs=_attn_specs(T, C, nrb) + [tile, tile,
                                           pl.BlockSpec((ATT_SUB * TQ, 128), lambda hp, st: (st, 4 + hp))],
        out_specs=[tile, colT, colT, colC, colC, tile,
                   pl.BlockSpec((2, N_DIAG, GRID_W, 128), lambda hp, st: (hp, 0, 0, 0))],
        out_shape=[jax.ShapeDtypeStruct((T, 512), F32), jax.ShapeDtypeStruct((T, 512), F32),
                   jax.ShapeDtypeStruct((T, 512), F32), jax.ShapeDtypeStruct((C, 512), F32),
                   jax.ShapeDtypeStruct((C, 512), F32), jax.ShapeDtypeStruct((T, 512), BF16),
                   jax.ShapeDtypeStruct((N_HEADS, N_DIAG, GRID_W, 128), F32)],
        compiler_params=_params(("arbitrary", "arbitrary")),
    )(qs, kn, z, ckn, zc, *([bias] * ATT_SUB), z, ob, lse, dmc)


def _rpb_grad(dacc, hsel, fold):
    n_off = 2 * WIN_C - 1
    n_dr = 2 * WIN_R - 1

    def body(a_ref, hsel_ref, fold_ref, o_ref):
        qc = lax.broadcasted_iota(jnp.int32, (GRID_W, 128), 0)
        lane = lax.broadcasted_iota(jnp.int32, (GRID_W, 128), 1)
        diff = lane % GRID_W - qc + (WIN_C - 1)
        left = lane < GRID_W

        def by_dr(dr):
            return a_ref[0, dr + 4] + pltpu.roll(a_ref[0, dr + 3], GRID_W, 1)

        out = jnp.zeros((32, 128), F32)
        for j in range((n_dr + 1) // 2):
            hi = pltpu.roll(by_dr(2 * j + 1), GRID_W, 1) if 2 * j + 1 < n_dr else 0.0
            pair = jnp.where(left, by_dr(2 * j), hi)
            parts = []
            for o in range(n_off):
                mv = jnp.where(diff == o, pair, 0.0)
                acc = mv[0:8]
                for r8 in range(1, GRID_W // 8):
                    acc = acc + mv[8 * r8:8 * (r8 + 1)]
                parts.append(acc)
            parts.append(jnp.zeros((8, 128), F32))
            stack = jnp.concatenate(parts, axis=0)
            s_hi = stack.astype(BF16)
            s_lo = (stack - s_hi.astype(F32)).astype(BF16)
            per_o = (jnp.dot(fold_ref[...], s_hi, preferred_element_type=F32)
                     + jnp.dot(fold_ref[...], s_lo, preferred_element_type=F32))
            out = out + _dot2(per_o, hsel_ref[j])
        o_ref[0] = out

    return pl.pallas_call(
        body, name="rpb_grad", grid=(N_HEADS,),
        in_specs=[pl.BlockSpec((1, N_DIAG, GRID_W, 128), lambda h: (h, 0, 0, 0)),
                  pl.BlockSpec((8, 128, 128), lambda h: (0, 0, 0)), pl.BlockSpec((32, 256), lambda h: (0, 0))],
        out_specs=pl.BlockSpec((1, 32, 128), lambda h: (h, 0, 0)),
        out_shape=jax.ShapeDtypeStruct((N_HEADS, 32, 128), F32),
        compiler_params=_params(("arbitrary",)),
    )(dacc, hsel, fold)


def _qk_bwd(z, col_q, col_k, dqs, dk, dv, ones_blk, gq, gk, foldm, tm, name):
    T = dk.shape[0]
    with_q = dqs is not None

    def norm_bwd(raw, dn, gain, ones_ref):
        rr = lax.rsqrt(_dot2(raw * raw, ones_ref[...]) * (1.0 / HEAD_DIM) + EPS)
        hat = raw * rr
        dgain = jnp.sum(dn * hat, axis=0, keepdims=True)
        dhat = dn * gain
        mean = _dot2(dhat * hat, ones_ref[...]) * (1.0 / HEAD_DIM)
        return rr * (dhat - hat * mean), dgain

    def body(*refs):
        if with_q:
            (bq_ref, bk_ref, dq_ref, dk_ref, dv_ref, ones_ref, gq_ref, gk_ref, fold_ref,
             dbq_ref, dbk_ref, dbv_ref, dgq_ref, dgk_ref, accq, acck) = refs
        else:
            (bk_ref, dk_ref, dv_ref, ones_ref, gk_ref, fold_ref, dbk_ref, dbv_ref, dgk_ref, acck) = refs
        i = pl.program_id(0)

        @pl.when(i == 0)
        def _():
            acck[...] = jnp.zeros_like(acck)
            if with_q:
                accq[...] = jnp.zeros_like(accq)

        dbk, dgk = norm_bwd(bk_ref[...].astype(F32), dk_ref[...], gk_ref[...], ones_ref)
        dbk_ref[...] = dbk.astype(BF16)
        acck[...] += dgk
        dbv_ref[...] = dv_ref[...].astype(BF16)
        if with_q:
            dbq, dgq = norm_bwd(bq_ref[...].astype(F32), dq_ref[...] * ATT_SCALE, gq_ref[...], ones_ref)
            dbq_ref[...] = dbq.astype(BF16)
            accq[...] += dgq

        @pl.when(i == pl.num_programs(0) - 1)
        def _():
            dgk_ref[...] = jnp.dot(jnp.broadcast_to(acck[...], (8, 512)), fold_ref[...],
                                   preferred_element_type=F32, precision=HI)
            if with_q:
                dgq_ref[...] = jnp.dot(jnp.broadcast_to(accq[...], (8, 512)), fold_ref[...],
                                       preferred_element_type=F32, precision=HI)

    row = pl.BlockSpec((tm, 512), lambda i: (i, 0))
    cst = lambda a, b: pl.BlockSpec((a, b), lambda i: (0, 0))
    out_row = jax.ShapeDtypeStruct((T, 512), BF16)
    gvec = jax.ShapeDtypeStruct((8, 128), F32)
    if with_q:
        in_specs = [pl.BlockSpec((tm, 512), lambda i: (i, col_q)), pl.BlockSpec((tm, 512), lambda i: (i, col_k)),
                    row, row, row, cst(512, 512), cst(1, 512), cst(1, 512), cst(512, 128)]
        args = (z, z, dqs, dk, dv, ones_blk, gq, gk, foldm)
        out_specs = [row, row, row, cst(8, 128), cst(8, 128)]
        out_shape = [out_row, out_row, out_row, gvec, gvec]
        scratch = [pltpu.VMEM((1, 512), F32), pltpu.VMEM((1, 512), F32)]
    else:
        in_specs = [pl.BlockSpec((tm, 512), lambda i: (i, col_k)), row, row, cst(512, 512), cst(1, 512), cst(512, 128)]
        args = (z, dk, dv, ones_blk, gk, foldm)
        out_specs = [row, row, cst(8, 128)]
        out_shape = [out_row, out_row, gvec]
        scratch = [pltpu.VMEM((1, 512), F32)]
    return pl.pallas_call(
        body, name=name, grid=(T // tm,), in_specs=in_specs, out_specs=out_specs, out_shape=out_shape,
        scratch_shapes=scratch, compiler_params=_params(("arbitrary",)),
    )(*args)


def _sgu_bwd(z, dmc, sgn, ws, wst, bsb, ones8, tm):
    T = z.shape[0]

    def body(au_ref, av_ref, ag_ref, d_ref, sg_ref, ws_ref, wst_ref, bsb_ref, ones_ref,
             dau_ref, dav_ref, dag_ref, dws_ref, dbs_ref, dsg_ref):
        @pl.when(pl.program_id(0) == 0)
        def _():
            dws_ref[...] = jnp.zeros_like(dws_ref)
            dbs_ref[...] = jnp.zeros_like(dbs_ref)
            dsg_ref[...] = jnp.zeros_like(dsg_ref)

        for g in range(SGU_GROUPS):
            ws_bf = ws_ref[g].astype(BF16)
            wst_bf = wst_ref[g].astype(BF16)
            sg = sg_ref[:, 128 * g:128 * (g + 1)]
            bsb_g = bsb_ref[g]
            for j in range(tm // CHUNK):
                rs, cs = slice(CHUNK * j, CHUNK * (j + 1)), slice(128 * g, 128 * (g + 1))
                au, av, ag = (au_ref[rs, cs].astype(F32), av_ref[rs, cs].astype(F32), ag_ref[rs, cs].astype(F32))
                d = d_ref[rs, cs].astype(F32)
                _, (gu, dgu, dgv, rr, vhat, vn, mixed, sig, sl) = _sgu_chunk_fwd(au, av, ag, sg, ws_bf, bsb_g)
                dau_ref[rs, cs] = (d * mixed * sl * dgu).astype(BF16)
                dag_ref[rs, cs] = (d * gu * mixed * (sig * (1.0 + ag * (1.0 - sig)))).astype(BF16)
                dmixed = d * gu * sl
                dmb = dmixed.astype(BF16)
                dm_lo = (dmixed - dmb.astype(F32)).astype(BF16)
                dbs_ref[g] += _nt(ones_ref[...], dmb) + _nt(ones_ref[...], dm_lo)
                dws_ref[g] += _nt(dmb, vn.astype(BF16))
                dvn = jnp.dot(wst_bf, dmb, preferred_element_type=F32)
                dsg_ref[:, 128 * g:128 * (g + 1)] += jnp.sum(dvn * vhat, axis=0, keepdims=True)
                dvhat = dvn * sg
                mean = jnp.mean(dvhat * vhat, axis=-1, keepdims=True)
                dav_ref[rs, cs] = (rr * (dvhat - vhat * mean) * dgv).astype(BF16)

    row = lambda col: pl.BlockSpec((tm, 512), lambda i: (i, col))
    c3 = lambda n: pl.BlockSpec((SGU_GROUPS, CHUNK, n), lambda i: (0, 0, 0))
    out_row = jax.ShapeDtypeStruct((T, 512), BF16)
    return pl.pallas_call(
        body, name="sgu_bwd", grid=(T // tm,),
        in_specs=[row(0), row(1), row(2), row(0), pl.BlockSpec((1, 512), lambda i: (0, 0)),
                  c3(CHUNK), c3(CHUNK), c3(128), pl.BlockSpec((8, 128), lambda i: (0, 0))],
        out_specs=[row(0), row(0), row(0), c3(CHUNK), pl.BlockSpec((SGU_GROUPS, 8, CHUNK), lambda i: (0, 0, 0)),
                   pl.BlockSpec((1, 512), lambda i: (0, 0))],
        out_shape=[out_row, out_row, out_row, jax.ShapeDtypeStruct((SGU_GROUPS, CHUNK, CHUNK), F32),
                   jax.ShapeDtypeStruct((SGU_GROUPS, 8, CHUNK), F32), jax.ShapeDtypeStruct((1, 512), F32)],
        compiler_params=_params(("arbitrary",)),
    )(z, z, z, dmc, sgn, ws, wst, bsb, ones8)


def _inproj_bwd_dx(dzs, w_cols, w_in_t, x, dy, ng, scale, shift, tm, name):
    T = x.shape[0]
    n = len(dzs)
    with_dx = dy is not None

    def body(*refs):
        dz_refs = refs[:n]
        if with_dx:
            w_ref, x_ref, dy_ref, g_ref, sc_ref, sh_ref, gx_ref, dsh_ref, dsc_ref, dg_ref = refs[n:]
        else:
            w_ref, x_ref, g_ref, sc_ref, sh_ref, dsh_ref, dsc_ref, dg_ref = refs[n:]

        @pl.when(pl.program_id(0) == 0)
        def _():
            dsh_ref[...] = jnp.zeros_like(dsh_ref)
            dsc_ref[...] = jnp.zeros_like(dsc_ref)
            dg_ref[...] = jnp.zeros_like(dg_ref)

        dh = jnp.dot(dz_refs[0][...], w_ref[0:512, :], preferred_element_type=F32)
        for k in range(1, n):
            dh = dh + jnp.dot(dz_refs[k][...], w_ref[512 * k:512 * (k + 1), :], preferred_element_type=F32)
        xv = x_ref[...]
        r = lax.rsqrt(jnp.mean(xv * xv, axis=-1, keepdims=True) + EPS)
        xn = xv * r
        gv, op = g_ref[...], 1.0 + sc_ref[...]
        dsh_ref[...] += jnp.sum(dh, axis=0, keepdims=True)
        dsc_ref[...] += jnp.sum(dh * xn * gv, axis=0, keepdims=True)
        dg_ref[...] += jnp.sum(dh * op * xn, axis=0, keepdims=True)
        if with_dx:
            dxn = dh * (gv * op)
            gx_ref[...] = r * (dxn - xn * jnp.mean(dxn * xn, axis=-1, keepdims=True)) + dy_ref[...]

    vec = pl.BlockSpec((1, D_MODEL), lambda i: (0, 0))
    rowf = pl.BlockSpec((tm, D_MODEL), lambda i: (i, 0))
    in_specs = [pl.BlockSpec((tm, 512), lambda i: (i, 0))] * n
    in_specs += [pl.BlockSpec((512 * n, D_MODEL), lambda i: (w_cols // n, 0)), rowf]
    args = list(dzs) + [w_in_t, x]
    vshape = jax.ShapeDtypeStruct((1, D_MODEL), F32)
    out_specs, out_shape = [vec, vec, vec], [vshape, vshape, vshape]
    if with_dx:
        in_specs.append(rowf)
        args.append(dy)
        out_specs = [rowf] + out_specs
        out_shape = [jax.ShapeDtypeStruct((T, D_MODEL), F32)] + out_shape
    in_specs += [vec, vec, vec]
    args += [ng, scale, shift]
    return pl.pallas_call(
        body, name=name, grid=(T // tm,), in_specs=in_specs, out_specs=out_specs, out_shape=out_shape,
        compiler_params=_params(("arbitrary",)),
    )(*args)


def _inproj_bwd_dw(h, dzs, hc, dzc_k, dzc_v, tk):
    T = h.shape[0]
    nt = T // tk
    n = len(dzs)

    def body(*refs):
        h_ref = refs[0]
        dz_refs = refs[1:1 + n]
        hc_ref, dzck_ref, dzcv_ref, o_ref, acc = refs[1 + n:]
        t = pl.program_id(0)

        @pl.when(t == 0)
        def _():
            acc[...] = jnp.zeros_like(acc)
            hct = hc_ref[...].T
            acc[:, 512 * 4:512 * 5] = jnp.dot(hct, dzck_ref[...], preferred_element_type=F32)
            acc[:, 512 * 5:512 * 6] = jnp.dot(hct, dzcv_ref[...], preferred_element_type=F32)

        ht = h_ref[...].T
        for k in range(n):
            acc[:, 512 * k:512 * (k + 1)] += jnp.dot(ht, dz_refs[k][...], preferred_element_type=F32)

        @pl.when(t == nt - 1)
        def _():
            o_ref[...] = acc[...].astype(BF16)

    whole = lambda a: pl.BlockSpec(a.shape, lambda t: (0, 0))
    return pl.pallas_call(
        body, name="inproj_bwd_dw", grid=(nt,),
        in_specs=([pl.BlockSpec((tk, D_MODEL), lambda t: (t, 0))] + [pl.BlockSpec((tk, 512), lambda t: (t, 0))] * n
                  + [whole(hc), whole(dzc_k), whole(dzc_v)]),
        out_specs=pl.BlockSpec((D_MODEL, 512 * n), lambda t: (0, 0)),
        out_shape=jax.ShapeDtypeStruct((D_MODEL, 512 * n), BF16),
        scratch_shapes=[pltpu.VMEM((D_MODEL, 512 * n), F32)],
        compiler_params=_params(("arbitrary",)),
    )(h, *dzs, hc, dzc_k, dzc_v)


def _adamw_sharded(w, gparts, m, v, tr, name):
    R, C = w.shape
    n_part = gparts.shape[0]

    def body(w_ref, gp_ref, m_ref, v_ref, g_ref, d_ref, m2_ref, v2_ref):
        g = gp_ref[0].astype(F32)
        for d in range(1, n_part):
            g = g + gp_ref[d].astype(F32)
        delta, m2, v2 = _adam(w_ref[...], g, m_ref[...], v_ref[...])
        g_ref[...] = g
        d_ref[...] = delta
        m2_ref[...] = m2
        v2_ref[...] = v2

    row = pl.BlockSpec((tr, C), lambda i: (i, 0))
    sh = jax.ShapeDtypeStruct((R, C), F32)
    return pl.pallas_call(
        body, name=name, grid=(R // tr,),
        in_specs=[row, pl.BlockSpec((n_part, tr, C), lambda i: (0, i, 0)), row, row],
        out_specs=[row, row, row, row], out_shape=[sh, sh, sh, sh],
        compiler_params=_params(("arbitrary",)),
    )(w, gparts, m, v)


def _pack_small_grads(vec_rows, dsg, dgq, dgk, dgk_c, loss_part, dws, dbs, drpb):
    n_vec = len(vec_rows)

    def body(*refs):
        vecs = refs[:n_vec]
        dsg_ref, dgq_ref, dgk_ref, dgkc_ref, loss_ref, dws_ref, dbs_ref, drpb_ref, v_ref, m_ref = refs[n_vec:]
        row = lax.broadcasted_iota(jnp.int32, (16, D_MODEL), 0)
        misc = jnp.concatenate([dsg_ref[...], dgq_ref[0:1, :], dgk_ref[0:1, :], dgkc_ref[0:1, :],
                                loss_ref[...]], axis=1)
        v = jnp.where(row == V_MISC, jnp.broadcast_to(misc, (16, D_MODEL)), 0.0)
        for (r, _), ref in zip(vec_rows, vecs):
            v = jnp.where(row == r, jnp.broadcast_to(ref[...], (16, D_MODEL)), v)
        v_ref[...] = v
        for g in range(SGU_GROUPS):
            m_ref[128 * g:128 * (g + 1), :] = dws_ref[g]
            m_ref[M_DBS + 8 * g:M_DBS + 8 * (g + 1), :] = dbs_ref[g]
        for hd in range(N_HEADS):
            m_ref[M_RPB + 32 * hd:M_RPB + 32 * (hd + 1), :] = drpb_ref[hd]

    return pl.pallas_call(
        body, name="pack_small_grads",
        out_shape=[jax.ShapeDtypeStruct((16, D_MODEL), F32), jax.ShapeDtypeStruct((M_ROWS, 128), F32)],
        compiler_params=_params(),
    )(*[a for _, a in vec_rows], dsg, dgq, dgk, dgk_c, loss_part, dws, dbs, drpb)


SMALL_NAMES = ("b_ada", "norm_g", "sgu_norm_g", "w_spatial", "b_spatial", "q_norm_g", "k_norm_g", "rpb")


def _adamw_small(vg, mg, ws, ms, vs):
    k = len(SMALL_NAMES)

    def body(*refs):
        vg_ref, mg_ref = refs[0], refs[1]
        w_refs = dict(zip(SMALL_NAMES, refs[2:2 + k]))
        m_refs = dict(zip(SMALL_NAMES, refs[2 + k:2 + 2 * k]))
        v_refs = dict(zip(SMALL_NAMES, refs[2 + 2 * k:2 + 3 * k]))
        o_refs = [dict(zip(SMALL_NAMES, refs[2 + (3 + i) * k:2 + (4 + i) * k])) for i in range(4)]
        loss_ref = refs[2 + 7 * k]

        sv = vg_ref[0]
        for d in range(1, N_DEV):
            sv = sv + vg_ref[d]
        loss_ref[...] = sv[V_MISC:V_MISC + 1, 896:1024]

        def total(lo, hi):
            s = mg_ref[0, lo:hi, :]
            for d in range(1, N_DEV):
                s = s + mg_ref[d, lo:hi, :]
            return s

        def emit(name, idx, g):
            res = _adam(w_refs[name][idx], g, m_refs[name][idx], v_refs[name][idx])
            for o, val in zip(o_refs, (g,) + res):
                o[name][idx] = val

        everything = (slice(None), slice(None))
        row = lambda r: sv[r:r + 1, :]
        emit("b_ada", everything, jnp.concatenate(
            [row(V_DSHIFT) + row(V_DCSHIFT), row(V_DSCALE) + row(V_DCSCALE), row(V_DGATE)], axis=1))
        emit("norm_g", everything, row(V_DNG) + row(V_DNG_CTX))
        misc = row(V_MISC)
        emit("sgu_norm_g", everything, misc[:, 0:512])
        emit("q_norm_g", everything, misc[:, 512:512 + HEAD_DIM])
        emit("k_norm_g", everything, misc[:, 640:640 + HEAD_DIM] + misc[:, 768:768 + HEAD_DIM])
        for g in range(SGU_GROUPS):
            emit("w_spatial", (0, g), total(128 * g, 128 * (g + 1)))
            emit("b_spatial", (0, slice(g, g + 1), slice(None)), total(M_DBS + 8 * g, M_DBS + 8 * (g + 1))[0:1, :])
        for hd in range(N_HEADS):
            by_dc = total(M_RPB + 32 * hd, M_RPB + 32 * (hd + 1))
            emit("rpb", (0, hd), by_dc.T[0:2 * WIN_R - 1, 0:2 * WIN_C - 1])

    shapes = [jax.ShapeDtypeStruct(w.shape, F32) for w in ws]
    res = pl.pallas_call(body, name="adamw_small", out_shape=shapes * 4 + [jax.ShapeDtypeStruct((1, 128), F32)],
                         compiler_params=_params())(vg, mg, *ws, *ms, *vs)
    return [res[i * k:(i + 1) * k] for i in range(4)], res[4 * k]


def _adamw_cctx(pc_g, w, m, v):
    def body(pc_ref, w_ref, m_ref, v_ref, g_ref, d_ref, m2_ref, v2_ref):
        pc = pc_ref[0, 0:1, :]
        for d in range(1, N_DEV):
            pc = pc + pc_ref[d, 0:1, :]
        cc = w_ref[...]
        sig = _sigmoid(cc)
        g = pc * (sig * (1.0 + cc * (1.0 - sig)))
        delta, m2, v2 = _adam(cc, g, m_ref[...], v_ref[...])
        g_ref[...] = g
        d_ref[...] = delta
        m2_ref[...] = m2
        v2_ref[...] = v2

    sh = jax.ShapeDtypeStruct((1, D_MODEL), F32)
    return pl.pallas_call(body, name="adamw_cctx", out_shape=[sh, sh, sh, sh], compiler_params=_params())(
        pc_g, w, m, v)


def _block_ones(n, blk):
    i = np.arange(n)
    return jnp.asarray((i[:, None] // blk == i[None, :] // blk).astype(np.float32), BF16)


def _rpb_pairs(rpb):
    n_off = 2 * WIN_C - 1
    cols = np.arange(GRID_W)
    c0 = np.clip(cols - WIN_C // 2, 0, GRID_W - WIN_C)
    in_win = (cols[None, :] >= c0[:, None]) & (cols[None, :] < c0[:, None] + WIN_C)
    dc = np.clip(cols[None, :] - cols[:, None] + (WIN_C - 1), 0, n_off - 1)
    expand = (dc[None] == np.arange(n_off)[:, None, None]) & in_win[None]
    toep = jnp.einsum("hrd,dqk->hrqk", rpb, jnp.asarray(expand, F32), precision=HI)
    toep = toep + jnp.asarray(np.where(in_win, 0.0, NEG_INF).astype(np.float32))
    neg = jnp.full((N_HEADS, 1, GRID_W, GRID_W), NEG_INF, F32)
    ext = jnp.concatenate([neg, toep, neg], axis=1)
    return jnp.concatenate([ext[:, :-1], ext[:, 1:]], axis=-1)


def _row_mask(rows):
    nrb = rows // Q_ROWS
    valid = np.zeros((3, Q_ROWS, 1, K_ROWS, 1), bool)
    for t, rb in enumerate((0, 1, nrb - 1)):
        kb = int(np.clip(Q_ROWS * rb - 4, 0, rows - K_ROWS))
        for i in range(Q_ROWS):
            r0 = int(np.clip(Q_ROWS * rb + i - WIN_R // 2, 0, rows - WIN_R))
            for j in range(K_ROWS):
                valid[t, i, 0, j, 0] = r0 <= kb + j < r0 + WIN_R
    full = np.broadcast_to(valid, (3, Q_ROWS, GRID_W, K_ROWS, GRID_W)).reshape(3, TQ, TK)
    return jnp.asarray(np.where(full, 0.0, NEG_INF).astype(np.float32))


def _bias_build(pairs, row_mask):
    def body(p_ref, m_ref, o_ref):
        t = pl.program_id(0)
        base = WIN_R - Q_ROWS * t
        for i in range(Q_ROWS):
            for mm in range(K_ROWS // 2):
                p = jnp.clip(base + 2 * mm - i, 0, 2 * WIN_R - 1)
                rs, cs = slice(GRID_W * i, GRID_W * (i + 1)), slice(128 * mm, 128 * (mm + 1))
                o_ref[0, 0, rs, cs] = p_ref[0, p] + m_ref[0, rs, cs]

    return pl.pallas_call(
        body, name="bias_build", grid=(3, N_HEADS),
        in_specs=[pl.BlockSpec((1, 2 * WIN_R, GRID_W, 128), lambda t, h: (h, 0, 0, 0)),
                  pl.BlockSpec((1, TQ, TK), lambda t, h: (t, 0, 0))],
        out_specs=pl.BlockSpec((1, 1, TQ, TK), lambda t, h: (t, h, 0, 0)),
        out_shape=jax.ShapeDtypeStruct((3, N_HEADS, TQ, TK), F32),
        compiler_params=_params(("arbitrary", "arbitrary")),
    )(pairs, row_mask)


def kernel(x, c, ctx, c_ctx, w_ada, b_ada, norm_g, w_in, sgu_norm_g, w_spatial, b_spatial, q_norm_g, k_norm_g, rpb, w_out, loss_target, m_c_ctx, m_w_ada, m_b_ada, m_norm_g, m_w_in, m_sgu_norm_g, m_w_spatial, m_b_spatial, m_q_norm_g, m_k_norm_g, m_rpb, m_w_out, v_c_ctx, v_w_ada, v_b_ada, v_norm_g, v_w_in, v_sgu_norm_g, v_w_spatial, v_b_spatial, v_q_norm_g, v_k_norm_g, v_rpb, v_w_out):
    me = 4 * lax.axis_index("x") + 2 * lax.axis_index("y") + lax.axis_index("c")
    x2, ctx2, tgt2 = x[0], ctx[0], loss_target[0]
    T, C = x2.shape[0], ctx2.shape[0]
    rows = T // GRID_W
    wada, win, wout = w_ada[0], w_in[0], w_out[0]
    ada_w = wada.shape[1]
    win_w = win.shape[1]

    row8 = lax.broadcasted_iota(jnp.int32, (8, D_MODEL), 0)
    c_blk = jnp.where(row8 == me, jnp.broadcast_to(c, (8, D_MODEL)), 0.0)
    c_stack, win_g, wout_g = _allgather([c_blk, win.T.astype(BF16), wout.astype(BF16)], "gather_weights")
    w_in_b = win_g.reshape(D_IN, D_MODEL)
    w_out_b = wout_g.reshape(D_MODEL, D_MODEL)
    b_sh = lax.dynamic_slice(b_ada, (0, me * ada_w), (1, ada_w))
    c_ctx_row = c_ctx.reshape(1, D_MODEL)
    s16, part = _ada_fwd(c_stack, c_ctx_row, wada, b_sh)
    (part_g,) = _allgather([part], "gather_mod")
    mod16 = part_g.transpose(1, 0, 2).reshape(16, 3 * D_MODEL)
    mod = lax.dynamic_slice(mod16, (me, 0), (1, 3 * D_MODEL))
    shift, scale, gate = mod[:, :D_MODEL], mod[:, D_MODEL:2 * D_MODEL], mod[:, 2 * D_MODEL:]
    cshift, cscale = mod16[8:9, :D_MODEL], mod16[8:9, D_MODEL:2 * D_MODEL]

    ones512 = _block_ones(512, HEAD_DIM)
    ones8 = jnp.ones((8, 128), BF16)
    foldm = jnp.asarray((np.arange(512)[:, None] % HEAD_DIM == np.arange(128)[None, :]).astype(np.float32))
    lane_half = np.arange(128)[None, :, None] // GRID_W
    hsel = jnp.asarray((2 * np.arange(8)[:, None, None] + lane_half == np.arange(128)[None, None, :]).astype(np.float32),
                       BF16)
    foldr = jnp.asarray((np.arange(256)[None, :] // 8 == np.arange(32)[:, None]).astype(np.float32), BF16)
    gq512 = jnp.tile(q_norm_g, (1, N_HEADS))
    gk512 = jnp.tile(k_norm_g, (1, N_HEADS))
    ws = w_spatial[0]
    wst = ws.transpose(0, 2, 1)
    bsb = jnp.broadcast_to(b_spatial[0][:, :, None], (SGU_GROUPS, CHUNK, 128))
    bias = _bias_build(_rpb_pairs(rpb[0]), _row_mask(rows))

    h, z, qs, kn = _inproj_fwd(x2, norm_g, scale, shift, w_in_b, ones512, gq512, gk512, 512)
    hc, zc, ckn = _ctx_fwd(ctx2, norm_g, cscale, cshift, w_in_b, ones512, gk512)
    out_a = _sgu_fwd(z, sgu_norm_g, ws, bsb, 512)
    ob, out_b, lse = _attn_fwd(qs, kn, z, ckn, zc, bias)

    dy, dmc, dw_out, dgate, loss_part = _outproj_loss_bwd(x2, tgt2, out_a, out_b, gate, w_out_b, 512)
    dqs, dk, dv, dck, dcv, db_g, dacc = _attn_bwd(qs, kn, z, ckn, zc, bias, ob, lse, dmc)
    drpb = _rpb_grad(dacc, hsel, foldr)
    db_q, db_k, db_v, dgq, dgk = _qk_bwd(z, 3, 4, dqs, dk, dv, ones512, gq512, gk512, foldm, 512, "qk_bwd")
    dzc_k, dzc_v, dgk_c = _qk_bwd(zc, None, 0, None, dck, dcv, ones512, None, gk512, foldm, C, "ctx_k_bwd")
    da_u, da_v, da_g, dws, dbs, dsg = _sgu_bwd(z, dmc, sgu_norm_g, ws, wst, bsb, ones8, 512)
    dzs = [da_u, da_v, da_g, db_q, db_k, db_v, db_g]
    grad_x, dshift, dscale, dng = _inproj_bwd_dx(dzs, 0, w_in_b, x2, dy, norm_g, scale, shift, 512, "inproj_bwd_dx")
    dcshift, dcscale, dng_c = _inproj_bwd_dx([dzc_k, dzc_v], 4, w_in_b, ctx2, None, norm_g, cscale, cshift, C,
                                             "ctx_bwd_dx")
    dw_in_b = _inproj_bwd_dw(h, dzs, hc, dzc_k, dzc_v, 512)

    dw_in_blocks = dw_in_b.reshape(D_MODEL, N_DEV, win_w).transpose(1, 0, 2)
    dw_out_blocks = dw_out.astype(BF16).reshape(N_DEV, D_MODEL // N_DEV, D_MODEL)
    gin_parts, gout_parts = _reduce_scatter([dw_in_blocks, dw_out_blocks], "scatter_grads")
    res_in = _adamw_sharded(win, gin_parts, m_w_in[0], v_w_in[0], 256, "adamw_w_in")
    res_out = _adamw_sharded(wout, gout_parts, m_w_out[0], v_w_out[0], 128, "adamw_w_out")

    zero_row = jnp.zeros((1, D_MODEL), F32)
    vec_rows = [(V_DSHIFT, dshift), (V_DSCALE, dscale), (V_DGATE, dgate), (V_DCSHIFT, dcshift),
                (V_DCSCALE, dcscale), (V_ZERO, zero_row), (V_DNG, dng), (V_DNG_CTX, dng_c)]
    vloc, mloc = _pack_small_grads(vec_rows, dsg, dgq, dgk, dgk_c, loss_part, dws, dbs, drpb)
    vg, mg = _allgather([vloc, mloc], "gather_small")
    small_w = (b_ada, norm_g, sgu_norm_g, w_spatial, b_spatial, q_norm_g, k_norm_g, rpb)
    small_m = (m_b_ada, m_norm_g, m_sgu_norm_g, m_w_spatial, m_b_spatial, m_q_norm_g, m_k_norm_g, m_rpb)
    small_v = (v_b_ada, v_norm_g, v_sgu_norm_g, v_w_spatial, v_b_spatial, v_q_norm_g, v_k_norm_g, v_rpb)
    res_small, loss_row = _adamw_small(vg, mg, small_w, small_m, small_v)

    dm_all = vg[:, V_DSHIFT:V_DGATE + 1, :].reshape(N_DEV, 3 * D_MODEL)
    dc_all = vg[:, V_DCSHIFT:V_ZERO + 1, :].reshape(N_DEV, 3 * D_MODEL)
    dm_sh = lax.dynamic_slice(dm_all, (0, me * ada_w), (N_DEV, ada_w))
    dc_sh = lax.dynamic_slice(dc_all, (0, me * ada_w), (N_DEV, ada_w))
    *res_ada, pc = _ada_bwd(s16, dm_sh, dc_sh, wada, m_w_ada[0], v_w_ada[0])
    (pc_g,) = _allgather([pc], "gather_cctx")
    res_cctx = _adamw_cctx(pc_g, c_ctx_row, m_c_ctx.reshape(1, D_MODEL), v_c_ctx.reshape(1, D_MODEL))

    loss = loss_row[0, 0]
    outs = [loss, grad_x[None]]
    for kind in range(4):
        by_name = dict(zip(SMALL_NAMES, res_small[kind]))
        by_name.update(c_ctx=res_cctx[kind].reshape(D_MODEL), w_ada=res_ada[kind][None],
                       w_in=res_in[kind][None], w_out=res_out[kind][None])
        outs += [by_name[nme] for nme in ("c_ctx", "w_ada", "b_ada", "norm_g", "w_in", "sgu_norm_g", "w_spatial",
                                          "b_spatial", "q_norm_g", "k_norm_g", "rpb", "w_out")]
    return tuple(outs)
```

```python
import functools

import numpy as np
import jax
import jax.numpy as jnp
from jax import lax
from jax.experimental import pallas as pl
from jax.experimental.pallas import tpu as pltpu

F32 = jnp.float32
BF16 = jnp.bfloat16
HI = lax.Precision.HIGHEST

N_DEV = 8
D_MODEL = 1024
D_A = 512
D_B = 512
D_IN = 3584
N_BRANCH = 7
HEAD_DIM = 64
N_HEADS = 8
GRID_W = 64
WIN_R = 8
WIN_C = 16
CHUNK = 128
SGU_GROUPS = 4
EPS = 1e-6
NEG_INF = -1e30
Q_ROWS = 4
K_ROWS = 12
TQ = Q_ROWS * GRID_W
TK = K_ROWS * GRID_W
N_DIAG = 22
ATT_SUB = 4
ATT_SCALE = HEAD_DIM ** -0.5
LOG2E = 1.4426950408889634
LN2 = 0.6931471805599453

ADAM_LR = 0.001
ADAM_B1 = 0.9
ADAM_B2 = 0.999
ADAM_EPS = 1e-08
ADAM_WD = 0.01
ADAM_STEP = 10

VMEM_LIMIT = 56 * 1024 * 1024
MESH = pl.DeviceIdType.MESH

V_DSHIFT, V_DSCALE, V_DGATE, V_DCSHIFT, V_DCSCALE, V_ZERO, V_DNG, V_DNG_CTX, V_MISC = range(9)
M_DBS, M_RPB, M_ROWS = 512, 544, 800


def _params(sem=None):
    return pltpu.CompilerParams(dimension_semantics=sem, vmem_limit_bytes=VMEM_LIMIT)


def _sigmoid(x):
    return 1.0 / (1.0 + jnp.exp(-x))


def _gelu_parts(x):
    cdf = 0.5 * (1.0 + lax.erf(x * 0.7071067811865476))
    pdf = jnp.exp(-0.5 * x * x) * 0.3989422804014327
    return x * cdf, cdf + x * pdf


def _nt(a, b):
    return lax.dot_general(a, b, (((1,), (1,)), ((), ())), preferred_element_type=F32)


def _tn(a, b):
    return lax.dot_general(a, b, (((0,), (0,)), ((), ())), preferred_element_type=F32)


def _dot2(v, ones_bf):
    hi = v.astype(BF16)
    lo = (v - hi.astype(F32)).astype(BF16)
    return (jnp.dot(hi, ones_bf, preferred_element_type=F32)
            + jnp.dot(lo, ones_bf, preferred_element_type=F32))


def _adam(w, g, m, v):
    m2 = ADAM_B1 * m + (1.0 - ADAM_B1) * g
    v2 = ADAM_B2 * v + (1.0 - ADAM_B2) * (g * g)
    m_hat = m2 / (1.0 - ADAM_B1 ** ADAM_STEP)
    v_hat = v2 / (1.0 - ADAM_B2 ** ADAM_STEP)
    delta = -ADAM_LR * (m_hat / (jnp.sqrt(v_hat) + ADAM_EPS) + ADAM_WD * w)
    return delta, m2, v2


def _allgather(arrs, name):
    n = len(arrs)
    outs_shape = [jax.ShapeDtypeStruct((N_DEV,) + tuple(a.shape), a.dtype) for a in arrs]

    def body(*refs):
        ins, outs = refs[:n], refs[n:2 * n]
        send_sems, recv_sems, loc_sems = refs[2 * n:]
        x, y, c = lax.axis_index("x"), lax.axis_index("y"), lax.axis_index("c")
        me, sib = (x, y, c), (x, y, 1 - c)
        chips = [(1 - x, y), (x, 1 - y), (1 - x, 1 - y)]

        def slot(px, py, pc):
            return 4 * px + 2 * py + pc

        def copy(a, k, block, to, src=None):
            return pltpu.make_async_remote_copy(
                src_ref=outs[a].at[slot(*block)] if src is None else src, dst_ref=outs[a].at[slot(*block)],
                send_sem=send_sems.at[7 * a + k], recv_sem=recv_sems.at[7 * a + k],
                device_id=to, device_id_type=MESH)

        started, locs = [], []
        for a in range(n):
            lc = pltpu.make_async_copy(ins[a], outs[a].at[slot(*me)], loc_sems.at[a])
            lc.start()
            locs.append(lc)
            for j, chip in enumerate(chips):
                started.append(copy(a, 1 + j, me, (*chip, c), src=ins[a]))
            started.append(copy(a, 0, me, sib, src=ins[a]))
        for cp in started:
            cp.start()
        for j, chip in enumerate(chips):
            for a in range(n):
                copy(a, 1 + j, (*chip, c), me).wait_recv()
                fwd = copy(a, 4 + j, (*chip, c), sib)
                fwd.start()
                started.append(fwd)
        for a in range(n):
            copy(a, 0, sib, me).wait_recv()
            for j, chip in enumerate(chips):
                copy(a, 4 + j, (*chip, 1 - c), me).wait_recv()
        for cp in started:
            cp.wait_send()
        for lc in locs:
            lc.wait()

    res = pl.pallas_call(
        body, name=name, out_shape=outs_shape,
        in_specs=[pl.BlockSpec(memory_space=pl.ANY)] * n,
        out_specs=[pl.BlockSpec(memory_space=pl.ANY)] * n,
        scratch_shapes=[pltpu.SemaphoreType.DMA((7 * n,)), pltpu.SemaphoreType.DMA((7 * n,)),
                        pltpu.SemaphoreType.DMA((n,))],
    )(*arrs)
    return list(res)


def _reduce_scatter(arrs, name):
    n = len(arrs)
    n_chip = N_DEV // 2
    outs_shape = [jax.ShapeDtypeStruct((n_chip,) + tuple(a.shape[1:]), a.dtype) for a in arrs]

    def body(*refs):
        ins, outs = refs[:n], refs[n:2 * n]
        tmps, sums = refs[2 * n:3 * n], refs[3 * n:4 * n]
        s1, r1, s2, r2 = refs[4 * n:]
        x, y, c = lax.axis_index("x"), lax.axis_index("y"), lax.axis_index("c")
        my_chip = 2 * x + y
        sib = (x, y, 1 - c)
        all_chips = [(0, 0), (0, 1), (1, 0), (1, 1)]
        others = [(1 - x, y), (x, 1 - y), (1 - x, 1 - y)]

        swaps = []
        for a in range(n):
            for q, (qx, qy) in enumerate(all_chips):
                cp = pltpu.make_async_remote_copy(
                    src_ref=ins[a].at[4 * qx + 2 * qy + (1 - c)], dst_ref=tmps[a].at[q],
                    send_sem=s1.at[n_chip * a + q], recv_sem=r1.at[n_chip * a + q],
                    device_id=sib, device_id_type=MESH)
                cp.start()
                swaps.append(cp)
        for a in range(n):
            for q, (qx, qy) in enumerate(all_chips):
                swaps[n_chip * a + q].wait_recv()
                both = ins[a][4 * qx + 2 * qy + c].astype(F32) + tmps[a][q].astype(F32)
                sums[a][q] = both.astype(sums[a].dtype)
        sends = []
        for a in range(n):
            for j, (tx, ty) in enumerate(others):
                cp = pltpu.make_async_remote_copy(
                    src_ref=sums[a].at[2 * tx + ty], dst_ref=outs[a].at[my_chip],
                    send_sem=s2.at[3 * a + j], recv_sem=r2.at[3 * a + j],
                    device_id=(tx, ty, c), device_id_type=MESH)
                cp.start()
                sends.append(cp)
            outs[a][my_chip] = sums[a][my_chip]
        for a in range(n):
            for j, (tx, ty) in enumerate(others):
                pltpu.make_async_remote_copy(
                    src_ref=sums[a].at[2 * tx + ty], dst_ref=outs[a].at[2 * tx + ty],
                    send_sem=s2.at[3 * a + j], recv_sem=r2.at[3 * a + j],
                    device_id=(tx, ty, c), device_id_type=MESH).wait_recv()
        for cp in swaps + sends:
            cp.wait_send()

    vm = pl.BlockSpec(memory_space=pltpu.VMEM)
    res = pl.pallas_call(
        body, name=name, out_shape=outs_shape, in_specs=[vm] * n, out_specs=[vm] * n,
        scratch_shapes=([pltpu.VMEM(s.shape, s.dtype) for s in outs_shape] * 2
                        + [pltpu.SemaphoreType.DMA((n_chip * n,)), pltpu.SemaphoreType.DMA((n_chip * n,)),
                           pltpu.SemaphoreType.DMA((3 * n,)), pltpu.SemaphoreType.DMA((3 * n,))]),
        compiler_params=_params(),
    )(*arrs)
    return list(res)


def _ada_fwd(c_stack, c_ctx_row, w_sh, b_sh):
    def body(cs_ref, cc_ref, w_ref, b_ref, s_ref, p_ref):
        c_all = cs_ref[0]
        for d in range(1, N_DEV):
            c_all = c_all + cs_ref[d]
        row = lax.broadcasted_iota(jnp.int32, (8, D_MODEL), 0)
        cc = jnp.where(row == 0, jnp.broadcast_to(cc_ref[...], (8, D_MODEL)), 0.0)
        call = jnp.concatenate([c_all, cc], axis=0)
        s = call * _sigmoid(call)
        s_ref[...] = s
        p_ref[...] = jnp.dot(s, w_ref[...], preferred_element_type=F32, precision=HI) + b_ref[...]

    return pl.pallas_call(
        body, name="ada_fwd",
        out_shape=[jax.ShapeDtypeStruct((16, D_MODEL), F32), jax.ShapeDtypeStruct((16, w_sh.shape[1]), F32)],
        compiler_params=_params(),
    )(c_stack, c_ctx_row, w_sh, b_sh)


def _ada_bwd(s16, dm, dc, w, m, v):
    def body(s_ref, dm_ref, dc_ref, w_ref, m_ref, v_ref, g_ref, d_ref, m2_ref, v2_ref, pc_ref):
        dct = jnp.sum(dc_ref[...], axis=0, keepdims=True)
        row = lax.broadcasted_iota(jnp.int32, dc_ref.shape, 0)
        dcb = jnp.where(row == 0, jnp.broadcast_to(dct, dc_ref.shape), 0.0)
        dm16 = jnp.concatenate([dm_ref[...], dcb], axis=0)
        g = lax.dot_general(s_ref[...], dm16, (((0,), (0,)), ((), ())),
                            preferred_element_type=F32, precision=HI)
        w_ = w_ref[...]
        delta, m2, v2 = _adam(w_, g, m_ref[...], v_ref[...])
        g_ref[...] = g
        d_ref[...] = delta
        m2_ref[...] = m2
        v2_ref[...] = v2
        pc_ref[...] = lax.dot_general(dcb, w_, (((1,), (1,)), ((), ())),
                                      preferred_element_type=F32, precision=HI)

    sh = jax.ShapeDtypeStruct(w.shape, F32)
    return pl.pallas_call(
        body, name="ada_bwd",
        out_shape=[sh, sh, sh, sh, jax.ShapeDtypeStruct((8, D_MODEL), F32)],
        compiler_params=_params(),
    )(s16, dm, dc, w, m, v)


def _head_norm(zk, ones_ref, gain):
    ss = _dot2(zk * zk, ones_ref[...])
    return zk * lax.rsqrt(ss * (1.0 / HEAD_DIM) + EPS) * gain


def _inproj_fwd(x, ng, scale, shift, w_in_t, ones_blk, gq, gk, tm):
    T = x.shape[0]

    def body(x_ref, g_ref, sc_ref, sh_ref, w_ref, ones_ref, gq_ref, gk_ref, h_ref, z_ref, qs_ref, kn_ref):
        xv = x_ref[...]
        r = lax.rsqrt(jnp.mean(xv * xv, axis=-1, keepdims=True) + EPS)
        h = (xv * r * g_ref[...]) * (1.0 + sc_ref[...]) + sh_ref[...]
        hb = h.astype(BF16)
        h_ref[...] = hb
        for k in range(N_BRANCH):
            zk = _nt(hb, w_ref[512 * k:512 * (k + 1), :])
            z_ref[:, 512 * k:512 * (k + 1)] = zk.astype(BF16)
            if k == 3:
                qs_ref[...] = (_head_norm(zk, ones_ref, gq_ref[...]) * (ATT_SCALE * LOG2E)).astype(BF16)
            if k == 4:
                kn_ref[...] = _head_norm(zk, ones_ref, gk_ref[...]).astype(BF16)

    vec = pl.BlockSpec((1, D_MODEL), lambda i: (0, 0))
    v512 = pl.BlockSpec((1, 512), lambda i: (0, 0))
    return pl.pallas_call(
        body, name="inproj_fwd", grid=(T // tm,),
        in_specs=[pl.BlockSpec((tm, D_MODEL), lambda i: (i, 0)), vec, vec, vec,
                  pl.BlockSpec((D_IN, D_MODEL), lambda i: (0, 0)),
                  pl.BlockSpec((512, 512), lambda i: (0, 0)), v512, v512],
        out_specs=[pl.BlockSpec((tm, D_MODEL), lambda i: (i, 0)),
                   pl.BlockSpec((tm, D_IN), lambda i: (i, 0)),
                   pl.BlockSpec((tm, 512), lambda i: (i, 0)),
                   pl.BlockSpec((tm, 512), lambda i: (i, 0))],
        out_shape=[jax.ShapeDtypeStruct((T, D_MODEL), BF16), jax.ShapeDtypeStruct((T, D_IN), BF16),
                   jax.ShapeDtypeStruct((T, 512), BF16), jax.ShapeDtypeStruct((T, 512), BF16)],
        compiler_params=_params(("arbitrary",)),
    )(x, ng, scale, shift, w_in_t, ones_blk, gq, gk)


def _ctx_fwd(ctx, ng, cscale, cshift, w_in_t, ones_blk, gk):
    C = ctx.shape[0]

    def body(x_ref, g_ref, sc_ref, sh_ref, w_ref, ones_ref, gk_ref, h_ref, z_ref, kn_ref):
        xv = x_ref[...]
        r = lax.rsqrt(jnp.mean(xv * xv, axis=-1, keepdims=True) + EPS)
        h = (xv * r * g_ref[...]) * (1.0 + sc_ref[...]) + sh_ref[...]
        hb = h.astype(BF16)
        h_ref[...] = hb
        zk = _nt(hb, w_ref[0:512, :])
        zv = _nt(hb, w_ref[512:1024, :])
        z_ref[:, 0:512] = zk.astype(BF16)
        z_ref[:, 512:1024] = zv.astype(BF16)
        kn_ref[...] = _head_norm(zk, ones_ref, gk_ref[...]).astype(BF16)

    vec = pl.BlockSpec((1, D_MODEL), lambda i: (0, 0))
    return pl.pallas_call(
        body, name="ctx_fwd", grid=(1,),
        in_specs=[pl.BlockSpec((C, D_MODEL), lambda i: (0, 0)), vec, vec, vec,
                  pl.BlockSpec((1024, D_MODEL), lambda i: (2, 0)),
                  pl.BlockSpec((512, 512), lambda i: (0, 0)), pl.BlockSpec((1, 512), lambda i: (0, 0))],
        out_specs=[pl.BlockSpec((C, D_MODEL), lambda i: (0, 0)), pl.BlockSpec((C, 1024), lambda i: (0, 0)),
                   pl.BlockSpec((C, 512), lambda i: (0, 0))],
        out_shape=[jax.ShapeDtypeStruct((C, D_MODEL), BF16), jax.ShapeDtypeStruct((C, 1024), BF16),
                   jax.ShapeDtypeStruct((C, 512), BF16)],
        compiler_params=_params(("arbitrary",)),
    )(ctx, ng, cscale, cshift, w_in_t, ones_blk, gk)


def _sgu_chunk_fwd(au, av, ag, sg, ws_bf, bsb):
    gu, dgu = _gelu_parts(au)
    gv, dgv = _gelu_parts(av)
    rr = lax.rsqrt(jnp.mean(gv * gv, axis=-1, keepdims=True) + EPS)
    vhat = gv * rr
    vn = vhat * sg
    mixed = jnp.dot(ws_bf, vn.astype(BF16), preferred_element_type=F32) + bsb
    sig = _sigmoid(ag)
    sl = ag * sig
    return gu * mixed * sl, (gu, dgu, dgv, rr, vhat, vn, mixed, sig, sl)


def _sgu_fwd(z, sgn, ws, bsb, tm):
    T = z.shape[0]

    def body(au_ref, av_ref, ag_ref, sg_ref, ws_ref, bsb_ref, o_ref):
        for g in range(SGU_GROUPS):
            ws_bf = ws_ref[g].astype(BF16)
            sg = sg_ref[:, 128 * g:128 * (g + 1)]
            bsb_g = bsb_ref[g]
            for j in range(tm // CHUNK):
                rs, cs = slice(CHUNK * j, CHUNK * (j + 1)), slice(128 * g, 128 * (g + 1))
                out, _ = _sgu_chunk_fwd(au_ref[rs, cs].astype(F32), av_ref[rs, cs].astype(F32),
                                        ag_ref[rs, cs].astype(F32), sg, ws_bf, bsb_g)
                o_ref[rs, cs] = out.astype(BF16)

    return pl.pallas_call(
        body, name="sgu_fwd", grid=(T // tm,),
        in_specs=[pl.BlockSpec((tm, 512), lambda i: (i, 0)), pl.BlockSpec((tm, 512), lambda i: (i, 1)),
                  pl.BlockSpec((tm, 512), lambda i: (i, 2)), pl.BlockSpec((1, 512), lambda i: (0, 0)),
                  pl.BlockSpec((SGU_GROUPS, CHUNK, CHUNK), lambda i: (0, 0, 0)),
                  pl.BlockSpec((SGU_GROUPS, CHUNK, 128), lambda i: (0, 0, 0))],
        out_specs=pl.BlockSpec((tm, 512), lambda i: (i, 0)),
        out_shape=jax.ShapeDtypeStruct((T, 512), BF16),
        compiler_params=_params(("arbitrary",)),
    )(z, z, z, sgn, ws, bsb)


def _attn_type(rb, nrb):
    return jnp.where(rb == 0, 0, jnp.where(rb == nrb - 1, 2, 1))


def _attn_specs(T, C, nrb):
    return [
        pl.BlockSpec((ATT_SUB * TQ, 128), lambda hp, st: (st, hp)),
        pl.BlockSpec((T, 128), lambda hp, st: (0, hp)),
        pl.BlockSpec((T, 128), lambda hp, st: (0, 20 + hp)),
        pl.BlockSpec((C, 128), lambda hp, st: (0, hp)),
        pl.BlockSpec((C, 128), lambda hp, st: (0, 4 + hp)),
    ] + [
        pl.BlockSpec((1, 2, TQ, TK), lambda hp, st, sub=sub: (_attn_type(ATT_SUB * st + sub, nrb), hp, 0, 0))
        for sub in range(ATT_SUB)
    ] + [pl.BlockSpec((ATT_SUB * TQ, 128), lambda hp, st: (st, 24 + hp))]


def _attn_fwd(qs, kn, z, ckn, zc, bias):
    T, C = qs.shape[0], ckn.shape[0]
    rows = T // GRID_W
    nrb = rows // Q_ROWS

    def body(q_ref, k_ref, v_ref, ck_ref, cv_ref, *rest):
        bias_refs = rest[:ATT_SUB]
        bg_ref, ob_ref, outb_ref, lse_ref = rest[ATT_SUB:]
        ck2, cv2 = ck_ref[...], cv_ref[...]
        lane = lax.broadcasted_iota(jnp.int32, (1, 128), 1)
        for sub, bias_ref in enumerate(bias_refs):
            rb = ATT_SUB * pl.program_id(1) + sub
            rs = slice(TQ * sub, TQ * (sub + 1))
            ks = pl.multiple_of(jnp.clip(Q_ROWS * rb - 4, 0, rows - K_ROWS) * GRID_W, GRID_W)
            q2 = q_ref[rs, :]
            k2 = k_ref[pl.ds(ks, TK), :]
            v2 = v_ref[pl.ds(ks, TK), :]
            o_acc = jnp.zeros((TQ, 128), F32)
            lse_acc = jnp.zeros((TQ, 128), F32)
            for hh in range(2):
                msk = (lane >= HEAD_DIM) == bool(hh)
                qm = jnp.where(msk, q2, jnp.zeros_like(q2))
                s = _nt(qm, k2) + bias_ref[0, hh]
                sc = _nt(qm, ck2)
                m = jnp.maximum(jnp.max(s, axis=-1, keepdims=True), jnp.max(sc, axis=-1, keepdims=True))
                p = jnp.exp2(s - m)
                pc = jnp.exp2(sc - m)
                va = jnp.where(msk, v2, jnp.ones_like(v2))
                cva = jnp.where(msk, cv2, jnp.ones_like(cv2))
                num = (jnp.dot(p.astype(BF16), va, preferred_element_type=F32)
                       + jnp.dot(pc.astype(BF16), cva, preferred_element_type=F32))
                den = pltpu.roll(num, HEAD_DIM, 1)
                o_acc = jnp.where(msk, num / den, o_acc)
                lse_acc = jnp.where(msk, m + jnp.log(den) * LOG2E, lse_acc)
            ob_ref[rs, :] = o_acc.astype(BF16)
            lse_ref[rs, :] = lse_acc
            bg = bg_ref[rs, :].astype(F32)
            outb_ref[rs, :] = (o_acc * (bg * _sigmoid(bg))).astype(BF16)

    tile = pl.BlockSpec((ATT_SUB * TQ, 128), lambda hp, st: (st, hp))
    return pl.pallas_call(
        body, name="attn_fwd", grid=(4, nrb // ATT_SUB),
        in_specs=_attn_specs(T, C, nrb),
        out_specs=[tile, tile, tile],
        out_shape=[jax.ShapeDtypeStruct((T, 512), BF16), jax.ShapeDtypeStruct((T, 512), BF16),
                   jax.ShapeDtypeStruct((T, 512), F32)],
        compiler_params=_params(("arbitrary", "arbitrary")),
    )(qs, kn, z, ckn, zc, *([bias] * ATT_SUB), z)


def _outproj_loss_bwd(x, tgt, out_a, out_b, gate, w_out, tm):
    T = x.shape[0]
    nt = T // tm

    def body(x_ref, t_ref, oa_ref, ob_ref, gate_ref, w_ref, dy_ref, dmc_ref, dw_ref, dgate_ref, loss_ref, acc):
        @pl.when(pl.program_id(0) == 0)
        def _():
            acc[...] = jnp.zeros_like(acc)
            dgate_ref[...] = jnp.zeros_like(dgate_ref)
            loss_ref[...] = jnp.zeros_like(loss_ref)

        oa, ob = oa_ref[...], ob_ref[...]
        gate_v = gate_ref[...]
        mix = (jnp.dot(oa, w_ref[0:512, :], preferred_element_type=F32)
               + jnp.dot(ob, w_ref[512:1024, :], preferred_element_type=F32))
        e = x_ref[...] + gate_v * mix - t_ref[...]
        se = jnp.sum(jnp.sum(e * e, axis=0, keepdims=True), axis=1, keepdims=True)
        loss_ref[...] += jnp.broadcast_to(se * (0.5 / D_MODEL), loss_ref.shape)
        dy = e * (1.0 / D_MODEL)
        dy_ref[...] = dy
        dgate_ref[...] += jnp.sum(dy * mix, axis=0, keepdims=True)
        dmix = (dy * gate_v).astype(BF16)
        dmc_ref[...] = _nt(dmix, w_ref[...]).astype(BF16)
        acc[0:512, :] += _tn(oa, dmix)
        acc[512:1024, :] += _tn(ob, dmix)

        @pl.when(pl.program_id(0) == nt - 1)
        def _():
            dw_ref[...] = acc[...].astype(BF16)

    row = lambda w: pl.BlockSpec((tm, w), lambda i: (i, 0))
    return pl.pallas_call(
        body, name="outproj_loss_bwd", grid=(nt,),
        in_specs=[row(D_MODEL), row(D_MODEL), row(512), row(512),
                  pl.BlockSpec((1, D_MODEL), lambda i: (0, 0)),
                  pl.BlockSpec((D_MODEL, D_MODEL), lambda i: (0, 0))],
        out_specs=[row(D_MODEL), row(D_MODEL), pl.BlockSpec((D_MODEL, D_MODEL), lambda i: (0, 0)),
                   pl.BlockSpec((1, D_MODEL), lambda i: (0, 0)), pl.BlockSpec((1, 128), lambda i: (0, 0))],
        out_shape=[jax.ShapeDtypeStruct((T, D_MODEL), F32), jax.ShapeDtypeStruct((T, D_MODEL), BF16),
                   jax.ShapeDtypeStruct((D_MODEL, D_MODEL), BF16), jax.ShapeDtypeStruct((1, D_MODEL), F32),
                   jax.ShapeDtypeStruct((1, 128), F32)],
        scratch_shapes=[pltpu.VMEM((D_MODEL, D_MODEL), F32)],
        compiler_params=_params(("arbitrary",)),
    )(x, tgt, out_a, out_b, gate, w_out)


def _attn_bwd(qs, kn, z, ckn, zc, bias, ob, lse, dmc):
    T, C = qs.shape[0], ckn.shape[0]
    rows = T // GRID_W
    nrb = rows // Q_ROWS

    def body(q_ref, k_ref, v_ref, ck_ref, cv_ref, *rest):
        bias_refs = rest[:ATT_SUB]
        (bg_ref, ob_ref, lse_ref, do_ref,
         dq_ref, dk_ref, dv_ref, dck_ref, dcv_ref, dbg_ref, dacc_ref) = rest[ATT_SUB:]

        @pl.when(pl.program_id(1) == 0)
        def _():
            dk_ref[...] = jnp.zeros_like(dk_ref)
            dv_ref[...] = jnp.zeros_like(dv_ref)
            dck_ref[...] = jnp.zeros_like(dck_ref)
            dcv_ref[...] = jnp.zeros_like(dcv_ref)
            dacc_ref[...] = jnp.zeros_like(dacc_ref)

        ck2, cv2 = ck_ref[...], cv_ref[...]
        lane = lax.broadcasted_iota(jnp.int32, (1, 128), 1)
        for sub, bias_ref in enumerate(bias_refs):
            rb = ATT_SUB * pl.program_id(1) + sub
            rs = slice(TQ * sub, TQ * (sub + 1))
            kb = jnp.clip(Q_ROWS * rb - 4, 0, rows - K_ROWS)
            ks = pl.multiple_of(kb * GRID_W, GRID_W)
            ebase = kb - Q_ROWS * rb + 11
            q2 = q_ref[rs, :]
            k2 = k_ref[pl.ds(ks, TK), :]
            v2 = v_ref[pl.ds(ks, TK), :]
            bg = bg_ref[rs, :].astype(F32)
            sig = _sigmoid(bg)
            obv = ob_ref[rs, :].astype(F32)
            dout = do_ref[rs, :].astype(F32)
            dbg_ref[rs, :] = (dout * obv * (sig * (1.0 + bg * (1.0 - sig)))).astype(BF16)
            d_o = dout * (bg * sig)
            d_oo = d_o * obv
            lse2 = lse_ref[rs, :]
            dq_acc = jnp.zeros((TQ, 128), F32)
            for hh in range(2):
                msk = (lane >= HEAD_DIM) == bool(hh)
                qm = jnp.where(msk, q2, jnp.zeros_like(q2))
                lse_h = jnp.max(jnp.where(msk, lse2, -jnp.inf), axis=-1, keepdims=True)
                p = jnp.exp2(_nt(qm, k2) + bias_ref[0, hh] - lse_h)
                pc = jnp.exp2(_nt(qm, ck2) - lse_h)
                dom_f = jnp.where(msk, d_o, 0.0)
                dom = dom_f.astype(BF16)
                delta = jnp.sum(jnp.where(msk, d_oo, 0.0), axis=-1, keepdims=True)
                d_hi = delta.astype(BF16).astype(F32)
                x0 = HEAD_DIM * (1 - hh)
                dom_aug = jnp.where(lane == x0, -d_hi, jnp.where(lane == x0 + 1, d_hi - delta, dom_f)).astype(BF16)
                extra = jnp.logical_or(lane == x0, lane == x0 + 1)
                va = jnp.where(msk, v2, jnp.where(extra, jnp.ones_like(v2), jnp.zeros_like(v2)))
                cva = jnp.where(msk, cv2, jnp.where(extra, jnp.ones_like(cv2), jnp.zeros_like(cv2)))
                ds = p * _nt(dom_aug, va)
                dsc = pc * _nt(dom_aug, cva)
                dsb, dscb = ds.astype(BF16), dsc.astype(BF16)
                dq_h = (jnp.dot(dsb, k2, preferred_element_type=F32)
                        + jnp.dot(dscb, ck2, preferred_element_type=F32))
                dq_acc = jnp.where(msk, dq_h, dq_acc)
                dk_ref[pl.ds(ks, TK), :] += _tn(dsb, qm)
                dv_ref[pl.ds(ks, TK), :] += _tn(p.astype(BF16), dom)
                dck_ref[...] += _tn(dscb, qm)
                dcv_ref[...] += _tn(pc.astype(BF16), dom)
                for i in range(Q_ROWS):
                    for mm in range(K_ROWS // 2):
                        dacc_ref[hh, ebase + (2 * mm - i)] += ds[GRID_W * i:GRID_W * (i + 1),
                                                                 128 * mm:128 * (mm + 1)]
            dq_ref[rs, :] = dq_acc

    tile = pl.BlockSpec((ATT_SUB * TQ, 128), lambda hp, st: (st, hp))
    colT = pl.BlockSpec((T, 128), lambda hp, st: (0, hp))
    colC = pl.BlockSpec((C, 128), lambda hp, st: (0, hp))
    return pl.pallas_call(
        body, name="attn_bwd", grid=(4, nrb // ATT_SUB),
        in_specs=_attn_specs(T, C, nrb) + [tile, tile,
                                           pl.BlockSpec((ATT_SUB * TQ, 128), lambda hp, st: (st, 4 + hp))],
        out_specs=[tile, colT, colT, colC, colC, tile,
                   pl.BlockSpec((2, N_DIAG, GRID_W, 128), lambda hp, st: (hp, 0, 0, 0))],
        out_shape=[jax.ShapeDtypeStruct((T, 512), F32), jax.ShapeDtypeStruct((T, 512), F32),
                   jax.ShapeDtypeStruct((T, 512), F32), jax.ShapeDtypeStruct((C, 512), F32),
                   jax.ShapeDtypeStruct((C, 512), F32), jax.ShapeDtypeStruct((T, 512), BF16),
                   jax.ShapeDtypeStruct((N_HEADS, N_DIAG, GRID_W, 128), F32)],
        compiler_params=_params(("arbitrary", "arbitrary")),
    )(qs, kn, z, ckn, zc, *([bias] * ATT_SUB), z, ob, lse, dmc)


def _rpb_grad(dacc, hsel, fold):
    n_off = 2 * WIN_C - 1
    n_dr = 2 * WIN_R - 1

    def body(a_ref, hsel_ref, fold_ref, o_ref):
        qc = lax.broadcasted_iota(jnp.int32, (GRID_W, 128), 0)
        lane = lax.broadcasted_iota(jnp.int32, (GRID_W, 128), 1)
        diff = lane % GRID_W - qc + (WIN_C - 1)
        left = lane < GRID_W

        def by_dr(dr):
            return a_ref[0, dr + 4] + pltpu.roll(a_ref[0, dr + 3], GRID_W, 1)

        out = jnp.zeros((32, 128), F32)
        for j in range((n_dr + 1) // 2):
            hi = pltpu.roll(by_dr(2 * j + 1), GRID_W, 1) if 2 * j + 1 < n_dr else 0.0
            pair = jnp.where(left, by_dr(2 * j), hi)
            parts = []
            for o in range(n_off):
                mv = jnp.where(diff == o, pair, 0.0)
                acc = mv[0:8]
                for r8 in range(1, GRID_W // 8):
                    acc = acc + mv[8 * r8:8 * (r8 + 1)]
                parts.append(acc)
            parts.append(jnp.zeros((8, 128), F32))
            stack = jnp.concatenate(parts, axis=0)
            s_hi = stack.astype(BF16)
            s_lo = (stack - s_hi.astype(F32)).astype(BF16)
            per_o = (jnp.dot(fold_ref[...], s_hi, preferred_element_type=F32)
                     + jnp.dot(fold_ref[...], s_lo, preferred_element_type=F32))
            out = out + _dot2(per_o, hsel_ref[j])
        o_ref[0] = out

    return pl.pallas_call(
        body, name="rpb_grad", grid=(N_HEADS,),
        in_specs=[pl.BlockSpec((1, N_DIAG, GRID_W, 128), lambda h: (h, 0, 0, 0)),
                  pl.BlockSpec((8, 128, 128), lambda h: (0, 0, 0)), pl.BlockSpec((32, 256), lambda h: (0, 0))],
        out_specs=pl.BlockSpec((1, 32, 128), lambda h: (h, 0, 0)),
        out_shape=jax.ShapeDtypeStruct((N_HEADS, 32, 128), F32),
        compiler_params=_params(("arbitrary",)),
    )(dacc, hsel, fold)


def _qk_bwd(z, col_q, col_k, dqs, dk, dv, ones_blk, gq, gk, foldm, tm, name):
    T = dk.shape[0]
    with_q = dqs is not None

    def norm_bwd(raw, dn, gain, ones_ref):
        rr = lax.rsqrt(_dot2(raw * raw, ones_ref[...]) * (1.0 / HEAD_DIM) + EPS)
        hat = raw * rr
        dgain = jnp.sum(dn * hat, axis=0, keepdims=True)
        dhat = dn * gain
        mean = _dot2(dhat * hat, ones_ref[...]) * (1.0 / HEAD_DIM)
        return rr * (dhat - hat * mean), dgain

    def body(*refs):
        if with_q:
            (bq_ref, bk_ref, dq_ref, dk_ref, dv_ref, ones_ref, gq_ref, gk_ref, fold_ref,
             dbq_ref, dbk_ref, dbv_ref, dgq_ref, dgk_ref, accq, acck) = refs
        else:
            (bk_ref, dk_ref, dv_ref, ones_ref, gk_ref, fold_ref, dbk_ref, dbv_ref, dgk_ref, acck) = refs
        i = pl.program_id(0)

        @pl.when(i == 0)
        def _():
            acck[...] = jnp.zeros_like(acck)
            if with_q:
                accq[...] = jnp.zeros_like(accq)

        dbk, dgk = norm_bwd(bk_ref[...].astype(F32), dk_ref[...] * LN2, gk_ref[...], ones_ref)
        dbk_ref[...] = dbk.astype(BF16)
        acck[...] += dgk
        dbv_ref[...] = dv_ref[...].astype(BF16)
        if with_q:
            dbq, dgq = norm_bwd(bq_ref[...].astype(F32), dq_ref[...] * ATT_SCALE, gq_ref[...], ones_ref)
            dbq_ref[...] = dbq.astype(BF16)
            accq[...] += dgq

        @pl.when(i == pl.num_programs(0) - 1)
        def _():
            dgk_ref[...] = jnp.dot(jnp.broadcast_to(acck[...], (8, 512)), fold_ref[...],
                                   preferred_element_type=F32, precision=HI)
            if with_q:
                dgq_ref[...] = jnp.dot(jnp.broadcast_to(accq[...], (8, 512)), fold_ref[...],
                                       preferred_element_type=F32, precision=HI)

    row = pl.BlockSpec((tm, 512), lambda i: (i, 0))
    cst = lambda a, b: pl.BlockSpec((a, b), lambda i: (0, 0))
    out_row = jax.ShapeDtypeStruct((T, 512), BF16)
    gvec = jax.ShapeDtypeStruct((8, 128), F32)
    if with_q:
        in_specs = [pl.BlockSpec((tm, 512), lambda i: (i, col_q)), pl.BlockSpec((tm, 512), lambda i: (i, col_k)),
                    row, row, row, cst(512, 512), cst(1, 512), cst(1, 512), cst(512, 128)]
        args = (z, z, dqs, dk, dv, ones_blk, gq, gk, foldm)
        out_specs = [row, row, row, cst(8, 128), cst(8, 128)]
        out_shape = [out_row, out_row, out_row, gvec, gvec]
        scratch = [pltpu.VMEM((1, 512), F32), pltpu.VMEM((1, 512), F32)]
    else:
        in_specs = [pl.BlockSpec((tm, 512), lambda i: (i, col_k)), row, row, cst(512, 512), cst(1, 512), cst(512, 128)]
        args = (z, dk, dv, ones_blk, gk, foldm)
        out_specs = [row, row, cst(8, 128)]
        out_shape = [out_row, out_row, gvec]
        scratch = [pltpu.VMEM((1, 512), F32)]
    return pl.pallas_call(
        body, name=name, grid=(T // tm,), in_specs=in_specs, out_specs=out_specs, out_shape=out_shape,
        scratch_shapes=scratch, compiler_params=_params(("arbitrary",)),
    )(*args)


def _sgu_bwd(z, dmc, sgn, ws, wst, bsb, ones8, tm):
    T = z.shape[0]

    def body(au_ref, av_ref, ag_ref, d_ref, sg_ref, ws_ref, wst_ref, bsb_ref, ones_ref,
             dau_ref, dav_ref, dag_ref, dws_ref, dbs_ref, dsg_ref):
        @pl.when(pl.program_id(0) == 0)
        def _():
            dws_ref[...] = jnp.zeros_like(dws_ref)
            dbs_ref[...] = jnp.zeros_like(dbs_ref)
            dsg_ref[...] = jnp.zeros_like(dsg_ref)

        for g in range(SGU_GROUPS):
            ws_bf = ws_ref[g].astype(BF16)
            wst_bf = wst_ref[g].astype(BF16)
            sg = sg_ref[:, 128 * g:128 * (g + 1)]
            bsb_g = bsb_ref[g]
            for j in range(tm // CHUNK):
                rs, cs = slice(CHUNK * j, CHUNK * (j + 1)), slice(128 * g, 128 * (g + 1))
                au, av, ag = (au_ref[rs, cs].astype(F32), av_ref[rs, cs].astype(F32), ag_ref[rs, cs].astype(F32))
                d = d_ref[rs, cs].astype(F32)
                _, (gu, dgu, dgv, rr, vhat, vn, mixed, sig, sl) = _sgu_chunk_fwd(au, av, ag, sg, ws_bf, bsb_g)
                dau_ref[rs, cs] = (d * mixed * sl * dgu).astype(BF16)
                dag_ref[rs, cs] = (d * gu * mixed * (sig * (1.0 + ag * (1.0 - sig)))).astype(BF16)
                dmixed = d * gu * sl
                dmb = dmixed.astype(BF16)
                dm_lo = (dmixed - dmb.astype(F32)).astype(BF16)
                dbs_ref[g] += _nt(ones_ref[...], dmb) + _nt(ones_ref[...], dm_lo)
                dws_ref[g] += _nt(dmb, vn.astype(BF16))
                dvn = jnp.dot(wst_bf, dmb, preferred_element_type=F32)
                dsg_ref[:, 128 * g:128 * (g + 1)] += jnp.sum(dvn * vhat, axis=0, keepdims=True)
                dvhat = dvn * sg
                mean = jnp.mean(dvhat * vhat, axis=-1, keepdims=True)
                dav_ref[rs, cs] = (rr * (dvhat - vhat * mean) * dgv).astype(BF16)

    row = lambda col: pl.BlockSpec((tm, 512), lambda i: (i, col))
    c3 = lambda n: pl.BlockSpec((SGU_GROUPS, CHUNK, n), lambda i: (0, 0, 0))
    out_row = jax.ShapeDtypeStruct((T, 512), BF16)
    return pl.pallas_call(
        body, name="sgu_bwd", grid=(T // tm,),
        in_specs=[row(0), row(1), row(2), row(0), pl.BlockSpec((1, 512), lambda i: (0, 0)),
                  c3(CHUNK), c3(CHUNK), c3(128), pl.BlockSpec((8, 128), lambda i: (0, 0))],
        out_specs=[row(0), row(0), row(0), c3(CHUNK), pl.BlockSpec((SGU_GROUPS, 8, CHUNK), lambda i: (0, 0, 0)),
                   pl.BlockSpec((1, 512), lambda i: (0, 0))],
        out_shape=[out_row, out_row, out_row, jax.ShapeDtypeStruct((SGU_GROUPS, CHUNK, CHUNK), F32),
                   jax.ShapeDtypeStruct((SGU_GROUPS, 8, CHUNK), F32), jax.ShapeDtypeStruct((1, 512), F32)],
        compiler_params=_params(("arbitrary",)),
    )(z, z, z, dmc, sgn, ws, wst, bsb, ones8)


def _inproj_bwd_dx(dzs, w_cols, w_in_t, x, dy, ng, scale, shift, tm, name):
    T = x.shape[0]
    n = len(dzs)
    with_dx = dy is not None

    def body(*refs):
        dz_refs = refs[:n]
        if with_dx:
            w_ref, x_ref, dy_ref, g_ref, sc_ref, sh_ref, gx_ref, dsh_ref, dsc_ref, dg_ref = refs[n:]
        else:
            w_ref, x_ref, g_ref, sc_ref, sh_ref, dsh_ref, dsc_ref, dg_ref = refs[n:]

        @pl.when(pl.program_id(0) == 0)
        def _():
            dsh_ref[...] = jnp.zeros_like(dsh_ref)
            dsc_ref[...] = jnp.zeros_like(dsc_ref)
            dg_ref[...] = jnp.zeros_like(dg_ref)

        dh = jnp.dot(dz_refs[0][...], w_ref[0:512, :], preferred_element_type=F32)
        for k in range(1, n):
            dh = dh + jnp.dot(dz_refs[k][...], w_ref[512 * k:512 * (k + 1), :], preferred_element_type=F32)
        xv = x_ref[...]
        r = lax.rsqrt(jnp.mean(xv * xv, axis=-1, keepdims=True) + EPS)
        xn = xv * r
        gv, op = g_ref[...], 1.0 + sc_ref[...]
        dsh_ref[...] += jnp.sum(dh, axis=0, keepdims=True)
        dsc_ref[...] += jnp.sum(dh * xn * gv, axis=0, keepdims=True)
        dg_ref[...] += jnp.sum(dh * op * xn, axis=0, keepdims=True)
        if with_dx:
            dxn = dh * (gv * op)
            gx_ref[...] = r * (dxn - xn * jnp.mean(dxn * xn, axis=-1, keepdims=True)) + dy_ref[...]

    vec = pl.BlockSpec((1, D_MODEL), lambda i: (0, 0))
    rowf = pl.BlockSpec((tm, D_MODEL), lambda i: (i, 0))
    in_specs = [pl.BlockSpec((tm, 512), lambda i: (i, 0))] * n
    in_specs += [pl.BlockSpec((512 * n, D_MODEL), lambda i: (w_cols // n, 0)), rowf]
    args = list(dzs) + [w_in_t, x]
    vshape = jax.ShapeDtypeStruct((1, D_MODEL), F32)
    out_specs, out_shape = [vec, vec, vec], [vshape, vshape, vshape]
    if with_dx:
        in_specs.append(rowf)
        args.append(dy)
        out_specs = [rowf] + out_specs
        out_shape = [jax.ShapeDtypeStruct((T, D_MODEL), F32)] + out_shape
    in_specs += [vec, vec, vec]
    args += [ng, scale, shift]
    return pl.pallas_call(
        body, name=name, grid=(T // tm,), in_specs=in_specs, out_specs=out_specs, out_shape=out_shape,
        compiler_params=_params(("arbitrary",)),
    )(*args)


def _inproj_bwd_dw(h, dzs, hc, dzc_k, dzc_v, tk):
    T = h.shape[0]
    nt = T // tk
    n = len(dzs)

    def body(*refs):
        h_ref = refs[0]
        dz_refs = refs[1:1 + n]
        hc_ref, dzck_ref, dzcv_ref, o_ref, acc = refs[1 + n:]
        t = pl.program_id(0)

        @pl.when(t == 0)
        def _():
            acc[...] = jnp.zeros_like(acc)
            acc[512 * 4:512 * 5, :] = _tn(dzck_ref[...], hc_ref[...])
            acc[512 * 5:512 * 6, :] = _tn(dzcv_ref[...], hc_ref[...])

        hv = h_ref[...]
        for k in range(n):
            acc[512 * k:512 * (k + 1), :] += _tn(dz_refs[k][...], hv)

        @pl.when(t == nt - 1)
        def _():
            o_ref[...] = acc[...].astype(BF16)

    whole = lambda a: pl.BlockSpec(a.shape, lambda t: (0, 0))
    return pl.pallas_call(
        body, name="inproj_bwd_dw", grid=(nt,),
        in_specs=([pl.BlockSpec((tk, D_MODEL), lambda t: (t, 0))] + [pl.BlockSpec((tk, 512), lambda t: (t, 0))] * n
                  + [whole(hc), whole(dzc_k), whole(dzc_v)]),
        out_specs=pl.BlockSpec((512 * n, D_MODEL), lambda t: (0, 0)),
        out_shape=jax.ShapeDtypeStruct((512 * n, D_MODEL), BF16),
        scratch_shapes=[pltpu.VMEM((512 * n, D_MODEL), F32)],
        compiler_params=_params(("arbitrary",)),
    )(h, *dzs, hc, dzc_k, dzc_v)


def _adamw_sharded(w, gparts, m, v, tr, name):
    R, C = w.shape
    n_part = gparts.shape[0]

    def body(w_ref, gp_ref, m_ref, v_ref, g_ref, d_ref, m2_ref, v2_ref):
        g = gp_ref[0].astype(F32)
        for d in range(1, n_part):
            g = g + gp_ref[d].astype(F32)
        delta, m2, v2 = _adam(w_ref[...], g, m_ref[...], v_ref[...])
        g_ref[...] = g
        d_ref[...] = delta
        m2_ref[...] = m2
        v2_ref[...] = v2

    row = pl.BlockSpec((tr, C), lambda i: (i, 0))
    sh = jax.ShapeDtypeStruct((R, C), F32)
    return pl.pallas_call(
        body, name=name, grid=(R // tr,),
        in_specs=[row, pl.BlockSpec((n_part, tr, C), lambda i: (0, i, 0)), row, row],
        out_specs=[row, row, row, row], out_shape=[sh, sh, sh, sh],
        compiler_params=_params(("arbitrary",)),
    )(w, gparts, m, v)


def _pack_small_grads(vec_rows, dsg, dgq, dgk, dgk_c, loss_part, dws, dbs, drpb):
    n_vec = len(vec_rows)

    def body(*refs):
        vecs = refs[:n_vec]
        dsg_ref, dgq_ref, dgk_ref, dgkc_ref, loss_ref, dws_ref, dbs_ref, drpb_ref, v_ref, m_ref = refs[n_vec:]
        row = lax.broadcasted_iota(jnp.int32, (16, D_MODEL), 0)
        misc = jnp.concatenate([dsg_ref[...], dgq_ref[0:1, :], dgk_ref[0:1, :], dgkc_ref[0:1, :],
                                loss_ref[...]], axis=1)
        v = jnp.where(row == V_MISC, jnp.broadcast_to(misc, (16, D_MODEL)), 0.0)
        for (r, _), ref in zip(vec_rows, vecs):
            v = jnp.where(row == r, jnp.broadcast_to(ref[...], (16, D_MODEL)), v)
        v_ref[...] = v
        for g in range(SGU_GROUPS):
            m_ref[128 * g:128 * (g + 1), :] = dws_ref[g]
            m_ref[M_DBS + 8 * g:M_DBS + 8 * (g + 1), :] = dbs_ref[g]
        for hd in range(N_HEADS):
            m_ref[M_RPB + 32 * hd:M_RPB + 32 * (hd + 1), :] = drpb_ref[hd]

    return pl.pallas_call(
        body, name="pack_small_grads",
        out_shape=[jax.ShapeDtypeStruct((16, D_MODEL), F32), jax.ShapeDtypeStruct((M_ROWS, 128), F32)],
        compiler_params=_params(),
    )(*[a for _, a in vec_rows], dsg, dgq, dgk, dgk_c, loss_part, dws, dbs, drpb)


SMALL_NAMES = ("b_ada", "norm_g", "sgu_norm_g", "w_spatial", "b_spatial", "q_norm_g", "k_norm_g", "rpb")


def _adamw_small(vg, mg, ws, ms, vs):
    k = len(SMALL_NAMES)

    def body(*refs):
        vg_ref, mg_ref = refs[0], refs[1]
        w_refs = dict(zip(SMALL_NAMES, refs[2:2 + k]))
        m_refs = dict(zip(SMALL_NAMES, refs[2 + k:2 + 2 * k]))
        v_refs = dict(zip(SMALL_NAMES, refs[2 + 2 * k:2 + 3 * k]))
        o_refs = [dict(zip(SMALL_NAMES, refs[2 + (3 + i) * k:2 + (4 + i) * k])) for i in range(4)]
        loss_ref = refs[2 + 7 * k]

        sv = vg_ref[0]
        for d in range(1, N_DEV):
            sv = sv + vg_ref[d]
        loss_ref[...] = sv[V_MISC:V_MISC + 1, 896:1024]

        def total(lo, hi):
            s = mg_ref[0, lo:hi, :]
            for d in range(1, N_DEV):
                s = s + mg_ref[d, lo:hi, :]
            return s

        def emit(name, idx, g):
            res = _adam(w_refs[name][idx], g, m_refs[name][idx], v_refs[name][idx])
            for o, val in zip(o_refs, (g,) + res):
                o[name][idx] = val

        everything = (slice(None), slice(None))
        row = lambda r: sv[r:r + 1, :]
        emit("b_ada", everything, jnp.concatenate(
            [row(V_DSHIFT) + row(V_DCSHIFT), row(V_DSCALE) + row(V_DCSCALE), row(V_DGATE)], axis=1))
        emit("norm_g", everything, row(V_DNG) + row(V_DNG_CTX))
        misc = row(V_MISC)
        emit("sgu_norm_g", everything, misc[:, 0:512])
        emit("q_norm_g", everything, misc[:, 512:512 + HEAD_DIM])
        emit("k_norm_g", everything, misc[:, 640:640 + HEAD_DIM] + misc[:, 768:768 + HEAD_DIM])
        for g in range(SGU_GROUPS):
            emit("w_spatial", (0, g), total(128 * g, 128 * (g + 1)))
            emit("b_spatial", (0, slice(g, g + 1), slice(None)), total(M_DBS + 8 * g, M_DBS + 8 * (g + 1))[0:1, :])
        for hd in range(N_HEADS):
            by_dc = total(M_RPB + 32 * hd, M_RPB + 32 * (hd + 1))
            emit("rpb", (0, hd), by_dc.T[0:2 * WIN_R - 1, 0:2 * WIN_C - 1])

    shapes = [jax.ShapeDtypeStruct(w.shape, F32) for w in ws]
    res = pl.pallas_call(body, name="adamw_small", out_shape=shapes * 4 + [jax.ShapeDtypeStruct((1, 128), F32)],
                         compiler_params=_params())(vg, mg, *ws, *ms, *vs)
    return [res[i * k:(i + 1) * k] for i in range(4)], res[4 * k]


def _adamw_cctx(pc_g, w, m, v):
    def body(pc_ref, w_ref, m_ref, v_ref, g_ref, d_ref, m2_ref, v2_ref):
        pc = pc_ref[0, 0:1, :]
        for d in range(1, N_DEV):
            pc = pc + pc_ref[d, 0:1, :]
        cc = w_ref[...]
        sig = _sigmoid(cc)
        g = pc * (sig * (1.0 + cc * (1.0 - sig)))
        delta, m2, v2 = _adam(cc, g, m_ref[...], v_ref[...])
        g_ref[...] = g
        d_ref[...] = delta
        m2_ref[...] = m2
        v2_ref[...] = v2

    sh = jax.ShapeDtypeStruct((1, D_MODEL), F32)
    return pl.pallas_call(body, name="adamw_cctx", out_shape=[sh, sh, sh, sh], compiler_params=_params())(
        pc_g, w, m, v)


def _block_ones(n, blk):
    i = np.arange(n)
    return jnp.asarray((i[:, None] // blk == i[None, :] // blk).astype(np.float32), BF16)


def _rpb_pairs(rpb):
    n_off = 2 * WIN_C - 1
    cols = np.arange(GRID_W)
    c0 = np.clip(cols - WIN_C // 2, 0, GRID_W - WIN_C)
    in_win = (cols[None, :] >= c0[:, None]) & (cols[None, :] < c0[:, None] + WIN_C)
    dc = np.clip(cols[None, :] - cols[:, None] + (WIN_C - 1), 0, n_off - 1)
    expand = (dc[None] == np.arange(n_off)[:, None, None]) & in_win[None]
    toep = jnp.einsum("hrd,dqk->hrqk", rpb, jnp.asarray(expand, F32), precision=HI)
    toep = toep + jnp.asarray(np.where(in_win, 0.0, NEG_INF).astype(np.float32))
    neg = jnp.full((N_HEADS, 1, GRID_W, GRID_W), NEG_INF, F32)
    ext = jnp.concatenate([neg, toep, neg], axis=1)
    return jnp.concatenate([ext[:, :-1], ext[:, 1:]], axis=-1)


def _row_mask(rows):
    nrb = rows // Q_ROWS
    valid = np.zeros((3, Q_ROWS, 1, K_ROWS, 1), bool)
    for t, rb in enumerate((0, 1, nrb - 1)):
        kb = int(np.clip(Q_ROWS * rb - 4, 0, rows - K_ROWS))
        for i in range(Q_ROWS):
            r0 = int(np.clip(Q_ROWS * rb + i - WIN_R // 2, 0, rows - WIN_R))
            for j in range(K_ROWS):
                valid[t, i, 0, j, 0] = r0 <= kb + j < r0 + WIN_R
    full = np.broadcast_to(valid, (3, Q_ROWS, GRID_W, K_ROWS, GRID_W)).reshape(3, TQ, TK)
    return jnp.asarray(np.where(full, 0.0, NEG_INF).astype(np.float32))


def _bias_build(pairs, row_mask):
    def body(p_ref, m_ref, o_ref):
        t = pl.program_id(0)
        base = WIN_R - Q_ROWS * t
        for i in range(Q_ROWS):
            for mm in range(K_ROWS // 2):
                p = jnp.clip(base + 2 * mm - i, 0, 2 * WIN_R - 1)
                rs, cs = slice(GRID_W * i, GRID_W * (i + 1)), slice(128 * mm, 128 * (mm + 1))
                o_ref[0, 0, rs, cs] = (p_ref[0, p] + m_ref[0, rs, cs]) * LOG2E

    return pl.pallas_call(
        body, name="bias_build", grid=(3, N_HEADS),
        in_specs=[pl.BlockSpec((1, 2 * WIN_R, GRID_W, 128), lambda t, h: (h, 0, 0, 0)),
                  pl.BlockSpec((1, TQ, TK), lambda t, h: (t, 0, 0))],
        out_specs=pl.BlockSpec((1, 1, TQ, TK), lambda t, h: (t, h, 0, 0)),
        out_shape=jax.ShapeDtypeStruct((3, N_HEADS, TQ, TK), F32),
        compiler_params=_params(("arbitrary", "arbitrary")),
    )(pairs, row_mask)


def kernel(x, c, ctx, c_ctx, w_ada, b_ada, norm_g, w_in, sgu_norm_g, w_spatial, b_spatial, q_norm_g, k_norm_g, rpb, w_out, loss_target, m_c_ctx, m_w_ada, m_b_ada, m_norm_g, m_w_in, m_sgu_norm_g, m_w_spatial, m_b_spatial, m_q_norm_g, m_k_norm_g, m_rpb, m_w_out, v_c_ctx, v_w_ada, v_b_ada, v_norm_g, v_w_in, v_sgu_norm_g, v_w_spatial, v_b_spatial, v_q_norm_g, v_k_norm_g, v_rpb, v_w_out):
    me = 4 * lax.axis_index("x") + 2 * lax.axis_index("y") + lax.axis_index("c")
    x2, ctx2, tgt2 = x[0], ctx[0], loss_target[0]
    T, C = x2.shape[0], ctx2.shape[0]
    rows = T // GRID_W
    wada, win_t, wout = w_ada[0], w_in[0].T, w_out[0]
    ada_w = wada.shape[1]
    win_w = win_t.shape[0]

    row8 = lax.broadcasted_iota(jnp.int32, (8, D_MODEL), 0)
    c_blk = jnp.where(row8 == me, jnp.broadcast_to(c, (8, D_MODEL)), 0.0)
    c_stack, win_g, wout_g = _allgather([c_blk, win_t.astype(BF16), wout.astype(BF16)], "gather_weights")
    w_in_b = win_g.reshape(D_IN, D_MODEL)
    w_out_b = wout_g.reshape(D_MODEL, D_MODEL)
    b_sh = lax.dynamic_slice(b_ada, (0, me * ada_w), (1, ada_w))
    c_ctx_row = c_ctx.reshape(1, D_MODEL)
    s16, part = _ada_fwd(c_stack, c_ctx_row, wada, b_sh)
    (part_g,) = _allgather([part], "gather_mod")
    mod16 = part_g.transpose(1, 0, 2).reshape(16, 3 * D_MODEL)
    mod = lax.dynamic_slice(mod16, (me, 0), (1, 3 * D_MODEL))
    shift, scale, gate = mod[:, :D_MODEL], mod[:, D_MODEL:2 * D_MODEL], mod[:, 2 * D_MODEL:]
    cshift, cscale = mod16[8:9, :D_MODEL], mod16[8:9, D_MODEL:2 * D_MODEL]

    ones512 = _block_ones(512, HEAD_DIM)
    ones8 = jnp.ones((8, 128), BF16)
    foldm = jnp.asarray((np.arange(512)[:, None] % HEAD_DIM == np.arange(128)[None, :]).astype(np.float32))
    lane_half = np.arange(128)[None, :, None] // GRID_W
    hsel = jnp.asarray((2 * np.arange(8)[:, None, None] + lane_half == np.arange(128)[None, None, :]).astype(np.float32),
                       BF16)
    foldr = jnp.asarray((np.arange(256)[None, :] // 8 == np.arange(32)[:, None]).astype(np.float32), BF16)
    gq512 = jnp.tile(q_norm_g, (1, N_HEADS))
    gk512 = jnp.tile(k_norm_g, (1, N_HEADS))
    ws = w_spatial[0]
    wst = ws.transpose(0, 2, 1)
    bsb = jnp.broadcast_to(b_spatial[0][:, :, None], (SGU_GROUPS, CHUNK, 128))
    bias = _bias_build(_rpb_pairs(rpb[0]), _row_mask(rows))

    h, z, qs, kn = _inproj_fwd(x2, norm_g, scale, shift, w_in_b, ones512, gq512, gk512, 512)
    hc, zc, ckn = _ctx_fwd(ctx2, norm_g, cscale, cshift, w_in_b, ones512, gk512)
    out_a = _sgu_fwd(z, sgu_norm_g, ws, bsb, 512)
    ob, out_b, lse = _attn_fwd(qs, kn, z, ckn, zc, bias)

    dy, dmc, dw_out, dgate, loss_part = _outproj_loss_bwd(x2, tgt2, out_a, out_b, gate, w_out_b, 512)
    dqs, dk, dv, dck, dcv, db_g, dacc = _attn_bwd(qs, kn, z, ckn, zc, bias, ob, lse, dmc)
    drpb = _rpb_grad(dacc, hsel, foldr)
    db_q, db_k, db_v, dgq, dgk = _qk_bwd(z, 3, 4, dqs, dk, dv, ones512, gq512, gk512, foldm, 512, "qk_bwd")
    dzc_k, dzc_v, dgk_c = _qk_bwd(zc, None, 0, None, dck, dcv, ones512, None, gk512, foldm, C, "ctx_k_bwd")
    da_u, da_v, da_g, dws, dbs, dsg = _sgu_bwd(z, dmc, sgu_norm_g, ws, wst, bsb, ones8, 512)
    dzs = [da_u, da_v, da_g, db_q, db_k, db_v, db_g]
    grad_x, dshift, dscale, dng = _inproj_bwd_dx(dzs, 0, w_in_b, x2, dy, norm_g, scale, shift, 512, "inproj_bwd_dx")
    dcshift, dcscale, dng_c = _inproj_bwd_dx([dzc_k, dzc_v], 4, w_in_b, ctx2, None, norm_g, cscale, cshift, C,
                                             "ctx_bwd_dx")
    dw_in_t = _inproj_bwd_dw(h, dzs, hc, dzc_k, dzc_v, 512)

    dw_in_blocks = dw_in_t.reshape(N_DEV, win_w, D_MODEL)
    dw_out_blocks = dw_out.reshape(N_DEV, D_MODEL // N_DEV, D_MODEL)
    gin_parts, gout_parts = _reduce_scatter([dw_in_blocks, dw_out_blocks], "scatter_grads")
    res_in = _adamw_sharded(win_t, gin_parts, m_w_in[0].T, v_w_in[0].T, 112, "adamw_w_in")
    res_out = _adamw_sharded(wout, gout_parts, m_w_out[0], v_w_out[0], 128, "adamw_w_out")

    zero_row = jnp.zeros((1, D_MODEL), F32)
    vec_rows = [(V_DSHIFT, dshift), (V_DSCALE, dscale), (V_DGATE, dgate), (V_DCSHIFT, dcshift),
                (V_DCSCALE, dcscale), (V_ZERO, zero_row), (V_DNG, dng), (V_DNG_CTX, dng_c)]
    vloc, mloc = _pack_small_grads(vec_rows, dsg, dgq, dgk, dgk_c, loss_part, dws, dbs, drpb)
    vg, mg = _allgather([vloc, mloc], "gather_small")
    small_w = (b_ada, norm_g, sgu_norm_g, w_spatial, b_spatial, q_norm_g, k_norm_g, rpb)
    small_m = (m_b_ada, m_norm_g, m_sgu_norm_g, m_w_spatial, m_b_spatial, m_q_norm_g, m_k_norm_g, m_rpb)
    small_v = (v_b_ada, v_norm_g, v_sgu_norm_g, v_w_spatial, v_b_spatial, v_q_norm_g, v_k_norm_g, v_rpb)
    res_small, loss_row = _adamw_small(vg, mg, small_w, small_m, small_v)

    dm_all = vg[:, V_DSHIFT:V_DGATE + 1, :].reshape(N_DEV, 3 * D_MODEL)
    dc_all = vg[:, V_DCSHIFT:V_ZERO + 1, :].reshape(N_DEV, 3 * D_MODEL)
    dm_sh = lax.dynamic_slice(dm_all, (0, me * ada_w), (N_DEV, ada_w))
    dc_sh = lax.dynamic_slice(dc_all, (0, me * ada_w), (N_DEV, ada_w))
    *res_ada, pc = _ada_bwd(s16, dm_sh, dc_sh, wada, m_w_ada[0], v_w_ada[0])
    (pc_g,) = _allgather([pc], "gather_cctx")
    res_cctx = _adamw_cctx(pc_g, c_ctx_row, m_c_ctx.reshape(1, D_MODEL), v_c_ctx.reshape(1, D_MODEL))

    loss = loss_row[0, 0]
    outs = [loss, grad_x[None]]
    for kind in range(4):
        by_name = dict(zip(SMALL_NAMES, res_small[kind]))
        by_name.update(c_ctx=res_cctx[kind].reshape(D_MODEL), w_ada=res_ada[kind][None],
                       w_in=res_in[kind].T[None], w_out=res_out[kind][None])
        outs += [by_name[nme] for nme in ("c_ctx", "w_ada", "b_ada", "norm_g", "w_in", "sgu_norm_g", "w_spatial",
                                          "b_spatial", "q_norm_g", "k_norm_g", "rpb", "w_out")]
    return tuple(outs)
```

```python
import functools

import numpy as np
import jax
import jax.numpy as jnp
from jax import lax
from jax.experimental import pallas as pl
from jax.experimental.pallas import tpu as pltpu

F32 = jnp.float32
BF16 = jnp.bfloat16
HI = lax.Precision.HIGHEST

N_DEV = 8
D_MODEL = 1024
D_A = 512
D_B = 512
D_IN = 3584
N_BRANCH = 7
HEAD_DIM = 64
N_HEADS = 8
GRID_W = 64
WIN_R = 8
WIN_C = 16
CHUNK = 128
SGU_GROUPS = 4
EPS = 1e-6
NEG_INF = -1e30
Q_ROWS = 4
K_ROWS = 12
TQ = Q_ROWS * GRID_W
TK = K_ROWS * GRID_W
N_DIAG = 22
ATT_SUB = 4
ATT_SCALE = HEAD_DIM ** -0.5
LOG2E = 1.4426950408889634
LN2 = 0.6931471805599453

ADAM_LR = 0.001
ADAM_B1 = 0.9
ADAM_B2 = 0.999
ADAM_EPS = 1e-08
ADAM_WD = 0.01
ADAM_STEP = 10

VMEM_LIMIT = 56 * 1024 * 1024
MESH = pl.DeviceIdType.MESH

V_DSHIFT, V_DSCALE, V_DGATE, V_DCSHIFT, V_DCSCALE, V_ZERO, V_DNG, V_DNG_CTX, V_MISC = range(9)
M_DBS, M_RPB, M_ROWS = 512, 544, 800


def _params(sem=None):
    return pltpu.CompilerParams(dimension_semantics=sem, vmem_limit_bytes=VMEM_LIMIT)


def _sigmoid(x):
    return 1.0 / (1.0 + jnp.exp(-x))


def _gelu_parts(x):
    cdf = 0.5 * (1.0 + lax.erf(x * 0.7071067811865476))
    pdf = jnp.exp(-0.5 * x * x) * 0.3989422804014327
    return x * cdf, cdf + x * pdf


def _nt(a, b):
    return lax.dot_general(a, b, (((1,), (1,)), ((), ())), preferred_element_type=F32)


def _tn(a, b):
    return lax.dot_general(a, b, (((0,), (0,)), ((), ())), preferred_element_type=F32)


def _dot2(v, ones_bf):
    hi = v.astype(BF16)
    lo = (v - hi.astype(F32)).astype(BF16)
    return (jnp.dot(hi, ones_bf, preferred_element_type=F32)
            + jnp.dot(lo, ones_bf, preferred_element_type=F32))


def _head_sum(v, ones_ref):
    return jnp.dot(v.astype(BF16), ones_ref[...], preferred_element_type=F32)


def _adam(w, g, m, v):
    m2 = ADAM_B1 * m + (1.0 - ADAM_B1) * g
    v2 = ADAM_B2 * v + (1.0 - ADAM_B2) * (g * g)
    m_hat = m2 / (1.0 - ADAM_B1 ** ADAM_STEP)
    v_hat = v2 / (1.0 - ADAM_B2 ** ADAM_STEP)
    delta = -ADAM_LR * (m_hat / (jnp.sqrt(v_hat) + ADAM_EPS) + ADAM_WD * w)
    return delta, m2, v2


def _allgather(arrs, name):
    n = len(arrs)
    outs_shape = [jax.ShapeDtypeStruct((N_DEV,) + tuple(a.shape), a.dtype) for a in arrs]

    def body(*refs):
        ins, outs = refs[:n], refs[n:2 * n]
        send_sems, recv_sems, loc_sems = refs[2 * n:]
        x, y, c = lax.axis_index("x"), lax.axis_index("y"), lax.axis_index("c")
        me, sib = (x, y, c), (x, y, 1 - c)
        chips = [(1 - x, y), (x, 1 - y), (1 - x, 1 - y)]

        def slot(px, py, pc):
            return 4 * px + 2 * py + pc

        def copy(a, k, block, to, src=None):
            return pltpu.make_async_remote_copy(
                src_ref=outs[a].at[slot(*block)] if src is None else src, dst_ref=outs[a].at[slot(*block)],
                send_sem=send_sems.at[7 * a + k], recv_sem=recv_sems.at[7 * a + k],
                device_id=to, device_id_type=MESH)

        started, locs = [], []
        for a in range(n):
            lc = pltpu.make_async_copy(ins[a], outs[a].at[slot(*me)], loc_sems.at[a])
            lc.start()
            locs.append(lc)
            for j, chip in enumerate(chips):
                started.append(copy(a, 1 + j, me, (*chip, c), src=ins[a]))
            started.append(copy(a, 0, me, sib, src=ins[a]))
        for cp in started:
            cp.start()
        for j, chip in enumerate(chips):
            for a in range(n):
                copy(a, 1 + j, (*chip, c), me).wait_recv()
                fwd = copy(a, 4 + j, (*chip, c), sib)
                fwd.start()
                started.append(fwd)
        for a in range(n):
            copy(a, 0, sib, me).wait_recv()
            for j, chip in enumerate(chips):
                copy(a, 4 + j, (*chip, 1 - c), me).wait_recv()
        for cp in started:
            cp.wait_send()
        for lc in locs:
            lc.wait()

    res = pl.pallas_call(
        body, name=name, out_shape=outs_shape,
        in_specs=[pl.BlockSpec(memory_space=pl.ANY)] * n,
        out_specs=[pl.BlockSpec(memory_space=pl.ANY)] * n,
        scratch_shapes=[pltpu.SemaphoreType.DMA((7 * n,)), pltpu.SemaphoreType.DMA((7 * n,)),
                        pltpu.SemaphoreType.DMA((n,))],
    )(*arrs)
    return list(res)


def _chip_presum(arrs, name):
    n = len(arrs)
    n_chip = N_DEV // 2
    outs_shape = [jax.ShapeDtypeStruct((n_chip,) + tuple(a.shape[1:]), a.dtype) for a in arrs]

    def body(*refs):
        ins, sums = refs[:n], refs[n:2 * n]
        tmps = refs[2 * n:3 * n]
        s1, r1 = refs[3 * n:]
        c = lax.axis_index("c")
        sib = (lax.axis_index("x"), lax.axis_index("y"), 1 - c)
        all_chips = [(0, 0), (0, 1), (1, 0), (1, 1)]

        swaps = []
        for a in range(n):
            for q, (qx, qy) in enumerate(all_chips):
                cp = pltpu.make_async_remote_copy(
                    src_ref=ins[a].at[4 * qx + 2 * qy + (1 - c)], dst_ref=tmps[a].at[q],
                    send_sem=s1.at[n_chip * a + q], recv_sem=r1.at[n_chip * a + q],
                    device_id=sib, device_id_type=MESH)
                cp.start()
                swaps.append(cp)
        for a in range(n):
            for q, (qx, qy) in enumerate(all_chips):
                swaps[n_chip * a + q].wait_recv()
                both = ins[a][4 * qx + 2 * qy + c].astype(F32) + tmps[a][q].astype(F32)
                sums[a][q] = both.astype(sums[a].dtype)
        for cp in swaps:
            cp.wait_send()

    vm = pl.BlockSpec(memory_space=pltpu.VMEM)
    res = pl.pallas_call(
        body, name=name, out_shape=outs_shape, in_specs=[vm] * n, out_specs=[vm] * n,
        scratch_shapes=([pltpu.VMEM(s.shape, s.dtype) for s in outs_shape]
                        + [pltpu.SemaphoreType.DMA((n_chip * n,)), pltpu.SemaphoreType.DMA((n_chip * n,))]),
        compiler_params=_params(),
    )(*arrs)
    return list(res)


class _Hosted:
    def __init__(self, kind, src):
        self.kind = kind
        n_slot = {"a2a": N_DEV, "ag": N_DEV, "chips": N_DEV // 2}[kind]
        blk = src.shape if kind == "ag" else src.shape[1:]
        self.out_shape = jax.ShapeDtypeStruct((n_slot,) + tuple(blk), src.dtype)
        self.n_peer = n_slot - 1
        self.scratch = [pltpu.SemaphoreType.DMA((self.n_peer,)), pltpu.SemaphoreType.DMA((self.n_peer,)),
                        pltpu.SemaphoreType.DMA]

    def _copies(self, src, dst, send_sems, recv_sems, loc_sem, landing):
        x, y, c = lax.axis_index("x"), lax.axis_index("y"), lax.axis_index("c")
        if self.kind == "chips":
            me = 2 * x + y
            peers = [((px, py, c), 2 * px + py) for px, py in ((1 - x, y), (x, 1 - y), (1 - x, 1 - y))]
        else:
            me = 4 * x + 2 * y + c
            peers = []
            for k in range(1, N_DEV):
                px = 1 - x if (k >> 2) & 1 else x
                py = 1 - y if (k >> 1) & 1 else y
                pc = 1 - c if k & 1 else c
                peers.append(((px, py, pc), 4 * px + 2 * py + pc))
        remote = []
        for k, (peer, pid) in enumerate(peers):
            s = src if self.kind == "ag" else src.at[pid]
            remote.append(pltpu.make_async_remote_copy(
                src_ref=s, dst_ref=dst.at[pid if landing else me],
                send_sem=send_sems.at[k], recv_sem=recv_sems.at[k], device_id=peer, device_id_type=MESH))
        local = pltpu.make_async_copy(src if self.kind == "ag" else src.at[me], dst.at[me], loc_sem)
        return remote, local

    def start(self, src, dst, send_sems, recv_sems, loc_sem):
        remote, local = self._copies(src, dst, send_sems, recv_sems, loc_sem, landing=False)
        for cp in remote:
            cp.start()
        local.start()

    def wait(self, src, dst, send_sems, recv_sems, loc_sem):
        remote, local = self._copies(src, dst, send_sems, recv_sems, loc_sem, landing=True)
        for cp in remote:
            cp.wait_recv()
        for cp in remote:
            cp.wait_send()
        local.wait()


ANY_SPEC = pl.BlockSpec(memory_space=pl.ANY)


def _ada_fwd(c_stack, c_ctx_row, w_sh, b_sh):
    def body(cs_ref, cc_ref, w_ref, b_ref, s_ref, p_ref):
        c_all = cs_ref[0]
        for d in range(1, N_DEV):
            c_all = c_all + cs_ref[d]
        row = lax.broadcasted_iota(jnp.int32, (8, D_MODEL), 0)
        cc = jnp.where(row == 0, jnp.broadcast_to(cc_ref[...], (8, D_MODEL)), 0.0)
        call = jnp.concatenate([c_all, cc], axis=0)
        s = call * _sigmoid(call)
        s_ref[...] = s
        p_ref[...] = jnp.dot(s, w_ref[...], preferred_element_type=F32, precision=HI) + b_ref[...]

    return pl.pallas_call(
        body, name="ada_fwd",
        out_shape=[jax.ShapeDtypeStruct((16, D_MODEL), F32), jax.ShapeDtypeStruct((16, w_sh.shape[1]), F32)],
        compiler_params=_params(),
    )(c_stack, c_ctx_row, w_sh, b_sh)


def _ada_bwd(s16, dm, dc, w, m, v):
    def body(s_ref, dm_ref, dc_ref, w_ref, m_ref, v_ref, g_ref, d_ref, m2_ref, v2_ref, pc_ref):
        dct = jnp.sum(dc_ref[...], axis=0, keepdims=True)
        row = lax.broadcasted_iota(jnp.int32, dc_ref.shape, 0)
        dcb = jnp.where(row == 0, jnp.broadcast_to(dct, dc_ref.shape), 0.0)
        dm16 = jnp.concatenate([dm_ref[...], dcb], axis=0)
        g = lax.dot_general(s_ref[...], dm16, (((0,), (0,)), ((), ())),
                            preferred_element_type=F32, precision=HI)
        w_ = w_ref[...]
        delta, m2, v2 = _adam(w_, g, m_ref[...], v_ref[...])
        g_ref[...] = g
        d_ref[...] = delta
        m2_ref[...] = m2
        v2_ref[...] = v2
        pc_ref[...] = lax.dot_general(dcb, w_, (((1,), (1,)), ((), ())),
                                      preferred_element_type=F32, precision=HI)

    sh = jax.ShapeDtypeStruct(w.shape, F32)
    return pl.pallas_call(
        body, name="ada_bwd",
        out_shape=[sh, sh, sh, sh, jax.ShapeDtypeStruct((8, D_MODEL), F32)],
        compiler_params=_params(),
    )(s16, dm, dc, w, m, v)


def _head_norm(zk, ones_ref, gain):
    ss = _head_sum(zk * zk, ones_ref)
    return zk * lax.rsqrt(ss * (1.0 / HEAD_DIM) + EPS) * gain


def _inproj_fwd(x, ng, scale, shift, w_in_t, ones_blk, gq, gk, tm):
    T = x.shape[0]

    def body(x_ref, g_ref, sc_ref, sh_ref, w_ref, ones_ref, gq_ref, gk_ref, h_ref, z_ref, qs_ref, kn_ref):
        xv = x_ref[...]
        r = lax.rsqrt(jnp.mean(xv * xv, axis=-1, keepdims=True) + EPS)
        h = (xv * r * g_ref[...]) * (1.0 + sc_ref[...]) + sh_ref[...]
        hb = h.astype(BF16)
        h_ref[...] = hb
        for k in range(N_BRANCH):
            zk = _nt(hb, w_ref[512 * k:512 * (k + 1), :])
            z_ref[:, 512 * k:512 * (k + 1)] = zk.astype(BF16)
            if k == 3:
                qs_ref[...] = (_head_norm(zk, ones_ref, gq_ref[...]) * (ATT_SCALE * LOG2E)).astype(BF16)
            if k == 4:
                kn_ref[...] = _head_norm(zk, ones_ref, gk_ref[...]).astype(BF16)

    vec = pl.BlockSpec((1, D_MODEL), lambda i: (0, 0))
    v512 = pl.BlockSpec((1, 512), lambda i: (0, 0))
    return pl.pallas_call(
        body, name="inproj_fwd", grid=(T // tm,),
        in_specs=[pl.BlockSpec((tm, D_MODEL), lambda i: (i, 0)), vec, vec, vec,
                  pl.BlockSpec((D_IN, D_MODEL), lambda i: (0, 0)),
                  pl.BlockSpec((512, 512), lambda i: (0, 0)), v512, v512],
        out_specs=[pl.BlockSpec((tm, D_MODEL), lambda i: (i, 0)),
                   pl.BlockSpec((tm, D_IN), lambda i: (i, 0)),
                   pl.BlockSpec((tm, 512), lambda i: (i, 0)),
                   pl.BlockSpec((tm, 512), lambda i: (i, 0))],
        out_shape=[jax.ShapeDtypeStruct((T, D_MODEL), BF16), jax.ShapeDtypeStruct((T, D_IN), BF16),
                   jax.ShapeDtypeStruct((T, 512), BF16), jax.ShapeDtypeStruct((T, 512), BF16)],
        compiler_params=_params(("arbitrary",)),
    )(x, ng, scale, shift, w_in_t, ones_blk, gq, gk)


def _ctx_fwd(ctx, ng, cscale, cshift, w_in_t, ones_blk, gk):
    C = ctx.shape[0]

    def body(x_ref, g_ref, sc_ref, sh_ref, w_ref, ones_ref, gk_ref, h_ref, z_ref, kn_ref):
        xv = x_ref[...]
        r = lax.rsqrt(jnp.mean(xv * xv, axis=-1, keepdims=True) + EPS)
        h = (xv * r * g_ref[...]) * (1.0 + sc_ref[...]) + sh_ref[...]
        hb = h.astype(BF16)
        h_ref[...] = hb
        zk = _nt(hb, w_ref[0:512, :])
        zv = _nt(hb, w_ref[512:1024, :])
        z_ref[:, 0:512] = zk.astype(BF16)
        z_ref[:, 512:1024] = zv.astype(BF16)
        kn_ref[...] = _head_norm(zk, ones_ref, gk_ref[...]).astype(BF16)

    vec = pl.BlockSpec((1, D_MODEL), lambda i: (0, 0))
    return pl.pallas_call(
        body, name="ctx_fwd", grid=(1,),
        in_specs=[pl.BlockSpec((C, D_MODEL), lambda i: (0, 0)), vec, vec, vec,
                  pl.BlockSpec((1024, D_MODEL), lambda i: (2, 0)),
                  pl.BlockSpec((512, 512), lambda i: (0, 0)), pl.BlockSpec((1, 512), lambda i: (0, 0))],
        out_specs=[pl.BlockSpec((C, D_MODEL), lambda i: (0, 0)), pl.BlockSpec((C, 1024), lambda i: (0, 0)),
                   pl.BlockSpec((C, 512), lambda i: (0, 0))],
        out_shape=[jax.ShapeDtypeStruct((C, D_MODEL), BF16), jax.ShapeDtypeStruct((C, 1024), BF16),
                   jax.ShapeDtypeStruct((C, 512), BF16)],
        compiler_params=_params(("arbitrary",)),
    )(ctx, ng, cscale, cshift, w_in_t, ones_blk, gk)


def _sgu_chunk_fwd(au, av, ag, sg, ws_bf, bsb):
    gu, dgu = _gelu_parts(au)
    gv, dgv = _gelu_parts(av)
    rr = lax.rsqrt(jnp.mean(gv * gv, axis=-1, keepdims=True) + EPS)
    vhat = gv * rr
    vn = vhat * sg
    mixed = jnp.dot(ws_bf, vn.astype(BF16), preferred_element_type=F32) + bsb
    sig = _sigmoid(ag)
    sl = ag * sig
    return gu * mixed * sl, (gu, dgu, dgv, rr, vhat, vn, mixed, sig, sl)


def _sgu_fwd(z, sgn, ws, bsb, tm):
    T = z.shape[0]

    def body(au_ref, av_ref, ag_ref, sg_ref, ws_ref, bsb_ref, o_ref):
        for g in range(SGU_GROUPS):
            ws_bf = ws_ref[g].astype(BF16)
            sg = sg_ref[:, 128 * g:128 * (g + 1)]
            bsb_g = bsb_ref[g]
            for j in range(tm // CHUNK):
                rs, cs = slice(CHUNK * j, CHUNK * (j + 1)), slice(128 * g, 128 * (g + 1))
                out, _ = _sgu_chunk_fwd(au_ref[rs, cs].astype(F32), av_ref[rs, cs].astype(F32),
                                        ag_ref[rs, cs].astype(F32), sg, ws_bf, bsb_g)
                o_ref[rs, cs] = out.astype(BF16)

    return pl.pallas_call(
        body, name="sgu_fwd", grid=(T // tm,),
        in_specs=[pl.BlockSpec((tm, 512), lambda i: (i, 0)), pl.BlockSpec((tm, 512), lambda i: (i, 1)),
                  pl.BlockSpec((tm, 512), lambda i: (i, 2)), pl.BlockSpec((1, 512), lambda i: (0, 0)),
                  pl.BlockSpec((SGU_GROUPS, CHUNK, CHUNK), lambda i: (0, 0, 0)),
                  pl.BlockSpec((SGU_GROUPS, CHUNK, 128), lambda i: (0, 0, 0))],
        out_specs=pl.BlockSpec((tm, 512), lambda i: (i, 0)),
        out_shape=jax.ShapeDtypeStruct((T, 512), BF16),
        compiler_params=_params(("arbitrary",)),
    )(z, z, z, sgn, ws, bsb)


def _attn_type(rb, nrb):
    return jnp.where(rb == 0, 0, jnp.where(rb == nrb - 1, 2, 1))


def _attn_specs(T, C):
    return [
        pl.BlockSpec((ATT_SUB * TQ, 128), lambda hp, st: (st, hp)),
        pl.BlockSpec((T, 128), lambda hp, st: (0, hp)),
        pl.BlockSpec((T, 128), lambda hp, st: (0, 20 + hp)),
        pl.BlockSpec((C, 128), lambda hp, st: (0, hp)),
        pl.BlockSpec((C, 128), lambda hp, st: (0, 4 + hp)),
        pl.BlockSpec((2, 2 * WIN_R, GRID_W, 128), lambda hp, st: (hp, 0, 0, 0)),
        pl.BlockSpec((3, TQ, TK), lambda hp, st: (0, 0, 0)),
        pl.BlockSpec((ATT_SUB * TQ, 128), lambda hp, st: (st, 24 + hp)),
    ]


def _build_bias(pairs_ref, mask_ref, bias_sc):
    for t in range(3):
        for hh in range(2):
            for i in range(Q_ROWS):
                for mm in range(K_ROWS // 2):
                    p = min(max(WIN_R - Q_ROWS * t + 2 * mm - i, 0), 2 * WIN_R - 1)
                    rs, cs = slice(GRID_W * i, GRID_W * (i + 1)), slice(128 * mm, 128 * (mm + 1))
                    bias_sc[t, hh, rs, cs] = (pairs_ref[hh, p] + mask_ref[t, rs, cs]) * LOG2E


def _attn_fwd(qs, kn, z, ckn, zc, pairs, row_mask):
    T, C = qs.shape[0], ckn.shape[0]
    rows = T // GRID_W
    nrb = rows // Q_ROWS

    def body(q_ref, k_ref, v_ref, ck_ref, cv_ref, pairs_ref, mask_ref, bg_ref, ob_ref, outb_ref, lse_ref, bias_sc):
        @pl.when(pl.program_id(1) == 0)
        def _():
            _build_bias(pairs_ref, mask_ref, bias_sc)

        ck2, cv2 = ck_ref[...], cv_ref[...]
        lane = lax.broadcasted_iota(jnp.int32, (1, 128), 1)
        for sub in range(ATT_SUB):
            rb = ATT_SUB * pl.program_id(1) + sub
            bias_ref = bias_sc.at[_attn_type(rb, nrb)]
            rs = slice(TQ * sub, TQ * (sub + 1))
            ks = pl.multiple_of(jnp.clip(Q_ROWS * rb - 4, 0, rows - K_ROWS) * GRID_W, GRID_W)
            q2 = q_ref[rs, :]
            k2 = k_ref[pl.ds(ks, TK), :]
            v2 = v_ref[pl.ds(ks, TK), :]
            o_acc = jnp.zeros((TQ, 128), F32)
            lse_acc = jnp.zeros((TQ, 128), F32)
            for hh in range(2):
                msk = (lane >= HEAD_DIM) == bool(hh)
                qm = jnp.where(msk, q2, jnp.zeros_like(q2))
                s = _nt(qm, k2) + bias_ref[hh]
                sc = _nt(qm, ck2)
                m = jnp.maximum(jnp.max(s, axis=-1, keepdims=True), jnp.max(sc, axis=-1, keepdims=True))
                p = jnp.exp2(s - m)
                pc = jnp.exp2(sc - m)
                va = jnp.where(msk, v2, jnp.ones_like(v2))
                cva = jnp.where(msk, cv2, jnp.ones_like(cv2))
                num = (jnp.dot(p.astype(BF16), va, preferred_element_type=F32)
                       + jnp.dot(pc.astype(BF16), cva, preferred_element_type=F32))
                den = pltpu.roll(num, HEAD_DIM, 1)
                o_acc = jnp.where(msk, num / den, o_acc)
                lse_acc = jnp.where(msk, m + jnp.log(den) * LOG2E, lse_acc)
            ob_ref[rs, :] = o_acc.astype(BF16)
            lse_ref[rs, :] = lse_acc
            bg = bg_ref[rs, :].astype(F32)
            outb_ref[rs, :] = (o_acc * (bg * _sigmoid(bg))).astype(BF16)

    tile = pl.BlockSpec((ATT_SUB * TQ, 128), lambda hp, st: (st, hp))
    return pl.pallas_call(
        body, name="attn_fwd", grid=(4, nrb // ATT_SUB),
        in_specs=_attn_specs(T, C),
        out_specs=[tile, tile, tile],
        out_shape=[jax.ShapeDtypeStruct((T, 512), BF16), jax.ShapeDtypeStruct((T, 512), BF16),
                   jax.ShapeDtypeStruct((T, 512), F32)],
        scratch_shapes=[pltpu.VMEM((3, 2, TQ, TK), F32)],
        compiler_params=_params(("arbitrary", "arbitrary")),
    )(qs, kn, z, ckn, zc, pairs, row_mask, z)


def _outproj_loss_bwd(x, tgt, out_a, out_b, gate, w_out, tm):
    T = x.shape[0]
    nt = T // tm

    def body(x_ref, t_ref, oa_ref, ob_ref, gate_ref, w_ref, dy_ref, dmc_ref, dw_ref, dgate_ref, loss_ref, acc):
        @pl.when(pl.program_id(0) == 0)
        def _():
            acc[...] = jnp.zeros_like(acc)
            dgate_ref[...] = jnp.zeros_like(dgate_ref)
            loss_ref[...] = jnp.zeros_like(loss_ref)

        oa, ob = oa_ref[...], ob_ref[...]
        gate_v = gate_ref[...]
        mix = (jnp.dot(oa, w_ref[0:512, :], preferred_element_type=F32)
               + jnp.dot(ob, w_ref[512:1024, :], preferred_element_type=F32))
        e = x_ref[...] + gate_v * mix - t_ref[...]
        se = jnp.sum(jnp.sum(e * e, axis=0, keepdims=True), axis=1, keepdims=True)
        loss_ref[...] += jnp.broadcast_to(se * (0.5 / D_MODEL), loss_ref.shape)
        dy = e * (1.0 / D_MODEL)
        dy_ref[...] = dy
        dgate_ref[...] += jnp.sum(dy * mix, axis=0, keepdims=True)
        dmix = (dy * gate_v).astype(BF16)
        dmc_ref[...] = _nt(dmix, w_ref[...]).astype(BF16)
        acc[0:512, :] += _tn(oa, dmix)
        acc[512:1024, :] += _tn(ob, dmix)

        @pl.when(pl.program_id(0) == nt - 1)
        def _():
            dw_ref[...] = acc[...].astype(BF16)

    row = lambda w: pl.BlockSpec((tm, w), lambda i: (i, 0))
    return pl.pallas_call(
        body, name="outproj_loss_bwd", grid=(nt,),
        in_specs=[row(D_MODEL), row(D_MODEL), row(512), row(512),
                  pl.BlockSpec((1, D_MODEL), lambda i: (0, 0)),
                  pl.BlockSpec((D_MODEL, D_MODEL), lambda i: (0, 0))],
        out_specs=[row(D_MODEL), row(D_MODEL), pl.BlockSpec((D_MODEL, D_MODEL), lambda i: (0, 0)),
                   pl.BlockSpec((1, D_MODEL), lambda i: (0, 0)), pl.BlockSpec((1, 128), lambda i: (0, 0))],
        out_shape=[jax.ShapeDtypeStruct((T, D_MODEL), F32), jax.ShapeDtypeStruct((T, D_MODEL), BF16),
                   jax.ShapeDtypeStruct((D_MODEL, D_MODEL), BF16), jax.ShapeDtypeStruct((1, D_MODEL), F32),
                   jax.ShapeDtypeStruct((1, 128), F32)],
        scratch_shapes=[pltpu.VMEM((D_MODEL, D_MODEL), F32)],
        compiler_params=_params(("arbitrary",)),
    )(x, tgt, out_a, out_b, gate, w_out)


def _attn_bwd(qs, kn, z, ckn, zc, pairs, row_mask, ob, lse, dmc, riders):
    T, C = qs.shape[0], ckn.shape[0]
    rows = T // GRID_W
    nrb = rows // Q_ROWS
    n_st = nrb // ATT_SUB
    n_rid = len(riders)

    def body(q_ref, k_ref, v_ref, ck_ref, cv_ref, pairs_ref, mask_ref, bg_ref, ob_ref, lse_ref, do_ref, *rest):
        rid_src = rest[:n_rid]
        dq_ref, dk_ref, dv_ref, dck_ref, dcv_ref, dbg_ref, dacc_ref = rest[n_rid:n_rid + 7]
        rid_dst = rest[n_rid + 7:2 * n_rid + 7]
        bias_sc = rest[2 * n_rid + 7]
        rid_sems = rest[2 * n_rid + 8:]
        hp, st = pl.program_id(0), pl.program_id(1)

        @pl.when(jnp.logical_and(hp == 0, st == 0))
        def _():
            for r, (ex, _) in enumerate(riders):
                ex.start(rid_src[r], rid_dst[r], *rid_sems[3 * r:3 * r + 3])

        @pl.when(st == 0)
        def _():
            _build_bias(pairs_ref, mask_ref, bias_sc)
            dk_ref[...] = jnp.zeros_like(dk_ref)
            dv_ref[...] = jnp.zeros_like(dv_ref)
            dck_ref[...] = jnp.zeros_like(dck_ref)
            dcv_ref[...] = jnp.zeros_like(dcv_ref)
            dacc_ref[...] = jnp.zeros_like(dacc_ref)

        ck2, cv2 = ck_ref[...], cv_ref[...]
        lane = lax.broadcasted_iota(jnp.int32, (1, 128), 1)
        for sub in range(ATT_SUB):
            rb = ATT_SUB * st + sub
            bias_ref = bias_sc.at[_attn_type(rb, nrb)]
            rs = slice(TQ * sub, TQ * (sub + 1))
            kb = jnp.clip(Q_ROWS * rb - 4, 0, rows - K_ROWS)
            ks = pl.multiple_of(kb * GRID_W, GRID_W)
            ebase = kb - Q_ROWS * rb + 11
            q2 = q_ref[rs, :]
            k2 = k_ref[pl.ds(ks, TK), :]
            v2 = v_ref[pl.ds(ks, TK), :]
            bg = bg_ref[rs, :].astype(F32)
            sig = _sigmoid(bg)
            obv = ob_ref[rs, :].astype(F32)
            dout = do_ref[rs, :].astype(F32)
            dbg_ref[rs, :] = (dout * obv * (sig * (1.0 + bg * (1.0 - sig)))).astype(BF16)
            d_o = dout * (bg * sig)
            d_oo = d_o * obv
            lse2 = lse_ref[rs, :]
            dq_acc = jnp.zeros((TQ, 128), F32)
            for hh in range(2):
                msk = (lane >= HEAD_DIM) == bool(hh)
                qm = jnp.where(msk, q2, jnp.zeros_like(q2))
                lse_h = jnp.max(jnp.where(msk, lse2, -jnp.inf), axis=-1, keepdims=True)
                p = jnp.exp2(_nt(qm, k2) + bias_ref[hh] - lse_h)
                pc = jnp.exp2(_nt(qm, ck2) - lse_h)
                dom_f = jnp.where(msk, d_o, 0.0)
                dom = dom_f.astype(BF16)
                delta = jnp.sum(jnp.where(msk, d_oo, 0.0), axis=-1, keepdims=True)
                d_hi = delta.astype(BF16).astype(F32)
                x0 = HEAD_DIM * (1 - hh)
                dom_aug = jnp.where(lane == x0, -d_hi, jnp.where(lane == x0 + 1, d_hi - delta, dom_f)).astype(BF16)
                extra = jnp.logical_or(lane == x0, lane == x0 + 1)
                va = jnp.where(msk, v2, jnp.where(extra, jnp.ones_like(v2), jnp.zeros_like(v2)))
                cva = jnp.where(msk, cv2, jnp.where(extra, jnp.ones_like(cv2), jnp.zeros_like(cv2)))
                ds = p * _nt(dom_aug, va)
                dsc = pc * _nt(dom_aug, cva)
                dsb, dscb = ds.astype(BF16), dsc.astype(BF16)
                dq_h = (jnp.dot(dsb, k2, preferred_element_type=F32)
                        + jnp.dot(dscb, ck2, preferred_element_type=F32))
                dq_acc = jnp.where(msk, dq_h, dq_acc)
                dk_ref[pl.ds(ks, TK), :] += _tn(dsb, qm)
                dv_ref[pl.ds(ks, TK), :] += _tn(p.astype(BF16), dom)
                dck_ref[...] += _tn(dscb, qm)
                dcv_ref[...] += _tn(pc.astype(BF16), dom)
                for i in range(Q_ROWS):
                    for mm in range(K_ROWS // 2):
                        dacc_ref[hh, ebase + (2 * mm - i)] += ds[GRID_W * i:GRID_W * (i + 1),
                                                                 128 * mm:128 * (mm + 1)]
            dq_ref[rs, :] = dq_acc

        @pl.when(jnp.logical_and(hp == pl.num_programs(0) - 1, st == n_st - 1))
        def _():
            for r, (ex, _) in enumerate(riders):
                ex.wait(rid_src[r], rid_dst[r], *rid_sems[3 * r:3 * r + 3])

    tile = pl.BlockSpec((ATT_SUB * TQ, 128), lambda hp, st: (st, hp))
    colT = pl.BlockSpec((T, 128), lambda hp, st: (0, hp))
    colC = pl.BlockSpec((C, 128), lambda hp, st: (0, hp))
    res = pl.pallas_call(
        body, name="attn_bwd", grid=(4, n_st),
        in_specs=(_attn_specs(T, C) + [tile, tile, pl.BlockSpec((ATT_SUB * TQ, 128), lambda hp, st: (st, 4 + hp))]
                  + [ANY_SPEC] * n_rid),
        out_specs=([tile, colT, colT, colC, colC, tile,
                    pl.BlockSpec((2, N_DIAG, GRID_W, 128), lambda hp, st: (hp, 0, 0, 0))] + [ANY_SPEC] * n_rid),
        out_shape=([jax.ShapeDtypeStruct((T, 512), F32), jax.ShapeDtypeStruct((T, 512), F32),
                    jax.ShapeDtypeStruct((T, 512), F32), jax.ShapeDtypeStruct((C, 512), F32),
                    jax.ShapeDtypeStruct((C, 512), F32), jax.ShapeDtypeStruct((T, 512), BF16),
                    jax.ShapeDtypeStruct((N_HEADS, N_DIAG, GRID_W, 128), F32)] + [ex.out_shape for ex, _ in riders]),
        scratch_shapes=[pltpu.VMEM((3, 2, TQ, TK), F32)] + [s for ex, _ in riders for s in ex.scratch],
        compiler_params=_params(("arbitrary", "arbitrary")),
    )(qs, kn, z, ckn, zc, pairs, row_mask, z, ob, lse, dmc, *[a for _, a in riders])
    return res[:7], res[7:]


def _rpb_grad(dacc, hsel, fold):
    n_off = 2 * WIN_C - 1
    n_dr = 2 * WIN_R - 1

    def body(a_ref, hsel_ref, fold_ref, o_ref):
        qc = lax.broadcasted_iota(jnp.int32, (GRID_W, 128), 0)
        lane = lax.broadcasted_iota(jnp.int32, (GRID_W, 128), 1)
        diff = lane % GRID_W - qc + (WIN_C - 1)
        left = lane < GRID_W

        def by_dr(dr):
            return a_ref[0, dr + 4] + pltpu.roll(a_ref[0, dr + 3], GRID_W, 1)

        out = jnp.zeros((32, 128), F32)
        for j in range((n_dr + 1) // 2):
            hi = pltpu.roll(by_dr(2 * j + 1), GRID_W, 1) if 2 * j + 1 < n_dr else 0.0
            pair = jnp.where(left, by_dr(2 * j), hi)
            parts = []
            for o in range(n_off):
                mv = jnp.where(diff == o, pair, 0.0)
                acc = mv[0:8]
                for r8 in range(1, GRID_W // 8):
                    acc = acc + mv[8 * r8:8 * (r8 + 1)]
                parts.append(acc)
            parts.append(jnp.zeros((8, 128), F32))
            stack = jnp.concatenate(parts, axis=0)
            s_hi = stack.astype(BF16)
            s_lo = (stack - s_hi.astype(F32)).astype(BF16)
            per_o = (jnp.dot(fold_ref[...], s_hi, preferred_element_type=F32)
                     + jnp.dot(fold_ref[...], s_lo, preferred_element_type=F32))
            out = out + _dot2(per_o, hsel_ref[j])
        o_ref[0] = out

    return pl.pallas_call(
        body, name="rpb_grad", grid=(N_HEADS,),
        in_specs=[pl.BlockSpec((1, N_DIAG, GRID_W, 128), lambda h: (h, 0, 0, 0)),
                  pl.BlockSpec((8, 128, 128), lambda h: (0, 0, 0)), pl.BlockSpec((32, 256), lambda h: (0, 0))],
        out_specs=pl.BlockSpec((1, 32, 128), lambda h: (h, 0, 0)),
        out_shape=jax.ShapeDtypeStruct((N_HEADS, 32, 128), F32),
        compiler_params=_params(("arbitrary",)),
    )(dacc, hsel, fold)


def _qk_bwd(z, col_q, col_k, dqs, dk, dv, ones_blk, gq, gk, foldm, tm, name):
    T = dk.shape[0]
    with_q = dqs is not None

    def norm_bwd(raw, dn, gain, ones_ref):
        rr = lax.rsqrt(_head_sum(raw * raw, ones_ref) * (1.0 / HEAD_DIM) + EPS)
        hat = raw * rr
        dgain = jnp.sum(dn * hat, axis=0, keepdims=True)
        dhat = dn * gain
        mean = _head_sum(dhat * hat, ones_ref) * (1.0 / HEAD_DIM)
        return rr * (dhat - hat * mean), dgain

    def body(*refs):
        if with_q:
            (bq_ref, bk_ref, dq_ref, dk_ref, dv_ref, ones_ref, gq_ref, gk_ref, fold_ref,
             dbq_ref, dbk_ref, dbv_ref, dgq_ref, dgk_ref, accq, acck) = refs
        else:
            (bk_ref, dk_ref, dv_ref, ones_ref, gk_ref, fold_ref, dbk_ref, dbv_ref, dgk_ref, acck) = refs
        i = pl.program_id(0)

        @pl.when(i == 0)
        def _():
            acck[...] = jnp.zeros_like(acck)
            if with_q:
                accq[...] = jnp.zeros_like(accq)

        dbk, dgk = norm_bwd(bk_ref[...].astype(F32), dk_ref[...] * LN2, gk_ref[...], ones_ref)
        dbk_ref[...] = dbk.astype(BF16)
        acck[...] += dgk
        dbv_ref[...] = dv_ref[...].astype(BF16)
        if with_q:
            dbq, dgq = norm_bwd(bq_ref[...].astype(F32), dq_ref[...] * ATT_SCALE, gq_ref[...], ones_ref)
            dbq_ref[...] = dbq.astype(BF16)
            accq[...] += dgq

        @pl.when(i == pl.num_programs(0) - 1)
        def _():
            dgk_ref[...] = jnp.dot(jnp.broadcast_to(acck[...], (8, 512)), fold_ref[...],
                                   preferred_element_type=F32, precision=HI)
            if with_q:
                dgq_ref[...] = jnp.dot(jnp.broadcast_to(accq[...], (8, 512)), fold_ref[...],
                                       preferred_element_type=F32, precision=HI)

    row = pl.BlockSpec((tm, 512), lambda i: (i, 0))
    cst = lambda a, b: pl.BlockSpec((a, b), lambda i: (0, 0))
    out_row = jax.ShapeDtypeStruct((T, 512), BF16)
    gvec = jax.ShapeDtypeStruct((8, 128), F32)
    if with_q:
        in_specs = [pl.BlockSpec((tm, 512), lambda i: (i, col_q)), pl.BlockSpec((tm, 512), lambda i: (i, col_k)),
                    row, row, row, cst(512, 512), cst(1, 512), cst(1, 512), cst(512, 128)]
        args = (z, z, dqs, dk, dv, ones_blk, gq, gk, foldm)
        out_specs = [row, row, row, cst(8, 128), cst(8, 128)]
        out_shape = [out_row, out_row, out_row, gvec, gvec]
        scratch = [pltpu.VMEM((1, 512), F32), pltpu.VMEM((1, 512), F32)]
    else:
        in_specs = [pl.BlockSpec((tm, 512), lambda i: (i, col_k)), row, row, cst(512, 512), cst(1, 512), cst(512, 128)]
        args = (z, dk, dv, ones_blk, gk, foldm)
        out_specs = [row, row, cst(8, 128)]
        out_shape = [out_row, out_row, gvec]
        scratch = [pltpu.VMEM((1, 512), F32)]
    return pl.pallas_call(
        body, name=name, grid=(T // tm,), in_specs=in_specs, out_specs=out_specs, out_shape=out_shape,
        scratch_shapes=scratch, compiler_params=_params(("arbitrary",)),
    )(*args)


def _sgu_bwd(z, dmc, sgn, ws, wst, bsb, ones8, tm):
    T = z.shape[0]

    def body(au_ref, av_ref, ag_ref, d_ref, sg_ref, ws_ref, wst_ref, bsb_ref, ones_ref,
             dau_ref, dav_ref, dag_ref, dws_ref, dbs_ref, dsg_ref):
        @pl.when(pl.program_id(0) == 0)
        def _():
            dws_ref[...] = jnp.zeros_like(dws_ref)
            dbs_ref[...] = jnp.zeros_like(dbs_ref)
            dsg_ref[...] = jnp.zeros_like(dsg_ref)

        for g in range(SGU_GROUPS):
            ws_bf = ws_ref[g].astype(BF16)
            wst_bf = wst_ref[g].astype(BF16)
            sg = sg_ref[:, 128 * g:128 * (g + 1)]
            bsb_g = bsb_ref[g]
            for j in range(tm // CHUNK):
                rs, cs = slice(CHUNK * j, CHUNK * (j + 1)), slice(128 * g, 128 * (g + 1))
                au, av, ag = (au_ref[rs, cs].astype(F32), av_ref[rs, cs].astype(F32), ag_ref[rs, cs].astype(F32))
                d = d_ref[rs, cs].astype(F32)
                _, (gu, dgu, dgv, rr, vhat, vn, mixed, sig, sl) = _sgu_chunk_fwd(au, av, ag, sg, ws_bf, bsb_g)
                dau_ref[rs, cs] = (d * mixed * sl * dgu).astype(BF16)
                dag_ref[rs, cs] = (d * gu * mixed * (sig * (1.0 + ag * (1.0 - sig)))).astype(BF16)
                dmixed = d * gu * sl
                dmb = dmixed.astype(BF16)
                dm_lo = (dmixed - dmb.astype(F32)).astype(BF16)
                dbs_ref[g] += _nt(ones_ref[...], dmb) + _nt(ones_ref[...], dm_lo)
                dws_ref[g] += _nt(dmb, vn.astype(BF16))
                dvn = jnp.dot(wst_bf, dmb, preferred_element_type=F32)
                dsg_ref[:, 128 * g:128 * (g + 1)] += jnp.sum(dvn * vhat, axis=0, keepdims=True)
                dvhat = dvn * sg
                mean = jnp.mean(dvhat * vhat, axis=-1, keepdims=True)
                dav_ref[rs, cs] = (rr * (dvhat - vhat * mean) * dgv).astype(BF16)

    row = lambda col: pl.BlockSpec((tm, 512), lambda i: (i, col))
    c3 = lambda n: pl.BlockSpec((SGU_GROUPS, CHUNK, n), lambda i: (0, 0, 0))
    out_row = jax.ShapeDtypeStruct((T, 512), BF16)
    return pl.pallas_call(
        body, name="sgu_bwd", grid=(T // tm,),
        in_specs=[row(0), row(1), row(2), row(0), pl.BlockSpec((1, 512), lambda i: (0, 0)),
                  c3(CHUNK), c3(CHUNK), c3(128), pl.BlockSpec((8, 128), lambda i: (0, 0))],
        out_specs=[row(0), row(0), row(0), c3(CHUNK), pl.BlockSpec((SGU_GROUPS, 8, CHUNK), lambda i: (0, 0, 0)),
                   pl.BlockSpec((1, 512), lambda i: (0, 0))],
        out_shape=[out_row, out_row, out_row, jax.ShapeDtypeStruct((SGU_GROUPS, CHUNK, CHUNK), F32),
                   jax.ShapeDtypeStruct((SGU_GROUPS, 8, CHUNK), F32), jax.ShapeDtypeStruct((1, 512), F32)],
        compiler_params=_params(("arbitrary",)),
    )(z, z, z, dmc, sgn, ws, wst, bsb, ones8)


def _inproj_bwd_dx(dzs, w_cols, w_in_t, x, dy, ng, scale, shift, tm, name, riders=()):
    T = x.shape[0]
    n = len(dzs)
    nt = T // tm
    with_dx = dy is not None
    n_own_in = n + 5 + with_dx
    n_own_out = 3 + with_dx
    n_rid = len(riders)

    def body(*refs):
        dz_refs = refs[:n]
        own = refs[n:n_own_in] + refs[n_own_in + n_rid:n_own_in + n_rid + n_own_out]
        rid_src = refs[n_own_in:n_own_in + n_rid]
        rid_dst = refs[n_own_in + n_rid + n_own_out:n_own_in + 2 * n_rid + n_own_out]
        rid_sems = refs[n_own_in + 2 * n_rid + n_own_out:]
        if with_dx:
            w_ref, x_ref, dy_ref, g_ref, sc_ref, sh_ref, gx_ref, dsh_ref, dsc_ref, dg_ref = own
        else:
            w_ref, x_ref, g_ref, sc_ref, sh_ref, dsh_ref, dsc_ref, dg_ref = own

        @pl.when(pl.program_id(0) == 0)
        def _():
            for r, (ex, _) in enumerate(riders):
                ex.start(rid_src[r], rid_dst[r], *rid_sems[3 * r:3 * r + 3])
            dsh_ref[...] = jnp.zeros_like(dsh_ref)
            dsc_ref[...] = jnp.zeros_like(dsc_ref)
            dg_ref[...] = jnp.zeros_like(dg_ref)

        dh = jnp.dot(dz_refs[0][...], w_ref[0:512, :], preferred_element_type=F32)
        for k in range(1, n):
            dh = dh + jnp.dot(dz_refs[k][...], w_ref[512 * k:512 * (k + 1), :], preferred_element_type=F32)
        xv = x_ref[...]
        r = lax.rsqrt(jnp.mean(xv * xv, axis=-1, keepdims=True) + EPS)
        xn = xv * r
        gv, op = g_ref[...], 1.0 + sc_ref[...]
        dsh_ref[...] += jnp.sum(dh, axis=0, keepdims=True)
        dsc_ref[...] += jnp.sum(dh * xn * gv, axis=0, keepdims=True)
        dg_ref[...] += jnp.sum(dh * op * xn, axis=0, keepdims=True)
        if with_dx:
            dxn = dh * (gv * op)
            gx_ref[...] = r * (dxn - xn * jnp.mean(dxn * xn, axis=-1, keepdims=True)) + dy_ref[...]

        @pl.when(pl.program_id(0) == nt - 1)
        def _():
            for r, (ex, _) in enumerate(riders):
                ex.wait(rid_src[r], rid_dst[r], *rid_sems[3 * r:3 * r + 3])

    vec = pl.BlockSpec((1, D_MODEL), lambda i: (0, 0))
    rowf = pl.BlockSpec((tm, D_MODEL), lambda i: (i, 0))
    in_specs = [pl.BlockSpec((tm, 512), lambda i: (i, 0))] * n
    in_specs += [pl.BlockSpec((512 * n, D_MODEL), lambda i: (w_cols // n, 0)), rowf]
    args = list(dzs) + [w_in_t, x]
    vshape = jax.ShapeDtypeStruct((1, D_MODEL), F32)
    out_specs, out_shape = [vec, vec, vec], [vshape, vshape, vshape]
    if with_dx:
        in_specs.append(rowf)
        args.append(dy)
        out_specs = [rowf] + out_specs
        out_shape = [jax.ShapeDtypeStruct((T, D_MODEL), F32)] + out_shape
    in_specs += [vec, vec, vec] + [ANY_SPEC] * n_rid
    args += [ng, scale, shift] + [a for _, a in riders]
    res = pl.pallas_call(
        body, name=name, grid=(nt,), in_specs=in_specs, out_specs=out_specs + [ANY_SPEC] * n_rid,
        out_shape=out_shape + [ex.out_shape for ex, _ in riders],
        scratch_shapes=[s for ex, _ in riders for s in ex.scratch],
        compiler_params=_params(("arbitrary",)),
    )(*args)
    return res[:n_own_out], res[n_own_out:]


def _inproj_bwd_dw(h, dzs, hc, dzc_k, dzc_v, tk, riders=()):
    T = h.shape[0]
    nt = T // tk
    n = len(dzs)
    n_rid = len(riders)

    def body(*refs):
        h_ref = refs[0]
        dz_refs = refs[1:1 + n]
        hc_ref, dzck_ref, dzcv_ref = refs[1 + n:4 + n]
        rid_src = refs[4 + n:4 + n + n_rid]
        o_ref = refs[4 + n + n_rid]
        rid_dst = refs[5 + n + n_rid:5 + n + 2 * n_rid]
        acc = refs[5 + n + 2 * n_rid]
        rid_sems = refs[6 + n + 2 * n_rid:]
        t = pl.program_id(0)

        @pl.when(t == 0)
        def _():
            for r, (ex, _) in enumerate(riders):
                ex.start(rid_src[r], rid_dst[r], *rid_sems[3 * r:3 * r + 3])
            acc[...] = jnp.zeros_like(acc)
            acc[512 * 4:512 * 5, :] = _tn(dzck_ref[...], hc_ref[...])
            acc[512 * 5:512 * 6, :] = _tn(dzcv_ref[...], hc_ref[...])

        hv = h_ref[...]
        for k in range(n):
            acc[512 * k:512 * (k + 1), :] += _tn(dz_refs[k][...], hv)

        @pl.when(t == nt - 1)
        def _():
            o_ref[...] = acc[...].astype(BF16)
            for r, (ex, _) in enumerate(riders):
                ex.wait(rid_src[r], rid_dst[r], *rid_sems[3 * r:3 * r + 3])

    whole = lambda a: pl.BlockSpec(a.shape, lambda t: (0, 0))
    res = pl.pallas_call(
        body, name="inproj_bwd_dw", grid=(nt,),
        in_specs=([pl.BlockSpec((tk, D_MODEL), lambda t: (t, 0))] + [pl.BlockSpec((tk, 512), lambda t: (t, 0))] * n
                  + [whole(hc), whole(dzc_k), whole(dzc_v)] + [ANY_SPEC] * n_rid),
        out_specs=[pl.BlockSpec((512 * n, D_MODEL), lambda t: (0, 0))] + [ANY_SPEC] * n_rid,
        out_shape=[jax.ShapeDtypeStruct((512 * n, D_MODEL), BF16)] + [ex.out_shape for ex, _ in riders],
        scratch_shapes=[pltpu.VMEM((512 * n, D_MODEL), F32)] + [s for ex, _ in riders for s in ex.scratch],
        compiler_params=_params(("arbitrary",)),
    )(h, *dzs, hc, dzc_k, dzc_v, *[a for _, a in riders])
    return res[0], res[1:]


def _adamw_sharded(w, gparts, m, v, tr, name):
    R, C = w.shape
    n_part = gparts.shape[0]

    def body(w_ref, gp_ref, m_ref, v_ref, g_ref, d_ref, m2_ref, v2_ref):
        g = gp_ref[0].astype(F32)
        for d in range(1, n_part):
            g = g + gp_ref[d].astype(F32)
        delta, m2, v2 = _adam(w_ref[...], g, m_ref[...], v_ref[...])
        g_ref[...] = g
        d_ref[...] = delta
        m2_ref[...] = m2
        v2_ref[...] = v2

    row = pl.BlockSpec((tr, C), lambda i: (i, 0))
    sh = jax.ShapeDtypeStruct((R, C), F32)
    return pl.pallas_call(
        body, name=name, grid=(R // tr,),
        in_specs=[row, pl.BlockSpec((n_part, tr, C), lambda i: (0, i, 0)), row, row],
        out_specs=[row, row, row, row], out_shape=[sh, sh, sh, sh],
        compiler_params=_params(("arbitrary",)),
    )(w, gparts, m, v)


def _pack_vectors(vec_rows, dsg, dgq, dgk, dgk_c, loss_part):
    n_vec = len(vec_rows)

    def body(*refs):
        vecs = refs[:n_vec]
        dsg_ref, dgq_ref, dgk_ref, dgkc_ref, loss_ref, v_ref = refs[n_vec:]
        row = lax.broadcasted_iota(jnp.int32, (16, D_MODEL), 0)
        misc = jnp.concatenate([dsg_ref[...], dgq_ref[0:1, :], dgk_ref[0:1, :], dgkc_ref[0:1, :],
                                loss_ref[...]], axis=1)
        v = jnp.where(row == V_MISC, jnp.broadcast_to(misc, (16, D_MODEL)), 0.0)
        for (r, _), ref in zip(vec_rows, vecs):
            v = jnp.where(row == r, jnp.broadcast_to(ref[...], (16, D_MODEL)), v)
        v_ref[...] = v

    return pl.pallas_call(
        body, name="pack_vectors", out_shape=jax.ShapeDtypeStruct((16, D_MODEL), F32), compiler_params=_params(),
    )(*[a for _, a in vec_rows], dsg, dgq, dgk, dgk_c, loss_part)


def _pack_matrices(dws, dbs, drpb):
    def body(dws_ref, dbs_ref, drpb_ref, m_ref):
        for g in range(SGU_GROUPS):
            m_ref[128 * g:128 * (g + 1), :] = dws_ref[g]
            m_ref[M_DBS + 8 * g:M_DBS + 8 * (g + 1), :] = dbs_ref[g]
        for hd in range(N_HEADS):
            m_ref[M_RPB + 32 * hd:M_RPB + 32 * (hd + 1), :] = drpb_ref[hd]

    return pl.pallas_call(
        body, name="pack_matrices", out_shape=jax.ShapeDtypeStruct((M_ROWS, 128), F32), compiler_params=_params(),
    )(dws, dbs, drpb)


SMALL_NAMES = ("b_ada", "norm_g", "sgu_norm_g", "w_spatial", "b_spatial", "q_norm_g", "k_norm_g", "rpb")


def _adamw_small(vg, mg, ws, ms, vs):
    k = len(SMALL_NAMES)

    def body(*refs):
        vg_ref, mg_ref = refs[0], refs[1]
        w_refs = dict(zip(SMALL_NAMES, refs[2:2 + k]))
        m_refs = dict(zip(SMALL_NAMES, refs[2 + k:2 + 2 * k]))
        v_refs = dict(zip(SMALL_NAMES, refs[2 + 2 * k:2 + 3 * k]))
        o_refs = [dict(zip(SMALL_NAMES, refs[2 + (3 + i) * k:2 + (4 + i) * k])) for i in range(4)]
        loss_ref = refs[2 + 7 * k]

        sv = vg_ref[0]
        for d in range(1, N_DEV):
            sv = sv + vg_ref[d]
        loss_ref[...] = sv[V_MISC:V_MISC + 1, 896:1024]

        def total(lo, hi):
            s = mg_ref[0, lo:hi, :]
            for d in range(1, N_DEV):
                s = s + mg_ref[d, lo:hi, :]
            return s

        def emit(name, idx, g):
            res = _adam(w_refs[name][idx], g, m_refs[name][idx], v_refs[name][idx])
            for o, val in zip(o_refs, (g,) + res):
                o[name][idx] = val

        everything = (slice(None), slice(None))
        row = lambda r: sv[r:r + 1, :]
        emit("b_ada", everything, jnp.concatenate(
            [row(V_DSHIFT) + row(V_DCSHIFT), row(V_DSCALE) + row(V_DCSCALE), row(V_DGATE)], axis=1))
        emit("norm_g", everything, row(V_DNG) + row(V_DNG_CTX))
        misc = row(V_MISC)
        emit("sgu_norm_g", everything, misc[:, 0:512])
        emit("q_norm_g", everything, misc[:, 512:512 + HEAD_DIM])
        emit("k_norm_g", everything, misc[:, 640:640 + HEAD_DIM] + misc[:, 768:768 + HEAD_DIM])
        for g in range(SGU_GROUPS):
            emit("w_spatial", (0, g), total(128 * g, 128 * (g + 1)))
            emit("b_spatial", (0, slice(g, g + 1), slice(None)), total(M_DBS + 8 * g, M_DBS + 8 * (g + 1))[0:1, :])
        for hd in range(N_HEADS):
            by_dc = total(M_RPB + 32 * hd, M_RPB + 32 * (hd + 1))
            emit("rpb", (0, hd), by_dc.T[0:2 * WIN_R - 1, 0:2 * WIN_C - 1])

    shapes = [jax.ShapeDtypeStruct(w.shape, F32) for w in ws]
    res = pl.pallas_call(body, name="adamw_small", out_shape=shapes * 4 + [jax.ShapeDtypeStruct((1, 128), F32)],
                         compiler_params=_params())(vg, mg, *ws, *ms, *vs)
    return [res[i * k:(i + 1) * k] for i in range(4)], res[4 * k]


def _adamw_cctx(pc_g, w, m, v):
    def body(pc_ref, w_ref, m_ref, v_ref, g_ref, d_ref, m2_ref, v2_ref):
        pc = pc_ref[0, 0:1, :]
        for d in range(1, N_DEV):
            pc = pc + pc_ref[d, 0:1, :]
        cc = w_ref[...]
        sig = _sigmoid(cc)
        g = pc * (sig * (1.0 + cc * (1.0 - sig)))
        delta, m2, v2 = _adam(cc, g, m_ref[...], v_ref[...])
        g_ref[...] = g
        d_ref[...] = delta
        m2_ref[...] = m2
        v2_ref[...] = v2

    sh = jax.ShapeDtypeStruct((1, D_MODEL), F32)
    return pl.pallas_call(body, name="adamw_cctx", out_shape=[sh, sh, sh, sh], compiler_params=_params())(
        pc_g, w, m, v)


def _block_ones(n, blk):
    i = np.arange(n)
    return jnp.asarray((i[:, None] // blk == i[None, :] // blk).astype(np.float32), BF16)


def _rpb_pairs(rpb):
    n_off = 2 * WIN_C - 1
    cols = np.arange(GRID_W)
    c0 = np.clip(cols - WIN_C // 2, 0, GRID_W - WIN_C)
    in_win = (cols[None, :] >= c0[:, None]) & (cols[None, :] < c0[:, None] + WIN_C)
    dc = np.clip(cols[None, :] - cols[:, None] + (WIN_C - 1), 0, n_off - 1)
    expand = (dc[None] == np.arange(n_off)[:, None, None]) & in_win[None]
    toep = jnp.einsum("hrd,dqk->hrqk", rpb, jnp.asarray(expand, F32), precision=HI)
    toep = toep + jnp.asarray(np.where(in_win, 0.0, NEG_INF).astype(np.float32))
    neg = jnp.full((N_HEADS, 1, GRID_W, GRID_W), NEG_INF, F32)
    ext = jnp.concatenate([neg, toep, neg], axis=1)
    return jnp.concatenate([ext[:, :-1], ext[:, 1:]], axis=-1)


def _row_mask(rows):
    nrb = rows // Q_ROWS
    valid = np.zeros((3, Q_ROWS, 1, K_ROWS, 1), bool)
    for t, rb in enumerate((0, 1, nrb - 1)):
        kb = int(np.clip(Q_ROWS * rb - 4, 0, rows - K_ROWS))
        for i in range(Q_ROWS):
            r0 = int(np.clip(Q_ROWS * rb + i - WIN_R // 2, 0, rows - WIN_R))
            for j in range(K_ROWS):
                valid[t, i, 0, j, 0] = r0 <= kb + j < r0 + WIN_R
    full = np.broadcast_to(valid, (3, Q_ROWS, GRID_W, K_ROWS, GRID_W)).reshape(3, TQ, TK)
    return jnp.asarray(np.where(full, 0.0, NEG_INF).astype(np.float32))


def kernel(x, c, ctx, c_ctx, w_ada, b_ada, norm_g, w_in, sgu_norm_g, w_spatial, b_spatial, q_norm_g, k_norm_g, rpb, w_out, loss_target, m_c_ctx, m_w_ada, m_b_ada, m_norm_g, m_w_in, m_sgu_norm_g, m_w_spatial, m_b_spatial, m_q_norm_g, m_k_norm_g, m_rpb, m_w_out, v_c_ctx, v_w_ada, v_b_ada, v_norm_g, v_w_in, v_sgu_norm_g, v_w_spatial, v_b_spatial, v_q_norm_g, v_k_norm_g, v_rpb, v_w_out):
    me = 4 * lax.axis_index("x") + 2 * lax.axis_index("y") + lax.axis_index("c")
    x2, ctx2, tgt2 = x[0], ctx[0], loss_target[0]
    T, C = x2.shape[0], ctx2.shape[0]
    rows = T // GRID_W
    wada, win_t, wout = w_ada[0], w_in[0].T, w_out[0]
    ada_w = wada.shape[1]
    win_w = win_t.shape[0]

    row8 = lax.broadcasted_iota(jnp.int32, (8, D_MODEL), 0)
    c_blk = jnp.where(row8 == me, jnp.broadcast_to(c, (8, D_MODEL)), 0.0)
    c_stack, win_g, wout_g = _allgather([c_blk, win_t.astype(BF16), wout.astype(BF16)], "gather_weights")
    w_in_b = win_g.reshape(D_IN, D_MODEL)
    w_out_b = wout_g.reshape(D_MODEL, D_MODEL)
    b_sh = lax.dynamic_slice(b_ada, (0, me * ada_w), (1, ada_w))
    c_ctx_row = c_ctx.reshape(1, D_MODEL)
    s16, part = _ada_fwd(c_stack, c_ctx_row, wada, b_sh)
    (part_g,) = _allgather([part], "gather_mod")
    mod16 = part_g.transpose(1, 0, 2).reshape(16, 3 * D_MODEL)
    mod = lax.dynamic_slice(mod16, (me, 0), (1, 3 * D_MODEL))
    shift, scale, gate = mod[:, :D_MODEL], mod[:, D_MODEL:2 * D_MODEL], mod[:, 2 * D_MODEL:]
    cshift, cscale = mod16[8:9, :D_MODEL], mod16[8:9, D_MODEL:2 * D_MODEL]

    ones512 = _block_ones(512, HEAD_DIM)
    ones8 = jnp.ones((8, 128), BF16)
    foldm = jnp.asarray((np.arange(512)[:, None] % HEAD_DIM == np.arange(128)[None, :]).astype(np.float32))
    lane_half = np.arange(128)[None, :, None] // GRID_W
    hsel = jnp.asarray((2 * np.arange(8)[:, None, None] + lane_half == np.arange(128)[None, None, :]).astype(np.float32),
                       BF16)
    foldr = jnp.asarray((np.arange(256)[None, :] // 8 == np.arange(32)[:, None]).astype(np.float32), BF16)
    gq512 = jnp.tile(q_norm_g, (1, N_HEADS))
    gk512 = jnp.tile(k_norm_g, (1, N_HEADS))
    ws = w_spatial[0]
    wst = ws.transpose(0, 2, 1)
    bsb = jnp.broadcast_to(b_spatial[0][:, :, None], (SGU_GROUPS, CHUNK, 128))
    pairs = _rpb_pairs(rpb[0])
    row_mask = _row_mask(rows)

    h, z, qs, kn = _inproj_fwd(x2, norm_g, scale, shift, w_in_b, ones512, gq512, gk512, 512)
    hc, zc, ckn = _ctx_fwd(ctx2, norm_g, cscale, cshift, w_in_b, ones512, gk512)
    out_a = _sgu_fwd(z, sgu_norm_g, ws, bsb, 512)
    ob, out_b, lse = _attn_fwd(qs, kn, z, ckn, zc, pairs, row_mask)

    dy, dmc, dw_out, dgate, loss_part = _outproj_loss_bwd(x2, tgt2, out_a, out_b, gate, w_out_b, 512)
    dw_out_blocks = dw_out.reshape(N_DEV, D_MODEL // N_DEV, D_MODEL)
    (dqs, dk, dv, dck, dcv, db_g, dacc), (gout_parts,) = _attn_bwd(
        qs, kn, z, ckn, zc, pairs, row_mask, ob, lse, dmc, [(_Hosted("a2a", dw_out_blocks), dw_out_blocks)])
    drpb = _rpb_grad(dacc, hsel, foldr)
    db_q, db_k, db_v, dgq, dgk = _qk_bwd(z, 3, 4, dqs, dk, dv, ones512, gq512, gk512, foldm, 512, "qk_bwd")
    dzc_k, dzc_v, dgk_c = _qk_bwd(zc, None, 0, None, dck, dcv, ones512, None, gk512, foldm, C, "ctx_k_bwd")
    da_u, da_v, da_g, dws, dbs, dsg = _sgu_bwd(z, dmc, sgu_norm_g, ws, wst, bsb, ones8, 512)
    dzs = [da_u, da_v, da_g, db_q, db_k, db_v, db_g]
    mloc = _pack_matrices(dws, dbs, drpb)
    dw_in_t, (mg,) = _inproj_bwd_dw(h, dzs, hc, dzc_k, dzc_v, 512, [(_Hosted("ag", mloc), mloc)])
    (chip_sums,) = _chip_presum([dw_in_t.reshape(N_DEV, win_w, D_MODEL)], "presum_dw_in")
    (grad_x, dshift, dscale, dng), (gin_parts,) = _inproj_bwd_dx(
        dzs, 0, w_in_b, x2, dy, norm_g, scale, shift, 512, "inproj_bwd_dx", [(_Hosted("chips", chip_sums), chip_sums)])
    (dcshift, dcscale, dng_c), _ = _inproj_bwd_dx([dzc_k, dzc_v], 4, w_in_b, ctx2, None, norm_g, cscale, cshift, C,
                                                  "ctx_bwd_dx")
    res_in = _adamw_sharded(win_t, gin_parts, m_w_in[0].T, v_w_in[0].T, 112, "adamw_w_in")
    res_out = _adamw_sharded(wout, gout_parts, m_w_out[0], v_w_out[0], 128, "adamw_w_out")

    zero_row = jnp.zeros((1, D_MODEL), F32)
    vec_rows = [(V_DSHIFT, dshift), (V_DSCALE, dscale), (V_DGATE, dgate), (V_DCSHIFT, dcshift),
                (V_DCSCALE, dcscale), (V_ZERO, zero_row), (V_DNG, dng), (V_DNG_CTX, dng_c)]
    vloc = _pack_vectors(vec_rows, dsg, dgq, dgk, dgk_c, loss_part)
    (vg,) = _allgather([vloc], "gather_small")
    small_w = (b_ada, norm_g, sgu_norm_g, w_spatial, b_spatial, q_norm_g, k_norm_g, rpb)
    small_m = (m_b_ada, m_norm_g, m_sgu_norm_g, m_w_spatial, m_b_spatial, m_q_norm_g, m_k_norm_g, m_rpb)
    small_v = (v_b_ada, v_norm_g, v_sgu_norm_g, v_w_spatial, v_b_spatial, v_q_norm_g, v_k_norm_g, v_rpb)
    res_small, loss_row = _adamw_small(vg, mg, small_w, small_m, small_v)

    dm_all = vg[:, V_DSHIFT:V_DGATE + 1, :].reshape(N_DEV, 3 * D_MODEL)
    dc_all = vg[:, V_DCSHIFT:V_ZERO + 1, :].reshape(N_DEV, 3 * D_MODEL)
    dm_sh = lax.dynamic_slice(dm_all, (0, me * ada_w), (N_DEV, ada_w))
    dc_sh = lax.dynamic_slice(dc_all, (0, me * ada_w), (N_DEV, ada_w))
    *res_ada, pc = _ada_bwd(s16, dm_sh, dc_sh, wada, m_w_ada[0], v_w_ada[0])
    (pc_g,) = _allgather([pc], "gather_cctx")
    res_cctx = _adamw_cctx(pc_g, c_ctx_row, m_c_ctx.reshape(1, D_MODEL), v_c_ctx.reshape(1, D_MODEL))

    loss = loss_row[0, 0]
    outs = [loss, grad_x[None]]
    for kind in range(4):
        by_name = dict(zip(SMALL_NAMES, res_small[kind]))
        by_name.update(c_ctx=res_cctx[kind].reshape(D_MODEL), w_ada=res_ada[kind][None],
                       w_in=res_in[kind].T[None], w_out=res_out[kind][None])
        outs += [by_name[nme] for nme in ("c_ctx", "w_ada", "b_ada", "norm_g", "w_in", "sgu_norm_g", "w_spatial",
                                          "b_spatial", "q_norm_g", "k_norm_g", "rpb", "w_out")]
    return tuple(outs)
```

```python
import functools

import numpy as np
import jax
import jax.numpy as jnp
from jax import lax
from jax.experimental import pallas as pl
from jax.experimental.pallas import tpu as pltpu

F32 = jnp.float32
BF16 = jnp.bfloat16
HI = lax.Precision.HIGHEST

N_DEV = 8
D_MODEL = 1024
D_A = 512
D_B = 512
D_IN = 3584
N_BRANCH = 7
HEAD_DIM = 64
N_HEADS = 8
GRID_W = 64
WIN_R = 8
WIN_C = 16
CHUNK = 128
SGU_GROUPS = 4
EPS = 1e-6
NEG_INF = -1e30
Q_ROWS = 4
K_ROWS = 12
TQ = Q_ROWS * GRID_W
TK = K_ROWS * GRID_W
N_DIAG = 22
ATT_SUB = 4
ATT_SCALE = HEAD_DIM ** -0.5
LOG2E = 1.4426950408889634
LN2 = 0.6931471805599453

ADAM_LR = 0.001
ADAM_B1 = 0.9
ADAM_B2 = 0.999
ADAM_EPS = 1e-08
ADAM_WD = 0.01
ADAM_STEP = 10

VMEM_LIMIT = 56 * 1024 * 1024
MESH = pl.DeviceIdType.MESH

V_DSHIFT, V_DSCALE, V_DGATE, V_DCSHIFT, V_DCSCALE, V_ZERO, V_DNG, V_DNG_CTX, V_MISC = range(9)
M_DBS, M_RPB, M_ROWS = 512, 544, 800


def _params(sem=None):
    return pltpu.CompilerParams(dimension_semantics=sem, vmem_limit_bytes=VMEM_LIMIT)


def _sigmoid(x):
    return 1.0 / (1.0 + jnp.exp(-x))


def _gelu_parts(x):
    cdf = 0.5 * (1.0 + lax.erf(x * 0.7071067811865476))
    pdf = jnp.exp(-0.5 * x * x) * 0.3989422804014327
    return x * cdf, cdf + x * pdf


def _nt(a, b):
    return lax.dot_general(a, b, (((1,), (1,)), ((), ())), preferred_element_type=F32)


def _tn(a, b):
    return lax.dot_general(a, b, (((0,), (0,)), ((), ())), preferred_element_type=F32)


def _dot2(v, ones_bf):
    hi = v.astype(BF16)
    lo = (v - hi.astype(F32)).astype(BF16)
    return (jnp.dot(hi, ones_bf, preferred_element_type=F32)
            + jnp.dot(lo, ones_bf, preferred_element_type=F32))


def _head_sum(v, ones_ref):
    return jnp.dot(v.astype(BF16), ones_ref[...], preferred_element_type=F32)


def _adam(w, g, m, v):
    m2 = ADAM_B1 * m + (1.0 - ADAM_B1) * g
    v2 = ADAM_B2 * v + (1.0 - ADAM_B2) * (g * g)
    m_hat = m2 / (1.0 - ADAM_B1 ** ADAM_STEP)
    v_hat = v2 / (1.0 - ADAM_B2 ** ADAM_STEP)
    delta = -ADAM_LR * (m_hat / (jnp.sqrt(v_hat) + ADAM_EPS) + ADAM_WD * w)
    return delta, m2, v2


def _allgather(arrs, name):
    n = len(arrs)
    outs_shape = [jax.ShapeDtypeStruct((N_DEV,) + tuple(a.shape), a.dtype) for a in arrs]

    def body(*refs):
        ins, outs = refs[:n], refs[n:2 * n]
        send_sems, recv_sems, loc_sems = refs[2 * n:]
        x, y, c = lax.axis_index("x"), lax.axis_index("y"), lax.axis_index("c")
        me, sib = (x, y, c), (x, y, 1 - c)
        chips = [(1 - x, y), (x, 1 - y), (1 - x, 1 - y)]

        def slot(px, py, pc):
            return 4 * px + 2 * py + pc

        def copy(a, k, block, to, src=None):
            return pltpu.make_async_remote_copy(
                src_ref=outs[a].at[slot(*block)] if src is None else src, dst_ref=outs[a].at[slot(*block)],
                send_sem=send_sems.at[7 * a + k], recv_sem=recv_sems.at[7 * a + k],
                device_id=to, device_id_type=MESH)

        started, locs = [], []
        for a in range(n):
            lc = pltpu.make_async_copy(ins[a], outs[a].at[slot(*me)], loc_sems.at[a])
            lc.start()
            locs.append(lc)
            for j, chip in enumerate(chips):
                started.append(copy(a, 1 + j, me, (*chip, c), src=ins[a]))
            started.append(copy(a, 0, me, sib, src=ins[a]))
        for cp in started:
            cp.start()
        for j, chip in enumerate(chips):
            for a in range(n):
                copy(a, 1 + j, (*chip, c), me).wait_recv()
                fwd = copy(a, 4 + j, (*chip, c), sib)
                fwd.start()
                started.append(fwd)
        for a in range(n):
            copy(a, 0, sib, me).wait_recv()
            for j, chip in enumerate(chips):
                copy(a, 4 + j, (*chip, 1 - c), me).wait_recv()
        for cp in started:
            cp.wait_send()
        for lc in locs:
            lc.wait()

    res = pl.pallas_call(
        body, name=name, out_shape=outs_shape,
        in_specs=[pl.BlockSpec(memory_space=pl.ANY)] * n,
        out_specs=[pl.BlockSpec(memory_space=pl.ANY)] * n,
        scratch_shapes=[pltpu.SemaphoreType.DMA((7 * n,)), pltpu.SemaphoreType.DMA((7 * n,)),
                        pltpu.SemaphoreType.DMA((n,))],
    )(*arrs)
    return list(res)


def _chip_presum(arrs, name):
    n = len(arrs)
    n_chip = N_DEV // 2
    outs_shape = [jax.ShapeDtypeStruct((n_chip,) + tuple(a.shape[1:]), a.dtype) for a in arrs]

    def body(*refs):
        ins, sums = refs[:n], refs[n:2 * n]
        tmps = refs[2 * n:3 * n]
        s1, r1 = refs[3 * n:]
        c = lax.axis_index("c")
        sib = (lax.axis_index("x"), lax.axis_index("y"), 1 - c)
        all_chips = [(0, 0), (0, 1), (1, 0), (1, 1)]

        swaps = []
        for a in range(n):
            for q, (qx, qy) in enumerate(all_chips):
                cp = pltpu.make_async_remote_copy(
                    src_ref=ins[a].at[4 * qx + 2 * qy + (1 - c)], dst_ref=tmps[a].at[q],
                    send_sem=s1.at[n_chip * a + q], recv_sem=r1.at[n_chip * a + q],
                    device_id=sib, device_id_type=MESH)
                cp.start()
                swaps.append(cp)
        for a in range(n):
            for q, (qx, qy) in enumerate(all_chips):
                swaps[n_chip * a + q].wait_recv()
                both = ins[a][4 * qx + 2 * qy + c].astype(F32) + tmps[a][q].astype(F32)
                sums[a][q] = both.astype(sums[a].dtype)
        for cp in swaps:
            cp.wait_send()

    vm = pl.BlockSpec(memory_space=pltpu.VMEM)
    res = pl.pallas_call(
        body, name=name, out_shape=outs_shape, in_specs=[vm] * n, out_specs=[vm] * n,
        scratch_shapes=([pltpu.VMEM(s.shape, s.dtype) for s in outs_shape]
                        + [pltpu.SemaphoreType.DMA((n_chip * n,)), pltpu.SemaphoreType.DMA((n_chip * n,))]),
        compiler_params=_params(),
    )(*arrs)
    return list(res)


class _Hosted:
    def __init__(self, kind, src):
        self.kind = kind
        n_slot = {"a2a": N_DEV, "ag": N_DEV, "chips": N_DEV // 2}[kind]
        blk = src.shape if kind == "ag" else src.shape[1:]
        self.out_shape = jax.ShapeDtypeStruct((n_slot,) + tuple(blk), src.dtype)
        self.n_peer = n_slot - 1
        self.scratch = [pltpu.SemaphoreType.DMA((self.n_peer,)), pltpu.SemaphoreType.DMA((self.n_peer,)),
                        pltpu.SemaphoreType.DMA]

    def _copies(self, src, dst, send_sems, recv_sems, loc_sem, landing):
        x, y, c = lax.axis_index("x"), lax.axis_index("y"), lax.axis_index("c")
        if self.kind == "chips":
            me = 2 * x + y
            peers = [((px, py, c), 2 * px + py) for px, py in ((1 - x, y), (x, 1 - y), (1 - x, 1 - y))]
        else:
            me = 4 * x + 2 * y + c
            peers = []
            for k in range(1, N_DEV):
                px = 1 - x if (k >> 2) & 1 else x
                py = 1 - y if (k >> 1) & 1 else y
                pc = 1 - c if k & 1 else c
                peers.append(((px, py, pc), 4 * px + 2 * py + pc))
        remote = []
        for k, (peer, pid) in enumerate(peers):
            s = src if self.kind == "ag" else src.at[pid]
            remote.append(pltpu.make_async_remote_copy(
                src_ref=s, dst_ref=dst.at[pid if landing else me],
                send_sem=send_sems.at[k], recv_sem=recv_sems.at[k], device_id=peer, device_id_type=MESH))
        local = pltpu.make_async_copy(src if self.kind == "ag" else src.at[me], dst.at[me], loc_sem)
        return remote, local

    def start(self, src, dst, send_sems, recv_sems, loc_sem):
        remote, local = self._copies(src, dst, send_sems, recv_sems, loc_sem, landing=False)
        for cp in remote:
            cp.start()
        local.start()

    def wait(self, src, dst, send_sems, recv_sems, loc_sem):
        remote, local = self._copies(src, dst, send_sems, recv_sems, loc_sem, landing=True)
        for cp in remote:
            cp.wait_recv()
        for cp in remote:
            cp.wait_send()
        local.wait()


ANY_SPEC = pl.BlockSpec(memory_space=pl.ANY)


def _ada_fwd(c_blk, c_ctx_row, w_sh, b_sh):
    n_col = w_sh.shape[1]

    def body(cb_ref, cc_ref, w_ref, b_ref, s_ref, parts_ref, cstack, part, s1, r1, s2, r2):
        x, y, c = lax.axis_index("x"), lax.axis_index("y"), lax.axis_index("c")
        me = 4 * x + 2 * y + c
        peers = []
        for k in range(1, N_DEV):
            px = 1 - x if (k >> 2) & 1 else x
            py = 1 - y if (k >> 1) & 1 else y
            pc = 1 - c if k & 1 else c
            peers.append(((px, py, pc), 4 * px + 2 * py + pc))

        def exchange(src, dst, send_sems, recv_sems):
            for k, (peer, _) in enumerate(peers):
                pltpu.make_async_remote_copy(src_ref=src, dst_ref=dst.at[me], send_sem=send_sems.at[k],
                                             recv_sem=recv_sems.at[k], device_id=peer, device_id_type=MESH).start()
            dst[me] = src[...]
            waits = [pltpu.make_async_remote_copy(src_ref=src, dst_ref=dst.at[pid], send_sem=send_sems.at[k],
                                                  recv_sem=recv_sems.at[k], device_id=peer, device_id_type=MESH)
                     for k, (peer, pid) in enumerate(peers)]
            for cp in waits:
                cp.wait_recv()
            for cp in waits:
                cp.wait_send()

        exchange(cb_ref, cstack, s1, r1)
        c_all = cstack[0]
        for d in range(1, N_DEV):
            c_all = c_all + cstack[d]
        row = lax.broadcasted_iota(jnp.int32, (8, D_MODEL), 0)
        cc = jnp.where(row == 0, jnp.broadcast_to(cc_ref[...], (8, D_MODEL)), 0.0)
        call = jnp.concatenate([c_all, cc], axis=0)
        s = call * _sigmoid(call)
        s_ref[...] = s
        part[...] = jnp.dot(s, w_ref[...], preferred_element_type=F32, precision=HI) + b_ref[...]
        exchange(part, parts_ref, s2, r2)

    sem7 = pltpu.SemaphoreType.DMA((N_DEV - 1,))
    return pl.pallas_call(
        body, name="ada_fwd",
        out_shape=[jax.ShapeDtypeStruct((16, D_MODEL), F32), jax.ShapeDtypeStruct((N_DEV, 16, n_col), F32)],
        scratch_shapes=[pltpu.VMEM((N_DEV, 8, D_MODEL), F32), pltpu.VMEM((16, n_col), F32), sem7, sem7, sem7, sem7],
        compiler_params=_params(),
    )(c_blk, c_ctx_row, w_sh, b_sh)


def _ada_bwd(s16, dm, dc, w, m, v):
    def body(s_ref, dm_ref, dc_ref, w_ref, m_ref, v_ref, g_ref, d_ref, m2_ref, v2_ref, pc_ref):
        dct = jnp.sum(dc_ref[...], axis=0, keepdims=True)
        row = lax.broadcasted_iota(jnp.int32, dc_ref.shape, 0)
        dcb = jnp.where(row == 0, jnp.broadcast_to(dct, dc_ref.shape), 0.0)
        dm16 = jnp.concatenate([dm_ref[...], dcb], axis=0)
        g = lax.dot_general(s_ref[...], dm16, (((0,), (0,)), ((), ())),
                            preferred_element_type=F32, precision=HI)
        w_ = w_ref[...]
        delta, m2, v2 = _adam(w_, g, m_ref[...], v_ref[...])
        g_ref[...] = g
        d_ref[...] = delta
        m2_ref[...] = m2
        v2_ref[...] = v2
        pc_ref[...] = lax.dot_general(dcb, w_, (((1,), (1,)), ((), ())),
                                      preferred_element_type=F32, precision=HI)

    sh = jax.ShapeDtypeStruct(w.shape, F32)
    return pl.pallas_call(
        body, name="ada_bwd",
        out_shape=[sh, sh, sh, sh, jax.ShapeDtypeStruct((8, D_MODEL), F32)],
        compiler_params=_params(),
    )(s16, dm, dc, w, m, v)


def _head_norm(zk, ones_ref, gain):
    ss = _head_sum(zk * zk, ones_ref)
    return zk * lax.rsqrt(ss * (1.0 / HEAD_DIM) + EPS) * gain


def _inproj_fwd(order, x, ng, scale, shift, w_blk_t, w_out_blk, tm):
    T = x.shape[0]
    nt = T // tm
    n_pass = N_DEV // 2
    blk_rows = w_blk_t.shape[0]
    rider = _Hosted("ag", w_out_blk)

    def body(order_ref, x_ref, g_ref, sc_ref, sh_ref, wb_ref, wo_ref, h_out, z_ref, wt_out, wo_out,
             hs, wt, send_sems, recv_sems, loc_sem, h_sem, wt_sem, *rider_sems):
        p, i = pl.program_id(0), pl.program_id(1)
        x, y, c = lax.axis_index("x"), lax.axis_index("y"), lax.axis_index("c")
        me, sib = (x, y, c), (x, y, 1 - c)
        chips = [(1 - x, y), (x, 1 - y), (1 - x, 1 - y)]

        def slot(px, py, pc):
            return 4 * px + 2 * py + pc

        def copy(k, block, to, src=None):
            return pltpu.make_async_remote_copy(
                src_ref=wt.at[slot(*block)] if src is None else src, dst_ref=wt.at[slot(*block)],
                send_sem=send_sems.at[k], recv_sem=recv_sems.at[k], device_id=to, device_id_type=MESH)

        own = pltpu.make_async_copy(wb_ref, wt.at[slot(*me)], loc_sem)
        h_copy = pltpu.make_async_copy(hs, h_out, h_sem)
        wt_copy = pltpu.make_async_copy(wt, wt_out, wt_sem)
        first = [copy(1 + j, me, (*chip, c), src=wb_ref) for j, chip in enumerate(chips)] + [copy(0, me, sib, src=wb_ref)]
        passed = [copy(4 + j, (*chip, c), sib) for j, chip in enumerate(chips)]

        @pl.when(jnp.logical_and(p == 0, i == 0))
        def _():
            own.start()
            for cp in first:
                cp.start()
            rider.start(wo_ref, wo_out, *rider_sems)
            own.wait()
            copy(0, sib, me).wait_recv()

        for j, chip in enumerate(chips):
            @pl.when(jnp.logical_and(p == j + 1, i == 0))
            def _(j=j, chip=chip):
                copy(1 + j, (*chip, c), me).wait_recv()
                passed[j].start()
                copy(4 + j, (*chip, 1 - c), me).wait_recv()
                if j == len(chips) - 1:
                    wt_copy.start()

        rows = pl.ds(pl.multiple_of(i * tm, tm), tm)

        @pl.when(p == 0)
        def _():
            xv = x_ref[...]
            r = lax.rsqrt(jnp.mean(xv * xv, axis=-1, keepdims=True) + EPS)
            hs[rows, :] = ((xv * r * g_ref[...]) * (1.0 + sc_ref[...]) + sh_ref[...]).astype(BF16)

        @pl.when(jnp.logical_and(p == 1, i == 0))
        def _():
            h_copy.start()

        w_pair = wt[pl.ds(2 * order_ref[p], 2)].reshape(2 * blk_rows, D_MODEL)
        z_ref[...] = _nt(hs[rows, :], w_pair).astype(BF16)

        @pl.when(jnp.logical_and(p == n_pass - 1, i == nt - 1))
        def _():
            for cp in first + passed:
                cp.wait_send()
            h_copy.wait()
            wt_copy.wait()
            rider.wait(wo_ref, wo_out, *rider_sems)

    vec = pl.BlockSpec((1, D_MODEL), lambda p, i, o: (0, 0))
    grid_spec = pltpu.PrefetchScalarGridSpec(
        num_scalar_prefetch=1, grid=(n_pass, nt),
        in_specs=[pl.BlockSpec((tm, D_MODEL), lambda p, i, o: (jnp.where(p == 0, i, nt - 1), 0)), vec, vec, vec,
                  ANY_SPEC, ANY_SPEC],
        out_specs=[ANY_SPEC, pl.BlockSpec((tm, 2 * blk_rows), lambda p, i, o: (i, o[p])), ANY_SPEC, ANY_SPEC],
        scratch_shapes=[pltpu.VMEM((T, D_MODEL), BF16), pltpu.VMEM((N_DEV, blk_rows, D_MODEL), BF16),
                        pltpu.SemaphoreType.DMA((7,)), pltpu.SemaphoreType.DMA((7,)), pltpu.SemaphoreType.DMA,
                        pltpu.SemaphoreType.DMA, pltpu.SemaphoreType.DMA] + rider.scratch)
    return pl.pallas_call(
        body, name="inproj_fwd", grid_spec=grid_spec,
        out_shape=[jax.ShapeDtypeStruct((T, D_MODEL), BF16), jax.ShapeDtypeStruct((T, D_IN), BF16),
                   jax.ShapeDtypeStruct((N_DEV, blk_rows, D_MODEL), BF16), rider.out_shape],
        compiler_params=_params(("arbitrary", "arbitrary")),
    )(order, x, ng, scale, shift, w_blk_t, w_out_blk)


def _qk_norm(z, ones_blk, gq, gk, tm):
    T = z.shape[0]

    def body(q_ref, k_ref, ones_ref, gq_ref, gk_ref, qs_ref, kn_ref):
        qs = _head_norm(q_ref[...].astype(F32), ones_ref, gq_ref[...]) * (ATT_SCALE * LOG2E)
        qs_ref[...] = qs.astype(BF16)
        kn_ref[...] = _head_norm(k_ref[...].astype(F32), ones_ref, gk_ref[...]).astype(BF16)

    v512 = pl.BlockSpec((1, 512), lambda i: (0, 0))
    return pl.pallas_call(
        body, name="qk_norm", grid=(T // tm,),
        in_specs=[pl.BlockSpec((tm, 512), lambda i: (i, 3)), pl.BlockSpec((tm, 512), lambda i: (i, 4)),
                  pl.BlockSpec((512, 512), lambda i: (0, 0)), v512, v512],
        out_specs=[pl.BlockSpec((tm, 512), lambda i: (i, 0)), pl.BlockSpec((tm, 512), lambda i: (i, 0))],
        out_shape=[jax.ShapeDtypeStruct((T, 512), BF16), jax.ShapeDtypeStruct((T, 512), BF16)],
        compiler_params=_params(("arbitrary",)),
    )(z, z, ones_blk, gq, gk)


def _ctx_fwd(ctx, ng, cscale, cshift, w_in_t, ones_blk, gk):
    C = ctx.shape[0]

    def body(x_ref, g_ref, sc_ref, sh_ref, w_ref, ones_ref, gk_ref, h_ref, z_ref, kn_ref):
        xv = x_ref[...]
        r = lax.rsqrt(jnp.mean(xv * xv, axis=-1, keepdims=True) + EPS)
        h = (xv * r * g_ref[...]) * (1.0 + sc_ref[...]) + sh_ref[...]
        hb = h.astype(BF16)
        h_ref[...] = hb
        zk = _nt(hb, w_ref[0:512, :])
        zv = _nt(hb, w_ref[512:1024, :])
        z_ref[:, 0:512] = zk.astype(BF16)
        z_ref[:, 512:1024] = zv.astype(BF16)
        kn_ref[...] = _head_norm(zk, ones_ref, gk_ref[...]).astype(BF16)

    vec = pl.BlockSpec((1, D_MODEL), lambda i: (0, 0))
    return pl.pallas_call(
        body, name="ctx_fwd", grid=(1,),
        in_specs=[pl.BlockSpec((C, D_MODEL), lambda i: (0, 0)), vec, vec, vec,
                  pl.BlockSpec((1024, D_MODEL), lambda i: (2, 0)),
                  pl.BlockSpec((512, 512), lambda i: (0, 0)), pl.BlockSpec((1, 512), lambda i: (0, 0))],
        out_specs=[pl.BlockSpec((C, D_MODEL), lambda i: (0, 0)), pl.BlockSpec((C, 1024), lambda i: (0, 0)),
                   pl.BlockSpec((C, 512), lambda i: (0, 0))],
        out_shape=[jax.ShapeDtypeStruct((C, D_MODEL), BF16), jax.ShapeDtypeStruct((C, 1024), BF16),
                   jax.ShapeDtypeStruct((C, 512), BF16)],
        compiler_params=_params(("arbitrary",)),
    )(ctx, ng, cscale, cshift, w_in_t, ones_blk, gk)


def _sgu_chunk_fwd(au, av, ag, sg, ws_bf, bsb):
    gu, dgu = _gelu_parts(au)
    gv, dgv = _gelu_parts(av)
    rr = lax.rsqrt(jnp.mean(gv * gv, axis=-1, keepdims=True) + EPS)
    vhat = gv * rr
    vn = vhat * sg
    mixed = jnp.dot(ws_bf, vn.astype(BF16), preferred_element_type=F32) + bsb
    sig = _sigmoid(ag)
    sl = ag * sig
    return gu * mixed * sl, (gu, dgu, dgv, rr, vhat, vn, mixed, sig, sl)


def _sgu_fwd(z, sgn, ws, bsb, tm):
    T = z.shape[0]

    def body(au_ref, av_ref, ag_ref, sg_ref, ws_ref, bsb_ref, o_ref):
        for g in range(SGU_GROUPS):
            ws_bf = ws_ref[g].astype(BF16)
            sg = sg_ref[:, 128 * g:128 * (g + 1)]
            bsb_g = bsb_ref[g]
            for j in range(tm // CHUNK):
                rs, cs = slice(CHUNK * j, CHUNK * (j + 1)), slice(128 * g, 128 * (g + 1))
                out, _ = _sgu_chunk_fwd(au_ref[rs, cs].astype(F32), av_ref[rs, cs].astype(F32),
                                        ag_ref[rs, cs].astype(F32), sg, ws_bf, bsb_g)
                o_ref[rs, cs] = out.astype(BF16)

    return pl.pallas_call(
        body, name="sgu_fwd", grid=(T // tm,),
        in_specs=[pl.BlockSpec((tm, 512), lambda i: (i, 0)), pl.BlockSpec((tm, 512), lambda i: (i, 1)),
                  pl.BlockSpec((tm, 512), lambda i: (i, 2)), pl.BlockSpec((1, 512), lambda i: (0, 0)),
                  pl.BlockSpec((SGU_GROUPS, CHUNK, CHUNK), lambda i: (0, 0, 0)),
                  pl.BlockSpec((SGU_GROUPS, CHUNK, 128), lambda i: (0, 0, 0))],
        out_specs=pl.BlockSpec((tm, 512), lambda i: (i, 0)),
        out_shape=jax.ShapeDtypeStruct((T, 512), BF16),
        compiler_params=_params(("arbitrary",)),
    )(z, z, z, sgn, ws, bsb)


def _attn_type(rb, nrb):
    return jnp.where(rb == 0, 0, jnp.where(rb == nrb - 1, 2, 1))


def _attn_specs(T, C):
    return [
        pl.BlockSpec((ATT_SUB * TQ, 128), lambda hp, st: (st, hp)),
        pl.BlockSpec((T, 128), lambda hp, st: (0, hp)),
        pl.BlockSpec((T, 128), lambda hp, st: (0, 20 + hp)),
        pl.BlockSpec((C, 128), lambda hp, st: (0, hp)),
        pl.BlockSpec((C, 128), lambda hp, st: (0, 4 + hp)),
        pl.BlockSpec((2, 2 * WIN_R, GRID_W, 128), lambda hp, st: (hp, 0, 0, 0)),
        pl.BlockSpec((3, TQ, TK), lambda hp, st: (0, 0, 0)),
        pl.BlockSpec((ATT_SUB * TQ, 128), lambda hp, st: (st, 24 + hp)),
    ]


def _build_bias(pairs_ref, mask_ref, bias_sc):
    for t in range(3):
        for hh in range(2):
            for i in range(Q_ROWS):
                for mm in range(K_ROWS // 2):
                    p = min(max(WIN_R - Q_ROWS * t + 2 * mm - i, 0), 2 * WIN_R - 1)
                    rs, cs = slice(GRID_W * i, GRID_W * (i + 1)), slice(128 * mm, 128 * (mm + 1))
                    bias_sc[t, hh, rs, cs] = (pairs_ref[hh, p] + mask_ref[t, rs, cs]) * LOG2E


def _attn_fwd(qs, kn, z, ckn, zc, pairs, row_mask):
    T, C = qs.shape[0], ckn.shape[0]
    rows = T // GRID_W
    nrb = rows // Q_ROWS

    def body(q_ref, k_ref, v_ref, ck_ref, cv_ref, pairs_ref, mask_ref, bg_ref, ob_ref, outb_ref, lse_ref, bias_sc):
        @pl.when(pl.program_id(1) == 0)
        def _():
            _build_bias(pairs_ref, mask_ref, bias_sc)

        ck2, cv2 = ck_ref[...], cv_ref[...]
        lane = lax.broadcasted_iota(jnp.int32, (1, 128), 1)
        for sub in range(ATT_SUB):
            rb = ATT_SUB * pl.program_id(1) + sub
            bias_ref = bias_sc.at[_attn_type(rb, nrb)]
            rs = slice(TQ * sub, TQ * (sub + 1))
            ks = pl.multiple_of(jnp.clip(Q_ROWS * rb - 4, 0, rows - K_ROWS) * GRID_W, GRID_W)
            q2 = q_ref[rs, :]
            k2 = k_ref[pl.ds(ks, TK), :]
            v2 = v_ref[pl.ds(ks, TK), :]
            o_acc = jnp.zeros((TQ, 128), F32)
            lse_acc = jnp.zeros((TQ, 128), F32)
            for hh in range(2):
                msk = (lane >= HEAD_DIM) == bool(hh)
                qm = jnp.where(msk, q2, jnp.zeros_like(q2))
                s = _nt(qm, k2) + bias_ref[hh]
                sc = _nt(qm, ck2)
                m = jnp.maximum(jnp.max(s, axis=-1, keepdims=True), jnp.max(sc, axis=-1, keepdims=True))
                p = jnp.exp2(s - m)
                pc = jnp.exp2(sc - m)
                va = jnp.where(msk, v2, jnp.ones_like(v2))
                cva = jnp.where(msk, cv2, jnp.ones_like(cv2))
                num = (jnp.dot(p.astype(BF16), va, preferred_element_type=F32)
                       + jnp.dot(pc.astype(BF16), cva, preferred_element_type=F32))
                den = pltpu.roll(num, HEAD_DIM, 1)
                o_acc = jnp.where(msk, num / den, o_acc)
                lse_acc = jnp.where(msk, m + jnp.log(den) * LOG2E, lse_acc)
            ob_ref[rs, :] = o_acc.astype(BF16)
            lse_ref[rs, :] = lse_acc
            bg = bg_ref[rs, :].astype(F32)
            outb_ref[rs, :] = (o_acc * (bg * _sigmoid(bg))).astype(BF16)

    tile = pl.BlockSpec((ATT_SUB * TQ, 128), lambda hp, st: (st, hp))
    return pl.pallas_call(
        body, name="attn_fwd", grid=(4, nrb // ATT_SUB),
        in_specs=_attn_specs(T, C),
        out_specs=[tile, tile, tile],
        out_shape=[jax.ShapeDtypeStruct((T, 512), BF16), jax.ShapeDtypeStruct((T, 512), BF16),
                   jax.ShapeDtypeStruct((T, 512), F32)],
        scratch_shapes=[pltpu.VMEM((3, 2, TQ, TK), F32)],
        compiler_params=_params(("arbitrary", "arbitrary")),
    )(qs, kn, z, ckn, zc, pairs, row_mask, z)


def _outproj_loss_bwd(x, tgt, out_a, out_b, gate, w_out, tm):
    T = x.shape[0]
    nt = T // tm

    def body(x_ref, t_ref, oa_ref, ob_ref, gate_ref, w_ref, dy_ref, dmc_ref, dw_ref, dgate_ref, loss_ref, acc):
        @pl.when(pl.program_id(0) == 0)
        def _():
            acc[...] = jnp.zeros_like(acc)
            dgate_ref[...] = jnp.zeros_like(dgate_ref)
            loss_ref[...] = jnp.zeros_like(loss_ref)

        oa, ob = oa_ref[...], ob_ref[...]
        gate_v = gate_ref[...]
        mix = (jnp.dot(oa, w_ref[0:512, :], preferred_element_type=F32)
               + jnp.dot(ob, w_ref[512:1024, :], preferred_element_type=F32))
        e = x_ref[...] + gate_v * mix - t_ref[...]
        se = jnp.sum(jnp.sum(e * e, axis=0, keepdims=True), axis=1, keepdims=True)
        loss_ref[...] += jnp.broadcast_to(se * (0.5 / D_MODEL), loss_ref.shape)
        dy = e * (1.0 / D_MODEL)
        dy_ref[...] = dy
        dgate_ref[...] += jnp.sum(dy * mix, axis=0, keepdims=True)
        dmix = (dy * gate_v).astype(BF16)
        dmc_ref[...] = _nt(dmix, w_ref[...]).astype(BF16)
        acc[0:512, :] += _tn(oa, dmix)
        acc[512:1024, :] += _tn(ob, dmix)

        @pl.when(pl.program_id(0) == nt - 1)
        def _():
            dw_ref[...] = acc[...].astype(BF16)

    row = lambda w: pl.BlockSpec((tm, w), lambda i: (i, 0))
    return pl.pallas_call(
        body, name="outproj_loss_bwd", grid=(nt,),
        in_specs=[row(D_MODEL), row(D_MODEL), row(512), row(512),
                  pl.BlockSpec((1, D_MODEL), lambda i: (0, 0)),
                  pl.BlockSpec((D_MODEL, D_MODEL), lambda i: (0, 0))],
        out_specs=[row(D_MODEL), row(D_MODEL), pl.BlockSpec((D_MODEL, D_MODEL), lambda i: (0, 0)),
                   pl.BlockSpec((1, D_MODEL), lambda i: (0, 0)), pl.BlockSpec((1, 128), lambda i: (0, 0))],
        out_shape=[jax.ShapeDtypeStruct((T, D_MODEL), F32), jax.ShapeDtypeStruct((T, D_MODEL), BF16),
                   jax.ShapeDtypeStruct((D_MODEL, D_MODEL), BF16), jax.ShapeDtypeStruct((1, D_MODEL), F32),
                   jax.ShapeDtypeStruct((1, 128), F32)],
        scratch_shapes=[pltpu.VMEM((D_MODEL, D_MODEL), F32)],
        compiler_params=_params(("arbitrary",)),
    )(x, tgt, out_a, out_b, gate, w_out)


def _attn_bwd(qs, kn, z, ckn, zc, pairs, row_mask, ob, lse, dmc, riders):
    T, C = qs.shape[0], ckn.shape[0]
    rows = T // GRID_W
    nrb = rows // Q_ROWS
    n_st = nrb // ATT_SUB
    n_rid = len(riders)

    def body(q_ref, k_ref, v_ref, ck_ref, cv_ref, pairs_ref, mask_ref, bg_ref, ob_ref, lse_ref, do_ref, *rest):
        rid_src = rest[:n_rid]
        dq_ref, dk_ref, dv_ref, dck_ref, dcv_ref, dbg_ref, dacc_ref = rest[n_rid:n_rid + 7]
        rid_dst = rest[n_rid + 7:2 * n_rid + 7]
        bias_sc = rest[2 * n_rid + 7]
        rid_sems = rest[2 * n_rid + 8:]
        hp, st = pl.program_id(0), pl.program_id(1)

        @pl.when(jnp.logical_and(hp == 0, st == 0))
        def _():
            for r, (ex, _) in enumerate(riders):
                ex.start(rid_src[r], rid_dst[r], *rid_sems[3 * r:3 * r + 3])

        @pl.when(st == 0)
        def _():
            _build_bias(pairs_ref, mask_ref, bias_sc)
            dk_ref[...] = jnp.zeros_like(dk_ref)
            dv_ref[...] = jnp.zeros_like(dv_ref)
            dck_ref[...] = jnp.zeros_like(dck_ref)
            dcv_ref[...] = jnp.zeros_like(dcv_ref)
            dacc_ref[...] = jnp.zeros_like(dacc_ref)

        ck2, cv2 = ck_ref[...], cv_ref[...]
        lane = lax.broadcasted_iota(jnp.int32, (1, 128), 1)
        for sub in range(ATT_SUB):
            rb = ATT_SUB * st + sub
            bias_ref = bias_sc.at[_attn_type(rb, nrb)]
            rs = slice(TQ * sub, TQ * (sub + 1))
            kb = jnp.clip(Q_ROWS * rb - 4, 0, rows - K_ROWS)
            ks = pl.multiple_of(kb * GRID_W, GRID_W)
            ebase = kb - Q_ROWS * rb + 11
            q2 = q_ref[rs, :]
            k2 = k_ref[pl.ds(ks, TK), :]
            v2 = v_ref[pl.ds(ks, TK), :]
            bg = bg_ref[rs, :].astype(F32)
            sig = _sigmoid(bg)
            obv = ob_ref[rs, :].astype(F32)
            dout = do_ref[rs, :].astype(F32)
            dbg_ref[rs, :] = (dout * obv * (sig * (1.0 + bg * (1.0 - sig)))).astype(BF16)
            d_o = dout * (bg * sig)
            d_oo = d_o * obv
            lse2 = lse_ref[rs, :]
            dq_acc = jnp.zeros((TQ, 128), F32)
            for hh in range(2):
                msk = (lane >= HEAD_DIM) == bool(hh)
                qm = jnp.where(msk, q2, jnp.zeros_like(q2))
                lse_h = jnp.max(jnp.where(msk, lse2, -jnp.inf), axis=-1, keepdims=True)
                p = jnp.exp2(_nt(qm, k2) + bias_ref[hh] - lse_h)
                pc = jnp.exp2(_nt(qm, ck2) - lse_h)
                dom_f = jnp.where(msk, d_o, 0.0)
                dom = dom_f.astype(BF16)
                delta = jnp.sum(jnp.where(msk, d_oo, 0.0), axis=-1, keepdims=True)
                d_hi = delta.astype(BF16).astype(F32)
                x0 = HEAD_DIM * (1 - hh)
                dom_aug = jnp.where(lane == x0, -d_hi, jnp.where(lane == x0 + 1, d_hi - delta, dom_f)).astype(BF16)
                extra = jnp.logical_or(lane == x0, lane == x0 + 1)
                va = jnp.where(msk, v2, jnp.where(extra, jnp.ones_like(v2), jnp.zeros_like(v2)))
                cva = jnp.where(msk, cv2, jnp.where(extra, jnp.ones_like(cv2), jnp.zeros_like(cv2)))
                ds = p * _nt(dom_aug, va)
                dsc = pc * _nt(dom_aug, cva)
                dsb, dscb = ds.astype(BF16), dsc.astype(BF16)
                dq_h = (jnp.dot(dsb, k2, preferred_element_type=F32)
                        + jnp.dot(dscb, ck2, preferred_element_type=F32))
                dq_acc = jnp.where(msk, dq_h, dq_acc)
                dk_ref[pl.ds(ks, TK), :] += _tn(dsb, qm)
                dv_ref[pl.ds(ks, TK), :] += _tn(p.astype(BF16), dom)
                dck_ref[...] += _tn(dscb, qm)
                dcv_ref[...] += _tn(pc.astype(BF16), dom)
                for i in range(Q_ROWS):
                    for mm in range(K_ROWS // 2):
                        dacc_ref[hh, ebase + (2 * mm - i)] += ds[GRID_W * i:GRID_W * (i + 1),
                                                                 128 * mm:128 * (mm + 1)]
            dq_ref[rs, :] = dq_acc

        @pl.when(jnp.logical_and(hp == pl.num_programs(0) - 1, st == n_st - 1))
        def _():
            for r, (ex, _) in enumerate(riders):
                ex.wait(rid_src[r], rid_dst[r], *rid_sems[3 * r:3 * r + 3])

    tile = pl.BlockSpec((ATT_SUB * TQ, 128), lambda hp, st: (st, hp))
    colT = pl.BlockSpec((T, 128), lambda hp, st: (0, hp))
    colC = pl.BlockSpec((C, 128), lambda hp, st: (0, hp))
    res = pl.pallas_call(
        body, name="attn_bwd", grid=(4, n_st),
        in_specs=(_attn_specs(T, C) + [tile, tile, pl.BlockSpec((ATT_SUB * TQ, 128), lambda hp, st: (st, 4 + hp))]
                  + [ANY_SPEC] * n_rid),
        out_specs=([tile, colT, colT, colC, colC, tile,
                    pl.BlockSpec((2, N_DIAG, GRID_W, 128), lambda hp, st: (hp, 0, 0, 0))] + [ANY_SPEC] * n_rid),
        out_shape=([jax.ShapeDtypeStruct((T, 512), F32), jax.ShapeDtypeStruct((T, 512), F32),
                    jax.ShapeDtypeStruct((T, 512), F32), jax.ShapeDtypeStruct((C, 512), F32),
                    jax.ShapeDtypeStruct((C, 512), F32), jax.ShapeDtypeStruct((T, 512), BF16),
                    jax.ShapeDtypeStruct((N_HEADS, N_DIAG, GRID_W, 128), F32)] + [ex.out_shape for ex, _ in riders]),
        scratch_shapes=[pltpu.VMEM((3, 2, TQ, TK), F32)] + [s for ex, _ in riders for s in ex.scratch],
        compiler_params=_params(("arbitrary", "arbitrary")),
    )(qs, kn, z, ckn, zc, pairs, row_mask, z, ob, lse, dmc, *[a for _, a in riders])
    return res[:7], res[7:]


def _rpb_grad(dacc, hsel, fold):
    n_off = 2 * WIN_C - 1
    n_dr = 2 * WIN_R - 1

    def body(a_ref, hsel_ref, fold_ref, o_ref):
        qc = lax.broadcasted_iota(jnp.int32, (GRID_W, 128), 0)
        lane = lax.broadcasted_iota(jnp.int32, (GRID_W, 128), 1)
        diff = lane % GRID_W - qc + (WIN_C - 1)
        left = lane < GRID_W

        def by_dr(dr):
            return a_ref[0, dr + 4] + pltpu.roll(a_ref[0, dr + 3], GRID_W, 1)

        out = jnp.zeros((32, 128), F32)
        for j in range((n_dr + 1) // 2):
            hi = pltpu.roll(by_dr(2 * j + 1), GRID_W, 1) if 2 * j + 1 < n_dr else 0.0
            pair = jnp.where(left, by_dr(2 * j), hi)
            parts = []
            for o in range(n_off):
                mv = jnp.where(diff == o, pair, 0.0)
                acc = mv[0:8]
                for r8 in range(1, GRID_W // 8):
                    acc = acc + mv[8 * r8:8 * (r8 + 1)]
                parts.append(acc)
            parts.append(jnp.zeros((8, 128), F32))
            stack = jnp.concatenate(parts, axis=0)
            s_hi = stack.astype(BF16)
            s_lo = (stack - s_hi.astype(F32)).astype(BF16)
            per_o = (jnp.dot(fold_ref[...], s_hi, preferred_element_type=F32)
                     + jnp.dot(fold_ref[...], s_lo, preferred_element_type=F32))
            out = out + _dot2(per_o, hsel_ref[j])
        o_ref[0] = out

    return pl.pallas_call(
        body, name="rpb_grad", grid=(N_HEADS,),
        in_specs=[pl.BlockSpec((1, N_DIAG, GRID_W, 128), lambda h: (h, 0, 0, 0)),
                  pl.BlockSpec((8, 128, 128), lambda h: (0, 0, 0)), pl.BlockSpec((32, 256), lambda h: (0, 0))],
        out_specs=pl.BlockSpec((1, 32, 128), lambda h: (h, 0, 0)),
        out_shape=jax.ShapeDtypeStruct((N_HEADS, 32, 128), F32),
        compiler_params=_params(("arbitrary",)),
    )(dacc, hsel, fold)


def _qk_bwd(z, col_q, col_k, dqs, dk, dv, ones_blk, gq, gk, foldm, tm, name):
    T = dk.shape[0]
    with_q = dqs is not None

    def norm_bwd(raw, dn, gain, ones_ref):
        rr = lax.rsqrt(_head_sum(raw * raw, ones_ref) * (1.0 / HEAD_DIM) + EPS)
        hat = raw * rr
        dgain = jnp.sum(dn * hat, axis=0, keepdims=True)
        dhat = dn * gain
        mean = _head_sum(dhat * hat, ones_ref) * (1.0 / HEAD_DIM)
        return rr * (dhat - hat * mean), dgain

    def body(*refs):
        if with_q:
            (bq_ref, bk_ref, dq_ref, dk_ref, dv_ref, ones_ref, gq_ref, gk_ref, fold_ref,
             dbq_ref, dbk_ref, dbv_ref, dgq_ref, dgk_ref, accq, acck) = refs
        else:
            (bk_ref, dk_ref, dv_ref, ones_ref, gk_ref, fold_ref, dbk_ref, dbv_ref, dgk_ref, acck) = refs
        i = pl.program_id(0)

        @pl.when(i == 0)
        def _():
            acck[...] = jnp.zeros_like(acck)
            if with_q:
                accq[...] = jnp.zeros_like(accq)

        dbk, dgk = norm_bwd(bk_ref[...].astype(F32), dk_ref[...] * LN2, gk_ref[...], ones_ref)
        dbk_ref[...] = dbk.astype(BF16)
        acck[...] += dgk
        dbv_ref[...] = dv_ref[...].astype(BF16)
        if with_q:
            dbq, dgq = norm_bwd(bq_ref[...].astype(F32), dq_ref[...] * ATT_SCALE, gq_ref[...], ones_ref)
            dbq_ref[...] = dbq.astype(BF16)
            accq[...] += dgq

        @pl.when(i == pl.num_programs(0) - 1)
        def _():
            dgk_ref[...] = jnp.dot(jnp.broadcast_to(acck[...], (8, 512)), fold_ref[...],
                                   preferred_element_type=F32, precision=HI)
            if with_q:
                dgq_ref[...] = jnp.dot(jnp.broadcast_to(accq[...], (8, 512)), fold_ref[...],
                                       preferred_element_type=F32, precision=HI)

    row = pl.BlockSpec((tm, 512), lambda i: (i, 0))
    cst = lambda a, b: pl.BlockSpec((a, b), lambda i: (0, 0))
    out_row = jax.ShapeDtypeStruct((T, 512), BF16)
    gvec = jax.ShapeDtypeStruct((8, 128), F32)
    if with_q:
        in_specs = [pl.BlockSpec((tm, 512), lambda i: (i, col_q)), pl.BlockSpec((tm, 512), lambda i: (i, col_k)),
                    row, row, row, cst(512, 512), cst(1, 512), cst(1, 512), cst(512, 128)]
        args = (z, z, dqs, dk, dv, ones_blk, gq, gk, foldm)
        out_specs = [row, row, row, cst(8, 128), cst(8, 128)]
        out_shape = [out_row, out_row, out_row, gvec, gvec]
        scratch = [pltpu.VMEM((1, 512), F32), pltpu.VMEM((1, 512), F32)]
    else:
        in_specs = [pl.BlockSpec((tm, 512), lambda i: (i, col_k)), row, row, cst(512, 512), cst(1, 512), cst(512, 128)]
        args = (z, dk, dv, ones_blk, gk, foldm)
        out_specs = [row, row, cst(8, 128)]
        out_shape = [out_row, out_row, gvec]
        scratch = [pltpu.VMEM((1, 512), F32)]
    return pl.pallas_call(
        body, name=name, grid=(T // tm,), in_specs=in_specs, out_specs=out_specs, out_shape=out_shape,
        scratch_shapes=scratch, compiler_params=_params(("arbitrary",)),
    )(*args)


def _sgu_bwd(z, dmc, sgn, ws, wst, bsb, ones8, tm):
    T = z.shape[0]

    def body(au_ref, av_ref, ag_ref, d_ref, sg_ref, ws_ref, wst_ref, bsb_ref, ones_ref,
             dau_ref, dav_ref, dag_ref, dws_ref, dbs_ref, dsg_ref):
        @pl.when(pl.program_id(0) == 0)
        def _():
            dws_ref[...] = jnp.zeros_like(dws_ref)
            dbs_ref[...] = jnp.zeros_like(dbs_ref)
            dsg_ref[...] = jnp.zeros_like(dsg_ref)

        for g in range(SGU_GROUPS):
            ws_bf = ws_ref[g].astype(BF16)
            wst_bf = wst_ref[g].astype(BF16)
            sg = sg_ref[:, 128 * g:128 * (g + 1)]
            bsb_g = bsb_ref[g]
            for j in range(tm // CHUNK):
                rs, cs = slice(CHUNK * j, CHUNK * (j + 1)), slice(128 * g, 128 * (g + 1))
                au, av, ag = (au_ref[rs, cs].astype(F32), av_ref[rs, cs].astype(F32), ag_ref[rs, cs].astype(F32))
                d = d_ref[rs, cs].astype(F32)
                _, (gu, dgu, dgv, rr, vhat, vn, mixed, sig, sl) = _sgu_chunk_fwd(au, av, ag, sg, ws_bf, bsb_g)
                dau_ref[rs, cs] = (d * mixed * sl * dgu).astype(BF16)
                dag_ref[rs, cs] = (d * gu * mixed * (sig * (1.0 + ag * (1.0 - sig)))).astype(BF16)
                dmixed = d * gu * sl
                dmb = dmixed.astype(BF16)
                dm_lo = (dmixed - dmb.astype(F32)).astype(BF16)
                dbs_ref[g] += _nt(ones_ref[...], dmb) + _nt(ones_ref[...], dm_lo)
                dws_ref[g] += _nt(dmb, vn.astype(BF16))
                dvn = jnp.dot(wst_bf, dmb, preferred_element_type=F32)
                dsg_ref[:, 128 * g:128 * (g + 1)] += jnp.sum(dvn * vhat, axis=0, keepdims=True)
                dvhat = dvn * sg
                mean = jnp.mean(dvhat * vhat, axis=-1, keepdims=True)
                dav_ref[rs, cs] = (rr * (dvhat - vhat * mean) * dgv).astype(BF16)

    row = lambda col: pl.BlockSpec((tm, 512), lambda i: (i, col))
    c3 = lambda n: pl.BlockSpec((SGU_GROUPS, CHUNK, n), lambda i: (0, 0, 0))
    out_row = jax.ShapeDtypeStruct((T, 512), BF16)
    return pl.pallas_call(
        body, name="sgu_bwd", grid=(T // tm,),
        in_specs=[row(0), row(1), row(2), row(0), pl.BlockSpec((1, 512), lambda i: (0, 0)),
                  c3(CHUNK), c3(CHUNK), c3(128), pl.BlockSpec((8, 128), lambda i: (0, 0))],
        out_specs=[row(0), row(0), row(0), c3(CHUNK), pl.BlockSpec((SGU_GROUPS, 8, CHUNK), lambda i: (0, 0, 0)),
                   pl.BlockSpec((1, 512), lambda i: (0, 0))],
        out_shape=[out_row, out_row, out_row, jax.ShapeDtypeStruct((SGU_GROUPS, CHUNK, CHUNK), F32),
                   jax.ShapeDtypeStruct((SGU_GROUPS, 8, CHUNK), F32), jax.ShapeDtypeStruct((1, 512), F32)],
        compiler_params=_params(("arbitrary",)),
    )(z, z, z, dmc, sgn, ws, wst, bsb, ones8)


def _inproj_bwd_dx(dzs, w_cols, w_in_t, x, dy, ng, scale, shift, tm, name, riders=()):
    T = x.shape[0]
    n = len(dzs)
    nt = T // tm
    with_dx = dy is not None
    n_own_in = n + 5 + with_dx
    n_own_out = 3 + with_dx
    n_rid = len(riders)

    def body(*refs):
        dz_refs = refs[:n]
        own = refs[n:n_own_in] + refs[n_own_in + n_rid:n_own_in + n_rid + n_own_out]
        rid_src = refs[n_own_in:n_own_in + n_rid]
        rid_dst = refs[n_own_in + n_rid + n_own_out:n_own_in + 2 * n_rid + n_own_out]
        rid_sems = refs[n_own_in + 2 * n_rid + n_own_out:]
        if with_dx:
            w_ref, x_ref, dy_ref, g_ref, sc_ref, sh_ref, gx_ref, dsh_ref, dsc_ref, dg_ref = own
        else:
            w_ref, x_ref, g_ref, sc_ref, sh_ref, dsh_ref, dsc_ref, dg_ref = own

        @pl.when(pl.program_id(0) == 0)
        def _():
            for r, (ex, _) in enumerate(riders):
                ex.start(rid_src[r], rid_dst[r], *rid_sems[3 * r:3 * r + 3])
            dsh_ref[...] = jnp.zeros_like(dsh_ref)
            dsc_ref[...] = jnp.zeros_like(dsc_ref)
            dg_ref[...] = jnp.zeros_like(dg_ref)

        dh = jnp.dot(dz_refs[0][...], w_ref[0:512, :], preferred_element_type=F32)
        for k in range(1, n):
            dh = dh + jnp.dot(dz_refs[k][...], w_ref[512 * k:512 * (k + 1), :], preferred_element_type=F32)
        xv = x_ref[...]
        r = lax.rsqrt(jnp.mean(xv * xv, axis=-1, keepdims=True) + EPS)
        xn = xv * r
        gv, op = g_ref[...], 1.0 + sc_ref[...]
        dsh_ref[...] += jnp.sum(dh, axis=0, keepdims=True)
        dsc_ref[...] += jnp.sum(dh * xn * gv, axis=0, keepdims=True)
        dg_ref[...] += jnp.sum(dh * op * xn, axis=0, keepdims=True)
        if with_dx:
            dxn = dh * (gv * op)
            gx_ref[...] = r * (dxn - xn * jnp.mean(dxn * xn, axis=-1, keepdims=True)) + dy_ref[...]

        @pl.when(pl.program_id(0) == nt - 1)
        def _():
            for r, (ex, _) in enumerate(riders):
                ex.wait(rid_src[r], rid_dst[r], *rid_sems[3 * r:3 * r + 3])

    vec = pl.BlockSpec((1, D_MODEL), lambda i: (0, 0))
    rowf = pl.BlockSpec((tm, D_MODEL), lambda i: (i, 0))
    in_specs = [pl.BlockSpec((tm, 512), lambda i: (i, 0))] * n
    in_specs += [pl.BlockSpec((512 * n, D_MODEL), lambda i: (w_cols // n, 0)), rowf]
    args = list(dzs) + [w_in_t, x]
    vshape = jax.ShapeDtypeStruct((1, D_MODEL), F32)
    out_specs, out_shape = [vec, vec, vec], [vshape, vshape, vshape]
    if with_dx:
        in_specs.append(rowf)
        args.append(dy)
        out_specs = [rowf] + out_specs
        out_shape = [jax.ShapeDtypeStruct((T, D_MODEL), F32)] + out_shape
    in_specs += [vec, vec, vec] + [ANY_SPEC] * n_rid
    args += [ng, scale, shift] + [a for _, a in riders]
    res = pl.pallas_call(
        body, name=name, grid=(nt,), in_specs=in_specs, out_specs=out_specs + [ANY_SPEC] * n_rid,
        out_shape=out_shape + [ex.out_shape for ex, _ in riders],
        scratch_shapes=[s for ex, _ in riders for s in ex.scratch],
        compiler_params=_params(("arbitrary",)),
    )(*args)
    return res[:n_own_out], res[n_own_out:]


def _inproj_bwd_dw(h, dzs, hc, dzc_k, dzc_v, tk, riders=()):
    T = h.shape[0]
    nt = T // tk
    n = len(dzs)
    n_rid = len(riders)

    def body(*refs):
        h_ref = refs[0]
        dz_refs = refs[1:1 + n]
        hc_ref, dzck_ref, dzcv_ref = refs[1 + n:4 + n]
        rid_src = refs[4 + n:4 + n + n_rid]
        o_ref = refs[4 + n + n_rid]
        rid_dst = refs[5 + n + n_rid:5 + n + 2 * n_rid]
        acc = refs[5 + n + 2 * n_rid]
        rid_sems = refs[6 + n + 2 * n_rid:]
        t = pl.program_id(0)

        @pl.when(t == 0)
        def _():
            for r, (ex, _) in enumerate(riders):
                ex.start(rid_src[r], rid_dst[r], *rid_sems[3 * r:3 * r + 3])
            acc[...] = jnp.zeros_like(acc)
            acc[512 * 4:512 * 5, :] = _tn(dzck_ref[...], hc_ref[...])
            acc[512 * 5:512 * 6, :] = _tn(dzcv_ref[...], hc_ref[...])

        hv = h_ref[...]
        for k in range(n):
            acc[512 * k:512 * (k + 1), :] += _tn(dz_refs[k][...], hv)

        @pl.when(t == nt - 1)
        def _():
            o_ref[...] = acc[...].astype(BF16)
            for r, (ex, _) in enumerate(riders):
                ex.wait(rid_src[r], rid_dst[r], *rid_sems[3 * r:3 * r + 3])

    whole = lambda a: pl.BlockSpec(a.shape, lambda t: (0, 0))
    res = pl.pallas_call(
        body, name="inproj_bwd_dw", grid=(nt,),
        in_specs=([pl.BlockSpec((tk, D_MODEL), lambda t: (t, 0))] + [pl.BlockSpec((tk, 512), lambda t: (t, 0))] * n
                  + [whole(hc), whole(dzc_k), whole(dzc_v)] + [ANY_SPEC] * n_rid),
        out_specs=[pl.BlockSpec((512 * n, D_MODEL), lambda t: (0, 0))] + [ANY_SPEC] * n_rid,
        out_shape=[jax.ShapeDtypeStruct((512 * n, D_MODEL), BF16)] + [ex.out_shape for ex, _ in riders],
        scratch_shapes=[pltpu.VMEM((512 * n, D_MODEL), F32)] + [s for ex, _ in riders for s in ex.scratch],
        compiler_params=_params(("arbitrary",)),
    )(h, *dzs, hc, dzc_k, dzc_v, *[a for _, a in riders])
    return res[0], res[1:]


def _adamw_sharded(w, gparts, m, v, tr, name):
    R, C = w.shape
    n_part = gparts.shape[0]

    def body(w_ref, gp_ref, m_ref, v_ref, g_ref, d_ref, m2_ref, v2_ref):
        g = gp_ref[0].astype(F32)
        for d in range(1, n_part):
            g = g + gp_ref[d].astype(F32)
        delta, m2, v2 = _adam(w_ref[...], g, m_ref[...], v_ref[...])
        g_ref[...] = g
        d_ref[...] = delta
        m2_ref[...] = m2
        v2_ref[...] = v2

    row = pl.BlockSpec((tr, C), lambda i: (i, 0))
    sh = jax.ShapeDtypeStruct((R, C), F32)
    return pl.pallas_call(
        body, name=name, grid=(R // tr,),
        in_specs=[row, pl.BlockSpec((n_part, tr, C), lambda i: (0, i, 0)), row, row],
        out_specs=[row, row, row, row], out_shape=[sh, sh, sh, sh],
        compiler_params=_params(("arbitrary",)),
    )(w, gparts, m, v)


def _pack_vectors(vec_rows, dsg, dgq, dgk, dgk_c, loss_part):
    n_vec = len(vec_rows)

    def body(*refs):
        vecs = refs[:n_vec]
        dsg_ref, dgq_ref, dgk_ref, dgkc_ref, loss_ref, v_ref = refs[n_vec:]
        row = lax.broadcasted_iota(jnp.int32, (16, D_MODEL), 0)
        misc = jnp.concatenate([dsg_ref[...], dgq_ref[0:1, :], dgk_ref[0:1, :], dgkc_ref[0:1, :],
                                loss_ref[...]], axis=1)
        v = jnp.where(row == V_MISC, jnp.broadcast_to(misc, (16, D_MODEL)), 0.0)
        for (r, _), ref in zip(vec_rows, vecs):
            v = jnp.where(row == r, jnp.broadcast_to(ref[...], (16, D_MODEL)), v)
        v_ref[...] = v

    return pl.pallas_call(
        body, name="pack_vectors", out_shape=jax.ShapeDtypeStruct((16, D_MODEL), F32), compiler_params=_params(),
    )(*[a for _, a in vec_rows], dsg, dgq, dgk, dgk_c, loss_part)


def _pack_matrices(dws, dbs, drpb):
    def body(dws_ref, dbs_ref, drpb_ref, m_ref):
        for g in range(SGU_GROUPS):
            m_ref[128 * g:128 * (g + 1), :] = dws_ref[g]
            m_ref[M_DBS + 8 * g:M_DBS + 8 * (g + 1), :] = dbs_ref[g]
        for hd in range(N_HEADS):
            m_ref[M_RPB + 32 * hd:M_RPB + 32 * (hd + 1), :] = drpb_ref[hd]

    return pl.pallas_call(
        body, name="pack_matrices", out_shape=jax.ShapeDtypeStruct((M_ROWS, 128), F32), compiler_params=_params(),
    )(dws, dbs, drpb)


SMALL_NAMES = ("b_ada", "norm_g", "sgu_norm_g", "w_spatial", "b_spatial", "q_norm_g", "k_norm_g", "rpb")


def _adamw_small(vg, mg, ws, ms, vs):
    k = len(SMALL_NAMES)

    def body(*refs):
        vg_ref, mg_ref = refs[0], refs[1]
        w_refs = dict(zip(SMALL_NAMES, refs[2:2 + k]))
        m_refs = dict(zip(SMALL_NAMES, refs[2 + k:2 + 2 * k]))
        v_refs = dict(zip(SMALL_NAMES, refs[2 + 2 * k:2 + 3 * k]))
        o_refs = [dict(zip(SMALL_NAMES, refs[2 + (3 + i) * k:2 + (4 + i) * k])) for i in range(4)]
        loss_ref = refs[2 + 7 * k]

        sv = vg_ref[0]
        for d in range(1, N_DEV):
            sv = sv + vg_ref[d]
        loss_ref[...] = sv[V_MISC:V_MISC + 1, 896:1024]

        def total(lo, hi):
            s = mg_ref[0, lo:hi, :]
            for d in range(1, N_DEV):
                s = s + mg_ref[d, lo:hi, :]
            return s

        def emit(name, idx, g):
            res = _adam(w_refs[name][idx], g, m_refs[name][idx], v_refs[name][idx])
            for o, val in zip(o_refs, (g,) + res):
                o[name][idx] = val

        everything = (slice(None), slice(None))
        row = lambda r: sv[r:r + 1, :]
        emit("b_ada", everything, jnp.concatenate(
            [row(V_DSHIFT) + row(V_DCSHIFT), row(V_DSCALE) + row(V_DCSCALE), row(V_DGATE)], axis=1))
        emit("norm_g", everything, row(V_DNG) + row(V_DNG_CTX))
        misc = row(V_MISC)
        emit("sgu_norm_g", everything, misc[:, 0:512])
        emit("q_norm_g", everything, misc[:, 512:512 + HEAD_DIM])
        emit("k_norm_g", everything, misc[:, 640:640 + HEAD_DIM] + misc[:, 768:768 + HEAD_DIM])
        for g in range(SGU_GROUPS):
            emit("w_spatial", (0, g), total(128 * g, 128 * (g + 1)))
            emit("b_spatial", (0, slice(g, g + 1), slice(None)), total(M_DBS + 8 * g, M_DBS + 8 * (g + 1))[0:1, :])
        for hd in range(N_HEADS):
            by_dc = total(M_RPB + 32 * hd, M_RPB + 32 * (hd + 1))
            emit("rpb", (0, hd), by_dc.T[0:2 * WIN_R - 1, 0:2 * WIN_C - 1])

    shapes = [jax.ShapeDtypeStruct(w.shape, F32) for w in ws]
    res = pl.pallas_call(body, name="adamw_small", out_shape=shapes * 4 + [jax.ShapeDtypeStruct((1, 128), F32)],
                         compiler_params=_params())(vg, mg, *ws, *ms, *vs)
    return [res[i * k:(i + 1) * k] for i in range(4)], res[4 * k]


def _adamw_cctx(pc_g, w, m, v):
    def body(pc_ref, w_ref, m_ref, v_ref, g_ref, d_ref, m2_ref, v2_ref):
        pc = pc_ref[0, 0:1, :]
        for d in range(1, N_DEV):
            pc = pc + pc_ref[d, 0:1, :]
        cc = w_ref[...]
        sig = _sigmoid(cc)
        g = pc * (sig * (1.0 + cc * (1.0 - sig)))
        delta, m2, v2 = _adam(cc, g, m_ref[...], v_ref[...])
        g_ref[...] = g
        d_ref[...] = delta
        m2_ref[...] = m2
        v2_ref[...] = v2

    sh = jax.ShapeDtypeStruct((1, D_MODEL), F32)
    return pl.pallas_call(body, name="adamw_cctx", out_shape=[sh, sh, sh, sh], compiler_params=_params())(
        pc_g, w, m, v)


def _block_ones(n, blk):
    i = np.arange(n)
    return jnp.asarray((i[:, None] // blk == i[None, :] // blk).astype(np.float32), BF16)


def _rpb_pairs(rpb):
    n_off = 2 * WIN_C - 1
    cols = np.arange(GRID_W)
    c0 = np.clip(cols - WIN_C // 2, 0, GRID_W - WIN_C)
    in_win = (cols[None, :] >= c0[:, None]) & (cols[None, :] < c0[:, None] + WIN_C)
    dc = np.clip(cols[None, :] - cols[:, None] + (WIN_C - 1), 0, n_off - 1)
    expand = (dc[None] == np.arange(n_off)[:, None, None]) & in_win[None]
    toep = jnp.einsum("hrd,dqk->hrqk", rpb, jnp.asarray(expand, F32), precision=HI)
    toep = toep + jnp.asarray(np.where(in_win, 0.0, NEG_INF).astype(np.float32))
    neg = jnp.full((N_HEADS, 1, GRID_W, GRID_W), NEG_INF, F32)
    ext = jnp.concatenate([neg, toep, neg], axis=1)
    return jnp.concatenate([ext[:, :-1], ext[:, 1:]], axis=-1)


def _row_mask(rows):
    nrb = rows // Q_ROWS
    valid = np.zeros((3, Q_ROWS, 1, K_ROWS, 1), bool)
    for t, rb in enumerate((0, 1, nrb - 1)):
        kb = int(np.clip(Q_ROWS * rb - 4, 0, rows - K_ROWS))
        for i in range(Q_ROWS):
            r0 = int(np.clip(Q_ROWS * rb + i - WIN_R // 2, 0, rows - WIN_R))
            for j in range(K_ROWS):
                valid[t, i, 0, j, 0] = r0 <= kb + j < r0 + WIN_R
    full = np.broadcast_to(valid, (3, Q_ROWS, GRID_W, K_ROWS, GRID_W)).reshape(3, TQ, TK)
    return jnp.asarray(np.where(full, 0.0, NEG_INF).astype(np.float32))


def kernel(x, c, ctx, c_ctx, w_ada, b_ada, norm_g, w_in, sgu_norm_g, w_spatial, b_spatial, q_norm_g, k_norm_g, rpb, w_out, loss_target, m_c_ctx, m_w_ada, m_b_ada, m_norm_g, m_w_in, m_sgu_norm_g, m_w_spatial, m_b_spatial, m_q_norm_g, m_k_norm_g, m_rpb, m_w_out, v_c_ctx, v_w_ada, v_b_ada, v_norm_g, v_w_in, v_sgu_norm_g, v_w_spatial, v_b_spatial, v_q_norm_g, v_k_norm_g, v_rpb, v_w_out):
    me = 4 * lax.axis_index("x") + 2 * lax.axis_index("y") + lax.axis_index("c")
    x2, ctx2, tgt2 = x[0], ctx[0], loss_target[0]
    T, C = x2.shape[0], ctx2.shape[0]
    rows = T // GRID_W
    wada, win_t, wout = w_ada[0], w_in[0].T, w_out[0]
    ada_w = wada.shape[1]
    win_w = win_t.shape[0]

    row8 = lax.broadcasted_iota(jnp.int32, (8, D_MODEL), 0)
    c_blk = jnp.where(row8 == me, jnp.broadcast_to(c, (8, D_MODEL)), 0.0)
    b_sh = lax.dynamic_slice(b_ada, (0, me * ada_w), (1, ada_w))
    c_ctx_row = c_ctx.reshape(1, D_MODEL)
    s16, part_g = _ada_fwd(c_blk, c_ctx_row, wada, b_sh)
    mod16 = part_g.transpose(1, 0, 2).reshape(16, 3 * D_MODEL)
    mod = lax.dynamic_slice(mod16, (me, 0), (1, 3 * D_MODEL))
    shift, scale, gate = mod[:, :D_MODEL], mod[:, D_MODEL:2 * D_MODEL], mod[:, 2 * D_MODEL:]
    cshift, cscale = mod16[8:9, :D_MODEL], mod16[8:9, D_MODEL:2 * D_MODEL]

    ones512 = _block_ones(512, HEAD_DIM)
    ones8 = jnp.ones((8, 128), BF16)
    foldm = jnp.asarray((np.arange(512)[:, None] % HEAD_DIM == np.arange(128)[None, :]).astype(np.float32))
    lane_half = np.arange(128)[None, :, None] // GRID_W
    hsel = jnp.asarray((2 * np.arange(8)[:, None, None] + lane_half == np.arange(128)[None, None, :]).astype(np.float32),
                       BF16)
    foldr = jnp.asarray((np.arange(256)[None, :] // 8 == np.arange(32)[:, None]).astype(np.float32), BF16)
    gq512 = jnp.tile(q_norm_g, (1, N_HEADS))
    gk512 = jnp.tile(k_norm_g, (1, N_HEADS))
    ws = w_spatial[0]
    wst = ws.transpose(0, 2, 1)
    bsb = jnp.broadcast_to(b_spatial[0][:, :, None], (SGU_GROUPS, CHUNK, 128))
    pairs = _rpb_pairs(rpb[0])
    row_mask = _row_mask(rows)

    my_chip = me // 2
    order = jnp.stack([my_chip, my_chip ^ 2, my_chip ^ 1, my_chip ^ 3]).astype(jnp.int32)
    h, z, win_g, wout_g = _inproj_fwd(order, x2, norm_g, scale, shift, win_t.astype(BF16), wout.astype(BF16), 512)
    w_in_b = win_g.reshape(D_IN, D_MODEL)
    w_out_b = wout_g.reshape(D_MODEL, D_MODEL)
    qs, kn = _qk_norm(z, ones512, gq512, gk512, 512)
    hc, zc, ckn = _ctx_fwd(ctx2, norm_g, cscale, cshift, w_in_b, ones512, gk512)
    out_a = _sgu_fwd(z, sgu_norm_g, ws, bsb, 512)
    ob, out_b, lse = _attn_fwd(qs, kn, z, ckn, zc, pairs, row_mask)

    dy, dmc, dw_out, dgate, loss_part = _outproj_loss_bwd(x2, tgt2, out_a, out_b, gate, w_out_b, 512)
    dw_out_blocks = dw_out.reshape(N_DEV, D_MODEL // N_DEV, D_MODEL)
    (dqs, dk, dv, dck, dcv, db_g, dacc), (gout_parts,) = _attn_bwd(
        qs, kn, z, ckn, zc, pairs, row_mask, ob, lse, dmc, [(_Hosted("a2a", dw_out_blocks), dw_out_blocks)])
    drpb = _rpb_grad(dacc, hsel, foldr)
    db_q, db_k, db_v, dgq, dgk = _qk_bwd(z, 3, 4, dqs, dk, dv, ones512, gq512, gk512, foldm, 512, "qk_bwd")
    dzc_k, dzc_v, dgk_c = _qk_bwd(zc, None, 0, None, dck, dcv, ones512, None, gk512, foldm, C, "ctx_k_bwd")
    da_u, da_v, da_g, dws, dbs, dsg = _sgu_bwd(z, dmc, sgu_norm_g, ws, wst, bsb, ones8, 512)
    dzs = [da_u, da_v, da_g, db_q, db_k, db_v, db_g]
    mloc = _pack_matrices(dws, dbs, drpb)
    dw_in_t, (mg,) = _inproj_bwd_dw(h, dzs, hc, dzc_k, dzc_v, 512, [(_Hosted("ag", mloc), mloc)])
    (chip_sums,) = _chip_presum([dw_in_t.reshape(N_DEV, win_w, D_MODEL)], "presum_dw_in")
    (grad_x, dshift, dscale, dng), (gin_parts,) = _inproj_bwd_dx(
        dzs, 0, w_in_b, x2, dy, norm_g, scale, shift, 512, "inproj_bwd_dx", [(_Hosted("chips", chip_sums), chip_sums)])
    (dcshift, dcscale, dng_c), _ = _inproj_bwd_dx([dzc_k, dzc_v], 4, w_in_b, ctx2, None, norm_g, cscale, cshift, C,
                                                  "ctx_bwd_dx")
    res_in = _adamw_sharded(win_t, gin_parts, m_w_in[0].T, v_w_in[0].T, 112, "adamw_w_in")
    res_out = _adamw_sharded(wout, gout_parts, m_w_out[0], v_w_out[0], 128, "adamw_w_out")

    zero_row = jnp.zeros((1, D_MODEL), F32)
    vec_rows = [(V_DSHIFT, dshift), (V_DSCALE, dscale), (V_DGATE, dgate), (V_DCSHIFT, dcshift),
                (V_DCSCALE, dcscale), (V_ZERO, zero_row), (V_DNG, dng), (V_DNG_CTX, dng_c)]
    vloc = _pack_vectors(vec_rows, dsg, dgq, dgk, dgk_c, loss_part)
    (vg,) = _allgather([vloc], "gather_small")
    small_w = (b_ada, norm_g, sgu_norm_g, w_spatial, b_spatial, q_norm_g, k_norm_g, rpb)
    small_m = (m_b_ada, m_norm_g, m_sgu_norm_g, m_w_spatial, m_b_spatial, m_q_norm_g, m_k_norm_g, m_rpb)
    small_v = (v_b_ada, v_norm_g, v_sgu_norm_g, v_w_spatial, v_b_spatial, v_q_norm_g, v_k_norm_g, v_rpb)
    res_small, loss_row = _adamw_small(vg, mg, small_w, small_m, small_v)

    dm_all = vg[:, V_DSHIFT:V_DGATE + 1, :].reshape(N_DEV, 3 * D_MODEL)
    dc_all = vg[:, V_DCSHIFT:V_ZERO + 1, :].reshape(N_DEV, 3 * D_MODEL)
    dm_sh = lax.dynamic_slice(dm_all, (0, me * ada_w), (N_DEV, ada_w))
    dc_sh = lax.dynamic_slice(dc_all, (0, me * ada_w), (N_DEV, ada_w))
    *res_ada, pc = _ada_bwd(s16, dm_sh, dc_sh, wada, m_w_ada[0], v_w_ada[0])
    (pc_g,) = _allgather([pc], "gather_cctx")
    res_cctx = _adamw_cctx(pc_g, c_ctx_row, m_c_ctx.reshape(1, D_MODEL), v_c_ctx.reshape(1, D_MODEL))

    loss = loss_row[0, 0]
    outs = [loss, grad_x[None]]
    for kind in range(4):
        by_name = dict(zip(SMALL_NAMES, res_small[kind]))
        by_name.update(c_ctx=res_cctx[kind].reshape(D_MODEL), w_ada=res_ada[kind][None],
                       w_in=res_in[kind].T[None], w_out=res_out[kind][None])
        outs += [by_name[nme] for nme in ("c_ctx", "w_ada", "b_ada", "norm_g", "w_in", "sgu_norm_g", "w_spatial",
                                          "b_spatial", "q_norm_g", "k_norm_g", "rpb", "w_out")]
    return tuple(outs)
```

```python
import functools

import numpy as np
import jax
import jax.numpy as jnp
from jax import lax
from jax.experimental import pallas as pl
from jax.experimental.pallas import tpu as pltpu

F32 = jnp.float32
BF16 = jnp.bfloat16
HI = lax.Precision.HIGHEST

N_DEV = 8
D_MODEL = 1024
D_A = 512
D_B = 512
D_IN = 3584
N_BRANCH = 7
HEAD_DIM = 64
N_HEADS = 8
GRID_W = 64
WIN_R = 8
WIN_C = 16
CHUNK = 128
SGU_GROUPS = 4
EPS = 1e-6
NEG_INF = -1e30
Q_ROWS = 4
K_ROWS = 12
TQ = Q_ROWS * GRID_W
TK = K_ROWS * GRID_W
N_DIAG = 22
ATT_SUB = 4
ATT_SCALE = HEAD_DIM ** -0.5
LOG2E = 1.4426950408889634
LN2 = 0.6931471805599453

ADAM_LR = 0.001
ADAM_B1 = 0.9
ADAM_B2 = 0.999
ADAM_EPS = 1e-08
ADAM_WD = 0.01
ADAM_STEP = 10

VMEM_LIMIT = 56 * 1024 * 1024
MESH = pl.DeviceIdType.MESH

V_DSHIFT, V_DSCALE, V_DGATE, V_DCSHIFT, V_DCSCALE, V_ZERO, V_DNG, V_DNG_CTX, V_MISC = range(9)
M_DBS, M_RPB, M_ROWS = 512, 544, 800


def _params(sem=None):
    return pltpu.CompilerParams(dimension_semantics=sem, vmem_limit_bytes=VMEM_LIMIT)


def _sigmoid(x):
    return 1.0 / (1.0 + jnp.exp(-x))


def _gelu_parts(x):
    cdf = 0.5 * (1.0 + lax.erf(x * 0.7071067811865476))
    pdf = jnp.exp(-0.5 * x * x) * 0.3989422804014327
    return x * cdf, cdf + x * pdf


def _nt(a, b):
    return lax.dot_general(a, b, (((1,), (1,)), ((), ())), preferred_element_type=F32)


def _tn(a, b):
    return lax.dot_general(a, b, (((0,), (0,)), ((), ())), preferred_element_type=F32)


def _dot2(v, ones_bf):
    hi = v.astype(BF16)
    lo = (v - hi.astype(F32)).astype(BF16)
    return (jnp.dot(hi, ones_bf, preferred_element_type=F32)
            + jnp.dot(lo, ones_bf, preferred_element_type=F32))


def _head_sum(v, ones_ref):
    return jnp.dot(v.astype(BF16), ones_ref[...], preferred_element_type=F32)


def _adam(w, g, m, v):
    m2 = ADAM_B1 * m + (1.0 - ADAM_B1) * g
    v2 = ADAM_B2 * v + (1.0 - ADAM_B2) * (g * g)
    m_hat = m2 / (1.0 - ADAM_B1 ** ADAM_STEP)
    v_hat = v2 / (1.0 - ADAM_B2 ** ADAM_STEP)
    delta = -ADAM_LR * (m_hat / (jnp.sqrt(v_hat) + ADAM_EPS) + ADAM_WD * w)
    return delta, m2, v2


def _allgather(arrs, name):
    n = len(arrs)
    outs_shape = [jax.ShapeDtypeStruct((N_DEV,) + tuple(a.shape), a.dtype) for a in arrs]

    def body(*refs):
        ins, outs = refs[:n], refs[n:2 * n]
        send_sems, recv_sems, loc_sems = refs[2 * n:]
        x, y, c = lax.axis_index("x"), lax.axis_index("y"), lax.axis_index("c")
        me, sib = (x, y, c), (x, y, 1 - c)
        chips = [(1 - x, y), (x, 1 - y), (1 - x, 1 - y)]

        def slot(px, py, pc):
            return 4 * px + 2 * py + pc

        def copy(a, k, block, to, src=None):
            return pltpu.make_async_remote_copy(
                src_ref=outs[a].at[slot(*block)] if src is None else src, dst_ref=outs[a].at[slot(*block)],
                send_sem=send_sems.at[7 * a + k], recv_sem=recv_sems.at[7 * a + k],
                device_id=to, device_id_type=MESH)

        started, locs = [], []
        for a in range(n):
            lc = pltpu.make_async_copy(ins[a], outs[a].at[slot(*me)], loc_sems.at[a])
            lc.start()
            locs.append(lc)
            for j, chip in enumerate(chips):
                started.append(copy(a, 1 + j, me, (*chip, c), src=ins[a]))
            started.append(copy(a, 0, me, sib, src=ins[a]))
        for cp in started:
            cp.start()
        for j, chip in enumerate(chips):
            for a in range(n):
                copy(a, 1 + j, (*chip, c), me).wait_recv()
                fwd = copy(a, 4 + j, (*chip, c), sib)
                fwd.start()
                started.append(fwd)
        for a in range(n):
            copy(a, 0, sib, me).wait_recv()
            for j, chip in enumerate(chips):
                copy(a, 4 + j, (*chip, 1 - c), me).wait_recv()
        for cp in started:
            cp.wait_send()
        for lc in locs:
            lc.wait()

    res = pl.pallas_call(
        body, name=name, out_shape=outs_shape,
        in_specs=[pl.BlockSpec(memory_space=pl.ANY)] * n,
        out_specs=[pl.BlockSpec(memory_space=pl.ANY)] * n,
        scratch_shapes=[pltpu.SemaphoreType.DMA((7 * n,)), pltpu.SemaphoreType.DMA((7 * n,)),
                        pltpu.SemaphoreType.DMA((n,))],
    )(*arrs)
    return list(res)


def _chip_presum(arrs, name):
    n = len(arrs)
    n_chip = N_DEV // 2
    outs_shape = [jax.ShapeDtypeStruct((n_chip,) + tuple(a.shape[1:]), a.dtype) for a in arrs]

    def body(*refs):
        ins, sums = refs[:n], refs[n:2 * n]
        tmps = refs[2 * n:3 * n]
        s1, r1 = refs[3 * n:]
        c = lax.axis_index("c")
        sib = (lax.axis_index("x"), lax.axis_index("y"), 1 - c)
        all_chips = [(0, 0), (0, 1), (1, 0), (1, 1)]

        swaps = []
        for a in range(n):
            for q, (qx, qy) in enumerate(all_chips):
                cp = pltpu.make_async_remote_copy(
                    src_ref=ins[a].at[4 * qx + 2 * qy + (1 - c)], dst_ref=tmps[a].at[q],
                    send_sem=s1.at[n_chip * a + q], recv_sem=r1.at[n_chip * a + q],
                    device_id=sib, device_id_type=MESH)
                cp.start()
                swaps.append(cp)
        for a in range(n):
            for q, (qx, qy) in enumerate(all_chips):
                swaps[n_chip * a + q].wait_recv()
                both = ins[a][4 * qx + 2 * qy + c].astype(F32) + tmps[a][q].astype(F32)
                sums[a][q] = both.astype(sums[a].dtype)
        for cp in swaps:
            cp.wait_send()

    vm = pl.BlockSpec(memory_space=pltpu.VMEM)
    res = pl.pallas_call(
        body, name=name, out_shape=outs_shape, in_specs=[vm] * n, out_specs=[vm] * n,
        scratch_shapes=([pltpu.VMEM(s.shape, s.dtype) for s in outs_shape]
                        + [pltpu.SemaphoreType.DMA((n_chip * n,)), pltpu.SemaphoreType.DMA((n_chip * n,))]),
        compiler_params=_params(),
    )(*arrs)
    return list(res)


class _Hosted:
    def __init__(self, kind, src):
        self.kind = kind
        n_slot = {"a2a": N_DEV, "ag": N_DEV, "chips": N_DEV // 2}[kind]
        blk = src.shape if kind == "ag" else src.shape[1:]
        self.out_shape = jax.ShapeDtypeStruct((n_slot,) + tuple(blk), src.dtype)
        self.n_peer = n_slot - 1
        self.scratch = [pltpu.SemaphoreType.DMA((self.n_peer,)), pltpu.SemaphoreType.DMA((self.n_peer,)),
                        pltpu.SemaphoreType.DMA]

    def _copies(self, src, dst, send_sems, recv_sems, loc_sem, landing):
        x, y, c = lax.axis_index("x"), lax.axis_index("y"), lax.axis_index("c")
        if self.kind == "chips":
            me = 2 * x + y
            peers = [((px, py, c), 2 * px + py) for px, py in ((1 - x, y), (x, 1 - y), (1 - x, 1 - y))]
        else:
            me = 4 * x + 2 * y + c
            peers = []
            for k in range(1, N_DEV):
                px = 1 - x if (k >> 2) & 1 else x
                py = 1 - y if (k >> 1) & 1 else y
                pc = 1 - c if k & 1 else c
                peers.append(((px, py, pc), 4 * px + 2 * py + pc))
        remote = []
        for k, (peer, pid) in enumerate(peers):
            s = src if self.kind == "ag" else src.at[pid]
            remote.append(pltpu.make_async_remote_copy(
                src_ref=s, dst_ref=dst.at[pid if landing else me],
                send_sem=send_sems.at[k], recv_sem=recv_sems.at[k], device_id=peer, device_id_type=MESH))
        local = pltpu.make_async_copy(src if self.kind == "ag" else src.at[me], dst.at[me], loc_sem)
        return remote, local

    def start(self, src, dst, send_sems, recv_sems, loc_sem):
        remote, local = self._copies(src, dst, send_sems, recv_sems, loc_sem, landing=False)
        for cp in remote:
            cp.start()
        local.start()

    def wait(self, src, dst, send_sems, recv_sems, loc_sem):
        remote, local = self._copies(src, dst, send_sems, recv_sems, loc_sem, landing=True)
        for cp in remote:
            cp.wait_recv()
        for cp in remote:
            cp.wait_send()
        local.wait()


ANY_SPEC = pl.BlockSpec(memory_space=pl.ANY)


def _ada_scratch(n_col):
    return ([pltpu.VMEM((N_DEV, 8, D_MODEL), F32), pltpu.VMEM((16, n_col), F32), pltpu.VMEM((N_DEV, 16, n_col), F32)]
            + [pltpu.SemaphoreType.DMA((N_DEV - 1,)) for _ in range(4)])


def _ada_modulation(cb_ref, cc_ref, w_ref, b_ref, cstack, part, parts, s1, r1, s2, r2):
    x, y, c = lax.axis_index("x"), lax.axis_index("y"), lax.axis_index("c")
    me = 4 * x + 2 * y + c
    peers = []
    for k in range(1, N_DEV):
        px = 1 - x if (k >> 2) & 1 else x
        py = 1 - y if (k >> 1) & 1 else y
        pc = 1 - c if k & 1 else c
        peers.append(((px, py, pc), 4 * px + 2 * py + pc))

    def exchange(src, dst, send_sems, recv_sems):
        for k, (peer, _) in enumerate(peers):
            pltpu.make_async_remote_copy(src_ref=src, dst_ref=dst.at[me], send_sem=send_sems.at[k],
                                         recv_sem=recv_sems.at[k], device_id=peer, device_id_type=MESH).start()
        dst[me] = src[...]
        waits = [pltpu.make_async_remote_copy(src_ref=src, dst_ref=dst.at[pid], send_sem=send_sems.at[k],
                                              recv_sem=recv_sems.at[k], device_id=peer, device_id_type=MESH)
                 for k, (peer, pid) in enumerate(peers)]
        for cp in waits:
            cp.wait_recv()
        for cp in waits:
            cp.wait_send()

    exchange(cb_ref, cstack, s1, r1)
    c_all = cstack[0]
    for d in range(1, N_DEV):
        c_all = c_all + cstack[d]
    row = lax.broadcasted_iota(jnp.int32, (8, D_MODEL), 0)
    cc = jnp.where(row == 0, jnp.broadcast_to(cc_ref[...], (8, D_MODEL)), 0.0)
    call = jnp.concatenate([c_all, cc], axis=0)
    s = call * _sigmoid(call)
    part[...] = jnp.dot(s, w_ref[...], preferred_element_type=F32, precision=HI) + b_ref[...]
    exchange(part, parts, s2, r2)
    return s


def _ada_bwd(s16, dm, dc, w, m, v):
    def body(s_ref, dm_ref, dc_ref, w_ref, m_ref, v_ref, g_ref, d_ref, m2_ref, v2_ref, pc_ref):
        dct = jnp.sum(dc_ref[...], axis=0, keepdims=True)
        row = lax.broadcasted_iota(jnp.int32, dc_ref.shape, 0)
        dcb = jnp.where(row == 0, jnp.broadcast_to(dct, dc_ref.shape), 0.0)
        dm16 = jnp.concatenate([dm_ref[...], dcb], axis=0)
        g = lax.dot_general(s_ref[...], dm16, (((0,), (0,)), ((), ())),
                            preferred_element_type=F32, precision=HI)
        w_ = w_ref[...]
        delta, m2, v2 = _adam(w_, g, m_ref[...], v_ref[...])
        g_ref[...] = g
        d_ref[...] = delta
        m2_ref[...] = m2
        v2_ref[...] = v2
        pc_ref[...] = lax.dot_general(dcb, w_, (((1,), (1,)), ((), ())),
                                      preferred_element_type=F32, precision=HI)

    sh = jax.ShapeDtypeStruct(w.shape, F32)
    return pl.pallas_call(
        body, name="ada_bwd",
        out_shape=[sh, sh, sh, sh, jax.ShapeDtypeStruct((8, D_MODEL), F32)],
        compiler_params=_params(),
    )(s16, dm, dc, w, m, v)


def _head_norm(zk, ones_ref, gain):
    ss = _head_sum(zk * zk, ones_ref)
    return zk * lax.rsqrt(ss * (1.0 / HEAD_DIM) + EPS) * gain


def _inproj_fwd(order, x, ng, c_blk, c_ctx_row, w_ada_sh, b_ada_sh, w_blk_t, w_out_blk, tm):
    T = x.shape[0]
    nt = T // tm
    n_pass = N_DEV // 2
    blk_rows = w_blk_t.shape[0]
    n_col = w_ada_sh.shape[1]
    rider = _Hosted("ag", w_out_blk)
    n_ada = len(_ada_scratch(n_col))

    def body(order_ref, x_ref, g_ref, cb_ref, cc_ref, wa_ref, ba_ref, wb_ref, wo_ref,
             h_out, z_ref, wt_out, wo_out, s_out, parts_out,
             hs, wt, modv, send_sems, recv_sems, loc_sem, h_sem, wt_sem, *rest):
        ada_sc, rider_sems = rest[:n_ada], rest[n_ada:]
        p, i = pl.program_id(0), pl.program_id(1)
        x, y, c = lax.axis_index("x"), lax.axis_index("y"), lax.axis_index("c")
        me, sib = (x, y, c), (x, y, 1 - c)
        chips = [(1 - x, y), (x, 1 - y), (1 - x, 1 - y)]

        def slot(px, py, pc):
            return 4 * px + 2 * py + pc

        def copy(k, block, to, src=None):
            return pltpu.make_async_remote_copy(
                src_ref=wt.at[slot(*block)] if src is None else src, dst_ref=wt.at[slot(*block)],
                send_sem=send_sems.at[k], recv_sem=recv_sems.at[k], device_id=to, device_id_type=MESH)

        own = pltpu.make_async_copy(wb_ref, wt.at[slot(*me)], loc_sem)
        h_copy = pltpu.make_async_copy(hs, h_out, h_sem)
        wt_copy = pltpu.make_async_copy(wt, wt_out, wt_sem)
        first = [copy(1 + j, me, (*chip, c), src=wb_ref) for j, chip in enumerate(chips)] + [copy(0, me, sib, src=wb_ref)]
        passed = [copy(4 + j, (*chip, c), sib) for j, chip in enumerate(chips)]

        @pl.when(jnp.logical_and(p == 0, i == 0))
        def _():
            own.start()
            for cp in first:
                cp.start()
            rider.start(wo_ref, wo_out, *rider_sems)
            s_out[...] = _ada_modulation(cb_ref, cc_ref, wa_ref, ba_ref, *ada_sc)
            parts = ada_sc[2]
            parts_out[...] = parts[...]
            my_row = pl.ds(slot(*me), 1)
            mod = jnp.concatenate([parts[d, my_row, :] for d in range(N_DEV)], axis=1)
            modv[0:1, :] = mod[:, 0:D_MODEL]
            modv[1:2, :] = mod[:, D_MODEL:2 * D_MODEL]
            own.wait()
            copy(0, sib, me).wait_recv()

        for j, chip in enumerate(chips):
            @pl.when(jnp.logical_and(p == j + 1, i == 0))
            def _(j=j, chip=chip):
                copy(1 + j, (*chip, c), me).wait_recv()
                passed[j].start()
                copy(4 + j, (*chip, 1 - c), me).wait_recv()
                if j == len(chips) - 1:
                    wt_copy.start()

        rows = pl.ds(pl.multiple_of(i * tm, tm), tm)

        @pl.when(p == 0)
        def _():
            xv = x_ref[...]
            r = lax.rsqrt(jnp.mean(xv * xv, axis=-1, keepdims=True) + EPS)
            hs[rows, :] = ((xv * r * g_ref[...]) * (1.0 + modv[1:2, :]) + modv[0:1, :]).astype(BF16)

        @pl.when(jnp.logical_and(p == 1, i == 0))
        def _():
            h_copy.start()

        w_pair = wt[pl.ds(2 * order_ref[p], 2)].reshape(2 * blk_rows, D_MODEL)
        z_ref[...] = _nt(hs[rows, :], w_pair).astype(BF16)

        @pl.when(jnp.logical_and(p == n_pass - 1, i == nt - 1))
        def _():
            for cp in first + passed:
                cp.wait_send()
            h_copy.wait()
            wt_copy.wait()
            rider.wait(wo_ref, wo_out, *rider_sems)

    whole = lambda shape: pl.BlockSpec(shape, lambda p, i, o: (0,) * len(shape))
    grid_spec = pltpu.PrefetchScalarGridSpec(
        num_scalar_prefetch=1, grid=(n_pass, nt),
        in_specs=[pl.BlockSpec((tm, D_MODEL), lambda p, i, o: (jnp.where(p == 0, i, nt - 1), 0)),
                  whole((1, D_MODEL)), whole((8, D_MODEL)), whole((1, D_MODEL)), whole((D_MODEL, n_col)),
                  whole((1, n_col)), ANY_SPEC, ANY_SPEC],
        out_specs=[ANY_SPEC, pl.BlockSpec((tm, 2 * blk_rows), lambda p, i, o: (i, o[p])), ANY_SPEC, ANY_SPEC,
                   whole((16, D_MODEL)), whole((N_DEV, 16, n_col))],
        scratch_shapes=[pltpu.VMEM((T, D_MODEL), BF16), pltpu.VMEM((N_DEV, blk_rows, D_MODEL), BF16),
                        pltpu.VMEM((8, D_MODEL), F32),
                        pltpu.SemaphoreType.DMA((7,)), pltpu.SemaphoreType.DMA((7,)), pltpu.SemaphoreType.DMA,
                        pltpu.SemaphoreType.DMA, pltpu.SemaphoreType.DMA] + _ada_scratch(n_col) + rider.scratch)
    return pl.pallas_call(
        body, name="inproj_fwd", grid_spec=grid_spec,
        out_shape=[jax.ShapeDtypeStruct((T, D_MODEL), BF16), jax.ShapeDtypeStruct((T, D_IN), BF16),
                   jax.ShapeDtypeStruct((N_DEV, blk_rows, D_MODEL), BF16), rider.out_shape,
                   jax.ShapeDtypeStruct((16, D_MODEL), F32), jax.ShapeDtypeStruct((N_DEV, 16, n_col), F32)],
        compiler_params=_params(("arbitrary", "arbitrary")),
    )(order, x, ng, c_blk, c_ctx_row, w_ada_sh, b_ada_sh, w_blk_t, w_out_blk)


def _qk_norm(z, ones_blk, gq, gk, tm):
    T = z.shape[0]

    def body(q_ref, k_ref, ones_ref, gq_ref, gk_ref, qs_ref, kn_ref):
        qs = _head_norm(q_ref[...].astype(F32), ones_ref, gq_ref[...]) * (ATT_SCALE * LOG2E)
        qs_ref[...] = qs.astype(BF16)
        kn_ref[...] = _head_norm(k_ref[...].astype(F32), ones_ref, gk_ref[...]).astype(BF16)

    v512 = pl.BlockSpec((1, 512), lambda i: (0, 0))
    return pl.pallas_call(
        body, name="qk_norm", grid=(T // tm,),
        in_specs=[pl.BlockSpec((tm, 512), lambda i: (i, 3)), pl.BlockSpec((tm, 512), lambda i: (i, 4)),
                  pl.BlockSpec((512, 512), lambda i: (0, 0)), v512, v512],
        out_specs=[pl.BlockSpec((tm, 512), lambda i: (i, 0)), pl.BlockSpec((tm, 512), lambda i: (i, 0))],
        out_shape=[jax.ShapeDtypeStruct((T, 512), BF16), jax.ShapeDtypeStruct((T, 512), BF16)],
        compiler_params=_params(("arbitrary",)),
    )(z, z, ones_blk, gq, gk)


def _ctx_fwd(ctx, ng, cscale, cshift, w_in_t, ones_blk, gk):
    C = ctx.shape[0]

    def body(x_ref, g_ref, sc_ref, sh_ref, w_ref, ones_ref, gk_ref, h_ref, z_ref, kn_ref):
        xv = x_ref[...]
        r = lax.rsqrt(jnp.mean(xv * xv, axis=-1, keepdims=True) + EPS)
        h = (xv * r * g_ref[...]) * (1.0 + sc_ref[...]) + sh_ref[...]
        hb = h.astype(BF16)
        h_ref[...] = hb
        zk = _nt(hb, w_ref[0:512, :])
        zv = _nt(hb, w_ref[512:1024, :])
        z_ref[:, 0:512] = zk.astype(BF16)
        z_ref[:, 512:1024] = zv.astype(BF16)
        kn_ref[...] = _head_norm(zk, ones_ref, gk_ref[...]).astype(BF16)

    vec = pl.BlockSpec((1, D_MODEL), lambda i: (0, 0))
    return pl.pallas_call(
        body, name="ctx_fwd", grid=(1,),
        in_specs=[pl.BlockSpec((C, D_MODEL), lambda i: (0, 0)), vec, vec, vec,
                  pl.BlockSpec((1024, D_MODEL), lambda i: (2, 0)),
                  pl.BlockSpec((512, 512), lambda i: (0, 0)), pl.BlockSpec((1, 512), lambda i: (0, 0))],
        out_specs=[pl.BlockSpec((C, D_MODEL), lambda i: (0, 0)), pl.BlockSpec((C, 1024), lambda i: (0, 0)),
                   pl.BlockSpec((C, 512), lambda i: (0, 0))],
        out_shape=[jax.ShapeDtypeStruct((C, D_MODEL), BF16), jax.ShapeDtypeStruct((C, 1024), BF16),
                   jax.ShapeDtypeStruct((C, 512), BF16)],
        compiler_params=_params(("arbitrary",)),
    )(ctx, ng, cscale, cshift, w_in_t, ones_blk, gk)


def _sgu_chunk_fwd(au, av, ag, sg, ws_bf, bsb):
    gu, dgu = _gelu_parts(au)
    gv, dgv = _gelu_parts(av)
    rr = lax.rsqrt(jnp.mean(gv * gv, axis=-1, keepdims=True) + EPS)
    vhat = gv * rr
    vn = vhat * sg
    mixed = jnp.dot(ws_bf, vn.astype(BF16), preferred_element_type=F32) + bsb
    sig = _sigmoid(ag)
    sl = ag * sig
    return gu * mixed * sl, (gu, dgu, dgv, rr, vhat, vn, mixed, sig, sl)


def _sgu_fwd(z, sgn, ws, bsb, tm):
    T = z.shape[0]

    def body(au_ref, av_ref, ag_ref, sg_ref, ws_ref, bsb_ref, o_ref):
        for g in range(SGU_GROUPS):
            ws_bf = ws_ref[g].astype(BF16)
            sg = sg_ref[:, 128 * g:128 * (g + 1)]
            bsb_g = bsb_ref[g]
            for j in range(tm // CHUNK):
                rs, cs = slice(CHUNK * j, CHUNK * (j + 1)), slice(128 * g, 128 * (g + 1))
                out, _ = _sgu_chunk_fwd(au_ref[rs, cs].astype(F32), av_ref[rs, cs].astype(F32),
                                        ag_ref[rs, cs].astype(F32), sg, ws_bf, bsb_g)
                o_ref[rs, cs] = out.astype(BF16)

    return pl.pallas_call(
        body, name="sgu_fwd", grid=(T // tm,),
        in_specs=[pl.BlockSpec((tm, 512), lambda i: (i, 0)), pl.BlockSpec((tm, 512), lambda i: (i, 1)),
                  pl.BlockSpec((tm, 512), lambda i: (i, 2)), pl.BlockSpec((1, 512), lambda i: (0, 0)),
                  pl.BlockSpec((SGU_GROUPS, CHUNK, CHUNK), lambda i: (0, 0, 0)),
                  pl.BlockSpec((SGU_GROUPS, CHUNK, 128), lambda i: (0, 0, 0))],
        out_specs=pl.BlockSpec((tm, 512), lambda i: (i, 0)),
        out_shape=jax.ShapeDtypeStruct((T, 512), BF16),
        compiler_params=_params(("arbitrary",)),
    )(z, z, z, sgn, ws, bsb)


def _attn_type(rb, nrb):
    return jnp.where(rb == 0, 0, jnp.where(rb == nrb - 1, 2, 1))


def _attn_specs(T, C):
    return [
        pl.BlockSpec((ATT_SUB * TQ, 128), lambda hp, st: (st, hp)),
        pl.BlockSpec((T, 128), lambda hp, st: (0, hp)),
        pl.BlockSpec((T, 128), lambda hp, st: (0, 20 + hp)),
        pl.BlockSpec((C, 128), lambda hp, st: (0, hp)),
        pl.BlockSpec((C, 128), lambda hp, st: (0, 4 + hp)),
        pl.BlockSpec((2, 2 * WIN_R, GRID_W, 128), lambda hp, st: (hp, 0, 0, 0)),
        pl.BlockSpec((3, TQ, TK), lambda hp, st: (0, 0, 0)),
        pl.BlockSpec((ATT_SUB * TQ, 128), lambda hp, st: (st, 24 + hp)),
    ]


def _build_bias(pairs_ref, mask_ref, bias_sc):
    for t in range(3):
        for hh in range(2):
            for i in range(Q_ROWS):
                for mm in range(K_ROWS // 2):
                    p = min(max(WIN_R - Q_ROWS * t + 2 * mm - i, 0), 2 * WIN_R - 1)
                    rs, cs = slice(GRID_W * i, GRID_W * (i + 1)), slice(128 * mm, 128 * (mm + 1))
                    bias_sc[t, hh, rs, cs] = (pairs_ref[hh, p] + mask_ref[t, rs, cs]) * LOG2E


def _attn_fwd(qs, kn, z, ckn, zc, pairs, row_mask):
    T, C = qs.shape[0], ckn.shape[0]
    rows = T // GRID_W
    nrb = rows // Q_ROWS

    def body(q_ref, k_ref, v_ref, ck_ref, cv_ref, pairs_ref, mask_ref, bg_ref, ob_ref, outb_ref, lse_ref, bias_sc):
        @pl.when(pl.program_id(1) == 0)
        def _():
            _build_bias(pairs_ref, mask_ref, bias_sc)

        ck2, cv2 = ck_ref[...], cv_ref[...]
        lane = lax.broadcasted_iota(jnp.int32, (1, 128), 1)
        for sub in range(ATT_SUB):
            rb = ATT_SUB * pl.program_id(1) + sub
            bias_ref = bias_sc.at[_attn_type(rb, nrb)]
            rs = slice(TQ * sub, TQ * (sub + 1))
            ks = pl.multiple_of(jnp.clip(Q_ROWS * rb - 4, 0, rows - K_ROWS) * GRID_W, GRID_W)
            q2 = q_ref[rs, :]
            k2 = k_ref[pl.ds(ks, TK), :]
            v2 = v_ref[pl.ds(ks, TK), :]
            o_acc = jnp.zeros((TQ, 128), F32)
            lse_acc = jnp.zeros((TQ, 128), F32)
            for hh in range(2):
                msk = (lane >= HEAD_DIM) == bool(hh)
                qm = jnp.where(msk, q2, jnp.zeros_like(q2))
                s = _nt(qm, k2) + bias_ref[hh]
                sc = _nt(qm, ck2)
                m = jnp.maximum(jnp.max(s, axis=-1, keepdims=True), jnp.max(sc, axis=-1, keepdims=True))
                p = jnp.exp2(s - m)
                pc = jnp.exp2(sc - m)
                va = jnp.where(msk, v2, jnp.ones_like(v2))
                cva = jnp.where(msk, cv2, jnp.ones_like(cv2))
                num = (jnp.dot(p.astype(BF16), va, preferred_element_type=F32)
                       + jnp.dot(pc.astype(BF16), cva, preferred_element_type=F32))
                den = pltpu.roll(num, HEAD_DIM, 1)
                o_acc = jnp.where(msk, num / den, o_acc)
                lse_acc = jnp.where(msk, m + jnp.log(den) * LOG2E, lse_acc)
            ob_ref[rs, :] = o_acc.astype(BF16)
            lse_ref[rs, :] = lse_acc
            bg = bg_ref[rs, :].astype(F32)
            outb_ref[rs, :] = (o_acc * (bg * _sigmoid(bg))).astype(BF16)

    tile = pl.BlockSpec((ATT_SUB * TQ, 128), lambda hp, st: (st, hp))
    return pl.pallas_call(
        body, name="attn_fwd", grid=(4, nrb // ATT_SUB),
        in_specs=_attn_specs(T, C),
        out_specs=[tile, tile, tile],
        out_shape=[jax.ShapeDtypeStruct((T, 512), BF16), jax.ShapeDtypeStruct((T, 512), BF16),
                   jax.ShapeDtypeStruct((T, 512), F32)],
        scratch_shapes=[pltpu.VMEM((3, 2, TQ, TK), F32)],
        compiler_params=_params(("arbitrary", "arbitrary")),
    )(qs, kn, z, ckn, zc, pairs, row_mask, z)


def _outproj_loss_bwd(x, tgt, out_a, out_b, gate, w_out, tm):
    T = x.shape[0]
    nt = T // tm

    def body(x_ref, t_ref, oa_ref, ob_ref, gate_ref, w_ref, dy_ref, dmc_ref, dw_ref, dgate_ref, loss_ref, acc):
        @pl.when(pl.program_id(0) == 0)
        def _():
            acc[...] = jnp.zeros_like(acc)
            dgate_ref[...] = jnp.zeros_like(dgate_ref)
            loss_ref[...] = jnp.zeros_like(loss_ref)

        oa, ob = oa_ref[...], ob_ref[...]
        gate_v = gate_ref[...]
        mix = (jnp.dot(oa, w_ref[0:512, :], preferred_element_type=F32)
               + jnp.dot(ob, w_ref[512:1024, :], preferred_element_type=F32))
        e = x_ref[...] + gate_v * mix - t_ref[...]
        se = jnp.sum(jnp.sum(e * e, axis=0, keepdims=True), axis=1, keepdims=True)
        loss_ref[...] += jnp.broadcast_to(se * (0.5 / D_MODEL), loss_ref.shape)
        dy = e * (1.0 / D_MODEL)
        dy_ref[...] = dy
        dgate_ref[...] += jnp.sum(dy * mix, axis=0, keepdims=True)
        dmix = (dy * gate_v).astype(BF16)
        dmc_ref[...] = _nt(dmix, w_ref[...]).astype(BF16)
        acc[0:512, :] += _tn(oa, dmix)
        acc[512:1024, :] += _tn(ob, dmix)

        @pl.when(pl.program_id(0) == nt - 1)
        def _():
            dw_ref[...] = acc[...].astype(BF16)

    row = lambda w: pl.BlockSpec((tm, w), lambda i: (i, 0))
    return pl.pallas_call(
        body, name="outproj_loss_bwd", grid=(nt,),
        in_specs=[row(D_MODEL), row(D_MODEL), row(512), row(512),
                  pl.BlockSpec((1, D_MODEL), lambda i: (0, 0)),
                  pl.BlockSpec((D_MODEL, D_MODEL), lambda i: (0, 0))],
        out_specs=[row(D_MODEL), row(D_MODEL), pl.BlockSpec((D_MODEL, D_MODEL), lambda i: (0, 0)),
                   pl.BlockSpec((1, D_MODEL), lambda i: (0, 0)), pl.BlockSpec((1, 128), lambda i: (0, 0))],
        out_shape=[jax.ShapeDtypeStruct((T, D_MODEL), F32), jax.ShapeDtypeStruct((T, D_MODEL), BF16),
                   jax.ShapeDtypeStruct((D_MODEL, D_MODEL), BF16), jax.ShapeDtypeStruct((1, D_MODEL), F32),
                   jax.ShapeDtypeStruct((1, 128), F32)],
        scratch_shapes=[pltpu.VMEM((D_MODEL, D_MODEL), F32)],
        compiler_params=_params(("arbitrary",)),
    )(x, tgt, out_a, out_b, gate, w_out)


def _attn_bwd(qs, kn, z, ckn, zc, pairs, row_mask, ob, lse, dmc, riders):
    T, C = qs.shape[0], ckn.shape[0]
    rows = T // GRID_W
    nrb = rows // Q_ROWS
    n_st = nrb // ATT_SUB
    n_rid = len(riders)

    def body(q_ref, k_ref, v_ref, ck_ref, cv_ref, pairs_ref, mask_ref, bg_ref, ob_ref, lse_ref, do_ref, *rest):
        rid_src = rest[:n_rid]
        dq_ref, dk_ref, dv_ref, dck_ref, dcv_ref, dbg_ref, dacc_ref = rest[n_rid:n_rid + 7]
        rid_dst = rest[n_rid + 7:2 * n_rid + 7]
        bias_sc = rest[2 * n_rid + 7]
        rid_sems = rest[2 * n_rid + 8:]
        hp, st = pl.program_id(0), pl.program_id(1)

        @pl.when(jnp.logical_and(hp == 0, st == 0))
        def _():
            for r, (ex, _) in enumerate(riders):
                ex.start(rid_src[r], rid_dst[r], *rid_sems[3 * r:3 * r + 3])

        @pl.when(st == 0)
        def _():
            _build_bias(pairs_ref, mask_ref, bias_sc)
            dk_ref[...] = jnp.zeros_like(dk_ref)
            dv_ref[...] = jnp.zeros_like(dv_ref)
            dck_ref[...] = jnp.zeros_like(dck_ref)
            dcv_ref[...] = jnp.zeros_like(dcv_ref)
            dacc_ref[...] = jnp.zeros_like(dacc_ref)

        ck2, cv2 = ck_ref[...], cv_ref[...]
        lane = lax.broadcasted_iota(jnp.int32, (1, 128), 1)
        for sub in range(ATT_SUB):
            rb = ATT_SUB * st + sub
            bias_ref = bias_sc.at[_attn_type(rb, nrb)]
            rs = slice(TQ * sub, TQ * (sub + 1))
            kb = jnp.clip(Q_ROWS * rb - 4, 0, rows - K_ROWS)
            ks = pl.multiple_of(kb * GRID_W, GRID_W)
            ebase = kb - Q_ROWS * rb + 11
            q2 = q_ref[rs, :]
            k2 = k_ref[pl.ds(ks, TK), :]
            v2 = v_ref[pl.ds(ks, TK), :]
            bg = bg_ref[rs, :].astype(F32)
            sig = _sigmoid(bg)
            obv = ob_ref[rs, :].astype(F32)
            dout = do_ref[rs, :].astype(F32)
            dbg_ref[rs, :] = (dout * obv * (sig * (1.0 + bg * (1.0 - sig)))).astype(BF16)
            d_o = dout * (bg * sig)
            d_oo = d_o * obv
            lse2 = lse_ref[rs, :]
            dq_acc = jnp.zeros((TQ, 128), F32)
            for hh in range(2):
                msk = (lane >= HEAD_DIM) == bool(hh)
                qm = jnp.where(msk, q2, jnp.zeros_like(q2))
                lse_h = jnp.max(jnp.where(msk, lse2, -jnp.inf), axis=-1, keepdims=True)
                p = jnp.exp2(_nt(qm, k2) + bias_ref[hh] - lse_h)
                pc = jnp.exp2(_nt(qm, ck2) - lse_h)
                dom_f = jnp.where(msk, d_o, 0.0)
                dom = dom_f.astype(BF16)
                delta = jnp.sum(jnp.where(msk, d_oo, 0.0), axis=-1, keepdims=True)
                d_hi = delta.astype(BF16).astype(F32)
                x0 = HEAD_DIM * (1 - hh)
                dom_aug = jnp.where(lane == x0, -d_hi, jnp.where(lane == x0 + 1, d_hi - delta, dom_f)).astype(BF16)
                extra = jnp.logical_or(lane == x0, lane == x0 + 1)
                va = jnp.where(msk, v2, jnp.where(extra, jnp.ones_like(v2), jnp.zeros_like(v2)))
                cva = jnp.where(msk, cv2, jnp.where(extra, jnp.ones_like(cv2), jnp.zeros_like(cv2)))
                ds = p * _nt(dom_aug, va)
                dsc = pc * _nt(dom_aug, cva)
                dsb, dscb = ds.astype(BF16), dsc.astype(BF16)
                dq_h = (jnp.dot(dsb, k2, preferred_element_type=F32)
                        + jnp.dot(dscb, ck2, preferred_element_type=F32))
                dq_acc = jnp.where(msk, dq_h, dq_acc)
                dk_ref[pl.ds(ks, TK), :] += _tn(dsb, qm)
                dv_ref[pl.ds(ks, TK), :] += _tn(p.astype(BF16), dom)
                dck_ref[...] += _tn(dscb, qm)
                dcv_ref[...] += _tn(pc.astype(BF16), dom)
                for i in range(Q_ROWS):
                    for mm in range(K_ROWS // 2):
                        dacc_ref[hh, ebase + (2 * mm - i)] += ds[GRID_W * i:GRID_W * (i + 1),
                                                                 128 * mm:128 * (mm + 1)]
            dq_ref[rs, :] = dq_acc

        @pl.when(jnp.logical_and(hp == pl.num_programs(0) - 1, st == n_st - 1))
        def _():
            for r, (ex, _) in enumerate(riders):
                ex.wait(rid_src[r], rid_dst[r], *rid_sems[3 * r:3 * r + 3])

    tile = pl.BlockSpec((ATT_SUB * TQ, 128), lambda hp, st: (st, hp))
    colT = pl.BlockSpec((T, 128), lambda hp, st: (0, hp))
    colC = pl.BlockSpec((C, 128), lambda hp, st: (0, hp))
    res = pl.pallas_call(
        body, name="attn_bwd", grid=(4, n_st),
        in_specs=(_attn_specs(T, C) + [tile, tile, pl.BlockSpec((ATT_SUB * TQ, 128), lambda hp, st: (st, 4 + hp))]
                  + [ANY_SPEC] * n_rid),
        out_specs=([tile, colT, colT, colC, colC, tile,
                    pl.BlockSpec((2, N_DIAG, GRID_W, 128), lambda hp, st: (hp, 0, 0, 0))] + [ANY_SPEC] * n_rid),
        out_shape=([jax.ShapeDtypeStruct((T, 512), F32), jax.ShapeDtypeStruct((T, 512), F32),
                    jax.ShapeDtypeStruct((T, 512), F32), jax.ShapeDtypeStruct((C, 512), F32),
                    jax.ShapeDtypeStruct((C, 512), F32), jax.ShapeDtypeStruct((T, 512), BF16),
                    jax.ShapeDtypeStruct((N_HEADS, N_DIAG, GRID_W, 128), F32)] + [ex.out_shape for ex, _ in riders]),
        scratch_shapes=[pltpu.VMEM((3, 2, TQ, TK), F32)] + [s for ex, _ in riders for s in ex.scratch],
        compiler_params=_params(("arbitrary", "arbitrary")),
    )(qs, kn, z, ckn, zc, pairs, row_mask, z, ob, lse, dmc, *[a for _, a in riders])
    return res[:7], res[7:]


def _rpb_grad(dacc, hsel, fold):
    n_off = 2 * WIN_C - 1
    n_dr = 2 * WIN_R - 1

    def body(a_ref, hsel_ref, fold_ref, o_ref):
        qc = lax.broadcasted_iota(jnp.int32, (GRID_W, 128), 0)
        lane = lax.broadcasted_iota(jnp.int32, (GRID_W, 128), 1)
        diff = lane % GRID_W - qc + (WIN_C - 1)
        left = lane < GRID_W

        def by_dr(dr):
            return a_ref[0, dr + 4] + pltpu.roll(a_ref[0, dr + 3], GRID_W, 1)

        out = jnp.zeros((32, 128), F32)
        for j in range((n_dr + 1) // 2):
            hi = pltpu.roll(by_dr(2 * j + 1), GRID_W, 1) if 2 * j + 1 < n_dr else 0.0
            pair = jnp.where(left, by_dr(2 * j), hi)
            parts = []
            for o in range(n_off):
                mv = jnp.where(diff == o, pair, 0.0)
                acc = mv[0:8]
                for r8 in range(1, GRID_W // 8):
                    acc = acc + mv[8 * r8:8 * (r8 + 1)]
                parts.append(acc)
            parts.append(jnp.zeros((8, 128), F32))
            stack = jnp.concatenate(parts, axis=0)
            s_hi = stack.astype(BF16)
            s_lo = (stack - s_hi.astype(F32)).astype(BF16)
            per_o = (jnp.dot(fold_ref[...], s_hi, preferred_element_type=F32)
                     + jnp.dot(fold_ref[...], s_lo, preferred_element_type=F32))
            out = out + _dot2(per_o, hsel_ref[j])
        o_ref[0] = out

    return pl.pallas_call(
        body, name="rpb_grad", grid=(N_HEADS,),
        in_specs=[pl.BlockSpec((1, N_DIAG, GRID_W, 128), lambda h: (h, 0, 0, 0)),
                  pl.BlockSpec((8, 128, 128), lambda h: (0, 0, 0)), pl.BlockSpec((32, 256), lambda h: (0, 0))],
        out_specs=pl.BlockSpec((1, 32, 128), lambda h: (h, 0, 0)),
        out_shape=jax.ShapeDtypeStruct((N_HEADS, 32, 128), F32),
        compiler_params=_params(("arbitrary",)),
    )(dacc, hsel, fold)


def _qk_bwd(z, col_q, col_k, dqs, dk, dv, ones_blk, gq, gk, foldm, tm, name):
    T = dk.shape[0]
    with_q = dqs is not None

    def norm_bwd(raw, dn, gain, ones_ref):
        rr = lax.rsqrt(_head_sum(raw * raw, ones_ref) * (1.0 / HEAD_DIM) + EPS)
        hat = raw * rr
        dgain = jnp.sum(dn * hat, axis=0, keepdims=True)
        dhat = dn * gain
        mean = _head_sum(dhat * hat, ones_ref) * (1.0 / HEAD_DIM)
        return rr * (dhat - hat * mean), dgain

    def body(*refs):
        if with_q:
            (bq_ref, bk_ref, dq_ref, dk_ref, dv_ref, ones_ref, gq_ref, gk_ref, fold_ref,
             dbq_ref, dbk_ref, dbv_ref, dgq_ref, dgk_ref, accq, acck) = refs
        else:
            (bk_ref, dk_ref, dv_ref, ones_ref, gk_ref, fold_ref, dbk_ref, dbv_ref, dgk_ref, acck) = refs
        i = pl.program_id(0)

        @pl.when(i == 0)
        def _():
            acck[...] = jnp.zeros_like(acck)
            if with_q:
                accq[...] = jnp.zeros_like(accq)

        dbk, dgk = norm_bwd(bk_ref[...].astype(F32), dk_ref[...] * LN2, gk_ref[...], ones_ref)
        dbk_ref[...] = dbk.astype(BF16)
        acck[...] += dgk
        dbv_ref[...] = dv_ref[...].astype(BF16)
        if with_q:
            dbq, dgq = norm_bwd(bq_ref[...].astype(F32), dq_ref[...] * ATT_SCALE, gq_ref[...], ones_ref)
            dbq_ref[...] = dbq.astype(BF16)
            accq[...] += dgq

        @pl.when(i == pl.num_programs(0) - 1)
        def _():
            dgk_ref[...] = jnp.dot(jnp.broadcast_to(acck[...], (8, 512)), fold_ref[...],
                                   preferred_element_type=F32, precision=HI)
            if with_q:
                dgq_ref[...] = jnp.dot(jnp.broadcast_to(accq[...], (8, 512)), fold_ref[...],
                                       preferred_element_type=F32, precision=HI)

    row = pl.BlockSpec((tm, 512), lambda i: (i, 0))
    cst = lambda a, b: pl.BlockSpec((a, b), lambda i: (0, 0))
    out_row = jax.ShapeDtypeStruct((T, 512), BF16)
    gvec = jax.ShapeDtypeStruct((8, 128), F32)
    if with_q:
        in_specs = [pl.BlockSpec((tm, 512), lambda i: (i, col_q)), pl.BlockSpec((tm, 512), lambda i: (i, col_k)),
                    row, row, row, cst(512, 512), cst(1, 512), cst(1, 512), cst(512, 128)]
        args = (z, z, dqs, dk, dv, ones_blk, gq, gk, foldm)
        out_specs = [row, row, row, cst(8, 128), cst(8, 128)]
        out_shape = [out_row, out_row, out_row, gvec, gvec]
        scratch = [pltpu.VMEM((1, 512), F32), pltpu.VMEM((1, 512), F32)]
    else:
        in_specs = [pl.BlockSpec((tm, 512), lambda i: (i, col_k)), row, row, cst(512, 512), cst(1, 512), cst(512, 128)]
        args = (z, dk, dv, ones_blk, gk, foldm)
        out_specs = [row, row, cst(8, 128)]
        out_shape = [out_row, out_row, gvec]
        scratch = [pltpu.VMEM((1, 512), F32)]
    return pl.pallas_call(
        body, name=name, grid=(T // tm,), in_specs=in_specs, out_specs=out_specs, out_shape=out_shape,
        scratch_shapes=scratch, compiler_params=_params(("arbitrary",)),
    )(*args)


def _sgu_bwd(z, dmc, sgn, ws, wst, bsb, ones8, tm):
    T = z.shape[0]

    def body(au_ref, av_ref, ag_ref, d_ref, sg_ref, ws_ref, wst_ref, bsb_ref, ones_ref,
             dau_ref, dav_ref, dag_ref, dws_ref, dbs_ref, dsg_ref):
        @pl.when(pl.program_id(0) == 0)
        def _():
            dws_ref[...] = jnp.zeros_like(dws_ref)
            dbs_ref[...] = jnp.zeros_like(dbs_ref)
            dsg_ref[...] = jnp.zeros_like(dsg_ref)

        for g in range(SGU_GROUPS):
            ws_bf = ws_ref[g].astype(BF16)
            wst_bf = wst_ref[g].astype(BF16)
            sg = sg_ref[:, 128 * g:128 * (g + 1)]
            bsb_g = bsb_ref[g]
            for j in range(tm // CHUNK):
                rs, cs = slice(CHUNK * j, CHUNK * (j + 1)), slice(128 * g, 128 * (g + 1))
                au, av, ag = (au_ref[rs, cs].astype(F32), av_ref[rs, cs].astype(F32), ag_ref[rs, cs].astype(F32))
                d = d_ref[rs, cs].astype(F32)
                _, (gu, dgu, dgv, rr, vhat, vn, mixed, sig, sl) = _sgu_chunk_fwd(au, av, ag, sg, ws_bf, bsb_g)
                dau_ref[rs, cs] = (d * mixed * sl * dgu).astype(BF16)
                dag_ref[rs, cs] = (d * gu * mixed * (sig * (1.0 + ag * (1.0 - sig)))).astype(BF16)
                dmixed = d * gu * sl
                dmb = dmixed.astype(BF16)
                dm_lo = (dmixed - dmb.astype(F32)).astype(BF16)
                dbs_ref[g] += _nt(ones_ref[...], dmb) + _nt(ones_ref[...], dm_lo)
                dws_ref[g] += _nt(dmb, vn.astype(BF16))
                dvn = jnp.dot(wst_bf, dmb, preferred_element_type=F32)
                dsg_ref[:, 128 * g:128 * (g + 1)] += jnp.sum(dvn * vhat, axis=0, keepdims=True)
                dvhat = dvn * sg
                mean = jnp.mean(dvhat * vhat, axis=-1, keepdims=True)
                dav_ref[rs, cs] = (rr * (dvhat - vhat * mean) * dgv).astype(BF16)

    row = lambda col: pl.BlockSpec((tm, 512), lambda i: (i, col))
    c3 = lambda n: pl.BlockSpec((SGU_GROUPS, CHUNK, n), lambda i: (0, 0, 0))
    out_row = jax.ShapeDtypeStruct((T, 512), BF16)
    return pl.pallas_call(
        body, name="sgu_bwd", grid=(T // tm,),
        in_specs=[row(0), row(1), row(2), row(0), pl.BlockSpec((1, 512), lambda i: (0, 0)),
                  c3(CHUNK), c3(CHUNK), c3(128), pl.BlockSpec((8, 128), lambda i: (0, 0))],
        out_specs=[row(0), row(0), row(0), c3(CHUNK), pl.BlockSpec((SGU_GROUPS, 8, CHUNK), lambda i: (0, 0, 0)),
                   pl.BlockSpec((1, 512), lambda i: (0, 0))],
        out_shape=[out_row, out_row, out_row, jax.ShapeDtypeStruct((SGU_GROUPS, CHUNK, CHUNK), F32),
                   jax.ShapeDtypeStruct((SGU_GROUPS, 8, CHUNK), F32), jax.ShapeDtypeStruct((1, 512), F32)],
        compiler_params=_params(("arbitrary",)),
    )(z, z, z, dmc, sgn, ws, wst, bsb, ones8)


def _inproj_bwd_dx(dzs, w_cols, w_in_t, x, dy, ng, scale, shift, tm, name, riders=()):
    T = x.shape[0]
    n = len(dzs)
    nt = T // tm
    with_dx = dy is not None
    n_own_in = n + 5 + with_dx
    n_own_out = 3 + with_dx
    n_rid = len(riders)

    def body(*refs):
        dz_refs = refs[:n]
        own = refs[n:n_own_in] + refs[n_own_in + n_rid:n_own_in + n_rid + n_own_out]
        rid_src = refs[n_own_in:n_own_in + n_rid]
        rid_dst = refs[n_own_in + n_rid + n_own_out:n_own_in + 2 * n_rid + n_own_out]
        rid_sems = refs[n_own_in + 2 * n_rid + n_own_out:]
        if with_dx:
            w_ref, x_ref, dy_ref, g_ref, sc_ref, sh_ref, gx_ref, dsh_ref, dsc_ref, dg_ref = own
        else:
            w_ref, x_ref, g_ref, sc_ref, sh_ref, dsh_ref, dsc_ref, dg_ref = own

        @pl.when(pl.program_id(0) == 0)
        def _():
            for r, (ex, _) in enumerate(riders):
                ex.start(rid_src[r], rid_dst[r], *rid_sems[3 * r:3 * r + 3])
            dsh_ref[...] = jnp.zeros_like(dsh_ref)
            dsc_ref[...] = jnp.zeros_like(dsc_ref)
            dg_ref[...] = jnp.zeros_like(dg_ref)

        dh = jnp.dot(dz_refs[0][...], w_ref[0:512, :], preferred_element_type=F32)
        for k in range(1, n):
            dh = dh + jnp.dot(dz_refs[k][...], w_ref[512 * k:512 * (k + 1), :], preferred_element_type=F32)
        xv = x_ref[...]
        r = lax.rsqrt(jnp.mean(xv * xv, axis=-1, keepdims=True) + EPS)
        xn = xv * r
        gv, op = g_ref[...], 1.0 + sc_ref[...]
        dsh_ref[...] += jnp.sum(dh, axis=0, keepdims=True)
        dsc_ref[...] += jnp.sum(dh * xn * gv, axis=0, keepdims=True)
        dg_ref[...] += jnp.sum(dh * op * xn, axis=0, keepdims=True)
        if with_dx:
            dxn = dh * (gv * op)
            gx_ref[...] = r * (dxn - xn * jnp.mean(dxn * xn, axis=-1, keepdims=True)) + dy_ref[...]

        @pl.when(pl.program_id(0) == nt - 1)
        def _():
            for r, (ex, _) in enumerate(riders):
                ex.wait(rid_src[r], rid_dst[r], *rid_sems[3 * r:3 * r + 3])

    vec = pl.BlockSpec((1, D_MODEL), lambda i: (0, 0))
    rowf = pl.BlockSpec((tm, D_MODEL), lambda i: (i, 0))
    in_specs = [pl.BlockSpec((tm, 512), lambda i: (i, 0))] * n
    in_specs += [pl.BlockSpec((512 * n, D_MODEL), lambda i: (w_cols // n, 0)), rowf]
    args = list(dzs) + [w_in_t, x]
    vshape = jax.ShapeDtypeStruct((1, D_MODEL), F32)
    out_specs, out_shape = [vec, vec, vec], [vshape, vshape, vshape]
    if with_dx:
        in_specs.append(rowf)
        args.append(dy)
        out_specs = [rowf] + out_specs
        out_shape = [jax.ShapeDtypeStruct((T, D_MODEL), F32)] + out_shape
    in_specs += [vec, vec, vec] + [ANY_SPEC] * n_rid
    args += [ng, scale, shift] + [a for _, a in riders]
    res = pl.pallas_call(
        body, name=name, grid=(nt,), in_specs=in_specs, out_specs=out_specs + [ANY_SPEC] * n_rid,
        out_shape=out_shape + [ex.out_shape for ex, _ in riders],
        scratch_shapes=[s for ex, _ in riders for s in ex.scratch],
        compiler_params=_params(("arbitrary",)),
    )(*args)
    return res[:n_own_out], res[n_own_out:]


def _inproj_bwd_dw(h, dzs, hc, dzc_k, dzc_v, tk, riders=()):
    T = h.shape[0]
    nt = T // tk
    n = len(dzs)
    n_rid = len(riders)

    def body(*refs):
        h_ref = refs[0]
        dz_refs = refs[1:1 + n]
        hc_ref, dzck_ref, dzcv_ref = refs[1 + n:4 + n]
        rid_src = refs[4 + n:4 + n + n_rid]
        o_ref = refs[4 + n + n_rid]
        rid_dst = refs[5 + n + n_rid:5 + n + 2 * n_rid]
        acc = refs[5 + n + 2 * n_rid]
        rid_sems = refs[6 + n + 2 * n_rid:]
        t = pl.program_id(0)

        @pl.when(t == 0)
        def _():
            for r, (ex, _) in enumerate(riders):
                ex.start(rid_src[r], rid_dst[r], *rid_sems[3 * r:3 * r + 3])
            acc[...] = jnp.zeros_like(acc)
            acc[512 * 4:512 * 5, :] = _tn(dzck_ref[...], hc_ref[...])
            acc[512 * 5:512 * 6, :] = _tn(dzcv_ref[...], hc_ref[...])

        hv = h_ref[...]
        for k in range(n):
            acc[512 * k:512 * (k + 1), :] += _tn(dz_refs[k][...], hv)

        @pl.when(t == nt - 1)
        def _():
            o_ref[...] = acc[...].astype(BF16)
            for r, (ex, _) in enumerate(riders):
                ex.wait(rid_src[r], rid_dst[r], *rid_sems[3 * r:3 * r + 3])

    whole = lambda a: pl.BlockSpec(a.shape, lambda t: (0, 0))
    res = pl.pallas_call(
        body, name="inproj_bwd_dw", grid=(nt,),
        in_specs=([pl.BlockSpec((tk, D_MODEL), lambda t: (t, 0))] + [pl.BlockSpec((tk, 512), lambda t: (t, 0))] * n
                  + [whole(hc), whole(dzc_k), whole(dzc_v)] + [ANY_SPEC] * n_rid),
        out_specs=[pl.BlockSpec((512 * n, D_MODEL), lambda t: (0, 0))] + [ANY_SPEC] * n_rid,
        out_shape=[jax.ShapeDtypeStruct((512 * n, D_MODEL), BF16)] + [ex.out_shape for ex, _ in riders],
        scratch_shapes=[pltpu.VMEM((512 * n, D_MODEL), F32)] + [s for ex, _ in riders for s in ex.scratch],
        compiler_params=_params(("arbitrary",)),
    )(h, *dzs, hc, dzc_k, dzc_v, *[a for _, a in riders])
    return res[0], res[1:]


def _adamw_sharded(w, gparts, m, v, tr, name):
    R, C = w.shape
    n_part = gparts.shape[0]

    def body(w_ref, gp_ref, m_ref, v_ref, g_ref, d_ref, m2_ref, v2_ref):
        g = gp_ref[0].astype(F32)
        for d in range(1, n_part):
            g = g + gp_ref[d].astype(F32)
        delta, m2, v2 = _adam(w_ref[...], g, m_ref[...], v_ref[...])
        g_ref[...] = g
        d_ref[...] = delta
        m2_ref[...] = m2
        v2_ref[...] = v2

    row = pl.BlockSpec((tr, C), lambda i: (i, 0))
    sh = jax.ShapeDtypeStruct((R, C), F32)
    return pl.pallas_call(
        body, name=name, grid=(R // tr,),
        in_specs=[row, pl.BlockSpec((n_part, tr, C), lambda i: (0, i, 0)), row, row],
        out_specs=[row, row, row, row], out_shape=[sh, sh, sh, sh],
        compiler_params=_params(("arbitrary",)),
    )(w, gparts, m, v)


def _pack_vectors(vec_rows, dsg, dgq, dgk, dgk_c, loss_part):
    n_vec = len(vec_rows)

    def body(*refs):
        vecs = refs[:n_vec]
        dsg_ref, dgq_ref, dgk_ref, dgkc_ref, loss_ref, v_ref = refs[n_vec:]
        row = lax.broadcasted_iota(jnp.int32, (16, D_MODEL), 0)
        misc = jnp.concatenate([dsg_ref[...], dgq_ref[0:1, :], dgk_ref[0:1, :], dgkc_ref[0:1, :],
                                loss_ref[...]], axis=1)
        v = jnp.where(row == V_MISC, jnp.broadcast_to(misc, (16, D_MODEL)), 0.0)
        for (r, _), ref in zip(vec_rows, vecs):
            v = jnp.where(row == r, jnp.broadcast_to(ref[...], (16, D_MODEL)), v)
        v_ref[...] = v

    return pl.pallas_call(
        body, name="pack_vectors", out_shape=jax.ShapeDtypeStruct((16, D_MODEL), F32), compiler_params=_params(),
    )(*[a for _, a in vec_rows], dsg, dgq, dgk, dgk_c, loss_part)


def _pack_matrices(dws, dbs, drpb):
    def body(dws_ref, dbs_ref, drpb_ref, m_ref):
        for g in range(SGU_GROUPS):
            m_ref[128 * g:128 * (g + 1), :] = dws_ref[g]
            m_ref[M_DBS + 8 * g:M_DBS + 8 * (g + 1), :] = dbs_ref[g]
        for hd in range(N_HEADS):
            m_ref[M_RPB + 32 * hd:M_RPB + 32 * (hd + 1), :] = drpb_ref[hd]

    return pl.pallas_call(
        body, name="pack_matrices", out_shape=jax.ShapeDtypeStruct((M_ROWS, 128), F32), compiler_params=_params(),
    )(dws, dbs, drpb)


SMALL_NAMES = ("b_ada", "norm_g", "sgu_norm_g", "w_spatial", "b_spatial", "q_norm_g", "k_norm_g", "rpb")


def _adamw_small(vg, mg, ws, ms, vs):
    k = len(SMALL_NAMES)

    def body(*refs):
        vg_ref, mg_ref = refs[0], refs[1]
        w_refs = dict(zip(SMALL_NAMES, refs[2:2 + k]))
        m_refs = dict(zip(SMALL_NAMES, refs[2 + k:2 + 2 * k]))
        v_refs = dict(zip(SMALL_NAMES, refs[2 + 2 * k:2 + 3 * k]))
        o_refs = [dict(zip(SMALL_NAMES, refs[2 + (3 + i) * k:2 + (4 + i) * k])) for i in range(4)]
        loss_ref = refs[2 + 7 * k]

        sv = vg_ref[0]
        for d in range(1, N_DEV):
            sv = sv + vg_ref[d]
        loss_ref[...] = sv[V_MISC:V_MISC + 1, 896:1024]

        def total(lo, hi):
            s = mg_ref[0, lo:hi, :]
            for d in range(1, N_DEV):
                s = s + mg_ref[d, lo:hi, :]
            return s

        def emit(name, idx, g):
            res = _adam(w_refs[name][idx], g, m_refs[name][idx], v_refs[name][idx])
            for o, val in zip(o_refs, (g,) + res):
                o[name][idx] = val

        everything = (slice(None), slice(None))
        row = lambda r: sv[r:r + 1, :]
        emit("b_ada", everything, jnp.concatenate(
            [row(V_DSHIFT) + row(V_DCSHIFT), row(V_DSCALE) + row(V_DCSCALE), row(V_DGATE)], axis=1))
        emit("norm_g", everything, row(V_DNG) + row(V_DNG_CTX))
        misc = row(V_MISC)
        emit("sgu_norm_g", everything, misc[:, 0:512])
        emit("q_norm_g", everything, misc[:, 512:512 + HEAD_DIM])
        emit("k_norm_g", everything, misc[:, 640:640 + HEAD_DIM] + misc[:, 768:768 + HEAD_DIM])
        for g in range(SGU_GROUPS):
            emit("w_spatial", (0, g), total(128 * g, 128 * (g + 1)))
            emit("b_spatial", (0, slice(g, g + 1), slice(None)), total(M_DBS + 8 * g, M_DBS + 8 * (g + 1))[0:1, :])
        for hd in range(N_HEADS):
            by_dc = total(M_RPB + 32 * hd, M_RPB + 32 * (hd + 1))
            emit("rpb", (0, hd), by_dc.T[0:2 * WIN_R - 1, 0:2 * WIN_C - 1])

    shapes = [jax.ShapeDtypeStruct(w.shape, F32) for w in ws]
    res = pl.pallas_call(body, name="adamw_small", out_shape=shapes * 4 + [jax.ShapeDtypeStruct((1, 128), F32)],
                         compiler_params=_params())(vg, mg, *ws, *ms, *vs)
    return [res[i * k:(i + 1) * k] for i in range(4)], res[4 * k]


def _adamw_cctx(pc_g, w, m, v):
    def body(pc_ref, w_ref, m_ref, v_ref, g_ref, d_ref, m2_ref, v2_ref):
        pc = pc_ref[0, 0:1, :]
        for d in range(1, N_DEV):
            pc = pc + pc_ref[d, 0:1, :]
        cc = w_ref[...]
        sig = _sigmoid(cc)
        g = pc * (sig * (1.0 + cc * (1.0 - sig)))
        delta, m2, v2 = _adam(cc, g, m_ref[...], v_ref[...])
        g_ref[...] = g
        d_ref[...] = delta
        m2_ref[...] = m2
        v2_ref[...] = v2

    sh = jax.ShapeDtypeStruct((1, D_MODEL), F32)
    return pl.pallas_call(body, name="adamw_cctx", out_shape=[sh, sh, sh, sh], compiler_params=_params())(
        pc_g, w, m, v)


def _block_ones(n, blk):
    i = np.arange(n)
    return jnp.asarray((i[:, None] // blk == i[None, :] // blk).astype(np.float32), BF16)


def _rpb_pairs(rpb):
    n_off = 2 * WIN_C - 1
    cols = np.arange(GRID_W)
    c0 = np.clip(cols - WIN_C // 2, 0, GRID_W - WIN_C)
    in_win = (cols[None, :] >= c0[:, None]) & (cols[None, :] < c0[:, None] + WIN_C)
    dc = np.clip(cols[None, :] - cols[:, None] + (WIN_C - 1), 0, n_off - 1)
    expand = (dc[None] == np.arange(n_off)[:, None, None]) & in_win[None]
    toep = jnp.einsum("hrd,dqk->hrqk", rpb, jnp.asarray(expand, F32), precision=HI)
    toep = toep + jnp.asarray(np.where(in_win, 0.0, NEG_INF).astype(np.float32))
    neg = jnp.full((N_HEADS, 1, GRID_W, GRID_W), NEG_INF, F32)
    ext = jnp.concatenate([neg, toep, neg], axis=1)
    return jnp.concatenate([ext[:, :-1], ext[:, 1:]], axis=-1)


def _row_mask(rows):
    nrb = rows // Q_ROWS
    valid = np.zeros((3, Q_ROWS, 1, K_ROWS, 1), bool)
    for t, rb in enumerate((0, 1, nrb - 1)):
        kb = int(np.clip(Q_ROWS * rb - 4, 0, rows - K_ROWS))
        for i in range(Q_ROWS):
            r0 = int(np.clip(Q_ROWS * rb + i - WIN_R // 2, 0, rows - WIN_R))
            for j in range(K_ROWS):
                valid[t, i, 0, j, 0] = r0 <= kb + j < r0 + WIN_R
    full = np.broadcast_to(valid, (3, Q_ROWS, GRID_W, K_ROWS, GRID_W)).reshape(3, TQ, TK)
    return jnp.asarray(np.where(full, 0.0, NEG_INF).astype(np.float32))


def kernel(x, c, ctx, c_ctx, w_ada, b_ada, norm_g, w_in, sgu_norm_g, w_spatial, b_spatial, q_norm_g, k_norm_g, rpb, w_out, loss_target, m_c_ctx, m_w_ada, m_b_ada, m_norm_g, m_w_in, m_sgu_norm_g, m_w_spatial, m_b_spatial, m_q_norm_g, m_k_norm_g, m_rpb, m_w_out, v_c_ctx, v_w_ada, v_b_ada, v_norm_g, v_w_in, v_sgu_norm_g, v_w_spatial, v_b_spatial, v_q_norm_g, v_k_norm_g, v_rpb, v_w_out):
    me = 4 * lax.axis_index("x") + 2 * lax.axis_index("y") + lax.axis_index("c")
    x2, ctx2, tgt2 = x[0], ctx[0], loss_target[0]
    T, C = x2.shape[0], ctx2.shape[0]
    rows = T // GRID_W
    wada, win_t, wout = w_ada[0], w_in[0].T, w_out[0]
    ada_w = wada.shape[1]
    win_w = win_t.shape[0]

    row8 = lax.broadcasted_iota(jnp.int32, (8, D_MODEL), 0)
    c_blk = jnp.where(row8 == me, jnp.broadcast_to(c, (8, D_MODEL)), 0.0)
    b_sh = lax.dynamic_slice(b_ada, (0, me * ada_w), (1, ada_w))
    c_ctx_row = c_ctx.reshape(1, D_MODEL)

    ones512 = _block_ones(512, HEAD_DIM)
    ones8 = jnp.ones((8, 128), BF16)
    foldm = jnp.asarray((np.arange(512)[:, None] % HEAD_DIM == np.arange(128)[None, :]).astype(np.float32))
    lane_half = np.arange(128)[None, :, None] // GRID_W
    hsel = jnp.asarray((2 * np.arange(8)[:, None, None] + lane_half == np.arange(128)[None, None, :]).astype(np.float32),
                       BF16)
    foldr = jnp.asarray((np.arange(256)[None, :] // 8 == np.arange(32)[:, None]).astype(np.float32), BF16)
    gq512 = jnp.tile(q_norm_g, (1, N_HEADS))
    gk512 = jnp.tile(k_norm_g, (1, N_HEADS))
    ws = w_spatial[0]
    wst = ws.transpose(0, 2, 1)
    bsb = jnp.broadcast_to(b_spatial[0][:, :, None], (SGU_GROUPS, CHUNK, 128))
    pairs = _rpb_pairs(rpb[0])
    row_mask = _row_mask(rows)

    my_chip = me // 2
    order = jnp.stack([my_chip, my_chip ^ 2, my_chip ^ 1, my_chip ^ 3]).astype(jnp.int32)
    h, z, win_g, wout_g, s16, part_g = _inproj_fwd(order, x2, norm_g, c_blk, c_ctx_row, wada, b_sh,
                                                   win_t.astype(BF16), wout.astype(BF16), 512)
    w_in_b = win_g.reshape(D_IN, D_MODEL)
    w_out_b = wout_g.reshape(D_MODEL, D_MODEL)
    mod16 = part_g.transpose(1, 0, 2).reshape(16, 3 * D_MODEL)
    mod = lax.dynamic_slice(mod16, (me, 0), (1, 3 * D_MODEL))
    shift, scale, gate = mod[:, :D_MODEL], mod[:, D_MODEL:2 * D_MODEL], mod[:, 2 * D_MODEL:]
    cshift, cscale = mod16[8:9, :D_MODEL], mod16[8:9, D_MODEL:2 * D_MODEL]
    qs, kn = _qk_norm(z, ones512, gq512, gk512, 512)
    hc, zc, ckn = _ctx_fwd(ctx2, norm_g, cscale, cshift, w_in_b, ones512, gk512)
    out_a = _sgu_fwd(z, sgu_norm_g, ws, bsb, 512)
    ob, out_b, lse = _attn_fwd(qs, kn, z, ckn, zc, pairs, row_mask)

    dy, dmc, dw_out, dgate, loss_part = _outproj_loss_bwd(x2, tgt2, out_a, out_b, gate, w_out_b, 512)
    dw_out_blocks = dw_out.reshape(N_DEV, D_MODEL // N_DEV, D_MODEL)
    (dqs, dk, dv, dck, dcv, db_g, dacc), (gout_parts,) = _attn_bwd(
        qs, kn, z, ckn, zc, pairs, row_mask, ob, lse, dmc, [(_Hosted("a2a", dw_out_blocks), dw_out_blocks)])
    drpb = _rpb_grad(dacc, hsel, foldr)
    db_q, db_k, db_v, dgq, dgk = _qk_bwd(z, 3, 4, dqs, dk, dv, ones512, gq512, gk512, foldm, 512, "qk_bwd")
    dzc_k, dzc_v, dgk_c = _qk_bwd(zc, None, 0, None, dck, dcv, ones512, None, gk512, foldm, C, "ctx_k_bwd")
    da_u, da_v, da_g, dws, dbs, dsg = _sgu_bwd(z, dmc, sgu_norm_g, ws, wst, bsb, ones8, 512)
    dzs = [da_u, da_v, da_g, db_q, db_k, db_v, db_g]
    mloc = _pack_matrices(dws, dbs, drpb)
    dw_in_t, (mg,) = _inproj_bwd_dw(h, dzs, hc, dzc_k, dzc_v, 512, [(_Hosted("ag", mloc), mloc)])
    (chip_sums,) = _chip_presum([dw_in_t.reshape(N_DEV, win_w, D_MODEL)], "presum_dw_in")
    (grad_x, dshift, dscale, dng), (gin_parts,) = _inproj_bwd_dx(
        dzs, 0, w_in_b, x2, dy, norm_g, scale, shift, 512, "inproj_bwd_dx", [(_Hosted("chips", chip_sums), chip_sums)])
    (dcshift, dcscale, dng_c), _ = _inproj_bwd_dx([dzc_k, dzc_v], 4, w_in_b, ctx2, None, norm_g, cscale, cshift, C,
                                                  "ctx_bwd_dx")
    res_in = _adamw_sharded(win_t, gin_parts, m_w_in[0].T, v_w_in[0].T, 112, "adamw_w_in")
    res_out = _adamw_sharded(wout, gout_parts, m_w_out[0], v_w_out[0], 128, "adamw_w_out")

    zero_row = jnp.zeros((1, D_MODEL), F32)
    vec_rows = [(V_DSHIFT, dshift), (V_DSCALE, dscale), (V_DGATE, dgate), (V_DCSHIFT, dcshift),
                (V_DCSCALE, dcscale), (V_ZERO, zero_row), (V_DNG, dng), (V_DNG_CTX, dng_c)]
    vloc = _pack_vectors(vec_rows, dsg, dgq, dgk, dgk_c, loss_part)
    (vg,) = _allgather([vloc], "gather_small")
    small_w = (b_ada, norm_g, sgu_norm_g, w_spatial, b_spatial, q_norm_g, k_norm_g, rpb)
    small_m = (m_b_ada, m_norm_g, m_sgu_norm_g, m_w_spatial, m_b_spatial, m_q_norm_g, m_k_norm_g, m_rpb)
    small_v = (v_b_ada, v_norm_g, v_sgu_norm_g, v_w_spatial, v_b_spatial, v_q_norm_g, v_k_norm_g, v_rpb)
    res_small, loss_row = _adamw_small(vg, mg, small_w, small_m, small_v)

    dm_all = vg[:, V_DSHIFT:V_DGATE + 1, :].reshape(N_DEV, 3 * D_MODEL)
    dc_all = vg[:, V_DCSHIFT:V_ZERO + 1, :].reshape(N_DEV, 3 * D_MODEL)
    dm_sh = lax.dynamic_slice(dm_all, (0, me * ada_w), (N_DEV, ada_w))
    dc_sh = lax.dynamic_slice(dc_all, (0, me * ada_w), (N_DEV, ada_w))
    *res_ada, pc = _ada_bwd(s16, dm_sh, dc_sh, wada, m_w_ada[0], v_w_ada[0])
    (pc_g,) = _allgather([pc], "gather_cctx")
    res_cctx = _adamw_cctx(pc_g, c_ctx_row, m_c_ctx.reshape(1, D_MODEL), v_c_ctx.reshape(1, D_MODEL))

    loss = loss_row[0, 0]
    outs = [loss, grad_x[None]]
    for kind in range(4):
        by_name = dict(zip(SMALL_NAMES, res_small[kind]))
        by_name.update(c_ctx=res_cctx[kind].reshape(D_MODEL), w_ada=res_ada[kind][None],
                       w_in=res_in[kind].T[None], w_out=res_out[kind][None])
        outs += [by_name[nme] for nme in ("c_ctx", "w_ada", "b_ada", "norm_g", "w_in", "sgu_norm_g", "w_spatial",
                                          "b_spatial", "q_norm_g", "k_norm_g", "rpb", "w_out")]
    return tuple(outs)
```

```python
import functools

import numpy as np
import jax
import jax.numpy as jnp
from jax import lax
from jax.experimental import pallas as pl
from jax.experimental.pallas import tpu as pltpu

F32 = jnp.float32
BF16 = jnp.bfloat16
HI = lax.Precision.HIGHEST

N_DEV = 8
D_MODEL = 1024
D_A = 512
D_B = 512
D_IN = 3584
N_BRANCH = 7
HEAD_DIM = 64
N_HEADS = 8
GRID_W = 64
WIN_R = 8
WIN_C = 16
CHUNK = 128
SGU_GROUPS = 4
EPS = 1e-6
NEG_INF = -1e30
Q_ROWS = 4
K_ROWS = 12
TQ = Q_ROWS * GRID_W
TK = K_ROWS * GRID_W
N_DIAG = 22
ATT_SUB = 4
ATT_SCALE = HEAD_DIM ** -0.5
LOG2E = 1.4426950408889634
LN2 = 0.6931471805599453

ADAM_LR = 0.001
ADAM_B1 = 0.9
ADAM_B2 = 0.999
ADAM_EPS = 1e-08
ADAM_WD = 0.01
ADAM_STEP = 10

VMEM_LIMIT = 56 * 1024 * 1024
MESH = pl.DeviceIdType.MESH

V_DSHIFT, V_DSCALE, V_DGATE, V_DCSHIFT, V_DCSCALE, V_ZERO, V_DNG, V_DNG_CTX, V_MISC = range(9)
M_DBS, M_RPB, M_ROWS = 512, 544, 800


def _params(sem=None):
    return pltpu.CompilerParams(dimension_semantics=sem, vmem_limit_bytes=VMEM_LIMIT)


def _sigmoid(x):
    return 1.0 / (1.0 + jnp.exp(-x))


def _gelu_parts(x):
    cdf = 0.5 * (1.0 + lax.erf(x * 0.7071067811865476))
    pdf = jnp.exp(-0.5 * x * x) * 0.3989422804014327
    return x * cdf, cdf + x * pdf


def _nt(a, b):
    return lax.dot_general(a, b, (((1,), (1,)), ((), ())), preferred_element_type=F32)


def _tn(a, b):
    return lax.dot_general(a, b, (((0,), (0,)), ((), ())), preferred_element_type=F32)


def _dot2(v, ones_bf):
    hi = v.astype(BF16)
    lo = (v - hi.astype(F32)).astype(BF16)
    return (jnp.dot(hi, ones_bf, preferred_element_type=F32)
            + jnp.dot(lo, ones_bf, preferred_element_type=F32))


def _head_sum(v, ones_ref):
    return jnp.dot(v.astype(BF16), ones_ref[...], preferred_element_type=F32)


def _adam(w, g, m, v):
    m2 = ADAM_B1 * m + (1.0 - ADAM_B1) * g
    v2 = ADAM_B2 * v + (1.0 - ADAM_B2) * (g * g)
    m_hat = m2 / (1.0 - ADAM_B1 ** ADAM_STEP)
    v_hat = v2 / (1.0 - ADAM_B2 ** ADAM_STEP)
    delta = -ADAM_LR * (m_hat / (jnp.sqrt(v_hat) + ADAM_EPS) + ADAM_WD * w)
    return delta, m2, v2


def _chip_presum(arrs, name):
    n = len(arrs)
    n_chip = N_DEV // 2
    outs_shape = [jax.ShapeDtypeStruct((n_chip,) + tuple(a.shape[1:]), a.dtype) for a in arrs]

    def body(*refs):
        ins, sums = refs[:n], refs[n:2 * n]
        tmps = refs[2 * n:3 * n]
        s1, r1 = refs[3 * n:]
        c = lax.axis_index("c")
        sib = (lax.axis_index("x"), lax.axis_index("y"), 1 - c)
        all_chips = [(0, 0), (0, 1), (1, 0), (1, 1)]

        swaps = []
        for a in range(n):
            for q, (qx, qy) in enumerate(all_chips):
                cp = pltpu.make_async_remote_copy(
                    src_ref=ins[a].at[4 * qx + 2 * qy + (1 - c)], dst_ref=tmps[a].at[q],
                    send_sem=s1.at[n_chip * a + q], recv_sem=r1.at[n_chip * a + q],
                    device_id=sib, device_id_type=MESH)
                cp.start()
                swaps.append(cp)
        for a in range(n):
            for q, (qx, qy) in enumerate(all_chips):
                swaps[n_chip * a + q].wait_recv()
                both = ins[a][4 * qx + 2 * qy + c].astype(F32) + tmps[a][q].astype(F32)
                sums[a][q] = both.astype(sums[a].dtype)
        for cp in swaps:
            cp.wait_send()

    vm = pl.BlockSpec(memory_space=pltpu.VMEM)
    res = pl.pallas_call(
        body, name=name, out_shape=outs_shape, in_specs=[vm] * n, out_specs=[vm] * n,
        scratch_shapes=([pltpu.VMEM(s.shape, s.dtype) for s in outs_shape]
                        + [pltpu.SemaphoreType.DMA((n_chip * n,)), pltpu.SemaphoreType.DMA((n_chip * n,))]),
        compiler_params=_params(),
    )(*arrs)
    return list(res)


class _Hosted:
    def __init__(self, kind, src):
        self.kind = kind
        n_slot = {"a2a": N_DEV, "ag": N_DEV, "chips": N_DEV // 2}[kind]
        blk = src.shape if kind == "ag" else src.shape[1:]
        self.out_shape = jax.ShapeDtypeStruct((n_slot,) + tuple(blk), src.dtype)
        self.n_peer = n_slot - 1
        self.scratch = [pltpu.SemaphoreType.DMA((self.n_peer,)), pltpu.SemaphoreType.DMA((self.n_peer,)),
                        pltpu.SemaphoreType.DMA]

    def _copies(self, src, dst, send_sems, recv_sems, loc_sem, landing):
        x, y, c = lax.axis_index("x"), lax.axis_index("y"), lax.axis_index("c")
        if self.kind == "chips":
            me = 2 * x + y
            peers = [((px, py, c), 2 * px + py) for px, py in ((1 - x, y), (x, 1 - y), (1 - x, 1 - y))]
        else:
            me = 4 * x + 2 * y + c
            peers = []
            for k in range(1, N_DEV):
                px = 1 - x if (k >> 2) & 1 else x
                py = 1 - y if (k >> 1) & 1 else y
                pc = 1 - c if k & 1 else c
                peers.append(((px, py, pc), 4 * px + 2 * py + pc))
        remote = []
        for k, (peer, pid) in enumerate(peers):
            s = src if self.kind == "ag" else src.at[pid]
            remote.append(pltpu.make_async_remote_copy(
                src_ref=s, dst_ref=dst.at[pid if landing else me],
                send_sem=send_sems.at[k], recv_sem=recv_sems.at[k], device_id=peer, device_id_type=MESH))
        local = pltpu.make_async_copy(src if self.kind == "ag" else src.at[me], dst.at[me], loc_sem)
        return remote, local

    def start(self, src, dst, send_sems, recv_sems, loc_sem):
        remote, local = self._copies(src, dst, send_sems, recv_sems, loc_sem, landing=False)
        for cp in remote:
            cp.start()
        local.start()

    def wait(self, src, dst, send_sems, recv_sems, loc_sem):
        remote, local = self._copies(src, dst, send_sems, recv_sems, loc_sem, landing=True)
        for cp in remote:
            cp.wait_recv()
        for cp in remote:
            cp.wait_send()
        local.wait()


ANY_SPEC = pl.BlockSpec(memory_space=pl.ANY)


def _allgather_direct(arr, name):
    ex = _Hosted("ag", arr)

    def body(src, dst, *sems):
        ex.start(src, dst, *sems)
        ex.wait(src, dst, *sems)

    return pl.pallas_call(body, name=name, out_shape=ex.out_shape, in_specs=[ANY_SPEC], out_specs=ANY_SPEC,
                          scratch_shapes=ex.scratch)(arr)


def _ada_scratch(n_col):
    return ([pltpu.VMEM((N_DEV, 8, D_MODEL), F32), pltpu.VMEM((16, n_col), F32), pltpu.VMEM((N_DEV, 16, n_col), F32)]
            + [pltpu.SemaphoreType.DMA((N_DEV - 1,)) for _ in range(4)])


def _ada_modulation(cb_ref, cc_ref, w_ref, b_ref, cstack, part, parts, s1, r1, s2, r2):
    x, y, c = lax.axis_index("x"), lax.axis_index("y"), lax.axis_index("c")
    me = 4 * x + 2 * y + c
    peers = []
    for k in range(1, N_DEV):
        px = 1 - x if (k >> 2) & 1 else x
        py = 1 - y if (k >> 1) & 1 else y
        pc = 1 - c if k & 1 else c
        peers.append(((px, py, pc), 4 * px + 2 * py + pc))

    def exchange(src, dst, send_sems, recv_sems):
        for k, (peer, _) in enumerate(peers):
            pltpu.make_async_remote_copy(src_ref=src, dst_ref=dst.at[me], send_sem=send_sems.at[k],
                                         recv_sem=recv_sems.at[k], device_id=peer, device_id_type=MESH).start()
        dst[me] = src[...]
        waits = [pltpu.make_async_remote_copy(src_ref=src, dst_ref=dst.at[pid], send_sem=send_sems.at[k],
                                              recv_sem=recv_sems.at[k], device_id=peer, device_id_type=MESH)
                 for k, (peer, pid) in enumerate(peers)]
        for cp in waits:
            cp.wait_recv()
        for cp in waits:
            cp.wait_send()

    exchange(cb_ref, cstack, s1, r1)
    c_all = cstack[0]
    for d in range(1, N_DEV):
        c_all = c_all + cstack[d]
    row = lax.broadcasted_iota(jnp.int32, (8, D_MODEL), 0)
    cc = jnp.where(row == 0, jnp.broadcast_to(cc_ref[...], (8, D_MODEL)), 0.0)
    call = jnp.concatenate([c_all, cc], axis=0)
    s = call * _sigmoid(call)
    part[...] = jnp.dot(s, w_ref[...], preferred_element_type=F32, precision=HI) + b_ref[...]
    exchange(part, parts, s2, r2)
    return s


def _ada_bwd(s16, dm, dc, w, m, v):
    def body(s_ref, dm_ref, dc_ref, w_ref, m_ref, v_ref, g_ref, d_ref, m2_ref, v2_ref, pc_ref):
        dct = jnp.sum(dc_ref[...], axis=0, keepdims=True)
        row = lax.broadcasted_iota(jnp.int32, dc_ref.shape, 0)
        dcb = jnp.where(row == 0, jnp.broadcast_to(dct, dc_ref.shape), 0.0)
        dm16 = jnp.concatenate([dm_ref[...], dcb], axis=0)
        g = lax.dot_general(s_ref[...], dm16, (((0,), (0,)), ((), ())),
                            preferred_element_type=F32, precision=HI)
        w_ = w_ref[...]
        delta, m2, v2 = _adam(w_, g, m_ref[...], v_ref[...])
        g_ref[...] = g
        d_ref[...] = delta
        m2_ref[...] = m2
        v2_ref[...] = v2
        pc_ref[...] = lax.dot_general(dcb, w_, (((1,), (1,)), ((), ())),
                                      preferred_element_type=F32, precision=HI)

    sh = jax.ShapeDtypeStruct(w.shape, F32)
    return pl.pallas_call(
        body, name="ada_bwd",
        out_shape=[sh, sh, sh, sh, jax.ShapeDtypeStruct((8, D_MODEL), F32)],
        compiler_params=_params(),
    )(s16, dm, dc, w, m, v)


def _head_norm(zk, ones_ref, gain):
    ss = _head_sum(zk * zk, ones_ref)
    return zk * lax.rsqrt(ss * (1.0 / HEAD_DIM) + EPS) * gain


def _inproj_fwd(order, x, ng, c_blk, c_ctx_row, w_ada_sh, b_ada_sh, w_blk_t, w_out_blk, tm):
    T = x.shape[0]
    nt = T // tm
    n_pass = N_DEV // 2
    blk_rows = w_blk_t.shape[0]
    n_col = w_ada_sh.shape[1]
    rider = _Hosted("ag", w_out_blk)
    n_ada = len(_ada_scratch(n_col))

    def body(order_ref, x_ref, g_ref, cb_ref, cc_ref, wa_ref, ba_ref, wb_ref, wo_ref,
             h_out, z_ref, wt_out, wo_out, s_out, parts_out,
             hs, wt, modv, send_sems, recv_sems, loc_sem, h_sem, wt_sem, *rest):
        ada_sc, rider_sems = rest[:n_ada], rest[n_ada:]
        p, i = pl.program_id(0), pl.program_id(1)
        x, y, c = lax.axis_index("x"), lax.axis_index("y"), lax.axis_index("c")
        me, sib = (x, y, c), (x, y, 1 - c)
        chips = [(1 - x, y), (x, 1 - y), (1 - x, 1 - y)]

        def slot(px, py, pc):
            return 4 * px + 2 * py + pc

        def copy(k, block, to, src=None):
            return pltpu.make_async_remote_copy(
                src_ref=wt.at[slot(*block)] if src is None else src, dst_ref=wt.at[slot(*block)],
                send_sem=send_sems.at[k], recv_sem=recv_sems.at[k], device_id=to, device_id_type=MESH)

        own = pltpu.make_async_copy(wb_ref, wt.at[slot(*me)], loc_sem)
        h_copy = pltpu.make_async_copy(hs, h_out, h_sem)
        wt_copy = pltpu.make_async_copy(wt, wt_out, wt_sem)
        first = [copy(1 + j, me, (*chip, c), src=wb_ref) for j, chip in enumerate(chips)] + [copy(0, me, sib, src=wb_ref)]
        passed = [copy(4 + j, (*chip, c), sib) for j, chip in enumerate(chips)]

        @pl.when(jnp.logical_and(p == 0, i == 0))
        def _():
            s_out[...] = _ada_modulation(cb_ref, cc_ref, wa_ref, ba_ref, *ada_sc)
            own.start()
            for cp in first:
                cp.start()
            rider.start(wo_ref, wo_out, *rider_sems)
            parts = ada_sc[2]
            parts_out[...] = parts[...]
            my_row = pl.ds(slot(*me), 1)
            mod = jnp.concatenate([parts[d, my_row, :] for d in range(N_DEV)], axis=1)
            modv[0:1, :] = mod[:, 0:D_MODEL]
            modv[1:2, :] = mod[:, D_MODEL:2 * D_MODEL]
            own.wait()
            copy(0, sib, me).wait_recv()

        for j, chip in enumerate(chips):
            @pl.when(jnp.logical_and(p == j + 1, i == 0))
            def _(j=j, chip=chip):
                copy(1 + j, (*chip, c), me).wait_recv()
                passed[j].start()
                copy(4 + j, (*chip, 1 - c), me).wait_recv()
                if j == len(chips) - 1:
                    wt_copy.start()

        rows = pl.ds(pl.multiple_of(i * tm, tm), tm)

        @pl.when(p == 0)
        def _():
            xv = x_ref[...]
            r = lax.rsqrt(jnp.mean(xv * xv, axis=-1, keepdims=True) + EPS)
            hs[rows, :] = ((xv * r * g_ref[...]) * (1.0 + modv[1:2, :]) + modv[0:1, :]).astype(BF16)

        @pl.when(jnp.logical_and(p == 1, i == 0))
        def _():
            h_copy.start()

        w_pair = wt[pl.ds(2 * order_ref[p], 2)].reshape(2 * blk_rows, D_MODEL)
        z_ref[...] = _nt(hs[rows, :], w_pair).astype(BF16)

        @pl.when(jnp.logical_and(p == n_pass - 1, i == nt - 1))
        def _():
            for cp in first + passed:
                cp.wait_send()
            h_copy.wait()
            wt_copy.wait()
            rider.wait(wo_ref, wo_out, *rider_sems)

    whole = lambda shape: pl.BlockSpec(shape, lambda p, i, o: (0,) * len(shape))
    grid_spec = pltpu.PrefetchScalarGridSpec(
        num_scalar_prefetch=1, grid=(n_pass, nt),
        in_specs=[pl.BlockSpec((tm, D_MODEL), lambda p, i, o: (jnp.where(p == 0, i, nt - 1), 0)),
                  whole((1, D_MODEL)), whole((8, D_MODEL)), whole((1, D_MODEL)), whole((D_MODEL, n_col)),
                  whole((1, n_col)), ANY_SPEC, ANY_SPEC],
        out_specs=[ANY_SPEC, pl.BlockSpec((tm, 2 * blk_rows), lambda p, i, o: (i, o[p])), ANY_SPEC, ANY_SPEC,
                   whole((16, D_MODEL)), whole((N_DEV, 16, n_col))],
        scratch_shapes=[pltpu.VMEM((T, D_MODEL), BF16), pltpu.VMEM((N_DEV, blk_rows, D_MODEL), BF16),
                        pltpu.VMEM((8, D_MODEL), F32),
                        pltpu.SemaphoreType.DMA((7,)), pltpu.SemaphoreType.DMA((7,)), pltpu.SemaphoreType.DMA,
                        pltpu.SemaphoreType.DMA, pltpu.SemaphoreType.DMA] + _ada_scratch(n_col) + rider.scratch)
    return pl.pallas_call(
        body, name="inproj_fwd", grid_spec=grid_spec,
        out_shape=[jax.ShapeDtypeStruct((T, D_MODEL), BF16), jax.ShapeDtypeStruct((T, D_IN), BF16),
                   jax.ShapeDtypeStruct((N_DEV, blk_rows, D_MODEL), BF16), rider.out_shape,
                   jax.ShapeDtypeStruct((16, D_MODEL), F32), jax.ShapeDtypeStruct((N_DEV, 16, n_col), F32)],
        compiler_params=_params(("arbitrary", "arbitrary")),
    )(order, x, ng, c_blk, c_ctx_row, w_ada_sh, b_ada_sh, w_blk_t, w_out_blk)


def _qk_norm(z, ones_blk, gq, gk, tm):
    T = z.shape[0]

    def body(q_ref, k_ref, ones_ref, gq_ref, gk_ref, qs_ref, kn_ref):
        qs = _head_norm(q_ref[...].astype(F32), ones_ref, gq_ref[...]) * (ATT_SCALE * LOG2E)
        qs_ref[...] = qs.astype(BF16)
        kn_ref[...] = _head_norm(k_ref[...].astype(F32), ones_ref, gk_ref[...]).astype(BF16)

    v512 = pl.BlockSpec((1, 512), lambda i: (0, 0))
    return pl.pallas_call(
        body, name="qk_norm", grid=(T // tm,),
        in_specs=[pl.BlockSpec((tm, 512), lambda i: (i, 3)), pl.BlockSpec((tm, 512), lambda i: (i, 4)),
                  pl.BlockSpec((512, 512), lambda i: (0, 0)), v512, v512],
        out_specs=[pl.BlockSpec((tm, 512), lambda i: (i, 0)), pl.BlockSpec((tm, 512), lambda i: (i, 0))],
        out_shape=[jax.ShapeDtypeStruct((T, 512), BF16), jax.ShapeDtypeStruct((T, 512), BF16)],
        compiler_params=_params(("arbitrary",)),
    )(z, z, ones_blk, gq, gk)


def _ctx_fwd(ctx, ng, cscale, cshift, w_in_t, ones_blk, gk):
    C = ctx.shape[0]

    def body(x_ref, g_ref, sc_ref, sh_ref, w_ref, ones_ref, gk_ref, h_ref, z_ref, kn_ref):
        xv = x_ref[...]
        r = lax.rsqrt(jnp.mean(xv * xv, axis=-1, keepdims=True) + EPS)
        h = (xv * r * g_ref[...]) * (1.0 + sc_ref[...]) + sh_ref[...]
        hb = h.astype(BF16)
        h_ref[...] = hb
        zk = _nt(hb, w_ref[0:512, :])
        zv = _nt(hb, w_ref[512:1024, :])
        z_ref[:, 0:512] = zk.astype(BF16)
        z_ref[:, 512:1024] = zv.astype(BF16)
        kn_ref[...] = _head_norm(zk, ones_ref, gk_ref[...]).astype(BF16)

    vec = pl.BlockSpec((1, D_MODEL), lambda i: (0, 0))
    return pl.pallas_call(
        body, name="ctx_fwd", grid=(1,),
        in_specs=[pl.BlockSpec((C, D_MODEL), lambda i: (0, 0)), vec, vec, vec,
                  pl.BlockSpec((1024, D_MODEL), lambda i: (2, 0)),
                  pl.BlockSpec((512, 512), lambda i: (0, 0)), pl.BlockSpec((1, 512), lambda i: (0, 0))],
        out_specs=[pl.BlockSpec((C, D_MODEL), lambda i: (0, 0)), pl.BlockSpec((C, 1024), lambda i: (0, 0)),
                   pl.BlockSpec((C, 512), lambda i: (0, 0))],
        out_shape=[jax.ShapeDtypeStruct((C, D_MODEL), BF16), jax.ShapeDtypeStruct((C, 1024), BF16),
                   jax.ShapeDtypeStruct((C, 512), BF16)],
        compiler_params=_params(("arbitrary",)),
    )(ctx, ng, cscale, cshift, w_in_t, ones_blk, gk)


def _sgu_chunk_fwd(au, av, ag, sg, ws_bf, bsb):
    gu, dgu = _gelu_parts(au)
    gv, dgv = _gelu_parts(av)
    rr = lax.rsqrt(jnp.mean(gv * gv, axis=-1, keepdims=True) + EPS)
    vhat = gv * rr
    vn = vhat * sg
    mixed = jnp.dot(ws_bf, vn.astype(BF16), preferred_element_type=F32) + bsb
    sig = _sigmoid(ag)
    sl = ag * sig
    return gu * mixed * sl, (gu, dgu, dgv, rr, vhat, vn, mixed, sig, sl)


def _sgu_fwd(z, sgn, ws, bsb, tm):
    T = z.shape[0]

    def body(au_ref, av_ref, ag_ref, sg_ref, ws_ref, bsb_ref, o_ref):
        for g in range(SGU_GROUPS):
            ws_bf = ws_ref[g].astype(BF16)
            sg = sg_ref[:, 128 * g:128 * (g + 1)]
            bsb_g = bsb_ref[g]
            for j in range(tm // CHUNK):
                rs, cs = slice(CHUNK * j, CHUNK * (j + 1)), slice(128 * g, 128 * (g + 1))
                out, _ = _sgu_chunk_fwd(au_ref[rs, cs].astype(F32), av_ref[rs, cs].astype(F32),
                                        ag_ref[rs, cs].astype(F32), sg, ws_bf, bsb_g)
                o_ref[rs, cs] = out.astype(BF16)

    return pl.pallas_call(
        body, name="sgu_fwd", grid=(T // tm,),
        in_specs=[pl.BlockSpec((tm, 512), lambda i: (i, 0)), pl.BlockSpec((tm, 512), lambda i: (i, 1)),
                  pl.BlockSpec((tm, 512), lambda i: (i, 2)), pl.BlockSpec((1, 512), lambda i: (0, 0)),
                  pl.BlockSpec((SGU_GROUPS, CHUNK, CHUNK), lambda i: (0, 0, 0)),
                  pl.BlockSpec((SGU_GROUPS, CHUNK, 128), lambda i: (0, 0, 0))],
        out_specs=pl.BlockSpec((tm, 512), lambda i: (i, 0)),
        out_shape=jax.ShapeDtypeStruct((T, 512), BF16),
        compiler_params=_params(("arbitrary",)),
    )(z, z, z, sgn, ws, bsb)


def _attn_type(rb, nrb):
    return jnp.where(rb == 0, 0, jnp.where(rb == nrb - 1, 2, 1))


def _attn_specs(T, C):
    return [
        pl.BlockSpec((ATT_SUB * TQ, 128), lambda hp, st: (st, hp)),
        pl.BlockSpec((T, 128), lambda hp, st: (0, hp)),
        pl.BlockSpec((T, 128), lambda hp, st: (0, 20 + hp)),
        pl.BlockSpec((C, 128), lambda hp, st: (0, hp)),
        pl.BlockSpec((C, 128), lambda hp, st: (0, 4 + hp)),
        pl.BlockSpec((2, 2 * WIN_R, GRID_W, 128), lambda hp, st: (hp, 0, 0, 0)),
        pl.BlockSpec((3, TQ, TK), lambda hp, st: (0, 0, 0)),
        pl.BlockSpec((ATT_SUB * TQ, 128), lambda hp, st: (st, 24 + hp)),
    ]


def _build_bias(pairs_ref, mask_ref, bias_sc):
    for t in range(3):
        for hh in range(2):
            for i in range(Q_ROWS):
                for mm in range(K_ROWS // 2):
                    p = min(max(WIN_R - Q_ROWS * t + 2 * mm - i, 0), 2 * WIN_R - 1)
                    rs, cs = slice(GRID_W * i, GRID_W * (i + 1)), slice(128 * mm, 128 * (mm + 1))
                    bias_sc[t, hh, rs, cs] = (pairs_ref[hh, p] + mask_ref[t, rs, cs]) * LOG2E


def _attn_fwd(qs, kn, z, ckn, zc, pairs, row_mask):
    T, C = qs.shape[0], ckn.shape[0]
    rows = T // GRID_W
    nrb = rows // Q_ROWS

    def body(q_ref, k_ref, v_ref, ck_ref, cv_ref, pairs_ref, mask_ref, bg_ref, ob_ref, outb_ref, lse_ref, bias_sc):
        @pl.when(pl.program_id(1) == 0)
        def _():
            _build_bias(pairs_ref, mask_ref, bias_sc)

        ck2, cv2 = ck_ref[...], cv_ref[...]
        lane = lax.broadcasted_iota(jnp.int32, (1, 128), 1)
        for sub in range(ATT_SUB):
            rb = ATT_SUB * pl.program_id(1) + sub
            bias_ref = bias_sc.at[_attn_type(rb, nrb)]
            rs = slice(TQ * sub, TQ * (sub + 1))
            ks = pl.multiple_of(jnp.clip(Q_ROWS * rb - 4, 0, rows - K_ROWS) * GRID_W, GRID_W)
            q2 = q_ref[rs, :]
            k2 = k_ref[pl.ds(ks, TK), :]
            v2 = v_ref[pl.ds(ks, TK), :]
            o_acc = jnp.zeros((TQ, 128), F32)
            lse_acc = jnp.zeros((TQ, 128), F32)
            for hh in range(2):
                msk = (lane >= HEAD_DIM) == bool(hh)
                qm = jnp.where(msk, q2, jnp.zeros_like(q2))
                s = _nt(qm, k2) + bias_ref[hh]
                sc = _nt(qm, ck2)
                m = jnp.maximum(jnp.max(s, axis=-1, keepdims=True), jnp.max(sc, axis=-1, keepdims=True))
                p = jnp.exp2(s - m)
                pc = jnp.exp2(sc - m)
                va = jnp.where(msk, v2, jnp.ones_like(v2))
                cva = jnp.where(msk, cv2, jnp.ones_like(cv2))
                num = (jnp.dot(p.astype(BF16), va, preferred_element_type=F32)
                       + jnp.dot(pc.astype(BF16), cva, preferred_element_type=F32))
                den = pltpu.roll(num, HEAD_DIM, 1)
                o_acc = jnp.where(msk, num / den, o_acc)
                lse_acc = jnp.where(msk, m + jnp.log(den) * LOG2E, lse_acc)
            ob_ref[rs, :] = o_acc.astype(BF16)
            lse_ref[rs, :] = lse_acc
            bg = bg_ref[rs, :].astype(F32)
            outb_ref[rs, :] = (o_acc * (bg * _sigmoid(bg))).astype(BF16)

    tile = pl.BlockSpec((ATT_SUB * TQ, 128), lambda hp, st: (st, hp))
    return pl.pallas_call(
        body, name="attn_fwd", grid=(4, nrb // ATT_SUB),
        in_specs=_attn_specs(T, C),
        out_specs=[tile, tile, tile],
        out_shape=[jax.ShapeDtypeStruct((T, 512), BF16), jax.ShapeDtypeStruct((T, 512), BF16),
                   jax.ShapeDtypeStruct((T, 512), F32)],
        scratch_shapes=[pltpu.VMEM((3, 2, TQ, TK), F32)],
        compiler_params=_params(("arbitrary", "arbitrary")),
    )(qs, kn, z, ckn, zc, pairs, row_mask, z)


def _outproj_loss_bwd(x, tgt, out_a, out_b, gate, w_out, tm):
    T = x.shape[0]
    nt = T // tm

    def body(x_ref, t_ref, oa_ref, ob_ref, gate_ref, w_ref, dy_ref, dmc_ref, dw_ref, dgate_ref, loss_ref, acc):
        @pl.when(pl.program_id(0) == 0)
        def _():
            acc[...] = jnp.zeros_like(acc)
            dgate_ref[...] = jnp.zeros_like(dgate_ref)
            loss_ref[...] = jnp.zeros_like(loss_ref)

        oa, ob = oa_ref[...], ob_ref[...]
        gate_v = gate_ref[...]
        mix = (jnp.dot(oa, w_ref[0:512, :], preferred_element_type=F32)
               + jnp.dot(ob, w_ref[512:1024, :], preferred_element_type=F32))
        e = x_ref[...] + gate_v * mix - t_ref[...]
        se = jnp.sum(jnp.sum(e * e, axis=0, keepdims=True), axis=1, keepdims=True)
        loss_ref[...] += jnp.broadcast_to(se * (0.5 / D_MODEL), loss_ref.shape)
        dy = e * (1.0 / D_MODEL)
        dy_ref[...] = dy
        dgate_ref[...] += jnp.sum(dy * mix, axis=0, keepdims=True)
        dmix = (dy * gate_v).astype(BF16)
        dmc_ref[...] = _nt(dmix, w_ref[...]).astype(BF16)
        acc[0:512, :] += _tn(oa, dmix)
        acc[512:1024, :] += _tn(ob, dmix)

        @pl.when(pl.program_id(0) == nt - 1)
        def _():
            dw_ref[...] = acc[...].astype(BF16)

    row = lambda w: pl.BlockSpec((tm, w), lambda i: (i, 0))
    return pl.pallas_call(
        body, name="outproj_loss_bwd", grid=(nt,),
        in_specs=[row(D_MODEL), row(D_MODEL), row(512), row(512),
                  pl.BlockSpec((1, D_MODEL), lambda i: (0, 0)),
                  pl.BlockSpec((D_MODEL, D_MODEL), lambda i: (0, 0))],
        out_specs=[row(D_MODEL), row(D_MODEL), pl.BlockSpec((D_MODEL, D_MODEL), lambda i: (0, 0)),
                   pl.BlockSpec((1, D_MODEL), lambda i: (0, 0)), pl.BlockSpec((1, 128), lambda i: (0, 0))],
        out_shape=[jax.ShapeDtypeStruct((T, D_MODEL), F32), jax.ShapeDtypeStruct((T, D_MODEL), BF16),
                   jax.ShapeDtypeStruct((D_MODEL, D_MODEL), BF16), jax.ShapeDtypeStruct((1, D_MODEL), F32),
                   jax.ShapeDtypeStruct((1, 128), F32)],
        scratch_shapes=[pltpu.VMEM((D_MODEL, D_MODEL), F32)],
        compiler_params=_params(("arbitrary",)),
    )(x, tgt, out_a, out_b, gate, w_out)


def _attn_bwd(qs, kn, z, ckn, zc, pairs, row_mask, ob, lse, dmc, hsel, fold, riders):
    T, C = qs.shape[0], ckn.shape[0]
    rows = T // GRID_W
    nrb = rows // Q_ROWS
    n_st = nrb // ATT_SUB
    n_rid = len(riders)

    def body(q_ref, k_ref, v_ref, ck_ref, cv_ref, pairs_ref, mask_ref, bg_ref, ob_ref, lse_ref, do_ref,
             hsel_ref, fold_ref, *rest):
        rid_src = rest[:n_rid]
        dq_ref, dk_ref, dv_ref, dck_ref, dcv_ref, dbg_ref, drpb_ref = rest[n_rid:n_rid + 7]
        rid_dst = rest[n_rid + 7:2 * n_rid + 7]
        bias_sc, dacc_ref = rest[2 * n_rid + 7:2 * n_rid + 9]
        rid_sems = rest[2 * n_rid + 9:]
        hp, st = pl.program_id(0), pl.program_id(1)

        @pl.when(jnp.logical_and(hp == 0, st == 0))
        def _():
            for r, (ex, _) in enumerate(riders):
                ex.start(rid_src[r], rid_dst[r], *rid_sems[3 * r:3 * r + 3])

        @pl.when(st == 0)
        def _():
            _build_bias(pairs_ref, mask_ref, bias_sc)
            dk_ref[...] = jnp.zeros_like(dk_ref)
            dv_ref[...] = jnp.zeros_like(dv_ref)
            dck_ref[...] = jnp.zeros_like(dck_ref)
            dcv_ref[...] = jnp.zeros_like(dcv_ref)
            dacc_ref[...] = jnp.zeros_like(dacc_ref)

        ck2, cv2 = ck_ref[...], cv_ref[...]
        lane = lax.broadcasted_iota(jnp.int32, (1, 128), 1)
        for sub in range(ATT_SUB):
            rb = ATT_SUB * st + sub
            bias_ref = bias_sc.at[_attn_type(rb, nrb)]
            rs = slice(TQ * sub, TQ * (sub + 1))
            kb = jnp.clip(Q_ROWS * rb - 4, 0, rows - K_ROWS)
            ks = pl.multiple_of(kb * GRID_W, GRID_W)
            ebase = kb - Q_ROWS * rb + 11
            q2 = q_ref[rs, :]
            k2 = k_ref[pl.ds(ks, TK), :]
            v2 = v_ref[pl.ds(ks, TK), :]
            bg = bg_ref[rs, :].astype(F32)
            sig = _sigmoid(bg)
            obv = ob_ref[rs, :].astype(F32)
            dout = do_ref[rs, :].astype(F32)
            dbg_ref[rs, :] = (dout * obv * (sig * (1.0 + bg * (1.0 - sig)))).astype(BF16)
            d_o = dout * (bg * sig)
            d_oo = d_o * obv
            lse2 = lse_ref[rs, :]
            dq_acc = jnp.zeros((TQ, 128), F32)
            for hh in range(2):
                msk = (lane >= HEAD_DIM) == bool(hh)
                qm = jnp.where(msk, q2, jnp.zeros_like(q2))
                lse_h = jnp.max(jnp.where(msk, lse2, -jnp.inf), axis=-1, keepdims=True)
                p = jnp.exp2(_nt(qm, k2) + bias_ref[hh] - lse_h)
                pc = jnp.exp2(_nt(qm, ck2) - lse_h)
                dom_f = jnp.where(msk, d_o, 0.0)
                dom = dom_f.astype(BF16)
                delta = jnp.sum(jnp.where(msk, d_oo, 0.0), axis=-1, keepdims=True)
                d_hi = delta.astype(BF16).astype(F32)
                x0 = HEAD_DIM * (1 - hh)
                dom_aug = jnp.where(lane == x0, -d_hi, jnp.where(lane == x0 + 1, d_hi - delta, dom_f)).astype(BF16)
                extra = jnp.logical_or(lane == x0, lane == x0 + 1)
                va = jnp.where(msk, v2, jnp.where(extra, jnp.ones_like(v2), jnp.zeros_like(v2)))
                cva = jnp.where(msk, cv2, jnp.where(extra, jnp.ones_like(cv2), jnp.zeros_like(cv2)))
                ds = p * _nt(dom_aug, va)
                dsc = pc * _nt(dom_aug, cva)
                dsb, dscb = ds.astype(BF16), dsc.astype(BF16)
                dq_h = (jnp.dot(dsb, k2, preferred_element_type=F32)
                        + jnp.dot(dscb, ck2, preferred_element_type=F32))
                dq_acc = jnp.where(msk, dq_h, dq_acc)
                dk_ref[pl.ds(ks, TK), :] += _tn(dsb, qm)
                dv_ref[pl.ds(ks, TK), :] += _tn(p.astype(BF16), dom)
                dck_ref[...] += _tn(dscb, qm)
                dcv_ref[...] += _tn(pc.astype(BF16), dom)
                for i in range(Q_ROWS):
                    for mm in range(K_ROWS // 2):
                        dacc_ref[hh, ebase + (2 * mm - i)] += ds[GRID_W * i:GRID_W * (i + 1),
                                                                 128 * mm:128 * (mm + 1)]
            dq_ref[rs, :] = dq_acc

        @pl.when(st == n_st - 1)
        def _():
            for hh in range(2):
                drpb_ref[hh] = _rpb_diag_sums(dacc_ref.at[hh], hsel_ref, fold_ref)

        @pl.when(jnp.logical_and(hp == pl.num_programs(0) - 1, st == n_st - 1))
        def _():
            for r, (ex, _) in enumerate(riders):
                ex.wait(rid_src[r], rid_dst[r], *rid_sems[3 * r:3 * r + 3])

    tile = pl.BlockSpec((ATT_SUB * TQ, 128), lambda hp, st: (st, hp))
    colT = pl.BlockSpec((T, 128), lambda hp, st: (0, hp))
    colC = pl.BlockSpec((C, 128), lambda hp, st: (0, hp))
    res = pl.pallas_call(
        body, name="attn_bwd", grid=(4, n_st),
        in_specs=(_attn_specs(T, C) + [tile, tile, pl.BlockSpec((ATT_SUB * TQ, 128), lambda hp, st: (st, 4 + hp)),
                                       pl.BlockSpec((8, 128, 128), lambda hp, st: (0, 0, 0)),
                                       pl.BlockSpec((32, 256), lambda hp, st: (0, 0))]
                  + [ANY_SPEC] * n_rid),
        out_specs=([tile, colT, colT, colC, colC, tile, pl.BlockSpec((2, 32, 128), lambda hp, st: (hp, 0, 0))]
                   + [ANY_SPEC] * n_rid),
        out_shape=([jax.ShapeDtypeStruct((T, 512), F32), jax.ShapeDtypeStruct((T, 512), F32),
                    jax.ShapeDtypeStruct((T, 512), F32), jax.ShapeDtypeStruct((C, 512), F32),
                    jax.ShapeDtypeStruct((C, 512), F32), jax.ShapeDtypeStruct((T, 512), BF16),
                    jax.ShapeDtypeStruct((N_HEADS, 32, 128), F32)] + [ex.out_shape for ex, _ in riders]),
        scratch_shapes=([pltpu.VMEM((3, 2, TQ, TK), F32), pltpu.VMEM((2, N_DIAG, GRID_W, 128), F32)]
                        + [s for ex, _ in riders for s in ex.scratch]),
        compiler_params=_params(("arbitrary", "arbitrary")),
    )(qs, kn, z, ckn, zc, pairs, row_mask, z, ob, lse, dmc, hsel, fold, *[a for _, a in riders])
    return res[:7], res[7:]


def _rpb_diag_sums(a_ref, hsel_ref, fold_ref):
    n_off = 2 * WIN_C - 1
    n_dr = 2 * WIN_R - 1
    qc = lax.broadcasted_iota(jnp.int32, (GRID_W, 128), 0)
    lane = lax.broadcasted_iota(jnp.int32, (GRID_W, 128), 1)
    diff = lane % GRID_W - qc + (WIN_C - 1)
    left = lane < GRID_W

    def by_dr(dr):
        return a_ref[dr + 4] + pltpu.roll(a_ref[dr + 3], GRID_W, 1)

    out = jnp.zeros((32, 128), F32)
    for j in range((n_dr + 1) // 2):
        hi = pltpu.roll(by_dr(2 * j + 1), GRID_W, 1) if 2 * j + 1 < n_dr else 0.0
        pair = jnp.where(left, by_dr(2 * j), hi)
        parts = []
        for o in range(n_off):
            mv = jnp.where(diff == o, pair, 0.0)
            acc = mv[0:8]
            for r8 in range(1, GRID_W // 8):
                acc = acc + mv[8 * r8:8 * (r8 + 1)]
            parts.append(acc)
        parts.append(jnp.zeros((8, 128), F32))
        stack = jnp.concatenate(parts, axis=0)
        s_hi = stack.astype(BF16)
        s_lo = (stack - s_hi.astype(F32)).astype(BF16)
        per_o = (jnp.dot(fold_ref[...], s_hi, preferred_element_type=F32)
                 + jnp.dot(fold_ref[...], s_lo, preferred_element_type=F32))
        out = out + _dot2(per_o, hsel_ref[j])
    return out


def _qk_bwd(z, col_q, col_k, dqs, dk, dv, ones_blk, gq, gk, foldm, tm, name):
    T = dk.shape[0]
    with_q = dqs is not None

    def norm_bwd(raw, dn, gain, ones_ref):
        rr = lax.rsqrt(_head_sum(raw * raw, ones_ref) * (1.0 / HEAD_DIM) + EPS)
        hat = raw * rr
        dgain = jnp.sum(dn * hat, axis=0, keepdims=True)
        dhat = dn * gain
        mean = _head_sum(dhat * hat, ones_ref) * (1.0 / HEAD_DIM)
        return rr * (dhat - hat * mean), dgain

    def body(*refs):
        if with_q:
            (bq_ref, bk_ref, dq_ref, dk_ref, dv_ref, ones_ref, gq_ref, gk_ref, fold_ref,
             dbq_ref, dbk_ref, dbv_ref, dgq_ref, dgk_ref, accq, acck) = refs
        else:
            (bk_ref, dk_ref, dv_ref, ones_ref, gk_ref, fold_ref, dbk_ref, dbv_ref, dgk_ref, acck) = refs
        i = pl.program_id(0)

        @pl.when(i == 0)
        def _():
            acck[...] = jnp.zeros_like(acck)
            if with_q:
                accq[...] = jnp.zeros_like(accq)

        dbk, dgk = norm_bwd(bk_ref[...].astype(F32), dk_ref[...] * LN2, gk_ref[...], ones_ref)
        dbk_ref[...] = dbk.astype(BF16)
        acck[...] += dgk
        dbv_ref[...] = dv_ref[...].astype(BF16)
        if with_q:
            dbq, dgq = norm_bwd(bq_ref[...].astype(F32), dq_ref[...] * ATT_SCALE, gq_ref[...], ones_ref)
            dbq_ref[...] = dbq.astype(BF16)
            accq[...] += dgq

        @pl.when(i == pl.num_programs(0) - 1)
        def _():
            dgk_ref[...] = jnp.dot(jnp.broadcast_to(acck[...], (8, 512)), fold_ref[...],
                                   preferred_element_type=F32, precision=HI)
            if with_q:
                dgq_ref[...] = jnp.dot(jnp.broadcast_to(accq[...], (8, 512)), fold_ref[...],
                                       preferred_element_type=F32, precision=HI)

    row = pl.BlockSpec((tm, 512), lambda i: (i, 0))
    cst = lambda a, b: pl.BlockSpec((a, b), lambda i: (0, 0))
    out_row = jax.ShapeDtypeStruct((T, 512), BF16)
    gvec = jax.ShapeDtypeStruct((8, 128), F32)
    if with_q:
        in_specs = [pl.BlockSpec((tm, 512), lambda i: (i, col_q)), pl.BlockSpec((tm, 512), lambda i: (i, col_k)),
                    row, row, row, cst(512, 512), cst(1, 512), cst(1, 512), cst(512, 128)]
        args = (z, z, dqs, dk, dv, ones_blk, gq, gk, foldm)
        out_specs = [row, row, row, cst(8, 128), cst(8, 128)]
        out_shape = [out_row, out_row, out_row, gvec, gvec]
        scratch = [pltpu.VMEM((1, 512), F32), pltpu.VMEM((1, 512), F32)]
    else:
        in_specs = [pl.BlockSpec((tm, 512), lambda i: (i, col_k)), row, row, cst(512, 512), cst(1, 512), cst(512, 128)]
        args = (z, dk, dv, ones_blk, gk, foldm)
        out_specs = [row, row, cst(8, 128)]
        out_shape = [out_row, out_row, gvec]
        scratch = [pltpu.VMEM((1, 512), F32)]
    return pl.pallas_call(
        body, name=name, grid=(T // tm,), in_specs=in_specs, out_specs=out_specs, out_shape=out_shape,
        scratch_shapes=scratch, compiler_params=_params(("arbitrary",)),
    )(*args)


def _sgu_bwd(z, dmc, sgn, ws, wst, bsb, ones8, tm):
    T = z.shape[0]

    def body(au_ref, av_ref, ag_ref, d_ref, sg_ref, ws_ref, wst_ref, bsb_ref, ones_ref,
             dau_ref, dav_ref, dag_ref, dws_ref, dbs_ref, dsg_ref):
        @pl.when(pl.program_id(0) == 0)
        def _():
            dws_ref[...] = jnp.zeros_like(dws_ref)
            dbs_ref[...] = jnp.zeros_like(dbs_ref)
            dsg_ref[...] = jnp.zeros_like(dsg_ref)

        for g in range(SGU_GROUPS):
            ws_bf = ws_ref[g].astype(BF16)
            wst_bf = wst_ref[g].astype(BF16)
            sg = sg_ref[:, 128 * g:128 * (g + 1)]
            bsb_g = bsb_ref[g]
            for j in range(tm // CHUNK):
                rs, cs = slice(CHUNK * j, CHUNK * (j + 1)), slice(128 * g, 128 * (g + 1))
                au, av, ag = (au_ref[rs, cs].astype(F32), av_ref[rs, cs].astype(F32), ag_ref[rs, cs].astype(F32))
                d = d_ref[rs, cs].astype(F32)
                _, (gu, dgu, dgv, rr, vhat, vn, mixed, sig, sl) = _sgu_chunk_fwd(au, av, ag, sg, ws_bf, bsb_g)
                dau_ref[rs, cs] = (d * mixed * sl * dgu).astype(BF16)
                dag_ref[rs, cs] = (d * gu * mixed * (sig * (1.0 + ag * (1.0 - sig)))).astype(BF16)
                dmixed = d * gu * sl
                dmb = dmixed.astype(BF16)
                dm_lo = (dmixed - dmb.astype(F32)).astype(BF16)
                dbs_ref[g] += _nt(ones_ref[...], dmb) + _nt(ones_ref[...], dm_lo)
                dws_ref[g] += _nt(dmb, vn.astype(BF16))
                dvn = jnp.dot(wst_bf, dmb, preferred_element_type=F32)
                dsg_ref[:, 128 * g:128 * (g + 1)] += jnp.sum(dvn * vhat, axis=0, keepdims=True)
                dvhat = dvn * sg
                mean = jnp.mean(dvhat * vhat, axis=-1, keepdims=True)
                dav_ref[rs, cs] = (rr * (dvhat - vhat * mean) * dgv).astype(BF16)

    row = lambda col: pl.BlockSpec((tm, 512), lambda i: (i, col))
    c3 = lambda n: pl.BlockSpec((SGU_GROUPS, CHUNK, n), lambda i: (0, 0, 0))
    out_row = jax.ShapeDtypeStruct((T, 512), BF16)
    return pl.pallas_call(
        body, name="sgu_bwd", grid=(T // tm,),
        in_specs=[row(0), row(1), row(2), row(0), pl.BlockSpec((1, 512), lambda i: (0, 0)),
                  c3(CHUNK), c3(CHUNK), c3(128), pl.BlockSpec((8, 128), lambda i: (0, 0))],
        out_specs=[row(0), row(0), row(0), c3(CHUNK), pl.BlockSpec((SGU_GROUPS, 8, CHUNK), lambda i: (0, 0, 0)),
                   pl.BlockSpec((1, 512), lambda i: (0, 0))],
        out_shape=[out_row, out_row, out_row, jax.ShapeDtypeStruct((SGU_GROUPS, CHUNK, CHUNK), F32),
                   jax.ShapeDtypeStruct((SGU_GROUPS, 8, CHUNK), F32), jax.ShapeDtypeStruct((1, 512), F32)],
        compiler_params=_params(("arbitrary",)),
    )(z, z, z, dmc, sgn, ws, wst, bsb, ones8)


def _inproj_bwd_dx(dzs, w_cols, w_in_t, x, dy, ng, scale, shift, tm, name, riders=()):
    T = x.shape[0]
    n = len(dzs)
    nt = T // tm
    with_dx = dy is not None
    n_own_in = n + 5 + with_dx
    n_own_out = 3 + with_dx
    n_rid = len(riders)

    def body(*refs):
        dz_refs = refs[:n]
        own = refs[n:n_own_in] + refs[n_own_in + n_rid:n_own_in + n_rid + n_own_out]
        rid_src = refs[n_own_in:n_own_in + n_rid]
        rid_dst = refs[n_own_in + n_rid + n_own_out:n_own_in + 2 * n_rid + n_own_out]
        rid_sems = refs[n_own_in + 2 * n_rid + n_own_out:]
        if with_dx:
            w_ref, x_ref, dy_ref, g_ref, sc_ref, sh_ref, gx_ref, dsh_ref, dsc_ref, dg_ref = own
        else:
            w_ref, x_ref, g_ref, sc_ref, sh_ref, dsh_ref, dsc_ref, dg_ref = own

        @pl.when(pl.program_id(0) == 0)
        def _():
            for r, (ex, _) in enumerate(riders):
                ex.start(rid_src[r], rid_dst[r], *rid_sems[3 * r:3 * r + 3])
            dsh_ref[...] = jnp.zeros_like(dsh_ref)
            dsc_ref[...] = jnp.zeros_like(dsc_ref)
            dg_ref[...] = jnp.zeros_like(dg_ref)

        dh = jnp.dot(dz_refs[0][...], w_ref[0:512, :], preferred_element_type=F32)
        for k in range(1, n):
            dh = dh + jnp.dot(dz_refs[k][...], w_ref[512 * k:512 * (k + 1), :], preferred_element_type=F32)
        xv = x_ref[...]
        r = lax.rsqrt(jnp.mean(xv * xv, axis=-1, keepdims=True) + EPS)
        xn = xv * r
        gv, op = g_ref[...], 1.0 + sc_ref[...]
        dsh_ref[...] += jnp.sum(dh, axis=0, keepdims=True)
        dsc_ref[...] += jnp.sum(dh * xn * gv, axis=0, keepdims=True)
        dg_ref[...] += jnp.sum(dh * op * xn, axis=0, keepdims=True)
        if with_dx:
            dxn = dh * (gv * op)
            gx_ref[...] = r * (dxn - xn * jnp.mean(dxn * xn, axis=-1, keepdims=True)) + dy_ref[...]

        @pl.when(pl.program_id(0) == nt - 1)
        def _():
            for r, (ex, _) in enumerate(riders):
                ex.wait(rid_src[r], rid_dst[r], *rid_sems[3 * r:3 * r + 3])

    vec = pl.BlockSpec((1, D_MODEL), lambda i: (0, 0))
    rowf = pl.BlockSpec((tm, D_MODEL), lambda i: (i, 0))
    in_specs = [pl.BlockSpec((tm, 512), lambda i: (i, 0))] * n
    in_specs += [pl.BlockSpec((512 * n, D_MODEL), lambda i: (w_cols // n, 0)), rowf]
    args = list(dzs) + [w_in_t, x]
    vshape = jax.ShapeDtypeStruct((1, D_MODEL), F32)
    out_specs, out_shape = [vec, vec, vec], [vshape, vshape, vshape]
    if with_dx:
        in_specs.append(rowf)
        args.append(dy)
        out_specs = [rowf] + out_specs
        out_shape = [jax.ShapeDtypeStruct((T, D_MODEL), F32)] + out_shape
    in_specs += [vec, vec, vec] + [ANY_SPEC] * n_rid
    args += [ng, scale, shift] + [a for _, a in riders]
    res = pl.pallas_call(
        body, name=name, grid=(nt,), in_specs=in_specs, out_specs=out_specs + [ANY_SPEC] * n_rid,
        out_shape=out_shape + [ex.out_shape for ex, _ in riders],
        scratch_shapes=[s for ex, _ in riders for s in ex.scratch],
        compiler_params=_params(("arbitrary",)),
    )(*args)
    return res[:n_own_out], res[n_own_out:]


def _inproj_bwd_dw(h, dzs, hc, dzc_k, dzc_v, tk, riders=()):
    T = h.shape[0]
    nt = T // tk
    n = len(dzs)
    n_rid = len(riders)

    def body(*refs):
        h_ref = refs[0]
        dz_refs = refs[1:1 + n]
        hc_ref, dzck_ref, dzcv_ref = refs[1 + n:4 + n]
        rid_src = refs[4 + n:4 + n + n_rid]
        o_ref = refs[4 + n + n_rid]
        rid_dst = refs[5 + n + n_rid:5 + n + 2 * n_rid]
        acc = refs[5 + n + 2 * n_rid]
        rid_sems = refs[6 + n + 2 * n_rid:]
        t = pl.program_id(0)

        @pl.when(t == 0)
        def _():
            for r, (ex, _) in enumerate(riders):
                ex.start(rid_src[r], rid_dst[r], *rid_sems[3 * r:3 * r + 3])
            acc[...] = jnp.zeros_like(acc)
            acc[512 * 4:512 * 5, :] = _tn(dzck_ref[...], hc_ref[...])
            acc[512 * 5:512 * 6, :] = _tn(dzcv_ref[...], hc_ref[...])

        hv = h_ref[...]
        for k in range(n):
            acc[512 * k:512 * (k + 1), :] += _tn(dz_refs[k][...], hv)

        @pl.when(t == nt - 1)
        def _():
            o_ref[...] = acc[...].astype(BF16)
            for r, (ex, _) in enumerate(riders):
                ex.wait(rid_src[r], rid_dst[r], *rid_sems[3 * r:3 * r + 3])

    whole = lambda a: pl.BlockSpec(a.shape, lambda t: (0, 0))
    res = pl.pallas_call(
        body, name="inproj_bwd_dw", grid=(nt,),
        in_specs=([pl.BlockSpec((tk, D_MODEL), lambda t: (t, 0))] + [pl.BlockSpec((tk, 512), lambda t: (t, 0))] * n
                  + [whole(hc), whole(dzc_k), whole(dzc_v)] + [ANY_SPEC] * n_rid),
        out_specs=[pl.BlockSpec((512 * n, D_MODEL), lambda t: (0, 0))] + [ANY_SPEC] * n_rid,
        out_shape=[jax.ShapeDtypeStruct((512 * n, D_MODEL), BF16)] + [ex.out_shape for ex, _ in riders],
        scratch_shapes=[pltpu.VMEM((512 * n, D_MODEL), F32)] + [s for ex, _ in riders for s in ex.scratch],
        compiler_params=_params(("arbitrary",)),
    )(h, *dzs, hc, dzc_k, dzc_v, *[a for _, a in riders])
    return res[0], res[1:]


def _adamw_sharded(w, gparts, m, v, tr, name):
    R, C = w.shape
    n_part = gparts.shape[0]

    def body(w_ref, gp_ref, m_ref, v_ref, g_ref, d_ref, m2_ref, v2_ref):
        g = gp_ref[0].astype(F32)
        for d in range(1, n_part):
            g = g + gp_ref[d].astype(F32)
        delta, m2, v2 = _adam(w_ref[...], g, m_ref[...], v_ref[...])
        g_ref[...] = g
        d_ref[...] = delta
        m2_ref[...] = m2
        v2_ref[...] = v2

    row = pl.BlockSpec((tr, C), lambda i: (i, 0))
    sh = jax.ShapeDtypeStruct((R, C), F32)
    return pl.pallas_call(
        body, name=name, grid=(R // tr,),
        in_specs=[row, pl.BlockSpec((n_part, tr, C), lambda i: (0, i, 0)), row, row],
        out_specs=[row, row, row, row], out_shape=[sh, sh, sh, sh],
        compiler_params=_params(("arbitrary",)),
    )(w, gparts, m, v)


def _pack_vectors(vec_rows, dsg, dgq, dgk, dgk_c, loss_part):
    n_vec = len(vec_rows)

    def body(*refs):
        vecs = refs[:n_vec]
        dsg_ref, dgq_ref, dgk_ref, dgkc_ref, loss_ref, v_ref = refs[n_vec:]
        row = lax.broadcasted_iota(jnp.int32, (16, D_MODEL), 0)
        misc = jnp.concatenate([dsg_ref[...], dgq_ref[0:1, :], dgk_ref[0:1, :], dgkc_ref[0:1, :],
                                loss_ref[...]], axis=1)
        v = jnp.where(row == V_MISC, jnp.broadcast_to(misc, (16, D_MODEL)), 0.0)
        for (r, _), ref in zip(vec_rows, vecs):
            v = jnp.where(row == r, jnp.broadcast_to(ref[...], (16, D_MODEL)), v)
        v_ref[...] = v

    return pl.pallas_call(
        body, name="pack_vectors", out_shape=jax.ShapeDtypeStruct((16, D_MODEL), F32), compiler_params=_params(),
    )(*[a for _, a in vec_rows], dsg, dgq, dgk, dgk_c, loss_part)


def _pack_matrices(dws, dbs, drpb):
    def body(dws_ref, dbs_ref, drpb_ref, m_ref):
        for g in range(SGU_GROUPS):
            m_ref[128 * g:128 * (g + 1), :] = dws_ref[g]
            m_ref[M_DBS + 8 * g:M_DBS + 8 * (g + 1), :] = dbs_ref[g]
        for hd in range(N_HEADS):
            m_ref[M_RPB + 32 * hd:M_RPB + 32 * (hd + 1), :] = drpb_ref[hd]

    return pl.pallas_call(
        body, name="pack_matrices", out_shape=jax.ShapeDtypeStruct((M_ROWS, 128), F32), compiler_params=_params(),
    )(dws, dbs, drpb)


SMALL_NAMES = ("b_ada", "norm_g", "sgu_norm_g", "w_spatial", "b_spatial", "q_norm_g", "k_norm_g", "rpb")


def _adamw_small(vg, mg, ws, ms, vs):
    k = len(SMALL_NAMES)

    def body(*refs):
        vg_ref, mg_ref = refs[0], refs[1]
        w_refs = dict(zip(SMALL_NAMES, refs[2:2 + k]))
        m_refs = dict(zip(SMALL_NAMES, refs[2 + k:2 + 2 * k]))
        v_refs = dict(zip(SMALL_NAMES, refs[2 + 2 * k:2 + 3 * k]))
        o_refs = [dict(zip(SMALL_NAMES, refs[2 + (3 + i) * k:2 + (4 + i) * k])) for i in range(4)]
        loss_ref = refs[2 + 7 * k]

        sv = vg_ref[0]
        for d in range(1, N_DEV):
            sv = sv + vg_ref[d]
        loss_ref[...] = sv[V_MISC:V_MISC + 1, 896:1024]

        def total(lo, hi):
            s = mg_ref[0, lo:hi, :]
            for d in range(1, N_DEV):
                s = s + mg_ref[d, lo:hi, :]
            return s

        def emit(name, idx, g):
            res = _adam(w_refs[name][idx], g, m_refs[name][idx], v_refs[name][idx])
            for o, val in zip(o_refs, (g,) + res):
                o[name][idx] = val

        everything = (slice(None), slice(None))
        row = lambda r: sv[r:r + 1, :]
        emit("b_ada", everything, jnp.concatenate(
            [row(V_DSHIFT) + row(V_DCSHIFT), row(V_DSCALE) + row(V_DCSCALE), row(V_DGATE)], axis=1))
        emit("norm_g", everything, row(V_DNG) + row(V_DNG_CTX))
        misc = row(V_MISC)
        emit("sgu_norm_g", everything, misc[:, 0:512])
        emit("q_norm_g", everything, misc[:, 512:512 + HEAD_DIM])
        emit("k_norm_g", everything, misc[:, 640:640 + HEAD_DIM] + misc[:, 768:768 + HEAD_DIM])
        for g in range(SGU_GROUPS):
            emit("w_spatial", (0, g), total(128 * g, 128 * (g + 1)))
            emit("b_spatial", (0, slice(g, g + 1), slice(None)), total(M_DBS + 8 * g, M_DBS + 8 * (g + 1))[0:1, :])
        for hd in range(N_HEADS):
            by_dc = total(M_RPB + 32 * hd, M_RPB + 32 * (hd + 1))
            emit("rpb", (0, hd), by_dc.T[0:2 * WIN_R - 1, 0:2 * WIN_C - 1])

    shapes = [jax.ShapeDtypeStruct(w.shape, F32) for w in ws]
    res = pl.pallas_call(body, name="adamw_small", out_shape=shapes * 4 + [jax.ShapeDtypeStruct((1, 128), F32)],
                         compiler_params=_params())(vg, mg, *ws, *ms, *vs)
    return [res[i * k:(i + 1) * k] for i in range(4)], res[4 * k]


def _adamw_cctx(pc_g, w, m, v):
    def body(pc_ref, w_ref, m_ref, v_ref, g_ref, d_ref, m2_ref, v2_ref):
        pc = pc_ref[0, 0:1, :]
        for d in range(1, N_DEV):
            pc = pc + pc_ref[d, 0:1, :]
        cc = w_ref[...]
        sig = _sigmoid(cc)
        g = pc * (sig * (1.0 + cc * (1.0 - sig)))
        delta, m2, v2 = _adam(cc, g, m_ref[...], v_ref[...])
        g_ref[...] = g
        d_ref[...] = delta
        m2_ref[...] = m2
        v2_ref[...] = v2

    sh = jax.ShapeDtypeStruct((1, D_MODEL), F32)
    return pl.pallas_call(body, name="adamw_cctx", out_shape=[sh, sh, sh, sh], compiler_params=_params())(
        pc_g, w, m, v)


def _block_ones(n, blk):
    i = np.arange(n)
    return jnp.asarray((i[:, None] // blk == i[None, :] // blk).astype(np.float32), BF16)


def _rpb_pairs(rpb):
    n_off = 2 * WIN_C - 1
    cols = np.arange(GRID_W)
    c0 = np.clip(cols - WIN_C // 2, 0, GRID_W - WIN_C)
    in_win = (cols[None, :] >= c0[:, None]) & (cols[None, :] < c0[:, None] + WIN_C)
    dc = np.clip(cols[None, :] - cols[:, None] + (WIN_C - 1), 0, n_off - 1)
    expand = (dc[None] == np.arange(n_off)[:, None, None]) & in_win[None]
    toep = jnp.einsum("hrd,dqk->hrqk", rpb, jnp.asarray(expand, F32), precision=HI)
    toep = toep + jnp.asarray(np.where(in_win, 0.0, NEG_INF).astype(np.float32))
    neg = jnp.full((N_HEADS, 1, GRID_W, GRID_W), NEG_INF, F32)
    ext = jnp.concatenate([neg, toep, neg], axis=1)
    return jnp.concatenate([ext[:, :-1], ext[:, 1:]], axis=-1)


def _row_mask(rows):
    nrb = rows // Q_ROWS
    valid = np.zeros((3, Q_ROWS, 1, K_ROWS, 1), bool)
    for t, rb in enumerate((0, 1, nrb - 1)):
        kb = int(np.clip(Q_ROWS * rb - 4, 0, rows - K_ROWS))
        for i in range(Q_ROWS):
            r0 = int(np.clip(Q_ROWS * rb + i - WIN_R // 2, 0, rows - WIN_R))
            for j in range(K_ROWS):
                valid[t, i, 0, j, 0] = r0 <= kb + j < r0 + WIN_R
    full = np.broadcast_to(valid, (3, Q_ROWS, GRID_W, K_ROWS, GRID_W)).reshape(3, TQ, TK)
    return jnp.asarray(np.where(full, 0.0, NEG_INF).astype(np.float32))


def kernel(x, c, ctx, c_ctx, w_ada, b_ada, norm_g, w_in, sgu_norm_g, w_spatial, b_spatial, q_norm_g, k_norm_g, rpb, w_out, loss_target, m_c_ctx, m_w_ada, m_b_ada, m_norm_g, m_w_in, m_sgu_norm_g, m_w_spatial, m_b_spatial, m_q_norm_g, m_k_norm_g, m_rpb, m_w_out, v_c_ctx, v_w_ada, v_b_ada, v_norm_g, v_w_in, v_sgu_norm_g, v_w_spatial, v_b_spatial, v_q_norm_g, v_k_norm_g, v_rpb, v_w_out):
    me = 4 * lax.axis_index("x") + 2 * lax.axis_index("y") + lax.axis_index("c")
    x2, ctx2, tgt2 = x[0], ctx[0], loss_target[0]
    T, C = x2.shape[0], ctx2.shape[0]
    rows = T // GRID_W
    wada, win_t, wout = w_ada[0], w_in[0].T, w_out[0]
    ada_w = wada.shape[1]
    win_w = win_t.shape[0]

    row8 = lax.broadcasted_iota(jnp.int32, (8, D_MODEL), 0)
    c_blk = jnp.where(row8 == me, jnp.broadcast_to(c, (8, D_MODEL)), 0.0)
    b_sh = lax.dynamic_slice(b_ada, (0, me * ada_w), (1, ada_w))
    c_ctx_row = c_ctx.reshape(1, D_MODEL)

    ones512 = _block_ones(512, HEAD_DIM)
    ones8 = jnp.ones((8, 128), BF16)
    foldm = jnp.asarray((np.arange(512)[:, None] % HEAD_DIM == np.arange(128)[None, :]).astype(np.float32))
    lane_half = np.arange(128)[None, :, None] // GRID_W
    hsel = jnp.asarray((2 * np.arange(8)[:, None, None] + lane_half == np.arange(128)[None, None, :]).astype(np.float32),
                       BF16)
    foldr = jnp.asarray((np.arange(256)[None, :] // 8 == np.arange(32)[:, None]).astype(np.float32), BF16)
    gq512 = jnp.tile(q_norm_g, (1, N_HEADS))
    gk512 = jnp.tile(k_norm_g, (1, N_HEADS))
    ws = w_spatial[0]
    wst = ws.transpose(0, 2, 1)
    bsb = jnp.broadcast_to(b_spatial[0][:, :, None], (SGU_GROUPS, CHUNK, 128))
    pairs = _rpb_pairs(rpb[0])
    row_mask = _row_mask(rows)

    my_chip = me // 2
    order = jnp.stack([my_chip, my_chip ^ 2, my_chip ^ 1, my_chip ^ 3]).astype(jnp.int32)
    h, z, win_g, wout_g, s16, part_g = _inproj_fwd(order, x2, norm_g, c_blk, c_ctx_row, wada, b_sh,
                                                   win_t.astype(BF16), wout.astype(BF16), 512)
    w_in_b = win_g.reshape(D_IN, D_MODEL)
    w_out_b = wout_g.reshape(D_MODEL, D_MODEL)
    mod16 = part_g.transpose(1, 0, 2).reshape(16, 3 * D_MODEL)
    mod = lax.dynamic_slice(mod16, (me, 0), (1, 3 * D_MODEL))
    shift, scale, gate = mod[:, :D_MODEL], mod[:, D_MODEL:2 * D_MODEL], mod[:, 2 * D_MODEL:]
    cshift, cscale = mod16[8:9, :D_MODEL], mod16[8:9, D_MODEL:2 * D_MODEL]
    qs, kn = _qk_norm(z, ones512, gq512, gk512, 512)
    hc, zc, ckn = _ctx_fwd(ctx2, norm_g, cscale, cshift, w_in_b, ones512, gk512)
    out_a = _sgu_fwd(z, sgu_norm_g, ws, bsb, 512)
    ob, out_b, lse = _attn_fwd(qs, kn, z, ckn, zc, pairs, row_mask)

    dy, dmc, dw_out, dgate, loss_part = _outproj_loss_bwd(x2, tgt2, out_a, out_b, gate, w_out_b, 512)
    dw_out_blocks = dw_out.reshape(N_DEV, D_MODEL // N_DEV, D_MODEL)
    (dqs, dk, dv, dck, dcv, db_g, drpb), (gout_parts,) = _attn_bwd(
        qs, kn, z, ckn, zc, pairs, row_mask, ob, lse, dmc, hsel, foldr,
        [(_Hosted("a2a", dw_out_blocks), dw_out_blocks)])
    db_q, db_k, db_v, dgq, dgk = _qk_bwd(z, 3, 4, dqs, dk, dv, ones512, gq512, gk512, foldm, 512, "qk_bwd")
    dzc_k, dzc_v, dgk_c = _qk_bwd(zc, None, 0, None, dck, dcv, ones512, None, gk512, foldm, C, "ctx_k_bwd")
    da_u, da_v, da_g, dws, dbs, dsg = _sgu_bwd(z, dmc, sgu_norm_g, ws, wst, bsb, ones8, 512)
    dzs = [da_u, da_v, da_g, db_q, db_k, db_v, db_g]
    mloc = _pack_matrices(dws, dbs, drpb)
    dw_in_t, (mg,) = _inproj_bwd_dw(h, dzs, hc, dzc_k, dzc_v, 512, [(_Hosted("ag", mloc), mloc)])
    (chip_sums,) = _chip_presum([dw_in_t.reshape(N_DEV, win_w, D_MODEL)], "presum_dw_in")
    (grad_x, dshift, dscale, dng), (gin_parts,) = _inproj_bwd_dx(
        dzs, 0, w_in_b, x2, dy, norm_g, scale, shift, 512, "inproj_bwd_dx", [(_Hosted("chips", chip_sums), chip_sums)])
    (dcshift, dcscale, dng_c), _ = _inproj_bwd_dx([dzc_k, dzc_v], 4, w_in_b, ctx2, None, norm_g, cscale, cshift, C,
                                                  "ctx_bwd_dx")
    res_in = _adamw_sharded(win_t, gin_parts, m_w_in[0].T, v_w_in[0].T, 112, "adamw_w_in")
    res_out = _adamw_sharded(wout, gout_parts, m_w_out[0], v_w_out[0], 128, "adamw_w_out")

    zero_row = jnp.zeros((1, D_MODEL), F32)
    vec_rows = [(V_DSHIFT, dshift), (V_DSCALE, dscale), (V_DGATE, dgate), (V_DCSHIFT, dcshift),
                (V_DCSCALE, dcscale), (V_ZERO, zero_row), (V_DNG, dng), (V_DNG_CTX, dng_c)]
    vloc = _pack_vectors(vec_rows, dsg, dgq, dgk, dgk_c, loss_part)
    vg = _allgather_direct(vloc, "gather_small")
    small_w = (b_ada, norm_g, sgu_norm_g, w_spatial, b_spatial, q_norm_g, k_norm_g, rpb)
    small_m = (m_b_ada, m_norm_g, m_sgu_norm_g, m_w_spatial, m_b_spatial, m_q_norm_g, m_k_norm_g, m_rpb)
    small_v = (v_b_ada, v_norm_g, v_sgu_norm_g, v_w_spatial, v_b_spatial, v_q_norm_g, v_k_norm_g, v_rpb)
    res_small, loss_row = _adamw_small(vg, mg, small_w, small_m, small_v)

    dm_all = vg[:, V_DSHIFT:V_DGATE + 1, :].reshape(N_DEV, 3 * D_MODEL)
    dc_all = vg[:, V_DCSHIFT:V_ZERO + 1, :].reshape(N_DEV, 3 * D_MODEL)
    dm_sh = lax.dynamic_slice(dm_all, (0, me * ada_w), (N_DEV, ada_w))
    dc_sh = lax.dynamic_slice(dc_all, (0, me * ada_w), (N_DEV, ada_w))
    *res_ada, pc = _ada_bwd(s16, dm_sh, dc_sh, wada, m_w_ada[0], v_w_ada[0])
    pc_g = _allgather_direct(pc, "gather_cctx")
    res_cctx = _adamw_cctx(pc_g, c_ctx_row, m_c_ctx.reshape(1, D_MODEL), v_c_ctx.reshape(1, D_MODEL))

    loss = loss_row[0, 0]
    outs = [loss, grad_x[None]]
    for kind in range(4):
        by_name = dict(zip(SMALL_NAMES, res_small[kind]))
        by_name.update(c_ctx=res_cctx[kind].reshape(D_MODEL), w_ada=res_ada[kind][None],
                       w_in=res_in[kind].T[None], w_out=res_out[kind][None])
        outs += [by_name[nme] for nme in ("c_ctx", "w_ada", "b_ada", "norm_g", "w_in", "sgu_norm_g", "w_spatial",
                                          "b_spatial", "q_norm_g", "k_norm_g", "rpb", "w_out")]
    return tuple(outs)
```

```python
import functools

import numpy as np
import jax
import jax.numpy as jnp
from jax import lax
from jax.experimental import pallas as pl
from jax.experimental.pallas import tpu as pltpu

F32 = jnp.float32
BF16 = jnp.bfloat16
HI = lax.Precision.HIGHEST

N_DEV = 8
D_MODEL = 1024
D_A = 512
D_B = 512
D_IN = 3584
N_BRANCH = 7
HEAD_DIM = 64
N_HEADS = 8
GRID_W = 64
WIN_R = 8
WIN_C = 16
CHUNK = 128
SGU_GROUPS = 4
EPS = 1e-6
NEG_INF = -1e30
Q_ROWS = 4
K_ROWS = 12
TQ = Q_ROWS * GRID_W
TK = K_ROWS * GRID_W
N_DIAG = 22
ATT_SUB = 4
ATT_SCALE = HEAD_DIM ** -0.5
LOG2E = 1.4426950408889634
LN2 = 0.6931471805599453

ADAM_LR = 0.001
ADAM_B1 = 0.9
ADAM_B2 = 0.999
ADAM_EPS = 1e-08
ADAM_WD = 0.01
ADAM_STEP = 10

VMEM_LIMIT = 56 * 1024 * 1024
MESH = pl.DeviceIdType.MESH

V_DSHIFT, V_DSCALE, V_DGATE, V_DCSHIFT, V_DCSCALE, V_ZERO, V_DNG, V_DNG_CTX, V_MISC = range(9)
M_DBS, M_RPB, M_ROWS = 512, 544, 800


def _params(sem=None):
    return pltpu.CompilerParams(dimension_semantics=sem, vmem_limit_bytes=VMEM_LIMIT)


def _sigmoid(x):
    return 1.0 / (1.0 + jnp.exp(-x))


def _gelu_parts(x):
    cdf = 0.5 * (1.0 + lax.erf(x * 0.7071067811865476))
    pdf = jnp.exp(-0.5 * x * x) * 0.3989422804014327
    return x * cdf, cdf + x * pdf


def _nt(a, b):
    return lax.dot_general(a, b, (((1,), (1,)), ((), ())), preferred_element_type=F32)


def _tn(a, b):
    return lax.dot_general(a, b, (((0,), (0,)), ((), ())), preferred_element_type=F32)


def _dot2(v, ones_bf):
    hi = v.astype(BF16)
    lo = (v - hi.astype(F32)).astype(BF16)
    return (jnp.dot(hi, ones_bf, preferred_element_type=F32)
            + jnp.dot(lo, ones_bf, preferred_element_type=F32))


def _head_sum(v, ones_ref):
    return jnp.dot(v.astype(BF16), ones_ref[...], preferred_element_type=F32)


def _adam(w, g, m, v):
    m2 = ADAM_B1 * m + (1.0 - ADAM_B1) * g
    v2 = ADAM_B2 * v + (1.0 - ADAM_B2) * (g * g)
    m_hat = m2 / (1.0 - ADAM_B1 ** ADAM_STEP)
    v_hat = v2 / (1.0 - ADAM_B2 ** ADAM_STEP)
    delta = -ADAM_LR * (m_hat / (jnp.sqrt(v_hat) + ADAM_EPS) + ADAM_WD * w)
    return delta, m2, v2


def _chip_presum(arrs, name):
    n = len(arrs)
    n_chip = N_DEV // 2
    outs_shape = [jax.ShapeDtypeStruct((n_chip,) + tuple(a.shape[1:]), a.dtype) for a in arrs]

    def body(*refs):
        ins, sums = refs[:n], refs[n:2 * n]
        tmps = refs[2 * n:3 * n]
        s1, r1 = refs[3 * n:]
        c = lax.axis_index("c")
        sib = (lax.axis_index("x"), lax.axis_index("y"), 1 - c)
        all_chips = [(0, 0), (0, 1), (1, 0), (1, 1)]

        swaps = []
        for a in range(n):
            for q, (qx, qy) in enumerate(all_chips):
                cp = pltpu.make_async_remote_copy(
                    src_ref=ins[a].at[4 * qx + 2 * qy + (1 - c)], dst_ref=tmps[a].at[q],
                    send_sem=s1.at[n_chip * a + q], recv_sem=r1.at[n_chip * a + q],
                    device_id=sib, device_id_type=MESH)
                cp.start()
                swaps.append(cp)
        for a in range(n):
            for q, (qx, qy) in enumerate(all_chips):
                swaps[n_chip * a + q].wait_recv()
                both = ins[a][4 * qx + 2 * qy + c].astype(F32) + tmps[a][q].astype(F32)
                sums[a][q] = both.astype(sums[a].dtype)
        for cp in swaps:
            cp.wait_send()

    vm = pl.BlockSpec(memory_space=pltpu.VMEM)
    res = pl.pallas_call(
        body, name=name, out_shape=outs_shape, in_specs=[vm] * n, out_specs=[vm] * n,
        scratch_shapes=([pltpu.VMEM(s.shape, s.dtype) for s in outs_shape]
                        + [pltpu.SemaphoreType.DMA((n_chip * n,)), pltpu.SemaphoreType.DMA((n_chip * n,))]),
        compiler_params=_params(),
    )(*arrs)
    return list(res)


class _Hosted:
    def __init__(self, kind, src):
        self.kind = kind
        n_slot = {"a2a": N_DEV, "ag": N_DEV, "chips": N_DEV // 2}[kind]
        blk = src.shape if kind == "ag" else src.shape[1:]
        self.out_shape = jax.ShapeDtypeStruct((n_slot,) + tuple(blk), src.dtype)
        self.n_peer = n_slot - 1
        self.scratch = [pltpu.SemaphoreType.DMA((self.n_peer,)), pltpu.SemaphoreType.DMA((self.n_peer,)),
                        pltpu.SemaphoreType.DMA]

    def _copies(self, src, dst, send_sems, recv_sems, loc_sem, landing):
        x, y, c = lax.axis_index("x"), lax.axis_index("y"), lax.axis_index("c")
        if self.kind == "chips":
            me = 2 * x + y
            peers = [((px, py, c), 2 * px + py) for px, py in ((1 - x, y), (x, 1 - y), (1 - x, 1 - y))]
        else:
            me = 4 * x + 2 * y + c
            peers = []
            for k in range(1, N_DEV):
                px = 1 - x if (k >> 2) & 1 else x
                py = 1 - y if (k >> 1) & 1 else y
                pc = 1 - c if k & 1 else c
                peers.append(((px, py, pc), 4 * px + 2 * py + pc))
        remote = []
        for k, (peer, pid) in enumerate(peers):
            s = src if self.kind == "ag" else src.at[pid]
            remote.append(pltpu.make_async_remote_copy(
                src_ref=s, dst_ref=dst.at[pid if landing else me],
                send_sem=send_sems.at[k], recv_sem=recv_sems.at[k], device_id=peer, device_id_type=MESH))
        local = pltpu.make_async_copy(src if self.kind == "ag" else src.at[me], dst.at[me], loc_sem)
        return remote, local

    def start(self, src, dst, send_sems, recv_sems, loc_sem):
        remote, local = self._copies(src, dst, send_sems, recv_sems, loc_sem, landing=False)
        for cp in remote:
            cp.start()
        local.start()

    def wait(self, src, dst, send_sems, recv_sems, loc_sem):
        remote, local = self._copies(src, dst, send_sems, recv_sems, loc_sem, landing=True)
        for cp in remote:
            cp.wait_recv()
        for cp in remote:
            cp.wait_send()
        local.wait()


ANY_SPEC = pl.BlockSpec(memory_space=pl.ANY)


def _allgather_direct(arr, name):
    ex = _Hosted("ag", arr)

    def body(src, dst, *sems):
        ex.start(src, dst, *sems)
        ex.wait(src, dst, *sems)

    return pl.pallas_call(body, name=name, out_shape=ex.out_shape, in_specs=[ANY_SPEC], out_specs=ANY_SPEC,
                          scratch_shapes=ex.scratch)(arr)


def _ada_scratch(n_col):
    return ([pltpu.VMEM((N_DEV, 8, D_MODEL), F32), pltpu.VMEM((16, n_col), F32), pltpu.VMEM((N_DEV, 16, n_col), F32)]
            + [pltpu.SemaphoreType.DMA((N_DEV - 1,)) for _ in range(4)])


def _ada_modulation(cb_ref, cc_ref, w_ref, b_ref, cstack, part, parts, s1, r1, s2, r2):
    x, y, c = lax.axis_index("x"), lax.axis_index("y"), lax.axis_index("c")
    me = 4 * x + 2 * y + c
    peers = []
    for k in range(1, N_DEV):
        px = 1 - x if (k >> 2) & 1 else x
        py = 1 - y if (k >> 1) & 1 else y
        pc = 1 - c if k & 1 else c
        peers.append(((px, py, pc), 4 * px + 2 * py + pc))

    def exchange(src, dst, send_sems, recv_sems):
        for k, (peer, _) in enumerate(peers):
            pltpu.make_async_remote_copy(src_ref=src, dst_ref=dst.at[me], send_sem=send_sems.at[k],
                                         recv_sem=recv_sems.at[k], device_id=peer, device_id_type=MESH).start()
        dst[me] = src[...]
        waits = [pltpu.make_async_remote_copy(src_ref=src, dst_ref=dst.at[pid], send_sem=send_sems.at[k],
                                              recv_sem=recv_sems.at[k], device_id=peer, device_id_type=MESH)
                 for k, (peer, pid) in enumerate(peers)]
        for cp in waits:
            cp.wait_recv()
        for cp in waits:
            cp.wait_send()

    exchange(cb_ref, cstack, s1, r1)
    c_all = cstack[0]
    for d in range(1, N_DEV):
        c_all = c_all + cstack[d]
    row = lax.broadcasted_iota(jnp.int32, (8, D_MODEL), 0)
    cc = jnp.where(row == 0, jnp.broadcast_to(cc_ref[...], (8, D_MODEL)), 0.0)
    call = jnp.concatenate([c_all, cc], axis=0)
    s = call * _sigmoid(call)
    part[...] = jnp.dot(s, w_ref[...], preferred_element_type=F32, precision=HI) + b_ref[...]
    exchange(part, parts, s2, r2)
    return s


def _ada_bwd(s16, dm, dc, w, m, v):
    def body(s_ref, dm_ref, dc_ref, w_ref, m_ref, v_ref, g_ref, d_ref, m2_ref, v2_ref, pc_ref):
        dct = jnp.sum(dc_ref[...], axis=0, keepdims=True)
        row = lax.broadcasted_iota(jnp.int32, dc_ref.shape, 0)
        dcb = jnp.where(row == 0, jnp.broadcast_to(dct, dc_ref.shape), 0.0)
        dm16 = jnp.concatenate([dm_ref[...], dcb], axis=0)
        g = lax.dot_general(s_ref[...], dm16, (((0,), (0,)), ((), ())),
                            preferred_element_type=F32, precision=HI)
        w_ = w_ref[...]
        delta, m2, v2 = _adam(w_, g, m_ref[...], v_ref[...])
        g_ref[...] = g
        d_ref[...] = delta
        m2_ref[...] = m2
        v2_ref[...] = v2
        pc_ref[...] = lax.dot_general(dcb, w_, (((1,), (1,)), ((), ())),
                                      preferred_element_type=F32, precision=HI)

    sh = jax.ShapeDtypeStruct(w.shape, F32)
    return pl.pallas_call(
        body, name="ada_bwd",
        out_shape=[sh, sh, sh, sh, jax.ShapeDtypeStruct((8, D_MODEL), F32)],
        compiler_params=_params(),
    )(s16, dm, dc, w, m, v)


def _head_norm(zk, ones_ref, gain):
    ss = _head_sum(zk * zk, ones_ref)
    return zk * lax.rsqrt(ss * (1.0 / HEAD_DIM) + EPS) * gain


def _inproj_fwd(order, x, ng, c_blk, c_ctx_row, w_ada_sh, b_ada_sh, w_blk_t, w_out_blk, tm):
    T = x.shape[0]
    nt = T // tm
    n_pass = N_DEV // 2
    blk_rows = w_blk_t.shape[0]
    n_col = w_ada_sh.shape[1]
    rider = _Hosted("ag", w_out_blk)
    n_ada = len(_ada_scratch(n_col))

    def body(order_ref, x_ref, g_ref, cb_ref, cc_ref, wa_ref, ba_ref, wb_ref, wo_ref,
             h_out, z_ref, wt_out, wo_out, s_out, parts_out,
             hs, wt, modv, send_sems, recv_sems, loc_sem, h_sem, wt_sem, *rest):
        ada_sc, rider_sems = rest[:n_ada], rest[n_ada:]
        p, i = pl.program_id(0), pl.program_id(1)
        x, y, c = lax.axis_index("x"), lax.axis_index("y"), lax.axis_index("c")
        me, sib = (x, y, c), (x, y, 1 - c)
        chips = [(1 - x, y), (x, 1 - y), (1 - x, 1 - y)]

        def slot(px, py, pc):
            return 4 * px + 2 * py + pc

        def copy(k, block, to, src=None):
            return pltpu.make_async_remote_copy(
                src_ref=wt.at[slot(*block)] if src is None else src, dst_ref=wt.at[slot(*block)],
                send_sem=send_sems.at[k], recv_sem=recv_sems.at[k], device_id=to, device_id_type=MESH)

        own = pltpu.make_async_copy(wb_ref, wt.at[slot(*me)], loc_sem)
        h_copy = pltpu.make_async_copy(hs, h_out, h_sem)
        wt_copy = pltpu.make_async_copy(wt, wt_out, wt_sem)
        first = [copy(1 + j, me, (*chip, c), src=wb_ref) for j, chip in enumerate(chips)] + [copy(0, me, sib, src=wb_ref)]
        passed = [copy(4 + j, (*chip, c), sib) for j, chip in enumerate(chips)]

        @pl.when(jnp.logical_and(p == 0, i == 0))
        def _():
            s_out[...] = _ada_modulation(cb_ref, cc_ref, wa_ref, ba_ref, *ada_sc)
            own.start()
            for cp in first:
                cp.start()
            rider.start(wo_ref, wo_out, *rider_sems)
            parts = ada_sc[2]
            parts_out[...] = parts[...]
            my_row = pl.ds(slot(*me), 1)
            mod = jnp.concatenate([parts[d, my_row, :] for d in range(N_DEV)], axis=1)
            modv[0:1, :] = mod[:, 0:D_MODEL]
            modv[1:2, :] = mod[:, D_MODEL:2 * D_MODEL]
            own.wait()
            copy(0, sib, me).wait_recv()

        for j, chip in enumerate(chips):
            @pl.when(jnp.logical_and(p == j + 1, i == 0))
            def _(j=j, chip=chip):
                copy(1 + j, (*chip, c), me).wait_recv()
                passed[j].start()
                copy(4 + j, (*chip, 1 - c), me).wait_recv()
                if j == len(chips) - 1:
                    wt_copy.start()

        rows = pl.ds(pl.multiple_of(i * tm, tm), tm)

        @pl.when(p == 0)
        def _():
            xv = x_ref[...]
            r = lax.rsqrt(jnp.mean(xv * xv, axis=-1, keepdims=True) + EPS)
            hs[rows, :] = ((xv * r * g_ref[...]) * (1.0 + modv[1:2, :]) + modv[0:1, :]).astype(BF16)

        @pl.when(jnp.logical_and(p == 1, i == 0))
        def _():
            h_copy.start()

        w_pair = wt[pl.ds(2 * order_ref[p], 2)].reshape(2 * blk_rows, D_MODEL)
        z_ref[...] = _nt(hs[rows, :], w_pair).astype(BF16)

        @pl.when(jnp.logical_and(p == n_pass - 1, i == nt - 1))
        def _():
            for cp in first + passed:
                cp.wait_send()
            h_copy.wait()
            wt_copy.wait()
            rider.wait(wo_ref, wo_out, *rider_sems)

    whole = lambda shape: pl.BlockSpec(shape, lambda p, i, o: (0,) * len(shape))
    grid_spec = pltpu.PrefetchScalarGridSpec(
        num_scalar_prefetch=1, grid=(n_pass, nt),
        in_specs=[pl.BlockSpec((tm, D_MODEL), lambda p, i, o: (jnp.where(p == 0, i, nt - 1), 0)),
                  whole((1, D_MODEL)), whole((8, D_MODEL)), whole((1, D_MODEL)), whole((D_MODEL, n_col)),
                  whole((1, n_col)), ANY_SPEC, ANY_SPEC],
        out_specs=[ANY_SPEC, pl.BlockSpec((tm, 2 * blk_rows), lambda p, i, o: (i, o[p])), ANY_SPEC, ANY_SPEC,
                   whole((16, D_MODEL)), whole((N_DEV, 16, n_col))],
        scratch_shapes=[pltpu.VMEM((T, D_MODEL), BF16), pltpu.VMEM((N_DEV, blk_rows, D_MODEL), BF16),
                        pltpu.VMEM((8, D_MODEL), F32),
                        pltpu.SemaphoreType.DMA((7,)), pltpu.SemaphoreType.DMA((7,)), pltpu.SemaphoreType.DMA,
                        pltpu.SemaphoreType.DMA, pltpu.SemaphoreType.DMA] + _ada_scratch(n_col) + rider.scratch)
    return pl.pallas_call(
        body, name="inproj_fwd", grid_spec=grid_spec,
        out_shape=[jax.ShapeDtypeStruct((T, D_MODEL), BF16), jax.ShapeDtypeStruct((T, D_IN), BF16),
                   jax.ShapeDtypeStruct((N_DEV, blk_rows, D_MODEL), BF16), rider.out_shape,
                   jax.ShapeDtypeStruct((16, D_MODEL), F32), jax.ShapeDtypeStruct((N_DEV, 16, n_col), F32)],
        compiler_params=_params(("arbitrary", "arbitrary")),
    )(order, x, ng, c_blk, c_ctx_row, w_ada_sh, b_ada_sh, w_blk_t, w_out_blk)


def _qk_norm(z, ones_blk, gq, gk, tm):
    T = z.shape[0]

    def body(q_ref, k_ref, ones_ref, gq_ref, gk_ref, qs_ref, kn_ref):
        qs = _head_norm(q_ref[...].astype(F32), ones_ref, gq_ref[...]) * (ATT_SCALE * LOG2E)
        qs_ref[...] = qs.astype(BF16)
        kn_ref[...] = _head_norm(k_ref[...].astype(F32), ones_ref, gk_ref[...]).astype(BF16)

    v512 = pl.BlockSpec((1, 512), lambda i: (0, 0))
    return pl.pallas_call(
        body, name="qk_norm", grid=(T // tm,),
        in_specs=[pl.BlockSpec((tm, 512), lambda i: (i, 3)), pl.BlockSpec((tm, 512), lambda i: (i, 4)),
                  pl.BlockSpec((512, 512), lambda i: (0, 0)), v512, v512],
        out_specs=[pl.BlockSpec((tm, 512), lambda i: (i, 0)), pl.BlockSpec((tm, 512), lambda i: (i, 0))],
        out_shape=[jax.ShapeDtypeStruct((T, 512), BF16), jax.ShapeDtypeStruct((T, 512), BF16)],
        compiler_params=_params(("arbitrary",)),
    )(z, z, ones_blk, gq, gk)


def _ctx_fwd(ctx, ng, cscale, cshift, w_in_t, ones_blk, gk):
    C = ctx.shape[0]

    def body(x_ref, g_ref, sc_ref, sh_ref, w_ref, ones_ref, gk_ref, h_ref, z_ref, kn_ref):
        xv = x_ref[...]
        r = lax.rsqrt(jnp.mean(xv * xv, axis=-1, keepdims=True) + EPS)
        h = (xv * r * g_ref[...]) * (1.0 + sc_ref[...]) + sh_ref[...]
        hb = h.astype(BF16)
        h_ref[...] = hb
        zk = _nt(hb, w_ref[0:512, :])
        zv = _nt(hb, w_ref[512:1024, :])
        z_ref[:, 0:512] = zk.astype(BF16)
        z_ref[:, 512:1024] = zv.astype(BF16)
        kn_ref[...] = _head_norm(zk, ones_ref, gk_ref[...]).astype(BF16)

    vec = pl.BlockSpec((1, D_MODEL), lambda i: (0, 0))
    return pl.pallas_call(
        body, name="ctx_fwd", grid=(1,),
        in_specs=[pl.BlockSpec((C, D_MODEL), lambda i: (0, 0)), vec, vec, vec,
                  pl.BlockSpec((1024, D_MODEL), lambda i: (2, 0)),
                  pl.BlockSpec((512, 512), lambda i: (0, 0)), pl.BlockSpec((1, 512), lambda i: (0, 0))],
        out_specs=[pl.BlockSpec((C, D_MODEL), lambda i: (0, 0)), pl.BlockSpec((C, 1024), lambda i: (0, 0)),
                   pl.BlockSpec((C, 512), lambda i: (0, 0))],
        out_shape=[jax.ShapeDtypeStruct((C, D_MODEL), BF16), jax.ShapeDtypeStruct((C, 1024), BF16),
                   jax.ShapeDtypeStruct((C, 512), BF16)],
        compiler_params=_params(("arbitrary",)),
    )(ctx, ng, cscale, cshift, w_in_t, ones_blk, gk)


def _sgu_chunk_fwd(au, av, ag, sg, ws_bf, bsb):
    gu, dgu = _gelu_parts(au)
    gv, dgv = _gelu_parts(av)
    rr = lax.rsqrt(jnp.mean(gv * gv, axis=-1, keepdims=True) + EPS)
    vhat = gv * rr
    vn = vhat * sg
    mixed = jnp.dot(ws_bf, vn.astype(BF16), preferred_element_type=F32) + bsb
    sig = _sigmoid(ag)
    sl = ag * sig
    return gu * mixed * sl, (gu, dgu, dgv, rr, vhat, vn, mixed, sig, sl)


def _sgu_fwd(z, sgn, ws, bsb, tm):
    T = z.shape[0]

    def body(au_ref, av_ref, ag_ref, sg_ref, ws_ref, bsb_ref, o_ref):
        for g in range(SGU_GROUPS):
            ws_bf = ws_ref[g].astype(BF16)
            sg = sg_ref[:, 128 * g:128 * (g + 1)]
            bsb_g = bsb_ref[g]
            for j in range(tm // CHUNK):
                rs, cs = slice(CHUNK * j, CHUNK * (j + 1)), slice(128 * g, 128 * (g + 1))
                out, _ = _sgu_chunk_fwd(au_ref[rs, cs].astype(F32), av_ref[rs, cs].astype(F32),
                                        ag_ref[rs, cs].astype(F32), sg, ws_bf, bsb_g)
                o_ref[rs, cs] = out.astype(BF16)

    return pl.pallas_call(
        body, name="sgu_fwd", grid=(T // tm,),
        in_specs=[pl.BlockSpec((tm, 512), lambda i: (i, 0)), pl.BlockSpec((tm, 512), lambda i: (i, 1)),
                  pl.BlockSpec((tm, 512), lambda i: (i, 2)), pl.BlockSpec((1, 512), lambda i: (0, 0)),
                  pl.BlockSpec((SGU_GROUPS, CHUNK, CHUNK), lambda i: (0, 0, 0)),
                  pl.BlockSpec((SGU_GROUPS, CHUNK, 128), lambda i: (0, 0, 0))],
        out_specs=pl.BlockSpec((tm, 512), lambda i: (i, 0)),
        out_shape=jax.ShapeDtypeStruct((T, 512), BF16),
        compiler_params=_params(("arbitrary",)),
    )(z, z, z, sgn, ws, bsb)


def _attn_type(rb, nrb):
    return jnp.where(rb == 0, 0, jnp.where(rb == nrb - 1, 2, 1))


def _attn_specs(T, C):
    return [
        pl.BlockSpec((ATT_SUB * TQ, 128), lambda hp, st: (st, hp)),
        pl.BlockSpec((T, 128), lambda hp, st: (0, hp)),
        pl.BlockSpec((T, 128), lambda hp, st: (0, 20 + hp)),
        pl.BlockSpec((C, 128), lambda hp, st: (0, hp)),
        pl.BlockSpec((C, 128), lambda hp, st: (0, 4 + hp)),
        pl.BlockSpec((2, 2 * WIN_R, GRID_W, 128), lambda hp, st: (hp, 0, 0, 0)),
        pl.BlockSpec((3, TQ, TK), lambda hp, st: (0, 0, 0)),
        pl.BlockSpec((ATT_SUB * TQ, 128), lambda hp, st: (st, 24 + hp)),
    ]


def _build_bias(pairs_ref, mask_ref, bias_sc):
    for t in range(3):
        for hh in range(2):
            for i in range(Q_ROWS):
                for mm in range(K_ROWS // 2):
                    p = min(max(WIN_R - Q_ROWS * t + 2 * mm - i, 0), 2 * WIN_R - 1)
                    rs, cs = slice(GRID_W * i, GRID_W * (i + 1)), slice(128 * mm, 128 * (mm + 1))
                    bias_sc[t, hh, rs, cs] = (pairs_ref[hh, p] + mask_ref[t, rs, cs]) * LOG2E


def _attn_fwd(qs, kn, z, ckn, zc, pairs, row_mask):
    T, C = qs.shape[0], ckn.shape[0]
    rows = T // GRID_W
    nrb = rows // Q_ROWS

    def body(q_ref, k_ref, v_ref, ck_ref, cv_ref, pairs_ref, mask_ref, bg_ref, ob_ref, outb_ref, lse_ref, bias_sc):
        @pl.when(pl.program_id(1) == 0)
        def _():
            _build_bias(pairs_ref, mask_ref, bias_sc)

        ck2, cv2 = ck_ref[...], cv_ref[...]
        lane = lax.broadcasted_iota(jnp.int32, (1, 128), 1)
        for sub in range(ATT_SUB):
            rb = ATT_SUB * pl.program_id(1) + sub
            bias_ref = bias_sc.at[_attn_type(rb, nrb)]
            rs = slice(TQ * sub, TQ * (sub + 1))
            ks = pl.multiple_of(jnp.clip(Q_ROWS * rb - 4, 0, rows - K_ROWS) * GRID_W, GRID_W)
            q2 = q_ref[rs, :]
            k2 = k_ref[pl.ds(ks, TK), :]
            v2 = v_ref[pl.ds(ks, TK), :]
            o_acc = jnp.zeros((TQ, 128), F32)
            lse_acc = jnp.zeros((TQ, 128), F32)
            for hh in range(2):
                msk = (lane >= HEAD_DIM) == bool(hh)
                qm = jnp.where(msk, q2, jnp.zeros_like(q2))
                s = _nt(qm, k2) + bias_ref[hh]
                sc = _nt(qm, ck2)
                m = jnp.maximum(jnp.max(s, axis=-1, keepdims=True), jnp.max(sc, axis=-1, keepdims=True))
                p = jnp.exp2(s - m)
                pc = jnp.exp2(sc - m)
                va = jnp.where(msk, v2, jnp.ones_like(v2))
                cva = jnp.where(msk, cv2, jnp.ones_like(cv2))
                num = (jnp.dot(p.astype(BF16), va, preferred_element_type=F32)
                       + jnp.dot(pc.astype(BF16), cva, preferred_element_type=F32))
                den = pltpu.roll(num, HEAD_DIM, 1)
                o_acc = jnp.where(msk, num / den, o_acc)
                lse_acc = jnp.where(msk, m + jnp.log(den) * LOG2E, lse_acc)
            ob_ref[rs, :] = o_acc.astype(BF16)
            lse_ref[rs, :] = lse_acc
            bg = bg_ref[rs, :].astype(F32)
            outb_ref[rs, :] = (o_acc * (bg * _sigmoid(bg))).astype(BF16)

    tile = pl.BlockSpec((ATT_SUB * TQ, 128), lambda hp, st: (st, hp))
    return pl.pallas_call(
        body, name="attn_fwd", grid=(4, nrb // ATT_SUB),
        in_specs=_attn_specs(T, C),
        out_specs=[tile, tile, tile],
        out_shape=[jax.ShapeDtypeStruct((T, 512), BF16), jax.ShapeDtypeStruct((T, 512), BF16),
                   jax.ShapeDtypeStruct((T, 512), F32)],
        scratch_shapes=[pltpu.VMEM((3, 2, TQ, TK), F32)],
        compiler_params=_params(("arbitrary", "arbitrary")),
    )(qs, kn, z, ckn, zc, pairs, row_mask, z)


def _outproj_loss_bwd(x, tgt, out_a, out_b, gate, w_out, tm):
    T = x.shape[0]
    nt = T // tm

    def body(x_ref, t_ref, oa_ref, ob_ref, gate_ref, w_ref, dy_ref, dmc_ref, dw_ref, dgate_ref, loss_ref, acc):
        @pl.when(pl.program_id(0) == 0)
        def _():
            acc[...] = jnp.zeros_like(acc)
            dgate_ref[...] = jnp.zeros_like(dgate_ref)
            loss_ref[...] = jnp.zeros_like(loss_ref)

        oa, ob = oa_ref[...], ob_ref[...]
        gate_v = gate_ref[...]
        mix = (jnp.dot(oa, w_ref[0:512, :], preferred_element_type=F32)
               + jnp.dot(ob, w_ref[512:1024, :], preferred_element_type=F32))
        e = x_ref[...] + gate_v * mix - t_ref[...]
        se = jnp.sum(jnp.sum(e * e, axis=0, keepdims=True), axis=1, keepdims=True)
        loss_ref[...] += jnp.broadcast_to(se * (0.5 / D_MODEL), loss_ref.shape)
        dy = e * (1.0 / D_MODEL)
        dy_ref[...] = dy
        dgate_ref[...] += jnp.sum(dy * mix, axis=0, keepdims=True)
        dmix = (dy * gate_v).astype(BF16)
        dmc_ref[...] = _nt(dmix, w_ref[...]).astype(BF16)
        acc[0:512, :] += _tn(oa, dmix)
        acc[512:1024, :] += _tn(ob, dmix)

        @pl.when(pl.program_id(0) == nt - 1)
        def _():
            dw_ref[...] = acc[...].astype(BF16)

    row = lambda w: pl.BlockSpec((tm, w), lambda i: (i, 0))
    return pl.pallas_call(
        body, name="outproj_loss_bwd", grid=(nt,),
        in_specs=[row(D_MODEL), row(D_MODEL), row(512), row(512),
                  pl.BlockSpec((1, D_MODEL), lambda i: (0, 0)),
                  pl.BlockSpec((D_MODEL, D_MODEL), lambda i: (0, 0))],
        out_specs=[row(D_MODEL), row(D_MODEL), pl.BlockSpec((D_MODEL, D_MODEL), lambda i: (0, 0)),
                   pl.BlockSpec((1, D_MODEL), lambda i: (0, 0)), pl.BlockSpec((1, 128), lambda i: (0, 0))],
        out_shape=[jax.ShapeDtypeStruct((T, D_MODEL), F32), jax.ShapeDtypeStruct((T, D_MODEL), BF16),
                   jax.ShapeDtypeStruct((D_MODEL, D_MODEL), BF16), jax.ShapeDtypeStruct((1, D_MODEL), F32),
                   jax.ShapeDtypeStruct((1, 128), F32)],
        scratch_shapes=[pltpu.VMEM((D_MODEL, D_MODEL), F32)],
        compiler_params=_params(("arbitrary",)),
    )(x, tgt, out_a, out_b, gate, w_out)


def _attn_bwd(qs, kn, z, ckn, zc, pairs, row_mask, ob, lse, dmc, hsel, fold, riders):
    T, C = qs.shape[0], ckn.shape[0]
    rows = T // GRID_W
    nrb = rows // Q_ROWS
    n_st = nrb // ATT_SUB
    n_rid = len(riders)

    def body(q_ref, k_ref, v_ref, ck_ref, cv_ref, pairs_ref, mask_ref, bg_ref, ob_ref, lse_ref, do_ref,
             hsel_ref, fold_ref, *rest):
        rid_src = rest[:n_rid]
        dq_ref, dk_ref, dv_ref, dck_ref, dcv_ref, dbg_ref, drpb_ref = rest[n_rid:n_rid + 7]
        rid_dst = rest[n_rid + 7:2 * n_rid + 7]
        bias_sc, dacc_ref = rest[2 * n_rid + 7:2 * n_rid + 9]
        rid_sems = rest[2 * n_rid + 9:]
        hp, st = pl.program_id(0), pl.program_id(1)

        @pl.when(jnp.logical_and(hp == 0, st == 0))
        def _():
            for r, (ex, _) in enumerate(riders):
                ex.start(rid_src[r], rid_dst[r], *rid_sems[3 * r:3 * r + 3])

        @pl.when(st == 0)
        def _():
            _build_bias(pairs_ref, mask_ref, bias_sc)
            dk_ref[...] = jnp.zeros_like(dk_ref)
            dv_ref[...] = jnp.zeros_like(dv_ref)
            dck_ref[...] = jnp.zeros_like(dck_ref)
            dcv_ref[...] = jnp.zeros_like(dcv_ref)
            dacc_ref[...] = jnp.zeros_like(dacc_ref)

        ck2, cv2 = ck_ref[...], cv_ref[...]
        lane = lax.broadcasted_iota(jnp.int32, (1, 128), 1)
        for sub in range(ATT_SUB):
            rb = ATT_SUB * st + sub
            bias_ref = bias_sc.at[_attn_type(rb, nrb)]
            rs = slice(TQ * sub, TQ * (sub + 1))
            kb = jnp.clip(Q_ROWS * rb - 4, 0, rows - K_ROWS)
            ks = pl.multiple_of(kb * GRID_W, GRID_W)
            ebase = kb - Q_ROWS * rb + 11
            q2 = q_ref[rs, :]
            k2 = k_ref[pl.ds(ks, TK), :]
            v2 = v_ref[pl.ds(ks, TK), :]
            bg = bg_ref[rs, :].astype(F32)
            sig = _sigmoid(bg)
            obv = ob_ref[rs, :].astype(F32)
            dout = do_ref[rs, :].astype(F32)
            dbg_ref[rs, :] = (dout * obv * (sig * (1.0 + bg * (1.0 - sig)))).astype(BF16)
            d_o = dout * (bg * sig)
            d_oo = d_o * obv
            lse2 = lse_ref[rs, :]
            dq_acc = jnp.zeros((TQ, 128), F32)
            for hh in range(2):
                msk = (lane >= HEAD_DIM) == bool(hh)
                qm = jnp.where(msk, q2, jnp.zeros_like(q2))
                lse_h = jnp.max(jnp.where(msk, lse2, -jnp.inf), axis=-1, keepdims=True)
                p = jnp.exp2(_nt(qm, k2) + bias_ref[hh] - lse_h)
                pc = jnp.exp2(_nt(qm, ck2) - lse_h)
                dom_f = jnp.where(msk, d_o, 0.0)
                dom = dom_f.astype(BF16)
                delta = jnp.sum(jnp.where(msk, d_oo, 0.0), axis=-1, keepdims=True)
                d_hi = delta.astype(BF16).astype(F32)
                x0 = HEAD_DIM * (1 - hh)
                dom_aug = jnp.where(lane == x0, -d_hi, jnp.where(lane == x0 + 1, d_hi - delta, dom_f)).astype(BF16)
                extra = jnp.logical_or(lane == x0, lane == x0 + 1)
                va = jnp.where(msk, v2, jnp.where(extra, jnp.ones_like(v2), jnp.zeros_like(v2)))
                cva = jnp.where(msk, cv2, jnp.where(extra, jnp.ones_like(cv2), jnp.zeros_like(cv2)))
                ds = p * _nt(dom_aug, va)
                dsc = pc * _nt(dom_aug, cva)
                dsb, dscb = ds.astype(BF16), dsc.astype(BF16)
                dq_h = (jnp.dot(dsb, k2, preferred_element_type=F32)
                        + jnp.dot(dscb, ck2, preferred_element_type=F32))
                dq_acc = jnp.where(msk, dq_h, dq_acc)
                dk_ref[pl.ds(ks, TK), :] += _tn(dsb, qm)
                dv_ref[pl.ds(ks, TK), :] += _tn(p.astype(BF16), dom)
                dck_ref[...] += _tn(dscb, qm)
                dcv_ref[...] += _tn(pc.astype(BF16), dom)
                for i in range(Q_ROWS):
                    for mm in range(K_ROWS // 2):
                        dacc_ref[hh, ebase + (2 * mm - i)] += ds[GRID_W * i:GRID_W * (i + 1),
                                                                 128 * mm:128 * (mm + 1)]
            dq_ref[rs, :] = dq_acc

        @pl.when(st == n_st - 1)
        def _():
            for hh in range(2):
                drpb_ref[hh] = _rpb_diag_sums(dacc_ref.at[hh], hsel_ref, fold_ref)

        @pl.when(jnp.logical_and(hp == pl.num_programs(0) - 1, st == n_st - 1))
        def _():
            for r, (ex, _) in enumerate(riders):
                ex.wait(rid_src[r], rid_dst[r], *rid_sems[3 * r:3 * r + 3])

    tile = pl.BlockSpec((ATT_SUB * TQ, 128), lambda hp, st: (st, hp))
    colT = pl.BlockSpec((T, 128), lambda hp, st: (0, hp))
    colC = pl.BlockSpec((C, 128), lambda hp, st: (0, hp))
    res = pl.pallas_call(
        body, name="attn_bwd", grid=(4, n_st),
        in_specs=(_attn_specs(T, C) + [tile, tile, pl.BlockSpec((ATT_SUB * TQ, 128), lambda hp, st: (st, 4 + hp)),
                                       pl.BlockSpec((8, 128, 128), lambda hp, st: (0, 0, 0)),
                                       pl.BlockSpec((32, 256), lambda hp, st: (0, 0))]
                  + [ANY_SPEC] * n_rid),
        out_specs=([tile, colT, colT, colC, colC, tile, pl.BlockSpec((2, 32, 128), lambda hp, st: (hp, 0, 0))]
                   + [ANY_SPEC] * n_rid),
        out_shape=([jax.ShapeDtypeStruct((T, 512), F32), jax.ShapeDtypeStruct((T, 512), F32),
                    jax.ShapeDtypeStruct((T, 512), F32), jax.ShapeDtypeStruct((C, 512), F32),
                    jax.ShapeDtypeStruct((C, 512), F32), jax.ShapeDtypeStruct((T, 512), BF16),
                    jax.ShapeDtypeStruct((N_HEADS, 32, 128), F32)] + [ex.out_shape for ex, _ in riders]),
        scratch_shapes=([pltpu.VMEM((3, 2, TQ, TK), F32), pltpu.VMEM((2, N_DIAG, GRID_W, 128), F32)]
                        + [s for ex, _ in riders for s in ex.scratch]),
        compiler_params=_params(("arbitrary", "arbitrary")),
    )(qs, kn, z, ckn, zc, pairs, row_mask, z, ob, lse, dmc, hsel, fold, *[a for _, a in riders])
    return res[:7], res[7:]


def _rpb_diag_sums(a_ref, hsel_ref, fold_ref):
    n_off = 2 * WIN_C - 1
    n_dr = 2 * WIN_R - 1
    qc = lax.broadcasted_iota(jnp.int32, (GRID_W, 128), 0)
    lane = lax.broadcasted_iota(jnp.int32, (GRID_W, 128), 1)
    diff = lane % GRID_W - qc + (WIN_C - 1)
    left = lane < GRID_W

    def by_dr(dr):
        return a_ref[dr + 4] + pltpu.roll(a_ref[dr + 3], GRID_W, 1)

    out = jnp.zeros((32, 128), F32)
    for j in range((n_dr + 1) // 2):
        hi = pltpu.roll(by_dr(2 * j + 1), GRID_W, 1) if 2 * j + 1 < n_dr else 0.0
        pair = jnp.where(left, by_dr(2 * j), hi)
        parts = []
        for o in range(n_off):
            mv = jnp.where(diff == o, pair, 0.0)
            acc = mv[0:8]
            for r8 in range(1, GRID_W // 8):
                acc = acc + mv[8 * r8:8 * (r8 + 1)]
            parts.append(acc)
        parts.append(jnp.zeros((8, 128), F32))
        stack = jnp.concatenate(parts, axis=0)
        s_hi = stack.astype(BF16)
        s_lo = (stack - s_hi.astype(F32)).astype(BF16)
        per_o = (jnp.dot(fold_ref[...], s_hi, preferred_element_type=F32)
                 + jnp.dot(fold_ref[...], s_lo, preferred_element_type=F32))
        out = out + _dot2(per_o, hsel_ref[j])
    return out


def _head_norm_bwd(raw, dn, gain, ones_ref):
    rr = lax.rsqrt(_head_sum(raw * raw, ones_ref) * (1.0 / HEAD_DIM) + EPS)
    hat = raw * rr
    dgain = jnp.sum(dn * hat, axis=0, keepdims=True)
    dhat = dn * gain
    mean = _head_sum(dhat * hat, ones_ref) * (1.0 / HEAD_DIM)
    return rr * (dhat - hat * mean), dgain


def _ctx_k_bwd(zc, dck, dcv, ones_blk, gk, foldm):
    C = dck.shape[0]

    def body(bk_ref, dk_ref, dv_ref, ones_ref, gk_ref, fold_ref, dbk_ref, dbv_ref, dgk_ref):
        dbk, dgk = _head_norm_bwd(bk_ref[...].astype(F32), dk_ref[...] * LN2, gk_ref[...], ones_ref)
        dbk_ref[...] = dbk.astype(BF16)
        dbv_ref[...] = dv_ref[...].astype(BF16)
        dgk_ref[...] = jnp.dot(jnp.broadcast_to(dgk, (8, 512)), fold_ref[...],
                               preferred_element_type=F32, precision=HI)

    row = pl.BlockSpec((C, 512), lambda i: (0, 0))
    cst = lambda a, b: pl.BlockSpec((a, b), lambda i: (0, 0))
    out_row = jax.ShapeDtypeStruct((C, 512), BF16)
    return pl.pallas_call(
        body, name="ctx_k_bwd", grid=(1,),
        in_specs=[row, row, row, cst(512, 512), cst(1, 512), cst(512, 128)],
        out_specs=[row, row, cst(8, 128)],
        out_shape=[out_row, out_row, jax.ShapeDtypeStruct((8, 128), F32)],
        compiler_params=_params(("arbitrary",)),
    )(zc, dck, dcv, ones_blk, gk, foldm)


def _bwd_mid(z, dmc, dqs, dk, dv, db_g, h, hc, dzc_k, dzc_v, sgn, ws, wst, bsb, ones8, ones_blk, gq, gk, foldm, tk):
    T = z.shape[0]
    nt = T // tk

    def body(au_ref, av_ref, ag_ref, bq_ref, bk_ref, d_ref, dq_ref, dk_ref, dv_ref, dbg_ref, h_ref,
             hc_ref, dzck_ref, dzcv_ref, sg_ref, ws_ref, wst_ref, bsb_ref, ones8_ref, ones_ref, gq_ref, gk_ref,
             fold_ref, dz_ref, dwt_out, dws_ref, dbs_ref, dsg_ref, dgq_ref, dgk_ref, acc, accq, acck, stage, sem):
        t = pl.program_id(0)

        @pl.when(t == 0)
        def _():
            acc[...] = jnp.zeros_like(acc)
            acc[512 * 4:512 * 5, :] = _tn(dzck_ref[...], hc_ref[...])
            acc[512 * 5:512 * 6, :] = _tn(dzcv_ref[...], hc_ref[...])
            dws_ref[...] = jnp.zeros_like(dws_ref)
            dbs_ref[...] = jnp.zeros_like(dbs_ref)
            dsg_ref[...] = jnp.zeros_like(dsg_ref)
            accq[...] = jnp.zeros_like(accq)
            acck[...] = jnp.zeros_like(acck)

        for g in range(SGU_GROUPS):
            ws_bf = ws_ref[g].astype(BF16)
            wst_bf = wst_ref[g].astype(BF16)
            sg = sg_ref[:, 128 * g:128 * (g + 1)]
            bsb_g = bsb_ref[g]
            for j in range(tk // CHUNK):
                rs, cs = slice(CHUNK * j, CHUNK * (j + 1)), slice(128 * g, 128 * (g + 1))
                au, av, ag = (au_ref[rs, cs].astype(F32), av_ref[rs, cs].astype(F32), ag_ref[rs, cs].astype(F32))
                d = d_ref[rs, cs].astype(F32)
                _, (gu, dgu, dgv, rr, vhat, vn, mixed, sig, sl) = _sgu_chunk_fwd(au, av, ag, sg, ws_bf, bsb_g)
                dz_ref[rs, 128 * g:128 * (g + 1)] = (d * mixed * sl * dgu).astype(BF16)
                dz_ref[rs, 1024 + 128 * g:1024 + 128 * (g + 1)] = (
                    d * gu * mixed * (sig * (1.0 + ag * (1.0 - sig)))).astype(BF16)
                dmixed = d * gu * sl
                dmb = dmixed.astype(BF16)
                dm_lo = (dmixed - dmb.astype(F32)).astype(BF16)
                dbs_ref[g] += _nt(ones8_ref[...], dmb) + _nt(ones8_ref[...], dm_lo)
                dws_ref[g] += _nt(dmb, vn.astype(BF16))
                dvn = jnp.dot(wst_bf, dmb, preferred_element_type=F32)
                dsg_ref[:, 128 * g:128 * (g + 1)] += jnp.sum(dvn * vhat, axis=0, keepdims=True)
                dvhat = dvn * sg
                mean = jnp.mean(dvhat * vhat, axis=-1, keepdims=True)
                dz_ref[rs, 512 + 128 * g:512 + 128 * (g + 1)] = (rr * (dvhat - vhat * mean) * dgv).astype(BF16)

        dbq, dgq = _head_norm_bwd(bq_ref[...].astype(F32), dq_ref[...] * ATT_SCALE, gq_ref[...], ones_ref)
        dz_ref[:, 512 * 3:512 * 4] = dbq.astype(BF16)
        accq[...] += dgq
        dbk, dgk = _head_norm_bwd(bk_ref[...].astype(F32), dk_ref[...] * LN2, gk_ref[...], ones_ref)
        dz_ref[:, 512 * 4:512 * 5] = dbk.astype(BF16)
        acck[...] += dgk
        dz_ref[:, 512 * 5:512 * 6] = dv_ref[...].astype(BF16)
        dz_ref[:, 512 * 6:512 * 7] = dbg_ref[...]

        hv = h_ref[...]
        for k in range(N_BRANCH):
            acc[512 * k:512 * (k + 1), :] += _tn(dz_ref[:, 512 * k:512 * (k + 1)], hv)

        @pl.when(t == nt - 1)
        def _():
            dgq_ref[...] = jnp.dot(jnp.broadcast_to(accq[...], (8, 512)), fold_ref[...],
                                   preferred_element_type=F32, precision=HI)
            dgk_ref[...] = jnp.dot(jnp.broadcast_to(acck[...], (8, 512)), fold_ref[...],
                                   preferred_element_type=F32, precision=HI)
            for k in range(N_BRANCH):
                stage[...] = acc[512 * k:512 * (k + 1), :].astype(BF16)
                out = pltpu.make_async_copy(stage, dwt_out.at[pl.ds(512 * k, 512)], sem)
                out.start()
                out.wait()

    zcol = lambda col: pl.BlockSpec((tk, 512), lambda t: (t, col))
    row = pl.BlockSpec((tk, 512), lambda t: (t, 0))
    whole = lambda a: pl.BlockSpec(a.shape, lambda t: (0,) * a.ndim)
    return pl.pallas_call(
        body, name="bwd_mid", grid=(nt,),
        in_specs=[zcol(0), zcol(1), zcol(2), zcol(3), zcol(4), row, row, row, row, row,
                  pl.BlockSpec((tk, D_MODEL), lambda t: (t, 0)), whole(hc), whole(dzc_k), whole(dzc_v),
                  whole(sgn), whole(ws), whole(wst), whole(bsb), whole(ones8), whole(ones_blk), whole(gq), whole(gk),
                  whole(foldm)],
        out_specs=[pl.BlockSpec((tk, D_IN), lambda t: (t, 0)), ANY_SPEC,
                   pl.BlockSpec((SGU_GROUPS, CHUNK, CHUNK), lambda t: (0, 0, 0)),
                   pl.BlockSpec((SGU_GROUPS, 8, CHUNK), lambda t: (0, 0, 0)),
                   pl.BlockSpec((1, 512), lambda t: (0, 0)), pl.BlockSpec((8, 128), lambda t: (0, 0)),
                   pl.BlockSpec((8, 128), lambda t: (0, 0))],
        out_shape=[jax.ShapeDtypeStruct((T, D_IN), BF16), jax.ShapeDtypeStruct((D_IN, D_MODEL), BF16),
                   jax.ShapeDtypeStruct((SGU_GROUPS, CHUNK, CHUNK), F32),
                   jax.ShapeDtypeStruct((SGU_GROUPS, 8, CHUNK), F32), jax.ShapeDtypeStruct((1, 512), F32),
                   jax.ShapeDtypeStruct((8, 128), F32), jax.ShapeDtypeStruct((8, 128), F32)],
        scratch_shapes=[pltpu.VMEM((D_IN, D_MODEL), F32), pltpu.VMEM((1, 512), F32), pltpu.VMEM((1, 512), F32),
                        pltpu.VMEM((512, D_MODEL), BF16), pltpu.SemaphoreType.DMA],
        compiler_params=_params(("arbitrary",)),
    )(z, z, z, z, z, dmc, dqs, dk, dv, db_g, h, hc, dzc_k, dzc_v, sgn, ws, wst, bsb, ones8, ones_blk, gq, gk, foldm)


def _inproj_bwd_dx(dzs, w_row0, w_in_t, x, dy, ng, scale, shift, tm, name, riders=()):
    T = x.shape[0]
    n = len(dzs)
    wpc = dzs[0].shape[1]
    nt = T // tm
    with_dx = dy is not None
    n_own_in = n + 5 + with_dx
    n_own_out = 3 + with_dx
    n_rid = len(riders)

    def body(*refs):
        dz_refs = refs[:n]
        own = refs[n:n_own_in] + refs[n_own_in + n_rid:n_own_in + n_rid + n_own_out]
        rid_src = refs[n_own_in:n_own_in + n_rid]
        rid_dst = refs[n_own_in + n_rid + n_own_out:n_own_in + 2 * n_rid + n_own_out]
        rid_sems = refs[n_own_in + 2 * n_rid + n_own_out:]
        if with_dx:
            w_ref, x_ref, dy_ref, g_ref, sc_ref, sh_ref, gx_ref, dsh_ref, dsc_ref, dg_ref = own
        else:
            w_ref, x_ref, g_ref, sc_ref, sh_ref, dsh_ref, dsc_ref, dg_ref = own

        @pl.when(pl.program_id(0) == 0)
        def _():
            for r, (ex, _) in enumerate(riders):
                ex.start(rid_src[r], rid_dst[r], *rid_sems[3 * r:3 * r + 3])
            dsh_ref[...] = jnp.zeros_like(dsh_ref)
            dsc_ref[...] = jnp.zeros_like(dsc_ref)
            dg_ref[...] = jnp.zeros_like(dg_ref)

        dh = jnp.dot(dz_refs[0][...], w_ref[0:wpc, :], preferred_element_type=F32)
        for k in range(1, n):
            dh = dh + jnp.dot(dz_refs[k][...], w_ref[wpc * k:wpc * (k + 1), :], preferred_element_type=F32)
        xv = x_ref[...]
        r = lax.rsqrt(jnp.mean(xv * xv, axis=-1, keepdims=True) + EPS)
        xn = xv * r
        gv, op = g_ref[...], 1.0 + sc_ref[...]
        dsh_ref[...] += jnp.sum(dh, axis=0, keepdims=True)
        dsc_ref[...] += jnp.sum(dh * xn * gv, axis=0, keepdims=True)
        dg_ref[...] += jnp.sum(dh * op * xn, axis=0, keepdims=True)
        if with_dx:
            dxn = dh * (gv * op)
            gx_ref[...] = r * (dxn - xn * jnp.mean(dxn * xn, axis=-1, keepdims=True)) + dy_ref[...]

        @pl.when(pl.program_id(0) == nt - 1)
        def _():
            for r, (ex, _) in enumerate(riders):
                ex.wait(rid_src[r], rid_dst[r], *rid_sems[3 * r:3 * r + 3])

    vec = pl.BlockSpec((1, D_MODEL), lambda i: (0, 0))
    rowf = pl.BlockSpec((tm, D_MODEL), lambda i: (i, 0))
    in_specs = [pl.BlockSpec((tm, wpc), lambda i: (i, 0))] * n
    in_specs += [pl.BlockSpec((wpc * n, D_MODEL), lambda i: (w_row0 // (wpc * n), 0)), rowf]
    args = list(dzs) + [w_in_t, x]
    vshape = jax.ShapeDtypeStruct((1, D_MODEL), F32)
    out_specs, out_shape = [vec, vec, vec], [vshape, vshape, vshape]
    if with_dx:
        in_specs.append(rowf)
        args.append(dy)
        out_specs = [rowf] + out_specs
        out_shape = [jax.ShapeDtypeStruct((T, D_MODEL), F32)] + out_shape
    in_specs += [vec, vec, vec] + [ANY_SPEC] * n_rid
    args += [ng, scale, shift] + [a for _, a in riders]
    res = pl.pallas_call(
        body, name=name, grid=(nt,), in_specs=in_specs, out_specs=out_specs + [ANY_SPEC] * n_rid,
        out_shape=out_shape + [ex.out_shape for ex, _ in riders],
        scratch_shapes=[s for ex, _ in riders for s in ex.scratch],
        compiler_params=_params(("arbitrary",)),
    )(*args)
    return res[:n_own_out], res[n_own_out:]


def _adamw_sharded(w, gparts, m, v, tr, name):
    R, C = w.shape
    n_part = gparts.shape[0]

    def body(w_ref, gp_ref, m_ref, v_ref, g_ref, d_ref, m2_ref, v2_ref):
        g = gp_ref[0].astype(F32)
        for d in range(1, n_part):
            g = g + gp_ref[d].astype(F32)
        delta, m2, v2 = _adam(w_ref[...], g, m_ref[...], v_ref[...])
        g_ref[...] = g
        d_ref[...] = delta
        m2_ref[...] = m2
        v2_ref[...] = v2

    row = pl.BlockSpec((tr, C), lambda i: (i, 0))
    sh = jax.ShapeDtypeStruct((R, C), F32)
    return pl.pallas_call(
        body, name=name, grid=(R // tr,),
        in_specs=[row, pl.BlockSpec((n_part, tr, C), lambda i: (0, i, 0)), row, row],
        out_specs=[row, row, row, row], out_shape=[sh, sh, sh, sh],
        compiler_params=_params(("arbitrary",)),
    )(w, gparts, m, v)


def _pack_vectors(vec_rows, dsg, dgq, dgk, dgk_c, loss_part):
    n_vec = len(vec_rows)

    def body(*refs):
        vecs = refs[:n_vec]
        dsg_ref, dgq_ref, dgk_ref, dgkc_ref, loss_ref, v_ref = refs[n_vec:]
        row = lax.broadcasted_iota(jnp.int32, (16, D_MODEL), 0)
        misc = jnp.concatenate([dsg_ref[...], dgq_ref[0:1, :], dgk_ref[0:1, :], dgkc_ref[0:1, :],
                                loss_ref[...]], axis=1)
        v = jnp.where(row == V_MISC, jnp.broadcast_to(misc, (16, D_MODEL)), 0.0)
        for (r, _), ref in zip(vec_rows, vecs):
            v = jnp.where(row == r, jnp.broadcast_to(ref[...], (16, D_MODEL)), v)
        v_ref[...] = v

    return pl.pallas_call(
        body, name="pack_vectors", out_shape=jax.ShapeDtypeStruct((16, D_MODEL), F32), compiler_params=_params(),
    )(*[a for _, a in vec_rows], dsg, dgq, dgk, dgk_c, loss_part)


def _pack_matrices(dws, dbs, drpb):
    def body(dws_ref, dbs_ref, drpb_ref, m_ref):
        for g in range(SGU_GROUPS):
            m_ref[128 * g:128 * (g + 1), :] = dws_ref[g]
            m_ref[M_DBS + 8 * g:M_DBS + 8 * (g + 1), :] = dbs_ref[g]
        for hd in range(N_HEADS):
            m_ref[M_RPB + 32 * hd:M_RPB + 32 * (hd + 1), :] = drpb_ref[hd]

    return pl.pallas_call(
        body, name="pack_matrices", out_shape=jax.ShapeDtypeStruct((M_ROWS, 128), F32), compiler_params=_params(),
    )(dws, dbs, drpb)


SMALL_NAMES = ("b_ada", "norm_g", "sgu_norm_g", "w_spatial", "b_spatial", "q_norm_g", "k_norm_g", "rpb")


def _adamw_small(vg, mg, ws, ms, vs):
    k = len(SMALL_NAMES)

    def body(*refs):
        vg_ref, mg_ref = refs[0], refs[1]
        w_refs = dict(zip(SMALL_NAMES, refs[2:2 + k]))
        m_refs = dict(zip(SMALL_NAMES, refs[2 + k:2 + 2 * k]))
        v_refs = dict(zip(SMALL_NAMES, refs[2 + 2 * k:2 + 3 * k]))
        o_refs = [dict(zip(SMALL_NAMES, refs[2 + (3 + i) * k:2 + (4 + i) * k])) for i in range(4)]
        loss_ref = refs[2 + 7 * k]

        sv = vg_ref[0]
        for d in range(1, N_DEV):
            sv = sv + vg_ref[d]
        loss_ref[...] = sv[V_MISC:V_MISC + 1, 896:1024]

        def total(lo, hi):
            s = mg_ref[0, lo:hi, :]
            for d in range(1, N_DEV):
                s = s + mg_ref[d, lo:hi, :]
            return s

        def emit(name, idx, g):
            res = _adam(w_refs[name][idx], g, m_refs[name][idx], v_refs[name][idx])
            for o, val in zip(o_refs, (g,) + res):
                o[name][idx] = val

        everything = (slice(None), slice(None))
        row = lambda r: sv[r:r + 1, :]
        emit("b_ada", everything, jnp.concatenate(
            [row(V_DSHIFT) + row(V_DCSHIFT), row(V_DSCALE) + row(V_DCSCALE), row(V_DGATE)], axis=1))
        emit("norm_g", everything, row(V_DNG) + row(V_DNG_CTX))
        misc = row(V_MISC)
        emit("sgu_norm_g", everything, misc[:, 0:512])
        emit("q_norm_g", everything, misc[:, 512:512 + HEAD_DIM])
        emit("k_norm_g", everything, misc[:, 640:640 + HEAD_DIM] + misc[:, 768:768 + HEAD_DIM])
        for g in range(SGU_GROUPS):
            emit("w_spatial", (0, g), total(128 * g, 128 * (g + 1)))
            emit("b_spatial", (0, slice(g, g + 1), slice(None)), total(M_DBS + 8 * g, M_DBS + 8 * (g + 1))[0:1, :])
        for hd in range(N_HEADS):
            by_dc = total(M_RPB + 32 * hd, M_RPB + 32 * (hd + 1))
            emit("rpb", (0, hd), by_dc.T[0:2 * WIN_R - 1, 0:2 * WIN_C - 1])

    shapes = [jax.ShapeDtypeStruct(w.shape, F32) for w in ws]
    res = pl.pallas_call(body, name="adamw_small", out_shape=shapes * 4 + [jax.ShapeDtypeStruct((1, 128), F32)],
                         compiler_params=_params())(vg, mg, *ws, *ms, *vs)
    return [res[i * k:(i + 1) * k] for i in range(4)], res[4 * k]


def _adamw_cctx(pc_g, w, m, v):
    def body(pc_ref, w_ref, m_ref, v_ref, g_ref, d_ref, m2_ref, v2_ref):
        pc = pc_ref[0, 0:1, :]
        for d in range(1, N_DEV):
            pc = pc + pc_ref[d, 0:1, :]
        cc = w_ref[...]
        sig = _sigmoid(cc)
        g = pc * (sig * (1.0 + cc * (1.0 - sig)))
        delta, m2, v2 = _adam(cc, g, m_ref[...], v_ref[...])
        g_ref[...] = g
        d_ref[...] = delta
        m2_ref[...] = m2
        v2_ref[...] = v2

    sh = jax.ShapeDtypeStruct((1, D_MODEL), F32)
    return pl.pallas_call(body, name="adamw_cctx", out_shape=[sh, sh, sh, sh], compiler_params=_params())(
        pc_g, w, m, v)


def _block_ones(n, blk):
    i = np.arange(n)
    return jnp.asarray((i[:, None] // blk == i[None, :] // blk).astype(np.float32), BF16)


def _rpb_pairs(rpb):
    n_off = 2 * WIN_C - 1
    cols = np.arange(GRID_W)
    c0 = np.clip(cols - WIN_C // 2, 0, GRID_W - WIN_C)
    in_win = (cols[None, :] >= c0[:, None]) & (cols[None, :] < c0[:, None] + WIN_C)
    dc = np.clip(cols[None, :] - cols[:, None] + (WIN_C - 1), 0, n_off - 1)
    expand = (dc[None] == np.arange(n_off)[:, None, None]) & in_win[None]
    toep = jnp.einsum("hrd,dqk->hrqk", rpb, jnp.asarray(expand, F32), precision=HI)
    toep = toep + jnp.asarray(np.where(in_win, 0.0, NEG_INF).astype(np.float32))
    neg = jnp.full((N_HEADS, 1, GRID_W, GRID_W), NEG_INF, F32)
    ext = jnp.concatenate([neg, toep, neg], axis=1)
    return jnp.concatenate([ext[:, :-1], ext[:, 1:]], axis=-1)


def _row_mask(rows):
    nrb = rows // Q_ROWS
    valid = np.zeros((3, Q_ROWS, 1, K_ROWS, 1), bool)
    for t, rb in enumerate((0, 1, nrb - 1)):
        kb = int(np.clip(Q_ROWS * rb - 4, 0, rows - K_ROWS))
        for i in range(Q_ROWS):
            r0 = int(np.clip(Q_ROWS * rb + i - WIN_R // 2, 0, rows - WIN_R))
            for j in range(K_ROWS):
                valid[t, i, 0, j, 0] = r0 <= kb + j < r0 + WIN_R
    full = np.broadcast_to(valid, (3, Q_ROWS, GRID_W, K_ROWS, GRID_W)).reshape(3, TQ, TK)
    return jnp.asarray(np.where(full, 0.0, NEG_INF).astype(np.float32))


def kernel(x, c, ctx, c_ctx, w_ada, b_ada, norm_g, w_in, sgu_norm_g, w_spatial, b_spatial, q_norm_g, k_norm_g, rpb, w_out, loss_target, m_c_ctx, m_w_ada, m_b_ada, m_norm_g, m_w_in, m_sgu_norm_g, m_w_spatial, m_b_spatial, m_q_norm_g, m_k_norm_g, m_rpb, m_w_out, v_c_ctx, v_w_ada, v_b_ada, v_norm_g, v_w_in, v_sgu_norm_g, v_w_spatial, v_b_spatial, v_q_norm_g, v_k_norm_g, v_rpb, v_w_out):
    me = 4 * lax.axis_index("x") + 2 * lax.axis_index("y") + lax.axis_index("c")
    x2, ctx2, tgt2 = x[0], ctx[0], loss_target[0]
    T, C = x2.shape[0], ctx2.shape[0]
    rows = T // GRID_W
    wada, win_t, wout = w_ada[0], w_in[0].T, w_out[0]
    ada_w = wada.shape[1]
    win_w = win_t.shape[0]

    row8 = lax.broadcasted_iota(jnp.int32, (8, D_MODEL), 0)
    c_blk = jnp.where(row8 == me, jnp.broadcast_to(c, (8, D_MODEL)), 0.0)
    b_sh = lax.dynamic_slice(b_ada, (0, me * ada_w), (1, ada_w))
    c_ctx_row = c_ctx.reshape(1, D_MODEL)

    ones512 = _block_ones(512, HEAD_DIM)
    ones8 = jnp.ones((8, 128), BF16)
    foldm = jnp.asarray((np.arange(512)[:, None] % HEAD_DIM == np.arange(128)[None, :]).astype(np.float32))
    lane_half = np.arange(128)[None, :, None] // GRID_W
    hsel = jnp.asarray((2 * np.arange(8)[:, None, None] + lane_half == np.arange(128)[None, None, :]).astype(np.float32),
                       BF16)
    foldr = jnp.asarray((np.arange(256)[None, :] // 8 == np.arange(32)[:, None]).astype(np.float32), BF16)
    gq512 = jnp.tile(q_norm_g, (1, N_HEADS))
    gk512 = jnp.tile(k_norm_g, (1, N_HEADS))
    ws = w_spatial[0]
    wst = ws.transpose(0, 2, 1)
    bsb = jnp.broadcast_to(b_spatial[0][:, :, None], (SGU_GROUPS, CHUNK, 128))
    pairs = _rpb_pairs(rpb[0])
    row_mask = _row_mask(rows)

    my_chip = me // 2
    order = jnp.stack([my_chip, my_chip ^ 2, my_chip ^ 1, my_chip ^ 3]).astype(jnp.int32)
    h, z, win_g, wout_g, s16, part_g = _inproj_fwd(order, x2, norm_g, c_blk, c_ctx_row, wada, b_sh,
                                                   win_t.astype(BF16), wout.astype(BF16), 512)
    w_in_b = win_g.reshape(D_IN, D_MODEL)
    w_out_b = wout_g.reshape(D_MODEL, D_MODEL)
    mod16 = part_g.transpose(1, 0, 2).reshape(16, 3 * D_MODEL)
    mod = lax.dynamic_slice(mod16, (me, 0), (1, 3 * D_MODEL))
    shift, scale, gate = mod[:, :D_MODEL], mod[:, D_MODEL:2 * D_MODEL], mod[:, 2 * D_MODEL:]
    cshift, cscale = mod16[8:9, :D_MODEL], mod16[8:9, D_MODEL:2 * D_MODEL]
    qs, kn = _qk_norm(z, ones512, gq512, gk512, 512)
    hc, zc, ckn = _ctx_fwd(ctx2, norm_g, cscale, cshift, w_in_b, ones512, gk512)
    out_a = _sgu_fwd(z, sgu_norm_g, ws, bsb, 512)
    ob, out_b, lse = _attn_fwd(qs, kn, z, ckn, zc, pairs, row_mask)

    dy, dmc, dw_out, dgate, loss_part = _outproj_loss_bwd(x2, tgt2, out_a, out_b, gate, w_out_b, 512)
    dw_out_blocks = dw_out.reshape(N_DEV, D_MODEL // N_DEV, D_MODEL)
    (dqs, dk, dv, dck, dcv, db_g, drpb), (gout_parts,) = _attn_bwd(
        qs, kn, z, ckn, zc, pairs, row_mask, ob, lse, dmc, hsel, foldr,
        [(_Hosted("a2a", dw_out_blocks), dw_out_blocks)])
    dzc_k, dzc_v, dgk_c = _ctx_k_bwd(zc, dck, dcv, ones512, gk512, foldm)
    dz, dw_in_t, dws, dbs, dsg, dgq, dgk = _bwd_mid(z, dmc, dqs, dk, dv, db_g, h, hc, dzc_k, dzc_v, sgu_norm_g,
                                                    ws, wst, bsb, ones8, ones512, gq512, gk512, foldm, 256)
    mloc = _pack_matrices(dws, dbs, drpb)
    (chip_sums,) = _chip_presum([dw_in_t.reshape(N_DEV, win_w, D_MODEL)], "presum_dw_in")
    (grad_x, dshift, dscale, dng), (gin_parts, mg) = _inproj_bwd_dx(
        [dz], 0, w_in_b, x2, dy, norm_g, scale, shift, 512, "inproj_bwd_dx",
        [(_Hosted("chips", chip_sums), chip_sums), (_Hosted("ag", mloc), mloc)])
    (dcshift, dcscale, dng_c), _ = _inproj_bwd_dx([dzc_k, dzc_v], 4 * 512, w_in_b, ctx2, None, norm_g, cscale,
                                                  cshift, C, "ctx_bwd_dx")
    res_in = _adamw_sharded(win_t, gin_parts, m_w_in[0].T, v_w_in[0].T, 112, "adamw_w_in")
    res_out = _adamw_sharded(wout, gout_parts, m_w_out[0], v_w_out[0], 128, "adamw_w_out")

    zero_row = jnp.zeros((1, D_MODEL), F32)
    vec_rows = [(V_DSHIFT, dshift), (V_DSCALE, dscale), (V_DGATE, dgate), (V_DCSHIFT, dcshift),
                (V_DCSCALE, dcscale), (V_ZERO, zero_row), (V_DNG, dng), (V_DNG_CTX, dng_c)]
    vloc = _pack_vectors(vec_rows, dsg, dgq, dgk, dgk_c, loss_part)
    vg = _allgather_direct(vloc, "gather_small")
    small_w = (b_ada, norm_g, sgu_norm_g, w_spatial, b_spatial, q_norm_g, k_norm_g, rpb)
    small_m = (m_b_ada, m_norm_g, m_sgu_norm_g, m_w_spatial, m_b_spatial, m_q_norm_g, m_k_norm_g, m_rpb)
    small_v = (v_b_ada, v_norm_g, v_sgu_norm_g, v_w_spatial, v_b_spatial, v_q_norm_g, v_k_norm_g, v_rpb)
    res_small, loss_row = _adamw_small(vg, mg, small_w, small_m, small_v)

    dm_all = vg[:, V_DSHIFT:V_DGATE + 1, :].reshape(N_DEV, 3 * D_MODEL)
    dc_all = vg[:, V_DCSHIFT:V_ZERO + 1, :].reshape(N_DEV, 3 * D_MODEL)
    dm_sh = lax.dynamic_slice(dm_all, (0, me * ada_w), (N_DEV, ada_w))
    dc_sh = lax.dynamic_slice(dc_all, (0, me * ada_w), (N_DEV, ada_w))
    *res_ada, pc = _ada_bwd(s16, dm_sh, dc_sh, wada, m_w_ada[0], v_w_ada[0])
    pc_g = _allgather_direct(pc, "gather_cctx")
    res_cctx = _adamw_cctx(pc_g, c_ctx_row, m_c_ctx.reshape(1, D_MODEL), v_c_ctx.reshape(1, D_MODEL))

    loss = loss_row[0, 0]
    outs = [loss, grad_x[None]]
    for kind in range(4):
        by_name = dict(zip(SMALL_NAMES, res_small[kind]))
        by_name.update(c_ctx=res_cctx[kind].reshape(D_MODEL), w_ada=res_ada[kind][None],
                       w_in=res_in[kind].T[None], w_out=res_out[kind][None])
        outs += [by_name[nme] for nme in ("c_ctx", "w_ada", "b_ada", "norm_g", "w_in", "sgu_norm_g", "w_spatial",
                                          "b_spatial", "q_norm_g", "k_norm_g", "rpb", "w_out")]
    return tuple(outs)
```

```python
import functools

import numpy as np
import jax
import jax.numpy as jnp
from jax import lax
from jax.experimental import pallas as pl
from jax.experimental.pallas import tpu as pltpu

F32 = jnp.float32
BF16 = jnp.bfloat16
HI = lax.Precision.HIGHEST

N_DEV = 8
D_MODEL = 1024
D_A = 512
D_B = 512
D_IN = 3584
N_BRANCH = 7
HEAD_DIM = 64
N_HEADS = 8
GRID_W = 64
WIN_R = 8
WIN_C = 16
CHUNK = 128
SGU_GROUPS = 4
EPS = 1e-6
NEG_INF = -1e30
Q_ROWS = 4
K_ROWS = 12
TQ = Q_ROWS * GRID_W
TK = K_ROWS * GRID_W
N_DIAG = 22
ATT_SUB = 4
ATT_SCALE = HEAD_DIM ** -0.5
LOG2E = 1.4426950408889634
LN2 = 0.6931471805599453

ADAM_LR = 0.001
ADAM_B1 = 0.9
ADAM_B2 = 0.999
ADAM_EPS = 1e-08
ADAM_WD = 0.01
ADAM_STEP = 10

VMEM_LIMIT = 56 * 1024 * 1024
MESH = pl.DeviceIdType.MESH

V_DSHIFT, V_DSCALE, V_DGATE, V_DCSHIFT, V_DCSCALE, V_ZERO, V_DNG, V_DNG_CTX, V_MISC = range(9)
M_DBS = 512


def _params(sem=None):
    return pltpu.CompilerParams(dimension_semantics=sem, vmem_limit_bytes=VMEM_LIMIT)


def _sigmoid(x):
    return 1.0 / (1.0 + jnp.exp(-x))


def _gelu_parts(x):
    cdf = 0.5 * (1.0 + lax.erf(x * 0.7071067811865476))
    pdf = jnp.exp(-0.5 * x * x) * 0.3989422804014327
    return x * cdf, cdf + x * pdf


def _nt(a, b):
    return lax.dot_general(a, b, (((1,), (1,)), ((), ())), preferred_element_type=F32)


def _tn(a, b):
    return lax.dot_general(a, b, (((0,), (0,)), ((), ())), preferred_element_type=F32)


def _dot2(v, ones_bf):
    hi = v.astype(BF16)
    lo = (v - hi.astype(F32)).astype(BF16)
    return (jnp.dot(hi, ones_bf, preferred_element_type=F32)
            + jnp.dot(lo, ones_bf, preferred_element_type=F32))


def _head_sum(v, ones_ref):
    return jnp.dot(v.astype(BF16), ones_ref[...], preferred_element_type=F32)


def _adam(w, g, m, v):
    m2 = ADAM_B1 * m + (1.0 - ADAM_B1) * g
    v2 = ADAM_B2 * v + (1.0 - ADAM_B2) * (g * g)
    m_hat = m2 / (1.0 - ADAM_B1 ** ADAM_STEP)
    v_hat = v2 / (1.0 - ADAM_B2 ** ADAM_STEP)
    delta = -ADAM_LR * (m_hat / (jnp.sqrt(v_hat) + ADAM_EPS) + ADAM_WD * w)
    return delta, m2, v2


def _chip_presum(arrs, name):
    n = len(arrs)
    n_chip = N_DEV // 2
    outs_shape = [jax.ShapeDtypeStruct((n_chip,) + tuple(a.shape[1:]), a.dtype) for a in arrs]

    def body(*refs):
        ins, sums = refs[:n], refs[n:2 * n]
        tmps = refs[2 * n:3 * n]
        s1, r1 = refs[3 * n:]
        c = lax.axis_index("c")
        sib = (lax.axis_index("x"), lax.axis_index("y"), 1 - c)
        all_chips = [(0, 0), (0, 1), (1, 0), (1, 1)]

        swaps = []
        for a in range(n):
            for q, (qx, qy) in enumerate(all_chips):
                cp = pltpu.make_async_remote_copy(
                    src_ref=ins[a].at[4 * qx + 2 * qy + (1 - c)], dst_ref=tmps[a].at[q],
                    send_sem=s1.at[n_chip * a + q], recv_sem=r1.at[n_chip * a + q],
                    device_id=sib, device_id_type=MESH)
                cp.start()
                swaps.append(cp)
        for a in range(n):
            for q, (qx, qy) in enumerate(all_chips):
                swaps[n_chip * a + q].wait_recv()
                both = ins[a][4 * qx + 2 * qy + c].astype(F32) + tmps[a][q].astype(F32)
                sums[a][q] = both.astype(sums[a].dtype)
        for cp in swaps:
            cp.wait_send()

    vm = pl.BlockSpec(memory_space=pltpu.VMEM)
    res = pl.pallas_call(
        body, name=name, out_shape=outs_shape, in_specs=[vm] * n, out_specs=[vm] * n,
        scratch_shapes=([pltpu.VMEM(s.shape, s.dtype) for s in outs_shape]
                        + [pltpu.SemaphoreType.DMA((n_chip * n,)), pltpu.SemaphoreType.DMA((n_chip * n,))]),
        compiler_params=_params(),
    )(*arrs)
    return list(res)


class _Hosted:
    def __init__(self, kind, src):
        self.kind = kind
        n_slot = {"a2a": N_DEV, "ag": N_DEV, "chips": N_DEV // 2}[kind]
        blk = src.shape if kind == "ag" else src.shape[1:]
        self.out_shape = jax.ShapeDtypeStruct((n_slot,) + tuple(blk), src.dtype)
        self.n_peer = n_slot - 1
        self.scratch = [pltpu.SemaphoreType.DMA((self.n_peer,)), pltpu.SemaphoreType.DMA((self.n_peer,)),
                        pltpu.SemaphoreType.DMA]

    def _copies(self, src, dst, send_sems, recv_sems, loc_sem, landing):
        x, y, c = lax.axis_index("x"), lax.axis_index("y"), lax.axis_index("c")
        if self.kind == "chips":
            me = 2 * x + y
            peers = [((px, py, c), 2 * px + py) for px, py in ((1 - x, y), (x, 1 - y), (1 - x, 1 - y))]
        else:
            me = 4 * x + 2 * y + c
            peers = []
            for k in range(1, N_DEV):
                px = 1 - x if (k >> 2) & 1 else x
                py = 1 - y if (k >> 1) & 1 else y
                pc = 1 - c if k & 1 else c
                peers.append(((px, py, pc), 4 * px + 2 * py + pc))
        remote = []
        for k, (peer, pid) in enumerate(peers):
            s = src if self.kind == "ag" else src.at[pid]
            remote.append(pltpu.make_async_remote_copy(
                src_ref=s, dst_ref=dst.at[pid if landing else me],
                send_sem=send_sems.at[k], recv_sem=recv_sems.at[k], device_id=peer, device_id_type=MESH))
        local = pltpu.make_async_copy(src if self.kind == "ag" else src.at[me], dst.at[me], loc_sem)
        return remote, local

    def start(self, src, dst, send_sems, recv_sems, loc_sem):
        remote, local = self._copies(src, dst, send_sems, recv_sems, loc_sem, landing=False)
        for cp in remote:
            cp.start()
        local.start()

    def wait(self, src, dst, send_sems, recv_sems, loc_sem):
        remote, local = self._copies(src, dst, send_sems, recv_sems, loc_sem, landing=True)
        for cp in remote:
            cp.wait_recv()
        for cp in remote:
            cp.wait_send()
        local.wait()


ANY_SPEC = pl.BlockSpec(memory_space=pl.ANY)


def _allgather_direct(arr, name):
    ex = _Hosted("ag", arr)

    def body(src, dst, *sems):
        ex.start(src, dst, *sems)
        ex.wait(src, dst, *sems)

    return pl.pallas_call(body, name=name, out_shape=ex.out_shape, in_specs=[ANY_SPEC], out_specs=ANY_SPEC,
                          scratch_shapes=ex.scratch)(arr)


def _ada_scratch(n_col):
    return ([pltpu.VMEM((N_DEV, 8, D_MODEL), F32), pltpu.VMEM((16, n_col), F32), pltpu.VMEM((N_DEV, 16, n_col), F32)]
            + [pltpu.SemaphoreType.DMA((N_DEV - 1,)) for _ in range(4)])


def _ada_modulation(cb_ref, cc_ref, w_ref, b_ref, cstack, part, parts, s1, r1, s2, r2):
    x, y, c = lax.axis_index("x"), lax.axis_index("y"), lax.axis_index("c")
    me = 4 * x + 2 * y + c
    peers = []
    for k in range(1, N_DEV):
        px = 1 - x if (k >> 2) & 1 else x
        py = 1 - y if (k >> 1) & 1 else y
        pc = 1 - c if k & 1 else c
        peers.append(((px, py, pc), 4 * px + 2 * py + pc))

    def exchange(src, dst, send_sems, recv_sems):
        for k, (peer, _) in enumerate(peers):
            pltpu.make_async_remote_copy(src_ref=src, dst_ref=dst.at[me], send_sem=send_sems.at[k],
                                         recv_sem=recv_sems.at[k], device_id=peer, device_id_type=MESH).start()
        dst[me] = src[...]
        waits = [pltpu.make_async_remote_copy(src_ref=src, dst_ref=dst.at[pid], send_sem=send_sems.at[k],
                                              recv_sem=recv_sems.at[k], device_id=peer, device_id_type=MESH)
                 for k, (peer, pid) in enumerate(peers)]
        for cp in waits:
            cp.wait_recv()
        for cp in waits:
            cp.wait_send()

    exchange(cb_ref, cstack, s1, r1)
    c_all = cstack[0]
    for d in range(1, N_DEV):
        c_all = c_all + cstack[d]
    row = lax.broadcasted_iota(jnp.int32, (8, D_MODEL), 0)
    cc = jnp.where(row == 0, jnp.broadcast_to(cc_ref[...], (8, D_MODEL)), 0.0)
    call = jnp.concatenate([c_all, cc], axis=0)
    s = call * _sigmoid(call)
    part[...] = jnp.dot(s, w_ref[...], preferred_element_type=F32, precision=HI) + b_ref[...]
    exchange(part, parts, s2, r2)
    return s


def _ada_bwd(s16, dm, dc, w, m, v):
    def body(s_ref, dm_ref, dc_ref, w_ref, m_ref, v_ref, g_ref, d_ref, m2_ref, v2_ref, pc_ref):
        dct = jnp.sum(dc_ref[...], axis=0, keepdims=True)
        row = lax.broadcasted_iota(jnp.int32, dc_ref.shape, 0)
        dcb = jnp.where(row == 0, jnp.broadcast_to(dct, dc_ref.shape), 0.0)
        dm16 = jnp.concatenate([dm_ref[...], dcb], axis=0)
        g = lax.dot_general(s_ref[...], dm16, (((0,), (0,)), ((), ())),
                            preferred_element_type=F32, precision=HI)
        w_ = w_ref[...]
        delta, m2, v2 = _adam(w_, g, m_ref[...], v_ref[...])
        g_ref[...] = g
        d_ref[...] = delta
        m2_ref[...] = m2
        v2_ref[...] = v2
        pc_ref[...] = lax.dot_general(dcb, w_, (((1,), (1,)), ((), ())),
                                      preferred_element_type=F32, precision=HI)

    sh = jax.ShapeDtypeStruct(w.shape, F32)
    return pl.pallas_call(
        body, name="ada_bwd",
        out_shape=[sh, sh, sh, sh, jax.ShapeDtypeStruct((8, D_MODEL), F32)],
        compiler_params=_params(),
    )(s16, dm, dc, w, m, v)


def _head_norm(zk, ones_ref, gain):
    ss = _head_sum(zk * zk, ones_ref)
    return zk * lax.rsqrt(ss * (1.0 / HEAD_DIM) + EPS) * gain


def _inproj_fwd(order, x, ng, c_blk, c_ctx_row, w_ada_sh, b_ada_sh, w_blk_t, w_out_blk, tm):
    T = x.shape[0]
    nt = T // tm
    n_pass = N_DEV // 2
    blk_rows = w_blk_t.shape[0]
    n_col = w_ada_sh.shape[1]
    rider = _Hosted("ag", w_out_blk)
    n_ada = len(_ada_scratch(n_col))

    def body(order_ref, x_ref, g_ref, cb_ref, cc_ref, wa_ref, ba_ref, wb_ref, wo_ref,
             h_out, z_ref, wt_out, wo_out, s_out, parts_out,
             hs, wt, modv, send_sems, recv_sems, loc_sem, h_sem, wt_sem, *rest):
        ada_sc, rider_sems = rest[:n_ada], rest[n_ada:]
        p, i = pl.program_id(0), pl.program_id(1)
        x, y, c = lax.axis_index("x"), lax.axis_index("y"), lax.axis_index("c")
        me, sib = (x, y, c), (x, y, 1 - c)
        chips = [(1 - x, y), (x, 1 - y), (1 - x, 1 - y)]

        def slot(px, py, pc):
            return 4 * px + 2 * py + pc

        def copy(k, block, to, src=None):
            return pltpu.make_async_remote_copy(
                src_ref=wt.at[slot(*block)] if src is None else src, dst_ref=wt.at[slot(*block)],
                send_sem=send_sems.at[k], recv_sem=recv_sems.at[k], device_id=to, device_id_type=MESH)

        own = pltpu.make_async_copy(wb_ref, wt.at[slot(*me)], loc_sem)
        h_copy = pltpu.make_async_copy(hs, h_out, h_sem)
        wt_copy = pltpu.make_async_copy(wt, wt_out, wt_sem)
        first = [copy(1 + j, me, (*chip, c), src=wb_ref) for j, chip in enumerate(chips)] + [copy(0, me, sib, src=wb_ref)]
        passed = [copy(4 + j, (*chip, c), sib) for j, chip in enumerate(chips)]

        @pl.when(jnp.logical_and(p == 0, i == 0))
        def _():
            s_out[...] = _ada_modulation(cb_ref, cc_ref, wa_ref, ba_ref, *ada_sc)
            own.start()
            for cp in first:
                cp.start()
            rider.start(wo_ref, wo_out, *rider_sems)
            parts = ada_sc[2]
            parts_out[...] = parts[...]
            my_row = pl.ds(slot(*me), 1)
            mod = jnp.concatenate([parts[d, my_row, :] for d in range(N_DEV)], axis=1)
            modv[0:1, :] = mod[:, 0:D_MODEL]
            modv[1:2, :] = mod[:, D_MODEL:2 * D_MODEL]
            own.wait()
            copy(0, sib, me).wait_recv()

        for j, chip in enumerate(chips):
            @pl.when(jnp.logical_and(p == j + 1, i == 0))
            def _(j=j, chip=chip):
                copy(1 + j, (*chip, c), me).wait_recv()
                passed[j].start()
                copy(4 + j, (*chip, 1 - c), me).wait_recv()
                if j == len(chips) - 1:
                    wt_copy.start()

        rows = pl.ds(pl.multiple_of(i * tm, tm), tm)

        @pl.when(p == 0)
        def _():
            xv = x_ref[...]
            r = lax.rsqrt(jnp.mean(xv * xv, axis=-1, keepdims=True) + EPS)
            hs[rows, :] = ((xv * r * g_ref[...]) * (1.0 + modv[1:2, :]) + modv[0:1, :]).astype(BF16)

        @pl.when(jnp.logical_and(p == 1, i == 0))
        def _():
            h_copy.start()

        w_pair = wt[pl.ds(2 * order_ref[p], 2)].reshape(2 * blk_rows, D_MODEL)
        z_ref[...] = _nt(hs[rows, :], w_pair).astype(BF16)

        @pl.when(jnp.logical_and(p == n_pass - 1, i == nt - 1))
        def _():
            for cp in first + passed:
                cp.wait_send()
            h_copy.wait()
            wt_copy.wait()
            rider.wait(wo_ref, wo_out, *rider_sems)

    whole = lambda shape: pl.BlockSpec(shape, lambda p, i, o: (0,) * len(shape))
    grid_spec = pltpu.PrefetchScalarGridSpec(
        num_scalar_prefetch=1, grid=(n_pass, nt),
        in_specs=[pl.BlockSpec((tm, D_MODEL), lambda p, i, o: (jnp.where(p == 0, i, nt - 1), 0)),
                  whole((1, D_MODEL)), whole((8, D_MODEL)), whole((1, D_MODEL)), whole((D_MODEL, n_col)),
                  whole((1, n_col)), ANY_SPEC, ANY_SPEC],
        out_specs=[ANY_SPEC, pl.BlockSpec((tm, 2 * blk_rows), lambda p, i, o: (i, o[p])), ANY_SPEC, ANY_SPEC,
                   whole((16, D_MODEL)), whole((N_DEV, 16, n_col))],
        scratch_shapes=[pltpu.VMEM((T, D_MODEL), BF16), pltpu.VMEM((N_DEV, blk_rows, D_MODEL), BF16),
                        pltpu.VMEM((8, D_MODEL), F32),
                        pltpu.SemaphoreType.DMA((7,)), pltpu.SemaphoreType.DMA((7,)), pltpu.SemaphoreType.DMA,
                        pltpu.SemaphoreType.DMA, pltpu.SemaphoreType.DMA] + _ada_scratch(n_col) + rider.scratch)
    return pl.pallas_call(
        body, name="inproj_fwd", grid_spec=grid_spec,
        out_shape=[jax.ShapeDtypeStruct((T, D_MODEL), BF16), jax.ShapeDtypeStruct((T, D_IN), BF16),
                   jax.ShapeDtypeStruct((N_DEV, blk_rows, D_MODEL), BF16), rider.out_shape,
                   jax.ShapeDtypeStruct((16, D_MODEL), F32), jax.ShapeDtypeStruct((N_DEV, 16, n_col), F32)],
        compiler_params=_params(("arbitrary", "arbitrary")),
    )(order, x, ng, c_blk, c_ctx_row, w_ada_sh, b_ada_sh, w_blk_t, w_out_blk)


def _qk_norm(z, ones_blk, gq, gk, tm):
    T = z.shape[0]

    def body(q_ref, k_ref, ones_ref, gq_ref, gk_ref, qs_ref, kn_ref):
        qs = _head_norm(q_ref[...].astype(F32), ones_ref, gq_ref[...]) * (ATT_SCALE * LOG2E)
        qs_ref[...] = qs.astype(BF16)
        kn_ref[...] = _head_norm(k_ref[...].astype(F32), ones_ref, gk_ref[...]).astype(BF16)

    v512 = pl.BlockSpec((1, 512), lambda i: (0, 0))
    return pl.pallas_call(
        body, name="qk_norm", grid=(T // tm,),
        in_specs=[pl.BlockSpec((tm, 512), lambda i: (i, 3)), pl.BlockSpec((tm, 512), lambda i: (i, 4)),
                  pl.BlockSpec((512, 512), lambda i: (0, 0)), v512, v512],
        out_specs=[pl.BlockSpec((tm, 512), lambda i: (i, 0)), pl.BlockSpec((tm, 512), lambda i: (i, 0))],
        out_shape=[jax.ShapeDtypeStruct((T, 512), BF16), jax.ShapeDtypeStruct((T, 512), BF16)],
        compiler_params=_params(("arbitrary",)),
    )(z, z, ones_blk, gq, gk)


def _ctx_fwd(ctx, ng, cscale, cshift, w_in_t, ones_blk, gk):
    C = ctx.shape[0]

    def body(x_ref, g_ref, sc_ref, sh_ref, w_ref, ones_ref, gk_ref, h_ref, z_ref, kn_ref):
        xv = x_ref[...]
        r = lax.rsqrt(jnp.mean(xv * xv, axis=-1, keepdims=True) + EPS)
        h = (xv * r * g_ref[...]) * (1.0 + sc_ref[...]) + sh_ref[...]
        hb = h.astype(BF16)
        h_ref[...] = hb
        zk = _nt(hb, w_ref[0:512, :])
        zv = _nt(hb, w_ref[512:1024, :])
        z_ref[:, 0:512] = zk.astype(BF16)
        z_ref[:, 512:1024] = zv.astype(BF16)
        kn_ref[...] = _head_norm(zk, ones_ref, gk_ref[...]).astype(BF16)

    vec = pl.BlockSpec((1, D_MODEL), lambda i: (0, 0))
    return pl.pallas_call(
        body, name="ctx_fwd", grid=(1,),
        in_specs=[pl.BlockSpec((C, D_MODEL), lambda i: (0, 0)), vec, vec, vec,
                  pl.BlockSpec((1024, D_MODEL), lambda i: (2, 0)),
                  pl.BlockSpec((512, 512), lambda i: (0, 0)), pl.BlockSpec((1, 512), lambda i: (0, 0))],
        out_specs=[pl.BlockSpec((C, D_MODEL), lambda i: (0, 0)), pl.BlockSpec((C, 1024), lambda i: (0, 0)),
                   pl.BlockSpec((C, 512), lambda i: (0, 0))],
        out_shape=[jax.ShapeDtypeStruct((C, D_MODEL), BF16), jax.ShapeDtypeStruct((C, 1024), BF16),
                   jax.ShapeDtypeStruct((C, 512), BF16)],
        compiler_params=_params(("arbitrary",)),
    )(ctx, ng, cscale, cshift, w_in_t, ones_blk, gk)


def _sgu_chunk_fwd(au, av, ag, sg, ws_bf, bsb):
    gu, dgu = _gelu_parts(au)
    gv, dgv = _gelu_parts(av)
    rr = lax.rsqrt(jnp.mean(gv * gv, axis=-1, keepdims=True) + EPS)
    vhat = gv * rr
    vn = vhat * sg
    mixed = jnp.dot(ws_bf, vn.astype(BF16), preferred_element_type=F32) + bsb
    sig = _sigmoid(ag)
    sl = ag * sig
    return gu * mixed * sl, (gu, dgu, dgv, rr, vhat, vn, mixed, sig, sl)


def _sgu_fwd(z, sgn, ws, bsb, tm):
    T = z.shape[0]

    def body(au_ref, av_ref, ag_ref, sg_ref, ws_ref, bsb_ref, o_ref):
        for g in range(SGU_GROUPS):
            ws_bf = ws_ref[g].astype(BF16)
            sg = sg_ref[:, 128 * g:128 * (g + 1)]
            bsb_g = bsb_ref[g]
            for j in range(tm // CHUNK):
                rs, cs = slice(CHUNK * j, CHUNK * (j + 1)), slice(128 * g, 128 * (g + 1))
                out, _ = _sgu_chunk_fwd(au_ref[rs, cs].astype(F32), av_ref[rs, cs].astype(F32),
                                        ag_ref[rs, cs].astype(F32), sg, ws_bf, bsb_g)
                o_ref[rs, cs] = out.astype(BF16)

    return pl.pallas_call(
        body, name="sgu_fwd", grid=(T // tm,),
        in_specs=[pl.BlockSpec((tm, 512), lambda i: (i, 0)), pl.BlockSpec((tm, 512), lambda i: (i, 1)),
                  pl.BlockSpec((tm, 512), lambda i: (i, 2)), pl.BlockSpec((1, 512), lambda i: (0, 0)),
                  pl.BlockSpec((SGU_GROUPS, CHUNK, CHUNK), lambda i: (0, 0, 0)),
                  pl.BlockSpec((SGU_GROUPS, CHUNK, 128), lambda i: (0, 0, 0))],
        out_specs=pl.BlockSpec((tm, 512), lambda i: (i, 0)),
        out_shape=jax.ShapeDtypeStruct((T, 512), BF16),
        compiler_params=_params(("arbitrary",)),
    )(z, z, z, sgn, ws, bsb)


def _attn_type(rb, nrb):
    return jnp.where(rb == 0, 0, jnp.where(rb == nrb - 1, 2, 1))


def _attn_specs(T, C):
    return [
        pl.BlockSpec((ATT_SUB * TQ, 128), lambda hp, st: (st, hp)),
        pl.BlockSpec((T, 128), lambda hp, st: (0, hp)),
        pl.BlockSpec((T, 128), lambda hp, st: (0, 20 + hp)),
        pl.BlockSpec((C, 128), lambda hp, st: (0, hp)),
        pl.BlockSpec((C, 128), lambda hp, st: (0, 4 + hp)),
        pl.BlockSpec((2, 2 * WIN_R, GRID_W, 128), lambda hp, st: (hp, 0, 0, 0)),
        pl.BlockSpec((3, TQ, TK), lambda hp, st: (0, 0, 0)),
        pl.BlockSpec((ATT_SUB * TQ, 128), lambda hp, st: (st, 24 + hp)),
    ]


def _build_bias(pairs_ref, mask_ref, bias_sc):
    for t in range(3):
        for hh in range(2):
            for i in range(Q_ROWS):
                for mm in range(K_ROWS // 2):
                    p = min(max(WIN_R - Q_ROWS * t + 2 * mm - i, 0), 2 * WIN_R - 1)
                    rs, cs = slice(GRID_W * i, GRID_W * (i + 1)), slice(128 * mm, 128 * (mm + 1))
                    bias_sc[t, hh, rs, cs] = (pairs_ref[hh, p] + mask_ref[t, rs, cs]) * LOG2E


def _attn_fwd(qs, kn, z, ckn, zc, pairs, row_mask):
    T, C = qs.shape[0], ckn.shape[0]
    rows = T // GRID_W
    nrb = rows // Q_ROWS

    def body(q_ref, k_ref, v_ref, ck_ref, cv_ref, pairs_ref, mask_ref, bg_ref, ob_ref, outb_ref, lse_ref, bias_sc):
        @pl.when(pl.program_id(1) == 0)
        def _():
            _build_bias(pairs_ref, mask_ref, bias_sc)

        ck2, cv2 = ck_ref[...], cv_ref[...]
        lane = lax.broadcasted_iota(jnp.int32, (1, 128), 1)
        for sub in range(ATT_SUB):
            rb = ATT_SUB * pl.program_id(1) + sub
            bias_ref = bias_sc.at[_attn_type(rb, nrb)]
            rs = slice(TQ * sub, TQ * (sub + 1))
            ks = pl.multiple_of(jnp.clip(Q_ROWS * rb - 4, 0, rows - K_ROWS) * GRID_W, GRID_W)
            q2 = q_ref[rs, :]
            k2 = k_ref[pl.ds(ks, TK), :]
            v2 = v_ref[pl.ds(ks, TK), :]
            o_acc = jnp.zeros((TQ, 128), F32)
            lse_acc = jnp.zeros((TQ, 128), F32)
            for hh in range(2):
                msk = (lane >= HEAD_DIM) == bool(hh)
                qm = jnp.where(msk, q2, jnp.zeros_like(q2))
                s = _nt(qm, k2) + bias_ref[hh]
                sc = _nt(qm, ck2)
                m = jnp.maximum(jnp.max(s, axis=-1, keepdims=True), jnp.max(sc, axis=-1, keepdims=True))
                p = jnp.exp2(s - m)
                pc = jnp.exp2(sc - m)
                va = jnp.where(msk, v2, jnp.ones_like(v2))
                cva = jnp.where(msk, cv2, jnp.ones_like(cv2))
                num = (jnp.dot(p.astype(BF16), va, preferred_element_type=F32)
                       + jnp.dot(pc.astype(BF16), cva, preferred_element_type=F32))
                den = pltpu.roll(num, HEAD_DIM, 1)
                o_acc = jnp.where(msk, num / den, o_acc)
                lse_acc = jnp.where(msk, m + jnp.log(den) * LOG2E, lse_acc)
            ob_ref[rs, :] = o_acc.astype(BF16)
            lse_ref[rs, :] = lse_acc
            bg = bg_ref[rs, :].astype(F32)
            outb_ref[rs, :] = (o_acc * (bg * _sigmoid(bg))).astype(BF16)

    tile = pl.BlockSpec((ATT_SUB * TQ, 128), lambda hp, st: (st, hp))
    return pl.pallas_call(
        body, name="attn_fwd", grid=(4, nrb // ATT_SUB),
        in_specs=_attn_specs(T, C),
        out_specs=[tile, tile, tile],
        out_shape=[jax.ShapeDtypeStruct((T, 512), BF16), jax.ShapeDtypeStruct((T, 512), BF16),
                   jax.ShapeDtypeStruct((T, 512), F32)],
        scratch_shapes=[pltpu.VMEM((3, 2, TQ, TK), F32)],
        compiler_params=_params(("arbitrary", "arbitrary")),
    )(qs, kn, z, ckn, zc, pairs, row_mask, z)


def _outproj_loss_bwd(x, tgt, out_a, out_b, gate, w_out, tm):
    T = x.shape[0]
    nt = T // tm

    def body(x_ref, t_ref, oa_ref, ob_ref, gate_ref, w_ref, dy_ref, dmc_ref, dw_ref, dgate_ref, loss_ref, acc):
        @pl.when(pl.program_id(0) == 0)
        def _():
            acc[...] = jnp.zeros_like(acc)
            dgate_ref[...] = jnp.zeros_like(dgate_ref)
            loss_ref[...] = jnp.zeros_like(loss_ref)

        oa, ob = oa_ref[...], ob_ref[...]
        gate_v = gate_ref[...]
        mix = (jnp.dot(oa, w_ref[0:512, :], preferred_element_type=F32)
               + jnp.dot(ob, w_ref[512:1024, :], preferred_element_type=F32))
        e = x_ref[...] + gate_v * mix - t_ref[...]
        se = jnp.sum(jnp.sum(e * e, axis=0, keepdims=True), axis=1, keepdims=True)
        loss_ref[...] += jnp.broadcast_to(se * (0.5 / D_MODEL), loss_ref.shape)
        dy = e * (1.0 / D_MODEL)
        dy_ref[...] = dy
        dgate_ref[...] += jnp.sum(dy * mix, axis=0, keepdims=True)
        dmix = (dy * gate_v).astype(BF16)
        dmc_ref[...] = _nt(dmix, w_ref[...]).astype(BF16)
        acc[0:512, :] += _tn(oa, dmix)
        acc[512:1024, :] += _tn(ob, dmix)

        @pl.when(pl.program_id(0) == nt - 1)
        def _():
            dw_ref[...] = acc[...].astype(BF16)

    row = lambda w: pl.BlockSpec((tm, w), lambda i: (i, 0))
    return pl.pallas_call(
        body, name="outproj_loss_bwd", grid=(nt,),
        in_specs=[row(D_MODEL), row(D_MODEL), row(512), row(512),
                  pl.BlockSpec((1, D_MODEL), lambda i: (0, 0)),
                  pl.BlockSpec((D_MODEL, D_MODEL), lambda i: (0, 0))],
        out_specs=[row(D_MODEL), row(D_MODEL), pl.BlockSpec((D_MODEL, D_MODEL), lambda i: (0, 0)),
                   pl.BlockSpec((1, D_MODEL), lambda i: (0, 0)), pl.BlockSpec((1, 128), lambda i: (0, 0))],
        out_shape=[jax.ShapeDtypeStruct((T, D_MODEL), F32), jax.ShapeDtypeStruct((T, D_MODEL), BF16),
                   jax.ShapeDtypeStruct((D_MODEL, D_MODEL), BF16), jax.ShapeDtypeStruct((1, D_MODEL), F32),
                   jax.ShapeDtypeStruct((1, 128), F32)],
        scratch_shapes=[pltpu.VMEM((D_MODEL, D_MODEL), F32)],
        compiler_params=_params(("arbitrary",)),
    )(x, tgt, out_a, out_b, gate, w_out)


def _attn_bwd(qs, kn, z, ckn, zc, pairs, row_mask, ob, lse, dmc, hsel, fold, riders):
    T, C = qs.shape[0], ckn.shape[0]
    rows = T // GRID_W
    nrb = rows // Q_ROWS
    n_st = nrb // ATT_SUB
    n_rid = len(riders)

    def body(q_ref, k_ref, v_ref, ck_ref, cv_ref, pairs_ref, mask_ref, bg_ref, ob_ref, lse_ref, do_ref,
             hsel_ref, fold_ref, *rest):
        rid_src = rest[:n_rid]
        dq_ref, dk_ref, dv_ref, dck_ref, dcv_ref, dbg_ref, drpb_ref = rest[n_rid:n_rid + 7]
        rid_dst = rest[n_rid + 7:2 * n_rid + 7]
        bias_sc, dacc_ref = rest[2 * n_rid + 7:2 * n_rid + 9]
        rid_sems = rest[2 * n_rid + 9:]
        hp, st = pl.program_id(0), pl.program_id(1)

        @pl.when(jnp.logical_and(hp == 0, st == 0))
        def _():
            for r, (ex, _) in enumerate(riders):
                ex.start(rid_src[r], rid_dst[r], *rid_sems[3 * r:3 * r + 3])

        @pl.when(st == 0)
        def _():
            _build_bias(pairs_ref, mask_ref, bias_sc)
            dk_ref[...] = jnp.zeros_like(dk_ref)
            dv_ref[...] = jnp.zeros_like(dv_ref)
            dck_ref[...] = jnp.zeros_like(dck_ref)
            dcv_ref[...] = jnp.zeros_like(dcv_ref)
            dacc_ref[...] = jnp.zeros_like(dacc_ref)

        ck2, cv2 = ck_ref[...], cv_ref[...]
        lane = lax.broadcasted_iota(jnp.int32, (1, 128), 1)
        for sub in range(ATT_SUB):
            rb = ATT_SUB * st + sub
            bias_ref = bias_sc.at[_attn_type(rb, nrb)]
            rs = slice(TQ * sub, TQ * (sub + 1))
            kb = jnp.clip(Q_ROWS * rb - 4, 0, rows - K_ROWS)
            ks = pl.multiple_of(kb * GRID_W, GRID_W)
            ebase = kb - Q_ROWS * rb + 11
            q2 = q_ref[rs, :]
            k2 = k_ref[pl.ds(ks, TK), :]
            v2 = v_ref[pl.ds(ks, TK), :]
            bg = bg_ref[rs, :].astype(F32)
            sig = _sigmoid(bg)
            obv = ob_ref[rs, :].astype(F32)
            dout = do_ref[rs, :].astype(F32)
            dbg_ref[rs, :] = (dout * obv * (sig * (1.0 + bg * (1.0 - sig)))).astype(BF16)
            d_o = dout * (bg * sig)
            d_oo = d_o * obv
            lse2 = lse_ref[rs, :]
            dq_acc = jnp.zeros((TQ, 128), F32)
            for hh in range(2):
                msk = (lane >= HEAD_DIM) == bool(hh)
                qm = jnp.where(msk, q2, jnp.zeros_like(q2))
                lse_h = jnp.max(jnp.where(msk, lse2, -jnp.inf), axis=-1, keepdims=True)
                p = jnp.exp2(_nt(qm, k2) + bias_ref[hh] - lse_h)
                pc = jnp.exp2(_nt(qm, ck2) - lse_h)
                dom_f = jnp.where(msk, d_o, 0.0)
                dom = dom_f.astype(BF16)
                delta = jnp.sum(jnp.where(msk, d_oo, 0.0), axis=-1, keepdims=True)
                d_hi = delta.astype(BF16).astype(F32)
                x0 = HEAD_DIM * (1 - hh)
                dom_aug = jnp.where(lane == x0, -d_hi, jnp.where(lane == x0 + 1, d_hi - delta, dom_f)).astype(BF16)
                extra = jnp.logical_or(lane == x0, lane == x0 + 1)
                va = jnp.where(msk, v2, jnp.where(extra, jnp.ones_like(v2), jnp.zeros_like(v2)))
                cva = jnp.where(msk, cv2, jnp.where(extra, jnp.ones_like(cv2), jnp.zeros_like(cv2)))
                ds = p * _nt(dom_aug, va)
                dsc = pc * _nt(dom_aug, cva)
                dsb, dscb = ds.astype(BF16), dsc.astype(BF16)
                dq_h = (jnp.dot(dsb, k2, preferred_element_type=F32)
                        + jnp.dot(dscb, ck2, preferred_element_type=F32))
                dq_acc = jnp.where(msk, dq_h, dq_acc)
                dk_ref[pl.ds(ks, TK), :] += _tn(dsb, qm)
                dv_ref[pl.ds(ks, TK), :] += _tn(p.astype(BF16), dom)
                dck_ref[...] += _tn(dscb, qm)
                dcv_ref[...] += _tn(pc.astype(BF16), dom)
                for i in range(Q_ROWS):
                    for mm in range(K_ROWS // 2):
                        dacc_ref[hh, ebase + (2 * mm - i)] += ds[GRID_W * i:GRID_W * (i + 1),
                                                                 128 * mm:128 * (mm + 1)]
            dq_ref[rs, :] = dq_acc

        @pl.when(st == n_st - 1)
        def _():
            for hh in range(2):
                drpb_ref[hh] = _rpb_diag_sums(dacc_ref.at[hh], hsel_ref, fold_ref)

        @pl.when(jnp.logical_and(hp == pl.num_programs(0) - 1, st == n_st - 1))
        def _():
            for r, (ex, _) in enumerate(riders):
                ex.wait(rid_src[r], rid_dst[r], *rid_sems[3 * r:3 * r + 3])

    tile = pl.BlockSpec((ATT_SUB * TQ, 128), lambda hp, st: (st, hp))
    colT = pl.BlockSpec((T, 128), lambda hp, st: (0, hp))
    colC = pl.BlockSpec((C, 128), lambda hp, st: (0, hp))
    res = pl.pallas_call(
        body, name="attn_bwd", grid=(4, n_st),
        in_specs=(_attn_specs(T, C) + [tile, tile, pl.BlockSpec((ATT_SUB * TQ, 128), lambda hp, st: (st, 4 + hp)),
                                       pl.BlockSpec((8, 128, 128), lambda hp, st: (0, 0, 0)),
                                       pl.BlockSpec((32, 256), lambda hp, st: (0, 0))]
                  + [ANY_SPEC] * n_rid),
        out_specs=([tile, colT, colT, colC, colC, tile, pl.BlockSpec((2, 32, 128), lambda hp, st: (hp, 0, 0))]
                   + [ANY_SPEC] * n_rid),
        out_shape=([jax.ShapeDtypeStruct((T, 512), F32), jax.ShapeDtypeStruct((T, 512), F32),
                    jax.ShapeDtypeStruct((T, 512), F32), jax.ShapeDtypeStruct((C, 512), F32),
                    jax.ShapeDtypeStruct((C, 512), F32), jax.ShapeDtypeStruct((T, 512), BF16),
                    jax.ShapeDtypeStruct((N_HEADS, 32, 128), F32)] + [ex.out_shape for ex, _ in riders]),
        scratch_shapes=([pltpu.VMEM((3, 2, TQ, TK), F32), pltpu.VMEM((2, N_DIAG, GRID_W, 128), F32)]
                        + [s for ex, _ in riders for s in ex.scratch]),
        compiler_params=_params(("arbitrary", "arbitrary")),
    )(qs, kn, z, ckn, zc, pairs, row_mask, z, ob, lse, dmc, hsel, fold, *[a for _, a in riders])
    return res[:7], res[7:]


def _rpb_diag_sums(a_ref, hsel_ref, fold_ref):
    n_off = 2 * WIN_C - 1
    n_dr = 2 * WIN_R - 1
    qc = lax.broadcasted_iota(jnp.int32, (GRID_W, 128), 0)
    lane = lax.broadcasted_iota(jnp.int32, (GRID_W, 128), 1)
    diff = lane % GRID_W - qc + (WIN_C - 1)
    left = lane < GRID_W

    def by_dr(dr):
        return a_ref[dr + 4] + pltpu.roll(a_ref[dr + 3], GRID_W, 1)

    out = jnp.zeros((32, 128), F32)
    for j in range((n_dr + 1) // 2):
        hi = pltpu.roll(by_dr(2 * j + 1), GRID_W, 1) if 2 * j + 1 < n_dr else 0.0
        pair = jnp.where(left, by_dr(2 * j), hi)
        parts = []
        for o in range(n_off):
            mv = jnp.where(diff == o, pair, 0.0)
            acc = mv[0:8]
            for r8 in range(1, GRID_W // 8):
                acc = acc + mv[8 * r8:8 * (r8 + 1)]
            parts.append(acc)
        parts.append(jnp.zeros((8, 128), F32))
        stack = jnp.concatenate(parts, axis=0)
        s_hi = stack.astype(BF16)
        s_lo = (stack - s_hi.astype(F32)).astype(BF16)
        per_o = (jnp.dot(fold_ref[...], s_hi, preferred_element_type=F32)
                 + jnp.dot(fold_ref[...], s_lo, preferred_element_type=F32))
        out = out + _dot2(per_o, hsel_ref[j])
    return out


def _head_norm_bwd(raw, dn, gain, ones_ref):
    rr = lax.rsqrt(_head_sum(raw * raw, ones_ref) * (1.0 / HEAD_DIM) + EPS)
    hat = raw * rr
    dgain = jnp.sum(dn * hat, axis=0, keepdims=True)
    dhat = dn * gain
    mean = _head_sum(dhat * hat, ones_ref) * (1.0 / HEAD_DIM)
    return rr * (dhat - hat * mean), dgain


def _ctx_k_bwd(zc, dck, dcv, ones_blk, gk, foldm):
    C = dck.shape[0]

    def body(bk_ref, dk_ref, dv_ref, ones_ref, gk_ref, fold_ref, dbk_ref, dbv_ref, dgk_ref):
        dbk, dgk = _head_norm_bwd(bk_ref[...].astype(F32), dk_ref[...] * LN2, gk_ref[...], ones_ref)
        dbk_ref[...] = dbk.astype(BF16)
        dbv_ref[...] = dv_ref[...].astype(BF16)
        dgk_ref[...] = jnp.dot(jnp.broadcast_to(dgk, (8, 512)), fold_ref[...],
                               preferred_element_type=F32, precision=HI)

    row = pl.BlockSpec((C, 512), lambda i: (0, 0))
    cst = lambda a, b: pl.BlockSpec((a, b), lambda i: (0, 0))
    out_row = jax.ShapeDtypeStruct((C, 512), BF16)
    return pl.pallas_call(
        body, name="ctx_k_bwd", grid=(1,),
        in_specs=[row, row, row, cst(512, 512), cst(1, 512), cst(512, 128)],
        out_specs=[row, row, cst(8, 128)],
        out_shape=[out_row, out_row, jax.ShapeDtypeStruct((8, 128), F32)],
        compiler_params=_params(("arbitrary",)),
    )(zc, dck, dcv, ones_blk, gk, foldm)


def _bwd_mid(z, dmc, dqs, dk, dv, db_g, h, hc, dzc_k, dzc_v, sgn, ws, wst, bsb, ones8, ones_blk, gq, gk, foldm, tk,
             riders=()):
    T = z.shape[0]
    nt = T // tk
    n_in, n_out, n_sc = 23, 7, 5
    n_rid = len(riders)

    def body(*refs):
        (au_ref, av_ref, ag_ref, bq_ref, bk_ref, d_ref, dq_ref, dk_ref, dv_ref, dbg_ref, h_ref,
         hc_ref, dzck_ref, dzcv_ref, sg_ref, ws_ref, wst_ref, bsb_ref, ones8_ref, ones_ref, gq_ref, gk_ref,
         fold_ref) = refs[:n_in]
        rid_src = refs[n_in:n_in + n_rid]
        dz_ref, dwt_out, dws_ref, dbs_ref, dsg_ref, dgq_ref, dgk_ref = refs[n_in + n_rid:n_in + n_rid + n_out]
        rid_dst = refs[n_in + n_rid + n_out:n_in + 2 * n_rid + n_out]
        acc, accq, acck, stage, sem = refs[n_in + 2 * n_rid + n_out:n_in + 2 * n_rid + n_out + n_sc]
        rid_sems = refs[n_in + 2 * n_rid + n_out + n_sc:]
        t = pl.program_id(0)

        @pl.when(t == 0)
        def _():
            for r, (ex, _) in enumerate(riders):
                ex.start(rid_src[r], rid_dst[r], *rid_sems[3 * r:3 * r + 3])
            acc[...] = jnp.zeros_like(acc)
            acc[512 * 4:512 * 5, :] = _tn(dzck_ref[...], hc_ref[...])
            acc[512 * 5:512 * 6, :] = _tn(dzcv_ref[...], hc_ref[...])
            dws_ref[...] = jnp.zeros_like(dws_ref)
            dbs_ref[...] = jnp.zeros_like(dbs_ref)
            dsg_ref[...] = jnp.zeros_like(dsg_ref)
            accq[...] = jnp.zeros_like(accq)
            acck[...] = jnp.zeros_like(acck)

        for g in range(SGU_GROUPS):
            ws_bf = ws_ref[g].astype(BF16)
            wst_bf = wst_ref[g].astype(BF16)
            sg = sg_ref[:, 128 * g:128 * (g + 1)]
            bsb_g = bsb_ref[g]
            for j in range(tk // CHUNK):
                rs, cs = slice(CHUNK * j, CHUNK * (j + 1)), slice(128 * g, 128 * (g + 1))
                au, av, ag = (au_ref[rs, cs].astype(F32), av_ref[rs, cs].astype(F32), ag_ref[rs, cs].astype(F32))
                d = d_ref[rs, cs].astype(F32)
                _, (gu, dgu, dgv, rr, vhat, vn, mixed, sig, sl) = _sgu_chunk_fwd(au, av, ag, sg, ws_bf, bsb_g)
                dz_ref[rs, 128 * g:128 * (g + 1)] = (d * mixed * sl * dgu).astype(BF16)
                dz_ref[rs, 1024 + 128 * g:1024 + 128 * (g + 1)] = (
                    d * gu * mixed * (sig * (1.0 + ag * (1.0 - sig)))).astype(BF16)
                dmixed = d * gu * sl
                dmb = dmixed.astype(BF16)
                dm_lo = (dmixed - dmb.astype(F32)).astype(BF16)
                dbs_ref[g] += _nt(ones8_ref[...], dmb) + _nt(ones8_ref[...], dm_lo)
                dws_ref[g] += _nt(dmb, vn.astype(BF16))
                dvn = jnp.dot(wst_bf, dmb, preferred_element_type=F32)
                dsg_ref[:, 128 * g:128 * (g + 1)] += jnp.sum(dvn * vhat, axis=0, keepdims=True)
                dvhat = dvn * sg
                mean = jnp.mean(dvhat * vhat, axis=-1, keepdims=True)
                dz_ref[rs, 512 + 128 * g:512 + 128 * (g + 1)] = (rr * (dvhat - vhat * mean) * dgv).astype(BF16)

        dbq, dgq = _head_norm_bwd(bq_ref[...].astype(F32), dq_ref[...] * ATT_SCALE, gq_ref[...], ones_ref)
        dz_ref[:, 512 * 3:512 * 4] = dbq.astype(BF16)
        accq[...] += dgq
        dbk, dgk = _head_norm_bwd(bk_ref[...].astype(F32), dk_ref[...] * LN2, gk_ref[...], ones_ref)
        dz_ref[:, 512 * 4:512 * 5] = dbk.astype(BF16)
        acck[...] += dgk
        dz_ref[:, 512 * 5:512 * 6] = dv_ref[...].astype(BF16)
        dz_ref[:, 512 * 6:512 * 7] = dbg_ref[...]

        hv = h_ref[...]
        for k in range(N_BRANCH):
            acc[512 * k:512 * (k + 1), :] += _tn(dz_ref[:, 512 * k:512 * (k + 1)], hv)

        @pl.when(t == nt - 1)
        def _():
            dgq_ref[...] = jnp.dot(jnp.broadcast_to(accq[...], (8, 512)), fold_ref[...],
                                   preferred_element_type=F32, precision=HI)
            dgk_ref[...] = jnp.dot(jnp.broadcast_to(acck[...], (8, 512)), fold_ref[...],
                                   preferred_element_type=F32, precision=HI)
            for k in range(N_BRANCH):
                stage[...] = acc[512 * k:512 * (k + 1), :].astype(BF16)
                out = pltpu.make_async_copy(stage, dwt_out.at[pl.ds(512 * k, 512)], sem)
                out.start()
                out.wait()
            for r, (ex, _) in enumerate(riders):
                ex.wait(rid_src[r], rid_dst[r], *rid_sems[3 * r:3 * r + 3])

    zcol = lambda col: pl.BlockSpec((tk, 512), lambda t: (t, col))
    row = pl.BlockSpec((tk, 512), lambda t: (t, 0))
    whole = lambda a: pl.BlockSpec(a.shape, lambda t: (0,) * a.ndim)
    res = pl.pallas_call(
        body, name="bwd_mid", grid=(nt,),
        in_specs=[zcol(0), zcol(1), zcol(2), zcol(3), zcol(4), row, row, row, row, row,
                  pl.BlockSpec((tk, D_MODEL), lambda t: (t, 0)), whole(hc), whole(dzc_k), whole(dzc_v),
                  whole(sgn), whole(ws), whole(wst), whole(bsb), whole(ones8), whole(ones_blk), whole(gq), whole(gk),
                  whole(foldm)] + [ANY_SPEC] * n_rid,
        out_specs=[pl.BlockSpec((tk, D_IN), lambda t: (t, 0)), ANY_SPEC,
                   pl.BlockSpec((SGU_GROUPS, CHUNK, CHUNK), lambda t: (0, 0, 0)),
                   pl.BlockSpec((SGU_GROUPS, 8, CHUNK), lambda t: (0, 0, 0)),
                   pl.BlockSpec((1, 512), lambda t: (0, 0)), pl.BlockSpec((8, 128), lambda t: (0, 0)),
                   pl.BlockSpec((8, 128), lambda t: (0, 0))] + [ANY_SPEC] * n_rid,
        out_shape=[jax.ShapeDtypeStruct((T, D_IN), BF16), jax.ShapeDtypeStruct((D_IN, D_MODEL), BF16),
                   jax.ShapeDtypeStruct((SGU_GROUPS, CHUNK, CHUNK), F32),
                   jax.ShapeDtypeStruct((SGU_GROUPS, 8, CHUNK), F32), jax.ShapeDtypeStruct((1, 512), F32),
                   jax.ShapeDtypeStruct((8, 128), F32), jax.ShapeDtypeStruct((8, 128), F32)]
                  + [ex.out_shape for ex, _ in riders],
        scratch_shapes=[pltpu.VMEM((D_IN, D_MODEL), F32), pltpu.VMEM((1, 512), F32), pltpu.VMEM((1, 512), F32),
                        pltpu.VMEM((512, D_MODEL), BF16), pltpu.SemaphoreType.DMA]
                       + [s for ex, _ in riders for s in ex.scratch],
        compiler_params=_params(("arbitrary",)),
    )(z, z, z, z, z, dmc, dqs, dk, dv, db_g, h, hc, dzc_k, dzc_v, sgn, ws, wst, bsb, ones8, ones_blk, gq, gk, foldm,
      *[a for _, a in riders])
    return res[:n_out], res[n_out:]


def _inproj_bwd_dx(dzs, w_row0, w_in_t, x, dy, ng, scale, shift, tm, name, riders=()):
    T = x.shape[0]
    n = len(dzs)
    wpc = dzs[0].shape[1]
    nt = T // tm
    with_dx = dy is not None
    n_own_in = n + 5 + with_dx
    n_own_out = 3 + with_dx
    n_rid = len(riders)

    def body(*refs):
        dz_refs = refs[:n]
        own = refs[n:n_own_in] + refs[n_own_in + n_rid:n_own_in + n_rid + n_own_out]
        rid_src = refs[n_own_in:n_own_in + n_rid]
        rid_dst = refs[n_own_in + n_rid + n_own_out:n_own_in + 2 * n_rid + n_own_out]
        rid_sems = refs[n_own_in + 2 * n_rid + n_own_out:]
        if with_dx:
            w_ref, x_ref, dy_ref, g_ref, sc_ref, sh_ref, gx_ref, dsh_ref, dsc_ref, dg_ref = own
        else:
            w_ref, x_ref, g_ref, sc_ref, sh_ref, dsh_ref, dsc_ref, dg_ref = own

        @pl.when(pl.program_id(0) == 0)
        def _():
            for r, (ex, _) in enumerate(riders):
                ex.start(rid_src[r], rid_dst[r], *rid_sems[3 * r:3 * r + 3])
            dsh_ref[...] = jnp.zeros_like(dsh_ref)
            dsc_ref[...] = jnp.zeros_like(dsc_ref)
            dg_ref[...] = jnp.zeros_like(dg_ref)

        dh = jnp.dot(dz_refs[0][...], w_ref[0:wpc, :], preferred_element_type=F32)
        for k in range(1, n):
            dh = dh + jnp.dot(dz_refs[k][...], w_ref[wpc * k:wpc * (k + 1), :], preferred_element_type=F32)
        xv = x_ref[...]
        r = lax.rsqrt(jnp.mean(xv * xv, axis=-1, keepdims=True) + EPS)
        xn = xv * r
        gv, op = g_ref[...], 1.0 + sc_ref[...]
        dsh_ref[...] += jnp.sum(dh, axis=0, keepdims=True)
        dsc_ref[...] += jnp.sum(dh * xn * gv, axis=0, keepdims=True)
        dg_ref[...] += jnp.sum(dh * op * xn, axis=0, keepdims=True)
        if with_dx:
            dxn = dh * (gv * op)
            gx_ref[...] = r * (dxn - xn * jnp.mean(dxn * xn, axis=-1, keepdims=True)) + dy_ref[...]

        @pl.when(pl.program_id(0) == nt - 1)
        def _():
            for r, (ex, _) in enumerate(riders):
                ex.wait(rid_src[r], rid_dst[r], *rid_sems[3 * r:3 * r + 3])

    vec = pl.BlockSpec((1, D_MODEL), lambda i: (0, 0))
    rowf = pl.BlockSpec((tm, D_MODEL), lambda i: (i, 0))
    in_specs = [pl.BlockSpec((tm, wpc), lambda i: (i, 0))] * n
    in_specs += [pl.BlockSpec((wpc * n, D_MODEL), lambda i: (w_row0 // (wpc * n), 0)), rowf]
    args = list(dzs) + [w_in_t, x]
    vshape = jax.ShapeDtypeStruct((1, D_MODEL), F32)
    out_specs, out_shape = [vec, vec, vec], [vshape, vshape, vshape]
    if with_dx:
        in_specs.append(rowf)
        args.append(dy)
        out_specs = [rowf] + out_specs
        out_shape = [jax.ShapeDtypeStruct((T, D_MODEL), F32)] + out_shape
    in_specs += [vec, vec, vec] + [ANY_SPEC] * n_rid
    args += [ng, scale, shift] + [a for _, a in riders]
    res = pl.pallas_call(
        body, name=name, grid=(nt,), in_specs=in_specs, out_specs=out_specs + [ANY_SPEC] * n_rid,
        out_shape=out_shape + [ex.out_shape for ex, _ in riders],
        scratch_shapes=[s for ex, _ in riders for s in ex.scratch],
        compiler_params=_params(("arbitrary",)),
    )(*args)
    return res[:n_own_out], res[n_own_out:]


def _adamw_sharded(w, gparts, m, v, tr, name):
    R, C = w.shape
    n_part = gparts.shape[0]

    def body(w_ref, gp_ref, m_ref, v_ref, g_ref, d_ref, m2_ref, v2_ref):
        g = gp_ref[0].astype(F32)
        for d in range(1, n_part):
            g = g + gp_ref[d].astype(F32)
        delta, m2, v2 = _adam(w_ref[...], g, m_ref[...], v_ref[...])
        g_ref[...] = g
        d_ref[...] = delta
        m2_ref[...] = m2
        v2_ref[...] = v2

    row = pl.BlockSpec((tr, C), lambda i: (i, 0))
    sh = jax.ShapeDtypeStruct((R, C), F32)
    return pl.pallas_call(
        body, name=name, grid=(R // tr,),
        in_specs=[row, pl.BlockSpec((n_part, tr, C), lambda i: (0, i, 0)), row, row],
        out_specs=[row, row, row, row], out_shape=[sh, sh, sh, sh],
        compiler_params=_params(("arbitrary",)),
    )(w, gparts, m, v)


def _pack_vectors(vec_rows, dsg, dgq, dgk, dgk_c, loss_part):
    n_vec = len(vec_rows)

    def body(*refs):
        vecs = refs[:n_vec]
        dsg_ref, dgq_ref, dgk_ref, dgkc_ref, loss_ref, v_ref = refs[n_vec:]
        row = lax.broadcasted_iota(jnp.int32, (16, D_MODEL), 0)
        misc = jnp.concatenate([dsg_ref[...], dgq_ref[0:1, :], dgk_ref[0:1, :], dgkc_ref[0:1, :],
                                loss_ref[...]], axis=1)
        v = jnp.where(row == V_MISC, jnp.broadcast_to(misc, (16, D_MODEL)), 0.0)
        for (r, _), ref in zip(vec_rows, vecs):
            v = jnp.where(row == r, jnp.broadcast_to(ref[...], (16, D_MODEL)), v)
        v_ref[...] = v

    return pl.pallas_call(
        body, name="pack_vectors", out_shape=jax.ShapeDtypeStruct((16, D_MODEL), F32), compiler_params=_params(),
    )(*[a for _, a in vec_rows], dsg, dgq, dgk, dgk_c, loss_part)


SMALL_NAMES = ("b_ada", "norm_g", "sgu_norm_g", "w_spatial", "b_spatial", "q_norm_g", "k_norm_g", "rpb")


def _adamw_small(vg, sg, rg, ws, ms, vs):
    k = len(SMALL_NAMES)

    def body(*refs):
        vg_ref, sg_ref, rg_ref = refs[0], refs[1], refs[2]
        refs = refs[1:]
        w_refs = dict(zip(SMALL_NAMES, refs[2:2 + k]))
        m_refs = dict(zip(SMALL_NAMES, refs[2 + k:2 + 2 * k]))
        v_refs = dict(zip(SMALL_NAMES, refs[2 + 2 * k:2 + 3 * k]))
        o_refs = [dict(zip(SMALL_NAMES, refs[2 + (3 + i) * k:2 + (4 + i) * k])) for i in range(4)]
        loss_ref = refs[2 + 7 * k]

        sv = vg_ref[0]
        for d in range(1, N_DEV):
            sv = sv + vg_ref[d]
        loss_ref[...] = sv[V_MISC:V_MISC + 1, 896:1024]

        def total(lo, hi, ref=sg_ref):
            s = ref[0, lo:hi, :].astype(F32)
            for d in range(1, N_DEV):
                s = s + ref[d, lo:hi, :].astype(F32)
            return s

        def emit(name, idx, g):
            res = _adam(w_refs[name][idx], g, m_refs[name][idx], v_refs[name][idx])
            for o, val in zip(o_refs, (g,) + res):
                o[name][idx] = val

        everything = (slice(None), slice(None))
        row = lambda r: sv[r:r + 1, :]
        emit("b_ada", everything, jnp.concatenate(
            [row(V_DSHIFT) + row(V_DCSHIFT), row(V_DSCALE) + row(V_DCSCALE), row(V_DGATE)], axis=1))
        emit("norm_g", everything, row(V_DNG) + row(V_DNG_CTX))
        misc = row(V_MISC)
        emit("sgu_norm_g", everything, misc[:, 0:512])
        emit("q_norm_g", everything, misc[:, 512:512 + HEAD_DIM])
        emit("k_norm_g", everything, misc[:, 640:640 + HEAD_DIM] + misc[:, 768:768 + HEAD_DIM])
        for g in range(SGU_GROUPS):
            emit("w_spatial", (0, g), total(128 * g, 128 * (g + 1)))
            emit("b_spatial", (0, slice(g, g + 1), slice(None)), total(M_DBS + 8 * g, M_DBS + 8 * (g + 1))[0:1, :])
        for hd in range(N_HEADS):
            by_dc = total(32 * hd, 32 * (hd + 1), rg_ref)
            emit("rpb", (0, hd), by_dc.T[0:2 * WIN_R - 1, 0:2 * WIN_C - 1])

    shapes = [jax.ShapeDtypeStruct(w.shape, F32) for w in ws]
    res = pl.pallas_call(body, name="adamw_small", out_shape=shapes * 4 + [jax.ShapeDtypeStruct((1, 128), F32)],
                         compiler_params=_params())(vg, sg, rg, *ws, *ms, *vs)
    return [res[i * k:(i + 1) * k] for i in range(4)], res[4 * k]


def _adamw_cctx(pc_g, w, m, v):
    def body(pc_ref, w_ref, m_ref, v_ref, g_ref, d_ref, m2_ref, v2_ref):
        pc = pc_ref[0, 0:1, :]
        for d in range(1, N_DEV):
            pc = pc + pc_ref[d, 0:1, :]
        cc = w_ref[...]
        sig = _sigmoid(cc)
        g = pc * (sig * (1.0 + cc * (1.0 - sig)))
        delta, m2, v2 = _adam(cc, g, m_ref[...], v_ref[...])
        g_ref[...] = g
        d_ref[...] = delta
        m2_ref[...] = m2
        v2_ref[...] = v2

    sh = jax.ShapeDtypeStruct((1, D_MODEL), F32)
    return pl.pallas_call(body, name="adamw_cctx", out_shape=[sh, sh, sh, sh], compiler_params=_params())(
        pc_g, w, m, v)


def _block_ones(n, blk):
    i = np.arange(n)
    return jnp.asarray((i[:, None] // blk == i[None, :] // blk).astype(np.float32), BF16)


def _rpb_pairs(rpb):
    n_off = 2 * WIN_C - 1
    cols = np.arange(GRID_W)
    c0 = np.clip(cols - WIN_C // 2, 0, GRID_W - WIN_C)
    in_win = (cols[None, :] >= c0[:, None]) & (cols[None, :] < c0[:, None] + WIN_C)
    dc = np.clip(cols[None, :] - cols[:, None] + (WIN_C - 1), 0, n_off - 1)
    expand = (dc[None] == np.arange(n_off)[:, None, None]) & in_win[None]
    toep = jnp.einsum("hrd,dqk->hrqk", rpb, jnp.asarray(expand, F32), precision=HI)
    toep = toep + jnp.asarray(np.where(in_win, 0.0, NEG_INF).astype(np.float32))
    neg = jnp.full((N_HEADS, 1, GRID_W, GRID_W), NEG_INF, F32)
    ext = jnp.concatenate([neg, toep, neg], axis=1)
    return jnp.concatenate([ext[:, :-1], ext[:, 1:]], axis=-1)


def _row_mask(rows):
    nrb = rows // Q_ROWS
    valid = np.zeros((3, Q_ROWS, 1, K_ROWS, 1), bool)
    for t, rb in enumerate((0, 1, nrb - 1)):
        kb = int(np.clip(Q_ROWS * rb - 4, 0, rows - K_ROWS))
        for i in range(Q_ROWS):
            r0 = int(np.clip(Q_ROWS * rb + i - WIN_R // 2, 0, rows - WIN_R))
            for j in range(K_ROWS):
                valid[t, i, 0, j, 0] = r0 <= kb + j < r0 + WIN_R
    full = np.broadcast_to(valid, (3, Q_ROWS, GRID_W, K_ROWS, GRID_W)).reshape(3, TQ, TK)
    return jnp.asarray(np.where(full, 0.0, NEG_INF).astype(np.float32))


def kernel(x, c, ctx, c_ctx, w_ada, b_ada, norm_g, w_in, sgu_norm_g, w_spatial, b_spatial, q_norm_g, k_norm_g, rpb, w_out, loss_target, m_c_ctx, m_w_ada, m_b_ada, m_norm_g, m_w_in, m_sgu_norm_g, m_w_spatial, m_b_spatial, m_q_norm_g, m_k_norm_g, m_rpb, m_w_out, v_c_ctx, v_w_ada, v_b_ada, v_norm_g, v_w_in, v_sgu_norm_g, v_w_spatial, v_b_spatial, v_q_norm_g, v_k_norm_g, v_rpb, v_w_out):
    me = 4 * lax.axis_index("x") + 2 * lax.axis_index("y") + lax.axis_index("c")
    x2, ctx2, tgt2 = x[0], ctx[0], loss_target[0]
    T, C = x2.shape[0], ctx2.shape[0]
    rows = T // GRID_W
    wada, win_t, wout = w_ada[0], w_in[0].T, w_out[0]
    ada_w = wada.shape[1]
    win_w = win_t.shape[0]

    row8 = lax.broadcasted_iota(jnp.int32, (8, D_MODEL), 0)
    c_blk = jnp.where(row8 == me, jnp.broadcast_to(c, (8, D_MODEL)), 0.0)
    b_sh = lax.dynamic_slice(b_ada, (0, me * ada_w), (1, ada_w))
    c_ctx_row = c_ctx.reshape(1, D_MODEL)

    ones512 = _block_ones(512, HEAD_DIM)
    ones8 = jnp.ones((8, 128), BF16)
    foldm = jnp.asarray((np.arange(512)[:, None] % HEAD_DIM == np.arange(128)[None, :]).astype(np.float32))
    lane_half = np.arange(128)[None, :, None] // GRID_W
    hsel = jnp.asarray((2 * np.arange(8)[:, None, None] + lane_half == np.arange(128)[None, None, :]).astype(np.float32),
                       BF16)
    foldr = jnp.asarray((np.arange(256)[None, :] // 8 == np.arange(32)[:, None]).astype(np.float32), BF16)
    gq512 = jnp.tile(q_norm_g, (1, N_HEADS))
    gk512 = jnp.tile(k_norm_g, (1, N_HEADS))
    ws = w_spatial[0]
    wst = ws.transpose(0, 2, 1)
    bsb = jnp.broadcast_to(b_spatial[0][:, :, None], (SGU_GROUPS, CHUNK, 128))
    pairs = _rpb_pairs(rpb[0])
    row_mask = _row_mask(rows)

    my_chip = me // 2
    order = jnp.stack([my_chip, my_chip ^ 2, my_chip ^ 1, my_chip ^ 3]).astype(jnp.int32)
    h, z, win_g, wout_g, s16, part_g = _inproj_fwd(order, x2, norm_g, c_blk, c_ctx_row, wada, b_sh,
                                                   win_t.astype(BF16), wout.astype(BF16), 512)
    w_in_b = win_g.reshape(D_IN, D_MODEL)
    w_out_b = wout_g.reshape(D_MODEL, D_MODEL)
    mod16 = part_g.transpose(1, 0, 2).reshape(16, 3 * D_MODEL)
    mod = lax.dynamic_slice(mod16, (me, 0), (1, 3 * D_MODEL))
    shift, scale, gate = mod[:, :D_MODEL], mod[:, D_MODEL:2 * D_MODEL], mod[:, 2 * D_MODEL:]
    cshift, cscale = mod16[8:9, :D_MODEL], mod16[8:9, D_MODEL:2 * D_MODEL]
    qs, kn = _qk_norm(z, ones512, gq512, gk512, 512)
    hc, zc, ckn = _ctx_fwd(ctx2, norm_g, cscale, cshift, w_in_b, ones512, gk512)
    out_a = _sgu_fwd(z, sgu_norm_g, ws, bsb, 512)
    ob, out_b, lse = _attn_fwd(qs, kn, z, ckn, zc, pairs, row_mask)

    dy, dmc, dw_out, dgate, loss_part = _outproj_loss_bwd(x2, tgt2, out_a, out_b, gate, w_out_b, 512)
    dw_out_blocks = dw_out.reshape(N_DEV, D_MODEL // N_DEV, D_MODEL)
    (dqs, dk, dv, dck, dcv, db_g, drpb), (gout_parts,) = _attn_bwd(
        qs, kn, z, ckn, zc, pairs, row_mask, ob, lse, dmc, hsel, foldr,
        [(_Hosted("a2a", dw_out_blocks), dw_out_blocks)])
    dzc_k, dzc_v, dgk_c = _ctx_k_bwd(zc, dck, dcv, ones512, gk512, foldm)
    rloc = drpb.reshape(N_HEADS * 32, 128)
    (dz, dw_in_t, dws, dbs, dsg, dgq, dgk), (rg,) = _bwd_mid(
        z, dmc, dqs, dk, dv, db_g, h, hc, dzc_k, dzc_v, sgu_norm_g, ws, wst, bsb, ones8, ones512, gq512, gk512, foldm,
        256, [(_Hosted("ag", rloc), rloc)])
    sloc = jnp.concatenate([dws.reshape(SGU_GROUPS * CHUNK, CHUNK), dbs.reshape(SGU_GROUPS * 8, CHUNK)]).astype(BF16)
    (chip_sums,) = _chip_presum([dw_in_t.reshape(N_DEV, win_w, D_MODEL)], "presum_dw_in")
    (grad_x, dshift, dscale, dng), (gin_parts, sg) = _inproj_bwd_dx(
        [dz], 0, w_in_b, x2, dy, norm_g, scale, shift, 512, "inproj_bwd_dx",
        [(_Hosted("chips", chip_sums), chip_sums), (_Hosted("ag", sloc), sloc)])
    (dcshift, dcscale, dng_c), _ = _inproj_bwd_dx([dzc_k, dzc_v], 4 * 512, w_in_b, ctx2, None, norm_g, cscale,
                                                  cshift, C, "ctx_bwd_dx")
    res_in = _adamw_sharded(win_t, gin_parts, m_w_in[0].T, v_w_in[0].T, 112, "adamw_w_in")
    res_out = _adamw_sharded(wout, gout_parts, m_w_out[0], v_w_out[0], 128, "adamw_w_out")

    zero_row = jnp.zeros((1, D_MODEL), F32)
    vec_rows = [(V_DSHIFT, dshift), (V_DSCALE, dscale), (V_DGATE, dgate), (V_DCSHIFT, dcshift),
                (V_DCSCALE, dcscale), (V_ZERO, zero_row), (V_DNG, dng), (V_DNG_CTX, dng_c)]
    vloc = _pack_vectors(vec_rows, dsg, dgq, dgk, dgk_c, loss_part)
    vg = _allgather_direct(vloc, "gather_small")
    small_w = (b_ada, norm_g, sgu_norm_g, w_spatial, b_spatial, q_norm_g, k_norm_g, rpb)
    small_m = (m_b_ada, m_norm_g, m_sgu_norm_g, m_w_spatial, m_b_spatial, m_q_norm_g, m_k_norm_g, m_rpb)
    small_v = (v_b_ada, v_norm_g, v_sgu_norm_g, v_w_spatial, v_b_spatial, v_q_norm_g, v_k_norm_g, v_rpb)
    res_small, loss_row = _adamw_small(vg, sg, rg, small_w, small_m, small_v)

    dm_all = vg[:, V_DSHIFT:V_DGATE + 1, :].reshape(N_DEV, 3 * D_MODEL)
    dc_all = vg[:, V_DCSHIFT:V_ZERO + 1, :].reshape(N_DEV, 3 * D_MODEL)
    dm_sh = lax.dynamic_slice(dm_all, (0, me * ada_w), (N_DEV, ada_w))
    dc_sh = lax.dynamic_slice(dc_all, (0, me * ada_w), (N_DEV, ada_w))
    *res_ada, pc = _ada_bwd(s16, dm_sh, dc_sh, wada, m_w_ada[0], v_w_ada[0])
    pc_g = _allgather_direct(pc, "gather_cctx")
    res_cctx = _adamw_cctx(pc_g, c_ctx_row, m_c_ctx.reshape(1, D_MODEL), v_c_ctx.reshape(1, D_MODEL))

    loss = loss_row[0, 0]
    outs = [loss, grad_x[None]]
    for kind in range(4):
        by_name = dict(zip(SMALL_NAMES, res_small[kind]))
        by_name.update(c_ctx=res_cctx[kind].reshape(D_MODEL), w_ada=res_ada[kind][None],
                       w_in=res_in[kind].T[None], w_out=res_out[kind][None])
        outs += [by_name[nme] for nme in ("c_ctx", "w_ada", "b_ada", "norm_g", "w_in", "sgu_norm_g", "w_spatial",
                                          "b_spatial", "q_norm_g", "k_norm_g", "rpb", "w_out")]
    return tuple(outs)
```

```python
import functools

import numpy as np
import jax
import jax.numpy as jnp
from jax import lax
from jax.experimental import pallas as pl
from jax.experimental.pallas import tpu as pltpu

F32 = jnp.float32
BF16 = jnp.bfloat16
HI = lax.Precision.HIGHEST

N_DEV = 8
D_MODEL = 1024
D_A = 512
D_B = 512
D_IN = 3584
N_BRANCH = 7
HEAD_DIM = 64
N_HEADS = 8
GRID_W = 64
WIN_R = 8
WIN_C = 16
CHUNK = 128
SGU_GROUPS = 4
EPS = 1e-6
NEG_INF = -1e30
Q_ROWS = 4
K_ROWS = 12
TQ = Q_ROWS * GRID_W
TK = K_ROWS * GRID_W
N_DIAG = 22
ATT_SUB = 4
ATT_SCALE = HEAD_DIM ** -0.5
LOG2E = 1.4426950408889634
LN2 = 0.6931471805599453

ADAM_LR = 0.001
ADAM_B1 = 0.9
ADAM_B2 = 0.999
ADAM_EPS = 1e-08
ADAM_WD = 0.01
ADAM_STEP = 10

VMEM_LIMIT = 56 * 1024 * 1024
MESH = pl.DeviceIdType.MESH

V_DSHIFT, V_DSCALE, V_DGATE, V_DCSHIFT, V_DCSCALE, V_ZERO, V_DNG, V_DNG_CTX, V_MISC = range(9)
M_DBS = 512


def _params(sem=None):
    return pltpu.CompilerParams(dimension_semantics=sem, vmem_limit_bytes=VMEM_LIMIT)


def _sigmoid(x):
    return 1.0 / (1.0 + jnp.exp(-x))


def _gelu_parts(x):
    cdf = 0.5 * (1.0 + lax.erf(x * 0.7071067811865476))
    pdf = jnp.exp(-0.5 * x * x) * 0.3989422804014327
    return x * cdf, cdf + x * pdf


def _nt(a, b):
    return lax.dot_general(a, b, (((1,), (1,)), ((), ())), preferred_element_type=F32)


def _tn(a, b):
    return lax.dot_general(a, b, (((0,), (0,)), ((), ())), preferred_element_type=F32)


def _dot2(v, ones_bf):
    hi = v.astype(BF16)
    lo = (v - hi.astype(F32)).astype(BF16)
    return (jnp.dot(hi, ones_bf, preferred_element_type=F32)
            + jnp.dot(lo, ones_bf, preferred_element_type=F32))


def _head_sum(v, ones_ref):
    return jnp.dot(v.astype(BF16), ones_ref[...], preferred_element_type=F32)


def _adam(w, g, m, v):
    m2 = ADAM_B1 * m + (1.0 - ADAM_B1) * g
    v2 = ADAM_B2 * v + (1.0 - ADAM_B2) * (g * g)
    m_hat = m2 / (1.0 - ADAM_B1 ** ADAM_STEP)
    v_hat = v2 / (1.0 - ADAM_B2 ** ADAM_STEP)
    delta = -ADAM_LR * (m_hat / (jnp.sqrt(v_hat) + ADAM_EPS) + ADAM_WD * w)
    return delta, m2, v2


class _Hosted:
    def __init__(self, kind, src):
        self.kind = kind
        n_slot = {"a2a": N_DEV, "ag": N_DEV, "chips": N_DEV // 2}[kind]
        blk = src.shape if kind == "ag" else src.shape[1:]
        self.out_shape = jax.ShapeDtypeStruct((n_slot,) + tuple(blk), src.dtype)
        self.n_peer = n_slot - 1
        self.scratch = [pltpu.SemaphoreType.DMA((self.n_peer,)), pltpu.SemaphoreType.DMA((self.n_peer,)),
                        pltpu.SemaphoreType.DMA]

    def _copies(self, src, dst, send_sems, recv_sems, loc_sem, landing):
        x, y, c = lax.axis_index("x"), lax.axis_index("y"), lax.axis_index("c")
        if self.kind == "chips":
            me = 2 * x + y
            peers = [((px, py, c), 2 * px + py) for px, py in ((1 - x, y), (x, 1 - y), (1 - x, 1 - y))]
        else:
            me = 4 * x + 2 * y + c
            peers = []
            for k in range(1, N_DEV):
                px = 1 - x if (k >> 2) & 1 else x
                py = 1 - y if (k >> 1) & 1 else y
                pc = 1 - c if k & 1 else c
                peers.append(((px, py, pc), 4 * px + 2 * py + pc))
        remote = []
        for k, (peer, pid) in enumerate(peers):
            s = src if self.kind == "ag" else src.at[pid]
            remote.append(pltpu.make_async_remote_copy(
                src_ref=s, dst_ref=dst.at[pid if landing else me],
                send_sem=send_sems.at[k], recv_sem=recv_sems.at[k], device_id=peer, device_id_type=MESH))
        local = pltpu.make_async_copy(src if self.kind == "ag" else src.at[me], dst.at[me], loc_sem)
        return remote, local

    def start(self, src, dst, send_sems, recv_sems, loc_sem):
        remote, local = self._copies(src, dst, send_sems, recv_sems, loc_sem, landing=False)
        for cp in remote:
            cp.start()
        local.start()

    def wait(self, src, dst, send_sems, recv_sems, loc_sem):
        remote, local = self._copies(src, dst, send_sems, recv_sems, loc_sem, landing=True)
        for cp in remote:
            cp.wait_recv()
        for cp in remote:
            cp.wait_send()
        local.wait()


ANY_SPEC = pl.BlockSpec(memory_space=pl.ANY)


def _allgather_direct(arrs, name):
    n = len(arrs)
    exs = [_Hosted("ag", a) for a in arrs]

    def body(*refs):
        srcs, dsts, sems = refs[:n], refs[n:2 * n], refs[2 * n:]
        for r, ex in enumerate(exs):
            ex.start(srcs[r], dsts[r], *sems[3 * r:3 * r + 3])
        for r, ex in enumerate(exs):
            ex.wait(srcs[r], dsts[r], *sems[3 * r:3 * r + 3])

    return pl.pallas_call(body, name=name, out_shape=[ex.out_shape for ex in exs], in_specs=[ANY_SPEC] * n,
                          out_specs=[ANY_SPEC] * n, scratch_shapes=[s for ex in exs for s in ex.scratch])(*arrs)


def _ada_scratch(n_col):
    return ([pltpu.VMEM((N_DEV, 8, D_MODEL), F32), pltpu.VMEM((16, n_col), F32), pltpu.VMEM((N_DEV, 16, n_col), F32)]
            + [pltpu.SemaphoreType.DMA((N_DEV - 1,)) for _ in range(4)])


def _ada_modulation(cb_ref, cc_ref, w_ref, b_ref, cstack, part, parts, s1, r1, s2, r2):
    x, y, c = lax.axis_index("x"), lax.axis_index("y"), lax.axis_index("c")
    me = 4 * x + 2 * y + c
    peers = []
    for k in range(1, N_DEV):
        px = 1 - x if (k >> 2) & 1 else x
        py = 1 - y if (k >> 1) & 1 else y
        pc = 1 - c if k & 1 else c
        peers.append(((px, py, pc), 4 * px + 2 * py + pc))

    def exchange(src, dst, send_sems, recv_sems):
        for k, (peer, _) in enumerate(peers):
            pltpu.make_async_remote_copy(src_ref=src, dst_ref=dst.at[me], send_sem=send_sems.at[k],
                                         recv_sem=recv_sems.at[k], device_id=peer, device_id_type=MESH).start()
        dst[me] = src[...]
        waits = [pltpu.make_async_remote_copy(src_ref=src, dst_ref=dst.at[pid], send_sem=send_sems.at[k],
                                              recv_sem=recv_sems.at[k], device_id=peer, device_id_type=MESH)
                 for k, (peer, pid) in enumerate(peers)]
        for cp in waits:
            cp.wait_recv()
        for cp in waits:
            cp.wait_send()

    exchange(cb_ref, cstack, s1, r1)
    c_all = cstack[0]
    for d in range(1, N_DEV):
        c_all = c_all + cstack[d]
    row = lax.broadcasted_iota(jnp.int32, (8, D_MODEL), 0)
    cc = jnp.where(row == 0, jnp.broadcast_to(cc_ref[...], (8, D_MODEL)), 0.0)
    call = jnp.concatenate([c_all, cc], axis=0)
    s = call * _sigmoid(call)
    part[...] = jnp.dot(s, w_ref[...], preferred_element_type=F32, precision=HI) + b_ref[...]
    exchange(part, parts, s2, r2)
    return s


def _ada_bwd(s16, dm, dc, w, m, v):
    def body(s_ref, dm_ref, dc_ref, w_ref, m_ref, v_ref, g_ref, d_ref, m2_ref, v2_ref, pc_ref):
        dct = jnp.sum(dc_ref[...], axis=0, keepdims=True)
        row = lax.broadcasted_iota(jnp.int32, dc_ref.shape, 0)
        dcb = jnp.where(row == 0, jnp.broadcast_to(dct, dc_ref.shape), 0.0)
        dm16 = jnp.concatenate([dm_ref[...], dcb], axis=0)
        g = lax.dot_general(s_ref[...], dm16, (((0,), (0,)), ((), ())),
                            preferred_element_type=F32, precision=HI)
        w_ = w_ref[...]
        delta, m2, v2 = _adam(w_, g, m_ref[...], v_ref[...])
        g_ref[...] = g
        d_ref[...] = delta
        m2_ref[...] = m2
        v2_ref[...] = v2
        pc_ref[...] = lax.dot_general(dcb, w_, (((1,), (1,)), ((), ())),
                                      preferred_element_type=F32, precision=HI)

    sh = jax.ShapeDtypeStruct(w.shape, F32)
    return pl.pallas_call(
        body, name="ada_bwd",
        out_shape=[sh, sh, sh, sh, jax.ShapeDtypeStruct((8, D_MODEL), F32)],
        compiler_params=_params(),
    )(s16, dm, dc, w, m, v)


def _head_norm(zk, ones_ref, gain):
    ss = _head_sum(zk * zk, ones_ref)
    return zk * lax.rsqrt(ss * (1.0 / HEAD_DIM) + EPS) * gain


def _inproj_fwd(order, x, ng, c_blk, c_ctx_row, w_ada_sh, b_ada_sh, w_blk_t, w_out_blk, tm):
    T = x.shape[0]
    nt = T // tm
    n_pass = N_DEV // 2
    blk_rows = w_blk_t.shape[0]
    n_col = w_ada_sh.shape[1]
    rider = _Hosted("ag", w_out_blk)
    n_ada = len(_ada_scratch(n_col))

    def body(order_ref, x_ref, g_ref, cb_ref, cc_ref, wa_ref, ba_ref, wb_ref, wo_ref,
             h_out, z_ref, wt_out, wo_out, s_out, parts_out,
             hs, wt, modv, send_sems, recv_sems, loc_sem, h_sem, wt_sem, *rest):
        ada_sc, rider_sems = rest[:n_ada], rest[n_ada:]
        p, i = pl.program_id(0), pl.program_id(1)
        x, y, c = lax.axis_index("x"), lax.axis_index("y"), lax.axis_index("c")
        me, sib = (x, y, c), (x, y, 1 - c)
        chips = [(1 - x, y), (x, 1 - y), (1 - x, 1 - y)]

        def slot(px, py, pc):
            return 4 * px + 2 * py + pc

        def copy(k, block, to, src=None):
            return pltpu.make_async_remote_copy(
                src_ref=wt.at[slot(*block)] if src is None else src, dst_ref=wt.at[slot(*block)],
                send_sem=send_sems.at[k], recv_sem=recv_sems.at[k], device_id=to, device_id_type=MESH)

        own = pltpu.make_async_copy(wb_ref, wt.at[slot(*me)], loc_sem)
        h_copy = pltpu.make_async_copy(hs, h_out, h_sem)
        wt_copy = pltpu.make_async_copy(wt, wt_out, wt_sem)
        first = [copy(1 + j, me, (*chip, c), src=wb_ref) for j, chip in enumerate(chips)] + [copy(0, me, sib, src=wb_ref)]
        passed = [copy(4 + j, (*chip, c), sib) for j, chip in enumerate(chips)]

        @pl.when(jnp.logical_and(p == 0, i == 0))
        def _():
            s_out[...] = _ada_modulation(cb_ref, cc_ref, wa_ref, ba_ref, *ada_sc)
            own.start()
            for cp in first:
                cp.start()
            rider.start(wo_ref, wo_out, *rider_sems)
            parts = ada_sc[2]
            parts_out[...] = parts[...]
            my_row = pl.ds(slot(*me), 1)
            mod = jnp.concatenate([parts[d, my_row, :] for d in range(N_DEV)], axis=1)
            modv[0:1, :] = mod[:, 0:D_MODEL]
            modv[1:2, :] = mod[:, D_MODEL:2 * D_MODEL]
            own.wait()
            copy(0, sib, me).wait_recv()

        for j, chip in enumerate(chips):
            @pl.when(jnp.logical_and(p == j + 1, i == 0))
            def _(j=j, chip=chip):
                copy(1 + j, (*chip, c), me).wait_recv()
                passed[j].start()
                copy(4 + j, (*chip, 1 - c), me).wait_recv()
                if j == len(chips) - 1:
                    wt_copy.start()

        rows = pl.ds(pl.multiple_of(i * tm, tm), tm)

        @pl.when(p == 0)
        def _():
            xv = x_ref[...]
            r = lax.rsqrt(jnp.mean(xv * xv, axis=-1, keepdims=True) + EPS)
            hs[rows, :] = ((xv * r * g_ref[...]) * (1.0 + modv[1:2, :]) + modv[0:1, :]).astype(BF16)

        @pl.when(jnp.logical_and(p == 1, i == 0))
        def _():
            h_copy.start()

        w_pair = wt[pl.ds(2 * order_ref[p], 2)].reshape(2 * blk_rows, D_MODEL)
        z_ref[...] = _nt(hs[rows, :], w_pair).astype(BF16)

        @pl.when(jnp.logical_and(p == n_pass - 1, i == nt - 1))
        def _():
            for cp in first + passed:
                cp.wait_send()
            h_copy.wait()
            wt_copy.wait()
            rider.wait(wo_ref, wo_out, *rider_sems)

    whole = lambda shape: pl.BlockSpec(shape, lambda p, i, o: (0,) * len(shape))
    grid_spec = pltpu.PrefetchScalarGridSpec(
        num_scalar_prefetch=1, grid=(n_pass, nt),
        in_specs=[pl.BlockSpec((tm, D_MODEL), lambda p, i, o: (jnp.where(p == 0, i, nt - 1), 0)),
                  whole((1, D_MODEL)), whole((8, D_MODEL)), whole((1, D_MODEL)), whole((D_MODEL, n_col)),
                  whole((1, n_col)), ANY_SPEC, ANY_SPEC],
        out_specs=[ANY_SPEC, pl.BlockSpec((tm, 2 * blk_rows), lambda p, i, o: (i, o[p])), ANY_SPEC, ANY_SPEC,
                   whole((16, D_MODEL)), whole((N_DEV, 16, n_col))],
        scratch_shapes=[pltpu.VMEM((T, D_MODEL), BF16), pltpu.VMEM((N_DEV, blk_rows, D_MODEL), BF16),
                        pltpu.VMEM((8, D_MODEL), F32),
                        pltpu.SemaphoreType.DMA((7,)), pltpu.SemaphoreType.DMA((7,)), pltpu.SemaphoreType.DMA,
                        pltpu.SemaphoreType.DMA, pltpu.SemaphoreType.DMA] + _ada_scratch(n_col) + rider.scratch)
    return pl.pallas_call(
        body, name="inproj_fwd", grid_spec=grid_spec,
        out_shape=[jax.ShapeDtypeStruct((T, D_MODEL), BF16), jax.ShapeDtypeStruct((T, D_IN), BF16),
                   jax.ShapeDtypeStruct((N_DEV, blk_rows, D_MODEL), BF16), rider.out_shape,
                   jax.ShapeDtypeStruct((16, D_MODEL), F32), jax.ShapeDtypeStruct((N_DEV, 16, n_col), F32)],
        compiler_params=_params(("arbitrary", "arbitrary")),
    )(order, x, ng, c_blk, c_ctx_row, w_ada_sh, b_ada_sh, w_blk_t, w_out_blk)


def _qk_norm(z, ones_blk, gq, gk, tm):
    T = z.shape[0]

    def body(q_ref, k_ref, ones_ref, gq_ref, gk_ref, qs_ref, kn_ref):
        qs = _head_norm(q_ref[...].astype(F32), ones_ref, gq_ref[...]) * (ATT_SCALE * LOG2E)
        qs_ref[...] = qs.astype(BF16)
        kn_ref[...] = _head_norm(k_ref[...].astype(F32), ones_ref, gk_ref[...]).astype(BF16)

    v512 = pl.BlockSpec((1, 512), lambda i: (0, 0))
    return pl.pallas_call(
        body, name="qk_norm", grid=(T // tm,),
        in_specs=[pl.BlockSpec((tm, 512), lambda i: (i, 3)), pl.BlockSpec((tm, 512), lambda i: (i, 4)),
                  pl.BlockSpec((512, 512), lambda i: (0, 0)), v512, v512],
        out_specs=[pl.BlockSpec((tm, 512), lambda i: (i, 0)), pl.BlockSpec((tm, 512), lambda i: (i, 0))],
        out_shape=[jax.ShapeDtypeStruct((T, 512), BF16), jax.ShapeDtypeStruct((T, 512), BF16)],
        compiler_params=_params(("arbitrary",)),
    )(z, z, ones_blk, gq, gk)


def _ctx_fwd(ctx, ng, cscale, cshift, w_in_t, ones_blk, gk):
    C = ctx.shape[0]

    def body(x_ref, g_ref, sc_ref, sh_ref, w_ref, ones_ref, gk_ref, h_ref, z_ref, kn_ref):
        xv = x_ref[...]
        r = lax.rsqrt(jnp.mean(xv * xv, axis=-1, keepdims=True) + EPS)
        h = (xv * r * g_ref[...]) * (1.0 + sc_ref[...]) + sh_ref[...]
        hb = h.astype(BF16)
        h_ref[...] = hb
        zk = _nt(hb, w_ref[0:512, :])
        zv = _nt(hb, w_ref[512:1024, :])
        z_ref[:, 0:512] = zk.astype(BF16)
        z_ref[:, 512:1024] = zv.astype(BF16)
        kn_ref[...] = _head_norm(zk, ones_ref, gk_ref[...]).astype(BF16)

    vec = pl.BlockSpec((1, D_MODEL), lambda i: (0, 0))
    return pl.pallas_call(
        body, name="ctx_fwd", grid=(1,),
        in_specs=[pl.BlockSpec((C, D_MODEL), lambda i: (0, 0)), vec, vec, vec,
                  pl.BlockSpec((1024, D_MODEL), lambda i: (2, 0)),
                  pl.BlockSpec((512, 512), lambda i: (0, 0)), pl.BlockSpec((1, 512), lambda i: (0, 0))],
        out_specs=[pl.BlockSpec((C, D_MODEL), lambda i: (0, 0)), pl.BlockSpec((C, 1024), lambda i: (0, 0)),
                   pl.BlockSpec((C, 512), lambda i: (0, 0))],
        out_shape=[jax.ShapeDtypeStruct((C, D_MODEL), BF16), jax.ShapeDtypeStruct((C, 1024), BF16),
                   jax.ShapeDtypeStruct((C, 512), BF16)],
        compiler_params=_params(("arbitrary",)),
    )(ctx, ng, cscale, cshift, w_in_t, ones_blk, gk)


def _sgu_chunk_fwd(au, av, ag, sg, ws_bf, bsb):
    gu, dgu = _gelu_parts(au)
    gv, dgv = _gelu_parts(av)
    rr = lax.rsqrt(jnp.mean(gv * gv, axis=-1, keepdims=True) + EPS)
    vhat = gv * rr
    vn = vhat * sg
    mixed = jnp.dot(ws_bf, vn.astype(BF16), preferred_element_type=F32) + bsb
    sig = _sigmoid(ag)
    sl = ag * sig
    return gu * mixed * sl, (gu, dgu, dgv, rr, vhat, vn, mixed, sig, sl)


def _sgu_fwd(z, sgn, ws, bsb, tm):
    T = z.shape[0]

    def body(au_ref, av_ref, ag_ref, sg_ref, ws_ref, bsb_ref, o_ref):
        for g in range(SGU_GROUPS):
            ws_bf = ws_ref[g].astype(BF16)
            sg = sg_ref[:, 128 * g:128 * (g + 1)]
            bsb_g = bsb_ref[g]
            for j in range(tm // CHUNK):
                rs, cs = slice(CHUNK * j, CHUNK * (j + 1)), slice(128 * g, 128 * (g + 1))
                out, _ = _sgu_chunk_fwd(au_ref[rs, cs].astype(F32), av_ref[rs, cs].astype(F32),
                                        ag_ref[rs, cs].astype(F32), sg, ws_bf, bsb_g)
                o_ref[rs, cs] = out.astype(BF16)

    return pl.pallas_call(
        body, name="sgu_fwd", grid=(T // tm,),
        in_specs=[pl.BlockSpec((tm, 512), lambda i: (i, 0)), pl.BlockSpec((tm, 512), lambda i: (i, 1)),
                  pl.BlockSpec((tm, 512), lambda i: (i, 2)), pl.BlockSpec((1, 512), lambda i: (0, 0)),
                  pl.BlockSpec((SGU_GROUPS, CHUNK, CHUNK), lambda i: (0, 0, 0)),
                  pl.BlockSpec((SGU_GROUPS, CHUNK, 128), lambda i: (0, 0, 0))],
        out_specs=pl.BlockSpec((tm, 512), lambda i: (i, 0)),
        out_shape=jax.ShapeDtypeStruct((T, 512), BF16),
        compiler_params=_params(("arbitrary",)),
    )(z, z, z, sgn, ws, bsb)


def _attn_type(rb, nrb):
    return jnp.where(rb == 0, 0, jnp.where(rb == nrb - 1, 2, 1))


def _attn_specs(T, C):
    return [
        pl.BlockSpec((ATT_SUB * TQ, 128), lambda hp, st: (st, hp)),
        pl.BlockSpec((T, 128), lambda hp, st: (0, hp)),
        pl.BlockSpec((T, 128), lambda hp, st: (0, 20 + hp)),
        pl.BlockSpec((C, 128), lambda hp, st: (0, hp)),
        pl.BlockSpec((C, 128), lambda hp, st: (0, 4 + hp)),
        pl.BlockSpec((2, 2 * WIN_R, GRID_W, 128), lambda hp, st: (hp, 0, 0, 0)),
        pl.BlockSpec((3, TQ, TK), lambda hp, st: (0, 0, 0)),
        pl.BlockSpec((ATT_SUB * TQ, 128), lambda hp, st: (st, 24 + hp)),
    ]


def _build_bias(pairs_ref, mask_ref, bias_sc):
    for t in range(3):
        for hh in range(2):
            for i in range(Q_ROWS):
                for mm in range(K_ROWS // 2):
                    p = min(max(WIN_R - Q_ROWS * t + 2 * mm - i, 0), 2 * WIN_R - 1)
                    rs, cs = slice(GRID_W * i, GRID_W * (i + 1)), slice(128 * mm, 128 * (mm + 1))
                    bias_sc[t, hh, rs, cs] = (pairs_ref[hh, p] + mask_ref[t, rs, cs]) * LOG2E


def _attn_fwd(qs, kn, z, ckn, zc, pairs, row_mask):
    T, C = qs.shape[0], ckn.shape[0]
    rows = T // GRID_W
    nrb = rows // Q_ROWS

    def body(q_ref, k_ref, v_ref, ck_ref, cv_ref, pairs_ref, mask_ref, bg_ref, ob_ref, outb_ref, lse_ref, bias_sc):
        @pl.when(pl.program_id(1) == 0)
        def _():
            _build_bias(pairs_ref, mask_ref, bias_sc)

        ck2, cv2 = ck_ref[...], cv_ref[...]
        lane = lax.broadcasted_iota(jnp.int32, (1, 128), 1)
        for sub in range(ATT_SUB):
            rb = ATT_SUB * pl.program_id(1) + sub
            bias_ref = bias_sc.at[_attn_type(rb, nrb)]
            rs = slice(TQ * sub, TQ * (sub + 1))
            ks = pl.multiple_of(jnp.clip(Q_ROWS * rb - 4, 0, rows - K_ROWS) * GRID_W, GRID_W)
            q2 = q_ref[rs, :]
            k2 = k_ref[pl.ds(ks, TK), :]
            v2 = v_ref[pl.ds(ks, TK), :]
            o_acc = jnp.zeros((TQ, 128), F32)
            lse_acc = jnp.zeros((TQ, 128), F32)
            for hh in range(2):
                msk = (lane >= HEAD_DIM) == bool(hh)
                qm = jnp.where(msk, q2, jnp.zeros_like(q2))
                s = _nt(qm, k2) + bias_ref[hh]
                sc = _nt(qm, ck2)
                m = jnp.maximum(jnp.max(s, axis=-1, keepdims=True), jnp.max(sc, axis=-1, keepdims=True))
                p = jnp.exp2(s - m)
                pc = jnp.exp2(sc - m)
                va = jnp.where(msk, v2, jnp.ones_like(v2))
                cva = jnp.where(msk, cv2, jnp.ones_like(cv2))
                num = (jnp.dot(p.astype(BF16), va, preferred_element_type=F32)
                       + jnp.dot(pc.astype(BF16), cva, preferred_element_type=F32))
                den = pltpu.roll(num, HEAD_DIM, 1)
                o_acc = jnp.where(msk, num / den, o_acc)
                lse_acc = jnp.where(msk, m + jnp.log(den) * LOG2E, lse_acc)
            ob_ref[rs, :] = o_acc.astype(BF16)
            lse_ref[rs, :] = lse_acc
            bg = bg_ref[rs, :].astype(F32)
            outb_ref[rs, :] = (o_acc * (bg * _sigmoid(bg))).astype(BF16)

    tile = pl.BlockSpec((ATT_SUB * TQ, 128), lambda hp, st: (st, hp))
    return pl.pallas_call(
        body, name="attn_fwd", grid=(4, nrb // ATT_SUB),
        in_specs=_attn_specs(T, C),
        out_specs=[tile, tile, tile],
        out_shape=[jax.ShapeDtypeStruct((T, 512), BF16), jax.ShapeDtypeStruct((T, 512), BF16),
                   jax.ShapeDtypeStruct((T, 512), F32)],
        scratch_shapes=[pltpu.VMEM((3, 2, TQ, TK), F32)],
        compiler_params=_params(("arbitrary", "arbitrary")),
    )(qs, kn, z, ckn, zc, pairs, row_mask, z)


def _outproj_loss_bwd(x, tgt, out_a, out_b, gate, w_out, tm):
    T = x.shape[0]
    nt = T // tm

    def body(x_ref, t_ref, oa_ref, ob_ref, gate_ref, w_ref, dy_ref, dmc_ref, dw_ref, dgate_ref, loss_ref, acc):
        @pl.when(pl.program_id(0) == 0)
        def _():
            acc[...] = jnp.zeros_like(acc)
            dgate_ref[...] = jnp.zeros_like(dgate_ref)
            loss_ref[...] = jnp.zeros_like(loss_ref)

        oa, ob = oa_ref[...], ob_ref[...]
        gate_v = gate_ref[...]
        mix = (jnp.dot(oa, w_ref[0:512, :], preferred_element_type=F32)
               + jnp.dot(ob, w_ref[512:1024, :], preferred_element_type=F32))
        e = x_ref[...] + gate_v * mix - t_ref[...]
        se = jnp.sum(jnp.sum(e * e, axis=0, keepdims=True), axis=1, keepdims=True)
        loss_ref[...] += jnp.broadcast_to(se * (0.5 / D_MODEL), loss_ref.shape)
        dy = e * (1.0 / D_MODEL)
        dy_ref[...] = dy
        dgate_ref[...] += jnp.sum(dy * mix, axis=0, keepdims=True)
        dmix = (dy * gate_v).astype(BF16)
        dmc_ref[...] = _nt(dmix, w_ref[...]).astype(BF16)
        acc[0:512, :] += _tn(oa, dmix)
        acc[512:1024, :] += _tn(ob, dmix)

        @pl.when(pl.program_id(0) == nt - 1)
        def _():
            dw_ref[...] = acc[...].astype(BF16)

    row = lambda w: pl.BlockSpec((tm, w), lambda i: (i, 0))
    return pl.pallas_call(
        body, name="outproj_loss_bwd", grid=(nt,),
        in_specs=[row(D_MODEL), row(D_MODEL), row(512), row(512),
                  pl.BlockSpec((1, D_MODEL), lambda i: (0, 0)),
                  pl.BlockSpec((D_MODEL, D_MODEL), lambda i: (0, 0))],
        out_specs=[row(D_MODEL), row(D_MODEL), pl.BlockSpec((D_MODEL, D_MODEL), lambda i: (0, 0)),
                   pl.BlockSpec((1, D_MODEL), lambda i: (0, 0)), pl.BlockSpec((1, 128), lambda i: (0, 0))],
        out_shape=[jax.ShapeDtypeStruct((T, D_MODEL), F32), jax.ShapeDtypeStruct((T, D_MODEL), BF16),
                   jax.ShapeDtypeStruct((D_MODEL, D_MODEL), BF16), jax.ShapeDtypeStruct((1, D_MODEL), F32),
                   jax.ShapeDtypeStruct((1, 128), F32)],
        scratch_shapes=[pltpu.VMEM((D_MODEL, D_MODEL), F32)],
        compiler_params=_params(("arbitrary",)),
    )(x, tgt, out_a, out_b, gate, w_out)


def _attn_bwd(qs, kn, z, ckn, zc, pairs, row_mask, ob, lse, dmc, hsel, fold, riders):
    T, C = qs.shape[0], ckn.shape[0]
    rows = T // GRID_W
    nrb = rows // Q_ROWS
    n_st = nrb // ATT_SUB
    n_rid = len(riders)

    def body(q_ref, k_ref, v_ref, ck_ref, cv_ref, pairs_ref, mask_ref, bg_ref, ob_ref, lse_ref, do_ref,
             hsel_ref, fold_ref, *rest):
        rid_src = rest[:n_rid]
        dq_ref, dk_ref, dv_ref, dck_ref, dcv_ref, dbg_ref, drpb_ref = rest[n_rid:n_rid + 7]
        rid_dst = rest[n_rid + 7:2 * n_rid + 7]
        bias_sc, dacc_ref = rest[2 * n_rid + 7:2 * n_rid + 9]
        rid_sems = rest[2 * n_rid + 9:]
        hp, st = pl.program_id(0), pl.program_id(1)

        @pl.when(jnp.logical_and(hp == 0, st == 0))
        def _():
            for r, (ex, _) in enumerate(riders):
                ex.start(rid_src[r], rid_dst[r], *rid_sems[3 * r:3 * r + 3])

        @pl.when(st == 0)
        def _():
            _build_bias(pairs_ref, mask_ref, bias_sc)
            dk_ref[...] = jnp.zeros_like(dk_ref)
            dv_ref[...] = jnp.zeros_like(dv_ref)
            dck_ref[...] = jnp.zeros_like(dck_ref)
            dcv_ref[...] = jnp.zeros_like(dcv_ref)
            dacc_ref[...] = jnp.zeros_like(dacc_ref)

        ck2, cv2 = ck_ref[...], cv_ref[...]
        lane = lax.broadcasted_iota(jnp.int32, (1, 128), 1)
        for sub in range(ATT_SUB):
            rb = ATT_SUB * st + sub
            bias_ref = bias_sc.at[_attn_type(rb, nrb)]
            rs = slice(TQ * sub, TQ * (sub + 1))
            kb = jnp.clip(Q_ROWS * rb - 4, 0, rows - K_ROWS)
            ks = pl.multiple_of(kb * GRID_W, GRID_W)
            ebase = kb - Q_ROWS * rb + 11
            q2 = q_ref[rs, :]
            k2 = k_ref[pl.ds(ks, TK), :]
            v2 = v_ref[pl.ds(ks, TK), :]
            bg = bg_ref[rs, :].astype(F32)
            sig = _sigmoid(bg)
            obv = ob_ref[rs, :].astype(F32)
            dout = do_ref[rs, :].astype(F32)
            dbg_ref[rs, :] = (dout * obv * (sig * (1.0 + bg * (1.0 - sig)))).astype(BF16)
            d_o = dout * (bg * sig)
            d_oo = d_o * obv
            lse2 = lse_ref[rs, :]
            dq_acc = jnp.zeros((TQ, 128), F32)
            for hh in range(2):
                msk = (lane >= HEAD_DIM) == bool(hh)
                qm = jnp.where(msk, q2, jnp.zeros_like(q2))
                lse_h = jnp.max(jnp.where(msk, lse2, -jnp.inf), axis=-1, keepdims=True)
                p = jnp.exp2(_nt(qm, k2) + bias_ref[hh] - lse_h)
                pc = jnp.exp2(_nt(qm, ck2) - lse_h)
                dom_f = jnp.where(msk, d_o, 0.0)
                dom = dom_f.astype(BF16)
                delta = jnp.sum(jnp.where(msk, d_oo, 0.0), axis=-1, keepdims=True)
                d_hi = delta.astype(BF16).astype(F32)
                x0 = HEAD_DIM * (1 - hh)
                dom_aug = jnp.where(lane == x0, -d_hi, jnp.where(lane == x0 + 1, d_hi - delta, dom_f)).astype(BF16)
                extra = jnp.logical_or(lane == x0, lane == x0 + 1)
                va = jnp.where(msk, v2, jnp.where(extra, jnp.ones_like(v2), jnp.zeros_like(v2)))
                cva = jnp.where(msk, cv2, jnp.where(extra, jnp.ones_like(cv2), jnp.zeros_like(cv2)))
                ds = p * _nt(dom_aug, va)
                dsc = pc * _nt(dom_aug, cva)
                dsb, dscb = ds.astype(BF16), dsc.astype(BF16)
                dq_h = (jnp.dot(dsb, k2, preferred_element_type=F32)
                        + jnp.dot(dscb, ck2, preferred_element_type=F32))
                dq_acc = jnp.where(msk, dq_h, dq_acc)
                dk_ref[pl.ds(ks, TK), :] += _tn(dsb, qm)
                dv_ref[pl.ds(ks, TK), :] += _tn(p.astype(BF16), dom)
                dck_ref[...] += _tn(dscb, qm)
                dcv_ref[...] += _tn(pc.astype(BF16), dom)
                for i in range(Q_ROWS):
                    for mm in range(K_ROWS // 2):
                        dacc_ref[hh, ebase + (2 * mm - i)] += ds[GRID_W * i:GRID_W * (i + 1),
                                                                 128 * mm:128 * (mm + 1)]
            dq_ref[rs, :] = dq_acc

        @pl.when(st == n_st - 1)
        def _():
            for hh in range(2):
                drpb_ref[hh] = _rpb_diag_sums(dacc_ref.at[hh], hsel_ref, fold_ref)

        @pl.when(jnp.logical_and(hp == pl.num_programs(0) - 1, st == n_st - 1))
        def _():
            for r, (ex, _) in enumerate(riders):
                ex.wait(rid_src[r], rid_dst[r], *rid_sems[3 * r:3 * r + 3])

    tile = pl.BlockSpec((ATT_SUB * TQ, 128), lambda hp, st: (st, hp))
    colT = pl.BlockSpec((T, 128), lambda hp, st: (0, hp))
    colC = pl.BlockSpec((C, 128), lambda hp, st: (0, hp))
    res = pl.pallas_call(
        body, name="attn_bwd", grid=(4, n_st),
        in_specs=(_attn_specs(T, C) + [tile, tile, pl.BlockSpec((ATT_SUB * TQ, 128), lambda hp, st: (st, 4 + hp)),
                                       pl.BlockSpec((8, 128, 128), lambda hp, st: (0, 0, 0)),
                                       pl.BlockSpec((32, 256), lambda hp, st: (0, 0))]
                  + [ANY_SPEC] * n_rid),
        out_specs=([tile, colT, colT, colC, colC, tile, pl.BlockSpec((2, 32, 128), lambda hp, st: (hp, 0, 0))]
                   + [ANY_SPEC] * n_rid),
        out_shape=([jax.ShapeDtypeStruct((T, 512), F32), jax.ShapeDtypeStruct((T, 512), F32),
                    jax.ShapeDtypeStruct((T, 512), F32), jax.ShapeDtypeStruct((C, 512), F32),
                    jax.ShapeDtypeStruct((C, 512), F32), jax.ShapeDtypeStruct((T, 512), BF16),
                    jax.ShapeDtypeStruct((N_HEADS, 32, 128), F32)] + [ex.out_shape for ex, _ in riders]),
        scratch_shapes=([pltpu.VMEM((3, 2, TQ, TK), F32), pltpu.VMEM((2, N_DIAG, GRID_W, 128), F32)]
                        + [s for ex, _ in riders for s in ex.scratch]),
        compiler_params=_params(("arbitrary", "arbitrary")),
    )(qs, kn, z, ckn, zc, pairs, row_mask, z, ob, lse, dmc, hsel, fold, *[a for _, a in riders])
    return res[:7], res[7:]


def _rpb_diag_sums(a_ref, hsel_ref, fold_ref):
    n_off = 2 * WIN_C - 1
    n_dr = 2 * WIN_R - 1
    qc = lax.broadcasted_iota(jnp.int32, (GRID_W, 128), 0)
    lane = lax.broadcasted_iota(jnp.int32, (GRID_W, 128), 1)
    diff = lane % GRID_W - qc + (WIN_C - 1)
    left = lane < GRID_W

    def by_dr(dr):
        return a_ref[dr + 4] + pltpu.roll(a_ref[dr + 3], GRID_W, 1)

    out = jnp.zeros((32, 128), F32)
    for j in range((n_dr + 1) // 2):
        hi = pltpu.roll(by_dr(2 * j + 1), GRID_W, 1) if 2 * j + 1 < n_dr else 0.0
        pair = jnp.where(left, by_dr(2 * j), hi)
        parts = []
        for o in range(n_off):
            mv = jnp.where(diff == o, pair, 0.0)
            acc = mv[0:8]
            for r8 in range(1, GRID_W // 8):
                acc = acc + mv[8 * r8:8 * (r8 + 1)]
            parts.append(acc)
        parts.append(jnp.zeros((8, 128), F32))
        stack = jnp.concatenate(parts, axis=0)
        s_hi = stack.astype(BF16)
        s_lo = (stack - s_hi.astype(F32)).astype(BF16)
        per_o = (jnp.dot(fold_ref[...], s_hi, preferred_element_type=F32)
                 + jnp.dot(fold_ref[...], s_lo, preferred_element_type=F32))
        out = out + _dot2(per_o, hsel_ref[j])
    return out


def _head_norm_bwd(raw, dn, gain, ones_ref):
    rr = lax.rsqrt(_head_sum(raw * raw, ones_ref) * (1.0 / HEAD_DIM) + EPS)
    hat = raw * rr
    dgain = jnp.sum(dn * hat, axis=0, keepdims=True)
    dhat = dn * gain
    mean = _head_sum(dhat * hat, ones_ref) * (1.0 / HEAD_DIM)
    return rr * (dhat - hat * mean), dgain


def _ctx_k_bwd(zc, dck, dcv, ones_blk, gk, foldm):
    C = dck.shape[0]

    def body(bk_ref, dk_ref, dv_ref, ones_ref, gk_ref, fold_ref, dbk_ref, dbv_ref, dgk_ref):
        dbk, dgk = _head_norm_bwd(bk_ref[...].astype(F32), dk_ref[...] * LN2, gk_ref[...], ones_ref)
        dbk_ref[...] = dbk.astype(BF16)
        dbv_ref[...] = dv_ref[...].astype(BF16)
        dgk_ref[...] = jnp.dot(jnp.broadcast_to(dgk, (8, 512)), fold_ref[...],
                               preferred_element_type=F32, precision=HI)

    row = pl.BlockSpec((C, 512), lambda i: (0, 0))
    cst = lambda a, b: pl.BlockSpec((a, b), lambda i: (0, 0))
    out_row = jax.ShapeDtypeStruct((C, 512), BF16)
    return pl.pallas_call(
        body, name="ctx_k_bwd", grid=(1,),
        in_specs=[row, row, row, cst(512, 512), cst(1, 512), cst(512, 128)],
        out_specs=[row, row, cst(8, 128)],
        out_shape=[out_row, out_row, jax.ShapeDtypeStruct((8, 128), F32)],
        compiler_params=_params(("arbitrary",)),
    )(zc, dck, dcv, ones_blk, gk, foldm)


def _bwd_mid(z, dmc, dqs, dk, dv, db_g, h, hc, dzc_k, dzc_v, sgn, ws, wst, bsb, ones8, ones_blk, gq, gk, foldm, tk,
             riders=()):
    T = z.shape[0]
    nt = T // tk
    blk = D_IN // N_DEV
    n_in, n_out, n_sc = 23, 7, 9
    n_rid = len(riders)

    def body(*refs):
        (au_ref, av_ref, ag_ref, bq_ref, bk_ref, d_ref, dq_ref, dk_ref, dv_ref, dbg_ref, h_ref,
         hc_ref, dzck_ref, dzcv_ref, sg_ref, ws_ref, wst_ref, bsb_ref, ones8_ref, ones_ref, gq_ref, gk_ref,
         fold_ref) = refs[:n_in]
        rid_src = refs[n_in:n_in + n_rid]
        dz_ref, sums_out, dws_ref, dbs_ref, dsg_ref, dgq_ref, dgk_ref = refs[n_in + n_rid:n_in + n_rid + n_out]
        rid_dst = refs[n_in + n_rid + n_out:n_in + 2 * n_rid + n_out]
        (acc, accq, acck, stage, sem, send_buf, tmp, s1, r1) = refs[n_in + 2 * n_rid + n_out:
                                                                   n_in + 2 * n_rid + n_out + n_sc]
        rid_sems = refs[n_in + 2 * n_rid + n_out + n_sc:]
        t = pl.program_id(0)

        @pl.when(t == 0)
        def _():
            for r, (ex, _) in enumerate(riders):
                ex.start(rid_src[r], rid_dst[r], *rid_sems[3 * r:3 * r + 3])
            acc[...] = jnp.zeros_like(acc)
            acc[512 * 4:512 * 5, :] = _tn(dzck_ref[...], hc_ref[...])
            acc[512 * 5:512 * 6, :] = _tn(dzcv_ref[...], hc_ref[...])
            dws_ref[...] = jnp.zeros_like(dws_ref)
            dbs_ref[...] = jnp.zeros_like(dbs_ref)
            dsg_ref[...] = jnp.zeros_like(dsg_ref)
            accq[...] = jnp.zeros_like(accq)
            acck[...] = jnp.zeros_like(acck)

        for g in range(SGU_GROUPS):
            ws_bf = ws_ref[g].astype(BF16)
            wst_bf = wst_ref[g].astype(BF16)
            sg = sg_ref[:, 128 * g:128 * (g + 1)]
            bsb_g = bsb_ref[g]
            for j in range(tk // CHUNK):
                rs, cs = slice(CHUNK * j, CHUNK * (j + 1)), slice(128 * g, 128 * (g + 1))
                au, av, ag = (au_ref[rs, cs].astype(F32), av_ref[rs, cs].astype(F32), ag_ref[rs, cs].astype(F32))
                d = d_ref[rs, cs].astype(F32)
                _, (gu, dgu, dgv, rr, vhat, vn, mixed, sig, sl) = _sgu_chunk_fwd(au, av, ag, sg, ws_bf, bsb_g)
                dz_ref[rs, 128 * g:128 * (g + 1)] = (d * mixed * sl * dgu).astype(BF16)
                dz_ref[rs, 1024 + 128 * g:1024 + 128 * (g + 1)] = (
                    d * gu * mixed * (sig * (1.0 + ag * (1.0 - sig)))).astype(BF16)
                dmixed = d * gu * sl
                dmb = dmixed.astype(BF16)
                dm_lo = (dmixed - dmb.astype(F32)).astype(BF16)
                dbs_ref[g] += _nt(ones8_ref[...], dmb) + _nt(ones8_ref[...], dm_lo)
                dws_ref[g] += _nt(dmb, vn.astype(BF16))
                dvn = jnp.dot(wst_bf, dmb, preferred_element_type=F32)
                dsg_ref[:, 128 * g:128 * (g + 1)] += jnp.sum(dvn * vhat, axis=0, keepdims=True)
                dvhat = dvn * sg
                mean = jnp.mean(dvhat * vhat, axis=-1, keepdims=True)
                dz_ref[rs, 512 + 128 * g:512 + 128 * (g + 1)] = (rr * (dvhat - vhat * mean) * dgv).astype(BF16)

        dbq, dgq = _head_norm_bwd(bq_ref[...].astype(F32), dq_ref[...] * ATT_SCALE, gq_ref[...], ones_ref)
        dz_ref[:, 512 * 3:512 * 4] = dbq.astype(BF16)
        accq[...] += dgq
        dbk, dgk = _head_norm_bwd(bk_ref[...].astype(F32), dk_ref[...] * LN2, gk_ref[...], ones_ref)
        dz_ref[:, 512 * 4:512 * 5] = dbk.astype(BF16)
        acck[...] += dgk
        dz_ref[:, 512 * 5:512 * 6] = dv_ref[...].astype(BF16)
        dz_ref[:, 512 * 6:512 * 7] = dbg_ref[...]

        hv = h_ref[...]
        for k in range(N_BRANCH):
            acc[512 * k:512 * (k + 1), :] += _tn(dz_ref[:, 512 * k:512 * (k + 1)], hv)

        @pl.when(t == nt - 1)
        def _():
            dgq_ref[...] = jnp.dot(jnp.broadcast_to(accq[...], (8, 512)), fold_ref[...],
                                   preferred_element_type=F32, precision=HI)
            dgk_ref[...] = jnp.dot(jnp.broadcast_to(acck[...], (8, 512)), fold_ref[...],
                                   preferred_element_type=F32, precision=HI)
            cidx = lax.axis_index("c")
            sib = (lax.axis_index("x"), lax.axis_index("y"), 1 - cidx)
            swaps = []
            for q in range(N_DEV // 2):
                theirs = acc[pl.ds(pl.multiple_of(2 * blk * q + blk * (1 - cidx), 8), blk), :]
                send_buf[q] = theirs.astype(BF16)
                cp = pltpu.make_async_remote_copy(src_ref=send_buf.at[q], dst_ref=tmp.at[q], send_sem=s1.at[q],
                                                  recv_sem=r1.at[q], device_id=sib, device_id_type=MESH)
                cp.start()
                swaps.append(cp)
            for q in range(N_DEV // 2):
                swaps[q].wait_recv()
                mine = acc[pl.ds(pl.multiple_of(2 * blk * q + blk * cidx, 8), blk), :]
                stage[...] = (mine + tmp[q].astype(F32)).astype(BF16)
                out = pltpu.make_async_copy(stage, sums_out.at[q], sem)
                out.start()
                out.wait()
            for cp in swaps:
                cp.wait_send()
            for r, (ex, _) in enumerate(riders):
                ex.wait(rid_src[r], rid_dst[r], *rid_sems[3 * r:3 * r + 3])

    zcol = lambda col: pl.BlockSpec((tk, 512), lambda t: (t, col))
    row = pl.BlockSpec((tk, 512), lambda t: (t, 0))
    whole = lambda a: pl.BlockSpec(a.shape, lambda t: (0,) * a.ndim)
    res = pl.pallas_call(
        body, name="bwd_mid", grid=(nt,),
        in_specs=[zcol(0), zcol(1), zcol(2), zcol(3), zcol(4), row, row, row, row, row,
                  pl.BlockSpec((tk, D_MODEL), lambda t: (t, 0)), whole(hc), whole(dzc_k), whole(dzc_v),
                  whole(sgn), whole(ws), whole(wst), whole(bsb), whole(ones8), whole(ones_blk), whole(gq), whole(gk),
                  whole(foldm)] + [ANY_SPEC] * n_rid,
        out_specs=[pl.BlockSpec((tk, D_IN), lambda t: (t, 0)), ANY_SPEC,
                   pl.BlockSpec((SGU_GROUPS, CHUNK, CHUNK), lambda t: (0, 0, 0)),
                   pl.BlockSpec((SGU_GROUPS, 8, CHUNK), lambda t: (0, 0, 0)),
                   pl.BlockSpec((1, 512), lambda t: (0, 0)), pl.BlockSpec((8, 128), lambda t: (0, 0)),
                   pl.BlockSpec((8, 128), lambda t: (0, 0))] + [ANY_SPEC] * n_rid,
        out_shape=[jax.ShapeDtypeStruct((T, D_IN), BF16), jax.ShapeDtypeStruct((N_DEV // 2, blk, D_MODEL), BF16),
                   jax.ShapeDtypeStruct((SGU_GROUPS, CHUNK, CHUNK), F32),
                   jax.ShapeDtypeStruct((SGU_GROUPS, 8, CHUNK), F32), jax.ShapeDtypeStruct((1, 512), F32),
                   jax.ShapeDtypeStruct((8, 128), F32), jax.ShapeDtypeStruct((8, 128), F32)]
                  + [ex.out_shape for ex, _ in riders],
        scratch_shapes=[pltpu.VMEM((D_IN, D_MODEL), F32), pltpu.VMEM((1, 512), F32), pltpu.VMEM((1, 512), F32),
                        pltpu.VMEM((blk, D_MODEL), BF16), pltpu.SemaphoreType.DMA,
                        pltpu.VMEM((N_DEV // 2, blk, D_MODEL), BF16), pltpu.VMEM((N_DEV // 2, blk, D_MODEL), BF16),
                        pltpu.SemaphoreType.DMA((N_DEV // 2,)), pltpu.SemaphoreType.DMA((N_DEV // 2,))]
                       + [s for ex, _ in riders for s in ex.scratch],
        compiler_params=_params(("arbitrary",)),
    )(z, z, z, z, z, dmc, dqs, dk, dv, db_g, h, hc, dzc_k, dzc_v, sgn, ws, wst, bsb, ones8, ones_blk, gq, gk, foldm,
      *[a for _, a in riders])
    return res[:n_out], res[n_out:]


def _inproj_bwd_dx(dzs, w_row0, w_in_t, x, dy, ng, scale, shift, tm, name, riders=()):
    T = x.shape[0]
    n = len(dzs)
    wpc = dzs[0].shape[1]
    nt = T // tm
    with_dx = dy is not None
    n_own_in = n + 5 + with_dx
    n_own_out = 3 + with_dx
    n_rid = len(riders)

    def body(*refs):
        dz_refs = refs[:n]
        own = refs[n:n_own_in] + refs[n_own_in + n_rid:n_own_in + n_rid + n_own_out]
        rid_src = refs[n_own_in:n_own_in + n_rid]
        rid_dst = refs[n_own_in + n_rid + n_own_out:n_own_in + 2 * n_rid + n_own_out]
        rid_sems = refs[n_own_in + 2 * n_rid + n_own_out:]
        if with_dx:
            w_ref, x_ref, dy_ref, g_ref, sc_ref, sh_ref, gx_ref, dsh_ref, dsc_ref, dg_ref = own
        else:
            w_ref, x_ref, g_ref, sc_ref, sh_ref, dsh_ref, dsc_ref, dg_ref = own

        @pl.when(pl.program_id(0) == 0)
        def _():
            for r, (ex, _) in enumerate(riders):
                ex.start(rid_src[r], rid_dst[r], *rid_sems[3 * r:3 * r + 3])
            dsh_ref[...] = jnp.zeros_like(dsh_ref)
            dsc_ref[...] = jnp.zeros_like(dsc_ref)
            dg_ref[...] = jnp.zeros_like(dg_ref)

        dh = jnp.dot(dz_refs[0][...], w_ref[0:wpc, :], preferred_element_type=F32)
        for k in range(1, n):
            dh = dh + jnp.dot(dz_refs[k][...], w_ref[wpc * k:wpc * (k + 1), :], preferred_element_type=F32)
        xv = x_ref[...]
        r = lax.rsqrt(jnp.mean(xv * xv, axis=-1, keepdims=True) + EPS)
        xn = xv * r
        gv, op = g_ref[...], 1.0 + sc_ref[...]
        dsh_ref[...] += jnp.sum(dh, axis=0, keepdims=True)
        dsc_ref[...] += jnp.sum(dh * xn * gv, axis=0, keepdims=True)
        dg_ref[...] += jnp.sum(dh * op * xn, axis=0, keepdims=True)
        if with_dx:
            dxn = dh * (gv * op)
            gx_ref[...] = r * (dxn - xn * jnp.mean(dxn * xn, axis=-1, keepdims=True)) + dy_ref[...]

        @pl.when(pl.program_id(0) == nt - 1)
        def _():
            for r, (ex, _) in enumerate(riders):
                ex.wait(rid_src[r], rid_dst[r], *rid_sems[3 * r:3 * r + 3])

    vec = pl.BlockSpec((1, D_MODEL), lambda i: (0, 0))
    rowf = pl.BlockSpec((tm, D_MODEL), lambda i: (i, 0))
    in_specs = [pl.BlockSpec((tm, wpc), lambda i: (i, 0))] * n
    in_specs += [pl.BlockSpec((wpc * n, D_MODEL), lambda i: (w_row0 // (wpc * n), 0)), rowf]
    args = list(dzs) + [w_in_t, x]
    vshape = jax.ShapeDtypeStruct((1, D_MODEL), F32)
    out_specs, out_shape = [vec, vec, vec], [vshape, vshape, vshape]
    if with_dx:
        in_specs.append(rowf)
        args.append(dy)
        out_specs = [rowf] + out_specs
        out_shape = [jax.ShapeDtypeStruct((T, D_MODEL), F32)] + out_shape
    in_specs += [vec, vec, vec] + [ANY_SPEC] * n_rid
    args += [ng, scale, shift] + [a for _, a in riders]
    res = pl.pallas_call(
        body, name=name, grid=(nt,), in_specs=in_specs, out_specs=out_specs + [ANY_SPEC] * n_rid,
        out_shape=out_shape + [ex.out_shape for ex, _ in riders],
        scratch_shapes=[s for ex, _ in riders for s in ex.scratch],
        compiler_params=_params(("arbitrary",)),
    )(*args)
    return res[:n_own_out], res[n_own_out:]


def _adamw_sharded(w, gparts, m, v, tr, name):
    R, C = w.shape
    n_part = gparts.shape[0]

    def body(w_ref, gp_ref, m_ref, v_ref, g_ref, d_ref, m2_ref, v2_ref):
        g = gp_ref[0].astype(F32)
        for d in range(1, n_part):
            g = g + gp_ref[d].astype(F32)
        delta, m2, v2 = _adam(w_ref[...], g, m_ref[...], v_ref[...])
        g_ref[...] = g
        d_ref[...] = delta
        m2_ref[...] = m2
        v2_ref[...] = v2

    row = pl.BlockSpec((tr, C), lambda i: (i, 0))
    sh = jax.ShapeDtypeStruct((R, C), F32)
    return pl.pallas_call(
        body, name=name, grid=(R // tr,),
        in_specs=[row, pl.BlockSpec((n_part, tr, C), lambda i: (0, i, 0)), row, row],
        out_specs=[row, row, row, row], out_shape=[sh, sh, sh, sh],
        compiler_params=_params(("arbitrary",)),
    )(w, gparts, m, v)


def _pack_vectors(vec_rows, dsg, dgq, dgk, dgk_c, loss_part):
    n_vec = len(vec_rows)

    def body(*refs):
        vecs = refs[:n_vec]
        dsg_ref, dgq_ref, dgk_ref, dgkc_ref, loss_ref, v_ref = refs[n_vec:]
        row = lax.broadcasted_iota(jnp.int32, (16, D_MODEL), 0)
        misc = jnp.concatenate([dsg_ref[...], dgq_ref[0:1, :], dgk_ref[0:1, :], dgkc_ref[0:1, :],
                                loss_ref[...]], axis=1)
        v = jnp.where(row == V_MISC, jnp.broadcast_to(misc, (16, D_MODEL)), 0.0)
        for (r, _), ref in zip(vec_rows, vecs):
            v = jnp.where(row == r, jnp.broadcast_to(ref[...], (16, D_MODEL)), v)
        v_ref[...] = v

    return pl.pallas_call(
        body, name="pack_vectors", out_shape=jax.ShapeDtypeStruct((16, D_MODEL), F32), compiler_params=_params(),
    )(*[a for _, a in vec_rows], dsg, dgq, dgk, dgk_c, loss_part)


SMALL_NAMES = ("b_ada", "norm_g", "sgu_norm_g", "w_spatial", "b_spatial", "q_norm_g", "k_norm_g", "rpb")


def _adamw_small(vg, sg, rg, ws, ms, vs):
    k = len(SMALL_NAMES)

    def body(*refs):
        vg_ref, sg_ref, rg_ref = refs[0], refs[1], refs[2]
        refs = refs[1:]
        w_refs = dict(zip(SMALL_NAMES, refs[2:2 + k]))
        m_refs = dict(zip(SMALL_NAMES, refs[2 + k:2 + 2 * k]))
        v_refs = dict(zip(SMALL_NAMES, refs[2 + 2 * k:2 + 3 * k]))
        o_refs = [dict(zip(SMALL_NAMES, refs[2 + (3 + i) * k:2 + (4 + i) * k])) for i in range(4)]
        loss_ref = refs[2 + 7 * k]

        sv = vg_ref[0]
        for d in range(1, N_DEV):
            sv = sv + vg_ref[d]
        loss_ref[...] = sv[V_MISC:V_MISC + 1, 896:1024]

        def total(lo, hi, ref=sg_ref):
            s = ref[0, lo:hi, :].astype(F32)
            for d in range(1, N_DEV):
                s = s + ref[d, lo:hi, :].astype(F32)
            return s

        def emit(name, idx, g):
            res = _adam(w_refs[name][idx], g, m_refs[name][idx], v_refs[name][idx])
            for o, val in zip(o_refs, (g,) + res):
                o[name][idx] = val

        everything = (slice(None), slice(None))
        row = lambda r: sv[r:r + 1, :]
        emit("b_ada", everything, jnp.concatenate(
            [row(V_DSHIFT) + row(V_DCSHIFT), row(V_DSCALE) + row(V_DCSCALE), row(V_DGATE)], axis=1))
        emit("norm_g", everything, row(V_DNG) + row(V_DNG_CTX))
        misc = row(V_MISC)
        emit("sgu_norm_g", everything, misc[:, 0:512])
        emit("q_norm_g", everything, misc[:, 512:512 + HEAD_DIM])
        emit("k_norm_g", everything, misc[:, 640:640 + HEAD_DIM] + misc[:, 768:768 + HEAD_DIM])
        for g in range(SGU_GROUPS):
            emit("w_spatial", (0, g), total(128 * g, 128 * (g + 1)))
            emit("b_spatial", (0, slice(g, g + 1), slice(None)), total(M_DBS + 8 * g, M_DBS + 8 * (g + 1))[0:1, :])
        for hd in range(N_HEADS):
            by_dc = total(32 * hd, 32 * (hd + 1), rg_ref)
            emit("rpb", (0, hd), by_dc.T[0:2 * WIN_R - 1, 0:2 * WIN_C - 1])

    shapes = [jax.ShapeDtypeStruct(w.shape, F32) for w in ws]
    res = pl.pallas_call(body, name="adamw_small", out_shape=shapes * 4 + [jax.ShapeDtypeStruct((1, 128), F32)],
                         compiler_params=_params())(vg, sg, rg, *ws, *ms, *vs)
    return [res[i * k:(i + 1) * k] for i in range(4)], res[4 * k]


def _adamw_cctx(pc_g, w, m, v):
    def body(pc_ref, w_ref, m_ref, v_ref, g_ref, d_ref, m2_ref, v2_ref):
        pc = pc_ref[0, 0:1, :]
        for d in range(1, N_DEV):
            pc = pc + pc_ref[d, 0:1, :]
        cc = w_ref[...]
        sig = _sigmoid(cc)
        g = pc * (sig * (1.0 + cc * (1.0 - sig)))
        delta, m2, v2 = _adam(cc, g, m_ref[...], v_ref[...])
        g_ref[...] = g
        d_ref[...] = delta
        m2_ref[...] = m2
        v2_ref[...] = v2

    sh = jax.ShapeDtypeStruct((1, D_MODEL), F32)
    return pl.pallas_call(body, name="adamw_cctx", out_shape=[sh, sh, sh, sh], compiler_params=_params())(
        pc_g, w, m, v)


def _block_ones(n, blk):
    i = np.arange(n)
    return jnp.asarray((i[:, None] // blk == i[None, :] // blk).astype(np.float32), BF16)


def _rpb_pairs(rpb):
    n_off = 2 * WIN_C - 1
    cols = np.arange(GRID_W)
    c0 = np.clip(cols - WIN_C // 2, 0, GRID_W - WIN_C)
    in_win = (cols[None, :] >= c0[:, None]) & (cols[None, :] < c0[:, None] + WIN_C)
    dc = np.clip(cols[None, :] - cols[:, None] + (WIN_C - 1), 0, n_off - 1)
    expand = (dc[None] == np.arange(n_off)[:, None, None]) & in_win[None]
    toep = jnp.einsum("hrd,dqk->hrqk", rpb, jnp.asarray(expand, F32), precision=HI)
    toep = toep + jnp.asarray(np.where(in_win, 0.0, NEG_INF).astype(np.float32))
    neg = jnp.full((N_HEADS, 1, GRID_W, GRID_W), NEG_INF, F32)
    ext = jnp.concatenate([neg, toep, neg], axis=1)
    return jnp.concatenate([ext[:, :-1], ext[:, 1:]], axis=-1)


def _row_mask(rows):
    nrb = rows // Q_ROWS
    valid = np.zeros((3, Q_ROWS, 1, K_ROWS, 1), bool)
    for t, rb in enumerate((0, 1, nrb - 1)):
        kb = int(np.clip(Q_ROWS * rb - 4, 0, rows - K_ROWS))
        for i in range(Q_ROWS):
            r0 = int(np.clip(Q_ROWS * rb + i - WIN_R // 2, 0, rows - WIN_R))
            for j in range(K_ROWS):
                valid[t, i, 0, j, 0] = r0 <= kb + j < r0 + WIN_R
    full = np.broadcast_to(valid, (3, Q_ROWS, GRID_W, K_ROWS, GRID_W)).reshape(3, TQ, TK)
    return jnp.asarray(np.where(full, 0.0, NEG_INF).astype(np.float32))


def kernel(x, c, ctx, c_ctx, w_ada, b_ada, norm_g, w_in, sgu_norm_g, w_spatial, b_spatial, q_norm_g, k_norm_g, rpb, w_out, loss_target, m_c_ctx, m_w_ada, m_b_ada, m_norm_g, m_w_in, m_sgu_norm_g, m_w_spatial, m_b_spatial, m_q_norm_g, m_k_norm_g, m_rpb, m_w_out, v_c_ctx, v_w_ada, v_b_ada, v_norm_g, v_w_in, v_sgu_norm_g, v_w_spatial, v_b_spatial, v_q_norm_g, v_k_norm_g, v_rpb, v_w_out):
    me = 4 * lax.axis_index("x") + 2 * lax.axis_index("y") + lax.axis_index("c")
    x2, ctx2, tgt2 = x[0], ctx[0], loss_target[0]
    T, C = x2.shape[0], ctx2.shape[0]
    rows = T // GRID_W
    wada, win_t, wout = w_ada[0], w_in[0].T, w_out[0]
    ada_w = wada.shape[1]
    win_w = win_t.shape[0]

    row8 = lax.broadcasted_iota(jnp.int32, (8, D_MODEL), 0)
    c_blk = jnp.where(row8 == me, jnp.broadcast_to(c, (8, D_MODEL)), 0.0)
    b_sh = lax.dynamic_slice(b_ada, (0, me * ada_w), (1, ada_w))
    c_ctx_row = c_ctx.reshape(1, D_MODEL)

    ones512 = _block_ones(512, HEAD_DIM)
    ones8 = jnp.ones((8, 128), BF16)
    foldm = jnp.asarray((np.arange(512)[:, None] % HEAD_DIM == np.arange(128)[None, :]).astype(np.float32))
    lane_half = np.arange(128)[None, :, None] // GRID_W
    hsel = jnp.asarray((2 * np.arange(8)[:, None, None] + lane_half == np.arange(128)[None, None, :]).astype(np.float32),
                       BF16)
    foldr = jnp.asarray((np.arange(256)[None, :] // 8 == np.arange(32)[:, None]).astype(np.float32), BF16)
    gq512 = jnp.tile(q_norm_g, (1, N_HEADS))
    gk512 = jnp.tile(k_norm_g, (1, N_HEADS))
    ws = w_spatial[0]
    wst = ws.transpose(0, 2, 1)
    bsb = jnp.broadcast_to(b_spatial[0][:, :, None], (SGU_GROUPS, CHUNK, 128))
    pairs = _rpb_pairs(rpb[0])
    row_mask = _row_mask(rows)

    my_chip = me // 2
    order = jnp.stack([my_chip, my_chip ^ 2, my_chip ^ 1, my_chip ^ 3]).astype(jnp.int32)
    h, z, win_g, wout_g, s16, part_g = _inproj_fwd(order, x2, norm_g, c_blk, c_ctx_row, wada, b_sh,
                                                   win_t.astype(BF16), wout.astype(BF16), 512)
    w_in_b = win_g.reshape(D_IN, D_MODEL)
    w_out_b = wout_g.reshape(D_MODEL, D_MODEL)
    mod16 = part_g.transpose(1, 0, 2).reshape(16, 3 * D_MODEL)
    mod = lax.dynamic_slice(mod16, (me, 0), (1, 3 * D_MODEL))
    shift, scale, gate = mod[:, :D_MODEL], mod[:, D_MODEL:2 * D_MODEL], mod[:, 2 * D_MODEL:]
    cshift, cscale = mod16[8:9, :D_MODEL], mod16[8:9, D_MODEL:2 * D_MODEL]
    qs, kn = _qk_norm(z, ones512, gq512, gk512, 512)
    hc, zc, ckn = _ctx_fwd(ctx2, norm_g, cscale, cshift, w_in_b, ones512, gk512)
    out_a = _sgu_fwd(z, sgu_norm_g, ws, bsb, 512)
    ob, out_b, lse = _attn_fwd(qs, kn, z, ckn, zc, pairs, row_mask)

    dy, dmc, dw_out, dgate, loss_part = _outproj_loss_bwd(x2, tgt2, out_a, out_b, gate, w_out_b, 512)
    dw_out_blocks = dw_out.reshape(N_DEV, D_MODEL // N_DEV, D_MODEL)
    (dqs, dk, dv, dck, dcv, db_g, drpb), (gout_parts,) = _attn_bwd(
        qs, kn, z, ckn, zc, pairs, row_mask, ob, lse, dmc, hsel, foldr,
        [(_Hosted("a2a", dw_out_blocks), dw_out_blocks)])
    dzc_k, dzc_v, dgk_c = _ctx_k_bwd(zc, dck, dcv, ones512, gk512, foldm)
    rloc = drpb.reshape(N_HEADS * 32, 128)
    (dz, chip_sums, dws, dbs, dsg, dgq, dgk), (rg,) = _bwd_mid(
        z, dmc, dqs, dk, dv, db_g, h, hc, dzc_k, dzc_v, sgu_norm_g, ws, wst, bsb, ones8, ones512, gq512, gk512, foldm,
        256, [(_Hosted("ag", rloc), rloc)])
    sloc = jnp.concatenate([dws.reshape(SGU_GROUPS * CHUNK, CHUNK), dbs.reshape(SGU_GROUPS * 8, CHUNK)]).astype(BF16)
    (grad_x, dshift, dscale, dng), (gin_parts,) = _inproj_bwd_dx(
        [dz], 0, w_in_b, x2, dy, norm_g, scale, shift, 512, "inproj_bwd_dx",
        [(_Hosted("chips", chip_sums), chip_sums)])
    (dcshift, dcscale, dng_c), _ = _inproj_bwd_dx([dzc_k, dzc_v], 4 * 512, w_in_b, ctx2, None, norm_g, cscale,
                                                  cshift, C, "ctx_bwd_dx")
    res_in = _adamw_sharded(win_t, gin_parts, m_w_in[0].T, v_w_in[0].T, 112, "adamw_w_in")
    res_out = _adamw_sharded(wout, gout_parts, m_w_out[0], v_w_out[0], 128, "adamw_w_out")

    zero_row = jnp.zeros((1, D_MODEL), F32)
    vec_rows = [(V_DSHIFT, dshift), (V_DSCALE, dscale), (V_DGATE, dgate), (V_DCSHIFT, dcshift),
                (V_DCSCALE, dcscale), (V_ZERO, zero_row), (V_DNG, dng), (V_DNG_CTX, dng_c)]
    vloc = _pack_vectors(vec_rows, dsg, dgq, dgk, dgk_c, loss_part)
    vg, sg = _allgather_direct([vloc, sloc], "gather_small")
    small_w = (b_ada, norm_g, sgu_norm_g, w_spatial, b_spatial, q_norm_g, k_norm_g, rpb)
    small_m = (m_b_ada, m_norm_g, m_sgu_norm_g, m_w_spatial, m_b_spatial, m_q_norm_g, m_k_norm_g, m_rpb)
    small_v = (v_b_ada, v_norm_g, v_sgu_norm_g, v_w_spatial, v_b_spatial, v_q_norm_g, v_k_norm_g, v_rpb)
    res_small, loss_row = _adamw_small(vg, sg, rg, small_w, small_m, small_v)

    dm_all = vg[:, V_DSHIFT:V_DGATE + 1, :].reshape(N_DEV, 3 * D_MODEL)
    dc_all = vg[:, V_DCSHIFT:V_ZERO + 1, :].reshape(N_DEV, 3 * D_MODEL)
    dm_sh = lax.dynamic_slice(dm_all, (0, me * ada_w), (N_DEV, ada_w))
    dc_sh = lax.dynamic_slice(dc_all, (0, me * ada_w), (N_DEV, ada_w))
    *res_ada, pc = _ada_bwd(s16, dm_sh, dc_sh, wada, m_w_ada[0], v_w_ada[0])
    (pc_g,) = _allgather_direct([pc], "gather_cctx")
    res_cctx = _adamw_cctx(pc_g, c_ctx_row, m_c_ctx.reshape(1, D_MODEL), v_c_ctx.reshape(1, D_MODEL))

    loss = loss_row[0, 0]
    outs = [loss, grad_x[None]]
    for kind in range(4):
        by_name = dict(zip(SMALL_NAMES, res_small[kind]))
        by_name.update(c_ctx=res_cctx[kind].reshape(D_MODEL), w_ada=res_ada[kind][None],
                       w_in=res_in[kind].T[None], w_out=res_out[kind][None])
        outs += [by_name[nme] for nme in ("c_ctx", "w_ada", "b_ada", "norm_g", "w_in", "sgu_norm_g", "w_spatial",
                                          "b_spatial", "q_norm_g", "k_norm_g", "rpb", "w_out")]
    return tuple(outs)
```

```python
import functools

import numpy as np
import jax
import jax.numpy as jnp
from jax import lax
from jax.experimental import pallas as pl
from jax.experimental.pallas import tpu as pltpu

F32 = jnp.float32
BF16 = jnp.bfloat16
HI = lax.Precision.HIGHEST

N_DEV = 8
D_MODEL = 1024
D_A = 512
D_B = 512
D_IN = 3584
N_BRANCH = 7
HEAD_DIM = 64
N_HEADS = 8
GRID_W = 64
WIN_R = 8
WIN_C = 16
CHUNK = 128
SGU_GROUPS = 4
EPS = 1e-6
NEG_INF = -1e30
Q_ROWS = 4
K_ROWS = 12
TQ = Q_ROWS * GRID_W
TK = K_ROWS * GRID_W
N_DIAG = 22
ATT_SUB = 4
ATT_SCALE = HEAD_DIM ** -0.5
LOG2E = 1.4426950408889634
LN2 = 0.6931471805599453

ADAM_LR = 0.001
ADAM_B1 = 0.9
ADAM_B2 = 0.999
ADAM_EPS = 1e-08
ADAM_WD = 0.01
ADAM_STEP = 10

VMEM_LIMIT = 56 * 1024 * 1024
MESH = pl.DeviceIdType.MESH

V_DSHIFT, V_DSCALE, V_DGATE, V_DCSHIFT, V_DCSCALE, V_ZERO, V_DNG, V_DNG_CTX, V_MISC = range(9)
M_DBS = 512


def _params(sem=None):
    return pltpu.CompilerParams(dimension_semantics=sem, vmem_limit_bytes=VMEM_LIMIT)


def _sigmoid(x):
    return 1.0 / (1.0 + jnp.exp(-x))


def _gelu_parts(x):
    cdf = 0.5 * (1.0 + lax.erf(x * 0.7071067811865476))
    pdf = jnp.exp(-0.5 * x * x) * 0.3989422804014327
    return x * cdf, cdf + x * pdf


def _nt(a, b):
    return lax.dot_general(a, b, (((1,), (1,)), ((), ())), preferred_element_type=F32)


def _tn(a, b):
    return lax.dot_general(a, b, (((0,), (0,)), ((), ())), preferred_element_type=F32)


def _dot2(v, ones_bf):
    hi = v.astype(BF16)
    lo = (v - hi.astype(F32)).astype(BF16)
    return (jnp.dot(hi, ones_bf, preferred_element_type=F32)
            + jnp.dot(lo, ones_bf, preferred_element_type=F32))


def _head_sum(v, ones_ref):
    return jnp.dot(v.astype(BF16), ones_ref[...], preferred_element_type=F32)


def _adam(w, g, m, v):
    m2 = ADAM_B1 * m + (1.0 - ADAM_B1) * g
    v2 = ADAM_B2 * v + (1.0 - ADAM_B2) * (g * g)
    m_hat = m2 / (1.0 - ADAM_B1 ** ADAM_STEP)
    v_hat = v2 / (1.0 - ADAM_B2 ** ADAM_STEP)
    delta = -ADAM_LR * (m_hat / (jnp.sqrt(v_hat) + ADAM_EPS) + ADAM_WD * w)
    return delta, m2, v2


class _Hosted:
    def __init__(self, kind, src):
        self.kind = kind
        n_slot = {"a2a": N_DEV, "ag": N_DEV, "chips": N_DEV // 2}[kind]
        blk = src.shape if kind == "ag" else src.shape[1:]
        self.out_shape = jax.ShapeDtypeStruct((n_slot,) + tuple(blk), src.dtype)
        self.n_peer = n_slot - 1
        self.scratch = [pltpu.SemaphoreType.DMA((self.n_peer,)), pltpu.SemaphoreType.DMA((self.n_peer,)),
                        pltpu.SemaphoreType.DMA]

    def _copies(self, src, dst, send_sems, recv_sems, loc_sem, landing):
        x, y, c = lax.axis_index("x"), lax.axis_index("y"), lax.axis_index("c")
        if self.kind == "chips":
            me = 2 * x + y
            peers = [((px, py, c), 2 * px + py) for px, py in ((1 - x, y), (x, 1 - y), (1 - x, 1 - y))]
        else:
            me = 4 * x + 2 * y + c
            peers = []
            for k in range(1, N_DEV):
                px = 1 - x if (k >> 2) & 1 else x
                py = 1 - y if (k >> 1) & 1 else y
                pc = 1 - c if k & 1 else c
                peers.append(((px, py, pc), 4 * px + 2 * py + pc))
        remote = []
        for k, (peer, pid) in enumerate(peers):
            s = src if self.kind == "ag" else src.at[pid]
            remote.append(pltpu.make_async_remote_copy(
                src_ref=s, dst_ref=dst.at[pid if landing else me],
                send_sem=send_sems.at[k], recv_sem=recv_sems.at[k], device_id=peer, device_id_type=MESH))
        local = pltpu.make_async_copy(src if self.kind == "ag" else src.at[me], dst.at[me], loc_sem)
        return remote, local

    def start(self, src, dst, send_sems, recv_sems, loc_sem):
        remote, local = self._copies(src, dst, send_sems, recv_sems, loc_sem, landing=False)
        for cp in remote:
            cp.start()
        local.start()

    def wait(self, src, dst, send_sems, recv_sems, loc_sem):
        remote, local = self._copies(src, dst, send_sems, recv_sems, loc_sem, landing=True)
        for cp in remote:
            cp.wait_recv()
        for cp in remote:
            cp.wait_send()
        local.wait()


ANY_SPEC = pl.BlockSpec(memory_space=pl.ANY)


def _ada_scratch(n_col):
    return ([pltpu.VMEM((N_DEV, 8, D_MODEL), F32), pltpu.VMEM((16, n_col), F32), pltpu.VMEM((N_DEV, 16, n_col), F32)]
            + [pltpu.SemaphoreType.DMA((N_DEV - 1,)) for _ in range(4)])


def _ada_modulation(cb_ref, cc_ref, w_ref, b_ref, cstack, part, parts, s1, r1, s2, r2):
    x, y, c = lax.axis_index("x"), lax.axis_index("y"), lax.axis_index("c")
    me = 4 * x + 2 * y + c
    peers = []
    for k in range(1, N_DEV):
        px = 1 - x if (k >> 2) & 1 else x
        py = 1 - y if (k >> 1) & 1 else y
        pc = 1 - c if k & 1 else c
        peers.append(((px, py, pc), 4 * px + 2 * py + pc))

    def exchange(src, dst, send_sems, recv_sems):
        for k, (peer, _) in enumerate(peers):
            pltpu.make_async_remote_copy(src_ref=src, dst_ref=dst.at[me], send_sem=send_sems.at[k],
                                         recv_sem=recv_sems.at[k], device_id=peer, device_id_type=MESH).start()
        dst[me] = src[...]
        waits = [pltpu.make_async_remote_copy(src_ref=src, dst_ref=dst.at[pid], send_sem=send_sems.at[k],
                                              recv_sem=recv_sems.at[k], device_id=peer, device_id_type=MESH)
                 for k, (peer, pid) in enumerate(peers)]
        for cp in waits:
            cp.wait_recv()
        for cp in waits:
            cp.wait_send()

    exchange(cb_ref, cstack, s1, r1)
    c_all = cstack[0]
    for d in range(1, N_DEV):
        c_all = c_all + cstack[d]
    row = lax.broadcasted_iota(jnp.int32, (8, D_MODEL), 0)
    cc = jnp.where(row == 0, jnp.broadcast_to(cc_ref[...], (8, D_MODEL)), 0.0)
    call = jnp.concatenate([c_all, cc], axis=0)
    s = call * _sigmoid(call)
    part[...] = jnp.dot(s, w_ref[...], preferred_element_type=F32, precision=HI) + b_ref[...]
    exchange(part, parts, s2, r2)
    return s


def _ada_bwd(s16, dm, dc, w, m, v):
    def body(s_ref, dm_ref, dc_ref, w_ref, m_ref, v_ref, g_ref, d_ref, m2_ref, v2_ref, pc_ref):
        dct = jnp.sum(dc_ref[...], axis=0, keepdims=True)
        row = lax.broadcasted_iota(jnp.int32, dc_ref.shape, 0)
        dcb = jnp.where(row == 0, jnp.broadcast_to(dct, dc_ref.shape), 0.0)
        dm16 = jnp.concatenate([dm_ref[...], dcb], axis=0)
        g = lax.dot_general(s_ref[...], dm16, (((0,), (0,)), ((), ())),
                            preferred_element_type=F32, precision=HI)
        w_ = w_ref[...]
        delta, m2, v2 = _adam(w_, g, m_ref[...], v_ref[...])
        g_ref[...] = g
        d_ref[...] = delta
        m2_ref[...] = m2
        v2_ref[...] = v2
        pc_ref[...] = lax.dot_general(dcb, w_, (((1,), (1,)), ((), ())),
                                      preferred_element_type=F32, precision=HI)

    sh = jax.ShapeDtypeStruct(w.shape, F32)
    return pl.pallas_call(
        body, name="ada_bwd",
        out_shape=[sh, sh, sh, sh, jax.ShapeDtypeStruct((8, D_MODEL), F32)],
        compiler_params=_params(),
    )(s16, dm, dc, w, m, v)


def _head_norm(zk, ones_ref, gain):
    ss = _head_sum(zk * zk, ones_ref)
    return zk * lax.rsqrt(ss * (1.0 / HEAD_DIM) + EPS) * gain


def _inproj_fwd(order, x, ng, c_blk, c_ctx_row, w_ada_sh, b_ada_sh, w_blk_t, w_out_blk, tm):
    T = x.shape[0]
    nt = T // tm
    n_pass = N_DEV // 2
    blk_rows = w_blk_t.shape[0]
    n_col = w_ada_sh.shape[1]
    rider = _Hosted("ag", w_out_blk)
    n_ada = len(_ada_scratch(n_col))

    def body(order_ref, x_ref, g_ref, cb_ref, cc_ref, wa_ref, ba_ref, wb_ref, wo_ref,
             h_out, z_ref, wt_out, wo_out, s_out, parts_out,
             hs, wt, modv, send_sems, recv_sems, loc_sem, h_sem, wt_sem, *rest):
        ada_sc, rider_sems = rest[:n_ada], rest[n_ada:]
        p, i = pl.program_id(0), pl.program_id(1)
        x, y, c = lax.axis_index("x"), lax.axis_index("y"), lax.axis_index("c")
        me, sib = (x, y, c), (x, y, 1 - c)
        chips = [(1 - x, y), (x, 1 - y), (1 - x, 1 - y)]

        def slot(px, py, pc):
            return 4 * px + 2 * py + pc

        def copy(k, block, to, src=None):
            return pltpu.make_async_remote_copy(
                src_ref=wt.at[slot(*block)] if src is None else src, dst_ref=wt.at[slot(*block)],
                send_sem=send_sems.at[k], recv_sem=recv_sems.at[k], device_id=to, device_id_type=MESH)

        own = pltpu.make_async_copy(wb_ref, wt.at[slot(*me)], loc_sem)
        h_copy = pltpu.make_async_copy(hs, h_out, h_sem)
        wt_copy = pltpu.make_async_copy(wt, wt_out, wt_sem)
        first = [copy(1 + j, me, (*chip, c), src=wb_ref) for j, chip in enumerate(chips)] + [copy(0, me, sib, src=wb_ref)]
        passed = [copy(4 + j, (*chip, c), sib) for j, chip in enumerate(chips)]

        @pl.when(jnp.logical_and(p == 0, i == 0))
        def _():
            s_out[...] = _ada_modulation(cb_ref, cc_ref, wa_ref, ba_ref, *ada_sc)
            own.start()
            for cp in first:
                cp.start()
            rider.start(wo_ref, wo_out, *rider_sems)
            parts = ada_sc[2]
            parts_out[...] = parts[...]
            my_row = pl.ds(slot(*me), 1)
            mod = jnp.concatenate([parts[d, my_row, :] for d in range(N_DEV)], axis=1)
            modv[0:1, :] = mod[:, 0:D_MODEL]
            modv[1:2, :] = mod[:, D_MODEL:2 * D_MODEL]
            own.wait()
            copy(0, sib, me).wait_recv()

        for j, chip in enumerate(chips):
            @pl.when(jnp.logical_and(p == j + 1, i == 0))
            def _(j=j, chip=chip):
                copy(1 + j, (*chip, c), me).wait_recv()
                passed[j].start()
                copy(4 + j, (*chip, 1 - c), me).wait_recv()
                if j == len(chips) - 1:
                    wt_copy.start()

        rows = pl.ds(pl.multiple_of(i * tm, tm), tm)

        @pl.when(p == 0)
        def _():
            xv = x_ref[...]
            r = lax.rsqrt(jnp.mean(xv * xv, axis=-1, keepdims=True) + EPS)
            hs[rows, :] = ((xv * r * g_ref[...]) * (1.0 + modv[1:2, :]) + modv[0:1, :]).astype(BF16)

        @pl.when(jnp.logical_and(p == 1, i == 0))
        def _():
            h_copy.start()

        w_pair = wt[pl.ds(2 * order_ref[p], 2)].reshape(2 * blk_rows, D_MODEL)
        z_ref[...] = _nt(hs[rows, :], w_pair).astype(BF16)

        @pl.when(jnp.logical_and(p == n_pass - 1, i == nt - 1))
        def _():
            for cp in first + passed:
                cp.wait_send()
            h_copy.wait()
            wt_copy.wait()
            rider.wait(wo_ref, wo_out, *rider_sems)

    whole = lambda shape: pl.BlockSpec(shape, lambda p, i, o: (0,) * len(shape))
    grid_spec = pltpu.PrefetchScalarGridSpec(
        num_scalar_prefetch=1, grid=(n_pass, nt),
        in_specs=[pl.BlockSpec((tm, D_MODEL), lambda p, i, o: (jnp.where(p == 0, i, nt - 1), 0)),
                  whole((1, D_MODEL)), whole((8, D_MODEL)), whole((1, D_MODEL)), whole((D_MODEL, n_col)),
                  whole((1, n_col)), ANY_SPEC, ANY_SPEC],
        out_specs=[ANY_SPEC, pl.BlockSpec((tm, 2 * blk_rows), lambda p, i, o: (i, o[p])), ANY_SPEC, ANY_SPEC,
                   whole((16, D_MODEL)), whole((N_DEV, 16, n_col))],
        scratch_shapes=[pltpu.VMEM((T, D_MODEL), BF16), pltpu.VMEM((N_DEV, blk_rows, D_MODEL), BF16),
                        pltpu.VMEM((8, D_MODEL), F32),
                        pltpu.SemaphoreType.DMA((7,)), pltpu.SemaphoreType.DMA((7,)), pltpu.SemaphoreType.DMA,
                        pltpu.SemaphoreType.DMA, pltpu.SemaphoreType.DMA] + _ada_scratch(n_col) + rider.scratch)
    return pl.pallas_call(
        body, name="inproj_fwd", grid_spec=grid_spec,
        out_shape=[jax.ShapeDtypeStruct((T, D_MODEL), BF16), jax.ShapeDtypeStruct((T, D_IN), BF16),
                   jax.ShapeDtypeStruct((N_DEV, blk_rows, D_MODEL), BF16), rider.out_shape,
                   jax.ShapeDtypeStruct((16, D_MODEL), F32), jax.ShapeDtypeStruct((N_DEV, 16, n_col), F32)],
        compiler_params=_params(("arbitrary", "arbitrary")),
    )(order, x, ng, c_blk, c_ctx_row, w_ada_sh, b_ada_sh, w_blk_t, w_out_blk)


def _qk_norm(z, ones_blk, gq, gk, tm):
    T = z.shape[0]

    def body(q_ref, k_ref, ones_ref, gq_ref, gk_ref, qs_ref, kn_ref):
        qs = _head_norm(q_ref[...].astype(F32), ones_ref, gq_ref[...]) * (ATT_SCALE * LOG2E)
        qs_ref[...] = qs.astype(BF16)
        kn_ref[...] = _head_norm(k_ref[...].astype(F32), ones_ref, gk_ref[...]).astype(BF16)

    v512 = pl.BlockSpec((1, 512), lambda i: (0, 0))
    return pl.pallas_call(
        body, name="qk_norm", grid=(T // tm,),
        in_specs=[pl.BlockSpec((tm, 512), lambda i: (i, 3)), pl.BlockSpec((tm, 512), lambda i: (i, 4)),
                  pl.BlockSpec((512, 512), lambda i: (0, 0)), v512, v512],
        out_specs=[pl.BlockSpec((tm, 512), lambda i: (i, 0)), pl.BlockSpec((tm, 512), lambda i: (i, 0))],
        out_shape=[jax.ShapeDtypeStruct((T, 512), BF16), jax.ShapeDtypeStruct((T, 512), BF16)],
        compiler_params=_params(("arbitrary",)),
    )(z, z, ones_blk, gq, gk)


def _ctx_fwd(ctx, ng, cscale, cshift, w_in_t, ones_blk, gk):
    C = ctx.shape[0]

    def body(x_ref, g_ref, sc_ref, sh_ref, w_ref, ones_ref, gk_ref, h_ref, z_ref, kn_ref):
        xv = x_ref[...]
        r = lax.rsqrt(jnp.mean(xv * xv, axis=-1, keepdims=True) + EPS)
        h = (xv * r * g_ref[...]) * (1.0 + sc_ref[...]) + sh_ref[...]
        hb = h.astype(BF16)
        h_ref[...] = hb
        zk = _nt(hb, w_ref[0:512, :])
        zv = _nt(hb, w_ref[512:1024, :])
        z_ref[:, 0:512] = zk.astype(BF16)
        z_ref[:, 512:1024] = zv.astype(BF16)
        kn_ref[...] = _head_norm(zk, ones_ref, gk_ref[...]).astype(BF16)

    vec = pl.BlockSpec((1, D_MODEL), lambda i: (0, 0))
    return pl.pallas_call(
        body, name="ctx_fwd", grid=(1,),
        in_specs=[pl.BlockSpec((C, D_MODEL), lambda i: (0, 0)), vec, vec, vec,
                  pl.BlockSpec((1024, D_MODEL), lambda i: (2, 0)),
                  pl.BlockSpec((512, 512), lambda i: (0, 0)), pl.BlockSpec((1, 512), lambda i: (0, 0))],
        out_specs=[pl.BlockSpec((C, D_MODEL), lambda i: (0, 0)), pl.BlockSpec((C, 1024), lambda i: (0, 0)),
                   pl.BlockSpec((C, 512), lambda i: (0, 0))],
        out_shape=[jax.ShapeDtypeStruct((C, D_MODEL), BF16), jax.ShapeDtypeStruct((C, 1024), BF16),
                   jax.ShapeDtypeStruct((C, 512), BF16)],
        compiler_params=_params(("arbitrary",)),
    )(ctx, ng, cscale, cshift, w_in_t, ones_blk, gk)


def _sgu_chunk_fwd(au, av, ag, sg, ws_bf, bsb):
    gu, dgu = _gelu_parts(au)
    gv, dgv = _gelu_parts(av)
    rr = lax.rsqrt(jnp.mean(gv * gv, axis=-1, keepdims=True) + EPS)
    vhat = gv * rr
    vn = vhat * sg
    mixed = jnp.dot(ws_bf, vn.astype(BF16), preferred_element_type=F32) + bsb
    sig = _sigmoid(ag)
    sl = ag * sig
    return gu * mixed * sl, (gu, dgu, dgv, rr, vhat, vn, mixed, sig, sl)


def _sgu_fwd(z, sgn, ws, bsb, tm):
    T = z.shape[0]

    def body(au_ref, av_ref, ag_ref, sg_ref, ws_ref, bsb_ref, o_ref):
        for g in range(SGU_GROUPS):
            ws_bf = ws_ref[g].astype(BF16)
            sg = sg_ref[:, 128 * g:128 * (g + 1)]
            bsb_g = bsb_ref[g]
            for j in range(tm // CHUNK):
                rs, cs = slice(CHUNK * j, CHUNK * (j + 1)), slice(128 * g, 128 * (g + 1))
                out, _ = _sgu_chunk_fwd(au_ref[rs, cs].astype(F32), av_ref[rs, cs].astype(F32),
                                        ag_ref[rs, cs].astype(F32), sg, ws_bf, bsb_g)
                o_ref[rs, cs] = out.astype(BF16)

    return pl.pallas_call(
        body, name="sgu_fwd", grid=(T // tm,),
        in_specs=[pl.BlockSpec((tm, 512), lambda i: (i, 0)), pl.BlockSpec((tm, 512), lambda i: (i, 1)),
                  pl.BlockSpec((tm, 512), lambda i: (i, 2)), pl.BlockSpec((1, 512), lambda i: (0, 0)),
                  pl.BlockSpec((SGU_GROUPS, CHUNK, CHUNK), lambda i: (0, 0, 0)),
                  pl.BlockSpec((SGU_GROUPS, CHUNK, 128), lambda i: (0, 0, 0))],
        out_specs=pl.BlockSpec((tm, 512), lambda i: (i, 0)),
        out_shape=jax.ShapeDtypeStruct((T, 512), BF16),
        compiler_params=_params(("arbitrary",)),
    )(z, z, z, sgn, ws, bsb)


def _attn_type(rb, nrb):
    return jnp.where(rb == 0, 0, jnp.where(rb == nrb - 1, 2, 1))


def _attn_specs(T, C):
    return [
        pl.BlockSpec((ATT_SUB * TQ, 128), lambda hp, st: (st, hp)),
        pl.BlockSpec((T, 128), lambda hp, st: (0, hp)),
        pl.BlockSpec((T, 128), lambda hp, st: (0, 20 + hp)),
        pl.BlockSpec((C, 128), lambda hp, st: (0, hp)),
        pl.BlockSpec((C, 128), lambda hp, st: (0, 4 + hp)),
        pl.BlockSpec((2, 2 * WIN_R, GRID_W, 128), lambda hp, st: (hp, 0, 0, 0)),
        pl.BlockSpec((3, TQ, TK), lambda hp, st: (0, 0, 0)),
        pl.BlockSpec((ATT_SUB * TQ, 128), lambda hp, st: (st, 24 + hp)),
    ]


def _build_bias(pairs_ref, mask_ref, bias_sc):
    for t in range(3):
        for hh in range(2):
            for i in range(Q_ROWS):
                for mm in range(K_ROWS // 2):
                    p = min(max(WIN_R - Q_ROWS * t + 2 * mm - i, 0), 2 * WIN_R - 1)
                    rs, cs = slice(GRID_W * i, GRID_W * (i + 1)), slice(128 * mm, 128 * (mm + 1))
                    bias_sc[t, hh, rs, cs] = (pairs_ref[hh, p] + mask_ref[t, rs, cs]) * LOG2E


def _attn_fwd(qs, kn, z, ckn, zc, pairs, row_mask):
    T, C = qs.shape[0], ckn.shape[0]
    rows = T // GRID_W
    nrb = rows // Q_ROWS

    def body(q_ref, k_ref, v_ref, ck_ref, cv_ref, pairs_ref, mask_ref, bg_ref, ob_ref, outb_ref, lse_ref, bias_sc):
        @pl.when(pl.program_id(1) == 0)
        def _():
            _build_bias(pairs_ref, mask_ref, bias_sc)

        ck2, cv2 = ck_ref[...], cv_ref[...]
        lane = lax.broadcasted_iota(jnp.int32, (1, 128), 1)
        for sub in range(ATT_SUB):
            rb = ATT_SUB * pl.program_id(1) + sub
            bias_ref = bias_sc.at[_attn_type(rb, nrb)]
            rs = slice(TQ * sub, TQ * (sub + 1))
            ks = pl.multiple_of(jnp.clip(Q_ROWS * rb - 4, 0, rows - K_ROWS) * GRID_W, GRID_W)
            q2 = q_ref[rs, :]
            k2 = k_ref[pl.ds(ks, TK), :]
            v2 = v_ref[pl.ds(ks, TK), :]
            o_acc = jnp.zeros((TQ, 128), F32)
            lse_acc = jnp.zeros((TQ, 128), F32)
            for hh in range(2):
                msk = (lane >= HEAD_DIM) == bool(hh)
                qm = jnp.where(msk, q2, jnp.zeros_like(q2))
                s = _nt(qm, k2) + bias_ref[hh]
                sc = _nt(qm, ck2)
                m = jnp.maximum(jnp.max(s, axis=-1, keepdims=True), jnp.max(sc, axis=-1, keepdims=True))
                p = jnp.exp2(s - m)
                pc = jnp.exp2(sc - m)
                va = jnp.where(msk, v2, jnp.ones_like(v2))
                cva = jnp.where(msk, cv2, jnp.ones_like(cv2))
                num = (jnp.dot(p.astype(BF16), va, preferred_element_type=F32)
                       + jnp.dot(pc.astype(BF16), cva, preferred_element_type=F32))
                den = pltpu.roll(num, HEAD_DIM, 1)
                o_acc = jnp.where(msk, num / den, o_acc)
                lse_acc = jnp.where(msk, m + jnp.log(den) * LOG2E, lse_acc)
            ob_ref[rs, :] = o_acc.astype(BF16)
            lse_ref[rs, :] = lse_acc
            bg = bg_ref[rs, :].astype(F32)
            outb_ref[rs, :] = (o_acc * (bg * _sigmoid(bg))).astype(BF16)

    tile = pl.BlockSpec((ATT_SUB * TQ, 128), lambda hp, st: (st, hp))
    return pl.pallas_call(
        body, name="attn_fwd", grid=(4, nrb // ATT_SUB),
        in_specs=_attn_specs(T, C),
        out_specs=[tile, tile, tile],
        out_shape=[jax.ShapeDtypeStruct((T, 512), BF16), jax.ShapeDtypeStruct((T, 512), BF16),
                   jax.ShapeDtypeStruct((T, 512), F32)],
        scratch_shapes=[pltpu.VMEM((3, 2, TQ, TK), F32)],
        compiler_params=_params(("arbitrary", "arbitrary")),
    )(qs, kn, z, ckn, zc, pairs, row_mask, z)


def _outproj_loss_bwd(x, tgt, out_a, out_b, gate, w_out, tm):
    T = x.shape[0]
    nt = T // tm

    def body(x_ref, t_ref, oa_ref, ob_ref, gate_ref, w_ref, dy_ref, dmc_ref, dw_ref, dgate_ref, loss_ref, acc):
        @pl.when(pl.program_id(0) == 0)
        def _():
            acc[...] = jnp.zeros_like(acc)
            dgate_ref[...] = jnp.zeros_like(dgate_ref)
            loss_ref[...] = jnp.zeros_like(loss_ref)

        oa, ob = oa_ref[...], ob_ref[...]
        gate_v = gate_ref[...]
        mix = (jnp.dot(oa, w_ref[0:512, :], preferred_element_type=F32)
               + jnp.dot(ob, w_ref[512:1024, :], preferred_element_type=F32))
        e = x_ref[...] + gate_v * mix - t_ref[...]
        se = jnp.sum(jnp.sum(e * e, axis=0, keepdims=True), axis=1, keepdims=True)
        loss_ref[...] += jnp.broadcast_to(se * (0.5 / D_MODEL), loss_ref.shape)
        dy = e * (1.0 / D_MODEL)
        dy_ref[...] = dy
        dgate_ref[...] += jnp.sum(dy * mix, axis=0, keepdims=True)
        dmix = (dy * gate_v).astype(BF16)
        dmc_ref[...] = _nt(dmix, w_ref[...]).astype(BF16)
        acc[0:512, :] += _tn(oa, dmix)
        acc[512:1024, :] += _tn(ob, dmix)

        @pl.when(pl.program_id(0) == nt - 1)
        def _():
            dw_ref[...] = acc[...].astype(BF16)

    row = lambda w: pl.BlockSpec((tm, w), lambda i: (i, 0))
    return pl.pallas_call(
        body, name="outproj_loss_bwd", grid=(nt,),
        in_specs=[row(D_MODEL), row(D_MODEL), row(512), row(512),
                  pl.BlockSpec((1, D_MODEL), lambda i: (0, 0)),
                  pl.BlockSpec((D_MODEL, D_MODEL), lambda i: (0, 0))],
        out_specs=[row(D_MODEL), row(D_MODEL), pl.BlockSpec((D_MODEL, D_MODEL), lambda i: (0, 0)),
                   pl.BlockSpec((1, D_MODEL), lambda i: (0, 0)), pl.BlockSpec((1, 128), lambda i: (0, 0))],
        out_shape=[jax.ShapeDtypeStruct((T, D_MODEL), F32), jax.ShapeDtypeStruct((T, D_MODEL), BF16),
                   jax.ShapeDtypeStruct((D_MODEL, D_MODEL), BF16), jax.ShapeDtypeStruct((1, D_MODEL), F32),
                   jax.ShapeDtypeStruct((1, 128), F32)],
        scratch_shapes=[pltpu.VMEM((D_MODEL, D_MODEL), F32)],
        compiler_params=_params(("arbitrary",)),
    )(x, tgt, out_a, out_b, gate, w_out)


def _attn_bwd(qs, kn, z, ckn, zc, pairs, row_mask, ob, lse, dmc, hsel, fold, riders):
    T, C = qs.shape[0], ckn.shape[0]
    rows = T // GRID_W
    nrb = rows // Q_ROWS
    n_st = nrb // ATT_SUB
    n_rid = len(riders)

    def body(q_ref, k_ref, v_ref, ck_ref, cv_ref, pairs_ref, mask_ref, bg_ref, ob_ref, lse_ref, do_ref,
             hsel_ref, fold_ref, *rest):
        rid_src = rest[:n_rid]
        dq_ref, dk_ref, dv_ref, dck_ref, dcv_ref, dbg_ref, drpb_ref = rest[n_rid:n_rid + 7]
        rid_dst = rest[n_rid + 7:2 * n_rid + 7]
        bias_sc, dacc_ref = rest[2 * n_rid + 7:2 * n_rid + 9]
        rid_sems = rest[2 * n_rid + 9:]
        hp, st = pl.program_id(0), pl.program_id(1)

        @pl.when(jnp.logical_and(hp == 0, st == 0))
        def _():
            for r, (ex, _) in enumerate(riders):
                ex.start(rid_src[r], rid_dst[r], *rid_sems[3 * r:3 * r + 3])

        @pl.when(st == 0)
        def _():
            _build_bias(pairs_ref, mask_ref, bias_sc)
            dk_ref[...] = jnp.zeros_like(dk_ref)
            dv_ref[...] = jnp.zeros_like(dv_ref)
            dck_ref[...] = jnp.zeros_like(dck_ref)
            dcv_ref[...] = jnp.zeros_like(dcv_ref)
            dacc_ref[...] = jnp.zeros_like(dacc_ref)

        ck2, cv2 = ck_ref[...], cv_ref[...]
        lane = lax.broadcasted_iota(jnp.int32, (1, 128), 1)
        for sub in range(ATT_SUB):
            rb = ATT_SUB * st + sub
            bias_ref = bias_sc.at[_attn_type(rb, nrb)]
            rs = slice(TQ * sub, TQ * (sub + 1))
            kb = jnp.clip(Q_ROWS * rb - 4, 0, rows - K_ROWS)
            ks = pl.multiple_of(kb * GRID_W, GRID_W)
            ebase = kb - Q_ROWS * rb + 11
            q2 = q_ref[rs, :]
            k2 = k_ref[pl.ds(ks, TK), :]
            v2 = v_ref[pl.ds(ks, TK), :]
            bg = bg_ref[rs, :].astype(F32)
            sig = _sigmoid(bg)
            obv = ob_ref[rs, :].astype(F32)
            dout = do_ref[rs, :].astype(F32)
            dbg_ref[rs, :] = (dout * obv * (sig * (1.0 + bg * (1.0 - sig)))).astype(BF16)
            d_o = dout * (bg * sig)
            d_oo = d_o * obv
            lse2 = lse_ref[rs, :]
            dq_acc = jnp.zeros((TQ, 128), F32)
            for hh in range(2):
                msk = (lane >= HEAD_DIM) == bool(hh)
                qm = jnp.where(msk, q2, jnp.zeros_like(q2))
                lse_h = jnp.max(jnp.where(msk, lse2, -jnp.inf), axis=-1, keepdims=True)
                p = jnp.exp2(_nt(qm, k2) + bias_ref[hh] - lse_h)
                pc = jnp.exp2(_nt(qm, ck2) - lse_h)
                dom_f = jnp.where(msk, d_o, 0.0)
                dom = dom_f.astype(BF16)
                delta = jnp.sum(jnp.where(msk, d_oo, 0.0), axis=-1, keepdims=True)
                d_hi = delta.astype(BF16).astype(F32)
                x0 = HEAD_DIM * (1 - hh)
                dom_aug = jnp.where(lane == x0, -d_hi, jnp.where(lane == x0 + 1, d_hi - delta, dom_f)).astype(BF16)
                extra = jnp.logical_or(lane == x0, lane == x0 + 1)
                va = jnp.where(msk, v2, jnp.where(extra, jnp.ones_like(v2), jnp.zeros_like(v2)))
                cva = jnp.where(msk, cv2, jnp.where(extra, jnp.ones_like(cv2), jnp.zeros_like(cv2)))
                ds = p * _nt(dom_aug, va)
                dsc = pc * _nt(dom_aug, cva)
                dsb, dscb = ds.astype(BF16), dsc.astype(BF16)
                dq_h = (jnp.dot(dsb, k2, preferred_element_type=F32)
                        + jnp.dot(dscb, ck2, preferred_element_type=F32))
                dq_acc = jnp.where(msk, dq_h, dq_acc)
                dk_ref[pl.ds(ks, TK), :] += _tn(dsb, qm)
                dv_ref[pl.ds(ks, TK), :] += _tn(p.astype(BF16), dom)
                dck_ref[...] += _tn(dscb, qm)
                dcv_ref[...] += _tn(pc.astype(BF16), dom)
                for i in range(Q_ROWS):
                    for mm in range(K_ROWS // 2):
                        dacc_ref[hh, ebase + (2 * mm - i)] += ds[GRID_W * i:GRID_W * (i + 1),
                                                                 128 * mm:128 * (mm + 1)]
            dq_ref[rs, :] = dq_acc

        @pl.when(st == n_st - 1)
        def _():
            for hh in range(2):
                drpb_ref[hh] = _rpb_diag_sums(dacc_ref.at[hh], hsel_ref, fold_ref)

        @pl.when(jnp.logical_and(hp == pl.num_programs(0) - 1, st == n_st - 1))
        def _():
            for r, (ex, _) in enumerate(riders):
                ex.wait(rid_src[r], rid_dst[r], *rid_sems[3 * r:3 * r + 3])

    tile = pl.BlockSpec((ATT_SUB * TQ, 128), lambda hp, st: (st, hp))
    colT = pl.BlockSpec((T, 128), lambda hp, st: (0, hp))
    colC = pl.BlockSpec((C, 128), lambda hp, st: (0, hp))
    res = pl.pallas_call(
        body, name="attn_bwd", grid=(4, n_st),
        in_specs=(_attn_specs(T, C) + [tile, tile, pl.BlockSpec((ATT_SUB * TQ, 128), lambda hp, st: (st, 4 + hp)),
                                       pl.BlockSpec((8, 128, 128), lambda hp, st: (0, 0, 0)),
                                       pl.BlockSpec((32, 256), lambda hp, st: (0, 0))]
                  + [ANY_SPEC] * n_rid),
        out_specs=([tile, colT, colT, colC, colC, tile, pl.BlockSpec((2, 32, 128), lambda hp, st: (hp, 0, 0))]
                   + [ANY_SPEC] * n_rid),
        out_shape=([jax.ShapeDtypeStruct((T, 512), F32), jax.ShapeDtypeStruct((T, 512), F32),
                    jax.ShapeDtypeStruct((T, 512), F32), jax.ShapeDtypeStruct((C, 512), F32),
                    jax.ShapeDtypeStruct((C, 512), F32), jax.ShapeDtypeStruct((T, 512), BF16),
                    jax.ShapeDtypeStruct((N_HEADS, 32, 128), F32)] + [ex.out_shape for ex, _ in riders]),
        scratch_shapes=([pltpu.VMEM((3, 2, TQ, TK), F32), pltpu.VMEM((2, N_DIAG, GRID_W, 128), F32)]
                        + [s for ex, _ in riders for s in ex.scratch]),
        compiler_params=_params(("arbitrary", "arbitrary")),
    )(qs, kn, z, ckn, zc, pairs, row_mask, z, ob, lse, dmc, hsel, fold, *[a for _, a in riders])
    return res[:7], res[7:]


def _rpb_diag_sums(a_ref, hsel_ref, fold_ref):
    n_off = 2 * WIN_C - 1
    n_dr = 2 * WIN_R - 1
    qc = lax.broadcasted_iota(jnp.int32, (GRID_W, 128), 0)
    lane = lax.broadcasted_iota(jnp.int32, (GRID_W, 128), 1)
    diff = lane % GRID_W - qc + (WIN_C - 1)
    left = lane < GRID_W

    def by_dr(dr):
        return a_ref[dr + 4] + pltpu.roll(a_ref[dr + 3], GRID_W, 1)

    out = jnp.zeros((32, 128), F32)
    for j in range((n_dr + 1) // 2):
        hi = pltpu.roll(by_dr(2 * j + 1), GRID_W, 1) if 2 * j + 1 < n_dr else 0.0
        pair = jnp.where(left, by_dr(2 * j), hi)
        parts = []
        for o in range(n_off):
            mv = jnp.where(diff == o, pair, 0.0)
            acc = mv[0:8]
            for r8 in range(1, GRID_W // 8):
                acc = acc + mv[8 * r8:8 * (r8 + 1)]
            parts.append(acc)
        parts.append(jnp.zeros((8, 128), F32))
        stack = jnp.concatenate(parts, axis=0)
        s_hi = stack.astype(BF16)
        s_lo = (stack - s_hi.astype(F32)).astype(BF16)
        per_o = (jnp.dot(fold_ref[...], s_hi, preferred_element_type=F32)
                 + jnp.dot(fold_ref[...], s_lo, preferred_element_type=F32))
        out = out + _dot2(per_o, hsel_ref[j])
    return out


def _head_norm_bwd(raw, dn, gain, ones_ref):
    rr = lax.rsqrt(_head_sum(raw * raw, ones_ref) * (1.0 / HEAD_DIM) + EPS)
    hat = raw * rr
    dgain = jnp.sum(dn * hat, axis=0, keepdims=True)
    dhat = dn * gain
    mean = _head_sum(dhat * hat, ones_ref) * (1.0 / HEAD_DIM)
    return rr * (dhat - hat * mean), dgain


def _ctx_k_bwd(zc, dck, dcv, ones_blk, gk, foldm):
    C = dck.shape[0]

    def body(bk_ref, dk_ref, dv_ref, ones_ref, gk_ref, fold_ref, dbk_ref, dbv_ref, dgk_ref):
        dbk, dgk = _head_norm_bwd(bk_ref[...].astype(F32), dk_ref[...] * LN2, gk_ref[...], ones_ref)
        dbk_ref[...] = dbk.astype(BF16)
        dbv_ref[...] = dv_ref[...].astype(BF16)
        dgk_ref[...] = jnp.dot(jnp.broadcast_to(dgk, (8, 512)), fold_ref[...],
                               preferred_element_type=F32, precision=HI)

    row = pl.BlockSpec((C, 512), lambda i: (0, 0))
    cst = lambda a, b: pl.BlockSpec((a, b), lambda i: (0, 0))
    out_row = jax.ShapeDtypeStruct((C, 512), BF16)
    return pl.pallas_call(
        body, name="ctx_k_bwd", grid=(1,),
        in_specs=[row, row, row, cst(512, 512), cst(1, 512), cst(512, 128)],
        out_specs=[row, row, cst(8, 128)],
        out_shape=[out_row, out_row, jax.ShapeDtypeStruct((8, 128), F32)],
        compiler_params=_params(("arbitrary",)),
    )(zc, dck, dcv, ones_blk, gk, foldm)


def _bwd_mid(z, dmc, dqs, dk, dv, db_g, h, hc, dzc_k, dzc_v, sgn, ws, wst, bsb, ones8, ones_blk, gq, gk, foldm, tk,
             riders=()):
    T = z.shape[0]
    nt = T // tk
    blk = D_IN // N_DEV
    n_in, n_out, n_sc = 23, 7, 9
    n_rid = len(riders)

    def body(*refs):
        (au_ref, av_ref, ag_ref, bq_ref, bk_ref, d_ref, dq_ref, dk_ref, dv_ref, dbg_ref, h_ref,
         hc_ref, dzck_ref, dzcv_ref, sg_ref, ws_ref, wst_ref, bsb_ref, ones8_ref, ones_ref, gq_ref, gk_ref,
         fold_ref) = refs[:n_in]
        rid_src = refs[n_in:n_in + n_rid]
        dz_ref, sums_out, dws_ref, dbs_ref, dsg_ref, dgq_ref, dgk_ref = refs[n_in + n_rid:n_in + n_rid + n_out]
        rid_dst = refs[n_in + n_rid + n_out:n_in + 2 * n_rid + n_out]
        (acc, accq, acck, stage, sem, send_buf, tmp, s1, r1) = refs[n_in + 2 * n_rid + n_out:
                                                                   n_in + 2 * n_rid + n_out + n_sc]
        rid_sems = refs[n_in + 2 * n_rid + n_out + n_sc:]
        t = pl.program_id(0)

        @pl.when(t == 0)
        def _():
            for r, (ex, _) in enumerate(riders):
                ex.start(rid_src[r], rid_dst[r], *rid_sems[3 * r:3 * r + 3])
            acc[...] = jnp.zeros_like(acc)
            acc[512 * 4:512 * 5, :] = _tn(dzck_ref[...], hc_ref[...])
            acc[512 * 5:512 * 6, :] = _tn(dzcv_ref[...], hc_ref[...])
            dws_ref[...] = jnp.zeros_like(dws_ref)
            dbs_ref[...] = jnp.zeros_like(dbs_ref)
            dsg_ref[...] = jnp.zeros_like(dsg_ref)
            accq[...] = jnp.zeros_like(accq)
            acck[...] = jnp.zeros_like(acck)

        for g in range(SGU_GROUPS):
            ws_bf = ws_ref[g].astype(BF16)
            wst_bf = wst_ref[g].astype(BF16)
            sg = sg_ref[:, 128 * g:128 * (g + 1)]
            bsb_g = bsb_ref[g]
            for j in range(tk // CHUNK):
                rs, cs = slice(CHUNK * j, CHUNK * (j + 1)), slice(128 * g, 128 * (g + 1))
                au, av, ag = (au_ref[rs, cs].astype(F32), av_ref[rs, cs].astype(F32), ag_ref[rs, cs].astype(F32))
                d = d_ref[rs, cs].astype(F32)
                _, (gu, dgu, dgv, rr, vhat, vn, mixed, sig, sl) = _sgu_chunk_fwd(au, av, ag, sg, ws_bf, bsb_g)
                dz_ref[rs, 128 * g:128 * (g + 1)] = (d * mixed * sl * dgu).astype(BF16)
                dz_ref[rs, 1024 + 128 * g:1024 + 128 * (g + 1)] = (
                    d * gu * mixed * (sig * (1.0 + ag * (1.0 - sig)))).astype(BF16)
                dmixed = d * gu * sl
                dmb = dmixed.astype(BF16)
                dm_lo = (dmixed - dmb.astype(F32)).astype(BF16)
                dbs_ref[g] += _nt(ones8_ref[...], dmb) + _nt(ones8_ref[...], dm_lo)
                dws_ref[g] += _nt(dmb, vn.astype(BF16))
                dvn = jnp.dot(wst_bf, dmb, preferred_element_type=F32)
                dsg_ref[:, 128 * g:128 * (g + 1)] += jnp.sum(dvn * vhat, axis=0, keepdims=True)
                dvhat = dvn * sg
                mean = jnp.mean(dvhat * vhat, axis=-1, keepdims=True)
                dz_ref[rs, 512 + 128 * g:512 + 128 * (g + 1)] = (rr * (dvhat - vhat * mean) * dgv).astype(BF16)

        dbq, dgq = _head_norm_bwd(bq_ref[...].astype(F32), dq_ref[...] * ATT_SCALE, gq_ref[...], ones_ref)
        dz_ref[:, 512 * 3:512 * 4] = dbq.astype(BF16)
        accq[...] += dgq
        dbk, dgk = _head_norm_bwd(bk_ref[...].astype(F32), dk_ref[...] * LN2, gk_ref[...], ones_ref)
        dz_ref[:, 512 * 4:512 * 5] = dbk.astype(BF16)
        acck[...] += dgk
        dz_ref[:, 512 * 5:512 * 6] = dv_ref[...].astype(BF16)
        dz_ref[:, 512 * 6:512 * 7] = dbg_ref[...]

        hv = h_ref[...]
        for k in range(N_BRANCH):
            acc[512 * k:512 * (k + 1), :] += _tn(dz_ref[:, 512 * k:512 * (k + 1)], hv)

        @pl.when(t == nt - 1)
        def _():
            dgq_ref[...] = jnp.dot(jnp.broadcast_to(accq[...], (8, 512)), fold_ref[...],
                                   preferred_element_type=F32, precision=HI)
            dgk_ref[...] = jnp.dot(jnp.broadcast_to(acck[...], (8, 512)), fold_ref[...],
                                   preferred_element_type=F32, precision=HI)
            cidx = lax.axis_index("c")
            sib = (lax.axis_index("x"), lax.axis_index("y"), 1 - cidx)
            swaps = []
            for q in range(N_DEV // 2):
                theirs = acc[pl.ds(pl.multiple_of(2 * blk * q + blk * (1 - cidx), 8), blk), :]
                send_buf[q] = theirs.astype(BF16)
                cp = pltpu.make_async_remote_copy(src_ref=send_buf.at[q], dst_ref=tmp.at[q], send_sem=s1.at[q],
                                                  recv_sem=r1.at[q], device_id=sib, device_id_type=MESH)
                cp.start()
                swaps.append(cp)
            for q in range(N_DEV // 2):
                swaps[q].wait_recv()
                mine = acc[pl.ds(pl.multiple_of(2 * blk * q + blk * cidx, 8), blk), :]
                stage[...] = (mine + tmp[q].astype(F32)).astype(BF16)
                out = pltpu.make_async_copy(stage, sums_out.at[q], sem)
                out.start()
                out.wait()
            for cp in swaps:
                cp.wait_send()
            for r, (ex, _) in enumerate(riders):
                ex.wait(rid_src[r], rid_dst[r], *rid_sems[3 * r:3 * r + 3])

    zcol = lambda col: pl.BlockSpec((tk, 512), lambda t: (t, col))
    row = pl.BlockSpec((tk, 512), lambda t: (t, 0))
    whole = lambda a: pl.BlockSpec(a.shape, lambda t: (0,) * a.ndim)
    res = pl.pallas_call(
        body, name="bwd_mid", grid=(nt,),
        in_specs=[zcol(0), zcol(1), zcol(2), zcol(3), zcol(4), row, row, row, row, row,
                  pl.BlockSpec((tk, D_MODEL), lambda t: (t, 0)), whole(hc), whole(dzc_k), whole(dzc_v),
                  whole(sgn), whole(ws), whole(wst), whole(bsb), whole(ones8), whole(ones_blk), whole(gq), whole(gk),
                  whole(foldm)] + [ANY_SPEC] * n_rid,
        out_specs=[pl.BlockSpec((tk, D_IN), lambda t: (t, 0)), ANY_SPEC,
                   pl.BlockSpec((SGU_GROUPS, CHUNK, CHUNK), lambda t: (0, 0, 0)),
                   pl.BlockSpec((SGU_GROUPS, 8, CHUNK), lambda t: (0, 0, 0)),
                   pl.BlockSpec((1, 512), lambda t: (0, 0)), pl.BlockSpec((8, 128), lambda t: (0, 0)),
                   pl.BlockSpec((8, 128), lambda t: (0, 0))] + [ANY_SPEC] * n_rid,
        out_shape=[jax.ShapeDtypeStruct((T, D_IN), BF16), jax.ShapeDtypeStruct((N_DEV // 2, blk, D_MODEL), BF16),
                   jax.ShapeDtypeStruct((SGU_GROUPS, CHUNK, CHUNK), F32),
                   jax.ShapeDtypeStruct((SGU_GROUPS, 8, CHUNK), F32), jax.ShapeDtypeStruct((1, 512), F32),
                   jax.ShapeDtypeStruct((8, 128), F32), jax.ShapeDtypeStruct((8, 128), F32)]
                  + [ex.out_shape for ex, _ in riders],
        scratch_shapes=[pltpu.VMEM((D_IN, D_MODEL), F32), pltpu.VMEM((1, 512), F32), pltpu.VMEM((1, 512), F32),
                        pltpu.VMEM((blk, D_MODEL), BF16), pltpu.SemaphoreType.DMA,
                        pltpu.VMEM((N_DEV // 2, blk, D_MODEL), BF16), pltpu.VMEM((N_DEV // 2, blk, D_MODEL), BF16),
                        pltpu.SemaphoreType.DMA((N_DEV // 2,)), pltpu.SemaphoreType.DMA((N_DEV // 2,))]
                       + [s for ex, _ in riders for s in ex.scratch],
        compiler_params=_params(("arbitrary",)),
    )(z, z, z, z, z, dmc, dqs, dk, dv, db_g, h, hc, dzc_k, dzc_v, sgn, ws, wst, bsb, ones8, ones_blk, gq, gk, foldm,
      *[a for _, a in riders])
    return res[:n_out], res[n_out:]


def _inproj_bwd_dx(dzs, w_row0, w_in_t, x, dy, ng, scale, shift, tm, name, riders=(), tiles=None, into=None):
    T = x.shape[0]
    n = len(dzs)
    wpc = dzs[0].shape[1]
    t0, nt = tiles if tiles is not None else (0, T // tm)
    with_dx = dy is not None
    n_own_in = n + 5 + with_dx
    n_own_out = 3 + with_dx
    n_rid = len(riders)
    n_alias = int(into is not None)

    def body(*refs):
        dz_refs = refs[:n]
        n_in = n_own_in + n_rid + n_alias
        own = refs[n:n_own_in] + refs[n_in:n_in + n_own_out]
        rid_src = refs[n_own_in:n_own_in + n_rid]
        rid_dst = refs[n_in + n_own_out:n_in + n_own_out + n_rid]
        rid_sems = refs[n_in + n_own_out + n_rid:]
        if with_dx:
            w_ref, x_ref, dy_ref, g_ref, sc_ref, sh_ref, gx_ref, dsh_ref, dsc_ref, dg_ref = own
        else:
            w_ref, x_ref, g_ref, sc_ref, sh_ref, dsh_ref, dsc_ref, dg_ref = own

        @pl.when(pl.program_id(0) == 0)
        def _():
            for r, (ex, _) in enumerate(riders):
                ex.start(rid_src[r], rid_dst[r], *rid_sems[3 * r:3 * r + 3])
            dsh_ref[...] = jnp.zeros_like(dsh_ref)
            dsc_ref[...] = jnp.zeros_like(dsc_ref)
            dg_ref[...] = jnp.zeros_like(dg_ref)

        dh = jnp.dot(dz_refs[0][...], w_ref[0:wpc, :], preferred_element_type=F32)
        for k in range(1, n):
            dh = dh + jnp.dot(dz_refs[k][...], w_ref[wpc * k:wpc * (k + 1), :], preferred_element_type=F32)
        xv = x_ref[...]
        r = lax.rsqrt(jnp.mean(xv * xv, axis=-1, keepdims=True) + EPS)
        xn = xv * r
        gv, op = g_ref[...], 1.0 + sc_ref[...]
        dsh_ref[...] += jnp.sum(dh, axis=0, keepdims=True)
        dsc_ref[...] += jnp.sum(dh * xn * gv, axis=0, keepdims=True)
        dg_ref[...] += jnp.sum(dh * op * xn, axis=0, keepdims=True)
        if with_dx:
            dxn = dh * (gv * op)
            gx_ref[...] = r * (dxn - xn * jnp.mean(dxn * xn, axis=-1, keepdims=True)) + dy_ref[...]

        @pl.when(pl.program_id(0) == nt - 1)
        def _():
            for r, (ex, _) in enumerate(riders):
                ex.wait(rid_src[r], rid_dst[r], *rid_sems[3 * r:3 * r + 3])

    vec = pl.BlockSpec((1, D_MODEL), lambda i: (0, 0))
    rowf = pl.BlockSpec((tm, D_MODEL), lambda i: (i + t0, 0))
    in_specs = [pl.BlockSpec((tm, wpc), lambda i: (i + t0, 0))] * n
    in_specs += [pl.BlockSpec((wpc * n, D_MODEL), lambda i: (w_row0 // (wpc * n), 0)), rowf]
    args = list(dzs) + [w_in_t, x]
    vshape = jax.ShapeDtypeStruct((1, D_MODEL), F32)
    out_specs, out_shape = [vec, vec, vec], [vshape, vshape, vshape]
    if with_dx:
        in_specs.append(rowf)
        args.append(dy)
        out_specs = [rowf] + out_specs
        out_shape = [jax.ShapeDtypeStruct((T, D_MODEL), F32)] + out_shape
    in_specs += [vec, vec, vec] + [ANY_SPEC] * (n_rid + n_alias)
    args += [ng, scale, shift] + [a for _, a in riders] + ([into] if n_alias else [])
    res = pl.pallas_call(
        body, name=name, grid=(nt,), in_specs=in_specs, out_specs=out_specs + [ANY_SPEC] * n_rid,
        out_shape=out_shape + [ex.out_shape for ex, _ in riders],
        scratch_shapes=[s for ex, _ in riders for s in ex.scratch],
        input_output_aliases={len(args) - 1: 0} if n_alias else {},
        compiler_params=_params(("arbitrary",)),
    )(*args)
    return res[:n_own_out], res[n_own_out:]


def _adamw_sharded(w, gparts, m, v, tr, name, riders=()):
    R, C = w.shape
    n_part = gparts.shape[0]
    nt = R // tr
    n_rid = len(riders)

    def body(w_ref, gp_ref, m_ref, v_ref, *rest):
        rid_src = rest[:n_rid]
        g_ref, d_ref, m2_ref, v2_ref = rest[n_rid:n_rid + 4]
        rid_dst = rest[n_rid + 4:2 * n_rid + 4]
        rid_sems = rest[2 * n_rid + 4:]

        @pl.when(pl.program_id(0) == 0)
        def _():
            for r, (ex, _) in enumerate(riders):
                ex.start(rid_src[r], rid_dst[r], *rid_sems[3 * r:3 * r + 3])

        g = gp_ref[0].astype(F32)
        for d in range(1, n_part):
            g = g + gp_ref[d].astype(F32)
        delta, m2, v2 = _adam(w_ref[...], g, m_ref[...], v_ref[...])
        g_ref[...] = g
        d_ref[...] = delta
        m2_ref[...] = m2
        v2_ref[...] = v2

        @pl.when(pl.program_id(0) == nt - 1)
        def _():
            for r, (ex, _) in enumerate(riders):
                ex.wait(rid_src[r], rid_dst[r], *rid_sems[3 * r:3 * r + 3])

    row = pl.BlockSpec((tr, C), lambda i: (i, 0))
    sh = jax.ShapeDtypeStruct((R, C), F32)
    res = pl.pallas_call(
        body, name=name, grid=(nt,),
        in_specs=[row, pl.BlockSpec((n_part, tr, C), lambda i: (0, i, 0)), row, row] + [ANY_SPEC] * n_rid,
        out_specs=[row, row, row, row] + [ANY_SPEC] * n_rid,
        out_shape=[sh, sh, sh, sh] + [ex.out_shape for ex, _ in riders],
        scratch_shapes=[s for ex, _ in riders for s in ex.scratch],
        compiler_params=_params(("arbitrary",)),
    )(w, gparts, m, v, *[a for _, a in riders])
    return res[:4], res[4:]


def _pack_vectors(vec_rows, dsg, dgq, dgk, dgk_c, loss_part):
    flat = [a for _, arrs in vec_rows for a in arrs]

    def body(*refs):
        vecs = list(refs[:len(flat)])
        dsg_ref, dgq_ref, dgk_ref, dgkc_ref, loss_ref, v_ref = refs[len(flat):]
        row = lax.broadcasted_iota(jnp.int32, (16, D_MODEL), 0)
        misc = jnp.concatenate([dsg_ref[...], dgq_ref[0:1, :], dgk_ref[0:1, :], dgkc_ref[0:1, :],
                                loss_ref[...]], axis=1)
        v = jnp.where(row == V_MISC, jnp.broadcast_to(misc, (16, D_MODEL)), 0.0)
        for r, arrs in vec_rows:
            val = vecs.pop(0)[...]
            for _ in arrs[1:]:
                val = val + vecs.pop(0)[...]
            v = jnp.where(row == r, jnp.broadcast_to(val, (16, D_MODEL)), v)
        v_ref[...] = v

    return pl.pallas_call(
        body, name="pack_vectors", out_shape=jax.ShapeDtypeStruct((16, D_MODEL), F32), compiler_params=_params(),
    )(*flat, dsg, dgq, dgk, dgk_c, loss_part)


SMALL_NAMES = ("b_ada", "norm_g", "sgu_norm_g", "w_spatial", "b_spatial", "q_norm_g", "k_norm_g", "rpb")


def _adamw_small(vg, sg, rg, ws, ms, vs):
    k = len(SMALL_NAMES)

    def body(*refs):
        vg_ref, sg_ref, rg_ref = refs[0], refs[1], refs[2]
        refs = refs[1:]
        w_refs = dict(zip(SMALL_NAMES, refs[2:2 + k]))
        m_refs = dict(zip(SMALL_NAMES, refs[2 + k:2 + 2 * k]))
        v_refs = dict(zip(SMALL_NAMES, refs[2 + 2 * k:2 + 3 * k]))
        o_refs = [dict(zip(SMALL_NAMES, refs[2 + (3 + i) * k:2 + (4 + i) * k])) for i in range(4)]
        loss_ref = refs[2 + 7 * k]

        sv = vg_ref[0]
        for d in range(1, N_DEV):
            sv = sv + vg_ref[d]
        loss_ref[...] = sv[V_MISC:V_MISC + 1, 896:1024]

        def total(lo, hi, ref=sg_ref):
            s = ref[0, lo:hi, :].astype(F32)
            for d in range(1, N_DEV):
                s = s + ref[d, lo:hi, :].astype(F32)
            return s

        def emit(name, idx, g):
            res = _adam(w_refs[name][idx], g, m_refs[name][idx], v_refs[name][idx])
            for o, val in zip(o_refs, (g,) + res):
                o[name][idx] = val

        everything = (slice(None), slice(None))
        row = lambda r: sv[r:r + 1, :]
        emit("b_ada", everything, jnp.concatenate(
            [row(V_DSHIFT) + row(V_DCSHIFT), row(V_DSCALE) + row(V_DCSCALE), row(V_DGATE)], axis=1))
        emit("norm_g", everything, row(V_DNG) + row(V_DNG_CTX))
        misc = row(V_MISC)
        emit("sgu_norm_g", everything, misc[:, 0:512])
        emit("q_norm_g", everything, misc[:, 512:512 + HEAD_DIM])
        emit("k_norm_g", everything, misc[:, 640:640 + HEAD_DIM] + misc[:, 768:768 + HEAD_DIM])
        for g in range(SGU_GROUPS):
            emit("w_spatial", (0, g), total(128 * g, 128 * (g + 1)))
            emit("b_spatial", (0, slice(g, g + 1), slice(None)), total(M_DBS + 8 * g, M_DBS + 8 * (g + 1))[0:1, :])
        for hd in range(N_HEADS):
            by_dc = total(32 * hd, 32 * (hd + 1), rg_ref)
            emit("rpb", (0, hd), by_dc.T[0:2 * WIN_R - 1, 0:2 * WIN_C - 1])

    shapes = [jax.ShapeDtypeStruct(w.shape, F32) for w in ws]
    res = pl.pallas_call(body, name="adamw_small", out_shape=shapes * 4 + [jax.ShapeDtypeStruct((1, 128), F32)],
                         compiler_params=_params())(vg, sg, rg, *ws, *ms, *vs)
    return [res[i * k:(i + 1) * k] for i in range(4)], res[4 * k]


def _adamw_cctx(pc_g, w, m, v):
    def body(pc_ref, w_ref, m_ref, v_ref, g_ref, d_ref, m2_ref, v2_ref):
        pc = pc_ref[0, 0:1, :]
        for d in range(1, N_DEV):
            pc = pc + pc_ref[d, 0:1, :]
        cc = w_ref[...]
        sig = _sigmoid(cc)
        g = pc * (sig * (1.0 + cc * (1.0 - sig)))
        delta, m2, v2 = _adam(cc, g, m_ref[...], v_ref[...])
        g_ref[...] = g
        d_ref[...] = delta
        m2_ref[...] = m2
        v2_ref[...] = v2

    sh = jax.ShapeDtypeStruct((1, D_MODEL), F32)
    return pl.pallas_call(body, name="adamw_cctx", out_shape=[sh, sh, sh, sh], compiler_params=_params())(
        pc_g, w, m, v)


def _block_ones(n, blk):
    i = np.arange(n)
    return jnp.asarray((i[:, None] // blk == i[None, :] // blk).astype(np.float32), BF16)


def _rpb_pairs(rpb):
    n_off = 2 * WIN_C - 1
    cols = np.arange(GRID_W)
    c0 = np.clip(cols - WIN_C // 2, 0, GRID_W - WIN_C)
    in_win = (cols[None, :] >= c0[:, None]) & (cols[None, :] < c0[:, None] + WIN_C)
    dc = np.clip(cols[None, :] - cols[:, None] + (WIN_C - 1), 0, n_off - 1)
    expand = (dc[None] == np.arange(n_off)[:, None, None]) & in_win[None]
    toep = jnp.einsum("hrd,dqk->hrqk", rpb, jnp.asarray(expand, F32), precision=HI)
    toep = toep + jnp.asarray(np.where(in_win, 0.0, NEG_INF).astype(np.float32))
    neg = jnp.full((N_HEADS, 1, GRID_W, GRID_W), NEG_INF, F32)
    ext = jnp.concatenate([neg, toep, neg], axis=1)
    return jnp.concatenate([ext[:, :-1], ext[:, 1:]], axis=-1)


def _row_mask(rows):
    nrb = rows // Q_ROWS
    valid = np.zeros((3, Q_ROWS, 1, K_ROWS, 1), bool)
    for t, rb in enumerate((0, 1, nrb - 1)):
        kb = int(np.clip(Q_ROWS * rb - 4, 0, rows - K_ROWS))
        for i in range(Q_ROWS):
            r0 = int(np.clip(Q_ROWS * rb + i - WIN_R // 2, 0, rows - WIN_R))
            for j in range(K_ROWS):
                valid[t, i, 0, j, 0] = r0 <= kb + j < r0 + WIN_R
    full = np.broadcast_to(valid, (3, Q_ROWS, GRID_W, K_ROWS, GRID_W)).reshape(3, TQ, TK)
    return jnp.asarray(np.where(full, 0.0, NEG_INF).astype(np.float32))


def kernel(x, c, ctx, c_ctx, w_ada, b_ada, norm_g, w_in, sgu_norm_g, w_spatial, b_spatial, q_norm_g, k_norm_g, rpb, w_out, loss_target, m_c_ctx, m_w_ada, m_b_ada, m_norm_g, m_w_in, m_sgu_norm_g, m_w_spatial, m_b_spatial, m_q_norm_g, m_k_norm_g, m_rpb, m_w_out, v_c_ctx, v_w_ada, v_b_ada, v_norm_g, v_w_in, v_sgu_norm_g, v_w_spatial, v_b_spatial, v_q_norm_g, v_k_norm_g, v_rpb, v_w_out):
    me = 4 * lax.axis_index("x") + 2 * lax.axis_index("y") + lax.axis_index("c")
    x2, ctx2, tgt2 = x[0], ctx[0], loss_target[0]
    T, C = x2.shape[0], ctx2.shape[0]
    rows = T // GRID_W
    wada, win_t, wout = w_ada[0], w_in[0].T, w_out[0]
    ada_w = wada.shape[1]
    win_w = win_t.shape[0]

    row8 = lax.broadcasted_iota(jnp.int32, (8, D_MODEL), 0)
    c_blk = jnp.where(row8 == me, jnp.broadcast_to(c, (8, D_MODEL)), 0.0)
    b_sh = lax.dynamic_slice(b_ada, (0, me * ada_w), (1, ada_w))
    c_ctx_row = c_ctx.reshape(1, D_MODEL)

    ones512 = _block_ones(512, HEAD_DIM)
    ones8 = jnp.ones((8, 128), BF16)
    foldm = jnp.asarray((np.arange(512)[:, None] % HEAD_DIM == np.arange(128)[None, :]).astype(np.float32))
    lane_half = np.arange(128)[None, :, None] // GRID_W
    hsel = jnp.asarray((2 * np.arange(8)[:, None, None] + lane_half == np.arange(128)[None, None, :]).astype(np.float32),
                       BF16)
    foldr = jnp.asarray((np.arange(256)[None, :] // 8 == np.arange(32)[:, None]).astype(np.float32), BF16)
    gq512 = jnp.tile(q_norm_g, (1, N_HEADS))
    gk512 = jnp.tile(k_norm_g, (1, N_HEADS))
    ws = w_spatial[0]
    wst = ws.transpose(0, 2, 1)
    bsb = jnp.broadcast_to(b_spatial[0][:, :, None], (SGU_GROUPS, CHUNK, 128))
    pairs = _rpb_pairs(rpb[0])
    row_mask = _row_mask(rows)

    my_chip = me // 2
    order = jnp.stack([my_chip, my_chip ^ 2, my_chip ^ 1, my_chip ^ 3]).astype(jnp.int32)
    h, z, win_g, wout_g, s16, part_g = _inproj_fwd(order, x2, norm_g, c_blk, c_ctx_row, wada, b_sh,
                                                   win_t.astype(BF16), wout.astype(BF16), 512)
    w_in_b = win_g.reshape(D_IN, D_MODEL)
    w_out_b = wout_g.reshape(D_MODEL, D_MODEL)
    mod16 = part_g.transpose(1, 0, 2).reshape(16, 3 * D_MODEL)
    mod = lax.dynamic_slice(mod16, (me, 0), (1, 3 * D_MODEL))
    shift, scale, gate = mod[:, :D_MODEL], mod[:, D_MODEL:2 * D_MODEL], mod[:, 2 * D_MODEL:]
    cshift, cscale = mod16[8:9, :D_MODEL], mod16[8:9, D_MODEL:2 * D_MODEL]
    qs, kn = _qk_norm(z, ones512, gq512, gk512, 512)
    hc, zc, ckn = _ctx_fwd(ctx2, norm_g, cscale, cshift, w_in_b, ones512, gk512)
    out_a = _sgu_fwd(z, sgu_norm_g, ws, bsb, 512)
    ob, out_b, lse = _attn_fwd(qs, kn, z, ckn, zc, pairs, row_mask)

    dy, dmc, dw_out, dgate, loss_part = _outproj_loss_bwd(x2, tgt2, out_a, out_b, gate, w_out_b, 512)
    dw_out_blocks = dw_out.reshape(N_DEV, D_MODEL // N_DEV, D_MODEL)
    (dqs, dk, dv, dck, dcv, db_g, drpb), (gout_parts,) = _attn_bwd(
        qs, kn, z, ckn, zc, pairs, row_mask, ob, lse, dmc, hsel, foldr,
        [(_Hosted("a2a", dw_out_blocks), dw_out_blocks)])
    dzc_k, dzc_v, dgk_c = _ctx_k_bwd(zc, dck, dcv, ones512, gk512, foldm)
    rloc = drpb.reshape(N_HEADS * 32, 128)
    (dz, chip_sums, dws, dbs, dsg, dgq, dgk), (rg,) = _bwd_mid(
        z, dmc, dqs, dk, dv, db_g, h, hc, dzc_k, dzc_v, sgu_norm_g, ws, wst, bsb, ones8, ones512, gq512, gk512, foldm,
        256, [(_Hosted("ag", rloc), rloc)])
    sloc = jnp.concatenate([dws.reshape(SGU_GROUPS * CHUNK, CHUNK), dbs.reshape(SGU_GROUPS * 8, CHUNK)]).astype(BF16)
    n_tile = T // 512
    n_first = n_tile - max(n_tile // 4, 1)
    (gx_part, dshift_a, dscale_a, dng_a), (gin_parts,) = _inproj_bwd_dx(
        [dz], 0, w_in_b, x2, dy, norm_g, scale, shift, 512, "inproj_bwd_dx",
        [(_Hosted("chips", chip_sums), chip_sums)], tiles=(0, n_first))
    (grad_x, dshift_b, dscale_b, dng_b), _ = _inproj_bwd_dx(
        [dz], 0, w_in_b, x2, dy, norm_g, scale, shift, 512, "inproj_bwd_dx_tail",
        tiles=(n_first, n_tile - n_first), into=gx_part)
    (dcshift, dcscale, dng_c), _ = _inproj_bwd_dx([dzc_k, dzc_v], 4 * 512, w_in_b, ctx2, None, norm_g, cscale,
                                                  cshift, C, "ctx_bwd_dx")

    zero_row = jnp.zeros((1, D_MODEL), F32)
    vec_rows = [(V_DSHIFT, [dshift_a, dshift_b]), (V_DSCALE, [dscale_a, dscale_b]), (V_DGATE, [dgate]),
                (V_DCSHIFT, [dcshift]), (V_DCSCALE, [dcscale]), (V_ZERO, [zero_row]), (V_DNG, [dng_a, dng_b]),
                (V_DNG_CTX, [dng_c])]
    vloc = _pack_vectors(vec_rows, dsg, dgq, dgk, dgk_c, loss_part)
    res_in, (vg, sg) = _adamw_sharded(win_t, gin_parts, m_w_in[0].T, v_w_in[0].T, 112, "adamw_w_in",
                                      [(_Hosted("ag", vloc), vloc), (_Hosted("ag", sloc), sloc)])
    small_w = (b_ada, norm_g, sgu_norm_g, w_spatial, b_spatial, q_norm_g, k_norm_g, rpb)
    small_m = (m_b_ada, m_norm_g, m_sgu_norm_g, m_w_spatial, m_b_spatial, m_q_norm_g, m_k_norm_g, m_rpb)
    small_v = (v_b_ada, v_norm_g, v_sgu_norm_g, v_w_spatial, v_b_spatial, v_q_norm_g, v_k_norm_g, v_rpb)
    res_small, loss_row = _adamw_small(vg, sg, rg, small_w, small_m, small_v)

    dm_all = vg[:, V_DSHIFT:V_DGATE + 1, :].reshape(N_DEV, 3 * D_MODEL)
    dc_all = vg[:, V_DCSHIFT:V_ZERO + 1, :].reshape(N_DEV, 3 * D_MODEL)
    dm_sh = lax.dynamic_slice(dm_all, (0, me * ada_w), (N_DEV, ada_w))
    dc_sh = lax.dynamic_slice(dc_all, (0, me * ada_w), (N_DEV, ada_w))
    *res_ada, pc = _ada_bwd(s16, dm_sh, dc_sh, wada, m_w_ada[0], v_w_ada[0])
    res_out, (pc_g,) = _adamw_sharded(wout, gout_parts, m_w_out[0], v_w_out[0], 128, "adamw_w_out",
                                      [(_Hosted("ag", pc), pc)])
    res_cctx = _adamw_cctx(pc_g, c_ctx_row, m_c_ctx.reshape(1, D_MODEL), v_c_ctx.reshape(1, D_MODEL))

    loss = loss_row[0, 0]
    outs = [loss, grad_x[None]]
    for kind in range(4):
        by_name = dict(zip(SMALL_NAMES, res_small[kind]))
        by_name.update(c_ctx=res_cctx[kind].reshape(D_MODEL), w_ada=res_ada[kind][None],
                       w_in=res_in[kind].T[None], w_out=res_out[kind][None])
        outs += [by_name[nme] for nme in ("c_ctx", "w_ada", "b_ada", "norm_g", "w_in", "sgu_norm_g", "w_spatial",
                                          "b_spatial", "q_norm_g", "k_norm_g", "rpb", "w_out")]
    return tuple(outs)
```

```python
import functools

import numpy as np
import jax
import jax.numpy as jnp
from jax import lax
from jax.experimental import pallas as pl
from jax.experimental.pallas import tpu as pltpu

F32 = jnp.float32
BF16 = jnp.bfloat16
HI = lax.Precision.HIGHEST

N_DEV = 8
D_MODEL = 1024
D_A = 512
D_B = 512
D_IN = 3584
N_BRANCH = 7
HEAD_DIM = 64
N_HEADS = 8
GRID_W = 64
WIN_R = 8
WIN_C = 16
CHUNK = 128
SGU_GROUPS = 4
EPS = 1e-6
NEG_INF = -1e30
Q_ROWS = 4
K_ROWS = 12
TQ = Q_ROWS * GRID_W
TK = K_ROWS * GRID_W
N_DIAG = 22
ATT_SUB = 4
ATT_SCALE = HEAD_DIM ** -0.5
LOG2E = 1.4426950408889634
LN2 = 0.6931471805599453

ADAM_LR = 0.001
ADAM_B1 = 0.9
ADAM_B2 = 0.999
ADAM_EPS = 1e-08
ADAM_WD = 0.01
ADAM_STEP = 10

VMEM_LIMIT = 56 * 1024 * 1024
MESH = pl.DeviceIdType.MESH

V_DSHIFT, V_DSCALE, V_DGATE, V_DCSHIFT, V_DCSCALE, V_ZERO, V_DNG, V_DNG_CTX, V_MISC = range(9)
M_DBS = 512


def _params(sem=None):
    return pltpu.CompilerParams(dimension_semantics=sem, vmem_limit_bytes=VMEM_LIMIT)


def _sigmoid(x):
    return 1.0 / (1.0 + jnp.exp(-x))


def _gelu_parts(x):
    cdf = 0.5 * (1.0 + lax.erf(x * 0.7071067811865476))
    pdf = jnp.exp(-0.5 * x * x) * 0.3989422804014327
    return x * cdf, cdf + x * pdf


def _nt(a, b):
    return lax.dot_general(a, b, (((1,), (1,)), ((), ())), preferred_element_type=F32)


def _tn(a, b):
    return lax.dot_general(a, b, (((0,), (0,)), ((), ())), preferred_element_type=F32)


def _dot2(v, ones_bf):
    hi = v.astype(BF16)
    lo = (v - hi.astype(F32)).astype(BF16)
    return (jnp.dot(hi, ones_bf, preferred_element_type=F32)
            + jnp.dot(lo, ones_bf, preferred_element_type=F32))


def _head_sum(v, ones_ref):
    return jnp.dot(v.astype(BF16), ones_ref[...], preferred_element_type=F32)


def _adam(w, g, m, v):
    m2 = ADAM_B1 * m + (1.0 - ADAM_B1) * g
    v2 = ADAM_B2 * v + (1.0 - ADAM_B2) * (g * g)
    m_hat = m2 / (1.0 - ADAM_B1 ** ADAM_STEP)
    v_hat = v2 / (1.0 - ADAM_B2 ** ADAM_STEP)
    delta = -ADAM_LR * (m_hat / (jnp.sqrt(v_hat) + ADAM_EPS) + ADAM_WD * w)
    return delta, m2, v2


class _Hosted:
    def __init__(self, kind, src):
        self.kind = kind
        n_slot = {"a2a": N_DEV, "ag": N_DEV, "chips": N_DEV // 2}[kind]
        blk = src.shape if kind == "ag" else src.shape[1:]
        self.out_shape = jax.ShapeDtypeStruct((n_slot,) + tuple(blk), src.dtype)
        self.n_peer = n_slot - 1
        self.scratch = [pltpu.SemaphoreType.DMA((self.n_peer,)), pltpu.SemaphoreType.DMA((self.n_peer,)),
                        pltpu.SemaphoreType.DMA]

    def _copies(self, src, dst, send_sems, recv_sems, loc_sem, landing):
        x, y, c = lax.axis_index("x"), lax.axis_index("y"), lax.axis_index("c")
        if self.kind == "chips":
            me = 2 * x + y
            peers = [((px, py, c), 2 * px + py) for px, py in ((1 - x, y), (x, 1 - y), (1 - x, 1 - y))]
        else:
            me = 4 * x + 2 * y + c
            peers = []
            for k in range(1, N_DEV):
                px = 1 - x if (k >> 2) & 1 else x
                py = 1 - y if (k >> 1) & 1 else y
                pc = 1 - c if k & 1 else c
                peers.append(((px, py, pc), 4 * px + 2 * py + pc))
        remote = []
        for k, (peer, pid) in enumerate(peers):
            s = src if self.kind == "ag" else src.at[pid]
            remote.append(pltpu.make_async_remote_copy(
                src_ref=s, dst_ref=dst.at[pid if landing else me],
                send_sem=send_sems.at[k], recv_sem=recv_sems.at[k], device_id=peer, device_id_type=MESH))
        local = pltpu.make_async_copy(src if self.kind == "ag" else src.at[me], dst.at[me], loc_sem)
        return remote, local

    def start(self, src, dst, send_sems, recv_sems, loc_sem):
        remote, local = self._copies(src, dst, send_sems, recv_sems, loc_sem, landing=False)
        for cp in remote:
            cp.start()
        local.start()

    def wait(self, src, dst, send_sems, recv_sems, loc_sem):
        remote, local = self._copies(src, dst, send_sems, recv_sems, loc_sem, landing=True)
        for cp in remote:
            cp.wait_recv()
        for cp in remote:
            cp.wait_send()
        local.wait()


ANY_SPEC = pl.BlockSpec(memory_space=pl.ANY)


def _ada_scratch(n_col):
    return ([pltpu.VMEM((N_DEV, 8, D_MODEL), F32), pltpu.VMEM((16, n_col), F32), pltpu.VMEM((N_DEV, 16, n_col), F32)]
            + [pltpu.SemaphoreType.DMA((N_DEV - 1,)) for _ in range(4)])


def _ada_modulation(cb_ref, cc_ref, w_ref, b_ref, cstack, part, parts, s1, r1, s2, r2):
    x, y, c = lax.axis_index("x"), lax.axis_index("y"), lax.axis_index("c")
    me = 4 * x + 2 * y + c
    peers = []
    for k in range(1, N_DEV):
        px = 1 - x if (k >> 2) & 1 else x
        py = 1 - y if (k >> 1) & 1 else y
        pc = 1 - c if k & 1 else c
        peers.append(((px, py, pc), 4 * px + 2 * py + pc))

    def exchange(src, dst, send_sems, recv_sems):
        for k, (peer, _) in enumerate(peers):
            pltpu.make_async_remote_copy(src_ref=src, dst_ref=dst.at[me], send_sem=send_sems.at[k],
                                         recv_sem=recv_sems.at[k], device_id=peer, device_id_type=MESH).start()
        dst[me] = src[...]
        waits = [pltpu.make_async_remote_copy(src_ref=src, dst_ref=dst.at[pid], send_sem=send_sems.at[k],
                                              recv_sem=recv_sems.at[k], device_id=peer, device_id_type=MESH)
                 for k, (peer, pid) in enumerate(peers)]
        for cp in waits:
            cp.wait_recv()
        for cp in waits:
            cp.wait_send()

    exchange(cb_ref, cstack, s1, r1)
    c_all = cstack[0]
    for d in range(1, N_DEV):
        c_all = c_all + cstack[d]
    row = lax.broadcasted_iota(jnp.int32, (8, D_MODEL), 0)
    cc = jnp.where(row == 0, jnp.broadcast_to(cc_ref[...], (8, D_MODEL)), 0.0)
    call = jnp.concatenate([c_all, cc], axis=0)
    s = call * _sigmoid(call)
    part[...] = jnp.dot(s, w_ref[...], preferred_element_type=F32, precision=HI) + b_ref[...]
    exchange(part, parts, s2, r2)
    return s


def _ada_bwd(s16, dm, dc, w, m, v):
    def body(s_ref, dm_ref, dc_ref, w_ref, m_ref, v_ref, g_ref, d_ref, m2_ref, v2_ref, pc_ref):
        dct = jnp.sum(dc_ref[...], axis=0, keepdims=True)
        row = lax.broadcasted_iota(jnp.int32, dc_ref.shape, 0)
        dcb = jnp.where(row == 0, jnp.broadcast_to(dct, dc_ref.shape), 0.0)
        dm16 = jnp.concatenate([dm_ref[...], dcb], axis=0)
        g = lax.dot_general(s_ref[...], dm16, (((0,), (0,)), ((), ())),
                            preferred_element_type=F32, precision=HI)
        w_ = w_ref[...]
        delta, m2, v2 = _adam(w_, g, m_ref[...], v_ref[...])
        g_ref[...] = g
        d_ref[...] = delta
        m2_ref[...] = m2
        v2_ref[...] = v2
        pc_ref[...] = lax.dot_general(dcb, w_, (((1,), (1,)), ((), ())),
                                      preferred_element_type=F32, precision=HI)

    sh = jax.ShapeDtypeStruct(w.shape, F32)
    return pl.pallas_call(
        body, name="ada_bwd",
        out_shape=[sh, sh, sh, sh, jax.ShapeDtypeStruct((8, D_MODEL), F32)],
        compiler_params=_params(),
    )(s16, dm, dc, w, m, v)


def _head_norm(zk, ones_ref, gain):
    ss = _head_sum(zk * zk, ones_ref)
    return zk * lax.rsqrt(ss * (1.0 / HEAD_DIM) + EPS) * gain


def _inproj_fwd(order, x, ng, c_blk, c_ctx_row, w_ada_sh, b_ada_sh, w_blk_t, w_out_blk, tm):
    T = x.shape[0]
    nt = T // tm
    n_pass = N_DEV // 2
    blk_rows = w_blk_t.shape[0]
    n_col = w_ada_sh.shape[1]
    rider = _Hosted("ag", w_out_blk)
    n_ada = len(_ada_scratch(n_col))

    def body(order_ref, x_ref, g_ref, cb_ref, cc_ref, wa_ref, ba_ref, wb_ref, wo_ref,
             h_out, z_ref, wt_out, wo_out, s_out, parts_out,
             hs, wt, modv, send_sems, recv_sems, loc_sem, h_sem, wt_sem, *rest):
        ada_sc, rider_sems = rest[:n_ada], rest[n_ada:]
        p, i = pl.program_id(0), pl.program_id(1)
        x, y, c = lax.axis_index("x"), lax.axis_index("y"), lax.axis_index("c")
        me, sib = (x, y, c), (x, y, 1 - c)
        chips = [(1 - x, y), (x, 1 - y), (1 - x, 1 - y)]

        def slot(px, py, pc):
            return 4 * px + 2 * py + pc

        def copy(k, block, to, src=None):
            return pltpu.make_async_remote_copy(
                src_ref=wt.at[slot(*block)] if src is None else src, dst_ref=wt.at[slot(*block)],
                send_sem=send_sems.at[k], recv_sem=recv_sems.at[k], device_id=to, device_id_type=MESH)

        own = pltpu.make_async_copy(wb_ref, wt.at[slot(*me)], loc_sem)
        h_copy = pltpu.make_async_copy(hs, h_out, h_sem)
        wt_copy = pltpu.make_async_copy(wt, wt_out, wt_sem)
        first = [copy(1 + j, me, (*chip, c), src=wb_ref) for j, chip in enumerate(chips)] + [copy(0, me, sib, src=wb_ref)]
        passed = [copy(4 + j, (*chip, c), sib) for j, chip in enumerate(chips)]

        @pl.when(jnp.logical_and(p == 0, i == 0))
        def _():
            s_out[...] = _ada_modulation(cb_ref, cc_ref, wa_ref, ba_ref, *ada_sc)
            own.start()
            for cp in first:
                cp.start()
            rider.start(wo_ref, wo_out, *rider_sems)
            parts = ada_sc[2]
            parts_out[...] = parts[...]
            my_row = pl.ds(slot(*me), 1)
            mod = jnp.concatenate([parts[d, my_row, :] for d in range(N_DEV)], axis=1)
            modv[0:1, :] = mod[:, 0:D_MODEL]
            modv[1:2, :] = mod[:, D_MODEL:2 * D_MODEL]
            own.wait()
            copy(0, sib, me).wait_recv()

        for j, chip in enumerate(chips):
            @pl.when(jnp.logical_and(p == j + 1, i == 0))
            def _(j=j, chip=chip):
                copy(1 + j, (*chip, c), me).wait_recv()
                passed[j].start()
                copy(4 + j, (*chip, 1 - c), me).wait_recv()
                if j == len(chips) - 1:
                    wt_copy.start()

        rows = pl.ds(pl.multiple_of(i * tm, tm), tm)

        @pl.when(p == 0)
        def _():
            xv = x_ref[...]
            r = lax.rsqrt(jnp.mean(xv * xv, axis=-1, keepdims=True) + EPS)
            hs[rows, :] = ((xv * r * g_ref[...]) * (1.0 + modv[1:2, :]) + modv[0:1, :]).astype(BF16)

        @pl.when(jnp.logical_and(p == 1, i == 0))
        def _():
            h_copy.start()

        w_pair = wt[pl.ds(2 * order_ref[p], 2)].reshape(2 * blk_rows, D_MODEL)
        z_ref[...] = _nt(hs[rows, :], w_pair).astype(BF16)

        @pl.when(jnp.logical_and(p == n_pass - 1, i == nt - 1))
        def _():
            for cp in first + passed:
                cp.wait_send()
            h_copy.wait()
            wt_copy.wait()
            rider.wait(wo_ref, wo_out, *rider_sems)

    whole = lambda shape: pl.BlockSpec(shape, lambda p, i, o: (0,) * len(shape))
    grid_spec = pltpu.PrefetchScalarGridSpec(
        num_scalar_prefetch=1, grid=(n_pass, nt),
        in_specs=[pl.BlockSpec((tm, D_MODEL), lambda p, i, o: (jnp.where(p == 0, i, nt - 1), 0)),
                  whole((1, D_MODEL)), whole((8, D_MODEL)), whole((1, D_MODEL)), whole((D_MODEL, n_col)),
                  whole((1, n_col)), ANY_SPEC, ANY_SPEC],
        out_specs=[ANY_SPEC, pl.BlockSpec((tm, 2 * blk_rows), lambda p, i, o: (i, o[p])), ANY_SPEC, ANY_SPEC,
                   whole((16, D_MODEL)), whole((N_DEV, 16, n_col))],
        scratch_shapes=[pltpu.VMEM((T, D_MODEL), BF16), pltpu.VMEM((N_DEV, blk_rows, D_MODEL), BF16),
                        pltpu.VMEM((8, D_MODEL), F32),
                        pltpu.SemaphoreType.DMA((7,)), pltpu.SemaphoreType.DMA((7,)), pltpu.SemaphoreType.DMA,
                        pltpu.SemaphoreType.DMA, pltpu.SemaphoreType.DMA] + _ada_scratch(n_col) + rider.scratch)
    return pl.pallas_call(
        body, name="inproj_fwd", grid_spec=grid_spec,
        out_shape=[jax.ShapeDtypeStruct((T, D_MODEL), BF16), jax.ShapeDtypeStruct((T, D_IN), BF16),
                   jax.ShapeDtypeStruct((N_DEV, blk_rows, D_MODEL), BF16), rider.out_shape,
                   jax.ShapeDtypeStruct((16, D_MODEL), F32), jax.ShapeDtypeStruct((N_DEV, 16, n_col), F32)],
        compiler_params=_params(("arbitrary", "arbitrary")),
    )(order, x, ng, c_blk, c_ctx_row, w_ada_sh, b_ada_sh, w_blk_t, w_out_blk)


def _qk_norm(z, ones_blk, gq, gk, tm):
    T = z.shape[0]

    def body(q_ref, k_ref, ones_ref, gq_ref, gk_ref, qs_ref, kn_ref):
        qs = _head_norm(q_ref[...].astype(F32), ones_ref, gq_ref[...]) * (ATT_SCALE * LOG2E)
        qs_ref[...] = qs.astype(BF16)
        kn_ref[...] = _head_norm(k_ref[...].astype(F32), ones_ref, gk_ref[...]).astype(BF16)

    v512 = pl.BlockSpec((1, 512), lambda i: (0, 0))
    return pl.pallas_call(
        body, name="qk_norm", grid=(T // tm,),
        in_specs=[pl.BlockSpec((tm, 512), lambda i: (i, 3)), pl.BlockSpec((tm, 512), lambda i: (i, 4)),
                  pl.BlockSpec((512, 512), lambda i: (0, 0)), v512, v512],
        out_specs=[pl.BlockSpec((tm, 512), lambda i: (i, 0)), pl.BlockSpec((tm, 512), lambda i: (i, 0))],
        out_shape=[jax.ShapeDtypeStruct((T, 512), BF16), jax.ShapeDtypeStruct((T, 512), BF16)],
        compiler_params=_params(("arbitrary",)),
    )(z, z, ones_blk, gq, gk)


def _ctx_fwd(ctx, ng, cscale, cshift, w_in_t, ones_blk, gk):
    C = ctx.shape[0]

    def body(x_ref, g_ref, sc_ref, sh_ref, w_ref, ones_ref, gk_ref, h_ref, z_ref, kn_ref):
        xv = x_ref[...]
        r = lax.rsqrt(jnp.mean(xv * xv, axis=-1, keepdims=True) + EPS)
        h = (xv * r * g_ref[...]) * (1.0 + sc_ref[...]) + sh_ref[...]
        hb = h.astype(BF16)
        h_ref[...] = hb
        zk = _nt(hb, w_ref[0:512, :])
        zv = _nt(hb, w_ref[512:1024, :])
        z_ref[:, 0:512] = zk.astype(BF16)
        z_ref[:, 512:1024] = zv.astype(BF16)
        kn_ref[...] = _head_norm(zk, ones_ref, gk_ref[...]).astype(BF16)

    vec = pl.BlockSpec((1, D_MODEL), lambda i: (0, 0))
    return pl.pallas_call(
        body, name="ctx_fwd", grid=(1,),
        in_specs=[pl.BlockSpec((C, D_MODEL), lambda i: (0, 0)), vec, vec, vec,
                  pl.BlockSpec((1024, D_MODEL), lambda i: (2, 0)),
                  pl.BlockSpec((512, 512), lambda i: (0, 0)), pl.BlockSpec((1, 512), lambda i: (0, 0))],
        out_specs=[pl.BlockSpec((C, D_MODEL), lambda i: (0, 0)), pl.BlockSpec((C, 1024), lambda i: (0, 0)),
                   pl.BlockSpec((C, 512), lambda i: (0, 0))],
        out_shape=[jax.ShapeDtypeStruct((C, D_MODEL), BF16), jax.ShapeDtypeStruct((C, 1024), BF16),
                   jax.ShapeDtypeStruct((C, 512), BF16)],
        compiler_params=_params(("arbitrary",)),
    )(ctx, ng, cscale, cshift, w_in_t, ones_blk, gk)


def _sgu_chunk_fwd(au, av, ag, sg, ws_bf, bsb):
    gu, dgu = _gelu_parts(au)
    gv, dgv = _gelu_parts(av)
    rr = lax.rsqrt(jnp.mean(gv * gv, axis=-1, keepdims=True) + EPS)
    vhat = gv * rr
    vn = vhat * sg
    mixed = jnp.dot(ws_bf, vn.astype(BF16), preferred_element_type=F32) + bsb
    sig = _sigmoid(ag)
    sl = ag * sig
    return gu * mixed * sl, (gu, dgu, dgv, rr, vhat, vn, mixed, sig, sl)


def _sgu_fwd(z, sgn, ws, bsb, tm):
    T = z.shape[0]

    def body(au_ref, av_ref, ag_ref, sg_ref, ws_ref, bsb_ref, o_ref):
        for g in range(SGU_GROUPS):
            ws_bf = ws_ref[g].astype(BF16)
            sg = sg_ref[:, 128 * g:128 * (g + 1)]
            bsb_g = bsb_ref[g]
            for j in range(tm // CHUNK):
                rs, cs = slice(CHUNK * j, CHUNK * (j + 1)), slice(128 * g, 128 * (g + 1))
                out, _ = _sgu_chunk_fwd(au_ref[rs, cs].astype(F32), av_ref[rs, cs].astype(F32),
                                        ag_ref[rs, cs].astype(F32), sg, ws_bf, bsb_g)
                o_ref[rs, cs] = out.astype(BF16)

    return pl.pallas_call(
        body, name="sgu_fwd", grid=(T // tm,),
        in_specs=[pl.BlockSpec((tm, 512), lambda i: (i, 0)), pl.BlockSpec((tm, 512), lambda i: (i, 1)),
                  pl.BlockSpec((tm, 512), lambda i: (i, 2)), pl.BlockSpec((1, 512), lambda i: (0, 0)),
                  pl.BlockSpec((SGU_GROUPS, CHUNK, CHUNK), lambda i: (0, 0, 0)),
                  pl.BlockSpec((SGU_GROUPS, CHUNK, 128), lambda i: (0, 0, 0))],
        out_specs=pl.BlockSpec((tm, 512), lambda i: (i, 0)),
        out_shape=jax.ShapeDtypeStruct((T, 512), BF16),
        compiler_params=_params(("arbitrary",)),
    )(z, z, z, sgn, ws, bsb)


def _attn_type(rb, nrb):
    return jnp.where(rb == 0, 0, jnp.where(rb == nrb - 1, 2, 1))


def _attn_specs(T, C):
    return [
        pl.BlockSpec((ATT_SUB * TQ, 128), lambda hp, st: (st, hp)),
        pl.BlockSpec((T, 128), lambda hp, st: (0, hp)),
        pl.BlockSpec((T, 128), lambda hp, st: (0, 20 + hp)),
        pl.BlockSpec((C, 128), lambda hp, st: (0, hp)),
        pl.BlockSpec((C, 128), lambda hp, st: (0, 4 + hp)),
        pl.BlockSpec((2, 2 * WIN_R, GRID_W, 128), lambda hp, st: (hp, 0, 0, 0)),
        pl.BlockSpec((3, TQ, TK), lambda hp, st: (0, 0, 0)),
        pl.BlockSpec((ATT_SUB * TQ, 128), lambda hp, st: (st, 24 + hp)),
    ]


def _build_bias(pairs_ref, mask_ref, bias_sc):
    for t in range(3):
        for hh in range(2):
            for i in range(Q_ROWS):
                for mm in range(K_ROWS // 2):
                    p = min(max(WIN_R - Q_ROWS * t + 2 * mm - i, 0), 2 * WIN_R - 1)
                    rs, cs = slice(GRID_W * i, GRID_W * (i + 1)), slice(128 * mm, 128 * (mm + 1))
                    bias_sc[t, hh, rs, cs] = (pairs_ref[hh, p] + mask_ref[t, rs, cs]) * LOG2E


def _attn_fwd(qs, kn, z, ckn, zc, pairs, row_mask):
    T, C = qs.shape[0], ckn.shape[0]
    rows = T // GRID_W
    nrb = rows // Q_ROWS

    def body(q_ref, k_ref, v_ref, ck_ref, cv_ref, pairs_ref, mask_ref, bg_ref, ob_ref, outb_ref, lse_ref, bias_sc):
        @pl.when(pl.program_id(1) == 0)
        def _():
            _build_bias(pairs_ref, mask_ref, bias_sc)

        ck2, cv2 = ck_ref[...], cv_ref[...]
        lane = lax.broadcasted_iota(jnp.int32, (1, 128), 1)
        for sub in range(ATT_SUB):
            rb = ATT_SUB * pl.program_id(1) + sub
            bias_ref = bias_sc.at[_attn_type(rb, nrb)]
            rs = slice(TQ * sub, TQ * (sub + 1))
            ks = pl.multiple_of(jnp.clip(Q_ROWS * rb - 4, 0, rows - K_ROWS) * GRID_W, GRID_W)
            q2 = q_ref[rs, :]
            k2 = k_ref[pl.ds(ks, TK), :]
            v2 = v_ref[pl.ds(ks, TK), :]
            o_acc = jnp.zeros((TQ, 128), F32)
            lse_acc = jnp.zeros((TQ, 128), F32)
            for hh in range(2):
                msk = (lane >= HEAD_DIM) == bool(hh)
                qm = jnp.where(msk, q2, jnp.zeros_like(q2))
                s = _nt(qm, k2) + bias_ref[hh]
                sc = _nt(qm, ck2)
                m = jnp.maximum(jnp.max(s, axis=-1, keepdims=True), jnp.max(sc, axis=-1, keepdims=True))
                p = jnp.exp2(s - m)
                pc = jnp.exp2(sc - m)
                va = jnp.where(msk, v2, jnp.ones_like(v2))
                cva = jnp.where(msk, cv2, jnp.ones_like(cv2))
                num = (jnp.dot(p.astype(BF16), va, preferred_element_type=F32)
                       + jnp.dot(pc.astype(BF16), cva, preferred_element_type=F32))
                den = pltpu.roll(num, HEAD_DIM, 1)
                o_acc = jnp.where(msk, num / den, o_acc)
                lse_acc = jnp.where(msk, m + jnp.log(den) * LOG2E, lse_acc)
            ob_ref[rs, :] = o_acc.astype(BF16)
            lse_ref[rs, :] = lse_acc
            bg = bg_ref[rs, :].astype(F32)
            outb_ref[rs, :] = (o_acc * (bg * _sigmoid(bg))).astype(BF16)

    tile = pl.BlockSpec((ATT_SUB * TQ, 128), lambda hp, st: (st, hp))
    return pl.pallas_call(
        body, name="attn_fwd", grid=(4, nrb // ATT_SUB),
        in_specs=_attn_specs(T, C),
        out_specs=[tile, tile, tile],
        out_shape=[jax.ShapeDtypeStruct((T, 512), BF16), jax.ShapeDtypeStruct((T, 512), BF16),
                   jax.ShapeDtypeStruct((T, 512), F32)],
        scratch_shapes=[pltpu.VMEM((3, 2, TQ, TK), F32)],
        compiler_params=_params(("arbitrary", "arbitrary")),
    )(qs, kn, z, ckn, zc, pairs, row_mask, z)


def _outproj_loss_bwd(x, tgt, out_a, out_b, gate, w_out, tm):
    T = x.shape[0]
    nt = T // tm

    def body(x_ref, t_ref, oa_ref, ob_ref, gate_ref, w_ref, dy_ref, dmc_ref, dw_ref, dgate_ref, loss_ref, acc):
        @pl.when(pl.program_id(0) == 0)
        def _():
            acc[...] = jnp.zeros_like(acc)
            dgate_ref[...] = jnp.zeros_like(dgate_ref)
            loss_ref[...] = jnp.zeros_like(loss_ref)

        oa, ob = oa_ref[...], ob_ref[...]
        gate_v = gate_ref[...]
        mix = (jnp.dot(oa, w_ref[0:512, :], preferred_element_type=F32)
               + jnp.dot(ob, w_ref[512:1024, :], preferred_element_type=F32))
        e = x_ref[...] + gate_v * mix - t_ref[...]
        se = jnp.sum(jnp.sum(e * e, axis=0, keepdims=True), axis=1, keepdims=True)
        loss_ref[...] += jnp.broadcast_to(se * (0.5 / D_MODEL), loss_ref.shape)
        dy = e * (1.0 / D_MODEL)
        dy_ref[...] = dy
        dgate_ref[...] += jnp.sum(dy * mix, axis=0, keepdims=True)
        dmix = (dy * gate_v).astype(BF16)
        dmc_ref[...] = _nt(dmix, w_ref[...]).astype(BF16)
        acc[0:512, :] += _tn(oa, dmix)
        acc[512:1024, :] += _tn(ob, dmix)

        @pl.when(pl.program_id(0) == nt - 1)
        def _():
            dw_ref[...] = acc[...].astype(BF16)

    row = lambda w: pl.BlockSpec((tm, w), lambda i: (i, 0))
    return pl.pallas_call(
        body, name="outproj_loss_bwd", grid=(nt,),
        in_specs=[row(D_MODEL), row(D_MODEL), row(512), row(512),
                  pl.BlockSpec((1, D_MODEL), lambda i: (0, 0)),
                  pl.BlockSpec((D_MODEL, D_MODEL), lambda i: (0, 0))],
        out_specs=[row(D_MODEL), row(D_MODEL), pl.BlockSpec((D_MODEL, D_MODEL), lambda i: (0, 0)),
                   pl.BlockSpec((1, D_MODEL), lambda i: (0, 0)), pl.BlockSpec((1, 128), lambda i: (0, 0))],
        out_shape=[jax.ShapeDtypeStruct((T, D_MODEL), F32), jax.ShapeDtypeStruct((T, D_MODEL), BF16),
                   jax.ShapeDtypeStruct((D_MODEL, D_MODEL), BF16), jax.ShapeDtypeStruct((1, D_MODEL), F32),
                   jax.ShapeDtypeStruct((1, 128), F32)],
        scratch_shapes=[pltpu.VMEM((D_MODEL, D_MODEL), F32)],
        compiler_params=_params(("arbitrary",)),
    )(x, tgt, out_a, out_b, gate, w_out)


def _attn_bwd(qs, kn, z, ckn, zc, pairs, row_mask, ob, lse, dmc, hsel, fold, riders):
    T, C = qs.shape[0], ckn.shape[0]
    rows = T // GRID_W
    nrb = rows // Q_ROWS
    n_st = nrb // ATT_SUB
    n_rid = len(riders)

    def body(q_ref, k_ref, v_ref, ck_ref, cv_ref, pairs_ref, mask_ref, bg_ref, ob_ref, lse_ref, do_ref,
             hsel_ref, fold_ref, *rest):
        rid_src = rest[:n_rid]
        dq_ref, dk_ref, dv_ref, dck_ref, dcv_ref, dbg_ref, drpb_ref = rest[n_rid:n_rid + 7]
        rid_dst = rest[n_rid + 7:2 * n_rid + 7]
        bias_sc, dacc_ref = rest[2 * n_rid + 7:2 * n_rid + 9]
        rid_sems = rest[2 * n_rid + 9:]
        hp, st = pl.program_id(0), pl.program_id(1)

        @pl.when(jnp.logical_and(hp == 0, st == 0))
        def _():
            for r, (ex, _) in enumerate(riders):
                ex.start(rid_src[r], rid_dst[r], *rid_sems[3 * r:3 * r + 3])

        @pl.when(st == 0)
        def _():
            _build_bias(pairs_ref, mask_ref, bias_sc)
            dk_ref[...] = jnp.zeros_like(dk_ref)
            dv_ref[...] = jnp.zeros_like(dv_ref)
            dck_ref[...] = jnp.zeros_like(dck_ref)
            dcv_ref[...] = jnp.zeros_like(dcv_ref)
            dacc_ref[...] = jnp.zeros_like(dacc_ref)

        ck2, cv2 = ck_ref[...], cv_ref[...]
        lane = lax.broadcasted_iota(jnp.int32, (1, 128), 1)
        for sub in range(ATT_SUB):
            rb = ATT_SUB * st + sub
            bias_ref = bias_sc.at[_attn_type(rb, nrb)]
            rs = slice(TQ * sub, TQ * (sub + 1))
            kb = jnp.clip(Q_ROWS * rb - 4, 0, rows - K_ROWS)
            ks = pl.multiple_of(kb * GRID_W, GRID_W)
            ebase = kb - Q_ROWS * rb + 11
            q2 = q_ref[rs, :]
            k2 = k_ref[pl.ds(ks, TK), :]
            v2 = v_ref[pl.ds(ks, TK), :]
            bg = bg_ref[rs, :].astype(F32)
            sig = _sigmoid(bg)
            obv = ob_ref[rs, :].astype(F32)
            dout = do_ref[rs, :].astype(F32)
            dbg_ref[rs, :] = (dout * obv * (sig * (1.0 + bg * (1.0 - sig)))).astype(BF16)
            d_o = dout * (bg * sig)
            d_oo = d_o * obv
            lse2 = lse_ref[rs, :]
            dq_acc = jnp.zeros((TQ, 128), F32)
            for hh in range(2):
                msk = (lane >= HEAD_DIM) == bool(hh)
                qm = jnp.where(msk, q2, jnp.zeros_like(q2))
                lse_h = jnp.max(jnp.where(msk, lse2, -jnp.inf), axis=-1, keepdims=True)
                p = jnp.exp2(_nt(qm, k2) + bias_ref[hh] - lse_h)
                pc = jnp.exp2(_nt(qm, ck2) - lse_h)
                dom_f = jnp.where(msk, d_o, 0.0)
                dom = dom_f.astype(BF16)
                delta = jnp.sum(jnp.where(msk, d_oo, 0.0), axis=-1, keepdims=True)
                d_hi = delta.astype(BF16).astype(F32)
                x0 = HEAD_DIM * (1 - hh)
                dom_aug = jnp.where(lane == x0, -d_hi, jnp.where(lane == x0 + 1, d_hi - delta, dom_f)).astype(BF16)
                extra = jnp.logical_or(lane == x0, lane == x0 + 1)
                va = jnp.where(msk, v2, jnp.where(extra, jnp.ones_like(v2), jnp.zeros_like(v2)))
                cva = jnp.where(msk, cv2, jnp.where(extra, jnp.ones_like(cv2), jnp.zeros_like(cv2)))
                ds = p * _nt(dom_aug, va)
                dsc = pc * _nt(dom_aug, cva)
                dsb, dscb = ds.astype(BF16), dsc.astype(BF16)
                dq_h = (jnp.dot(dsb, k2, preferred_element_type=F32)
                        + jnp.dot(dscb, ck2, preferred_element_type=F32))
                dq_acc = jnp.where(msk, dq_h, dq_acc)
                dk_ref[pl.ds(ks, TK), :] += _tn(dsb, qm)
                dv_ref[pl.ds(ks, TK), :] += _tn(p.astype(BF16), dom)
                dck_ref[...] += _tn(dscb, qm)
                dcv_ref[...] += _tn(pc.astype(BF16), dom)
                for i in range(Q_ROWS):
                    for mm in range(K_ROWS // 2):
                        dacc_ref[hh, ebase + (2 * mm - i)] += ds[GRID_W * i:GRID_W * (i + 1),
                                                                 128 * mm:128 * (mm + 1)]
            dq_ref[rs, :] = dq_acc

        @pl.when(st == n_st - 1)
        def _():
            for hh in range(2):
                drpb_ref[hh] = _rpb_diag_sums(dacc_ref.at[hh], hsel_ref, fold_ref)

        @pl.when(jnp.logical_and(hp == pl.num_programs(0) - 1, st == n_st - 1))
        def _():
            for r, (ex, _) in enumerate(riders):
                ex.wait(rid_src[r], rid_dst[r], *rid_sems[3 * r:3 * r + 3])

    tile = pl.BlockSpec((ATT_SUB * TQ, 128), lambda hp, st: (st, hp))
    colT = pl.BlockSpec((T, 128), lambda hp, st: (0, hp))
    colC = pl.BlockSpec((C, 128), lambda hp, st: (0, hp))
    res = pl.pallas_call(
        body, name="attn_bwd", grid=(4, n_st),
        in_specs=(_attn_specs(T, C) + [tile, tile, pl.BlockSpec((ATT_SUB * TQ, 128), lambda hp, st: (st, 4 + hp)),
                                       pl.BlockSpec((8, 128, 128), lambda hp, st: (0, 0, 0)),
                                       pl.BlockSpec((32, 256), lambda hp, st: (0, 0))]
                  + [ANY_SPEC] * n_rid),
        out_specs=([tile, colT, colT, colC, colC, tile, pl.BlockSpec((2, 32, 128), lambda hp, st: (hp, 0, 0))]
                   + [ANY_SPEC] * n_rid),
        out_shape=([jax.ShapeDtypeStruct((T, 512), F32), jax.ShapeDtypeStruct((T, 512), F32),
                    jax.ShapeDtypeStruct((T, 512), F32), jax.ShapeDtypeStruct((C, 512), F32),
                    jax.ShapeDtypeStruct((C, 512), F32), jax.ShapeDtypeStruct((T, 512), BF16),
                    jax.ShapeDtypeStruct((N_HEADS, 32, 128), F32)] + [ex.out_shape for ex, _ in riders]),
        scratch_shapes=([pltpu.VMEM((3, 2, TQ, TK), F32), pltpu.VMEM((2, N_DIAG, GRID_W, 128), F32)]
                        + [s for ex, _ in riders for s in ex.scratch]),
        compiler_params=_params(("arbitrary", "arbitrary")),
    )(qs, kn, z, ckn, zc, pairs, row_mask, z, ob, lse, dmc, hsel, fold, *[a for _, a in riders])
    return res[:7], res[7:]


def _rpb_diag_sums(a_ref, hsel_ref, fold_ref):
    n_off = 2 * WIN_C - 1
    n_dr = 2 * WIN_R - 1
    qc = lax.broadcasted_iota(jnp.int32, (GRID_W, 128), 0)
    lane = lax.broadcasted_iota(jnp.int32, (GRID_W, 128), 1)
    diff = lane % GRID_W - qc + (WIN_C - 1)
    left = lane < GRID_W

    def by_dr(dr):
        return a_ref[dr + 4] + pltpu.roll(a_ref[dr + 3], GRID_W, 1)

    out = jnp.zeros((32, 128), F32)
    for j in range((n_dr + 1) // 2):
        hi = pltpu.roll(by_dr(2 * j + 1), GRID_W, 1) if 2 * j + 1 < n_dr else 0.0
        pair = jnp.where(left, by_dr(2 * j), hi)
        parts = []
        for o in range(n_off):
            mv = jnp.where(diff == o, pair, 0.0)
            acc = mv[0:8]
            for r8 in range(1, GRID_W // 8):
                acc = acc + mv[8 * r8:8 * (r8 + 1)]
            parts.append(acc)
        parts.append(jnp.zeros((8, 128), F32))
        stack = jnp.concatenate(parts, axis=0)
        s_hi = stack.astype(BF16)
        s_lo = (stack - s_hi.astype(F32)).astype(BF16)
        per_o = (jnp.dot(fold_ref[...], s_hi, preferred_element_type=F32)
                 + jnp.dot(fold_ref[...], s_lo, preferred_element_type=F32))
        out = out + _dot2(per_o, hsel_ref[j])
    return out


def _head_norm_bwd(raw, dn, gain, ones_ref):
    rr = lax.rsqrt(_head_sum(raw * raw, ones_ref) * (1.0 / HEAD_DIM) + EPS)
    hat = raw * rr
    dgain = jnp.sum(dn * hat, axis=0, keepdims=True)
    dhat = dn * gain
    mean = _head_sum(dhat * hat, ones_ref) * (1.0 / HEAD_DIM)
    return rr * (dhat - hat * mean), dgain


def _ctx_k_bwd(zc, dck, dcv, ones_blk, gk, foldm):
    C = dck.shape[0]

    def body(bk_ref, dk_ref, dv_ref, ones_ref, gk_ref, fold_ref, dbk_ref, dbv_ref, dgk_ref):
        dbk, dgk = _head_norm_bwd(bk_ref[...].astype(F32), dk_ref[...] * LN2, gk_ref[...], ones_ref)
        dbk_ref[...] = dbk.astype(BF16)
        dbv_ref[...] = dv_ref[...].astype(BF16)
        dgk_ref[...] = jnp.dot(jnp.broadcast_to(dgk, (8, 512)), fold_ref[...],
                               preferred_element_type=F32, precision=HI)

    row = pl.BlockSpec((C, 512), lambda i: (0, 0))
    cst = lambda a, b: pl.BlockSpec((a, b), lambda i: (0, 0))
    out_row = jax.ShapeDtypeStruct((C, 512), BF16)
    return pl.pallas_call(
        body, name="ctx_k_bwd", grid=(1,),
        in_specs=[row, row, row, cst(512, 512), cst(1, 512), cst(512, 128)],
        out_specs=[row, row, cst(8, 128)],
        out_shape=[out_row, out_row, jax.ShapeDtypeStruct((8, 128), F32)],
        compiler_params=_params(("arbitrary",)),
    )(zc, dck, dcv, ones_blk, gk, foldm)


def _bwd_mid(z, dmc, dqs, dk, dv, db_g, h, hc, dzc_k, dzc_v, sgn, ws, wst, bsb, ones8, ones_blk, gq, gk, foldm, tk,
             riders=()):
    T = z.shape[0]
    nt = T // tk
    blk = D_IN // N_DEV
    n_in, n_out, n_sc = 23, 7, 9
    n_rid = len(riders)

    def body(*refs):
        (au_ref, av_ref, ag_ref, bq_ref, bk_ref, d_ref, dq_ref, dk_ref, dv_ref, dbg_ref, h_ref,
         hc_ref, dzck_ref, dzcv_ref, sg_ref, ws_ref, wst_ref, bsb_ref, ones8_ref, ones_ref, gq_ref, gk_ref,
         fold_ref) = refs[:n_in]
        rid_src = refs[n_in:n_in + n_rid]
        dz_ref, sums_out, dws_ref, dbs_ref, dsg_ref, dgq_ref, dgk_ref = refs[n_in + n_rid:n_in + n_rid + n_out]
        rid_dst = refs[n_in + n_rid + n_out:n_in + 2 * n_rid + n_out]
        (acc, accq, acck, stage, sem, send_buf, tmp, s1, r1) = refs[n_in + 2 * n_rid + n_out:
                                                                   n_in + 2 * n_rid + n_out + n_sc]
        rid_sems = refs[n_in + 2 * n_rid + n_out + n_sc:]
        t = pl.program_id(0)

        @pl.when(t == 0)
        def _():
            for r, (ex, _) in enumerate(riders):
                ex.start(rid_src[r], rid_dst[r], *rid_sems[3 * r:3 * r + 3])
            acc[...] = jnp.zeros_like(acc)
            acc[512 * 4:512 * 5, :] = _tn(dzck_ref[...], hc_ref[...])
            acc[512 * 5:512 * 6, :] = _tn(dzcv_ref[...], hc_ref[...])
            dws_ref[...] = jnp.zeros_like(dws_ref)
            dbs_ref[...] = jnp.zeros_like(dbs_ref)
            dsg_ref[...] = jnp.zeros_like(dsg_ref)
            accq[...] = jnp.zeros_like(accq)
            acck[...] = jnp.zeros_like(acck)

        for g in range(SGU_GROUPS):
            ws_bf = ws_ref[g].astype(BF16)
            wst_bf = wst_ref[g].astype(BF16)
            sg = sg_ref[:, 128 * g:128 * (g + 1)]
            bsb_g = bsb_ref[g]
            for j in range(tk // CHUNK):
                rs, cs = slice(CHUNK * j, CHUNK * (j + 1)), slice(128 * g, 128 * (g + 1))
                au, av, ag = (au_ref[rs, cs].astype(F32), av_ref[rs, cs].astype(F32), ag_ref[rs, cs].astype(F32))
                d = d_ref[rs, cs].astype(F32)
                _, (gu, dgu, dgv, rr, vhat, vn, mixed, sig, sl) = _sgu_chunk_fwd(au, av, ag, sg, ws_bf, bsb_g)
                dz_ref[rs, 128 * g:128 * (g + 1)] = (d * mixed * sl * dgu).astype(BF16)
                dz_ref[rs, 1024 + 128 * g:1024 + 128 * (g + 1)] = (
                    d * gu * mixed * (sig * (1.0 + ag * (1.0 - sig)))).astype(BF16)
                dmixed = d * gu * sl
                dmb = dmixed.astype(BF16)
                dm_lo = (dmixed - dmb.astype(F32)).astype(BF16)
                dbs_ref[g] += _nt(ones8_ref[...], dmb) + _nt(ones8_ref[...], dm_lo)
                dws_ref[g] += _nt(dmb, vn.astype(BF16))
                dvn = jnp.dot(wst_bf, dmb, preferred_element_type=F32)
                dsg_ref[:, 128 * g:128 * (g + 1)] += jnp.sum(dvn * vhat, axis=0, keepdims=True)
                dvhat = dvn * sg
                mean = jnp.mean(dvhat * vhat, axis=-1, keepdims=True)
                dz_ref[rs, 512 + 128 * g:512 + 128 * (g + 1)] = (rr * (dvhat - vhat * mean) * dgv).astype(BF16)

        dbq, dgq = _head_norm_bwd(bq_ref[...].astype(F32), dq_ref[...] * ATT_SCALE, gq_ref[...], ones_ref)
        dz_ref[:, 512 * 3:512 * 4] = dbq.astype(BF16)
        accq[...] += dgq
        dbk, dgk = _head_norm_bwd(bk_ref[...].astype(F32), dk_ref[...] * LN2, gk_ref[...], ones_ref)
        dz_ref[:, 512 * 4:512 * 5] = dbk.astype(BF16)
        acck[...] += dgk
        dz_ref[:, 512 * 5:512 * 6] = dv_ref[...].astype(BF16)
        dz_ref[:, 512 * 6:512 * 7] = dbg_ref[...]

        hv = h_ref[...]
        for k in range(N_BRANCH):
            acc[512 * k:512 * (k + 1), :] += _tn(dz_ref[:, 512 * k:512 * (k + 1)], hv)

        @pl.when(t == nt - 1)
        def _():
            dgq_ref[...] = jnp.dot(jnp.broadcast_to(accq[...], (8, 512)), fold_ref[...],
                                   preferred_element_type=F32, precision=HI)
            dgk_ref[...] = jnp.dot(jnp.broadcast_to(acck[...], (8, 512)), fold_ref[...],
                                   preferred_element_type=F32, precision=HI)
            cidx = lax.axis_index("c")
            sib = (lax.axis_index("x"), lax.axis_index("y"), 1 - cidx)
            swaps = []
            for q in range(N_DEV // 2):
                theirs = acc[pl.ds(pl.multiple_of(2 * blk * q + blk * (1 - cidx), 8), blk), :]
                send_buf[q] = theirs.astype(BF16)
                cp = pltpu.make_async_remote_copy(src_ref=send_buf.at[q], dst_ref=tmp.at[q], send_sem=s1.at[q],
                                                  recv_sem=r1.at[q], device_id=sib, device_id_type=MESH)
                cp.start()
                swaps.append(cp)
            for q in range(N_DEV // 2):
                swaps[q].wait_recv()
                mine = acc[pl.ds(pl.multiple_of(2 * blk * q + blk * cidx, 8), blk), :]
                stage[...] = (mine + tmp[q].astype(F32)).astype(BF16)
                out = pltpu.make_async_copy(stage, sums_out.at[q], sem)
                out.start()
                out.wait()
            for cp in swaps:
                cp.wait_send()
            for r, (ex, _) in enumerate(riders):
                ex.wait(rid_src[r], rid_dst[r], *rid_sems[3 * r:3 * r + 3])

    zcol = lambda col: pl.BlockSpec((tk, 512), lambda t: (t, col))
    row = pl.BlockSpec((tk, 512), lambda t: (t, 0))
    whole = lambda a: pl.BlockSpec(a.shape, lambda t: (0,) * a.ndim)
    res = pl.pallas_call(
        body, name="bwd_mid", grid=(nt,),
        in_specs=[zcol(0), zcol(1), zcol(2), zcol(3), zcol(4), row, row, row, row, row,
                  pl.BlockSpec((tk, D_MODEL), lambda t: (t, 0)), whole(hc), whole(dzc_k), whole(dzc_v),
                  whole(sgn), whole(ws), whole(wst), whole(bsb), whole(ones8), whole(ones_blk), whole(gq), whole(gk),
                  whole(foldm)] + [ANY_SPEC] * n_rid,
        out_specs=[pl.BlockSpec((tk, D_IN), lambda t: (t, 0)), ANY_SPEC,
                   pl.BlockSpec((SGU_GROUPS, CHUNK, CHUNK), lambda t: (0, 0, 0)),
                   pl.BlockSpec((SGU_GROUPS, 8, CHUNK), lambda t: (0, 0, 0)),
                   pl.BlockSpec((1, 512), lambda t: (0, 0)), pl.BlockSpec((8, 128), lambda t: (0, 0)),
                   pl.BlockSpec((8, 128), lambda t: (0, 0))] + [ANY_SPEC] * n_rid,
        out_shape=[jax.ShapeDtypeStruct((T, D_IN), BF16), jax.ShapeDtypeStruct((N_DEV // 2, blk, D_MODEL), BF16),
                   jax.ShapeDtypeStruct((SGU_GROUPS, CHUNK, CHUNK), F32),
                   jax.ShapeDtypeStruct((SGU_GROUPS, 8, CHUNK), F32), jax.ShapeDtypeStruct((1, 512), F32),
                   jax.ShapeDtypeStruct((8, 128), F32), jax.ShapeDtypeStruct((8, 128), F32)]
                  + [ex.out_shape for ex, _ in riders],
        scratch_shapes=[pltpu.VMEM((D_IN, D_MODEL), F32), pltpu.VMEM((1, 512), F32), pltpu.VMEM((1, 512), F32),
                        pltpu.VMEM((blk, D_MODEL), BF16), pltpu.SemaphoreType.DMA,
                        pltpu.VMEM((N_DEV // 2, blk, D_MODEL), BF16), pltpu.VMEM((N_DEV // 2, blk, D_MODEL), BF16),
                        pltpu.SemaphoreType.DMA((N_DEV // 2,)), pltpu.SemaphoreType.DMA((N_DEV // 2,))]
                       + [s for ex, _ in riders for s in ex.scratch],
        compiler_params=_params(("arbitrary",)),
    )(z, z, z, z, z, dmc, dqs, dk, dv, db_g, h, hc, dzc_k, dzc_v, sgn, ws, wst, bsb, ones8, ones_blk, gq, gk, foldm,
      *[a for _, a in riders])
    return res[:n_out], res[n_out:]


def _inproj_bwd_dx(dzs, w_row0, w_in_t, x, dy, ng, scale, shift, tm, name, riders=()):
    T = x.shape[0]
    n = len(dzs)
    wpc = dzs[0].shape[1]
    nt = T // tm
    with_dx = dy is not None
    n_own_in = n + 5 + with_dx
    n_own_out = 3 + with_dx
    n_rid = len(riders)

    def body(*refs):
        dz_refs = refs[:n]
        n_in = n_own_in + n_rid
        own = refs[n:n_own_in] + refs[n_in:n_in + n_own_out]
        rid_src = refs[n_own_in:n_own_in + n_rid]
        rid_dst = refs[n_in + n_own_out:n_in + n_own_out + n_rid]
        rid_sems = refs[n_in + n_own_out + n_rid:]
        if with_dx:
            w_ref, x_ref, dy_ref, g_ref, sc_ref, sh_ref, gx_ref, dsh_ref, dsc_ref, dg_ref = own
        else:
            w_ref, x_ref, g_ref, sc_ref, sh_ref, dsh_ref, dsc_ref, dg_ref = own

        @pl.when(pl.program_id(0) == 0)
        def _():
            for r, (ex, _) in enumerate(riders):
                ex.start(rid_src[r], rid_dst[r], *rid_sems[3 * r:3 * r + 3])
            dsh_ref[...] = jnp.zeros_like(dsh_ref)
            dsc_ref[...] = jnp.zeros_like(dsc_ref)
            dg_ref[...] = jnp.zeros_like(dg_ref)

        dh = jnp.dot(dz_refs[0][...], w_ref[0:wpc, :], preferred_element_type=F32)
        for k in range(1, n):
            dh = dh + jnp.dot(dz_refs[k][...], w_ref[wpc * k:wpc * (k + 1), :], preferred_element_type=F32)
        xv = x_ref[...]
        r = lax.rsqrt(jnp.mean(xv * xv, axis=-1, keepdims=True) + EPS)
        xn = xv * r
        gv, op = g_ref[...], 1.0 + sc_ref[...]
        dsh_ref[...] += jnp.sum(dh, axis=0, keepdims=True)
        dsc_ref[...] += jnp.sum(dh * xn * gv, axis=0, keepdims=True)
        dg_ref[...] += jnp.sum(dh * op * xn, axis=0, keepdims=True)
        if with_dx:
            dxn = dh * (gv * op)
            gx_ref[...] = r * (dxn - xn * jnp.mean(dxn * xn, axis=-1, keepdims=True)) + dy_ref[...]

        @pl.when(pl.program_id(0) == nt - 1)
        def _():
            for r, (ex, _) in enumerate(riders):
                ex.wait(rid_src[r], rid_dst[r], *rid_sems[3 * r:3 * r + 3])

    vec = pl.BlockSpec((1, D_MODEL), lambda i: (0, 0))
    rowf = pl.BlockSpec((tm, D_MODEL), lambda i: (i, 0))
    in_specs = [pl.BlockSpec((tm, wpc), lambda i: (i, 0))] * n
    in_specs += [pl.BlockSpec((wpc * n, D_MODEL), lambda i: (w_row0 // (wpc * n), 0)), rowf]
    args = list(dzs) + [w_in_t, x]
    vshape = jax.ShapeDtypeStruct((1, D_MODEL), F32)
    out_specs, out_shape = [vec, vec, vec], [vshape, vshape, vshape]
    if with_dx:
        in_specs.append(rowf)
        args.append(dy)
        out_specs = [rowf] + out_specs
        out_shape = [jax.ShapeDtypeStruct((T, D_MODEL), F32)] + out_shape
    in_specs += [vec, vec, vec] + [ANY_SPEC] * n_rid
    args += [ng, scale, shift] + [a for _, a in riders]
    res = pl.pallas_call(
        body, name=name, grid=(nt,), in_specs=in_specs, out_specs=out_specs + [ANY_SPEC] * n_rid,
        out_shape=out_shape + [ex.out_shape for ex, _ in riders],
        scratch_shapes=[s for ex, _ in riders for s in ex.scratch],
        compiler_params=_params(("arbitrary",)),
    )(*args)
    return res[:n_own_out], res[n_own_out:]


def _adamw_sharded(w, gparts, m, v, tr, name, riders=()):
    R, C = w.shape
    n_part = gparts.shape[0]
    nt = R // tr
    n_rid = len(riders)

    def body(w_ref, gp_ref, m_ref, v_ref, *rest):
        rid_src = rest[:n_rid]
        g_ref, d_ref, m2_ref, v2_ref = rest[n_rid:n_rid + 4]
        rid_dst = rest[n_rid + 4:2 * n_rid + 4]
        rid_sems = rest[2 * n_rid + 4:]

        @pl.when(pl.program_id(0) == 0)
        def _():
            for r, (ex, _) in enumerate(riders):
                ex.start(rid_src[r], rid_dst[r], *rid_sems[3 * r:3 * r + 3])

        g = gp_ref[0].astype(F32)
        for d in range(1, n_part):
            g = g + gp_ref[d].astype(F32)
        delta, m2, v2 = _adam(w_ref[...], g, m_ref[...], v_ref[...])
        g_ref[...] = g
        d_ref[...] = delta
        m2_ref[...] = m2
        v2_ref[...] = v2

        @pl.when(pl.program_id(0) == nt - 1)
        def _():
            for r, (ex, _) in enumerate(riders):
                ex.wait(rid_src[r], rid_dst[r], *rid_sems[3 * r:3 * r + 3])

    row = pl.BlockSpec((tr, C), lambda i: (i, 0))
    sh = jax.ShapeDtypeStruct((R, C), F32)
    res = pl.pallas_call(
        body, name=name, grid=(nt,),
        in_specs=[row, pl.BlockSpec((n_part, tr, C), lambda i: (0, i, 0)), row, row] + [ANY_SPEC] * n_rid,
        out_specs=[row, row, row, row] + [ANY_SPEC] * n_rid,
        out_shape=[sh, sh, sh, sh] + [ex.out_shape for ex, _ in riders],
        scratch_shapes=[s for ex, _ in riders for s in ex.scratch],
        compiler_params=_params(("arbitrary",)),
    )(w, gparts, m, v, *[a for _, a in riders])
    return res[:4], res[4:]


def _pack_vectors(vec_rows, dsg, dgq, dgk, dgk_c, loss_part):
    flat = [a for _, arrs in vec_rows for a in arrs]

    def body(*refs):
        vecs = list(refs[:len(flat)])
        dsg_ref, dgq_ref, dgk_ref, dgkc_ref, loss_ref, v_ref = refs[len(flat):]
        row = lax.broadcasted_iota(jnp.int32, (16, D_MODEL), 0)
        misc = jnp.concatenate([dsg_ref[...], dgq_ref[0:1, :], dgk_ref[0:1, :], dgkc_ref[0:1, :],
                                loss_ref[...]], axis=1)
        v = jnp.where(row == V_MISC, jnp.broadcast_to(misc, (16, D_MODEL)), 0.0)
        for r, arrs in vec_rows:
            val = vecs.pop(0)[...]
            for _ in arrs[1:]:
                val = val + vecs.pop(0)[...]
            v = jnp.where(row == r, jnp.broadcast_to(val, (16, D_MODEL)), v)
        v_ref[...] = v

    return pl.pallas_call(
        body, name="pack_vectors", out_shape=jax.ShapeDtypeStruct((16, D_MODEL), F32), compiler_params=_params(),
    )(*flat, dsg, dgq, dgk, dgk_c, loss_part)


SMALL_NAMES = ("b_ada", "norm_g", "sgu_norm_g", "w_spatial", "b_spatial", "q_norm_g", "k_norm_g", "rpb")


def _adamw_small(vg, sg, rg, ws, ms, vs):
    k = len(SMALL_NAMES)

    def body(*refs):
        vg_ref, sg_ref, rg_ref = refs[0], refs[1], refs[2]
        refs = refs[1:]
        w_refs = dict(zip(SMALL_NAMES, refs[2:2 + k]))
        m_refs = dict(zip(SMALL_NAMES, refs[2 + k:2 + 2 * k]))
        v_refs = dict(zip(SMALL_NAMES, refs[2 + 2 * k:2 + 3 * k]))
        o_refs = [dict(zip(SMALL_NAMES, refs[2 + (3 + i) * k:2 + (4 + i) * k])) for i in range(4)]
        loss_ref = refs[2 + 7 * k]

        sv = vg_ref[0]
        for d in range(1, N_DEV):
            sv = sv + vg_ref[d]
        loss_ref[...] = sv[V_MISC:V_MISC + 1, 896:1024]

        def total(lo, hi, ref=sg_ref):
            s = ref[0, lo:hi, :].astype(F32)
            for d in range(1, N_DEV):
                s = s + ref[d, lo:hi, :].astype(F32)
            return s

        def emit(name, idx, g):
            res = _adam(w_refs[name][idx], g, m_refs[name][idx], v_refs[name][idx])
            for o, val in zip(o_refs, (g,) + res):
                o[name][idx] = val

        everything = (slice(None), slice(None))
        row = lambda r: sv[r:r + 1, :]
        emit("b_ada", everything, jnp.concatenate(
            [row(V_DSHIFT) + row(V_DCSHIFT), row(V_DSCALE) + row(V_DCSCALE), row(V_DGATE)], axis=1))
        emit("norm_g", everything, row(V_DNG) + row(V_DNG_CTX))
        misc = row(V_MISC)
        emit("sgu_norm_g", everything, misc[:, 0:512])
        emit("q_norm_g", everything, misc[:, 512:512 + HEAD_DIM])
        emit("k_norm_g", everything, misc[:, 640:640 + HEAD_DIM] + misc[:, 768:768 + HEAD_DIM])
        for g in range(SGU_GROUPS):
            emit("w_spatial", (0, g), total(128 * g, 128 * (g + 1)))
            emit("b_spatial", (0, slice(g, g + 1), slice(None)), total(M_DBS + 8 * g, M_DBS + 8 * (g + 1))[0:1, :])
        for hd in range(N_HEADS):
            by_dc = total(32 * hd, 32 * (hd + 1), rg_ref)
            emit("rpb", (0, hd), by_dc.T[0:2 * WIN_R - 1, 0:2 * WIN_C - 1])

    shapes = [jax.ShapeDtypeStruct(w.shape, F32) for w in ws]
    res = pl.pallas_call(body, name="adamw_small", out_shape=shapes * 4 + [jax.ShapeDtypeStruct((1, 128), F32)],
                         compiler_params=_params())(vg, sg, rg, *ws, *ms, *vs)
    return [res[i * k:(i + 1) * k] for i in range(4)], res[4 * k]


def _adamw_cctx(pc_g, w, m, v):
    def body(pc_ref, w_ref, m_ref, v_ref, g_ref, d_ref, m2_ref, v2_ref):
        pc = pc_ref[0, 0:1, :]
        for d in range(1, N_DEV):
            pc = pc + pc_ref[d, 0:1, :]
        cc = w_ref[...]
        sig = _sigmoid(cc)
        g = pc * (sig * (1.0 + cc * (1.0 - sig)))
        delta, m2, v2 = _adam(cc, g, m_ref[...], v_ref[...])
        g_ref[...] = g
        d_ref[...] = delta
        m2_ref[...] = m2
        v2_ref[...] = v2

    sh = jax.ShapeDtypeStruct((1, D_MODEL), F32)
    return pl.pallas_call(body, name="adamw_cctx", out_shape=[sh, sh, sh, sh], compiler_params=_params())(
        pc_g, w, m, v)


def _block_ones(n, blk):
    i = np.arange(n)
    return jnp.asarray((i[:, None] // blk == i[None, :] // blk).astype(np.float32), BF16)


def _rpb_pairs(rpb):
    n_off = 2 * WIN_C - 1
    cols = np.arange(GRID_W)
    c0 = np.clip(cols - WIN_C // 2, 0, GRID_W - WIN_C)
    in_win = (cols[None, :] >= c0[:, None]) & (cols[None, :] < c0[:, None] + WIN_C)
    dc = np.clip(cols[None, :] - cols[:, None] + (WIN_C - 1), 0, n_off - 1)
    expand = (dc[None] == np.arange(n_off)[:, None, None]) & in_win[None]
    toep = jnp.einsum("hrd,dqk->hrqk", rpb, jnp.asarray(expand, F32), precision=HI)
    toep = toep + jnp.asarray(np.where(in_win, 0.0, NEG_INF).astype(np.float32))
    neg = jnp.full((N_HEADS, 1, GRID_W, GRID_W), NEG_INF, F32)
    ext = jnp.concatenate([neg, toep, neg], axis=1)
    return jnp.concatenate([ext[:, :-1], ext[:, 1:]], axis=-1)


def _row_mask(rows):
    nrb = rows // Q_ROWS
    valid = np.zeros((3, Q_ROWS, 1, K_ROWS, 1), bool)
    for t, rb in enumerate((0, 1, nrb - 1)):
        kb = int(np.clip(Q_ROWS * rb - 4, 0, rows - K_ROWS))
        for i in range(Q_ROWS):
            r0 = int(np.clip(Q_ROWS * rb + i - WIN_R // 2, 0, rows - WIN_R))
            for j in range(K_ROWS):
                valid[t, i, 0, j, 0] = r0 <= kb + j < r0 + WIN_R
    full = np.broadcast_to(valid, (3, Q_ROWS, GRID_W, K_ROWS, GRID_W)).reshape(3, TQ, TK)
    return jnp.asarray(np.where(full, 0.0, NEG_INF).astype(np.float32))


def kernel(x, c, ctx, c_ctx, w_ada, b_ada, norm_g, w_in, sgu_norm_g, w_spatial, b_spatial, q_norm_g, k_norm_g, rpb, w_out, loss_target, m_c_ctx, m_w_ada, m_b_ada, m_norm_g, m_w_in, m_sgu_norm_g, m_w_spatial, m_b_spatial, m_q_norm_g, m_k_norm_g, m_rpb, m_w_out, v_c_ctx, v_w_ada, v_b_ada, v_norm_g, v_w_in, v_sgu_norm_g, v_w_spatial, v_b_spatial, v_q_norm_g, v_k_norm_g, v_rpb, v_w_out):
    me = 4 * lax.axis_index("x") + 2 * lax.axis_index("y") + lax.axis_index("c")
    x2, ctx2, tgt2 = x[0], ctx[0], loss_target[0]
    T, C = x2.shape[0], ctx2.shape[0]
    rows = T // GRID_W
    wada, win_t, wout = w_ada[0], w_in[0].T, w_out[0]
    ada_w = wada.shape[1]
    win_w = win_t.shape[0]

    row8 = lax.broadcasted_iota(jnp.int32, (8, D_MODEL), 0)
    c_blk = jnp.where(row8 == me, jnp.broadcast_to(c, (8, D_MODEL)), 0.0)
    b_sh = lax.dynamic_slice(b_ada, (0, me * ada_w), (1, ada_w))
    c_ctx_row = c_ctx.reshape(1, D_MODEL)

    ones512 = _block_ones(512, HEAD_DIM)
    ones8 = jnp.ones((8, 128), BF16)
    foldm = jnp.asarray((np.arange(512)[:, None] % HEAD_DIM == np.arange(128)[None, :]).astype(np.float32))
    lane_half = np.arange(128)[None, :, None] // GRID_W
    hsel = jnp.asarray((2 * np.arange(8)[:, None, None] + lane_half == np.arange(128)[None, None, :]).astype(np.float32),
                       BF16)
    foldr = jnp.asarray((np.arange(256)[None, :] // 8 == np.arange(32)[:, None]).astype(np.float32), BF16)
    gq512 = jnp.tile(q_norm_g, (1, N_HEADS))
    gk512 = jnp.tile(k_norm_g, (1, N_HEADS))
    ws = w_spatial[0]
    wst = ws.transpose(0, 2, 1)
    bsb = jnp.broadcast_to(b_spatial[0][:, :, None], (SGU_GROUPS, CHUNK, 128))
    pairs = _rpb_pairs(rpb[0])
    row_mask = _row_mask(rows)

    my_chip = me // 2
    order = jnp.stack([my_chip, my_chip ^ 2, my_chip ^ 1, my_chip ^ 3]).astype(jnp.int32)
    h, z, win_g, wout_g, s16, part_g = _inproj_fwd(order, x2, norm_g, c_blk, c_ctx_row, wada, b_sh,
                                                   win_t.astype(BF16), wout.astype(BF16), 512)
    w_in_b = win_g.reshape(D_IN, D_MODEL)
    w_out_b = wout_g.reshape(D_MODEL, D_MODEL)
    mod16 = part_g.transpose(1, 0, 2).reshape(16, 3 * D_MODEL)
    mod = lax.dynamic_slice(mod16, (me, 0), (1, 3 * D_MODEL))
    shift, scale, gate = mod[:, :D_MODEL], mod[:, D_MODEL:2 * D_MODEL], mod[:, 2 * D_MODEL:]
    cshift, cscale = mod16[8:9, :D_MODEL], mod16[8:9, D_MODEL:2 * D_MODEL]
    qs, kn = _qk_norm(z, ones512, gq512, gk512, 512)
    hc, zc, ckn = _ctx_fwd(ctx2, norm_g, cscale, cshift, w_in_b, ones512, gk512)
    out_a = _sgu_fwd(z, sgu_norm_g, ws, bsb, 512)
    ob, out_b, lse = _attn_fwd(qs, kn, z, ckn, zc, pairs, row_mask)

    dy, dmc, dw_out, dgate, loss_part = _outproj_loss_bwd(x2, tgt2, out_a, out_b, gate, w_out_b, 512)
    dw_out_blocks = dw_out.reshape(N_DEV, D_MODEL // N_DEV, D_MODEL)
    (dqs, dk, dv, dck, dcv, db_g, drpb), (gout_parts,) = _attn_bwd(
        qs, kn, z, ckn, zc, pairs, row_mask, ob, lse, dmc, hsel, foldr,
        [(_Hosted("a2a", dw_out_blocks), dw_out_blocks)])
    dzc_k, dzc_v, dgk_c = _ctx_k_bwd(zc, dck, dcv, ones512, gk512, foldm)
    rloc = drpb.reshape(N_HEADS * 32, 128)
    (dz, chip_sums, dws, dbs, dsg, dgq, dgk), (rg,) = _bwd_mid(
        z, dmc, dqs, dk, dv, db_g, h, hc, dzc_k, dzc_v, sgu_norm_g, ws, wst, bsb, ones8, ones512, gq512, gk512, foldm,
        256, [(_Hosted("ag", rloc), rloc)])
    sloc = jnp.concatenate([dws.reshape(SGU_GROUPS * CHUNK, CHUNK), dbs.reshape(SGU_GROUPS * 8, CHUNK)]).astype(BF16)
    (grad_x, dshift, dscale, dng), (gin_parts,) = _inproj_bwd_dx(
        [dz], 0, w_in_b, x2, dy, norm_g, scale, shift, 512, "inproj_bwd_dx",
        [(_Hosted("chips", chip_sums), chip_sums)])
    (dcshift, dcscale, dng_c), _ = _inproj_bwd_dx([dzc_k, dzc_v], 4 * 512, w_in_b, ctx2, None, norm_g, cscale,
                                                  cshift, C, "ctx_bwd_dx")

    zero_row = jnp.zeros((1, D_MODEL), F32)
    vec_rows = [(V_DSHIFT, [dshift]), (V_DSCALE, [dscale]), (V_DGATE, [dgate]), (V_DCSHIFT, [dcshift]),
                (V_DCSCALE, [dcscale]), (V_ZERO, [zero_row]), (V_DNG, [dng]), (V_DNG_CTX, [dng_c])]
    vloc = _pack_vectors(vec_rows, dsg, dgq, dgk, dgk_c, loss_part)
    res_in, (vg, sg) = _adamw_sharded(win_t, gin_parts, m_w_in[0].T, v_w_in[0].T, 112, "adamw_w_in",
                                      [(_Hosted("ag", vloc), vloc), (_Hosted("ag", sloc), sloc)])
    small_w = (b_ada, norm_g, sgu_norm_g, w_spatial, b_spatial, q_norm_g, k_norm_g, rpb)
    small_m = (m_b_ada, m_norm_g, m_sgu_norm_g, m_w_spatial, m_b_spatial, m_q_norm_g, m_k_norm_g, m_rpb)
    small_v = (v_b_ada, v_norm_g, v_sgu_norm_g, v_w_spatial, v_b_spatial, v_q_norm_g, v_k_norm_g, v_rpb)
    res_small, loss_row = _adamw_small(vg, sg, rg, small_w, small_m, small_v)

    dm_all = vg[:, V_DSHIFT:V_DGATE + 1, :].reshape(N_DEV, 3 * D_MODEL)
    dc_all = vg[:, V_DCSHIFT:V_ZERO + 1, :].reshape(N_DEV, 3 * D_MODEL)
    dm_sh = lax.dynamic_slice(dm_all, (0, me * ada_w), (N_DEV, ada_w))
    dc_sh = lax.dynamic_slice(dc_all, (0, me * ada_w), (N_DEV, ada_w))
    *res_ada, pc = _ada_bwd(s16, dm_sh, dc_sh, wada, m_w_ada[0], v_w_ada[0])
    res_out, (pc_g,) = _adamw_sharded(wout, gout_parts, m_w_out[0], v_w_out[0], 128, "adamw_w_out",
                                      [(_Hosted("ag", pc), pc)])
    res_cctx = _adamw_cctx(pc_g, c_ctx_row, m_c_ctx.reshape(1, D_MODEL), v_c_ctx.reshape(1, D_MODEL))

    loss = loss_row[0, 0]
    outs = [loss, grad_x[None]]
    for kind in range(4):
        by_name = dict(zip(SMALL_NAMES, res_small[kind]))
        by_name.update(c_ctx=res_cctx[kind].reshape(D_MODEL), w_ada=res_ada[kind][None],
                       w_in=res_in[kind].T[None], w_out=res_out[kind][None])
        outs += [by_name[nme] for nme in ("c_ctx", "w_ada", "b_ada", "norm_g", "w_in", "sgu_norm_g", "w_spatial",
                                          "b_spatial", "q_norm_g", "k_norm_g", "rpb", "w_out")]
    return tuple(outs)
```

```python
import functools

import numpy as np
import jax
import jax.numpy as jnp
from jax import lax
from jax.experimental import pallas as pl
from jax.experimental.pallas import tpu as pltpu

F32 = jnp.float32
BF16 = jnp.bfloat16
HI = lax.Precision.HIGHEST

N_DEV = 8
D_MODEL = 1024
D_A = 512
D_B = 512
D_IN = 3584
N_BRANCH = 7
HEAD_DIM = 64
N_HEADS = 8
GRID_W = 64
WIN_R = 8
WIN_C = 16
CHUNK = 128
SGU_GROUPS = 4
EPS = 1e-6
NEG_INF = -1e30
Q_ROWS = 4
K_ROWS = 12
TQ = Q_ROWS * GRID_W
TK = K_ROWS * GRID_W
N_DIAG = 22
ATT_SUB = 4
ATT_SCALE = HEAD_DIM ** -0.5
LOG2E = 1.4426950408889634
LN2 = 0.6931471805599453

ADAM_LR = 0.001
ADAM_B1 = 0.9
ADAM_B2 = 0.999
ADAM_EPS = 1e-08
ADAM_WD = 0.01
ADAM_STEP = 10

VMEM_LIMIT = 56 * 1024 * 1024
MESH = pl.DeviceIdType.MESH

V_DSHIFT, V_DSCALE, V_DGATE, V_DCSHIFT, V_DCSCALE, V_ZERO, V_DNG, V_DNG_CTX, V_MISC = range(9)
M_DBS = 512


def _params(sem=None):
    return pltpu.CompilerParams(dimension_semantics=sem, vmem_limit_bytes=VMEM_LIMIT)


def _sigmoid(x):
    return 1.0 / (1.0 + jnp.exp(-x))


def _gelu_parts(x):
    cdf = 0.5 * (1.0 + lax.erf(x * 0.7071067811865476))
    pdf = jnp.exp(-0.5 * x * x) * 0.3989422804014327
    return x * cdf, cdf + x * pdf


def _nt(a, b):
    return lax.dot_general(a, b, (((1,), (1,)), ((), ())), preferred_element_type=F32)


def _tn(a, b):
    return lax.dot_general(a, b, (((0,), (0,)), ((), ())), preferred_element_type=F32)


def _dot2(v, ones_bf):
    hi = v.astype(BF16)
    lo = (v - hi.astype(F32)).astype(BF16)
    return (jnp.dot(hi, ones_bf, preferred_element_type=F32)
            + jnp.dot(lo, ones_bf, preferred_element_type=F32))


def _head_sum(v, ones_ref):
    return jnp.dot(v.astype(BF16), ones_ref[...], preferred_element_type=F32)


def _adam(w, g, m, v):
    m2 = ADAM_B1 * m + (1.0 - ADAM_B1) * g
    v2 = ADAM_B2 * v + (1.0 - ADAM_B2) * (g * g)
    m_hat = m2 / (1.0 - ADAM_B1 ** ADAM_STEP)
    v_hat = v2 / (1.0 - ADAM_B2 ** ADAM_STEP)
    delta = -ADAM_LR * (m_hat / (jnp.sqrt(v_hat) + ADAM_EPS) + ADAM_WD * w)
    return delta, m2, v2


class _Hosted:
    def __init__(self, kind, src):
        self.kind = kind
        n_slot = {"a2a": N_DEV, "ag": N_DEV, "chips": N_DEV // 2}[kind]
        blk = src.shape if kind == "ag" else src.shape[1:]
        self.out_shape = jax.ShapeDtypeStruct((n_slot,) + tuple(blk), src.dtype)
        self.n_peer = n_slot - 1
        self.scratch = [pltpu.SemaphoreType.DMA((self.n_peer,)), pltpu.SemaphoreType.DMA((self.n_peer,)),
                        pltpu.SemaphoreType.DMA]

    def _copies(self, src, dst, send_sems, recv_sems, loc_sem, landing):
        x, y, c = lax.axis_index("x"), lax.axis_index("y"), lax.axis_index("c")
        if self.kind == "chips":
            me = 2 * x + y
            peers = [((px, py, c), 2 * px + py) for px, py in ((1 - x, y), (x, 1 - y), (1 - x, 1 - y))]
        else:
            me = 4 * x + 2 * y + c
            peers = []
            for k in range(1, N_DEV):
                px = 1 - x if (k >> 2) & 1 else x
                py = 1 - y if (k >> 1) & 1 else y
                pc = 1 - c if k & 1 else c
                peers.append(((px, py, pc), 4 * px + 2 * py + pc))
        remote = []
        for k, (peer, pid) in enumerate(peers):
            s = src if self.kind == "ag" else src.at[pid]
            remote.append(pltpu.make_async_remote_copy(
                src_ref=s, dst_ref=dst.at[pid if landing else me],
                send_sem=send_sems.at[k], recv_sem=recv_sems.at[k], device_id=peer, device_id_type=MESH))
        local = pltpu.make_async_copy(src if self.kind == "ag" else src.at[me], dst.at[me], loc_sem)
        return remote, local

    def start(self, src, dst, send_sems, recv_sems, loc_sem):
        remote, local = self._copies(src, dst, send_sems, recv_sems, loc_sem, landing=False)
        for cp in remote:
            cp.start()
        local.start()

    def wait(self, src, dst, send_sems, recv_sems, loc_sem):
        remote, local = self._copies(src, dst, send_sems, recv_sems, loc_sem, landing=True)
        for cp in remote:
            cp.wait_recv()
        for cp in remote:
            cp.wait_send()
        local.wait()


ANY_SPEC = pl.BlockSpec(memory_space=pl.ANY)


def _allgather_direct(arrs, name):
    n = len(arrs)
    exs = [_Hosted("ag", a) for a in arrs]

    def body(*refs):
        srcs, dsts, sems = refs[:n], refs[n:2 * n], refs[2 * n:]
        for r, ex in enumerate(exs):
            ex.start(srcs[r], dsts[r], *sems[3 * r:3 * r + 3])
        for r, ex in enumerate(exs):
            ex.wait(srcs[r], dsts[r], *sems[3 * r:3 * r + 3])

    return pl.pallas_call(body, name=name, out_shape=[ex.out_shape for ex in exs], in_specs=[ANY_SPEC] * n,
                          out_specs=[ANY_SPEC] * n, scratch_shapes=[s for ex in exs for s in ex.scratch])(*arrs)


def _ada_scratch(n_col):
    return ([pltpu.VMEM((N_DEV, 8, D_MODEL), F32), pltpu.VMEM((16, n_col), F32), pltpu.VMEM((N_DEV, 16, n_col), F32)]
            + [pltpu.SemaphoreType.DMA((N_DEV - 1,)) for _ in range(4)])


def _ada_modulation(cb_ref, cc_ref, w_ref, b_ref, cstack, part, parts, s1, r1, s2, r2):
    x, y, c = lax.axis_index("x"), lax.axis_index("y"), lax.axis_index("c")
    me = 4 * x + 2 * y + c
    peers = []
    for k in range(1, N_DEV):
        px = 1 - x if (k >> 2) & 1 else x
        py = 1 - y if (k >> 1) & 1 else y
        pc = 1 - c if k & 1 else c
        peers.append(((px, py, pc), 4 * px + 2 * py + pc))

    def exchange(src, dst, send_sems, recv_sems):
        for k, (peer, _) in enumerate(peers):
            pltpu.make_async_remote_copy(src_ref=src, dst_ref=dst.at[me], send_sem=send_sems.at[k],
                                         recv_sem=recv_sems.at[k], device_id=peer, device_id_type=MESH).start()
        dst[me] = src[...]
        waits = [pltpu.make_async_remote_copy(src_ref=src, dst_ref=dst.at[pid], send_sem=send_sems.at[k],
                                              recv_sem=recv_sems.at[k], device_id=peer, device_id_type=MESH)
                 for k, (peer, pid) in enumerate(peers)]
        for cp in waits:
            cp.wait_recv()
        for cp in waits:
            cp.wait_send()

    exchange(cb_ref, cstack, s1, r1)
    c_all = cstack[0]
    for d in range(1, N_DEV):
        c_all = c_all + cstack[d]
    row = lax.broadcasted_iota(jnp.int32, (8, D_MODEL), 0)
    cc = jnp.where(row == 0, jnp.broadcast_to(cc_ref[...], (8, D_MODEL)), 0.0)
    call = jnp.concatenate([c_all, cc], axis=0)
    s = call * _sigmoid(call)
    part[...] = jnp.dot(s, w_ref[...], preferred_element_type=F32, precision=HI) + b_ref[...]
    exchange(part, parts, s2, r2)
    return s


def _ada_bwd(s16, dm, dc, w, m, v):
    def body(s_ref, dm_ref, dc_ref, w_ref, m_ref, v_ref, g_ref, d_ref, m2_ref, v2_ref, pc_ref):
        dct = jnp.sum(dc_ref[...], axis=0, keepdims=True)
        row = lax.broadcasted_iota(jnp.int32, dc_ref.shape, 0)
        dcb = jnp.where(row == 0, jnp.broadcast_to(dct, dc_ref.shape), 0.0)
        dm16 = jnp.concatenate([dm_ref[...], dcb], axis=0)
        g = lax.dot_general(s_ref[...], dm16, (((0,), (0,)), ((), ())),
                            preferred_element_type=F32, precision=HI)
        w_ = w_ref[...]
        delta, m2, v2 = _adam(w_, g, m_ref[...], v_ref[...])
        g_ref[...] = g
        d_ref[...] = delta
        m2_ref[...] = m2
        v2_ref[...] = v2
        pc_ref[...] = lax.dot_general(dcb, w_, (((1,), (1,)), ((), ())),
                                      preferred_element_type=F32, precision=HI)

    sh = jax.ShapeDtypeStruct(w.shape, F32)
    return pl.pallas_call(
        body, name="ada_bwd",
        out_shape=[sh, sh, sh, sh, jax.ShapeDtypeStruct((8, D_MODEL), F32)],
        compiler_params=_params(),
    )(s16, dm, dc, w, m, v)


def _head_norm(zk, ones_ref, gain):
    ss = _head_sum(zk * zk, ones_ref)
    return zk * lax.rsqrt(ss * (1.0 / HEAD_DIM) + EPS) * gain


def _inproj_fwd(order, x, ng, c_blk, c_ctx_row, w_ada_sh, b_ada_sh, w_blk_t, w_out_blk, tm):
    T = x.shape[0]
    nt = T // tm
    n_pass = N_DEV // 2
    blk_rows = w_blk_t.shape[0]
    n_col = w_ada_sh.shape[1]
    rider = _Hosted("ag", w_out_blk)
    n_ada = len(_ada_scratch(n_col))

    def body(order_ref, x_ref, g_ref, cb_ref, cc_ref, wa_ref, ba_ref, wb_ref, wo_ref,
             h_out, z_ref, wt_out, wo_out, s_out, parts_out,
             hs, wt, modv, send_sems, recv_sems, loc_sem, h_sem, wt_sem, *rest):
        ada_sc, rider_sems = rest[:n_ada], rest[n_ada:]
        p, i = pl.program_id(0), pl.program_id(1)
        x, y, c = lax.axis_index("x"), lax.axis_index("y"), lax.axis_index("c")
        me, sib = (x, y, c), (x, y, 1 - c)
        chips = [(1 - x, y), (x, 1 - y), (1 - x, 1 - y)]

        def slot(px, py, pc):
            return 4 * px + 2 * py + pc

        def copy(k, block, to, src=None):
            return pltpu.make_async_remote_copy(
                src_ref=wt.at[slot(*block)] if src is None else src, dst_ref=wt.at[slot(*block)],
                send_sem=send_sems.at[k], recv_sem=recv_sems.at[k], device_id=to, device_id_type=MESH)

        own = pltpu.make_async_copy(wb_ref, wt.at[slot(*me)], loc_sem)
        h_copy = pltpu.make_async_copy(hs, h_out, h_sem)
        wt_copy = pltpu.make_async_copy(wt, wt_out, wt_sem)
        first = [copy(1 + j, me, (*chip, c), src=wb_ref) for j, chip in enumerate(chips[:2])] + [copy(0, me, sib, src=wb_ref)]
        passed = [copy(4 + j, (*chip, c), sib) for j, chip in enumerate(chips)]
        relay_src = (jnp.where(c == 0, 1 - x, x), jnp.where(c == 0, y, 1 - y), c)
        relay_dst = (jnp.where(c == 0, x, 1 - x), jnp.where(c == 0, 1 - y, y), c)
        relay = copy(3, relay_src, relay_dst)

        @pl.when(jnp.logical_and(p == 0, i == 0))
        def _():
            s_out[...] = _ada_modulation(cb_ref, cc_ref, wa_ref, ba_ref, *ada_sc)
            own.start()
            for cp in first:
                cp.start()
            rider.start(wo_ref, wo_out, *rider_sems)
            parts = ada_sc[2]
            parts_out[...] = parts[...]
            my_row = pl.ds(slot(*me), 1)
            mod = jnp.concatenate([parts[d, my_row, :] for d in range(N_DEV)], axis=1)
            modv[0:1, :] = mod[:, 0:D_MODEL]
            modv[1:2, :] = mod[:, D_MODEL:2 * D_MODEL]
            own.wait()
            copy(0, sib, me).wait_recv()

        @pl.when(jnp.logical_and(p == 1, i == 0))
        def _():
            for j, chip in enumerate(chips[:2]):
                copy(1 + j, (*chip, c), me).wait_recv()
            relay.start()
            passed[0].start()
            passed[1].start()
            copy(4, (*chips[0], 1 - c), me).wait_recv()

        @pl.when(jnp.logical_and(p == 2, i == 0))
        def _():
            copy(5, (*chips[1], 1 - c), me).wait_recv()

        @pl.when(jnp.logical_and(p == 3, i == 0))
        def _():
            copy(3, (*chips[2], c), me).wait_recv()
            passed[2].start()
            copy(6, (*chips[2], 1 - c), me).wait_recv()
            wt_copy.start()

        rows = pl.ds(pl.multiple_of(i * tm, tm), tm)

        @pl.when(p == 0)
        def _():
            xv = x_ref[...]
            r = lax.rsqrt(jnp.mean(xv * xv, axis=-1, keepdims=True) + EPS)
            hs[rows, :] = ((xv * r * g_ref[...]) * (1.0 + modv[1:2, :]) + modv[0:1, :]).astype(BF16)

        @pl.when(jnp.logical_and(p == 1, i == 0))
        def _():
            h_copy.start()

        w_pair = wt[pl.ds(2 * order_ref[p], 2)].reshape(2 * blk_rows, D_MODEL)
        z_ref[...] = _nt(hs[rows, :], w_pair).astype(BF16)

        @pl.when(jnp.logical_and(p == n_pass - 1, i == nt - 1))
        def _():
            for cp in first + passed + [relay]:
                cp.wait_send()
            h_copy.wait()
            wt_copy.wait()
            rider.wait(wo_ref, wo_out, *rider_sems)

    whole = lambda shape: pl.BlockSpec(shape, lambda p, i, o: (0,) * len(shape))
    grid_spec = pltpu.PrefetchScalarGridSpec(
        num_scalar_prefetch=1, grid=(n_pass, nt),
        in_specs=[pl.BlockSpec((tm, D_MODEL), lambda p, i, o: (jnp.where(p == 0, i, nt - 1), 0)),
                  whole((1, D_MODEL)), whole((8, D_MODEL)), whole((1, D_MODEL)), whole((D_MODEL, n_col)),
                  whole((1, n_col)), ANY_SPEC, ANY_SPEC],
        out_specs=[ANY_SPEC, pl.BlockSpec((tm, 2 * blk_rows), lambda p, i, o: (i, o[p])), ANY_SPEC, ANY_SPEC,
                   whole((16, D_MODEL)), whole((N_DEV, 16, n_col))],
        scratch_shapes=[pltpu.VMEM((T, D_MODEL), BF16), pltpu.VMEM((N_DEV, blk_rows, D_MODEL), BF16),
                        pltpu.VMEM((8, D_MODEL), F32),
                        pltpu.SemaphoreType.DMA((7,)), pltpu.SemaphoreType.DMA((7,)), pltpu.SemaphoreType.DMA,
                        pltpu.SemaphoreType.DMA, pltpu.SemaphoreType.DMA] + _ada_scratch(n_col) + rider.scratch)
    return pl.pallas_call(
        body, name="inproj_fwd", grid_spec=grid_spec,
        out_shape=[jax.ShapeDtypeStruct((T, D_MODEL), BF16), jax.ShapeDtypeStruct((T, D_IN), BF16),
                   jax.ShapeDtypeStruct((N_DEV, blk_rows, D_MODEL), BF16), rider.out_shape,
                   jax.ShapeDtypeStruct((16, D_MODEL), F32), jax.ShapeDtypeStruct((N_DEV, 16, n_col), F32)],
        compiler_params=_params(("arbitrary", "arbitrary")),
    )(order, x, ng, c_blk, c_ctx_row, w_ada_sh, b_ada_sh, w_blk_t, w_out_blk)


def _qk_norm(z, ones_blk, gq, gk, tm):
    T = z.shape[0]

    def body(q_ref, k_ref, ones_ref, gq_ref, gk_ref, qs_ref, kn_ref):
        qs = _head_norm(q_ref[...].astype(F32), ones_ref, gq_ref[...]) * (ATT_SCALE * LOG2E)
        qs_ref[...] = qs.astype(BF16)
        kn_ref[...] = _head_norm(k_ref[...].astype(F32), ones_ref, gk_ref[...]).astype(BF16)

    v512 = pl.BlockSpec((1, 512), lambda i: (0, 0))
    return pl.pallas_call(
        body, name="qk_norm", grid=(T // tm,),
        in_specs=[pl.BlockSpec((tm, 512), lambda i: (i, 3)), pl.BlockSpec((tm, 512), lambda i: (i, 4)),
                  pl.BlockSpec((512, 512), lambda i: (0, 0)), v512, v512],
        out_specs=[pl.BlockSpec((tm, 512), lambda i: (i, 0)), pl.BlockSpec((tm, 512), lambda i: (i, 0))],
        out_shape=[jax.ShapeDtypeStruct((T, 512), BF16), jax.ShapeDtypeStruct((T, 512), BF16)],
        compiler_params=_params(("arbitrary",)),
    )(z, z, ones_blk, gq, gk)


def _ctx_fwd(ctx, ng, cscale, cshift, w_in_t, ones_blk, gk):
    C = ctx.shape[0]

    def body(x_ref, g_ref, sc_ref, sh_ref, w_ref, ones_ref, gk_ref, h_ref, z_ref, kn_ref):
        xv = x_ref[...]
        r = lax.rsqrt(jnp.mean(xv * xv, axis=-1, keepdims=True) + EPS)
        h = (xv * r * g_ref[...]) * (1.0 + sc_ref[...]) + sh_ref[...]
        hb = h.astype(BF16)
        h_ref[...] = hb
        zk = _nt(hb, w_ref[0:512, :])
        zv = _nt(hb, w_ref[512:1024, :])
        z_ref[:, 0:512] = zk.astype(BF16)
        z_ref[:, 512:1024] = zv.astype(BF16)
        kn_ref[...] = _head_norm(zk, ones_ref, gk_ref[...]).astype(BF16)

    vec = pl.BlockSpec((1, D_MODEL), lambda i: (0, 0))
    return pl.pallas_call(
        body, name="ctx_fwd", grid=(1,),
        in_specs=[pl.BlockSpec((C, D_MODEL), lambda i: (0, 0)), vec, vec, vec,
                  pl.BlockSpec((1024, D_MODEL), lambda i: (2, 0)),
                  pl.BlockSpec((512, 512), lambda i: (0, 0)), pl.BlockSpec((1, 512), lambda i: (0, 0))],
        out_specs=[pl.BlockSpec((C, D_MODEL), lambda i: (0, 0)), pl.BlockSpec((C, 1024), lambda i: (0, 0)),
                   pl.BlockSpec((C, 512), lambda i: (0, 0))],
        out_shape=[jax.ShapeDtypeStruct((C, D_MODEL), BF16), jax.ShapeDtypeStruct((C, 1024), BF16),
                   jax.ShapeDtypeStruct((C, 512), BF16)],
        compiler_params=_params(("arbitrary",)),
    )(ctx, ng, cscale, cshift, w_in_t, ones_blk, gk)


def _sgu_chunk_fwd(au, av, ag, sg, ws_bf, bsb):
    gu, dgu = _gelu_parts(au)
    gv, dgv = _gelu_parts(av)
    rr = lax.rsqrt(jnp.mean(gv * gv, axis=-1, keepdims=True) + EPS)
    vhat = gv * rr
    vn = vhat * sg
    mixed = jnp.dot(ws_bf, vn.astype(BF16), preferred_element_type=F32) + bsb
    sig = _sigmoid(ag)
    sl = ag * sig
    return gu * mixed * sl, (gu, dgu, dgv, rr, vhat, vn, mixed, sig, sl)


def _sgu_fwd(z, sgn, ws, bsb, tm):
    T = z.shape[0]

    def body(au_ref, av_ref, ag_ref, sg_ref, ws_ref, bsb_ref, o_ref):
        for g in range(SGU_GROUPS):
            ws_bf = ws_ref[g].astype(BF16)
            sg = sg_ref[:, 128 * g:128 * (g + 1)]
            bsb_g = bsb_ref[g]
            for j in range(tm // CHUNK):
                rs, cs = slice(CHUNK * j, CHUNK * (j + 1)), slice(128 * g, 128 * (g + 1))
                out, _ = _sgu_chunk_fwd(au_ref[rs, cs].astype(F32), av_ref[rs, cs].astype(F32),
                                        ag_ref[rs, cs].astype(F32), sg, ws_bf, bsb_g)
                o_ref[rs, cs] = out.astype(BF16)

    return pl.pallas_call(
        body, name="sgu_fwd", grid=(T // tm,),
        in_specs=[pl.BlockSpec((tm, 512), lambda i: (i, 0)), pl.BlockSpec((tm, 512), lambda i: (i, 1)),
                  pl.BlockSpec((tm, 512), lambda i: (i, 2)), pl.BlockSpec((1, 512), lambda i: (0, 0)),
                  pl.BlockSpec((SGU_GROUPS, CHUNK, CHUNK), lambda i: (0, 0, 0)),
                  pl.BlockSpec((SGU_GROUPS, CHUNK, 128), lambda i: (0, 0, 0))],
        out_specs=pl.BlockSpec((tm, 512), lambda i: (i, 0)),
        out_shape=jax.ShapeDtypeStruct((T, 512), BF16),
        compiler_params=_params(("arbitrary",)),
    )(z, z, z, sgn, ws, bsb)


def _attn_type(rb, nrb):
    return jnp.where(rb == 0, 0, jnp.where(rb == nrb - 1, 2, 1))


def _attn_specs(T, C):
    return [
        pl.BlockSpec((ATT_SUB * TQ, 128), lambda hp, st: (st, hp)),
        pl.BlockSpec((T, 128), lambda hp, st: (0, hp)),
        pl.BlockSpec((T, 128), lambda hp, st: (0, 20 + hp)),
        pl.BlockSpec((C, 128), lambda hp, st: (0, hp)),
        pl.BlockSpec((C, 128), lambda hp, st: (0, 4 + hp)),
        pl.BlockSpec((2, 2 * WIN_R, GRID_W, 128), lambda hp, st: (hp, 0, 0, 0)),
        pl.BlockSpec((3, TQ, TK), lambda hp, st: (0, 0, 0)),
        pl.BlockSpec((ATT_SUB * TQ, 128), lambda hp, st: (st, 24 + hp)),
    ]


def _build_bias(pairs_ref, mask_ref, bias_sc):
    for t in range(3):
        for hh in range(2):
            for i in range(Q_ROWS):
                for mm in range(K_ROWS // 2):
                    p = min(max(WIN_R - Q_ROWS * t + 2 * mm - i, 0), 2 * WIN_R - 1)
                    rs, cs = slice(GRID_W * i, GRID_W * (i + 1)), slice(128 * mm, 128 * (mm + 1))
                    bias_sc[t, hh, rs, cs] = (pairs_ref[hh, p] + mask_ref[t, rs, cs]) * LOG2E


def _attn_fwd(qs, kn, z, ckn, zc, pairs, row_mask):
    T, C = qs.shape[0], ckn.shape[0]
    rows = T // GRID_W
    nrb = rows // Q_ROWS

    def body(q_ref, k_ref, v_ref, ck_ref, cv_ref, pairs_ref, mask_ref, bg_ref, ob_ref, outb_ref, lse_ref, bias_sc):
        @pl.when(pl.program_id(1) == 0)
        def _():
            _build_bias(pairs_ref, mask_ref, bias_sc)

        ck2, cv2 = ck_ref[...], cv_ref[...]
        lane = lax.broadcasted_iota(jnp.int32, (1, 128), 1)
        for sub in range(ATT_SUB):
            rb = ATT_SUB * pl.program_id(1) + sub
            bias_ref = bias_sc.at[_attn_type(rb, nrb)]
            rs = slice(TQ * sub, TQ * (sub + 1))
            ks = pl.multiple_of(jnp.clip(Q_ROWS * rb - 4, 0, rows - K_ROWS) * GRID_W, GRID_W)
            q2 = q_ref[rs, :]
            k2 = k_ref[pl.ds(ks, TK), :]
            v2 = v_ref[pl.ds(ks, TK), :]
            o_acc = jnp.zeros((TQ, 128), F32)
            lse_acc = jnp.zeros((TQ, 128), F32)
            for hh in range(2):
                msk = (lane >= HEAD_DIM) == bool(hh)
                qm = jnp.where(msk, q2, jnp.zeros_like(q2))
                s = _nt(qm, k2) + bias_ref[hh]
                sc = _nt(qm, ck2)
                m = jnp.maximum(jnp.max(s, axis=-1, keepdims=True), jnp.max(sc, axis=-1, keepdims=True))
                p = jnp.exp2(s - m)
                pc = jnp.exp2(sc - m)
                va = jnp.where(msk, v2, jnp.ones_like(v2))
                cva = jnp.where(msk, cv2, jnp.ones_like(cv2))
                num = (jnp.dot(p.astype(BF16), va, preferred_element_type=F32)
                       + jnp.dot(pc.astype(BF16), cva, preferred_element_type=F32))
                den = pltpu.roll(num, HEAD_DIM, 1)
                o_acc = jnp.where(msk, num / den, o_acc)
                lse_acc = jnp.where(msk, m + jnp.log(den) * LOG2E, lse_acc)
            ob_ref[rs, :] = o_acc.astype(BF16)
            lse_ref[rs, :] = lse_acc
            bg = bg_ref[rs, :].astype(F32)
            outb_ref[rs, :] = (o_acc * (bg * _sigmoid(bg))).astype(BF16)

    tile = pl.BlockSpec((ATT_SUB * TQ, 128), lambda hp, st: (st, hp))
    return pl.pallas_call(
        body, name="attn_fwd", grid=(4, nrb // ATT_SUB),
        in_specs=_attn_specs(T, C),
        out_specs=[tile, tile, tile],
        out_shape=[jax.ShapeDtypeStruct((T, 512), BF16), jax.ShapeDtypeStruct((T, 512), BF16),
                   jax.ShapeDtypeStruct((T, 512), F32)],
        scratch_shapes=[pltpu.VMEM((3, 2, TQ, TK), F32)],
        compiler_params=_params(("arbitrary", "arbitrary")),
    )(qs, kn, z, ckn, zc, pairs, row_mask, z)


def _outproj_loss_bwd(x, tgt, out_a, out_b, gate, w_out, tm):
    T = x.shape[0]
    nt = T // tm

    def body(x_ref, t_ref, oa_ref, ob_ref, gate_ref, w_ref, dy_ref, dmc_ref, dw_ref, dgate_ref, loss_ref, acc):
        @pl.when(pl.program_id(0) == 0)
        def _():
            acc[...] = jnp.zeros_like(acc)
            dgate_ref[...] = jnp.zeros_like(dgate_ref)
            loss_ref[...] = jnp.zeros_like(loss_ref)

        oa, ob = oa_ref[...], ob_ref[...]
        gate_v = gate_ref[...]
        mix = (jnp.dot(oa, w_ref[0:512, :], preferred_element_type=F32)
               + jnp.dot(ob, w_ref[512:1024, :], preferred_element_type=F32))
        e = x_ref[...] + gate_v * mix - t_ref[...]
        se = jnp.sum(jnp.sum(e * e, axis=0, keepdims=True), axis=1, keepdims=True)
        loss_ref[...] += jnp.broadcast_to(se * (0.5 / D_MODEL), loss_ref.shape)
        dy = e * (1.0 / D_MODEL)
        dy_ref[...] = dy
        dgate_ref[...] += jnp.sum(dy * mix, axis=0, keepdims=True)
        dmix = (dy * gate_v).astype(BF16)
        dmc_ref[...] = _nt(dmix, w_ref[...]).astype(BF16)
        acc[0:512, :] += _tn(oa, dmix)
        acc[512:1024, :] += _tn(ob, dmix)

        @pl.when(pl.program_id(0) == nt - 1)
        def _():
            dw_ref[...] = acc[...].astype(BF16)

    row = lambda w: pl.BlockSpec((tm, w), lambda i: (i, 0))
    return pl.pallas_call(
        body, name="outproj_loss_bwd", grid=(nt,),
        in_specs=[row(D_MODEL), row(D_MODEL), row(512), row(512),
                  pl.BlockSpec((1, D_MODEL), lambda i: (0, 0)),
                  pl.BlockSpec((D_MODEL, D_MODEL), lambda i: (0, 0))],
        out_specs=[row(D_MODEL), row(D_MODEL), pl.BlockSpec((D_MODEL, D_MODEL), lambda i: (0, 0)),
                   pl.BlockSpec((1, D_MODEL), lambda i: (0, 0)), pl.BlockSpec((1, 128), lambda i: (0, 0))],
        out_shape=[jax.ShapeDtypeStruct((T, D_MODEL), F32), jax.ShapeDtypeStruct((T, D_MODEL), BF16),
                   jax.ShapeDtypeStruct((D_MODEL, D_MODEL), BF16), jax.ShapeDtypeStruct((1, D_MODEL), F32),
                   jax.ShapeDtypeStruct((1, 128), F32)],
        scratch_shapes=[pltpu.VMEM((D_MODEL, D_MODEL), F32)],
        compiler_params=_params(("arbitrary",)),
    )(x, tgt, out_a, out_b, gate, w_out)


def _attn_bwd(qs, kn, z, ckn, zc, pairs, row_mask, ob, lse, dmc, hsel, fold, riders):
    T, C = qs.shape[0], ckn.shape[0]
    rows = T // GRID_W
    nrb = rows // Q_ROWS
    n_st = nrb // ATT_SUB
    n_rid = len(riders)

    def body(q_ref, k_ref, v_ref, ck_ref, cv_ref, pairs_ref, mask_ref, bg_ref, ob_ref, lse_ref, do_ref,
             hsel_ref, fold_ref, *rest):
        rid_src = rest[:n_rid]
        dq_ref, dk_ref, dv_ref, dck_ref, dcv_ref, dbg_ref, drpb_ref = rest[n_rid:n_rid + 7]
        rid_dst = rest[n_rid + 7:2 * n_rid + 7]
        bias_sc, dacc_ref = rest[2 * n_rid + 7:2 * n_rid + 9]
        rid_sems = rest[2 * n_rid + 9:]
        hp, st = pl.program_id(0), pl.program_id(1)

        @pl.when(jnp.logical_and(hp == 0, st == 0))
        def _():
            for r, (ex, _) in enumerate(riders):
                ex.start(rid_src[r], rid_dst[r], *rid_sems[3 * r:3 * r + 3])

        @pl.when(st == 0)
        def _():
            _build_bias(pairs_ref, mask_ref, bias_sc)
            dk_ref[...] = jnp.zeros_like(dk_ref)
            dv_ref[...] = jnp.zeros_like(dv_ref)
            dck_ref[...] = jnp.zeros_like(dck_ref)
            dcv_ref[...] = jnp.zeros_like(dcv_ref)
            dacc_ref[...] = jnp.zeros_like(dacc_ref)

        ck2, cv2 = ck_ref[...], cv_ref[...]
        lane = lax.broadcasted_iota(jnp.int32, (1, 128), 1)
        for sub in range(ATT_SUB):
            rb = ATT_SUB * st + sub
            bias_ref = bias_sc.at[_attn_type(rb, nrb)]
            rs = slice(TQ * sub, TQ * (sub + 1))
            kb = jnp.clip(Q_ROWS * rb - 4, 0, rows - K_ROWS)
            ks = pl.multiple_of(kb * GRID_W, GRID_W)
            ebase = kb - Q_ROWS * rb + 11
            q2 = q_ref[rs, :]
            k2 = k_ref[pl.ds(ks, TK), :]
            v2 = v_ref[pl.ds(ks, TK), :]
            bg = bg_ref[rs, :].astype(F32)
            sig = _sigmoid(bg)
            obv = ob_ref[rs, :].astype(F32)
            dout = do_ref[rs, :].astype(F32)
            dbg_ref[rs, :] = (dout * obv * (sig * (1.0 + bg * (1.0 - sig)))).astype(BF16)
            d_o = dout * (bg * sig)
            d_oo = d_o * obv
            lse2 = lse_ref[rs, :]
            dq_acc = jnp.zeros((TQ, 128), F32)
            for hh in range(2):
                msk = (lane >= HEAD_DIM) == bool(hh)
                qm = jnp.where(msk, q2, jnp.zeros_like(q2))
                lse_h = jnp.max(jnp.where(msk, lse2, -jnp.inf), axis=-1, keepdims=True)
                p = jnp.exp2(_nt(qm, k2) + bias_ref[hh] - lse_h)
                pc = jnp.exp2(_nt(qm, ck2) - lse_h)
                dom_f = jnp.where(msk, d_o, 0.0)
                dom = dom_f.astype(BF16)
                delta = jnp.sum(jnp.where(msk, d_oo, 0.0), axis=-1, keepdims=True)
                d_hi = delta.astype(BF16).astype(F32)
                x0 = HEAD_DIM * (1 - hh)
                dom_aug = jnp.where(lane == x0, -d_hi, jnp.where(lane == x0 + 1, d_hi - delta, dom_f)).astype(BF16)
                extra = jnp.logical_or(lane == x0, lane == x0 + 1)
                va = jnp.where(msk, v2, jnp.where(extra, jnp.ones_like(v2), jnp.zeros_like(v2)))
                cva = jnp.where(msk, cv2, jnp.where(extra, jnp.ones_like(cv2), jnp.zeros_like(cv2)))
                ds = p * _nt(dom_aug, va)
                dsc = pc * _nt(dom_aug, cva)
                dsb, dscb = ds.astype(BF16), dsc.astype(BF16)
                dq_h = (jnp.dot(dsb, k2, preferred_element_type=F32)
                        + jnp.dot(dscb, ck2, preferred_element_type=F32))
                dq_acc = jnp.where(msk, dq_h, dq_acc)
                dk_ref[pl.ds(ks, TK), :] += _tn(dsb, qm)
                dv_ref[pl.ds(ks, TK), :] += _tn(p.astype(BF16), dom)
                dck_ref[...] += _tn(dscb, qm)
                dcv_ref[...] += _tn(pc.astype(BF16), dom)
                for i in range(Q_ROWS):
                    for mm in range(K_ROWS // 2):
                        dacc_ref[hh, ebase + (2 * mm - i)] += ds[GRID_W * i:GRID_W * (i + 1),
                                                                 128 * mm:128 * (mm + 1)]
            dq_ref[rs, :] = dq_acc

        @pl.when(st == n_st - 1)
        def _():
            for hh in range(2):
                drpb_ref[hh] = _rpb_diag_sums(dacc_ref.at[hh], hsel_ref, fold_ref)

        @pl.when(jnp.logical_and(hp == pl.num_programs(0) - 1, st == n_st - 1))
        def _():
            for r, (ex, _) in enumerate(riders):
                ex.wait(rid_src[r], rid_dst[r], *rid_sems[3 * r:3 * r + 3])

    tile = pl.BlockSpec((ATT_SUB * TQ, 128), lambda hp, st: (st, hp))
    colT = pl.BlockSpec((T, 128), lambda hp, st: (0, hp))
    colC = pl.BlockSpec((C, 128), lambda hp, st: (0, hp))
    res = pl.pallas_call(
        body, name="attn_bwd", grid=(4, n_st),
        in_specs=(_attn_specs(T, C) + [tile, tile, pl.BlockSpec((ATT_SUB * TQ, 128), lambda hp, st: (st, 4 + hp)),
                                       pl.BlockSpec((8, 128, 128), lambda hp, st: (0, 0, 0)),
                                       pl.BlockSpec((32, 256), lambda hp, st: (0, 0))]
                  + [ANY_SPEC] * n_rid),
        out_specs=([tile, colT, colT, colC, colC, tile, pl.BlockSpec((2, 32, 128), lambda hp, st: (hp, 0, 0))]
                   + [ANY_SPEC] * n_rid),
        out_shape=([jax.ShapeDtypeStruct((T, 512), F32), jax.ShapeDtypeStruct((T, 512), F32),
                    jax.ShapeDtypeStruct((T, 512), F32), jax.ShapeDtypeStruct((C, 512), F32),
                    jax.ShapeDtypeStruct((C, 512), F32), jax.ShapeDtypeStruct((T, 512), BF16),
                    jax.ShapeDtypeStruct((N_HEADS, 32, 128), F32)] + [ex.out_shape for ex, _ in riders]),
        scratch_shapes=([pltpu.VMEM((3, 2, TQ, TK), F32), pltpu.VMEM((2, N_DIAG, GRID_W, 128), F32)]
                        + [s for ex, _ in riders for s in ex.scratch]),
        compiler_params=_params(("arbitrary", "arbitrary")),
    )(qs, kn, z, ckn, zc, pairs, row_mask, z, ob, lse, dmc, hsel, fold, *[a for _, a in riders])
    return res[:7], res[7:]


def _rpb_diag_sums(a_ref, hsel_ref, fold_ref):
    n_off = 2 * WIN_C - 1
    n_dr = 2 * WIN_R - 1
    qc = lax.broadcasted_iota(jnp.int32, (GRID_W, 128), 0)
    lane = lax.broadcasted_iota(jnp.int32, (GRID_W, 128), 1)
    diff = lane % GRID_W - qc + (WIN_C - 1)
    left = lane < GRID_W

    def by_dr(dr):
        return a_ref[dr + 4] + pltpu.roll(a_ref[dr + 3], GRID_W, 1)

    out = jnp.zeros((32, 128), F32)
    for j in range((n_dr + 1) // 2):
        hi = pltpu.roll(by_dr(2 * j + 1), GRID_W, 1) if 2 * j + 1 < n_dr else 0.0
        pair = jnp.where(left, by_dr(2 * j), hi)
        parts = []
        for o in range(n_off):
            mv = jnp.where(diff == o, pair, 0.0)
            acc = mv[0:8]
            for r8 in range(1, GRID_W // 8):
                acc = acc + mv[8 * r8:8 * (r8 + 1)]
            parts.append(acc)
        parts.append(jnp.zeros((8, 128), F32))
        stack = jnp.concatenate(parts, axis=0)
        s_hi = stack.astype(BF16)
        s_lo = (stack - s_hi.astype(F32)).astype(BF16)
        per_o = (jnp.dot(fold_ref[...], s_hi, preferred_element_type=F32)
                 + jnp.dot(fold_ref[...], s_lo, preferred_element_type=F32))
        out = out + _dot2(per_o, hsel_ref[j])
    return out


def _head_norm_bwd(raw, dn, gain, ones_ref):
    rr = lax.rsqrt(_head_sum(raw * raw, ones_ref) * (1.0 / HEAD_DIM) + EPS)
    hat = raw * rr
    dgain = jnp.sum(dn * hat, axis=0, keepdims=True)
    dhat = dn * gain
    mean = _head_sum(dhat * hat, ones_ref) * (1.0 / HEAD_DIM)
    return rr * (dhat - hat * mean), dgain


def _ctx_k_bwd(zc, dck, dcv, ones_blk, gk, foldm):
    C = dck.shape[0]

    def body(bk_ref, dk_ref, dv_ref, ones_ref, gk_ref, fold_ref, dbk_ref, dbv_ref, dgk_ref):
        dbk, dgk = _head_norm_bwd(bk_ref[...].astype(F32), dk_ref[...] * LN2, gk_ref[...], ones_ref)
        dbk_ref[...] = dbk.astype(BF16)
        dbv_ref[...] = dv_ref[...].astype(BF16)
        dgk_ref[...] = jnp.dot(jnp.broadcast_to(dgk, (8, 512)), fold_ref[...],
                               preferred_element_type=F32, precision=HI)

    row = pl.BlockSpec((C, 512), lambda i: (0, 0))
    cst = lambda a, b: pl.BlockSpec((a, b), lambda i: (0, 0))
    out_row = jax.ShapeDtypeStruct((C, 512), BF16)
    return pl.pallas_call(
        body, name="ctx_k_bwd", grid=(1,),
        in_specs=[row, row, row, cst(512, 512), cst(1, 512), cst(512, 128)],
        out_specs=[row, row, cst(8, 128)],
        out_shape=[out_row, out_row, jax.ShapeDtypeStruct((8, 128), F32)],
        compiler_params=_params(("arbitrary",)),
    )(zc, dck, dcv, ones_blk, gk, foldm)


def _bwd_mid(z, dmc, dqs, dk, dv, db_g, h, hc, dzc_k, dzc_v, sgn, ws, wst, bsb, ones8, ones_blk, gq, gk, foldm, tk,
             riders=()):
    T = z.shape[0]
    nt = T // tk
    blk = D_IN // N_DEV
    n_in, n_out, n_sc = 23, 7, 9
    n_rid = len(riders)

    def body(*refs):
        (au_ref, av_ref, ag_ref, bq_ref, bk_ref, d_ref, dq_ref, dk_ref, dv_ref, dbg_ref, h_ref,
         hc_ref, dzck_ref, dzcv_ref, sg_ref, ws_ref, wst_ref, bsb_ref, ones8_ref, ones_ref, gq_ref, gk_ref,
         fold_ref) = refs[:n_in]
        rid_src = refs[n_in:n_in + n_rid]
        dz_ref, sums_out, dws_ref, dbs_ref, dsg_ref, dgq_ref, dgk_ref = refs[n_in + n_rid:n_in + n_rid + n_out]
        rid_dst = refs[n_in + n_rid + n_out:n_in + 2 * n_rid + n_out]
        (acc, accq, acck, stage, sem, send_buf, tmp, s1, r1) = refs[n_in + 2 * n_rid + n_out:
                                                                   n_in + 2 * n_rid + n_out + n_sc]
        rid_sems = refs[n_in + 2 * n_rid + n_out + n_sc:]
        t = pl.program_id(0)

        @pl.when(t == 0)
        def _():
            for r, (ex, _) in enumerate(riders):
                ex.start(rid_src[r], rid_dst[r], *rid_sems[3 * r:3 * r + 3])
            acc[...] = jnp.zeros_like(acc)
            acc[512 * 4:512 * 5, :] = _tn(dzck_ref[...], hc_ref[...])
            acc[512 * 5:512 * 6, :] = _tn(dzcv_ref[...], hc_ref[...])
            dws_ref[...] = jnp.zeros_like(dws_ref)
            dbs_ref[...] = jnp.zeros_like(dbs_ref)
            dsg_ref[...] = jnp.zeros_like(dsg_ref)
            accq[...] = jnp.zeros_like(accq)
            acck[...] = jnp.zeros_like(acck)

        for g in range(SGU_GROUPS):
            ws_bf = ws_ref[g].astype(BF16)
            wst_bf = wst_ref[g].astype(BF16)
            sg = sg_ref[:, 128 * g:128 * (g + 1)]
            bsb_g = bsb_ref[g]
            for j in range(tk // CHUNK):
                rs, cs = slice(CHUNK * j, CHUNK * (j + 1)), slice(128 * g, 128 * (g + 1))
                au, av, ag = (au_ref[rs, cs].astype(F32), av_ref[rs, cs].astype(F32), ag_ref[rs, cs].astype(F32))
                d = d_ref[rs, cs].astype(F32)
                _, (gu, dgu, dgv, rr, vhat, vn, mixed, sig, sl) = _sgu_chunk_fwd(au, av, ag, sg, ws_bf, bsb_g)
                dz_ref[rs, 128 * g:128 * (g + 1)] = (d * mixed * sl * dgu).astype(BF16)
                dz_ref[rs, 1024 + 128 * g:1024 + 128 * (g + 1)] = (
                    d * gu * mixed * (sig * (1.0 + ag * (1.0 - sig)))).astype(BF16)
                dmixed = d * gu * sl
                dmb = dmixed.astype(BF16)
                dm_lo = (dmixed - dmb.astype(F32)).astype(BF16)
                dbs_ref[g] += _nt(ones8_ref[...], dmb) + _nt(ones8_ref[...], dm_lo)
                dws_ref[g] += _nt(dmb, vn.astype(BF16))
                dvn = jnp.dot(wst_bf, dmb, preferred_element_type=F32)
                dsg_ref[:, 128 * g:128 * (g + 1)] += jnp.sum(dvn * vhat, axis=0, keepdims=True)
                dvhat = dvn * sg
                mean = jnp.mean(dvhat * vhat, axis=-1, keepdims=True)
                dz_ref[rs, 512 + 128 * g:512 + 128 * (g + 1)] = (rr * (dvhat - vhat * mean) * dgv).astype(BF16)

        dbq, dgq = _head_norm_bwd(bq_ref[...].astype(F32), dq_ref[...] * ATT_SCALE, gq_ref[...], ones_ref)
        dz_ref[:, 512 * 3:512 * 4] = dbq.astype(BF16)
        accq[...] += dgq
        dbk, dgk = _head_norm_bwd(bk_ref[...].astype(F32), dk_ref[...] * LN2, gk_ref[...], ones_ref)
        dz_ref[:, 512 * 4:512 * 5] = dbk.astype(BF16)
        acck[...] += dgk
        dz_ref[:, 512 * 5:512 * 6] = dv_ref[...].astype(BF16)
        dz_ref[:, 512 * 6:512 * 7] = dbg_ref[...]

        hv = h_ref[...]
        for k in range(N_BRANCH):
            acc[512 * k:512 * (k + 1), :] += _tn(dz_ref[:, 512 * k:512 * (k + 1)], hv)

        @pl.when(t == nt - 1)
        def _():
            dgq_ref[...] = jnp.dot(jnp.broadcast_to(accq[...], (8, 512)), fold_ref[...],
                                   preferred_element_type=F32, precision=HI)
            dgk_ref[...] = jnp.dot(jnp.broadcast_to(acck[...], (8, 512)), fold_ref[...],
                                   preferred_element_type=F32, precision=HI)
            cidx = lax.axis_index("c")
            sib = (lax.axis_index("x"), lax.axis_index("y"), 1 - cidx)
            swaps = []
            for q in range(N_DEV // 2):
                theirs = acc[pl.ds(pl.multiple_of(2 * blk * q + blk * (1 - cidx), 8), blk), :]
                send_buf[q] = theirs.astype(BF16)
                cp = pltpu.make_async_remote_copy(src_ref=send_buf.at[q], dst_ref=tmp.at[q], send_sem=s1.at[q],
                                                  recv_sem=r1.at[q], device_id=sib, device_id_type=MESH)
                cp.start()
                swaps.append(cp)
            for q in range(N_DEV // 2):
                swaps[q].wait_recv()
                mine = acc[pl.ds(pl.multiple_of(2 * blk * q + blk * cidx, 8), blk), :]
                stage[...] = (mine + tmp[q].astype(F32)).astype(BF16)
                out = pltpu.make_async_copy(stage, sums_out.at[q], sem)
                out.start()
                out.wait()
            for cp in swaps:
                cp.wait_send()
            for r, (ex, _) in enumerate(riders):
                ex.wait(rid_src[r], rid_dst[r], *rid_sems[3 * r:3 * r + 3])

    zcol = lambda col: pl.BlockSpec((tk, 512), lambda t: (t, col))
    row = pl.BlockSpec((tk, 512), lambda t: (t, 0))
    whole = lambda a: pl.BlockSpec(a.shape, lambda t: (0,) * a.ndim)
    res = pl.pallas_call(
        body, name="bwd_mid", grid=(nt,),
        in_specs=[zcol(0), zcol(1), zcol(2), zcol(3), zcol(4), row, row, row, row, row,
                  pl.BlockSpec((tk, D_MODEL), lambda t: (t, 0)), whole(hc), whole(dzc_k), whole(dzc_v),
                  whole(sgn), whole(ws), whole(wst), whole(bsb), whole(ones8), whole(ones_blk), whole(gq), whole(gk),
                  whole(foldm)] + [ANY_SPEC] * n_rid,
        out_specs=[pl.BlockSpec((tk, D_IN), lambda t: (t, 0)), ANY_SPEC,
                   pl.BlockSpec((SGU_GROUPS, CHUNK, CHUNK), lambda t: (0, 0, 0)),
                   pl.BlockSpec((SGU_GROUPS, 8, CHUNK), lambda t: (0, 0, 0)),
                   pl.BlockSpec((1, 512), lambda t: (0, 0)), pl.BlockSpec((8, 128), lambda t: (0, 0)),
                   pl.BlockSpec((8, 128), lambda t: (0, 0))] + [ANY_SPEC] * n_rid,
        out_shape=[jax.ShapeDtypeStruct((T, D_IN), BF16), jax.ShapeDtypeStruct((N_DEV // 2, blk, D_MODEL), BF16),
                   jax.ShapeDtypeStruct((SGU_GROUPS, CHUNK, CHUNK), F32),
                   jax.ShapeDtypeStruct((SGU_GROUPS, 8, CHUNK), F32), jax.ShapeDtypeStruct((1, 512), F32),
                   jax.ShapeDtypeStruct((8, 128), F32), jax.ShapeDtypeStruct((8, 128), F32)]
                  + [ex.out_shape for ex, _ in riders],
        scratch_shapes=[pltpu.VMEM((D_IN, D_MODEL), F32), pltpu.VMEM((1, 512), F32), pltpu.VMEM((1, 512), F32),
                        pltpu.VMEM((blk, D_MODEL), BF16), pltpu.SemaphoreType.DMA,
                        pltpu.VMEM((N_DEV // 2, blk, D_MODEL), BF16), pltpu.VMEM((N_DEV // 2, blk, D_MODEL), BF16),
                        pltpu.SemaphoreType.DMA((N_DEV // 2,)), pltpu.SemaphoreType.DMA((N_DEV // 2,))]
                       + [s for ex, _ in riders for s in ex.scratch],
        compiler_params=_params(("arbitrary",)),
    )(z, z, z, z, z, dmc, dqs, dk, dv, db_g, h, hc, dzc_k, dzc_v, sgn, ws, wst, bsb, ones8, ones_blk, gq, gk, foldm,
      *[a for _, a in riders])
    return res[:n_out], res[n_out:]


def _inproj_bwd_dx(dzs, w_row0, w_in_t, x, dy, ng, scale, shift, tm, name, riders=()):
    T = x.shape[0]
    n = len(dzs)
    wpc = dzs[0].shape[1]
    nt = T // tm
    with_dx = dy is not None
    n_own_in = n + 5 + with_dx
    n_own_out = 3 + with_dx
    n_rid = len(riders)

    def body(*refs):
        dz_refs = refs[:n]
        n_in = n_own_in + n_rid
        own = refs[n:n_own_in] + refs[n_in:n_in + n_own_out]
        rid_src = refs[n_own_in:n_own_in + n_rid]
        rid_dst = refs[n_in + n_own_out:n_in + n_own_out + n_rid]
        rid_sems = refs[n_in + n_own_out + n_rid:]
        if with_dx:
            w_ref, x_ref, dy_ref, g_ref, sc_ref, sh_ref, gx_ref, dsh_ref, dsc_ref, dg_ref = own
        else:
            w_ref, x_ref, g_ref, sc_ref, sh_ref, dsh_ref, dsc_ref, dg_ref = own

        @pl.when(pl.program_id(0) == 0)
        def _():
            for r, (ex, _) in enumerate(riders):
                ex.start(rid_src[r], rid_dst[r], *rid_sems[3 * r:3 * r + 3])
            dsh_ref[...] = jnp.zeros_like(dsh_ref)
            dsc_ref[...] = jnp.zeros_like(dsc_ref)
            dg_ref[...] = jnp.zeros_like(dg_ref)

        dh = jnp.dot(dz_refs[0][...], w_ref[0:wpc, :], preferred_element_type=F32)
        for k in range(1, n):
            dh = dh + jnp.dot(dz_refs[k][...], w_ref[wpc * k:wpc * (k + 1), :], preferred_element_type=F32)
        xv = x_ref[...]
        r = lax.rsqrt(jnp.mean(xv * xv, axis=-1, keepdims=True) + EPS)
        xn = xv * r
        gv, op = g_ref[...], 1.0 + sc_ref[...]
        dsh_ref[...] += jnp.sum(dh, axis=0, keepdims=True)
        dsc_ref[...] += jnp.sum(dh * xn * gv, axis=0, keepdims=True)
        dg_ref[...] += jnp.sum(dh * op * xn, axis=0, keepdims=True)
        if with_dx:
            dxn = dh * (gv * op)
            gx_ref[...] = r * (dxn - xn * jnp.mean(dxn * xn, axis=-1, keepdims=True)) + dy_ref[...]

        @pl.when(pl.program_id(0) == nt - 1)
        def _():
            for r, (ex, _) in enumerate(riders):
                ex.wait(rid_src[r], rid_dst[r], *rid_sems[3 * r:3 * r + 3])

    vec = pl.BlockSpec((1, D_MODEL), lambda i: (0, 0))
    rowf = pl.BlockSpec((tm, D_MODEL), lambda i: (i, 0))
    in_specs = [pl.BlockSpec((tm, wpc), lambda i: (i, 0))] * n
    in_specs += [pl.BlockSpec((wpc * n, D_MODEL), lambda i: (w_row0 // (wpc * n), 0)), rowf]
    args = list(dzs) + [w_in_t, x]
    vshape = jax.ShapeDtypeStruct((1, D_MODEL), F32)
    out_specs, out_shape = [vec, vec, vec], [vshape, vshape, vshape]
    if with_dx:
        in_specs.append(rowf)
        args.append(dy)
        out_specs = [rowf] + out_specs
        out_shape = [jax.ShapeDtypeStruct((T, D_MODEL), F32)] + out_shape
    in_specs += [vec, vec, vec] + [ANY_SPEC] * n_rid
    args += [ng, scale, shift] + [a for _, a in riders]
    res = pl.pallas_call(
        body, name=name, grid=(nt,), in_specs=in_specs, out_specs=out_specs + [ANY_SPEC] * n_rid,
        out_shape=out_shape + [ex.out_shape for ex, _ in riders],
        scratch_shapes=[s for ex, _ in riders for s in ex.scratch],
        compiler_params=_params(("arbitrary",)),
    )(*args)
    return res[:n_own_out], res[n_own_out:]


def _adamw_sharded(w, gparts, m, v, tr, name, riders=()):
    R, C = w.shape
    n_part = gparts.shape[0]
    nt = R // tr
    n_rid = len(riders)

    def body(w_ref, gp_ref, m_ref, v_ref, *rest):
        rid_src = rest[:n_rid]
        g_ref, d_ref, m2_ref, v2_ref = rest[n_rid:n_rid + 4]
        rid_dst = rest[n_rid + 4:2 * n_rid + 4]
        rid_sems = rest[2 * n_rid + 4:]

        @pl.when(pl.program_id(0) == 0)
        def _():
            for r, (ex, _) in enumerate(riders):
                ex.start(rid_src[r], rid_dst[r], *rid_sems[3 * r:3 * r + 3])

        g = gp_ref[0].astype(F32)
        for d in range(1, n_part):
            g = g + gp_ref[d].astype(F32)
        delta, m2, v2 = _adam(w_ref[...], g, m_ref[...], v_ref[...])
        g_ref[...] = g
        d_ref[...] = delta
        m2_ref[...] = m2
        v2_ref[...] = v2

        @pl.when(pl.program_id(0) == nt - 1)
        def _():
            for r, (ex, _) in enumerate(riders):
                ex.wait(rid_src[r], rid_dst[r], *rid_sems[3 * r:3 * r + 3])

    row = pl.BlockSpec((tr, C), lambda i: (i, 0))
    sh = jax.ShapeDtypeStruct((R, C), F32)
    res = pl.pallas_call(
        body, name=name, grid=(nt,),
        in_specs=[row, pl.BlockSpec((n_part, tr, C), lambda i: (0, i, 0)), row, row] + [ANY_SPEC] * n_rid,
        out_specs=[row, row, row, row] + [ANY_SPEC] * n_rid,
        out_shape=[sh, sh, sh, sh] + [ex.out_shape for ex, _ in riders],
        scratch_shapes=[s for ex, _ in riders for s in ex.scratch],
        compiler_params=_params(("arbitrary",)),
    )(w, gparts, m, v, *[a for _, a in riders])
    return res[:4], res[4:]


def _pack_vectors(vec_rows, dsg, dgq, dgk, dgk_c, loss_part):
    flat = [a for _, arrs in vec_rows for a in arrs]

    def body(*refs):
        vecs = list(refs[:len(flat)])
        dsg_ref, dgq_ref, dgk_ref, dgkc_ref, loss_ref, v_ref = refs[len(flat):]
        row = lax.broadcasted_iota(jnp.int32, (16, D_MODEL), 0)
        misc = jnp.concatenate([dsg_ref[...], dgq_ref[0:1, :], dgk_ref[0:1, :], dgkc_ref[0:1, :],
                                loss_ref[...]], axis=1)
        v = jnp.where(row == V_MISC, jnp.broadcast_to(misc, (16, D_MODEL)), 0.0)
        for r, arrs in vec_rows:
            val = vecs.pop(0)[...]
            for _ in arrs[1:]:
                val = val + vecs.pop(0)[...]
            v = jnp.where(row == r, jnp.broadcast_to(val, (16, D_MODEL)), v)
        v_ref[...] = v

    return pl.pallas_call(
        body, name="pack_vectors", out_shape=jax.ShapeDtypeStruct((16, D_MODEL), F32), compiler_params=_params(),
    )(*flat, dsg, dgq, dgk, dgk_c, loss_part)


SMALL_NAMES = ("b_ada", "norm_g", "sgu_norm_g", "w_spatial", "b_spatial", "q_norm_g", "k_norm_g", "rpb")


def _adamw_small(vg, sg, rg, ws, ms, vs):
    k = len(SMALL_NAMES)

    def body(*refs):
        vg_ref, sg_ref, rg_ref = refs[0], refs[1], refs[2]
        refs = refs[1:]
        w_refs = dict(zip(SMALL_NAMES, refs[2:2 + k]))
        m_refs = dict(zip(SMALL_NAMES, refs[2 + k:2 + 2 * k]))
        v_refs = dict(zip(SMALL_NAMES, refs[2 + 2 * k:2 + 3 * k]))
        o_refs = [dict(zip(SMALL_NAMES, refs[2 + (3 + i) * k:2 + (4 + i) * k])) for i in range(4)]
        loss_ref = refs[2 + 7 * k]

        sv = vg_ref[0]
        for d in range(1, N_DEV):
            sv = sv + vg_ref[d]
        loss_ref[...] = sv[V_MISC:V_MISC + 1, 896:1024]

        def total(lo, hi, ref=sg_ref):
            s = ref[0, lo:hi, :].astype(F32)
            for d in range(1, N_DEV):
                s = s + ref[d, lo:hi, :].astype(F32)
            return s

        def emit(name, idx, g):
            res = _adam(w_refs[name][idx], g, m_refs[name][idx], v_refs[name][idx])
            for o, val in zip(o_refs, (g,) + res):
                o[name][idx] = val

        everything = (slice(None), slice(None))
        row = lambda r: sv[r:r + 1, :]
        emit("b_ada", everything, jnp.concatenate(
            [row(V_DSHIFT) + row(V_DCSHIFT), row(V_DSCALE) + row(V_DCSCALE), row(V_DGATE)], axis=1))
        emit("norm_g", everything, row(V_DNG) + row(V_DNG_CTX))
        misc = row(V_MISC)
        emit("sgu_norm_g", everything, misc[:, 0:512])
        emit("q_norm_g", everything, misc[:, 512:512 + HEAD_DIM])
        emit("k_norm_g", everything, misc[:, 640:640 + HEAD_DIM] + misc[:, 768:768 + HEAD_DIM])
        for g in range(SGU_GROUPS):
            emit("w_spatial", (0, g), total(128 * g, 128 * (g + 1)))
            emit("b_spatial", (0, slice(g, g + 1), slice(None)), total(M_DBS + 8 * g, M_DBS + 8 * (g + 1))[0:1, :])
        for hd in range(N_HEADS):
            by_dc = total(32 * hd, 32 * (hd + 1), rg_ref)
            emit("rpb", (0, hd), by_dc.T[0:2 * WIN_R - 1, 0:2 * WIN_C - 1])

    shapes = [jax.ShapeDtypeStruct(w.shape, F32) for w in ws]
    res = pl.pallas_call(body, name="adamw_small", out_shape=shapes * 4 + [jax.ShapeDtypeStruct((1, 128), F32)],
                         compiler_params=_params())(vg, sg, rg, *ws, *ms, *vs)
    return [res[i * k:(i + 1) * k] for i in range(4)], res[4 * k]


def _adamw_cctx(pc_g, w, m, v):
    def body(pc_ref, w_ref, m_ref, v_ref, g_ref, d_ref, m2_ref, v2_ref):
        pc = pc_ref[0, 0:1, :]
        for d in range(1, N_DEV):
            pc = pc + pc_ref[d, 0:1, :]
        cc = w_ref[...]
        sig = _sigmoid(cc)
        g = pc * (sig * (1.0 + cc * (1.0 - sig)))
        delta, m2, v2 = _adam(cc, g, m_ref[...], v_ref[...])
        g_ref[...] = g
        d_ref[...] = delta
        m2_ref[...] = m2
        v2_ref[...] = v2

    sh = jax.ShapeDtypeStruct((1, D_MODEL), F32)
    return pl.pallas_call(body, name="adamw_cctx", out_shape=[sh, sh, sh, sh], compiler_params=_params())(
        pc_g, w, m, v)


def _block_ones(n, blk):
    i = np.arange(n)
    return jnp.asarray((i[:, None] // blk == i[None, :] // blk).astype(np.float32), BF16)


def _rpb_pairs(rpb):
    n_off = 2 * WIN_C - 1
    cols = np.arange(GRID_W)
    c0 = np.clip(cols - WIN_C // 2, 0, GRID_W - WIN_C)
    in_win = (cols[None, :] >= c0[:, None]) & (cols[None, :] < c0[:, None] + WIN_C)
    dc = np.clip(cols[None, :] - cols[:, None] + (WIN_C - 1), 0, n_off - 1)
    expand = (dc[None] == np.arange(n_off)[:, None, None]) & in_win[None]
    toep = jnp.einsum("hrd,dqk->hrqk", rpb, jnp.asarray(expand, F32), precision=HI)
    toep = toep + jnp.asarray(np.where(in_win, 0.0, NEG_INF).astype(np.float32))
    neg = jnp.full((N_HEADS, 1, GRID_W, GRID_W), NEG_INF, F32)
    ext = jnp.concatenate([neg, toep, neg], axis=1)
    return jnp.concatenate([ext[:, :-1], ext[:, 1:]], axis=-1)


def _row_mask(rows):
    nrb = rows // Q_ROWS
    valid = np.zeros((3, Q_ROWS, 1, K_ROWS, 1), bool)
    for t, rb in enumerate((0, 1, nrb - 1)):
        kb = int(np.clip(Q_ROWS * rb - 4, 0, rows - K_ROWS))
        for i in range(Q_ROWS):
            r0 = int(np.clip(Q_ROWS * rb + i - WIN_R // 2, 0, rows - WIN_R))
            for j in range(K_ROWS):
                valid[t, i, 0, j, 0] = r0 <= kb + j < r0 + WIN_R
    full = np.broadcast_to(valid, (3, Q_ROWS, GRID_W, K_ROWS, GRID_W)).reshape(3, TQ, TK)
    return jnp.asarray(np.where(full, 0.0, NEG_INF).astype(np.float32))


def kernel(x, c, ctx, c_ctx, w_ada, b_ada, norm_g, w_in, sgu_norm_g, w_spatial, b_spatial, q_norm_g, k_norm_g, rpb, w_out, loss_target, m_c_ctx, m_w_ada, m_b_ada, m_norm_g, m_w_in, m_sgu_norm_g, m_w_spatial, m_b_spatial, m_q_norm_g, m_k_norm_g, m_rpb, m_w_out, v_c_ctx, v_w_ada, v_b_ada, v_norm_g, v_w_in, v_sgu_norm_g, v_w_spatial, v_b_spatial, v_q_norm_g, v_k_norm_g, v_rpb, v_w_out):
    me = 4 * lax.axis_index("x") + 2 * lax.axis_index("y") + lax.axis_index("c")
    x2, ctx2, tgt2 = x[0], ctx[0], loss_target[0]
    T, C = x2.shape[0], ctx2.shape[0]
    rows = T // GRID_W
    wada, win_t, wout = w_ada[0], w_in[0].T, w_out[0]
    ada_w = wada.shape[1]
    win_w = win_t.shape[0]

    row8 = lax.broadcasted_iota(jnp.int32, (8, D_MODEL), 0)
    c_blk = jnp.where(row8 == me, jnp.broadcast_to(c, (8, D_MODEL)), 0.0)
    b_sh = lax.dynamic_slice(b_ada, (0, me * ada_w), (1, ada_w))
    c_ctx_row = c_ctx.reshape(1, D_MODEL)

    ones512 = _block_ones(512, HEAD_DIM)
    ones8 = jnp.ones((8, 128), BF16)
    foldm = jnp.asarray((np.arange(512)[:, None] % HEAD_DIM == np.arange(128)[None, :]).astype(np.float32))
    lane_half = np.arange(128)[None, :, None] // GRID_W
    hsel = jnp.asarray((2 * np.arange(8)[:, None, None] + lane_half == np.arange(128)[None, None, :]).astype(np.float32),
                       BF16)
    foldr = jnp.asarray((np.arange(256)[None, :] // 8 == np.arange(32)[:, None]).astype(np.float32), BF16)
    gq512 = jnp.tile(q_norm_g, (1, N_HEADS))
    gk512 = jnp.tile(k_norm_g, (1, N_HEADS))
    ws = w_spatial[0]
    wst = ws.transpose(0, 2, 1)
    bsb = jnp.broadcast_to(b_spatial[0][:, :, None], (SGU_GROUPS, CHUNK, 128))
    pairs = _rpb_pairs(rpb[0])
    row_mask = _row_mask(rows)

    my_chip = me // 2
    order = jnp.stack([my_chip, my_chip ^ 2, my_chip ^ 1, my_chip ^ 3]).astype(jnp.int32)
    h, z, win_g, wout_g, s16, part_g = _inproj_fwd(order, x2, norm_g, c_blk, c_ctx_row, wada, b_sh,
                                                   win_t.astype(BF16), wout.astype(BF16), 512)
    w_in_b = win_g.reshape(D_IN, D_MODEL)
    w_out_b = wout_g.reshape(D_MODEL, D_MODEL)
    mod16 = part_g.transpose(1, 0, 2).reshape(16, 3 * D_MODEL)
    mod = lax.dynamic_slice(mod16, (me, 0), (1, 3 * D_MODEL))
    shift, scale, gate = mod[:, :D_MODEL], mod[:, D_MODEL:2 * D_MODEL], mod[:, 2 * D_MODEL:]
    cshift, cscale = mod16[8:9, :D_MODEL], mod16[8:9, D_MODEL:2 * D_MODEL]
    qs, kn = _qk_norm(z, ones512, gq512, gk512, 512)
    hc, zc, ckn = _ctx_fwd(ctx2, norm_g, cscale, cshift, w_in_b, ones512, gk512)
    out_a = _sgu_fwd(z, sgu_norm_g, ws, bsb, 512)
    ob, out_b, lse = _attn_fwd(qs, kn, z, ckn, zc, pairs, row_mask)

    dy, dmc, dw_out, dgate, loss_part = _outproj_loss_bwd(x2, tgt2, out_a, out_b, gate, w_out_b, 512)
    dw_out_blocks = dw_out.reshape(N_DEV, D_MODEL // N_DEV, D_MODEL)
    (dqs, dk, dv, dck, dcv, db_g, drpb), (gout_parts,) = _attn_bwd(
        qs, kn, z, ckn, zc, pairs, row_mask, ob, lse, dmc, hsel, foldr,
        [(_Hosted("a2a", dw_out_blocks), dw_out_blocks)])
    dzc_k, dzc_v, dgk_c = _ctx_k_bwd(zc, dck, dcv, ones512, gk512, foldm)
    rloc = drpb.reshape(N_HEADS * 32, 128)
    (dz, chip_sums, dws, dbs, dsg, dgq, dgk), (rg,) = _bwd_mid(
        z, dmc, dqs, dk, dv, db_g, h, hc, dzc_k, dzc_v, sgu_norm_g, ws, wst, bsb, ones8, ones512, gq512, gk512, foldm,
        256, [(_Hosted("ag", rloc), rloc)])
    sloc = jnp.concatenate([dws.reshape(SGU_GROUPS * CHUNK, CHUNK), dbs.reshape(SGU_GROUPS * 8, CHUNK)]).astype(BF16)
    (grad_x, dshift, dscale, dng), (gin_parts,) = _inproj_bwd_dx(
        [dz], 0, w_in_b, x2, dy, norm_g, scale, shift, 512, "inproj_bwd_dx",
        [(_Hosted("chips", chip_sums), chip_sums)])
    (dcshift, dcscale, dng_c), _ = _inproj_bwd_dx([dzc_k, dzc_v], 4 * 512, w_in_b, ctx2, None, norm_g, cscale,
                                                  cshift, C, "ctx_bwd_dx")

    res_in, _ = _adamw_sharded(win_t, gin_parts, m_w_in[0].T, v_w_in[0].T, 112, "adamw_w_in")
    res_out, _ = _adamw_sharded(wout, gout_parts, m_w_out[0], v_w_out[0], 128, "adamw_w_out")

    zero_row = jnp.zeros((1, D_MODEL), F32)
    vec_rows = [(V_DSHIFT, [dshift]), (V_DSCALE, [dscale]), (V_DGATE, [dgate]), (V_DCSHIFT, [dcshift]),
                (V_DCSCALE, [dcscale]), (V_ZERO, [zero_row]), (V_DNG, [dng]), (V_DNG_CTX, [dng_c])]
    vloc = _pack_vectors(vec_rows, dsg, dgq, dgk, dgk_c, loss_part)
    vg, sg = _allgather_direct([vloc, sloc], "gather_small")
    small_w =(b_ada, norm_g, sgu_norm_g, w_spatial, b_spatial, q_norm_g, k_norm_g, rpb)
    small_m = (m_b_ada, m_norm_g, m_sgu_norm_g, m_w_spatial, m_b_spatial, m_q_norm_g, m_k_norm_g, m_rpb)
    small_v = (v_b_ada, v_norm_g, v_sgu_norm_g, v_w_spatial, v_b_spatial, v_q_norm_g, v_k_norm_g, v_rpb)
    res_small, loss_row = _adamw_small(vg, sg, rg, small_w, small_m, small_v)

    dm_all = vg[:, V_DSHIFT:V_DGATE + 1, :].reshape(N_DEV, 3 * D_MODEL)
    dc_all = vg[:, V_DCSHIFT:V_ZERO + 1, :].reshape(N_DEV, 3 * D_MODEL)
    dm_sh = lax.dynamic_slice(dm_all, (0, me * ada_w), (N_DEV, ada_w))
    dc_sh = lax.dynamic_slice(dc_all, (0, me * ada_w), (N_DEV, ada_w))
    *res_ada, pc = _ada_bwd(s16, dm_sh, dc_sh, wada, m_w_ada[0], v_w_ada[0])
    (pc_g,) = _allgather_direct([pc], "gather_cctx")
    res_cctx = _adamw_cctx(pc_g, c_ctx_row, m_c_ctx.reshape(1, D_MODEL), v_c_ctx.reshape(1, D_MODEL))

    loss = loss_row[0, 0]
    outs = [loss, grad_x[None]]
    for kind in range(4):
        by_name = dict(zip(SMALL_NAMES, res_small[kind]))
        by_name.update(c_ctx=res_cctx[kind].reshape(D_MODEL), w_ada=res_ada[kind][None],
                       w_in=res_in[kind].T[None], w_out=res_out[kind][None])
        outs += [by_name[nme] for nme in ("c_ctx", "w_ada", "b_ada", "norm_g", "w_in", "sgu_norm_g", "w_spatial",
                                          "b_spatial", "q_norm_g", "k_norm_g", "rpb", "w_out")]
    return tuple(outs)
```

```python
import functools

import numpy as np
import jax
import jax.numpy as jnp
from jax import lax
from jax.experimental import pallas as pl
from jax.experimental.pallas import tpu as pltpu

F32 = jnp.float32
BF16 = jnp.bfloat16
HI = lax.Precision.HIGHEST

N_DEV = 8
D_MODEL = 1024
D_A = 512
D_B = 512
D_IN = 3584
N_BRANCH = 7
HEAD_DIM = 64
N_HEADS = 8
GRID_W = 64
WIN_R = 8
WIN_C = 16
CHUNK = 128
SGU_GROUPS = 4
EPS = 1e-6
NEG_INF = -1e30
Q_ROWS = 4
K_ROWS = 12
TQ = Q_ROWS * GRID_W
TK = K_ROWS * GRID_W
N_DIAG = 22
ATT_SUB = 4
ATT_SCALE = HEAD_DIM ** -0.5
LOG2E = 1.4426950408889634
LN2 = 0.6931471805599453

ADAM_LR = 0.001
ADAM_B1 = 0.9
ADAM_B2 = 0.999
ADAM_EPS = 1e-08
ADAM_WD = 0.01
ADAM_STEP = 10

VMEM_LIMIT = 56 * 1024 * 1024
MESH = pl.DeviceIdType.MESH

V_DSHIFT, V_DSCALE, V_DGATE, V_DCSHIFT, V_DCSCALE, V_ZERO, V_DNG, V_DNG_CTX, V_MISC = range(9)
M_DBS = 512


def _params(sem=None):
    return pltpu.CompilerParams(dimension_semantics=sem, vmem_limit_bytes=VMEM_LIMIT)


def _sigmoid(x):
    return 1.0 / (1.0 + jnp.exp(-x))


def _gelu_parts(x):
    cdf = 0.5 * (1.0 + lax.erf(x * 0.7071067811865476))
    pdf = jnp.exp(-0.5 * x * x) * 0.3989422804014327
    return x * cdf, cdf + x * pdf


def _nt(a, b):
    return lax.dot_general(a, b, (((1,), (1,)), ((), ())), preferred_element_type=F32)


def _tn(a, b):
    return lax.dot_general(a, b, (((0,), (0,)), ((), ())), preferred_element_type=F32)


def _dot2(v, ones_bf):
    hi = v.astype(BF16)
    lo = (v - hi.astype(F32)).astype(BF16)
    return (jnp.dot(hi, ones_bf, preferred_element_type=F32)
            + jnp.dot(lo, ones_bf, preferred_element_type=F32))


def _head_sum(v, ones_ref):
    return jnp.dot(v.astype(BF16), ones_ref[...], preferred_element_type=F32)


def _adam(w, g, m, v):
    m2 = ADAM_B1 * m + (1.0 - ADAM_B1) * g
    v2 = ADAM_B2 * v + (1.0 - ADAM_B2) * (g * g)
    m_hat = m2 / (1.0 - ADAM_B1 ** ADAM_STEP)
    v_hat = v2 / (1.0 - ADAM_B2 ** ADAM_STEP)
    delta = -ADAM_LR * (m_hat / (jnp.sqrt(v_hat) + ADAM_EPS) + ADAM_WD * w)
    return delta, m2, v2


class _Hosted:
    def __init__(self, kind, src):
        self.kind = kind
        n_slot = {"a2a": N_DEV, "ag": N_DEV, "chips": N_DEV // 2}[kind]
        blk = src.shape if kind == "ag" else src.shape[1:]
        self.out_shape = jax.ShapeDtypeStruct((n_slot,) + tuple(blk), src.dtype)
        self.n_peer = n_slot - 1
        self.scratch = [pltpu.SemaphoreType.DMA((self.n_peer,)), pltpu.SemaphoreType.DMA((self.n_peer,)),
                        pltpu.SemaphoreType.DMA]

    def _copies(self, src, dst, send_sems, recv_sems, loc_sem, landing):
        x, y, c = lax.axis_index("x"), lax.axis_index("y"), lax.axis_index("c")
        if self.kind == "chips":
            me = 2 * x + y
            peers = [((px, py, c), 2 * px + py) for px, py in ((1 - x, y), (x, 1 - y), (1 - x, 1 - y))]
        else:
            me = 4 * x + 2 * y + c
            peers = []
            for k in range(1, N_DEV):
                px = 1 - x if (k >> 2) & 1 else x
                py = 1 - y if (k >> 1) & 1 else y
                pc = 1 - c if k & 1 else c
                peers.append(((px, py, pc), 4 * px + 2 * py + pc))
        remote = []
        for k, (peer, pid) in enumerate(peers):
            s = src if self.kind == "ag" else src.at[pid]
            remote.append(pltpu.make_async_remote_copy(
                src_ref=s, dst_ref=dst.at[pid if landing else me],
                send_sem=send_sems.at[k], recv_sem=recv_sems.at[k], device_id=peer, device_id_type=MESH))
        local = pltpu.make_async_copy(src if self.kind == "ag" else src.at[me], dst.at[me], loc_sem)
        return remote, local

    def start(self, src, dst, send_sems, recv_sems, loc_sem):
        remote, local = self._copies(src, dst, send_sems, recv_sems, loc_sem, landing=False)
        for cp in remote:
            cp.start()
        local.start()

    def wait(self, src, dst, send_sems, recv_sems, loc_sem):
        remote, local = self._copies(src, dst, send_sems, recv_sems, loc_sem, landing=True)
        for cp in remote:
            cp.wait_recv()
        for cp in remote:
            cp.wait_send()
        local.wait()


ANY_SPEC = pl.BlockSpec(memory_space=pl.ANY)


def _allgather_direct(arrs, name):
    n = len(arrs)
    exs = [_Hosted("ag", a) for a in arrs]

    def body(*refs):
        srcs, dsts, sems = refs[:n], refs[n:2 * n], refs[2 * n:]
        for r, ex in enumerate(exs):
            ex.start(srcs[r], dsts[r], *sems[3 * r:3 * r + 3])
        for r, ex in enumerate(exs):
            ex.wait(srcs[r], dsts[r], *sems[3 * r:3 * r + 3])

    return pl.pallas_call(body, name=name, out_shape=[ex.out_shape for ex in exs], in_specs=[ANY_SPEC] * n,
                          out_specs=[ANY_SPEC] * n, scratch_shapes=[s for ex in exs for s in ex.scratch])(*arrs)


def _ada_scratch(n_col):
    return ([pltpu.VMEM((N_DEV, 8, D_MODEL), F32), pltpu.VMEM((16, n_col), F32), pltpu.VMEM((N_DEV, 16, n_col), F32)]
            + [pltpu.SemaphoreType.DMA((N_DEV - 1,)) for _ in range(4)])


def _ada_modulation(cb_ref, cc_ref, w_ref, b_ref, cstack, part, parts, s1, r1, s2, r2):
    x, y, c = lax.axis_index("x"), lax.axis_index("y"), lax.axis_index("c")
    me = 4 * x + 2 * y + c
    peers = []
    for k in range(1, N_DEV):
        px = 1 - x if (k >> 2) & 1 else x
        py = 1 - y if (k >> 1) & 1 else y
        pc = 1 - c if k & 1 else c
        peers.append(((px, py, pc), 4 * px + 2 * py + pc))

    def exchange(src, dst, send_sems, recv_sems):
        for k, (peer, _) in enumerate(peers):
            pltpu.make_async_remote_copy(src_ref=src, dst_ref=dst.at[me], send_sem=send_sems.at[k],
                                         recv_sem=recv_sems.at[k], device_id=peer, device_id_type=MESH).start()
        dst[me] = src[...]
        waits = [pltpu.make_async_remote_copy(src_ref=src, dst_ref=dst.at[pid], send_sem=send_sems.at[k],
                                              recv_sem=recv_sems.at[k], device_id=peer, device_id_type=MESH)
                 for k, (peer, pid) in enumerate(peers)]
        for cp in waits:
            cp.wait_recv()
        for cp in waits:
            cp.wait_send()

    exchange(cb_ref, cstack, s1, r1)
    c_all = cstack[0]
    for d in range(1, N_DEV):
        c_all = c_all + cstack[d]
    row = lax.broadcasted_iota(jnp.int32, (8, D_MODEL), 0)
    cc = jnp.where(row == 0, jnp.broadcast_to(cc_ref[...], (8, D_MODEL)), 0.0)
    call = jnp.concatenate([c_all, cc], axis=0)
    s = call * _sigmoid(call)
    part[...] = jnp.dot(s, w_ref[...], preferred_element_type=F32, precision=HI) + b_ref[...]
    exchange(part, parts, s2, r2)
    return s


def _ada_bwd(s16, dm, dc, w, m, v):
    def body(s_ref, dm_ref, dc_ref, w_ref, m_ref, v_ref, g_ref, d_ref, m2_ref, v2_ref, pc_ref):
        dct = jnp.sum(dc_ref[...], axis=0, keepdims=True)
        row = lax.broadcasted_iota(jnp.int32, dc_ref.shape, 0)
        dcb = jnp.where(row == 0, jnp.broadcast_to(dct, dc_ref.shape), 0.0)
        dm16 = jnp.concatenate([dm_ref[...], dcb], axis=0)
        g = lax.dot_general(s_ref[...], dm16, (((0,), (0,)), ((), ())),
                            preferred_element_type=F32, precision=HI)
        w_ = w_ref[...]
        delta, m2, v2 = _adam(w_, g, m_ref[...], v_ref[...])
        g_ref[...] = g
        d_ref[...] = delta
        m2_ref[...] = m2
        v2_ref[...] = v2
        pc_ref[...] = lax.dot_general(dcb, w_, (((1,), (1,)), ((), ())),
                                      preferred_element_type=F32, precision=HI)

    sh = jax.ShapeDtypeStruct(w.shape, F32)
    return pl.pallas_call(
        body, name="ada_bwd",
        out_shape=[sh, sh, sh, sh, jax.ShapeDtypeStruct((8, D_MODEL), F32)],
        compiler_params=_params(),
    )(s16, dm, dc, w, m, v)


def _head_norm(zk, ones_ref, gain):
    ss = _head_sum(zk * zk, ones_ref)
    return zk * lax.rsqrt(ss * (1.0 / HEAD_DIM) + EPS) * gain


def _inproj_fwd(order, x, ng, c_blk, c_ctx_row, w_ada_sh, b_ada_sh, w_blk_t, tm):
    T = x.shape[0]
    nt = T // tm
    n_pass = N_DEV // 2
    blk_rows = w_blk_t.shape[0]
    n_col = w_ada_sh.shape[1]

    def body(order_ref, x_ref, g_ref, cb_ref, cc_ref, wa_ref, ba_ref, wb_ref,
             h_out, z_ref, wt_out, s_out, parts_out,
             hs, wt, modv, send_sems, recv_sems, loc_sem, h_sem, wt_sem, *ada_sc):
        p, i = pl.program_id(0), pl.program_id(1)
        x, y, c = lax.axis_index("x"), lax.axis_index("y"), lax.axis_index("c")
        me, sib = (x, y, c), (x, y, 1 - c)
        chips = [(1 - x, y), (x, 1 - y), (1 - x, 1 - y)]

        def slot(px, py, pc):
            return 4 * px + 2 * py + pc

        def copy(k, block, to, src=None):
            return pltpu.make_async_remote_copy(
                src_ref=wt.at[slot(*block)] if src is None else src, dst_ref=wt.at[slot(*block)],
                send_sem=send_sems.at[k], recv_sem=recv_sems.at[k], device_id=to, device_id_type=MESH)

        own = pltpu.make_async_copy(wb_ref, wt.at[slot(*me)], loc_sem)
        h_copy = pltpu.make_async_copy(hs, h_out, h_sem)
        wt_copy = pltpu.make_async_copy(wt, wt_out, wt_sem)
        first = [copy(1 + j, me, (*chip, c), src=wb_ref) for j, chip in enumerate(chips[:2])] + [copy(0, me, sib, src=wb_ref)]
        passed = [copy(4 + j, (*chip, c), sib) for j, chip in enumerate(chips)]
        relay_src = (jnp.where(c == 0, 1 - x, x), jnp.where(c == 0, y, 1 - y), c)
        relay_dst = (jnp.where(c == 0, x, 1 - x), jnp.where(c == 0, 1 - y, y), c)
        relay = copy(3, relay_src, relay_dst)

        @pl.when(jnp.logical_and(p == 0, i == 0))
        def _():
            s_out[...] = _ada_modulation(cb_ref, cc_ref, wa_ref, ba_ref, *ada_sc)
            own.start()
            for cp in first:
                cp.start()
            parts = ada_sc[2]
            parts_out[...] = parts[...]
            my_row = pl.ds(slot(*me), 1)
            mod = jnp.concatenate([parts[d, my_row, :] for d in range(N_DEV)], axis=1)
            modv[0:1, :] = mod[:, 0:D_MODEL]
            modv[1:2, :] = mod[:, D_MODEL:2 * D_MODEL]
            own.wait()
            copy(0, sib, me).wait_recv()

        @pl.when(jnp.logical_and(p == 1, i == 0))
        def _():
            for j, chip in enumerate(chips[:2]):
                copy(1 + j, (*chip, c), me).wait_recv()
            relay.start()
            passed[0].start()
            passed[1].start()
            copy(4, (*chips[0], 1 - c), me).wait_recv()

        @pl.when(jnp.logical_and(p == 2, i == 0))
        def _():
            copy(5, (*chips[1], 1 - c), me).wait_recv()

        @pl.when(jnp.logical_and(p == 3, i == 0))
        def _():
            copy(3, (*chips[2], c), me).wait_recv()
            passed[2].start()
            copy(6, (*chips[2], 1 - c), me).wait_recv()
            wt_copy.start()

        rows = pl.ds(pl.multiple_of(i * tm, tm), tm)

        @pl.when(p == 0)
        def _():
            xv = x_ref[...]
            r = lax.rsqrt(jnp.mean(xv * xv, axis=-1, keepdims=True) + EPS)
            hs[rows, :] = ((xv * r * g_ref[...]) * (1.0 + modv[1:2, :]) + modv[0:1, :]).astype(BF16)

        @pl.when(jnp.logical_and(p == 1, i == 0))
        def _():
            h_copy.start()

        w_pair = wt[pl.ds(2 * order_ref[p], 2)].reshape(2 * blk_rows, D_MODEL)
        z_ref[...] = _nt(hs[rows, :], w_pair).astype(BF16)

        @pl.when(jnp.logical_and(p == n_pass - 1, i == nt - 1))
        def _():
            for cp in first + passed + [relay]:
                cp.wait_send()
            h_copy.wait()
            wt_copy.wait()

    whole = lambda shape: pl.BlockSpec(shape, lambda p, i, o: (0,) * len(shape))
    grid_spec = pltpu.PrefetchScalarGridSpec(
        num_scalar_prefetch=1, grid=(n_pass, nt),
        in_specs=[pl.BlockSpec((tm, D_MODEL), lambda p, i, o: (jnp.where(p == 0, i, nt - 1), 0)),
                  whole((1, D_MODEL)), whole((8, D_MODEL)), whole((1, D_MODEL)), whole((D_MODEL, n_col)),
                  whole((1, n_col)), ANY_SPEC],
        out_specs=[ANY_SPEC, pl.BlockSpec((tm, 2 * blk_rows), lambda p, i, o: (i, o[p])), ANY_SPEC,
                   whole((16, D_MODEL)), whole((N_DEV, 16, n_col))],
        scratch_shapes=[pltpu.VMEM((T, D_MODEL), BF16), pltpu.VMEM((N_DEV, blk_rows, D_MODEL), BF16),
                        pltpu.VMEM((8, D_MODEL), F32),
                        pltpu.SemaphoreType.DMA((7,)), pltpu.SemaphoreType.DMA((7,)), pltpu.SemaphoreType.DMA,
                        pltpu.SemaphoreType.DMA, pltpu.SemaphoreType.DMA] + _ada_scratch(n_col))
    return pl.pallas_call(
        body, name="inproj_fwd", grid_spec=grid_spec,
        out_shape=[jax.ShapeDtypeStruct((T, D_MODEL), BF16), jax.ShapeDtypeStruct((T, D_IN), BF16),
                   jax.ShapeDtypeStruct((N_DEV, blk_rows, D_MODEL), BF16),
                   jax.ShapeDtypeStruct((16, D_MODEL), F32), jax.ShapeDtypeStruct((N_DEV, 16, n_col), F32)],
        compiler_params=_params(("arbitrary", "arbitrary")),
    )(order, x, ng, c_blk, c_ctx_row, w_ada_sh, b_ada_sh, w_blk_t)


def _qk_norm(z, ones_blk, gq, gk, tm):
    T = z.shape[0]

    def body(q_ref, k_ref, ones_ref, gq_ref, gk_ref, qs_ref, kn_ref):
        qs = _head_norm(q_ref[...].astype(F32), ones_ref, gq_ref[...]) * (ATT_SCALE * LOG2E)
        qs_ref[...] = qs.astype(BF16)
        kn_ref[...] = _head_norm(k_ref[...].astype(F32), ones_ref, gk_ref[...]).astype(BF16)

    v512 = pl.BlockSpec((1, 512), lambda i: (0, 0))
    return pl.pallas_call(
        body, name="qk_norm", grid=(T // tm,),
        in_specs=[pl.BlockSpec((tm, 512), lambda i: (i, 3)), pl.BlockSpec((tm, 512), lambda i: (i, 4)),
                  pl.BlockSpec((512, 512), lambda i: (0, 0)), v512, v512],
        out_specs=[pl.BlockSpec((tm, 512), lambda i: (i, 0)), pl.BlockSpec((tm, 512), lambda i: (i, 0))],
        out_shape=[jax.ShapeDtypeStruct((T, 512), BF16), jax.ShapeDtypeStruct((T, 512), BF16)],
        compiler_params=_params(("arbitrary",)),
    )(z, z, ones_blk, gq, gk)


def _ctx_fwd(ctx, ng, cscale, cshift, w_in_t, ones_blk, gk):
    C = ctx.shape[0]

    def body(x_ref, g_ref, sc_ref, sh_ref, w_ref, ones_ref, gk_ref, h_ref, z_ref, kn_ref):
        xv = x_ref[...]
        r = lax.rsqrt(jnp.mean(xv * xv, axis=-1, keepdims=True) + EPS)
        h = (xv * r * g_ref[...]) * (1.0 + sc_ref[...]) + sh_ref[...]
        hb = h.astype(BF16)
        h_ref[...] = hb
        zk = _nt(hb, w_ref[0:512, :])
        zv = _nt(hb, w_ref[512:1024, :])
        z_ref[:, 0:512] = zk.astype(BF16)
        z_ref[:, 512:1024] = zv.astype(BF16)
        kn_ref[...] = _head_norm(zk, ones_ref, gk_ref[...]).astype(BF16)

    vec = pl.BlockSpec((1, D_MODEL), lambda i: (0, 0))
    return pl.pallas_call(
        body, name="ctx_fwd", grid=(1,),
        in_specs=[pl.BlockSpec((C, D_MODEL), lambda i: (0, 0)), vec, vec, vec,
                  pl.BlockSpec((1024, D_MODEL), lambda i: (2, 0)),
                  pl.BlockSpec((512, 512), lambda i: (0, 0)), pl.BlockSpec((1, 512), lambda i: (0, 0))],
        out_specs=[pl.BlockSpec((C, D_MODEL), lambda i: (0, 0)), pl.BlockSpec((C, 1024), lambda i: (0, 0)),
                   pl.BlockSpec((C, 512), lambda i: (0, 0))],
        out_shape=[jax.ShapeDtypeStruct((C, D_MODEL), BF16), jax.ShapeDtypeStruct((C, 1024), BF16),
                   jax.ShapeDtypeStruct((C, 512), BF16)],
        compiler_params=_params(("arbitrary",)),
    )(ctx, ng, cscale, cshift, w_in_t, ones_blk, gk)


def _sgu_chunk_fwd(au, av, ag, sg, ws_bf, bsb):
    gu, dgu = _gelu_parts(au)
    gv, dgv = _gelu_parts(av)
    rr = lax.rsqrt(jnp.mean(gv * gv, axis=-1, keepdims=True) + EPS)
    vhat = gv * rr
    vn = vhat * sg
    mixed = jnp.dot(ws_bf, vn.astype(BF16), preferred_element_type=F32) + bsb
    sig = _sigmoid(ag)
    sl = ag * sig
    return gu * mixed * sl, (gu, dgu, dgv, rr, vhat, vn, mixed, sig, sl)


def _sgu_fwd(z, sgn, ws, bsb, tm):
    T = z.shape[0]

    def body(au_ref, av_ref, ag_ref, sg_ref, ws_ref, bsb_ref, o_ref):
        for g in range(SGU_GROUPS):
            ws_bf = ws_ref[g].astype(BF16)
            sg = sg_ref[:, 128 * g:128 * (g + 1)]
            bsb_g = bsb_ref[g]
            for j in range(tm // CHUNK):
                rs, cs = slice(CHUNK * j, CHUNK * (j + 1)), slice(128 * g, 128 * (g + 1))
                out, _ = _sgu_chunk_fwd(au_ref[rs, cs].astype(F32), av_ref[rs, cs].astype(F32),
                                        ag_ref[rs, cs].astype(F32), sg, ws_bf, bsb_g)
                o_ref[rs, cs] = out.astype(BF16)

    return pl.pallas_call(
        body, name="sgu_fwd", grid=(T // tm,),
        in_specs=[pl.BlockSpec((tm, 512), lambda i: (i, 0)), pl.BlockSpec((tm, 512), lambda i: (i, 1)),
                  pl.BlockSpec((tm, 512), lambda i: (i, 2)), pl.BlockSpec((1, 512), lambda i: (0, 0)),
                  pl.BlockSpec((SGU_GROUPS, CHUNK, CHUNK), lambda i: (0, 0, 0)),
                  pl.BlockSpec((SGU_GROUPS, CHUNK, 128), lambda i: (0, 0, 0))],
        out_specs=pl.BlockSpec((tm, 512), lambda i: (i, 0)),
        out_shape=jax.ShapeDtypeStruct((T, 512), BF16),
        compiler_params=_params(("arbitrary",)),
    )(z, z, z, sgn, ws, bsb)


def _attn_type(rb, nrb):
    return jnp.where(rb == 0, 0, jnp.where(rb == nrb - 1, 2, 1))


def _attn_specs(T, C):
    return [
        pl.BlockSpec((ATT_SUB * TQ, 128), lambda hp, st: (st, hp)),
        pl.BlockSpec((T, 128), lambda hp, st: (0, hp)),
        pl.BlockSpec((T, 128), lambda hp, st: (0, 20 + hp)),
        pl.BlockSpec((C, 128), lambda hp, st: (0, hp)),
        pl.BlockSpec((C, 128), lambda hp, st: (0, 4 + hp)),
        pl.BlockSpec((2, 2 * WIN_R, GRID_W, 128), lambda hp, st: (hp, 0, 0, 0)),
        pl.BlockSpec((3, TQ, TK), lambda hp, st: (0, 0, 0)),
        pl.BlockSpec((ATT_SUB * TQ, 128), lambda hp, st: (st, 24 + hp)),
    ]


def _build_bias(pairs_ref, mask_ref, bias_sc):
    for t in range(3):
        for hh in range(2):
            for i in range(Q_ROWS):
                for mm in range(K_ROWS // 2):
                    p = min(max(WIN_R - Q_ROWS * t + 2 * mm - i, 0), 2 * WIN_R - 1)
                    rs, cs = slice(GRID_W * i, GRID_W * (i + 1)), slice(128 * mm, 128 * (mm + 1))
                    bias_sc[t, hh, rs, cs] = (pairs_ref[hh, p] + mask_ref[t, rs, cs]) * LOG2E


def _attn_fwd(qs, kn, z, ckn, zc, pairs, row_mask, riders=()):
    T, C = qs.shape[0], ckn.shape[0]
    rows = T // GRID_W
    nrb = rows // Q_ROWS
    n_st = nrb // ATT_SUB
    n_rid = len(riders)

    def body(q_ref, k_ref, v_ref, ck_ref, cv_ref, pairs_ref, mask_ref, bg_ref, *rest):
        rid_src = rest[:n_rid]
        ob_ref, outb_ref, lse_ref = rest[n_rid:n_rid + 3]
        rid_dst = rest[n_rid + 3:2 * n_rid + 3]
        bias_sc = rest[2 * n_rid + 3]
        rid_sems = rest[2 * n_rid + 4:]

        @pl.when(jnp.logical_and(pl.program_id(0) == 0, pl.program_id(1) == 0))
        def _():
            for r, (ex, _) in enumerate(riders):
                ex.start(rid_src[r], rid_dst[r], *rid_sems[3 * r:3 * r + 3])

        @pl.when(pl.program_id(1) == 0)
        def _():
            _build_bias(pairs_ref, mask_ref, bias_sc)

        ck2, cv2 = ck_ref[...], cv_ref[...]
        lane = lax.broadcasted_iota(jnp.int32, (1, 128), 1)
        for sub in range(ATT_SUB):
            rb = ATT_SUB * pl.program_id(1) + sub
            bias_ref = bias_sc.at[_attn_type(rb, nrb)]
            rs = slice(TQ * sub, TQ * (sub + 1))
            ks = pl.multiple_of(jnp.clip(Q_ROWS * rb - 4, 0, rows - K_ROWS) * GRID_W, GRID_W)
            q2 = q_ref[rs, :]
            k2 = k_ref[pl.ds(ks, TK), :]
            v2 = v_ref[pl.ds(ks, TK), :]
            o_acc = jnp.zeros((TQ, 128), F32)
            lse_acc = jnp.zeros((TQ, 128), F32)
            for hh in range(2):
                msk = (lane >= HEAD_DIM) == bool(hh)
                qm = jnp.where(msk, q2, jnp.zeros_like(q2))
                s = _nt(qm, k2) + bias_ref[hh]
                sc = _nt(qm, ck2)
                m = jnp.maximum(jnp.max(s, axis=-1, keepdims=True), jnp.max(sc, axis=-1, keepdims=True))
                p = jnp.exp2(s - m)
                pc = jnp.exp2(sc - m)
                va = jnp.where(msk, v2, jnp.ones_like(v2))
                cva = jnp.where(msk, cv2, jnp.ones_like(cv2))
                num = (jnp.dot(p.astype(BF16), va, preferred_element_type=F32)
                       + jnp.dot(pc.astype(BF16), cva, preferred_element_type=F32))
                den = pltpu.roll(num, HEAD_DIM, 1)
                o_acc = jnp.where(msk, num / den, o_acc)
                lse_acc = jnp.where(msk, m + jnp.log(den) * LOG2E, lse_acc)
            ob_ref[rs, :] = o_acc.astype(BF16)
            lse_ref[rs, :] = lse_acc
            bg = bg_ref[rs, :].astype(F32)
            outb_ref[rs, :] = (o_acc * (bg * _sigmoid(bg))).astype(BF16)

        @pl.when(jnp.logical_and(pl.program_id(0) == pl.num_programs(0) - 1, pl.program_id(1) == n_st - 1))
        def _():
            for r, (ex, _) in enumerate(riders):
                ex.wait(rid_src[r], rid_dst[r], *rid_sems[3 * r:3 * r + 3])

    tile = pl.BlockSpec((ATT_SUB * TQ, 128), lambda hp, st: (st, hp))
    res = pl.pallas_call(
        body, name="attn_fwd", grid=(4, n_st),
        in_specs=_attn_specs(T, C) + [ANY_SPEC] * n_rid,
        out_specs=[tile, tile, tile] + [ANY_SPEC] * n_rid,
        out_shape=[jax.ShapeDtypeStruct((T, 512), BF16), jax.ShapeDtypeStruct((T, 512), BF16),
                   jax.ShapeDtypeStruct((T, 512), F32)] + [ex.out_shape for ex, _ in riders],
        scratch_shapes=[pltpu.VMEM((3, 2, TQ, TK), F32)] + [s for ex, _ in riders for s in ex.scratch],
        compiler_params=_params(("arbitrary", "arbitrary")),
    )(qs, kn, z, ckn, zc, pairs, row_mask, z, *[a for _, a in riders])
    return res[:3], res[3:]


def _outproj_loss_bwd(x, tgt, out_a, out_b, gate, w_out, tm):
    T = x.shape[0]
    nt = T // tm

    def body(x_ref, t_ref, oa_ref, ob_ref, gate_ref, w_ref, dy_ref, dmc_ref, dw_ref, dgate_ref, loss_ref, acc):
        @pl.when(pl.program_id(0) == 0)
        def _():
            acc[...] = jnp.zeros_like(acc)
            dgate_ref[...] = jnp.zeros_like(dgate_ref)
            loss_ref[...] = jnp.zeros_like(loss_ref)

        oa, ob = oa_ref[...], ob_ref[...]
        gate_v = gate_ref[...]
        mix = (jnp.dot(oa, w_ref[0:512, :], preferred_element_type=F32)
               + jnp.dot(ob, w_ref[512:1024, :], preferred_element_type=F32))
        e = x_ref[...] + gate_v * mix - t_ref[...]
        se = jnp.sum(jnp.sum(e * e, axis=0, keepdims=True), axis=1, keepdims=True)
        loss_ref[...] += jnp.broadcast_to(se * (0.5 / D_MODEL), loss_ref.shape)
        dy = e * (1.0 / D_MODEL)
        dy_ref[...] = dy
        dgate_ref[...] += jnp.sum(dy * mix, axis=0, keepdims=True)
        dmix = (dy * gate_v).astype(BF16)
        dmc_ref[...] = _nt(dmix, w_ref[...]).astype(BF16)
        acc[0:512, :] += _tn(oa, dmix)
        acc[512:1024, :] += _tn(ob, dmix)

        @pl.when(pl.program_id(0) == nt - 1)
        def _():
            dw_ref[...] = acc[...].astype(BF16)

    row = lambda w: pl.BlockSpec((tm, w), lambda i: (i, 0))
    return pl.pallas_call(
        body, name="outproj_loss_bwd", grid=(nt,),
        in_specs=[row(D_MODEL), row(D_MODEL), row(512), row(512),
                  pl.BlockSpec((1, D_MODEL), lambda i: (0, 0)),
                  pl.BlockSpec((D_MODEL, D_MODEL), lambda i: (0, 0))],
        out_specs=[row(D_MODEL), row(D_MODEL), pl.BlockSpec((D_MODEL, D_MODEL), lambda i: (0, 0)),
                   pl.BlockSpec((1, D_MODEL), lambda i: (0, 0)), pl.BlockSpec((1, 128), lambda i: (0, 0))],
        out_shape=[jax.ShapeDtypeStruct((T, D_MODEL), F32), jax.ShapeDtypeStruct((T, D_MODEL), BF16),
                   jax.ShapeDtypeStruct((D_MODEL, D_MODEL), BF16), jax.ShapeDtypeStruct((1, D_MODEL), F32),
                   jax.ShapeDtypeStruct((1, 128), F32)],
        scratch_shapes=[pltpu.VMEM((D_MODEL, D_MODEL), F32)],
        compiler_params=_params(("arbitrary",)),
    )(x, tgt, out_a, out_b, gate, w_out)


def _attn_bwd(qs, kn, z, ckn, zc, pairs, row_mask, ob, lse, dmc, hsel, fold, riders):
    T, C = qs.shape[0], ckn.shape[0]
    rows = T // GRID_W
    nrb = rows // Q_ROWS
    n_st = nrb // ATT_SUB
    n_rid = len(riders)

    def body(q_ref, k_ref, v_ref, ck_ref, cv_ref, pairs_ref, mask_ref, bg_ref, ob_ref, lse_ref, do_ref,
             hsel_ref, fold_ref, *rest):
        rid_src = rest[:n_rid]
        dq_ref, dk_ref, dv_ref, dck_ref, dcv_ref, dbg_ref, drpb_ref = rest[n_rid:n_rid + 7]
        rid_dst = rest[n_rid + 7:2 * n_rid + 7]
        bias_sc, dacc_ref = rest[2 * n_rid + 7:2 * n_rid + 9]
        rid_sems = rest[2 * n_rid + 9:]
        hp, st = pl.program_id(0), pl.program_id(1)

        @pl.when(jnp.logical_and(hp == 0, st == 0))
        def _():
            for r, (ex, _) in enumerate(riders):
                ex.start(rid_src[r], rid_dst[r], *rid_sems[3 * r:3 * r + 3])

        @pl.when(st == 0)
        def _():
            _build_bias(pairs_ref, mask_ref, bias_sc)
            dk_ref[...] = jnp.zeros_like(dk_ref)
            dv_ref[...] = jnp.zeros_like(dv_ref)
            dck_ref[...] = jnp.zeros_like(dck_ref)
            dcv_ref[...] = jnp.zeros_like(dcv_ref)
            dacc_ref[...] = jnp.zeros_like(dacc_ref)

        ck2, cv2 = ck_ref[...], cv_ref[...]
        lane = lax.broadcasted_iota(jnp.int32, (1, 128), 1)
        for sub in range(ATT_SUB):
            rb = ATT_SUB * st + sub
            bias_ref = bias_sc.at[_attn_type(rb, nrb)]
            rs = slice(TQ * sub, TQ * (sub + 1))
            kb = jnp.clip(Q_ROWS * rb - 4, 0, rows - K_ROWS)
            ks = pl.multiple_of(kb * GRID_W, GRID_W)
            ebase = kb - Q_ROWS * rb + 11
            q2 = q_ref[rs, :]
            k2 = k_ref[pl.ds(ks, TK), :]
            v2 = v_ref[pl.ds(ks, TK), :]
            bg = bg_ref[rs, :].astype(F32)
            sig = _sigmoid(bg)
            obv = ob_ref[rs, :].astype(F32)
            dout = do_ref[rs, :].astype(F32)
            dbg_ref[rs, :] = (dout * obv * (sig * (1.0 + bg * (1.0 - sig)))).astype(BF16)
            d_o = dout * (bg * sig)
            d_oo = d_o * obv
            lse2 = lse_ref[rs, :]
            dq_acc = jnp.zeros((TQ, 128), F32)
            for hh in range(2):
                msk = (lane >= HEAD_DIM) == bool(hh)
                qm = jnp.where(msk, q2, jnp.zeros_like(q2))
                lse_h = jnp.max(jnp.where(msk, lse2, -jnp.inf), axis=-1, keepdims=True)
                p = jnp.exp2(_nt(qm, k2) + bias_ref[hh] - lse_h)
                pc = jnp.exp2(_nt(qm, ck2) - lse_h)
                dom_f = jnp.where(msk, d_o, 0.0)
                dom = dom_f.astype(BF16)
                delta = jnp.sum(jnp.where(msk, d_oo, 0.0), axis=-1, keepdims=True)
                d_hi = delta.astype(BF16).astype(F32)
                x0 = HEAD_DIM * (1 - hh)
                dom_aug = jnp.where(lane == x0, -d_hi, jnp.where(lane == x0 + 1, d_hi - delta, dom_f)).astype(BF16)
                extra = jnp.logical_or(lane == x0, lane == x0 + 1)
                va = jnp.where(msk, v2, jnp.where(extra, jnp.ones_like(v2), jnp.zeros_like(v2)))
                cva = jnp.where(msk, cv2, jnp.where(extra, jnp.ones_like(cv2), jnp.zeros_like(cv2)))
                ds = p * _nt(dom_aug, va)
                dsc = pc * _nt(dom_aug, cva)
                dsb, dscb = ds.astype(BF16), dsc.astype(BF16)
                dq_h = (jnp.dot(dsb, k2, preferred_element_type=F32)
                        + jnp.dot(dscb, ck2, preferred_element_type=F32))
                dq_acc = jnp.where(msk, dq_h, dq_acc)
                dk_ref[pl.ds(ks, TK), :] += _tn(dsb, qm)
                dv_ref[pl.ds(ks, TK), :] += _tn(p.astype(BF16), dom)
                dck_ref[...] += _tn(dscb, qm)
                dcv_ref[...] += _tn(pc.astype(BF16), dom)
                for i in range(Q_ROWS):
                    for mm in range(K_ROWS // 2):
                        dacc_ref[hh, ebase + (2 * mm - i)] += ds[GRID_W * i:GRID_W * (i + 1),
                                                                 128 * mm:128 * (mm + 1)]
            dq_ref[rs, :] = dq_acc

        @pl.when(st == n_st - 1)
        def _():
            for hh in range(2):
                drpb_ref[hh] = _rpb_diag_sums(dacc_ref.at[hh], hsel_ref, fold_ref)

        @pl.when(jnp.logical_and(hp == pl.num_programs(0) - 1, st == n_st - 1))
        def _():
            for r, (ex, _) in enumerate(riders):
                ex.wait(rid_src[r], rid_dst[r], *rid_sems[3 * r:3 * r + 3])

    tile = pl.BlockSpec((ATT_SUB * TQ, 128), lambda hp, st: (st, hp))
    colT = pl.BlockSpec((T, 128), lambda hp, st: (0, hp))
    colC = pl.BlockSpec((C, 128), lambda hp, st: (0, hp))
    res = pl.pallas_call(
        body, name="attn_bwd", grid=(4, n_st),
        in_specs=(_attn_specs(T, C) + [tile, tile, pl.BlockSpec((ATT_SUB * TQ, 128), lambda hp, st: (st, 4 + hp)),
                                       pl.BlockSpec((8, 128, 128), lambda hp, st: (0, 0, 0)),
                                       pl.BlockSpec((32, 256), lambda hp, st: (0, 0))]
                  + [ANY_SPEC] * n_rid),
        out_specs=([tile, colT, colT, colC, colC, tile, pl.BlockSpec((2, 32, 128), lambda hp, st: (hp, 0, 0))]
                   + [ANY_SPEC] * n_rid),
        out_shape=([jax.ShapeDtypeStruct((T, 512), F32), jax.ShapeDtypeStruct((T, 512), F32),
                    jax.ShapeDtypeStruct((T, 512), F32), jax.ShapeDtypeStruct((C, 512), F32),
                    jax.ShapeDtypeStruct((C, 512), F32), jax.ShapeDtypeStruct((T, 512), BF16),
                    jax.ShapeDtypeStruct((N_HEADS, 32, 128), F32)] + [ex.out_shape for ex, _ in riders]),
        scratch_shapes=([pltpu.VMEM((3, 2, TQ, TK), F32), pltpu.VMEM((2, N_DIAG, GRID_W, 128), F32)]
                        + [s for ex, _ in riders for s in ex.scratch]),
        compiler_params=_params(("arbitrary", "arbitrary")),
    )(qs, kn, z, ckn, zc, pairs, row_mask, z, ob, lse, dmc, hsel, fold, *[a for _, a in riders])
    return res[:7], res[7:]


def _rpb_diag_sums(a_ref, hsel_ref, fold_ref):
    n_off = 2 * WIN_C - 1
    n_dr = 2 * WIN_R - 1
    qc = lax.broadcasted_iota(jnp.int32, (GRID_W, 128), 0)
    lane = lax.broadcasted_iota(jnp.int32, (GRID_W, 128), 1)
    diff = lane % GRID_W - qc + (WIN_C - 1)
    left = lane < GRID_W

    def by_dr(dr):
        return a_ref[dr + 4] + pltpu.roll(a_ref[dr + 3], GRID_W, 1)

    out = jnp.zeros((32, 128), F32)
    for j in range((n_dr + 1) // 2):
        hi = pltpu.roll(by_dr(2 * j + 1), GRID_W, 1) if 2 * j + 1 < n_dr else 0.0
        pair = jnp.where(left, by_dr(2 * j), hi)
        parts = []
        for o in range(n_off):
            mv = jnp.where(diff == o, pair, 0.0)
            acc = mv[0:8]
            for r8 in range(1, GRID_W // 8):
                acc = acc + mv[8 * r8:8 * (r8 + 1)]
            parts.append(acc)
        parts.append(jnp.zeros((8, 128), F32))
        stack = jnp.concatenate(parts, axis=0)
        s_hi = stack.astype(BF16)
        s_lo = (stack - s_hi.astype(F32)).astype(BF16)
        per_o = (jnp.dot(fold_ref[...], s_hi, preferred_element_type=F32)
                 + jnp.dot(fold_ref[...], s_lo, preferred_element_type=F32))
        out = out + _dot2(per_o, hsel_ref[j])
    return out


def _head_norm_bwd(raw, dn, gain, ones_ref):
    rr = lax.rsqrt(_head_sum(raw * raw, ones_ref) * (1.0 / HEAD_DIM) + EPS)
    hat = raw * rr
    dgain = jnp.sum(dn * hat, axis=0, keepdims=True)
    dhat = dn * gain
    mean = _head_sum(dhat * hat, ones_ref) * (1.0 / HEAD_DIM)
    return rr * (dhat - hat * mean), dgain


def _ctx_k_bwd(zc, dck, dcv, ones_blk, gk, foldm):
    C = dck.shape[0]

    def body(bk_ref, dk_ref, dv_ref, ones_ref, gk_ref, fold_ref, dbk_ref, dbv_ref, dgk_ref):
        dbk, dgk = _head_norm_bwd(bk_ref[...].astype(F32), dk_ref[...] * LN2, gk_ref[...], ones_ref)
        dbk_ref[...] = dbk.astype(BF16)
        dbv_ref[...] = dv_ref[...].astype(BF16)
        dgk_ref[...] = jnp.dot(jnp.broadcast_to(dgk, (8, 512)), fold_ref[...],
                               preferred_element_type=F32, precision=HI)

    row = pl.BlockSpec((C, 512), lambda i: (0, 0))
    cst = lambda a, b: pl.BlockSpec((a, b), lambda i: (0, 0))
    out_row = jax.ShapeDtypeStruct((C, 512), BF16)
    return pl.pallas_call(
        body, name="ctx_k_bwd", grid=(1,),
        in_specs=[row, row, row, cst(512, 512), cst(1, 512), cst(512, 128)],
        out_specs=[row, row, cst(8, 128)],
        out_shape=[out_row, out_row, jax.ShapeDtypeStruct((8, 128), F32)],
        compiler_params=_params(("arbitrary",)),
    )(zc, dck, dcv, ones_blk, gk, foldm)


def _bwd_mid(z, dmc, dqs, dk, dv, db_g, h, hc, dzc_k, dzc_v, sgn, ws, wst, bsb, ones8, ones_blk, gq, gk, foldm, tk,
             riders=()):
    T = z.shape[0]
    nt = T // tk
    blk = D_IN // N_DEV
    n_in, n_out, n_sc = 23, 7, 9
    n_rid = len(riders)

    def body(*refs):
        (au_ref, av_ref, ag_ref, bq_ref, bk_ref, d_ref, dq_ref, dk_ref, dv_ref, dbg_ref, h_ref,
         hc_ref, dzck_ref, dzcv_ref, sg_ref, ws_ref, wst_ref, bsb_ref, ones8_ref, ones_ref, gq_ref, gk_ref,
         fold_ref) = refs[:n_in]
        rid_src = refs[n_in:n_in + n_rid]
        dz_ref, sums_out, dws_ref, dbs_ref, dsg_ref, dgq_ref, dgk_ref = refs[n_in + n_rid:n_in + n_rid + n_out]
        rid_dst = refs[n_in + n_rid + n_out:n_in + 2 * n_rid + n_out]
        (acc, accq, acck, stage, sem, send_buf, tmp, s1, r1) = refs[n_in + 2 * n_rid + n_out:
                                                                   n_in + 2 * n_rid + n_out + n_sc]
        rid_sems = refs[n_in + 2 * n_rid + n_out + n_sc:]
        t = pl.program_id(0)

        @pl.when(t == 0)
        def _():
            for r, (ex, _) in enumerate(riders):
                ex.start(rid_src[r], rid_dst[r], *rid_sems[3 * r:3 * r + 3])
            acc[...] = jnp.zeros_like(acc)
            acc[512 * 4:512 * 5, :] = _tn(dzck_ref[...], hc_ref[...])
            acc[512 * 5:512 * 6, :] = _tn(dzcv_ref[...], hc_ref[...])
            dws_ref[...] = jnp.zeros_like(dws_ref)
            dbs_ref[...] = jnp.zeros_like(dbs_ref)
            dsg_ref[...] = jnp.zeros_like(dsg_ref)
            accq[...] = jnp.zeros_like(accq)
            acck[...] = jnp.zeros_like(acck)

        for g in range(SGU_GROUPS):
            ws_bf = ws_ref[g].astype(BF16)
            wst_bf = wst_ref[g].astype(BF16)
            sg = sg_ref[:, 128 * g:128 * (g + 1)]
            bsb_g = bsb_ref[g]
            for j in range(tk // CHUNK):
                rs, cs = slice(CHUNK * j, CHUNK * (j + 1)), slice(128 * g, 128 * (g + 1))
                au, av, ag = (au_ref[rs, cs].astype(F32), av_ref[rs, cs].astype(F32), ag_ref[rs, cs].astype(F32))
                d = d_ref[rs, cs].astype(F32)
                _, (gu, dgu, dgv, rr, vhat, vn, mixed, sig, sl) = _sgu_chunk_fwd(au, av, ag, sg, ws_bf, bsb_g)
                dz_ref[rs, 128 * g:128 * (g + 1)] = (d * mixed * sl * dgu).astype(BF16)
                dz_ref[rs, 1024 + 128 * g:1024 + 128 * (g + 1)] = (
                    d * gu * mixed * (sig * (1.0 + ag * (1.0 - sig)))).astype(BF16)
                dmixed = d * gu * sl
                dmb = dmixed.astype(BF16)
                dm_lo = (dmixed - dmb.astype(F32)).astype(BF16)
                dbs_ref[g] += _nt(ones8_ref[...], dmb) + _nt(ones8_ref[...], dm_lo)
                dws_ref[g] += _nt(dmb, vn.astype(BF16))
                dvn = jnp.dot(wst_bf, dmb, preferred_element_type=F32)
                dsg_ref[:, 128 * g:128 * (g + 1)] += jnp.sum(dvn * vhat, axis=0, keepdims=True)
                dvhat = dvn * sg
                mean = jnp.mean(dvhat * vhat, axis=-1, keepdims=True)
                dz_ref[rs, 512 + 128 * g:512 + 128 * (g + 1)] = (rr * (dvhat - vhat * mean) * dgv).astype(BF16)

        dbq, dgq = _head_norm_bwd(bq_ref[...].astype(F32), dq_ref[...] * ATT_SCALE, gq_ref[...], ones_ref)
        dz_ref[:, 512 * 3:512 * 4] = dbq.astype(BF16)
        accq[...] += dgq
        dbk, dgk = _head_norm_bwd(bk_ref[...].astype(F32), dk_ref[...] * LN2, gk_ref[...], ones_ref)
        dz_ref[:, 512 * 4:512 * 5] = dbk.astype(BF16)
        acck[...] += dgk
        dz_ref[:, 512 * 5:512 * 6] = dv_ref[...].astype(BF16)
        dz_ref[:, 512 * 6:512 * 7] = dbg_ref[...]

        hv = h_ref[...]
        for k in range(N_BRANCH):
            acc[512 * k:512 * (k + 1), :] += _tn(dz_ref[:, 512 * k:512 * (k + 1)], hv)

        @pl.when(t == nt - 1)
        def _():
            dgq_ref[...] = jnp.dot(jnp.broadcast_to(accq[...], (8, 512)), fold_ref[...],
                                   preferred_element_type=F32, precision=HI)
            dgk_ref[...] = jnp.dot(jnp.broadcast_to(acck[...], (8, 512)), fold_ref[...],
                                   preferred_element_type=F32, precision=HI)
            cidx = lax.axis_index("c")
            sib = (lax.axis_index("x"), lax.axis_index("y"), 1 - cidx)
            swaps = []
            for q in range(N_DEV // 2):
                theirs = acc[pl.ds(pl.multiple_of(2 * blk * q + blk * (1 - cidx), 8), blk), :]
                send_buf[q] = theirs.astype(BF16)
                cp = pltpu.make_async_remote_copy(src_ref=send_buf.at[q], dst_ref=tmp.at[q], send_sem=s1.at[q],
                                                  recv_sem=r1.at[q], device_id=sib, device_id_type=MESH)
                cp.start()
                swaps.append(cp)
            for q in range(N_DEV // 2):
                swaps[q].wait_recv()
                mine = acc[pl.ds(pl.multiple_of(2 * blk * q + blk * cidx, 8), blk), :]
                stage[...] = (mine + tmp[q].astype(F32)).astype(BF16)
                out = pltpu.make_async_copy(stage, sums_out.at[q], sem)
                out.start()
                out.wait()
            for cp in swaps:
                cp.wait_send()
            for r, (ex, _) in enumerate(riders):
                ex.wait(rid_src[r], rid_dst[r], *rid_sems[3 * r:3 * r + 3])

    zcol = lambda col: pl.BlockSpec((tk, 512), lambda t: (t, col))
    row = pl.BlockSpec((tk, 512), lambda t: (t, 0))
    whole = lambda a: pl.BlockSpec(a.shape, lambda t: (0,) * a.ndim)
    res = pl.pallas_call(
        body, name="bwd_mid", grid=(nt,),
        in_specs=[zcol(0), zcol(1), zcol(2), zcol(3), zcol(4), row, row, row, row, row,
                  pl.BlockSpec((tk, D_MODEL), lambda t: (t, 0)), whole(hc), whole(dzc_k), whole(dzc_v),
                  whole(sgn), whole(ws), whole(wst), whole(bsb), whole(ones8), whole(ones_blk), whole(gq), whole(gk),
                  whole(foldm)] + [ANY_SPEC] * n_rid,
        out_specs=[pl.BlockSpec((tk, D_IN), lambda t: (t, 0)), ANY_SPEC,
                   pl.BlockSpec((SGU_GROUPS, CHUNK, CHUNK), lambda t: (0, 0, 0)),
                   pl.BlockSpec((SGU_GROUPS, 8, CHUNK), lambda t: (0, 0, 0)),
                   pl.BlockSpec((1, 512), lambda t: (0, 0)), pl.BlockSpec((8, 128), lambda t: (0, 0)),
                   pl.BlockSpec((8, 128), lambda t: (0, 0))] + [ANY_SPEC] * n_rid,
        out_shape=[jax.ShapeDtypeStruct((T, D_IN), BF16), jax.ShapeDtypeStruct((N_DEV // 2, blk, D_MODEL), BF16),
                   jax.ShapeDtypeStruct((SGU_GROUPS, CHUNK, CHUNK), F32),
                   jax.ShapeDtypeStruct((SGU_GROUPS, 8, CHUNK), F32), jax.ShapeDtypeStruct((1, 512), F32),
                   jax.ShapeDtypeStruct((8, 128), F32), jax.ShapeDtypeStruct((8, 128), F32)]
                  + [ex.out_shape for ex, _ in riders],
        scratch_shapes=[pltpu.VMEM((D_IN, D_MODEL), F32), pltpu.VMEM((1, 512), F32), pltpu.VMEM((1, 512), F32),
                        pltpu.VMEM((blk, D_MODEL), BF16), pltpu.SemaphoreType.DMA,
                        pltpu.VMEM((N_DEV // 2, blk, D_MODEL), BF16), pltpu.VMEM((N_DEV // 2, blk, D_MODEL), BF16),
                        pltpu.SemaphoreType.DMA((N_DEV // 2,)), pltpu.SemaphoreType.DMA((N_DEV // 2,))]
                       + [s for ex, _ in riders for s in ex.scratch],
        compiler_params=_params(("arbitrary",)),
    )(z, z, z, z, z, dmc, dqs, dk, dv, db_g, h, hc, dzc_k, dzc_v, sgn, ws, wst, bsb, ones8, ones_blk, gq, gk, foldm,
      *[a for _, a in riders])
    return res[:n_out], res[n_out:]


def _inproj_bwd_dx(dzs, w_row0, w_in_t, x, dy, ng, scale, shift, tm, name, riders=()):
    T = x.shape[0]
    n = len(dzs)
    wpc = dzs[0].shape[1]
    nt = T // tm
    with_dx = dy is not None
    n_own_in = n + 5 + with_dx
    n_own_out = 3 + with_dx
    n_rid = len(riders)

    def body(*refs):
        dz_refs = refs[:n]
        n_in = n_own_in + n_rid
        own = refs[n:n_own_in] + refs[n_in:n_in + n_own_out]
        rid_src = refs[n_own_in:n_own_in + n_rid]
        rid_dst = refs[n_in + n_own_out:n_in + n_own_out + n_rid]
        rid_sems = refs[n_in + n_own_out + n_rid:]
        if with_dx:
            w_ref, x_ref, dy_ref, g_ref, sc_ref, sh_ref, gx_ref, dsh_ref, dsc_ref, dg_ref = own
        else:
            w_ref, x_ref, g_ref, sc_ref, sh_ref, dsh_ref, dsc_ref, dg_ref = own

        @pl.when(pl.program_id(0) == 0)
        def _():
            for r, (ex, _) in enumerate(riders):
                ex.start(rid_src[r], rid_dst[r], *rid_sems[3 * r:3 * r + 3])
            dsh_ref[...] = jnp.zeros_like(dsh_ref)
            dsc_ref[...] = jnp.zeros_like(dsc_ref)
            dg_ref[...] = jnp.zeros_like(dg_ref)

        dh = jnp.dot(dz_refs[0][...], w_ref[0:wpc, :], preferred_element_type=F32)
        for k in range(1, n):
            dh = dh + jnp.dot(dz_refs[k][...], w_ref[wpc * k:wpc * (k + 1), :], preferred_element_type=F32)
        xv = x_ref[...]
        r = lax.rsqrt(jnp.mean(xv * xv, axis=-1, keepdims=True) + EPS)
        xn = xv * r
        gv, op = g_ref[...], 1.0 + sc_ref[...]
        dsh_ref[...] += jnp.sum(dh, axis=0, keepdims=True)
        dsc_ref[...] += jnp.sum(dh * xn * gv, axis=0, keepdims=True)
        dg_ref[...] += jnp.sum(dh * op * xn, axis=0, keepdims=True)
        if with_dx:
            dxn = dh * (gv * op)
            gx_ref[...] = r * (dxn - xn * jnp.mean(dxn * xn, axis=-1, keepdims=True)) + dy_ref[...]

        @pl.when(pl.program_id(0) == nt - 1)
        def _():
            for r, (ex, _) in enumerate(riders):
                ex.wait(rid_src[r], rid_dst[r], *rid_sems[3 * r:3 * r + 3])

    vec = pl.BlockSpec((1, D_MODEL), lambda i: (0, 0))
    rowf = pl.BlockSpec((tm, D_MODEL), lambda i: (i, 0))
    in_specs = [pl.BlockSpec((tm, wpc), lambda i: (i, 0))] * n
    in_specs += [pl.BlockSpec((wpc * n, D_MODEL), lambda i: (w_row0 // (wpc * n), 0)), rowf]
    args = list(dzs) + [w_in_t, x]
    vshape = jax.ShapeDtypeStruct((1, D_MODEL), F32)
    out_specs, out_shape = [vec, vec, vec], [vshape, vshape, vshape]
    if with_dx:
        in_specs.append(rowf)
        args.append(dy)
        out_specs = [rowf] + out_specs
        out_shape = [jax.ShapeDtypeStruct((T, D_MODEL), F32)] + out_shape
    in_specs += [vec, vec, vec] + [ANY_SPEC] * n_rid
    args += [ng, scale, shift] + [a for _, a in riders]
    res = pl.pallas_call(
        body, name=name, grid=(nt,), in_specs=in_specs, out_specs=out_specs + [ANY_SPEC] * n_rid,
        out_shape=out_shape + [ex.out_shape for ex, _ in riders],
        scratch_shapes=[s for ex, _ in riders for s in ex.scratch],
        compiler_params=_params(("arbitrary",)),
    )(*args)
    return res[:n_own_out], res[n_own_out:]


def _adamw_sharded(w, gparts, m, v, tr, name, riders=()):
    R, C = w.shape
    n_part = gparts.shape[0]
    nt = R // tr
    n_rid = len(riders)

    def body(w_ref, gp_ref, m_ref, v_ref, *rest):
        rid_src = rest[:n_rid]
        g_ref, d_ref, m2_ref, v2_ref = rest[n_rid:n_rid + 4]
        rid_dst = rest[n_rid + 4:2 * n_rid + 4]
        rid_sems = rest[2 * n_rid + 4:]

        @pl.when(pl.program_id(0) == 0)
        def _():
            for r, (ex, _) in enumerate(riders):
                ex.start(rid_src[r], rid_dst[r], *rid_sems[3 * r:3 * r + 3])

        g = gp_ref[0].astype(F32)
        for d in range(1, n_part):
            g = g + gp_ref[d].astype(F32)
        delta, m2, v2 = _adam(w_ref[...], g, m_ref[...], v_ref[...])
        g_ref[...] = g
        d_ref[...] = delta
        m2_ref[...] = m2
        v2_ref[...] = v2

        @pl.when(pl.program_id(0) == nt - 1)
        def _():
            for r, (ex, _) in enumerate(riders):
                ex.wait(rid_src[r], rid_dst[r], *rid_sems[3 * r:3 * r + 3])

    row = pl.BlockSpec((tr, C), lambda i: (i, 0))
    sh = jax.ShapeDtypeStruct((R, C), F32)
    res = pl.pallas_call(
        body, name=name, grid=(nt,),
        in_specs=[row, pl.BlockSpec((n_part, tr, C), lambda i: (0, i, 0)), row, row] + [ANY_SPEC] * n_rid,
        out_specs=[row, row, row, row] + [ANY_SPEC] * n_rid,
        out_shape=[sh, sh, sh, sh] + [ex.out_shape for ex, _ in riders],
        scratch_shapes=[s for ex, _ in riders for s in ex.scratch],
        compiler_params=_params(("arbitrary",)),
    )(w, gparts, m, v, *[a for _, a in riders])
    return res[:4], res[4:]


def _pack_vectors(vec_rows, dsg, dgq, dgk, dgk_c, loss_part):
    flat = [a for _, arrs in vec_rows for a in arrs]

    def body(*refs):
        vecs = list(refs[:len(flat)])
        dsg_ref, dgq_ref, dgk_ref, dgkc_ref, loss_ref, v_ref = refs[len(flat):]
        row = lax.broadcasted_iota(jnp.int32, (16, D_MODEL), 0)
        misc = jnp.concatenate([dsg_ref[...], dgq_ref[0:1, :], dgk_ref[0:1, :], dgkc_ref[0:1, :],
                                loss_ref[...]], axis=1)
        v = jnp.where(row == V_MISC, jnp.broadcast_to(misc, (16, D_MODEL)), 0.0)
        for r, arrs in vec_rows:
            val = vecs.pop(0)[...]
            for _ in arrs[1:]:
                val = val + vecs.pop(0)[...]
            v = jnp.where(row == r, jnp.broadcast_to(val, (16, D_MODEL)), v)
        v_ref[...] = v

    return pl.pallas_call(
        body, name="pack_vectors", out_shape=jax.ShapeDtypeStruct((16, D_MODEL), F32), compiler_params=_params(),
    )(*flat, dsg, dgq, dgk, dgk_c, loss_part)


SMALL_NAMES = ("b_ada", "norm_g", "sgu_norm_g", "w_spatial", "b_spatial", "q_norm_g", "k_norm_g", "rpb")


def _adamw_small(vg, sg, rg, ws, ms, vs):
    k = len(SMALL_NAMES)

    def body(*refs):
        vg_ref, sg_ref, rg_ref = refs[0], refs[1], refs[2]
        refs = refs[1:]
        w_refs = dict(zip(SMALL_NAMES, refs[2:2 + k]))
        m_refs = dict(zip(SMALL_NAMES, refs[2 + k:2 + 2 * k]))
        v_refs = dict(zip(SMALL_NAMES, refs[2 + 2 * k:2 + 3 * k]))
        o_refs = [dict(zip(SMALL_NAMES, refs[2 + (3 + i) * k:2 + (4 + i) * k])) for i in range(4)]
        loss_ref = refs[2 + 7 * k]

        sv = vg_ref[0]
        for d in range(1, N_DEV):
            sv = sv + vg_ref[d]
        loss_ref[...] = sv[V_MISC:V_MISC + 1, 896:1024]

        def total(lo, hi, ref=sg_ref):
            s = ref[0, lo:hi, :].astype(F32)
            for d in range(1, N_DEV):
                s = s + ref[d, lo:hi, :].astype(F32)
            return s

        def emit(name, idx, g):
            res = _adam(w_refs[name][idx], g, m_refs[name][idx], v_refs[name][idx])
            for o, val in zip(o_refs, (g,) + res):
                o[name][idx] = val

        everything = (slice(None), slice(None))
        row = lambda r: sv[r:r + 1, :]
        emit("b_ada", everything, jnp.concatenate(
            [row(V_DSHIFT) + row(V_DCSHIFT), row(V_DSCALE) + row(V_DCSCALE), row(V_DGATE)], axis=1))
        emit("norm_g", everything, row(V_DNG) + row(V_DNG_CTX))
        misc = row(V_MISC)
        emit("sgu_norm_g", everything, misc[:, 0:512])
        emit("q_norm_g", everything, misc[:, 512:512 + HEAD_DIM])
        emit("k_norm_g", everything, misc[:, 640:640 + HEAD_DIM] + misc[:, 768:768 + HEAD_DIM])
        for g in range(SGU_GROUPS):
            emit("w_spatial", (0, g), total(128 * g, 128 * (g + 1)))
            emit("b_spatial", (0, slice(g, g + 1), slice(None)), total(M_DBS + 8 * g, M_DBS + 8 * (g + 1))[0:1, :])
        for hd in range(N_HEADS):
            by_dc = total(32 * hd, 32 * (hd + 1), rg_ref)
            emit("rpb", (0, hd), by_dc.T[0:2 * WIN_R - 1, 0:2 * WIN_C - 1])

    shapes = [jax.ShapeDtypeStruct(w.shape, F32) for w in ws]
    res = pl.pallas_call(body, name="adamw_small", out_shape=shapes * 4 + [jax.ShapeDtypeStruct((1, 128), F32)],
                         compiler_params=_params())(vg, sg, rg, *ws, *ms, *vs)
    return [res[i * k:(i + 1) * k] for i in range(4)], res[4 * k]


def _adamw_cctx(pc_g, w, m, v):
    def body(pc_ref, w_ref, m_ref, v_ref, g_ref, d_ref, m2_ref, v2_ref):
        pc = pc_ref[0, 0:1, :]
        for d in range(1, N_DEV):
            pc = pc + pc_ref[d, 0:1, :]
        cc = w_ref[...]
        sig = _sigmoid(cc)
        g = pc * (sig * (1.0 + cc * (1.0 - sig)))
        delta, m2, v2 = _adam(cc, g, m_ref[...], v_ref[...])
        g_ref[...] = g
        d_ref[...] = delta
        m2_ref[...] = m2
        v2_ref[...] = v2

    sh = jax.ShapeDtypeStruct((1, D_MODEL), F32)
    return pl.pallas_call(body, name="adamw_cctx", out_shape=[sh, sh, sh, sh], compiler_params=_params())(
        pc_g, w, m, v)


def _block_ones(n, blk):
    i = np.arange(n)
    return jnp.asarray((i[:, None] // blk == i[None, :] // blk).astype(np.float32), BF16)


def _rpb_pairs(rpb):
    n_off = 2 * WIN_C - 1
    cols = np.arange(GRID_W)
    c0 = np.clip(cols - WIN_C // 2, 0, GRID_W - WIN_C)
    in_win = (cols[None, :] >= c0[:, None]) & (cols[None, :] < c0[:, None] + WIN_C)
    dc = np.clip(cols[None, :] - cols[:, None] + (WIN_C - 1), 0, n_off - 1)
    expand = (dc[None] == np.arange(n_off)[:, None, None]) & in_win[None]
    toep = jnp.einsum("hrd,dqk->hrqk", rpb, jnp.asarray(expand, F32), precision=HI)
    toep = toep + jnp.asarray(np.where(in_win, 0.0, NEG_INF).astype(np.float32))
    neg = jnp.full((N_HEADS, 1, GRID_W, GRID_W), NEG_INF, F32)
    ext = jnp.concatenate([neg, toep, neg], axis=1)
    return jnp.concatenate([ext[:, :-1], ext[:, 1:]], axis=-1)


def _row_mask(rows):
    nrb = rows // Q_ROWS
    valid = np.zeros((3, Q_ROWS, 1, K_ROWS, 1), bool)
    for t, rb in enumerate((0, 1, nrb - 1)):
        kb = int(np.clip(Q_ROWS * rb - 4, 0, rows - K_ROWS))
        for i in range(Q_ROWS):
            r0 = int(np.clip(Q_ROWS * rb + i - WIN_R // 2, 0, rows - WIN_R))
            for j in range(K_ROWS):
                valid[t, i, 0, j, 0] = r0 <= kb + j < r0 + WIN_R
    full = np.broadcast_to(valid, (3, Q_ROWS, GRID_W, K_ROWS, GRID_W)).reshape(3, TQ, TK)
    return jnp.asarray(np.where(full, 0.0, NEG_INF).astype(np.float32))


def kernel(x, c, ctx, c_ctx, w_ada, b_ada, norm_g, w_in, sgu_norm_g, w_spatial, b_spatial, q_norm_g, k_norm_g, rpb, w_out, loss_target, m_c_ctx, m_w_ada, m_b_ada, m_norm_g, m_w_in, m_sgu_norm_g, m_w_spatial, m_b_spatial, m_q_norm_g, m_k_norm_g, m_rpb, m_w_out, v_c_ctx, v_w_ada, v_b_ada, v_norm_g, v_w_in, v_sgu_norm_g, v_w_spatial, v_b_spatial, v_q_norm_g, v_k_norm_g, v_rpb, v_w_out):
    me = 4 * lax.axis_index("x") + 2 * lax.axis_index("y") + lax.axis_index("c")
    x2, ctx2, tgt2 = x[0], ctx[0], loss_target[0]
    T, C = x2.shape[0], ctx2.shape[0]
    rows = T // GRID_W
    wada, win_t, wout = w_ada[0], w_in[0].T, w_out[0]
    ada_w = wada.shape[1]
    win_w = win_t.shape[0]

    row8 = lax.broadcasted_iota(jnp.int32, (8, D_MODEL), 0)
    c_blk = jnp.where(row8 == me, jnp.broadcast_to(c, (8, D_MODEL)), 0.0)
    b_sh = lax.dynamic_slice(b_ada, (0, me * ada_w), (1, ada_w))
    c_ctx_row = c_ctx.reshape(1, D_MODEL)

    ones512 = _block_ones(512, HEAD_DIM)
    ones8 = jnp.ones((8, 128), BF16)
    foldm = jnp.asarray((np.arange(512)[:, None] % HEAD_DIM == np.arange(128)[None, :]).astype(np.float32))
    lane_half = np.arange(128)[None, :, None] // GRID_W
    hsel = jnp.asarray((2 * np.arange(8)[:, None, None] + lane_half == np.arange(128)[None, None, :]).astype(np.float32),
                       BF16)
    foldr = jnp.asarray((np.arange(256)[None, :] // 8 == np.arange(32)[:, None]).astype(np.float32), BF16)
    gq512 = jnp.tile(q_norm_g, (1, N_HEADS))
    gk512 = jnp.tile(k_norm_g, (1, N_HEADS))
    ws = w_spatial[0]
    wst = ws.transpose(0, 2, 1)
    bsb = jnp.broadcast_to(b_spatial[0][:, :, None], (SGU_GROUPS, CHUNK, 128))
    pairs = _rpb_pairs(rpb[0])
    row_mask = _row_mask(rows)

    my_chip = me // 2
    order = jnp.stack([my_chip, my_chip ^ 2, my_chip ^ 1, my_chip ^ 3]).astype(jnp.int32)
    h, z, win_g, s16, part_g = _inproj_fwd(order, x2, norm_g, c_blk, c_ctx_row, wada, b_sh, win_t.astype(BF16), 512)
    w_in_b = win_g.reshape(D_IN, D_MODEL)
    mod16 = part_g.transpose(1, 0, 2).reshape(16, 3 * D_MODEL)
    mod = lax.dynamic_slice(mod16, (me, 0), (1, 3 * D_MODEL))
    shift, scale, gate = mod[:, :D_MODEL], mod[:, D_MODEL:2 * D_MODEL], mod[:, 2 * D_MODEL:]
    cshift, cscale = mod16[8:9, :D_MODEL], mod16[8:9, D_MODEL:2 * D_MODEL]
    qs, kn = _qk_norm(z, ones512, gq512, gk512, 512)
    hc, zc, ckn = _ctx_fwd(ctx2, norm_g, cscale, cshift, w_in_b, ones512, gk512)
    out_a = _sgu_fwd(z, sgu_norm_g, ws, bsb, 512)
    wout_blk = wout.astype(BF16)
    (ob, out_b, lse), (wout_g,) = _attn_fwd(qs, kn, z, ckn, zc, pairs, row_mask, [(_Hosted("ag", wout_blk), wout_blk)])
    w_out_b = wout_g.reshape(D_MODEL, D_MODEL)

    dy, dmc, dw_out, dgate, loss_part = _outproj_loss_bwd(x2, tgt2, out_a, out_b, gate, w_out_b, 512)
    dw_out_blocks = dw_out.reshape(N_DEV, D_MODEL // N_DEV, D_MODEL)
    (dqs, dk, dv, dck, dcv, db_g, drpb), (gout_parts,) = _attn_bwd(
        qs, kn, z, ckn, zc, pairs, row_mask, ob, lse, dmc, hsel, foldr,
        [(_Hosted("a2a", dw_out_blocks), dw_out_blocks)])
    dzc_k, dzc_v, dgk_c = _ctx_k_bwd(zc, dck, dcv, ones512, gk512, foldm)
    rloc = drpb.reshape(N_HEADS * 32, 128)
    (dz, chip_sums, dws, dbs, dsg, dgq, dgk), (rg,) = _bwd_mid(
        z, dmc, dqs, dk, dv, db_g, h, hc, dzc_k, dzc_v, sgu_norm_g, ws, wst, bsb, ones8, ones512, gq512, gk512, foldm,
        256, [(_Hosted("ag", rloc), rloc)])
    sloc = jnp.concatenate([dws.reshape(SGU_GROUPS * CHUNK, CHUNK), dbs.reshape(SGU_GROUPS * 8, CHUNK)]).astype(BF16)
    (grad_x, dshift, dscale, dng), (gin_parts,) = _inproj_bwd_dx(
        [dz], 0, w_in_b, x2, dy, norm_g, scale, shift, 512, "inproj_bwd_dx",
        [(_Hosted("chips", chip_sums), chip_sums)])
    (dcshift, dcscale, dng_c), _ = _inproj_bwd_dx([dzc_k, dzc_v], 4 * 512, w_in_b, ctx2, None, norm_g, cscale,
                                                  cshift, C, "ctx_bwd_dx")

    res_in, _ = _adamw_sharded(win_t, gin_parts, m_w_in[0].T, v_w_in[0].T, 112, "adamw_w_in")
    res_out, _ = _adamw_sharded(wout, gout_parts, m_w_out[0], v_w_out[0], 128, "adamw_w_out")

    zero_row = jnp.zeros((1, D_MODEL), F32)
    vec_rows = [(V_DSHIFT, [dshift]), (V_DSCALE, [dscale]), (V_DGATE, [dgate]), (V_DCSHIFT, [dcshift]),
                (V_DCSCALE, [dcscale]), (V_ZERO, [zero_row]), (V_DNG, [dng]), (V_DNG_CTX, [dng_c])]
    vloc = _pack_vectors(vec_rows, dsg, dgq, dgk, dgk_c, loss_part)
    vg, sg = _allgather_direct([vloc, sloc], "gather_small")
    small_w =(b_ada, norm_g, sgu_norm_g, w_spatial, b_spatial, q_norm_g, k_norm_g, rpb)
    small_m = (m_b_ada, m_norm_g, m_sgu_norm_g, m_w_spatial, m_b_spatial, m_q_norm_g, m_k_norm_g, m_rpb)
    small_v = (v_b_ada, v_norm_g, v_sgu_norm_g, v_w_spatial, v_b_spatial, v_q_norm_g, v_k_norm_g, v_rpb)
    res_small, loss_row = _adamw_small(vg, sg, rg, small_w, small_m, small_v)

    dm_all = vg[:, V_DSHIFT:V_DGATE + 1, :].reshape(N_DEV, 3 * D_MODEL)
    dc_all = vg[:, V_DCSHIFT:V_ZERO + 1, :].reshape(N_DEV, 3 * D_MODEL)
    dm_sh = lax.dynamic_slice(dm_all, (0, me * ada_w), (N_DEV, ada_w))
    dc_sh = lax.dynamic_slice(dc_all, (0, me * ada_w), (N_DEV, ada_w))
    *res_ada, pc = _ada_bwd(s16, dm_sh, dc_sh, wada, m_w_ada[0], v_w_ada[0])
    (pc_g,) = _allgather_direct([pc], "gather_cctx")
    res_cctx = _adamw_cctx(pc_g, c_ctx_row, m_c_ctx.reshape(1, D_MODEL), v_c_ctx.reshape(1, D_MODEL))

    loss = loss_row[0, 0]
    outs = [loss, grad_x[None]]
    for kind in range(4):
        by_name = dict(zip(SMALL_NAMES, res_small[kind]))
        by_name.update(c_ctx=res_cctx[kind].reshape(D_MODEL), w_ada=res_ada[kind][None],
                       w_in=res_in[kind].T[None], w_out=res_out[kind][None])
        outs += [by_name[nme] for nme in ("c_ctx", "w_ada", "b_ada", "norm_g", "w_in", "sgu_norm_g", "w_spatial",
                                          "b_spatial", "q_norm_g", "k_norm_g", "rpb", "w_out")]
    return tuple(outs)
```

```python
import functools

import numpy as np
import jax
import jax.numpy as jnp
from jax import lax
from jax.experimental import pallas as pl
from jax.experimental.pallas import tpu as pltpu

F32 = jnp.float32
BF16 = jnp.bfloat16
HI = lax.Precision.HIGHEST

N_DEV = 8
D_MODEL = 1024
D_A = 512
D_B = 512
D_IN = 3584
N_BRANCH = 7
HEAD_DIM = 64
N_HEADS = 8
GRID_W = 64
WIN_R = 8
WIN_C = 16
CHUNK = 128
SGU_GROUPS = 4
EPS = 1e-6
NEG_INF = -1e30
Q_ROWS = 4
K_ROWS = 12
TQ = Q_ROWS * GRID_W
TK = K_ROWS * GRID_W
N_DIAG = 22
ATT_SUB = 4
ATT_SCALE = HEAD_DIM ** -0.5
LOG2E = 1.4426950408889634
LN2 = 0.6931471805599453

ADAM_LR = 0.001
ADAM_B1 = 0.9
ADAM_B2 = 0.999
ADAM_EPS = 1e-08
ADAM_WD = 0.01
ADAM_STEP = 10

VMEM_LIMIT = 56 * 1024 * 1024
MESH = pl.DeviceIdType.MESH

V_DSHIFT, V_DSCALE, V_DGATE, V_DCSHIFT, V_DCSCALE, V_ZERO, V_DNG, V_DNG_CTX, V_MISC = range(9)
M_DBS = 512


def _params(sem=None):
    return pltpu.CompilerParams(dimension_semantics=sem, vmem_limit_bytes=VMEM_LIMIT)


def _sigmoid(x):
    return 1.0 / (1.0 + jnp.exp(-x))


def _gelu_parts(x):
    cdf = 0.5 * (1.0 + lax.erf(x * 0.7071067811865476))
    pdf = jnp.exp(-0.5 * x * x) * 0.3989422804014327
    return x * cdf, cdf + x * pdf


def _nt(a, b):
    return lax.dot_general(a, b, (((1,), (1,)), ((), ())), preferred_element_type=F32)


def _tn(a, b):
    return lax.dot_general(a, b, (((0,), (0,)), ((), ())), preferred_element_type=F32)


def _dot2(v, ones_bf):
    hi = v.astype(BF16)
    lo = (v - hi.astype(F32)).astype(BF16)
    return (jnp.dot(hi, ones_bf, preferred_element_type=F32)
            + jnp.dot(lo, ones_bf, preferred_element_type=F32))


def _head_sum(v, ones_ref):
    return jnp.dot(v.astype(BF16), ones_ref[...], preferred_element_type=F32)


def _adam(w, g, m, v):
    m2 = ADAM_B1 * m + (1.0 - ADAM_B1) * g
    v2 = ADAM_B2 * v + (1.0 - ADAM_B2) * (g * g)
    m_hat = m2 / (1.0 - ADAM_B1 ** ADAM_STEP)
    v_hat = v2 / (1.0 - ADAM_B2 ** ADAM_STEP)
    delta = -ADAM_LR * (m_hat / (jnp.sqrt(v_hat) + ADAM_EPS) + ADAM_WD * w)
    return delta, m2, v2


class _Hosted:
    def __init__(self, kind, src):
        self.kind = kind
        n_slot = {"a2a": N_DEV, "ag": N_DEV, "chips": N_DEV // 2}[kind]
        blk = src.shape if kind == "ag" else src.shape[1:]
        self.out_shape = jax.ShapeDtypeStruct((n_slot,) + tuple(blk), src.dtype)
        self.n_peer = n_slot - 1
        self.scratch = [pltpu.SemaphoreType.DMA((self.n_peer,)), pltpu.SemaphoreType.DMA((self.n_peer,)),
                        pltpu.SemaphoreType.DMA]

    def _copies(self, src, dst, send_sems, recv_sems, loc_sem, landing):
        x, y, c = lax.axis_index("x"), lax.axis_index("y"), lax.axis_index("c")
        if self.kind == "chips":
            me = 2 * x + y
            peers = [((px, py, c), 2 * px + py) for px, py in ((1 - x, y), (x, 1 - y), (1 - x, 1 - y))]
        else:
            me = 4 * x + 2 * y + c
            peers = []
            for k in range(1, N_DEV):
                px = 1 - x if (k >> 2) & 1 else x
                py = 1 - y if (k >> 1) & 1 else y
                pc = 1 - c if k & 1 else c
                peers.append(((px, py, pc), 4 * px + 2 * py + pc))
        remote = []
        for k, (peer, pid) in enumerate(peers):
            s = src if self.kind == "ag" else src.at[pid]
            remote.append(pltpu.make_async_remote_copy(
                src_ref=s, dst_ref=dst.at[pid if landing else me],
                send_sem=send_sems.at[k], recv_sem=recv_sems.at[k], device_id=peer, device_id_type=MESH))
        local = pltpu.make_async_copy(src if self.kind == "ag" else src.at[me], dst.at[me], loc_sem)
        return remote, local

    def start(self, src, dst, send_sems, recv_sems, loc_sem):
        remote, local = self._copies(src, dst, send_sems, recv_sems, loc_sem, landing=False)
        for cp in remote:
            cp.start()
        local.start()

    def wait(self, src, dst, send_sems, recv_sems, loc_sem):
        remote, local = self._copies(src, dst, send_sems, recv_sems, loc_sem, landing=True)
        for cp in remote:
            cp.wait_recv()
        for cp in remote:
            cp.wait_send()
        local.wait()


ANY_SPEC = pl.BlockSpec(memory_space=pl.ANY)


def _allgather_direct(arrs, name):
    n = len(arrs)
    exs = [_Hosted("ag", a) for a in arrs]

    def body(*refs):
        srcs, dsts, sems = refs[:n], refs[n:2 * n], refs[2 * n:]
        for r, ex in enumerate(exs):
            ex.start(srcs[r], dsts[r], *sems[3 * r:3 * r + 3])
        for r, ex in enumerate(exs):
            ex.wait(srcs[r], dsts[r], *sems[3 * r:3 * r + 3])

    return pl.pallas_call(body, name=name, out_shape=[ex.out_shape for ex in exs], in_specs=[ANY_SPEC] * n,
                          out_specs=[ANY_SPEC] * n, scratch_shapes=[s for ex in exs for s in ex.scratch])(*arrs)


def _ada_scratch(n_col):
    return ([pltpu.VMEM((N_DEV, 8, D_MODEL), F32), pltpu.VMEM((16, n_col), F32), pltpu.VMEM((N_DEV, 16, n_col), F32)]
            + [pltpu.SemaphoreType.DMA((N_DEV - 1,)) for _ in range(4)])


def _ada_modulation(cb_ref, cc_ref, w_ref, b_ref, cstack, part, parts, s1, r1, s2, r2):
    x, y, c = lax.axis_index("x"), lax.axis_index("y"), lax.axis_index("c")
    me = 4 * x + 2 * y + c
    peers = []
    for k in range(1, N_DEV):
        px = 1 - x if (k >> 2) & 1 else x
        py = 1 - y if (k >> 1) & 1 else y
        pc = 1 - c if k & 1 else c
        peers.append(((px, py, pc), 4 * px + 2 * py + pc))

    def exchange(src, dst, send_sems, recv_sems):
        for k, (peer, _) in enumerate(peers):
            pltpu.make_async_remote_copy(src_ref=src, dst_ref=dst.at[me], send_sem=send_sems.at[k],
                                         recv_sem=recv_sems.at[k], device_id=peer, device_id_type=MESH).start()
        dst[me] = src[...]
        waits = [pltpu.make_async_remote_copy(src_ref=src, dst_ref=dst.at[pid], send_sem=send_sems.at[k],
                                              recv_sem=recv_sems.at[k], device_id=peer, device_id_type=MESH)
                 for k, (peer, pid) in enumerate(peers)]
        for cp in waits:
            cp.wait_recv()
        for cp in waits:
            cp.wait_send()

    exchange(cb_ref, cstack, s1, r1)
    c_all = cstack[0]
    for d in range(1, N_DEV):
        c_all = c_all + cstack[d]
    row = lax.broadcasted_iota(jnp.int32, (8, D_MODEL), 0)
    cc = jnp.where(row == 0, jnp.broadcast_to(cc_ref[...], (8, D_MODEL)), 0.0)
    call = jnp.concatenate([c_all, cc], axis=0)
    s = call * _sigmoid(call)
    part[...] = jnp.dot(s, w_ref[...], preferred_element_type=F32, precision=HI) + b_ref[...]
    exchange(part, parts, s2, r2)
    return s


def _ada_bwd(s16, dm, dc, w, m, v):
    def body(s_ref, dm_ref, dc_ref, w_ref, m_ref, v_ref, g_ref, d_ref, m2_ref, v2_ref, pc_ref):
        dct = jnp.sum(dc_ref[...], axis=0, keepdims=True)
        row = lax.broadcasted_iota(jnp.int32, dc_ref.shape, 0)
        dcb = jnp.where(row == 0, jnp.broadcast_to(dct, dc_ref.shape), 0.0)
        dm16 = jnp.concatenate([dm_ref[...], dcb], axis=0)
        g = lax.dot_general(s_ref[...], dm16, (((0,), (0,)), ((), ())),
                            preferred_element_type=F32, precision=HI)
        w_ = w_ref[...]
        delta, m2, v2 = _adam(w_, g, m_ref[...], v_ref[...])
        g_ref[...] = g
        d_ref[...] = delta
        m2_ref[...] = m2
        v2_ref[...] = v2
        pc_ref[...] = lax.dot_general(dcb, w_, (((1,), (1,)), ((), ())),
                                      preferred_element_type=F32, precision=HI)

    sh = jax.ShapeDtypeStruct(w.shape, F32)
    return pl.pallas_call(
        body, name="ada_bwd",
        out_shape=[sh, sh, sh, sh, jax.ShapeDtypeStruct((8, D_MODEL), F32)],
        compiler_params=_params(),
    )(s16, dm, dc, w, m, v)


def _head_norm(zk, ones_ref, gain):
    ss = _head_sum(zk * zk, ones_ref)
    return zk * lax.rsqrt(ss * (1.0 / HEAD_DIM) + EPS) * gain


def _inproj_fwd(order, x, ng, c_blk, c_ctx_row, w_ada_sh, b_ada_sh, w_blk_t, tm):
    T = x.shape[0]
    nt = T // tm
    n_pass = N_DEV // 2
    blk_rows = w_blk_t.shape[0]
    n_col = w_ada_sh.shape[1]

    def body(order_ref, x_ref, g_ref, cb_ref, cc_ref, wa_ref, ba_ref, wb_ref,
             h_out, z_ref, wt_out, s_out, parts_out,
             hs, wt, modv, send_sems, recv_sems, loc_sem, h_sem, wt_sem, *ada_sc):
        p, i = pl.program_id(0), pl.program_id(1)
        x, y, c = lax.axis_index("x"), lax.axis_index("y"), lax.axis_index("c")
        me, sib = (x, y, c), (x, y, 1 - c)
        chips = [(1 - x, y), (x, 1 - y), (1 - x, 1 - y)]

        def slot(px, py, pc):
            return 4 * px + 2 * py + pc

        def copy(k, block, to, src=None):
            return pltpu.make_async_remote_copy(
                src_ref=wt.at[slot(*block)] if src is None else src, dst_ref=wt.at[slot(*block)],
                send_sem=send_sems.at[k], recv_sem=recv_sems.at[k], device_id=to, device_id_type=MESH)

        own = pltpu.make_async_copy(wb_ref, wt.at[slot(*me)], loc_sem)
        h_copy = pltpu.make_async_copy(hs, h_out, h_sem)
        wt_copy = pltpu.make_async_copy(wt, wt_out, wt_sem)
        first = [copy(1 + j, me, (*chip, c), src=wb_ref) for j, chip in enumerate(chips[:2])] + [copy(0, me, sib, src=wb_ref)]
        passed = [copy(4 + j, (*chip, c), sib) for j, chip in enumerate(chips)]
        relay_src = (jnp.where(c == 0, 1 - x, x), jnp.where(c == 0, y, 1 - y), c)
        relay_dst = (jnp.where(c == 0, x, 1 - x), jnp.where(c == 0, 1 - y, y), c)
        relay = copy(3, relay_src, relay_dst)

        @pl.when(jnp.logical_and(p == 0, i == 0))
        def _():
            s_out[...] = _ada_modulation(cb_ref, cc_ref, wa_ref, ba_ref, *ada_sc)
            own.start()
            for cp in first:
                cp.start()
            parts = ada_sc[2]
            parts_out[...] = parts[...]
            my_row = pl.ds(slot(*me), 1)
            mod = jnp.concatenate([parts[d, my_row, :] for d in range(N_DEV)], axis=1)
            modv[0:1, :] = mod[:, 0:D_MODEL]
            modv[1:2, :] = mod[:, D_MODEL:2 * D_MODEL]
            own.wait()
            copy(0, sib, me).wait_recv()

        @pl.when(jnp.logical_and(p == 1, i == 0))
        def _():
            for j, chip in enumerate(chips[:2]):
                copy(1 + j, (*chip, c), me).wait_recv()
            relay.start()
            passed[0].start()
            passed[1].start()
            copy(4, (*chips[0], 1 - c), me).wait_recv()

        @pl.when(jnp.logical_and(p == 2, i == 0))
        def _():
            copy(5, (*chips[1], 1 - c), me).wait_recv()

        @pl.when(jnp.logical_and(p == 3, i == 0))
        def _():
            copy(3, (*chips[2], c), me).wait_recv()
            passed[2].start()
            copy(6, (*chips[2], 1 - c), me).wait_recv()
            wt_copy.start()

        rows = pl.ds(pl.multiple_of(i * tm, tm), tm)

        @pl.when(p == 0)
        def _():
            xv = x_ref[...]
            r = lax.rsqrt(jnp.mean(xv * xv, axis=-1, keepdims=True) + EPS)
            hs[rows, :] = ((xv * r * g_ref[...]) * (1.0 + modv[1:2, :]) + modv[0:1, :]).astype(BF16)

        @pl.when(jnp.logical_and(p == 1, i == 0))
        def _():
            h_copy.start()

        w_pair = wt[pl.ds(2 * order_ref[p], 2)].reshape(2 * blk_rows, D_MODEL)
        z_ref[...] = _nt(hs[rows, :], w_pair).astype(BF16)

        @pl.when(jnp.logical_and(p == n_pass - 1, i == nt - 1))
        def _():
            for cp in first + passed + [relay]:
                cp.wait_send()
            h_copy.wait()
            wt_copy.wait()

    whole = lambda shape: pl.BlockSpec(shape, lambda p, i, o: (0,) * len(shape))
    grid_spec = pltpu.PrefetchScalarGridSpec(
        num_scalar_prefetch=1, grid=(n_pass, nt),
        in_specs=[pl.BlockSpec((tm, D_MODEL), lambda p, i, o: (jnp.where(p == 0, i, nt - 1), 0)),
                  whole((1, D_MODEL)), whole((8, D_MODEL)), whole((1, D_MODEL)), whole((D_MODEL, n_col)),
                  whole((1, n_col)), ANY_SPEC],
        out_specs=[ANY_SPEC, pl.BlockSpec((tm, 2 * blk_rows), lambda p, i, o: (i, o[p])), ANY_SPEC,
                   whole((16, D_MODEL)), whole((N_DEV, 16, n_col))],
        scratch_shapes=[pltpu.VMEM((T, D_MODEL), BF16), pltpu.VMEM((N_DEV, blk_rows, D_MODEL), BF16),
                        pltpu.VMEM((8, D_MODEL), F32),
                        pltpu.SemaphoreType.DMA((7,)), pltpu.SemaphoreType.DMA((7,)), pltpu.SemaphoreType.DMA,
                        pltpu.SemaphoreType.DMA, pltpu.SemaphoreType.DMA] + _ada_scratch(n_col))
    return pl.pallas_call(
        body, name="inproj_fwd", grid_spec=grid_spec,
        out_shape=[jax.ShapeDtypeStruct((T, D_MODEL), BF16), jax.ShapeDtypeStruct((T, D_IN), BF16),
                   jax.ShapeDtypeStruct((N_DEV, blk_rows, D_MODEL), BF16),
                   jax.ShapeDtypeStruct((16, D_MODEL), F32), jax.ShapeDtypeStruct((N_DEV, 16, n_col), F32)],
        compiler_params=_params(("arbitrary", "arbitrary")),
    )(order, x, ng, c_blk, c_ctx_row, w_ada_sh, b_ada_sh, w_blk_t)


def _ctx_fwd(ctx, ng, cscale, cshift, w_in_t, ones_blk, gk):
    C = ctx.shape[0]

    def body(x_ref, g_ref, sc_ref, sh_ref, w_ref, ones_ref, gk_ref, h_ref, z_ref, kn_ref):
        xv = x_ref[...]
        r = lax.rsqrt(jnp.mean(xv * xv, axis=-1, keepdims=True) + EPS)
        h = (xv * r * g_ref[...]) * (1.0 + sc_ref[...]) + sh_ref[...]
        hb = h.astype(BF16)
        h_ref[...] = hb
        zk = _nt(hb, w_ref[0:512, :])
        zv = _nt(hb, w_ref[512:1024, :])
        z_ref[:, 0:512] = zk.astype(BF16)
        z_ref[:, 512:1024] = zv.astype(BF16)
        kn_ref[...] = _head_norm(zk, ones_ref, gk_ref[...]).astype(BF16)

    vec = pl.BlockSpec((1, D_MODEL), lambda i: (0, 0))
    return pl.pallas_call(
        body, name="ctx_fwd", grid=(1,),
        in_specs=[pl.BlockSpec((C, D_MODEL), lambda i: (0, 0)), vec, vec, vec,
                  pl.BlockSpec((1024, D_MODEL), lambda i: (2, 0)),
                  pl.BlockSpec((512, 512), lambda i: (0, 0)), pl.BlockSpec((1, 512), lambda i: (0, 0))],
        out_specs=[pl.BlockSpec((C, D_MODEL), lambda i: (0, 0)), pl.BlockSpec((C, 1024), lambda i: (0, 0)),
                   pl.BlockSpec((C, 512), lambda i: (0, 0))],
        out_shape=[jax.ShapeDtypeStruct((C, D_MODEL), BF16), jax.ShapeDtypeStruct((C, 1024), BF16),
                   jax.ShapeDtypeStruct((C, 512), BF16)],
        compiler_params=_params(("arbitrary",)),
    )(ctx, ng, cscale, cshift, w_in_t, ones_blk, gk)


def _sgu_chunk_fwd(au, av, ag, sg, ws_bf, bsb):
    gu, dgu = _gelu_parts(au)
    gv, dgv = _gelu_parts(av)
    rr = lax.rsqrt(jnp.mean(gv * gv, axis=-1, keepdims=True) + EPS)
    vhat = gv * rr
    vn = vhat * sg
    mixed = jnp.dot(ws_bf, vn.astype(BF16), preferred_element_type=F32) + bsb
    sig = _sigmoid(ag)
    sl = ag * sig
    return gu * mixed * sl, (gu, dgu, dgv, rr, vhat, vn, mixed, sig, sl)


def _sgu_qk_fwd(z, sgn, ws, bsb, ones_blk, gq, gk, tm):
    T = z.shape[0]

    def body(au_ref, av_ref, ag_ref, q_ref, k_ref, sg_ref, ws_ref, bsb_ref, ones_ref, gq_ref, gk_ref,
             o_ref, qs_ref, kn_ref):
        qs = _head_norm(q_ref[...].astype(F32), ones_ref, gq_ref[...]) * (ATT_SCALE * LOG2E)
        qs_ref[...] = qs.astype(BF16)
        kn_ref[...] = _head_norm(k_ref[...].astype(F32), ones_ref, gk_ref[...]).astype(BF16)
        for g in range(SGU_GROUPS):
            ws_bf = ws_ref[g].astype(BF16)
            sg = sg_ref[:, 128 * g:128 * (g + 1)]
            bsb_g = bsb_ref[g]
            for j in range(tm // CHUNK):
                rs, cs = slice(CHUNK * j, CHUNK * (j + 1)), slice(128 * g, 128 * (g + 1))
                out, _ = _sgu_chunk_fwd(au_ref[rs, cs].astype(F32), av_ref[rs, cs].astype(F32),
                                        ag_ref[rs, cs].astype(F32), sg, ws_bf, bsb_g)
                o_ref[rs, cs] = out.astype(BF16)

    zcol = lambda col: pl.BlockSpec((tm, 512), lambda i: (i, col))
    v512 = pl.BlockSpec((1, 512), lambda i: (0, 0))
    row = pl.BlockSpec((tm, 512), lambda i: (i, 0))
    out = jax.ShapeDtypeStruct((T, 512), BF16)
    return pl.pallas_call(
        body, name="sgu_qk_fwd", grid=(T // tm,),
        in_specs=[zcol(0), zcol(1), zcol(2), zcol(3), zcol(4), v512,
                  pl.BlockSpec((SGU_GROUPS, CHUNK, CHUNK), lambda i: (0, 0, 0)),
                  pl.BlockSpec((SGU_GROUPS, CHUNK, 128), lambda i: (0, 0, 0)),
                  pl.BlockSpec((512, 512), lambda i: (0, 0)), v512, v512],
        out_specs=[row, row, row], out_shape=[out, out, out],
        compiler_params=_params(("arbitrary",)),
    )(z, z, z, z, z, sgn, ws, bsb, ones_blk, gq, gk)


def _attn_type(rb, nrb):
    return jnp.where(rb == 0, 0, jnp.where(rb == nrb - 1, 2, 1))


def _attn_specs(T, C):
    return [
        pl.BlockSpec((ATT_SUB * TQ, 128), lambda hp, st: (st, hp)),
        pl.BlockSpec((T, 128), lambda hp, st: (0, hp)),
        pl.BlockSpec((T, 128), lambda hp, st: (0, 20 + hp)),
        pl.BlockSpec((C, 128), lambda hp, st: (0, hp)),
        pl.BlockSpec((C, 128), lambda hp, st: (0, 4 + hp)),
        pl.BlockSpec((2, 2 * WIN_R, GRID_W, 128), lambda hp, st: (hp, 0, 0, 0)),
        pl.BlockSpec((3, TQ, TK), lambda hp, st: (0, 0, 0)),
        pl.BlockSpec((ATT_SUB * TQ, 128), lambda hp, st: (st, 24 + hp)),
    ]


def _build_bias(pairs_ref, mask_ref, bias_sc):
    for t in range(3):
        for hh in range(2):
            for i in range(Q_ROWS):
                for mm in range(K_ROWS // 2):
                    p = min(max(WIN_R - Q_ROWS * t + 2 * mm - i, 0), 2 * WIN_R - 1)
                    rs, cs = slice(GRID_W * i, GRID_W * (i + 1)), slice(128 * mm, 128 * (mm + 1))
                    bias_sc[t, hh, rs, cs] = (pairs_ref[hh, p] + mask_ref[t, rs, cs]) * LOG2E


def _attn_fwd(qs, kn, z, ckn, zc, pairs, row_mask, riders=()):
    T, C = qs.shape[0], ckn.shape[0]
    rows = T // GRID_W
    nrb = rows // Q_ROWS
    n_st = nrb // ATT_SUB
    n_rid = len(riders)

    def body(q_ref, k_ref, v_ref, ck_ref, cv_ref, pairs_ref, mask_ref, bg_ref, *rest):
        rid_src = rest[:n_rid]
        ob_ref, outb_ref, lse_ref = rest[n_rid:n_rid + 3]
        rid_dst = rest[n_rid + 3:2 * n_rid + 3]
        bias_sc = rest[2 * n_rid + 3]
        rid_sems = rest[2 * n_rid + 4:]

        @pl.when(jnp.logical_and(pl.program_id(0) == 0, pl.program_id(1) == 0))
        def _():
            for r, (ex, _) in enumerate(riders):
                ex.start(rid_src[r], rid_dst[r], *rid_sems[3 * r:3 * r + 3])

        @pl.when(pl.program_id(1) == 0)
        def _():
            _build_bias(pairs_ref, mask_ref, bias_sc)

        ck2, cv2 = ck_ref[...], cv_ref[...]
        lane = lax.broadcasted_iota(jnp.int32, (1, 128), 1)
        for sub in range(ATT_SUB):
            rb = ATT_SUB * pl.program_id(1) + sub
            bias_ref = bias_sc.at[_attn_type(rb, nrb)]
            rs = slice(TQ * sub, TQ * (sub + 1))
            ks = pl.multiple_of(jnp.clip(Q_ROWS * rb - 4, 0, rows - K_ROWS) * GRID_W, GRID_W)
            q2 = q_ref[rs, :]
            k2 = k_ref[pl.ds(ks, TK), :]
            v2 = v_ref[pl.ds(ks, TK), :]
            o_acc = jnp.zeros((TQ, 128), F32)
            lse_acc = jnp.zeros((TQ, 128), F32)
            for hh in range(2):
                msk = (lane >= HEAD_DIM) == bool(hh)
                qm = jnp.where(msk, q2, jnp.zeros_like(q2))
                s = _nt(qm, k2) + bias_ref[hh]
                sc = _nt(qm, ck2)
                m = jnp.maximum(jnp.max(s, axis=-1, keepdims=True), jnp.max(sc, axis=-1, keepdims=True))
                p = jnp.exp2(s - m)
                pc = jnp.exp2(sc - m)
                va = jnp.where(msk, v2, jnp.ones_like(v2))
                cva = jnp.where(msk, cv2, jnp.ones_like(cv2))
                num = (jnp.dot(p.astype(BF16), va, preferred_element_type=F32)
                       + jnp.dot(pc.astype(BF16), cva, preferred_element_type=F32))
                den = pltpu.roll(num, HEAD_DIM, 1)
                o_acc = jnp.where(msk, num / den, o_acc)
                lse_acc = jnp.where(msk, m + jnp.log(den) * LOG2E, lse_acc)
            ob_ref[rs, :] = o_acc.astype(BF16)
            lse_ref[rs, :] = lse_acc
            bg = bg_ref[rs, :].astype(F32)
            outb_ref[rs, :] = (o_acc * (bg * _sigmoid(bg))).astype(BF16)

        @pl.when(jnp.logical_and(pl.program_id(0) == pl.num_programs(0) - 1, pl.program_id(1) == n_st - 1))
        def _():
            for r, (ex, _) in enumerate(riders):
                ex.wait(rid_src[r], rid_dst[r], *rid_sems[3 * r:3 * r + 3])

    tile = pl.BlockSpec((ATT_SUB * TQ, 128), lambda hp, st: (st, hp))
    res = pl.pallas_call(
        body, name="attn_fwd", grid=(4, n_st),
        in_specs=_attn_specs(T, C) + [ANY_SPEC] * n_rid,
        out_specs=[tile, tile, tile] + [ANY_SPEC] * n_rid,
        out_shape=[jax.ShapeDtypeStruct((T, 512), BF16), jax.ShapeDtypeStruct((T, 512), BF16),
                   jax.ShapeDtypeStruct((T, 512), F32)] + [ex.out_shape for ex, _ in riders],
        scratch_shapes=[pltpu.VMEM((3, 2, TQ, TK), F32)] + [s for ex, _ in riders for s in ex.scratch],
        compiler_params=_params(("arbitrary", "arbitrary")),
    )(qs, kn, z, ckn, zc, pairs, row_mask, z, *[a for _, a in riders])
    return res[:3], res[3:]


def _outproj_loss_bwd(x, tgt, out_a, out_b, gate, w_out, tm):
    T = x.shape[0]
    nt = T // tm

    def body(x_ref, t_ref, oa_ref, ob_ref, gate_ref, w_ref, dy_ref, dmc_ref, dw_ref, dgate_ref, loss_ref, acc):
        @pl.when(pl.program_id(0) == 0)
        def _():
            acc[...] = jnp.zeros_like(acc)
            dgate_ref[...] = jnp.zeros_like(dgate_ref)
            loss_ref[...] = jnp.zeros_like(loss_ref)

        oa, ob = oa_ref[...], ob_ref[...]
        gate_v = gate_ref[...]
        mix = (jnp.dot(oa, w_ref[0:512, :], preferred_element_type=F32)
               + jnp.dot(ob, w_ref[512:1024, :], preferred_element_type=F32))
        e = x_ref[...] + gate_v * mix - t_ref[...]
        se = jnp.sum(jnp.sum(e * e, axis=0, keepdims=True), axis=1, keepdims=True)
        loss_ref[...] += jnp.broadcast_to(se * (0.5 / D_MODEL), loss_ref.shape)
        dy = e * (1.0 / D_MODEL)
        dy_ref[...] = dy
        dgate_ref[...] += jnp.sum(dy * mix, axis=0, keepdims=True)
        dmix = (dy * gate_v).astype(BF16)
        dmc_ref[...] = _nt(dmix, w_ref[...]).astype(BF16)
        acc[0:512, :] += _tn(oa, dmix)
        acc[512:1024, :] += _tn(ob, dmix)

        @pl.when(pl.program_id(0) == nt - 1)
        def _():
            dw_ref[...] = acc[...].astype(BF16)

    row = lambda w: pl.BlockSpec((tm, w), lambda i: (i, 0))
    return pl.pallas_call(
        body, name="outproj_loss_bwd", grid=(nt,),
        in_specs=[row(D_MODEL), row(D_MODEL), row(512), row(512),
                  pl.BlockSpec((1, D_MODEL), lambda i: (0, 0)),
                  pl.BlockSpec((D_MODEL, D_MODEL), lambda i: (0, 0))],
        out_specs=[row(D_MODEL), row(D_MODEL), pl.BlockSpec((D_MODEL, D_MODEL), lambda i: (0, 0)),
                   pl.BlockSpec((1, D_MODEL), lambda i: (0, 0)), pl.BlockSpec((1, 128), lambda i: (0, 0))],
        out_shape=[jax.ShapeDtypeStruct((T, D_MODEL), F32), jax.ShapeDtypeStruct((T, D_MODEL), BF16),
                   jax.ShapeDtypeStruct((D_MODEL, D_MODEL), BF16), jax.ShapeDtypeStruct((1, D_MODEL), F32),
                   jax.ShapeDtypeStruct((1, 128), F32)],
        scratch_shapes=[pltpu.VMEM((D_MODEL, D_MODEL), F32)],
        compiler_params=_params(("arbitrary",)),
    )(x, tgt, out_a, out_b, gate, w_out)


def _attn_bwd(qs, kn, z, ckn, zc, pairs, row_mask, ob, lse, dmc, hsel, fold, riders):
    T, C = qs.shape[0], ckn.shape[0]
    rows = T // GRID_W
    nrb = rows // Q_ROWS
    n_st = nrb // ATT_SUB
    n_rid = len(riders)

    def body(q_ref, k_ref, v_ref, ck_ref, cv_ref, pairs_ref, mask_ref, bg_ref, ob_ref, lse_ref, do_ref,
             hsel_ref, fold_ref, *rest):
        rid_src = rest[:n_rid]
        dq_ref, dk_ref, dv_ref, dck_ref, dcv_ref, dbg_ref, drpb_ref = rest[n_rid:n_rid + 7]
        rid_dst = rest[n_rid + 7:2 * n_rid + 7]
        bias_sc, dacc_ref = rest[2 * n_rid + 7:2 * n_rid + 9]
        rid_sems = rest[2 * n_rid + 9:]
        hp, st = pl.program_id(0), pl.program_id(1)

        @pl.when(jnp.logical_and(hp == 0, st == 0))
        def _():
            for r, (ex, _) in enumerate(riders):
                ex.start(rid_src[r], rid_dst[r], *rid_sems[3 * r:3 * r + 3])

        @pl.when(st == 0)
        def _():
            _build_bias(pairs_ref, mask_ref, bias_sc)
            dk_ref[...] = jnp.zeros_like(dk_ref)
            dv_ref[...] = jnp.zeros_like(dv_ref)
            dck_ref[...] = jnp.zeros_like(dck_ref)
            dcv_ref[...] = jnp.zeros_like(dcv_ref)
            dacc_ref[...] = jnp.zeros_like(dacc_ref)

        ck2, cv2 = ck_ref[...], cv_ref[...]
        lane = lax.broadcasted_iota(jnp.int32, (1, 128), 1)
        for sub in range(ATT_SUB):
            rb = ATT_SUB * st + sub
            bias_ref = bias_sc.at[_attn_type(rb, nrb)]
            rs = slice(TQ * sub, TQ * (sub + 1))
            kb = jnp.clip(Q_ROWS * rb - 4, 0, rows - K_ROWS)
            ks = pl.multiple_of(kb * GRID_W, GRID_W)
            ebase = kb - Q_ROWS * rb + 11
            q2 = q_ref[rs, :]
            k2 = k_ref[pl.ds(ks, TK), :]
            v2 = v_ref[pl.ds(ks, TK), :]
            bg = bg_ref[rs, :].astype(F32)
            sig = _sigmoid(bg)
            obv = ob_ref[rs, :].astype(F32)
            dout = do_ref[rs, :].astype(F32)
            dbg_ref[rs, :] = (dout * obv * (sig * (1.0 + bg * (1.0 - sig)))).astype(BF16)
            d_o = dout * (bg * sig)
            d_oo = d_o * obv
            lse2 = lse_ref[rs, :]
            dq_acc = jnp.zeros((TQ, 128), F32)
            for hh in range(2):
                msk = (lane >= HEAD_DIM) == bool(hh)
                qm = jnp.where(msk, q2, jnp.zeros_like(q2))
                lse_h = jnp.max(jnp.where(msk, lse2, -jnp.inf), axis=-1, keepdims=True)
                p = jnp.exp2(_nt(qm, k2) + bias_ref[hh] - lse_h)
                pc = jnp.exp2(_nt(qm, ck2) - lse_h)
                dom_f = jnp.where(msk, d_o, 0.0)
                dom = dom_f.astype(BF16)
                delta = jnp.sum(jnp.where(msk, d_oo, 0.0), axis=-1, keepdims=True)
                d_hi = delta.astype(BF16).astype(F32)
                x0 = HEAD_DIM * (1 - hh)
                dom_aug = jnp.where(lane == x0, -d_hi, jnp.where(lane == x0 + 1, d_hi - delta, dom_f)).astype(BF16)
                extra = jnp.logical_or(lane == x0, lane == x0 + 1)
                va = jnp.where(msk, v2, jnp.where(extra, jnp.ones_like(v2), jnp.zeros_like(v2)))
                cva = jnp.where(msk, cv2, jnp.where(extra, jnp.ones_like(cv2), jnp.zeros_like(cv2)))
                ds = p * _nt(dom_aug, va)
                dsc = pc * _nt(dom_aug, cva)
                dsb, dscb = ds.astype(BF16), dsc.astype(BF16)
                dq_h = (jnp.dot(dsb, k2, preferred_element_type=F32)
                        + jnp.dot(dscb, ck2, preferred_element_type=F32))
                dq_acc = jnp.where(msk, dq_h, dq_acc)
                dk_ref[pl.ds(ks, TK), :] += _tn(dsb, qm)
                dv_ref[pl.ds(ks, TK), :] += _tn(p.astype(BF16), dom)
                dck_ref[...] += _tn(dscb, qm)
                dcv_ref[...] += _tn(pc.astype(BF16), dom)
                for i in range(Q_ROWS):
                    for mm in range(K_ROWS // 2):
                        dacc_ref[hh, ebase + (2 * mm - i)] += ds[GRID_W * i:GRID_W * (i + 1),
                                                                 128 * mm:128 * (mm + 1)]
            dq_ref[rs, :] = dq_acc

        @pl.when(st == n_st - 1)
        def _():
            for hh in range(2):
                drpb_ref[hh] = _rpb_diag_sums(dacc_ref.at[hh], hsel_ref, fold_ref)

        @pl.when(jnp.logical_and(hp == pl.num_programs(0) - 1, st == n_st - 1))
        def _():
            for r, (ex, _) in enumerate(riders):
                ex.wait(rid_src[r], rid_dst[r], *rid_sems[3 * r:3 * r + 3])

    tile = pl.BlockSpec((ATT_SUB * TQ, 128), lambda hp, st: (st, hp))
    colT = pl.BlockSpec((T, 128), lambda hp, st: (0, hp))
    colC = pl.BlockSpec((C, 128), lambda hp, st: (0, hp))
    res = pl.pallas_call(
        body, name="attn_bwd", grid=(4, n_st),
        in_specs=(_attn_specs(T, C) + [tile, tile, pl.BlockSpec((ATT_SUB * TQ, 128), lambda hp, st: (st, 4 + hp)),
                                       pl.BlockSpec((8, 128, 128), lambda hp, st: (0, 0, 0)),
                                       pl.BlockSpec((32, 256), lambda hp, st: (0, 0))]
                  + [ANY_SPEC] * n_rid),
        out_specs=([tile, colT, colT, colC, colC, tile, pl.BlockSpec((2, 32, 128), lambda hp, st: (hp, 0, 0))]
                   + [ANY_SPEC] * n_rid),
        out_shape=([jax.ShapeDtypeStruct((T, 512), F32), jax.ShapeDtypeStruct((T, 512), F32),
                    jax.ShapeDtypeStruct((T, 512), F32), jax.ShapeDtypeStruct((C, 512), F32),
                    jax.ShapeDtypeStruct((C, 512), F32), jax.ShapeDtypeStruct((T, 512), BF16),
                    jax.ShapeDtypeStruct((N_HEADS, 32, 128), F32)] + [ex.out_shape for ex, _ in riders]),
        scratch_shapes=([pltpu.VMEM((3, 2, TQ, TK), F32), pltpu.VMEM((2, N_DIAG, GRID_W, 128), F32)]
                        + [s for ex, _ in riders for s in ex.scratch]),
        compiler_params=_params(("arbitrary", "arbitrary")),
    )(qs, kn, z, ckn, zc, pairs, row_mask, z, ob, lse, dmc, hsel, fold, *[a for _, a in riders])
    return res[:7], res[7:]


def _rpb_diag_sums(a_ref, hsel_ref, fold_ref):
    n_off = 2 * WIN_C - 1
    n_dr = 2 * WIN_R - 1
    qc = lax.broadcasted_iota(jnp.int32, (GRID_W, 128), 0)
    lane = lax.broadcasted_iota(jnp.int32, (GRID_W, 128), 1)
    diff = lane % GRID_W - qc + (WIN_C - 1)
    left = lane < GRID_W

    def by_dr(dr):
        return a_ref[dr + 4] + pltpu.roll(a_ref[dr + 3], GRID_W, 1)

    out = jnp.zeros((32, 128), F32)
    for j in range((n_dr + 1) // 2):
        hi = pltpu.roll(by_dr(2 * j + 1), GRID_W, 1) if 2 * j + 1 < n_dr else 0.0
        pair = jnp.where(left, by_dr(2 * j), hi)
        parts = []
        for o in range(n_off):
            mv = jnp.where(diff == o, pair, 0.0)
            acc = mv[0:8]
            for r8 in range(1, GRID_W // 8):
                acc = acc + mv[8 * r8:8 * (r8 + 1)]
            parts.append(acc)
        parts.append(jnp.zeros((8, 128), F32))
        stack = jnp.concatenate(parts, axis=0)
        s_hi = stack.astype(BF16)
        s_lo = (stack - s_hi.astype(F32)).astype(BF16)
        per_o = (jnp.dot(fold_ref[...], s_hi, preferred_element_type=F32)
                 + jnp.dot(fold_ref[...], s_lo, preferred_element_type=F32))
        out = out + _dot2(per_o, hsel_ref[j])
    return out


def _head_norm_bwd(raw, dn, gain, ones_ref):
    rr = lax.rsqrt(_head_sum(raw * raw, ones_ref) * (1.0 / HEAD_DIM) + EPS)
    hat = raw * rr
    dgain = jnp.sum(dn * hat, axis=0, keepdims=True)
    dhat = dn * gain
    mean = _head_sum(dhat * hat, ones_ref) * (1.0 / HEAD_DIM)
    return rr * (dhat - hat * mean), dgain


def _ctx_k_bwd(zc, dck, dcv, ones_blk, gk, foldm):
    C = dck.shape[0]

    def body(bk_ref, dk_ref, dv_ref, ones_ref, gk_ref, fold_ref, dbk_ref, dbv_ref, dgk_ref):
        dbk, dgk = _head_norm_bwd(bk_ref[...].astype(F32), dk_ref[...] * LN2, gk_ref[...], ones_ref)
        dbk_ref[...] = dbk.astype(BF16)
        dbv_ref[...] = dv_ref[...].astype(BF16)
        dgk_ref[...] = jnp.dot(jnp.broadcast_to(dgk, (8, 512)), fold_ref[...],
                               preferred_element_type=F32, precision=HI)

    row = pl.BlockSpec((C, 512), lambda i: (0, 0))
    cst = lambda a, b: pl.BlockSpec((a, b), lambda i: (0, 0))
    out_row = jax.ShapeDtypeStruct((C, 512), BF16)
    return pl.pallas_call(
        body, name="ctx_k_bwd", grid=(1,),
        in_specs=[row, row, row, cst(512, 512), cst(1, 512), cst(512, 128)],
        out_specs=[row, row, cst(8, 128)],
        out_shape=[out_row, out_row, jax.ShapeDtypeStruct((8, 128), F32)],
        compiler_params=_params(("arbitrary",)),
    )(zc, dck, dcv, ones_blk, gk, foldm)


def _bwd_mid(z, dmc, dqs, dk, dv, db_g, h, hc, dzc_k, dzc_v, sgn, ws, wst, bsb, ones8, ones_blk, gq, gk, foldm, tk,
             riders=()):
    T = z.shape[0]
    nt = T // tk
    blk = D_IN // N_DEV
    n_in, n_out, n_sc = 23, 7, 9
    n_rid = len(riders)

    def body(*refs):
        (au_ref, av_ref, ag_ref, bq_ref, bk_ref, d_ref, dq_ref, dk_ref, dv_ref, dbg_ref, h_ref,
         hc_ref, dzck_ref, dzcv_ref, sg_ref, ws_ref, wst_ref, bsb_ref, ones8_ref, ones_ref, gq_ref, gk_ref,
         fold_ref) = refs[:n_in]
        rid_src = refs[n_in:n_in + n_rid]
        dz_ref, sums_out, dws_ref, dbs_ref, dsg_ref, dgq_ref, dgk_ref = refs[n_in + n_rid:n_in + n_rid + n_out]
        rid_dst = refs[n_in + n_rid + n_out:n_in + 2 * n_rid + n_out]
        (acc, accq, acck, stage, sem, send_buf, tmp, s1, r1) = refs[n_in + 2 * n_rid + n_out:
                                                                   n_in + 2 * n_rid + n_out + n_sc]
        rid_sems = refs[n_in + 2 * n_rid + n_out + n_sc:]
        t = pl.program_id(0)

        @pl.when(t == 0)
        def _():
            for r, (ex, _) in enumerate(riders):
                ex.start(rid_src[r], rid_dst[r], *rid_sems[3 * r:3 * r + 3])
            acc[...] = jnp.zeros_like(acc)
            acc[512 * 4:512 * 5, :] = _tn(dzck_ref[...], hc_ref[...])
            acc[512 * 5:512 * 6, :] = _tn(dzcv_ref[...], hc_ref[...])
            dws_ref[...] = jnp.zeros_like(dws_ref)
            dbs_ref[...] = jnp.zeros_like(dbs_ref)
            dsg_ref[...] = jnp.zeros_like(dsg_ref)
            accq[...] = jnp.zeros_like(accq)
            acck[...] = jnp.zeros_like(acck)

        for g in range(SGU_GROUPS):
            ws_bf = ws_ref[g].astype(BF16)
            wst_bf = wst_ref[g].astype(BF16)
            sg = sg_ref[:, 128 * g:128 * (g + 1)]
            bsb_g = bsb_ref[g]
            for j in range(tk // CHUNK):
                rs, cs = slice(CHUNK * j, CHUNK * (j + 1)), slice(128 * g, 128 * (g + 1))
                au, av, ag = (au_ref[rs, cs].astype(F32), av_ref[rs, cs].astype(F32), ag_ref[rs, cs].astype(F32))
                d = d_ref[rs, cs].astype(F32)
                _, (gu, dgu, dgv, rr, vhat, vn, mixed, sig, sl) = _sgu_chunk_fwd(au, av, ag, sg, ws_bf, bsb_g)
                dz_ref[rs, 128 * g:128 * (g + 1)] = (d * mixed * sl * dgu).astype(BF16)
                dz_ref[rs, 1024 + 128 * g:1024 + 128 * (g + 1)] = (
                    d * gu * mixed * (sig * (1.0 + ag * (1.0 - sig)))).astype(BF16)
                dmixed = d * gu * sl
                dmb = dmixed.astype(BF16)
                dm_lo = (dmixed - dmb.astype(F32)).astype(BF16)
                dbs_ref[g] += _nt(ones8_ref[...], dmb) + _nt(ones8_ref[...], dm_lo)
                dws_ref[g] += _nt(dmb, vn.astype(BF16))
                dvn = jnp.dot(wst_bf, dmb, preferred_element_type=F32)
                dsg_ref[:, 128 * g:128 * (g + 1)] += jnp.sum(dvn * vhat, axis=0, keepdims=True)
                dvhat = dvn * sg
                mean = jnp.mean(dvhat * vhat, axis=-1, keepdims=True)
                dz_ref[rs, 512 + 128 * g:512 + 128 * (g + 1)] = (rr * (dvhat - vhat * mean) * dgv).astype(BF16)

        dbq, dgq = _head_norm_bwd(bq_ref[...].astype(F32), dq_ref[...] * ATT_SCALE, gq_ref[...], ones_ref)
        dz_ref[:, 512 * 3:512 * 4] = dbq.astype(BF16)
        accq[...] += dgq
        dbk, dgk = _head_norm_bwd(bk_ref[...].astype(F32), dk_ref[...] * LN2, gk_ref[...], ones_ref)
        dz_ref[:, 512 * 4:512 * 5] = dbk.astype(BF16)
        acck[...] += dgk
        dz_ref[:, 512 * 5:512 * 6] = dv_ref[...].astype(BF16)
        dz_ref[:, 512 * 6:512 * 7] = dbg_ref[...]

        hv = h_ref[...]
        for k in range(N_BRANCH):
            acc[512 * k:512 * (k + 1), :] += _tn(dz_ref[:, 512 * k:512 * (k + 1)], hv)

        @pl.when(t == nt - 1)
        def _():
            dgq_ref[...] = jnp.dot(jnp.broadcast_to(accq[...], (8, 512)), fold_ref[...],
                                   preferred_element_type=F32, precision=HI)
            dgk_ref[...] = jnp.dot(jnp.broadcast_to(acck[...], (8, 512)), fold_ref[...],
                                   preferred_element_type=F32, precision=HI)
            cidx = lax.axis_index("c")
            sib = (lax.axis_index("x"), lax.axis_index("y"), 1 - cidx)
            swaps = []
            for q in range(N_DEV // 2):
                theirs = acc[pl.ds(pl.multiple_of(2 * blk * q + blk * (1 - cidx), 8), blk), :]
                send_buf[q] = theirs.astype(BF16)
                cp = pltpu.make_async_remote_copy(src_ref=send_buf.at[q], dst_ref=tmp.at[q], send_sem=s1.at[q],
                                                  recv_sem=r1.at[q], device_id=sib, device_id_type=MESH)
                cp.start()
                swaps.append(cp)
            for q in range(N_DEV // 2):
                swaps[q].wait_recv()
                mine = acc[pl.ds(pl.multiple_of(2 * blk * q + blk * cidx, 8), blk), :]
                stage[...] = (mine + tmp[q].astype(F32)).astype(BF16)
                out = pltpu.make_async_copy(stage, sums_out.at[q], sem)
                out.start()
                out.wait()
            for cp in swaps:
                cp.wait_send()
            for r, (ex, _) in enumerate(riders):
                ex.wait(rid_src[r], rid_dst[r], *rid_sems[3 * r:3 * r + 3])

    zcol = lambda col: pl.BlockSpec((tk, 512), lambda t: (t, col))
    row = pl.BlockSpec((tk, 512), lambda t: (t, 0))
    whole = lambda a: pl.BlockSpec(a.shape, lambda t: (0,) * a.ndim)
    res = pl.pallas_call(
        body, name="bwd_mid", grid=(nt,),
        in_specs=[zcol(0), zcol(1), zcol(2), zcol(3), zcol(4), row, row, row, row, row,
                  pl.BlockSpec((tk, D_MODEL), lambda t: (t, 0)), whole(hc), whole(dzc_k), whole(dzc_v),
                  whole(sgn), whole(ws), whole(wst), whole(bsb), whole(ones8), whole(ones_blk), whole(gq), whole(gk),
                  whole(foldm)] + [ANY_SPEC] * n_rid,
        out_specs=[pl.BlockSpec((tk, D_IN), lambda t: (t, 0)), ANY_SPEC,
                   pl.BlockSpec((SGU_GROUPS, CHUNK, CHUNK), lambda t: (0, 0, 0)),
                   pl.BlockSpec((SGU_GROUPS, 8, CHUNK), lambda t: (0, 0, 0)),
                   pl.BlockSpec((1, 512), lambda t: (0, 0)), pl.BlockSpec((8, 128), lambda t: (0, 0)),
                   pl.BlockSpec((8, 128), lambda t: (0, 0))] + [ANY_SPEC] * n_rid,
        out_shape=[jax.ShapeDtypeStruct((T, D_IN), BF16), jax.ShapeDtypeStruct((N_DEV // 2, blk, D_MODEL), BF16),
                   jax.ShapeDtypeStruct((SGU_GROUPS, CHUNK, CHUNK), F32),
                   jax.ShapeDtypeStruct((SGU_GROUPS, 8, CHUNK), F32), jax.ShapeDtypeStruct((1, 512), F32),
                   jax.ShapeDtypeStruct((8, 128), F32), jax.ShapeDtypeStruct((8, 128), F32)]
                  + [ex.out_shape for ex, _ in riders],
        scratch_shapes=[pltpu.VMEM((D_IN, D_MODEL), F32), pltpu.VMEM((1, 512), F32), pltpu.VMEM((1, 512), F32),
                        pltpu.VMEM((blk, D_MODEL), BF16), pltpu.SemaphoreType.DMA,
                        pltpu.VMEM((N_DEV // 2, blk, D_MODEL), BF16), pltpu.VMEM((N_DEV // 2, blk, D_MODEL), BF16),
                        pltpu.SemaphoreType.DMA((N_DEV // 2,)), pltpu.SemaphoreType.DMA((N_DEV // 2,))]
                       + [s for ex, _ in riders for s in ex.scratch],
        compiler_params=_params(("arbitrary",)),
    )(z, z, z, z, z, dmc, dqs, dk, dv, db_g, h, hc, dzc_k, dzc_v, sgn, ws, wst, bsb, ones8, ones_blk, gq, gk, foldm,
      *[a for _, a in riders])
    return res[:n_out], res[n_out:]


def _inproj_bwd_dx(dzs, w_row0, w_in_t, x, dy, ng, scale, shift, tm, name, riders=()):
    T = x.shape[0]
    n = len(dzs)
    wpc = dzs[0].shape[1]
    nt = T // tm
    with_dx = dy is not None
    n_own_in = n + 5 + with_dx
    n_own_out = 3 + with_dx
    n_rid = len(riders)

    def body(*refs):
        dz_refs = refs[:n]
        n_in = n_own_in + n_rid
        own = refs[n:n_own_in] + refs[n_in:n_in + n_own_out]
        rid_src = refs[n_own_in:n_own_in + n_rid]
        rid_dst = refs[n_in + n_own_out:n_in + n_own_out + n_rid]
        rid_sems = refs[n_in + n_own_out + n_rid:]
        if with_dx:
            w_ref, x_ref, dy_ref, g_ref, sc_ref, sh_ref, gx_ref, dsh_ref, dsc_ref, dg_ref = own
        else:
            w_ref, x_ref, g_ref, sc_ref, sh_ref, dsh_ref, dsc_ref, dg_ref = own

        @pl.when(pl.program_id(0) == 0)
        def _():
            for r, (ex, _) in enumerate(riders):
                ex.start(rid_src[r], rid_dst[r], *rid_sems[3 * r:3 * r + 3])
            dsh_ref[...] = jnp.zeros_like(dsh_ref)
            dsc_ref[...] = jnp.zeros_like(dsc_ref)
            dg_ref[...] = jnp.zeros_like(dg_ref)

        dh = jnp.dot(dz_refs[0][...], w_ref[0:wpc, :], preferred_element_type=F32)
        for k in range(1, n):
            dh = dh + jnp.dot(dz_refs[k][...], w_ref[wpc * k:wpc * (k + 1), :], preferred_element_type=F32)
        xv = x_ref[...]
        r = lax.rsqrt(jnp.mean(xv * xv, axis=-1, keepdims=True) + EPS)
        xn = xv * r
        gv, op = g_ref[...], 1.0 + sc_ref[...]
        dsh_ref[...] += jnp.sum(dh, axis=0, keepdims=True)
        dsc_ref[...] += jnp.sum(dh * xn * gv, axis=0, keepdims=True)
        dg_ref[...] += jnp.sum(dh * op * xn, axis=0, keepdims=True)
        if with_dx:
            dxn = dh * (gv * op)
            gx_ref[...] = r * (dxn - xn * jnp.mean(dxn * xn, axis=-1, keepdims=True)) + dy_ref[...]

        @pl.when(pl.program_id(0) == nt - 1)
        def _():
            for r, (ex, _) in enumerate(riders):
                ex.wait(rid_src[r], rid_dst[r], *rid_sems[3 * r:3 * r + 3])

    vec = pl.BlockSpec((1, D_MODEL), lambda i: (0, 0))
    rowf = pl.BlockSpec((tm, D_MODEL), lambda i: (i, 0))
    in_specs = [pl.BlockSpec((tm, wpc), lambda i: (i, 0))] * n
    in_specs += [pl.BlockSpec((wpc * n, D_MODEL), lambda i: (w_row0 // (wpc * n), 0)), rowf]
    args = list(dzs) + [w_in_t, x]
    vshape = jax.ShapeDtypeStruct((1, D_MODEL), F32)
    out_specs, out_shape = [vec, vec, vec], [vshape, vshape, vshape]
    if with_dx:
        in_specs.append(rowf)
        args.append(dy)
        out_specs = [rowf] + out_specs
        out_shape = [jax.ShapeDtypeStruct((T, D_MODEL), F32)] + out_shape
    in_specs += [vec, vec, vec] + [ANY_SPEC] * n_rid
    args += [ng, scale, shift] + [a for _, a in riders]
    res = pl.pallas_call(
        body, name=name, grid=(nt,), in_specs=in_specs, out_specs=out_specs + [ANY_SPEC] * n_rid,
        out_shape=out_shape + [ex.out_shape for ex, _ in riders],
        scratch_shapes=[s for ex, _ in riders for s in ex.scratch],
        compiler_params=_params(("arbitrary",)),
    )(*args)
    return res[:n_own_out], res[n_own_out:]


def _adamw_sharded(w, gparts, m, v, tr, name, riders=()):
    R, C = w.shape
    n_part = gparts.shape[0]
    nt = R // tr
    n_rid = len(riders)

    def body(w_ref, gp_ref, m_ref, v_ref, *rest):
        rid_src = rest[:n_rid]
        g_ref, d_ref, m2_ref, v2_ref = rest[n_rid:n_rid + 4]
        rid_dst = rest[n_rid + 4:2 * n_rid + 4]
        rid_sems = rest[2 * n_rid + 4:]

        @pl.when(pl.program_id(0) == 0)
        def _():
            for r, (ex, _) in enumerate(riders):
                ex.start(rid_src[r], rid_dst[r], *rid_sems[3 * r:3 * r + 3])

        g = gp_ref[0].astype(F32)
        for d in range(1, n_part):
            g = g + gp_ref[d].astype(F32)
        delta, m2, v2 = _adam(w_ref[...], g, m_ref[...], v_ref[...])
        g_ref[...] = g
        d_ref[...] = delta
        m2_ref[...] = m2
        v2_ref[...] = v2

        @pl.when(pl.program_id(0) == nt - 1)
        def _():
            for r, (ex, _) in enumerate(riders):
                ex.wait(rid_src[r], rid_dst[r], *rid_sems[3 * r:3 * r + 3])

    row = pl.BlockSpec((tr, C), lambda i: (i, 0))
    sh = jax.ShapeDtypeStruct((R, C), F32)
    res = pl.pallas_call(
        body, name=name, grid=(nt,),
        in_specs=[row, pl.BlockSpec((n_part, tr, C), lambda i: (0, i, 0)), row, row] + [ANY_SPEC] * n_rid,
        out_specs=[row, row, row, row] + [ANY_SPEC] * n_rid,
        out_shape=[sh, sh, sh, sh] + [ex.out_shape for ex, _ in riders],
        scratch_shapes=[s for ex, _ in riders for s in ex.scratch],
        compiler_params=_params(("arbitrary",)),
    )(w, gparts, m, v, *[a for _, a in riders])
    return res[:4], res[4:]


def _pack_vectors(vec_rows, dsg, dgq, dgk, dgk_c, loss_part):
    flat = [a for _, arrs in vec_rows for a in arrs]

    def body(*refs):
        vecs = list(refs[:len(flat)])
        dsg_ref, dgq_ref, dgk_ref, dgkc_ref, loss_ref, v_ref = refs[len(flat):]
        row = lax.broadcasted_iota(jnp.int32, (16, D_MODEL), 0)
        misc = jnp.concatenate([dsg_ref[...], dgq_ref[0:1, :], dgk_ref[0:1, :], dgkc_ref[0:1, :],
                                loss_ref[...]], axis=1)
        v = jnp.where(row == V_MISC, jnp.broadcast_to(misc, (16, D_MODEL)), 0.0)
        for r, arrs in vec_rows:
            val = vecs.pop(0)[...]
            for _ in arrs[1:]:
                val = val + vecs.pop(0)[...]
            v = jnp.where(row == r, jnp.broadcast_to(val, (16, D_MODEL)), v)
        v_ref[...] = v

    return pl.pallas_call(
        body, name="pack_vectors", out_shape=jax.ShapeDtypeStruct((16, D_MODEL), F32), compiler_params=_params(),
    )(*flat, dsg, dgq, dgk, dgk_c, loss_part)


SMALL_NAMES = ("b_ada", "norm_g", "sgu_norm_g", "w_spatial", "b_spatial", "q_norm_g", "k_norm_g", "rpb")


def _adamw_small(vg, sg, rg, ws, ms, vs):
    k = len(SMALL_NAMES)

    def body(*refs):
        vg_ref, sg_ref, rg_ref = refs[0], refs[1], refs[2]
        refs = refs[1:]
        w_refs = dict(zip(SMALL_NAMES, refs[2:2 + k]))
        m_refs = dict(zip(SMALL_NAMES, refs[2 + k:2 + 2 * k]))
        v_refs = dict(zip(SMALL_NAMES, refs[2 + 2 * k:2 + 3 * k]))
        o_refs = [dict(zip(SMALL_NAMES, refs[2 + (3 + i) * k:2 + (4 + i) * k])) for i in range(4)]
        loss_ref = refs[2 + 7 * k]

        sv = vg_ref[0]
        for d in range(1, N_DEV):
            sv = sv + vg_ref[d]
        loss_ref[...] = sv[V_MISC:V_MISC + 1, 896:1024]

        def total(lo, hi, ref=sg_ref):
            s = ref[0, lo:hi, :].astype(F32)
            for d in range(1, N_DEV):
                s = s + ref[d, lo:hi, :].astype(F32)
            return s

        def emit(name, idx, g):
            res = _adam(w_refs[name][idx], g, m_refs[name][idx], v_refs[name][idx])
            for o, val in zip(o_refs, (g,) + res):
                o[name][idx] = val

        everything = (slice(None), slice(None))
        row = lambda r: sv[r:r + 1, :]
        emit("b_ada", everything, jnp.concatenate(
            [row(V_DSHIFT) + row(V_DCSHIFT), row(V_DSCALE) + row(V_DCSCALE), row(V_DGATE)], axis=1))
        emit("norm_g", everything, row(V_DNG) + row(V_DNG_CTX))
        misc = row(V_MISC)
        emit("sgu_norm_g", everything, misc[:, 0:512])
        emit("q_norm_g", everything, misc[:, 512:512 + HEAD_DIM])
        emit("k_norm_g", everything, misc[:, 640:640 + HEAD_DIM] + misc[:, 768:768 + HEAD_DIM])
        for g in range(SGU_GROUPS):
            emit("w_spatial", (0, g), total(128 * g, 128 * (g + 1)))
            emit("b_spatial", (0, slice(g, g + 1), slice(None)), total(M_DBS + 8 * g, M_DBS + 8 * (g + 1))[0:1, :])
        for hd in range(N_HEADS):
            by_dc = total(32 * hd, 32 * (hd + 1), rg_ref)
            emit("rpb", (0, hd), by_dc.T[0:2 * WIN_R - 1, 0:2 * WIN_C - 1])

    shapes = [jax.ShapeDtypeStruct(w.shape, F32) for w in ws]
    res = pl.pallas_call(body, name="adamw_small", out_shape=shapes * 4 + [jax.ShapeDtypeStruct((1, 128), F32)],
                         compiler_params=_params())(vg, sg, rg, *ws, *ms, *vs)
    return [res[i * k:(i + 1) * k] for i in range(4)], res[4 * k]


def _adamw_cctx(pc_g, w, m, v):
    def body(pc_ref, w_ref, m_ref, v_ref, g_ref, d_ref, m2_ref, v2_ref):
        pc = pc_ref[0, 0:1, :]
        for d in range(1, N_DEV):
            pc = pc + pc_ref[d, 0:1, :]
        cc = w_ref[...]
        sig = _sigmoid(cc)
        g = pc * (sig * (1.0 + cc * (1.0 - sig)))
        delta, m2, v2 = _adam(cc, g, m_ref[...], v_ref[...])
        g_ref[...] = g
        d_ref[...] = delta
        m2_ref[...] = m2
        v2_ref[...] = v2

    sh = jax.ShapeDtypeStruct((1, D_MODEL), F32)
    return pl.pallas_call(body, name="adamw_cctx", out_shape=[sh, sh, sh, sh], compiler_params=_params())(
        pc_g, w, m, v)


def _block_ones(n, blk):
    i = np.arange(n)
    return jnp.asarray((i[:, None] // blk == i[None, :] // blk).astype(np.float32), BF16)


def _rpb_pairs(rpb):
    n_off = 2 * WIN_C - 1
    cols = np.arange(GRID_W)
    c0 = np.clip(cols - WIN_C // 2, 0, GRID_W - WIN_C)
    in_win = (cols[None, :] >= c0[:, None]) & (cols[None, :] < c0[:, None] + WIN_C)
    dc = np.clip(cols[None, :] - cols[:, None] + (WIN_C - 1), 0, n_off - 1)
    expand = (dc[None] == np.arange(n_off)[:, None, None]) & in_win[None]
    toep = jnp.einsum("hrd,dqk->hrqk", rpb, jnp.asarray(expand, F32), precision=HI)
    toep = toep + jnp.asarray(np.where(in_win, 0.0, NEG_INF).astype(np.float32))
    neg = jnp.full((N_HEADS, 1, GRID_W, GRID_W), NEG_INF, F32)
    ext = jnp.concatenate([neg, toep, neg], axis=1)
    return jnp.concatenate([ext[:, :-1], ext[:, 1:]], axis=-1)


def _row_mask(rows):
    nrb = rows // Q_ROWS
    valid = np.zeros((3, Q_ROWS, 1, K_ROWS, 1), bool)
    for t, rb in enumerate((0, 1, nrb - 1)):
        kb = int(np.clip(Q_ROWS * rb - 4, 0, rows - K_ROWS))
        for i in range(Q_ROWS):
            r0 = int(np.clip(Q_ROWS * rb + i - WIN_R // 2, 0, rows - WIN_R))
            for j in range(K_ROWS):
                valid[t, i, 0, j, 0] = r0 <= kb + j < r0 + WIN_R
    full = np.broadcast_to(valid, (3, Q_ROWS, GRID_W, K_ROWS, GRID_W)).reshape(3, TQ, TK)
    return jnp.asarray(np.where(full, 0.0, NEG_INF).astype(np.float32))


def kernel(x, c, ctx, c_ctx, w_ada, b_ada, norm_g, w_in, sgu_norm_g, w_spatial, b_spatial, q_norm_g, k_norm_g, rpb, w_out, loss_target, m_c_ctx, m_w_ada, m_b_ada, m_norm_g, m_w_in, m_sgu_norm_g, m_w_spatial, m_b_spatial, m_q_norm_g, m_k_norm_g, m_rpb, m_w_out, v_c_ctx, v_w_ada, v_b_ada, v_norm_g, v_w_in, v_sgu_norm_g, v_w_spatial, v_b_spatial, v_q_norm_g, v_k_norm_g, v_rpb, v_w_out):
    me = 4 * lax.axis_index("x") + 2 * lax.axis_index("y") + lax.axis_index("c")
    x2, ctx2, tgt2 = x[0], ctx[0], loss_target[0]
    T, C = x2.shape[0], ctx2.shape[0]
    rows = T // GRID_W
    wada, win_t, wout = w_ada[0], w_in[0].T, w_out[0]
    ada_w = wada.shape[1]
    win_w = win_t.shape[0]

    row8 = lax.broadcasted_iota(jnp.int32, (8, D_MODEL), 0)
    c_blk = jnp.where(row8 == me, jnp.broadcast_to(c, (8, D_MODEL)), 0.0)
    b_sh = lax.dynamic_slice(b_ada, (0, me * ada_w), (1, ada_w))
    c_ctx_row = c_ctx.reshape(1, D_MODEL)

    ones512 = _block_ones(512, HEAD_DIM)
    ones8 = jnp.ones((8, 128), BF16)
    foldm = jnp.asarray((np.arange(512)[:, None] % HEAD_DIM == np.arange(128)[None, :]).astype(np.float32))
    lane_half = np.arange(128)[None, :, None] // GRID_W
    hsel = jnp.asarray((2 * np.arange(8)[:, None, None] + lane_half == np.arange(128)[None, None, :]).astype(np.float32),
                       BF16)
    foldr = jnp.asarray((np.arange(256)[None, :] // 8 == np.arange(32)[:, None]).astype(np.float32), BF16)
    gq512 = jnp.tile(q_norm_g, (1, N_HEADS))
    gk512 = jnp.tile(k_norm_g, (1, N_HEADS))
    ws = w_spatial[0]
    wst = ws.transpose(0, 2, 1)
    bsb = jnp.broadcast_to(b_spatial[0][:, :, None], (SGU_GROUPS, CHUNK, 128))
    pairs = _rpb_pairs(rpb[0])
    row_mask = _row_mask(rows)

    my_chip = me // 2
    order = jnp.stack([my_chip, my_chip ^ 2, my_chip ^ 1, my_chip ^ 3]).astype(jnp.int32)
    h, z, win_g, s16, part_g = _inproj_fwd(order, x2, norm_g, c_blk, c_ctx_row, wada, b_sh, win_t.astype(BF16), 512)
    w_in_b = win_g.reshape(D_IN, D_MODEL)
    mod16 = part_g.transpose(1, 0, 2).reshape(16, 3 * D_MODEL)
    mod = lax.dynamic_slice(mod16, (me, 0), (1, 3 * D_MODEL))
    shift, scale, gate = mod[:, :D_MODEL], mod[:, D_MODEL:2 * D_MODEL], mod[:, 2 * D_MODEL:]
    cshift, cscale = mod16[8:9, :D_MODEL], mod16[8:9, D_MODEL:2 * D_MODEL]
    out_a, qs, kn = _sgu_qk_fwd(z, sgu_norm_g, ws, bsb, ones512, gq512, gk512, 512)
    hc, zc, ckn = _ctx_fwd(ctx2, norm_g, cscale, cshift, w_in_b, ones512, gk512)
    wout_blk = wout.astype(BF16)
    (ob, out_b, lse), (wout_g,) = _attn_fwd(qs, kn, z, ckn, zc, pairs, row_mask, [(_Hosted("ag", wout_blk), wout_blk)])
    w_out_b = wout_g.reshape(D_MODEL, D_MODEL)

    dy, dmc, dw_out, dgate, loss_part = _outproj_loss_bwd(x2, tgt2, out_a, out_b, gate, w_out_b, 512)
    dw_out_blocks = dw_out.reshape(N_DEV, D_MODEL // N_DEV, D_MODEL)
    (dqs, dk, dv, dck, dcv, db_g, drpb), (gout_parts,) = _attn_bwd(
        qs, kn, z, ckn, zc, pairs, row_mask, ob, lse, dmc, hsel, foldr,
        [(_Hosted("a2a", dw_out_blocks), dw_out_blocks)])
    dzc_k, dzc_v, dgk_c = _ctx_k_bwd(zc, dck, dcv, ones512, gk512, foldm)
    rloc = drpb.reshape(N_HEADS * 32, 128)
    (dz, chip_sums, dws, dbs, dsg, dgq, dgk), (rg,) = _bwd_mid(
        z, dmc, dqs, dk, dv, db_g, h, hc, dzc_k, dzc_v, sgu_norm_g, ws, wst, bsb, ones8, ones512, gq512, gk512, foldm,
        512, [(_Hosted("ag", rloc), rloc)])
    sloc = jnp.concatenate([dws.reshape(SGU_GROUPS * CHUNK, CHUNK), dbs.reshape(SGU_GROUPS * 8, CHUNK)]).astype(BF16)
    (grad_x, dshift, dscale, dng), (gin_parts,) = _inproj_bwd_dx(
        [dz], 0, w_in_b, x2, dy, norm_g, scale, shift, 512, "inproj_bwd_dx",
        [(_Hosted("chips", chip_sums), chip_sums)])
    (dcshift, dcscale, dng_c), _ = _inproj_bwd_dx([dzc_k, dzc_v], 4 * 512, w_in_b, ctx2, None, norm_g, cscale,
                                                  cshift, C, "ctx_bwd_dx")

    res_in, _ = _adamw_sharded(win_t, gin_parts, m_w_in[0].T, v_w_in[0].T, 112, "adamw_w_in")
    res_out, _ = _adamw_sharded(wout, gout_parts, m_w_out[0], v_w_out[0], 128, "adamw_w_out")

    zero_row = jnp.zeros((1, D_MODEL), F32)
    vec_rows = [(V_DSHIFT, [dshift]), (V_DSCALE, [dscale]), (V_DGATE, [dgate]), (V_DCSHIFT, [dcshift]),
                (V_DCSCALE, [dcscale]), (V_ZERO, [zero_row]), (V_DNG, [dng]), (V_DNG_CTX, [dng_c])]
    vloc = _pack_vectors(vec_rows, dsg, dgq, dgk, dgk_c, loss_part)
    vg, sg = _allgather_direct([vloc, sloc], "gather_small")
    small_w =(b_ada, norm_g, sgu_norm_g, w_spatial, b_spatial, q_norm_g, k_norm_g, rpb)
    small_m = (m_b_ada, m_norm_g, m_sgu_norm_g, m_w_spatial, m_b_spatial, m_q_norm_g, m_k_norm_g, m_rpb)
    small_v = (v_b_ada, v_norm_g, v_sgu_norm_g, v_w_spatial, v_b_spatial, v_q_norm_g, v_k_norm_g, v_rpb)
    res_small, loss_row = _adamw_small(vg, sg, rg, small_w, small_m, small_v)

    dm_all = vg[:, V_DSHIFT:V_DGATE + 1, :].reshape(N_DEV, 3 * D_MODEL)
    dc_all = vg[:, V_DCSHIFT:V_ZERO + 1, :].reshape(N_DEV, 3 * D_MODEL)
    dm_sh = lax.dynamic_slice(dm_all, (0, me * ada_w), (N_DEV, ada_w))
    dc_sh = lax.dynamic_slice(dc_all, (0, me * ada_w), (N_DEV, ada_w))
    *res_ada, pc = _ada_bwd(s16, dm_sh, dc_sh, wada, m_w_ada[0], v_w_ada[0])
    (pc_g,) = _allgather_direct([pc], "gather_cctx")
    res_cctx = _adamw_cctx(pc_g, c_ctx_row, m_c_ctx.reshape(1, D_MODEL), v_c_ctx.reshape(1, D_MODEL))

    loss = loss_row[0, 0]
    outs = [loss, grad_x[None]]
    for kind in range(4):
        by_name = dict(zip(SMALL_NAMES, res_small[kind]))
        by_name.update(c_ctx=res_cctx[kind].reshape(D_MODEL), w_ada=res_ada[kind][None],
                       w_in=res_in[kind].T[None], w_out=res_out[kind][None])
        outs += [by_name[nme] for nme in ("c_ctx", "w_ada", "b_ada", "norm_g", "w_in", "sgu_norm_g", "w_spatial",
                                          "b_spatial", "q_norm_g", "k_norm_g", "rpb", "w_out")]
    return tuple(outs)
```

```python
import functools

import numpy as np
import jax
import jax.numpy as jnp
from jax import lax
from jax.experimental import pallas as pl
from jax.experimental.pallas import tpu as pltpu

F32 = jnp.float32
BF16 = jnp.bfloat16
HI = lax.Precision.HIGHEST

N_DEV = 8
D_MODEL = 1024
D_A = 512
D_B = 512
D_IN = 3584
N_BRANCH = 7
HEAD_DIM = 64
N_HEADS = 8
GRID_W = 64
WIN_R = 8
WIN_C = 16
CHUNK = 128
SGU_GROUPS = 4
EPS = 1e-6
NEG_INF = -1e30
Q_ROWS = 4
K_ROWS = 12
TQ = Q_ROWS * GRID_W
TK = K_ROWS * GRID_W
N_DIAG = 22
ATT_SUB = 8
ATT_SCALE = HEAD_DIM ** -0.5
LOG2E = 1.4426950408889634
LN2 = 0.6931471805599453

ADAM_LR = 0.001
ADAM_B1 = 0.9
ADAM_B2 = 0.999
ADAM_EPS = 1e-08
ADAM_WD = 0.01
ADAM_STEP = 10

VMEM_LIMIT = 56 * 1024 * 1024
MESH = pl.DeviceIdType.MESH

V_DSHIFT, V_DSCALE, V_DGATE, V_DCSHIFT, V_DCSCALE, V_ZERO, V_DNG, V_DNG_CTX, V_MISC = range(9)
M_DBS = 512


def _params(sem=None):
    return pltpu.CompilerParams(dimension_semantics=sem, vmem_limit_bytes=VMEM_LIMIT)


def _sigmoid(x):
    return 1.0 / (1.0 + jnp.exp(-x))


def _gelu_parts(x):
    cdf = 0.5 * (1.0 + lax.erf(x * 0.7071067811865476))
    pdf = jnp.exp(-0.5 * x * x) * 0.3989422804014327
    return x * cdf, cdf + x * pdf


def _nt(a, b):
    return lax.dot_general(a, b, (((1,), (1,)), ((), ())), preferred_element_type=F32)


def _tn(a, b):
    return lax.dot_general(a, b, (((0,), (0,)), ((), ())), preferred_element_type=F32)


def _dot2(v, ones_bf):
    hi = v.astype(BF16)
    lo = (v - hi.astype(F32)).astype(BF16)
    return (jnp.dot(hi, ones_bf, preferred_element_type=F32)
            + jnp.dot(lo, ones_bf, preferred_element_type=F32))


def _head_sum(v, ones_ref):
    return jnp.dot(v.astype(BF16), ones_ref[...], preferred_element_type=F32)


def _adam(w, g, m, v):
    m2 = ADAM_B1 * m + (1.0 - ADAM_B1) * g
    v2 = ADAM_B2 * v + (1.0 - ADAM_B2) * (g * g)
    m_hat = m2 / (1.0 - ADAM_B1 ** ADAM_STEP)
    v_hat = v2 / (1.0 - ADAM_B2 ** ADAM_STEP)
    delta = -ADAM_LR * (m_hat / (jnp.sqrt(v_hat) + ADAM_EPS) + ADAM_WD * w)
    return delta, m2, v2


class _Hosted:
    def __init__(self, kind, src):
        self.kind = kind
        n_slot = {"a2a": N_DEV, "ag": N_DEV, "chips": N_DEV // 2}[kind]
        blk = src.shape if kind == "ag" else src.shape[1:]
        self.out_shape = jax.ShapeDtypeStruct((n_slot,) + tuple(blk), src.dtype)
        self.n_peer = n_slot - 1
        self.scratch = [pltpu.SemaphoreType.DMA((self.n_peer,)), pltpu.SemaphoreType.DMA((self.n_peer,)),
                        pltpu.SemaphoreType.DMA]

    def _copies(self, src, dst, send_sems, recv_sems, loc_sem, landing):
        x, y, c = lax.axis_index("x"), lax.axis_index("y"), lax.axis_index("c")
        if self.kind == "chips":
            me = 2 * x + y
            peers = [((px, py, c), 2 * px + py) for px, py in ((1 - x, y), (x, 1 - y), (1 - x, 1 - y))]
        else:
            me = 4 * x + 2 * y + c
            peers = []
            for k in range(1, N_DEV):
                px = 1 - x if (k >> 2) & 1 else x
                py = 1 - y if (k >> 1) & 1 else y
                pc = 1 - c if k & 1 else c
                peers.append(((px, py, pc), 4 * px + 2 * py + pc))
        remote = []
        for k, (peer, pid) in enumerate(peers):
            s = src if self.kind == "ag" else src.at[pid]
            remote.append(pltpu.make_async_remote_copy(
                src_ref=s, dst_ref=dst.at[pid if landing else me],
                send_sem=send_sems.at[k], recv_sem=recv_sems.at[k], device_id=peer, device_id_type=MESH))
        local = pltpu.make_async_copy(src if self.kind == "ag" else src.at[me], dst.at[me], loc_sem)
        return remote, local

    def start(self, src, dst, send_sems, recv_sems, loc_sem):
        remote, local = self._copies(src, dst, send_sems, recv_sems, loc_sem, landing=False)
        for cp in remote:
            cp.start()
        local.start()

    def wait(self, src, dst, send_sems, recv_sems, loc_sem):
        remote, local = self._copies(src, dst, send_sems, recv_sems, loc_sem, landing=True)
        for cp in remote:
            cp.wait_recv()
        for cp in remote:
            cp.wait_send()
        local.wait()


ANY_SPEC = pl.BlockSpec(memory_space=pl.ANY)


def _allgather_direct(arrs, name):
    n = len(arrs)
    exs = [_Hosted("ag", a) for a in arrs]

    def body(*refs):
        srcs, dsts, sems = refs[:n], refs[n:2 * n], refs[2 * n:]
        for r, ex in enumerate(exs):
            ex.start(srcs[r], dsts[r], *sems[3 * r:3 * r + 3])
        for r, ex in enumerate(exs):
            ex.wait(srcs[r], dsts[r], *sems[3 * r:3 * r + 3])

    return pl.pallas_call(body, name=name, out_shape=[ex.out_shape for ex in exs], in_specs=[ANY_SPEC] * n,
                          out_specs=[ANY_SPEC] * n, scratch_shapes=[s for ex in exs for s in ex.scratch])(*arrs)


def _ada_scratch(n_col):
    return ([pltpu.VMEM((N_DEV, 8, D_MODEL), F32), pltpu.VMEM((16, n_col), F32), pltpu.VMEM((N_DEV, 16, n_col), F32)]
            + [pltpu.SemaphoreType.DMA((N_DEV - 1,)) for _ in range(4)])


def _ada_modulation(cb_ref, cc_ref, w_ref, b_ref, cstack, part, parts, s1, r1, s2, r2):
    x, y, c = lax.axis_index("x"), lax.axis_index("y"), lax.axis_index("c")
    me = 4 * x + 2 * y + c
    peers = []
    for k in range(1, N_DEV):
        px = 1 - x if (k >> 2) & 1 else x
        py = 1 - y if (k >> 1) & 1 else y
        pc = 1 - c if k & 1 else c
        peers.append(((px, py, pc), 4 * px + 2 * py + pc))

    def exchange(src, dst, send_sems, recv_sems):
        for k, (peer, _) in enumerate(peers):
            pltpu.make_async_remote_copy(src_ref=src, dst_ref=dst.at[me], send_sem=send_sems.at[k],
                                         recv_sem=recv_sems.at[k], device_id=peer, device_id_type=MESH).start()
        dst[me] = src[...]
        waits = [pltpu.make_async_remote_copy(src_ref=src, dst_ref=dst.at[pid], send_sem=send_sems.at[k],
                                              recv_sem=recv_sems.at[k], device_id=peer, device_id_type=MESH)
                 for k, (peer, pid) in enumerate(peers)]
        for cp in waits:
            cp.wait_recv()
        for cp in waits:
            cp.wait_send()

    exchange(cb_ref, cstack, s1, r1)
    c_all = cstack[0]
    for d in range(1, N_DEV):
        c_all = c_all + cstack[d]
    row = lax.broadcasted_iota(jnp.int32, (8, D_MODEL), 0)
    cc = jnp.where(row == 0, jnp.broadcast_to(cc_ref[...], (8, D_MODEL)), 0.0)
    call = jnp.concatenate([c_all, cc], axis=0)
    s = call * _sigmoid(call)
    part[...] = jnp.dot(s, w_ref[...], preferred_element_type=F32, precision=HI) + b_ref[...]
    exchange(part, parts, s2, r2)
    return s


def _ada_bwd(s16, dm, dc, w, m, v):
    def body(s_ref, dm_ref, dc_ref, w_ref, m_ref, v_ref, g_ref, d_ref, m2_ref, v2_ref, pc_ref):
        dct = jnp.sum(dc_ref[...], axis=0, keepdims=True)
        row = lax.broadcasted_iota(jnp.int32, dc_ref.shape, 0)
        dcb = jnp.where(row == 0, jnp.broadcast_to(dct, dc_ref.shape), 0.0)
        dm16 = jnp.concatenate([dm_ref[...], dcb], axis=0)
        g = lax.dot_general(s_ref[...], dm16, (((0,), (0,)), ((), ())),
                            preferred_element_type=F32, precision=HI)
        w_ = w_ref[...]
        delta, m2, v2 = _adam(w_, g, m_ref[...], v_ref[...])
        g_ref[...] = g
        d_ref[...] = delta
        m2_ref[...] = m2
        v2_ref[...] = v2
        pc_ref[...] = lax.dot_general(dcb, w_, (((1,), (1,)), ((), ())),
                                      preferred_element_type=F32, precision=HI)

    sh = jax.ShapeDtypeStruct(w.shape, F32)
    return pl.pallas_call(
        body, name="ada_bwd",
        out_shape=[sh, sh, sh, sh, jax.ShapeDtypeStruct((8, D_MODEL), F32)],
        compiler_params=_params(),
    )(s16, dm, dc, w, m, v)


def _head_norm(zk, ones_ref, gain):
    ss = _head_sum(zk * zk, ones_ref)
    return zk * lax.rsqrt(ss * (1.0 / HEAD_DIM) + EPS) * gain


def _inproj_fwd(order, x, ng, c_blk, c_ctx_row, w_ada_sh, b_ada_sh, w_blk_t, tm):
    T = x.shape[0]
    nt = T // tm
    n_pass = N_DEV // 2
    blk_rows = w_blk_t.shape[0]
    n_col = w_ada_sh.shape[1]

    def body(order_ref, x_ref, g_ref, cb_ref, cc_ref, wa_ref, ba_ref, wb_ref,
             h_out, z_ref, wt_out, s_out, parts_out,
             hs, wt, modv, send_sems, recv_sems, loc_sem, h_sem, wt_sem, *ada_sc):
        p, i = pl.program_id(0), pl.program_id(1)
        x, y, c = lax.axis_index("x"), lax.axis_index("y"), lax.axis_index("c")
        me, sib = (x, y, c), (x, y, 1 - c)
        chips = [(1 - x, y), (x, 1 - y), (1 - x, 1 - y)]

        def slot(px, py, pc):
            return 4 * px + 2 * py + pc

        def copy(k, block, to, src=None):
            return pltpu.make_async_remote_copy(
                src_ref=wt.at[slot(*block)] if src is None else src, dst_ref=wt.at[slot(*block)],
                send_sem=send_sems.at[k], recv_sem=recv_sems.at[k], device_id=to, device_id_type=MESH)

        own = pltpu.make_async_copy(wb_ref, wt.at[slot(*me)], loc_sem)
        h_copy = pltpu.make_async_copy(hs, h_out, h_sem)
        wt_copy = pltpu.make_async_copy(wt, wt_out, wt_sem)
        first = [copy(1 + j, me, (*chip, c), src=wb_ref) for j, chip in enumerate(chips[:2])] + [copy(0, me, sib, src=wb_ref)]
        passed = [copy(4 + j, (*chip, c), sib) for j, chip in enumerate(chips)]
        relay_src = (jnp.where(c == 0, 1 - x, x), jnp.where(c == 0, y, 1 - y), c)
        relay_dst = (jnp.where(c == 0, x, 1 - x), jnp.where(c == 0, 1 - y, y), c)
        relay = copy(3, relay_src, relay_dst)

        @pl.when(jnp.logical_and(p == 0, i == 0))
        def _():
            s_out[...] = _ada_modulation(cb_ref, cc_ref, wa_ref, ba_ref, *ada_sc)
            own.start()
            for cp in first:
                cp.start()
            parts = ada_sc[2]
            parts_out[...] = parts[...]
            my_row = pl.ds(slot(*me), 1)
            mod = jnp.concatenate([parts[d, my_row, :] for d in range(N_DEV)], axis=1)
            modv[0:1, :] = mod[:, 0:D_MODEL]
            modv[1:2, :] = mod[:, D_MODEL:2 * D_MODEL]
            own.wait()
            copy(0, sib, me).wait_recv()

        @pl.when(jnp.logical_and(p == 1, i == 0))
        def _():
            for j, chip in enumerate(chips[:2]):
                copy(1 + j, (*chip, c), me).wait_recv()
            relay.start()
            passed[0].start()
            passed[1].start()
            copy(4, (*chips[0], 1 - c), me).wait_recv()

        @pl.when(jnp.logical_and(p == 2, i == 0))
        def _():
            copy(5, (*chips[1], 1 - c), me).wait_recv()

        @pl.when(jnp.logical_and(p == 3, i == 0))
        def _():
            copy(3, (*chips[2], c), me).wait_recv()
            passed[2].start()
            copy(6, (*chips[2], 1 - c), me).wait_recv()
            wt_copy.start()

        rows = pl.ds(pl.multiple_of(i * tm, tm), tm)

        @pl.when(p == 0)
        def _():
            xv = x_ref[...]
            r = lax.rsqrt(jnp.mean(xv * xv, axis=-1, keepdims=True) + EPS)
            hs[rows, :] = ((xv * r * g_ref[...]) * (1.0 + modv[1:2, :]) + modv[0:1, :]).astype(BF16)

        @pl.when(jnp.logical_and(p == 1, i == 0))
        def _():
            h_copy.start()

        w_pair = wt[pl.ds(2 * order_ref[p], 2)].reshape(2 * blk_rows, D_MODEL)
        z_ref[...] = _nt(hs[rows, :], w_pair).astype(BF16)

        @pl.when(jnp.logical_and(p == n_pass - 1, i == nt - 1))
        def _():
            for cp in first + passed + [relay]:
                cp.wait_send()
            h_copy.wait()
            wt_copy.wait()

    whole = lambda shape: pl.BlockSpec(shape, lambda p, i, o: (0,) * len(shape))
    grid_spec = pltpu.PrefetchScalarGridSpec(
        num_scalar_prefetch=1, grid=(n_pass, nt),
        in_specs=[pl.BlockSpec((tm, D_MODEL), lambda p, i, o: (jnp.where(p == 0, i, nt - 1), 0)),
                  whole((1, D_MODEL)), whole((8, D_MODEL)), whole((1, D_MODEL)), whole((D_MODEL, n_col)),
                  whole((1, n_col)), ANY_SPEC],
        out_specs=[ANY_SPEC, pl.BlockSpec((tm, 2 * blk_rows), lambda p, i, o: (i, o[p])), ANY_SPEC,
                   whole((16, D_MODEL)), whole((N_DEV, 16, n_col))],
        scratch_shapes=[pltpu.VMEM((T, D_MODEL), BF16), pltpu.VMEM((N_DEV, blk_rows, D_MODEL), BF16),
                        pltpu.VMEM((8, D_MODEL), F32),
                        pltpu.SemaphoreType.DMA((7,)), pltpu.SemaphoreType.DMA((7,)), pltpu.SemaphoreType.DMA,
                        pltpu.SemaphoreType.DMA, pltpu.SemaphoreType.DMA] + _ada_scratch(n_col))
    return pl.pallas_call(
        body, name="inproj_fwd", grid_spec=grid_spec,
        out_shape=[jax.ShapeDtypeStruct((T, D_MODEL), BF16), jax.ShapeDtypeStruct((T, D_IN), BF16),
                   jax.ShapeDtypeStruct((N_DEV, blk_rows, D_MODEL), BF16),
                   jax.ShapeDtypeStruct((16, D_MODEL), F32), jax.ShapeDtypeStruct((N_DEV, 16, n_col), F32)],
        compiler_params=_params(("arbitrary", "arbitrary")),
    )(order, x, ng, c_blk, c_ctx_row, w_ada_sh, b_ada_sh, w_blk_t)


def _ctx_fwd(ctx, ng, cscale, cshift, w_in_t, ones_blk, gk):
    C = ctx.shape[0]

    def body(x_ref, g_ref, sc_ref, sh_ref, w_ref, ones_ref, gk_ref, h_ref, z_ref, kn_ref):
        xv = x_ref[...]
        r = lax.rsqrt(jnp.mean(xv * xv, axis=-1, keepdims=True) + EPS)
        h = (xv * r * g_ref[...]) * (1.0 + sc_ref[...]) + sh_ref[...]
        hb = h.astype(BF16)
        h_ref[...] = hb
        zk = _nt(hb, w_ref[0:512, :])
        zv = _nt(hb, w_ref[512:1024, :])
        z_ref[:, 0:512] = zk.astype(BF16)
        z_ref[:, 512:1024] = zv.astype(BF16)
        kn_ref[...] = _head_norm(zk, ones_ref, gk_ref[...]).astype(BF16)

    vec = pl.BlockSpec((1, D_MODEL), lambda i: (0, 0))
    return pl.pallas_call(
        body, name="ctx_fwd", grid=(1,),
        in_specs=[pl.BlockSpec((C, D_MODEL), lambda i: (0, 0)), vec, vec, vec,
                  pl.BlockSpec((1024, D_MODEL), lambda i: (2, 0)),
                  pl.BlockSpec((512, 512), lambda i: (0, 0)), pl.BlockSpec((1, 512), lambda i: (0, 0))],
        out_specs=[pl.BlockSpec((C, D_MODEL), lambda i: (0, 0)), pl.BlockSpec((C, 1024), lambda i: (0, 0)),
                   pl.BlockSpec((C, 512), lambda i: (0, 0))],
        out_shape=[jax.ShapeDtypeStruct((C, D_MODEL), BF16), jax.ShapeDtypeStruct((C, 1024), BF16),
                   jax.ShapeDtypeStruct((C, 512), BF16)],
        compiler_params=_params(("arbitrary",)),
    )(ctx, ng, cscale, cshift, w_in_t, ones_blk, gk)


def _sgu_chunk_fwd(au, av, ag, sg, ws_bf, bsb):
    gu, dgu = _gelu_parts(au)
    gv, dgv = _gelu_parts(av)
    rr = lax.rsqrt(jnp.mean(gv * gv, axis=-1, keepdims=True) + EPS)
    vhat = gv * rr
    vn = vhat * sg
    mixed = jnp.dot(ws_bf, vn.astype(BF16), preferred_element_type=F32) + bsb
    sig = _sigmoid(ag)
    sl = ag * sig
    return gu * mixed * sl, (gu, dgu, dgv, rr, vhat, vn, mixed, sig, sl)


def _sgu_qk_fwd(z, sgn, ws, bsb, ones_blk, gq, gk, tm):
    T = z.shape[0]

    def body(au_ref, av_ref, ag_ref, q_ref, k_ref, sg_ref, ws_ref, bsb_ref, ones_ref, gq_ref, gk_ref,
             o_ref, qs_ref, kn_ref):
        qs = _head_norm(q_ref[...].astype(F32), ones_ref, gq_ref[...]) * (ATT_SCALE * LOG2E)
        qs_ref[...] = qs.astype(BF16)
        kn_ref[...] = _head_norm(k_ref[...].astype(F32), ones_ref, gk_ref[...]).astype(BF16)
        for g in range(SGU_GROUPS):
            ws_bf = ws_ref[g].astype(BF16)
            sg = sg_ref[:, 128 * g:128 * (g + 1)]
            bsb_g = bsb_ref[g]
            for j in range(tm // CHUNK):
                rs, cs = slice(CHUNK * j, CHUNK * (j + 1)), slice(128 * g, 128 * (g + 1))
                out, _ = _sgu_chunk_fwd(au_ref[rs, cs].astype(F32), av_ref[rs, cs].astype(F32),
                                        ag_ref[rs, cs].astype(F32), sg, ws_bf, bsb_g)
                o_ref[rs, cs] = out.astype(BF16)

    zcol = lambda col: pl.BlockSpec((tm, 512), lambda i: (i, col))
    v512 = pl.BlockSpec((1, 512), lambda i: (0, 0))
    row = pl.BlockSpec((tm, 512), lambda i: (i, 0))
    out = jax.ShapeDtypeStruct((T, 512), BF16)
    return pl.pallas_call(
        body, name="sgu_qk_fwd", grid=(T // tm,),
        in_specs=[zcol(0), zcol(1), zcol(2), zcol(3), zcol(4), v512,
                  pl.BlockSpec((SGU_GROUPS, CHUNK, CHUNK), lambda i: (0, 0, 0)),
                  pl.BlockSpec((SGU_GROUPS, CHUNK, 128), lambda i: (0, 0, 0)),
                  pl.BlockSpec((512, 512), lambda i: (0, 0)), v512, v512],
        out_specs=[row, row, row], out_shape=[out, out, out],
        compiler_params=_params(("arbitrary",)),
    )(z, z, z, z, z, sgn, ws, bsb, ones_blk, gq, gk)


def _attn_type(rb, nrb):
    return jnp.where(rb == 0, 0, jnp.where(rb == nrb - 1, 2, 1))


def _attn_specs(T, C, att_sub):
    return [
        pl.BlockSpec((att_sub * TQ, 128), lambda hp, st: (st, hp)),
        pl.BlockSpec((T, 128), lambda hp, st: (0, hp)),
        pl.BlockSpec((T, 128), lambda hp, st: (0, 20 + hp)),
        pl.BlockSpec((C, 128), lambda hp, st: (0, hp)),
        pl.BlockSpec((C, 128), lambda hp, st: (0, 4 + hp)),
        pl.BlockSpec((2, 2 * WIN_R, GRID_W, 128), lambda hp, st: (hp, 0, 0, 0)),
        pl.BlockSpec((3, TQ, TK), lambda hp, st: (0, 0, 0)),
        pl.BlockSpec((att_sub * TQ, 128), lambda hp, st: (st, 24 + hp)),
    ]


def _build_bias(pairs_ref, mask_ref, bias_sc):
    for t in range(3):
        for hh in range(2):
            for i in range(Q_ROWS):
                for mm in range(K_ROWS // 2):
                    p = min(max(WIN_R - Q_ROWS * t + 2 * mm - i, 0), 2 * WIN_R - 1)
                    rs, cs = slice(GRID_W * i, GRID_W * (i + 1)), slice(128 * mm, 128 * (mm + 1))
                    bias_sc[t, hh, rs, cs] = (pairs_ref[hh, p] + mask_ref[t, rs, cs]) * LOG2E


def _attn_fwd(qs, kn, z, ckn, zc, pairs, row_mask, riders=()):
    T, C = qs.shape[0], ckn.shape[0]
    rows = T // GRID_W
    nrb = rows // Q_ROWS
    att_sub = min(ATT_SUB, nrb)
    n_st = nrb // att_sub
    n_rid = len(riders)

    def body(q_ref, k_ref, v_ref, ck_ref, cv_ref, pairs_ref, mask_ref, bg_ref, *rest):
        rid_src = rest[:n_rid]
        ob_ref, outb_ref, lse_ref = rest[n_rid:n_rid + 3]
        rid_dst = rest[n_rid + 3:2 * n_rid + 3]
        bias_sc = rest[2 * n_rid + 3]
        rid_sems = rest[2 * n_rid + 4:]

        @pl.when(jnp.logical_and(pl.program_id(0) == 0, pl.program_id(1) == 0))
        def _():
            for r, (ex, _) in enumerate(riders):
                ex.start(rid_src[r], rid_dst[r], *rid_sems[3 * r:3 * r + 3])

        @pl.when(pl.program_id(1) == 0)
        def _():
            _build_bias(pairs_ref, mask_ref, bias_sc)

        ck2, cv2 = ck_ref[...], cv_ref[...]
        lane = lax.broadcasted_iota(jnp.int32, (1, 128), 1)
        for sub in range(att_sub):
            rb = att_sub * pl.program_id(1) + sub
            bias_ref = bias_sc.at[_attn_type(rb, nrb)]
            rs = slice(TQ * sub, TQ * (sub + 1))
            ks = pl.multiple_of(jnp.clip(Q_ROWS * rb - 4, 0, rows - K_ROWS) * GRID_W, GRID_W)
            q2 = q_ref[rs, :]
            k2 = k_ref[pl.ds(ks, TK), :]
            v2 = v_ref[pl.ds(ks, TK), :]
            o_acc = jnp.zeros((TQ, 128), F32)
            lse_acc = jnp.zeros((TQ, 128), F32)
            for hh in range(2):
                msk = (lane >= HEAD_DIM) == bool(hh)
                qm = jnp.where(msk, q2, jnp.zeros_like(q2))
                s = _nt(qm, k2) + bias_ref[hh]
                sc = _nt(qm, ck2)
                m = jnp.maximum(jnp.max(s, axis=-1, keepdims=True), jnp.max(sc, axis=-1, keepdims=True))
                p = jnp.exp2(s - m)
                pc = jnp.exp2(sc - m)
                va = jnp.where(msk, v2, jnp.ones_like(v2))
                cva = jnp.where(msk, cv2, jnp.ones_like(cv2))
                num = (jnp.dot(p.astype(BF16), va, preferred_element_type=F32)
                       + jnp.dot(pc.astype(BF16), cva, preferred_element_type=F32))
                den = pltpu.roll(num, HEAD_DIM, 1)
                o_acc = jnp.where(msk, num / den, o_acc)
                lse_acc = jnp.where(msk, m + jnp.log(den) * LOG2E, lse_acc)
            ob_ref[rs, :] = o_acc.astype(BF16)
            lse_ref[rs, :] = lse_acc
            bg = bg_ref[rs, :].astype(F32)
            outb_ref[rs, :] = (o_acc * (bg * _sigmoid(bg))).astype(BF16)

        @pl.when(jnp.logical_and(pl.program_id(0) == pl.num_programs(0) - 1, pl.program_id(1) == n_st - 1))
        def _():
            for r, (ex, _) in enumerate(riders):
                ex.wait(rid_src[r], rid_dst[r], *rid_sems[3 * r:3 * r + 3])

    tile = pl.BlockSpec((att_sub * TQ, 128), lambda hp, st: (st, hp))
    res = pl.pallas_call(
        body, name="attn_fwd", grid=(4, n_st),
        in_specs=_attn_specs(T, C, att_sub) + [ANY_SPEC] * n_rid,
        out_specs=[tile, tile, tile] + [ANY_SPEC] * n_rid,
        out_shape=[jax.ShapeDtypeStruct((T, 512), BF16), jax.ShapeDtypeStruct((T, 512), BF16),
                   jax.ShapeDtypeStruct((T, 512), F32)] + [ex.out_shape for ex, _ in riders],
        scratch_shapes=[pltpu.VMEM((3, 2, TQ, TK), F32)] + [s for ex, _ in riders for s in ex.scratch],
        compiler_params=_params(("arbitrary", "arbitrary")),
    )(qs, kn, z, ckn, zc, pairs, row_mask, z, *[a for _, a in riders])
    return res[:3], res[3:]


def _outproj_loss_bwd(x, tgt, out_a, out_b, gate, w_out, tm):
    T = x.shape[0]
    nt = T // tm

    def body(x_ref, t_ref, oa_ref, ob_ref, gate_ref, w_ref, dy_ref, dmc_ref, dw_ref, dgate_ref, loss_ref, acc):
        @pl.when(pl.program_id(0) == 0)
        def _():
            acc[...] = jnp.zeros_like(acc)
            dgate_ref[...] = jnp.zeros_like(dgate_ref)
            loss_ref[...] = jnp.zeros_like(loss_ref)

        oa, ob = oa_ref[...], ob_ref[...]
        gate_v = gate_ref[...]
        mix = (jnp.dot(oa, w_ref[0:512, :], preferred_element_type=F32)
               + jnp.dot(ob, w_ref[512:1024, :], preferred_element_type=F32))
        e = x_ref[...] + gate_v * mix - t_ref[...]
        se = jnp.sum(jnp.sum(e * e, axis=0, keepdims=True), axis=1, keepdims=True)
        loss_ref[...] += jnp.broadcast_to(se * (0.5 / D_MODEL), loss_ref.shape)
        dy = e * (1.0 / D_MODEL)
        dy_ref[...] = dy
        dgate_ref[...] += jnp.sum(dy * mix, axis=0, keepdims=True)
        dmix = (dy * gate_v).astype(BF16)
        dmc_ref[...] = _nt(dmix, w_ref[...]).astype(BF16)
        acc[0:512, :] += _tn(oa, dmix)
        acc[512:1024, :] += _tn(ob, dmix)

        @pl.when(pl.program_id(0) == nt - 1)
        def _():
            dw_ref[...] = acc[...].astype(BF16)

    row = lambda w: pl.BlockSpec((tm, w), lambda i: (i, 0))
    return pl.pallas_call(
        body, name="outproj_loss_bwd", grid=(nt,),
        in_specs=[row(D_MODEL), row(D_MODEL), row(512), row(512),
                  pl.BlockSpec((1, D_MODEL), lambda i: (0, 0)),
                  pl.BlockSpec((D_MODEL, D_MODEL), lambda i: (0, 0))],
        out_specs=[row(D_MODEL), row(D_MODEL), pl.BlockSpec((D_MODEL, D_MODEL), lambda i: (0, 0)),
                   pl.BlockSpec((1, D_MODEL), lambda i: (0, 0)), pl.BlockSpec((1, 128), lambda i: (0, 0))],
        out_shape=[jax.ShapeDtypeStruct((T, D_MODEL), F32), jax.ShapeDtypeStruct((T, D_MODEL), BF16),
                   jax.ShapeDtypeStruct((D_MODEL, D_MODEL), BF16), jax.ShapeDtypeStruct((1, D_MODEL), F32),
                   jax.ShapeDtypeStruct((1, 128), F32)],
        scratch_shapes=[pltpu.VMEM((D_MODEL, D_MODEL), F32)],
        compiler_params=_params(("arbitrary",)),
    )(x, tgt, out_a, out_b, gate, w_out)


def _attn_bwd(qs, kn, z, ckn, zc, pairs, row_mask, ob, lse, dmc, hsel, fold, riders):
    T, C = qs.shape[0], ckn.shape[0]
    rows = T // GRID_W
    nrb = rows // Q_ROWS
    att_sub = min(ATT_SUB, nrb)
    n_st = nrb // att_sub
    n_rid = len(riders)

    def body(q_ref, k_ref, v_ref, ck_ref, cv_ref, pairs_ref, mask_ref, bg_ref, ob_ref, lse_ref, do_ref,
             hsel_ref, fold_ref, *rest):
        rid_src = rest[:n_rid]
        dq_ref, dk_ref, dv_ref, dck_ref, dcv_ref, dbg_ref, drpb_ref = rest[n_rid:n_rid + 7]
        rid_dst = rest[n_rid + 7:2 * n_rid + 7]
        bias_sc, dacc_ref = rest[2 * n_rid + 7:2 * n_rid + 9]
        rid_sems = rest[2 * n_rid + 9:]
        hp, st = pl.program_id(0), pl.program_id(1)

        @pl.when(jnp.logical_and(hp == 0, st == 0))
        def _():
            for r, (ex, _) in enumerate(riders):
                ex.start(rid_src[r], rid_dst[r], *rid_sems[3 * r:3 * r + 3])

        @pl.when(st == 0)
        def _():
            _build_bias(pairs_ref, mask_ref, bias_sc)
            dk_ref[...] = jnp.zeros_like(dk_ref)
            dv_ref[...] = jnp.zeros_like(dv_ref)
            dck_ref[...] = jnp.zeros_like(dck_ref)
            dcv_ref[...] = jnp.zeros_like(dcv_ref)
            dacc_ref[...] = jnp.zeros_like(dacc_ref)

        ck2, cv2 = ck_ref[...], cv_ref[...]
        lane = lax.broadcasted_iota(jnp.int32, (1, 128), 1)
        for sub in range(att_sub):
            rb = att_sub * st + sub
            bias_ref = bias_sc.at[_attn_type(rb, nrb)]
            rs = slice(TQ * sub, TQ * (sub + 1))
            kb = jnp.clip(Q_ROWS * rb - 4, 0, rows - K_ROWS)
            ks = pl.multiple_of(kb * GRID_W, GRID_W)
            ebase = kb - Q_ROWS * rb + 11
            q2 = q_ref[rs, :]
            k2 = k_ref[pl.ds(ks, TK), :]
            v2 = v_ref[pl.ds(ks, TK), :]
            bg = bg_ref[rs, :].astype(F32)
            sig = _sigmoid(bg)
            obv = ob_ref[rs, :].astype(F32)
            dout = do_ref[rs, :].astype(F32)
            dbg_ref[rs, :] = (dout * obv * (sig * (1.0 + bg * (1.0 - sig)))).astype(BF16)
            d_o = dout * (bg * sig)
            d_oo = d_o * obv
            lse2 = lse_ref[rs, :]
            dq_acc = jnp.zeros((TQ, 128), F32)
            for hh in range(2):
                msk = (lane >= HEAD_DIM) == bool(hh)
                qm = jnp.where(msk, q2, jnp.zeros_like(q2))
                lse_h = jnp.max(jnp.where(msk, lse2, -jnp.inf), axis=-1, keepdims=True)
                p = jnp.exp2(_nt(qm, k2) + bias_ref[hh] - lse_h)
                pc = jnp.exp2(_nt(qm, ck2) - lse_h)
                dom_f = jnp.where(msk, d_o, 0.0)
                dom = dom_f.astype(BF16)
                delta = jnp.sum(jnp.where(msk, d_oo, 0.0), axis=-1, keepdims=True)
                d_hi = delta.astype(BF16).astype(F32)
                x0 = HEAD_DIM * (1 - hh)
                dom_aug = jnp.where(lane == x0, -d_hi, jnp.where(lane == x0 + 1, d_hi - delta, dom_f)).astype(BF16)
                extra = jnp.logical_or(lane == x0, lane == x0 + 1)
                va = jnp.where(msk, v2, jnp.where(extra, jnp.ones_like(v2), jnp.zeros_like(v2)))
                cva = jnp.where(msk, cv2, jnp.where(extra, jnp.ones_like(cv2), jnp.zeros_like(cv2)))
                ds = p * _nt(dom_aug, va)
                dsc = pc * _nt(dom_aug, cva)
                dsb, dscb = ds.astype(BF16), dsc.astype(BF16)
                dq_h = (jnp.dot(dsb, k2, preferred_element_type=F32)
                        + jnp.dot(dscb, ck2, preferred_element_type=F32))
                dq_acc = jnp.where(msk, dq_h, dq_acc)
                dk_ref[pl.ds(ks, TK), :] += _tn(dsb, qm)
                dv_ref[pl.ds(ks, TK), :] += _tn(p.astype(BF16), dom)
                dck_ref[...] += _tn(dscb, qm)
                dcv_ref[...] += _tn(pc.astype(BF16), dom)
                for i in range(Q_ROWS):
                    for mm in range(K_ROWS // 2):
                        dacc_ref[hh, ebase + (2 * mm - i)] += ds[GRID_W * i:GRID_W * (i + 1),
                                                                 128 * mm:128 * (mm + 1)]
            dq_ref[rs, :] = dq_acc

        @pl.when(st == n_st - 1)
        def _():
            for hh in range(2):
                drpb_ref[hh] = _rpb_diag_sums(dacc_ref.at[hh], hsel_ref, fold_ref)

        @pl.when(jnp.logical_and(hp == pl.num_programs(0) - 1, st == n_st - 1))
        def _():
            for r, (ex, _) in enumerate(riders):
                ex.wait(rid_src[r], rid_dst[r], *rid_sems[3 * r:3 * r + 3])

    tile = pl.BlockSpec((att_sub * TQ, 128), lambda hp, st: (st, hp))
    colT = pl.BlockSpec((T, 128), lambda hp, st: (0, hp))
    colC = pl.BlockSpec((C, 128), lambda hp, st: (0, hp))
    res = pl.pallas_call(
        body, name="attn_bwd", grid=(4, n_st),
        in_specs=(_attn_specs(T, C, att_sub) + [tile, tile,
                                                pl.BlockSpec((att_sub * TQ, 128), lambda hp, st: (st, 4 + hp)),
                                       pl.BlockSpec((8, 128, 128), lambda hp, st: (0, 0, 0)),
                                       pl.BlockSpec((32, 256), lambda hp, st: (0, 0))]
                  + [ANY_SPEC] * n_rid),
        out_specs=([tile, colT, colT, colC, colC, tile, pl.BlockSpec((2, 32, 128), lambda hp, st: (hp, 0, 0))]
                   + [ANY_SPEC] * n_rid),
        out_shape=([jax.ShapeDtypeStruct((T, 512), F32), jax.ShapeDtypeStruct((T, 512), F32),
                    jax.ShapeDtypeStruct((T, 512), F32), jax.ShapeDtypeStruct((C, 512), F32),
                    jax.ShapeDtypeStruct((C, 512), F32), jax.ShapeDtypeStruct((T, 512), BF16),
                    jax.ShapeDtypeStruct((N_HEADS, 32, 128), F32)] + [ex.out_shape for ex, _ in riders]),
        scratch_shapes=([pltpu.VMEM((3, 2, TQ, TK), F32), pltpu.VMEM((2, N_DIAG, GRID_W, 128), F32)]
                        + [s for ex, _ in riders for s in ex.scratch]),
        compiler_params=_params(("arbitrary", "arbitrary")),
    )(qs, kn, z, ckn, zc, pairs, row_mask, z, ob, lse, dmc, hsel, fold, *[a for _, a in riders])
    return res[:7], res[7:]


def _rpb_diag_sums(a_ref, hsel_ref, fold_ref):
    n_off = 2 * WIN_C - 1
    n_dr = 2 * WIN_R - 1
    qc = lax.broadcasted_iota(jnp.int32, (GRID_W, 128), 0)
    lane = lax.broadcasted_iota(jnp.int32, (GRID_W, 128), 1)
    diff = lane % GRID_W - qc + (WIN_C - 1)
    left = lane < GRID_W

    def by_dr(dr):
        return a_ref[dr + 4] + pltpu.roll(a_ref[dr + 3], GRID_W, 1)

    out = jnp.zeros((32, 128), F32)
    for j in range((n_dr + 1) // 2):
        hi = pltpu.roll(by_dr(2 * j + 1), GRID_W, 1) if 2 * j + 1 < n_dr else 0.0
        pair = jnp.where(left, by_dr(2 * j), hi)
        parts = []
        for o in range(n_off):
            mv = jnp.where(diff == o, pair, 0.0)
            acc = mv[0:8]
            for r8 in range(1, GRID_W // 8):
                acc = acc + mv[8 * r8:8 * (r8 + 1)]
            parts.append(acc)
        parts.append(jnp.zeros((8, 128), F32))
        stack = jnp.concatenate(parts, axis=0)
        s_hi = stack.astype(BF16)
        s_lo = (stack - s_hi.astype(F32)).astype(BF16)
        per_o = (jnp.dot(fold_ref[...], s_hi, preferred_element_type=F32)
                 + jnp.dot(fold_ref[...], s_lo, preferred_element_type=F32))
        out = out + _dot2(per_o, hsel_ref[j])
    return out


def _head_norm_bwd(raw, dn, gain, ones_ref):
    rr = lax.rsqrt(_head_sum(raw * raw, ones_ref) * (1.0 / HEAD_DIM) + EPS)
    hat = raw * rr
    dgain = jnp.sum(dn * hat, axis=0, keepdims=True)
    dhat = dn * gain
    mean = _head_sum(dhat * hat, ones_ref) * (1.0 / HEAD_DIM)
    return rr * (dhat - hat * mean), dgain


def _ctx_k_bwd(zc, dck, dcv, ones_blk, gk, foldm):
    C = dck.shape[0]

    def body(bk_ref, dk_ref, dv_ref, ones_ref, gk_ref, fold_ref, dbk_ref, dbv_ref, dgk_ref):
        dbk, dgk = _head_norm_bwd(bk_ref[...].astype(F32), dk_ref[...] * LN2, gk_ref[...], ones_ref)
        dbk_ref[...] = dbk.astype(BF16)
        dbv_ref[...] = dv_ref[...].astype(BF16)
        dgk_ref[...] = jnp.dot(jnp.broadcast_to(dgk, (8, 512)), fold_ref[...],
                               preferred_element_type=F32, precision=HI)

    row = pl.BlockSpec((C, 512), lambda i: (0, 0))
    cst = lambda a, b: pl.BlockSpec((a, b), lambda i: (0, 0))
    out_row = jax.ShapeDtypeStruct((C, 512), BF16)
    return pl.pallas_call(
        body, name="ctx_k_bwd", grid=(1,),
        in_specs=[row, row, row, cst(512, 512), cst(1, 512), cst(512, 128)],
        out_specs=[row, row, cst(8, 128)],
        out_shape=[out_row, out_row, jax.ShapeDtypeStruct((8, 128), F32)],
        compiler_params=_params(("arbitrary",)),
    )(zc, dck, dcv, ones_blk, gk, foldm)


def _bwd_mid(z, dmc, dqs, dk, dv, db_g, h, hc, dzc_k, dzc_v, sgn, ws, wst, bsb, ones8, ones_blk, gq, gk, foldm, tk,
             riders=()):
    T = z.shape[0]
    nt = T // tk
    blk = D_IN // N_DEV
    n_in, n_out, n_sc = 23, 7, 9
    n_rid = len(riders)

    def body(*refs):
        (au_ref, av_ref, ag_ref, bq_ref, bk_ref, d_ref, dq_ref, dk_ref, dv_ref, dbg_ref, h_ref,
         hc_ref, dzck_ref, dzcv_ref, sg_ref, ws_ref, wst_ref, bsb_ref, ones8_ref, ones_ref, gq_ref, gk_ref,
         fold_ref) = refs[:n_in]
        rid_src = refs[n_in:n_in + n_rid]
        dz_ref, sums_out, dws_ref, dbs_ref, dsg_ref, dgq_ref, dgk_ref = refs[n_in + n_rid:n_in + n_rid + n_out]
        rid_dst = refs[n_in + n_rid + n_out:n_in + 2 * n_rid + n_out]
        (acc, accq, acck, stage, sem, send_buf, tmp, s1, r1) = refs[n_in + 2 * n_rid + n_out:
                                                                   n_in + 2 * n_rid + n_out + n_sc]
        rid_sems = refs[n_in + 2 * n_rid + n_out + n_sc:]
        t = pl.program_id(0)

        @pl.when(t == 0)
        def _():
            for r, (ex, _) in enumerate(riders):
                ex.start(rid_src[r], rid_dst[r], *rid_sems[3 * r:3 * r + 3])
            acc[...] = jnp.zeros_like(acc)
            acc[512 * 4:512 * 5, :] = _tn(dzck_ref[...], hc_ref[...])
            acc[512 * 5:512 * 6, :] = _tn(dzcv_ref[...], hc_ref[...])
            dws_ref[...] = jnp.zeros_like(dws_ref)
            dbs_ref[...] = jnp.zeros_like(dbs_ref)
            dsg_ref[...] = jnp.zeros_like(dsg_ref)
            accq[...] = jnp.zeros_like(accq)
            acck[...] = jnp.zeros_like(acck)

        for g in range(SGU_GROUPS):
            ws_bf = ws_ref[g].astype(BF16)
            wst_bf = wst_ref[g].astype(BF16)
            sg = sg_ref[:, 128 * g:128 * (g + 1)]
            bsb_g = bsb_ref[g]
            for j in range(tk // CHUNK):
                rs, cs = slice(CHUNK * j, CHUNK * (j + 1)), slice(128 * g, 128 * (g + 1))
                au, av, ag = (au_ref[rs, cs].astype(F32), av_ref[rs, cs].astype(F32), ag_ref[rs, cs].astype(F32))
                d = d_ref[rs, cs].astype(F32)
                _, (gu, dgu, dgv, rr, vhat, vn, mixed, sig, sl) = _sgu_chunk_fwd(au, av, ag, sg, ws_bf, bsb_g)
                dz_ref[rs, 128 * g:128 * (g + 1)] = (d * mixed * sl * dgu).astype(BF16)
                dz_ref[rs, 1024 + 128 * g:1024 + 128 * (g + 1)] = (
                    d * gu * mixed * (sig * (1.0 + ag * (1.0 - sig)))).astype(BF16)
                dmixed = d * gu * sl
                dmb = dmixed.astype(BF16)
                dm_lo = (dmixed - dmb.astype(F32)).astype(BF16)
                dbs_ref[g] += _nt(ones8_ref[...], dmb) + _nt(ones8_ref[...], dm_lo)
                dws_ref[g] += _nt(dmb, vn.astype(BF16))
                dvn = jnp.dot(wst_bf, dmb, preferred_element_type=F32)
                dsg_ref[:, 128 * g:128 * (g + 1)] += jnp.sum(dvn * vhat, axis=0, keepdims=True)
                dvhat = dvn * sg
                mean = jnp.mean(dvhat * vhat, axis=-1, keepdims=True)
                dz_ref[rs, 512 + 128 * g:512 + 128 * (g + 1)] = (rr * (dvhat - vhat * mean) * dgv).astype(BF16)

        dbq, dgq = _head_norm_bwd(bq_ref[...].astype(F32), dq_ref[...] * ATT_SCALE, gq_ref[...], ones_ref)
        dz_ref[:, 512 * 3:512 * 4] = dbq.astype(BF16)
        accq[...] += dgq
        dbk, dgk = _head_norm_bwd(bk_ref[...].astype(F32), dk_ref[...] * LN2, gk_ref[...], ones_ref)
        dz_ref[:, 512 * 4:512 * 5] = dbk.astype(BF16)
        acck[...] += dgk
        dz_ref[:, 512 * 5:512 * 6] = dv_ref[...].astype(BF16)
        dz_ref[:, 512 * 6:512 * 7] = dbg_ref[...]

        hv = h_ref[...]
        for k in range(N_BRANCH):
            acc[512 * k:512 * (k + 1), :] += _tn(dz_ref[:, 512 * k:512 * (k + 1)], hv)

        @pl.when(t == nt - 1)
        def _():
            dgq_ref[...] = jnp.dot(jnp.broadcast_to(accq[...], (8, 512)), fold_ref[...],
                                   preferred_element_type=F32, precision=HI)
            dgk_ref[...] = jnp.dot(jnp.broadcast_to(acck[...], (8, 512)), fold_ref[...],
                                   preferred_element_type=F32, precision=HI)
            cidx = lax.axis_index("c")
            sib = (lax.axis_index("x"), lax.axis_index("y"), 1 - cidx)
            swaps = []
            for q in range(N_DEV // 2):
                theirs = acc[pl.ds(pl.multiple_of(2 * blk * q + blk * (1 - cidx), 8), blk), :]
                send_buf[q] = theirs.astype(BF16)
                cp = pltpu.make_async_remote_copy(src_ref=send_buf.at[q], dst_ref=tmp.at[q], send_sem=s1.at[q],
                                                  recv_sem=r1.at[q], device_id=sib, device_id_type=MESH)
                cp.start()
                swaps.append(cp)
            for q in range(N_DEV // 2):
                swaps[q].wait_recv()
                mine = acc[pl.ds(pl.multiple_of(2 * blk * q + blk * cidx, 8), blk), :]
                stage[...] = (mine + tmp[q].astype(F32)).astype(BF16)
                out = pltpu.make_async_copy(stage, sums_out.at[q], sem)
                out.start()
                out.wait()
            for cp in swaps:
                cp.wait_send()
            for r, (ex, _) in enumerate(riders):
                ex.wait(rid_src[r], rid_dst[r], *rid_sems[3 * r:3 * r + 3])

    zcol = lambda col: pl.BlockSpec((tk, 512), lambda t: (t, col))
    row = pl.BlockSpec((tk, 512), lambda t: (t, 0))
    whole = lambda a: pl.BlockSpec(a.shape, lambda t: (0,) * a.ndim)
    res = pl.pallas_call(
        body, name="bwd_mid", grid=(nt,),
        in_specs=[zcol(0), zcol(1), zcol(2), zcol(3), zcol(4), row, row, row, row, row,
                  pl.BlockSpec((tk, D_MODEL), lambda t: (t, 0)), whole(hc), whole(dzc_k), whole(dzc_v),
                  whole(sgn), whole(ws), whole(wst), whole(bsb), whole(ones8), whole(ones_blk), whole(gq), whole(gk),
                  whole(foldm)] + [ANY_SPEC] * n_rid,
        out_specs=[pl.BlockSpec((tk, D_IN), lambda t: (t, 0)), ANY_SPEC,
                   pl.BlockSpec((SGU_GROUPS, CHUNK, CHUNK), lambda t: (0, 0, 0)),
                   pl.BlockSpec((SGU_GROUPS, 8, CHUNK), lambda t: (0, 0, 0)),
                   pl.BlockSpec((1, 512), lambda t: (0, 0)), pl.BlockSpec((8, 128), lambda t: (0, 0)),
                   pl.BlockSpec((8, 128), lambda t: (0, 0))] + [ANY_SPEC] * n_rid,
        out_shape=[jax.ShapeDtypeStruct((T, D_IN), BF16), jax.ShapeDtypeStruct((N_DEV // 2, blk, D_MODEL), BF16),
                   jax.ShapeDtypeStruct((SGU_GROUPS, CHUNK, CHUNK), F32),
                   jax.ShapeDtypeStruct((SGU_GROUPS, 8, CHUNK), F32), jax.ShapeDtypeStruct((1, 512), F32),
                   jax.ShapeDtypeStruct((8, 128), F32), jax.ShapeDtypeStruct((8, 128), F32)]
                  + [ex.out_shape for ex, _ in riders],
        scratch_shapes=[pltpu.VMEM((D_IN, D_MODEL), F32), pltpu.VMEM((1, 512), F32), pltpu.VMEM((1, 512), F32),
                        pltpu.VMEM((blk, D_MODEL), BF16), pltpu.SemaphoreType.DMA,
                        pltpu.VMEM((N_DEV // 2, blk, D_MODEL), BF16), pltpu.VMEM((N_DEV // 2, blk, D_MODEL), BF16),
                        pltpu.SemaphoreType.DMA((N_DEV // 2,)), pltpu.SemaphoreType.DMA((N_DEV // 2,))]
                       + [s for ex, _ in riders for s in ex.scratch],
        compiler_params=_params(("arbitrary",)),
    )(z, z, z, z, z, dmc, dqs, dk, dv, db_g, h, hc, dzc_k, dzc_v, sgn, ws, wst, bsb, ones8, ones_blk, gq, gk, foldm,
      *[a for _, a in riders])
    return res[:n_out], res[n_out:]


def _inproj_bwd_dx(dzs, w_row0, w_in_t, x, dy, ng, scale, shift, tm, name, riders=()):
    T = x.shape[0]
    n = len(dzs)
    wpc = dzs[0].shape[1]
    nt = T // tm
    with_dx = dy is not None
    n_own_in = n + 5 + with_dx
    n_own_out = 3 + with_dx
    n_rid = len(riders)

    def body(*refs):
        dz_refs = refs[:n]
        n_in = n_own_in + n_rid
        own = refs[n:n_own_in] + refs[n_in:n_in + n_own_out]
        rid_src = refs[n_own_in:n_own_in + n_rid]
        rid_dst = refs[n_in + n_own_out:n_in + n_own_out + n_rid]
        rid_sems = refs[n_in + n_own_out + n_rid:]
        if with_dx:
            w_ref, x_ref, dy_ref, g_ref, sc_ref, sh_ref, gx_ref, dsh_ref, dsc_ref, dg_ref = own
        else:
            w_ref, x_ref, g_ref, sc_ref, sh_ref, dsh_ref, dsc_ref, dg_ref = own

        @pl.when(pl.program_id(0) == 0)
        def _():
            for r, (ex, _) in enumerate(riders):
                ex.start(rid_src[r], rid_dst[r], *rid_sems[3 * r:3 * r + 3])
            dsh_ref[...] = jnp.zeros_like(dsh_ref)
            dsc_ref[...] = jnp.zeros_like(dsc_ref)
            dg_ref[...] = jnp.zeros_like(dg_ref)

        dh = jnp.dot(dz_refs[0][...], w_ref[0:wpc, :], preferred_element_type=F32)
        for k in range(1, n):
            dh = dh + jnp.dot(dz_refs[k][...], w_ref[wpc * k:wpc * (k + 1), :], preferred_element_type=F32)
        xv = x_ref[...]
        r = lax.rsqrt(jnp.mean(xv * xv, axis=-1, keepdims=True) + EPS)
        xn = xv * r
        gv, op = g_ref[...], 1.0 + sc_ref[...]
        dsh_ref[...] += jnp.sum(dh, axis=0, keepdims=True)
        dsc_ref[...] += jnp.sum(dh * xn * gv, axis=0, keepdims=True)
        dg_ref[...] += jnp.sum(dh * op * xn, axis=0, keepdims=True)
        if with_dx:
            dxn = dh * (gv * op)
            gx_ref[...] = r * (dxn - xn * jnp.mean(dxn * xn, axis=-1, keepdims=True)) + dy_ref[...]

        @pl.when(pl.program_id(0) == nt - 1)
        def _():
            for r, (ex, _) in enumerate(riders):
                ex.wait(rid_src[r], rid_dst[r], *rid_sems[3 * r:3 * r + 3])

    vec = pl.BlockSpec((1, D_MODEL), lambda i: (0, 0))
    rowf = pl.BlockSpec((tm, D_MODEL), lambda i: (i, 0))
    in_specs = [pl.BlockSpec((tm, wpc), lambda i: (i, 0))] * n
    in_specs += [pl.BlockSpec((wpc * n, D_MODEL), lambda i: (w_row0 // (wpc * n), 0)), rowf]
    args = list(dzs) + [w_in_t, x]
    vshape = jax.ShapeDtypeStruct((1, D_MODEL), F32)
    out_specs, out_shape = [vec, vec, vec], [vshape, vshape, vshape]
    if with_dx:
        in_specs.append(rowf)
        args.append(dy)
        out_specs = [rowf] + out_specs
        out_shape = [jax.ShapeDtypeStruct((T, D_MODEL), F32)] + out_shape
    in_specs += [vec, vec, vec] + [ANY_SPEC] * n_rid
    args += [ng, scale, shift] + [a for _, a in riders]
    res = pl.pallas_call(
        body, name=name, grid=(nt,), in_specs=in_specs, out_specs=out_specs + [ANY_SPEC] * n_rid,
        out_shape=out_shape + [ex.out_shape for ex, _ in riders],
        scratch_shapes=[s for ex, _ in riders for s in ex.scratch],
        compiler_params=_params(("arbitrary",)),
    )(*args)
    return res[:n_own_out], res[n_own_out:]


def _adamw_sharded(w, gparts, m, v, tr, name, riders=()):
    R, C = w.shape
    n_part = gparts.shape[0]
    nt = R // tr
    n_rid = len(riders)

    def body(w_ref, gp_ref, m_ref, v_ref, *rest):
        rid_src = rest[:n_rid]
        g_ref, d_ref, m2_ref, v2_ref = rest[n_rid:n_rid + 4]
        rid_dst = rest[n_rid + 4:2 * n_rid + 4]
        rid_sems = rest[2 * n_rid + 4:]

        @pl.when(pl.program_id(0) == 0)
        def _():
            for r, (ex, _) in enumerate(riders):
                ex.start(rid_src[r], rid_dst[r], *rid_sems[3 * r:3 * r + 3])

        g = gp_ref[0].astype(F32)
        for d in range(1, n_part):
            g = g + gp_ref[d].astype(F32)
        delta, m2, v2 = _adam(w_ref[...], g, m_ref[...], v_ref[...])
        g_ref[...] = g
        d_ref[...] = delta
        m2_ref[...] = m2
        v2_ref[...] = v2

        @pl.when(pl.program_id(0) == nt - 1)
        def _():
            for r, (ex, _) in enumerate(riders):
                ex.wait(rid_src[r], rid_dst[r], *rid_sems[3 * r:3 * r + 3])

    row = pl.BlockSpec((tr, C), lambda i: (i, 0))
    sh = jax.ShapeDtypeStruct((R, C), F32)
    res = pl.pallas_call(
        body, name=name, grid=(nt,),
        in_specs=[row, pl.BlockSpec((n_part, tr, C), lambda i: (0, i, 0)), row, row] + [ANY_SPEC] * n_rid,
        out_specs=[row, row, row, row] + [ANY_SPEC] * n_rid,
        out_shape=[sh, sh, sh, sh] + [ex.out_shape for ex, _ in riders],
        scratch_shapes=[s for ex, _ in riders for s in ex.scratch],
        compiler_params=_params(("arbitrary",)),
    )(w, gparts, m, v, *[a for _, a in riders])
    return res[:4], res[4:]


def _pack_vectors(vec_rows, dsg, dgq, dgk, dgk_c, loss_part):
    flat = [a for _, arrs in vec_rows for a in arrs]

    def body(*refs):
        vecs = list(refs[:len(flat)])
        dsg_ref, dgq_ref, dgk_ref, dgkc_ref, loss_ref, v_ref = refs[len(flat):]
        row = lax.broadcasted_iota(jnp.int32, (16, D_MODEL), 0)
        misc = jnp.concatenate([dsg_ref[...], dgq_ref[0:1, :], dgk_ref[0:1, :], dgkc_ref[0:1, :],
                                loss_ref[...]], axis=1)
        v = jnp.where(row == V_MISC, jnp.broadcast_to(misc, (16, D_MODEL)), 0.0)
        for r, arrs in vec_rows:
            val = vecs.pop(0)[...]
            for _ in arrs[1:]:
                val = val + vecs.pop(0)[...]
            v = jnp.where(row == r, jnp.broadcast_to(val, (16, D_MODEL)), v)
        v_ref[...] = v

    return pl.pallas_call(
        body, name="pack_vectors", out_shape=jax.ShapeDtypeStruct((16, D_MODEL), F32), compiler_params=_params(),
    )(*flat, dsg, dgq, dgk, dgk_c, loss_part)


SMALL_NAMES = ("b_ada", "norm_g", "sgu_norm_g", "w_spatial", "b_spatial", "q_norm_g", "k_norm_g", "rpb")


def _adamw_small(vg, sg, rg, ws, ms, vs):
    k = len(SMALL_NAMES)

    def body(*refs):
        vg_ref, sg_ref, rg_ref = refs[0], refs[1], refs[2]
        refs = refs[1:]
        w_refs = dict(zip(SMALL_NAMES, refs[2:2 + k]))
        m_refs = dict(zip(SMALL_NAMES, refs[2 + k:2 + 2 * k]))
        v_refs = dict(zip(SMALL_NAMES, refs[2 + 2 * k:2 + 3 * k]))
        o_refs = [dict(zip(SMALL_NAMES, refs[2 + (3 + i) * k:2 + (4 + i) * k])) for i in range(4)]
        loss_ref = refs[2 + 7 * k]

        sv = vg_ref[0]
        for d in range(1, N_DEV):
            sv = sv + vg_ref[d]
        loss_ref[...] = sv[V_MISC:V_MISC + 1, 896:1024]

        def total(lo, hi, ref=sg_ref):
            s = ref[0, lo:hi, :].astype(F32)
            for d in range(1, N_DEV):
                s = s + ref[d, lo:hi, :].astype(F32)
            return s

        def emit(name, idx, g):
            res = _adam(w_refs[name][idx], g, m_refs[name][idx], v_refs[name][idx])
            for o, val in zip(o_refs, (g,) + res):
                o[name][idx] = val

        everything = (slice(None), slice(None))
        row = lambda r: sv[r:r + 1, :]
        emit("b_ada", everything, jnp.concatenate(
            [row(V_DSHIFT) + row(V_DCSHIFT), row(V_DSCALE) + row(V_DCSCALE), row(V_DGATE)], axis=1))
        emit("norm_g", everything, row(V_DNG) + row(V_DNG_CTX))
        misc = row(V_MISC)
        emit("sgu_norm_g", everything, misc[:, 0:512])
        emit("q_norm_g", everything, misc[:, 512:512 + HEAD_DIM])
        emit("k_norm_g", everything, misc[:, 640:640 + HEAD_DIM] + misc[:, 768:768 + HEAD_DIM])
        for g in range(SGU_GROUPS):
            emit("w_spatial", (0, g), total(128 * g, 128 * (g + 1)))
            emit("b_spatial", (0, slice(g, g + 1), slice(None)), total(M_DBS + 8 * g, M_DBS + 8 * (g + 1))[0:1, :])
        for hd in range(N_HEADS):
            by_dc = total(32 * hd, 32 * (hd + 1), rg_ref)
            emit("rpb", (0, hd), by_dc.T[0:2 * WIN_R - 1, 0:2 * WIN_C - 1])

    shapes = [jax.ShapeDtypeStruct(w.shape, F32) for w in ws]
    res = pl.pallas_call(body, name="adamw_small", out_shape=shapes * 4 + [jax.ShapeDtypeStruct((1, 128), F32)],
                         compiler_params=_params())(vg, sg, rg, *ws, *ms, *vs)
    return [res[i * k:(i + 1) * k] for i in range(4)], res[4 * k]


def _adamw_cctx(pc_g, w, m, v):
    def body(pc_ref, w_ref, m_ref, v_ref, g_ref, d_ref, m2_ref, v2_ref):
        pc = pc_ref[0, 0:1, :]
        for d in range(1, N_DEV):
            pc = pc + pc_ref[d, 0:1, :]
        cc = w_ref[...]
        sig = _sigmoid(cc)
        g = pc * (sig * (1.0 + cc * (1.0 - sig)))
        delta, m2, v2 = _adam(cc, g, m_ref[...], v_ref[...])
        g_ref[...] = g
        d_ref[...] = delta
        m2_ref[...] = m2
        v2_ref[...] = v2

    sh = jax.ShapeDtypeStruct((1, D_MODEL), F32)
    return pl.pallas_call(body, name="adamw_cctx", out_shape=[sh, sh, sh, sh], compiler_params=_params())(
        pc_g, w, m, v)


def _block_ones(n, blk):
    i = np.arange(n)
    return jnp.asarray((i[:, None] // blk == i[None, :] // blk).astype(np.float32), BF16)


def _rpb_pairs(rpb):
    n_off = 2 * WIN_C - 1
    cols = np.arange(GRID_W)
    c0 = np.clip(cols - WIN_C // 2, 0, GRID_W - WIN_C)
    in_win = (cols[None, :] >= c0[:, None]) & (cols[None, :] < c0[:, None] + WIN_C)
    dc = np.clip(cols[None, :] - cols[:, None] + (WIN_C - 1), 0, n_off - 1)
    expand = (dc[None] == np.arange(n_off)[:, None, None]) & in_win[None]
    toep = jnp.einsum("hrd,dqk->hrqk", rpb, jnp.asarray(expand, F32), precision=HI)
    toep = toep + jnp.asarray(np.where(in_win, 0.0, NEG_INF).astype(np.float32))
    neg = jnp.full((N_HEADS, 1, GRID_W, GRID_W), NEG_INF, F32)
    ext = jnp.concatenate([neg, toep, neg], axis=1)
    return jnp.concatenate([ext[:, :-1], ext[:, 1:]], axis=-1)


def _row_mask(rows):
    nrb = rows // Q_ROWS
    valid = np.zeros((3, Q_ROWS, 1, K_ROWS, 1), bool)
    for t, rb in enumerate((0, 1, nrb - 1)):
        kb = int(np.clip(Q_ROWS * rb - 4, 0, rows - K_ROWS))
        for i in range(Q_ROWS):
            r0 = int(np.clip(Q_ROWS * rb + i - WIN_R // 2, 0, rows - WIN_R))
            for j in range(K_ROWS):
                valid[t, i, 0, j, 0] = r0 <= kb + j < r0 + WIN_R
    full = np.broadcast_to(valid, (3, Q_ROWS, GRID_W, K_ROWS, GRID_W)).reshape(3, TQ, TK)
    return jnp.asarray(np.where(full, 0.0, NEG_INF).astype(np.float32))


def kernel(x, c, ctx, c_ctx, w_ada, b_ada, norm_g, w_in, sgu_norm_g, w_spatial, b_spatial, q_norm_g, k_norm_g, rpb, w_out, loss_target, m_c_ctx, m_w_ada, m_b_ada, m_norm_g, m_w_in, m_sgu_norm_g, m_w_spatial, m_b_spatial, m_q_norm_g, m_k_norm_g, m_rpb, m_w_out, v_c_ctx, v_w_ada, v_b_ada, v_norm_g, v_w_in, v_sgu_norm_g, v_w_spatial, v_b_spatial, v_q_norm_g, v_k_norm_g, v_rpb, v_w_out):
    me = 4 * lax.axis_index("x") + 2 * lax.axis_index("y") + lax.axis_index("c")
    x2, ctx2, tgt2 = x[0], ctx[0], loss_target[0]
    T, C = x2.shape[0], ctx2.shape[0]
    rows = T // GRID_W
    wada, win_t, wout = w_ada[0], w_in[0].T, w_out[0]
    ada_w = wada.shape[1]
    win_w = win_t.shape[0]

    row8 = lax.broadcasted_iota(jnp.int32, (8, D_MODEL), 0)
    c_blk = jnp.where(row8 == me, jnp.broadcast_to(c, (8, D_MODEL)), 0.0)
    b_sh = lax.dynamic_slice(b_ada, (0, me * ada_w), (1, ada_w))
    c_ctx_row = c_ctx.reshape(1, D_MODEL)

    ones512 = _block_ones(512, HEAD_DIM)
    ones8 = jnp.ones((8, 128), BF16)
    foldm = jnp.asarray((np.arange(512)[:, None] % HEAD_DIM == np.arange(128)[None, :]).astype(np.float32))
    lane_half = np.arange(128)[None, :, None] // GRID_W
    hsel = jnp.asarray((2 * np.arange(8)[:, None, None] + lane_half == np.arange(128)[None, None, :]).astype(np.float32),
                       BF16)
    foldr = jnp.asarray((np.arange(256)[None, :] // 8 == np.arange(32)[:, None]).astype(np.float32), BF16)
    gq512 = jnp.tile(q_norm_g, (1, N_HEADS))
    gk512 = jnp.tile(k_norm_g, (1, N_HEADS))
    ws = w_spatial[0]
    wst = ws.transpose(0, 2, 1)
    bsb = jnp.broadcast_to(b_spatial[0][:, :, None], (SGU_GROUPS, CHUNK, 128))
    pairs = _rpb_pairs(rpb[0])
    row_mask = _row_mask(rows)

    my_chip = me // 2
    order = jnp.stack([my_chip, my_chip ^ 2, my_chip ^ 1, my_chip ^ 3]).astype(jnp.int32)
    h, z, win_g, s16, part_g = _inproj_fwd(order, x2, norm_g, c_blk, c_ctx_row, wada, b_sh, win_t.astype(BF16), 512)
    w_in_b = win_g.reshape(D_IN, D_MODEL)
    mod16 = part_g.transpose(1, 0, 2).reshape(16, 3 * D_MODEL)
    mod = lax.dynamic_slice(mod16, (me, 0), (1, 3 * D_MODEL))
    shift, scale, gate = mod[:, :D_MODEL], mod[:, D_MODEL:2 * D_MODEL], mod[:, 2 * D_MODEL:]
    cshift, cscale = mod16[8:9, :D_MODEL], mod16[8:9, D_MODEL:2 * D_MODEL]
    out_a, qs, kn = _sgu_qk_fwd(z, sgu_norm_g, ws, bsb, ones512, gq512, gk512, 512)
    hc, zc, ckn = _ctx_fwd(ctx2, norm_g, cscale, cshift, w_in_b, ones512, gk512)
    wout_blk = wout.astype(BF16)
    (ob, out_b, lse), (wout_g,) = _attn_fwd(qs, kn, z, ckn, zc, pairs, row_mask, [(_Hosted("ag", wout_blk), wout_blk)])
    w_out_b = wout_g.reshape(D_MODEL, D_MODEL)

    dy, dmc, dw_out, dgate, loss_part = _outproj_loss_bwd(x2, tgt2, out_a, out_b, gate, w_out_b, 512)
    dw_out_blocks = dw_out.reshape(N_DEV, D_MODEL // N_DEV, D_MODEL)
    (dqs, dk, dv, dck, dcv, db_g, drpb), (gout_parts,) = _attn_bwd(
        qs, kn, z, ckn, zc, pairs, row_mask, ob, lse, dmc, hsel, foldr,
        [(_Hosted("a2a", dw_out_blocks), dw_out_blocks)])
    dzc_k, dzc_v, dgk_c = _ctx_k_bwd(zc, dck, dcv, ones512, gk512, foldm)
    rloc = drpb.reshape(N_HEADS * 32, 128)
    (dz, chip_sums, dws, dbs, dsg, dgq, dgk), (rg,) = _bwd_mid(
        z, dmc, dqs, dk, dv, db_g, h, hc, dzc_k, dzc_v, sgu_norm_g, ws, wst, bsb, ones8, ones512, gq512, gk512, foldm,
        512, [(_Hosted("ag", rloc), rloc)])
    sloc = jnp.concatenate([dws.reshape(SGU_GROUPS * CHUNK, CHUNK), dbs.reshape(SGU_GROUPS * 8, CHUNK)]).astype(BF16)
    (grad_x, dshift, dscale, dng), (gin_parts,) = _inproj_bwd_dx(
        [dz], 0, w_in_b, x2, dy, norm_g, scale, shift, 512, "inproj_bwd_dx",
        [(_Hosted("chips", chip_sums), chip_sums)])
    (dcshift, dcscale, dng_c), _ = _inproj_bwd_dx([dzc_k, dzc_v], 4 * 512, w_in_b, ctx2, None, norm_g, cscale,
                                                  cshift, C, "ctx_bwd_dx")

    res_in, _ = _adamw_sharded(win_t, gin_parts, m_w_in[0].T, v_w_in[0].T, 112, "adamw_w_in")
    res_out, _ = _adamw_sharded(wout, gout_parts, m_w_out[0], v_w_out[0], 128, "adamw_w_out")

    zero_row = jnp.zeros((1, D_MODEL), F32)
    vec_rows = [(V_DSHIFT, [dshift]), (V_DSCALE, [dscale]), (V_DGATE, [dgate]), (V_DCSHIFT, [dcshift]),
                (V_DCSCALE, [dcscale]), (V_ZERO, [zero_row]), (V_DNG, [dng]), (V_DNG_CTX, [dng_c])]
    vloc = _pack_vectors(vec_rows, dsg, dgq, dgk, dgk_c, loss_part)
    vg, sg = _allgather_direct([vloc, sloc], "gather_small")
    small_w =(b_ada, norm_g, sgu_norm_g, w_spatial, b_spatial, q_norm_g, k_norm_g, rpb)
    small_m = (m_b_ada, m_norm_g, m_sgu_norm_g, m_w_spatial, m_b_spatial, m_q_norm_g, m_k_norm_g, m_rpb)
    small_v = (v_b_ada, v_norm_g, v_sgu_norm_g, v_w_spatial, v_b_spatial, v_q_norm_g, v_k_norm_g, v_rpb)
    res_small, loss_row = _adamw_small(vg, sg, rg, small_w, small_m, small_v)

    dm_all = vg[:, V_DSHIFT:V_DGATE + 1, :].reshape(N_DEV, 3 * D_MODEL)
    dc_all = vg[:, V_DCSHIFT:V_ZERO + 1, :].reshape(N_DEV, 3 * D_MODEL)
    dm_sh = lax.dynamic_slice(dm_all, (0, me * ada_w), (N_DEV, ada_w))
    dc_sh = lax.dynamic_slice(dc_all, (0, me * ada_w), (N_DEV, ada_w))
    *res_ada, pc = _ada_bwd(s16, dm_sh, dc_sh, wada, m_w_ada[0], v_w_ada[0])
    (pc_g,) = _allgather_direct([pc], "gather_cctx")
    res_cctx = _adamw_cctx(pc_g, c_ctx_row, m_c_ctx.reshape(1, D_MODEL), v_c_ctx.reshape(1, D_MODEL))

    loss = loss_row[0, 0]
    outs = [loss, grad_x[None]]
    for kind in range(4):
        by_name = dict(zip(SMALL_NAMES, res_small[kind]))
        by_name.update(c_ctx=res_cctx[kind].reshape(D_MODEL), w_ada=res_ada[kind][None],
                       w_in=res_in[kind].T[None], w_out=res_out[kind][None])
        outs += [by_name[nme] for nme in ("c_ctx", "w_ada", "b_ada", "norm_g", "w_in", "sgu_norm_g", "w_spatial",
                                          "b_spatial", "q_norm_g", "k_norm_g", "rpb", "w_out")]
    return tuple(outs)
```

```python
import functools

import numpy as np
import jax
import jax.numpy as jnp
from jax import lax
from jax.experimental import pallas as pl
from jax.experimental.pallas import tpu as pltpu

F32 = jnp.float32
BF16 = jnp.bfloat16
HI = lax.Precision.HIGHEST

N_DEV = 8
D_MODEL = 1024
D_A = 512
D_B = 512
D_IN = 3584
N_BRANCH = 7
HEAD_DIM = 64
N_HEADS = 8
GRID_W = 64
WIN_R = 8
WIN_C = 16
CHUNK = 128
SGU_GROUPS = 4
EPS = 1e-6
NEG_INF = -1e30
Q_ROWS = 4
K_ROWS = 12
TQ = Q_ROWS * GRID_W
TK = K_ROWS * GRID_W
N_DIAG = 22
ATT_SUB = 16
ATT_SCALE = HEAD_DIM ** -0.5
LOG2E = 1.4426950408889634
LN2 = 0.6931471805599453

ADAM_LR = 0.001
ADAM_B1 = 0.9
ADAM_B2 = 0.999
ADAM_EPS = 1e-08
ADAM_WD = 0.01
ADAM_STEP = 10

VMEM_LIMIT = 56 * 1024 * 1024
MESH = pl.DeviceIdType.MESH

V_DSHIFT, V_DSCALE, V_DGATE, V_DCSHIFT, V_DCSCALE, V_ZERO, V_DNG, V_DNG_CTX, V_MISC = range(9)
M_DBS = 512


def _params(sem=None):
    return pltpu.CompilerParams(dimension_semantics=sem, vmem_limit_bytes=VMEM_LIMIT)


def _sigmoid(x):
    return 1.0 / (1.0 + jnp.exp(-x))


def _gelu_parts(x):
    cdf = 0.5 * (1.0 + lax.erf(x * 0.7071067811865476))
    pdf = jnp.exp(-0.5 * x * x) * 0.3989422804014327
    return x * cdf, cdf + x * pdf


def _nt(a, b):
    return lax.dot_general(a, b, (((1,), (1,)), ((), ())), preferred_element_type=F32)


def _tn(a, b):
    return lax.dot_general(a, b, (((0,), (0,)), ((), ())), preferred_element_type=F32)


def _dot2(v, ones_bf):
    hi = v.astype(BF16)
    lo = (v - hi.astype(F32)).astype(BF16)
    return (jnp.dot(hi, ones_bf, preferred_element_type=F32)
            + jnp.dot(lo, ones_bf, preferred_element_type=F32))


def _head_sum(v, ones_ref):
    return jnp.dot(v.astype(BF16), ones_ref[...], preferred_element_type=F32)


def _adam(w, g, m, v):
    m2 = ADAM_B1 * m + (1.0 - ADAM_B1) * g
    v2 = ADAM_B2 * v + (1.0 - ADAM_B2) * (g * g)
    m_hat = m2 / (1.0 - ADAM_B1 ** ADAM_STEP)
    v_hat = v2 / (1.0 - ADAM_B2 ** ADAM_STEP)
    delta = -ADAM_LR * (m_hat / (jnp.sqrt(v_hat) + ADAM_EPS) + ADAM_WD * w)
    return delta, m2, v2


class _Hosted:
    def __init__(self, kind, src):
        self.kind = kind
        n_slot = {"a2a": N_DEV, "ag": N_DEV, "chips": N_DEV // 2}[kind]
        blk = src.shape if kind == "ag" else src.shape[1:]
        self.out_shape = jax.ShapeDtypeStruct((n_slot,) + tuple(blk), src.dtype)
        self.n_peer = n_slot - 1
        self.scratch = [pltpu.SemaphoreType.DMA((self.n_peer,)), pltpu.SemaphoreType.DMA((self.n_peer,)),
                        pltpu.SemaphoreType.DMA]

    def _copies(self, src, dst, send_sems, recv_sems, loc_sem, landing):
        x, y, c = lax.axis_index("x"), lax.axis_index("y"), lax.axis_index("c")
        if self.kind == "chips":
            me = 2 * x + y
            peers = [((px, py, c), 2 * px + py) for px, py in ((1 - x, y), (x, 1 - y), (1 - x, 1 - y))]
        else:
            me = 4 * x + 2 * y + c
            peers = []
            for k in range(1, N_DEV):
                px = 1 - x if (k >> 2) & 1 else x
                py = 1 - y if (k >> 1) & 1 else y
                pc = 1 - c if k & 1 else c
                peers.append(((px, py, pc), 4 * px + 2 * py + pc))
        remote = []
        for k, (peer, pid) in enumerate(peers):
            s = src if self.kind == "ag" else src.at[pid]
            remote.append(pltpu.make_async_remote_copy(
                src_ref=s, dst_ref=dst.at[pid if landing else me],
                send_sem=send_sems.at[k], recv_sem=recv_sems.at[k], device_id=peer, device_id_type=MESH))
        local = pltpu.make_async_copy(src if self.kind == "ag" else src.at[me], dst.at[me], loc_sem)
        return remote, local

    def start(self, src, dst, send_sems, recv_sems, loc_sem):
        remote, local = self._copies(src, dst, send_sems, recv_sems, loc_sem, landing=False)
        for cp in remote:
            cp.start()
        local.start()

    def wait(self, src, dst, send_sems, recv_sems, loc_sem):
        remote, local = self._copies(src, dst, send_sems, recv_sems, loc_sem, landing=True)
        for cp in remote:
            cp.wait_recv()
        for cp in remote:
            cp.wait_send()
        local.wait()


ANY_SPEC = pl.BlockSpec(memory_space=pl.ANY)


def _allgather_direct(arrs, name):
    n = len(arrs)
    exs = [_Hosted("ag", a) for a in arrs]

    def body(*refs):
        srcs, dsts, sems = refs[:n], refs[n:2 * n], refs[2 * n:]
        for r, ex in enumerate(exs):
            ex.start(srcs[r], dsts[r], *sems[3 * r:3 * r + 3])
        for r, ex in enumerate(exs):
            ex.wait(srcs[r], dsts[r], *sems[3 * r:3 * r + 3])

    return pl.pallas_call(body, name=name, out_shape=[ex.out_shape for ex in exs], in_specs=[ANY_SPEC] * n,
                          out_specs=[ANY_SPEC] * n, scratch_shapes=[s for ex in exs for s in ex.scratch])(*arrs)


def _ada_scratch(n_col):
    return ([pltpu.VMEM((N_DEV, 8, D_MODEL), F32), pltpu.VMEM((16, n_col), F32), pltpu.VMEM((N_DEV, 16, n_col), F32)]
            + [pltpu.SemaphoreType.DMA((N_DEV - 1,)) for _ in range(4)])


def _ada_modulation(cb_ref, cc_ref, w_ref, b_ref, cstack, part, parts, s1, r1, s2, r2):
    x, y, c = lax.axis_index("x"), lax.axis_index("y"), lax.axis_index("c")
    me = 4 * x + 2 * y + c
    peers = []
    for k in range(1, N_DEV):
        px = 1 - x if (k >> 2) & 1 else x
        py = 1 - y if (k >> 1) & 1 else y
        pc = 1 - c if k & 1 else c
        peers.append(((px, py, pc), 4 * px + 2 * py + pc))

    def exchange(src, dst, send_sems, recv_sems):
        for k, (peer, _) in enumerate(peers):
            pltpu.make_async_remote_copy(src_ref=src, dst_ref=dst.at[me], send_sem=send_sems.at[k],
                                         recv_sem=recv_sems.at[k], device_id=peer, device_id_type=MESH).start()
        dst[me] = src[...]
        waits = [pltpu.make_async_remote_copy(src_ref=src, dst_ref=dst.at[pid], send_sem=send_sems.at[k],
                                              recv_sem=recv_sems.at[k], device_id=peer, device_id_type=MESH)
                 for k, (peer, pid) in enumerate(peers)]
        for cp in waits:
            cp.wait_recv()
        for cp in waits:
            cp.wait_send()

    exchange(cb_ref, cstack, s1, r1)
    c_all = cstack[0]
    for d in range(1, N_DEV):
        c_all = c_all + cstack[d]
    row = lax.broadcasted_iota(jnp.int32, (8, D_MODEL), 0)
    cc = jnp.where(row == 0, jnp.broadcast_to(cc_ref[...], (8, D_MODEL)), 0.0)
    call = jnp.concatenate([c_all, cc], axis=0)
    s = call * _sigmoid(call)
    part[...] = jnp.dot(s, w_ref[...], preferred_element_type=F32, precision=HI) + b_ref[...]
    exchange(part, parts, s2, r2)
    return s


def _ada_bwd(s16, dm, dc, w, m, v):
    def body(s_ref, dm_ref, dc_ref, w_ref, m_ref, v_ref, g_ref, d_ref, m2_ref, v2_ref, pc_ref):
        dct = jnp.sum(dc_ref[...], axis=0, keepdims=True)
        row = lax.broadcasted_iota(jnp.int32, dc_ref.shape, 0)
        dcb = jnp.where(row == 0, jnp.broadcast_to(dct, dc_ref.shape), 0.0)
        dm16 = jnp.concatenate([dm_ref[...], dcb], axis=0)
        g = lax.dot_general(s_ref[...], dm16, (((0,), (0,)), ((), ())),
                            preferred_element_type=F32, precision=HI)
        w_ = w_ref[...]
        delta, m2, v2 = _adam(w_, g, m_ref[...], v_ref[...])
        g_ref[...] = g
        d_ref[...] = delta
        m2_ref[...] = m2
        v2_ref[...] = v2
        pc_ref[...] = lax.dot_general(dcb, w_, (((1,), (1,)), ((), ())),
                                      preferred_element_type=F32, precision=HI)

    R, n_col = w.shape
    tr = 256
    sh = jax.ShapeDtypeStruct(w.shape, F32)
    rows = pl.BlockSpec((tr, n_col), lambda i: (i, 0))
    small = pl.BlockSpec(dm.shape, lambda i: (0, 0))
    return pl.pallas_call(
        body, name="ada_bwd", grid=(R // tr,),
        in_specs=[pl.BlockSpec((16, tr), lambda i: (0, i)), small, small, rows, rows, rows],
        out_specs=[rows, rows, rows, rows, pl.BlockSpec((8, tr), lambda i: (0, i))],
        out_shape=[sh, sh, sh, sh, jax.ShapeDtypeStruct((8, D_MODEL), F32)],
        compiler_params=_params(("arbitrary",)),
    )(s16, dm, dc, w, m, v)


def _head_norm(zk, ones_ref, gain):
    ss = _head_sum(zk * zk, ones_ref)
    return zk * lax.rsqrt(ss * (1.0 / HEAD_DIM) + EPS) * gain


def _inproj_fwd(order, x, ng, c_blk, c_ctx_row, w_ada_sh, b_ada_sh, w_blk_t, tm):
    T = x.shape[0]
    nt = T // tm
    n_pass = N_DEV // 2
    blk_rows = w_blk_t.shape[0]
    n_col = w_ada_sh.shape[1]

    def body(order_ref, x_ref, g_ref, cb_ref, cc_ref, wa_ref, ba_ref, wb_ref,
             h_out, z_ref, wt_out, s_out, parts_out,
             hs, wt, modv, send_sems, recv_sems, loc_sem, h_sem, wt_sem, *ada_sc):
        p, i = pl.program_id(0), pl.program_id(1)
        x, y, c = lax.axis_index("x"), lax.axis_index("y"), lax.axis_index("c")
        me, sib = (x, y, c), (x, y, 1 - c)
        chips = [(1 - x, y), (x, 1 - y), (1 - x, 1 - y)]

        def slot(px, py, pc):
            return 4 * px + 2 * py + pc

        def copy(k, block, to, src=None):
            return pltpu.make_async_remote_copy(
                src_ref=wt.at[slot(*block)] if src is None else src, dst_ref=wt.at[slot(*block)],
                send_sem=send_sems.at[k], recv_sem=recv_sems.at[k], device_id=to, device_id_type=MESH)

        own = pltpu.make_async_copy(wb_ref, wt.at[slot(*me)], loc_sem)
        h_copy = pltpu.make_async_copy(hs, h_out, h_sem)
        wt_copy = pltpu.make_async_copy(wt, wt_out, wt_sem)
        first = [copy(1 + j, me, (*chip, c), src=wb_ref) for j, chip in enumerate(chips[:2])] + [copy(0, me, sib, src=wb_ref)]
        passed = [copy(4 + j, (*chip, c), sib) for j, chip in enumerate(chips)]
        relay_src = (jnp.where(c == 0, 1 - x, x), jnp.where(c == 0, y, 1 - y), c)
        relay_dst = (jnp.where(c == 0, x, 1 - x), jnp.where(c == 0, 1 - y, y), c)
        relay = copy(3, relay_src, relay_dst)

        @pl.when(jnp.logical_and(p == 0, i == 0))
        def _():
            s_out[...] = _ada_modulation(cb_ref, cc_ref, wa_ref, ba_ref, *ada_sc)
            own.start()
            for cp in first:
                cp.start()
            parts = ada_sc[2]
            parts_out[...] = parts[...]
            my_row = pl.ds(slot(*me), 1)
            mod = jnp.concatenate([parts[d, my_row, :] for d in range(N_DEV)], axis=1)
            modv[0:1, :] = mod[:, 0:D_MODEL]
            modv[1:2, :] = mod[:, D_MODEL:2 * D_MODEL]
            own.wait()
            copy(0, sib, me).wait_recv()

        @pl.when(jnp.logical_and(p == 1, i == 0))
        def _():
            for j, chip in enumerate(chips[:2]):
                copy(1 + j, (*chip, c), me).wait_recv()
            relay.start()
            passed[0].start()
            passed[1].start()
            copy(4, (*chips[0], 1 - c), me).wait_recv()

        @pl.when(jnp.logical_and(p == 2, i == 0))
        def _():
            copy(5, (*chips[1], 1 - c), me).wait_recv()

        @pl.when(jnp.logical_and(p == 3, i == 0))
        def _():
            copy(3, (*chips[2], c), me).wait_recv()
            passed[2].start()
            copy(6, (*chips[2], 1 - c), me).wait_recv()
            wt_copy.start()

        rows = pl.ds(pl.multiple_of(i * tm, tm), tm)

        @pl.when(p == 0)
        def _():
            xv = x_ref[...]
            r = lax.rsqrt(jnp.mean(xv * xv, axis=-1, keepdims=True) + EPS)
            hs[rows, :] = ((xv * r * g_ref[...]) * (1.0 + modv[1:2, :]) + modv[0:1, :]).astype(BF16)

        @pl.when(jnp.logical_and(p == 1, i == 0))
        def _():
            h_copy.start()

        w_pair = wt[pl.ds(2 * order_ref[p], 2)].reshape(2 * blk_rows, D_MODEL)
        z_ref[...] = _nt(hs[rows, :], w_pair).astype(BF16)

        @pl.when(jnp.logical_and(p == n_pass - 1, i == nt - 1))
        def _():
            for cp in first + passed + [relay]:
                cp.wait_send()
            h_copy.wait()
            wt_copy.wait()

    whole = lambda shape: pl.BlockSpec(shape, lambda p, i, o: (0,) * len(shape))
    grid_spec = pltpu.PrefetchScalarGridSpec(
        num_scalar_prefetch=1, grid=(n_pass, nt),
        in_specs=[pl.BlockSpec((tm, D_MODEL), lambda p, i, o: (jnp.where(p == 0, i, nt - 1), 0)),
                  whole((1, D_MODEL)), whole((8, D_MODEL)), whole((1, D_MODEL)), whole((D_MODEL, n_col)),
                  whole((1, n_col)), ANY_SPEC],
        out_specs=[ANY_SPEC, pl.BlockSpec((tm, 2 * blk_rows), lambda p, i, o: (i, o[p])), ANY_SPEC,
                   whole((16, D_MODEL)), whole((N_DEV, 16, n_col))],
        scratch_shapes=[pltpu.VMEM((T, D_MODEL), BF16), pltpu.VMEM((N_DEV, blk_rows, D_MODEL), BF16),
                        pltpu.VMEM((8, D_MODEL), F32),
                        pltpu.SemaphoreType.DMA((7,)), pltpu.SemaphoreType.DMA((7,)), pltpu.SemaphoreType.DMA,
                        pltpu.SemaphoreType.DMA, pltpu.SemaphoreType.DMA] + _ada_scratch(n_col))
    return pl.pallas_call(
        body, name="inproj_fwd", grid_spec=grid_spec,
        out_shape=[jax.ShapeDtypeStruct((T, D_MODEL), BF16), jax.ShapeDtypeStruct((T, D_IN), BF16),
                   jax.ShapeDtypeStruct((N_DEV, blk_rows, D_MODEL), BF16),
                   jax.ShapeDtypeStruct((16, D_MODEL), F32), jax.ShapeDtypeStruct((N_DEV, 16, n_col), F32)],
        compiler_params=_params(("arbitrary", "arbitrary")),
    )(order, x, ng, c_blk, c_ctx_row, w_ada_sh, b_ada_sh, w_blk_t)


def _ctx_fwd(ctx, ng, cscale, cshift, w_in_t, ones_blk, gk):
    C = ctx.shape[0]

    def body(x_ref, g_ref, sc_ref, sh_ref, w_ref, ones_ref, gk_ref, h_ref, z_ref, kn_ref):
        xv = x_ref[...]
        r = lax.rsqrt(jnp.mean(xv * xv, axis=-1, keepdims=True) + EPS)
        h = (xv * r * g_ref[...]) * (1.0 + sc_ref[...]) + sh_ref[...]
        hb = h.astype(BF16)
        h_ref[...] = hb
        zk = _nt(hb, w_ref[0:512, :])
        zv = _nt(hb, w_ref[512:1024, :])
        z_ref[:, 0:512] = zk.astype(BF16)
        z_ref[:, 512:1024] = zv.astype(BF16)
        kn_ref[...] = _head_norm(zk, ones_ref, gk_ref[...]).astype(BF16)

    vec = pl.BlockSpec((1, D_MODEL), lambda i: (0, 0))
    return pl.pallas_call(
        body, name="ctx_fwd", grid=(1,),
        in_specs=[pl.BlockSpec((C, D_MODEL), lambda i: (0, 0)), vec, vec, vec,
                  pl.BlockSpec((1024, D_MODEL), lambda i: (2, 0)),
                  pl.BlockSpec((512, 512), lambda i: (0, 0)), pl.BlockSpec((1, 512), lambda i: (0, 0))],
        out_specs=[pl.BlockSpec((C, D_MODEL), lambda i: (0, 0)), pl.BlockSpec((C, 1024), lambda i: (0, 0)),
                   pl.BlockSpec((C, 512), lambda i: (0, 0))],
        out_shape=[jax.ShapeDtypeStruct((C, D_MODEL), BF16), jax.ShapeDtypeStruct((C, 1024), BF16),
                   jax.ShapeDtypeStruct((C, 512), BF16)],
        compiler_params=_params(("arbitrary",)),
    )(ctx, ng, cscale, cshift, w_in_t, ones_blk, gk)


def _sgu_chunk_fwd(au, av, ag, sg, ws_bf, bsb):
    gu, dgu = _gelu_parts(au)
    gv, dgv = _gelu_parts(av)
    rr = lax.rsqrt(jnp.mean(gv * gv, axis=-1, keepdims=True) + EPS)
    vhat = gv * rr
    vn = vhat * sg
    mixed = jnp.dot(ws_bf, vn.astype(BF16), preferred_element_type=F32) + bsb
    sig = _sigmoid(ag)
    sl = ag * sig
    return gu * mixed * sl, (gu, dgu, dgv, rr, vhat, vn, mixed, sig, sl)


def _sgu_qk_fwd(z, sgn, ws, bsb, ones_blk, gq, gk, tm):
    T = z.shape[0]

    def body(au_ref, av_ref, ag_ref, q_ref, k_ref, sg_ref, ws_ref, bsb_ref, ones_ref, gq_ref, gk_ref,
             o_ref, qs_ref, kn_ref):
        qs = _head_norm(q_ref[...].astype(F32), ones_ref, gq_ref[...]) * (ATT_SCALE * LOG2E)
        qs_ref[...] = qs.astype(BF16)
        kn_ref[...] = _head_norm(k_ref[...].astype(F32), ones_ref, gk_ref[...]).astype(BF16)
        for g in range(SGU_GROUPS):
            ws_bf = ws_ref[g].astype(BF16)
            sg = sg_ref[:, 128 * g:128 * (g + 1)]
            bsb_g = bsb_ref[g]
            for j in range(tm // CHUNK):
                rs, cs = slice(CHUNK * j, CHUNK * (j + 1)), slice(128 * g, 128 * (g + 1))
                out, _ = _sgu_chunk_fwd(au_ref[rs, cs].astype(F32), av_ref[rs, cs].astype(F32),
                                        ag_ref[rs, cs].astype(F32), sg, ws_bf, bsb_g)
                o_ref[rs, cs] = out.astype(BF16)

    zcol = lambda col: pl.BlockSpec((tm, 512), lambda i: (i, col))
    v512 = pl.BlockSpec((1, 512), lambda i: (0, 0))
    row = pl.BlockSpec((tm, 512), lambda i: (i, 0))
    out = jax.ShapeDtypeStruct((T, 512), BF16)
    return pl.pallas_call(
        body, name="sgu_qk_fwd", grid=(T // tm,),
        in_specs=[zcol(0), zcol(1), zcol(2), zcol(3), zcol(4), v512,
                  pl.BlockSpec((SGU_GROUPS, CHUNK, CHUNK), lambda i: (0, 0, 0)),
                  pl.BlockSpec((SGU_GROUPS, CHUNK, 128), lambda i: (0, 0, 0)),
                  pl.BlockSpec((512, 512), lambda i: (0, 0)), v512, v512],
        out_specs=[row, row, row], out_shape=[out, out, out],
        compiler_params=_params(("arbitrary",)),
    )(z, z, z, z, z, sgn, ws, bsb, ones_blk, gq, gk)


def _attn_type(rb, nrb):
    return jnp.where(rb == 0, 0, jnp.where(rb == nrb - 1, 2, 1))


def _attn_specs(T, C, att_sub):
    return [
        pl.BlockSpec((att_sub * TQ, 128), lambda hp, st: (st, hp)),
        pl.BlockSpec((T, 128), lambda hp, st: (0, hp)),
        pl.BlockSpec((T, 128), lambda hp, st: (0, 20 + hp)),
        pl.BlockSpec((C, 128), lambda hp, st: (0, hp)),
        pl.BlockSpec((C, 128), lambda hp, st: (0, 4 + hp)),
        pl.BlockSpec((2, 2 * WIN_R, GRID_W, 128), lambda hp, st: (hp, 0, 0, 0)),
        pl.BlockSpec((3, TQ, TK), lambda hp, st: (0, 0, 0)),
        pl.BlockSpec((att_sub * TQ, 128), lambda hp, st: (st, 24 + hp)),
    ]


def _build_bias(pairs_ref, mask_ref, bias_sc):
    for t in range(3):
        for hh in range(2):
            for i in range(Q_ROWS):
                for mm in range(K_ROWS // 2):
                    p = min(max(WIN_R - Q_ROWS * t + 2 * mm - i, 0), 2 * WIN_R - 1)
                    rs, cs = slice(GRID_W * i, GRID_W * (i + 1)), slice(128 * mm, 128 * (mm + 1))
                    bias_sc[t, hh, rs, cs] = (pairs_ref[hh, p] + mask_ref[t, rs, cs]) * LOG2E


def _attn_fwd(qs, kn, z, ckn, zc, pairs, row_mask, riders=()):
    T, C = qs.shape[0], ckn.shape[0]
    rows = T // GRID_W
    nrb = rows // Q_ROWS
    att_sub = min(ATT_SUB, nrb)
    n_st = nrb // att_sub
    n_rid = len(riders)

    def body(q_ref, k_ref, v_ref, ck_ref, cv_ref, pairs_ref, mask_ref, bg_ref, *rest):
        rid_src = rest[:n_rid]
        ob_ref, outb_ref, lse_ref = rest[n_rid:n_rid + 3]
        rid_dst = rest[n_rid + 3:2 * n_rid + 3]
        bias_sc = rest[2 * n_rid + 3]
        rid_sems = rest[2 * n_rid + 4:]

        @pl.when(jnp.logical_and(pl.program_id(0) == 0, pl.program_id(1) == 0))
        def _():
            for r, (ex, _) in enumerate(riders):
                ex.start(rid_src[r], rid_dst[r], *rid_sems[3 * r:3 * r + 3])

        @pl.when(pl.program_id(1) == 0)
        def _():
            _build_bias(pairs_ref, mask_ref, bias_sc)

        ck2, cv2 = ck_ref[...], cv_ref[...]
        lane = lax.broadcasted_iota(jnp.int32, (1, 128), 1)
        for sub in range(att_sub):
            rb = att_sub * pl.program_id(1) + sub
            bias_ref = bias_sc.at[_attn_type(rb, nrb)]
            rs = slice(TQ * sub, TQ * (sub + 1))
            ks = pl.multiple_of(jnp.clip(Q_ROWS * rb - 4, 0, rows - K_ROWS) * GRID_W, GRID_W)
            q2 = q_ref[rs, :]
            k2 = k_ref[pl.ds(ks, TK), :]
            v2 = v_ref[pl.ds(ks, TK), :]
            o_acc = jnp.zeros((TQ, 128), F32)
            lse_acc = jnp.zeros((TQ, 128), F32)
            for hh in range(2):
                msk = (lane >= HEAD_DIM) == bool(hh)
                qm = jnp.where(msk, q2, jnp.zeros_like(q2))
                s = _nt(qm, k2) + bias_ref[hh]
                sc = _nt(qm, ck2)
                m = jnp.maximum(jnp.max(s, axis=-1, keepdims=True), jnp.max(sc, axis=-1, keepdims=True))
                p = jnp.exp2(s - m)
                pc = jnp.exp2(sc - m)
                va = jnp.where(msk, v2, jnp.ones_like(v2))
                cva = jnp.where(msk, cv2, jnp.ones_like(cv2))
                num = (jnp.dot(p.astype(BF16), va, preferred_element_type=F32)
                       + jnp.dot(pc.astype(BF16), cva, preferred_element_type=F32))
                den = pltpu.roll(num, HEAD_DIM, 1)
                o_acc = jnp.where(msk, num / den, o_acc)
                lse_acc = jnp.where(msk, m + jnp.log(den) * LOG2E, lse_acc)
            ob_ref[rs, :] = o_acc.astype(BF16)
            lse_ref[rs, :] = lse_acc
            bg = bg_ref[rs, :].astype(F32)
            outb_ref[rs, :] = (o_acc * (bg * _sigmoid(bg))).astype(BF16)

        @pl.when(jnp.logical_and(pl.program_id(0) == pl.num_programs(0) - 1, pl.program_id(1) == n_st - 1))
        def _():
            for r, (ex, _) in enumerate(riders):
                ex.wait(rid_src[r], rid_dst[r], *rid_sems[3 * r:3 * r + 3])

    tile = pl.BlockSpec((att_sub * TQ, 128), lambda hp, st: (st, hp))
    res = pl.pallas_call(
        body, name="attn_fwd", grid=(4, n_st),
        in_specs=_attn_specs(T, C, att_sub) + [ANY_SPEC] * n_rid,
        out_specs=[tile, tile, tile] + [ANY_SPEC] * n_rid,
        out_shape=[jax.ShapeDtypeStruct((T, 512), BF16), jax.ShapeDtypeStruct((T, 512), BF16),
                   jax.ShapeDtypeStruct((T, 512), F32)] + [ex.out_shape for ex, _ in riders],
        scratch_shapes=[pltpu.VMEM((3, 2, TQ, TK), F32)] + [s for ex, _ in riders for s in ex.scratch],
        compiler_params=_params(("arbitrary", "arbitrary")),
    )(qs, kn, z, ckn, zc, pairs, row_mask, z, *[a for _, a in riders])
    return res[:3], res[3:]


def _outproj_loss_bwd(x, tgt, out_a, out_b, gate, w_out, tm):
    T = x.shape[0]
    nt = T // tm

    def body(x_ref, t_ref, oa_ref, ob_ref, gate_ref, w_ref, dy_ref, dmc_ref, dw_ref, dgate_ref, loss_ref, acc):
        @pl.when(pl.program_id(0) == 0)
        def _():
            acc[...] = jnp.zeros_like(acc)
            dgate_ref[...] = jnp.zeros_like(dgate_ref)
            loss_ref[...] = jnp.zeros_like(loss_ref)

        oa, ob = oa_ref[...], ob_ref[...]
        gate_v = gate_ref[...]
        mix = (jnp.dot(oa, w_ref[0:512, :], preferred_element_type=F32)
               + jnp.dot(ob, w_ref[512:1024, :], preferred_element_type=F32))
        e = x_ref[...] + gate_v * mix - t_ref[...]
        se = jnp.sum(jnp.sum(e * e, axis=0, keepdims=True), axis=1, keepdims=True)
        loss_ref[...] += jnp.broadcast_to(se * (0.5 / D_MODEL), loss_ref.shape)
        dy = e * (1.0 / D_MODEL)
        dy_ref[...] = dy
        dgate_ref[...] += jnp.sum(dy * mix, axis=0, keepdims=True)
        dmix = (dy * gate_v).astype(BF16)
        dmc_ref[...] = _nt(dmix, w_ref[...]).astype(BF16)
        acc[0:512, :] += _tn(oa, dmix)
        acc[512:1024, :] += _tn(ob, dmix)

        @pl.when(pl.program_id(0) == nt - 1)
        def _():
            dw_ref[...] = acc[...].astype(BF16)

    row = lambda w: pl.BlockSpec((tm, w), lambda i: (i, 0))
    return pl.pallas_call(
        body, name="outproj_loss_bwd", grid=(nt,),
        in_specs=[row(D_MODEL), row(D_MODEL), row(512), row(512),
                  pl.BlockSpec((1, D_MODEL), lambda i: (0, 0)),
                  pl.BlockSpec((D_MODEL, D_MODEL), lambda i: (0, 0))],
        out_specs=[row(D_MODEL), row(D_MODEL), pl.BlockSpec((D_MODEL, D_MODEL), lambda i: (0, 0)),
                   pl.BlockSpec((1, D_MODEL), lambda i: (0, 0)), pl.BlockSpec((1, 128), lambda i: (0, 0))],
        out_shape=[jax.ShapeDtypeStruct((T, D_MODEL), F32), jax.ShapeDtypeStruct((T, D_MODEL), BF16),
                   jax.ShapeDtypeStruct((D_MODEL, D_MODEL), BF16), jax.ShapeDtypeStruct((1, D_MODEL), F32),
                   jax.ShapeDtypeStruct((1, 128), F32)],
        scratch_shapes=[pltpu.VMEM((D_MODEL, D_MODEL), F32)],
        compiler_params=_params(("arbitrary",)),
    )(x, tgt, out_a, out_b, gate, w_out)


def _attn_bwd(qs, kn, z, ckn, zc, pairs, row_mask, ob, lse, dmc, hsel, fold, riders):
    T, C = qs.shape[0], ckn.shape[0]
    rows = T // GRID_W
    nrb = rows // Q_ROWS
    att_sub = min(ATT_SUB, nrb)
    n_st = nrb // att_sub
    n_rid = len(riders)

    def body(q_ref, k_ref, v_ref, ck_ref, cv_ref, pairs_ref, mask_ref, bg_ref, ob_ref, lse_ref, do_ref,
             hsel_ref, fold_ref, *rest):
        rid_src = rest[:n_rid]
        dq_ref, dk_ref, dv_ref, dck_ref, dcv_ref, dbg_ref, drpb_ref = rest[n_rid:n_rid + 7]
        rid_dst = rest[n_rid + 7:2 * n_rid + 7]
        bias_sc, dacc_ref = rest[2 * n_rid + 7:2 * n_rid + 9]
        rid_sems = rest[2 * n_rid + 9:]
        hp, st = pl.program_id(0), pl.program_id(1)

        @pl.when(jnp.logical_and(hp == 0, st == 0))
        def _():
            for r, (ex, _) in enumerate(riders):
                ex.start(rid_src[r], rid_dst[r], *rid_sems[3 * r:3 * r + 3])

        @pl.when(st == 0)
        def _():
            _build_bias(pairs_ref, mask_ref, bias_sc)
            dk_ref[...] = jnp.zeros_like(dk_ref)
            dv_ref[...] = jnp.zeros_like(dv_ref)
            dck_ref[...] = jnp.zeros_like(dck_ref)
            dcv_ref[...] = jnp.zeros_like(dcv_ref)
            dacc_ref[...] = jnp.zeros_like(dacc_ref)

        ck2, cv2 = ck_ref[...], cv_ref[...]
        lane = lax.broadcasted_iota(jnp.int32, (1, 128), 1)
        for sub in range(att_sub):
            rb = att_sub * st + sub
            bias_ref = bias_sc.at[_attn_type(rb, nrb)]
            rs = slice(TQ * sub, TQ * (sub + 1))
            kb = jnp.clip(Q_ROWS * rb - 4, 0, rows - K_ROWS)
            ks = pl.multiple_of(kb * GRID_W, GRID_W)
            ebase = kb - Q_ROWS * rb + 11
            q2 = q_ref[rs, :]
            k2 = k_ref[pl.ds(ks, TK), :]
            v2 = v_ref[pl.ds(ks, TK), :]
            bg = bg_ref[rs, :].astype(F32)
            sig = _sigmoid(bg)
            obv = ob_ref[rs, :].astype(F32)
            dout = do_ref[rs, :].astype(F32)
            dbg_ref[rs, :] = (dout * obv * (sig * (1.0 + bg * (1.0 - sig)))).astype(BF16)
            d_o = dout * (bg * sig)
            d_oo = d_o * obv
            lse2 = lse_ref[rs, :]
            dq_acc = jnp.zeros((TQ, 128), F32)
            for hh in range(2):
                msk = (lane >= HEAD_DIM) == bool(hh)
                qm = jnp.where(msk, q2, jnp.zeros_like(q2))
                lse_h = jnp.max(jnp.where(msk, lse2, -jnp.inf), axis=-1, keepdims=True)
                p = jnp.exp2(_nt(qm, k2) + bias_ref[hh] - lse_h)
                pc = jnp.exp2(_nt(qm, ck2) - lse_h)
                dom_f = jnp.where(msk, d_o, 0.0)
                dom = dom_f.astype(BF16)
                delta = jnp.sum(jnp.where(msk, d_oo, 0.0), axis=-1, keepdims=True)
                d_hi = delta.astype(BF16).astype(F32)
                x0 = HEAD_DIM * (1 - hh)
                dom_aug = jnp.where(lane == x0, -d_hi, jnp.where(lane == x0 + 1, d_hi - delta, dom_f)).astype(BF16)
                extra = jnp.logical_or(lane == x0, lane == x0 + 1)
                va = jnp.where(msk, v2, jnp.where(extra, jnp.ones_like(v2), jnp.zeros_like(v2)))
                cva = jnp.where(msk, cv2, jnp.where(extra, jnp.ones_like(cv2), jnp.zeros_like(cv2)))
                ds = p * _nt(dom_aug, va)
                dsc = pc * _nt(dom_aug, cva)
                dsb, dscb = ds.astype(BF16), dsc.astype(BF16)
                dq_h = (jnp.dot(dsb, k2, preferred_element_type=F32)
                        + jnp.dot(dscb, ck2, preferred_element_type=F32))
                dq_acc = jnp.where(msk, dq_h, dq_acc)
                dk_ref[pl.ds(ks, TK), :] += _tn(dsb, qm)
                dv_ref[pl.ds(ks, TK), :] += _tn(p.astype(BF16), dom)
                dck_ref[...] += _tn(dscb, qm)
                dcv_ref[...] += _tn(pc.astype(BF16), dom)
                for i in range(Q_ROWS):
                    for mm in range(K_ROWS // 2):
                        dacc_ref[hh, ebase + (2 * mm - i)] += ds[GRID_W * i:GRID_W * (i + 1),
                                                                 128 * mm:128 * (mm + 1)]
            dq_ref[rs, :] = dq_acc

        @pl.when(st == n_st - 1)
        def _():
            for hh in range(2):
                drpb_ref[hh] = _rpb_diag_sums(dacc_ref.at[hh], hsel_ref, fold_ref)

        @pl.when(jnp.logical_and(hp == pl.num_programs(0) - 1, st == n_st - 1))
        def _():
            for r, (ex, _) in enumerate(riders):
                ex.wait(rid_src[r], rid_dst[r], *rid_sems[3 * r:3 * r + 3])

    tile = pl.BlockSpec((att_sub * TQ, 128), lambda hp, st: (st, hp))
    colT = pl.BlockSpec((T, 128), lambda hp, st: (0, hp))
    colC = pl.BlockSpec((C, 128), lambda hp, st: (0, hp))
    res = pl.pallas_call(
        body, name="attn_bwd", grid=(4, n_st),
        in_specs=(_attn_specs(T, C, att_sub) + [tile, tile,
                                                pl.BlockSpec((att_sub * TQ, 128), lambda hp, st: (st, 4 + hp)),
                                       pl.BlockSpec((8, 128, 128), lambda hp, st: (0, 0, 0)),
                                       pl.BlockSpec((32, 256), lambda hp, st: (0, 0))]
                  + [ANY_SPEC] * n_rid),
        out_specs=([tile, colT, colT, colC, colC, tile, pl.BlockSpec((2, 32, 128), lambda hp, st: (hp, 0, 0))]
                   + [ANY_SPEC] * n_rid),
        out_shape=([jax.ShapeDtypeStruct((T, 512), F32), jax.ShapeDtypeStruct((T, 512), F32),
                    jax.ShapeDtypeStruct((T, 512), F32), jax.ShapeDtypeStruct((C, 512), F32),
                    jax.ShapeDtypeStruct((C, 512), F32), jax.ShapeDtypeStruct((T, 512), BF16),
                    jax.ShapeDtypeStruct((N_HEADS, 32, 128), F32)] + [ex.out_shape for ex, _ in riders]),
        scratch_shapes=([pltpu.VMEM((3, 2, TQ, TK), F32), pltpu.VMEM((2, N_DIAG, GRID_W, 128), F32)]
                        + [s for ex, _ in riders for s in ex.scratch]),
        compiler_params=_params(("arbitrary", "arbitrary")),
    )(qs, kn, z, ckn, zc, pairs, row_mask, z, ob, lse, dmc, hsel, fold, *[a for _, a in riders])
    return res[:7], res[7:]


def _rpb_diag_sums(a_ref, hsel_ref, fold_ref):
    n_off = 2 * WIN_C - 1
    n_dr = 2 * WIN_R - 1
    qc = lax.broadcasted_iota(jnp.int32, (GRID_W, 128), 0)
    lane = lax.broadcasted_iota(jnp.int32, (GRID_W, 128), 1)
    diff = lane % GRID_W - qc + (WIN_C - 1)
    left = lane < GRID_W

    def by_dr(dr):
        return a_ref[dr + 4] + pltpu.roll(a_ref[dr + 3], GRID_W, 1)

    out = jnp.zeros((32, 128), F32)
    for j in range((n_dr + 1) // 2):
        hi = pltpu.roll(by_dr(2 * j + 1), GRID_W, 1) if 2 * j + 1 < n_dr else 0.0
        pair = jnp.where(left, by_dr(2 * j), hi)
        parts = []
        for o in range(n_off):
            mv = jnp.where(diff == o, pair, 0.0)
            acc = mv[0:8]
            for r8 in range(1, GRID_W // 8):
                acc = acc + mv[8 * r8:8 * (r8 + 1)]
            parts.append(acc)
        parts.append(jnp.zeros((8, 128), F32))
        stack = jnp.concatenate(parts, axis=0)
        s_hi = stack.astype(BF16)
        s_lo = (stack - s_hi.astype(F32)).astype(BF16)
        per_o = (jnp.dot(fold_ref[...], s_hi, preferred_element_type=F32)
                 + jnp.dot(fold_ref[...], s_lo, preferred_element_type=F32))
        out = out + _dot2(per_o, hsel_ref[j])
    return out


def _head_norm_bwd(raw, dn, gain, ones_ref):
    rr = lax.rsqrt(_head_sum(raw * raw, ones_ref) * (1.0 / HEAD_DIM) + EPS)
    hat = raw * rr
    dgain = jnp.sum(dn * hat, axis=0, keepdims=True)
    dhat = dn * gain
    mean = _head_sum(dhat * hat, ones_ref) * (1.0 / HEAD_DIM)
    return rr * (dhat - hat * mean), dgain


def _ctx_k_bwd(zc, dck, dcv, ones_blk, gk, foldm):
    C = dck.shape[0]

    def body(bk_ref, dk_ref, dv_ref, ones_ref, gk_ref, fold_ref, dbk_ref, dbv_ref, dgk_ref):
        dbk, dgk = _head_norm_bwd(bk_ref[...].astype(F32), dk_ref[...] * LN2, gk_ref[...], ones_ref)
        dbk_ref[...] = dbk.astype(BF16)
        dbv_ref[...] = dv_ref[...].astype(BF16)
        dgk_ref[...] = jnp.dot(jnp.broadcast_to(dgk, (8, 512)), fold_ref[...],
                               preferred_element_type=F32, precision=HI)

    row = pl.BlockSpec((C, 512), lambda i: (0, 0))
    cst = lambda a, b: pl.BlockSpec((a, b), lambda i: (0, 0))
    out_row = jax.ShapeDtypeStruct((C, 512), BF16)
    return pl.pallas_call(
        body, name="ctx_k_bwd", grid=(1,),
        in_specs=[row, row, row, cst(512, 512), cst(1, 512), cst(512, 128)],
        out_specs=[row, row, cst(8, 128)],
        out_shape=[out_row, out_row, jax.ShapeDtypeStruct((8, 128), F32)],
        compiler_params=_params(("arbitrary",)),
    )(zc, dck, dcv, ones_blk, gk, foldm)


def _bwd_mid(z, dmc, dqs, dk, dv, db_g, h, hc, dzc_k, dzc_v, sgn, ws, wst, bsb, ones8, ones_blk, gq, gk, foldm, tk,
             riders=()):
    T = z.shape[0]
    nt = T // tk
    blk = D_IN // N_DEV
    n_in, n_out, n_sc = 23, 7, 9
    n_rid = len(riders)

    def body(*refs):
        (au_ref, av_ref, ag_ref, bq_ref, bk_ref, d_ref, dq_ref, dk_ref, dv_ref, dbg_ref, h_ref,
         hc_ref, dzck_ref, dzcv_ref, sg_ref, ws_ref, wst_ref, bsb_ref, ones8_ref, ones_ref, gq_ref, gk_ref,
         fold_ref) = refs[:n_in]
        rid_src = refs[n_in:n_in + n_rid]
        dz_ref, sums_out, dws_ref, dbs_ref, dsg_ref, dgq_ref, dgk_ref = refs[n_in + n_rid:n_in + n_rid + n_out]
        rid_dst = refs[n_in + n_rid + n_out:n_in + 2 * n_rid + n_out]
        (acc, accq, acck, stage, sem, send_buf, tmp, s1, r1) = refs[n_in + 2 * n_rid + n_out:
                                                                   n_in + 2 * n_rid + n_out + n_sc]
        rid_sems = refs[n_in + 2 * n_rid + n_out + n_sc:]
        t = pl.program_id(0)

        @pl.when(t == 0)
        def _():
            for r, (ex, _) in enumerate(riders):
                ex.start(rid_src[r], rid_dst[r], *rid_sems[3 * r:3 * r + 3])
            acc[...] = jnp.zeros_like(acc)
            acc[512 * 4:512 * 5, :] = _tn(dzck_ref[...], hc_ref[...])
            acc[512 * 5:512 * 6, :] = _tn(dzcv_ref[...], hc_ref[...])
            dws_ref[...] = jnp.zeros_like(dws_ref)
            dbs_ref[...] = jnp.zeros_like(dbs_ref)
            dsg_ref[...] = jnp.zeros_like(dsg_ref)
            accq[...] = jnp.zeros_like(accq)
            acck[...] = jnp.zeros_like(acck)

        for g in range(SGU_GROUPS):
            ws_bf = ws_ref[g].astype(BF16)
            wst_bf = wst_ref[g].astype(BF16)
            sg = sg_ref[:, 128 * g:128 * (g + 1)]
            bsb_g = bsb_ref[g]
            for j in range(tk // CHUNK):
                rs, cs = slice(CHUNK * j, CHUNK * (j + 1)), slice(128 * g, 128 * (g + 1))
                au, av, ag = (au_ref[rs, cs].astype(F32), av_ref[rs, cs].astype(F32), ag_ref[rs, cs].astype(F32))
                d = d_ref[rs, cs].astype(F32)
                _, (gu, dgu, dgv, rr, vhat, vn, mixed, sig, sl) = _sgu_chunk_fwd(au, av, ag, sg, ws_bf, bsb_g)
                dz_ref[rs, 128 * g:128 * (g + 1)] = (d * mixed * sl * dgu).astype(BF16)
                dz_ref[rs, 1024 + 128 * g:1024 + 128 * (g + 1)] = (
                    d * gu * mixed * (sig * (1.0 + ag * (1.0 - sig)))).astype(BF16)
                dmixed = d * gu * sl
                dmb = dmixed.astype(BF16)
                dm_lo = (dmixed - dmb.astype(F32)).astype(BF16)
                dbs_ref[g] += _nt(ones8_ref[...], dmb) + _nt(ones8_ref[...], dm_lo)
                dws_ref[g] += _nt(dmb, vn.astype(BF16))
                dvn = jnp.dot(wst_bf, dmb, preferred_element_type=F32)
                dsg_ref[:, 128 * g:128 * (g + 1)] += jnp.sum(dvn * vhat, axis=0, keepdims=True)
                dvhat = dvn * sg
                mean = jnp.mean(dvhat * vhat, axis=-1, keepdims=True)
                dz_ref[rs, 512 + 128 * g:512 + 128 * (g + 1)] = (rr * (dvhat - vhat * mean) * dgv).astype(BF16)

        dbq, dgq = _head_norm_bwd(bq_ref[...].astype(F32), dq_ref[...] * ATT_SCALE, gq_ref[...], ones_ref)
        dz_ref[:, 512 * 3:512 * 4] = dbq.astype(BF16)
        accq[...] += dgq
        dbk, dgk = _head_norm_bwd(bk_ref[...].astype(F32), dk_ref[...] * LN2, gk_ref[...], ones_ref)
        dz_ref[:, 512 * 4:512 * 5] = dbk.astype(BF16)
        acck[...] += dgk
        dz_ref[:, 512 * 5:512 * 6] = dv_ref[...].astype(BF16)
        dz_ref[:, 512 * 6:512 * 7] = dbg_ref[...]

        hv = h_ref[...]
        for k in range(N_BRANCH):
            acc[512 * k:512 * (k + 1), :] += _tn(dz_ref[:, 512 * k:512 * (k + 1)], hv)

        @pl.when(t == nt - 1)
        def _():
            dgq_ref[...] = jnp.dot(jnp.broadcast_to(accq[...], (8, 512)), fold_ref[...],
                                   preferred_element_type=F32, precision=HI)
            dgk_ref[...] = jnp.dot(jnp.broadcast_to(acck[...], (8, 512)), fold_ref[...],
                                   preferred_element_type=F32, precision=HI)
            cidx = lax.axis_index("c")
            sib = (lax.axis_index("x"), lax.axis_index("y"), 1 - cidx)
            swaps = []
            for q in range(N_DEV // 2):
                theirs = acc[pl.ds(pl.multiple_of(2 * blk * q + blk * (1 - cidx), 8), blk), :]
                send_buf[q] = theirs.astype(BF16)
                cp = pltpu.make_async_remote_copy(src_ref=send_buf.at[q], dst_ref=tmp.at[q], send_sem=s1.at[q],
                                                  recv_sem=r1.at[q], device_id=sib, device_id_type=MESH)
                cp.start()
                swaps.append(cp)
            for q in range(N_DEV // 2):
                swaps[q].wait_recv()
                mine = acc[pl.ds(pl.multiple_of(2 * blk * q + blk * cidx, 8), blk), :]
                stage[...] = (mine + tmp[q].astype(F32)).astype(BF16)
                out = pltpu.make_async_copy(stage, sums_out.at[q], sem)
                out.start()
                out.wait()
            for cp in swaps:
                cp.wait_send()
            for r, (ex, _) in enumerate(riders):
                ex.wait(rid_src[r], rid_dst[r], *rid_sems[3 * r:3 * r + 3])

    zcol = lambda col: pl.BlockSpec((tk, 512), lambda t: (t, col))
    row = pl.BlockSpec((tk, 512), lambda t: (t, 0))
    whole = lambda a: pl.BlockSpec(a.shape, lambda t: (0,) * a.ndim)
    res = pl.pallas_call(
        body, name="bwd_mid", grid=(nt,),
        in_specs=[zcol(0), zcol(1), zcol(2), zcol(3), zcol(4), row, row, row, row, row,
                  pl.BlockSpec((tk, D_MODEL), lambda t: (t, 0)), whole(hc), whole(dzc_k), whole(dzc_v),
                  whole(sgn), whole(ws), whole(wst), whole(bsb), whole(ones8), whole(ones_blk), whole(gq), whole(gk),
                  whole(foldm)] + [ANY_SPEC] * n_rid,
        out_specs=[pl.BlockSpec((tk, D_IN), lambda t: (t, 0)), ANY_SPEC,
                   pl.BlockSpec((SGU_GROUPS, CHUNK, CHUNK), lambda t: (0, 0, 0)),
                   pl.BlockSpec((SGU_GROUPS, 8, CHUNK), lambda t: (0, 0, 0)),
                   pl.BlockSpec((1, 512), lambda t: (0, 0)), pl.BlockSpec((8, 128), lambda t: (0, 0)),
                   pl.BlockSpec((8, 128), lambda t: (0, 0))] + [ANY_SPEC] * n_rid,
        out_shape=[jax.ShapeDtypeStruct((T, D_IN), BF16), jax.ShapeDtypeStruct((N_DEV // 2, blk, D_MODEL), BF16),
                   jax.ShapeDtypeStruct((SGU_GROUPS, CHUNK, CHUNK), F32),
                   jax.ShapeDtypeStruct((SGU_GROUPS, 8, CHUNK), F32), jax.ShapeDtypeStruct((1, 512), F32),
                   jax.ShapeDtypeStruct((8, 128), F32), jax.ShapeDtypeStruct((8, 128), F32)]
                  + [ex.out_shape for ex, _ in riders],
        scratch_shapes=[pltpu.VMEM((D_IN, D_MODEL), F32), pltpu.VMEM((1, 512), F32), pltpu.VMEM((1, 512), F32),
                        pltpu.VMEM((blk, D_MODEL), BF16), pltpu.SemaphoreType.DMA,
                        pltpu.VMEM((N_DEV // 2, blk, D_MODEL), BF16), pltpu.VMEM((N_DEV // 2, blk, D_MODEL), BF16),
                        pltpu.SemaphoreType.DMA((N_DEV // 2,)), pltpu.SemaphoreType.DMA((N_DEV // 2,))]
                       + [s for ex, _ in riders for s in ex.scratch],
        compiler_params=_params(("arbitrary",)),
    )(z, z, z, z, z, dmc, dqs, dk, dv, db_g, h, hc, dzc_k, dzc_v, sgn, ws, wst, bsb, ones8, ones_blk, gq, gk, foldm,
      *[a for _, a in riders])
    return res[:n_out], res[n_out:]


def _inproj_bwd_dx(dzs, w_row0, w_in_t, x, dy, ng, scale, shift, tm, name, riders=()):
    T = x.shape[0]
    n = len(dzs)
    wpc = dzs[0].shape[1]
    nt = T // tm
    with_dx = dy is not None
    n_own_in = n + 5 + with_dx
    n_own_out = 3 + with_dx
    n_rid = len(riders)

    def body(*refs):
        dz_refs = refs[:n]
        n_in = n_own_in + n_rid
        own = refs[n:n_own_in] + refs[n_in:n_in + n_own_out]
        rid_src = refs[n_own_in:n_own_in + n_rid]
        rid_dst = refs[n_in + n_own_out:n_in + n_own_out + n_rid]
        rid_sems = refs[n_in + n_own_out + n_rid:]
        if with_dx:
            w_ref, x_ref, dy_ref, g_ref, sc_ref, sh_ref, gx_ref, dsh_ref, dsc_ref, dg_ref = own
        else:
            w_ref, x_ref, g_ref, sc_ref, sh_ref, dsh_ref, dsc_ref, dg_ref = own

        @pl.when(pl.program_id(0) == 0)
        def _():
            for r, (ex, _) in enumerate(riders):
                ex.start(rid_src[r], rid_dst[r], *rid_sems[3 * r:3 * r + 3])
            dsh_ref[...] = jnp.zeros_like(dsh_ref)
            dsc_ref[...] = jnp.zeros_like(dsc_ref)
            dg_ref[...] = jnp.zeros_like(dg_ref)

        dh = jnp.dot(dz_refs[0][...], w_ref[0:wpc, :], preferred_element_type=F32)
        for k in range(1, n):
            dh = dh + jnp.dot(dz_refs[k][...], w_ref[wpc * k:wpc * (k + 1), :], preferred_element_type=F32)
        xv = x_ref[...]
        r = lax.rsqrt(jnp.mean(xv * xv, axis=-1, keepdims=True) + EPS)
        xn = xv * r
        gv, op = g_ref[...], 1.0 + sc_ref[...]
        dsh_ref[...] += jnp.sum(dh, axis=0, keepdims=True)
        dsc_ref[...] += jnp.sum(dh * xn * gv, axis=0, keepdims=True)
        dg_ref[...] += jnp.sum(dh * op * xn, axis=0, keepdims=True)
        if with_dx:
            dxn = dh * (gv * op)
            gx_ref[...] = r * (dxn - xn * jnp.mean(dxn * xn, axis=-1, keepdims=True)) + dy_ref[...]

        @pl.when(pl.program_id(0) == nt - 1)
        def _():
            for r, (ex, _) in enumerate(riders):
                ex.wait(rid_src[r], rid_dst[r], *rid_sems[3 * r:3 * r + 3])

    vec = pl.BlockSpec((1, D_MODEL), lambda i: (0, 0))
    rowf = pl.BlockSpec((tm, D_MODEL), lambda i: (i, 0))
    in_specs = [pl.BlockSpec((tm, wpc), lambda i: (i, 0))] * n
    in_specs += [pl.BlockSpec((wpc * n, D_MODEL), lambda i: (w_row0 // (wpc * n), 0)), rowf]
    args = list(dzs) + [w_in_t, x]
    vshape = jax.ShapeDtypeStruct((1, D_MODEL), F32)
    out_specs, out_shape = [vec, vec, vec], [vshape, vshape, vshape]
    if with_dx:
        in_specs.append(rowf)
        args.append(dy)
        out_specs = [rowf] + out_specs
        out_shape = [jax.ShapeDtypeStruct((T, D_MODEL), F32)] + out_shape
    in_specs += [vec, vec, vec] + [ANY_SPEC] * n_rid
    args += [ng, scale, shift] + [a for _, a in riders]
    res = pl.pallas_call(
        body, name=name, grid=(nt,), in_specs=in_specs, out_specs=out_specs + [ANY_SPEC] * n_rid,
        out_shape=out_shape + [ex.out_shape for ex, _ in riders],
        scratch_shapes=[s for ex, _ in riders for s in ex.scratch],
        compiler_params=_params(("arbitrary",)),
    )(*args)
    return res[:n_own_out], res[n_own_out:]


def _adamw_sharded(w, gparts, m, v, tr, name, riders=()):
    R, C = w.shape
    n_part = gparts.shape[0]
    nt = R // tr
    n_rid = len(riders)

    def body(w_ref, gp_ref, m_ref, v_ref, *rest):
        rid_src = rest[:n_rid]
        g_ref, d_ref, m2_ref, v2_ref = rest[n_rid:n_rid + 4]
        rid_dst = rest[n_rid + 4:2 * n_rid + 4]
        rid_sems = rest[2 * n_rid + 4:]

        @pl.when(pl.program_id(0) == 0)
        def _():
            for r, (ex, _) in enumerate(riders):
                ex.start(rid_src[r], rid_dst[r], *rid_sems[3 * r:3 * r + 3])

        g = gp_ref[0].astype(F32)
        for d in range(1, n_part):
            g = g + gp_ref[d].astype(F32)
        delta, m2, v2 = _adam(w_ref[...], g, m_ref[...], v_ref[...])
        g_ref[...] = g
        d_ref[...] = delta
        m2_ref[...] = m2
        v2_ref[...] = v2

        @pl.when(pl.program_id(0) == nt - 1)
        def _():
            for r, (ex, _) in enumerate(riders):
                ex.wait(rid_src[r], rid_dst[r], *rid_sems[3 * r:3 * r + 3])

    row = pl.BlockSpec((tr, C), lambda i: (i, 0))
    sh = jax.ShapeDtypeStruct((R, C), F32)
    res = pl.pallas_call(
        body, name=name, grid=(nt,),
        in_specs=[row, pl.BlockSpec((n_part, tr, C), lambda i: (0, i, 0)), row, row] + [ANY_SPEC] * n_rid,
        out_specs=[row, row, row, row] + [ANY_SPEC] * n_rid,
        out_shape=[sh, sh, sh, sh] + [ex.out_shape for ex, _ in riders],
        scratch_shapes=[s for ex, _ in riders for s in ex.scratch],
        compiler_params=_params(("arbitrary",)),
    )(w, gparts, m, v, *[a for _, a in riders])
    return res[:4], res[4:]


def _pack_vectors(vec_rows, dsg, dgq, dgk, dgk_c, loss_part):
    flat = [a for _, arrs in vec_rows for a in arrs]

    def body(*refs):
        vecs = list(refs[:len(flat)])
        dsg_ref, dgq_ref, dgk_ref, dgkc_ref, loss_ref, v_ref = refs[len(flat):]
        row = lax.broadcasted_iota(jnp.int32, (16, D_MODEL), 0)
        misc = jnp.concatenate([dsg_ref[...], dgq_ref[0:1, :], dgk_ref[0:1, :], dgkc_ref[0:1, :],
                                loss_ref[...]], axis=1)
        v = jnp.where(row == V_MISC, jnp.broadcast_to(misc, (16, D_MODEL)), 0.0)
        for r, arrs in vec_rows:
            val = vecs.pop(0)[...]
            for _ in arrs[1:]:
                val = val + vecs.pop(0)[...]
            v = jnp.where(row == r, jnp.broadcast_to(val, (16, D_MODEL)), v)
        v_ref[...] = v

    return pl.pallas_call(
        body, name="pack_vectors", out_shape=jax.ShapeDtypeStruct((16, D_MODEL), F32), compiler_params=_params(),
    )(*flat, dsg, dgq, dgk, dgk_c, loss_part)


SMALL_NAMES = ("b_ada", "norm_g", "sgu_norm_g", "w_spatial", "b_spatial", "q_norm_g", "k_norm_g", "rpb")


def _adamw_small(vg, sg, rg, ws, ms, vs):
    k = len(SMALL_NAMES)

    def body(*refs):
        vg_ref, sg_ref, rg_ref = refs[0], refs[1], refs[2]
        refs = refs[1:]
        w_refs = dict(zip(SMALL_NAMES, refs[2:2 + k]))
        m_refs = dict(zip(SMALL_NAMES, refs[2 + k:2 + 2 * k]))
        v_refs = dict(zip(SMALL_NAMES, refs[2 + 2 * k:2 + 3 * k]))
        o_refs = [dict(zip(SMALL_NAMES, refs[2 + (3 + i) * k:2 + (4 + i) * k])) for i in range(4)]
        loss_ref = refs[2 + 7 * k]

        sv = vg_ref[0]
        for d in range(1, N_DEV):
            sv = sv + vg_ref[d]
        loss_ref[...] = sv[V_MISC:V_MISC + 1, 896:1024]

        def total(lo, hi, ref=sg_ref):
            s = ref[0, lo:hi, :].astype(F32)
            for d in range(1, N_DEV):
                s = s + ref[d, lo:hi, :].astype(F32)
            return s

        def emit(name, idx, g):
            res = _adam(w_refs[name][idx], g, m_refs[name][idx], v_refs[name][idx])
            for o, val in zip(o_refs, (g,) + res):
                o[name][idx] = val

        everything = (slice(None), slice(None))
        row = lambda r: sv[r:r + 1, :]
        emit("b_ada", everything, jnp.concatenate(
            [row(V_DSHIFT) + row(V_DCSHIFT), row(V_DSCALE) + row(V_DCSCALE), row(V_DGATE)], axis=1))
        emit("norm_g", everything, row(V_DNG) + row(V_DNG_CTX))
        misc = row(V_MISC)
        emit("sgu_norm_g", everything, misc[:, 0:512])
        emit("q_norm_g", everything, misc[:, 512:512 + HEAD_DIM])
        emit("k_norm_g", everything, misc[:, 640:640 + HEAD_DIM] + misc[:, 768:768 + HEAD_DIM])
        for g in range(SGU_GROUPS):
            emit("w_spatial", (0, g), total(128 * g, 128 * (g + 1)))
            emit("b_spatial", (0, slice(g, g + 1), slice(None)), total(M_DBS + 8 * g, M_DBS + 8 * (g + 1))[0:1, :])
        for hd in range(N_HEADS):
            by_dc = total(32 * hd, 32 * (hd + 1), rg_ref)
            emit("rpb", (0, hd), by_dc.T[0:2 * WIN_R - 1, 0:2 * WIN_C - 1])

    shapes = [jax.ShapeDtypeStruct(w.shape, F32) for w in ws]
    res = pl.pallas_call(body, name="adamw_small", out_shape=shapes * 4 + [jax.ShapeDtypeStruct((1, 128), F32)],
                         compiler_params=_params())(vg, sg, rg, *ws, *ms, *vs)
    return [res[i * k:(i + 1) * k] for i in range(4)], res[4 * k]


def _adamw_cctx(pc_g, w, m, v):
    def body(pc_ref, w_ref, m_ref, v_ref, g_ref, d_ref, m2_ref, v2_ref):
        pc = pc_ref[0, 0:1, :]
        for d in range(1, N_DEV):
            pc = pc + pc_ref[d, 0:1, :]
        cc = w_ref[...]
        sig = _sigmoid(cc)
        g = pc * (sig * (1.0 + cc * (1.0 - sig)))
        delta, m2, v2 = _adam(cc, g, m_ref[...], v_ref[...])
        g_ref[...] = g
        d_ref[...] = delta
        m2_ref[...] = m2
        v2_ref[...] = v2

    sh = jax.ShapeDtypeStruct((1, D_MODEL), F32)
    return pl.pallas_call(body, name="adamw_cctx", out_shape=[sh, sh, sh, sh], compiler_params=_params())(
        pc_g, w, m, v)


def _block_ones(n, blk):
    i = np.arange(n)
    return jnp.asarray((i[:, None] // blk == i[None, :] // blk).astype(np.float32), BF16)


def _rpb_pairs(rpb):
    n_off = 2 * WIN_C - 1
    cols = np.arange(GRID_W)
    c0 = np.clip(cols - WIN_C // 2, 0, GRID_W - WIN_C)
    in_win = (cols[None, :] >= c0[:, None]) & (cols[None, :] < c0[:, None] + WIN_C)
    dc = np.clip(cols[None, :] - cols[:, None] + (WIN_C - 1), 0, n_off - 1)
    expand = (dc[None] == np.arange(n_off)[:, None, None]) & in_win[None]
    toep = jnp.einsum("hrd,dqk->hrqk", rpb, jnp.asarray(expand, F32), precision=HI)
    toep = toep + jnp.asarray(np.where(in_win, 0.0, NEG_INF).astype(np.float32))
    neg = jnp.full((N_HEADS, 1, GRID_W, GRID_W), NEG_INF, F32)
    ext = jnp.concatenate([neg, toep, neg], axis=1)
    return jnp.concatenate([ext[:, :-1], ext[:, 1:]], axis=-1)


def _row_mask(rows):
    nrb = rows // Q_ROWS
    valid = np.zeros((3, Q_ROWS, 1, K_ROWS, 1), bool)
    for t, rb in enumerate((0, 1, nrb - 1)):
        kb = int(np.clip(Q_ROWS * rb - 4, 0, rows - K_ROWS))
        for i in range(Q_ROWS):
            r0 = int(np.clip(Q_ROWS * rb + i - WIN_R // 2, 0, rows - WIN_R))
            for j in range(K_ROWS):
                valid[t, i, 0, j, 0] = r0 <= kb + j < r0 + WIN_R
    full = np.broadcast_to(valid, (3, Q_ROWS, GRID_W, K_ROWS, GRID_W)).reshape(3, TQ, TK)
    return jnp.asarray(np.where(full, 0.0, NEG_INF).astype(np.float32))


def kernel(x, c, ctx, c_ctx, w_ada, b_ada, norm_g, w_in, sgu_norm_g, w_spatial, b_spatial, q_norm_g, k_norm_g, rpb, w_out, loss_target, m_c_ctx, m_w_ada, m_b_ada, m_norm_g, m_w_in, m_sgu_norm_g, m_w_spatial, m_b_spatial, m_q_norm_g, m_k_norm_g, m_rpb, m_w_out, v_c_ctx, v_w_ada, v_b_ada, v_norm_g, v_w_in, v_sgu_norm_g, v_w_spatial, v_b_spatial, v_q_norm_g, v_k_norm_g, v_rpb, v_w_out):
    me = 4 * lax.axis_index("x") + 2 * lax.axis_index("y") + lax.axis_index("c")
    x2, ctx2, tgt2 = x[0], ctx[0], loss_target[0]
    T, C = x2.shape[0], ctx2.shape[0]
    rows = T // GRID_W
    wada, win_t, wout = w_ada[0], w_in[0].T, w_out[0]
    ada_w = wada.shape[1]
    win_w = win_t.shape[0]

    row8 = lax.broadcasted_iota(jnp.int32, (8, D_MODEL), 0)
    c_blk = jnp.where(row8 == me, jnp.broadcast_to(c, (8, D_MODEL)), 0.0)
    b_sh = lax.dynamic_slice(b_ada, (0, me * ada_w), (1, ada_w))
    c_ctx_row = c_ctx.reshape(1, D_MODEL)

    ones512 = _block_ones(512, HEAD_DIM)
    ones8 = jnp.ones((8, 128), BF16)
    foldm = jnp.asarray((np.arange(512)[:, None] % HEAD_DIM == np.arange(128)[None, :]).astype(np.float32))
    lane_half = np.arange(128)[None, :, None] // GRID_W
    hsel = jnp.asarray((2 * np.arange(8)[:, None, None] + lane_half == np.arange(128)[None, None, :]).astype(np.float32),
                       BF16)
    foldr = jnp.asarray((np.arange(256)[None, :] // 8 == np.arange(32)[:, None]).astype(np.float32), BF16)
    gq512 = jnp.tile(q_norm_g, (1, N_HEADS))
    gk512 = jnp.tile(k_norm_g, (1, N_HEADS))
    ws = w_spatial[0]
    wst = ws.transpose(0, 2, 1)
    bsb = jnp.broadcast_to(b_spatial[0][:, :, None], (SGU_GROUPS, CHUNK, 128))
    pairs = _rpb_pairs(rpb[0])
    row_mask = _row_mask(rows)

    my_chip = me // 2
    order = jnp.stack([my_chip, my_chip ^ 2, my_chip ^ 1, my_chip ^ 3]).astype(jnp.int32)
    h, z, win_g, s16, part_g = _inproj_fwd(order, x2, norm_g, c_blk, c_ctx_row, wada, b_sh, win_t.astype(BF16), 1024)
    w_in_b = win_g.reshape(D_IN, D_MODEL)
    mod16 = part_g.transpose(1, 0, 2).reshape(16, 3 * D_MODEL)
    mod = lax.dynamic_slice(mod16, (me, 0), (1, 3 * D_MODEL))
    shift, scale, gate = mod[:, :D_MODEL], mod[:, D_MODEL:2 * D_MODEL], mod[:, 2 * D_MODEL:]
    cshift, cscale = mod16[8:9, :D_MODEL], mod16[8:9, D_MODEL:2 * D_MODEL]
    out_a, qs, kn = _sgu_qk_fwd(z, sgu_norm_g, ws, bsb, ones512, gq512, gk512, 1024)
    hc, zc, ckn = _ctx_fwd(ctx2, norm_g, cscale, cshift, w_in_b, ones512, gk512)
    wout_blk = wout.astype(BF16)
    (ob, out_b, lse), (wout_g,) = _attn_fwd(qs, kn, z, ckn, zc, pairs, row_mask, [(_Hosted("ag", wout_blk), wout_blk)])
    w_out_b = wout_g.reshape(D_MODEL, D_MODEL)

    dy, dmc, dw_out, dgate, loss_part = _outproj_loss_bwd(x2, tgt2, out_a, out_b, gate, w_out_b, 512)
    dw_out_blocks = dw_out.reshape(N_DEV, D_MODEL // N_DEV, D_MODEL)
    (dqs, dk, dv, dck, dcv, db_g, drpb), (gout_parts,) = _attn_bwd(
        qs, kn, z, ckn, zc, pairs, row_mask, ob, lse, dmc, hsel, foldr,
        [(_Hosted("a2a", dw_out_blocks), dw_out_blocks)])
    dzc_k, dzc_v, dgk_c = _ctx_k_bwd(zc, dck, dcv, ones512, gk512, foldm)
    rloc = drpb.reshape(N_HEADS * 32, 128)
    (dz, chip_sums, dws, dbs, dsg, dgq, dgk), (rg,) = _bwd_mid(
        z, dmc, dqs, dk, dv, db_g, h, hc, dzc_k, dzc_v, sgu_norm_g, ws, wst, bsb, ones8, ones512, gq512, gk512, foldm,
        512, [(_Hosted("ag", rloc), rloc)])
    sloc = jnp.concatenate([dws.reshape(SGU_GROUPS * CHUNK, CHUNK), dbs.reshape(SGU_GROUPS * 8, CHUNK)]).astype(BF16)
    (grad_x, dshift, dscale, dng), (gin_parts,) = _inproj_bwd_dx(
        [dz], 0, w_in_b, x2, dy, norm_g, scale, shift, 512, "inproj_bwd_dx",
        [(_Hosted("chips", chip_sums), chip_sums)])
    (dcshift, dcscale, dng_c), _ = _inproj_bwd_dx([dzc_k, dzc_v], 4 * 512, w_in_b, ctx2, None, norm_g, cscale,
                                                  cshift, C, "ctx_bwd_dx")

    res_in, _ = _adamw_sharded(win_t, gin_parts, m_w_in[0].T, v_w_in[0].T, 112, "adamw_w_in")
    res_out, _ = _adamw_sharded(wout, gout_parts, m_w_out[0], v_w_out[0], 128, "adamw_w_out")

    zero_row = jnp.zeros((1, D_MODEL), F32)
    vec_rows = [(V_DSHIFT, [dshift]), (V_DSCALE, [dscale]), (V_DGATE, [dgate]), (V_DCSHIFT, [dcshift]),
                (V_DCSCALE, [dcscale]), (V_ZERO, [zero_row]), (V_DNG, [dng]), (V_DNG_CTX, [dng_c])]
    vloc = _pack_vectors(vec_rows, dsg, dgq, dgk, dgk_c, loss_part)
    vg, sg = _allgather_direct([vloc, sloc], "gather_small")
    small_w =(b_ada, norm_g, sgu_norm_g, w_spatial, b_spatial, q_norm_g, k_norm_g, rpb)
    small_m = (m_b_ada, m_norm_g, m_sgu_norm_g, m_w_spatial, m_b_spatial, m_q_norm_g, m_k_norm_g, m_rpb)
    small_v = (v_b_ada, v_norm_g, v_sgu_norm_g, v_w_spatial, v_b_spatial, v_q_norm_g, v_k_norm_g, v_rpb)
    res_small, loss_row = _adamw_small(vg, sg, rg, small_w, small_m, small_v)

    dm_all = vg[:, V_DSHIFT:V_DGATE + 1, :].reshape(N_DEV, 3 * D_MODEL)
    dc_all = vg[:, V_DCSHIFT:V_ZERO + 1, :].reshape(N_DEV, 3 * D_MODEL)
    dm_sh = lax.dynamic_slice(dm_all, (0, me * ada_w), (N_DEV, ada_w))
    dc_sh = lax.dynamic_slice(dc_all, (0, me * ada_w), (N_DEV, ada_w))
    *res_ada, pc = _ada_bwd(s16, dm_sh, dc_sh, wada, m_w_ada[0], v_w_ada[0])
    (pc_g,) = _allgather_direct([pc], "gather_cctx")
    res_cctx = _adamw_cctx(pc_g, c_ctx_row, m_c_ctx.reshape(1, D_MODEL), v_c_ctx.reshape(1, D_MODEL))

    loss = loss_row[0, 0]
    outs = [loss, grad_x[None]]
    for kind in range(4):
        by_name = dict(zip(SMALL_NAMES, res_small[kind]))
        by_name.update(c_ctx=res_cctx[kind].reshape(D_MODEL), w_ada=res_ada[kind][None],
                       w_in=res_in[kind].T[None], w_out=res_out[kind][None])
        outs += [by_name[nme] for nme in ("c_ctx", "w_ada", "b_ada", "norm_g", "w_in", "sgu_norm_g", "w_spatial",
                                          "b_spatial", "q_norm_g", "k_norm_g", "rpb", "w_out")]
    return tuple(outs)
```

```python
import functools

import numpy as np
import jax
import jax.numpy as jnp
from jax import lax
from jax.experimental import pallas as pl
from jax.experimental.pallas import tpu as pltpu

F32 = jnp.float32
BF16 = jnp.bfloat16
HI = lax.Precision.HIGHEST

N_DEV = 8
D_MODEL = 1024
D_A = 512
D_B = 512
D_IN = 3584
N_BRANCH = 7
HEAD_DIM = 64
N_HEADS = 8
GRID_W = 64
WIN_R = 8
WIN_C = 16
CHUNK = 128
SGU_GROUPS = 4
EPS = 1e-6
NEG_INF = -1e30
Q_ROWS = 4
K_ROWS = 12
TQ = Q_ROWS * GRID_W
TK = K_ROWS * GRID_W
N_DIAG = 22
ATT_SUB = 8
ATT_SCALE = HEAD_DIM ** -0.5
LOG2E = 1.4426950408889634
LN2 = 0.6931471805599453

ADAM_LR = 0.001
ADAM_B1 = 0.9
ADAM_B2 = 0.999
ADAM_EPS = 1e-08
ADAM_WD = 0.01
ADAM_STEP = 10

VMEM_LIMIT = 56 * 1024 * 1024
MESH = pl.DeviceIdType.MESH

V_DSHIFT, V_DSCALE, V_DGATE, V_DCSHIFT, V_DCSCALE, V_ZERO, V_DNG, V_DNG_CTX, V_MISC = range(9)
M_DBS = 512


def _params(sem=None):
    return pltpu.CompilerParams(dimension_semantics=sem, vmem_limit_bytes=VMEM_LIMIT)


def _sigmoid(x):
    return 1.0 / (1.0 + jnp.exp(-x))


def _gelu_parts(x):
    cdf = 0.5 * (1.0 + lax.erf(x * 0.7071067811865476))
    pdf = jnp.exp(-0.5 * x * x) * 0.3989422804014327
    return x * cdf, cdf + x * pdf


def _nt(a, b):
    return lax.dot_general(a, b, (((1,), (1,)), ((), ())), preferred_element_type=F32)


def _tn(a, b):
    return lax.dot_general(a, b, (((0,), (0,)), ((), ())), preferred_element_type=F32)


def _dot2(v, ones_bf):
    hi = v.astype(BF16)
    lo = (v - hi.astype(F32)).astype(BF16)
    return (jnp.dot(hi, ones_bf, preferred_element_type=F32)
            + jnp.dot(lo, ones_bf, preferred_element_type=F32))


def _head_sum(v, ones_ref):
    return jnp.dot(v.astype(BF16), ones_ref[...], preferred_element_type=F32)


def _adam(w, g, m, v):
    m2 = ADAM_B1 * m + (1.0 - ADAM_B1) * g
    v2 = ADAM_B2 * v + (1.0 - ADAM_B2) * (g * g)
    m_hat = m2 / (1.0 - ADAM_B1 ** ADAM_STEP)
    v_hat = v2 / (1.0 - ADAM_B2 ** ADAM_STEP)
    delta = -ADAM_LR * (m_hat / (jnp.sqrt(v_hat) + ADAM_EPS) + ADAM_WD * w)
    return delta, m2, v2


class _Hosted:
    def __init__(self, kind, src):
        self.kind = kind
        n_slot = {"a2a": N_DEV, "ag": N_DEV, "chips": N_DEV // 2}[kind]
        blk = src.shape if kind == "ag" else src.shape[1:]
        self.out_shape = jax.ShapeDtypeStruct((n_slot,) + tuple(blk), src.dtype)
        self.n_peer = n_slot - 1
        self.scratch = [pltpu.SemaphoreType.DMA((self.n_peer,)), pltpu.SemaphoreType.DMA((self.n_peer,)),
                        pltpu.SemaphoreType.DMA]

    def _copies(self, src, dst, send_sems, recv_sems, loc_sem, landing):
        x, y, c = lax.axis_index("x"), lax.axis_index("y"), lax.axis_index("c")
        if self.kind == "chips":
            me = 2 * x + y
            peers = [((px, py, c), 2 * px + py) for px, py in ((1 - x, y), (x, 1 - y), (1 - x, 1 - y))]
        else:
            me = 4 * x + 2 * y + c
            peers = []
            for k in range(1, N_DEV):
                px = 1 - x if (k >> 2) & 1 else x
                py = 1 - y if (k >> 1) & 1 else y
                pc = 1 - c if k & 1 else c
                peers.append(((px, py, pc), 4 * px + 2 * py + pc))
        remote = []
        for k, (peer, pid) in enumerate(peers):
            s = src if self.kind == "ag" else src.at[pid]
            remote.append(pltpu.make_async_remote_copy(
                src_ref=s, dst_ref=dst.at[pid if landing else me],
                send_sem=send_sems.at[k], recv_sem=recv_sems.at[k], device_id=peer, device_id_type=MESH))
        local = pltpu.make_async_copy(src if self.kind == "ag" else src.at[me], dst.at[me], loc_sem)
        return remote, local

    def start(self, src, dst, send_sems, recv_sems, loc_sem):
        remote, local = self._copies(src, dst, send_sems, recv_sems, loc_sem, landing=False)
        for cp in remote:
            cp.start()
        local.start()

    def wait(self, src, dst, send_sems, recv_sems, loc_sem):
        remote, local = self._copies(src, dst, send_sems, recv_sems, loc_sem, landing=True)
        for cp in remote:
            cp.wait_recv()
        for cp in remote:
            cp.wait_send()
        local.wait()


ANY_SPEC = pl.BlockSpec(memory_space=pl.ANY)


def _allgather_direct(arrs, name):
    n = len(arrs)
    exs = [_Hosted("ag", a) for a in arrs]

    def body(*refs):
        srcs, dsts, sems = refs[:n], refs[n:2 * n], refs[2 * n:]
        for r, ex in enumerate(exs):
            ex.start(srcs[r], dsts[r], *sems[3 * r:3 * r + 3])
        for r, ex in enumerate(exs):
            ex.wait(srcs[r], dsts[r], *sems[3 * r:3 * r + 3])

    return pl.pallas_call(body, name=name, out_shape=[ex.out_shape for ex in exs], in_specs=[ANY_SPEC] * n,
                          out_specs=[ANY_SPEC] * n, scratch_shapes=[s for ex in exs for s in ex.scratch])(*arrs)


def _ada_scratch(n_col):
    return ([pltpu.VMEM((N_DEV, 8, D_MODEL), F32), pltpu.VMEM((16, n_col), F32), pltpu.VMEM((N_DEV, 16, n_col), F32)]
            + [pltpu.SemaphoreType.DMA((N_DEV - 1,)) for _ in range(4)])


def _ada_modulation(cb_ref, cc_ref, w_ref, b_ref, cstack, part, parts, s1, r1, s2, r2):
    x, y, c = lax.axis_index("x"), lax.axis_index("y"), lax.axis_index("c")
    me = 4 * x + 2 * y + c
    peers = []
    for k in range(1, N_DEV):
        px = 1 - x if (k >> 2) & 1 else x
        py = 1 - y if (k >> 1) & 1 else y
        pc = 1 - c if k & 1 else c
        peers.append(((px, py, pc), 4 * px + 2 * py + pc))

    def exchange(src, dst, send_sems, recv_sems):
        for k, (peer, _) in enumerate(peers):
            pltpu.make_async_remote_copy(src_ref=src, dst_ref=dst.at[me], send_sem=send_sems.at[k],
                                         recv_sem=recv_sems.at[k], device_id=peer, device_id_type=MESH).start()
        dst[me] = src[...]
        waits = [pltpu.make_async_remote_copy(src_ref=src, dst_ref=dst.at[pid], send_sem=send_sems.at[k],
                                              recv_sem=recv_sems.at[k], device_id=peer, device_id_type=MESH)
                 for k, (peer, pid) in enumerate(peers)]
        for cp in waits:
            cp.wait_recv()
        for cp in waits:
            cp.wait_send()

    exchange(cb_ref, cstack, s1, r1)
    c_all = cstack[0]
    for d in range(1, N_DEV):
        c_all = c_all + cstack[d]
    row = lax.broadcasted_iota(jnp.int32, (8, D_MODEL), 0)
    cc = jnp.where(row == 0, jnp.broadcast_to(cc_ref[...], (8, D_MODEL)), 0.0)
    call = jnp.concatenate([c_all, cc], axis=0)
    s = call * _sigmoid(call)
    part[...] = jnp.dot(s, w_ref[...], preferred_element_type=F32, precision=HI) + b_ref[...]
    exchange(part, parts, s2, r2)
    return s


def _ada_bwd(s16, dm, dc, w, m, v):
    def body(s_ref, dm_ref, dc_ref, w_ref, m_ref, v_ref, g_ref, d_ref, m2_ref, v2_ref, pc_ref):
        dct = jnp.sum(dc_ref[...], axis=0, keepdims=True)
        row = lax.broadcasted_iota(jnp.int32, dc_ref.shape, 0)
        dcb = jnp.where(row == 0, jnp.broadcast_to(dct, dc_ref.shape), 0.0)
        dm16 = jnp.concatenate([dm_ref[...], dcb], axis=0)
        g = lax.dot_general(s_ref[...], dm16, (((0,), (0,)), ((), ())),
                            preferred_element_type=F32, precision=HI)
        w_ = w_ref[...]
        delta, m2, v2 = _adam(w_, g, m_ref[...], v_ref[...])
        g_ref[...] = g
        d_ref[...] = delta
        m2_ref[...] = m2
        v2_ref[...] = v2
        pc_ref[...] = lax.dot_general(dcb, w_, (((1,), (1,)), ((), ())),
                                      preferred_element_type=F32, precision=HI)

    R, n_col = w.shape
    tr = 256
    sh = jax.ShapeDtypeStruct(w.shape, F32)
    rows = pl.BlockSpec((tr, n_col), lambda i: (i, 0))
    small = pl.BlockSpec(dm.shape, lambda i: (0, 0))
    return pl.pallas_call(
        body, name="ada_bwd", grid=(R // tr,),
        in_specs=[pl.BlockSpec((16, tr), lambda i: (0, i)), small, small, rows, rows, rows],
        out_specs=[rows, rows, rows, rows, pl.BlockSpec((8, tr), lambda i: (0, i))],
        out_shape=[sh, sh, sh, sh, jax.ShapeDtypeStruct((8, D_MODEL), F32)],
        compiler_params=_params(("arbitrary",)),
    )(s16, dm, dc, w, m, v)


def _head_norm(zk, ones_ref, gain):
    ss = _head_sum(zk * zk, ones_ref)
    return zk * lax.rsqrt(ss * (1.0 / HEAD_DIM) + EPS) * gain


def _inproj_fwd(order, x, ng, c_blk, c_ctx_row, w_ada_sh, b_ada_sh, w_blk_t, tm):
    T = x.shape[0]
    nt = T // tm
    n_pass = N_DEV // 2
    blk_rows = w_blk_t.shape[0]
    n_col = w_ada_sh.shape[1]

    def body(order_ref, x_ref, g_ref, cb_ref, cc_ref, wa_ref, ba_ref, wb_ref,
             h_out, z_ref, wt_out, s_out, parts_out,
             hs, wt, modv, send_sems, recv_sems, loc_sem, h_sem, wt_sem, *ada_sc):
        p, i = pl.program_id(0), pl.program_id(1)
        x, y, c = lax.axis_index("x"), lax.axis_index("y"), lax.axis_index("c")
        me, sib = (x, y, c), (x, y, 1 - c)
        chips = [(1 - x, y), (x, 1 - y), (1 - x, 1 - y)]

        def slot(px, py, pc):
            return 4 * px + 2 * py + pc

        def copy(k, block, to, src=None):
            return pltpu.make_async_remote_copy(
                src_ref=wt.at[slot(*block)] if src is None else src, dst_ref=wt.at[slot(*block)],
                send_sem=send_sems.at[k], recv_sem=recv_sems.at[k], device_id=to, device_id_type=MESH)

        own = pltpu.make_async_copy(wb_ref, wt.at[slot(*me)], loc_sem)
        h_copy = pltpu.make_async_copy(hs, h_out, h_sem)
        wt_copy = pltpu.make_async_copy(wt, wt_out, wt_sem)
        first = [copy(1 + j, me, (*chip, c), src=wb_ref) for j, chip in enumerate(chips[:2])] + [copy(0, me, sib, src=wb_ref)]
        passed = [copy(4 + j, (*chip, c), sib) for j, chip in enumerate(chips)]
        relay_src = (jnp.where(c == 0, 1 - x, x), jnp.where(c == 0, y, 1 - y), c)
        relay_dst = (jnp.where(c == 0, x, 1 - x), jnp.where(c == 0, 1 - y, y), c)
        relay = copy(3, relay_src, relay_dst)

        @pl.when(jnp.logical_and(p == 0, i == 0))
        def _():
            s_out[...] = _ada_modulation(cb_ref, cc_ref, wa_ref, ba_ref, *ada_sc)
            own.start()
            for cp in first:
                cp.start()
            parts = ada_sc[2]
            parts_out[...] = parts[...]
            my_row = pl.ds(slot(*me), 1)
            mod = jnp.concatenate([parts[d, my_row, :] for d in range(N_DEV)], axis=1)
            modv[0:1, :] = mod[:, 0:D_MODEL]
            modv[1:2, :] = mod[:, D_MODEL:2 * D_MODEL]
            own.wait()
            copy(0, sib, me).wait_recv()

        @pl.when(jnp.logical_and(p == 1, i == 0))
        def _():
            for j, chip in enumerate(chips[:2]):
                copy(1 + j, (*chip, c), me).wait_recv()
            relay.start()
            passed[0].start()
            passed[1].start()
            copy(4, (*chips[0], 1 - c), me).wait_recv()

        @pl.when(jnp.logical_and(p == 2, i == 0))
        def _():
            copy(5, (*chips[1], 1 - c), me).wait_recv()

        @pl.when(jnp.logical_and(p == 3, i == 0))
        def _():
            copy(3, (*chips[2], c), me).wait_recv()
            passed[2].start()
            copy(6, (*chips[2], 1 - c), me).wait_recv()
            wt_copy.start()

        rows = pl.ds(pl.multiple_of(i * tm, tm), tm)

        @pl.when(p == 0)
        def _():
            xv = x_ref[...]
            r = lax.rsqrt(jnp.mean(xv * xv, axis=-1, keepdims=True) + EPS)
            hs[rows, :] = ((xv * r * g_ref[...]) * (1.0 + modv[1:2, :]) + modv[0:1, :]).astype(BF16)

        @pl.when(jnp.logical_and(p == 1, i == 0))
        def _():
            h_copy.start()

        w_pair = wt[pl.ds(2 * order_ref[p], 2)].reshape(2 * blk_rows, D_MODEL)
        z_ref[...] = _nt(hs[rows, :], w_pair).astype(BF16)

        @pl.when(jnp.logical_and(p == n_pass - 1, i == nt - 1))
        def _():
            for cp in first + passed + [relay]:
                cp.wait_send()
            h_copy.wait()
            wt_copy.wait()

    whole = lambda shape: pl.BlockSpec(shape, lambda p, i, o: (0,) * len(shape))
    grid_spec = pltpu.PrefetchScalarGridSpec(
        num_scalar_prefetch=1, grid=(n_pass, nt),
        in_specs=[pl.BlockSpec((tm, D_MODEL), lambda p, i, o: (jnp.where(p == 0, i, nt - 1), 0)),
                  whole((1, D_MODEL)), whole((8, D_MODEL)), whole((1, D_MODEL)), whole((D_MODEL, n_col)),
                  whole((1, n_col)), ANY_SPEC],
        out_specs=[ANY_SPEC, pl.BlockSpec((tm, 2 * blk_rows), lambda p, i, o: (i, o[p])), ANY_SPEC,
                   whole((16, D_MODEL)), whole((N_DEV, 16, n_col))],
        scratch_shapes=[pltpu.VMEM((T, D_MODEL), BF16), pltpu.VMEM((N_DEV, blk_rows, D_MODEL), BF16),
                        pltpu.VMEM((8, D_MODEL), F32),
                        pltpu.SemaphoreType.DMA((7,)), pltpu.SemaphoreType.DMA((7,)), pltpu.SemaphoreType.DMA,
                        pltpu.SemaphoreType.DMA, pltpu.SemaphoreType.DMA] + _ada_scratch(n_col))
    return pl.pallas_call(
        body, name="inproj_fwd", grid_spec=grid_spec,
        out_shape=[jax.ShapeDtypeStruct((T, D_MODEL), BF16), jax.ShapeDtypeStruct((T, D_IN), BF16),
                   jax.ShapeDtypeStruct((N_DEV, blk_rows, D_MODEL), BF16),
                   jax.ShapeDtypeStruct((16, D_MODEL), F32), jax.ShapeDtypeStruct((N_DEV, 16, n_col), F32)],
        compiler_params=_params(("arbitrary", "arbitrary")),
    )(order, x, ng, c_blk, c_ctx_row, w_ada_sh, b_ada_sh, w_blk_t)


def _ctx_fwd(ctx, ng, cscale, cshift, w_in_t, ones_blk, gk):
    C = ctx.shape[0]

    def body(x_ref, g_ref, sc_ref, sh_ref, w_ref, ones_ref, gk_ref, h_ref, z_ref, kn_ref):
        xv = x_ref[...]
        r = lax.rsqrt(jnp.mean(xv * xv, axis=-1, keepdims=True) + EPS)
        h = (xv * r * g_ref[...]) * (1.0 + sc_ref[...]) + sh_ref[...]
        hb = h.astype(BF16)
        h_ref[...] = hb
        zk = _nt(hb, w_ref[0:512, :])
        zv = _nt(hb, w_ref[512:1024, :])
        z_ref[:, 0:512] = zk.astype(BF16)
        z_ref[:, 512:1024] = zv.astype(BF16)
        kn_ref[...] = _head_norm(zk, ones_ref, gk_ref[...]).astype(BF16)

    vec = pl.BlockSpec((1, D_MODEL), lambda i: (0, 0))
    return pl.pallas_call(
        body, name="ctx_fwd", grid=(1,),
        in_specs=[pl.BlockSpec((C, D_MODEL), lambda i: (0, 0)), vec, vec, vec,
                  pl.BlockSpec((1024, D_MODEL), lambda i: (2, 0)),
                  pl.BlockSpec((512, 512), lambda i: (0, 0)), pl.BlockSpec((1, 512), lambda i: (0, 0))],
        out_specs=[pl.BlockSpec((C, D_MODEL), lambda i: (0, 0)), pl.BlockSpec((C, 1024), lambda i: (0, 0)),
                   pl.BlockSpec((C, 512), lambda i: (0, 0))],
        out_shape=[jax.ShapeDtypeStruct((C, D_MODEL), BF16), jax.ShapeDtypeStruct((C, 1024), BF16),
                   jax.ShapeDtypeStruct((C, 512), BF16)],
        compiler_params=_params(("arbitrary",)),
    )(ctx, ng, cscale, cshift, w_in_t, ones_blk, gk)


def _sgu_chunk_fwd(au, av, ag, sg, ws_bf, bsb):
    gu, dgu = _gelu_parts(au)
    gv, dgv = _gelu_parts(av)
    rr = lax.rsqrt(jnp.mean(gv * gv, axis=-1, keepdims=True) + EPS)
    vhat = gv * rr
    vn = vhat * sg
    mixed = jnp.dot(ws_bf, vn.astype(BF16), preferred_element_type=F32) + bsb
    sig = _sigmoid(ag)
    sl = ag * sig
    return gu * mixed * sl, (gu, dgu, dgv, rr, vhat, vn, mixed, sig, sl)


def _sgu_qk_fwd(z, sgn, ws, bsb, ones_blk, gq, gk, tm):
    T = z.shape[0]

    def body(au_ref, av_ref, ag_ref, q_ref, k_ref, sg_ref, ws_ref, bsb_ref, ones_ref, gq_ref, gk_ref,
             o_ref, qs_ref, kn_ref):
        qs = _head_norm(q_ref[...].astype(F32), ones_ref, gq_ref[...]) * (ATT_SCALE * LOG2E)
        qs_ref[...] = qs.astype(BF16)
        kn_ref[...] = _head_norm(k_ref[...].astype(F32), ones_ref, gk_ref[...]).astype(BF16)
        for g in range(SGU_GROUPS):
            ws_bf = ws_ref[g].astype(BF16)
            sg = sg_ref[:, 128 * g:128 * (g + 1)]
            bsb_g = bsb_ref[g]
            for j in range(tm // CHUNK):
                rs, cs = slice(CHUNK * j, CHUNK * (j + 1)), slice(128 * g, 128 * (g + 1))
                out, _ = _sgu_chunk_fwd(au_ref[rs, cs].astype(F32), av_ref[rs, cs].astype(F32),
                                        ag_ref[rs, cs].astype(F32), sg, ws_bf, bsb_g)
                o_ref[rs, cs] = out.astype(BF16)

    zcol = lambda col: pl.BlockSpec((tm, 512), lambda i: (i, col))
    v512 = pl.BlockSpec((1, 512), lambda i: (0, 0))
    row = pl.BlockSpec((tm, 512), lambda i: (i, 0))
    out = jax.ShapeDtypeStruct((T, 512), BF16)
    return pl.pallas_call(
        body, name="sgu_qk_fwd", grid=(T // tm,),
        in_specs=[zcol(0), zcol(1), zcol(2), zcol(3), zcol(4), v512,
                  pl.BlockSpec((SGU_GROUPS, CHUNK, CHUNK), lambda i: (0, 0, 0)),
                  pl.BlockSpec((SGU_GROUPS, CHUNK, 128), lambda i: (0, 0, 0)),
                  pl.BlockSpec((512, 512), lambda i: (0, 0)), v512, v512],
        out_specs=[row, row, row], out_shape=[out, out, out],
        compiler_params=_params(("arbitrary",)),
    )(z, z, z, z, z, sgn, ws, bsb, ones_blk, gq, gk)


def _attn_type(rb, nrb):
    return jnp.where(rb == 0, 0, jnp.where(rb == nrb - 1, 2, 1))


def _attn_specs(T, C, att_sub):
    return [
        pl.BlockSpec((att_sub * TQ, 128), lambda hp, st: (st, hp)),
        pl.BlockSpec((T, 128), lambda hp, st: (0, hp)),
        pl.BlockSpec((T, 128), lambda hp, st: (0, 20 + hp)),
        pl.BlockSpec((C, 128), lambda hp, st: (0, hp)),
        pl.BlockSpec((C, 128), lambda hp, st: (0, 4 + hp)),
        pl.BlockSpec((2, 2 * WIN_R, GRID_W, 128), lambda hp, st: (hp, 0, 0, 0)),
        pl.BlockSpec((3, TQ, TK), lambda hp, st: (0, 0, 0)),
        pl.BlockSpec((att_sub * TQ, 128), lambda hp, st: (st, 24 + hp)),
    ]


def _build_bias(pairs_ref, mask_ref, bias_sc):
    for t in range(3):
        for hh in range(2):
            for i in range(Q_ROWS):
                for mm in range(K_ROWS // 2):
                    p = min(max(WIN_R - Q_ROWS * t + 2 * mm - i, 0), 2 * WIN_R - 1)
                    rs, cs = slice(GRID_W * i, GRID_W * (i + 1)), slice(128 * mm, 128 * (mm + 1))
                    bias_sc[t, hh, rs, cs] = (pairs_ref[hh, p] + mask_ref[t, rs, cs]) * LOG2E


def _attn_fwd(qs, kn, z, ckn, zc, pairs, row_mask, riders=()):
    T, C = qs.shape[0], ckn.shape[0]
    rows = T // GRID_W
    nrb = rows // Q_ROWS
    att_sub = min(ATT_SUB, nrb)
    n_st = nrb // att_sub
    n_rid = len(riders)

    def body(q_ref, k_ref, v_ref, ck_ref, cv_ref, pairs_ref, mask_ref, bg_ref, *rest):
        rid_src = rest[:n_rid]
        ob_ref, outb_ref, lse_ref = rest[n_rid:n_rid + 3]
        rid_dst = rest[n_rid + 3:2 * n_rid + 3]
        bias_sc = rest[2 * n_rid + 3]
        rid_sems = rest[2 * n_rid + 4:]

        @pl.when(jnp.logical_and(pl.program_id(0) == 0, pl.program_id(1) == 0))
        def _():
            for r, (ex, _) in enumerate(riders):
                ex.start(rid_src[r], rid_dst[r], *rid_sems[3 * r:3 * r + 3])

        @pl.when(pl.program_id(1) == 0)
        def _():
            _build_bias(pairs_ref, mask_ref, bias_sc)

        ck2, cv2 = ck_ref[...], cv_ref[...]
        lane = lax.broadcasted_iota(jnp.int32, (1, 128), 1)
        for sub in range(att_sub):
            rb = att_sub * pl.program_id(1) + sub
            bias_ref = bias_sc.at[_attn_type(rb, nrb)]
            rs = slice(TQ * sub, TQ * (sub + 1))
            ks = pl.multiple_of(jnp.clip(Q_ROWS * rb - 4, 0, rows - K_ROWS) * GRID_W, GRID_W)
            q2 = q_ref[rs, :]
            k2 = k_ref[pl.ds(ks, TK), :]
            v2 = v_ref[pl.ds(ks, TK), :]
            o_acc = jnp.zeros((TQ, 128), F32)
            lse_acc = jnp.zeros((TQ, 128), F32)
            for hh in range(2):
                msk = (lane >= HEAD_DIM) == bool(hh)
                qm = jnp.where(msk, q2, jnp.zeros_like(q2))
                s = _nt(qm, k2) + bias_ref[hh]
                sc = _nt(qm, ck2)
                m = jnp.maximum(jnp.max(s, axis=-1, keepdims=True), jnp.max(sc, axis=-1, keepdims=True))
                p = jnp.exp2(s - m)
                pc = jnp.exp2(sc - m)
                va = jnp.where(msk, v2, jnp.ones_like(v2))
                cva = jnp.where(msk, cv2, jnp.ones_like(cv2))
                num = (jnp.dot(p.astype(BF16), va, preferred_element_type=F32)
                       + jnp.dot(pc.astype(BF16), cva, preferred_element_type=F32))
                den = pltpu.roll(num, HEAD_DIM, 1)
                o_acc = jnp.where(msk, num / den, o_acc)
                lse_acc = jnp.where(msk, m + jnp.log(den) * LOG2E, lse_acc)
            ob_ref[rs, :] = o_acc.astype(BF16)
            lse_ref[rs, :] = lse_acc
            bg = bg_ref[rs, :].astype(F32)
            outb_ref[rs, :] = (o_acc * (bg * _sigmoid(bg))).astype(BF16)

        @pl.when(jnp.logical_and(pl.program_id(0) == pl.num_programs(0) - 1, pl.program_id(1) == n_st - 1))
        def _():
            for r, (ex, _) in enumerate(riders):
                ex.wait(rid_src[r], rid_dst[r], *rid_sems[3 * r:3 * r + 3])

    tile = pl.BlockSpec((att_sub * TQ, 128), lambda hp, st: (st, hp))
    res = pl.pallas_call(
        body, name="attn_fwd", grid=(4, n_st),
        in_specs=_attn_specs(T, C, att_sub) + [ANY_SPEC] * n_rid,
        out_specs=[tile, tile, tile] + [ANY_SPEC] * n_rid,
        out_shape=[jax.ShapeDtypeStruct((T, 512), BF16), jax.ShapeDtypeStruct((T, 512), BF16),
                   jax.ShapeDtypeStruct((T, 512), F32)] + [ex.out_shape for ex, _ in riders],
        scratch_shapes=[pltpu.VMEM((3, 2, TQ, TK), F32)] + [s for ex, _ in riders for s in ex.scratch],
        compiler_params=_params(("arbitrary", "arbitrary")),
    )(qs, kn, z, ckn, zc, pairs, row_mask, z, *[a for _, a in riders])
    return res[:3], res[3:]


def _outproj_loss_bwd(x, tgt, out_a, out_b, gate, w_out, tm):
    T = x.shape[0]
    nt = T // tm

    def body(x_ref, t_ref, oa_ref, ob_ref, gate_ref, w_ref, dy_ref, dmc_ref, dw_ref, dgate_ref, loss_ref, acc):
        @pl.when(pl.program_id(0) == 0)
        def _():
            acc[...] = jnp.zeros_like(acc)
            dgate_ref[...] = jnp.zeros_like(dgate_ref)
            loss_ref[...] = jnp.zeros_like(loss_ref)

        oa, ob = oa_ref[...], ob_ref[...]
        gate_v = gate_ref[...]
        mix = (jnp.dot(oa, w_ref[0:512, :], preferred_element_type=F32)
               + jnp.dot(ob, w_ref[512:1024, :], preferred_element_type=F32))
        e = x_ref[...] + gate_v * mix - t_ref[...]
        se = jnp.sum(jnp.sum(e * e, axis=0, keepdims=True), axis=1, keepdims=True)
        loss_ref[...] += jnp.broadcast_to(se * (0.5 / D_MODEL), loss_ref.shape)
        dy = e * (1.0 / D_MODEL)
        dy_ref[...] = dy
        dgate_ref[...] += jnp.sum(dy * mix, axis=0, keepdims=True)
        dmix = (dy * gate_v).astype(BF16)
        dmc_ref[...] = _nt(dmix, w_ref[...]).astype(BF16)
        acc[0:512, :] += _tn(oa, dmix)
        acc[512:1024, :] += _tn(ob, dmix)

        @pl.when(pl.program_id(0) == nt - 1)
        def _():
            dw_ref[...] = acc[...].astype(BF16)

    row = lambda w: pl.BlockSpec((tm, w), lambda i: (i, 0))
    return pl.pallas_call(
        body, name="outproj_loss_bwd", grid=(nt,),
        in_specs=[row(D_MODEL), row(D_MODEL), row(512), row(512),
                  pl.BlockSpec((1, D_MODEL), lambda i: (0, 0)),
                  pl.BlockSpec((D_MODEL, D_MODEL), lambda i: (0, 0))],
        out_specs=[row(D_MODEL), row(D_MODEL), pl.BlockSpec((D_MODEL, D_MODEL), lambda i: (0, 0)),
                   pl.BlockSpec((1, D_MODEL), lambda i: (0, 0)), pl.BlockSpec((1, 128), lambda i: (0, 0))],
        out_shape=[jax.ShapeDtypeStruct((T, D_MODEL), F32), jax.ShapeDtypeStruct((T, D_MODEL), BF16),
                   jax.ShapeDtypeStruct((D_MODEL, D_MODEL), BF16), jax.ShapeDtypeStruct((1, D_MODEL), F32),
                   jax.ShapeDtypeStruct((1, 128), F32)],
        scratch_shapes=[pltpu.VMEM((D_MODEL, D_MODEL), F32)],
        compiler_params=_params(("arbitrary",)),
    )(x, tgt, out_a, out_b, gate, w_out)


def _attn_bwd(qs, kn, z, ckn, zc, pairs, row_mask, ob, lse, dmc, hsel, fold, riders):
    T, C = qs.shape[0], ckn.shape[0]
    rows = T // GRID_W
    nrb = rows // Q_ROWS
    att_sub = min(ATT_SUB, nrb)
    n_st = nrb // att_sub
    n_rid = len(riders)

    def body(q_ref, k_ref, v_ref, ck_ref, cv_ref, pairs_ref, mask_ref, bg_ref, ob_ref, lse_ref, do_ref,
             hsel_ref, fold_ref, *rest):
        rid_src = rest[:n_rid]
        dq_ref, dk_ref, dv_ref, dck_ref, dcv_ref, dbg_ref, drpb_ref = rest[n_rid:n_rid + 7]
        rid_dst = rest[n_rid + 7:2 * n_rid + 7]
        bias_sc, dacc_ref = rest[2 * n_rid + 7:2 * n_rid + 9]
        rid_sems = rest[2 * n_rid + 9:]
        hp, st = pl.program_id(0), pl.program_id(1)

        @pl.when(jnp.logical_and(hp == 0, st == 0))
        def _():
            for r, (ex, _) in enumerate(riders):
                ex.start(rid_src[r], rid_dst[r], *rid_sems[3 * r:3 * r + 3])

        @pl.when(st == 0)
        def _():
            _build_bias(pairs_ref, mask_ref, bias_sc)
            dk_ref[...] = jnp.zeros_like(dk_ref)
            dv_ref[...] = jnp.zeros_like(dv_ref)
            dck_ref[...] = jnp.zeros_like(dck_ref)
            dcv_ref[...] = jnp.zeros_like(dcv_ref)
            dacc_ref[...] = jnp.zeros_like(dacc_ref)

        ck2, cv2 = ck_ref[...], cv_ref[...]
        lane = lax.broadcasted_iota(jnp.int32, (1, 128), 1)
        for sub in range(att_sub):
            rb = att_sub * st + sub
            bias_ref = bias_sc.at[_attn_type(rb, nrb)]
            rs = slice(TQ * sub, TQ * (sub + 1))
            kb = jnp.clip(Q_ROWS * rb - 4, 0, rows - K_ROWS)
            ks = pl.multiple_of(kb * GRID_W, GRID_W)
            ebase = kb - Q_ROWS * rb + 11
            q2 = q_ref[rs, :]
            k2 = k_ref[pl.ds(ks, TK), :]
            v2 = v_ref[pl.ds(ks, TK), :]
            bg = bg_ref[rs, :].astype(F32)
            sig = _sigmoid(bg)
            obv = ob_ref[rs, :].astype(F32)
            dout = do_ref[rs, :].astype(F32)
            dbg_ref[rs, :] = (dout * obv * (sig * (1.0 + bg * (1.0 - sig)))).astype(BF16)
            d_o = dout * (bg * sig)
            d_oo = d_o * obv
            lse2 = lse_ref[rs, :]
            dq_acc = jnp.zeros((TQ, 128), F32)
            for hh in range(2):
                msk = (lane >= HEAD_DIM) == bool(hh)
                qm = jnp.where(msk, q2, jnp.zeros_like(q2))
                lse_h = jnp.max(jnp.where(msk, lse2, -jnp.inf), axis=-1, keepdims=True)
                p = jnp.exp2(_nt(qm, k2) + bias_ref[hh] - lse_h)
                pc = jnp.exp2(_nt(qm, ck2) - lse_h)
                dom_f = jnp.where(msk, d_o, 0.0)
                dom = dom_f.astype(BF16)
                delta = jnp.sum(jnp.where(msk, d_oo, 0.0), axis=-1, keepdims=True)
                d_hi = delta.astype(BF16).astype(F32)
                x0 = HEAD_DIM * (1 - hh)
                dom_aug = jnp.where(lane == x0, -d_hi, jnp.where(lane == x0 + 1, d_hi - delta, dom_f)).astype(BF16)
                extra = jnp.logical_or(lane == x0, lane == x0 + 1)
                va = jnp.where(msk, v2, jnp.where(extra, jnp.ones_like(v2), jnp.zeros_like(v2)))
                cva = jnp.where(msk, cv2, jnp.where(extra, jnp.ones_like(cv2), jnp.zeros_like(cv2)))
                ds = p * _nt(dom_aug, va)
                dsc = pc * _nt(dom_aug, cva)
                dsb, dscb = ds.astype(BF16), dsc.astype(BF16)
                dq_h = (jnp.dot(dsb, k2, preferred_element_type=F32)
                        + jnp.dot(dscb, ck2, preferred_element_type=F32))
                dq_acc = jnp.where(msk, dq_h, dq_acc)
                dk_ref[pl.ds(ks, TK), :] += _tn(dsb, qm)
                dv_ref[pl.ds(ks, TK), :] += _tn(p.astype(BF16), dom)
                dck_ref[...] += _tn(dscb, qm)
                dcv_ref[...] += _tn(pc.astype(BF16), dom)
                for i in range(Q_ROWS):
                    for mm in range(K_ROWS // 2):
                        dacc_ref[hh, ebase + (2 * mm - i)] += ds[GRID_W * i:GRID_W * (i + 1),
                                                                 128 * mm:128 * (mm + 1)]
            dq_ref[rs, :] = dq_acc

        @pl.when(st == n_st - 1)
        def _():
            for hh in range(2):
                drpb_ref[hh] = _rpb_diag_sums(dacc_ref.at[hh], hsel_ref, fold_ref)

        @pl.when(jnp.logical_and(hp == pl.num_programs(0) - 1, st == n_st - 1))
        def _():
            for r, (ex, _) in enumerate(riders):
                ex.wait(rid_src[r], rid_dst[r], *rid_sems[3 * r:3 * r + 3])

    tile = pl.BlockSpec((att_sub * TQ, 128), lambda hp, st: (st, hp))
    colT = pl.BlockSpec((T, 128), lambda hp, st: (0, hp))
    colC = pl.BlockSpec((C, 128), lambda hp, st: (0, hp))
    res = pl.pallas_call(
        body, name="attn_bwd", grid=(4, n_st),
        in_specs=(_attn_specs(T, C, att_sub) + [tile, tile,
                                                pl.BlockSpec((att_sub * TQ, 128), lambda hp, st: (st, 4 + hp)),
                                       pl.BlockSpec((8, 128, 128), lambda hp, st: (0, 0, 0)),
                                       pl.BlockSpec((32, 256), lambda hp, st: (0, 0))]
                  + [ANY_SPEC] * n_rid),
        out_specs=([tile, colT, colT, colC, colC, tile, pl.BlockSpec((2, 32, 128), lambda hp, st: (hp, 0, 0))]
                   + [ANY_SPEC] * n_rid),
        out_shape=([jax.ShapeDtypeStruct((T, 512), F32), jax.ShapeDtypeStruct((T, 512), F32),
                    jax.ShapeDtypeStruct((T, 512), F32), jax.ShapeDtypeStruct((C, 512), F32),
                    jax.ShapeDtypeStruct((C, 512), F32), jax.ShapeDtypeStruct((T, 512), BF16),
                    jax.ShapeDtypeStruct((N_HEADS, 32, 128), F32)] + [ex.out_shape for ex, _ in riders]),
        scratch_shapes=([pltpu.VMEM((3, 2, TQ, TK), F32), pltpu.VMEM((2, N_DIAG, GRID_W, 128), F32)]
                        + [s for ex, _ in riders for s in ex.scratch]),
        compiler_params=_params(("arbitrary", "arbitrary")),
    )(qs, kn, z, ckn, zc, pairs, row_mask, z, ob, lse, dmc, hsel, fold, *[a for _, a in riders])
    return res[:7], res[7:]


def _rpb_diag_sums(a_ref, hsel_ref, fold_ref):
    n_off = 2 * WIN_C - 1
    n_dr = 2 * WIN_R - 1
    qc = lax.broadcasted_iota(jnp.int32, (GRID_W, 128), 0)
    lane = lax.broadcasted_iota(jnp.int32, (GRID_W, 128), 1)
    diff = lane % GRID_W - qc + (WIN_C - 1)
    left = lane < GRID_W

    def by_dr(dr):
        return a_ref[dr + 4] + pltpu.roll(a_ref[dr + 3], GRID_W, 1)

    out = jnp.zeros((32, 128), F32)
    for j in range((n_dr + 1) // 2):
        hi = pltpu.roll(by_dr(2 * j + 1), GRID_W, 1) if 2 * j + 1 < n_dr else 0.0
        pair = jnp.where(left, by_dr(2 * j), hi)
        parts = []
        for o in range(n_off):
            mv = jnp.where(diff == o, pair, 0.0)
            acc = mv[0:8]
            for r8 in range(1, GRID_W // 8):
                acc = acc + mv[8 * r8:8 * (r8 + 1)]
            parts.append(acc)
        parts.append(jnp.zeros((8, 128), F32))
        stack = jnp.concatenate(parts, axis=0)
        s_hi = stack.astype(BF16)
        s_lo = (stack - s_hi.astype(F32)).astype(BF16)
        per_o = (jnp.dot(fold_ref[...], s_hi, preferred_element_type=F32)
                 + jnp.dot(fold_ref[...], s_lo, preferred_element_type=F32))
        out = out + _dot2(per_o, hsel_ref[j])
    return out


def _head_norm_bwd(raw, dn, gain, ones_ref):
    rr = lax.rsqrt(_head_sum(raw * raw, ones_ref) * (1.0 / HEAD_DIM) + EPS)
    hat = raw * rr
    dgain = jnp.sum(dn * hat, axis=0, keepdims=True)
    dhat = dn * gain
    mean = _head_sum(dhat * hat, ones_ref) * (1.0 / HEAD_DIM)
    return rr * (dhat - hat * mean), dgain


def _ctx_k_bwd(zc, dck, dcv, ones_blk, gk, foldm):
    C = dck.shape[0]

    def body(bk_ref, dk_ref, dv_ref, ones_ref, gk_ref, fold_ref, dbk_ref, dbv_ref, dgk_ref):
        dbk, dgk = _head_norm_bwd(bk_ref[...].astype(F32), dk_ref[...] * LN2, gk_ref[...], ones_ref)
        dbk_ref[...] = dbk.astype(BF16)
        dbv_ref[...] = dv_ref[...].astype(BF16)
        dgk_ref[...] = jnp.dot(jnp.broadcast_to(dgk, (8, 512)), fold_ref[...],
                               preferred_element_type=F32, precision=HI)

    row = pl.BlockSpec((C, 512), lambda i: (0, 0))
    cst = lambda a, b: pl.BlockSpec((a, b), lambda i: (0, 0))
    out_row = jax.ShapeDtypeStruct((C, 512), BF16)
    return pl.pallas_call(
        body, name="ctx_k_bwd", grid=(1,),
        in_specs=[row, row, row, cst(512, 512), cst(1, 512), cst(512, 128)],
        out_specs=[row, row, cst(8, 128)],
        out_shape=[out_row, out_row, jax.ShapeDtypeStruct((8, 128), F32)],
        compiler_params=_params(("arbitrary",)),
    )(zc, dck, dcv, ones_blk, gk, foldm)


def _bwd_mid(z, dmc, dqs, dk, dv, db_g, h, hc, dzc_k, dzc_v, sgn, ws, wst, bsb, ones8, ones_blk, gq, gk, foldm, tk,
             riders=()):
    T = z.shape[0]
    nt = T // tk
    blk = D_IN // N_DEV
    n_in, n_out, n_sc = 23, 7, 9
    n_rid = len(riders)

    def body(*refs):
        (au_ref, av_ref, ag_ref, bq_ref, bk_ref, d_ref, dq_ref, dk_ref, dv_ref, dbg_ref, h_ref,
         hc_ref, dzck_ref, dzcv_ref, sg_ref, ws_ref, wst_ref, bsb_ref, ones8_ref, ones_ref, gq_ref, gk_ref,
         fold_ref) = refs[:n_in]
        rid_src = refs[n_in:n_in + n_rid]
        dz_ref, sums_out, dws_ref, dbs_ref, dsg_ref, dgq_ref, dgk_ref = refs[n_in + n_rid:n_in + n_rid + n_out]
        rid_dst = refs[n_in + n_rid + n_out:n_in + 2 * n_rid + n_out]
        (acc, accq, acck, stage, sem, send_buf, tmp, s1, r1) = refs[n_in + 2 * n_rid + n_out:
                                                                   n_in + 2 * n_rid + n_out + n_sc]
        rid_sems = refs[n_in + 2 * n_rid + n_out + n_sc:]
        t = pl.program_id(0)

        @pl.when(t == 0)
        def _():
            for r, (ex, _) in enumerate(riders):
                ex.start(rid_src[r], rid_dst[r], *rid_sems[3 * r:3 * r + 3])
            acc[...] = jnp.zeros_like(acc)
            acc[512 * 4:512 * 5, :] = _tn(dzck_ref[...], hc_ref[...])
            acc[512 * 5:512 * 6, :] = _tn(dzcv_ref[...], hc_ref[...])
            dws_ref[...] = jnp.zeros_like(dws_ref)
            dbs_ref[...] = jnp.zeros_like(dbs_ref)
            dsg_ref[...] = jnp.zeros_like(dsg_ref)
            accq[...] = jnp.zeros_like(accq)
            acck[...] = jnp.zeros_like(acck)

        for g in range(SGU_GROUPS):
            ws_bf = ws_ref[g].astype(BF16)
            wst_bf = wst_ref[g].astype(BF16)
            sg = sg_ref[:, 128 * g:128 * (g + 1)]
            bsb_g = bsb_ref[g]
            for j in range(tk // CHUNK):
                rs, cs = slice(CHUNK * j, CHUNK * (j + 1)), slice(128 * g, 128 * (g + 1))
                au, av, ag = (au_ref[rs, cs].astype(F32), av_ref[rs, cs].astype(F32), ag_ref[rs, cs].astype(F32))
                d = d_ref[rs, cs].astype(F32)
                _, (gu, dgu, dgv, rr, vhat, vn, mixed, sig, sl) = _sgu_chunk_fwd(au, av, ag, sg, ws_bf, bsb_g)
                dz_ref[rs, 128 * g:128 * (g + 1)] = (d * mixed * sl * dgu).astype(BF16)
                dz_ref[rs, 1024 + 128 * g:1024 + 128 * (g + 1)] = (
                    d * gu * mixed * (sig * (1.0 + ag * (1.0 - sig)))).astype(BF16)
                dmixed = d * gu * sl
                dmb = dmixed.astype(BF16)
                dm_lo = (dmixed - dmb.astype(F32)).astype(BF16)
                dbs_ref[g] += _nt(ones8_ref[...], dmb) + _nt(ones8_ref[...], dm_lo)
                dws_ref[g] += _nt(dmb, vn.astype(BF16))
                dvn = jnp.dot(wst_bf, dmb, preferred_element_type=F32)
                dsg_ref[:, 128 * g:128 * (g + 1)] += jnp.sum(dvn * vhat, axis=0, keepdims=True)
                dvhat = dvn * sg
                mean = jnp.mean(dvhat * vhat, axis=-1, keepdims=True)
                dz_ref[rs, 512 + 128 * g:512 + 128 * (g + 1)] = (rr * (dvhat - vhat * mean) * dgv).astype(BF16)

        dbq, dgq = _head_norm_bwd(bq_ref[...].astype(F32), dq_ref[...] * ATT_SCALE, gq_ref[...], ones_ref)
        dz_ref[:, 512 * 3:512 * 4] = dbq.astype(BF16)
        accq[...] += dgq
        dbk, dgk = _head_norm_bwd(bk_ref[...].astype(F32), dk_ref[...] * LN2, gk_ref[...], ones_ref)
        dz_ref[:, 512 * 4:512 * 5] = dbk.astype(BF16)
        acck[...] += dgk
        dz_ref[:, 512 * 5:512 * 6] = dv_ref[...].astype(BF16)
        dz_ref[:, 512 * 6:512 * 7] = dbg_ref[...]

        hv = h_ref[...]
        for k in range(N_BRANCH):
            acc[512 * k:512 * (k + 1), :] += _tn(dz_ref[:, 512 * k:512 * (k + 1)], hv)

        @pl.when(t == nt - 1)
        def _():
            dgq_ref[...] = jnp.dot(jnp.broadcast_to(accq[...], (8, 512)), fold_ref[...],
                                   preferred_element_type=F32, precision=HI)
            dgk_ref[...] = jnp.dot(jnp.broadcast_to(acck[...], (8, 512)), fold_ref[...],
                                   preferred_element_type=F32, precision=HI)
            cidx = lax.axis_index("c")
            sib = (lax.axis_index("x"), lax.axis_index("y"), 1 - cidx)
            swaps = []
            for q in range(N_DEV // 2):
                theirs = acc[pl.ds(pl.multiple_of(2 * blk * q + blk * (1 - cidx), 8), blk), :]
                send_buf[q] = theirs.astype(BF16)
                cp = pltpu.make_async_remote_copy(src_ref=send_buf.at[q], dst_ref=tmp.at[q], send_sem=s1.at[q],
                                                  recv_sem=r1.at[q], device_id=sib, device_id_type=MESH)
                cp.start()
                swaps.append(cp)
            for q in range(N_DEV // 2):
                swaps[q].wait_recv()
                mine = acc[pl.ds(pl.multiple_of(2 * blk * q + blk * cidx, 8), blk), :]
                stage[...] = (mine + tmp[q].astype(F32)).astype(BF16)
                out = pltpu.make_async_copy(stage, sums_out.at[q], sem)
                out.start()
                out.wait()
            for cp in swaps:
                cp.wait_send()
            for r, (ex, _) in enumerate(riders):
                ex.wait(rid_src[r], rid_dst[r], *rid_sems[3 * r:3 * r + 3])

    zcol = lambda col: pl.BlockSpec((tk, 512), lambda t: (t, col))
    row = pl.BlockSpec((tk, 512), lambda t: (t, 0))
    whole = lambda a: pl.BlockSpec(a.shape, lambda t: (0,) * a.ndim)
    res = pl.pallas_call(
        body, name="bwd_mid", grid=(nt,),
        in_specs=[zcol(0), zcol(1), zcol(2), zcol(3), zcol(4), row, row, row, row, row,
                  pl.BlockSpec((tk, D_MODEL), lambda t: (t, 0)), whole(hc), whole(dzc_k), whole(dzc_v),
                  whole(sgn), whole(ws), whole(wst), whole(bsb), whole(ones8), whole(ones_blk), whole(gq), whole(gk),
                  whole(foldm)] + [ANY_SPEC] * n_rid,
        out_specs=[pl.BlockSpec((tk, D_IN), lambda t: (t, 0)), ANY_SPEC,
                   pl.BlockSpec((SGU_GROUPS, CHUNK, CHUNK), lambda t: (0, 0, 0)),
                   pl.BlockSpec((SGU_GROUPS, 8, CHUNK), lambda t: (0, 0, 0)),
                   pl.BlockSpec((1, 512), lambda t: (0, 0)), pl.BlockSpec((8, 128), lambda t: (0, 0)),
                   pl.BlockSpec((8, 128), lambda t: (0, 0))] + [ANY_SPEC] * n_rid,
        out_shape=[jax.ShapeDtypeStruct((T, D_IN), BF16), jax.ShapeDtypeStruct((N_DEV // 2, blk, D_MODEL), BF16),
                   jax.ShapeDtypeStruct((SGU_GROUPS, CHUNK, CHUNK), F32),
                   jax.ShapeDtypeStruct((SGU_GROUPS, 8, CHUNK), F32), jax.ShapeDtypeStruct((1, 512), F32),
                   jax.ShapeDtypeStruct((8, 128), F32), jax.ShapeDtypeStruct((8, 128), F32)]
                  + [ex.out_shape for ex, _ in riders],
        scratch_shapes=[pltpu.VMEM((D_IN, D_MODEL), F32), pltpu.VMEM((1, 512), F32), pltpu.VMEM((1, 512), F32),
                        pltpu.VMEM((blk, D_MODEL), BF16), pltpu.SemaphoreType.DMA,
                        pltpu.VMEM((N_DEV // 2, blk, D_MODEL), BF16), pltpu.VMEM((N_DEV // 2, blk, D_MODEL), BF16),
                        pltpu.SemaphoreType.DMA((N_DEV // 2,)), pltpu.SemaphoreType.DMA((N_DEV // 2,))]
                       + [s for ex, _ in riders for s in ex.scratch],
        compiler_params=_params(("arbitrary",)),
    )(z, z, z, z, z, dmc, dqs, dk, dv, db_g, h, hc, dzc_k, dzc_v, sgn, ws, wst, bsb, ones8, ones_blk, gq, gk, foldm,
      *[a for _, a in riders])
    return res[:n_out], res[n_out:]


def _inproj_bwd_dx(dzs, w_row0, w_in_t, x, dy, ng, scale, shift, tm, name, riders=()):
    T = x.shape[0]
    n = len(dzs)
    wpc = dzs[0].shape[1]
    nt = T // tm
    with_dx = dy is not None
    n_own_in = n + 5 + with_dx
    n_own_out = 3 + with_dx
    n_rid = len(riders)

    def body(*refs):
        dz_refs = refs[:n]
        n_in = n_own_in + n_rid
        own = refs[n:n_own_in] + refs[n_in:n_in + n_own_out]
        rid_src = refs[n_own_in:n_own_in + n_rid]
        rid_dst = refs[n_in + n_own_out:n_in + n_own_out + n_rid]
        rid_sems = refs[n_in + n_own_out + n_rid:]
        if with_dx:
            w_ref, x_ref, dy_ref, g_ref, sc_ref, sh_ref, gx_ref, dsh_ref, dsc_ref, dg_ref = own
        else:
            w_ref, x_ref, g_ref, sc_ref, sh_ref, dsh_ref, dsc_ref, dg_ref = own

        @pl.when(pl.program_id(0) == 0)
        def _():
            for r, (ex, _) in enumerate(riders):
                ex.start(rid_src[r], rid_dst[r], *rid_sems[3 * r:3 * r + 3])
            dsh_ref[...] = jnp.zeros_like(dsh_ref)
            dsc_ref[...] = jnp.zeros_like(dsc_ref)
            dg_ref[...] = jnp.zeros_like(dg_ref)

        dh = jnp.dot(dz_refs[0][...], w_ref[0:wpc, :], preferred_element_type=F32)
        for k in range(1, n):
            dh = dh + jnp.dot(dz_refs[k][...], w_ref[wpc * k:wpc * (k + 1), :], preferred_element_type=F32)
        xv = x_ref[...]
        r = lax.rsqrt(jnp.mean(xv * xv, axis=-1, keepdims=True) + EPS)
        xn = xv * r
        gv, op = g_ref[...], 1.0 + sc_ref[...]
        dsh_ref[...] += jnp.sum(dh, axis=0, keepdims=True)
        dsc_ref[...] += jnp.sum(dh * xn * gv, axis=0, keepdims=True)
        dg_ref[...] += jnp.sum(dh * op * xn, axis=0, keepdims=True)
        if with_dx:
            dxn = dh * (gv * op)
            gx_ref[...] = r * (dxn - xn * jnp.mean(dxn * xn, axis=-1, keepdims=True)) + dy_ref[...]

        @pl.when(pl.program_id(0) == nt - 1)
        def _():
            for r, (ex, _) in enumerate(riders):
                ex.wait(rid_src[r], rid_dst[r], *rid_sems[3 * r:3 * r + 3])

    vec = pl.BlockSpec((1, D_MODEL), lambda i: (0, 0))
    rowf = pl.BlockSpec((tm, D_MODEL), lambda i: (i, 0))
    in_specs = [pl.BlockSpec((tm, wpc), lambda i: (i, 0))] * n
    in_specs += [pl.BlockSpec((wpc * n, D_MODEL), lambda i: (w_row0 // (wpc * n), 0)), rowf]
    args = list(dzs) + [w_in_t, x]
    vshape = jax.ShapeDtypeStruct((1, D_MODEL), F32)
    out_specs, out_shape = [vec, vec, vec], [vshape, vshape, vshape]
    if with_dx:
        in_specs.append(rowf)
        args.append(dy)
        out_specs = [rowf] + out_specs
        out_shape = [jax.ShapeDtypeStruct((T, D_MODEL), F32)] + out_shape
    in_specs += [vec, vec, vec] + [ANY_SPEC] * n_rid
    args += [ng, scale, shift] + [a for _, a in riders]
    res = pl.pallas_call(
        body, name=name, grid=(nt,), in_specs=in_specs, out_specs=out_specs + [ANY_SPEC] * n_rid,
        out_shape=out_shape + [ex.out_shape for ex, _ in riders],
        scratch_shapes=[s for ex, _ in riders for s in ex.scratch],
        compiler_params=_params(("arbitrary",)),
    )(*args)
    return res[:n_own_out], res[n_own_out:]


def _adamw_sharded(w, gparts, m, v, tr, name, riders=()):
    R, C = w.shape
    n_part = gparts.shape[0]
    nt = R // tr
    n_rid = len(riders)

    def body(w_ref, gp_ref, m_ref, v_ref, *rest):
        rid_src = rest[:n_rid]
        g_ref, d_ref, m2_ref, v2_ref = rest[n_rid:n_rid + 4]
        rid_dst = rest[n_rid + 4:2 * n_rid + 4]
        rid_sems = rest[2 * n_rid + 4:]

        @pl.when(pl.program_id(0) == 0)
        def _():
            for r, (ex, _) in enumerate(riders):
                ex.start(rid_src[r], rid_dst[r], *rid_sems[3 * r:3 * r + 3])

        g = gp_ref[0].astype(F32)
        for d in range(1, n_part):
            g = g + gp_ref[d].astype(F32)
        delta, m2, v2 = _adam(w_ref[...], g, m_ref[...], v_ref[...])
        g_ref[...] = g
        d_ref[...] = delta
        m2_ref[...] = m2
        v2_ref[...] = v2

        @pl.when(pl.program_id(0) == nt - 1)
        def _():
            for r, (ex, _) in enumerate(riders):
                ex.wait(rid_src[r], rid_dst[r], *rid_sems[3 * r:3 * r + 3])

    row = pl.BlockSpec((tr, C), lambda i: (i, 0))
    sh = jax.ShapeDtypeStruct((R, C), F32)
    res = pl.pallas_call(
        body, name=name, grid=(nt,),
        in_specs=[row, pl.BlockSpec((n_part, tr, C), lambda i: (0, i, 0)), row, row] + [ANY_SPEC] * n_rid,
        out_specs=[row, row, row, row] + [ANY_SPEC] * n_rid,
        out_shape=[sh, sh, sh, sh] + [ex.out_shape for ex, _ in riders],
        scratch_shapes=[s for ex, _ in riders for s in ex.scratch],
        compiler_params=_params(("arbitrary",)),
    )(w, gparts, m, v, *[a for _, a in riders])
    return res[:4], res[4:]


def _pack_vectors(vec_rows, dsg, dgq, dgk, dgk_c, loss_part):
    flat = [a for _, arrs in vec_rows for a in arrs]

    def body(*refs):
        vecs = list(refs[:len(flat)])
        dsg_ref, dgq_ref, dgk_ref, dgkc_ref, loss_ref, v_ref = refs[len(flat):]
        row = lax.broadcasted_iota(jnp.int32, (16, D_MODEL), 0)
        misc = jnp.concatenate([dsg_ref[...], dgq_ref[0:1, :], dgk_ref[0:1, :], dgkc_ref[0:1, :],
                                loss_ref[...]], axis=1)
        v = jnp.where(row == V_MISC, jnp.broadcast_to(misc, (16, D_MODEL)), 0.0)
        for r, arrs in vec_rows:
            val = vecs.pop(0)[...]
            for _ in arrs[1:]:
                val = val + vecs.pop(0)[...]
            v = jnp.where(row == r, jnp.broadcast_to(val, (16, D_MODEL)), v)
        v_ref[...] = v

    return pl.pallas_call(
        body, name="pack_vectors", out_shape=jax.ShapeDtypeStruct((16, D_MODEL), F32), compiler_params=_params(),
    )(*flat, dsg, dgq, dgk, dgk_c, loss_part)


SMALL_NAMES = ("b_ada", "norm_g", "sgu_norm_g", "w_spatial", "b_spatial", "q_norm_g", "k_norm_g", "rpb")


def _adamw_small(vg, sg, rg, ws, ms, vs):
    k = len(SMALL_NAMES)

    def body(*refs):
        vg_ref, sg_ref, rg_ref = refs[0], refs[1], refs[2]
        refs = refs[1:]
        w_refs = dict(zip(SMALL_NAMES, refs[2:2 + k]))
        m_refs = dict(zip(SMALL_NAMES, refs[2 + k:2 + 2 * k]))
        v_refs = dict(zip(SMALL_NAMES, refs[2 + 2 * k:2 + 3 * k]))
        o_refs = [dict(zip(SMALL_NAMES, refs[2 + (3 + i) * k:2 + (4 + i) * k])) for i in range(4)]
        loss_ref = refs[2 + 7 * k]

        sv = vg_ref[0]
        for d in range(1, N_DEV):
            sv = sv + vg_ref[d]
        loss_ref[...] = sv[V_MISC:V_MISC + 1, 896:1024]

        def total(lo, hi, ref=sg_ref):
            s = ref[0, lo:hi, :].astype(F32)
            for d in range(1, N_DEV):
                s = s + ref[d, lo:hi, :].astype(F32)
            return s

        def emit(name, idx, g):
            res = _adam(w_refs[name][idx], g, m_refs[name][idx], v_refs[name][idx])
            for o, val in zip(o_refs, (g,) + res):
                o[name][idx] = val

        everything = (slice(None), slice(None))
        row = lambda r: sv[r:r + 1, :]
        emit("b_ada", everything, jnp.concatenate(
            [row(V_DSHIFT) + row(V_DCSHIFT), row(V_DSCALE) + row(V_DCSCALE), row(V_DGATE)], axis=1))
        emit("norm_g", everything, row(V_DNG) + row(V_DNG_CTX))
        misc = row(V_MISC)
        emit("sgu_norm_g", everything, misc[:, 0:512])
        emit("q_norm_g", everything, misc[:, 512:512 + HEAD_DIM])
        emit("k_norm_g", everything, misc[:, 640:640 + HEAD_DIM] + misc[:, 768:768 + HEAD_DIM])
        for g in range(SGU_GROUPS):
            emit("w_spatial", (0, g), total(128 * g, 128 * (g + 1)))
            emit("b_spatial", (0, slice(g, g + 1), slice(None)), total(M_DBS + 8 * g, M_DBS + 8 * (g + 1))[0:1, :])
        for hd in range(N_HEADS):
            by_dc = total(32 * hd, 32 * (hd + 1), rg_ref)
            emit("rpb", (0, hd), by_dc.T[0:2 * WIN_R - 1, 0:2 * WIN_C - 1])

    shapes = [jax.ShapeDtypeStruct(w.shape, F32) for w in ws]
    res = pl.pallas_call(body, name="adamw_small", out_shape=shapes * 4 + [jax.ShapeDtypeStruct((1, 128), F32)],
                         compiler_params=_params())(vg, sg, rg, *ws, *ms, *vs)
    return [res[i * k:(i + 1) * k] for i in range(4)], res[4 * k]


def _adamw_cctx(pc_g, w, m, v):
    def body(pc_ref, w_ref, m_ref, v_ref, g_ref, d_ref, m2_ref, v2_ref):
        pc = pc_ref[0, 0:1, :]
        for d in range(1, N_DEV):
            pc = pc + pc_ref[d, 0:1, :]
        cc = w_ref[...]
        sig = _sigmoid(cc)
        g = pc * (sig * (1.0 + cc * (1.0 - sig)))
        delta, m2, v2 = _adam(cc, g, m_ref[...], v_ref[...])
        g_ref[...] = g
        d_ref[...] = delta
        m2_ref[...] = m2
        v2_ref[...] = v2

    sh = jax.ShapeDtypeStruct((1, D_MODEL), F32)
    return pl.pallas_call(body, name="adamw_cctx", out_shape=[sh, sh, sh, sh], compiler_params=_params())(
        pc_g, w, m, v)


def _block_ones(n, blk):
    i = np.arange(n)
    return jnp.asarray((i[:, None] // blk == i[None, :] // blk).astype(np.float32), BF16)


def _rpb_pairs(rpb):
    n_off = 2 * WIN_C - 1
    cols = np.arange(GRID_W)
    c0 = np.clip(cols - WIN_C // 2, 0, GRID_W - WIN_C)
    in_win = (cols[None, :] >= c0[:, None]) & (cols[None, :] < c0[:, None] + WIN_C)
    dc = np.clip(cols[None, :] - cols[:, None] + (WIN_C - 1), 0, n_off - 1)
    expand = (dc[None] == np.arange(n_off)[:, None, None]) & in_win[None]
    toep = jnp.einsum("hrd,dqk->hrqk", rpb, jnp.asarray(expand, F32), precision=HI)
    toep = toep + jnp.asarray(np.where(in_win, 0.0, NEG_INF).astype(np.float32))
    neg = jnp.full((N_HEADS, 1, GRID_W, GRID_W), NEG_INF, F32)
    ext = jnp.concatenate([neg, toep, neg], axis=1)
    return jnp.concatenate([ext[:, :-1], ext[:, 1:]], axis=-1)


def _row_mask(rows):
    nrb = rows // Q_ROWS
    valid = np.zeros((3, Q_ROWS, 1, K_ROWS, 1), bool)
    for t, rb in enumerate((0, 1, nrb - 1)):
        kb = int(np.clip(Q_ROWS * rb - 4, 0, rows - K_ROWS))
        for i in range(Q_ROWS):
            r0 = int(np.clip(Q_ROWS * rb + i - WIN_R // 2, 0, rows - WIN_R))
            for j in range(K_ROWS):
                valid[t, i, 0, j, 0] = r0 <= kb + j < r0 + WIN_R
    full = np.broadcast_to(valid, (3, Q_ROWS, GRID_W, K_ROWS, GRID_W)).reshape(3, TQ, TK)
    return jnp.asarray(np.where(full, 0.0, NEG_INF).astype(np.float32))


def kernel(x, c, ctx, c_ctx, w_ada, b_ada, norm_g, w_in, sgu_norm_g, w_spatial, b_spatial, q_norm_g, k_norm_g, rpb, w_out, loss_target, m_c_ctx, m_w_ada, m_b_ada, m_norm_g, m_w_in, m_sgu_norm_g, m_w_spatial, m_b_spatial, m_q_norm_g, m_k_norm_g, m_rpb, m_w_out, v_c_ctx, v_w_ada, v_b_ada, v_norm_g, v_w_in, v_sgu_norm_g, v_w_spatial, v_b_spatial, v_q_norm_g, v_k_norm_g, v_rpb, v_w_out):
    me = 4 * lax.axis_index("x") + 2 * lax.axis_index("y") + lax.axis_index("c")
    x2, ctx2, tgt2 = x[0], ctx[0], loss_target[0]
    T, C = x2.shape[0], ctx2.shape[0]
    rows = T // GRID_W
    wada, win_t, wout = w_ada[0], w_in[0].T, w_out[0]
    ada_w = wada.shape[1]
    win_w = win_t.shape[0]

    row8 = lax.broadcasted_iota(jnp.int32, (8, D_MODEL), 0)
    c_blk = jnp.where(row8 == me, jnp.broadcast_to(c, (8, D_MODEL)), 0.0)
    b_sh = lax.dynamic_slice(b_ada, (0, me * ada_w), (1, ada_w))
    c_ctx_row = c_ctx.reshape(1, D_MODEL)

    ones512 = _block_ones(512, HEAD_DIM)
    ones8 = jnp.ones((8, 128), BF16)
    foldm = jnp.asarray((np.arange(512)[:, None] % HEAD_DIM == np.arange(128)[None, :]).astype(np.float32))
    lane_half = np.arange(128)[None, :, None] // GRID_W
    hsel = jnp.asarray((2 * np.arange(8)[:, None, None] + lane_half == np.arange(128)[None, None, :]).astype(np.float32),
                       BF16)
    foldr = jnp.asarray((np.arange(256)[None, :] // 8 == np.arange(32)[:, None]).astype(np.float32), BF16)
    gq512 = jnp.tile(q_norm_g, (1, N_HEADS))
    gk512 = jnp.tile(k_norm_g, (1, N_HEADS))
    ws = w_spatial[0]
    wst = ws.transpose(0, 2, 1)
    bsb = jnp.broadcast_to(b_spatial[0][:, :, None], (SGU_GROUPS, CHUNK, 128))
    pairs = _rpb_pairs(rpb[0])
    row_mask = _row_mask(rows)

    my_chip = me // 2
    order = jnp.stack([my_chip, my_chip ^ 2, my_chip ^ 1, my_chip ^ 3]).astype(jnp.int32)
    h, z, win_g, s16, part_g = _inproj_fwd(order, x2, norm_g, c_blk, c_ctx_row, wada, b_sh, win_t.astype(BF16), 1024)
    w_in_b = win_g.reshape(D_IN, D_MODEL)
    mod16 = part_g.transpose(1, 0, 2).reshape(16, 3 * D_MODEL)
    mod = lax.dynamic_slice(mod16, (me, 0), (1, 3 * D_MODEL))
    shift, scale, gate = mod[:, :D_MODEL], mod[:, D_MODEL:2 * D_MODEL], mod[:, 2 * D_MODEL:]
    cshift, cscale = mod16[8:9, :D_MODEL], mod16[8:9, D_MODEL:2 * D_MODEL]
    out_a, qs, kn = _sgu_qk_fwd(z, sgu_norm_g, ws, bsb, ones512, gq512, gk512, 512)
    hc, zc, ckn = _ctx_fwd(ctx2, norm_g, cscale, cshift, w_in_b, ones512, gk512)
    wout_blk = wout.astype(BF16)
    (ob, out_b, lse), (wout_g,) = _attn_fwd(qs, kn, z, ckn, zc, pairs, row_mask, [(_Hosted("ag", wout_blk), wout_blk)])
    w_out_b = wout_g.reshape(D_MODEL, D_MODEL)

    dy, dmc, dw_out, dgate, loss_part = _outproj_loss_bwd(x2, tgt2, out_a, out_b, gate, w_out_b, 512)
    dw_out_blocks = dw_out.reshape(N_DEV, D_MODEL // N_DEV, D_MODEL)
    (dqs, dk, dv, dck, dcv, db_g, drpb), (gout_parts,) = _attn_bwd(
        qs, kn, z, ckn, zc, pairs, row_mask, ob, lse, dmc, hsel, foldr,
        [(_Hosted("a2a", dw_out_blocks), dw_out_blocks)])
    dzc_k, dzc_v, dgk_c = _ctx_k_bwd(zc, dck, dcv, ones512, gk512, foldm)
    rloc = drpb.reshape(N_HEADS * 32, 128)
    (dz, chip_sums, dws, dbs, dsg, dgq, dgk), (rg,) = _bwd_mid(
        z, dmc, dqs, dk, dv, db_g, h, hc, dzc_k, dzc_v, sgu_norm_g, ws, wst, bsb, ones8, ones512, gq512, gk512, foldm,
        512, [(_Hosted("ag", rloc), rloc)])
    sloc = jnp.concatenate([dws.reshape(SGU_GROUPS * CHUNK, CHUNK), dbs.reshape(SGU_GROUPS * 8, CHUNK)]).astype(BF16)
    (grad_x, dshift, dscale, dng), (gin_parts,) = _inproj_bwd_dx(
        [dz], 0, w_in_b, x2, dy, norm_g, scale, shift, 512, "inproj_bwd_dx",
        [(_Hosted("chips", chip_sums), chip_sums)])
    (dcshift, dcscale, dng_c), _ = _inproj_bwd_dx([dzc_k, dzc_v], 4 * 512, w_in_b, ctx2, None, norm_g, cscale,
                                                  cshift, C, "ctx_bwd_dx")

    res_in, _ = _adamw_sharded(win_t, gin_parts, m_w_in[0].T, v_w_in[0].T, 112, "adamw_w_in")
    res_out, _ = _adamw_sharded(wout, gout_parts, m_w_out[0], v_w_out[0], 128, "adamw_w_out")

    zero_row = jnp.zeros((1, D_MODEL), F32)
    vec_rows = [(V_DSHIFT, [dshift]), (V_DSCALE, [dscale]), (V_DGATE, [dgate]), (V_DCSHIFT, [dcshift]),
                (V_DCSCALE, [dcscale]), (V_ZERO, [zero_row]), (V_DNG, [dng]), (V_DNG_CTX, [dng_c])]
    vloc = _pack_vectors(vec_rows, dsg, dgq, dgk, dgk_c, loss_part)
    vg, sg = _allgather_direct([vloc, sloc], "gather_small")
    small_w =(b_ada, norm_g, sgu_norm_g, w_spatial, b_spatial, q_norm_g, k_norm_g, rpb)
    small_m = (m_b_ada, m_norm_g, m_sgu_norm_g, m_w_spatial, m_b_spatial, m_q_norm_g, m_k_norm_g, m_rpb)
    small_v = (v_b_ada, v_norm_g, v_sgu_norm_g, v_w_spatial, v_b_spatial, v_q_norm_g, v_k_norm_g, v_rpb)
    res_small, loss_row = _adamw_small(vg, sg, rg, small_w, small_m, small_v)

    dm_all = vg[:, V_DSHIFT:V_DGATE + 1, :].reshape(N_DEV, 3 * D_MODEL)
    dc_all = vg[:, V_DCSHIFT:V_ZERO + 1, :].reshape(N_DEV, 3 * D_MODEL)
    dm_sh = lax.dynamic_slice(dm_all, (0, me * ada_w), (N_DEV, ada_w))
    dc_sh = lax.dynamic_slice(dc_all, (0, me * ada_w), (N_DEV, ada_w))
    *res_ada, pc = _ada_bwd(s16, dm_sh, dc_sh, wada, m_w_ada[0], v_w_ada[0])
    (pc_g,) = _allgather_direct([pc], "gather_cctx")
    res_cctx = _adamw_cctx(pc_g, c_ctx_row, m_c_ctx.reshape(1, D_MODEL), v_c_ctx.reshape(1, D_MODEL))

    loss = loss_row[0, 0]
    outs = [loss, grad_x[None]]
    for kind in range(4):
        by_name = dict(zip(SMALL_NAMES, res_small[kind]))
        by_name.update(c_ctx=res_cctx[kind].reshape(D_MODEL), w_ada=res_ada[kind][None],
                       w_in=res_in[kind].T[None], w_out=res_out[kind][None])
        outs += [by_name[nme] for nme in ("c_ctx", "w_ada", "b_ada", "norm_g", "w_in", "sgu_norm_g", "w_spatial",
                                          "b_spatial", "q_norm_g", "k_norm_g", "rpb", "w_out")]
    return tuple(outs)
```

```python
import functools

import numpy as np
import jax
import jax.numpy as jnp
from jax import lax
from jax.experimental import pallas as pl
from jax.experimental.pallas import tpu as pltpu

F32 = jnp.float32
BF16 = jnp.bfloat16
HI = lax.Precision.HIGHEST

N_DEV = 8
D_MODEL = 1024
D_A = 512
D_B = 512
D_IN = 3584
N_BRANCH = 7
HEAD_DIM = 64
N_HEADS = 8
GRID_W = 64
WIN_R = 8
WIN_C = 16
CHUNK = 128
SGU_GROUPS = 4
EPS = 1e-6
NEG_INF = -1e30
Q_ROWS = 4
K_ROWS = 12
TQ = Q_ROWS * GRID_W
TK = K_ROWS * GRID_W
N_DIAG = 22
ATT_SUB = 8
ATT_SCALE = HEAD_DIM ** -0.5
LOG2E = 1.4426950408889634
LN2 = 0.6931471805599453

ADAM_LR = 0.001
ADAM_B1 = 0.9
ADAM_B2 = 0.999
ADAM_EPS = 1e-08
ADAM_WD = 0.01
ADAM_STEP = 10

VMEM_LIMIT = 56 * 1024 * 1024
MESH = pl.DeviceIdType.MESH

V_DSHIFT, V_DSCALE, V_DGATE, V_DCSHIFT, V_DCSCALE, V_ZERO, V_DNG, V_DNG_CTX, V_MISC = range(9)
M_DBS = 512
MOD_SHIFT, MOD_SCALE, MOD_GATE, MOD_CSHIFT, MOD_CSCALE = range(5)


def _params(sem=None):
    return pltpu.CompilerParams(dimension_semantics=sem, vmem_limit_bytes=VMEM_LIMIT)


def _sigmoid(x):
    return 1.0 / (1.0 + jnp.exp(-x))


def _gelu_parts(x):
    cdf = 0.5 * (1.0 + lax.erf(x * 0.7071067811865476))
    pdf = jnp.exp(-0.5 * x * x) * 0.3989422804014327
    return x * cdf, cdf + x * pdf


def _nt(a, b):
    return lax.dot_general(a, b, (((1,), (1,)), ((), ())), preferred_element_type=F32)


def _tn(a, b):
    return lax.dot_general(a, b, (((0,), (0,)), ((), ())), preferred_element_type=F32)


def _dot2(v, ones_bf):
    hi = v.astype(BF16)
    lo = (v - hi.astype(F32)).astype(BF16)
    return (jnp.dot(hi, ones_bf, preferred_element_type=F32)
            + jnp.dot(lo, ones_bf, preferred_element_type=F32))


def _head_sum(v, ones_ref):
    return jnp.dot(v.astype(BF16), ones_ref[...], preferred_element_type=F32)


def _adam(w, g, m, v):
    m2 = ADAM_B1 * m + (1.0 - ADAM_B1) * g
    v2 = ADAM_B2 * v + (1.0 - ADAM_B2) * (g * g)
    m_hat = m2 / (1.0 - ADAM_B1 ** ADAM_STEP)
    v_hat = v2 / (1.0 - ADAM_B2 ** ADAM_STEP)
    delta = -ADAM_LR * (m_hat / (jnp.sqrt(v_hat) + ADAM_EPS) + ADAM_WD * w)
    return delta, m2, v2


class _Hosted:
    def __init__(self, kind, src):
        self.kind = kind
        n_slot = {"a2a": N_DEV, "ag": N_DEV, "chips": N_DEV // 2}[kind]
        blk = src.shape if kind == "ag" else src.shape[1:]
        self.out_shape = jax.ShapeDtypeStruct((n_slot,) + tuple(blk), src.dtype)
        self.n_peer = n_slot - 1
        self.scratch = [pltpu.SemaphoreType.DMA((self.n_peer,)), pltpu.SemaphoreType.DMA((self.n_peer,)),
                        pltpu.SemaphoreType.DMA]

    def _copies(self, src, dst, send_sems, recv_sems, loc_sem, landing):
        x, y, c = lax.axis_index("x"), lax.axis_index("y"), lax.axis_index("c")
        if self.kind == "chips":
            me = 2 * x + y
            peers = [((px, py, c), 2 * px + py) for px, py in ((1 - x, y), (x, 1 - y), (1 - x, 1 - y))]
        else:
            me = 4 * x + 2 * y + c
            peers = []
            for k in range(1, N_DEV):
                px = 1 - x if (k >> 2) & 1 else x
                py = 1 - y if (k >> 1) & 1 else y
                pc = 1 - c if k & 1 else c
                peers.append(((px, py, pc), 4 * px + 2 * py + pc))
        remote = []
        for k, (peer, pid) in enumerate(peers):
            s = src if self.kind == "ag" else src.at[pid]
            remote.append(pltpu.make_async_remote_copy(
                src_ref=s, dst_ref=dst.at[pid if landing else me],
                send_sem=send_sems.at[k], recv_sem=recv_sems.at[k], device_id=peer, device_id_type=MESH))
        local = pltpu.make_async_copy(src if self.kind == "ag" else src.at[me], dst.at[me], loc_sem)
        return remote, local

    def start(self, src, dst, send_sems, recv_sems, loc_sem):
        remote, local = self._copies(src, dst, send_sems, recv_sems, loc_sem, landing=False)
        for cp in remote:
            cp.start()
        local.start()

    def wait(self, src, dst, send_sems, recv_sems, loc_sem):
        remote, local = self._copies(src, dst, send_sems, recv_sems, loc_sem, landing=True)
        for cp in remote:
            cp.wait_recv()
        for cp in remote:
            cp.wait_send()
        local.wait()


ANY_SPEC = pl.BlockSpec(memory_space=pl.ANY)


def _allgather_direct(arrs, name):
    n = len(arrs)
    exs = [_Hosted("ag", a) for a in arrs]

    def body(*refs):
        srcs, dsts, sems = refs[:n], refs[n:2 * n], refs[2 * n:]
        for r, ex in enumerate(exs):
            ex.start(srcs[r], dsts[r], *sems[3 * r:3 * r + 3])
        for r, ex in enumerate(exs):
            ex.wait(srcs[r], dsts[r], *sems[3 * r:3 * r + 3])

    return pl.pallas_call(body, name=name, out_shape=[ex.out_shape for ex in exs], in_specs=[ANY_SPEC] * n,
                          out_specs=[ANY_SPEC] * n, scratch_shapes=[s for ex in exs for s in ex.scratch])(*arrs)


def _ada_scratch(n_col):
    return ([pltpu.VMEM((N_DEV, 8, D_MODEL), F32), pltpu.VMEM((16, n_col), F32), pltpu.VMEM((N_DEV, 16, n_col), F32)]
            + [pltpu.SemaphoreType.DMA((N_DEV - 1,)) for _ in range(4)])


def _ada_modulation(cb_ref, cc_ref, w_ref, b_ref, cstack, part, parts, s1, r1, s2, r2):
    x, y, c = lax.axis_index("x"), lax.axis_index("y"), lax.axis_index("c")
    me = 4 * x + 2 * y + c
    peers = []
    for k in range(1, N_DEV):
        px = 1 - x if (k >> 2) & 1 else x
        py = 1 - y if (k >> 1) & 1 else y
        pc = 1 - c if k & 1 else c
        peers.append(((px, py, pc), 4 * px + 2 * py + pc))

    def exchange(src, dst, send_sems, recv_sems):
        for k, (peer, _) in enumerate(peers):
            pltpu.make_async_remote_copy(src_ref=src, dst_ref=dst.at[me], send_sem=send_sems.at[k],
                                         recv_sem=recv_sems.at[k], device_id=peer, device_id_type=MESH).start()
        dst[me] = src[...]
        waits = [pltpu.make_async_remote_copy(src_ref=src, dst_ref=dst.at[pid], send_sem=send_sems.at[k],
                                              recv_sem=recv_sems.at[k], device_id=peer, device_id_type=MESH)
                 for k, (peer, pid) in enumerate(peers)]
        for cp in waits:
            cp.wait_recv()
        for cp in waits:
            cp.wait_send()

    exchange(cb_ref, cstack, s1, r1)
    c_all = cstack[0]
    for d in range(1, N_DEV):
        c_all = c_all + cstack[d]
    row = lax.broadcasted_iota(jnp.int32, (8, D_MODEL), 0)
    cc = jnp.where(row == 0, jnp.broadcast_to(cc_ref[...], (8, D_MODEL)), 0.0)
    call = jnp.concatenate([c_all, cc], axis=0)
    s = call * _sigmoid(call)
    part[...] = jnp.dot(s, w_ref[...], preferred_element_type=F32, precision=HI) + b_ref[...]
    exchange(part, parts, s2, r2)
    return s


def _ada_bwd(s16, dm, dc, w, m, v):
    def body(s_ref, dm_ref, dc_ref, w_ref, m_ref, v_ref, g_ref, d_ref, m2_ref, v2_ref, pc_ref):
        dct = jnp.sum(dc_ref[...], axis=0, keepdims=True)
        row = lax.broadcasted_iota(jnp.int32, dc_ref.shape, 0)
        dcb = jnp.where(row == 0, jnp.broadcast_to(dct, dc_ref.shape), 0.0)
        dm16 = jnp.concatenate([dm_ref[...], dcb], axis=0)
        g = lax.dot_general(s_ref[...], dm16, (((0,), (0,)), ((), ())),
                            preferred_element_type=F32, precision=HI)
        w_ = w_ref[...]
        delta, m2, v2 = _adam(w_, g, m_ref[...], v_ref[...])
        g_ref[...] = g
        d_ref[...] = delta
        m2_ref[...] = m2
        v2_ref[...] = v2
        pc_ref[...] = lax.dot_general(dcb, w_, (((1,), (1,)), ((), ())),
                                      preferred_element_type=F32, precision=HI)

    R, n_col = w.shape
    tr = 256
    sh = jax.ShapeDtypeStruct(w.shape, F32)
    rows = pl.BlockSpec((tr, n_col), lambda i: (i, 0))
    small = pl.BlockSpec(dm.shape, lambda i: (0, 0))
    return pl.pallas_call(
        body, name="ada_bwd", grid=(R // tr,),
        in_specs=[pl.BlockSpec((16, tr), lambda i: (0, i)), small, small, rows, rows, rows],
        out_specs=[rows, rows, rows, rows, pl.BlockSpec((8, tr), lambda i: (0, i))],
        out_shape=[sh, sh, sh, sh, jax.ShapeDtypeStruct((8, D_MODEL), F32)],
        compiler_params=_params(("arbitrary",)),
    )(s16, dm, dc, w, m, v)


def _head_norm(zk, ones_ref, gain):
    ss = _head_sum(zk * zk, ones_ref)
    return zk * lax.rsqrt(ss * (1.0 / HEAD_DIM) + EPS) * gain


def _inproj_fwd(order, x, ng, c_blk, c_ctx_row, w_ada_sh, b_ada_sh, w_blk_t, tm):
    T = x.shape[0]
    nt = T // tm
    n_pass = N_DEV // 2
    blk_rows = w_blk_t.shape[0]
    n_col = w_ada_sh.shape[1]

    def body(order_ref, x_ref, g_ref, cb_ref, cc_ref, wa_ref, ba_ref, wb_ref,
             h_out, z_ref, wt_out, s_out, mod_out,
             hs, wt, modv, send_sems, recv_sems, loc_sem, h_sem, wt_sem, *ada_sc):
        p, i = pl.program_id(0), pl.program_id(1)
        x, y, c = lax.axis_index("x"), lax.axis_index("y"), lax.axis_index("c")
        me, sib = (x, y, c), (x, y, 1 - c)
        chips = [(1 - x, y), (x, 1 - y), (1 - x, 1 - y)]

        def slot(px, py, pc):
            return 4 * px + 2 * py + pc

        def copy(k, block, to, src=None):
            return pltpu.make_async_remote_copy(
                src_ref=wt.at[slot(*block)] if src is None else src, dst_ref=wt.at[slot(*block)],
                send_sem=send_sems.at[k], recv_sem=recv_sems.at[k], device_id=to, device_id_type=MESH)

        own = pltpu.make_async_copy(wb_ref, wt.at[slot(*me)], loc_sem)
        h_copy = pltpu.make_async_copy(hs, h_out, h_sem)
        wt_copy = pltpu.make_async_copy(wt, wt_out, wt_sem)
        first = [copy(1 + j, me, (*chip, c), src=wb_ref) for j, chip in enumerate(chips[:2])] + [copy(0, me, sib, src=wb_ref)]
        passed = [copy(4 + j, (*chip, c), sib) for j, chip in enumerate(chips)]
        relay_src = (jnp.where(c == 0, 1 - x, x), jnp.where(c == 0, y, 1 - y), c)
        relay_dst = (jnp.where(c == 0, x, 1 - x), jnp.where(c == 0, 1 - y, y), c)
        relay = copy(3, relay_src, relay_dst)

        @pl.when(jnp.logical_and(p == 0, i == 0))
        def _():
            s_out[...] = _ada_modulation(cb_ref, cc_ref, wa_ref, ba_ref, *ada_sc)
            own.start()
            for cp in first:
                cp.start()
            parts = ada_sc[2]
            my_row = pl.ds(slot(*me), 1)
            mod = jnp.concatenate([parts[d, my_row, :] for d in range(N_DEV)], axis=1)
            cmod = jnp.concatenate([parts[d, 8:9, :] for d in range(N_DEV)], axis=1)
            modv[...] = jnp.zeros_like(modv)
            for r, val in enumerate((mod[:, 0:D_MODEL], mod[:, D_MODEL:2 * D_MODEL], mod[:, 2 * D_MODEL:],
                                     cmod[:, 0:D_MODEL], cmod[:, D_MODEL:2 * D_MODEL])):
                modv[r:r + 1, :] = val
                mod_out[r] = val
            for r in range(5, 8):
                mod_out[r] = jnp.zeros((1, D_MODEL), F32)
            own.wait()
            copy(0, sib, me).wait_recv()

        @pl.when(jnp.logical_and(p == 1, i == 0))
        def _():
            for j, chip in enumerate(chips[:2]):
                copy(1 + j, (*chip, c), me).wait_recv()
            relay.start()
            passed[0].start()
            passed[1].start()
            copy(4, (*chips[0], 1 - c), me).wait_recv()

        @pl.when(jnp.logical_and(p == 2, i == 0))
        def _():
            copy(5, (*chips[1], 1 - c), me).wait_recv()

        @pl.when(jnp.logical_and(p == 3, i == 0))
        def _():
            copy(3, (*chips[2], c), me).wait_recv()
            passed[2].start()
            copy(6, (*chips[2], 1 - c), me).wait_recv()
            wt_copy.start()

        rows = pl.ds(pl.multiple_of(i * tm, tm), tm)

        @pl.when(p == 0)
        def _():
            xv = x_ref[...]
            r = lax.rsqrt(jnp.mean(xv * xv, axis=-1, keepdims=True) + EPS)
            hs[rows, :] = ((xv * r * g_ref[...]) * (1.0 + modv[1:2, :]) + modv[0:1, :]).astype(BF16)

        @pl.when(jnp.logical_and(p == 1, i == 0))
        def _():
            h_copy.start()

        w_pair = wt[pl.ds(2 * order_ref[p], 2)].reshape(2 * blk_rows, D_MODEL)
        z_ref[...] = _nt(hs[rows, :], w_pair).astype(BF16)

        @pl.when(jnp.logical_and(p == n_pass - 1, i == nt - 1))
        def _():
            for cp in first + passed + [relay]:
                cp.wait_send()
            h_copy.wait()
            wt_copy.wait()

    whole = lambda shape: pl.BlockSpec(shape, lambda p, i, o: (0,) * len(shape))
    grid_spec = pltpu.PrefetchScalarGridSpec(
        num_scalar_prefetch=1, grid=(n_pass, nt),
        in_specs=[pl.BlockSpec((tm, D_MODEL), lambda p, i, o: (jnp.where(p == 0, i, nt - 1), 0)),
                  whole((1, D_MODEL)), whole((8, D_MODEL)), whole((1, D_MODEL)), whole((D_MODEL, n_col)),
                  whole((1, n_col)), ANY_SPEC],
        out_specs=[ANY_SPEC, pl.BlockSpec((tm, 2 * blk_rows), lambda p, i, o: (i, o[p])), ANY_SPEC,
                   whole((16, D_MODEL)), whole((8, 1, D_MODEL))],
        scratch_shapes=[pltpu.VMEM((T, D_MODEL), BF16), pltpu.VMEM((N_DEV, blk_rows, D_MODEL), BF16),
                        pltpu.VMEM((8, D_MODEL), F32),
                        pltpu.SemaphoreType.DMA((7,)), pltpu.SemaphoreType.DMA((7,)), pltpu.SemaphoreType.DMA,
                        pltpu.SemaphoreType.DMA, pltpu.SemaphoreType.DMA] + _ada_scratch(n_col))
    return pl.pallas_call(
        body, name="inproj_fwd", grid_spec=grid_spec,
        out_shape=[jax.ShapeDtypeStruct((T, D_MODEL), BF16), jax.ShapeDtypeStruct((T, D_IN), BF16),
                   jax.ShapeDtypeStruct((N_DEV, blk_rows, D_MODEL), BF16),
                   jax.ShapeDtypeStruct((16, D_MODEL), F32), jax.ShapeDtypeStruct((8, 1, D_MODEL), F32)],
        compiler_params=_params(("arbitrary", "arbitrary")),
    )(order, x, ng, c_blk, c_ctx_row, w_ada_sh, b_ada_sh, w_blk_t)


def _mod_row(row):
    return pl.BlockSpec((1, 1, D_MODEL), lambda *idx: (row, 0, 0))


def _ctx_fwd(ctx, ng, mod, w_in_t, ones_blk, gk):
    C = ctx.shape[0]

    def body(x_ref, g_ref, sc_ref, sh_ref, w_ref, ones_ref, gk_ref, h_ref, z_ref, kn_ref):
        xv = x_ref[...]
        r = lax.rsqrt(jnp.mean(xv * xv, axis=-1, keepdims=True) + EPS)
        h = (xv * r * g_ref[...]) * (1.0 + sc_ref[0]) + sh_ref[0]
        hb = h.astype(BF16)
        h_ref[...] = hb
        zk = _nt(hb, w_ref[0:512, :])
        zv = _nt(hb, w_ref[512:1024, :])
        z_ref[:, 0:512] = zk.astype(BF16)
        z_ref[:, 512:1024] = zv.astype(BF16)
        kn_ref[...] = _head_norm(zk, ones_ref, gk_ref[...]).astype(BF16)

    vec = pl.BlockSpec((1, D_MODEL), lambda i: (0, 0))
    return pl.pallas_call(
        body, name="ctx_fwd", grid=(1,),
        in_specs=[pl.BlockSpec((C, D_MODEL), lambda i: (0, 0)), vec, _mod_row(MOD_CSCALE), _mod_row(MOD_CSHIFT),
                  pl.BlockSpec((1024, D_MODEL), lambda i: (2, 0)),
                  pl.BlockSpec((512, 512), lambda i: (0, 0)), pl.BlockSpec((1, 512), lambda i: (0, 0))],
        out_specs=[pl.BlockSpec((C, D_MODEL), lambda i: (0, 0)), pl.BlockSpec((C, 1024), lambda i: (0, 0)),
                   pl.BlockSpec((C, 512), lambda i: (0, 0))],
        out_shape=[jax.ShapeDtypeStruct((C, D_MODEL), BF16), jax.ShapeDtypeStruct((C, 1024), BF16),
                   jax.ShapeDtypeStruct((C, 512), BF16)],
        compiler_params=_params(("arbitrary",)),
    )(ctx, ng, mod, mod, w_in_t, ones_blk, gk)


def _sgu_chunk_fwd(au, av, ag, sg, ws_bf, bsb):
    gu, dgu = _gelu_parts(au)
    gv, dgv = _gelu_parts(av)
    rr = lax.rsqrt(jnp.mean(gv * gv, axis=-1, keepdims=True) + EPS)
    vhat = gv * rr
    vn = vhat * sg
    mixed = jnp.dot(ws_bf, vn.astype(BF16), preferred_element_type=F32) + bsb
    sig = _sigmoid(ag)
    sl = ag * sig
    return gu * mixed * sl, (gu, dgu, dgv, rr, vhat, vn, mixed, sig, sl)


def _sgu_qk_fwd(z, sgn, ws, bsb, ones_blk, gq, gk, tm):
    T = z.shape[0]

    def body(au_ref, av_ref, ag_ref, q_ref, k_ref, sg_ref, ws_ref, bsb_ref, ones_ref, gq_ref, gk_ref,
             o_ref, qs_ref, kn_ref):
        qs = _head_norm(q_ref[...].astype(F32), ones_ref, gq_ref[...]) * (ATT_SCALE * LOG2E)
        qs_ref[...] = qs.astype(BF16)
        kn_ref[...] = _head_norm(k_ref[...].astype(F32), ones_ref, gk_ref[...]).astype(BF16)
        for g in range(SGU_GROUPS):
            ws_bf = ws_ref[g].astype(BF16)
            sg = sg_ref[:, 128 * g:128 * (g + 1)]
            bsb_g = bsb_ref[g]
            for j in range(tm // CHUNK):
                rs, cs = slice(CHUNK * j, CHUNK * (j + 1)), slice(128 * g, 128 * (g + 1))
                out, _ = _sgu_chunk_fwd(au_ref[rs, cs].astype(F32), av_ref[rs, cs].astype(F32),
                                        ag_ref[rs, cs].astype(F32), sg, ws_bf, bsb_g)
                o_ref[rs, cs] = out.astype(BF16)

    zcol = lambda col: pl.BlockSpec((tm, 512), lambda i: (i, col))
    v512 = pl.BlockSpec((1, 512), lambda i: (0, 0))
    row = pl.BlockSpec((tm, 512), lambda i: (i, 0))
    out = jax.ShapeDtypeStruct((T, 512), BF16)
    return pl.pallas_call(
        body, name="sgu_qk_fwd", grid=(T // tm,),
        in_specs=[zcol(0), zcol(1), zcol(2), zcol(3), zcol(4), v512,
                  pl.BlockSpec((SGU_GROUPS, CHUNK, CHUNK), lambda i: (0, 0, 0)),
                  pl.BlockSpec((SGU_GROUPS, CHUNK, 128), lambda i: (0, 0, 0)),
                  pl.BlockSpec((512, 512), lambda i: (0, 0)), v512, v512],
        out_specs=[row, row, row], out_shape=[out, out, out],
        compiler_params=_params(("arbitrary",)),
    )(z, z, z, z, z, sgn, ws, bsb, ones_blk, gq, gk)


def _attn_type(rb, nrb):
    return jnp.where(rb == 0, 0, jnp.where(rb == nrb - 1, 2, 1))


def _attn_specs(T, C, att_sub):
    return [
        pl.BlockSpec((att_sub * TQ, 128), lambda hp, st: (st, hp)),
        pl.BlockSpec((T, 128), lambda hp, st: (0, hp)),
        pl.BlockSpec((T, 128), lambda hp, st: (0, 20 + hp)),
        pl.BlockSpec((C, 128), lambda hp, st: (0, hp)),
        pl.BlockSpec((C, 128), lambda hp, st: (0, 4 + hp)),
        pl.BlockSpec((2, 2 * WIN_R, GRID_W, 128), lambda hp, st: (hp, 0, 0, 0)),
        pl.BlockSpec((3, TQ, TK), lambda hp, st: (0, 0, 0)),
        pl.BlockSpec((att_sub * TQ, 128), lambda hp, st: (st, 24 + hp)),
    ]


def _build_bias(pairs_ref, mask_ref, bias_sc):
    for t in range(3):
        for hh in range(2):
            for i in range(Q_ROWS):
                for mm in range(K_ROWS // 2):
                    p = min(max(WIN_R - Q_ROWS * t + 2 * mm - i, 0), 2 * WIN_R - 1)
                    rs, cs = slice(GRID_W * i, GRID_W * (i + 1)), slice(128 * mm, 128 * (mm + 1))
                    bias_sc[t, hh, rs, cs] = (pairs_ref[hh, p] + mask_ref[t, rs, cs]) * LOG2E


def _attn_fwd(qs, kn, z, ckn, zc, pairs, row_mask, riders=()):
    T, C = qs.shape[0], ckn.shape[0]
    rows = T // GRID_W
    nrb = rows // Q_ROWS
    att_sub = min(ATT_SUB, nrb)
    n_st = nrb // att_sub
    n_rid = len(riders)

    def body(q_ref, k_ref, v_ref, ck_ref, cv_ref, pairs_ref, mask_ref, bg_ref, *rest):
        rid_src = rest[:n_rid]
        ob_ref, outb_ref, lse_ref = rest[n_rid:n_rid + 3]
        rid_dst = rest[n_rid + 3:2 * n_rid + 3]
        bias_sc = rest[2 * n_rid + 3]
        rid_sems = rest[2 * n_rid + 4:]

        @pl.when(jnp.logical_and(pl.program_id(0) == 0, pl.program_id(1) == 0))
        def _():
            for r, (ex, _) in enumerate(riders):
                ex.start(rid_src[r], rid_dst[r], *rid_sems[3 * r:3 * r + 3])

        @pl.when(pl.program_id(1) == 0)
        def _():
            _build_bias(pairs_ref, mask_ref, bias_sc)

        ck2, cv2 = ck_ref[...], cv_ref[...]
        lane = lax.broadcasted_iota(jnp.int32, (1, 128), 1)
        for sub in range(att_sub):
            rb = att_sub * pl.program_id(1) + sub
            bias_ref = bias_sc.at[_attn_type(rb, nrb)]
            rs = slice(TQ * sub, TQ * (sub + 1))
            ks = pl.multiple_of(jnp.clip(Q_ROWS * rb - 4, 0, rows - K_ROWS) * GRID_W, GRID_W)
            q2 = q_ref[rs, :]
            k2 = k_ref[pl.ds(ks, TK), :]
            v2 = v_ref[pl.ds(ks, TK), :]
            o_acc = jnp.zeros((TQ, 128), F32)
            lse_acc = jnp.zeros((TQ, 128), F32)
            for hh in range(2):
                msk = (lane >= HEAD_DIM) == bool(hh)
                qm = jnp.where(msk, q2, jnp.zeros_like(q2))
                s = _nt(qm, k2) + bias_ref[hh]
                sc = _nt(qm, ck2)
                m = jnp.maximum(jnp.max(s, axis=-1, keepdims=True), jnp.max(sc, axis=-1, keepdims=True))
                p = jnp.exp2(s - m)
                pc = jnp.exp2(sc - m)
                va = jnp.where(msk, v2, jnp.ones_like(v2))
                cva = jnp.where(msk, cv2, jnp.ones_like(cv2))
                num = (jnp.dot(p.astype(BF16), va, preferred_element_type=F32)
                       + jnp.dot(pc.astype(BF16), cva, preferred_element_type=F32))
                den = pltpu.roll(num, HEAD_DIM, 1)
                o_acc = jnp.where(msk, num / den, o_acc)
                lse_acc = jnp.where(msk, m + jnp.log(den) * LOG2E, lse_acc)
            ob_ref[rs, :] = o_acc.astype(BF16)
            lse_ref[rs, :] = lse_acc
            bg = bg_ref[rs, :].astype(F32)
            outb_ref[rs, :] = (o_acc * (bg * _sigmoid(bg))).astype(BF16)

        @pl.when(jnp.logical_and(pl.program_id(0) == pl.num_programs(0) - 1, pl.program_id(1) == n_st - 1))
        def _():
            for r, (ex, _) in enumerate(riders):
                ex.wait(rid_src[r], rid_dst[r], *rid_sems[3 * r:3 * r + 3])

    tile = pl.BlockSpec((att_sub * TQ, 128), lambda hp, st: (st, hp))
    res = pl.pallas_call(
        body, name="attn_fwd", grid=(4, n_st),
        in_specs=_attn_specs(T, C, att_sub) + [ANY_SPEC] * n_rid,
        out_specs=[tile, tile, tile] + [ANY_SPEC] * n_rid,
        out_shape=[jax.ShapeDtypeStruct((T, 512), BF16), jax.ShapeDtypeStruct((T, 512), BF16),
                   jax.ShapeDtypeStruct((T, 512), F32)] + [ex.out_shape for ex, _ in riders],
        scratch_shapes=[pltpu.VMEM((3, 2, TQ, TK), F32)] + [s for ex, _ in riders for s in ex.scratch],
        compiler_params=_params(("arbitrary", "arbitrary")),
    )(qs, kn, z, ckn, zc, pairs, row_mask, z, *[a for _, a in riders])
    return res[:3], res[3:]


def _outproj_loss_bwd(x, tgt, out_a, out_b, mod, w_out, tm):
    T = x.shape[0]
    nt = T // tm

    def body(x_ref, t_ref, oa_ref, ob_ref, gate_ref, w_ref, dy_ref, dmc_ref, dw_ref, dgate_ref, loss_ref, acc):
        @pl.when(pl.program_id(0) == 0)
        def _():
            acc[...] = jnp.zeros_like(acc)
            dgate_ref[...] = jnp.zeros_like(dgate_ref)
            loss_ref[...] = jnp.zeros_like(loss_ref)

        oa, ob = oa_ref[...], ob_ref[...]
        gate_v = gate_ref[0]
        mix = (jnp.dot(oa, w_ref[0:512, :], preferred_element_type=F32)
               + jnp.dot(ob, w_ref[512:1024, :], preferred_element_type=F32))
        e = x_ref[...] + gate_v * mix - t_ref[...]
        se = jnp.sum(jnp.sum(e * e, axis=0, keepdims=True), axis=1, keepdims=True)
        loss_ref[...] += jnp.broadcast_to(se * (0.5 / D_MODEL), loss_ref.shape)
        dy = e * (1.0 / D_MODEL)
        dy_ref[...] = dy
        dgate_ref[...] += jnp.sum(dy * mix, axis=0, keepdims=True)
        dmix = (dy * gate_v).astype(BF16)
        dmc_ref[...] = _nt(dmix, w_ref[...]).astype(BF16)
        acc[0:512, :] += _tn(oa, dmix)
        acc[512:1024, :] += _tn(ob, dmix)

        @pl.when(pl.program_id(0) == nt - 1)
        def _():
            dw_ref[...] = acc[...].astype(BF16)

    row = lambda w: pl.BlockSpec((tm, w), lambda i: (i, 0))
    return pl.pallas_call(
        body, name="outproj_loss_bwd", grid=(nt,),
        in_specs=[row(D_MODEL), row(D_MODEL), row(512), row(512), _mod_row(MOD_GATE),
                  pl.BlockSpec((D_MODEL, D_MODEL), lambda i: (0, 0))],
        out_specs=[row(D_MODEL), row(D_MODEL), pl.BlockSpec((D_MODEL, D_MODEL), lambda i: (0, 0)),
                   pl.BlockSpec((1, D_MODEL), lambda i: (0, 0)), pl.BlockSpec((1, 128), lambda i: (0, 0))],
        out_shape=[jax.ShapeDtypeStruct((T, D_MODEL), F32), jax.ShapeDtypeStruct((T, D_MODEL), BF16),
                   jax.ShapeDtypeStruct((D_MODEL, D_MODEL), BF16), jax.ShapeDtypeStruct((1, D_MODEL), F32),
                   jax.ShapeDtypeStruct((1, 128), F32)],
        scratch_shapes=[pltpu.VMEM((D_MODEL, D_MODEL), F32)],
        compiler_params=_params(("arbitrary",)),
    )(x, tgt, out_a, out_b, mod, w_out)


def _attn_bwd(qs, kn, z, ckn, zc, pairs, row_mask, ob, lse, dmc, hsel, fold, riders):
    T, C = qs.shape[0], ckn.shape[0]
    rows = T // GRID_W
    nrb = rows // Q_ROWS
    att_sub = min(ATT_SUB, nrb)
    n_st = nrb // att_sub
    n_rid = len(riders)

    def body(q_ref, k_ref, v_ref, ck_ref, cv_ref, pairs_ref, mask_ref, bg_ref, ob_ref, lse_ref, do_ref,
             hsel_ref, fold_ref, *rest):
        rid_src = rest[:n_rid]
        dq_ref, dk_ref, dv_ref, dck_ref, dcv_ref, dbg_ref, drpb_ref = rest[n_rid:n_rid + 7]
        rid_dst = rest[n_rid + 7:2 * n_rid + 7]
        bias_sc, dacc_ref = rest[2 * n_rid + 7:2 * n_rid + 9]
        rid_sems = rest[2 * n_rid + 9:]
        hp, st = pl.program_id(0), pl.program_id(1)

        @pl.when(jnp.logical_and(hp == 0, st == 0))
        def _():
            for r, (ex, _) in enumerate(riders):
                ex.start(rid_src[r], rid_dst[r], *rid_sems[3 * r:3 * r + 3])

        @pl.when(st == 0)
        def _():
            _build_bias(pairs_ref, mask_ref, bias_sc)
            dk_ref[...] = jnp.zeros_like(dk_ref)
            dv_ref[...] = jnp.zeros_like(dv_ref)
            dck_ref[...] = jnp.zeros_like(dck_ref)
            dcv_ref[...] = jnp.zeros_like(dcv_ref)
            dacc_ref[...] = jnp.zeros_like(dacc_ref)

        ck2, cv2 = ck_ref[...], cv_ref[...]
        lane = lax.broadcasted_iota(jnp.int32, (1, 128), 1)
        for sub in range(att_sub):
            rb = att_sub * st + sub
            bias_ref = bias_sc.at[_attn_type(rb, nrb)]
            rs = slice(TQ * sub, TQ * (sub + 1))
            kb = jnp.clip(Q_ROWS * rb - 4, 0, rows - K_ROWS)
            ks = pl.multiple_of(kb * GRID_W, GRID_W)
            ebase = kb - Q_ROWS * rb + 11
            q2 = q_ref[rs, :]
            k2 = k_ref[pl.ds(ks, TK), :]
            v2 = v_ref[pl.ds(ks, TK), :]
            bg = bg_ref[rs, :].astype(F32)
            sig = _sigmoid(bg)
            obv = ob_ref[rs, :].astype(F32)
            dout = do_ref[rs, :].astype(F32)
            dbg_ref[rs, :] = (dout * obv * (sig * (1.0 + bg * (1.0 - sig)))).astype(BF16)
            d_o = dout * (bg * sig)
            d_oo = d_o * obv
            lse2 = lse_ref[rs, :]
            dq_acc = jnp.zeros((TQ, 128), F32)
            for hh in range(2):
                msk = (lane >= HEAD_DIM) == bool(hh)
                qm = jnp.where(msk, q2, jnp.zeros_like(q2))
                lse_h = jnp.max(jnp.where(msk, lse2, -jnp.inf), axis=-1, keepdims=True)
                p = jnp.exp2(_nt(qm, k2) + bias_ref[hh] - lse_h)
                pc = jnp.exp2(_nt(qm, ck2) - lse_h)
                dom_f = jnp.where(msk, d_o, 0.0)
                dom = dom_f.astype(BF16)
                delta = jnp.sum(jnp.where(msk, d_oo, 0.0), axis=-1, keepdims=True)
                d_hi = delta.astype(BF16).astype(F32)
                x0 = HEAD_DIM * (1 - hh)
                dom_aug = jnp.where(lane == x0, -d_hi, jnp.where(lane == x0 + 1, d_hi - delta, dom_f)).astype(BF16)
                extra = jnp.logical_or(lane == x0, lane == x0 + 1)
                va = jnp.where(msk, v2, jnp.where(extra, jnp.ones_like(v2), jnp.zeros_like(v2)))
                cva = jnp.where(msk, cv2, jnp.where(extra, jnp.ones_like(cv2), jnp.zeros_like(cv2)))
                ds = p * _nt(dom_aug, va)
                dsc = pc * _nt(dom_aug, cva)
                dsb, dscb = ds.astype(BF16), dsc.astype(BF16)
                dq_h = (jnp.dot(dsb, k2, preferred_element_type=F32)
                        + jnp.dot(dscb, ck2, preferred_element_type=F32))
                dq_acc = jnp.where(msk, dq_h, dq_acc)
                dk_ref[pl.ds(ks, TK), :] += _tn(dsb, qm)
                dv_ref[pl.ds(ks, TK), :] += _tn(p.astype(BF16), dom)
                dck_ref[...] += _tn(dscb, qm)
                dcv_ref[...] += _tn(pc.astype(BF16), dom)
                for i in range(Q_ROWS):
                    for mm in range(K_ROWS // 2):
                        dacc_ref[hh, ebase + (2 * mm - i)] += ds[GRID_W * i:GRID_W * (i + 1),
                                                                 128 * mm:128 * (mm + 1)]
            dq_ref[rs, :] = dq_acc

        @pl.when(st == n_st - 1)
        def _():
            for hh in range(2):
                drpb_ref[hh] = _rpb_diag_sums(dacc_ref.at[hh], hsel_ref, fold_ref)

        @pl.when(jnp.logical_and(hp == pl.num_programs(0) - 1, st == n_st - 1))
        def _():
            for r, (ex, _) in enumerate(riders):
                ex.wait(rid_src[r], rid_dst[r], *rid_sems[3 * r:3 * r + 3])

    tile = pl.BlockSpec((att_sub * TQ, 128), lambda hp, st: (st, hp))
    colT = pl.BlockSpec((T, 128), lambda hp, st: (0, hp))
    colC = pl.BlockSpec((C, 128), lambda hp, st: (0, hp))
    res = pl.pallas_call(
        body, name="attn_bwd", grid=(4, n_st),
        in_specs=(_attn_specs(T, C, att_sub) + [tile, tile,
                                                pl.BlockSpec((att_sub * TQ, 128), lambda hp, st: (st, 4 + hp)),
                                       pl.BlockSpec((8, 128, 128), lambda hp, st: (0, 0, 0)),
                                       pl.BlockSpec((32, 256), lambda hp, st: (0, 0))]
                  + [ANY_SPEC] * n_rid),
        out_specs=([tile, colT, colT, colC, colC, tile, pl.BlockSpec((2, 32, 128), lambda hp, st: (hp, 0, 0))]
                   + [ANY_SPEC] * n_rid),
        out_shape=([jax.ShapeDtypeStruct((T, 512), F32), jax.ShapeDtypeStruct((T, 512), F32),
                    jax.ShapeDtypeStruct((T, 512), F32), jax.ShapeDtypeStruct((C, 512), F32),
                    jax.ShapeDtypeStruct((C, 512), F32), jax.ShapeDtypeStruct((T, 512), BF16),
                    jax.ShapeDtypeStruct((N_HEADS, 32, 128), F32)] + [ex.out_shape for ex, _ in riders]),
        scratch_shapes=([pltpu.VMEM((3, 2, TQ, TK), F32), pltpu.VMEM((2, N_DIAG, GRID_W, 128), F32)]
                        + [s for ex, _ in riders for s in ex.scratch]),
        compiler_params=_params(("arbitrary", "arbitrary")),
    )(qs, kn, z, ckn, zc, pairs, row_mask, z, ob, lse, dmc, hsel, fold, *[a for _, a in riders])
    return res[:7], res[7:]


def _rpb_diag_sums(a_ref, hsel_ref, fold_ref):
    n_off = 2 * WIN_C - 1
    n_dr = 2 * WIN_R - 1
    qc = lax.broadcasted_iota(jnp.int32, (GRID_W, 128), 0)
    lane = lax.broadcasted_iota(jnp.int32, (GRID_W, 128), 1)
    diff = lane % GRID_W - qc + (WIN_C - 1)
    left = lane < GRID_W

    def by_dr(dr):
        return a_ref[dr + 4] + pltpu.roll(a_ref[dr + 3], GRID_W, 1)

    out = jnp.zeros((32, 128), F32)
    for j in range((n_dr + 1) // 2):
        hi = pltpu.roll(by_dr(2 * j + 1), GRID_W, 1) if 2 * j + 1 < n_dr else 0.0
        pair = jnp.where(left, by_dr(2 * j), hi)
        parts = []
        for o in range(n_off):
            mv = jnp.where(diff == o, pair, 0.0)
            acc = mv[0:8]
            for r8 in range(1, GRID_W // 8):
                acc = acc + mv[8 * r8:8 * (r8 + 1)]
            parts.append(acc)
        parts.append(jnp.zeros((8, 128), F32))
        stack = jnp.concatenate(parts, axis=0)
        s_hi = stack.astype(BF16)
        s_lo = (stack - s_hi.astype(F32)).astype(BF16)
        per_o = (jnp.dot(fold_ref[...], s_hi, preferred_element_type=F32)
                 + jnp.dot(fold_ref[...], s_lo, preferred_element_type=F32))
        out = out + _dot2(per_o, hsel_ref[j])
    return out


def _head_norm_bwd(raw, dn, gain, ones_ref):
    rr = lax.rsqrt(_head_sum(raw * raw, ones_ref) * (1.0 / HEAD_DIM) + EPS)
    hat = raw * rr
    dgain = jnp.sum(dn * hat, axis=0, keepdims=True)
    dhat = dn * gain
    mean = _head_sum(dhat * hat, ones_ref) * (1.0 / HEAD_DIM)
    return rr * (dhat - hat * mean), dgain


def _ctx_k_bwd(zc, dck, dcv, ones_blk, gk, foldm):
    C = dck.shape[0]

    def body(bk_ref, dk_ref, dv_ref, ones_ref, gk_ref, fold_ref, dbk_ref, dbv_ref, dgk_ref):
        dbk, dgk = _head_norm_bwd(bk_ref[...].astype(F32), dk_ref[...] * LN2, gk_ref[...], ones_ref)
        dbk_ref[...] = dbk.astype(BF16)
        dbv_ref[...] = dv_ref[...].astype(BF16)
        dgk_ref[...] = jnp.dot(jnp.broadcast_to(dgk, (8, 512)), fold_ref[...],
                               preferred_element_type=F32, precision=HI)

    row = pl.BlockSpec((C, 512), lambda i: (0, 0))
    cst = lambda a, b: pl.BlockSpec((a, b), lambda i: (0, 0))
    out_row = jax.ShapeDtypeStruct((C, 512), BF16)
    return pl.pallas_call(
        body, name="ctx_k_bwd", grid=(1,),
        in_specs=[row, row, row, cst(512, 512), cst(1, 512), cst(512, 128)],
        out_specs=[row, row, cst(8, 128)],
        out_shape=[out_row, out_row, jax.ShapeDtypeStruct((8, 128), F32)],
        compiler_params=_params(("arbitrary",)),
    )(zc, dck, dcv, ones_blk, gk, foldm)


def _bwd_mid(z, dmc, dqs, dk, dv, db_g, h, hc, dzc_k, dzc_v, sgn, ws, wst, bsb, ones8, ones_blk, gq, gk, foldm, tk,
             riders=()):
    T = z.shape[0]
    nt = T // tk
    blk = D_IN // N_DEV
    n_in, n_out, n_sc = 23, 7, 9
    n_rid = len(riders)

    def body(*refs):
        (au_ref, av_ref, ag_ref, bq_ref, bk_ref, d_ref, dq_ref, dk_ref, dv_ref, dbg_ref, h_ref,
         hc_ref, dzck_ref, dzcv_ref, sg_ref, ws_ref, wst_ref, bsb_ref, ones8_ref, ones_ref, gq_ref, gk_ref,
         fold_ref) = refs[:n_in]
        rid_src = refs[n_in:n_in + n_rid]
        dz_ref, sums_out, dws_ref, dbs_ref, dsg_ref, dgq_ref, dgk_ref = refs[n_in + n_rid:n_in + n_rid + n_out]
        rid_dst = refs[n_in + n_rid + n_out:n_in + 2 * n_rid + n_out]
        (acc, accq, acck, stage, sem, send_buf, tmp, s1, r1) = refs[n_in + 2 * n_rid + n_out:
                                                                   n_in + 2 * n_rid + n_out + n_sc]
        rid_sems = refs[n_in + 2 * n_rid + n_out + n_sc:]
        t = pl.program_id(0)

        @pl.when(t == 0)
        def _():
            for r, (ex, _) in enumerate(riders):
                ex.start(rid_src[r], rid_dst[r], *rid_sems[3 * r:3 * r + 3])
            acc[...] = jnp.zeros_like(acc)
            acc[512 * 4:512 * 5, :] = _tn(dzck_ref[...], hc_ref[...])
            acc[512 * 5:512 * 6, :] = _tn(dzcv_ref[...], hc_ref[...])
            dws_ref[...] = jnp.zeros_like(dws_ref)
            dbs_ref[...] = jnp.zeros_like(dbs_ref)
            dsg_ref[...] = jnp.zeros_like(dsg_ref)
            accq[...] = jnp.zeros_like(accq)
            acck[...] = jnp.zeros_like(acck)

        for g in range(SGU_GROUPS):
            ws_bf = ws_ref[g].astype(BF16)
            wst_bf = wst_ref[g].astype(BF16)
            sg = sg_ref[:, 128 * g:128 * (g + 1)]
            bsb_g = bsb_ref[g]
            for j in range(tk // CHUNK):
                rs, cs = slice(CHUNK * j, CHUNK * (j + 1)), slice(128 * g, 128 * (g + 1))
                au, av, ag = (au_ref[rs, cs].astype(F32), av_ref[rs, cs].astype(F32), ag_ref[rs, cs].astype(F32))
                d = d_ref[rs, cs].astype(F32)
                _, (gu, dgu, dgv, rr, vhat, vn, mixed, sig, sl) = _sgu_chunk_fwd(au, av, ag, sg, ws_bf, bsb_g)
                dz_ref[rs, 128 * g:128 * (g + 1)] = (d * mixed * sl * dgu).astype(BF16)
                dz_ref[rs, 1024 + 128 * g:1024 + 128 * (g + 1)] = (
                    d * gu * mixed * (sig * (1.0 + ag * (1.0 - sig)))).astype(BF16)
                dmixed = d * gu * sl
                dmb = dmixed.astype(BF16)
                dm_lo = (dmixed - dmb.astype(F32)).astype(BF16)
                dbs_ref[g] += _nt(ones8_ref[...], dmb) + _nt(ones8_ref[...], dm_lo)
                dws_ref[g] += _nt(dmb, vn.astype(BF16))
                dvn = jnp.dot(wst_bf, dmb, preferred_element_type=F32)
                dsg_ref[:, 128 * g:128 * (g + 1)] += jnp.sum(dvn * vhat, axis=0, keepdims=True)
                dvhat = dvn * sg
                mean = jnp.mean(dvhat * vhat, axis=-1, keepdims=True)
                dz_ref[rs, 512 + 128 * g:512 + 128 * (g + 1)] = (rr * (dvhat - vhat * mean) * dgv).astype(BF16)

        dbq, dgq = _head_norm_bwd(bq_ref[...].astype(F32), dq_ref[...] * ATT_SCALE, gq_ref[...], ones_ref)
        dz_ref[:, 512 * 3:512 * 4] = dbq.astype(BF16)
        accq[...] += dgq
        dbk, dgk = _head_norm_bwd(bk_ref[...].astype(F32), dk_ref[...] * LN2, gk_ref[...], ones_ref)
        dz_ref[:, 512 * 4:512 * 5] = dbk.astype(BF16)
        acck[...] += dgk
        dz_ref[:, 512 * 5:512 * 6] = dv_ref[...].astype(BF16)
        dz_ref[:, 512 * 6:512 * 7] = dbg_ref[...]

        hv = h_ref[...]
        for k in range(N_BRANCH):
            acc[512 * k:512 * (k + 1), :] += _tn(dz_ref[:, 512 * k:512 * (k + 1)], hv)

        @pl.when(t == nt - 1)
        def _():
            dgq_ref[...] = jnp.dot(jnp.broadcast_to(accq[...], (8, 512)), fold_ref[...],
                                   preferred_element_type=F32, precision=HI)
            dgk_ref[...] = jnp.dot(jnp.broadcast_to(acck[...], (8, 512)), fold_ref[...],
                                   preferred_element_type=F32, precision=HI)
            cidx = lax.axis_index("c")
            sib = (lax.axis_index("x"), lax.axis_index("y"), 1 - cidx)
            swaps = []
            for q in range(N_DEV // 2):
                theirs = acc[pl.ds(pl.multiple_of(2 * blk * q + blk * (1 - cidx), 8), blk), :]
                send_buf[q] = theirs.astype(BF16)
                cp = pltpu.make_async_remote_copy(src_ref=send_buf.at[q], dst_ref=tmp.at[q], send_sem=s1.at[q],
                                                  recv_sem=r1.at[q], device_id=sib, device_id_type=MESH)
                cp.start()
                swaps.append(cp)
            for q in range(N_DEV // 2):
                swaps[q].wait_recv()
                mine = acc[pl.ds(pl.multiple_of(2 * blk * q + blk * cidx, 8), blk), :]
                stage[...] = (mine + tmp[q].astype(F32)).astype(BF16)
                out = pltpu.make_async_copy(stage, sums_out.at[q], sem)
                out.start()
                out.wait()
            for cp in swaps:
                cp.wait_send()
            for r, (ex, _) in enumerate(riders):
                ex.wait(rid_src[r], rid_dst[r], *rid_sems[3 * r:3 * r + 3])

    zcol = lambda col: pl.BlockSpec((tk, 512), lambda t: (t, col))
    row = pl.BlockSpec((tk, 512), lambda t: (t, 0))
    whole = lambda a: pl.BlockSpec(a.shape, lambda t: (0,) * a.ndim)
    res = pl.pallas_call(
        body, name="bwd_mid", grid=(nt,),
        in_specs=[zcol(0), zcol(1), zcol(2), zcol(3), zcol(4), row, row, row, row, row,
                  pl.BlockSpec((tk, D_MODEL), lambda t: (t, 0)), whole(hc), whole(dzc_k), whole(dzc_v),
                  whole(sgn), whole(ws), whole(wst), whole(bsb), whole(ones8), whole(ones_blk), whole(gq), whole(gk),
                  whole(foldm)] + [ANY_SPEC] * n_rid,
        out_specs=[pl.BlockSpec((tk, D_IN), lambda t: (t, 0)), ANY_SPEC,
                   pl.BlockSpec((SGU_GROUPS, CHUNK, CHUNK), lambda t: (0, 0, 0)),
                   pl.BlockSpec((SGU_GROUPS, 8, CHUNK), lambda t: (0, 0, 0)),
                   pl.BlockSpec((1, 512), lambda t: (0, 0)), pl.BlockSpec((8, 128), lambda t: (0, 0)),
                   pl.BlockSpec((8, 128), lambda t: (0, 0))] + [ANY_SPEC] * n_rid,
        out_shape=[jax.ShapeDtypeStruct((T, D_IN), BF16), jax.ShapeDtypeStruct((N_DEV // 2, blk, D_MODEL), BF16),
                   jax.ShapeDtypeStruct((SGU_GROUPS, CHUNK, CHUNK), F32),
                   jax.ShapeDtypeStruct((SGU_GROUPS, 8, CHUNK), F32), jax.ShapeDtypeStruct((1, 512), F32),
                   jax.ShapeDtypeStruct((8, 128), F32), jax.ShapeDtypeStruct((8, 128), F32)]
                  + [ex.out_shape for ex, _ in riders],
        scratch_shapes=[pltpu.VMEM((D_IN, D_MODEL), F32), pltpu.VMEM((1, 512), F32), pltpu.VMEM((1, 512), F32),
                        pltpu.VMEM((blk, D_MODEL), BF16), pltpu.SemaphoreType.DMA,
                        pltpu.VMEM((N_DEV // 2, blk, D_MODEL), BF16), pltpu.VMEM((N_DEV // 2, blk, D_MODEL), BF16),
                        pltpu.SemaphoreType.DMA((N_DEV // 2,)), pltpu.SemaphoreType.DMA((N_DEV // 2,))]
                       + [s for ex, _ in riders for s in ex.scratch],
        compiler_params=_params(("arbitrary",)),
    )(z, z, z, z, z, dmc, dqs, dk, dv, db_g, h, hc, dzc_k, dzc_v, sgn, ws, wst, bsb, ones8, ones_blk, gq, gk, foldm,
      *[a for _, a in riders])
    return res[:n_out], res[n_out:]


def _inproj_bwd_dx(dzs, w_row0, w_in_t, x, dy, ng, mod, scale_row, tm, name, riders=()):
    T = x.shape[0]
    n = len(dzs)
    wpc = dzs[0].shape[1]
    nt = T // tm
    with_dx = dy is not None
    n_own_in = n + 4 + with_dx
    n_own_out = 3 + with_dx
    n_rid = len(riders)

    def body(*refs):
        dz_refs = refs[:n]
        n_in = n_own_in + n_rid
        own = refs[n:n_own_in] + refs[n_in:n_in + n_own_out]
        rid_src = refs[n_own_in:n_own_in + n_rid]
        rid_dst = refs[n_in + n_own_out:n_in + n_own_out + n_rid]
        rid_sems = refs[n_in + n_own_out + n_rid:]
        if with_dx:
            w_ref, x_ref, dy_ref, g_ref, sc_ref, gx_ref, dsh_ref, dsc_ref, dg_ref = own
        else:
            w_ref, x_ref, g_ref, sc_ref, dsh_ref, dsc_ref, dg_ref = own

        @pl.when(pl.program_id(0) == 0)
        def _():
            for r, (ex, _) in enumerate(riders):
                ex.start(rid_src[r], rid_dst[r], *rid_sems[3 * r:3 * r + 3])
            dsh_ref[...] = jnp.zeros_like(dsh_ref)
            dsc_ref[...] = jnp.zeros_like(dsc_ref)
            dg_ref[...] = jnp.zeros_like(dg_ref)

        dh = jnp.dot(dz_refs[0][...], w_ref[0:wpc, :], preferred_element_type=F32)
        for k in range(1, n):
            dh = dh + jnp.dot(dz_refs[k][...], w_ref[wpc * k:wpc * (k + 1), :], preferred_element_type=F32)
        xv = x_ref[...]
        r = lax.rsqrt(jnp.mean(xv * xv, axis=-1, keepdims=True) + EPS)
        xn = xv * r
        gv, op = g_ref[...], 1.0 + sc_ref[0]
        dsh_ref[...] += jnp.sum(dh, axis=0, keepdims=True)
        dsc_ref[...] += jnp.sum(dh * xn * gv, axis=0, keepdims=True)
        dg_ref[...] += jnp.sum(dh * op * xn, axis=0, keepdims=True)
        if with_dx:
            dxn = dh * (gv * op)
            gx_ref[...] = r * (dxn - xn * jnp.mean(dxn * xn, axis=-1, keepdims=True)) + dy_ref[...]

        @pl.when(pl.program_id(0) == nt - 1)
        def _():
            for r, (ex, _) in enumerate(riders):
                ex.wait(rid_src[r], rid_dst[r], *rid_sems[3 * r:3 * r + 3])

    vec = pl.BlockSpec((1, D_MODEL), lambda i: (0, 0))
    rowf = pl.BlockSpec((tm, D_MODEL), lambda i: (i, 0))
    in_specs = [pl.BlockSpec((tm, wpc), lambda i: (i, 0))] * n
    in_specs += [pl.BlockSpec((wpc * n, D_MODEL), lambda i: (w_row0 // (wpc * n), 0)), rowf]
    args = list(dzs) + [w_in_t, x]
    vshape = jax.ShapeDtypeStruct((1, D_MODEL), F32)
    out_specs, out_shape = [vec, vec, vec], [vshape, vshape, vshape]
    if with_dx:
        in_specs.append(rowf)
        args.append(dy)
        out_specs = [rowf] + out_specs
        out_shape = [jax.ShapeDtypeStruct((T, D_MODEL), F32)] + out_shape
    in_specs += [vec, _mod_row(scale_row)] + [ANY_SPEC] * n_rid
    args += [ng, mod] + [a for _, a in riders]
    res = pl.pallas_call(
        body, name=name, grid=(nt,), in_specs=in_specs, out_specs=out_specs + [ANY_SPEC] * n_rid,
        out_shape=out_shape + [ex.out_shape for ex, _ in riders],
        scratch_shapes=[s for ex, _ in riders for s in ex.scratch],
        compiler_params=_params(("arbitrary",)),
    )(*args)
    return res[:n_own_out], res[n_own_out:]


def _adamw_sharded(w, gparts, m, v, tr, name, riders=()):
    R, C = w.shape
    n_part = gparts.shape[0]
    nt = R // tr
    n_rid = len(riders)

    def body(w_ref, gp_ref, m_ref, v_ref, *rest):
        rid_src = rest[:n_rid]
        g_ref, d_ref, m2_ref, v2_ref = rest[n_rid:n_rid + 4]
        rid_dst = rest[n_rid + 4:2 * n_rid + 4]
        rid_sems = rest[2 * n_rid + 4:]

        @pl.when(pl.program_id(0) == 0)
        def _():
            for r, (ex, _) in enumerate(riders):
                ex.start(rid_src[r], rid_dst[r], *rid_sems[3 * r:3 * r + 3])

        g = gp_ref[0].astype(F32)
        for d in range(1, n_part):
            g = g + gp_ref[d].astype(F32)
        delta, m2, v2 = _adam(w_ref[...], g, m_ref[...], v_ref[...])
        g_ref[...] = g
        d_ref[...] = delta
        m2_ref[...] = m2
        v2_ref[...] = v2

        @pl.when(pl.program_id(0) == nt - 1)
        def _():
            for r, (ex, _) in enumerate(riders):
                ex.wait(rid_src[r], rid_dst[r], *rid_sems[3 * r:3 * r + 3])

    row = pl.BlockSpec((tr, C), lambda i: (i, 0))
    sh = jax.ShapeDtypeStruct((R, C), F32)
    res = pl.pallas_call(
        body, name=name, grid=(nt,),
        in_specs=[row, pl.BlockSpec((n_part, tr, C), lambda i: (0, i, 0)), row, row] + [ANY_SPEC] * n_rid,
        out_specs=[row, row, row, row] + [ANY_SPEC] * n_rid,
        out_shape=[sh, sh, sh, sh] + [ex.out_shape for ex, _ in riders],
        scratch_shapes=[s for ex, _ in riders for s in ex.scratch],
        compiler_params=_params(("arbitrary",)),
    )(w, gparts, m, v, *[a for _, a in riders])
    return res[:4], res[4:]


def _pack_vectors(vec_rows, dsg, dgq, dgk, dgk_c, loss_part):
    flat = [a for _, arrs in vec_rows for a in arrs]

    def body(*refs):
        vecs = list(refs[:len(flat)])
        dsg_ref, dgq_ref, dgk_ref, dgkc_ref, loss_ref, v_ref = refs[len(flat):]
        row = lax.broadcasted_iota(jnp.int32, (16, D_MODEL), 0)
        misc = jnp.concatenate([dsg_ref[...], dgq_ref[0:1, :], dgk_ref[0:1, :], dgkc_ref[0:1, :],
                                loss_ref[...]], axis=1)
        v = jnp.where(row == V_MISC, jnp.broadcast_to(misc, (16, D_MODEL)), 0.0)
        for r, arrs in vec_rows:
            val = vecs.pop(0)[...]
            for _ in arrs[1:]:
                val = val + vecs.pop(0)[...]
            v = jnp.where(row == r, jnp.broadcast_to(val, (16, D_MODEL)), v)
        v_ref[...] = v

    return pl.pallas_call(
        body, name="pack_vectors", out_shape=jax.ShapeDtypeStruct((16, D_MODEL), F32), compiler_params=_params(),
    )(*flat, dsg, dgq, dgk, dgk_c, loss_part)


SMALL_NAMES = ("b_ada", "norm_g", "sgu_norm_g", "w_spatial", "b_spatial", "q_norm_g", "k_norm_g", "rpb")


def _adamw_small(vg, sg, rg, ws, ms, vs):
    k = len(SMALL_NAMES)

    def body(*refs):
        vg_ref, sg_ref, rg_ref = refs[0], refs[1], refs[2]
        refs = refs[1:]
        w_refs = dict(zip(SMALL_NAMES, refs[2:2 + k]))
        m_refs = dict(zip(SMALL_NAMES, refs[2 + k:2 + 2 * k]))
        v_refs = dict(zip(SMALL_NAMES, refs[2 + 2 * k:2 + 3 * k]))
        o_refs = [dict(zip(SMALL_NAMES, refs[2 + (3 + i) * k:2 + (4 + i) * k])) for i in range(4)]
        loss_ref = refs[2 + 7 * k]

        sv = vg_ref[0]
        for d in range(1, N_DEV):
            sv = sv + vg_ref[d]
        loss_ref[...] = sv[V_MISC:V_MISC + 1, 896:1024]

        def total(lo, hi, ref=sg_ref):
            s = ref[0, lo:hi, :].astype(F32)
            for d in range(1, N_DEV):
                s = s + ref[d, lo:hi, :].astype(F32)
            return s

        def emit(name, idx, g):
            res = _adam(w_refs[name][idx], g, m_refs[name][idx], v_refs[name][idx])
            for o, val in zip(o_refs, (g,) + res):
                o[name][idx] = val

        everything = (slice(None), slice(None))
        row = lambda r: sv[r:r + 1, :]
        emit("b_ada", everything, jnp.concatenate(
            [row(V_DSHIFT) + row(V_DCSHIFT), row(V_DSCALE) + row(V_DCSCALE), row(V_DGATE)], axis=1))
        emit("norm_g", everything, row(V_DNG) + row(V_DNG_CTX))
        misc = row(V_MISC)
        emit("sgu_norm_g", everything, misc[:, 0:512])
        emit("q_norm_g", everything, misc[:, 512:512 + HEAD_DIM])
        emit("k_norm_g", everything, misc[:, 640:640 + HEAD_DIM] + misc[:, 768:768 + HEAD_DIM])
        for g in range(SGU_GROUPS):
            emit("w_spatial", (0, g), total(128 * g, 128 * (g + 1)))
            emit("b_spatial", (0, slice(g, g + 1), slice(None)), total(M_DBS + 8 * g, M_DBS + 8 * (g + 1))[0:1, :])
        for hd in range(N_HEADS):
            by_dc = total(32 * hd, 32 * (hd + 1), rg_ref)
            emit("rpb", (0, hd), by_dc.T[0:2 * WIN_R - 1, 0:2 * WIN_C - 1])

    shapes = [jax.ShapeDtypeStruct(w.shape, F32) for w in ws]
    res = pl.pallas_call(body, name="adamw_small", out_shape=shapes * 4 + [jax.ShapeDtypeStruct((1, 128), F32)],
                         compiler_params=_params())(vg, sg, rg, *ws, *ms, *vs)
    return [res[i * k:(i + 1) * k] for i in range(4)], res[4 * k]


def _adamw_cctx(pc_g, w, m, v):
    def body(pc_ref, w_ref, m_ref, v_ref, g_ref, d_ref, m2_ref, v2_ref):
        pc = pc_ref[0, 0:1, :]
        for d in range(1, N_DEV):
            pc = pc + pc_ref[d, 0:1, :]
        cc = w_ref[...]
        sig = _sigmoid(cc)
        g = pc * (sig * (1.0 + cc * (1.0 - sig)))
        delta, m2, v2 = _adam(cc, g, m_ref[...], v_ref[...])
        g_ref[...] = g
        d_ref[...] = delta
        m2_ref[...] = m2
        v2_ref[...] = v2

    sh = jax.ShapeDtypeStruct((1, D_MODEL), F32)
    return pl.pallas_call(body, name="adamw_cctx", out_shape=[sh, sh, sh, sh], compiler_params=_params())(
        pc_g, w, m, v)


def _block_ones(n, blk):
    i = np.arange(n)
    return jnp.asarray((i[:, None] // blk == i[None, :] // blk).astype(np.float32), BF16)


def _rpb_pairs(rpb):
    n_off = 2 * WIN_C - 1
    cols = np.arange(GRID_W)
    c0 = np.clip(cols - WIN_C // 2, 0, GRID_W - WIN_C)
    in_win = (cols[None, :] >= c0[:, None]) & (cols[None, :] < c0[:, None] + WIN_C)
    dc = np.clip(cols[None, :] - cols[:, None] + (WIN_C - 1), 0, n_off - 1)
    expand = (dc[None] == np.arange(n_off)[:, None, None]) & in_win[None]
    toep = jnp.einsum("hrd,dqk->hrqk", rpb, jnp.asarray(expand, F32), precision=HI)
    toep = toep + jnp.asarray(np.where(in_win, 0.0, NEG_INF).astype(np.float32))
    neg = jnp.full((N_HEADS, 1, GRID_W, GRID_W), NEG_INF, F32)
    ext = jnp.concatenate([neg, toep, neg], axis=1)
    return jnp.concatenate([ext[:, :-1], ext[:, 1:]], axis=-1)


def _row_mask(rows):
    nrb = rows // Q_ROWS
    valid = np.zeros((3, Q_ROWS, 1, K_ROWS, 1), bool)
    for t, rb in enumerate((0, 1, nrb - 1)):
        kb = int(np.clip(Q_ROWS * rb - 4, 0, rows - K_ROWS))
        for i in range(Q_ROWS):
            r0 = int(np.clip(Q_ROWS * rb + i - WIN_R // 2, 0, rows - WIN_R))
            for j in range(K_ROWS):
                valid[t, i, 0, j, 0] = r0 <= kb + j < r0 + WIN_R
    full = np.broadcast_to(valid, (3, Q_ROWS, GRID_W, K_ROWS, GRID_W)).reshape(3, TQ, TK)
    return jnp.asarray(np.where(full, 0.0, NEG_INF).astype(np.float32))


def kernel(x, c, ctx, c_ctx, w_ada, b_ada, norm_g, w_in, sgu_norm_g, w_spatial, b_spatial, q_norm_g, k_norm_g, rpb, w_out, loss_target, m_c_ctx, m_w_ada, m_b_ada, m_norm_g, m_w_in, m_sgu_norm_g, m_w_spatial, m_b_spatial, m_q_norm_g, m_k_norm_g, m_rpb, m_w_out, v_c_ctx, v_w_ada, v_b_ada, v_norm_g, v_w_in, v_sgu_norm_g, v_w_spatial, v_b_spatial, v_q_norm_g, v_k_norm_g, v_rpb, v_w_out):
    me = 4 * lax.axis_index("x") + 2 * lax.axis_index("y") + lax.axis_index("c")
    x2, ctx2, tgt2 = x[0], ctx[0], loss_target[0]
    T, C = x2.shape[0], ctx2.shape[0]
    rows = T // GRID_W
    wada, win_t, wout = w_ada[0], w_in[0].T, w_out[0]
    ada_w = wada.shape[1]
    win_w = win_t.shape[0]

    row8 = lax.broadcasted_iota(jnp.int32, (8, D_MODEL), 0)
    c_blk = jnp.where(row8 == me, jnp.broadcast_to(c, (8, D_MODEL)), 0.0)
    b_sh = lax.dynamic_slice(b_ada, (0, me * ada_w), (1, ada_w))
    c_ctx_row = c_ctx.reshape(1, D_MODEL)

    ones512 = _block_ones(512, HEAD_DIM)
    ones8 = jnp.ones((8, 128), BF16)
    foldm = jnp.asarray((np.arange(512)[:, None] % HEAD_DIM == np.arange(128)[None, :]).astype(np.float32))
    lane_half = np.arange(128)[None, :, None] // GRID_W
    hsel = jnp.asarray((2 * np.arange(8)[:, None, None] + lane_half == np.arange(128)[None, None, :]).astype(np.float32),
                       BF16)
    foldr = jnp.asarray((np.arange(256)[None, :] // 8 == np.arange(32)[:, None]).astype(np.float32), BF16)
    gq512 = jnp.tile(q_norm_g, (1, N_HEADS))
    gk512 = jnp.tile(k_norm_g, (1, N_HEADS))
    ws = w_spatial[0]
    wst = ws.transpose(0, 2, 1)
    bsb = jnp.broadcast_to(b_spatial[0][:, :, None], (SGU_GROUPS, CHUNK, 128))
    pairs = _rpb_pairs(rpb[0])
    row_mask = _row_mask(rows)

    my_chip = me // 2
    order = jnp.stack([my_chip, my_chip ^ 2, my_chip ^ 1, my_chip ^ 3]).astype(jnp.int32)
    h, z, win_g, s16, mod = _inproj_fwd(order, x2, norm_g, c_blk, c_ctx_row, wada, b_sh, win_t.astype(BF16), 1024)
    w_in_b = win_g.reshape(D_IN, D_MODEL)
    out_a, qs, kn = _sgu_qk_fwd(z, sgu_norm_g, ws, bsb, ones512, gq512, gk512, 512)
    hc, zc, ckn = _ctx_fwd(ctx2, norm_g, mod, w_in_b, ones512, gk512)
    wout_blk = wout.astype(BF16)
    (ob, out_b, lse), (wout_g,) = _attn_fwd(qs, kn, z, ckn, zc, pairs, row_mask, [(_Hosted("ag", wout_blk), wout_blk)])
    w_out_b = wout_g.reshape(D_MODEL, D_MODEL)

    dy, dmc, dw_out, dgate, loss_part = _outproj_loss_bwd(x2, tgt2, out_a, out_b, mod, w_out_b, 512)
    dw_out_blocks = dw_out.reshape(N_DEV, D_MODEL // N_DEV, D_MODEL)
    (dqs, dk, dv, dck, dcv, db_g, drpb), (gout_parts,) = _attn_bwd(
        qs, kn, z, ckn, zc, pairs, row_mask, ob, lse, dmc, hsel, foldr,
        [(_Hosted("a2a", dw_out_blocks), dw_out_blocks)])
    dzc_k, dzc_v, dgk_c = _ctx_k_bwd(zc, dck, dcv, ones512, gk512, foldm)
    rloc = drpb.reshape(N_HEADS * 32, 128)
    (dz, chip_sums, dws, dbs, dsg, dgq, dgk), (rg,) = _bwd_mid(
        z, dmc, dqs, dk, dv, db_g, h, hc, dzc_k, dzc_v, sgu_norm_g, ws, wst, bsb, ones8, ones512, gq512, gk512, foldm,
        512, [(_Hosted("ag", rloc), rloc)])
    sloc = jnp.concatenate([dws.reshape(SGU_GROUPS * CHUNK, CHUNK), dbs.reshape(SGU_GROUPS * 8, CHUNK)]).astype(BF16)
    (grad_x, dshift, dscale, dng), (gin_parts,) = _inproj_bwd_dx(
        [dz], 0, w_in_b, x2, dy, norm_g, mod, MOD_SCALE, 512, "inproj_bwd_dx",
        [(_Hosted("chips", chip_sums), chip_sums)])
    (dcshift, dcscale, dng_c), _ = _inproj_bwd_dx([dzc_k, dzc_v], 4 * 512, w_in_b, ctx2, None, norm_g, mod,
                                                  MOD_CSCALE, C, "ctx_bwd_dx")

    res_in, _ = _adamw_sharded(win_t, gin_parts, m_w_in[0].T, v_w_in[0].T, 112, "adamw_w_in")
    res_out, _ = _adamw_sharded(wout, gout_parts, m_w_out[0], v_w_out[0], 128, "adamw_w_out")

    zero_row = jnp.zeros((1, D_MODEL), F32)
    vec_rows = [(V_DSHIFT, [dshift]), (V_DSCALE, [dscale]), (V_DGATE, [dgate]), (V_DCSHIFT, [dcshift]),
                (V_DCSCALE, [dcscale]), (V_ZERO, [zero_row]), (V_DNG, [dng]), (V_DNG_CTX, [dng_c])]
    vloc = _pack_vectors(vec_rows, dsg, dgq, dgk, dgk_c, loss_part)
    vg, sg = _allgather_direct([vloc, sloc], "gather_small")
    small_w =(b_ada, norm_g, sgu_norm_g, w_spatial, b_spatial, q_norm_g, k_norm_g, rpb)
    small_m = (m_b_ada, m_norm_g, m_sgu_norm_g, m_w_spatial, m_b_spatial, m_q_norm_g, m_k_norm_g, m_rpb)
    small_v = (v_b_ada, v_norm_g, v_sgu_norm_g, v_w_spatial, v_b_spatial, v_q_norm_g, v_k_norm_g, v_rpb)
    res_small, loss_row = _adamw_small(vg, sg, rg, small_w, small_m, small_v)

    dm_all = vg[:, V_DSHIFT:V_DGATE + 1, :].reshape(N_DEV, 3 * D_MODEL)
    dc_all = vg[:, V_DCSHIFT:V_ZERO + 1, :].reshape(N_DEV, 3 * D_MODEL)
    dm_sh = lax.dynamic_slice(dm_all, (0, me * ada_w), (N_DEV, ada_w))
    dc_sh = lax.dynamic_slice(dc_all, (0, me * ada_w), (N_DEV, ada_w))
    *res_ada, pc = _ada_bwd(s16, dm_sh, dc_sh, wada, m_w_ada[0], v_w_ada[0])
    (pc_g,) = _allgather_direct([pc], "gather_cctx")
    res_cctx = _adamw_cctx(pc_g, c_ctx_row, m_c_ctx.reshape(1, D_MODEL), v_c_ctx.reshape(1, D_MODEL))

    loss = loss_row[0, 0]
    outs = [loss, grad_x[None]]
    for kind in range(4):
        by_name = dict(zip(SMALL_NAMES, res_small[kind]))
        by_name.update(c_ctx=res_cctx[kind].reshape(D_MODEL), w_ada=res_ada[kind][None],
                       w_in=res_in[kind].T[None], w_out=res_out[kind][None])
        outs += [by_name[nme] for nme in ("c_ctx", "w_ada", "b_ada", "norm_g", "w_in", "sgu_norm_g", "w_spatial",
                                          "b_spatial", "q_norm_g", "k_norm_g", "rpb", "w_out")]
    return tuple(outs)
```

```python
import functools

import numpy as np
import jax
import jax.numpy as jnp
from jax import lax
from jax.experimental import pallas as pl
from jax.experimental.pallas import tpu as pltpu

F32 = jnp.float32
BF16 = jnp.bfloat16
HI = lax.Precision.HIGHEST

N_DEV = 8
D_MODEL = 1024
D_A = 512
D_B = 512
D_IN = 3584
N_BRANCH = 7
HEAD_DIM = 64
N_HEADS = 8
GRID_W = 64
WIN_R = 8
WIN_C = 16
CHUNK = 128
SGU_GROUPS = 4
EPS = 1e-6
NEG_INF = -1e30
Q_ROWS = 4
K_ROWS = 12
TQ = Q_ROWS * GRID_W
TK = K_ROWS * GRID_W
N_DIAG = 22
ATT_SUB = 8
ATT_SCALE = HEAD_DIM ** -0.5
LOG2E = 1.4426950408889634
LN2 = 0.6931471805599453

ADAM_LR = 0.001
ADAM_B1 = 0.9
ADAM_B2 = 0.999
ADAM_EPS = 1e-08
ADAM_WD = 0.01
ADAM_STEP = 10

VMEM_LIMIT = 56 * 1024 * 1024
MESH = pl.DeviceIdType.MESH

V_DSHIFT, V_DSCALE, V_DGATE, V_DCSHIFT, V_DCSCALE, V_ZERO, V_DNG, V_DNG_CTX, V_MISC = range(9)
M_DBS = 512
MOD_SHIFT, MOD_SCALE, MOD_GATE, MOD_CSHIFT, MOD_CSCALE = range(5)


def _params(sem=None):
    return pltpu.CompilerParams(dimension_semantics=sem, vmem_limit_bytes=VMEM_LIMIT)


def _sigmoid(x):
    return 1.0 / (1.0 + jnp.exp(-x))


def _gelu_parts(x):
    cdf = 0.5 * (1.0 + lax.erf(x * 0.7071067811865476))
    pdf = jnp.exp(-0.5 * x * x) * 0.3989422804014327
    return x * cdf, cdf + x * pdf


def _nt(a, b):
    return lax.dot_general(a, b, (((1,), (1,)), ((), ())), preferred_element_type=F32)


def _tn(a, b):
    return lax.dot_general(a, b, (((0,), (0,)), ((), ())), preferred_element_type=F32)


def _dot2(v, ones_bf):
    hi = v.astype(BF16)
    lo = (v - hi.astype(F32)).astype(BF16)
    return (jnp.dot(hi, ones_bf, preferred_element_type=F32)
            + jnp.dot(lo, ones_bf, preferred_element_type=F32))


def _head_sum(v, ones_ref):
    return jnp.dot(v.astype(BF16), ones_ref[...], preferred_element_type=F32)


def _adam(w, g, m, v):
    m2 = ADAM_B1 * m + (1.0 - ADAM_B1) * g
    v2 = ADAM_B2 * v + (1.0 - ADAM_B2) * (g * g)
    m_hat = m2 / (1.0 - ADAM_B1 ** ADAM_STEP)
    v_hat = v2 / (1.0 - ADAM_B2 ** ADAM_STEP)
    delta = -ADAM_LR * (m_hat / (jnp.sqrt(v_hat) + ADAM_EPS) + ADAM_WD * w)
    return delta, m2, v2


class _Hosted:
    def __init__(self, kind, src):
        self.kind = kind
        n_slot = {"a2a": N_DEV, "ag": N_DEV, "chips": N_DEV // 2}[kind]
        blk = src.shape if kind == "ag" else src.shape[1:]
        self.out_shape = jax.ShapeDtypeStruct((n_slot,) + tuple(blk), src.dtype)
        self.n_peer = n_slot - 1
        self.scratch = [pltpu.SemaphoreType.DMA((self.n_peer,)), pltpu.SemaphoreType.DMA((self.n_peer,)),
                        pltpu.SemaphoreType.DMA]

    def _copies(self, src, dst, send_sems, recv_sems, loc_sem, landing):
        x, y, c = lax.axis_index("x"), lax.axis_index("y"), lax.axis_index("c")
        if self.kind == "chips":
            me = 2 * x + y
            peers = [((px, py, c), 2 * px + py) for px, py in ((1 - x, y), (x, 1 - y), (1 - x, 1 - y))]
        else:
            me = 4 * x + 2 * y + c
            peers = []
            for k in range(1, N_DEV):
                px = 1 - x if (k >> 2) & 1 else x
                py = 1 - y if (k >> 1) & 1 else y
                pc = 1 - c if k & 1 else c
                peers.append(((px, py, pc), 4 * px + 2 * py + pc))
        remote = []
        for k, (peer, pid) in enumerate(peers):
            s = src if self.kind == "ag" else src.at[pid]
            remote.append(pltpu.make_async_remote_copy(
                src_ref=s, dst_ref=dst.at[pid if landing else me],
                send_sem=send_sems.at[k], recv_sem=recv_sems.at[k], device_id=peer, device_id_type=MESH))
        local = pltpu.make_async_copy(src if self.kind == "ag" else src.at[me], dst.at[me], loc_sem)
        return remote, local

    def start(self, src, dst, send_sems, recv_sems, loc_sem):
        remote, local = self._copies(src, dst, send_sems, recv_sems, loc_sem, landing=False)
        for cp in remote:
            cp.start()
        local.start()

    def wait(self, src, dst, send_sems, recv_sems, loc_sem):
        remote, local = self._copies(src, dst, send_sems, recv_sems, loc_sem, landing=True)
        for cp in remote:
            cp.wait_recv()
        for cp in remote:
            cp.wait_send()
        local.wait()


ANY_SPEC = pl.BlockSpec(memory_space=pl.ANY)


def _allgather_direct(arrs, name):
    n = len(arrs)
    exs = [_Hosted("ag", a) for a in arrs]

    def body(*refs):
        srcs, dsts, sems = refs[:n], refs[n:2 * n], refs[2 * n:]
        for r, ex in enumerate(exs):
            ex.start(srcs[r], dsts[r], *sems[3 * r:3 * r + 3])
        for r, ex in enumerate(exs):
            ex.wait(srcs[r], dsts[r], *sems[3 * r:3 * r + 3])

    return pl.pallas_call(body, name=name, out_shape=[ex.out_shape for ex in exs], in_specs=[ANY_SPEC] * n,
                          out_specs=[ANY_SPEC] * n, scratch_shapes=[s for ex in exs for s in ex.scratch])(*arrs)


def _ada_scratch(n_col):
    return ([pltpu.VMEM((N_DEV, 8, D_MODEL), F32), pltpu.VMEM((16, n_col), F32), pltpu.VMEM((N_DEV, 16, n_col), F32)]
            + [pltpu.SemaphoreType.DMA((N_DEV - 1,)) for _ in range(4)])


def _ada_modulation(cb_ref, cc_ref, w_ref, b_ref, cstack, part, parts, s1, r1, s2, r2):
    x, y, c = lax.axis_index("x"), lax.axis_index("y"), lax.axis_index("c")
    me = 4 * x + 2 * y + c
    peers = []
    for k in range(1, N_DEV):
        px = 1 - x if (k >> 2) & 1 else x
        py = 1 - y if (k >> 1) & 1 else y
        pc = 1 - c if k & 1 else c
        peers.append(((px, py, pc), 4 * px + 2 * py + pc))

    def exchange(src, dst, send_sems, recv_sems):
        for k, (peer, _) in enumerate(peers):
            pltpu.make_async_remote_copy(src_ref=src, dst_ref=dst.at[me], send_sem=send_sems.at[k],
                                         recv_sem=recv_sems.at[k], device_id=peer, device_id_type=MESH).start()
        dst[me] = src[...]
        waits = [pltpu.make_async_remote_copy(src_ref=src, dst_ref=dst.at[pid], send_sem=send_sems.at[k],
                                              recv_sem=recv_sems.at[k], device_id=peer, device_id_type=MESH)
                 for k, (peer, pid) in enumerate(peers)]
        for cp in waits:
            cp.wait_recv()
        for cp in waits:
            cp.wait_send()

    exchange(cb_ref, cstack, s1, r1)
    c_all = cstack[0]
    for d in range(1, N_DEV):
        c_all = c_all + cstack[d]
    row = lax.broadcasted_iota(jnp.int32, (8, D_MODEL), 0)
    cc = jnp.where(row == 0, jnp.broadcast_to(cc_ref[...], (8, D_MODEL)), 0.0)
    call = jnp.concatenate([c_all, cc], axis=0)
    s = call * _sigmoid(call)
    part[...] = jnp.dot(s, w_ref[...], preferred_element_type=F32, precision=HI) + b_ref[...]
    exchange(part, parts, s2, r2)
    return s


def _ada_bwd(s16, dm, dc, w, m, v):
    def body(s_ref, dm_ref, dc_ref, w_ref, m_ref, v_ref, g_ref, d_ref, m2_ref, v2_ref, pc_ref):
        dct = jnp.sum(dc_ref[...], axis=0, keepdims=True)
        row = lax.broadcasted_iota(jnp.int32, dc_ref.shape, 0)
        dcb = jnp.where(row == 0, jnp.broadcast_to(dct, dc_ref.shape), 0.0)
        dm16 = jnp.concatenate([dm_ref[...], dcb], axis=0)
        g = lax.dot_general(s_ref[...], dm16, (((0,), (0,)), ((), ())),
                            preferred_element_type=F32, precision=HI)
        w_ = w_ref[...]
        delta, m2, v2 = _adam(w_, g, m_ref[...], v_ref[...])
        g_ref[...] = g
        d_ref[...] = delta
        m2_ref[...] = m2
        v2_ref[...] = v2
        pc_ref[...] = lax.dot_general(dcb, w_, (((1,), (1,)), ((), ())),
                                      preferred_element_type=F32, precision=HI)

    R, n_col = w.shape
    tr = 256
    sh = jax.ShapeDtypeStruct(w.shape, F32)
    rows = pl.BlockSpec((tr, n_col), lambda i: (i, 0))
    small = pl.BlockSpec(dm.shape, lambda i: (0, 0))
    return pl.pallas_call(
        body, name="ada_bwd", grid=(R // tr,),
        in_specs=[pl.BlockSpec((16, tr), lambda i: (0, i)), small, small, rows, rows, rows],
        out_specs=[rows, rows, rows, rows, pl.BlockSpec((8, tr), lambda i: (0, i))],
        out_shape=[sh, sh, sh, sh, jax.ShapeDtypeStruct((8, D_MODEL), F32)],
        compiler_params=_params(("arbitrary",)),
    )(s16, dm, dc, w, m, v)


def _head_norm(zk, ones_ref, gain):
    ss = _head_sum(zk * zk, ones_ref)
    return zk * lax.rsqrt(ss * (1.0 / HEAD_DIM) + EPS) * gain


def _inproj_fwd(order, x, ng, c_blk, c_ctx_row, w_ada_sh, b_ada_sh, w_blk_t, tm):
    T = x.shape[0]
    nt = T // tm
    n_pass = N_DEV // 2
    blk_rows = w_blk_t.shape[0]
    n_col = w_ada_sh.shape[1]

    def body(order_ref, x_ref, g_ref, cb_ref, cc_ref, wa_ref, ba_ref, wb_ref,
             h_out, z_ref, wt_out, s_out, mod_out,
             hs, wt, modv, send_sems, recv_sems, loc_sem, h_sem, wt_sem, *ada_sc):
        p, i = pl.program_id(0), pl.program_id(1)
        x, y, c = lax.axis_index("x"), lax.axis_index("y"), lax.axis_index("c")
        me, sib = (x, y, c), (x, y, 1 - c)
        chips = [(1 - x, y), (x, 1 - y), (1 - x, 1 - y)]

        def slot(px, py, pc):
            return 4 * px + 2 * py + pc

        def copy(k, block, to, src=None):
            return pltpu.make_async_remote_copy(
                src_ref=wt.at[slot(*block)] if src is None else src, dst_ref=wt.at[slot(*block)],
                send_sem=send_sems.at[k], recv_sem=recv_sems.at[k], device_id=to, device_id_type=MESH)

        own = pltpu.make_async_copy(wb_ref, wt.at[slot(*me)], loc_sem)
        h_copy = pltpu.make_async_copy(hs, h_out, h_sem)
        wt_copy = pltpu.make_async_copy(wt, wt_out, wt_sem)
        first = [copy(1 + j, me, (*chip, c), src=wb_ref) for j, chip in enumerate(chips[:2])] + [copy(0, me, sib, src=wb_ref)]
        passed = [copy(4 + j, (*chip, c), sib) for j, chip in enumerate(chips)]
        relay_src = (jnp.where(c == 0, 1 - x, x), jnp.where(c == 0, y, 1 - y), c)
        relay_dst = (jnp.where(c == 0, x, 1 - x), jnp.where(c == 0, 1 - y, y), c)
        relay = copy(3, relay_src, relay_dst)

        @pl.when(jnp.logical_and(p == 0, i == 0))
        def _():
            s_out[...] = _ada_modulation(cb_ref, cc_ref, wa_ref, ba_ref, *ada_sc)
            own.start()
            for cp in first:
                cp.start()
            parts = ada_sc[2]
            my_row = pl.ds(slot(*me), 1)
            mod = jnp.concatenate([parts[d, my_row, :] for d in range(N_DEV)], axis=1)
            cmod = jnp.concatenate([parts[d, 8:9, :] for d in range(N_DEV)], axis=1)
            modv[...] = jnp.zeros_like(modv)
            for r, val in enumerate((mod[:, 0:D_MODEL], mod[:, D_MODEL:2 * D_MODEL], mod[:, 2 * D_MODEL:],
                                     cmod[:, 0:D_MODEL], cmod[:, D_MODEL:2 * D_MODEL])):
                modv[r:r + 1, :] = val
                mod_out[r] = val
            for r in range(5, 8):
                mod_out[r] = jnp.zeros((1, D_MODEL), F32)
            own.wait()
            copy(0, sib, me).wait_recv()

        @pl.when(jnp.logical_and(p == 1, i == 0))
        def _():
            for j, chip in enumerate(chips[:2]):
                copy(1 + j, (*chip, c), me).wait_recv()
            relay.start()
            passed[0].start()
            passed[1].start()
            copy(4, (*chips[0], 1 - c), me).wait_recv()

        @pl.when(jnp.logical_and(p == 2, i == 0))
        def _():
            copy(5, (*chips[1], 1 - c), me).wait_recv()

        @pl.when(jnp.logical_and(p == 3, i == 0))
        def _():
            copy(3, (*chips[2], c), me).wait_recv()
            passed[2].start()
            copy(6, (*chips[2], 1 - c), me).wait_recv()
            wt_copy.start()

        rows = pl.ds(pl.multiple_of(i * tm, tm), tm)

        @pl.when(p == 0)
        def _():
            xv = x_ref[...]
            r = lax.rsqrt(jnp.mean(xv * xv, axis=-1, keepdims=True) + EPS)
            hs[rows, :] = ((xv * r * g_ref[...]) * (1.0 + modv[1:2, :]) + modv[0:1, :]).astype(BF16)

        @pl.when(jnp.logical_and(p == 1, i == 0))
        def _():
            h_copy.start()

        w_pair = wt[pl.ds(2 * order_ref[p], 2)].reshape(2 * blk_rows, D_MODEL)
        z_ref[...] = _nt(hs[rows, :], w_pair).astype(BF16)

        @pl.when(jnp.logical_and(p == n_pass - 1, i == nt - 1))
        def _():
            for cp in first + passed + [relay]:
                cp.wait_send()
            h_copy.wait()
            wt_copy.wait()

    whole = lambda shape: pl.BlockSpec(shape, lambda p, i, o: (0,) * len(shape))
    grid_spec = pltpu.PrefetchScalarGridSpec(
        num_scalar_prefetch=1, grid=(n_pass, nt),
        in_specs=[pl.BlockSpec((tm, D_MODEL), lambda p, i, o: (jnp.where(p == 0, i, nt - 1), 0)),
                  whole((1, D_MODEL)), whole((8, D_MODEL)), whole((1, D_MODEL)), whole((D_MODEL, n_col)),
                  whole((1, n_col)), ANY_SPEC],
        out_specs=[ANY_SPEC, pl.BlockSpec((tm, 2 * blk_rows), lambda p, i, o: (i, o[p])), ANY_SPEC,
                   whole((16, D_MODEL)), whole((8, 1, D_MODEL))],
        scratch_shapes=[pltpu.VMEM((T, D_MODEL), BF16), pltpu.VMEM((N_DEV, blk_rows, D_MODEL), BF16),
                        pltpu.VMEM((8, D_MODEL), F32),
                        pltpu.SemaphoreType.DMA((7,)), pltpu.SemaphoreType.DMA((7,)), pltpu.SemaphoreType.DMA,
                        pltpu.SemaphoreType.DMA, pltpu.SemaphoreType.DMA] + _ada_scratch(n_col))
    return pl.pallas_call(
        body, name="inproj_fwd", grid_spec=grid_spec,
        out_shape=[jax.ShapeDtypeStruct((T, D_MODEL), BF16), jax.ShapeDtypeStruct((T, D_IN), BF16),
                   jax.ShapeDtypeStruct((N_DEV, blk_rows, D_MODEL), BF16),
                   jax.ShapeDtypeStruct((16, D_MODEL), F32), jax.ShapeDtypeStruct((8, 1, D_MODEL), F32)],
        compiler_params=_params(("arbitrary", "arbitrary")),
    )(order, x, ng, c_blk, c_ctx_row, w_ada_sh, b_ada_sh, w_blk_t)


def _mod_row(row):
    return pl.BlockSpec((1, 1, D_MODEL), lambda *idx: (row, 0, 0))


def _ctx_fwd(ctx, ng, mod, w_in_t, ones_blk, gk):
    C = ctx.shape[0]

    def body(x_ref, g_ref, sc_ref, sh_ref, w_ref, ones_ref, gk_ref, h_ref, z_ref, kn_ref):
        xv = x_ref[...]
        r = lax.rsqrt(jnp.mean(xv * xv, axis=-1, keepdims=True) + EPS)
        h = (xv * r * g_ref[...]) * (1.0 + sc_ref[0]) + sh_ref[0]
        hb = h.astype(BF16)
        h_ref[...] = hb
        zk = _nt(hb, w_ref[0:512, :])
        zv = _nt(hb, w_ref[512:1024, :])
        z_ref[:, 0:512] = zk.astype(BF16)
        z_ref[:, 512:1024] = zv.astype(BF16)
        kn_ref[...] = _head_norm(zk, ones_ref, gk_ref[...]).astype(BF16)

    vec = pl.BlockSpec((1, D_MODEL), lambda i: (0, 0))
    return pl.pallas_call(
        body, name="ctx_fwd", grid=(1,),
        in_specs=[pl.BlockSpec((C, D_MODEL), lambda i: (0, 0)), vec, _mod_row(MOD_CSCALE), _mod_row(MOD_CSHIFT),
                  pl.BlockSpec((1024, D_MODEL), lambda i: (2, 0)),
                  pl.BlockSpec((512, 512), lambda i: (0, 0)), pl.BlockSpec((1, 512), lambda i: (0, 0))],
        out_specs=[pl.BlockSpec((C, D_MODEL), lambda i: (0, 0)), pl.BlockSpec((C, 1024), lambda i: (0, 0)),
                   pl.BlockSpec((C, 512), lambda i: (0, 0))],
        out_shape=[jax.ShapeDtypeStruct((C, D_MODEL), BF16), jax.ShapeDtypeStruct((C, 1024), BF16),
                   jax.ShapeDtypeStruct((C, 512), BF16)],
        compiler_params=_params(("arbitrary",)),
    )(ctx, ng, mod, mod, w_in_t, ones_blk, gk)


def _sgu_chunk_fwd(au, av, ag, sg, ws_bf, bsb):
    gu, dgu = _gelu_parts(au)
    gv, dgv = _gelu_parts(av)
    rr = lax.rsqrt(jnp.mean(gv * gv, axis=-1, keepdims=True) + EPS)
    vhat = gv * rr
    vn = vhat * sg
    mixed = jnp.dot(ws_bf, vn.astype(BF16), preferred_element_type=F32) + bsb
    sig = _sigmoid(ag)
    sl = ag * sig
    return gu * mixed * sl, (gu, dgu, dgv, rr, vhat, vn, mixed, sig, sl)


def _sgu_qk_fwd(z, sgn, ws, bsb, ones_blk, gq, gk, tm):
    T = z.shape[0]

    def body(au_ref, av_ref, ag_ref, q_ref, k_ref, sg_ref, ws_ref, bsb_ref, ones_ref, gq_ref, gk_ref,
             o_ref, qs_ref, kn_ref):
        qs = _head_norm(q_ref[...].astype(F32), ones_ref, gq_ref[...]) * (ATT_SCALE * LOG2E)
        qs_ref[...] = qs.astype(BF16)
        kn_ref[...] = _head_norm(k_ref[...].astype(F32), ones_ref, gk_ref[...]).astype(BF16)
        for g in range(SGU_GROUPS):
            ws_bf = ws_ref[g].astype(BF16)
            sg = sg_ref[:, 128 * g:128 * (g + 1)]
            bsb_g = bsb_ref[g]
            for j in range(tm // CHUNK):
                rs, cs = slice(CHUNK * j, CHUNK * (j + 1)), slice(128 * g, 128 * (g + 1))
                out, _ = _sgu_chunk_fwd(au_ref[rs, cs].astype(F32), av_ref[rs, cs].astype(F32),
                                        ag_ref[rs, cs].astype(F32), sg, ws_bf, bsb_g)
                o_ref[rs, cs] = out.astype(BF16)

    zcol = lambda col: pl.BlockSpec((tm, 512), lambda i: (i, col))
    v512 = pl.BlockSpec((1, 512), lambda i: (0, 0))
    row = pl.BlockSpec((tm, 512), lambda i: (i, 0))
    out = jax.ShapeDtypeStruct((T, 512), BF16)
    return pl.pallas_call(
        body, name="sgu_qk_fwd", grid=(T // tm,),
        in_specs=[zcol(0), zcol(1), zcol(2), zcol(3), zcol(4), v512,
                  pl.BlockSpec((SGU_GROUPS, CHUNK, CHUNK), lambda i: (0, 0, 0)),
                  pl.BlockSpec((SGU_GROUPS, CHUNK, 128), lambda i: (0, 0, 0)),
                  pl.BlockSpec((512, 512), lambda i: (0, 0)), v512, v512],
        out_specs=[row, row, row], out_shape=[out, out, out],
        compiler_params=_params(("arbitrary",)),
    )(z, z, z, z, z, sgn, ws, bsb, ones_blk, gq, gk)


def _attn_type(rb, nrb):
    return jnp.where(rb == 0, 0, jnp.where(rb == nrb - 1, 2, 1))


def _attn_specs(T, C, att_sub):
    return [
        pl.BlockSpec((att_sub * TQ, 128), lambda hp, st: (st, hp)),
        pl.BlockSpec((T, 128), lambda hp, st: (0, hp)),
        pl.BlockSpec((T, 128), lambda hp, st: (0, 20 + hp)),
        pl.BlockSpec((C, 128), lambda hp, st: (0, hp)),
        pl.BlockSpec((C, 128), lambda hp, st: (0, 4 + hp)),
        pl.BlockSpec((2, 2 * WIN_R, GRID_W, 128), lambda hp, st: (hp, 0, 0, 0)),
        pl.BlockSpec((3, TQ, TK), lambda hp, st: (0, 0, 0)),
        pl.BlockSpec((att_sub * TQ, 128), lambda hp, st: (st, 24 + hp)),
    ]


def _build_bias(pairs_ref, mask_ref, bias_sc):
    for t in range(3):
        for hh in range(2):
            for i in range(Q_ROWS):
                for mm in range(K_ROWS // 2):
                    p = min(max(WIN_R - Q_ROWS * t + 2 * mm - i, 0), 2 * WIN_R - 1)
                    rs, cs = slice(GRID_W * i, GRID_W * (i + 1)), slice(128 * mm, 128 * (mm + 1))
                    bias_sc[t, hh, rs, cs] = (pairs_ref[hh, p] + mask_ref[t, rs, cs]) * LOG2E


def _attn_fwd(qs, kn, z, ckn, zc, pairs, row_mask, riders=()):
    T, C = qs.shape[0], ckn.shape[0]
    rows = T // GRID_W
    nrb = rows // Q_ROWS
    att_sub = min(ATT_SUB, nrb)
    n_st = nrb // att_sub
    n_rid = len(riders)

    def body(q_ref, k_ref, v_ref, ck_ref, cv_ref, pairs_ref, mask_ref, bg_ref, *rest):
        rid_src = rest[:n_rid]
        ob_ref, outb_ref, lse_ref = rest[n_rid:n_rid + 3]
        rid_dst = rest[n_rid + 3:2 * n_rid + 3]
        bias_sc = rest[2 * n_rid + 3]
        rid_sems = rest[2 * n_rid + 4:]

        @pl.when(jnp.logical_and(pl.program_id(0) == 0, pl.program_id(1) == 0))
        def _():
            for r, (ex, _) in enumerate(riders):
                ex.start(rid_src[r], rid_dst[r], *rid_sems[3 * r:3 * r + 3])

        @pl.when(pl.program_id(1) == 0)
        def _():
            _build_bias(pairs_ref, mask_ref, bias_sc)

        ck2, cv2 = ck_ref[...], cv_ref[...]
        lane = lax.broadcasted_iota(jnp.int32, (1, 128), 1)
        for sub in range(att_sub):
            rb = att_sub * pl.program_id(1) + sub
            bias_ref = bias_sc.at[_attn_type(rb, nrb)]
            rs = slice(TQ * sub, TQ * (sub + 1))
            ks = pl.multiple_of(jnp.clip(Q_ROWS * rb - 4, 0, rows - K_ROWS) * GRID_W, GRID_W)
            q2 = q_ref[rs, :]
            k2 = k_ref[pl.ds(ks, TK), :]
            v2 = v_ref[pl.ds(ks, TK), :]
            o_acc = jnp.zeros((TQ, 128), F32)
            lse_acc = jnp.zeros((TQ, 128), F32)
            for hh in range(2):
                msk = (lane >= HEAD_DIM) == bool(hh)
                qm = jnp.where(msk, q2, jnp.zeros_like(q2))
                s = _nt(qm, k2) + bias_ref[hh]
                sc = _nt(qm, ck2)
                m = jnp.maximum(jnp.max(s, axis=-1, keepdims=True), jnp.max(sc, axis=-1, keepdims=True))
                p = jnp.exp2(s - m)
                pc = jnp.exp2(sc - m)
                va = jnp.where(msk, v2, jnp.ones_like(v2))
                cva = jnp.where(msk, cv2, jnp.ones_like(cv2))
                num = (jnp.dot(p.astype(BF16), va, preferred_element_type=F32)
                       + jnp.dot(pc.astype(BF16), cva, preferred_element_type=F32))
                den = pltpu.roll(num, HEAD_DIM, 1)
                o_acc = jnp.where(msk, num / den, o_acc)
                lse_acc = jnp.where(msk, m + jnp.log(den) * LOG2E, lse_acc)
            ob_ref[rs, :] = o_acc.astype(BF16)
            lse_ref[rs, :] = lse_acc
            bg = bg_ref[rs, :].astype(F32)
            outb_ref[rs, :] = (o_acc * (bg * _sigmoid(bg))).astype(BF16)

        @pl.when(jnp.logical_and(pl.program_id(0) == pl.num_programs(0) - 1, pl.program_id(1) == n_st - 1))
        def _():
            for r, (ex, _) in enumerate(riders):
                ex.wait(rid_src[r], rid_dst[r], *rid_sems[3 * r:3 * r + 3])

    tile = pl.BlockSpec((att_sub * TQ, 128), lambda hp, st: (st, hp))
    res = pl.pallas_call(
        body, name="attn_fwd", grid=(4, n_st),
        in_specs=_attn_specs(T, C, att_sub) + [ANY_SPEC] * n_rid,
        out_specs=[tile, tile, tile] + [ANY_SPEC] * n_rid,
        out_shape=[jax.ShapeDtypeStruct((T, 512), BF16), jax.ShapeDtypeStruct((T, 512), BF16),
                   jax.ShapeDtypeStruct((T, 512), F32)] + [ex.out_shape for ex, _ in riders],
        scratch_shapes=[pltpu.VMEM((3, 2, TQ, TK), F32)] + [s for ex, _ in riders for s in ex.scratch],
        compiler_params=_params(("arbitrary", "arbitrary")),
    )(qs, kn, z, ckn, zc, pairs, row_mask, z, *[a for _, a in riders])
    return res[:3], res[3:]


def _outproj_loss_bwd(x, tgt, out_a, out_b, mod, w_out, tm):
    T = x.shape[0]
    nt = T // tm

    def body(x_ref, t_ref, oa_ref, ob_ref, gate_ref, w_ref, dy_ref, dmc_ref, dw_ref, dgate_ref, loss_ref, acc):
        @pl.when(pl.program_id(0) == 0)
        def _():
            acc[...] = jnp.zeros_like(acc)
            dgate_ref[...] = jnp.zeros_like(dgate_ref)
            loss_ref[...] = jnp.zeros_like(loss_ref)

        oa, ob = oa_ref[...], ob_ref[...]
        gate_v = gate_ref[0]
        mix = (jnp.dot(oa, w_ref[0:512, :], preferred_element_type=F32)
               + jnp.dot(ob, w_ref[512:1024, :], preferred_element_type=F32))
        e = x_ref[...] + gate_v * mix - t_ref[...]
        se = jnp.sum(jnp.sum(e * e, axis=0, keepdims=True), axis=1, keepdims=True)
        loss_ref[...] += jnp.broadcast_to(se * (0.5 / D_MODEL), loss_ref.shape)
        dy = e * (1.0 / D_MODEL)
        dy_ref[...] = dy
        dgate_ref[...] += jnp.sum(dy * mix, axis=0, keepdims=True)
        dmix = (dy * gate_v).astype(BF16)
        dmc_ref[...] = _nt(dmix, w_ref[...]).astype(BF16)
        acc[0:512, :] += _tn(oa, dmix)
        acc[512:1024, :] += _tn(ob, dmix)

        @pl.when(pl.program_id(0) == nt - 1)
        def _():
            dw_ref[...] = acc[...].astype(BF16)

    row = lambda w: pl.BlockSpec((tm, w), lambda i: (i, 0))
    return pl.pallas_call(
        body, name="outproj_loss_bwd", grid=(nt,),
        in_specs=[row(D_MODEL), row(D_MODEL), row(512), row(512), _mod_row(MOD_GATE),
                  pl.BlockSpec((D_MODEL, D_MODEL), lambda i: (0, 0))],
        out_specs=[row(D_MODEL), row(D_MODEL), pl.BlockSpec((D_MODEL, D_MODEL), lambda i: (0, 0)),
                   pl.BlockSpec((1, D_MODEL), lambda i: (0, 0)), pl.BlockSpec((1, 128), lambda i: (0, 0))],
        out_shape=[jax.ShapeDtypeStruct((T, D_MODEL), F32), jax.ShapeDtypeStruct((T, D_MODEL), BF16),
                   jax.ShapeDtypeStruct((D_MODEL, D_MODEL), BF16), jax.ShapeDtypeStruct((1, D_MODEL), F32),
                   jax.ShapeDtypeStruct((1, 128), F32)],
        scratch_shapes=[pltpu.VMEM((D_MODEL, D_MODEL), F32)],
        compiler_params=_params(("arbitrary",)),
    )(x, tgt, out_a, out_b, mod, w_out)


def _attn_bwd(qs, kn, z, ckn, zc, pairs, row_mask, ob, lse, dmc, hsel, fold, riders):
    T, C = qs.shape[0], ckn.shape[0]
    rows = T // GRID_W
    nrb = rows // Q_ROWS
    att_sub = min(ATT_SUB, nrb)
    n_st = nrb // att_sub
    n_rid = len(riders)

    def body(q_ref, k_ref, v_ref, ck_ref, cv_ref, pairs_ref, mask_ref, bg_ref, ob_ref, lse_ref, do_ref,
             hsel_ref, fold_ref, *rest):
        rid_src = rest[:n_rid]
        dq_ref, dk_ref, dv_ref, dck_ref, dcv_ref, dbg_ref, drpb_ref = rest[n_rid:n_rid + 7]
        rid_dst = rest[n_rid + 7:2 * n_rid + 7]
        bias_sc, dacc_ref = rest[2 * n_rid + 7:2 * n_rid + 9]
        rid_sems = rest[2 * n_rid + 9:]
        hp, st = pl.program_id(0), pl.program_id(1)

        @pl.when(jnp.logical_and(hp == 0, st == 0))
        def _():
            for r, (ex, _) in enumerate(riders):
                ex.start(rid_src[r], rid_dst[r], *rid_sems[3 * r:3 * r + 3])

        @pl.when(st == 0)
        def _():
            _build_bias(pairs_ref, mask_ref, bias_sc)
            dk_ref[...] = jnp.zeros_like(dk_ref)
            dv_ref[...] = jnp.zeros_like(dv_ref)
            dck_ref[...] = jnp.zeros_like(dck_ref)
            dcv_ref[...] = jnp.zeros_like(dcv_ref)
            dacc_ref[...] = jnp.zeros_like(dacc_ref)

        ck2, cv2 = ck_ref[...], cv_ref[...]
        lane = lax.broadcasted_iota(jnp.int32, (1, 128), 1)
        for sub in range(att_sub):
            rb = att_sub * st + sub
            bias_ref = bias_sc.at[_attn_type(rb, nrb)]
            rs = slice(TQ * sub, TQ * (sub + 1))
            kb = jnp.clip(Q_ROWS * rb - 4, 0, rows - K_ROWS)
            ks = pl.multiple_of(kb * GRID_W, GRID_W)
            ebase = kb - Q_ROWS * rb + 11
            q2 = q_ref[rs, :]
            k2 = k_ref[pl.ds(ks, TK), :]
            v2 = v_ref[pl.ds(ks, TK), :]
            bg = bg_ref[rs, :].astype(F32)
            sig = _sigmoid(bg)
            obv = ob_ref[rs, :].astype(F32)
            dout = do_ref[rs, :].astype(F32)
            dbg_ref[rs, :] = (dout * obv * (sig * (1.0 + bg * (1.0 - sig)))).astype(BF16)
            d_o = dout * (bg * sig)
            d_oo = d_o * obv
            lse2 = lse_ref[rs, :]
            dq_acc = jnp.zeros((TQ, 128), F32)
            for hh in range(2):
                msk = (lane >= HEAD_DIM) == bool(hh)
                qm = jnp.where(msk, q2, jnp.zeros_like(q2))
                lse_h = jnp.max(jnp.where(msk, lse2, -jnp.inf), axis=-1, keepdims=True)
                p = jnp.exp2(_nt(qm, k2) + bias_ref[hh] - lse_h)
                pc = jnp.exp2(_nt(qm, ck2) - lse_h)
                dom_f = jnp.where(msk, d_o, 0.0)
                dom = dom_f.astype(BF16)
                delta = jnp.sum(jnp.where(msk, d_oo, 0.0), axis=-1, keepdims=True)
                d_hi = delta.astype(BF16).astype(F32)
                x0 = HEAD_DIM * (1 - hh)
                dom_aug = jnp.where(lane == x0, -d_hi, jnp.where(lane == x0 + 1, d_hi - delta, dom_f)).astype(BF16)
                extra = jnp.logical_or(lane == x0, lane == x0 + 1)
                va = jnp.where(msk, v2, jnp.where(extra, jnp.ones_like(v2), jnp.zeros_like(v2)))
                cva = jnp.where(msk, cv2, jnp.where(extra, jnp.ones_like(cv2), jnp.zeros_like(cv2)))
                ds = p * _nt(dom_aug, va)
                dsc = pc * _nt(dom_aug, cva)
                dsb, dscb = ds.astype(BF16), dsc.astype(BF16)
                dq_h = (jnp.dot(dsb, k2, preferred_element_type=F32)
                        + jnp.dot(dscb, ck2, preferred_element_type=F32))
                dq_acc = jnp.where(msk, dq_h, dq_acc)
                dk_ref[pl.ds(ks, TK), :] += _tn(dsb, qm)
                dv_ref[pl.ds(ks, TK), :] += _tn(p.astype(BF16), dom)
                dck_ref[...] += _tn(dscb, qm)
                dcv_ref[...] += _tn(pc.astype(BF16), dom)
                for i in range(Q_ROWS):
                    for mm in range(K_ROWS // 2):
                        dacc_ref[hh, ebase + (2 * mm - i)] += ds[GRID_W * i:GRID_W * (i + 1),
                                                                 128 * mm:128 * (mm + 1)]
            dq_ref[rs, :] = dq_acc

        @pl.when(st == n_st - 1)
        def _():
            for hh in range(2):
                drpb_ref[hh] = _rpb_diag_sums(dacc_ref.at[hh], hsel_ref, fold_ref)

        @pl.when(jnp.logical_and(hp == pl.num_programs(0) - 1, st == n_st - 1))
        def _():
            for r, (ex, _) in enumerate(riders):
                ex.wait(rid_src[r], rid_dst[r], *rid_sems[3 * r:3 * r + 3])

    tile = pl.BlockSpec((att_sub * TQ, 128), lambda hp, st: (st, hp))
    colT = pl.BlockSpec((T, 128), lambda hp, st: (0, hp))
    colC = pl.BlockSpec((C, 128), lambda hp, st: (0, hp))
    res = pl.pallas_call(
        body, name="attn_bwd", grid=(4, n_st),
        in_specs=(_attn_specs(T, C, att_sub) + [tile, tile,
                                                pl.BlockSpec((att_sub * TQ, 128), lambda hp, st: (st, 4 + hp)),
                                       pl.BlockSpec((8, 128, 128), lambda hp, st: (0, 0, 0)),
                                       pl.BlockSpec((32, 256), lambda hp, st: (0, 0))]
                  + [ANY_SPEC] * n_rid),
        out_specs=([tile, colT, colT, colC, colC, tile, pl.BlockSpec((2, 32, 128), lambda hp, st: (hp, 0, 0))]
                   + [ANY_SPEC] * n_rid),
        out_shape=([jax.ShapeDtypeStruct((T, 512), F32), jax.ShapeDtypeStruct((T, 512), F32),
                    jax.ShapeDtypeStruct((T, 512), F32), jax.ShapeDtypeStruct((C, 512), F32),
                    jax.ShapeDtypeStruct((C, 512), F32), jax.ShapeDtypeStruct((T, 512), BF16),
                    jax.ShapeDtypeStruct((N_HEADS, 32, 128), F32)] + [ex.out_shape for ex, _ in riders]),
        scratch_shapes=([pltpu.VMEM((3, 2, TQ, TK), F32), pltpu.VMEM((2, N_DIAG, GRID_W, 128), F32)]
                        + [s for ex, _ in riders for s in ex.scratch]),
        compiler_params=_params(("arbitrary", "arbitrary")),
    )(qs, kn, z, ckn, zc, pairs, row_mask, z, ob, lse, dmc, hsel, fold, *[a for _, a in riders])
    return res[:7], res[7:]


def _rpb_diag_sums(a_ref, hsel_ref, fold_ref):
    n_off = 2 * WIN_C - 1
    n_dr = 2 * WIN_R - 1
    qc = lax.broadcasted_iota(jnp.int32, (GRID_W, 128), 0)
    lane = lax.broadcasted_iota(jnp.int32, (GRID_W, 128), 1)
    diff = lane % GRID_W - qc + (WIN_C - 1)
    left = lane < GRID_W

    def by_dr(dr):
        return a_ref[dr + 4] + pltpu.roll(a_ref[dr + 3], GRID_W, 1)

    out = jnp.zeros((32, 128), F32)
    for j in range((n_dr + 1) // 2):
        hi = pltpu.roll(by_dr(2 * j + 1), GRID_W, 1) if 2 * j + 1 < n_dr else 0.0
        pair = jnp.where(left, by_dr(2 * j), hi)
        parts = []
        for o in range(n_off):
            mv = jnp.where(diff == o, pair, 0.0)
            acc = mv[0:8]
            for r8 in range(1, GRID_W // 8):
                acc = acc + mv[8 * r8:8 * (r8 + 1)]
            parts.append(acc)
        parts.append(jnp.zeros((8, 128), F32))
        stack = jnp.concatenate(parts, axis=0)
        s_hi = stack.astype(BF16)
        s_lo = (stack - s_hi.astype(F32)).astype(BF16)
        per_o = (jnp.dot(fold_ref[...], s_hi, preferred_element_type=F32)
                 + jnp.dot(fold_ref[...], s_lo, preferred_element_type=F32))
        out = out + _dot2(per_o, hsel_ref[j])
    return out


def _head_norm_bwd(raw, dn, gain, ones_ref):
    rr = lax.rsqrt(_head_sum(raw * raw, ones_ref) * (1.0 / HEAD_DIM) + EPS)
    hat = raw * rr
    dgain = jnp.sum(dn * hat, axis=0, keepdims=True)
    dhat = dn * gain
    mean = _head_sum(dhat * hat, ones_ref) * (1.0 / HEAD_DIM)
    return rr * (dhat - hat * mean), dgain


def _ctx_k_bwd(zc, dck, dcv, ones_blk, gk, foldm):
    C = dck.shape[0]

    def body(bk_ref, dk_ref, dv_ref, ones_ref, gk_ref, fold_ref, dbk_ref, dbv_ref, dgk_ref):
        dbk, dgk = _head_norm_bwd(bk_ref[...].astype(F32), dk_ref[...] * LN2, gk_ref[...], ones_ref)
        dbk_ref[...] = dbk.astype(BF16)
        dbv_ref[...] = dv_ref[...].astype(BF16)
        dgk_ref[...] = jnp.dot(jnp.broadcast_to(dgk, (8, 512)), fold_ref[...],
                               preferred_element_type=F32, precision=HI)

    row = pl.BlockSpec((C, 512), lambda i: (0, 0))
    cst = lambda a, b: pl.BlockSpec((a, b), lambda i: (0, 0))
    out_row = jax.ShapeDtypeStruct((C, 512), BF16)
    return pl.pallas_call(
        body, name="ctx_k_bwd", grid=(1,),
        in_specs=[row, row, row, cst(512, 512), cst(1, 512), cst(512, 128)],
        out_specs=[row, row, cst(8, 128)],
        out_shape=[out_row, out_row, jax.ShapeDtypeStruct((8, 128), F32)],
        compiler_params=_params(("arbitrary",)),
    )(zc, dck, dcv, ones_blk, gk, foldm)


def _bwd_mid(z, dmc, dqs, dk, dv, db_g, h, hc, dzc_k, dzc_v, sgn, ws, wst, bsb, ones8, ones_blk, gq, gk, foldm, tk,
             riders=()):
    T = z.shape[0]
    nt = T // tk
    blk = D_IN // N_DEV
    n_in, n_out, n_sc = 23, 7, 9
    n_rid = len(riders)

    def body(*refs):
        (au_ref, av_ref, ag_ref, bq_ref, bk_ref, d_ref, dq_ref, dk_ref, dv_ref, dbg_ref, h_ref,
         hc_ref, dzck_ref, dzcv_ref, sg_ref, ws_ref, wst_ref, bsb_ref, ones8_ref, ones_ref, gq_ref, gk_ref,
         fold_ref) = refs[:n_in]
        rid_src = refs[n_in:n_in + n_rid]
        dz_ref, sums_out, dws_ref, dbs_ref, dsg_ref, dgq_ref, dgk_ref = refs[n_in + n_rid:n_in + n_rid + n_out]
        rid_dst = refs[n_in + n_rid + n_out:n_in + 2 * n_rid + n_out]
        (acc, accq, acck, stage, sem, send_buf, tmp, s1, r1) = refs[n_in + 2 * n_rid + n_out:
                                                                   n_in + 2 * n_rid + n_out + n_sc]
        rid_sems = refs[n_in + 2 * n_rid + n_out + n_sc:]
        t = pl.program_id(0)

        @pl.when(t == 0)
        def _():
            for r, (ex, _) in enumerate(riders):
                ex.start(rid_src[r], rid_dst[r], *rid_sems[3 * r:3 * r + 3])
            acc[...] = jnp.zeros_like(acc)
            acc[512 * 4:512 * 5, :] = _tn(dzck_ref[...], hc_ref[...])
            acc[512 * 5:512 * 6, :] = _tn(dzcv_ref[...], hc_ref[...])
            dws_ref[...] = jnp.zeros_like(dws_ref)
            dbs_ref[...] = jnp.zeros_like(dbs_ref)
            dsg_ref[...] = jnp.zeros_like(dsg_ref)
            accq[...] = jnp.zeros_like(accq)
            acck[...] = jnp.zeros_like(acck)

        for g in range(SGU_GROUPS):
            ws_bf = ws_ref[g].astype(BF16)
            wst_bf = wst_ref[g].astype(BF16)
            sg = sg_ref[:, 128 * g:128 * (g + 1)]
            bsb_g = bsb_ref[g]
            for j in range(tk // CHUNK):
                rs, cs = slice(CHUNK * j, CHUNK * (j + 1)), slice(128 * g, 128 * (g + 1))
                au, av, ag = (au_ref[rs, cs].astype(F32), av_ref[rs, cs].astype(F32), ag_ref[rs, cs].astype(F32))
                d = d_ref[rs, cs].astype(F32)
                _, (gu, dgu, dgv, rr, vhat, vn, mixed, sig, sl) = _sgu_chunk_fwd(au, av, ag, sg, ws_bf, bsb_g)
                dz_ref[rs, 128 * g:128 * (g + 1)] = (d * mixed * sl * dgu).astype(BF16)
                dz_ref[rs, 1024 + 128 * g:1024 + 128 * (g + 1)] = (
                    d * gu * mixed * (sig * (1.0 + ag * (1.0 - sig)))).astype(BF16)
                dmixed = d * gu * sl
                dmb = dmixed.astype(BF16)
                dm_lo = (dmixed - dmb.astype(F32)).astype(BF16)
                dbs_ref[g] += _nt(ones8_ref[...], dmb) + _nt(ones8_ref[...], dm_lo)
                dws_ref[g] += _nt(dmb, vn.astype(BF16))
                dvn = jnp.dot(wst_bf, dmb, preferred_element_type=F32)
                dsg_ref[:, 128 * g:128 * (g + 1)] += jnp.sum(dvn * vhat, axis=0, keepdims=True)
                dvhat = dvn * sg
                mean = jnp.mean(dvhat * vhat, axis=-1, keepdims=True)
                dz_ref[rs, 512 + 128 * g:512 + 128 * (g + 1)] = (rr * (dvhat - vhat * mean) * dgv).astype(BF16)

        dbq, dgq = _head_norm_bwd(bq_ref[...].astype(F32), dq_ref[...] * ATT_SCALE, gq_ref[...], ones_ref)
        dz_ref[:, 512 * 3:512 * 4] = dbq.astype(BF16)
        accq[...] += dgq
        dbk, dgk = _head_norm_bwd(bk_ref[...].astype(F32), dk_ref[...] * LN2, gk_ref[...], ones_ref)
        dz_ref[:, 512 * 4:512 * 5] = dbk.astype(BF16)
        acck[...] += dgk
        dz_ref[:, 512 * 5:512 * 6] = dv_ref[...].astype(BF16)
        dz_ref[:, 512 * 6:512 * 7] = dbg_ref[...]

        hv = h_ref[...]
        for k in range(N_BRANCH):
            acc[512 * k:512 * (k + 1), :] += _tn(dz_ref[:, 512 * k:512 * (k + 1)], hv)

        @pl.when(t == nt - 1)
        def _():
            dgq_ref[...] = jnp.dot(jnp.broadcast_to(accq[...], (8, 512)), fold_ref[...],
                                   preferred_element_type=F32, precision=HI)
            dgk_ref[...] = jnp.dot(jnp.broadcast_to(acck[...], (8, 512)), fold_ref[...],
                                   preferred_element_type=F32, precision=HI)
            cidx = lax.axis_index("c")
            sib = (lax.axis_index("x"), lax.axis_index("y"), 1 - cidx)
            swaps = []
            for q in range(N_DEV // 2):
                theirs = acc[pl.ds(pl.multiple_of(2 * blk * q + blk * (1 - cidx), 8), blk), :]
                send_buf[q] = theirs.astype(BF16)
                cp = pltpu.make_async_remote_copy(src_ref=send_buf.at[q], dst_ref=tmp.at[q], send_sem=s1.at[q],
                                                  recv_sem=r1.at[q], device_id=sib, device_id_type=MESH)
                cp.start()
                swaps.append(cp)
            for q in range(N_DEV // 2):
                swaps[q].wait_recv()
                mine = acc[pl.ds(pl.multiple_of(2 * blk * q + blk * cidx, 8), blk), :]
                stage[...] = (mine + tmp[q].astype(F32)).astype(BF16)
                out = pltpu.make_async_copy(stage, sums_out.at[q], sem)
                out.start()
                out.wait()
            for cp in swaps:
                cp.wait_send()
            for r, (ex, _) in enumerate(riders):
                ex.wait(rid_src[r], rid_dst[r], *rid_sems[3 * r:3 * r + 3])

    zcol = lambda col: pl.BlockSpec((tk, 512), lambda t: (t, col))
    row = pl.BlockSpec((tk, 512), lambda t: (t, 0))
    whole = lambda a: pl.BlockSpec(a.shape, lambda t: (0,) * a.ndim)
    res = pl.pallas_call(
        body, name="bwd_mid", grid=(nt,),
        in_specs=[zcol(0), zcol(1), zcol(2), zcol(3), zcol(4), row, row, row, row, row,
                  pl.BlockSpec((tk, D_MODEL), lambda t: (t, 0)), whole(hc), whole(dzc_k), whole(dzc_v),
                  whole(sgn), whole(ws), whole(wst), whole(bsb), whole(ones8), whole(ones_blk), whole(gq), whole(gk),
                  whole(foldm)] + [ANY_SPEC] * n_rid,
        out_specs=[pl.BlockSpec((tk, D_IN), lambda t: (t, 0)), ANY_SPEC,
                   pl.BlockSpec((SGU_GROUPS, CHUNK, CHUNK), lambda t: (0, 0, 0)),
                   pl.BlockSpec((SGU_GROUPS, 8, CHUNK), lambda t: (0, 0, 0)),
                   pl.BlockSpec((1, 512), lambda t: (0, 0)), pl.BlockSpec((8, 128), lambda t: (0, 0)),
                   pl.BlockSpec((8, 128), lambda t: (0, 0))] + [ANY_SPEC] * n_rid,
        out_shape=[jax.ShapeDtypeStruct((T, D_IN), BF16), jax.ShapeDtypeStruct((N_DEV // 2, blk, D_MODEL), BF16),
                   jax.ShapeDtypeStruct((SGU_GROUPS, CHUNK, CHUNK), F32),
                   jax.ShapeDtypeStruct((SGU_GROUPS, 8, CHUNK), F32), jax.ShapeDtypeStruct((1, 512), F32),
                   jax.ShapeDtypeStruct((8, 128), F32), jax.ShapeDtypeStruct((8, 128), F32)]
                  + [ex.out_shape for ex, _ in riders],
        scratch_shapes=[pltpu.VMEM((D_IN, D_MODEL), F32), pltpu.VMEM((1, 512), F32), pltpu.VMEM((1, 512), F32),
                        pltpu.VMEM((blk, D_MODEL), BF16), pltpu.SemaphoreType.DMA,
                        pltpu.VMEM((N_DEV // 2, blk, D_MODEL), BF16), pltpu.VMEM((N_DEV // 2, blk, D_MODEL), BF16),
                        pltpu.SemaphoreType.DMA((N_DEV // 2,)), pltpu.SemaphoreType.DMA((N_DEV // 2,))]
                       + [s for ex, _ in riders for s in ex.scratch],
        compiler_params=_params(("arbitrary",)),
    )(z, z, z, z, z, dmc, dqs, dk, dv, db_g, h, hc, dzc_k, dzc_v, sgn, ws, wst, bsb, ones8, ones_blk, gq, gk, foldm,
      *[a for _, a in riders])
    return res[:n_out], res[n_out:]


def _inproj_bwd_dx(dzs, w_row0, w_in_t, x, dy, ng, mod, scale_row, tm, name, riders=()):
    T = x.shape[0]
    n = len(dzs)
    wpc = dzs[0].shape[1]
    nt = T // tm
    with_dx = dy is not None
    n_own_in = n + 4 + with_dx
    n_own_out = 3 + with_dx
    n_rid = len(riders)

    def body(*refs):
        dz_refs = refs[:n]
        n_in = n_own_in + n_rid
        own = refs[n:n_own_in] + refs[n_in:n_in + n_own_out]
        rid_src = refs[n_own_in:n_own_in + n_rid]
        rid_dst = refs[n_in + n_own_out:n_in + n_own_out + n_rid]
        rid_sems = refs[n_in + n_own_out + n_rid:]
        if with_dx:
            w_ref, x_ref, dy_ref, g_ref, sc_ref, gx_ref, dsh_ref, dsc_ref, dg_ref = own
        else:
            w_ref, x_ref, g_ref, sc_ref, dsh_ref, dsc_ref, dg_ref = own

        @pl.when(pl.program_id(0) == 0)
        def _():
            for r, (ex, _) in enumerate(riders):
                ex.start(rid_src[r], rid_dst[r], *rid_sems[3 * r:3 * r + 3])
            dsh_ref[...] = jnp.zeros_like(dsh_ref)
            dsc_ref[...] = jnp.zeros_like(dsc_ref)
            dg_ref[...] = jnp.zeros_like(dg_ref)

        dh = jnp.dot(dz_refs[0][...], w_ref[0:wpc, :], preferred_element_type=F32)
        for k in range(1, n):
            dh = dh + jnp.dot(dz_refs[k][...], w_ref[wpc * k:wpc * (k + 1), :], preferred_element_type=F32)
        xv = x_ref[...]
        r = lax.rsqrt(jnp.mean(xv * xv, axis=-1, keepdims=True) + EPS)
        xn = xv * r
        gv, op = g_ref[...], 1.0 + sc_ref[0]
        dsh_ref[...] += jnp.sum(dh, axis=0, keepdims=True)
        dsc_ref[...] += jnp.sum(dh * xn * gv, axis=0, keepdims=True)
        dg_ref[...] += jnp.sum(dh * op * xn, axis=0, keepdims=True)
        if with_dx:
            dxn = dh * (gv * op)
            gx_ref[...] = r * (dxn - xn * jnp.mean(dxn * xn, axis=-1, keepdims=True)) + dy_ref[...]

        @pl.when(pl.program_id(0) == nt - 1)
        def _():
            for r, (ex, _) in enumerate(riders):
                ex.wait(rid_src[r], rid_dst[r], *rid_sems[3 * r:3 * r + 3])

    vec = pl.BlockSpec((1, D_MODEL), lambda i: (0, 0))
    rowf = pl.BlockSpec((tm, D_MODEL), lambda i: (i, 0))
    in_specs = [pl.BlockSpec((tm, wpc), lambda i: (i, 0))] * n
    in_specs += [pl.BlockSpec((wpc * n, D_MODEL), lambda i: (w_row0 // (wpc * n), 0)), rowf]
    args = list(dzs) + [w_in_t, x]
    vshape = jax.ShapeDtypeStruct((1, D_MODEL), F32)
    out_specs, out_shape = [vec, vec, vec], [vshape, vshape, vshape]
    if with_dx:
        in_specs.append(rowf)
        args.append(dy)
        out_specs = [rowf] + out_specs
        out_shape = [jax.ShapeDtypeStruct((T, D_MODEL), F32)] + out_shape
    in_specs += [vec, _mod_row(scale_row)] + [ANY_SPEC] * n_rid
    args += [ng, mod] + [a for _, a in riders]
    res = pl.pallas_call(
        body, name=name, grid=(nt,), in_specs=in_specs, out_specs=out_specs + [ANY_SPEC] * n_rid,
        out_shape=out_shape + [ex.out_shape for ex, _ in riders],
        scratch_shapes=[s for ex, _ in riders for s in ex.scratch],
        compiler_params=_params(("arbitrary",)),
    )(*args)
    return res[:n_own_out], res[n_own_out:]


def _adamw_sharded(w, gparts, m, v, tr, name, riders=()):
    R, C = w.shape
    n_part = gparts.shape[0]
    nt = R // tr
    n_rid = len(riders)

    def body(w_ref, gp_ref, m_ref, v_ref, *rest):
        rid_src = rest[:n_rid]
        g_ref, d_ref, m2_ref, v2_ref = rest[n_rid:n_rid + 4]
        rid_dst = rest[n_rid + 4:2 * n_rid + 4]
        rid_sems = rest[2 * n_rid + 4:]

        @pl.when(pl.program_id(0) == 0)
        def _():
            for r, (ex, _) in enumerate(riders):
                ex.start(rid_src[r], rid_dst[r], *rid_sems[3 * r:3 * r + 3])

        g = gp_ref[0].astype(F32)
        for d in range(1, n_part):
            g = g + gp_ref[d].astype(F32)
        delta, m2, v2 = _adam(w_ref[...], g, m_ref[...], v_ref[...])
        g_ref[...] = g
        d_ref[...] = delta
        m2_ref[...] = m2
        v2_ref[...] = v2

        @pl.when(pl.program_id(0) == nt - 1)
        def _():
            for r, (ex, _) in enumerate(riders):
                ex.wait(rid_src[r], rid_dst[r], *rid_sems[3 * r:3 * r + 3])

    row = pl.BlockSpec((tr, C), lambda i: (i, 0))
    sh = jax.ShapeDtypeStruct((R, C), F32)
    res = pl.pallas_call(
        body, name=name, grid=(nt,),
        in_specs=[row, pl.BlockSpec((n_part, tr, C), lambda i: (0, i, 0)), row, row] + [ANY_SPEC] * n_rid,
        out_specs=[row, row, row, row] + [ANY_SPEC] * n_rid,
        out_shape=[sh, sh, sh, sh] + [ex.out_shape for ex, _ in riders],
        scratch_shapes=[s for ex, _ in riders for s in ex.scratch],
        compiler_params=_params(("arbitrary",)),
    )(w, gparts, m, v, *[a for _, a in riders])
    return res[:4], res[4:]


def _pack_vectors(vec_rows, dsg, dgq, dgk, dgk_c, loss_part):
    flat = [a for _, arrs in vec_rows for a in arrs]

    def body(*refs):
        vecs = list(refs[:len(flat)])
        dsg_ref, dgq_ref, dgk_ref, dgkc_ref, loss_ref, v_ref = refs[len(flat):]
        row = lax.broadcasted_iota(jnp.int32, (16, D_MODEL), 0)
        misc = jnp.concatenate([dsg_ref[...], dgq_ref[0:1, :], dgk_ref[0:1, :], dgkc_ref[0:1, :],
                                loss_ref[...]], axis=1)
        v = jnp.where(row == V_MISC, jnp.broadcast_to(misc, (16, D_MODEL)), 0.0)
        for r, arrs in vec_rows:
            val = vecs.pop(0)[...]
            for _ in arrs[1:]:
                val = val + vecs.pop(0)[...]
            v = jnp.where(row == r, jnp.broadcast_to(val, (16, D_MODEL)), v)
        v_ref[...] = v

    return pl.pallas_call(
        body, name="pack_vectors", out_shape=jax.ShapeDtypeStruct((16, D_MODEL), F32), compiler_params=_params(),
    )(*flat, dsg, dgq, dgk, dgk_c, loss_part)


SMALL_NAMES = ("b_ada", "norm_g", "sgu_norm_g", "w_spatial", "b_spatial", "q_norm_g", "k_norm_g", "rpb")


def _adamw_small(vg, sg, rg, ws, ms, vs):
    k = len(SMALL_NAMES)

    def body(*refs):
        vg_ref, sg_ref, rg_ref = refs[0], refs[1], refs[2]
        refs = refs[1:]
        w_refs = dict(zip(SMALL_NAMES, refs[2:2 + k]))
        m_refs = dict(zip(SMALL_NAMES, refs[2 + k:2 + 2 * k]))
        v_refs = dict(zip(SMALL_NAMES, refs[2 + 2 * k:2 + 3 * k]))
        o_refs = [dict(zip(SMALL_NAMES, refs[2 + (3 + i) * k:2 + (4 + i) * k])) for i in range(4)]
        loss_ref = refs[2 + 7 * k]

        sv = vg_ref[0]
        for d in range(1, N_DEV):
            sv = sv + vg_ref[d]
        loss_ref[...] = sv[V_MISC:V_MISC + 1, 896:1024]

        def total(lo, hi, ref=sg_ref):
            s = ref[0, lo:hi, :].astype(F32)
            for d in range(1, N_DEV):
                s = s + ref[d, lo:hi, :].astype(F32)
            return s

        def emit(name, idx, g):
            res = _adam(w_refs[name][idx], g, m_refs[name][idx], v_refs[name][idx])
            for o, val in zip(o_refs, (g,) + res):
                o[name][idx] = val

        everything = (slice(None), slice(None))
        row = lambda r: sv[r:r + 1, :]
        emit("b_ada", everything, jnp.concatenate(
            [row(V_DSHIFT) + row(V_DCSHIFT), row(V_DSCALE) + row(V_DCSCALE), row(V_DGATE)], axis=1))
        emit("norm_g", everything, row(V_DNG) + row(V_DNG_CTX))
        misc = row(V_MISC)
        emit("sgu_norm_g", everything, misc[:, 0:512])
        emit("q_norm_g", everything, misc[:, 512:512 + HEAD_DIM])
        emit("k_norm_g", everything, misc[:, 640:640 + HEAD_DIM] + misc[:, 768:768 + HEAD_DIM])
        for g in range(SGU_GROUPS):
            emit("w_spatial", (0, g), total(128 * g, 128 * (g + 1)))
            emit("b_spatial", (0, slice(g, g + 1), slice(None)), total(M_DBS + 8 * g, M_DBS + 8 * (g + 1))[0:1, :])
        for hd in range(N_HEADS):
            by_dc = total(32 * hd, 32 * (hd + 1), rg_ref)
            emit("rpb", (0, hd), by_dc.T[0:2 * WIN_R - 1, 0:2 * WIN_C - 1])

    shapes = [jax.ShapeDtypeStruct(w.shape, F32) for w in ws]
    res = pl.pallas_call(body, name="adamw_small", out_shape=shapes * 4 + [jax.ShapeDtypeStruct((1, 128), F32)],
                         compiler_params=_params())(vg, sg, rg, *ws, *ms, *vs)
    return [res[i * k:(i + 1) * k] for i in range(4)], res[4 * k]


def _adamw_cctx(pc_g, w, m, v):
    def body(pc_ref, w_ref, m_ref, v_ref, g_ref, d_ref, m2_ref, v2_ref):
        pc = pc_ref[0, 0:1, :]
        for d in range(1, N_DEV):
            pc = pc + pc_ref[d, 0:1, :]
        cc = w_ref[...]
        sig = _sigmoid(cc)
        g = pc * (sig * (1.0 + cc * (1.0 - sig)))
        delta, m2, v2 = _adam(cc, g, m_ref[...], v_ref[...])
        g_ref[...] = g
        d_ref[...] = delta
        m2_ref[...] = m2
        v2_ref[...] = v2

    sh = jax.ShapeDtypeStruct((1, D_MODEL), F32)
    return pl.pallas_call(body, name="adamw_cctx", out_shape=[sh, sh, sh, sh], compiler_params=_params())(
        pc_g, w, m, v)


def _block_ones(n, blk):
    i = np.arange(n)
    return jnp.asarray((i[:, None] // blk == i[None, :] // blk).astype(np.float32), BF16)


def _rpb_pairs(rpb):
    n_off = 2 * WIN_C - 1
    cols = np.arange(GRID_W)
    c0 = np.clip(cols - WIN_C // 2, 0, GRID_W - WIN_C)
    in_win = (cols[None, :] >= c0[:, None]) & (cols[None, :] < c0[:, None] + WIN_C)
    dc = np.clip(cols[None, :] - cols[:, None] + (WIN_C - 1), 0, n_off - 1)
    expand = (dc[None] == np.arange(n_off)[:, None, None]) & in_win[None]
    toep = jnp.einsum("hrd,dqk->hrqk", rpb, jnp.asarray(expand, F32), precision=HI)
    toep = toep + jnp.asarray(np.where(in_win, 0.0, NEG_INF).astype(np.float32))
    neg = jnp.full((N_HEADS, 1, GRID_W, GRID_W), NEG_INF, F32)
    ext = jnp.concatenate([neg, toep, neg], axis=1)
    return jnp.concatenate([ext[:, :-1], ext[:, 1:]], axis=-1)


def _row_mask(rows):
    nrb = rows // Q_ROWS
    valid = np.zeros((3, Q_ROWS, 1, K_ROWS, 1), bool)
    for t, rb in enumerate((0, 1, nrb - 1)):
        kb = int(np.clip(Q_ROWS * rb - 4, 0, rows - K_ROWS))
        for i in range(Q_ROWS):
            r0 = int(np.clip(Q_ROWS * rb + i - WIN_R // 2, 0, rows - WIN_R))
            for j in range(K_ROWS):
                valid[t, i, 0, j, 0] = r0 <= kb + j < r0 + WIN_R
    full = np.broadcast_to(valid, (3, Q_ROWS, GRID_W, K_ROWS, GRID_W)).reshape(3, TQ, TK)
    return jnp.asarray(np.where(full, 0.0, NEG_INF).astype(np.float32))


def kernel(x, c, ctx, c_ctx, w_ada, b_ada, norm_g, w_in, sgu_norm_g, w_spatial, b_spatial, q_norm_g, k_norm_g, rpb, w_out, loss_target, m_c_ctx, m_w_ada, m_b_ada, m_norm_g, m_w_in, m_sgu_norm_g, m_w_spatial, m_b_spatial, m_q_norm_g, m_k_norm_g, m_rpb, m_w_out, v_c_ctx, v_w_ada, v_b_ada, v_norm_g, v_w_in, v_sgu_norm_g, v_w_spatial, v_b_spatial, v_q_norm_g, v_k_norm_g, v_rpb, v_w_out):
    me = 4 * lax.axis_index("x") + 2 * lax.axis_index("y") + lax.axis_index("c")
    x2, ctx2, tgt2 = x[0], ctx[0], loss_target[0]
    T, C = x2.shape[0], ctx2.shape[0]
    rows = T // GRID_W
    wada, win_t, wout = w_ada[0], w_in[0].T, w_out[0]
    ada_w = wada.shape[1]
    win_w = win_t.shape[0]

    row8 = lax.broadcasted_iota(jnp.int32, (8, D_MODEL), 0)
    c_blk = jnp.where(row8 == me, jnp.broadcast_to(c, (8, D_MODEL)), 0.0)
    b_sh = lax.dynamic_slice(b_ada, (0, me * ada_w), (1, ada_w))
    c_ctx_row = c_ctx.reshape(1, D_MODEL)

    ones512 = _block_ones(512, HEAD_DIM)
    ones8 = jnp.ones((8, 128), BF16)
    foldm = jnp.asarray((np.arange(512)[:, None] % HEAD_DIM == np.arange(128)[None, :]).astype(np.float32))
    lane_half = np.arange(128)[None, :, None] // GRID_W
    hsel = jnp.asarray((2 * np.arange(8)[:, None, None] + lane_half == np.arange(128)[None, None, :]).astype(np.float32),
                       BF16)
    foldr = jnp.asarray((np.arange(256)[None, :] // 8 == np.arange(32)[:, None]).astype(np.float32), BF16)
    gq512 = jnp.tile(q_norm_g, (1, N_HEADS))
    gk512 = jnp.tile(k_norm_g, (1, N_HEADS))
    ws = w_spatial[0]
    wst = ws.transpose(0, 2, 1)
    bsb = jnp.broadcast_to(b_spatial[0][:, :, None], (SGU_GROUPS, CHUNK, 128))
    pairs = _rpb_pairs(rpb[0])
    row_mask = _row_mask(rows)

    my_chip = me // 2
    order = jnp.stack([my_chip, my_chip ^ 2, my_chip ^ 1, my_chip ^ 3]).astype(jnp.int32)
    h, z, win_g, s16, mod = _inproj_fwd(order, x2, norm_g, c_blk, c_ctx_row, wada, b_sh, win_t.astype(BF16), 1024)
    w_in_b = win_g.reshape(D_IN, D_MODEL)
    out_a, qs, kn = _sgu_qk_fwd(z, sgu_norm_g, ws, bsb, ones512, gq512, gk512, 512)
    hc, zc, ckn = _ctx_fwd(ctx2, norm_g, mod, w_in_b, ones512, gk512)
    wout_blk = wout.astype(BF16)
    (ob, out_b, lse), (wout_g,) = _attn_fwd(qs, kn, z, ckn, zc, pairs, row_mask, [(_Hosted("ag", wout_blk), wout_blk)])
    w_out_b = wout_g.reshape(D_MODEL, D_MODEL)

    dy, dmc, dw_out, dgate, loss_part = _outproj_loss_bwd(x2, tgt2, out_a, out_b, mod, w_out_b, 1024)
    dw_out_blocks = dw_out.reshape(N_DEV, D_MODEL // N_DEV, D_MODEL)
    (dqs, dk, dv, dck, dcv, db_g, drpb), (gout_parts,) = _attn_bwd(
        qs, kn, z, ckn, zc, pairs, row_mask, ob, lse, dmc, hsel, foldr,
        [(_Hosted("a2a", dw_out_blocks), dw_out_blocks)])
    dzc_k, dzc_v, dgk_c = _ctx_k_bwd(zc, dck, dcv, ones512, gk512, foldm)
    rloc = drpb.reshape(N_HEADS * 32, 128)
    (dz, chip_sums, dws, dbs, dsg, dgq, dgk), (rg,) = _bwd_mid(
        z, dmc, dqs, dk, dv, db_g, h, hc, dzc_k, dzc_v, sgu_norm_g, ws, wst, bsb, ones8, ones512, gq512, gk512, foldm,
        512, [(_Hosted("ag", rloc), rloc)])
    sloc = jnp.concatenate([dws.reshape(SGU_GROUPS * CHUNK, CHUNK), dbs.reshape(SGU_GROUPS * 8, CHUNK)]).astype(BF16)
    (grad_x, dshift, dscale, dng), (gin_parts,) = _inproj_bwd_dx(
        [dz], 0, w_in_b, x2, dy, norm_g, mod, MOD_SCALE, 512, "inproj_bwd_dx",
        [(_Hosted("chips", chip_sums), chip_sums)])
    (dcshift, dcscale, dng_c), _ = _inproj_bwd_dx([dzc_k, dzc_v], 4 * 512, w_in_b, ctx2, None, norm_g, mod,
                                                  MOD_CSCALE, C, "ctx_bwd_dx")

    res_in, _ = _adamw_sharded(win_t, gin_parts, m_w_in[0].T, v_w_in[0].T, 224, "adamw_w_in")
    res_out, _ = _adamw_sharded(wout, gout_parts, m_w_out[0], v_w_out[0], 128, "adamw_w_out")

    zero_row = jnp.zeros((1, D_MODEL), F32)
    vec_rows = [(V_DSHIFT, [dshift]), (V_DSCALE, [dscale]), (V_DGATE, [dgate]), (V_DCSHIFT, [dcshift]),
                (V_DCSCALE, [dcscale]), (V_ZERO, [zero_row]), (V_DNG, [dng]), (V_DNG_CTX, [dng_c])]
    vloc = _pack_vectors(vec_rows, dsg, dgq, dgk, dgk_c, loss_part)
    vg, sg = _allgather_direct([vloc, sloc], "gather_small")
    small_w =(b_ada, norm_g, sgu_norm_g, w_spatial, b_spatial, q_norm_g, k_norm_g, rpb)
    small_m = (m_b_ada, m_norm_g, m_sgu_norm_g, m_w_spatial, m_b_spatial, m_q_norm_g, m_k_norm_g, m_rpb)
    small_v = (v_b_ada, v_norm_g, v_sgu_norm_g, v_w_spatial, v_b_spatial, v_q_norm_g, v_k_norm_g, v_rpb)
    res_small, loss_row = _adamw_small(vg, sg, rg, small_w, small_m, small_v)

    dm_all = vg[:, V_DSHIFT:V_DGATE + 1, :].reshape(N_DEV, 3 * D_MODEL)
    dc_all = vg[:, V_DCSHIFT:V_ZERO + 1, :].reshape(N_DEV, 3 * D_MODEL)
    dm_sh = lax.dynamic_slice(dm_all, (0, me * ada_w), (N_DEV, ada_w))
    dc_sh = lax.dynamic_slice(dc_all, (0, me * ada_w), (N_DEV, ada_w))
    *res_ada, pc = _ada_bwd(s16, dm_sh, dc_sh, wada, m_w_ada[0], v_w_ada[0])
    (pc_g,) = _allgather_direct([pc], "gather_cctx")
    res_cctx = _adamw_cctx(pc_g, c_ctx_row, m_c_ctx.reshape(1, D_MODEL), v_c_ctx.reshape(1, D_MODEL))

    loss = loss_row[0, 0]
    outs = [loss, grad_x[None]]
    for kind in range(4):
        by_name = dict(zip(SMALL_NAMES, res_small[kind]))
        by_name.update(c_ctx=res_cctx[kind].reshape(D_MODEL), w_ada=res_ada[kind][None],
                       w_in=res_in[kind].T[None], w_out=res_out[kind][None])
        outs += [by_name[nme] for nme in ("c_ctx", "w_ada", "b_ada", "norm_g", "w_in", "sgu_norm_g", "w_spatial",
                                          "b_spatial", "q_norm_g", "k_norm_g", "rpb", "w_out")]
    return tuple(outs)
```

```python
import functools

import numpy as np
import jax
import jax.numpy as jnp
from jax import lax
from jax.experimental import pallas as pl
from jax.experimental.pallas import tpu as pltpu

F32 = jnp.float32
BF16 = jnp.bfloat16
HI = lax.Precision.HIGHEST

N_DEV = 8
D_MODEL = 1024
D_A = 512
D_B = 512
D_IN = 3584
N_BRANCH = 7
HEAD_DIM = 64
N_HEADS = 8
GRID_W = 64
WIN_R = 8
WIN_C = 16
CHUNK = 128
SGU_GROUPS = 4
EPS = 1e-6
NEG_INF = -1e30
Q_ROWS = 4
K_ROWS = 12
TQ = Q_ROWS * GRID_W
TK = K_ROWS * GRID_W
N_DIAG = 22
ATT_SUB = 8
ATT_SCALE = HEAD_DIM ** -0.5
LOG2E = 1.4426950408889634
LN2 = 0.6931471805599453

ADAM_LR = 0.001
ADAM_B1 = 0.9
ADAM_B2 = 0.999
ADAM_EPS = 1e-08
ADAM_WD = 0.01
ADAM_STEP = 10

VMEM_LIMIT = 56 * 1024 * 1024
MESH = pl.DeviceIdType.MESH

V_DSHIFT, V_DSCALE, V_DGATE, V_DCSHIFT, V_DCSCALE, V_ZERO, V_DNG, V_DNG_CTX, V_MISC = range(9)
M_DBS = 512
MOD_SHIFT, MOD_SCALE, MOD_GATE, MOD_CSHIFT, MOD_CSCALE = range(5)


def _params(sem=None):
    return pltpu.CompilerParams(dimension_semantics=sem, vmem_limit_bytes=VMEM_LIMIT)


def _sigmoid(x):
    return 1.0 / (1.0 + jnp.exp(-x))


def _gelu_parts(x):
    cdf = 0.5 * (1.0 + lax.erf(x * 0.7071067811865476))
    pdf = jnp.exp(-0.5 * x * x) * 0.3989422804014327
    return x * cdf, cdf + x * pdf


def _nt(a, b):
    return lax.dot_general(a, b, (((1,), (1,)), ((), ())), preferred_element_type=F32)


def _tn(a, b):
    return lax.dot_general(a, b, (((0,), (0,)), ((), ())), preferred_element_type=F32)


def _dot2(v, ones_bf):
    hi = v.astype(BF16)
    lo = (v - hi.astype(F32)).astype(BF16)
    return (jnp.dot(hi, ones_bf, preferred_element_type=F32)
            + jnp.dot(lo, ones_bf, preferred_element_type=F32))


def _head_sum(v, ones_ref):
    return jnp.dot(v.astype(BF16), ones_ref[...], preferred_element_type=F32)


def _adam(w, g, m, v):
    m2 = ADAM_B1 * m + (1.0 - ADAM_B1) * g
    v2 = ADAM_B2 * v + (1.0 - ADAM_B2) * (g * g)
    m_hat = m2 / (1.0 - ADAM_B1 ** ADAM_STEP)
    v_hat = v2 / (1.0 - ADAM_B2 ** ADAM_STEP)
    delta = -ADAM_LR * (m_hat / (jnp.sqrt(v_hat) + ADAM_EPS) + ADAM_WD * w)
    return delta, m2, v2


class _Hosted:
    def __init__(self, kind, src):
        self.kind = kind
        n_slot = {"a2a": N_DEV, "ag": N_DEV, "chips": N_DEV // 2}[kind]
        blk = src.shape if kind == "ag" else src.shape[1:]
        self.out_shape = jax.ShapeDtypeStruct((n_slot,) + tuple(blk), src.dtype)
        self.n_peer = n_slot - 1
        self.scratch = [pltpu.SemaphoreType.DMA((self.n_peer,)), pltpu.SemaphoreType.DMA((self.n_peer,)),
                        pltpu.SemaphoreType.DMA]

    def _copies(self, src, dst, send_sems, recv_sems, loc_sem, landing):
        x, y, c = lax.axis_index("x"), lax.axis_index("y"), lax.axis_index("c")
        if self.kind == "chips":
            me = 2 * x + y
            peers = [((px, py, c), 2 * px + py) for px, py in ((1 - x, y), (x, 1 - y), (1 - x, 1 - y))]
        else:
            me = 4 * x + 2 * y + c
            peers = []
            for k in range(1, N_DEV):
                px = 1 - x if (k >> 2) & 1 else x
                py = 1 - y if (k >> 1) & 1 else y
                pc = 1 - c if k & 1 else c
                peers.append(((px, py, pc), 4 * px + 2 * py + pc))
        remote = []
        for k, (peer, pid) in enumerate(peers):
            s = src if self.kind == "ag" else src.at[pid]
            remote.append(pltpu.make_async_remote_copy(
                src_ref=s, dst_ref=dst.at[pid if landing else me],
                send_sem=send_sems.at[k], recv_sem=recv_sems.at[k], device_id=peer, device_id_type=MESH))
        local = pltpu.make_async_copy(src if self.kind == "ag" else src.at[me], dst.at[me], loc_sem)
        return remote, local

    def start(self, src, dst, send_sems, recv_sems, loc_sem):
        remote, local = self._copies(src, dst, send_sems, recv_sems, loc_sem, landing=False)
        for cp in remote:
            cp.start()
        local.start()

    def wait(self, src, dst, send_sems, recv_sems, loc_sem):
        remote, local = self._copies(src, dst, send_sems, recv_sems, loc_sem, landing=True)
        for cp in remote:
            cp.wait_recv()
        for cp in remote:
            cp.wait_send()
        local.wait()


ANY_SPEC = pl.BlockSpec(memory_space=pl.ANY)


def _allgather_direct(arrs, name):
    n = len(arrs)
    exs = [_Hosted("ag", a) for a in arrs]

    def body(*refs):
        srcs, dsts, sems = refs[:n], refs[n:2 * n], refs[2 * n:]
        for r, ex in enumerate(exs):
            ex.start(srcs[r], dsts[r], *sems[3 * r:3 * r + 3])
        for r, ex in enumerate(exs):
            ex.wait(srcs[r], dsts[r], *sems[3 * r:3 * r + 3])

    return pl.pallas_call(body, name=name, out_shape=[ex.out_shape for ex in exs], in_specs=[ANY_SPEC] * n,
                          out_specs=[ANY_SPEC] * n, scratch_shapes=[s for ex in exs for s in ex.scratch])(*arrs)


def _ada_scratch(n_col):
    return ([pltpu.VMEM((N_DEV, 8, D_MODEL), F32), pltpu.VMEM((16, n_col), F32), pltpu.VMEM((N_DEV, 16, n_col), F32)]
            + [pltpu.SemaphoreType.DMA((N_DEV - 1,)) for _ in range(4)])


def _ada_modulation(cb_ref, cc_ref, w_ref, b_ref, cstack, part, parts, s1, r1, s2, r2):
    x, y, c = lax.axis_index("x"), lax.axis_index("y"), lax.axis_index("c")
    me = 4 * x + 2 * y + c
    peers = []
    for k in range(1, N_DEV):
        px = 1 - x if (k >> 2) & 1 else x
        py = 1 - y if (k >> 1) & 1 else y
        pc = 1 - c if k & 1 else c
        peers.append(((px, py, pc), 4 * px + 2 * py + pc))

    def exchange(src, dst, send_sems, recv_sems):
        for k, (peer, _) in enumerate(peers):
            pltpu.make_async_remote_copy(src_ref=src, dst_ref=dst.at[me], send_sem=send_sems.at[k],
                                         recv_sem=recv_sems.at[k], device_id=peer, device_id_type=MESH).start()
        dst[me] = src[...]
        waits = [pltpu.make_async_remote_copy(src_ref=src, dst_ref=dst.at[pid], send_sem=send_sems.at[k],
                                              recv_sem=recv_sems.at[k], device_id=peer, device_id_type=MESH)
                 for k, (peer, pid) in enumerate(peers)]
        for cp in waits:
            cp.wait_recv()
        for cp in waits:
            cp.wait_send()

    exchange(cb_ref, cstack, s1, r1)
    c_all = cstack[0]
    for d in range(1, N_DEV):
        c_all = c_all + cstack[d]
    row = lax.broadcasted_iota(jnp.int32, (8, D_MODEL), 0)
    cc = jnp.where(row == 0, jnp.broadcast_to(cc_ref[...], (8, D_MODEL)), 0.0)
    call = jnp.concatenate([c_all, cc], axis=0)
    s = call * _sigmoid(call)
    part[...] = jnp.dot(s, w_ref[...], preferred_element_type=F32, precision=HI) + b_ref[...]
    exchange(part, parts, s2, r2)
    return s


def _ada_bwd(s16, dm, dc, w, m, v):
    def body(s_ref, dm_ref, dc_ref, w_ref, m_ref, v_ref, g_ref, d_ref, m2_ref, v2_ref, pc_ref):
        dct = jnp.sum(dc_ref[...], axis=0, keepdims=True)
        row = lax.broadcasted_iota(jnp.int32, dc_ref.shape, 0)
        dcb = jnp.where(row == 0, jnp.broadcast_to(dct, dc_ref.shape), 0.0)
        dm16 = jnp.concatenate([dm_ref[...], dcb], axis=0)
        g = lax.dot_general(s_ref[...], dm16, (((0,), (0,)), ((), ())),
                            preferred_element_type=F32, precision=HI)
        w_ = w_ref[...]
        delta, m2, v2 = _adam(w_, g, m_ref[...], v_ref[...])
        g_ref[...] = g
        d_ref[...] = delta
        m2_ref[...] = m2
        v2_ref[...] = v2
        pc_ref[...] = lax.dot_general(dcb, w_, (((1,), (1,)), ((), ())),
                                      preferred_element_type=F32, precision=HI)

    R, n_col = w.shape
    tr = 256
    sh = jax.ShapeDtypeStruct(w.shape, F32)
    rows = pl.BlockSpec((tr, n_col), lambda i: (i, 0))
    small = pl.BlockSpec(dm.shape, lambda i: (0, 0))
    return pl.pallas_call(
        body, name="ada_bwd", grid=(R // tr,),
        in_specs=[pl.BlockSpec((16, tr), lambda i: (0, i)), small, small, rows, rows, rows],
        out_specs=[rows, rows, rows, rows, pl.BlockSpec((8, tr), lambda i: (0, i))],
        out_shape=[sh, sh, sh, sh, jax.ShapeDtypeStruct((8, D_MODEL), F32)],
        compiler_params=_params(("arbitrary",)),
    )(s16, dm, dc, w, m, v)


def _head_norm(zk, ones_ref, gain):
    ss = _head_sum(zk * zk, ones_ref)
    return zk * lax.rsqrt(ss * (1.0 / HEAD_DIM) + EPS) * gain


def _inproj_fwd(order, x, ng, c_blk, c_ctx_row, w_ada_sh, b_ada_sh, w_blk_t, tm):
    T = x.shape[0]
    nt = T // tm
    n_pass = N_DEV // 2
    blk_rows = w_blk_t.shape[0]
    n_col = w_ada_sh.shape[1]

    def body(order_ref, x_ref, g_ref, cb_ref, cc_ref, wa_ref, ba_ref, wb_ref,
             h_out, z_ref, wt_out, s_out, mod_out,
             hs, wt, modv, send_sems, recv_sems, loc_sem, h_sem, wt_sem, *ada_sc):
        p, i = pl.program_id(0), pl.program_id(1)
        x, y, c = lax.axis_index("x"), lax.axis_index("y"), lax.axis_index("c")
        me, sib = (x, y, c), (x, y, 1 - c)
        chips = [(1 - x, y), (x, 1 - y), (1 - x, 1 - y)]

        def slot(px, py, pc):
            return 4 * px + 2 * py + pc

        def copy(k, block, to, src=None):
            return pltpu.make_async_remote_copy(
                src_ref=wt.at[slot(*block)] if src is None else src, dst_ref=wt.at[slot(*block)],
                send_sem=send_sems.at[k], recv_sem=recv_sems.at[k], device_id=to, device_id_type=MESH)

        own = pltpu.make_async_copy(wb_ref, wt.at[slot(*me)], loc_sem)
        h_copy = pltpu.make_async_copy(hs, h_out, h_sem)
        wt_copy = pltpu.make_async_copy(wt, wt_out, wt_sem)
        first = [copy(1 + j, me, (*chip, c), src=wb_ref) for j, chip in enumerate(chips[:2])] + [copy(0, me, sib, src=wb_ref)]
        passed = [copy(4 + j, (*chip, c), sib) for j, chip in enumerate(chips)]
        relay_src = (jnp.where(c == 0, 1 - x, x), jnp.where(c == 0, y, 1 - y), c)
        relay_dst = (jnp.where(c == 0, x, 1 - x), jnp.where(c == 0, 1 - y, y), c)
        relay = copy(3, relay_src, relay_dst)

        @pl.when(jnp.logical_and(p == 0, i == 0))
        def _():
            s_out[...] = _ada_modulation(cb_ref, cc_ref, wa_ref, ba_ref, *ada_sc)
            own.start()
            for cp in first:
                cp.start()
            parts = ada_sc[2]
            my_row = pl.ds(slot(*me), 1)
            mod = jnp.concatenate([parts[d, my_row, :] for d in range(N_DEV)], axis=1)
            cmod = jnp.concatenate([parts[d, 8:9, :] for d in range(N_DEV)], axis=1)
            modv[...] = jnp.zeros_like(modv)
            for r, val in enumerate((mod[:, 0:D_MODEL], mod[:, D_MODEL:2 * D_MODEL], mod[:, 2 * D_MODEL:],
                                     cmod[:, 0:D_MODEL], cmod[:, D_MODEL:2 * D_MODEL])):
                modv[r:r + 1, :] = val
                mod_out[r] = val
            for r in range(5, 8):
                mod_out[r] = jnp.zeros((1, D_MODEL), F32)
            own.wait()
            copy(0, sib, me).wait_recv()

        @pl.when(jnp.logical_and(p == 1, i == 0))
        def _():
            for j, chip in enumerate(chips[:2]):
                copy(1 + j, (*chip, c), me).wait_recv()
            relay.start()
            passed[0].start()
            passed[1].start()
            copy(4, (*chips[0], 1 - c), me).wait_recv()

        @pl.when(jnp.logical_and(p == 2, i == 0))
        def _():
            copy(5, (*chips[1], 1 - c), me).wait_recv()

        @pl.when(jnp.logical_and(p == 3, i == 0))
        def _():
            copy(3, (*chips[2], c), me).wait_recv()
            passed[2].start()
            copy(6, (*chips[2], 1 - c), me).wait_recv()
            wt_copy.start()

        rows = pl.ds(pl.multiple_of(i * tm, tm), tm)

        @pl.when(p == 0)
        def _():
            xv = x_ref[...]
            r = lax.rsqrt(jnp.mean(xv * xv, axis=-1, keepdims=True) + EPS)
            hs[rows, :] = ((xv * r * g_ref[...]) * (1.0 + modv[1:2, :]) + modv[0:1, :]).astype(BF16)

        @pl.when(jnp.logical_and(p == 1, i == 0))
        def _():
            h_copy.start()

        w_pair = wt[pl.ds(2 * order_ref[p], 2)].reshape(2 * blk_rows, D_MODEL)
        z_ref[...] = _nt(hs[rows, :], w_pair).astype(BF16)

        @pl.when(jnp.logical_and(p == n_pass - 1, i == nt - 1))
        def _():
            for cp in first + passed + [relay]:
                cp.wait_send()
            h_copy.wait()
            wt_copy.wait()

    whole = lambda shape: pl.BlockSpec(shape, lambda p, i, o: (0,) * len(shape))
    grid_spec = pltpu.PrefetchScalarGridSpec(
        num_scalar_prefetch=1, grid=(n_pass, nt),
        in_specs=[pl.BlockSpec((tm, D_MODEL), lambda p, i, o: (jnp.where(p == 0, i, nt - 1), 0)),
                  whole((1, D_MODEL)), whole((8, D_MODEL)), whole((1, D_MODEL)), whole((D_MODEL, n_col)),
                  whole((1, n_col)), ANY_SPEC],
        out_specs=[ANY_SPEC, pl.BlockSpec((tm, 2 * blk_rows), lambda p, i, o: (i, o[p])), ANY_SPEC,
                   whole((16, D_MODEL)), whole((8, 1, D_MODEL))],
        scratch_shapes=[pltpu.VMEM((T, D_MODEL), BF16), pltpu.VMEM((N_DEV, blk_rows, D_MODEL), BF16),
                        pltpu.VMEM((8, D_MODEL), F32),
                        pltpu.SemaphoreType.DMA((7,)), pltpu.SemaphoreType.DMA((7,)), pltpu.SemaphoreType.DMA,
                        pltpu.SemaphoreType.DMA, pltpu.SemaphoreType.DMA] + _ada_scratch(n_col))
    return pl.pallas_call(
        body, name="inproj_fwd", grid_spec=grid_spec,
        out_shape=[jax.ShapeDtypeStruct((T, D_MODEL), BF16), jax.ShapeDtypeStruct((T, D_IN), BF16),
                   jax.ShapeDtypeStruct((N_DEV, blk_rows, D_MODEL), BF16),
                   jax.ShapeDtypeStruct((16, D_MODEL), F32), jax.ShapeDtypeStruct((8, 1, D_MODEL), F32)],
        compiler_params=_params(("arbitrary", "arbitrary")),
    )(order, x, ng, c_blk, c_ctx_row, w_ada_sh, b_ada_sh, w_blk_t)


def _mod_row(row):
    return pl.BlockSpec((1, 1, D_MODEL), lambda *idx: (row, 0, 0))


def _ctx_fwd(ctx, ng, mod, w_in_t, ones_blk, gk):
    C = ctx.shape[0]

    def body(x_ref, g_ref, sc_ref, sh_ref, w_ref, ones_ref, gk_ref, h_ref, z_ref, kn_ref):
        xv = x_ref[...]
        r = lax.rsqrt(jnp.mean(xv * xv, axis=-1, keepdims=True) + EPS)
        h = (xv * r * g_ref[...]) * (1.0 + sc_ref[0]) + sh_ref[0]
        hb = h.astype(BF16)
        h_ref[...] = hb
        zk = _nt(hb, w_ref[0:512, :])
        zv = _nt(hb, w_ref[512:1024, :])
        z_ref[:, 0:512] = zk.astype(BF16)
        z_ref[:, 512:1024] = zv.astype(BF16)
        kn_ref[...] = _head_norm(zk, ones_ref, gk_ref[...]).astype(BF16)

    vec = pl.BlockSpec((1, D_MODEL), lambda i: (0, 0))
    return pl.pallas_call(
        body, name="ctx_fwd", grid=(1,),
        in_specs=[pl.BlockSpec((C, D_MODEL), lambda i: (0, 0)), vec, _mod_row(MOD_CSCALE), _mod_row(MOD_CSHIFT),
                  pl.BlockSpec((1024, D_MODEL), lambda i: (2, 0)),
                  pl.BlockSpec((512, 512), lambda i: (0, 0)), pl.BlockSpec((1, 512), lambda i: (0, 0))],
        out_specs=[pl.BlockSpec((C, D_MODEL), lambda i: (0, 0)), pl.BlockSpec((C, 1024), lambda i: (0, 0)),
                   pl.BlockSpec((C, 512), lambda i: (0, 0))],
        out_shape=[jax.ShapeDtypeStruct((C, D_MODEL), BF16), jax.ShapeDtypeStruct((C, 1024), BF16),
                   jax.ShapeDtypeStruct((C, 512), BF16)],
        compiler_params=_params(("arbitrary",)),
    )(ctx, ng, mod, mod, w_in_t, ones_blk, gk)


def _sgu_chunk_fwd(au, av, ag, sg, ws_bf, bsb):
    gu, dgu = _gelu_parts(au)
    gv, dgv = _gelu_parts(av)
    rr = lax.rsqrt(jnp.mean(gv * gv, axis=-1, keepdims=True) + EPS)
    vhat = gv * rr
    vn = vhat * sg
    mixed = jnp.dot(ws_bf, vn.astype(BF16), preferred_element_type=F32) + bsb
    sig = _sigmoid(ag)
    sl = ag * sig
    return gu * mixed * sl, (gu, dgu, dgv, rr, vhat, vn, mixed, sig, sl)


def _sgu_qk_fwd(z, sgn, ws, bsb, ones_blk, gq, gk, tm):
    T = z.shape[0]

    def body(au_ref, av_ref, ag_ref, q_ref, k_ref, sg_ref, ws_ref, bsb_ref, ones_ref, gq_ref, gk_ref,
             o_ref, qs_ref, kn_ref):
        qs = _head_norm(q_ref[...].astype(F32), ones_ref, gq_ref[...]) * (ATT_SCALE * LOG2E)
        qs_ref[...] = qs.astype(BF16)
        kn_ref[...] = _head_norm(k_ref[...].astype(F32), ones_ref, gk_ref[...]).astype(BF16)
        for g in range(SGU_GROUPS):
            ws_bf = ws_ref[g].astype(BF16)
            sg = sg_ref[:, 128 * g:128 * (g + 1)]
            bsb_g = bsb_ref[g]
            for j in range(tm // CHUNK):
                rs, cs = slice(CHUNK * j, CHUNK * (j + 1)), slice(128 * g, 128 * (g + 1))
                out, _ = _sgu_chunk_fwd(au_ref[rs, cs].astype(F32), av_ref[rs, cs].astype(F32),
                                        ag_ref[rs, cs].astype(F32), sg, ws_bf, bsb_g)
                o_ref[rs, cs] = out.astype(BF16)

    zcol = lambda col: pl.BlockSpec((tm, 512), lambda i: (i, col))
    v512 = pl.BlockSpec((1, 512), lambda i: (0, 0))
    row = pl.BlockSpec((tm, 512), lambda i: (i, 0))
    out = jax.ShapeDtypeStruct((T, 512), BF16)
    return pl.pallas_call(
        body, name="sgu_qk_fwd", grid=(T // tm,),
        in_specs=[zcol(0), zcol(1), zcol(2), zcol(3), zcol(4), v512,
                  pl.BlockSpec((SGU_GROUPS, CHUNK, CHUNK), lambda i: (0, 0, 0)),
                  pl.BlockSpec((SGU_GROUPS, CHUNK, 128), lambda i: (0, 0, 0)),
                  pl.BlockSpec((512, 512), lambda i: (0, 0)), v512, v512],
        out_specs=[row, row, row], out_shape=[out, out, out],
        compiler_params=_params(("arbitrary",)),
    )(z, z, z, z, z, sgn, ws, bsb, ones_blk, gq, gk)


def _attn_type(rb, nrb):
    return jnp.where(rb == 0, 0, jnp.where(rb == nrb - 1, 2, 1))


def _attn_specs(T, C, att_sub):
    return [
        pl.BlockSpec((att_sub * TQ, 128), lambda hp, st: (st, hp)),
        pl.BlockSpec((T, 128), lambda hp, st: (0, hp)),
        pl.BlockSpec((T, 128), lambda hp, st: (0, 20 + hp)),
        pl.BlockSpec((C, 128), lambda hp, st: (0, hp)),
        pl.BlockSpec((C, 128), lambda hp, st: (0, 4 + hp)),
        pl.BlockSpec((2, 2 * WIN_R, GRID_W, 128), lambda hp, st: (hp, 0, 0, 0)),
        pl.BlockSpec((3, TQ, TK), lambda hp, st: (0, 0, 0)),
        pl.BlockSpec((att_sub * TQ, 128), lambda hp, st: (st, 24 + hp)),
    ]


def _build_bias(pairs_ref, mask_ref, bias_sc):
    for t in range(3):
        for hh in range(2):
            for i in range(Q_ROWS):
                for mm in range(K_ROWS // 2):
                    p = min(max(WIN_R - Q_ROWS * t + 2 * mm - i, 0), 2 * WIN_R - 1)
                    rs, cs = slice(GRID_W * i, GRID_W * (i + 1)), slice(128 * mm, 128 * (mm + 1))
                    bias_sc[t, hh, rs, cs] = (pairs_ref[hh, p] + mask_ref[t, rs, cs]) * LOG2E


def _attn_fwd(qs, kn, z, ckn, zc, pairs, row_mask, riders=()):
    T, C = qs.shape[0], ckn.shape[0]
    rows = T // GRID_W
    nrb = rows // Q_ROWS
    att_sub = min(ATT_SUB, nrb)
    n_st = nrb // att_sub
    n_rid = len(riders)

    def body(q_ref, k_ref, v_ref, ck_ref, cv_ref, pairs_ref, mask_ref, bg_ref, *rest):
        rid_src = rest[:n_rid]
        ob_ref, outb_ref, lse_ref = rest[n_rid:n_rid + 3]
        rid_dst = rest[n_rid + 3:2 * n_rid + 3]
        bias_sc = rest[2 * n_rid + 3]
        rid_sems = rest[2 * n_rid + 4:]

        @pl.when(jnp.logical_and(pl.program_id(0) == 0, pl.program_id(1) == 0))
        def _():
            for r, (ex, _) in enumerate(riders):
                ex.start(rid_src[r], rid_dst[r], *rid_sems[3 * r:3 * r + 3])

        @pl.when(pl.program_id(1) == 0)
        def _():
            _build_bias(pairs_ref, mask_ref, bias_sc)

        ck2, cv2 = ck_ref[...], cv_ref[...]
        lane = lax.broadcasted_iota(jnp.int32, (1, 128), 1)
        for sub in range(att_sub):
            rb = att_sub * pl.program_id(1) + sub
            bias_ref = bias_sc.at[_attn_type(rb, nrb)]
            rs = slice(TQ * sub, TQ * (sub + 1))
            ks = pl.multiple_of(jnp.clip(Q_ROWS * rb - 4, 0, rows - K_ROWS) * GRID_W, GRID_W)
            q2 = q_ref[rs, :]
            k2 = k_ref[pl.ds(ks, TK), :]
            v2 = v_ref[pl.ds(ks, TK), :]
            o_acc = jnp.zeros((TQ, 128), F32)
            lse_acc = jnp.zeros((TQ, 128), F32)
            for hh in range(2):
                msk = (lane >= HEAD_DIM) == bool(hh)
                qm = jnp.where(msk, q2, jnp.zeros_like(q2))
                s = _nt(qm, k2) + bias_ref[hh]
                sc = _nt(qm, ck2)
                m = jnp.maximum(jnp.max(s, axis=-1, keepdims=True), jnp.max(sc, axis=-1, keepdims=True))
                p = jnp.exp2(s - m)
                pc = jnp.exp2(sc - m)
                va = jnp.where(msk, v2, jnp.ones_like(v2))
                cva = jnp.where(msk, cv2, jnp.ones_like(cv2))
                num = (jnp.dot(p.astype(BF16), va, preferred_element_type=F32)
                       + jnp.dot(pc.astype(BF16), cva, preferred_element_type=F32))
                den = pltpu.roll(num, HEAD_DIM, 1)
                o_acc = jnp.where(msk, num / den, o_acc)
                lse_acc = jnp.where(msk, m + jnp.log(den) * LOG2E, lse_acc)
            ob_ref[rs, :] = o_acc.astype(BF16)
            lse_ref[rs, :] = lse_acc
            bg = bg_ref[rs, :].astype(F32)
            outb_ref[rs, :] = (o_acc * (bg * _sigmoid(bg))).astype(BF16)

        @pl.when(jnp.logical_and(pl.program_id(0) == pl.num_programs(0) - 1, pl.program_id(1) == n_st - 1))
        def _():
            for r, (ex, _) in enumerate(riders):
                ex.wait(rid_src[r], rid_dst[r], *rid_sems[3 * r:3 * r + 3])

    tile = pl.BlockSpec((att_sub * TQ, 128), lambda hp, st: (st, hp))
    res = pl.pallas_call(
        body, name="attn_fwd", grid=(4, n_st),
        in_specs=_attn_specs(T, C, att_sub) + [ANY_SPEC] * n_rid,
        out_specs=[tile, tile, tile] + [ANY_SPEC] * n_rid,
        out_shape=[jax.ShapeDtypeStruct((T, 512), BF16), jax.ShapeDtypeStruct((T, 512), BF16),
                   jax.ShapeDtypeStruct((T, 512), F32)] + [ex.out_shape for ex, _ in riders],
        scratch_shapes=[pltpu.VMEM((3, 2, TQ, TK), F32)] + [s for ex, _ in riders for s in ex.scratch],
        compiler_params=_params(("arbitrary", "arbitrary")),
    )(qs, kn, z, ckn, zc, pairs, row_mask, z, *[a for _, a in riders])
    return res[:3], res[3:]


def _outproj_loss_bwd(x, tgt, out_a, out_b, mod, w_out, tm):
    T = x.shape[0]
    nt = T // tm

    def body(x_ref, t_ref, oa_ref, ob_ref, gate_ref, w_ref, dy_ref, dmc_ref, dw_ref, dgate_ref, loss_ref, acc):
        @pl.when(pl.program_id(0) == 0)
        def _():
            acc[...] = jnp.zeros_like(acc)
            dgate_ref[...] = jnp.zeros_like(dgate_ref)
            loss_ref[...] = jnp.zeros_like(loss_ref)

        oa, ob = oa_ref[...], ob_ref[...]
        gate_v = gate_ref[0]
        mix = (jnp.dot(oa, w_ref[0:512, :], preferred_element_type=F32)
               + jnp.dot(ob, w_ref[512:1024, :], preferred_element_type=F32))
        e = x_ref[...] + gate_v * mix - t_ref[...]
        se = jnp.sum(jnp.sum(e * e, axis=0, keepdims=True), axis=1, keepdims=True)
        loss_ref[...] += jnp.broadcast_to(se * (0.5 / D_MODEL), loss_ref.shape)
        dy = e * (1.0 / D_MODEL)
        dy_ref[...] = dy
        dgate_ref[...] += jnp.sum(dy * mix, axis=0, keepdims=True)
        dmix = (dy * gate_v).astype(BF16)
        dmc_ref[...] = _nt(dmix, w_ref[...]).astype(BF16)
        acc[0:512, :] += _tn(oa, dmix)
        acc[512:1024, :] += _tn(ob, dmix)

        @pl.when(pl.program_id(0) == nt - 1)
        def _():
            dw_ref[...] = acc[...].astype(BF16)

    row = lambda w: pl.BlockSpec((tm, w), lambda i: (i, 0))
    return pl.pallas_call(
        body, name="outproj_loss_bwd", grid=(nt,),
        in_specs=[row(D_MODEL), row(D_MODEL), row(512), row(512), _mod_row(MOD_GATE),
                  pl.BlockSpec((D_MODEL, D_MODEL), lambda i: (0, 0))],
        out_specs=[row(D_MODEL), row(D_MODEL), pl.BlockSpec((D_MODEL, D_MODEL), lambda i: (0, 0)),
                   pl.BlockSpec((1, D_MODEL), lambda i: (0, 0)), pl.BlockSpec((1, 128), lambda i: (0, 0))],
        out_shape=[jax.ShapeDtypeStruct((T, D_MODEL), F32), jax.ShapeDtypeStruct((T, D_MODEL), BF16),
                   jax.ShapeDtypeStruct((D_MODEL, D_MODEL), BF16), jax.ShapeDtypeStruct((1, D_MODEL), F32),
                   jax.ShapeDtypeStruct((1, 128), F32)],
        scratch_shapes=[pltpu.VMEM((D_MODEL, D_MODEL), F32)],
        compiler_params=_params(("arbitrary",)),
    )(x, tgt, out_a, out_b, mod, w_out)


def _attn_bwd(qs, kn, z, ckn, zc, pairs, row_mask, ob, lse, dmc, hsel, fold, riders):
    T, C = qs.shape[0], ckn.shape[0]
    rows = T // GRID_W
    nrb = rows // Q_ROWS
    att_sub = min(ATT_SUB, nrb)
    n_st = nrb // att_sub
    n_rid = len(riders)

    def body(q_ref, k_ref, v_ref, ck_ref, cv_ref, pairs_ref, mask_ref, bg_ref, ob_ref, lse_ref, do_ref,
             hsel_ref, fold_ref, *rest):
        rid_src = rest[:n_rid]
        dq_ref, dk_ref, dv_ref, dck_ref, dcv_ref, dbg_ref, drpb_ref = rest[n_rid:n_rid + 7]
        rid_dst = rest[n_rid + 7:2 * n_rid + 7]
        bias_sc, dacc_ref = rest[2 * n_rid + 7:2 * n_rid + 9]
        rid_sems = rest[2 * n_rid + 9:]
        hp, st = pl.program_id(0), pl.program_id(1)

        @pl.when(jnp.logical_and(hp == 0, st == 0))
        def _():
            for r, (ex, _) in enumerate(riders):
                ex.start(rid_src[r], rid_dst[r], *rid_sems[3 * r:3 * r + 3])

        @pl.when(st == 0)
        def _():
            _build_bias(pairs_ref, mask_ref, bias_sc)
            dk_ref[...] = jnp.zeros_like(dk_ref)
            dv_ref[...] = jnp.zeros_like(dv_ref)
            dck_ref[...] = jnp.zeros_like(dck_ref)
            dcv_ref[...] = jnp.zeros_like(dcv_ref)
            dacc_ref[...] = jnp.zeros_like(dacc_ref)

        ck2, cv2 = ck_ref[...], cv_ref[...]
        lane = lax.broadcasted_iota(jnp.int32, (1, 128), 1)
        for sub in range(att_sub):
            rb = att_sub * st + sub
            bias_ref = bias_sc.at[_attn_type(rb, nrb)]
            rs = slice(TQ * sub, TQ * (sub + 1))
            kb = jnp.clip(Q_ROWS * rb - 4, 0, rows - K_ROWS)
            ks = pl.multiple_of(kb * GRID_W, GRID_W)
            ebase = kb - Q_ROWS * rb + 11
            q2 = q_ref[rs, :]
            k2 = k_ref[pl.ds(ks, TK), :]
            v2 = v_ref[pl.ds(ks, TK), :]
            bg = bg_ref[rs, :].astype(F32)
            sig = _sigmoid(bg)
            obv = ob_ref[rs, :].astype(F32)
            dout = do_ref[rs, :].astype(F32)
            dbg_ref[rs, :] = (dout * obv * (sig * (1.0 + bg * (1.0 - sig)))).astype(BF16)
            d_o = dout * (bg * sig)
            d_oo = d_o * obv
            lse2 = lse_ref[rs, :]
            dq_acc = jnp.zeros((TQ, 128), F32)
            for hh in range(2):
                msk = (lane >= HEAD_DIM) == bool(hh)
                qm = jnp.where(msk, q2, jnp.zeros_like(q2))
                lse_h = jnp.max(jnp.where(msk, lse2, -jnp.inf), axis=-1, keepdims=True)
                p = jnp.exp2(_nt(qm, k2) + bias_ref[hh] - lse_h)
                pc = jnp.exp2(_nt(qm, ck2) - lse_h)
                dom_f = jnp.where(msk, d_o, 0.0)
                dom = dom_f.astype(BF16)
                delta = jnp.sum(jnp.where(msk, d_oo, 0.0), axis=-1, keepdims=True)
                d_hi = delta.astype(BF16).astype(F32)
                x0 = HEAD_DIM * (1 - hh)
                dom_aug = jnp.where(lane == x0, -d_hi, jnp.where(lane == x0 + 1, d_hi - delta, dom_f)).astype(BF16)
                extra = jnp.logical_or(lane == x0, lane == x0 + 1)
                va = jnp.where(msk, v2, jnp.where(extra, jnp.ones_like(v2), jnp.zeros_like(v2)))
                cva = jnp.where(msk, cv2, jnp.where(extra, jnp.ones_like(cv2), jnp.zeros_like(cv2)))
                ds = p * _nt(dom_aug, va)
                dsc = pc * _nt(dom_aug, cva)
                dsb, dscb = ds.astype(BF16), dsc.astype(BF16)
                dq_h = (jnp.dot(dsb, k2, preferred_element_type=F32)
                        + jnp.dot(dscb, ck2, preferred_element_type=F32))
                dq_acc = jnp.where(msk, dq_h, dq_acc)
                dk_ref[pl.ds(ks, TK), :] += _tn(dsb, qm)
                dv_ref[pl.ds(ks, TK), :] += _tn(p.astype(BF16), dom)
                dck_ref[...] += _tn(dscb, qm)
                dcv_ref[...] += _tn(pc.astype(BF16), dom)
                for i in range(Q_ROWS):
                    for mm in range(K_ROWS // 2):
                        dacc_ref[hh, ebase + (2 * mm - i)] += ds[GRID_W * i:GRID_W * (i + 1),
                                                                 128 * mm:128 * (mm + 1)]
            dq_ref[rs, :] = dq_acc

        @pl.when(st == n_st - 1)
        def _():
            for hh in range(2):
                drpb_ref[hh] = _rpb_diag_sums(dacc_ref.at[hh], hsel_ref, fold_ref)

        @pl.when(jnp.logical_and(hp == pl.num_programs(0) - 1, st == n_st - 1))
        def _():
            for r, (ex, _) in enumerate(riders):
                ex.wait(rid_src[r], rid_dst[r], *rid_sems[3 * r:3 * r + 3])

    tile = pl.BlockSpec((att_sub * TQ, 128), lambda hp, st: (st, hp))
    colT = pl.BlockSpec((T, 128), lambda hp, st: (0, hp))
    colC = pl.BlockSpec((C, 128), lambda hp, st: (0, hp))
    res = pl.pallas_call(
        body, name="attn_bwd", grid=(4, n_st),
        in_specs=(_attn_specs(T, C, att_sub) + [tile, tile,
                                                pl.BlockSpec((att_sub * TQ, 128), lambda hp, st: (st, 4 + hp)),
                                       pl.BlockSpec((8, 128, 128), lambda hp, st: (0, 0, 0)),
                                       pl.BlockSpec((32, 256), lambda hp, st: (0, 0))]
                  + [ANY_SPEC] * n_rid),
        out_specs=([tile, colT, colT, colC, colC, tile, pl.BlockSpec((2, 32, 128), lambda hp, st: (hp, 0, 0))]
                   + [ANY_SPEC] * n_rid),
        out_shape=([jax.ShapeDtypeStruct((T, 512), F32), jax.ShapeDtypeStruct((T, 512), F32),
                    jax.ShapeDtypeStruct((T, 512), F32), jax.ShapeDtypeStruct((C, 512), F32),
                    jax.ShapeDtypeStruct((C, 512), F32), jax.ShapeDtypeStruct((T, 512), BF16),
                    jax.ShapeDtypeStruct((N_HEADS, 32, 128), F32)] + [ex.out_shape for ex, _ in riders]),
        scratch_shapes=([pltpu.VMEM((3, 2, TQ, TK), F32), pltpu.VMEM((2, N_DIAG, GRID_W, 128), F32)]
                        + [s for ex, _ in riders for s in ex.scratch]),
        compiler_params=_params(("arbitrary", "arbitrary")),
    )(qs, kn, z, ckn, zc, pairs, row_mask, z, ob, lse, dmc, hsel, fold, *[a for _, a in riders])
    return res[:7], res[7:]


def _rpb_diag_sums(a_ref, hsel_ref, fold_ref):
    n_off = 2 * WIN_C - 1
    n_dr = 2 * WIN_R - 1
    qc = lax.broadcasted_iota(jnp.int32, (GRID_W, 128), 0)
    lane = lax.broadcasted_iota(jnp.int32, (GRID_W, 128), 1)
    diff = lane % GRID_W - qc + (WIN_C - 1)
    left = lane < GRID_W

    def by_dr(dr):
        return a_ref[dr + 4] + pltpu.roll(a_ref[dr + 3], GRID_W, 1)

    out = jnp.zeros((32, 128), F32)
    for j in range((n_dr + 1) // 2):
        hi = pltpu.roll(by_dr(2 * j + 1), GRID_W, 1) if 2 * j + 1 < n_dr else 0.0
        pair = jnp.where(left, by_dr(2 * j), hi)
        parts = []
        for o in range(n_off):
            mv = jnp.where(diff == o, pair, 0.0)
            acc = mv[0:8]
            for r8 in range(1, GRID_W // 8):
                acc = acc + mv[8 * r8:8 * (r8 + 1)]
            parts.append(acc)
        parts.append(jnp.zeros((8, 128), F32))
        stack = jnp.concatenate(parts, axis=0)
        s_hi = stack.astype(BF16)
        s_lo = (stack - s_hi.astype(F32)).astype(BF16)
        per_o = (jnp.dot(fold_ref[...], s_hi, preferred_element_type=F32)
                 + jnp.dot(fold_ref[...], s_lo, preferred_element_type=F32))
        out = out + _dot2(per_o, hsel_ref[j])
    return out


def _head_norm_bwd(raw, dn, gain, ones_ref):
    rr = lax.rsqrt(_head_sum(raw * raw, ones_ref) * (1.0 / HEAD_DIM) + EPS)
    hat = raw * rr
    dgain = jnp.sum(dn * hat, axis=0, keepdims=True)
    dhat = dn * gain
    mean = _head_sum(dhat * hat, ones_ref) * (1.0 / HEAD_DIM)
    return rr * (dhat - hat * mean), dgain


def _ctx_k_bwd(zc, dck, dcv, ones_blk, gk, foldm):
    C = dck.shape[0]

    def body(bk_ref, dk_ref, dv_ref, ones_ref, gk_ref, fold_ref, dbk_ref, dbv_ref, dgk_ref):
        dbk, dgk = _head_norm_bwd(bk_ref[...].astype(F32), dk_ref[...] * LN2, gk_ref[...], ones_ref)
        dbk_ref[...] = dbk.astype(BF16)
        dbv_ref[...] = dv_ref[...].astype(BF16)
        dgk_ref[...] = jnp.dot(jnp.broadcast_to(dgk, (8, 512)), fold_ref[...],
                               preferred_element_type=F32, precision=HI)

    row = pl.BlockSpec((C, 512), lambda i: (0, 0))
    cst = lambda a, b: pl.BlockSpec((a, b), lambda i: (0, 0))
    out_row = jax.ShapeDtypeStruct((C, 512), BF16)
    return pl.pallas_call(
        body, name="ctx_k_bwd", grid=(1,),
        in_specs=[row, row, row, cst(512, 512), cst(1, 512), cst(512, 128)],
        out_specs=[row, row, cst(8, 128)],
        out_shape=[out_row, out_row, jax.ShapeDtypeStruct((8, 128), F32)],
        compiler_params=_params(("arbitrary",)),
    )(zc, dck, dcv, ones_blk, gk, foldm)


def _bwd_mid(z, dmc, dqs, dk, dv, db_g, h, hc, dzc_k, dzc_v, sgn, ws, wst, bsb, ones8, ones_blk, gq, gk, foldm, tk,
             riders=()):
    T = z.shape[0]
    nt = T // tk
    blk = D_IN // N_DEV
    n_in, n_out, n_sc = 23, 7, 9
    n_rid = len(riders)

    def body(*refs):
        (au_ref, av_ref, ag_ref, bq_ref, bk_ref, d_ref, dq_ref, dk_ref, dv_ref, dbg_ref, h_ref,
         hc_ref, dzck_ref, dzcv_ref, sg_ref, ws_ref, wst_ref, bsb_ref, ones8_ref, ones_ref, gq_ref, gk_ref,
         fold_ref) = refs[:n_in]
        rid_src = refs[n_in:n_in + n_rid]
        dz_ref, sums_out, dws_ref, dbs_ref, dsg_ref, dgq_ref, dgk_ref = refs[n_in + n_rid:n_in + n_rid + n_out]
        rid_dst = refs[n_in + n_rid + n_out:n_in + 2 * n_rid + n_out]
        (acc, accq, acck, stage, sem, send_buf, tmp, s1, r1) = refs[n_in + 2 * n_rid + n_out:
                                                                   n_in + 2 * n_rid + n_out + n_sc]
        rid_sems = refs[n_in + 2 * n_rid + n_out + n_sc:]
        t = pl.program_id(0)

        @pl.when(t == 0)
        def _():
            for r, (ex, _) in enumerate(riders):
                ex.start(rid_src[r], rid_dst[r], *rid_sems[3 * r:3 * r + 3])
            acc[...] = jnp.zeros_like(acc)
            acc[512 * 4:512 * 5, :] = _tn(dzck_ref[...], hc_ref[...])
            acc[512 * 5:512 * 6, :] = _tn(dzcv_ref[...], hc_ref[...])
            dws_ref[...] = jnp.zeros_like(dws_ref)
            dbs_ref[...] = jnp.zeros_like(dbs_ref)
            dsg_ref[...] = jnp.zeros_like(dsg_ref)
            accq[...] = jnp.zeros_like(accq)
            acck[...] = jnp.zeros_like(acck)

        for g in range(SGU_GROUPS):
            ws_bf = ws_ref[g].astype(BF16)
            wst_bf = wst_ref[g].astype(BF16)
            sg = sg_ref[:, 128 * g:128 * (g + 1)]
            bsb_g = bsb_ref[g]
            for j in range(tk // CHUNK):
                rs, cs = slice(CHUNK * j, CHUNK * (j + 1)), slice(128 * g, 128 * (g + 1))
                au, av, ag = (au_ref[rs, cs].astype(F32), av_ref[rs, cs].astype(F32), ag_ref[rs, cs].astype(F32))
                d = d_ref[rs, cs].astype(F32)
                _, (gu, dgu, dgv, rr, vhat, vn, mixed, sig, sl) = _sgu_chunk_fwd(au, av, ag, sg, ws_bf, bsb_g)
                dz_ref[rs, 128 * g:128 * (g + 1)] = (d * mixed * sl * dgu).astype(BF16)
                dz_ref[rs, 1024 + 128 * g:1024 + 128 * (g + 1)] = (
                    d * gu * mixed * (sig * (1.0 + ag * (1.0 - sig)))).astype(BF16)
                dmixed = d * gu * sl
                dmb = dmixed.astype(BF16)
                dm_lo = (dmixed - dmb.astype(F32)).astype(BF16)
                dbs_ref[g] += _nt(ones8_ref[...], dmb) + _nt(ones8_ref[...], dm_lo)
                dws_ref[g] += _nt(dmb, vn.astype(BF16))
                dvn = jnp.dot(wst_bf, dmb, preferred_element_type=F32)
                dsg_ref[:, 128 * g:128 * (g + 1)] += jnp.sum(dvn * vhat, axis=0, keepdims=True)
                dvhat = dvn * sg
                mean = jnp.mean(dvhat * vhat, axis=-1, keepdims=True)
                dz_ref[rs, 512 + 128 * g:512 + 128 * (g + 1)] = (rr * (dvhat - vhat * mean) * dgv).astype(BF16)

        dbq, dgq = _head_norm_bwd(bq_ref[...].astype(F32), dq_ref[...] * ATT_SCALE, gq_ref[...], ones_ref)
        dz_ref[:, 512 * 3:512 * 4] = dbq.astype(BF16)
        accq[...] += dgq
        dbk, dgk = _head_norm_bwd(bk_ref[...].astype(F32), dk_ref[...] * LN2, gk_ref[...], ones_ref)
        dz_ref[:, 512 * 4:512 * 5] = dbk.astype(BF16)
        acck[...] += dgk
        dz_ref[:, 512 * 5:512 * 6] = dv_ref[...].astype(BF16)
        dz_ref[:, 512 * 6:512 * 7] = dbg_ref[...]

        hv = h_ref[...]
        for k in range(N_BRANCH):
            acc[512 * k:512 * (k + 1), :] += _tn(dz_ref[:, 512 * k:512 * (k + 1)], hv)

        @pl.when(t == nt - 1)
        def _():
            dgq_ref[...] = jnp.dot(jnp.broadcast_to(accq[...], (8, 512)), fold_ref[...],
                                   preferred_element_type=F32, precision=HI)
            dgk_ref[...] = jnp.dot(jnp.broadcast_to(acck[...], (8, 512)), fold_ref[...],
                                   preferred_element_type=F32, precision=HI)
            cidx = lax.axis_index("c")
            sib = (lax.axis_index("x"), lax.axis_index("y"), 1 - cidx)
            swaps = []
            for q in range(N_DEV // 2):
                theirs = acc[pl.ds(pl.multiple_of(2 * blk * q + blk * (1 - cidx), 8), blk), :]
                send_buf[q] = theirs.astype(BF16)
                cp = pltpu.make_async_remote_copy(src_ref=send_buf.at[q], dst_ref=tmp.at[q], send_sem=s1.at[q],
                                                  recv_sem=r1.at[q], device_id=sib, device_id_type=MESH)
                cp.start()
                swaps.append(cp)
            for q in range(N_DEV // 2):
                swaps[q].wait_recv()
                mine = acc[pl.ds(pl.multiple_of(2 * blk * q + blk * cidx, 8), blk), :]
                stage[...] = (mine + tmp[q].astype(F32)).astype(BF16)
                out = pltpu.make_async_copy(stage, sums_out.at[q], sem)
                out.start()
                out.wait()
            for cp in swaps:
                cp.wait_send()
            for r, (ex, _) in enumerate(riders):
                ex.wait(rid_src[r], rid_dst[r], *rid_sems[3 * r:3 * r + 3])

    zcol = lambda col: pl.BlockSpec((tk, 512), lambda t: (t, col))
    row = pl.BlockSpec((tk, 512), lambda t: (t, 0))
    whole = lambda a: pl.BlockSpec(a.shape, lambda t: (0,) * a.ndim)
    res = pl.pallas_call(
        body, name="bwd_mid", grid=(nt,),
        in_specs=[zcol(0), zcol(1), zcol(2), zcol(3), zcol(4), row, row, row, row, row,
                  pl.BlockSpec((tk, D_MODEL), lambda t: (t, 0)), whole(hc), whole(dzc_k), whole(dzc_v),
                  whole(sgn), whole(ws), whole(wst), whole(bsb), whole(ones8), whole(ones_blk), whole(gq), whole(gk),
                  whole(foldm)] + [ANY_SPEC] * n_rid,
        out_specs=[pl.BlockSpec((tk, D_IN), lambda t: (t, 0)), ANY_SPEC,
                   pl.BlockSpec((SGU_GROUPS, CHUNK, CHUNK), lambda t: (0, 0, 0)),
                   pl.BlockSpec((SGU_GROUPS, 8, CHUNK), lambda t: (0, 0, 0)),
                   pl.BlockSpec((1, 512), lambda t: (0, 0)), pl.BlockSpec((8, 128), lambda t: (0, 0)),
                   pl.BlockSpec((8, 128), lambda t: (0, 0))] + [ANY_SPEC] * n_rid,
        out_shape=[jax.ShapeDtypeStruct((T, D_IN), BF16), jax.ShapeDtypeStruct((N_DEV // 2, blk, D_MODEL), BF16),
                   jax.ShapeDtypeStruct((SGU_GROUPS, CHUNK, CHUNK), F32),
                   jax.ShapeDtypeStruct((SGU_GROUPS, 8, CHUNK), F32), jax.ShapeDtypeStruct((1, 512), F32),
                   jax.ShapeDtypeStruct((8, 128), F32), jax.ShapeDtypeStruct((8, 128), F32)]
                  + [ex.out_shape for ex, _ in riders],
        scratch_shapes=[pltpu.VMEM((D_IN, D_MODEL), F32), pltpu.VMEM((1, 512), F32), pltpu.VMEM((1, 512), F32),
                        pltpu.VMEM((blk, D_MODEL), BF16), pltpu.SemaphoreType.DMA,
                        pltpu.VMEM((N_DEV // 2, blk, D_MODEL), BF16), pltpu.VMEM((N_DEV // 2, blk, D_MODEL), BF16),
                        pltpu.SemaphoreType.DMA((N_DEV // 2,)), pltpu.SemaphoreType.DMA((N_DEV // 2,))]
                       + [s for ex, _ in riders for s in ex.scratch],
        compiler_params=_params(("arbitrary",)),
    )(z, z, z, z, z, dmc, dqs, dk, dv, db_g, h, hc, dzc_k, dzc_v, sgn, ws, wst, bsb, ones8, ones_blk, gq, gk, foldm,
      *[a for _, a in riders])
    return res[:n_out], res[n_out:]


def _inproj_bwd_dx(dzs, w_row0, w_in_t, x, dy, ng, mod, scale_row, tm, name, riders=()):
    T = x.shape[0]
    n = len(dzs)
    wpc = dzs[0].shape[1]
    nt = T // tm
    with_dx = dy is not None
    n_own_in = n + 4 + with_dx
    n_own_out = 3 + with_dx
    n_rid = len(riders)

    def body(*refs):
        dz_refs = refs[:n]
        n_in = n_own_in + n_rid
        own = refs[n:n_own_in] + refs[n_in:n_in + n_own_out]
        rid_src = refs[n_own_in:n_own_in + n_rid]
        rid_dst = refs[n_in + n_own_out:n_in + n_own_out + n_rid]
        rid_sems = refs[n_in + n_own_out + n_rid:]
        if with_dx:
            w_ref, x_ref, dy_ref, g_ref, sc_ref, gx_ref, dsh_ref, dsc_ref, dg_ref = own
        else:
            w_ref, x_ref, g_ref, sc_ref, dsh_ref, dsc_ref, dg_ref = own

        @pl.when(pl.program_id(0) == 0)
        def _():
            for r, (ex, _) in enumerate(riders):
                ex.start(rid_src[r], rid_dst[r], *rid_sems[3 * r:3 * r + 3])
            dsh_ref[...] = jnp.zeros_like(dsh_ref)
            dsc_ref[...] = jnp.zeros_like(dsc_ref)
            dg_ref[...] = jnp.zeros_like(dg_ref)

        dh = jnp.dot(dz_refs[0][...], w_ref[0:wpc, :], preferred_element_type=F32)
        for k in range(1, n):
            dh = dh + jnp.dot(dz_refs[k][...], w_ref[wpc * k:wpc * (k + 1), :], preferred_element_type=F32)
        xv = x_ref[...]
        r = lax.rsqrt(jnp.mean(xv * xv, axis=-1, keepdims=True) + EPS)
        xn = xv * r
        gv, op = g_ref[...], 1.0 + sc_ref[0]
        dsh_ref[...] += jnp.sum(dh, axis=0, keepdims=True)
        dsc_ref[...] += jnp.sum(dh * xn * gv, axis=0, keepdims=True)
        dg_ref[...] += jnp.sum(dh * op * xn, axis=0, keepdims=True)
        if with_dx:
            dxn = dh * (gv * op)
            gx_ref[...] = r * (dxn - xn * jnp.mean(dxn * xn, axis=-1, keepdims=True)) + dy_ref[...]

        @pl.when(pl.program_id(0) == nt - 1)
        def _():
            for r, (ex, _) in enumerate(riders):
                ex.wait(rid_src[r], rid_dst[r], *rid_sems[3 * r:3 * r + 3])

    vec = pl.BlockSpec((1, D_MODEL), lambda i: (0, 0))
    rowf = pl.BlockSpec((tm, D_MODEL), lambda i: (i, 0))
    in_specs = [pl.BlockSpec((tm, wpc), lambda i: (i, 0))] * n
    in_specs += [pl.BlockSpec((wpc * n, D_MODEL), lambda i: (w_row0 // (wpc * n), 0)), rowf]
    args = list(dzs) + [w_in_t, x]
    vshape = jax.ShapeDtypeStruct((1, D_MODEL), F32)
    out_specs, out_shape = [vec, vec, vec], [vshape, vshape, vshape]
    if with_dx:
        in_specs.append(rowf)
        args.append(dy)
        out_specs = [rowf] + out_specs
        out_shape = [jax.ShapeDtypeStruct((T, D_MODEL), F32)] + out_shape
    in_specs += [vec, _mod_row(scale_row)] + [ANY_SPEC] * n_rid
    args += [ng, mod] + [a for _, a in riders]
    res = pl.pallas_call(
        body, name=name, grid=(nt,), in_specs=in_specs, out_specs=out_specs + [ANY_SPEC] * n_rid,
        out_shape=out_shape + [ex.out_shape for ex, _ in riders],
        scratch_shapes=[s for ex, _ in riders for s in ex.scratch],
        compiler_params=_params(("arbitrary",)),
    )(*args)
    return res[:n_own_out], res[n_own_out:]


def _adamw_sharded(w, gparts, m, v, tr, name, riders=()):
    R, C = w.shape
    n_part = gparts.shape[0]
    nt = R // tr
    n_rid = len(riders)

    def body(w_ref, gp_ref, m_ref, v_ref, *rest):
        rid_src = rest[:n_rid]
        g_ref, d_ref, m2_ref, v2_ref = rest[n_rid:n_rid + 4]
        rid_dst = rest[n_rid + 4:2 * n_rid + 4]
        rid_sems = rest[2 * n_rid + 4:]

        @pl.when(pl.program_id(0) == 0)
        def _():
            for r, (ex, _) in enumerate(riders):
                ex.start(rid_src[r], rid_dst[r], *rid_sems[3 * r:3 * r + 3])

        g = gp_ref[0].astype(F32)
        for d in range(1, n_part):
            g = g + gp_ref[d].astype(F32)
        delta, m2, v2 = _adam(w_ref[...], g, m_ref[...], v_ref[...])
        g_ref[...] = g
        d_ref[...] = delta
        m2_ref[...] = m2
        v2_ref[...] = v2

        @pl.when(pl.program_id(0) == nt - 1)
        def _():
            for r, (ex, _) in enumerate(riders):
                ex.wait(rid_src[r], rid_dst[r], *rid_sems[3 * r:3 * r + 3])

    row = pl.BlockSpec((tr, C), lambda i: (i, 0))
    sh = jax.ShapeDtypeStruct((R, C), F32)
    res = pl.pallas_call(
        body, name=name, grid=(nt,),
        in_specs=[row, pl.BlockSpec((n_part, tr, C), lambda i: (0, i, 0)), row, row] + [ANY_SPEC] * n_rid,
        out_specs=[row, row, row, row] + [ANY_SPEC] * n_rid,
        out_shape=[sh, sh, sh, sh] + [ex.out_shape for ex, _ in riders],
        scratch_shapes=[s for ex, _ in riders for s in ex.scratch],
        compiler_params=_params(("arbitrary",)),
    )(w, gparts, m, v, *[a for _, a in riders])
    return res[:4], res[4:]


def _pack_vectors(vec_rows, dsg, dgq, dgk, dgk_c, loss_part):
    flat = [a for _, arrs in vec_rows for a in arrs]

    def body(*refs):
        vecs = list(refs[:len(flat)])
        dsg_ref, dgq_ref, dgk_ref, dgkc_ref, loss_ref, v_ref = refs[len(flat):]
        row = lax.broadcasted_iota(jnp.int32, (16, D_MODEL), 0)
        misc = jnp.concatenate([dsg_ref[...], dgq_ref[0:1, :], dgk_ref[0:1, :], dgkc_ref[0:1, :],
                                loss_ref[...]], axis=1)
        v = jnp.where(row == V_MISC, jnp.broadcast_to(misc, (16, D_MODEL)), 0.0)
        for r, arrs in vec_rows:
            val = vecs.pop(0)[...]
            for _ in arrs[1:]:
                val = val + vecs.pop(0)[...]
            v = jnp.where(row == r, jnp.broadcast_to(val, (16, D_MODEL)), v)
        v_ref[...] = v

    return pl.pallas_call(
        body, name="pack_vectors", out_shape=jax.ShapeDtypeStruct((16, D_MODEL), F32), compiler_params=_params(),
    )(*flat, dsg, dgq, dgk, dgk_c, loss_part)


SMALL_NAMES = ("b_ada", "norm_g", "sgu_norm_g", "w_spatial", "b_spatial", "q_norm_g", "k_norm_g", "rpb")


def _adamw_small(vg, sg, rg, ws, ms, vs):
    k = len(SMALL_NAMES)

    def body(*refs):
        vg_ref, sg_ref, rg_ref = refs[0], refs[1], refs[2]
        refs = refs[1:]
        w_refs = dict(zip(SMALL_NAMES, refs[2:2 + k]))
        m_refs = dict(zip(SMALL_NAMES, refs[2 + k:2 + 2 * k]))
        v_refs = dict(zip(SMALL_NAMES, refs[2 + 2 * k:2 + 3 * k]))
        o_refs = [dict(zip(SMALL_NAMES, refs[2 + (3 + i) * k:2 + (4 + i) * k])) for i in range(4)]
        loss_ref = refs[2 + 7 * k]

        sv = vg_ref[0]
        for d in range(1, N_DEV):
            sv = sv + vg_ref[d]
        loss_ref[...] = sv[V_MISC:V_MISC + 1, 896:1024]

        def total(lo, hi, ref=sg_ref):
            s = ref[0, lo:hi, :].astype(F32)
            for d in range(1, N_DEV):
                s = s + ref[d, lo:hi, :].astype(F32)
            return s

        def emit(name, idx, g):
            res = _adam(w_refs[name][idx], g, m_refs[name][idx], v_refs[name][idx])
            for o, val in zip(o_refs, (g,) + res):
                o[name][idx] = val

        everything = (slice(None), slice(None))
        row = lambda r: sv[r:r + 1, :]
        emit("b_ada", everything, jnp.concatenate(
            [row(V_DSHIFT) + row(V_DCSHIFT), row(V_DSCALE) + row(V_DCSCALE), row(V_DGATE)], axis=1))
        emit("norm_g", everything, row(V_DNG) + row(V_DNG_CTX))
        misc = row(V_MISC)
        emit("sgu_norm_g", everything, misc[:, 0:512])
        emit("q_norm_g", everything, misc[:, 512:512 + HEAD_DIM])
        emit("k_norm_g", everything, misc[:, 640:640 + HEAD_DIM] + misc[:, 768:768 + HEAD_DIM])
        for g in range(SGU_GROUPS):
            emit("w_spatial", (0, g), total(128 * g, 128 * (g + 1)))
            emit("b_spatial", (0, slice(g, g + 1), slice(None)), total(M_DBS + 8 * g, M_DBS + 8 * (g + 1))[0:1, :])
        for hd in range(N_HEADS):
            by_dc = total(32 * hd, 32 * (hd + 1), rg_ref)
            emit("rpb", (0, hd), by_dc.T[0:2 * WIN_R - 1, 0:2 * WIN_C - 1])

    shapes = [jax.ShapeDtypeStruct(w.shape, F32) for w in ws]
    res = pl.pallas_call(body, name="adamw_small", out_shape=shapes * 4 + [jax.ShapeDtypeStruct((1, 128), F32)],
                         compiler_params=_params())(vg, sg, rg, *ws, *ms, *vs)
    return [res[i * k:(i + 1) * k] for i in range(4)], res[4 * k]


def _adamw_cctx(pc_g, w, m, v):
    def body(pc_ref, w_ref, m_ref, v_ref, g_ref, d_ref, m2_ref, v2_ref):
        pc = pc_ref[0, 0:1, :]
        for d in range(1, N_DEV):
            pc = pc + pc_ref[d, 0:1, :]
        cc = w_ref[...]
        sig = _sigmoid(cc)
        g = pc * (sig * (1.0 + cc * (1.0 - sig)))
        delta, m2, v2 = _adam(cc, g, m_ref[...], v_ref[...])
        g_ref[...] = g
        d_ref[...] = delta
        m2_ref[...] = m2
        v2_ref[...] = v2

    sh = jax.ShapeDtypeStruct((1, D_MODEL), F32)
    return pl.pallas_call(body, name="adamw_cctx", out_shape=[sh, sh, sh, sh], compiler_params=_params())(
        pc_g, w, m, v)


def _block_ones(n, blk):
    i = np.arange(n)
    return jnp.asarray((i[:, None] // blk == i[None, :] // blk).astype(np.float32), BF16)


def _rpb_pairs(rpb):
    n_off = 2 * WIN_C - 1
    cols = np.arange(GRID_W)
    c0 = np.clip(cols - WIN_C // 2, 0, GRID_W - WIN_C)
    in_win = (cols[None, :] >= c0[:, None]) & (cols[None, :] < c0[:, None] + WIN_C)
    dc = np.clip(cols[None, :] - cols[:, None] + (WIN_C - 1), 0, n_off - 1)
    expand = (dc[None] == np.arange(n_off)[:, None, None]) & in_win[None]
    toep = jnp.einsum("hrd,dqk->hrqk", rpb, jnp.asarray(expand, F32), precision=HI)
    toep = toep + jnp.asarray(np.where(in_win, 0.0, NEG_INF).astype(np.float32))
    neg = jnp.full((N_HEADS, 1, GRID_W, GRID_W), NEG_INF, F32)
    ext = jnp.concatenate([neg, toep, neg], axis=1)
    return jnp.concatenate([ext[:, :-1], ext[:, 1:]], axis=-1)


def _row_mask(rows):
    nrb = rows // Q_ROWS
    valid = np.zeros((3, Q_ROWS, 1, K_ROWS, 1), bool)
    for t, rb in enumerate((0, 1, nrb - 1)):
        kb = int(np.clip(Q_ROWS * rb - 4, 0, rows - K_ROWS))
        for i in range(Q_ROWS):
            r0 = int(np.clip(Q_ROWS * rb + i - WIN_R // 2, 0, rows - WIN_R))
            for j in range(K_ROWS):
                valid[t, i, 0, j, 0] = r0 <= kb + j < r0 + WIN_R
    full = np.broadcast_to(valid, (3, Q_ROWS, GRID_W, K_ROWS, GRID_W)).reshape(3, TQ, TK)
    return jnp.asarray(np.where(full, 0.0, NEG_INF).astype(np.float32))


def kernel(x, c, ctx, c_ctx, w_ada, b_ada, norm_g, w_in, sgu_norm_g, w_spatial, b_spatial, q_norm_g, k_norm_g, rpb, w_out, loss_target, m_c_ctx, m_w_ada, m_b_ada, m_norm_g, m_w_in, m_sgu_norm_g, m_w_spatial, m_b_spatial, m_q_norm_g, m_k_norm_g, m_rpb, m_w_out, v_c_ctx, v_w_ada, v_b_ada, v_norm_g, v_w_in, v_sgu_norm_g, v_w_spatial, v_b_spatial, v_q_norm_g, v_k_norm_g, v_rpb, v_w_out):
    me = 4 * lax.axis_index("x") + 2 * lax.axis_index("y") + lax.axis_index("c")
    x2, ctx2, tgt2 = x[0], ctx[0], loss_target[0]
    T, C = x2.shape[0], ctx2.shape[0]
    rows = T // GRID_W
    wada, win_t, wout = w_ada[0], w_in[0].T, w_out[0]
    ada_w = wada.shape[1]
    win_w = win_t.shape[0]

    row8 = lax.broadcasted_iota(jnp.int32, (8, D_MODEL), 0)
    c_blk = jnp.where(row8 == me, jnp.broadcast_to(c, (8, D_MODEL)), 0.0)
    b_sh = lax.dynamic_slice(b_ada, (0, me * ada_w), (1, ada_w))
    c_ctx_row = c_ctx.reshape(1, D_MODEL)

    ones512 = _block_ones(512, HEAD_DIM)
    ones8 = jnp.ones((8, 128), BF16)
    foldm = jnp.asarray((np.arange(512)[:, None] % HEAD_DIM == np.arange(128)[None, :]).astype(np.float32))
    lane_half = np.arange(128)[None, :, None] // GRID_W
    hsel = jnp.asarray((2 * np.arange(8)[:, None, None] + lane_half == np.arange(128)[None, None, :]).astype(np.float32),
                       BF16)
    foldr = jnp.asarray((np.arange(256)[None, :] // 8 == np.arange(32)[:, None]).astype(np.float32), BF16)
    gq512 = jnp.tile(q_norm_g, (1, N_HEADS))
    gk512 = jnp.tile(k_norm_g, (1, N_HEADS))
    ws = w_spatial[0]
    wst = ws.transpose(0, 2, 1)
    bsb = jnp.broadcast_to(b_spatial[0][:, :, None], (SGU_GROUPS, CHUNK, 128))
    pairs = _rpb_pairs(rpb[0])
    row_mask = _row_mask(rows)

    my_chip = me // 2
    order = jnp.stack([my_chip, my_chip ^ 2, my_chip ^ 1, my_chip ^ 3]).astype(jnp.int32)
    h, z, win_g, s16, mod = _inproj_fwd(order, x2, norm_g, c_blk, c_ctx_row, wada, b_sh, win_t.astype(BF16), 1024)
    w_in_b = win_g.reshape(D_IN, D_MODEL)
    out_a, qs, kn = _sgu_qk_fwd(z, sgu_norm_g, ws, bsb, ones512, gq512, gk512, 512)
    hc, zc, ckn = _ctx_fwd(ctx2, norm_g, mod, w_in_b, ones512, gk512)
    wout_blk = wout.astype(BF16)
    (ob, out_b, lse), (wout_g,) = _attn_fwd(qs, kn, z, ckn, zc, pairs, row_mask, [(_Hosted("ag", wout_blk), wout_blk)])
    w_out_b = wout_g.reshape(D_MODEL, D_MODEL)

    dy, dmc, dw_out, dgate, loss_part = _outproj_loss_bwd(x2, tgt2, out_a, out_b, mod, w_out_b, 512)
    dw_out_blocks = dw_out.reshape(N_DEV, D_MODEL // N_DEV, D_MODEL)
    (dqs, dk, dv, dck, dcv, db_g, drpb), (gout_parts,) = _attn_bwd(
        qs, kn, z, ckn, zc, pairs, row_mask, ob, lse, dmc, hsel, foldr,
        [(_Hosted("a2a", dw_out_blocks), dw_out_blocks)])
    dzc_k, dzc_v, dgk_c = _ctx_k_bwd(zc, dck, dcv, ones512, gk512, foldm)
    rloc = drpb.reshape(N_HEADS * 32, 128)
    (dz, chip_sums, dws, dbs, dsg, dgq, dgk), (rg,) = _bwd_mid(
        z, dmc, dqs, dk, dv, db_g, h, hc, dzc_k, dzc_v, sgu_norm_g, ws, wst, bsb, ones8, ones512, gq512, gk512, foldm,
        512, [(_Hosted("ag", rloc), rloc)])
    sloc = jnp.concatenate([dws.reshape(SGU_GROUPS * CHUNK, CHUNK), dbs.reshape(SGU_GROUPS * 8, CHUNK)]).astype(BF16)
    (grad_x, dshift, dscale, dng), (gin_parts,) = _inproj_bwd_dx(
        [dz], 0, w_in_b, x2, dy, norm_g, mod, MOD_SCALE, 512, "inproj_bwd_dx",
        [(_Hosted("chips", chip_sums), chip_sums)])
    (dcshift, dcscale, dng_c), _ = _inproj_bwd_dx([dzc_k, dzc_v], 4 * 512, w_in_b, ctx2, None, norm_g, mod,
                                                  MOD_CSCALE, C, "ctx_bwd_dx")

    res_in, _ = _adamw_sharded(win_t, gin_parts, m_w_in[0].T, v_w_in[0].T, 224, "adamw_w_in")
    res_out, _ = _adamw_sharded(wout, gout_parts, m_w_out[0], v_w_out[0], 128, "adamw_w_out")

    zero_row = jnp.zeros((1, D_MODEL), F32)
    vec_rows = [(V_DSHIFT, [dshift]), (V_DSCALE, [dscale]), (V_DGATE, [dgate]), (V_DCSHIFT, [dcshift]),
                (V_DCSCALE, [dcscale]), (V_ZERO, [zero_row]), (V_DNG, [dng]), (V_DNG_CTX, [dng_c])]
    vloc = _pack_vectors(vec_rows, dsg, dgq, dgk, dgk_c, loss_part)
    vg, sg = _allgather_direct([vloc, sloc], "gather_small")
    small_w =(b_ada, norm_g, sgu_norm_g, w_spatial, b_spatial, q_norm_g, k_norm_g, rpb)
    small_m = (m_b_ada, m_norm_g, m_sgu_norm_g, m_w_spatial, m_b_spatial, m_q_norm_g, m_k_norm_g, m_rpb)
    small_v = (v_b_ada, v_norm_g, v_sgu_norm_g, v_w_spatial, v_b_spatial, v_q_norm_g, v_k_norm_g, v_rpb)
    res_small, loss_row = _adamw_small(vg, sg, rg, small_w, small_m, small_v)

    dm_all = vg[:, V_DSHIFT:V_DGATE + 1, :].reshape(N_DEV, 3 * D_MODEL)
    dc_all = vg[:, V_DCSHIFT:V_ZERO + 1, :].reshape(N_DEV, 3 * D_MODEL)
    dm_sh = lax.dynamic_slice(dm_all, (0, me * ada_w), (N_DEV, ada_w))
    dc_sh = lax.dynamic_slice(dc_all, (0, me * ada_w), (N_DEV, ada_w))
    *res_ada, pc = _ada_bwd(s16, dm_sh, dc_sh, wada, m_w_ada[0], v_w_ada[0])
    (pc_g,) = _allgather_direct([pc], "gather_cctx")
    res_cctx = _adamw_cctx(pc_g, c_ctx_row, m_c_ctx.reshape(1, D_MODEL), v_c_ctx.reshape(1, D_MODEL))

    loss = loss_row[0, 0]
    outs = [loss, grad_x[None]]
    for kind in range(4):
        by_name = dict(zip(SMALL_NAMES, res_small[kind]))
        by_name.update(c_ctx=res_cctx[kind].reshape(D_MODEL), w_ada=res_ada[kind][None],
                       w_in=res_in[kind].T[None], w_out=res_out[kind][None])
        outs += [by_name[nme] for nme in ("c_ctx", "w_ada", "b_ada", "norm_g", "w_in", "sgu_norm_g", "w_spatial",
                                          "b_spatial", "q_norm_g", "k_norm_g", "rpb", "w_out")]
    return tuple(outs)
```

```python
import functools

import numpy as np
import jax
import jax.numpy as jnp
from jax import lax
from jax.experimental import pallas as pl
from jax.experimental.pallas import tpu as pltpu

F32 = jnp.float32
BF16 = jnp.bfloat16
HI = lax.Precision.HIGHEST

N_DEV = 8
D_MODEL = 1024
D_A = 512
D_B = 512
D_IN = 3584
N_BRANCH = 7
HEAD_DIM = 64
N_HEADS = 8
GRID_W = 64
WIN_R = 8
WIN_C = 16
CHUNK = 128
SGU_GROUPS = 4
EPS = 1e-6
NEG_INF = -1e30
Q_ROWS = 4
K_ROWS = 12
TQ = Q_ROWS * GRID_W
TK = K_ROWS * GRID_W
N_DIAG = 22
ATT_SUB = 8
ATT_SCALE = HEAD_DIM ** -0.5
LOG2E = 1.4426950408889634
LN2 = 0.6931471805599453

ADAM_LR = 0.001
ADAM_B1 = 0.9
ADAM_B2 = 0.999
ADAM_EPS = 1e-08
ADAM_WD = 0.01
ADAM_STEP = 10

VMEM_LIMIT = 56 * 1024 * 1024
MESH = pl.DeviceIdType.MESH

V_DSHIFT, V_DSCALE, V_DGATE, V_DCSHIFT, V_DCSCALE, V_ZERO, V_DNG, V_DNG_CTX, V_MISC = range(9)
M_DBS = 512
MOD_SHIFT, MOD_SCALE, MOD_GATE, MOD_CSHIFT, MOD_CSCALE = range(5)


def _params(sem=None):
    return pltpu.CompilerParams(dimension_semantics=sem, vmem_limit_bytes=VMEM_LIMIT)


def _sigmoid(x):
    return 1.0 / (1.0 + jnp.exp(-x))


def _gelu_parts(x):
    cdf = 0.5 * (1.0 + lax.erf(x * 0.7071067811865476))
    pdf = jnp.exp(-0.5 * x * x) * 0.3989422804014327
    return x * cdf, cdf + x * pdf


def _nt(a, b):
    return lax.dot_general(a, b, (((1,), (1,)), ((), ())), preferred_element_type=F32)


def _tn(a, b):
    return lax.dot_general(a, b, (((0,), (0,)), ((), ())), preferred_element_type=F32)


def _dot2(v, ones_bf):
    hi = v.astype(BF16)
    lo = (v - hi.astype(F32)).astype(BF16)
    return (jnp.dot(hi, ones_bf, preferred_element_type=F32)
            + jnp.dot(lo, ones_bf, preferred_element_type=F32))


def _head_sum(v, ones_ref):
    return jnp.dot(v.astype(BF16), ones_ref[...], preferred_element_type=F32)


def _adam(w, g, m, v):
    m2 = ADAM_B1 * m + (1.0 - ADAM_B1) * g
    v2 = ADAM_B2 * v + (1.0 - ADAM_B2) * (g * g)
    m_hat = m2 / (1.0 - ADAM_B1 ** ADAM_STEP)
    v_hat = v2 / (1.0 - ADAM_B2 ** ADAM_STEP)
    delta = -ADAM_LR * (m_hat / (jnp.sqrt(v_hat) + ADAM_EPS) + ADAM_WD * w)
    return delta, m2, v2


class _Hosted:
    def __init__(self, kind, src):
        self.kind = kind
        n_slot = {"a2a": N_DEV, "ag": N_DEV, "chips": N_DEV // 2}[kind]
        blk = src.shape if kind == "ag" else src.shape[1:]
        self.out_shape = jax.ShapeDtypeStruct((n_slot,) + tuple(blk), src.dtype)
        self.n_peer = n_slot - 1
        self.scratch = [pltpu.SemaphoreType.DMA((self.n_peer,)), pltpu.SemaphoreType.DMA((self.n_peer,)),
                        pltpu.SemaphoreType.DMA]

    def _copies(self, src, dst, send_sems, recv_sems, loc_sem, landing):
        x, y, c = lax.axis_index("x"), lax.axis_index("y"), lax.axis_index("c")
        if self.kind == "chips":
            me = 2 * x + y
            peers = [((px, py, c), 2 * px + py) for px, py in ((1 - x, 1 - y), (1 - x, y), (x, 1 - y))]
        else:
            me = 4 * x + 2 * y + c
            peers = []
            for k in range(1, N_DEV):
                px = 1 - x if (k >> 2) & 1 else x
                py = 1 - y if (k >> 1) & 1 else y
                pc = 1 - c if k & 1 else c
                peers.append(((px, py, pc), 4 * px + 2 * py + pc))
        remote = []
        for k, (peer, pid) in enumerate(peers):
            s = src if self.kind == "ag" else src.at[pid]
            remote.append(pltpu.make_async_remote_copy(
                src_ref=s, dst_ref=dst.at[pid if landing else me],
                send_sem=send_sems.at[k], recv_sem=recv_sems.at[k], device_id=peer, device_id_type=MESH))
        local = pltpu.make_async_copy(src if self.kind == "ag" else src.at[me], dst.at[me], loc_sem)
        return remote, local

    def start(self, src, dst, send_sems, recv_sems, loc_sem):
        remote, local = self._copies(src, dst, send_sems, recv_sems, loc_sem, landing=False)
        for cp in remote:
            cp.start()
        local.start()

    def wait(self, src, dst, send_sems, recv_sems, loc_sem):
        remote, local = self._copies(src, dst, send_sems, recv_sems, loc_sem, landing=True)
        for cp in remote:
            cp.wait_recv()
        for cp in remote:
            cp.wait_send()
        local.wait()


ANY_SPEC = pl.BlockSpec(memory_space=pl.ANY)


def _allgather_direct(arrs, name):
    n = len(arrs)
    exs = [_Hosted("ag", a) for a in arrs]

    def body(*refs):
        srcs, dsts, sems = refs[:n], refs[n:2 * n], refs[2 * n:]
        for r, ex in enumerate(exs):
            ex.start(srcs[r], dsts[r], *sems[3 * r:3 * r + 3])
        for r, ex in enumerate(exs):
            ex.wait(srcs[r], dsts[r], *sems[3 * r:3 * r + 3])

    return pl.pallas_call(body, name=name, out_shape=[ex.out_shape for ex in exs], in_specs=[ANY_SPEC] * n,
                          out_specs=[ANY_SPEC] * n, scratch_shapes=[s for ex in exs for s in ex.scratch])(*arrs)


def _ada_scratch(n_col):
    return ([pltpu.VMEM((N_DEV, 8, D_MODEL), F32), pltpu.VMEM((16, n_col), F32), pltpu.VMEM((N_DEV, 16, n_col), F32)]
            + [pltpu.SemaphoreType.DMA((N_DEV - 1,)) for _ in range(4)])


def _ada_modulation(cb_ref, cc_ref, w_ref, b_ref, cstack, part, parts, s1, r1, s2, r2):
    x, y, c = lax.axis_index("x"), lax.axis_index("y"), lax.axis_index("c")
    me = 4 * x + 2 * y + c
    peers = []
    for k in range(1, N_DEV):
        px = 1 - x if (k >> 2) & 1 else x
        py = 1 - y if (k >> 1) & 1 else y
        pc = 1 - c if k & 1 else c
        peers.append(((px, py, pc), 4 * px + 2 * py + pc))

    def exchange(src, dst, send_sems, recv_sems):
        for k, (peer, _) in enumerate(peers):
            pltpu.make_async_remote_copy(src_ref=src, dst_ref=dst.at[me], send_sem=send_sems.at[k],
                                         recv_sem=recv_sems.at[k], device_id=peer, device_id_type=MESH).start()
        dst[me] = src[...]
        waits = [pltpu.make_async_remote_copy(src_ref=src, dst_ref=dst.at[pid], send_sem=send_sems.at[k],
                                              recv_sem=recv_sems.at[k], device_id=peer, device_id_type=MESH)
                 for k, (peer, pid) in enumerate(peers)]
        for cp in waits:
            cp.wait_recv()
        for cp in waits:
            cp.wait_send()

    exchange(cb_ref, cstack, s1, r1)
    c_all = cstack[0]
    for d in range(1, N_DEV):
        c_all = c_all + cstack[d]
    row = lax.broadcasted_iota(jnp.int32, (8, D_MODEL), 0)
    cc = jnp.where(row == 0, jnp.broadcast_to(cc_ref[...], (8, D_MODEL)), 0.0)
    call = jnp.concatenate([c_all, cc], axis=0)
    s = call * _sigmoid(call)
    part[...] = jnp.dot(s, w_ref[...], preferred_element_type=F32, precision=HI) + b_ref[...]
    exchange(part, parts, s2, r2)
    return s


def _ada_bwd(s16, dm, dc, w, m, v):
    def body(s_ref, dm_ref, dc_ref, w_ref, m_ref, v_ref, g_ref, d_ref, m2_ref, v2_ref, pc_ref):
        dct = jnp.sum(dc_ref[...], axis=0, keepdims=True)
        row = lax.broadcasted_iota(jnp.int32, dc_ref.shape, 0)
        dcb = jnp.where(row == 0, jnp.broadcast_to(dct, dc_ref.shape), 0.0)
        dm16 = jnp.concatenate([dm_ref[...], dcb], axis=0)
        g = lax.dot_general(s_ref[...], dm16, (((0,), (0,)), ((), ())),
                            preferred_element_type=F32, precision=HI)
        w_ = w_ref[...]
        delta, m2, v2 = _adam(w_, g, m_ref[...], v_ref[...])
        g_ref[...] = g
        d_ref[...] = delta
        m2_ref[...] = m2
        v2_ref[...] = v2
        pc_ref[...] = lax.dot_general(dcb, w_, (((1,), (1,)), ((), ())),
                                      preferred_element_type=F32, precision=HI)

    R, n_col = w.shape
    tr = 256
    sh = jax.ShapeDtypeStruct(w.shape, F32)
    rows = pl.BlockSpec((tr, n_col), lambda i: (i, 0))
    small = pl.BlockSpec(dm.shape, lambda i: (0, 0))
    return pl.pallas_call(
        body, name="ada_bwd", grid=(R // tr,),
        in_specs=[pl.BlockSpec((16, tr), lambda i: (0, i)), small, small, rows, rows, rows],
        out_specs=[rows, rows, rows, rows, pl.BlockSpec((8, tr), lambda i: (0, i))],
        out_shape=[sh, sh, sh, sh, jax.ShapeDtypeStruct((8, D_MODEL), F32)],
        compiler_params=_params(("arbitrary",)),
    )(s16, dm, dc, w, m, v)


def _head_norm(zk, ones_ref, gain):
    ss = _head_sum(zk * zk, ones_ref)
    return zk * lax.rsqrt(ss * (1.0 / HEAD_DIM) + EPS) * gain


def _inproj_fwd(order, x, ng, c_blk, c_ctx_row, w_ada_sh, b_ada_sh, w_blk_t, tm):
    T = x.shape[0]
    nt = T // tm
    n_pass = N_DEV // 2
    blk_rows = w_blk_t.shape[0]
    n_col = w_ada_sh.shape[1]

    def body(order_ref, x_ref, g_ref, cb_ref, cc_ref, wa_ref, ba_ref, wb_ref,
             h_out, z_ref, wt_out, s_out, mod_out,
             hs, wt, modv, send_sems, recv_sems, loc_sem, h_sem, wt_sem, *ada_sc):
        p, i = pl.program_id(0), pl.program_id(1)
        x, y, c = lax.axis_index("x"), lax.axis_index("y"), lax.axis_index("c")
        me, sib = (x, y, c), (x, y, 1 - c)
        chips = [(1 - x, y), (x, 1 - y), (1 - x, 1 - y)]

        def slot(px, py, pc):
            return 4 * px + 2 * py + pc

        def copy(k, block, to, src=None):
            return pltpu.make_async_remote_copy(
                src_ref=wt.at[slot(*block)] if src is None else src, dst_ref=wt.at[slot(*block)],
                send_sem=send_sems.at[k], recv_sem=recv_sems.at[k], device_id=to, device_id_type=MESH)

        own = pltpu.make_async_copy(wb_ref, wt.at[slot(*me)], loc_sem)
        h_copy = pltpu.make_async_copy(hs, h_out, h_sem)
        wt_copy = pltpu.make_async_copy(wt, wt_out, wt_sem)
        first = [copy(1 + j, me, (*chip, c), src=wb_ref) for j, chip in enumerate(chips[:2])] + [copy(0, me, sib, src=wb_ref)]
        passed = [copy(4 + j, (*chip, c), sib) for j, chip in enumerate(chips)]
        relay_src = (jnp.where(c == 0, 1 - x, x), jnp.where(c == 0, y, 1 - y), c)
        relay_dst = (jnp.where(c == 0, x, 1 - x), jnp.where(c == 0, 1 - y, y), c)
        relay = copy(3, relay_src, relay_dst)

        @pl.when(jnp.logical_and(p == 0, i == 0))
        def _():
            s_out[...] = _ada_modulation(cb_ref, cc_ref, wa_ref, ba_ref, *ada_sc)
            own.start()
            for cp in first:
                cp.start()
            parts = ada_sc[2]
            my_row = pl.ds(slot(*me), 1)
            mod = jnp.concatenate([parts[d, my_row, :] for d in range(N_DEV)], axis=1)
            cmod = jnp.concatenate([parts[d, 8:9, :] for d in range(N_DEV)], axis=1)
            modv[...] = jnp.zeros_like(modv)
            for r, val in enumerate((mod[:, 0:D_MODEL], mod[:, D_MODEL:2 * D_MODEL], mod[:, 2 * D_MODEL:],
                                     cmod[:, 0:D_MODEL], cmod[:, D_MODEL:2 * D_MODEL])):
                modv[r:r + 1, :] = val
                mod_out[r] = val
            for r in range(5, 8):
                mod_out[r] = jnp.zeros((1, D_MODEL), F32)
            own.wait()
            copy(0, sib, me).wait_recv()

        @pl.when(jnp.logical_and(p == 1, i == 0))
        def _():
            for j, chip in enumerate(chips[:2]):
                copy(1 + j, (*chip, c), me).wait_recv()
            relay.start()
            passed[0].start()
            passed[1].start()
            copy(4, (*chips[0], 1 - c), me).wait_recv()

        @pl.when(jnp.logical_and(p == 2, i == 0))
        def _():
            copy(5, (*chips[1], 1 - c), me).wait_recv()

        @pl.when(jnp.logical_and(p == 3, i == 0))
        def _():
            copy(3, (*chips[2], c), me).wait_recv()
            passed[2].start()
            copy(6, (*chips[2], 1 - c), me).wait_recv()
            wt_copy.start()

        rows = pl.ds(pl.multiple_of(i * tm, tm), tm)

        @pl.when(p == 0)
        def _():
            xv = x_ref[...]
            r = lax.rsqrt(jnp.mean(xv * xv, axis=-1, keepdims=True) + EPS)
            hs[rows, :] = ((xv * r * g_ref[...]) * (1.0 + modv[1:2, :]) + modv[0:1, :]).astype(BF16)

        @pl.when(jnp.logical_and(p == 1, i == 0))
        def _():
            h_copy.start()

        w_pair = wt[pl.ds(2 * order_ref[p], 2)].reshape(2 * blk_rows, D_MODEL)
        z_ref[...] = _nt(hs[rows, :], w_pair).astype(BF16)

        @pl.when(jnp.logical_and(p == n_pass - 1, i == nt - 1))
        def _():
            for cp in first + passed + [relay]:
                cp.wait_send()
            h_copy.wait()
            wt_copy.wait()

    whole = lambda shape: pl.BlockSpec(shape, lambda p, i, o: (0,) * len(shape))
    grid_spec = pltpu.PrefetchScalarGridSpec(
        num_scalar_prefetch=1, grid=(n_pass, nt),
        in_specs=[pl.BlockSpec((tm, D_MODEL), lambda p, i, o: (jnp.where(p == 0, i, nt - 1), 0)),
                  whole((1, D_MODEL)), whole((8, D_MODEL)), whole((1, D_MODEL)), whole((D_MODEL, n_col)),
                  whole((1, n_col)), ANY_SPEC],
        out_specs=[ANY_SPEC, pl.BlockSpec((tm, 2 * blk_rows), lambda p, i, o: (i, o[p])), ANY_SPEC,
                   whole((16, D_MODEL)), whole((8, 1, D_MODEL))],
        scratch_shapes=[pltpu.VMEM((T, D_MODEL), BF16), pltpu.VMEM((N_DEV, blk_rows, D_MODEL), BF16),
                        pltpu.VMEM((8, D_MODEL), F32),
                        pltpu.SemaphoreType.DMA((7,)), pltpu.SemaphoreType.DMA((7,)), pltpu.SemaphoreType.DMA,
                        pltpu.SemaphoreType.DMA, pltpu.SemaphoreType.DMA] + _ada_scratch(n_col))
    return pl.pallas_call(
        body, name="inproj_fwd", grid_spec=grid_spec,
        out_shape=[jax.ShapeDtypeStruct((T, D_MODEL), BF16), jax.ShapeDtypeStruct((T, D_IN), BF16),
                   jax.ShapeDtypeStruct((N_DEV, blk_rows, D_MODEL), BF16),
                   jax.ShapeDtypeStruct((16, D_MODEL), F32), jax.ShapeDtypeStruct((8, 1, D_MODEL), F32)],
        compiler_params=_params(("arbitrary", "arbitrary")),
    )(order, x, ng, c_blk, c_ctx_row, w_ada_sh, b_ada_sh, w_blk_t)


def _mod_row(row):
    return pl.BlockSpec((1, 1, D_MODEL), lambda *idx: (row, 0, 0))


def _ctx_fwd(ctx, ng, mod, w_in_t, ones_blk, gk):
    C = ctx.shape[0]

    def body(x_ref, g_ref, sc_ref, sh_ref, w_ref, ones_ref, gk_ref, h_ref, z_ref, kn_ref):
        xv = x_ref[...]
        r = lax.rsqrt(jnp.mean(xv * xv, axis=-1, keepdims=True) + EPS)
        h = (xv * r * g_ref[...]) * (1.0 + sc_ref[0]) + sh_ref[0]
        hb = h.astype(BF16)
        h_ref[...] = hb
        zk = _nt(hb, w_ref[0:512, :])
        zv = _nt(hb, w_ref[512:1024, :])
        z_ref[:, 0:512] = zk.astype(BF16)
        z_ref[:, 512:1024] = zv.astype(BF16)
        kn_ref[...] = _head_norm(zk, ones_ref, gk_ref[...]).astype(BF16)

    vec = pl.BlockSpec((1, D_MODEL), lambda i: (0, 0))
    return pl.pallas_call(
        body, name="ctx_fwd", grid=(1,),
        in_specs=[pl.BlockSpec((C, D_MODEL), lambda i: (0, 0)), vec, _mod_row(MOD_CSCALE), _mod_row(MOD_CSHIFT),
                  pl.BlockSpec((1024, D_MODEL), lambda i: (2, 0)),
                  pl.BlockSpec((512, 512), lambda i: (0, 0)), pl.BlockSpec((1, 512), lambda i: (0, 0))],
        out_specs=[pl.BlockSpec((C, D_MODEL), lambda i: (0, 0)), pl.BlockSpec((C, 1024), lambda i: (0, 0)),
                   pl.BlockSpec((C, 512), lambda i: (0, 0))],
        out_shape=[jax.ShapeDtypeStruct((C, D_MODEL), BF16), jax.ShapeDtypeStruct((C, 1024), BF16),
                   jax.ShapeDtypeStruct((C, 512), BF16)],
        compiler_params=_params(("arbitrary",)),
    )(ctx, ng, mod, mod, w_in_t, ones_blk, gk)


def _sgu_chunk_fwd(au, av, ag, sg, ws_bf, bsb):
    gu, dgu = _gelu_parts(au)
    gv, dgv = _gelu_parts(av)
    rr = lax.rsqrt(jnp.mean(gv * gv, axis=-1, keepdims=True) + EPS)
    vhat = gv * rr
    vn = vhat * sg
    mixed = jnp.dot(ws_bf, vn.astype(BF16), preferred_element_type=F32) + bsb
    sig = _sigmoid(ag)
    sl = ag * sig
    return gu * mixed * sl, (gu, dgu, dgv, rr, vhat, vn, mixed, sig, sl)


def _sgu_qk_fwd(z, sgn, ws, bsb, ones_blk, gq, gk, tm):
    T = z.shape[0]

    def body(au_ref, av_ref, ag_ref, q_ref, k_ref, sg_ref, ws_ref, bsb_ref, ones_ref, gq_ref, gk_ref,
             o_ref, qs_ref, kn_ref):
        qs = _head_norm(q_ref[...].astype(F32), ones_ref, gq_ref[...]) * (ATT_SCALE * LOG2E)
        qs_ref[...] = qs.astype(BF16)
        kn_ref[...] = _head_norm(k_ref[...].astype(F32), ones_ref, gk_ref[...]).astype(BF16)
        for g in range(SGU_GROUPS):
            ws_bf = ws_ref[g].astype(BF16)
            sg = sg_ref[:, 128 * g:128 * (g + 1)]
            bsb_g = bsb_ref[g]
            for j in range(tm // CHUNK):
                rs, cs = slice(CHUNK * j, CHUNK * (j + 1)), slice(128 * g, 128 * (g + 1))
                out, _ = _sgu_chunk_fwd(au_ref[rs, cs].astype(F32), av_ref[rs, cs].astype(F32),
                                        ag_ref[rs, cs].astype(F32), sg, ws_bf, bsb_g)
                o_ref[rs, cs] = out.astype(BF16)

    zcol = lambda col: pl.BlockSpec((tm, 512), lambda i: (i, col))
    v512 = pl.BlockSpec((1, 512), lambda i: (0, 0))
    row = pl.BlockSpec((tm, 512), lambda i: (i, 0))
    out = jax.ShapeDtypeStruct((T, 512), BF16)
    return pl.pallas_call(
        body, name="sgu_qk_fwd", grid=(T // tm,),
        in_specs=[zcol(0), zcol(1), zcol(2), zcol(3), zcol(4), v512,
                  pl.BlockSpec((SGU_GROUPS, CHUNK, CHUNK), lambda i: (0, 0, 0)),
                  pl.BlockSpec((SGU_GROUPS, CHUNK, 128), lambda i: (0, 0, 0)),
                  pl.BlockSpec((512, 512), lambda i: (0, 0)), v512, v512],
        out_specs=[row, row, row], out_shape=[out, out, out],
        compiler_params=_params(("arbitrary",)),
    )(z, z, z, z, z, sgn, ws, bsb, ones_blk, gq, gk)


def _attn_type(rb, nrb):
    return jnp.where(rb == 0, 0, jnp.where(rb == nrb - 1, 2, 1))


def _attn_specs(T, C, att_sub):
    return [
        pl.BlockSpec((att_sub * TQ, 128), lambda hp, st: (st, hp)),
        pl.BlockSpec((T, 128), lambda hp, st: (0, hp)),
        pl.BlockSpec((T, 128), lambda hp, st: (0, 20 + hp)),
        pl.BlockSpec((C, 128), lambda hp, st: (0, hp)),
        pl.BlockSpec((C, 128), lambda hp, st: (0, 4 + hp)),
        pl.BlockSpec((2, 2 * WIN_R, GRID_W, 128), lambda hp, st: (hp, 0, 0, 0)),
        pl.BlockSpec((3, TQ, TK), lambda hp, st: (0, 0, 0)),
        pl.BlockSpec((att_sub * TQ, 128), lambda hp, st: (st, 24 + hp)),
    ]


def _build_bias(pairs_ref, mask_ref, bias_sc):
    for t in range(3):
        for hh in range(2):
            for i in range(Q_ROWS):
                for mm in range(K_ROWS // 2):
                    p = min(max(WIN_R - Q_ROWS * t + 2 * mm - i, 0), 2 * WIN_R - 1)
                    rs, cs = slice(GRID_W * i, GRID_W * (i + 1)), slice(128 * mm, 128 * (mm + 1))
                    bias_sc[t, hh, rs, cs] = (pairs_ref[hh, p] + mask_ref[t, rs, cs]) * LOG2E


def _attn_fwd(qs, kn, z, ckn, zc, pairs, row_mask, riders=()):
    T, C = qs.shape[0], ckn.shape[0]
    rows = T // GRID_W
    nrb = rows // Q_ROWS
    att_sub = min(ATT_SUB, nrb)
    n_st = nrb // att_sub
    n_rid = len(riders)

    def body(q_ref, k_ref, v_ref, ck_ref, cv_ref, pairs_ref, mask_ref, bg_ref, *rest):
        rid_src = rest[:n_rid]
        ob_ref, outb_ref, lse_ref = rest[n_rid:n_rid + 3]
        rid_dst = rest[n_rid + 3:2 * n_rid + 3]
        bias_sc = rest[2 * n_rid + 3]
        rid_sems = rest[2 * n_rid + 4:]

        @pl.when(jnp.logical_and(pl.program_id(0) == 0, pl.program_id(1) == 0))
        def _():
            for r, (ex, _) in enumerate(riders):
                ex.start(rid_src[r], rid_dst[r], *rid_sems[3 * r:3 * r + 3])

        @pl.when(pl.program_id(1) == 0)
        def _():
            _build_bias(pairs_ref, mask_ref, bias_sc)

        ck2, cv2 = ck_ref[...], cv_ref[...]
        lane = lax.broadcasted_iota(jnp.int32, (1, 128), 1)
        for sub in range(att_sub):
            rb = att_sub * pl.program_id(1) + sub
            bias_ref = bias_sc.at[_attn_type(rb, nrb)]
            rs = slice(TQ * sub, TQ * (sub + 1))
            ks = pl.multiple_of(jnp.clip(Q_ROWS * rb - 4, 0, rows - K_ROWS) * GRID_W, GRID_W)
            q2 = q_ref[rs, :]
            k2 = k_ref[pl.ds(ks, TK), :]
            v2 = v_ref[pl.ds(ks, TK), :]
            o_acc = jnp.zeros((TQ, 128), F32)
            lse_acc = jnp.zeros((TQ, 128), F32)
            for hh in range(2):
                msk = (lane >= HEAD_DIM) == bool(hh)
                qm = jnp.where(msk, q2, jnp.zeros_like(q2))
                s = _nt(qm, k2) + bias_ref[hh]
                sc = _nt(qm, ck2)
                m = jnp.maximum(jnp.max(s, axis=-1, keepdims=True), jnp.max(sc, axis=-1, keepdims=True))
                p = jnp.exp2(s - m)
                pc = jnp.exp2(sc - m)
                va = jnp.where(msk, v2, jnp.ones_like(v2))
                cva = jnp.where(msk, cv2, jnp.ones_like(cv2))
                num = (jnp.dot(p.astype(BF16), va, preferred_element_type=F32)
                       + jnp.dot(pc.astype(BF16), cva, preferred_element_type=F32))
                den = pltpu.roll(num, HEAD_DIM, 1)
                o_acc = jnp.where(msk, num / den, o_acc)
                lse_acc = jnp.where(msk, m + jnp.log(den) * LOG2E, lse_acc)
            ob_ref[rs, :] = o_acc.astype(BF16)
            lse_ref[rs, :] = lse_acc
            bg = bg_ref[rs, :].astype(F32)
            outb_ref[rs, :] = (o_acc * (bg * _sigmoid(bg))).astype(BF16)

        @pl.when(jnp.logical_and(pl.program_id(0) == pl.num_programs(0) - 1, pl.program_id(1) == n_st - 1))
        def _():
            for r, (ex, _) in enumerate(riders):
                ex.wait(rid_src[r], rid_dst[r], *rid_sems[3 * r:3 * r + 3])

    tile = pl.BlockSpec((att_sub * TQ, 128), lambda hp, st: (st, hp))
    res = pl.pallas_call(
        body, name="attn_fwd", grid=(4, n_st),
        in_specs=_attn_specs(T, C, att_sub) + [ANY_SPEC] * n_rid,
        out_specs=[tile, tile, tile] + [ANY_SPEC] * n_rid,
        out_shape=[jax.ShapeDtypeStruct((T, 512), BF16), jax.ShapeDtypeStruct((T, 512), BF16),
                   jax.ShapeDtypeStruct((T, 512), F32)] + [ex.out_shape for ex, _ in riders],
        scratch_shapes=[pltpu.VMEM((3, 2, TQ, TK), F32)] + [s for ex, _ in riders for s in ex.scratch],
        compiler_params=_params(("arbitrary", "arbitrary")),
    )(qs, kn, z, ckn, zc, pairs, row_mask, z, *[a for _, a in riders])
    return res[:3], res[3:]


def _outproj_loss_bwd(x, tgt, out_a, out_b, mod, w_out, tm):
    T = x.shape[0]
    nt = T // tm

    def body(x_ref, t_ref, oa_ref, ob_ref, gate_ref, w_ref, dy_ref, dmc_ref, dw_ref, dgate_ref, loss_ref, acc):
        @pl.when(pl.program_id(0) == 0)
        def _():
            acc[...] = jnp.zeros_like(acc)
            dgate_ref[...] = jnp.zeros_like(dgate_ref)
            loss_ref[...] = jnp.zeros_like(loss_ref)

        oa, ob = oa_ref[...], ob_ref[...]
        gate_v = gate_ref[0]
        mix = (jnp.dot(oa, w_ref[0:512, :], preferred_element_type=F32)
               + jnp.dot(ob, w_ref[512:1024, :], preferred_element_type=F32))
        e = x_ref[...] + gate_v * mix - t_ref[...]
        se = jnp.sum(jnp.sum(e * e, axis=0, keepdims=True), axis=1, keepdims=True)
        loss_ref[...] += jnp.broadcast_to(se * (0.5 / D_MODEL), loss_ref.shape)
        dy = e * (1.0 / D_MODEL)
        dy_ref[...] = dy
        dgate_ref[...] += jnp.sum(dy * mix, axis=0, keepdims=True)
        dmix = (dy * gate_v).astype(BF16)
        dmc_ref[...] = _nt(dmix, w_ref[...]).astype(BF16)
        acc[0:512, :] += _tn(oa, dmix)
        acc[512:1024, :] += _tn(ob, dmix)

        @pl.when(pl.program_id(0) == nt - 1)
        def _():
            dw_ref[...] = acc[...].astype(BF16)

    row = lambda w: pl.BlockSpec((tm, w), lambda i: (i, 0))
    return pl.pallas_call(
        body, name="outproj_loss_bwd", grid=(nt,),
        in_specs=[row(D_MODEL), row(D_MODEL), row(512), row(512), _mod_row(MOD_GATE),
                  pl.BlockSpec((D_MODEL, D_MODEL), lambda i: (0, 0))],
        out_specs=[row(D_MODEL), row(D_MODEL), pl.BlockSpec((D_MODEL, D_MODEL), lambda i: (0, 0)),
                   pl.BlockSpec((1, D_MODEL), lambda i: (0, 0)), pl.BlockSpec((1, 128), lambda i: (0, 0))],
        out_shape=[jax.ShapeDtypeStruct((T, D_MODEL), F32), jax.ShapeDtypeStruct((T, D_MODEL), BF16),
                   jax.ShapeDtypeStruct((D_MODEL, D_MODEL), BF16), jax.ShapeDtypeStruct((1, D_MODEL), F32),
                   jax.ShapeDtypeStruct((1, 128), F32)],
        scratch_shapes=[pltpu.VMEM((D_MODEL, D_MODEL), F32)],
        compiler_params=_params(("arbitrary",)),
    )(x, tgt, out_a, out_b, mod, w_out)


def _attn_bwd(qs, kn, z, ckn, zc, pairs, row_mask, ob, lse, dmc, hsel, fold, riders):
    T, C = qs.shape[0], ckn.shape[0]
    rows = T // GRID_W
    nrb = rows // Q_ROWS
    att_sub = min(ATT_SUB, nrb)
    n_st = nrb // att_sub
    n_rid = len(riders)

    def body(q_ref, k_ref, v_ref, ck_ref, cv_ref, pairs_ref, mask_ref, bg_ref, ob_ref, lse_ref, do_ref,
             hsel_ref, fold_ref, *rest):
        rid_src = rest[:n_rid]
        dq_ref, dk_ref, dv_ref, dck_ref, dcv_ref, dbg_ref, drpb_ref = rest[n_rid:n_rid + 7]
        rid_dst = rest[n_rid + 7:2 * n_rid + 7]
        bias_sc, dacc_ref = rest[2 * n_rid + 7:2 * n_rid + 9]
        rid_sems = rest[2 * n_rid + 9:]
        hp, st = pl.program_id(0), pl.program_id(1)

        @pl.when(jnp.logical_and(hp == 0, st == 0))
        def _():
            for r, (ex, _) in enumerate(riders):
                ex.start(rid_src[r], rid_dst[r], *rid_sems[3 * r:3 * r + 3])

        @pl.when(st == 0)
        def _():
            _build_bias(pairs_ref, mask_ref, bias_sc)
            dk_ref[...] = jnp.zeros_like(dk_ref)
            dv_ref[...] = jnp.zeros_like(dv_ref)
            dck_ref[...] = jnp.zeros_like(dck_ref)
            dcv_ref[...] = jnp.zeros_like(dcv_ref)
            dacc_ref[...] = jnp.zeros_like(dacc_ref)

        ck2, cv2 = ck_ref[...], cv_ref[...]
        lane = lax.broadcasted_iota(jnp.int32, (1, 128), 1)
        for sub in range(att_sub):
            rb = att_sub * st + sub
            bias_ref = bias_sc.at[_attn_type(rb, nrb)]
            rs = slice(TQ * sub, TQ * (sub + 1))
            kb = jnp.clip(Q_ROWS * rb - 4, 0, rows - K_ROWS)
            ks = pl.multiple_of(kb * GRID_W, GRID_W)
            ebase = kb - Q_ROWS * rb + 11
            q2 = q_ref[rs, :]
            k2 = k_ref[pl.ds(ks, TK), :]
            v2 = v_ref[pl.ds(ks, TK), :]
            bg = bg_ref[rs, :].astype(F32)
            sig = _sigmoid(bg)
            obv = ob_ref[rs, :].astype(F32)
            dout = do_ref[rs, :].astype(F32)
            dbg_ref[rs, :] = (dout * obv * (sig * (1.0 + bg * (1.0 - sig)))).astype(BF16)
            d_o = dout * (bg * sig)
            d_oo = d_o * obv
            lse2 = lse_ref[rs, :]
            dq_acc = jnp.zeros((TQ, 128), F32)
            for hh in range(2):
                msk = (lane >= HEAD_DIM) == bool(hh)
                qm = jnp.where(msk, q2, jnp.zeros_like(q2))
                lse_h = jnp.max(jnp.where(msk, lse2, -jnp.inf), axis=-1, keepdims=True)
                p = jnp.exp2(_nt(qm, k2) + bias_ref[hh] - lse_h)
                pc = jnp.exp2(_nt(qm, ck2) - lse_h)
                dom_f = jnp.where(msk, d_o, 0.0)
                dom = dom_f.astype(BF16)
                delta = jnp.sum(jnp.where(msk, d_oo, 0.0), axis=-1, keepdims=True)
                d_hi = delta.astype(BF16).astype(F32)
                x0 = HEAD_DIM * (1 - hh)
                dom_aug = jnp.where(lane == x0, -d_hi, jnp.where(lane == x0 + 1, d_hi - delta, dom_f)).astype(BF16)
                extra = jnp.logical_or(lane == x0, lane == x0 + 1)
                va = jnp.where(msk, v2, jnp.where(extra, jnp.ones_like(v2), jnp.zeros_like(v2)))
                cva = jnp.where(msk, cv2, jnp.where(extra, jnp.ones_like(cv2), jnp.zeros_like(cv2)))
                ds = p * _nt(dom_aug, va)
                dsc = pc * _nt(dom_aug, cva)
                dsb, dscb = ds.astype(BF16), dsc.astype(BF16)
                dq_h = (jnp.dot(dsb, k2, preferred_element_type=F32)
                        + jnp.dot(dscb, ck2, preferred_element_type=F32))
                dq_acc = jnp.where(msk, dq_h, dq_acc)
                dk_ref[pl.ds(ks, TK), :] += _tn(dsb, qm)
                dv_ref[pl.ds(ks, TK), :] += _tn(p.astype(BF16), dom)
                dck_ref[...] += _tn(dscb, qm)
                dcv_ref[...] += _tn(pc.astype(BF16), dom)
                for i in range(Q_ROWS):
                    for mm in range(K_ROWS // 2):
                        dacc_ref[hh, ebase + (2 * mm - i)] += ds[GRID_W * i:GRID_W * (i + 1),
                                                                 128 * mm:128 * (mm + 1)]
            dq_ref[rs, :] = dq_acc

        @pl.when(st == n_st - 1)
        def _():
            for hh in range(2):
                drpb_ref[hh] = _rpb_diag_sums(dacc_ref.at[hh], hsel_ref, fold_ref)

        @pl.when(jnp.logical_and(hp == pl.num_programs(0) - 1, st == n_st - 1))
        def _():
            for r, (ex, _) in enumerate(riders):
                ex.wait(rid_src[r], rid_dst[r], *rid_sems[3 * r:3 * r + 3])

    tile = pl.BlockSpec((att_sub * TQ, 128), lambda hp, st: (st, hp))
    colT = pl.BlockSpec((T, 128), lambda hp, st: (0, hp))
    colC = pl.BlockSpec((C, 128), lambda hp, st: (0, hp))
    res = pl.pallas_call(
        body, name="attn_bwd", grid=(4, n_st),
        in_specs=(_attn_specs(T, C, att_sub) + [tile, tile,
                                                pl.BlockSpec((att_sub * TQ, 128), lambda hp, st: (st, 4 + hp)),
                                       pl.BlockSpec((8, 128, 128), lambda hp, st: (0, 0, 0)),
                                       pl.BlockSpec((32, 256), lambda hp, st: (0, 0))]
                  + [ANY_SPEC] * n_rid),
        out_specs=([tile, colT, colT, colC, colC, tile, pl.BlockSpec((2, 32, 128), lambda hp, st: (hp, 0, 0))]
                   + [ANY_SPEC] * n_rid),
        out_shape=([jax.ShapeDtypeStruct((T, 512), F32), jax.ShapeDtypeStruct((T, 512), F32),
                    jax.ShapeDtypeStruct((T, 512), F32), jax.ShapeDtypeStruct((C, 512), F32),
                    jax.ShapeDtypeStruct((C, 512), F32), jax.ShapeDtypeStruct((T, 512), BF16),
                    jax.ShapeDtypeStruct((N_HEADS, 32, 128), F32)] + [ex.out_shape for ex, _ in riders]),
        scratch_shapes=([pltpu.VMEM((3, 2, TQ, TK), F32), pltpu.VMEM((2, N_DIAG, GRID_W, 128), F32)]
                        + [s for ex, _ in riders for s in ex.scratch]),
        compiler_params=_params(("arbitrary", "arbitrary")),
    )(qs, kn, z, ckn, zc, pairs, row_mask, z, ob, lse, dmc, hsel, fold, *[a for _, a in riders])
    return res[:7], res[7:]


def _rpb_diag_sums(a_ref, hsel_ref, fold_ref):
    n_off = 2 * WIN_C - 1
    n_dr = 2 * WIN_R - 1
    qc = lax.broadcasted_iota(jnp.int32, (GRID_W, 128), 0)
    lane = lax.broadcasted_iota(jnp.int32, (GRID_W, 128), 1)
    diff = lane % GRID_W - qc + (WIN_C - 1)
    left = lane < GRID_W

    def by_dr(dr):
        return a_ref[dr + 4] + pltpu.roll(a_ref[dr + 3], GRID_W, 1)

    out = jnp.zeros((32, 128), F32)
    for j in range((n_dr + 1) // 2):
        hi = pltpu.roll(by_dr(2 * j + 1), GRID_W, 1) if 2 * j + 1 < n_dr else 0.0
        pair = jnp.where(left, by_dr(2 * j), hi)
        parts = []
        for o in range(n_off):
            mv = jnp.where(diff == o, pair, 0.0)
            acc = mv[0:8]
            for r8 in range(1, GRID_W // 8):
                acc = acc + mv[8 * r8:8 * (r8 + 1)]
            parts.append(acc)
        parts.append(jnp.zeros((8, 128), F32))
        stack = jnp.concatenate(parts, axis=0)
        s_hi = stack.astype(BF16)
        s_lo = (stack - s_hi.astype(F32)).astype(BF16)
        per_o = (jnp.dot(fold_ref[...], s_hi, preferred_element_type=F32)
                 + jnp.dot(fold_ref[...], s_lo, preferred_element_type=F32))
        out = out + _dot2(per_o, hsel_ref[j])
    return out


def _head_norm_bwd(raw, dn, gain, ones_ref):
    rr = lax.rsqrt(_head_sum(raw * raw, ones_ref) * (1.0 / HEAD_DIM) + EPS)
    hat = raw * rr
    dgain = jnp.sum(dn * hat, axis=0, keepdims=True)
    dhat = dn * gain
    mean = _head_sum(dhat * hat, ones_ref) * (1.0 / HEAD_DIM)
    return rr * (dhat - hat * mean), dgain


def _ctx_k_bwd(zc, dck, dcv, ones_blk, gk, foldm):
    C = dck.shape[0]

    def body(bk_ref, dk_ref, dv_ref, ones_ref, gk_ref, fold_ref, dbk_ref, dbv_ref, dgk_ref):
        dbk, dgk = _head_norm_bwd(bk_ref[...].astype(F32), dk_ref[...] * LN2, gk_ref[...], ones_ref)
        dbk_ref[...] = dbk.astype(BF16)
        dbv_ref[...] = dv_ref[...].astype(BF16)
        dgk_ref[...] = jnp.dot(jnp.broadcast_to(dgk, (8, 512)), fold_ref[...],
                               preferred_element_type=F32, precision=HI)

    row = pl.BlockSpec((C, 512), lambda i: (0, 0))
    cst = lambda a, b: pl.BlockSpec((a, b), lambda i: (0, 0))
    out_row = jax.ShapeDtypeStruct((C, 512), BF16)
    return pl.pallas_call(
        body, name="ctx_k_bwd", grid=(1,),
        in_specs=[row, row, row, cst(512, 512), cst(1, 512), cst(512, 128)],
        out_specs=[row, row, cst(8, 128)],
        out_shape=[out_row, out_row, jax.ShapeDtypeStruct((8, 128), F32)],
        compiler_params=_params(("arbitrary",)),
    )(zc, dck, dcv, ones_blk, gk, foldm)


def _bwd_mid(z, dmc, dqs, dk, dv, db_g, h, hc, dzc_k, dzc_v, sgn, ws, wst, bsb, ones8, ones_blk, gq, gk, foldm, tk,
             riders=()):
    T = z.shape[0]
    nt = T // tk
    blk = D_IN // N_DEV
    n_in, n_out, n_sc = 23, 7, 9
    n_rid = len(riders)

    def body(*refs):
        (au_ref, av_ref, ag_ref, bq_ref, bk_ref, d_ref, dq_ref, dk_ref, dv_ref, dbg_ref, h_ref,
         hc_ref, dzck_ref, dzcv_ref, sg_ref, ws_ref, wst_ref, bsb_ref, ones8_ref, ones_ref, gq_ref, gk_ref,
         fold_ref) = refs[:n_in]
        rid_src = refs[n_in:n_in + n_rid]
        dz_ref, sums_out, dws_ref, dbs_ref, dsg_ref, dgq_ref, dgk_ref = refs[n_in + n_rid:n_in + n_rid + n_out]
        rid_dst = refs[n_in + n_rid + n_out:n_in + 2 * n_rid + n_out]
        (acc, accq, acck, stage, sem, send_buf, tmp, s1, r1) = refs[n_in + 2 * n_rid + n_out:
                                                                   n_in + 2 * n_rid + n_out + n_sc]
        rid_sems = refs[n_in + 2 * n_rid + n_out + n_sc:]
        t = pl.program_id(0)

        @pl.when(t == 0)
        def _():
            for r, (ex, _) in enumerate(riders):
                ex.start(rid_src[r], rid_dst[r], *rid_sems[3 * r:3 * r + 3])
            acc[...] = jnp.zeros_like(acc)
            acc[512 * 4:512 * 5, :] = _tn(dzck_ref[...], hc_ref[...])
            acc[512 * 5:512 * 6, :] = _tn(dzcv_ref[...], hc_ref[...])
            dws_ref[...] = jnp.zeros_like(dws_ref)
            dbs_ref[...] = jnp.zeros_like(dbs_ref)
            dsg_ref[...] = jnp.zeros_like(dsg_ref)
            accq[...] = jnp.zeros_like(accq)
            acck[...] = jnp.zeros_like(acck)

        for g in range(SGU_GROUPS):
            ws_bf = ws_ref[g].astype(BF16)
            wst_bf = wst_ref[g].astype(BF16)
            sg = sg_ref[:, 128 * g:128 * (g + 1)]
            bsb_g = bsb_ref[g]
            for j in range(tk // CHUNK):
                rs, cs = slice(CHUNK * j, CHUNK * (j + 1)), slice(128 * g, 128 * (g + 1))
                au, av, ag = (au_ref[rs, cs].astype(F32), av_ref[rs, cs].astype(F32), ag_ref[rs, cs].astype(F32))
                d = d_ref[rs, cs].astype(F32)
                _, (gu, dgu, dgv, rr, vhat, vn, mixed, sig, sl) = _sgu_chunk_fwd(au, av, ag, sg, ws_bf, bsb_g)
                dz_ref[rs, 128 * g:128 * (g + 1)] = (d * mixed * sl * dgu).astype(BF16)
                dz_ref[rs, 1024 + 128 * g:1024 + 128 * (g + 1)] = (
                    d * gu * mixed * (sig * (1.0 + ag * (1.0 - sig)))).astype(BF16)
                dmixed = d * gu * sl
                dmb = dmixed.astype(BF16)
                dm_lo = (dmixed - dmb.astype(F32)).astype(BF16)
                dbs_ref[g] += _nt(ones8_ref[...], dmb) + _nt(ones8_ref[...], dm_lo)
                dws_ref[g] += _nt(dmb, vn.astype(BF16))
                dvn = jnp.dot(wst_bf, dmb, preferred_element_type=F32)
                dsg_ref[:, 128 * g:128 * (g + 1)] += jnp.sum(dvn * vhat, axis=0, keepdims=True)
                dvhat = dvn * sg
                mean = jnp.mean(dvhat * vhat, axis=-1, keepdims=True)
                dz_ref[rs, 512 + 128 * g:512 + 128 * (g + 1)] = (rr * (dvhat - vhat * mean) * dgv).astype(BF16)

        dbq, dgq = _head_norm_bwd(bq_ref[...].astype(F32), dq_ref[...] * ATT_SCALE, gq_ref[...], ones_ref)
        dz_ref[:, 512 * 3:512 * 4] = dbq.astype(BF16)
        accq[...] += dgq
        dbk, dgk = _head_norm_bwd(bk_ref[...].astype(F32), dk_ref[...] * LN2, gk_ref[...], ones_ref)
        dz_ref[:, 512 * 4:512 * 5] = dbk.astype(BF16)
        acck[...] += dgk
        dz_ref[:, 512 * 5:512 * 6] = dv_ref[...].astype(BF16)
        dz_ref[:, 512 * 6:512 * 7] = dbg_ref[...]

        hv = h_ref[...]
        for k in range(N_BRANCH):
            acc[512 * k:512 * (k + 1), :] += _tn(dz_ref[:, 512 * k:512 * (k + 1)], hv)

        @pl.when(t == nt - 1)
        def _():
            dgq_ref[...] = jnp.dot(jnp.broadcast_to(accq[...], (8, 512)), fold_ref[...],
                                   preferred_element_type=F32, precision=HI)
            dgk_ref[...] = jnp.dot(jnp.broadcast_to(acck[...], (8, 512)), fold_ref[...],
                                   preferred_element_type=F32, precision=HI)
            cidx = lax.axis_index("c")
            sib = (lax.axis_index("x"), lax.axis_index("y"), 1 - cidx)
            swaps = []
            for q in range(N_DEV // 2):
                theirs = acc[pl.ds(pl.multiple_of(2 * blk * q + blk * (1 - cidx), 8), blk), :]
                send_buf[q] = theirs.astype(BF16)
                cp = pltpu.make_async_remote_copy(src_ref=send_buf.at[q], dst_ref=tmp.at[q], send_sem=s1.at[q],
                                                  recv_sem=r1.at[q], device_id=sib, device_id_type=MESH)
                cp.start()
                swaps.append(cp)
            for q in range(N_DEV // 2):
                swaps[q].wait_recv()
                mine = acc[pl.ds(pl.multiple_of(2 * blk * q + blk * cidx, 8), blk), :]
                stage[...] = (mine + tmp[q].astype(F32)).astype(BF16)
                out = pltpu.make_async_copy(stage, sums_out.at[q], sem)
                out.start()
                out.wait()
            for cp in swaps:
                cp.wait_send()
            for r, (ex, _) in enumerate(riders):
                ex.wait(rid_src[r], rid_dst[r], *rid_sems[3 * r:3 * r + 3])

    zcol = lambda col: pl.BlockSpec((tk, 512), lambda t: (t, col))
    row = pl.BlockSpec((tk, 512), lambda t: (t, 0))
    whole = lambda a: pl.BlockSpec(a.shape, lambda t: (0,) * a.ndim)
    res = pl.pallas_call(
        body, name="bwd_mid", grid=(nt,),
        in_specs=[zcol(0), zcol(1), zcol(2), zcol(3), zcol(4), row, row, row, row, row,
                  pl.BlockSpec((tk, D_MODEL), lambda t: (t, 0)), whole(hc), whole(dzc_k), whole(dzc_v),
                  whole(sgn), whole(ws), whole(wst), whole(bsb), whole(ones8), whole(ones_blk), whole(gq), whole(gk),
                  whole(foldm)] + [ANY_SPEC] * n_rid,
        out_specs=[pl.BlockSpec((tk, D_IN), lambda t: (t, 0)), ANY_SPEC,
                   pl.BlockSpec((SGU_GROUPS, CHUNK, CHUNK), lambda t: (0, 0, 0)),
                   pl.BlockSpec((SGU_GROUPS, 8, CHUNK), lambda t: (0, 0, 0)),
                   pl.BlockSpec((1, 512), lambda t: (0, 0)), pl.BlockSpec((8, 128), lambda t: (0, 0)),
                   pl.BlockSpec((8, 128), lambda t: (0, 0))] + [ANY_SPEC] * n_rid,
        out_shape=[jax.ShapeDtypeStruct((T, D_IN), BF16), jax.ShapeDtypeStruct((N_DEV // 2, blk, D_MODEL), BF16),
                   jax.ShapeDtypeStruct((SGU_GROUPS, CHUNK, CHUNK), F32),
                   jax.ShapeDtypeStruct((SGU_GROUPS, 8, CHUNK), F32), jax.ShapeDtypeStruct((1, 512), F32),
                   jax.ShapeDtypeStruct((8, 128), F32), jax.ShapeDtypeStruct((8, 128), F32)]
                  + [ex.out_shape for ex, _ in riders],
        scratch_shapes=[pltpu.VMEM((D_IN, D_MODEL), F32), pltpu.VMEM((1, 512), F32), pltpu.VMEM((1, 512), F32),
                        pltpu.VMEM((blk, D_MODEL), BF16), pltpu.SemaphoreType.DMA,
                        pltpu.VMEM((N_DEV // 2, blk, D_MODEL), BF16), pltpu.VMEM((N_DEV // 2, blk, D_MODEL), BF16),
                        pltpu.SemaphoreType.DMA((N_DEV // 2,)), pltpu.SemaphoreType.DMA((N_DEV // 2,))]
                       + [s for ex, _ in riders for s in ex.scratch],
        compiler_params=_params(("arbitrary",)),
    )(z, z, z, z, z, dmc, dqs, dk, dv, db_g, h, hc, dzc_k, dzc_v, sgn, ws, wst, bsb, ones8, ones_blk, gq, gk, foldm,
      *[a for _, a in riders])
    return res[:n_out], res[n_out:]


def _inproj_bwd_dx(dzs, w_row0, w_in_t, x, dy, ng, mod, scale_row, tm, name, riders=()):
    T = x.shape[0]
    n = len(dzs)
    wpc = dzs[0].shape[1]
    nt = T // tm
    with_dx = dy is not None
    n_own_in = n + 4 + with_dx
    n_own_out = 3 + with_dx
    n_rid = len(riders)

    def body(*refs):
        dz_refs = refs[:n]
        n_in = n_own_in + n_rid
        own = refs[n:n_own_in] + refs[n_in:n_in + n_own_out]
        rid_src = refs[n_own_in:n_own_in + n_rid]
        rid_dst = refs[n_in + n_own_out:n_in + n_own_out + n_rid]
        rid_sems = refs[n_in + n_own_out + n_rid:]
        if with_dx:
            w_ref, x_ref, dy_ref, g_ref, sc_ref, gx_ref, dsh_ref, dsc_ref, dg_ref = own
        else:
            w_ref, x_ref, g_ref, sc_ref, dsh_ref, dsc_ref, dg_ref = own

        @pl.when(pl.program_id(0) == 0)
        def _():
            for r, (ex, _) in enumerate(riders):
                ex.start(rid_src[r], rid_dst[r], *rid_sems[3 * r:3 * r + 3])
            dsh_ref[...] = jnp.zeros_like(dsh_ref)
            dsc_ref[...] = jnp.zeros_like(dsc_ref)
            dg_ref[...] = jnp.zeros_like(dg_ref)

        dh = jnp.dot(dz_refs[0][...], w_ref[0:wpc, :], preferred_element_type=F32)
        for k in range(1, n):
            dh = dh + jnp.dot(dz_refs[k][...], w_ref[wpc * k:wpc * (k + 1), :], preferred_element_type=F32)
        xv = x_ref[...]
        r = lax.rsqrt(jnp.mean(xv * xv, axis=-1, keepdims=True) + EPS)
        xn = xv * r
        gv, op = g_ref[...], 1.0 + sc_ref[0]
        dsh_ref[...] += jnp.sum(dh, axis=0, keepdims=True)
        dsc_ref[...] += jnp.sum(dh * xn * gv, axis=0, keepdims=True)
        dg_ref[...] += jnp.sum(dh * op * xn, axis=0, keepdims=True)
        if with_dx:
            dxn = dh * (gv * op)
            gx_ref[...] = r * (dxn - xn * jnp.mean(dxn * xn, axis=-1, keepdims=True)) + dy_ref[...]

        @pl.when(pl.program_id(0) == nt - 1)
        def _():
            for r, (ex, _) in enumerate(riders):
                ex.wait(rid_src[r], rid_dst[r], *rid_sems[3 * r:3 * r + 3])

    vec = pl.BlockSpec((1, D_MODEL), lambda i: (0, 0))
    rowf = pl.BlockSpec((tm, D_MODEL), lambda i: (i, 0))
    in_specs = [pl.BlockSpec((tm, wpc), lambda i: (i, 0))] * n
    in_specs += [pl.BlockSpec((wpc * n, D_MODEL), lambda i: (w_row0 // (wpc * n), 0)), rowf]
    args = list(dzs) + [w_in_t, x]
    vshape = jax.ShapeDtypeStruct((1, D_MODEL), F32)
    out_specs, out_shape = [vec, vec, vec], [vshape, vshape, vshape]
    if with_dx:
        in_specs.append(rowf)
        args.append(dy)
        out_specs = [rowf] + out_specs
        out_shape = [jax.ShapeDtypeStruct((T, D_MODEL), F32)] + out_shape
    in_specs += [vec, _mod_row(scale_row)] + [ANY_SPEC] * n_rid
    args += [ng, mod] + [a for _, a in riders]
    res = pl.pallas_call(
        body, name=name, grid=(nt,), in_specs=in_specs, out_specs=out_specs + [ANY_SPEC] * n_rid,
        out_shape=out_shape + [ex.out_shape for ex, _ in riders],
        scratch_shapes=[s for ex, _ in riders for s in ex.scratch],
        compiler_params=_params(("arbitrary",)),
    )(*args)
    return res[:n_own_out], res[n_own_out:]


def _adamw_sharded(w, gparts, m, v, tr, name, riders=()):
    R, C = w.shape
    n_part = gparts.shape[0]
    nt = R // tr
    n_rid = len(riders)

    def body(w_ref, gp_ref, m_ref, v_ref, *rest):
        rid_src = rest[:n_rid]
        g_ref, d_ref, m2_ref, v2_ref = rest[n_rid:n_rid + 4]
        rid_dst = rest[n_rid + 4:2 * n_rid + 4]
        rid_sems = rest[2 * n_rid + 4:]

        @pl.when(pl.program_id(0) == 0)
        def _():
            for r, (ex, _) in enumerate(riders):
                ex.start(rid_src[r], rid_dst[r], *rid_sems[3 * r:3 * r + 3])

        g = gp_ref[0].astype(F32)
        for d in range(1, n_part):
            g = g + gp_ref[d].astype(F32)
        delta, m2, v2 = _adam(w_ref[...], g, m_ref[...], v_ref[...])
        g_ref[...] = g
        d_ref[...] = delta
        m2_ref[...] = m2
        v2_ref[...] = v2

        @pl.when(pl.program_id(0) == nt - 1)
        def _():
            for r, (ex, _) in enumerate(riders):
                ex.wait(rid_src[r], rid_dst[r], *rid_sems[3 * r:3 * r + 3])

    row = pl.BlockSpec((tr, C), lambda i: (i, 0))
    sh = jax.ShapeDtypeStruct((R, C), F32)
    res = pl.pallas_call(
        body, name=name, grid=(nt,),
        in_specs=[row, pl.BlockSpec((n_part, tr, C), lambda i: (0, i, 0)), row, row] + [ANY_SPEC] * n_rid,
        out_specs=[row, row, row, row] + [ANY_SPEC] * n_rid,
        out_shape=[sh, sh, sh, sh] + [ex.out_shape for ex, _ in riders],
        scratch_shapes=[s for ex, _ in riders for s in ex.scratch],
        compiler_params=_params(("arbitrary",)),
    )(w, gparts, m, v, *[a for _, a in riders])
    return res[:4], res[4:]


def _pack_vectors(vec_rows, dsg, dgq, dgk, dgk_c, loss_part):
    flat = [a for _, arrs in vec_rows for a in arrs]

    def body(*refs):
        vecs = list(refs[:len(flat)])
        dsg_ref, dgq_ref, dgk_ref, dgkc_ref, loss_ref, v_ref = refs[len(flat):]
        row = lax.broadcasted_iota(jnp.int32, (16, D_MODEL), 0)
        misc = jnp.concatenate([dsg_ref[...], dgq_ref[0:1, :], dgk_ref[0:1, :], dgkc_ref[0:1, :],
                                loss_ref[...]], axis=1)
        v = jnp.where(row == V_MISC, jnp.broadcast_to(misc, (16, D_MODEL)), 0.0)
        for r, arrs in vec_rows:
            val = vecs.pop(0)[...]
            for _ in arrs[1:]:
                val = val + vecs.pop(0)[...]
            v = jnp.where(row == r, jnp.broadcast_to(val, (16, D_MODEL)), v)
        v_ref[...] = v

    return pl.pallas_call(
        body, name="pack_vectors", out_shape=jax.ShapeDtypeStruct((16, D_MODEL), F32), compiler_params=_params(),
    )(*flat, dsg, dgq, dgk, dgk_c, loss_part)


SMALL_NAMES = ("b_ada", "norm_g", "sgu_norm_g", "w_spatial", "b_spatial", "q_norm_g", "k_norm_g", "rpb")


def _adamw_small(vg, sg, rg, ws, ms, vs):
    k = len(SMALL_NAMES)

    def body(*refs):
        vg_ref, sg_ref, rg_ref = refs[0], refs[1], refs[2]
        refs = refs[1:]
        w_refs = dict(zip(SMALL_NAMES, refs[2:2 + k]))
        m_refs = dict(zip(SMALL_NAMES, refs[2 + k:2 + 2 * k]))
        v_refs = dict(zip(SMALL_NAMES, refs[2 + 2 * k:2 + 3 * k]))
        o_refs = [dict(zip(SMALL_NAMES, refs[2 + (3 + i) * k:2 + (4 + i) * k])) for i in range(4)]
        loss_ref = refs[2 + 7 * k]

        sv = vg_ref[0]
        for d in range(1, N_DEV):
            sv = sv + vg_ref[d]
        loss_ref[...] = sv[V_MISC:V_MISC + 1, 896:1024]

        def total(lo, hi, ref=sg_ref):
            s = ref[0, lo:hi, :].astype(F32)
            for d in range(1, N_DEV):
                s = s + ref[d, lo:hi, :].astype(F32)
            return s

        def emit(name, idx, g):
            res = _adam(w_refs[name][idx], g, m_refs[name][idx], v_refs[name][idx])
            for o, val in zip(o_refs, (g,) + res):
                o[name][idx] = val

        everything = (slice(None), slice(None))
        row = lambda r: sv[r:r + 1, :]
        emit("b_ada", everything, jnp.concatenate(
            [row(V_DSHIFT) + row(V_DCSHIFT), row(V_DSCALE) + row(V_DCSCALE), row(V_DGATE)], axis=1))
        emit("norm_g", everything, row(V_DNG) + row(V_DNG_CTX))
        misc = row(V_MISC)
        emit("sgu_norm_g", everything, misc[:, 0:512])
        emit("q_norm_g", everything, misc[:, 512:512 + HEAD_DIM])
        emit("k_norm_g", everything, misc[:, 640:640 + HEAD_DIM] + misc[:, 768:768 + HEAD_DIM])
        for g in range(SGU_GROUPS):
            emit("w_spatial", (0, g), total(128 * g, 128 * (g + 1)))
            emit("b_spatial", (0, slice(g, g + 1), slice(None)), total(M_DBS + 8 * g, M_DBS + 8 * (g + 1))[0:1, :])
        for hd in range(N_HEADS):
            by_dc = total(32 * hd, 32 * (hd + 1), rg_ref)
            emit("rpb", (0, hd), by_dc.T[0:2 * WIN_R - 1, 0:2 * WIN_C - 1])

    shapes = [jax.ShapeDtypeStruct(w.shape, F32) for w in ws]
    res = pl.pallas_call(body, name="adamw_small", out_shape=shapes * 4 + [jax.ShapeDtypeStruct((1, 128), F32)],
                         compiler_params=_params())(vg, sg, rg, *ws, *ms, *vs)
    return [res[i * k:(i + 1) * k] for i in range(4)], res[4 * k]


def _adamw_cctx(pc_g, w, m, v):
    def body(pc_ref, w_ref, m_ref, v_ref, g_ref, d_ref, m2_ref, v2_ref):
        pc = pc_ref[0, 0:1, :]
        for d in range(1, N_DEV):
            pc = pc + pc_ref[d, 0:1, :]
        cc = w_ref[...]
        sig = _sigmoid(cc)
        g = pc * (sig * (1.0 + cc * (1.0 - sig)))
        delta, m2, v2 = _adam(cc, g, m_ref[...], v_ref[...])
        g_ref[...] = g
        d_ref[...] = delta
        m2_ref[...] = m2
        v2_ref[...] = v2

    sh = jax.ShapeDtypeStruct((1, D_MODEL), F32)
    return pl.pallas_call(body, name="adamw_cctx", out_shape=[sh, sh, sh, sh], compiler_params=_params())(
        pc_g, w, m, v)


def _block_ones(n, blk):
    i = np.arange(n)
    return jnp.asarray((i[:, None] // blk == i[None, :] // blk).astype(np.float32), BF16)


def _rpb_pairs(rpb):
    n_off = 2 * WIN_C - 1
    cols = np.arange(GRID_W)
    c0 = np.clip(cols - WIN_C // 2, 0, GRID_W - WIN_C)
    in_win = (cols[None, :] >= c0[:, None]) & (cols[None, :] < c0[:, None] + WIN_C)
    dc = np.clip(cols[None, :] - cols[:, None] + (WIN_C - 1), 0, n_off - 1)
    expand = (dc[None] == np.arange(n_off)[:, None, None]) & in_win[None]
    toep = jnp.einsum("hrd,dqk->hrqk", rpb, jnp.asarray(expand, F32), precision=HI)
    toep = toep + jnp.asarray(np.where(in_win, 0.0, NEG_INF).astype(np.float32))
    neg = jnp.full((N_HEADS, 1, GRID_W, GRID_W), NEG_INF, F32)
    ext = jnp.concatenate([neg, toep, neg], axis=1)
    return jnp.concatenate([ext[:, :-1], ext[:, 1:]], axis=-1)


def _row_mask(rows):
    nrb = rows // Q_ROWS
    valid = np.zeros((3, Q_ROWS, 1, K_ROWS, 1), bool)
    for t, rb in enumerate((0, 1, nrb - 1)):
        kb = int(np.clip(Q_ROWS * rb - 4, 0, rows - K_ROWS))
        for i in range(Q_ROWS):
            r0 = int(np.clip(Q_ROWS * rb + i - WIN_R // 2, 0, rows - WIN_R))
            for j in range(K_ROWS):
                valid[t, i, 0, j, 0] = r0 <= kb + j < r0 + WIN_R
    full = np.broadcast_to(valid, (3, Q_ROWS, GRID_W, K_ROWS, GRID_W)).reshape(3, TQ, TK)
    return jnp.asarray(np.where(full, 0.0, NEG_INF).astype(np.float32))


def kernel(x, c, ctx, c_ctx, w_ada, b_ada, norm_g, w_in, sgu_norm_g, w_spatial, b_spatial, q_norm_g, k_norm_g, rpb, w_out, loss_target, m_c_ctx, m_w_ada, m_b_ada, m_norm_g, m_w_in, m_sgu_norm_g, m_w_spatial, m_b_spatial, m_q_norm_g, m_k_norm_g, m_rpb, m_w_out, v_c_ctx, v_w_ada, v_b_ada, v_norm_g, v_w_in, v_sgu_norm_g, v_w_spatial, v_b_spatial, v_q_norm_g, v_k_norm_g, v_rpb, v_w_out):
    me = 4 * lax.axis_index("x") + 2 * lax.axis_index("y") + lax.axis_index("c")
    x2, ctx2, tgt2 = x[0], ctx[0], loss_target[0]
    T, C = x2.shape[0], ctx2.shape[0]
    rows = T // GRID_W
    wada, win_t, wout = w_ada[0], w_in[0].T, w_out[0]
    ada_w = wada.shape[1]
    win_w = win_t.shape[0]

    row8 = lax.broadcasted_iota(jnp.int32, (8, D_MODEL), 0)
    c_blk = jnp.where(row8 == me, jnp.broadcast_to(c, (8, D_MODEL)), 0.0)
    b_sh = lax.dynamic_slice(b_ada, (0, me * ada_w), (1, ada_w))
    c_ctx_row = c_ctx.reshape(1, D_MODEL)

    ones512 = _block_ones(512, HEAD_DIM)
    ones8 = jnp.ones((8, 128), BF16)
    foldm = jnp.asarray((np.arange(512)[:, None] % HEAD_DIM == np.arange(128)[None, :]).astype(np.float32))
    lane_half = np.arange(128)[None, :, None] // GRID_W
    hsel = jnp.asarray((2 * np.arange(8)[:, None, None] + lane_half == np.arange(128)[None, None, :]).astype(np.float32),
                       BF16)
    foldr = jnp.asarray((np.arange(256)[None, :] // 8 == np.arange(32)[:, None]).astype(np.float32), BF16)
    gq512 = jnp.tile(q_norm_g, (1, N_HEADS))
    gk512 = jnp.tile(k_norm_g, (1, N_HEADS))
    ws = w_spatial[0]
    wst = ws.transpose(0, 2, 1)
    bsb = jnp.broadcast_to(b_spatial[0][:, :, None], (SGU_GROUPS, CHUNK, 128))
    pairs = _rpb_pairs(rpb[0])
    row_mask = _row_mask(rows)

    my_chip = me // 2
    order = jnp.stack([my_chip, my_chip ^ 2, my_chip ^ 1, my_chip ^ 3]).astype(jnp.int32)
    h, z, win_g, s16, mod = _inproj_fwd(order, x2, norm_g, c_blk, c_ctx_row, wada, b_sh, win_t.astype(BF16), 1024)
    w_in_b = win_g.reshape(D_IN, D_MODEL)
    out_a, qs, kn = _sgu_qk_fwd(z, sgu_norm_g, ws, bsb, ones512, gq512, gk512, 512)
    hc, zc, ckn = _ctx_fwd(ctx2, norm_g, mod, w_in_b, ones512, gk512)
    wout_blk = wout.astype(BF16)
    (ob, out_b, lse), (wout_g,) = _attn_fwd(qs, kn, z, ckn, zc, pairs, row_mask, [(_Hosted("ag", wout_blk), wout_blk)])
    w_out_b = wout_g.reshape(D_MODEL, D_MODEL)

    dy, dmc, dw_out, dgate, loss_part = _outproj_loss_bwd(x2, tgt2, out_a, out_b, mod, w_out_b, 1024)
    dw_out_blocks = dw_out.reshape(N_DEV, D_MODEL // N_DEV, D_MODEL)
    (dqs, dk, dv, dck, dcv, db_g, drpb), (gout_parts,) = _attn_bwd(
        qs, kn, z, ckn, zc, pairs, row_mask, ob, lse, dmc, hsel, foldr,
        [(_Hosted("a2a", dw_out_blocks), dw_out_blocks)])
    dzc_k, dzc_v, dgk_c = _ctx_k_bwd(zc, dck, dcv, ones512, gk512, foldm)
    rloc = drpb.reshape(N_HEADS * 32, 128)
    (dz, chip_sums, dws, dbs, dsg, dgq, dgk), (rg,) = _bwd_mid(
        z, dmc, dqs, dk, dv, db_g, h, hc, dzc_k, dzc_v, sgu_norm_g, ws, wst, bsb, ones8, ones512, gq512, gk512, foldm,
        512, [(_Hosted("ag", rloc), rloc)])
    sloc = jnp.concatenate([dws.reshape(SGU_GROUPS * CHUNK, CHUNK), dbs.reshape(SGU_GROUPS * 8, CHUNK)]).astype(BF16)
    (grad_x, dshift, dscale, dng), (gin_parts,) = _inproj_bwd_dx(
        [dz], 0, w_in_b, x2, dy, norm_g, mod, MOD_SCALE, 512, "inproj_bwd_dx",
        [(_Hosted("chips", chip_sums), chip_sums)])
    (dcshift, dcscale, dng_c), _ = _inproj_bwd_dx([dzc_k, dzc_v], 4 * 512, w_in_b, ctx2, None, norm_g, mod,
                                                  MOD_CSCALE, C, "ctx_bwd_dx")

    res_in, _ = _adamw_sharded(win_t, gin_parts, m_w_in[0].T, v_w_in[0].T, 224, "adamw_w_in")
    res_out, _ = _adamw_sharded(wout, gout_parts, m_w_out[0], v_w_out[0], 128, "adamw_w_out")

    zero_row = jnp.zeros((1, D_MODEL), F32)
    vec_rows = [(V_DSHIFT, [dshift]), (V_DSCALE, [dscale]), (V_DGATE, [dgate]), (V_DCSHIFT, [dcshift]),
                (V_DCSCALE, [dcscale]), (V_ZERO, [zero_row]), (V_DNG, [dng]), (V_DNG_CTX, [dng_c])]
    vloc = _pack_vectors(vec_rows, dsg, dgq, dgk, dgk_c, loss_part)
    vg, sg = _allgather_direct([vloc, sloc], "gather_small")
    small_w =(b_ada, norm_g, sgu_norm_g, w_spatial, b_spatial, q_norm_g, k_norm_g, rpb)
    small_m = (m_b_ada, m_norm_g, m_sgu_norm_g, m_w_spatial, m_b_spatial, m_q_norm_g, m_k_norm_g, m_rpb)
    small_v = (v_b_ada, v_norm_g, v_sgu_norm_g, v_w_spatial, v_b_spatial, v_q_norm_g, v_k_norm_g, v_rpb)
    res_small, loss_row = _adamw_small(vg, sg, rg, small_w, small_m, small_v)

    dm_all = vg[:, V_DSHIFT:V_DGATE + 1, :].reshape(N_DEV, 3 * D_MODEL)
    dc_all = vg[:, V_DCSHIFT:V_ZERO + 1, :].reshape(N_DEV, 3 * D_MODEL)
    dm_sh = lax.dynamic_slice(dm_all, (0, me * ada_w), (N_DEV, ada_w))
    dc_sh = lax.dynamic_slice(dc_all, (0, me * ada_w), (N_DEV, ada_w))
    *res_ada, pc = _ada_bwd(s16, dm_sh, dc_sh, wada, m_w_ada[0], v_w_ada[0])
    (pc_g,) = _allgather_direct([pc], "gather_cctx")
    res_cctx = _adamw_cctx(pc_g, c_ctx_row, m_c_ctx.reshape(1, D_MODEL), v_c_ctx.reshape(1, D_MODEL))

    loss = loss_row[0, 0]
    outs = [loss, grad_x[None]]
    for kind in range(4):
        by_name = dict(zip(SMALL_NAMES, res_small[kind]))
        by_name.update(c_ctx=res_cctx[kind].reshape(D_MODEL), w_ada=res_ada[kind][None],
                       w_in=res_in[kind].T[None], w_out=res_out[kind][None])
        outs += [by_name[nme] for nme in ("c_ctx", "w_ada", "b_ada", "norm_g", "w_in", "sgu_norm_g", "w_spatial",
                                          "b_spatial", "q_norm_g", "k_norm_g", "rpb", "w_out")]
    return tuple(outs)
```

```python
import functools

import numpy as np
import jax
import jax.numpy as jnp
from jax import lax
from jax.experimental import pallas as pl
from jax.experimental.pallas import tpu as pltpu

F32 = jnp.float32
BF16 = jnp.bfloat16
HI = lax.Precision.HIGHEST

N_DEV = 8
D_MODEL = 1024
D_A = 512
D_B = 512
D_IN = 3584
N_BRANCH = 7
HEAD_DIM = 64
N_HEADS = 8
GRID_W = 64
WIN_R = 8
WIN_C = 16
CHUNK = 128
SGU_GROUPS = 4
EPS = 1e-6
NEG_INF = -1e30
Q_ROWS = 4
K_ROWS = 12
TQ = Q_ROWS * GRID_W
TK = K_ROWS * GRID_W
N_DIAG = 22
ATT_SUB = 8
ATT_SCALE = HEAD_DIM ** -0.5
LOG2E = 1.4426950408889634
LN2 = 0.6931471805599453

ADAM_LR = 0.001
ADAM_B1 = 0.9
ADAM_B2 = 0.999
ADAM_EPS = 1e-08
ADAM_WD = 0.01
ADAM_STEP = 10

VMEM_LIMIT = 56 * 1024 * 1024
MESH = pl.DeviceIdType.MESH

V_DSHIFT, V_DSCALE, V_DGATE, V_DCSHIFT, V_DCSCALE, V_ZERO, V_DNG, V_DNG_CTX, V_MISC = range(9)
M_DBS = 512
MOD_SHIFT, MOD_SCALE, MOD_GATE, MOD_CSHIFT, MOD_CSCALE = range(5)


def _params(sem=None):
    return pltpu.CompilerParams(dimension_semantics=sem, vmem_limit_bytes=VMEM_LIMIT)


def _sigmoid(x):
    return 1.0 / (1.0 + jnp.exp(-x))


def _gelu_parts(x):
    cdf = 0.5 * (1.0 + lax.erf(x * 0.7071067811865476))
    pdf = jnp.exp(-0.5 * x * x) * 0.3989422804014327
    return x * cdf, cdf + x * pdf


def _nt(a, b):
    return lax.dot_general(a, b, (((1,), (1,)), ((), ())), preferred_element_type=F32)


def _tn(a, b):
    return lax.dot_general(a, b, (((0,), (0,)), ((), ())), preferred_element_type=F32)


def _dot2(v, ones_bf):
    hi = v.astype(BF16)
    lo = (v - hi.astype(F32)).astype(BF16)
    return (jnp.dot(hi, ones_bf, preferred_element_type=F32)
            + jnp.dot(lo, ones_bf, preferred_element_type=F32))


def _head_sum(v, ones_ref):
    return jnp.dot(v.astype(BF16), ones_ref[...], preferred_element_type=F32)


def _adam(w, g, m, v):
    m2 = ADAM_B1 * m + (1.0 - ADAM_B1) * g
    v2 = ADAM_B2 * v + (1.0 - ADAM_B2) * (g * g)
    m_hat = m2 / (1.0 - ADAM_B1 ** ADAM_STEP)
    v_hat = v2 / (1.0 - ADAM_B2 ** ADAM_STEP)
    delta = -ADAM_LR * (m_hat / (jnp.sqrt(v_hat) + ADAM_EPS) + ADAM_WD * w)
    return delta, m2, v2


class _Hosted:
    def __init__(self, kind, src):
        self.kind = kind
        n_slot = {"a2a": N_DEV, "ag": N_DEV, "chips": N_DEV // 2}[kind]
        blk = src.shape if kind == "ag" else src.shape[1:]
        self.out_shape = jax.ShapeDtypeStruct((n_slot,) + tuple(blk), src.dtype)
        self.n_peer = n_slot - 1
        self.scratch = [pltpu.SemaphoreType.DMA((self.n_peer,)), pltpu.SemaphoreType.DMA((self.n_peer,)),
                        pltpu.SemaphoreType.DMA]

    def _copies(self, src, dst, send_sems, recv_sems, loc_sem, landing):
        x, y, c = lax.axis_index("x"), lax.axis_index("y"), lax.axis_index("c")
        if self.kind == "chips":
            me = 2 * x + y
            peers = [((px, py, c), 2 * px + py) for px, py in ((1 - x, 1 - y), (1 - x, y), (x, 1 - y))]
        else:
            me = 4 * x + 2 * y + c
            peers = []
            for k in (6, 7, 4, 5, 2, 3, 1):
                px = 1 - x if (k >> 2) & 1 else x
                py = 1 - y if (k >> 1) & 1 else y
                pc = 1 - c if k & 1 else c
                peers.append(((px, py, pc), 4 * px + 2 * py + pc))
        remote = []
        for k, (peer, pid) in enumerate(peers):
            s = src if self.kind == "ag" else src.at[pid]
            remote.append(pltpu.make_async_remote_copy(
                src_ref=s, dst_ref=dst.at[pid if landing else me],
                send_sem=send_sems.at[k], recv_sem=recv_sems.at[k], device_id=peer, device_id_type=MESH))
        local = pltpu.make_async_copy(src if self.kind == "ag" else src.at[me], dst.at[me], loc_sem)
        return remote, local

    def start(self, src, dst, send_sems, recv_sems, loc_sem):
        remote, local = self._copies(src, dst, send_sems, recv_sems, loc_sem, landing=False)
        for cp in remote:
            cp.start()
        local.start()

    def wait(self, src, dst, send_sems, recv_sems, loc_sem):
        remote, local = self._copies(src, dst, send_sems, recv_sems, loc_sem, landing=True)
        for cp in remote:
            cp.wait_recv()
        for cp in remote:
            cp.wait_send()
        local.wait()


ANY_SPEC = pl.BlockSpec(memory_space=pl.ANY)


def _allgather_direct(arrs, name):
    n = len(arrs)
    exs = [_Hosted("ag", a) for a in arrs]

    def body(*refs):
        srcs, dsts, sems = refs[:n], refs[n:2 * n], refs[2 * n:]
        for r, ex in enumerate(exs):
            ex.start(srcs[r], dsts[r], *sems[3 * r:3 * r + 3])
        for r, ex in enumerate(exs):
            ex.wait(srcs[r], dsts[r], *sems[3 * r:3 * r + 3])

    return pl.pallas_call(body, name=name, out_shape=[ex.out_shape for ex in exs], in_specs=[ANY_SPEC] * n,
                          out_specs=[ANY_SPEC] * n, scratch_shapes=[s for ex in exs for s in ex.scratch])(*arrs)


def _ada_scratch(n_col):
    return ([pltpu.VMEM((N_DEV, 8, D_MODEL), F32), pltpu.VMEM((16, n_col), F32), pltpu.VMEM((N_DEV, 16, n_col), F32)]
            + [pltpu.SemaphoreType.DMA((N_DEV - 1,)) for _ in range(4)])


def _ada_modulation(cb_ref, cc_ref, w_ref, b_ref, cstack, part, parts, s1, r1, s2, r2):
    x, y, c = lax.axis_index("x"), lax.axis_index("y"), lax.axis_index("c")
    me = 4 * x + 2 * y + c
    peers = []
    for k in range(1, N_DEV):
        px = 1 - x if (k >> 2) & 1 else x
        py = 1 - y if (k >> 1) & 1 else y
        pc = 1 - c if k & 1 else c
        peers.append(((px, py, pc), 4 * px + 2 * py + pc))

    def exchange(src, dst, send_sems, recv_sems):
        for k, (peer, _) in enumerate(peers):
            pltpu.make_async_remote_copy(src_ref=src, dst_ref=dst.at[me], send_sem=send_sems.at[k],
                                         recv_sem=recv_sems.at[k], device_id=peer, device_id_type=MESH).start()
        dst[me] = src[...]
        waits = [pltpu.make_async_remote_copy(src_ref=src, dst_ref=dst.at[pid], send_sem=send_sems.at[k],
                                              recv_sem=recv_sems.at[k], device_id=peer, device_id_type=MESH)
                 for k, (peer, pid) in enumerate(peers)]
        for cp in waits:
            cp.wait_recv()
        for cp in waits:
            cp.wait_send()

    exchange(cb_ref, cstack, s1, r1)
    c_all = cstack[0]
    for d in range(1, N_DEV):
        c_all = c_all + cstack[d]
    row = lax.broadcasted_iota(jnp.int32, (8, D_MODEL), 0)
    cc = jnp.where(row == 0, jnp.broadcast_to(cc_ref[...], (8, D_MODEL)), 0.0)
    call = jnp.concatenate([c_all, cc], axis=0)
    s = call * _sigmoid(call)
    part[...] = jnp.dot(s, w_ref[...], preferred_element_type=F32, precision=HI) + b_ref[...]
    exchange(part, parts, s2, r2)
    return s


def _ada_bwd(s16, dm, dc, w, m, v):
    def body(s_ref, dm_ref, dc_ref, w_ref, m_ref, v_ref, g_ref, d_ref, m2_ref, v2_ref, pc_ref):
        dct = jnp.sum(dc_ref[...], axis=0, keepdims=True)
        row = lax.broadcasted_iota(jnp.int32, dc_ref.shape, 0)
        dcb = jnp.where(row == 0, jnp.broadcast_to(dct, dc_ref.shape), 0.0)
        dm16 = jnp.concatenate([dm_ref[...], dcb], axis=0)
        g = lax.dot_general(s_ref[...], dm16, (((0,), (0,)), ((), ())),
                            preferred_element_type=F32, precision=HI)
        w_ = w_ref[...]
        delta, m2, v2 = _adam(w_, g, m_ref[...], v_ref[...])
        g_ref[...] = g
        d_ref[...] = delta
        m2_ref[...] = m2
        v2_ref[...] = v2
        pc_ref[...] = lax.dot_general(dcb, w_, (((1,), (1,)), ((), ())),
                                      preferred_element_type=F32, precision=HI)

    R, n_col = w.shape
    tr = 256
    sh = jax.ShapeDtypeStruct(w.shape, F32)
    rows = pl.BlockSpec((tr, n_col), lambda i: (i, 0))
    small = pl.BlockSpec(dm.shape, lambda i: (0, 0))
    return pl.pallas_call(
        body, name="ada_bwd", grid=(R // tr,),
        in_specs=[pl.BlockSpec((16, tr), lambda i: (0, i)), small, small, rows, rows, rows],
        out_specs=[rows, rows, rows, rows, pl.BlockSpec((8, tr), lambda i: (0, i))],
        out_shape=[sh, sh, sh, sh, jax.ShapeDtypeStruct((8, D_MODEL), F32)],
        compiler_params=_params(("arbitrary",)),
    )(s16, dm, dc, w, m, v)


def _head_norm(zk, ones_ref, gain):
    ss = _head_sum(zk * zk, ones_ref)
    return zk * lax.rsqrt(ss * (1.0 / HEAD_DIM) + EPS) * gain


def _inproj_fwd(order, x, ng, c_blk, c_ctx_row, w_ada_sh, b_ada_sh, w_blk_t, tm):
    T = x.shape[0]
    nt = T // tm
    n_pass = N_DEV // 2
    blk_rows = w_blk_t.shape[0]
    n_col = w_ada_sh.shape[1]

    def body(order_ref, x_ref, g_ref, cb_ref, cc_ref, wa_ref, ba_ref, wb_ref,
             h_out, z_ref, wt_out, s_out, mod_out,
             hs, wt, modv, send_sems, recv_sems, loc_sem, h_sem, wt_sem, *ada_sc):
        p, i = pl.program_id(0), pl.program_id(1)
        x, y, c = lax.axis_index("x"), lax.axis_index("y"), lax.axis_index("c")
        me, sib = (x, y, c), (x, y, 1 - c)
        chips = [(1 - x, y), (x, 1 - y), (1 - x, 1 - y)]

        def slot(px, py, pc):
            return 4 * px + 2 * py + pc

        def copy(k, block, to, src=None):
            return pltpu.make_async_remote_copy(
                src_ref=wt.at[slot(*block)] if src is None else src, dst_ref=wt.at[slot(*block)],
                send_sem=send_sems.at[k], recv_sem=recv_sems.at[k], device_id=to, device_id_type=MESH)

        own = pltpu.make_async_copy(wb_ref, wt.at[slot(*me)], loc_sem)
        h_copy = pltpu.make_async_copy(hs, h_out, h_sem)
        wt_copy = pltpu.make_async_copy(wt, wt_out, wt_sem)
        first = [copy(1 + j, me, (*chip, c), src=wb_ref) for j, chip in enumerate(chips[:2])] + [copy(0, me, sib, src=wb_ref)]
        passed = [copy(4 + j, (*chip, c), sib) for j, chip in enumerate(chips)]
        relay_src = (jnp.where(c == 0, 1 - x, x), jnp.where(c == 0, y, 1 - y), c)
        relay_dst = (jnp.where(c == 0, x, 1 - x), jnp.where(c == 0, 1 - y, y), c)
        relay = copy(3, relay_src, relay_dst)

        @pl.when(jnp.logical_and(p == 0, i == 0))
        def _():
            s_out[...] = _ada_modulation(cb_ref, cc_ref, wa_ref, ba_ref, *ada_sc)
            own.start()
            for cp in first:
                cp.start()
            parts = ada_sc[2]
            my_row = pl.ds(slot(*me), 1)
            mod = jnp.concatenate([parts[d, my_row, :] for d in range(N_DEV)], axis=1)
            cmod = jnp.concatenate([parts[d, 8:9, :] for d in range(N_DEV)], axis=1)
            modv[...] = jnp.zeros_like(modv)
            for r, val in enumerate((mod[:, 0:D_MODEL], mod[:, D_MODEL:2 * D_MODEL], mod[:, 2 * D_MODEL:],
                                     cmod[:, 0:D_MODEL], cmod[:, D_MODEL:2 * D_MODEL])):
                modv[r:r + 1, :] = val
                mod_out[r] = val
            for r in range(5, 8):
                mod_out[r] = jnp.zeros((1, D_MODEL), F32)
            own.wait()
            copy(0, sib, me).wait_recv()

        @pl.when(jnp.logical_and(p == 1, i == 0))
        def _():
            for j, chip in enumerate(chips[:2]):
                copy(1 + j, (*chip, c), me).wait_recv()
            relay.start()
            passed[0].start()
            passed[1].start()
            copy(4, (*chips[0], 1 - c), me).wait_recv()

        @pl.when(jnp.logical_and(p == 2, i == 0))
        def _():
            copy(5, (*chips[1], 1 - c), me).wait_recv()

        @pl.when(jnp.logical_and(p == 3, i == 0))
        def _():
            copy(3, (*chips[2], c), me).wait_recv()
            passed[2].start()
            copy(6, (*chips[2], 1 - c), me).wait_recv()
            wt_copy.start()

        rows = pl.ds(pl.multiple_of(i * tm, tm), tm)

        @pl.when(p == 0)
        def _():
            xv = x_ref[...]
            r = lax.rsqrt(jnp.mean(xv * xv, axis=-1, keepdims=True) + EPS)
            hs[rows, :] = ((xv * r * g_ref[...]) * (1.0 + modv[1:2, :]) + modv[0:1, :]).astype(BF16)

        @pl.when(jnp.logical_and(p == 1, i == 0))
        def _():
            h_copy.start()

        w_pair = wt[pl.ds(2 * order_ref[p], 2)].reshape(2 * blk_rows, D_MODEL)
        z_ref[...] = _nt(hs[rows, :], w_pair).astype(BF16)

        @pl.when(jnp.logical_and(p == n_pass - 1, i == nt - 1))
        def _():
            for cp in first + passed + [relay]:
                cp.wait_send()
            h_copy.wait()
            wt_copy.wait()

    whole = lambda shape: pl.BlockSpec(shape, lambda p, i, o: (0,) * len(shape))
    grid_spec = pltpu.PrefetchScalarGridSpec(
        num_scalar_prefetch=1, grid=(n_pass, nt),
        in_specs=[pl.BlockSpec((tm, D_MODEL), lambda p, i, o: (jnp.where(p == 0, i, nt - 1), 0)),
                  whole((1, D_MODEL)), whole((8, D_MODEL)), whole((1, D_MODEL)), whole((D_MODEL, n_col)),
                  whole((1, n_col)), ANY_SPEC],
        out_specs=[ANY_SPEC, pl.BlockSpec((tm, 2 * blk_rows), lambda p, i, o: (i, o[p])), ANY_SPEC,
                   whole((16, D_MODEL)), whole((8, 1, D_MODEL))],
        scratch_shapes=[pltpu.VMEM((T, D_MODEL), BF16), pltpu.VMEM((N_DEV, blk_rows, D_MODEL), BF16),
                        pltpu.VMEM((8, D_MODEL), F32),
                        pltpu.SemaphoreType.DMA((7,)), pltpu.SemaphoreType.DMA((7,)), pltpu.SemaphoreType.DMA,
                        pltpu.SemaphoreType.DMA, pltpu.SemaphoreType.DMA] + _ada_scratch(n_col))
    return pl.pallas_call(
        body, name="inproj_fwd", grid_spec=grid_spec,
        out_shape=[jax.ShapeDtypeStruct((T, D_MODEL), BF16), jax.ShapeDtypeStruct((T, D_IN), BF16),
                   jax.ShapeDtypeStruct((N_DEV, blk_rows, D_MODEL), BF16),
                   jax.ShapeDtypeStruct((16, D_MODEL), F32), jax.ShapeDtypeStruct((8, 1, D_MODEL), F32)],
        compiler_params=_params(("arbitrary", "arbitrary")),
    )(order, x, ng, c_blk, c_ctx_row, w_ada_sh, b_ada_sh, w_blk_t)


def _mod_row(row):
    return pl.BlockSpec((1, 1, D_MODEL), lambda *idx: (row, 0, 0))


def _ctx_fwd(ctx, ng, mod, w_in_t, ones_blk, gk):
    C = ctx.shape[0]

    def body(x_ref, g_ref, sc_ref, sh_ref, w_ref, ones_ref, gk_ref, h_ref, z_ref, kn_ref):
        xv = x_ref[...]
        r = lax.rsqrt(jnp.mean(xv * xv, axis=-1, keepdims=True) + EPS)
        h = (xv * r * g_ref[...]) * (1.0 + sc_ref[0]) + sh_ref[0]
        hb = h.astype(BF16)
        h_ref[...] = hb
        zk = _nt(hb, w_ref[0:512, :])
        zv = _nt(hb, w_ref[512:1024, :])
        z_ref[:, 0:512] = zk.astype(BF16)
        z_ref[:, 512:1024] = zv.astype(BF16)
        kn_ref[...] = _head_norm(zk, ones_ref, gk_ref[...]).astype(BF16)

    vec = pl.BlockSpec((1, D_MODEL), lambda i: (0, 0))
    return pl.pallas_call(
        body, name="ctx_fwd", grid=(1,),
        in_specs=[pl.BlockSpec((C, D_MODEL), lambda i: (0, 0)), vec, _mod_row(MOD_CSCALE), _mod_row(MOD_CSHIFT),
                  pl.BlockSpec((1024, D_MODEL), lambda i: (2, 0)),
                  pl.BlockSpec((512, 512), lambda i: (0, 0)), pl.BlockSpec((1, 512), lambda i: (0, 0))],
        out_specs=[pl.BlockSpec((C, D_MODEL), lambda i: (0, 0)), pl.BlockSpec((C, 1024), lambda i: (0, 0)),
                   pl.BlockSpec((C, 512), lambda i: (0, 0))],
        out_shape=[jax.ShapeDtypeStruct((C, D_MODEL), BF16), jax.ShapeDtypeStruct((C, 1024), BF16),
                   jax.ShapeDtypeStruct((C, 512), BF16)],
        compiler_params=_params(("arbitrary",)),
    )(ctx, ng, mod, mod, w_in_t, ones_blk, gk)


def _sgu_chunk_fwd(au, av, ag, sg, ws_bf, bsb):
    gu, dgu = _gelu_parts(au)
    gv, dgv = _gelu_parts(av)
    rr = lax.rsqrt(jnp.mean(gv * gv, axis=-1, keepdims=True) + EPS)
    vhat = gv * rr
    vn = vhat * sg
    mixed = jnp.dot(ws_bf, vn.astype(BF16), preferred_element_type=F32) + bsb
    sig = _sigmoid(ag)
    sl = ag * sig
    return gu * mixed * sl, (gu, dgu, dgv, rr, vhat, vn, mixed, sig, sl)


def _sgu_qk_fwd(z, sgn, ws, bsb, ones_blk, gq, gk, tm):
    T = z.shape[0]

    def body(au_ref, av_ref, ag_ref, q_ref, k_ref, sg_ref, ws_ref, bsb_ref, ones_ref, gq_ref, gk_ref,
             o_ref, qs_ref, kn_ref):
        qs = _head_norm(q_ref[...].astype(F32), ones_ref, gq_ref[...]) * (ATT_SCALE * LOG2E)
        qs_ref[...] = qs.astype(BF16)
        kn_ref[...] = _head_norm(k_ref[...].astype(F32), ones_ref, gk_ref[...]).astype(BF16)
        for g in range(SGU_GROUPS):
            ws_bf = ws_ref[g].astype(BF16)
            sg = sg_ref[:, 128 * g:128 * (g + 1)]
            bsb_g = bsb_ref[g]
            for j in range(tm // CHUNK):
                rs, cs = slice(CHUNK * j, CHUNK * (j + 1)), slice(128 * g, 128 * (g + 1))
                out, _ = _sgu_chunk_fwd(au_ref[rs, cs].astype(F32), av_ref[rs, cs].astype(F32),
                                        ag_ref[rs, cs].astype(F32), sg, ws_bf, bsb_g)
                o_ref[rs, cs] = out.astype(BF16)

    zcol = lambda col: pl.BlockSpec((tm, 512), lambda i: (i, col))
    v512 = pl.BlockSpec((1, 512), lambda i: (0, 0))
    row = pl.BlockSpec((tm, 512), lambda i: (i, 0))
    out = jax.ShapeDtypeStruct((T, 512), BF16)
    return pl.pallas_call(
        body, name="sgu_qk_fwd", grid=(T // tm,),
        in_specs=[zcol(0), zcol(1), zcol(2), zcol(3), zcol(4), v512,
                  pl.BlockSpec((SGU_GROUPS, CHUNK, CHUNK), lambda i: (0, 0, 0)),
                  pl.BlockSpec((SGU_GROUPS, CHUNK, 128), lambda i: (0, 0, 0)),
                  pl.BlockSpec((512, 512), lambda i: (0, 0)), v512, v512],
        out_specs=[row, row, row], out_shape=[out, out, out],
        compiler_params=_params(("arbitrary",)),
    )(z, z, z, z, z, sgn, ws, bsb, ones_blk, gq, gk)


def _attn_type(rb, nrb):
    return jnp.where(rb == 0, 0, jnp.where(rb == nrb - 1, 2, 1))


def _attn_specs(T, C, att_sub):
    return [
        pl.BlockSpec((att_sub * TQ, 128), lambda hp, st: (st, hp)),
        pl.BlockSpec((T, 128), lambda hp, st: (0, hp)),
        pl.BlockSpec((T, 128), lambda hp, st: (0, 20 + hp)),
        pl.BlockSpec((C, 128), lambda hp, st: (0, hp)),
        pl.BlockSpec((C, 128), lambda hp, st: (0, 4 + hp)),
        pl.BlockSpec((2, 2 * WIN_R, GRID_W, 128), lambda hp, st: (hp, 0, 0, 0)),
        pl.BlockSpec((3, TQ, TK), lambda hp, st: (0, 0, 0)),
        pl.BlockSpec((att_sub * TQ, 128), lambda hp, st: (st, 24 + hp)),
    ]


def _build_bias(pairs_ref, mask_ref, bias_sc):
    for t in range(3):
        for hh in range(2):
            for i in range(Q_ROWS):
                for mm in range(K_ROWS // 2):
                    p = min(max(WIN_R - Q_ROWS * t + 2 * mm - i, 0), 2 * WIN_R - 1)
                    rs, cs = slice(GRID_W * i, GRID_W * (i + 1)), slice(128 * mm, 128 * (mm + 1))
                    bias_sc[t, hh, rs, cs] = (pairs_ref[hh, p] + mask_ref[t, rs, cs]) * LOG2E


def _attn_fwd(qs, kn, z, ckn, zc, pairs, row_mask, riders=()):
    T, C = qs.shape[0], ckn.shape[0]
    rows = T // GRID_W
    nrb = rows // Q_ROWS
    att_sub = min(ATT_SUB, nrb)
    n_st = nrb // att_sub
    n_rid = len(riders)

    def body(q_ref, k_ref, v_ref, ck_ref, cv_ref, pairs_ref, mask_ref, bg_ref, *rest):
        rid_src = rest[:n_rid]
        ob_ref, outb_ref, lse_ref = rest[n_rid:n_rid + 3]
        rid_dst = rest[n_rid + 3:2 * n_rid + 3]
        bias_sc = rest[2 * n_rid + 3]
        rid_sems = rest[2 * n_rid + 4:]

        @pl.when(jnp.logical_and(pl.program_id(0) == 0, pl.program_id(1) == 0))
        def _():
            for r, (ex, _) in enumerate(riders):
                ex.start(rid_src[r], rid_dst[r], *rid_sems[3 * r:3 * r + 3])

        @pl.when(pl.program_id(1) == 0)
        def _():
            _build_bias(pairs_ref, mask_ref, bias_sc)

        ck2, cv2 = ck_ref[...], cv_ref[...]
        lane = lax.broadcasted_iota(jnp.int32, (1, 128), 1)
        for sub in range(att_sub):
            rb = att_sub * pl.program_id(1) + sub
            bias_ref = bias_sc.at[_attn_type(rb, nrb)]
            rs = slice(TQ * sub, TQ * (sub + 1))
            ks = pl.multiple_of(jnp.clip(Q_ROWS * rb - 4, 0, rows - K_ROWS) * GRID_W, GRID_W)
            q2 = q_ref[rs, :]
            k2 = k_ref[pl.ds(ks, TK), :]
            v2 = v_ref[pl.ds(ks, TK), :]
            o_acc = jnp.zeros((TQ, 128), F32)
            lse_acc = jnp.zeros((TQ, 128), F32)
            for hh in range(2):
                msk = (lane >= HEAD_DIM) == bool(hh)
                qm = jnp.where(msk, q2, jnp.zeros_like(q2))
                s = _nt(qm, k2) + bias_ref[hh]
                sc = _nt(qm, ck2)
                m = jnp.maximum(jnp.max(s, axis=-1, keepdims=True), jnp.max(sc, axis=-1, keepdims=True))
                p = jnp.exp2(s - m)
                pc = jnp.exp2(sc - m)
                va = jnp.where(msk, v2, jnp.ones_like(v2))
                cva = jnp.where(msk, cv2, jnp.ones_like(cv2))
                num = (jnp.dot(p.astype(BF16), va, preferred_element_type=F32)
                       + jnp.dot(pc.astype(BF16), cva, preferred_element_type=F32))
                den = pltpu.roll(num, HEAD_DIM, 1)
                o_acc = jnp.where(msk, num / den, o_acc)
                lse_acc = jnp.where(msk, m + jnp.log(den) * LOG2E, lse_acc)
            ob_ref[rs, :] = o_acc.astype(BF16)
            lse_ref[rs, :] = lse_acc
            bg = bg_ref[rs, :].astype(F32)
            outb_ref[rs, :] = (o_acc * (bg * _sigmoid(bg))).astype(BF16)

        @pl.when(jnp.logical_and(pl.program_id(0) == pl.num_programs(0) - 1, pl.program_id(1) == n_st - 1))
        def _():
            for r, (ex, _) in enumerate(riders):
                ex.wait(rid_src[r], rid_dst[r], *rid_sems[3 * r:3 * r + 3])

    tile = pl.BlockSpec((att_sub * TQ, 128), lambda hp, st: (st, hp))
    res = pl.pallas_call(
        body, name="attn_fwd", grid=(4, n_st),
        in_specs=_attn_specs(T, C, att_sub) + [ANY_SPEC] * n_rid,
        out_specs=[tile, tile, tile] + [ANY_SPEC] * n_rid,
        out_shape=[jax.ShapeDtypeStruct((T, 512), BF16), jax.ShapeDtypeStruct((T, 512), BF16),
                   jax.ShapeDtypeStruct((T, 512), F32)] + [ex.out_shape for ex, _ in riders],
        scratch_shapes=[pltpu.VMEM((3, 2, TQ, TK), F32)] + [s for ex, _ in riders for s in ex.scratch],
        compiler_params=_params(("arbitrary", "arbitrary")),
    )(qs, kn, z, ckn, zc, pairs, row_mask, z, *[a for _, a in riders])
    return res[:3], res[3:]


def _outproj_loss_bwd(x, tgt, out_a, out_b, mod, w_out, tm):
    T = x.shape[0]
    nt = T // tm

    def body(x_ref, t_ref, oa_ref, ob_ref, gate_ref, w_ref, dy_ref, dmc_ref, dw_ref, dgate_ref, loss_ref, acc):
        @pl.when(pl.program_id(0) == 0)
        def _():
            acc[...] = jnp.zeros_like(acc)
            dgate_ref[...] = jnp.zeros_like(dgate_ref)
            loss_ref[...] = jnp.zeros_like(loss_ref)

        oa, ob = oa_ref[...], ob_ref[...]
        gate_v = gate_ref[0]
        mix = (jnp.dot(oa, w_ref[0:512, :], preferred_element_type=F32)
               + jnp.dot(ob, w_ref[512:1024, :], preferred_element_type=F32))
        e = x_ref[...] + gate_v * mix - t_ref[...]
        se = jnp.sum(jnp.sum(e * e, axis=0, keepdims=True), axis=1, keepdims=True)
        loss_ref[...] += jnp.broadcast_to(se * (0.5 / D_MODEL), loss_ref.shape)
        dy = e * (1.0 / D_MODEL)
        dy_ref[...] = dy
        dgate_ref[...] += jnp.sum(dy * mix, axis=0, keepdims=True)
        dmix = (dy * gate_v).astype(BF16)
        dmc_ref[...] = _nt(dmix, w_ref[...]).astype(BF16)
        acc[0:512, :] += _tn(oa, dmix)
        acc[512:1024, :] += _tn(ob, dmix)

        @pl.when(pl.program_id(0) == nt - 1)
        def _():
            dw_ref[...] = acc[...].astype(BF16)

    row = lambda w: pl.BlockSpec((tm, w), lambda i: (i, 0))
    return pl.pallas_call(
        body, name="outproj_loss_bwd", grid=(nt,),
        in_specs=[row(D_MODEL), row(D_MODEL), row(512), row(512), _mod_row(MOD_GATE),
                  pl.BlockSpec((D_MODEL, D_MODEL), lambda i: (0, 0))],
        out_specs=[row(D_MODEL), row(D_MODEL), pl.BlockSpec((D_MODEL, D_MODEL), lambda i: (0, 0)),
                   pl.BlockSpec((1, D_MODEL), lambda i: (0, 0)), pl.BlockSpec((1, 128), lambda i: (0, 0))],
        out_shape=[jax.ShapeDtypeStruct((T, D_MODEL), F32), jax.ShapeDtypeStruct((T, D_MODEL), BF16),
                   jax.ShapeDtypeStruct((D_MODEL, D_MODEL), BF16), jax.ShapeDtypeStruct((1, D_MODEL), F32),
                   jax.ShapeDtypeStruct((1, 128), F32)],
        scratch_shapes=[pltpu.VMEM((D_MODEL, D_MODEL), F32)],
        compiler_params=_params(("arbitrary",)),
    )(x, tgt, out_a, out_b, mod, w_out)


def _attn_bwd(qs, kn, z, ckn, zc, pairs, row_mask, ob, lse, dmc, hsel, fold, riders):
    T, C = qs.shape[0], ckn.shape[0]
    rows = T // GRID_W
    nrb = rows // Q_ROWS
    att_sub = min(ATT_SUB, nrb)
    n_st = nrb // att_sub
    n_rid = len(riders)

    def body(q_ref, k_ref, v_ref, ck_ref, cv_ref, pairs_ref, mask_ref, bg_ref, ob_ref, lse_ref, do_ref,
             hsel_ref, fold_ref, *rest):
        rid_src = rest[:n_rid]
        dq_ref, dk_ref, dv_ref, dck_ref, dcv_ref, dbg_ref, drpb_ref = rest[n_rid:n_rid + 7]
        rid_dst = rest[n_rid + 7:2 * n_rid + 7]
        bias_sc, dacc_ref = rest[2 * n_rid + 7:2 * n_rid + 9]
        rid_sems = rest[2 * n_rid + 9:]
        hp, st = pl.program_id(0), pl.program_id(1)

        @pl.when(jnp.logical_and(hp == 0, st == 0))
        def _():
            for r, (ex, _) in enumerate(riders):
                ex.start(rid_src[r], rid_dst[r], *rid_sems[3 * r:3 * r + 3])

        @pl.when(st == 0)
        def _():
            _build_bias(pairs_ref, mask_ref, bias_sc)
            dk_ref[...] = jnp.zeros_like(dk_ref)
            dv_ref[...] = jnp.zeros_like(dv_ref)
            dck_ref[...] = jnp.zeros_like(dck_ref)
            dcv_ref[...] = jnp.zeros_like(dcv_ref)
            dacc_ref[...] = jnp.zeros_like(dacc_ref)

        ck2, cv2 = ck_ref[...], cv_ref[...]
        lane = lax.broadcasted_iota(jnp.int32, (1, 128), 1)
        for sub in range(att_sub):
            rb = att_sub * st + sub
            bias_ref = bias_sc.at[_attn_type(rb, nrb)]
            rs = slice(TQ * sub, TQ * (sub + 1))
            kb = jnp.clip(Q_ROWS * rb - 4, 0, rows - K_ROWS)
            ks = pl.multiple_of(kb * GRID_W, GRID_W)
            ebase = kb - Q_ROWS * rb + 11
            q2 = q_ref[rs, :]
            k2 = k_ref[pl.ds(ks, TK), :]
            v2 = v_ref[pl.ds(ks, TK), :]
            bg = bg_ref[rs, :].astype(F32)
            sig = _sigmoid(bg)
            obv = ob_ref[rs, :].astype(F32)
            dout = do_ref[rs, :].astype(F32)
            dbg_ref[rs, :] = (dout * obv * (sig * (1.0 + bg * (1.0 - sig)))).astype(BF16)
            d_o = dout * (bg * sig)
            d_oo = d_o * obv
            lse2 = lse_ref[rs, :]
            dq_acc = jnp.zeros((TQ, 128), F32)
            for hh in range(2):
                msk = (lane >= HEAD_DIM) == bool(hh)
                qm = jnp.where(msk, q2, jnp.zeros_like(q2))
                lse_h = jnp.max(jnp.where(msk, lse2, -jnp.inf), axis=-1, keepdims=True)
                p = jnp.exp2(_nt(qm, k2) + bias_ref[hh] - lse_h)
                pc = jnp.exp2(_nt(qm, ck2) - lse_h)
                dom_f = jnp.where(msk, d_o, 0.0)
                dom = dom_f.astype(BF16)
                delta = jnp.sum(jnp.where(msk, d_oo, 0.0), axis=-1, keepdims=True)
                d_hi = delta.astype(BF16).astype(F32)
                x0 = HEAD_DIM * (1 - hh)
                dom_aug = jnp.where(lane == x0, -d_hi, jnp.where(lane == x0 + 1, d_hi - delta, dom_f)).astype(BF16)
                extra = jnp.logical_or(lane == x0, lane == x0 + 1)
                va = jnp.where(msk, v2, jnp.where(extra, jnp.ones_like(v2), jnp.zeros_like(v2)))
                cva = jnp.where(msk, cv2, jnp.where(extra, jnp.ones_like(cv2), jnp.zeros_like(cv2)))
                ds = p * _nt(dom_aug, va)
                dsc = pc * _nt(dom_aug, cva)
                dsb, dscb = ds.astype(BF16), dsc.astype(BF16)
                dq_h = (jnp.dot(dsb, k2, preferred_element_type=F32)
                        + jnp.dot(dscb, ck2, preferred_element_type=F32))
                dq_acc = jnp.where(msk, dq_h, dq_acc)
                dk_ref[pl.ds(ks, TK), :] += _tn(dsb, qm)
                dv_ref[pl.ds(ks, TK), :] += _tn(p.astype(BF16), dom)
                dck_ref[...] += _tn(dscb, qm)
                dcv_ref[...] += _tn(pc.astype(BF16), dom)
                for i in range(Q_ROWS):
                    for mm in range(K_ROWS // 2):
                        dacc_ref[hh, ebase + (2 * mm - i)] += ds[GRID_W * i:GRID_W * (i + 1),
                                                                 128 * mm:128 * (mm + 1)]
            dq_ref[rs, :] = dq_acc

        @pl.when(st == n_st - 1)
        def _():
            for hh in range(2):
                drpb_ref[hh] = _rpb_diag_sums(dacc_ref.at[hh], hsel_ref, fold_ref)

        @pl.when(jnp.logical_and(hp == pl.num_programs(0) - 1, st == n_st - 1))
        def _():
            for r, (ex, _) in enumerate(riders):
                ex.wait(rid_src[r], rid_dst[r], *rid_sems[3 * r:3 * r + 3])

    tile = pl.BlockSpec((att_sub * TQ, 128), lambda hp, st: (st, hp))
    colT = pl.BlockSpec((T, 128), lambda hp, st: (0, hp))
    colC = pl.BlockSpec((C, 128), lambda hp, st: (0, hp))
    res = pl.pallas_call(
        body, name="attn_bwd", grid=(4, n_st),
        in_specs=(_attn_specs(T, C, att_sub) + [tile, tile,
                                                pl.BlockSpec((att_sub * TQ, 128), lambda hp, st: (st, 4 + hp)),
                                       pl.BlockSpec((8, 128, 128), lambda hp, st: (0, 0, 0)),
                                       pl.BlockSpec((32, 256), lambda hp, st: (0, 0))]
                  + [ANY_SPEC] * n_rid),
        out_specs=([tile, colT, colT, colC, colC, tile, pl.BlockSpec((2, 32, 128), lambda hp, st: (hp, 0, 0))]
                   + [ANY_SPEC] * n_rid),
        out_shape=([jax.ShapeDtypeStruct((T, 512), F32), jax.ShapeDtypeStruct((T, 512), F32),
                    jax.ShapeDtypeStruct((T, 512), F32), jax.ShapeDtypeStruct((C, 512), F32),
                    jax.ShapeDtypeStruct((C, 512), F32), jax.ShapeDtypeStruct((T, 512), BF16),
                    jax.ShapeDtypeStruct((N_HEADS, 32, 128), F32)] + [ex.out_shape for ex, _ in riders]),
        scratch_shapes=([pltpu.VMEM((3, 2, TQ, TK), F32), pltpu.VMEM((2, N_DIAG, GRID_W, 128), F32)]
                        + [s for ex, _ in riders for s in ex.scratch]),
        compiler_params=_params(("arbitrary", "arbitrary")),
    )(qs, kn, z, ckn, zc, pairs, row_mask, z, ob, lse, dmc, hsel, fold, *[a for _, a in riders])
    return res[:7], res[7:]


def _rpb_diag_sums(a_ref, hsel_ref, fold_ref):
    n_off = 2 * WIN_C - 1
    n_dr = 2 * WIN_R - 1
    qc = lax.broadcasted_iota(jnp.int32, (GRID_W, 128), 0)
    lane = lax.broadcasted_iota(jnp.int32, (GRID_W, 128), 1)
    diff = lane % GRID_W - qc + (WIN_C - 1)
    left = lane < GRID_W

    def by_dr(dr):
        return a_ref[dr + 4] + pltpu.roll(a_ref[dr + 3], GRID_W, 1)

    out = jnp.zeros((32, 128), F32)
    for j in range((n_dr + 1) // 2):
        hi = pltpu.roll(by_dr(2 * j + 1), GRID_W, 1) if 2 * j + 1 < n_dr else 0.0
        pair = jnp.where(left, by_dr(2 * j), hi)
        parts = []
        for o in range(n_off):
            mv = jnp.where(diff == o, pair, 0.0)
            acc = mv[0:8]
            for r8 in range(1, GRID_W // 8):
                acc = acc + mv[8 * r8:8 * (r8 + 1)]
            parts.append(acc)
        parts.append(jnp.zeros((8, 128), F32))
        stack = jnp.concatenate(parts, axis=0)
        s_hi = stack.astype(BF16)
        s_lo = (stack - s_hi.astype(F32)).astype(BF16)
        per_o = (jnp.dot(fold_ref[...], s_hi, preferred_element_type=F32)
                 + jnp.dot(fold_ref[...], s_lo, preferred_element_type=F32))
        out = out + _dot2(per_o, hsel_ref[j])
    return out


def _head_norm_bwd(raw, dn, gain, ones_ref):
    rr = lax.rsqrt(_head_sum(raw * raw, ones_ref) * (1.0 / HEAD_DIM) + EPS)
    hat = raw * rr
    dgain = jnp.sum(dn * hat, axis=0, keepdims=True)
    dhat = dn * gain
    mean = _head_sum(dhat * hat, ones_ref) * (1.0 / HEAD_DIM)
    return rr * (dhat - hat * mean), dgain


def _ctx_k_bwd(zc, dck, dcv, ones_blk, gk, foldm):
    C = dck.shape[0]

    def body(bk_ref, dk_ref, dv_ref, ones_ref, gk_ref, fold_ref, dbk_ref, dbv_ref, dgk_ref):
        dbk, dgk = _head_norm_bwd(bk_ref[...].astype(F32), dk_ref[...] * LN2, gk_ref[...], ones_ref)
        dbk_ref[...] = dbk.astype(BF16)
        dbv_ref[...] = dv_ref[...].astype(BF16)
        dgk_ref[...] = jnp.dot(jnp.broadcast_to(dgk, (8, 512)), fold_ref[...],
                               preferred_element_type=F32, precision=HI)

    row = pl.BlockSpec((C, 512), lambda i: (0, 0))
    cst = lambda a, b: pl.BlockSpec((a, b), lambda i: (0, 0))
    out_row = jax.ShapeDtypeStruct((C, 512), BF16)
    return pl.pallas_call(
        body, name="ctx_k_bwd", grid=(1,),
        in_specs=[row, row, row, cst(512, 512), cst(1, 512), cst(512, 128)],
        out_specs=[row, row, cst(8, 128)],
        out_shape=[out_row, out_row, jax.ShapeDtypeStruct((8, 128), F32)],
        compiler_params=_params(("arbitrary",)),
    )(zc, dck, dcv, ones_blk, gk, foldm)


def _bwd_mid(z, dmc, dqs, dk, dv, db_g, h, hc, dzc_k, dzc_v, sgn, ws, wst, bsb, ones8, ones_blk, gq, gk, foldm, tk,
             riders=()):
    T = z.shape[0]
    nt = T // tk
    blk = D_IN // N_DEV
    n_in, n_out, n_sc = 23, 7, 9
    n_rid = len(riders)

    def body(*refs):
        (au_ref, av_ref, ag_ref, bq_ref, bk_ref, d_ref, dq_ref, dk_ref, dv_ref, dbg_ref, h_ref,
         hc_ref, dzck_ref, dzcv_ref, sg_ref, ws_ref, wst_ref, bsb_ref, ones8_ref, ones_ref, gq_ref, gk_ref,
         fold_ref) = refs[:n_in]
        rid_src = refs[n_in:n_in + n_rid]
        dz_ref, sums_out, dws_ref, dbs_ref, dsg_ref, dgq_ref, dgk_ref = refs[n_in + n_rid:n_in + n_rid + n_out]
        rid_dst = refs[n_in + n_rid + n_out:n_in + 2 * n_rid + n_out]
        (acc, accq, acck, stage, sem, send_buf, tmp, s1, r1) = refs[n_in + 2 * n_rid + n_out:
                                                                   n_in + 2 * n_rid + n_out + n_sc]
        rid_sems = refs[n_in + 2 * n_rid + n_out + n_sc:]
        t = pl.program_id(0)

        @pl.when(t == 0)
        def _():
            for r, (ex, _) in enumerate(riders):
                ex.start(rid_src[r], rid_dst[r], *rid_sems[3 * r:3 * r + 3])
            acc[...] = jnp.zeros_like(acc)
            acc[512 * 4:512 * 5, :] = _tn(dzck_ref[...], hc_ref[...])
            acc[512 * 5:512 * 6, :] = _tn(dzcv_ref[...], hc_ref[...])
            dws_ref[...] = jnp.zeros_like(dws_ref)
            dbs_ref[...] = jnp.zeros_like(dbs_ref)
            dsg_ref[...] = jnp.zeros_like(dsg_ref)
            accq[...] = jnp.zeros_like(accq)
            acck[...] = jnp.zeros_like(acck)

        for g in range(SGU_GROUPS):
            ws_bf = ws_ref[g].astype(BF16)
            wst_bf = wst_ref[g].astype(BF16)
            sg = sg_ref[:, 128 * g:128 * (g + 1)]
            bsb_g = bsb_ref[g]
            for j in range(tk // CHUNK):
                rs, cs = slice(CHUNK * j, CHUNK * (j + 1)), slice(128 * g, 128 * (g + 1))
                au, av, ag = (au_ref[rs, cs].astype(F32), av_ref[rs, cs].astype(F32), ag_ref[rs, cs].astype(F32))
                d = d_ref[rs, cs].astype(F32)
                _, (gu, dgu, dgv, rr, vhat, vn, mixed, sig, sl) = _sgu_chunk_fwd(au, av, ag, sg, ws_bf, bsb_g)
                dz_ref[rs, 128 * g:128 * (g + 1)] = (d * mixed * sl * dgu).astype(BF16)
                dz_ref[rs, 1024 + 128 * g:1024 + 128 * (g + 1)] = (
                    d * gu * mixed * (sig * (1.0 + ag * (1.0 - sig)))).astype(BF16)
                dmixed = d * gu * sl
                dmb = dmixed.astype(BF16)
                dm_lo = (dmixed - dmb.astype(F32)).astype(BF16)
                dbs_ref[g] += _nt(ones8_ref[...], dmb) + _nt(ones8_ref[...], dm_lo)
                dws_ref[g] += _nt(dmb, vn.astype(BF16))
                dvn = jnp.dot(wst_bf, dmb, preferred_element_type=F32)
                dsg_ref[:, 128 * g:128 * (g + 1)] += jnp.sum(dvn * vhat, axis=0, keepdims=True)
                dvhat = dvn * sg
                mean = jnp.mean(dvhat * vhat, axis=-1, keepdims=True)
                dz_ref[rs, 512 + 128 * g:512 + 128 * (g + 1)] = (rr * (dvhat - vhat * mean) * dgv).astype(BF16)

        dbq, dgq = _head_norm_bwd(bq_ref[...].astype(F32), dq_ref[...] * ATT_SCALE, gq_ref[...], ones_ref)
        dz_ref[:, 512 * 3:512 * 4] = dbq.astype(BF16)
        accq[...] += dgq
        dbk, dgk = _head_norm_bwd(bk_ref[...].astype(F32), dk_ref[...] * LN2, gk_ref[...], ones_ref)
        dz_ref[:, 512 * 4:512 * 5] = dbk.astype(BF16)
        acck[...] += dgk
        dz_ref[:, 512 * 5:512 * 6] = dv_ref[...].astype(BF16)
        dz_ref[:, 512 * 6:512 * 7] = dbg_ref[...]

        hv = h_ref[...]
        for k in range(N_BRANCH):
            acc[512 * k:512 * (k + 1), :] += _tn(dz_ref[:, 512 * k:512 * (k + 1)], hv)

        @pl.when(t == nt - 1)
        def _():
            dgq_ref[...] = jnp.dot(jnp.broadcast_to(accq[...], (8, 512)), fold_ref[...],
                                   preferred_element_type=F32, precision=HI)
            dgk_ref[...] = jnp.dot(jnp.broadcast_to(acck[...], (8, 512)), fold_ref[...],
                                   preferred_element_type=F32, precision=HI)
            cidx = lax.axis_index("c")
            sib = (lax.axis_index("x"), lax.axis_index("y"), 1 - cidx)
            swaps = []
            for q in range(N_DEV // 2):
                theirs = acc[pl.ds(pl.multiple_of(2 * blk * q + blk * (1 - cidx), 8), blk), :]
                send_buf[q] = theirs.astype(BF16)
                cp = pltpu.make_async_remote_copy(src_ref=send_buf.at[q], dst_ref=tmp.at[q], send_sem=s1.at[q],
                                                  recv_sem=r1.at[q], device_id=sib, device_id_type=MESH)
                cp.start()
                swaps.append(cp)
            for q in range(N_DEV // 2):
                swaps[q].wait_recv()
                mine = acc[pl.ds(pl.multiple_of(2 * blk * q + blk * cidx, 8), blk), :]
                stage[...] = (mine + tmp[q].astype(F32)).astype(BF16)
                out = pltpu.make_async_copy(stage, sums_out.at[q], sem)
                out.start()
                out.wait()
            for cp in swaps:
                cp.wait_send()
            for r, (ex, _) in enumerate(riders):
                ex.wait(rid_src[r], rid_dst[r], *rid_sems[3 * r:3 * r + 3])

    zcol = lambda col: pl.BlockSpec((tk, 512), lambda t: (t, col))
    row = pl.BlockSpec((tk, 512), lambda t: (t, 0))
    whole = lambda a: pl.BlockSpec(a.shape, lambda t: (0,) * a.ndim)
    res = pl.pallas_call(
        body, name="bwd_mid", grid=(nt,),
        in_specs=[zcol(0), zcol(1), zcol(2), zcol(3), zcol(4), row, row, row, row, row,
                  pl.BlockSpec((tk, D_MODEL), lambda t: (t, 0)), whole(hc), whole(dzc_k), whole(dzc_v),
                  whole(sgn), whole(ws), whole(wst), whole(bsb), whole(ones8), whole(ones_blk), whole(gq), whole(gk),
                  whole(foldm)] + [ANY_SPEC] * n_rid,
        out_specs=[pl.BlockSpec((tk, D_IN), lambda t: (t, 0)), ANY_SPEC,
                   pl.BlockSpec((SGU_GROUPS, CHUNK, CHUNK), lambda t: (0, 0, 0)),
                   pl.BlockSpec((SGU_GROUPS, 8, CHUNK), lambda t: (0, 0, 0)),
                   pl.BlockSpec((1, 512), lambda t: (0, 0)), pl.BlockSpec((8, 128), lambda t: (0, 0)),
                   pl.BlockSpec((8, 128), lambda t: (0, 0))] + [ANY_SPEC] * n_rid,
        out_shape=[jax.ShapeDtypeStruct((T, D_IN), BF16), jax.ShapeDtypeStruct((N_DEV // 2, blk, D_MODEL), BF16),
                   jax.ShapeDtypeStruct((SGU_GROUPS, CHUNK, CHUNK), F32),
                   jax.ShapeDtypeStruct((SGU_GROUPS, 8, CHUNK), F32), jax.ShapeDtypeStruct((1, 512), F32),
                   jax.ShapeDtypeStruct((8, 128), F32), jax.ShapeDtypeStruct((8, 128), F32)]
                  + [ex.out_shape for ex, _ in riders],
        scratch_shapes=[pltpu.VMEM((D_IN, D_MODEL), F32), pltpu.VMEM((1, 512), F32), pltpu.VMEM((1, 512), F32),
                        pltpu.VMEM((blk, D_MODEL), BF16), pltpu.SemaphoreType.DMA,
                        pltpu.VMEM((N_DEV // 2, blk, D_MODEL), BF16), pltpu.VMEM((N_DEV // 2, blk, D_MODEL), BF16),
                        pltpu.SemaphoreType.DMA((N_DEV // 2,)), pltpu.SemaphoreType.DMA((N_DEV // 2,))]
                       + [s for ex, _ in riders for s in ex.scratch],
        compiler_params=_params(("arbitrary",)),
    )(z, z, z, z, z, dmc, dqs, dk, dv, db_g, h, hc, dzc_k, dzc_v, sgn, ws, wst, bsb, ones8, ones_blk, gq, gk, foldm,
      *[a for _, a in riders])
    return res[:n_out], res[n_out:]


def _inproj_bwd_dx(dzs, w_row0, w_in_t, x, dy, ng, mod, scale_row, tm, name, riders=()):
    T = x.shape[0]
    n = len(dzs)
    wpc = dzs[0].shape[1]
    nt = T // tm
    with_dx = dy is not None
    n_own_in = n + 4 + with_dx
    n_own_out = 3 + with_dx
    n_rid = len(riders)

    def body(*refs):
        dz_refs = refs[:n]
        n_in = n_own_in + n_rid
        own = refs[n:n_own_in] + refs[n_in:n_in + n_own_out]
        rid_src = refs[n_own_in:n_own_in + n_rid]
        rid_dst = refs[n_in + n_own_out:n_in + n_own_out + n_rid]
        rid_sems = refs[n_in + n_own_out + n_rid:]
        if with_dx:
            w_ref, x_ref, dy_ref, g_ref, sc_ref, gx_ref, dsh_ref, dsc_ref, dg_ref = own
        else:
            w_ref, x_ref, g_ref, sc_ref, dsh_ref, dsc_ref, dg_ref = own

        @pl.when(pl.program_id(0) == 0)
        def _():
            for r, (ex, _) in enumerate(riders):
                ex.start(rid_src[r], rid_dst[r], *rid_sems[3 * r:3 * r + 3])
            dsh_ref[...] = jnp.zeros_like(dsh_ref)
            dsc_ref[...] = jnp.zeros_like(dsc_ref)
            dg_ref[...] = jnp.zeros_like(dg_ref)

        dh = jnp.dot(dz_refs[0][...], w_ref[0:wpc, :], preferred_element_type=F32)
        for k in range(1, n):
            dh = dh + jnp.dot(dz_refs[k][...], w_ref[wpc * k:wpc * (k + 1), :], preferred_element_type=F32)
        xv = x_ref[...]
        r = lax.rsqrt(jnp.mean(xv * xv, axis=-1, keepdims=True) + EPS)
        xn = xv * r
        gv, op = g_ref[...], 1.0 + sc_ref[0]
        dsh_ref[...] += jnp.sum(dh, axis=0, keepdims=True)
        dsc_ref[...] += jnp.sum(dh * xn * gv, axis=0, keepdims=True)
        dg_ref[...] += jnp.sum(dh * op * xn, axis=0, keepdims=True)
        if with_dx:
            dxn = dh * (gv * op)
            gx_ref[...] = r * (dxn - xn * jnp.mean(dxn * xn, axis=-1, keepdims=True)) + dy_ref[...]

        @pl.when(pl.program_id(0) == nt - 1)
        def _():
            for r, (ex, _) in enumerate(riders):
                ex.wait(rid_src[r], rid_dst[r], *rid_sems[3 * r:3 * r + 3])

    vec = pl.BlockSpec((1, D_MODEL), lambda i: (0, 0))
    rowf = pl.BlockSpec((tm, D_MODEL), lambda i: (i, 0))
    in_specs = [pl.BlockSpec((tm, wpc), lambda i: (i, 0))] * n
    in_specs += [pl.BlockSpec((wpc * n, D_MODEL), lambda i: (w_row0 // (wpc * n), 0)), rowf]
    args = list(dzs) + [w_in_t, x]
    vshape = jax.ShapeDtypeStruct((1, D_MODEL), F32)
    out_specs, out_shape = [vec, vec, vec], [vshape, vshape, vshape]
    if with_dx:
        in_specs.append(rowf)
        args.append(dy)
        out_specs = [rowf] + out_specs
        out_shape = [jax.ShapeDtypeStruct((T, D_MODEL), F32)] + out_shape
    in_specs += [vec, _mod_row(scale_row)] + [ANY_SPEC] * n_rid
    args += [ng, mod] + [a for _, a in riders]
    res = pl.pallas_call(
        body, name=name, grid=(nt,), in_specs=in_specs, out_specs=out_specs + [ANY_SPEC] * n_rid,
        out_shape=out_shape + [ex.out_shape for ex, _ in riders],
        scratch_shapes=[s for ex, _ in riders for s in ex.scratch],
        compiler_params=_params(("arbitrary",)),
    )(*args)
    return res[:n_own_out], res[n_own_out:]


def _adamw_sharded(w, gparts, m, v, tr, name, riders=()):
    R, C = w.shape
    n_part = gparts.shape[0]
    nt = R // tr
    n_rid = len(riders)

    def body(w_ref, gp_ref, m_ref, v_ref, *rest):
        rid_src = rest[:n_rid]
        g_ref, d_ref, m2_ref, v2_ref = rest[n_rid:n_rid + 4]
        rid_dst = rest[n_rid + 4:2 * n_rid + 4]
        rid_sems = rest[2 * n_rid + 4:]

        @pl.when(pl.program_id(0) == 0)
        def _():
            for r, (ex, _) in enumerate(riders):
                ex.start(rid_src[r], rid_dst[r], *rid_sems[3 * r:3 * r + 3])

        g = gp_ref[0].astype(F32)
        for d in range(1, n_part):
            g = g + gp_ref[d].astype(F32)
        delta, m2, v2 = _adam(w_ref[...], g, m_ref[...], v_ref[...])
        g_ref[...] = g
        d_ref[...] = delta
        m2_ref[...] = m2
        v2_ref[...] = v2

        @pl.when(pl.program_id(0) == nt - 1)
        def _():
            for r, (ex, _) in enumerate(riders):
                ex.wait(rid_src[r], rid_dst[r], *rid_sems[3 * r:3 * r + 3])

    row = pl.BlockSpec((tr, C), lambda i: (i, 0))
    sh = jax.ShapeDtypeStruct((R, C), F32)
    res = pl.pallas_call(
        body, name=name, grid=(nt,),
        in_specs=[row, pl.BlockSpec((n_part, tr, C), lambda i: (0, i, 0)), row, row] + [ANY_SPEC] * n_rid,
        out_specs=[row, row, row, row] + [ANY_SPEC] * n_rid,
        out_shape=[sh, sh, sh, sh] + [ex.out_shape for ex, _ in riders],
        scratch_shapes=[s for ex, _ in riders for s in ex.scratch],
        compiler_params=_params(("arbitrary",)),
    )(w, gparts, m, v, *[a for _, a in riders])
    return res[:4], res[4:]


def _pack_vectors(vec_rows, dsg, dgq, dgk, dgk_c, loss_part):
    flat = [a for _, arrs in vec_rows for a in arrs]

    def body(*refs):
        vecs = list(refs[:len(flat)])
        dsg_ref, dgq_ref, dgk_ref, dgkc_ref, loss_ref, v_ref = refs[len(flat):]
        row = lax.broadcasted_iota(jnp.int32, (16, D_MODEL), 0)
        misc = jnp.concatenate([dsg_ref[...], dgq_ref[0:1, :], dgk_ref[0:1, :], dgkc_ref[0:1, :],
                                loss_ref[...]], axis=1)
        v = jnp.where(row == V_MISC, jnp.broadcast_to(misc, (16, D_MODEL)), 0.0)
        for r, arrs in vec_rows:
            val = vecs.pop(0)[...]
            for _ in arrs[1:]:
                val = val + vecs.pop(0)[...]
            v = jnp.where(row == r, jnp.broadcast_to(val, (16, D_MODEL)), v)
        v_ref[...] = v

    return pl.pallas_call(
        body, name="pack_vectors", out_shape=jax.ShapeDtypeStruct((16, D_MODEL), F32), compiler_params=_params(),
    )(*flat, dsg, dgq, dgk, dgk_c, loss_part)


SMALL_NAMES = ("b_ada", "norm_g", "sgu_norm_g", "w_spatial", "b_spatial", "q_norm_g", "k_norm_g", "rpb")


def _adamw_small(vg, sg, rg, ws, ms, vs):
    k = len(SMALL_NAMES)

    def body(*refs):
        vg_ref, sg_ref, rg_ref = refs[0], refs[1], refs[2]
        refs = refs[1:]
        w_refs = dict(zip(SMALL_NAMES, refs[2:2 + k]))
        m_refs = dict(zip(SMALL_NAMES, refs[2 + k:2 + 2 * k]))
        v_refs = dict(zip(SMALL_NAMES, refs[2 + 2 * k:2 + 3 * k]))
        o_refs = [dict(zip(SMALL_NAMES, refs[2 + (3 + i) * k:2 + (4 + i) * k])) for i in range(4)]
        loss_ref = refs[2 + 7 * k]

        sv = vg_ref[0]
        for d in range(1, N_DEV):
            sv = sv + vg_ref[d]
        loss_ref[...] = sv[V_MISC:V_MISC + 1, 896:1024]

        def total(lo, hi, ref=sg_ref):
            s = ref[0, lo:hi, :].astype(F32)
            for d in range(1, N_DEV):
                s = s + ref[d, lo:hi, :].astype(F32)
            return s

        def emit(name, idx, g):
            res = _adam(w_refs[name][idx], g, m_refs[name][idx], v_refs[name][idx])
            for o, val in zip(o_refs, (g,) + res):
                o[name][idx] = val

        everything = (slice(None), slice(None))
        row = lambda r: sv[r:r + 1, :]
        emit("b_ada", everything, jnp.concatenate(
            [row(V_DSHIFT) + row(V_DCSHIFT), row(V_DSCALE) + row(V_DCSCALE), row(V_DGATE)], axis=1))
        emit("norm_g", everything, row(V_DNG) + row(V_DNG_CTX))
        misc = row(V_MISC)
        emit("sgu_norm_g", everything, misc[:, 0:512])
        emit("q_norm_g", everything, misc[:, 512:512 + HEAD_DIM])
        emit("k_norm_g", everything, misc[:, 640:640 + HEAD_DIM] + misc[:, 768:768 + HEAD_DIM])
        for g in range(SGU_GROUPS):
            emit("w_spatial", (0, g), total(128 * g, 128 * (g + 1)))
            emit("b_spatial", (0, slice(g, g + 1), slice(None)), total(M_DBS + 8 * g, M_DBS + 8 * (g + 1))[0:1, :])
        for hd in range(N_HEADS):
            by_dc = total(32 * hd, 32 * (hd + 1), rg_ref)
            emit("rpb", (0, hd), by_dc.T[0:2 * WIN_R - 1, 0:2 * WIN_C - 1])

    shapes = [jax.ShapeDtypeStruct(w.shape, F32) for w in ws]
    res = pl.pallas_call(body, name="adamw_small", out_shape=shapes * 4 + [jax.ShapeDtypeStruct((1, 128), F32)],
                         compiler_params=_params())(vg, sg, rg, *ws, *ms, *vs)
    return [res[i * k:(i + 1) * k] for i in range(4)], res[4 * k]


def _adamw_cctx(pc_g, w, m, v):
    def body(pc_ref, w_ref, m_ref, v_ref, g_ref, d_ref, m2_ref, v2_ref):
        pc = pc_ref[0, 0:1, :]
        for d in range(1, N_DEV):
            pc = pc + pc_ref[d, 0:1, :]
        cc = w_ref[...]
        sig = _sigmoid(cc)
        g = pc * (sig * (1.0 + cc * (1.0 - sig)))
        delta, m2, v2 = _adam(cc, g, m_ref[...], v_ref[...])
        g_ref[...] = g
        d_ref[...] = delta
        m2_ref[...] = m2
        v2_ref[...] = v2

    sh = jax.ShapeDtypeStruct((1, D_MODEL), F32)
    return pl.pallas_call(body, name="adamw_cctx", out_shape=[sh, sh, sh, sh], compiler_params=_params())(
        pc_g, w, m, v)


def _block_ones(n, blk):
    i = np.arange(n)
    return jnp.asarray((i[:, None] // blk == i[None, :] // blk).astype(np.float32), BF16)


def _rpb_pairs(rpb):
    n_off = 2 * WIN_C - 1
    cols = np.arange(GRID_W)
    c0 = np.clip(cols - WIN_C // 2, 0, GRID_W - WIN_C)
    in_win = (cols[None, :] >= c0[:, None]) & (cols[None, :] < c0[:, None] + WIN_C)
    dc = np.clip(cols[None, :] - cols[:, None] + (WIN_C - 1), 0, n_off - 1)
    expand = (dc[None] == np.arange(n_off)[:, None, None]) & in_win[None]
    toep = jnp.einsum("hrd,dqk->hrqk", rpb, jnp.asarray(expand, F32), precision=HI)
    toep = toep + jnp.asarray(np.where(in_win, 0.0, NEG_INF).astype(np.float32))
    neg = jnp.full((N_HEADS, 1, GRID_W, GRID_W), NEG_INF, F32)
    ext = jnp.concatenate([neg, toep, neg], axis=1)
    return jnp.concatenate([ext[:, :-1], ext[:, 1:]], axis=-1)


def _row_mask(rows):
    nrb = rows // Q_ROWS
    valid = np.zeros((3, Q_ROWS, 1, K_ROWS, 1), bool)
    for t, rb in enumerate((0, 1, nrb - 1)):
        kb = int(np.clip(Q_ROWS * rb - 4, 0, rows - K_ROWS))
        for i in range(Q_ROWS):
            r0 = int(np.clip(Q_ROWS * rb + i - WIN_R // 2, 0, rows - WIN_R))
            for j in range(K_ROWS):
                valid[t, i, 0, j, 0] = r0 <= kb + j < r0 + WIN_R
    full = np.broadcast_to(valid, (3, Q_ROWS, GRID_W, K_ROWS, GRID_W)).reshape(3, TQ, TK)
    return jnp.asarray(np.where(full, 0.0, NEG_INF).astype(np.float32))


def kernel(x, c, ctx, c_ctx, w_ada, b_ada, norm_g, w_in, sgu_norm_g, w_spatial, b_spatial, q_norm_g, k_norm_g, rpb, w_out, loss_target, m_c_ctx, m_w_ada, m_b_ada, m_norm_g, m_w_in, m_sgu_norm_g, m_w_spatial, m_b_spatial, m_q_norm_g, m_k_norm_g, m_rpb, m_w_out, v_c_ctx, v_w_ada, v_b_ada, v_norm_g, v_w_in, v_sgu_norm_g, v_w_spatial, v_b_spatial, v_q_norm_g, v_k_norm_g, v_rpb, v_w_out):
    me = 4 * lax.axis_index("x") + 2 * lax.axis_index("y") + lax.axis_index("c")
    x2, ctx2, tgt2 = x[0], ctx[0], loss_target[0]
    T, C = x2.shape[0], ctx2.shape[0]
    rows = T // GRID_W
    wada, win_t, wout = w_ada[0], w_in[0].T, w_out[0]
    ada_w = wada.shape[1]
    win_w = win_t.shape[0]

    row8 = lax.broadcasted_iota(jnp.int32, (8, D_MODEL), 0)
    c_blk = jnp.where(row8 == me, jnp.broadcast_to(c, (8, D_MODEL)), 0.0)
    b_sh = lax.dynamic_slice(b_ada, (0, me * ada_w), (1, ada_w))
    c_ctx_row = c_ctx.reshape(1, D_MODEL)

    ones512 = _block_ones(512, HEAD_DIM)
    ones8 = jnp.ones((8, 128), BF16)
    foldm = jnp.asarray((np.arange(512)[:, None] % HEAD_DIM == np.arange(128)[None, :]).astype(np.float32))
    lane_half = np.arange(128)[None, :, None] // GRID_W
    hsel = jnp.asarray((2 * np.arange(8)[:, None, None] + lane_half == np.arange(128)[None, None, :]).astype(np.float32),
                       BF16)
    foldr = jnp.asarray((np.arange(256)[None, :] // 8 == np.arange(32)[:, None]).astype(np.float32), BF16)
    gq512 = jnp.tile(q_norm_g, (1, N_HEADS))
    gk512 = jnp.tile(k_norm_g, (1, N_HEADS))
    ws = w_spatial[0]
    wst = ws.transpose(0, 2, 1)
    bsb = jnp.broadcast_to(b_spatial[0][:, :, None], (SGU_GROUPS, CHUNK, 128))
    pairs = _rpb_pairs(rpb[0])
    row_mask = _row_mask(rows)

    my_chip = me // 2
    order = jnp.stack([my_chip, my_chip ^ 2, my_chip ^ 1, my_chip ^ 3]).astype(jnp.int32)
    h, z, win_g, s16, mod = _inproj_fwd(order, x2, norm_g, c_blk, c_ctx_row, wada, b_sh, win_t.astype(BF16), 1024)
    w_in_b = win_g.reshape(D_IN, D_MODEL)
    out_a, qs, kn = _sgu_qk_fwd(z, sgu_norm_g, ws, bsb, ones512, gq512, gk512, 512)
    hc, zc, ckn = _ctx_fwd(ctx2, norm_g, mod, w_in_b, ones512, gk512)
    wout_blk = wout.astype(BF16)
    (ob, out_b, lse), (wout_g,) = _attn_fwd(qs, kn, z, ckn, zc, pairs, row_mask, [(_Hosted("ag", wout_blk), wout_blk)])
    w_out_b = wout_g.reshape(D_MODEL, D_MODEL)

    dy, dmc, dw_out, dgate, loss_part = _outproj_loss_bwd(x2, tgt2, out_a, out_b, mod, w_out_b, 1024)
    dw_out_blocks = dw_out.reshape(N_DEV, D_MODEL // N_DEV, D_MODEL)
    (dqs, dk, dv, dck, dcv, db_g, drpb), (gout_parts,) = _attn_bwd(
        qs, kn, z, ckn, zc, pairs, row_mask, ob, lse, dmc, hsel, foldr,
        [(_Hosted("a2a", dw_out_blocks), dw_out_blocks)])
    dzc_k, dzc_v, dgk_c = _ctx_k_bwd(zc, dck, dcv, ones512, gk512, foldm)
    rloc = drpb.reshape(N_HEADS * 32, 128)
    (dz, chip_sums, dws, dbs, dsg, dgq, dgk), (rg,) = _bwd_mid(
        z, dmc, dqs, dk, dv, db_g, h, hc, dzc_k, dzc_v, sgu_norm_g, ws, wst, bsb, ones8, ones512, gq512, gk512, foldm,
        512, [(_Hosted("ag", rloc), rloc)])
    sloc = jnp.concatenate([dws.reshape(SGU_GROUPS * CHUNK, CHUNK), dbs.reshape(SGU_GROUPS * 8, CHUNK)]).astype(BF16)
    (grad_x, dshift, dscale, dng), (gin_parts,) = _inproj_bwd_dx(
        [dz], 0, w_in_b, x2, dy, norm_g, mod, MOD_SCALE, 512, "inproj_bwd_dx",
        [(_Hosted("chips", chip_sums), chip_sums)])
    (dcshift, dcscale, dng_c), _ = _inproj_bwd_dx([dzc_k, dzc_v], 4 * 512, w_in_b, ctx2, None, norm_g, mod,
                                                  MOD_CSCALE, C, "ctx_bwd_dx")

    res_in, _ = _adamw_sharded(win_t, gin_parts, m_w_in[0].T, v_w_in[0].T, 224, "adamw_w_in")
    res_out, _ = _adamw_sharded(wout, gout_parts, m_w_out[0], v_w_out[0], 128, "adamw_w_out")

    zero_row = jnp.zeros((1, D_MODEL), F32)
    vec_rows = [(V_DSHIFT, [dshift]), (V_DSCALE, [dscale]), (V_DGATE, [dgate]), (V_DCSHIFT, [dcshift]),
                (V_DCSCALE, [dcscale]), (V_ZERO, [zero_row]), (V_DNG, [dng]), (V_DNG_CTX, [dng_c])]
    vloc = _pack_vectors(vec_rows, dsg, dgq, dgk, dgk_c, loss_part)
    vg, sg = _allgather_direct([vloc, sloc], "gather_small")
    small_w =(b_ada, norm_g, sgu_norm_g, w_spatial, b_spatial, q_norm_g, k_norm_g, rpb)
    small_m = (m_b_ada, m_norm_g, m_sgu_norm_g, m_w_spatial, m_b_spatial, m_q_norm_g, m_k_norm_g, m_rpb)
    small_v = (v_b_ada, v_norm_g, v_sgu_norm_g, v_w_spatial, v_b_spatial, v_q_norm_g, v_k_norm_g, v_rpb)
    res_small, loss_row = _adamw_small(vg, sg, rg, small_w, small_m, small_v)

    dm_all = vg[:, V_DSHIFT:V_DGATE + 1, :].reshape(N_DEV, 3 * D_MODEL)
    dc_all = vg[:, V_DCSHIFT:V_ZERO + 1, :].reshape(N_DEV, 3 * D_MODEL)
    dm_sh = lax.dynamic_slice(dm_all, (0, me * ada_w), (N_DEV, ada_w))
    dc_sh = lax.dynamic_slice(dc_all, (0, me * ada_w), (N_DEV, ada_w))
    *res_ada, pc = _ada_bwd(s16, dm_sh, dc_sh, wada, m_w_ada[0], v_w_ada[0])
    (pc_g,) = _allgather_direct([pc], "gather_cctx")
    res_cctx = _adamw_cctx(pc_g, c_ctx_row, m_c_ctx.reshape(1, D_MODEL), v_c_ctx.reshape(1, D_MODEL))

    loss = loss_row[0, 0]
    outs = [loss, grad_x[None]]
    for kind in range(4):
        by_name = dict(zip(SMALL_NAMES, res_small[kind]))
        by_name.update(c_ctx=res_cctx[kind].reshape(D_MODEL), w_ada=res_ada[kind][None],
                       w_in=res_in[kind].T[None], w_out=res_out[kind][None])
        outs += [by_name[nme] for nme in ("c_ctx", "w_ada", "b_ada", "norm_g", "w_in", "sgu_norm_g", "w_spatial",
                                          "b_spatial", "q_norm_g", "k_norm_g", "rpb", "w_out")]
    return tuple(outs)
```
